```python
import math
import jax, jax.numpy as jnp
from jax import lax
import numpy as np

D_MODEL = 1024
BATCH = 8
SEQ = 4096
DEPTH = 2

N_MIXERS = 2
N_MAMBA_LAYERS = (DEPTH + 1) // 2
N_ATTN_LAYERS = DEPTH // 2

SSM_EXPAND = 2
D_INNER = SSM_EXPAND * D_MODEL
SSM_HEAD_DIM = 64
SSM_HEADS = D_INNER // SSM_HEAD_DIM
SSM_GROUPS = 4
SSM_HEADS_PER_GROUP = SSM_HEADS // SSM_GROUPS
SSM_STATE = 128
CONV_WIDTH = 4
SSD_CHUNK = 128
CONV_DIM = D_INNER + 2 * SSM_GROUPS * SSM_STATE
IN_PROJ_DIM = 2 * D_INNER + 2 * SSM_GROUPS * SSM_STATE + SSM_HEADS

ATT_HEAD_DIM = 64
ATT_HEADS = D_MODEL // ATT_HEAD_DIM
DIL_PATTERNS = ((128, 1), (512, 4), (2048, 16))
N_DIL_GROUPS = len(DIL_PATTERNS)
QKV_DIM = N_DIL_GROUPS * 3 * ATT_HEADS * ATT_HEAD_DIM

FFN_HIDDEN = ((-(-8 * D_MODEL // 3) + 255) // 256) * 256

PLE_DIM = 256

NORM_EPS = 1e-6

kernel_name = "hybrid_ssd_dilated_attn_trunk"


def rmsnorm(x, gain):
    xf = x.astype(jnp.float32)
    y = xf * lax.rsqrt(jnp.mean(xf * xf, axis=-1, keepdims=True) + NORM_EPS)
    return (y * gain.astype(jnp.float32)).astype(x.dtype)


def causal_depthwise_conv(u, w, bias):
    k_width, chans = w.shape
    out = lax.conv_general_dilated(
        u, w[:, None, :].astype(u.dtype), window_strides=(1,),
        padding=[(k_width - 1, 0)], dimension_numbers=("NWC", "WIO", "NWC"),
        feature_group_count=chans)
    return out + bias.astype(u.dtype)


def ssd_chunked(x, dt, a, bm, cm):
    f32 = jnp.float32
    b, t = x.shape[:2]
    nc, cl = t // SSD_CHUNK, SSD_CHUNK
    g, hg = SSM_GROUPS, SSM_HEADS_PER_GROUP
    xs = (x.astype(f32) * dt[..., None]).reshape(b, nc, cl, g, hg, SSM_HEAD_DIM)
    a_dt = (dt * a).reshape(b, nc, cl, g, hg).transpose(0, 1, 3, 4, 2)
    a_cs = jnp.cumsum(a_dt, axis=-1)
    bc = bm.astype(f32).reshape(b, nc, cl, g, SSM_STATE)
    cc = cm.astype(f32).reshape(b, nc, cl, g, SSM_STATE)
    causal = jnp.tril(jnp.ones((cl, cl), dtype=bool))
    seg = a_cs[..., :, None] - a_cs[..., None, :]
    lmat = jnp.exp(jnp.where(causal, seg, -jnp.inf))
    cb = jnp.einsum("bclgn,bcsgn->bcgls", cc, bc)
    y_diag = jnp.einsum("bcgls,bcghls,bcsghp->bclghp", cb, lmat, xs)
    decay = jnp.exp(a_cs[..., -1:] - a_cs)
    states = jnp.einsum("bclgn,bcghl,bclghp->bcghpn", bc, decay, xs)
    chunk_decay = jnp.exp(a_cs[..., -1])

    def step(carry, inp):
        st, dec = inp
        return carry * dec[..., None, None] + st, carry

    init = jnp.zeros((b, g, hg, SSM_HEAD_DIM, SSM_STATE), f32)
    _, prev = lax.scan(step, init, (jnp.moveaxis(states, 1, 0), jnp.moveaxis(chunk_decay, 1, 0)))
    prev = jnp.moveaxis(prev, 0, 1)
    y_off = jnp.einsum("bclgn,bcghpn,bcghl->bclghp", cc, prev, jnp.exp(a_cs))
    return (y_diag + y_off).reshape(b, t, SSM_HEADS, SSM_HEAD_DIM)


def mamba2_mixer(h, w_in, conv_w, conv_b, dt_bias, a_log, d_skip, norm_w, w_out):
    b, t, _ = h.shape
    zxbcdt = h @ w_in.astype(h.dtype)
    z = zxbcdt[..., :D_INNER]
    xbc = zxbcdt[..., D_INNER:D_INNER + CONV_DIM]
    dt_raw = zxbcdt[..., D_INNER + CONV_DIM:]
    xbc = jax.nn.silu(causal_depthwise_conv(xbc, conv_w, conv_b))
    xs = xbc[..., :D_INNER]
    bm = xbc[..., D_INNER:D_INNER + SSM_GROUPS * SSM_STATE].reshape(b, t, SSM_GROUPS, SSM_STATE)
    cm = xbc[..., D_INNER + SSM_GROUPS * SSM_STATE:].reshape(b, t, SSM_GROUPS, SSM_STATE)
    dt = jax.nn.softplus(dt_raw.astype(jnp.float32) + dt_bias.astype(jnp.float32))
    a = -jnp.exp(a_log.astype(jnp.float32))
    xh = xs.reshape(b, t, SSM_HEADS, SSM_HEAD_DIM)
    y = ssd_chunked(xh, dt, a, bm, cm)
    y = y + xh.astype(jnp.float32) * d_skip.astype(jnp.float32)[:, None]
    y = y.reshape(b, t, D_INNER) * jax.nn.silu(z.astype(jnp.float32))
    y = rmsnorm(y.reshape(b, t, SSM_GROUPS, -1), norm_w.reshape(SSM_GROUPS, -1)).reshape(b, t, D_INNER)
    return y.astype(h.dtype) @ w_out.astype(h.dtype)


def alibi_slopes(n_heads):
    return 2.0 ** (-8.0 * jnp.arange(1, n_heads + 1, dtype=jnp.float32) / n_heads)


def dilated_group_attention(q, k, v, window, dilation, slopes):
    f32 = jnp.float32
    b, t, nh, e = q.shape
    span = window // dilation
    blk = span
    lu = t // dilation
    nb = -(-lu // blk)
    lp = nb * blk

    def to_blocks(arr):
        arr = arr.reshape(b, lu, dilation, nh, e)
        arr = jnp.pad(arr, ((0, 0), (0, lp - lu), (0, 0), (0, 0), (0, 0)))
        return arr.reshape(b, nb, blk, dilation, nh, e)

    qb, kb, vb = to_blocks(q.astype(f32)), to_blocks(k.astype(f32)), to_blocks(v.astype(f32))
    pad_prev = ((0, 0), (1, 0), (0, 0), (0, 0), (0, 0), (0, 0))
    kcat = jnp.concatenate([jnp.pad(kb, pad_prev)[:, :nb], kb], axis=2)
    vcat = jnp.concatenate([jnp.pad(vb, pad_prev)[:, :nb], vb], axis=2)
    scores = jnp.einsum("bnqrhe,bnkrhe->bnrhqk", qb, kcat) * (1.0 / math.sqrt(e))
    qi = jnp.arange(blk)[:, None]
    ki = jnp.arange(2 * blk)[None, :]
    dist = qi + blk - ki
    in_band = (dist >= 0) & (dist <= span)
    key_u = jnp.arange(nb)[:, None] * blk - blk + jnp.arange(2 * blk)[None, :]
    valid = in_band[None] & (key_u >= 0)[:, None, :]
    bias = -slopes[:, None, None] * (dilation * dist).astype(f32)[None]
    logits = jnp.where(valid[None, :, None, None], scores + bias[None, None, None], -jnp.inf)
    lse = jax.nn.logsumexp(logits, axis=-1)
    probs = jnp.exp(logits - lse[..., None])
    out = jnp.einsum("bnrhqk,bnkrhe->bnqrhe", probs, vcat)
    out = out.reshape(b, lp, dilation, nh, e)[:, :lu].reshape(b, t, nh, e)
    lse = lse.transpose(0, 1, 4, 2, 3).reshape(b, lp, dilation, nh)[:, :lu].reshape(b, t, nh)
    return out, lse


def dilated_attention_mixer(h, w_qkv, q_gain, k_gain, w_o):
    b, t, _ = h.shape
    qkv = (h @ w_qkv.astype(h.dtype)).reshape(b, t, N_DIL_GROUPS, 3, ATT_HEADS, ATT_HEAD_DIM)
    q = rmsnorm(qkv[:, :, :, 0], q_gain)
    k = rmsnorm(qkv[:, :, :, 1], k_gain)
    v = qkv[:, :, :, 2]
    slopes = alibi_slopes(ATT_HEADS)
    outs, lses = [], []
    for g, (window, dilation) in enumerate(DIL_PATTERNS):
        o_g, l_g = dilated_group_attention(q[:, :, g], k[:, :, g], v[:, :, g], window, dilation, slopes)
        outs.append(o_g)
        lses.append(l_g)
    alpha = jax.nn.softmax(jnp.stack(lses), axis=0)
    o = jnp.einsum("gbth,gbthe->bthe", alpha, jnp.stack(outs))
    return o.reshape(b, t, ATT_HEADS * ATT_HEAD_DIM).astype(h.dtype) @ w_o.astype(h.dtype)


def swiglu(h, w_gate, w_up, w_down):
    return (jax.nn.silu(h @ w_gate.astype(h.dtype)) * (h @ w_up.astype(h.dtype))) @ w_down.astype(h.dtype)


def _fwd_setup_inputs(seed: int = 0) -> dict:
    key = jax.random.key(seed)
    ks = jax.random.split(key, 24)
    f32 = jnp.float32

    def nrm(k, shape, scale):
        return jax.random.normal(k, shape, f32) * scale

    nm, na = N_MAMBA_LAYERS, N_ATTN_LAYERS
    dt0 = jnp.exp(jax.random.uniform(ks[8], (nm, SSM_HEADS), f32, math.log(1e-3), math.log(1e-1)))
    return {
        "x": nrm(ks[0], (BATCH, SEQ, D_MODEL), 1.0),
        "p": nrm(ks[1], (DEPTH, BATCH, SEQ, PLE_DIM), 1.0),
        "norm_mix": 1.0 + nrm(ks[2], (DEPTH, D_MODEL), 0.02),
        "norm_ffn": 1.0 + nrm(ks[3], (DEPTH, D_MODEL), 0.02),
        "ssm_w_in": nrm(ks[4], (nm, D_MODEL, IN_PROJ_DIM), D_MODEL ** -0.5),
        "ssm_conv_w": nrm(ks[5], (nm, CONV_WIDTH, CONV_DIM), CONV_WIDTH ** -0.5),
        "ssm_conv_b": nrm(ks[6], (nm, CONV_DIM), 0.02),
        "ssm_dt_bias": dt0 + jnp.log(-jnp.expm1(-dt0)),
        "ssm_a_log": jnp.log(jax.random.uniform(ks[9], (nm, SSM_HEADS), f32, 1.0, 16.0)),
        "ssm_d_skip": 1.0 + nrm(ks[10], (nm, SSM_HEADS), 0.1),
        "ssm_norm_w": 1.0 + nrm(ks[11], (nm, D_INNER), 0.02),
        "ssm_w_out": nrm(ks[12], (nm, D_INNER, D_MODEL), D_INNER ** -0.5),
        "att_w_qkv": nrm(ks[13], (na, D_MODEL, QKV_DIM), D_MODEL ** -0.5),
        "att_q_norm": 1.0 + nrm(ks[14], (na, ATT_HEAD_DIM), 0.02),
        "att_k_norm": 1.0 + nrm(ks[15], (na, ATT_HEAD_DIM), 0.02),
        "att_w_o": nrm(ks[16], (na, ATT_HEADS * ATT_HEAD_DIM, D_MODEL), (ATT_HEADS * ATT_HEAD_DIM) ** -0.5),
        "ffn_w_gate": nrm(ks[17], (DEPTH, D_MODEL, FFN_HIDDEN), D_MODEL ** -0.5),
        "ffn_w_up": nrm(ks[18], (DEPTH, D_MODEL, FFN_HIDDEN), D_MODEL ** -0.5),
        "ffn_w_down": nrm(ks[19], (DEPTH, FFN_HIDDEN, D_MODEL), FFN_HIDDEN ** -0.5),
        "ple_w_proj": nrm(ks[20], (DEPTH, PLE_DIM, D_MODEL), PLE_DIM ** -0.5),
        "ple_w_gate": nrm(ks[21], (DEPTH, D_MODEL, D_MODEL), D_MODEL ** -0.5),
    }


def _fwd_reference(x, p, norm_mix, norm_ffn, ssm_w_in, ssm_conv_w, ssm_conv_b, ssm_dt_bias,
              ssm_a_log, ssm_d_skip, ssm_norm_w, ssm_w_out, att_w_qkv, att_q_norm,
              att_k_norm, att_w_o, ffn_w_gate, ffn_w_up, ffn_w_down, ple_w_proj, ple_w_gate):
    for i in range(DEPTH):
        j = i // N_MIXERS
        h = rmsnorm(x, norm_mix[i])
        if i % N_MIXERS == 0:
            mix = mamba2_mixer(h, ssm_w_in[j], ssm_conv_w[j], ssm_conv_b[j], ssm_dt_bias[j],
                               ssm_a_log[j], ssm_d_skip[j], ssm_norm_w[j], ssm_w_out[j])
        else:
            mix = dilated_attention_mixer(h, att_w_qkv[j], att_q_norm[j], att_k_norm[j], att_w_o[j])
        x = x + mix.astype(x.dtype)
        x = x + swiglu(rmsnorm(x, norm_ffn[i]), ffn_w_gate[i], ffn_w_up[i], ffn_w_down[i]).astype(x.dtype)
        gate = jax.nn.sigmoid((x @ ple_w_gate[i].astype(x.dtype)).astype(jnp.float32))
        ple = (p[i].astype(x.dtype) @ ple_w_proj[i].astype(x.dtype)).astype(jnp.float32)
        x = x + (gate * ple).astype(x.dtype)
    return x


import jax as _jax
import jax.numpy as _jnp

TWIN_FORMAT = 'train_step'
FWD_PARAMS = ['x', 'p', 'norm_mix', 'norm_ffn', 'ssm_w_in', 'ssm_conv_w', 'ssm_conv_b', 'ssm_dt_bias', 'ssm_a_log', 'ssm_d_skip', 'ssm_norm_w', 'ssm_w_out', 'att_w_qkv', 'att_q_norm', 'att_k_norm', 'att_w_o', 'ffn_w_gate', 'ffn_w_up', 'ffn_w_down', 'ple_w_proj', 'ple_w_gate']
TWIN_WEIGHTS = ['norm_mix', 'norm_ffn', 'ssm_w_in', 'ssm_conv_w', 'ssm_conv_b', 'ssm_dt_bias', 'ssm_a_log', 'ssm_d_skip', 'ssm_norm_w', 'ssm_w_out', 'att_w_qkv', 'att_q_norm', 'att_k_norm', 'att_w_o', 'ffn_w_gate', 'ffn_w_up', 'ffn_w_down', 'ple_w_proj', 'ple_w_gate']
TWIN_DIFF_INPUT = 'x'
TWIN_INPUTS = ['x', 'p', 'norm_mix', 'norm_ffn', 'ssm_w_in', 'ssm_conv_w', 'ssm_conv_b', 'ssm_dt_bias', 'ssm_a_log', 'ssm_d_skip', 'ssm_norm_w', 'ssm_w_out', 'att_w_qkv', 'att_q_norm', 'att_k_norm', 'att_w_o', 'ffn_w_gate', 'ffn_w_up', 'ffn_w_down', 'ple_w_proj', 'ple_w_gate', 'loss_target', 'm_norm_mix', 'm_norm_ffn', 'm_ssm_w_in', 'm_ssm_conv_w', 'm_ssm_conv_b', 'm_ssm_dt_bias', 'm_ssm_a_log', 'm_ssm_d_skip', 'm_ssm_norm_w', 'm_ssm_w_out', 'm_att_w_qkv', 'm_att_q_norm', 'm_att_k_norm', 'm_att_w_o', 'm_ffn_w_gate', 'm_ffn_w_up', 'm_ffn_w_down', 'm_ple_w_proj', 'm_ple_w_gate', 'v_norm_mix', 'v_norm_ffn', 'v_ssm_w_in', 'v_ssm_conv_w', 'v_ssm_conv_b', 'v_ssm_dt_bias', 'v_ssm_a_log', 'v_ssm_d_skip', 'v_ssm_norm_w', 'v_ssm_w_out', 'v_att_w_qkv', 'v_att_q_norm', 'v_att_k_norm', 'v_att_w_o', 'v_ffn_w_gate', 'v_ffn_w_up', 'v_ffn_w_down', 'v_ple_w_proj', 'v_ple_w_gate']
TWIN_OUTPUTS = ['loss', 'grad_x', 'grad_norm_mix', 'grad_norm_ffn', 'grad_ssm_w_in', 'grad_ssm_conv_w', 'grad_ssm_conv_b', 'grad_ssm_dt_bias', 'grad_ssm_a_log', 'grad_ssm_d_skip', 'grad_ssm_norm_w', 'grad_ssm_w_out', 'grad_att_w_qkv', 'grad_att_q_norm', 'grad_att_k_norm', 'grad_att_w_o', 'grad_ffn_w_gate', 'grad_ffn_w_up', 'grad_ffn_w_down', 'grad_ple_w_proj', 'grad_ple_w_gate', 'delta_norm_mix', 'delta_norm_ffn', 'delta_ssm_w_in', 'delta_ssm_conv_w', 'delta_ssm_conv_b', 'delta_ssm_dt_bias', 'delta_ssm_a_log', 'delta_ssm_d_skip', 'delta_ssm_norm_w', 'delta_ssm_w_out', 'delta_att_w_qkv', 'delta_att_q_norm', 'delta_att_k_norm', 'delta_att_w_o', 'delta_ffn_w_gate', 'delta_ffn_w_up', 'delta_ffn_w_down', 'delta_ple_w_proj', 'delta_ple_w_gate', 'new_m_norm_mix', 'new_m_norm_ffn', 'new_m_ssm_w_in', 'new_m_ssm_conv_w', 'new_m_ssm_conv_b', 'new_m_ssm_dt_bias', 'new_m_ssm_a_log', 'new_m_ssm_d_skip', 'new_m_ssm_norm_w', 'new_m_ssm_w_out', 'new_m_att_w_qkv', 'new_m_att_q_norm', 'new_m_att_k_norm', 'new_m_att_w_o', 'new_m_ffn_w_gate', 'new_m_ffn_w_up', 'new_m_ffn_w_down', 'new_m_ple_w_proj', 'new_m_ple_w_gate', 'new_v_norm_mix', 'new_v_norm_ffn', 'new_v_ssm_w_in', 'new_v_ssm_conv_w', 'new_v_ssm_conv_b', 'new_v_ssm_dt_bias', 'new_v_ssm_a_log', 'new_v_ssm_d_skip', 'new_v_ssm_norm_w', 'new_v_ssm_w_out', 'new_v_att_w_qkv', 'new_v_att_q_norm', 'new_v_att_k_norm', 'new_v_att_w_o', 'new_v_ffn_w_gate', 'new_v_ffn_w_up', 'new_v_ffn_w_down', 'new_v_ple_w_proj', 'new_v_ple_w_gate']
TWIN_LEAF_KINDS = {'loss': 'loss', 'grad_x': 'grad_x', 'grad_norm_mix': 'grad_w', 'grad_norm_ffn': 'grad_w', 'grad_ssm_w_in': 'grad_w', 'grad_ssm_conv_w': 'grad_w', 'grad_ssm_conv_b': 'grad_w', 'grad_ssm_dt_bias': 'grad_w', 'grad_ssm_a_log': 'grad_w', 'grad_ssm_d_skip': 'grad_w', 'grad_ssm_norm_w': 'grad_w', 'grad_ssm_w_out': 'grad_w', 'grad_att_w_qkv': 'grad_w', 'grad_att_q_norm': 'grad_w', 'grad_att_k_norm': 'grad_w', 'grad_att_w_o': 'grad_w', 'grad_ffn_w_gate': 'grad_w', 'grad_ffn_w_up': 'grad_w', 'grad_ffn_w_down': 'grad_w', 'grad_ple_w_proj': 'grad_w', 'grad_ple_w_gate': 'grad_w', 'delta_norm_mix': 'delta_w', 'delta_norm_ffn': 'delta_w', 'delta_ssm_w_in': 'delta_w', 'delta_ssm_conv_w': 'delta_w', 'delta_ssm_conv_b': 'delta_w', 'delta_ssm_dt_bias': 'delta_w', 'delta_ssm_a_log': 'delta_w', 'delta_ssm_d_skip': 'delta_w', 'delta_ssm_norm_w': 'delta_w', 'delta_ssm_w_out': 'delta_w', 'delta_att_w_qkv': 'delta_w', 'delta_att_q_norm': 'delta_w', 'delta_att_k_norm': 'delta_w', 'delta_att_w_o': 'delta_w', 'delta_ffn_w_gate': 'delta_w', 'delta_ffn_w_up': 'delta_w', 'delta_ffn_w_down': 'delta_w', 'delta_ple_w_proj': 'delta_w', 'delta_ple_w_gate': 'delta_w', 'new_m_norm_mix': 'new_m', 'new_m_norm_ffn': 'new_m', 'new_m_ssm_w_in': 'new_m', 'new_m_ssm_conv_w': 'new_m', 'new_m_ssm_conv_b': 'new_m', 'new_m_ssm_dt_bias': 'new_m', 'new_m_ssm_a_log': 'new_m', 'new_m_ssm_d_skip': 'new_m', 'new_m_ssm_norm_w': 'new_m', 'new_m_ssm_w_out': 'new_m', 'new_m_att_w_qkv': 'new_m', 'new_m_att_q_norm': 'new_m', 'new_m_att_k_norm': 'new_m', 'new_m_att_w_o': 'new_m', 'new_m_ffn_w_gate': 'new_m', 'new_m_ffn_w_up': 'new_m', 'new_m_ffn_w_down': 'new_m', 'new_m_ple_w_proj': 'new_m', 'new_m_ple_w_gate': 'new_m', 'new_v_norm_mix': 'new_v', 'new_v_norm_ffn': 'new_v', 'new_v_ssm_w_in': 'new_v', 'new_v_ssm_conv_w': 'new_v', 'new_v_ssm_conv_b': 'new_v', 'new_v_ssm_dt_bias': 'new_v', 'new_v_ssm_a_log': 'new_v', 'new_v_ssm_d_skip': 'new_v', 'new_v_ssm_norm_w': 'new_v', 'new_v_ssm_w_out': 'new_v', 'new_v_att_w_qkv': 'new_v', 'new_v_att_q_norm': 'new_v', 'new_v_att_k_norm': 'new_v', 'new_v_att_w_o': 'new_v', 'new_v_ffn_w_gate': 'new_v', 'new_v_ffn_w_up': 'new_v', 'new_v_ffn_w_down': 'new_v', 'new_v_ple_w_proj': 'new_v', 'new_v_ple_w_gate': 'new_v'}


def _forward(args):
    return _fwd_reference(*[args[k] for k in FWD_PARAMS])


def _output_shape():
    def fwd():
        inp = _fwd_setup_inputs(0)
        return _fwd_reference(*[inp[k] for k in FWD_PARAMS])
    out = _jax.eval_shape(fwd)
    return out.shape, out.dtype

N_MICROBATCH = 1
ADAM_LR = 0.001
ADAM_B1 = 0.9
ADAM_B2 = 0.999
ADAM_EPS = 1e-08
ADAM_WD = 0.01
ADAM_STEP = 10
PER_EXAMPLE_BATCH_AXIS = {'x': 0, 'p': 1, 'loss_target': 0}
SHARED_INPUTS = []
_WEIGHT_DTYPES = {'norm_mix': _jnp.float32, 'norm_ffn': _jnp.float32, 'ssm_w_in': _jnp.float32, 'ssm_conv_w': _jnp.float32, 'ssm_conv_b': _jnp.float32, 'ssm_dt_bias': _jnp.float32, 'ssm_a_log': _jnp.float32, 'ssm_d_skip': _jnp.float32, 'ssm_norm_w': _jnp.float32, 'ssm_w_out': _jnp.float32, 'att_w_qkv': _jnp.float32, 'att_q_norm': _jnp.float32, 'att_k_norm': _jnp.float32, 'att_w_o': _jnp.float32, 'ffn_w_gate': _jnp.float32, 'ffn_w_up': _jnp.float32, 'ffn_w_down': _jnp.float32, 'ple_w_proj': _jnp.float32, 'ple_w_gate': _jnp.float32}
MOMENT_SCALE = {'norm_mix': 2.198211e+00, 'norm_ffn': 2.530267e+01, 'ssm_w_in': 4.182957e-01, 'ssm_conv_w': 9.727939e-01, 'ssm_conv_b': 3.237719e+00, 'ssm_dt_bias': 1.910288e+00, 'ssm_a_log': 9.968449e+00, 'ssm_d_skip': 9.300198e+00, 'ssm_norm_w': 2.330067e+01, 'ssm_w_out': 2.497783e+00, 'att_w_qkv': 2.728200e-01, 'att_q_norm': 1.823197e+01, 'att_k_norm': 1.824708e+01, 'att_w_o': 8.293288e-01, 'ffn_w_gate': 4.686426e-01, 'ffn_w_up': 3.340103e-01, 'ffn_w_down': 5.261277e-01, 'ple_w_proj': 5.889339e-01, 'ple_w_gate': 5.389801e-01}


def _to_microbatches(a, axis):
    t = _jnp.moveaxis(a, axis, 0)
    t = t.reshape((N_MICROBATCH, t.shape[0] // N_MICROBATCH) + t.shape[1:])
    return _jnp.moveaxis(t, 1, axis + 1)


def setup_inputs(seed: int = 0) -> dict:
    inp = _fwd_setup_inputs(seed)
    key = _jax.random.fold_in(_jax.random.key(seed), 7919)
    shape, _ = _output_shape()
    out = dict(inp)
    out["loss_target"] = _jax.random.normal(_jax.random.fold_in(key, 0), shape, _jnp.float32)
    for i, name in enumerate(TWIN_WEIGHTS):
        w = inp[name].astype(_jnp.float32)
        if MOMENT_SCALE is None:
            s = _jnp.sqrt(_jnp.mean(_jnp.square(w)) + 1e-30)
        else:
            s = MOMENT_SCALE[name]
        km, kv = _jax.random.split(_jax.random.fold_in(key, i + 1))
        out[name] = w
        out["m_" + name] = s * _jax.random.normal(km, w.shape, _jnp.float32)
        out["v_" + name] = (s * s) * _jax.random.uniform(kv, w.shape, _jnp.float32, 0.5, 1.5)
    if N_MICROBATCH > 1:
        for name, axis in PER_EXAMPLE_BATCH_AXIS.items():
            out[name] = _to_microbatches(out[name], axis)
    return {'x': out['x'], 'p': out['p'], 'norm_mix': out['norm_mix'], 'norm_ffn': out['norm_ffn'], 'ssm_w_in': out['ssm_w_in'], 'ssm_conv_w': out['ssm_conv_w'], 'ssm_conv_b': out['ssm_conv_b'], 'ssm_dt_bias': out['ssm_dt_bias'], 'ssm_a_log': out['ssm_a_log'], 'ssm_d_skip': out['ssm_d_skip'], 'ssm_norm_w': out['ssm_norm_w'], 'ssm_w_out': out['ssm_w_out'], 'att_w_qkv': out['att_w_qkv'], 'att_q_norm': out['att_q_norm'], 'att_k_norm': out['att_k_norm'], 'att_w_o': out['att_w_o'], 'ffn_w_gate': out['ffn_w_gate'], 'ffn_w_up': out['ffn_w_up'], 'ffn_w_down': out['ffn_w_down'], 'ple_w_proj': out['ple_w_proj'], 'ple_w_gate': out['ple_w_gate'], 'loss_target': out['loss_target'], 'm_norm_mix': out['m_norm_mix'], 'm_norm_ffn': out['m_norm_ffn'], 'm_ssm_w_in': out['m_ssm_w_in'], 'm_ssm_conv_w': out['m_ssm_conv_w'], 'm_ssm_conv_b': out['m_ssm_conv_b'], 'm_ssm_dt_bias': out['m_ssm_dt_bias'], 'm_ssm_a_log': out['m_ssm_a_log'], 'm_ssm_d_skip': out['m_ssm_d_skip'], 'm_ssm_norm_w': out['m_ssm_norm_w'], 'm_ssm_w_out': out['m_ssm_w_out'], 'm_att_w_qkv': out['m_att_w_qkv'], 'm_att_q_norm': out['m_att_q_norm'], 'm_att_k_norm': out['m_att_k_norm'], 'm_att_w_o': out['m_att_w_o'], 'm_ffn_w_gate': out['m_ffn_w_gate'], 'm_ffn_w_up': out['m_ffn_w_up'], 'm_ffn_w_down': out['m_ffn_w_down'], 'm_ple_w_proj': out['m_ple_w_proj'], 'm_ple_w_gate': out['m_ple_w_gate'], 'v_norm_mix': out['v_norm_mix'], 'v_norm_ffn': out['v_norm_ffn'], 'v_ssm_w_in': out['v_ssm_w_in'], 'v_ssm_conv_w': out['v_ssm_conv_w'], 'v_ssm_conv_b': out['v_ssm_conv_b'], 'v_ssm_dt_bias': out['v_ssm_dt_bias'], 'v_ssm_a_log': out['v_ssm_a_log'], 'v_ssm_d_skip': out['v_ssm_d_skip'], 'v_ssm_norm_w': out['v_ssm_norm_w'], 'v_ssm_w_out': out['v_ssm_w_out'], 'v_att_w_qkv': out['v_att_w_qkv'], 'v_att_q_norm': out['v_att_q_norm'], 'v_att_k_norm': out['v_att_k_norm'], 'v_att_w_o': out['v_att_w_o'], 'v_ffn_w_gate': out['v_ffn_w_gate'], 'v_ffn_w_up': out['v_ffn_w_up'], 'v_ffn_w_down': out['v_ffn_w_down'], 'v_ple_w_proj': out['v_ple_w_proj'], 'v_ple_w_gate': out['v_ple_w_gate']}


def _loss(weights, diff, rest, loss_target):
    with _jax.named_scope("forward"):
        args = {**rest, TWIN_DIFF_INPUT: diff, **{k: w.astype(_WEIGHT_DTYPES[k]) for k, w in weights.items()}}
        y = _forward(args)
    with _jax.named_scope("loss_head"):
        err = _jnp.square(y.astype(_jnp.float32) - loss_target)
        return 0.5 * _jnp.sum(_jnp.mean(err, axis=-1)) if err.ndim else 0.5 * err


def _adamw(w, g, m, v):
    m = ADAM_B1 * m + (1.0 - ADAM_B1) * g
    v = ADAM_B2 * v + (1.0 - ADAM_B2) * _jnp.square(g)
    m_hat = m / (1.0 - ADAM_B1 ** ADAM_STEP)
    v_hat = v / (1.0 - ADAM_B2 ** ADAM_STEP)
    delta = -ADAM_LR * (m_hat / (_jnp.sqrt(v_hat) + ADAM_EPS) + ADAM_WD * w)
    return delta, m, v


def reference(x, p, norm_mix, norm_ffn, ssm_w_in, ssm_conv_w, ssm_conv_b, ssm_dt_bias, ssm_a_log, ssm_d_skip, ssm_norm_w, ssm_w_out, att_w_qkv, att_q_norm, att_k_norm, att_w_o, ffn_w_gate, ffn_w_up, ffn_w_down, ple_w_proj, ple_w_gate, loss_target, m_norm_mix, m_norm_ffn, m_ssm_w_in, m_ssm_conv_w, m_ssm_conv_b, m_ssm_dt_bias, m_ssm_a_log, m_ssm_d_skip, m_ssm_norm_w, m_ssm_w_out, m_att_w_qkv, m_att_q_norm, m_att_k_norm, m_att_w_o, m_ffn_w_gate, m_ffn_w_up, m_ffn_w_down, m_ple_w_proj, m_ple_w_gate, v_norm_mix, v_norm_ffn, v_ssm_w_in, v_ssm_conv_w, v_ssm_conv_b, v_ssm_dt_bias, v_ssm_a_log, v_ssm_d_skip, v_ssm_norm_w, v_ssm_w_out, v_att_w_qkv, v_att_q_norm, v_att_k_norm, v_att_w_o, v_ffn_w_gate, v_ffn_w_up, v_ffn_w_down, v_ple_w_proj, v_ple_w_gate):
    given = dict(x=x, p=p, norm_mix=norm_mix, norm_ffn=norm_ffn, ssm_w_in=ssm_w_in, ssm_conv_w=ssm_conv_w, ssm_conv_b=ssm_conv_b, ssm_dt_bias=ssm_dt_bias, ssm_a_log=ssm_a_log, ssm_d_skip=ssm_d_skip, ssm_norm_w=ssm_norm_w, ssm_w_out=ssm_w_out, att_w_qkv=att_w_qkv, att_q_norm=att_q_norm, att_k_norm=att_k_norm, att_w_o=att_w_o, ffn_w_gate=ffn_w_gate, ffn_w_up=ffn_w_up, ffn_w_down=ffn_w_down, ple_w_proj=ple_w_proj, ple_w_gate=ple_w_gate, loss_target=loss_target, m_norm_mix=m_norm_mix, m_norm_ffn=m_norm_ffn, m_ssm_w_in=m_ssm_w_in, m_ssm_conv_w=m_ssm_conv_w, m_ssm_conv_b=m_ssm_conv_b, m_ssm_dt_bias=m_ssm_dt_bias, m_ssm_a_log=m_ssm_a_log, m_ssm_d_skip=m_ssm_d_skip, m_ssm_norm_w=m_ssm_norm_w, m_ssm_w_out=m_ssm_w_out, m_att_w_qkv=m_att_w_qkv, m_att_q_norm=m_att_q_norm, m_att_k_norm=m_att_k_norm, m_att_w_o=m_att_w_o, m_ffn_w_gate=m_ffn_w_gate, m_ffn_w_up=m_ffn_w_up, m_ffn_w_down=m_ffn_w_down, m_ple_w_proj=m_ple_w_proj, m_ple_w_gate=m_ple_w_gate, v_norm_mix=v_norm_mix, v_norm_ffn=v_norm_ffn, v_ssm_w_in=v_ssm_w_in, v_ssm_conv_w=v_ssm_conv_w, v_ssm_conv_b=v_ssm_conv_b, v_ssm_dt_bias=v_ssm_dt_bias, v_ssm_a_log=v_ssm_a_log, v_ssm_d_skip=v_ssm_d_skip, v_ssm_norm_w=v_ssm_norm_w, v_ssm_w_out=v_ssm_w_out, v_att_w_qkv=v_att_w_qkv, v_att_q_norm=v_att_q_norm, v_att_k_norm=v_att_k_norm, v_att_w_o=v_att_w_o, v_ffn_w_gate=v_ffn_w_gate, v_ffn_w_up=v_ffn_w_up, v_ffn_w_down=v_ffn_w_down, v_ple_w_proj=v_ple_w_proj, v_ple_w_gate=v_ple_w_gate)
    weights = {n: given[n] for n in TWIN_WEIGHTS}
    shared = {n: given[n] for n in SHARED_INPUTS}
    per_example = {n: given[n] for n in ['x', 'p']}
    grad_fn = _jax.value_and_grad(_loss, argnums=(0, 1))

    def one_microbatch(ex, loss_target):
        ex = dict(ex)
        diff = ex.pop(TWIN_DIFF_INPUT)
        return grad_fn(weights, diff, {**shared, **ex}, loss_target)

    if N_MICROBATCH == 1:
        loss, (grad_w, grad_x) = one_microbatch(per_example, given["loss_target"])
    else:
        def body(carry, xs):
            loss_sum, grad_sum = carry
            l_k, (gw_k, gx_k) = one_microbatch(xs[0], xs[1])
            with _jax.named_scope("update"):
                return (loss_sum + l_k, _jax.tree.map(_jnp.add, grad_sum, gw_k)), gx_k

        init = (_jnp.zeros((), _jnp.float32), _jax.tree.map(_jnp.zeros_like, weights))
        (loss, grad_w), grad_x = _jax.lax.scan(body, init, (per_example, given["loss_target"]))
    with _jax.named_scope("update"):
        delta_w, new_m, new_v = {}, {}, {}
        for n in TWIN_WEIGHTS:
            delta_w[n], new_m[n], new_v[n] = _adamw(weights[n], grad_w[n], given["m_" + n], given["v_" + n])
    return (loss, grad_x, *[grad_w[n] for n in TWIN_WEIGHTS], *[delta_w[n] for n in TWIN_WEIGHTS],
            *[new_m[n] for n in TWIN_WEIGHTS], *[new_v[n] for n in TWIN_WEIGHTS])
```

```python
import functools
import math

import jax
import jax.numpy as jnp
from jax import lax
from jax.experimental import pallas as pl
from jax.experimental.pallas import tpu as pltpu

F32 = jnp.float32
BF16 = jnp.bfloat16
HIGHEST = lax.Precision.HIGHEST

NORM_EPS = 1e-6
ADAM_LR, ADAM_B1, ADAM_B2, ADAM_EPS, ADAM_WD, ADAM_STEP = 0.001, 0.9, 0.999, 1e-08, 0.01, 10

D_MODEL = 1024
D_INNER = 2048
SSM_HEADS = 32
SSM_HEAD_DIM = 64
SSM_GROUPS = 4
SSM_STATE = 128
SSD_CHUNK = 128
CONV_DIM = 3072
CONV_WIDTH = 4
ATT_HEADS = 16
ATT_HEAD_DIM = 64
DIL_PATTERNS = ((128, 1), (512, 4), (2048, 16))
ATT_BLOCK = 128
FFN_HIDDEN = 2816
PLE_DIM = 256

LANES = 128
V7X_VMEM_LIMIT = 56 * 1024 * 1024
NEG_BIG = -1e30

N_CHIPS = 4


def _params(*sem):
    return pltpu.CompilerParams(dimension_semantics=sem, vmem_limit_bytes=V7X_VMEM_LIMIT)


def _tile(n, pref):
    if n <= pref:
        return n
    best = None
    for t in range(LANES, pref + 1, LANES):
        if n % t == 0:
            best = t
    assert best is not None, (n, pref)
    return best


def _sigmoid(v):
    return 1.0 / (1.0 + jnp.exp(-v))


def _dot(a, b):
    return jnp.dot(a, b, preferred_element_type=F32)


def _dot_nt(a, b):
    return lax.dot_general(a, b, (((1,), (1,)), ((), ())), preferred_element_type=F32)


def _dot_tn(a, b):
    return lax.dot_general(a, b, (((0,), (0,)), ((), ())), preferred_element_type=F32)


def _head_block_diag():
    i = lax.broadcasted_iota(jnp.int32, (LANES, LANES), 0) // ATT_HEAD_DIM
    j = lax.broadcasted_iota(jnp.int32, (LANES, LANES), 1) // ATT_HEAD_DIM
    return (i == j).astype(F32)


def _head_sums(z, bd):
    parts = []
    for t in range(z.shape[1] // LANES):
        zt = z[:, t * LANES:(t + 1) * LANES]
        parts.append(jnp.dot(zt, bd, preferred_element_type=F32, precision=HIGHEST))
    return parts[0] if len(parts) == 1 else jnp.concatenate(parts, axis=1)


def _lane_lt64(rows):
    return lax.broadcasted_iota(jnp.int32, (rows, LANES), 1) < ATT_HEAD_DIM


def _matmul(a, b, *, mode, name, out_dtype=F32, addend=None, tm=1024, tn=512, tk_max=3072):
    m, k = a.shape
    if mode == "nn":
        k2, n = b.shape
    else:
        n, k2 = b.shape
    assert k == k2, (a.shape, b.shape, mode)
    tm, tn, tk = _tile(m, tm), _tile(n, tn), _tile(k, tk_max)
    nk = k // tk
    has_add = addend is not None

    def body(*refs):
        a_ref, b_ref = refs[0], refs[1]
        add_ref = refs[2] if has_add else None
        o_ref, acc_ref = refs[-2], refs[-1]
        kk = pl.program_id(2)
        av = a_ref[...].astype(BF16)
        bv = b_ref[...].astype(BF16)
        part = _dot(av, bv) if mode == "nn" else _dot_nt(av, bv)

        @pl.when(kk == 0)
        def _():
            acc_ref[...] = part

        @pl.when(kk > 0)
        def _():
            acc_ref[...] += part

        @pl.when(kk == nk - 1)
        def _():
            res = acc_ref[...]
            if has_add:
                res = res + add_ref[...]
            o_ref[...] = res.astype(out_dtype)

    a_spec = pl.BlockSpec((tm, tk), lambda i, j, kk: (i, kk))
    if mode == "nn":
        b_spec = pl.BlockSpec((tk, tn), lambda i, j, kk: (kk, j))
    else:
        b_spec = pl.BlockSpec((tn, tk), lambda i, j, kk: (j, kk))
    in_specs = [a_spec, b_spec]
    args = [a, b]
    if has_add:
        in_specs.append(pl.BlockSpec((tm, tn), lambda i, j, kk: (i, j)))
        args.append(addend)
    return pl.pallas_call(
        body, name=name, grid=(m // tm, n // tn, nk),
        in_specs=in_specs, out_specs=pl.BlockSpec((tm, tn), lambda i, j, kk: (i, j)),
        out_shape=jax.ShapeDtypeStruct((m, n), out_dtype),
        scratch_shapes=[pltpu.VMEM((tm, tn), F32)],
        compiler_params=_params("parallel", "parallel", "arbitrary"),
    )(*args)


def _matmul_tn(a, b, *, name, tm=1024, tn=512, tk=1024):
    t, m = a.shape
    t2, n = b.shape
    assert t == t2
    tm, tn, tk = _tile(m, tm), _tile(n, tn), _tile(t, tk)

    def body(a_ref, b_ref, o_ref):
        part = _dot_tn(a_ref[...].astype(BF16), b_ref[...].astype(BF16))

        @pl.when(pl.program_id(2) == 0)
        def _():
            o_ref[...] = part

        @pl.when(pl.program_id(2) > 0)
        def _():
            o_ref[...] += part

    return pl.pallas_call(
        body, name=name, grid=(m // tm, n // tn, t // tk),
        in_specs=[pl.BlockSpec((tk, tm), lambda i, j, kk: (kk, i)),
                  pl.BlockSpec((tk, tn), lambda i, j, kk: (kk, j))],
        out_specs=pl.BlockSpec((tm, tn), lambda i, j, kk: (i, j)),
        out_shape=jax.ShapeDtypeStruct((m, n), F32),
        compiler_params=_params("parallel", "parallel", "arbitrary"),
    )(a, b)


def _rmsnorm_fwd(x, gain, *, name):
    t, d = x.shape
    tm = _tile(t, 512)

    def body(x_ref, g_ref, o_ref):
        xv = x_ref[...]
        r = lax.rsqrt(jnp.mean(xv * xv, axis=-1, keepdims=True) + NORM_EPS)
        o_ref[...] = (xv * r * g_ref[...]).astype(BF16)

    return pl.pallas_call(
        body, name=name, grid=(t // tm,),
        in_specs=[pl.BlockSpec((tm, d), lambda i: (i, 0)), pl.BlockSpec((1, d), lambda i: (0, 0))],
        out_specs=pl.BlockSpec((tm, d), lambda i: (i, 0)),
        out_shape=jax.ShapeDtypeStruct((t, d), BF16),
        compiler_params=_params("parallel"),
    )(x, gain)


def _rmsnorm_bwd(x, gain, dy, dres, *, name):
    t, d = x.shape
    tm = _tile(t, 512)

    def body(x_ref, g_ref, dy_ref, dres_ref, dx_ref, dg_ref):
        xv = x_ref[...]
        r = lax.rsqrt(jnp.mean(xv * xv, axis=-1, keepdims=True) + NORM_EPS)
        xh = xv * r
        dyv = dy_ref[...]
        dxh = dyv * g_ref[...]
        mean = jnp.mean(dxh * xh, axis=-1, keepdims=True)
        dx_ref[...] = dres_ref[...] + r * (dxh - xh * mean)
        part = jnp.sum(dyv * xh, axis=0, keepdims=True)

        @pl.when(pl.program_id(0) == 0)
        def _():
            dg_ref[...] = part

        @pl.when(pl.program_id(0) > 0)
        def _():
            dg_ref[...] += part

    row = pl.BlockSpec((tm, d), lambda i: (i, 0))
    vec = pl.BlockSpec((1, d), lambda i: (0, 0))
    return pl.pallas_call(
        body, name=name, grid=(t // tm,),
        in_specs=[row, vec, row, row], out_specs=[row, vec],
        out_shape=[jax.ShapeDtypeStruct((t, d), F32), jax.ShapeDtypeStruct((1, d), F32)],
        compiler_params=_params("arbitrary"),
    )(x, gain, dy, dres)


def _loss_head(y, target):
    t, d = y.shape
    tm = _tile(t, 512)
    steps = t // tm

    def body(y_ref, t_ref, dy_ref, l_ref, acc_ref):
        e = y_ref[...] - t_ref[...]
        dy_ref[...] = e * (1.0 / d)
        part = jnp.sum(e * e, axis=0, keepdims=True)

        @pl.when(pl.program_id(0) == 0)
        def _():
            acc_ref[...] = part

        @pl.when(pl.program_id(0) > 0)
        def _():
            acc_ref[...] += part

        @pl.when(pl.program_id(0) == steps - 1)
        def _():
            l_ref[...] = jnp.full((1, LANES), (0.5 / d), F32) * jnp.sum(acc_ref[...])

    row = pl.BlockSpec((tm, d), lambda i: (i, 0))
    return pl.pallas_call(
        body, name="loss_head", grid=(steps,),
        in_specs=[row, row], out_specs=[row, pl.BlockSpec((1, LANES), lambda i: (0, 0))],
        out_shape=[jax.ShapeDtypeStruct((t, d), F32), jax.ShapeDtypeStruct((1, LANES), F32)],
        scratch_shapes=[pltpu.VMEM((1, d), F32)],
        compiler_params=_params("arbitrary"),
    )(y, target)


def _swiglu_fwd(h, w_gate, w_up, *, name):
    t, d = h.shape
    f = w_gate.shape[1]
    tm, tn = _tile(t, 1024), _tile(f, 256)

    def body(h_ref, wg_ref, wu_ref, g_ref, u_ref, a_ref):
        hv = h_ref[...]
        g = _dot(hv, wg_ref[...])
        u = _dot(hv, wu_ref[...])
        g_ref[...] = g
        u_ref[...] = u
        a_ref[...] = (g * _sigmoid(g) * u).astype(BF16)

    wspec = pl.BlockSpec((d, tn), lambda i, j: (0, j))
    ospec = pl.BlockSpec((tm, tn), lambda i, j: (i, j))
    return pl.pallas_call(
        body, name=name, grid=(t // tm, f // tn),
        in_specs=[pl.BlockSpec((tm, d), lambda i, j: (i, 0)), wspec, wspec],
        out_specs=[ospec, ospec, ospec],
        out_shape=[jax.ShapeDtypeStruct((t, f), F32), jax.ShapeDtypeStruct((t, f), F32),
                   jax.ShapeDtypeStruct((t, f), BF16)],
        compiler_params=_params("parallel", "parallel"),
    )(h, w_gate, w_up)


def _swiglu_bwd(dx, w_down, g, u, *, name):
    t, d = dx.shape
    f = w_down.shape[0]
    tm, tn = _tile(t, 1024), _tile(f, 256)

    def body(dx_ref, wd_ref, g_ref, u_ref, dg_ref, du_ref):
        dact = _dot_nt(dx_ref[...].astype(BF16), wd_ref[...])
        gv, uv = g_ref[...], u_ref[...]
        sg = _sigmoid(gv)
        dg_ref[...] = (dact * uv * sg * (1.0 + gv * (1.0 - sg))).astype(BF16)
        du_ref[...] = (dact * gv * sg).astype(BF16)

    ospec = pl.BlockSpec((tm, tn), lambda i, j: (i, j))
    return pl.pallas_call(
        body, name=name, grid=(t // tm, f // tn),
        in_specs=[pl.BlockSpec((tm, d), lambda i, j: (i, 0)), pl.BlockSpec((tn, d), lambda i, j: (j, 0)),
                  ospec, ospec],
        out_specs=[ospec, ospec],
        out_shape=[jax.ShapeDtypeStruct((t, f), BF16), jax.ShapeDtypeStruct((t, f), BF16)],
        compiler_params=_params("parallel", "parallel"),
    )(dx, w_down, g, u)


def _ple_fwd(x, p, w_gate, w_proj, *, name):
    t, d = x.shape
    e = p.shape[1]
    tm, tn = _tile(t, 1024), _tile(d, 512)

    def body(xf_ref, xr_ref, p_ref, wg_ref, wp_ref, o_ref):
        s = _dot(xf_ref[...].astype(BF16), wg_ref[...])
        ple = _dot(p_ref[...].astype(BF16), wp_ref[...])
        o_ref[...] = xr_ref[...] + _sigmoid(s) * ple

    return pl.pallas_call(
        body, name=name, grid=(t // tm, d // tn),
        in_specs=[pl.BlockSpec((tm, d), lambda i, j: (i, 0)), pl.BlockSpec((tm, tn), lambda i, j: (i, j)),
                  pl.BlockSpec((tm, e), lambda i, j: (i, 0)), pl.BlockSpec((d, tn), lambda i, j: (0, j)),
                  pl.BlockSpec((e, tn), lambda i, j: (0, j))],
        out_specs=pl.BlockSpec((tm, tn), lambda i, j: (i, j)),
        out_shape=jax.ShapeDtypeStruct((t, d), F32),
        compiler_params=_params("parallel", "parallel"),
    )(x, x, p, w_gate, w_proj)


def _ple_bwd(x, p, w_gate, w_proj, dout, *, name):
    t, d = x.shape
    e = p.shape[1]
    tm, tn = _tile(t, 1024), _tile(d, 512)

    def body(xf_ref, p_ref, wg_ref, wp_ref, do_ref, ds_ref, dple_ref):
        s = _dot(xf_ref[...].astype(BF16), wg_ref[...])
        ple = _dot(p_ref[...].astype(BF16), wp_ref[...])
        gate = _sigmoid(s)
        dov = do_ref[...]
        dple_ref[...] = (dov * gate).astype(BF16)
        ds_ref[...] = (dov * ple * gate * (1.0 - gate)).astype(BF16)

    ospec = pl.BlockSpec((tm, tn), lambda i, j: (i, j))
    return pl.pallas_call(
        body, name=name, grid=(t // tm, d // tn),
        in_specs=[pl.BlockSpec((tm, d), lambda i, j: (i, 0)), pl.BlockSpec((tm, e), lambda i, j: (i, 0)),
                  pl.BlockSpec((d, tn), lambda i, j: (0, j)), pl.BlockSpec((e, tn), lambda i, j: (0, j)), ospec],
        out_specs=[ospec, ospec],
        out_shape=[jax.ShapeDtypeStruct((t, d), BF16), jax.ShapeDtypeStruct((t, d), BF16)],
        compiler_params=_params("parallel", "parallel"),
    )(x, p, w_gate, w_proj, dout)


CONV_TIME_TILE = 256
CONV_HALO = 8


def _conv_taps(ext, w):
    acc = ext[CONV_HALO:, :] * w[CONV_WIDTH - 1:CONV_WIDTH, :]
    shifted = [ext[CONV_HALO:, :]]
    for j in range(1, CONV_WIDTH):
        sh = pltpu.roll(ext, j, 0)[CONV_HALO:, :]
        shifted.append(sh)
        acc = acc + sh * w[CONV_WIDTH - 1 - j:CONV_WIDTH - j, :]
    return acc, shifted


def _conv_fwd(u, w, b):
    t, c = u.shape
    tc = _tile(c, 256)
    tt = CONV_TIME_TILE

    def body(u_ref, w_ref, b_ref, o_ref):
        wv, bv = w_ref[...], b_ref[...]

        def tile(start, ext):
            pre = _conv_taps(ext, wv)[0] + bv
            o_ref[pl.ds(start, tt), :] = pre * _sigmoid(pre)

        tile(0, jnp.concatenate([jnp.zeros((CONV_HALO, tc), F32), u_ref[0:tt, :]], axis=0))

        def loop(i, carry):
            start = pl.multiple_of(i * tt, tt)
            tile(start, u_ref[pl.ds(start - CONV_HALO, tt + CONV_HALO), :])
            return carry

        lax.fori_loop(1, t // tt, loop, 0)

    col = pl.BlockSpec((t, tc), lambda j: (0, j))
    return pl.pallas_call(
        body, name="conv_fwd", grid=(c // tc,),
        in_specs=[col, pl.BlockSpec((CONV_WIDTH, tc), lambda j: (0, j)), pl.BlockSpec((1, tc), lambda j: (0, j))],
        out_specs=col, out_shape=jax.ShapeDtypeStruct((t, c), F32),
        compiler_params=_params("parallel"),
    )(u, w, b)


def _conv_bwd(u, w, b, dact):
    t, c = u.shape
    tc = _tile(c, 256)
    tt = CONV_TIME_TILE

    def body(u_ref, w_ref, b_ref, da_ref, du_ref, dw_ref, db_ref, dpre_ref):
        wv, bv = w_ref[...], b_ref[...]

        def tile(start, ext, sums):
            acc, shifted = _conv_taps(ext, wv)
            pre = acc + bv
            sg = _sigmoid(pre)
            dpre = da_ref[pl.ds(start, tt), :] * (sg * (1.0 + pre * (1.0 - sg)))
            dpre_ref[pl.ds(start, tt), :] = dpre
            new = [sums[0] + jnp.sum(dpre, axis=0, keepdims=True)]
            for j in range(CONV_WIDTH):
                new.append(sums[1 + j] + jnp.sum(dpre * shifted[j], axis=0, keepdims=True))
            return tuple(new)

        zero = jnp.zeros((1, tc), F32)
        sums = tile(0, jnp.concatenate([jnp.zeros((CONV_HALO, tc), F32), u_ref[0:tt, :]], axis=0),
                    (zero,) * (1 + CONV_WIDTH))

        def loop(i, sums):
            start = pl.multiple_of(i * tt, tt)
            return tile(start, u_ref[pl.ds(start - CONV_HALO, tt + CONV_HALO), :], sums)

        sums = lax.fori_loop(1, t // tt, loop, sums)
        db_ref[...] = sums[0]
        dw_ref[...] = jnp.concatenate([sums[1 + (CONV_WIDTH - 1 - k)] for k in range(CONV_WIDTH)], axis=0)
        dpre_ref[pl.ds(t, CONV_HALO), :] = jnp.zeros((CONV_HALO, tc), F32)

        def loop2(i, carry):
            start = pl.multiple_of(i * tt, tt)
            ext = dpre_ref[pl.ds(start, tt + CONV_HALO), :]
            acc = ext[0:tt, :] * wv[CONV_WIDTH - 1:CONV_WIDTH, :]
            for j in range(1, CONV_WIDTH):
                acc = acc + pltpu.roll(ext, tt + CONV_HALO - j, 0)[0:tt, :] * wv[CONV_WIDTH - 1 - j:CONV_WIDTH - j, :]
            du_ref[pl.ds(start, tt), :] = acc.astype(BF16)
            return carry

        lax.fori_loop(0, t // tt, loop2, 0)

    col = pl.BlockSpec((t, tc), lambda j: (0, j))
    return pl.pallas_call(
        body, name="conv_bwd", grid=(c // tc,),
        in_specs=[col, pl.BlockSpec((CONV_WIDTH, tc), lambda j: (0, j)), pl.BlockSpec((1, tc), lambda j: (0, j)), col],
        out_specs=[col, pl.BlockSpec((CONV_WIDTH, tc), lambda j: (0, j)), pl.BlockSpec((1, tc), lambda j: (0, j))],
        out_shape=[jax.ShapeDtypeStruct((t, c), BF16), jax.ShapeDtypeStruct((CONV_WIDTH, c), F32),
                   jax.ShapeDtypeStruct((1, c), F32)],
        scratch_shapes=[pltpu.VMEM((t + CONV_HALO, tc), F32)],
        compiler_params=_params("parallel"),
    )(u, w, b, dact)


def _softplus(v):
    e = jnp.exp(-jnp.abs(v))
    w = 1.0 + e
    log1p = jnp.where(w == 1.0, e, jnp.log(w) * (e / jnp.where(w == 1.0, 1.0, w - 1.0)))
    return jnp.maximum(v, 0.0) + log1p


def _ssd_prep_fwd(dt_raw, dt_bias, a_log):
    t = dt_raw.shape[0]
    cl = SSD_CHUNK

    def body(r_ref, b_ref, al_ref, dt_ref, acs_ref):
        dt = _softplus(r_ref[...] + b_ref[...])
        adt = dt * (-jnp.exp(al_ref[...]))
        li = lax.broadcasted_iota(jnp.int32, (cl, cl), 0)
        si = lax.broadcasted_iota(jnp.int32, (cl, cl), 1)
        tri = (si <= li).astype(F32)
        dt_ref[...] = dt
        acs_ref[...] = jnp.dot(tri, adt, preferred_element_type=F32, precision=HIGHEST)

    row = pl.BlockSpec((cl, LANES), lambda i: (i, 0))
    vec = pl.BlockSpec((1, LANES), lambda i: (0, 0))
    return pl.pallas_call(
        body, name="ssd_prep_fwd", grid=(t // cl,),
        in_specs=[row, vec, vec], out_specs=[row, row],
        out_shape=[jax.ShapeDtypeStruct((t, LANES), F32), jax.ShapeDtypeStruct((t, LANES), F32)],
        compiler_params=_params("parallel"),
    )(dt_raw, dt_bias, a_log)


def _ssd_prep_bwd(dt_raw, dt_bias, ddt):
    t = dt_raw.shape[0]
    tm = _tile(t, 512)

    def body(r_ref, b_ref, d_ref, o_ref, db_ref):
        g = d_ref[...] * _sigmoid(r_ref[...] + b_ref[...])
        o_ref[...] = g.astype(BF16)
        part = jnp.sum(g, axis=0, keepdims=True)

        @pl.when(pl.program_id(0) == 0)
        def _():
            db_ref[...] = part

        @pl.when(pl.program_id(0) > 0)
        def _():
            db_ref[...] += part

    row = pl.BlockSpec((tm, LANES), lambda i: (i, 0))
    vec = pl.BlockSpec((1, LANES), lambda i: (0, 0))
    return pl.pallas_call(
        body, name="ssd_prep_bwd", grid=(t // tm,),
        in_specs=[row, vec, row], out_specs=[row, vec],
        out_shape=[jax.ShapeDtypeStruct((t, LANES), BF16), jax.ShapeDtypeStruct((1, LANES), F32)],
        compiler_params=_params("arbitrary"),
    )(dt_raw, dt_bias, ddt)


GROUP_W = D_INNER // SSM_GROUPS
PAIRS_PER_GROUP = GROUP_W // LANES


def _head_cols(acs_pair, lt64):
    rolled = pltpu.roll(acs_pair, ATT_HEAD_DIM, 1)
    return jnp.where(lt64, acs_pair, rolled), jnp.where(lt64, rolled, acs_pair)


def _ssd_fwd(xbc, dt_rep, acs_rep, acs_t, dskip_rep):
    t = xbc.shape[0]
    cl = SSD_CHUNK
    nc = t // cl

    def body(xbc_ref, dt_ref, acs_ref, acst_ref, dskip_ref, y_ref, hin_ref, state_ref):
        @pl.when(pl.program_id(0) == 0)
        def _():
            state_ref[...] = jnp.zeros_like(state_ref)

        lt64 = _lane_lt64(cl)
        li = lax.broadcasted_iota(jnp.int32, (cl, cl), 0)
        si = lax.broadcasted_iota(jnp.int32, (cl, cl), 1)
        causal = li >= si
        hin_ref[...] = state_ref[...]
        for g in range(SSM_GROUPS):
            gsl = slice(g * GROUP_W, (g + 1) * GROUP_W)
            xg = xbc_ref[:, gsl]
            bg = xbc_ref[:, D_INNER + g * SSM_STATE:D_INNER + (g + 1) * SSM_STATE]
            cg = xbc_ref[:, D_INNER + SSM_GROUPS * SSM_STATE + g * SSM_STATE:
                         D_INNER + SSM_GROUPS * SSM_STATE + (g + 1) * SSM_STATE]
            acs = acs_ref[:, gsl]
            xdt = xg * dt_ref[:, gsl]
            atot = acs[cl - 1:cl, :]
            hin = state_ref[:, gsl]
            cgb = cg.astype(BF16)
            gmat = _dot_nt(cgb, bg.astype(BF16))
            yoff = _dot(cgb, hin.astype(BF16)) * jnp.exp(acs)
            snew = _dot(bg.T.astype(BF16), (xdt * jnp.exp(atot - acs)).astype(BF16))
            state_ref[:, gsl] = hin * jnp.exp(atot) + snew
            xdtb = xdt.astype(BF16)
            for pr in range(PAIRS_PER_GROUP):
                psl = slice(pr * LANES, (pr + 1) * LANES)
                cols = _head_cols(acs[:, psl], lt64)
                xp = xdtb[:, psl]
                ys = []
                for hh in range(2):
                    h = (g * PAIRS_PER_GROUP + pr) * 2 + hh
                    seg = cols[hh] - acst_ref[h:h + 1, :]
                    lm = jnp.exp(jnp.where(causal, seg, NEG_BIG))
                    ys.append(_dot((gmat * lm).astype(BF16), xp))
                ydiag = jnp.where(lt64, ys[0], ys[1])
                osl = slice(g * GROUP_W + pr * LANES, g * GROUP_W + (pr + 1) * LANES)
                y_ref[:, osl] = ydiag + yoff[:, psl] + xg[:, psl] * dskip_ref[:, osl]

    row = lambda w: pl.BlockSpec((cl, w), lambda c: (c, 0))
    return pl.pallas_call(
        body, name="ssd_fwd", grid=(nc,),
        in_specs=[row(CONV_DIM), row(D_INNER), row(D_INNER),
                  pl.BlockSpec((SSM_HEADS, cl), lambda c: (0, c)), pl.BlockSpec((1, D_INNER), lambda c: (0, 0))],
        out_specs=[row(D_INNER), pl.BlockSpec((None, SSM_STATE, D_INNER), lambda c: (c, 0, 0))],
        out_shape=[jax.ShapeDtypeStruct((t, D_INNER), F32), jax.ShapeDtypeStruct((nc, SSM_STATE, D_INNER), F32)],
        scratch_shapes=[pltpu.VMEM((SSM_STATE, D_INNER), F32)],
        compiler_params=_params("arbitrary"),
    )(xbc, dt_rep, acs_rep, acs_t, dskip_rep)


def _ssd_bwd(xbc, dt_rep, acs_rep, acs_t, dskip_rep, a_rep, hin_all, dy):
    t = xbc.shape[0]
    cl = SSD_CHUNK
    nc = t // cl

    def body(xbc_ref, dt_ref, acs_ref, acst_ref, dskip_ref, a_ref, hin_ref, dy_ref,
             dxbc_ref, ddt_ref, da_ref, dds_ref, dstate_ref, dacs_ref, dxs_ref):
        step = pl.program_id(0)

        @pl.when(step == 0)
        def _():
            dstate_ref[...] = jnp.zeros_like(dstate_ref)
            da_ref[...] = jnp.zeros_like(da_ref)
            dds_ref[...] = jnp.zeros_like(dds_ref)

        bd = _head_block_diag()
        lt64 = _lane_lt64(cl)
        li = lax.broadcasted_iota(jnp.int32, (cl, cl), 0)
        si = lax.broadcasted_iota(jnp.int32, (cl, cl), 1)
        lower = li >= si
        upper = si >= li
        last_row = lax.broadcasted_iota(jnp.int32, (cl, GROUP_W), 0) == cl - 1
        for g in range(SSM_GROUPS):
            gsl = slice(g * GROUP_W, (g + 1) * GROUP_W)
            bsl = slice(D_INNER + g * SSM_STATE, D_INNER + (g + 1) * SSM_STATE)
            csl = slice(D_INNER + SSM_GROUPS * SSM_STATE + g * SSM_STATE,
                        D_INNER + SSM_GROUPS * SSM_STATE + (g + 1) * SSM_STATE)
            xg = xbc_ref[:, gsl]
            bg = xbc_ref[:, bsl]
            cg = xbc_ref[:, csl]
            bgb, cgb = bg.astype(BF16), cg.astype(BF16)
            acs = acs_ref[:, gsl]
            xdt = xg * dt_ref[:, gsl]
            atot = acs[cl - 1:cl, :]
            eg = jnp.exp(acs)
            dk = jnp.exp(atot - acs)
            etot = jnp.exp(atot)
            hin = hin_ref[:, gsl]
            hinb = hin.astype(BF16)
            dh = dstate_ref[:, gsl]
            dhb = dh.astype(BF16)
            dyg = dy_ref[:, gsl]

            gmat = _dot_nt(cgb, bgb)
            gmat_t = _dot_nt(bgb, cgb)
            ch = _dot(cgb, hinb)
            dacs = _head_sums(dyg * ch * eg, bd)
            dye = (dyg * eg).astype(BF16)
            dc = _dot_nt(dye, hinb)
            dhin = _dot(cg.T.astype(BF16), dye)
            bdh = _dot(bgb, dhb)
            dxs = bdh * dk
            xdk = xdt * dk
            db = _dot_nt(xdk.astype(BF16), dhb)
            ddk = _head_sums(bdh * xdk, bd)
            dacs = dacs - ddk
            datot = jnp.sum(ddk, axis=0, keepdims=True) + etot * _head_sums(
                jnp.sum(dh * hin, axis=0, keepdims=True), bd)
            dacs = dacs + jnp.where(last_row, datot, 0.0)
            dstate_ref[:, gsl] = dh * etot + dhin

            xdtb = xdt.astype(BF16)
            dgsum = jnp.zeros((cl, cl), F32)
            dgsum_t = jnp.zeros((cl, cl), F32)
            for pr in range(PAIRS_PER_GROUP):
                psl = slice(pr * LANES, (pr + 1) * LANES)
                cols = _head_cols(acs[:, psl], lt64)
                xp = xdtb[:, psl]
                dyp = dyg[:, psl].astype(BF16)
                dx1, dac = [], []
                for hh in range(2):
                    h = (g * PAIRS_PER_GROUP + pr) * 2 + hh
                    mine = lt64 if hh == 0 else jnp.logical_not(lt64)
                    row = acst_ref[h:h + 1, :]
                    lm = jnp.exp(jnp.where(lower, cols[hh] - row, NEG_BIG))
                    lm_t = jnp.exp(jnp.where(upper, row - cols[hh], NEG_BIG))
                    dyh = jnp.where(mine, dyp, jnp.zeros_like(dyp))
                    xh = jnp.where(mine, xp, jnp.zeros_like(xp))
                    dm = _dot_nt(dyh, xp)
                    dm_t = _dot_nt(xh, dyp)
                    m_t = gmat_t * lm_t
                    dx1.append(_dot(m_t.astype(BF16), dyp))
                    w = dm * (gmat * lm)
                    w_t = dm_t * m_t
                    dac.append(jnp.sum(w, axis=1, keepdims=True) - jnp.sum(w_t, axis=1, keepdims=True))
                    dgsum = dgsum + dm * lm
                    dgsum_t = dgsum_t + dm_t * lm_t
                osl = slice(g * GROUP_W + pr * LANES, g * GROUP_W + (pr + 1) * LANES)
                dxs_ref[:, osl] = dxs[:, psl] + jnp.where(lt64, dx1[0], dx1[1])
                dacs_ref[:, osl] = dacs[:, psl] + jnp.where(lt64, jnp.broadcast_to(dac[0], (cl, LANES)),
                                                             jnp.broadcast_to(dac[1], (cl, LANES)))
            dxbc_ref[:, csl] = dc + _dot(dgsum.astype(BF16), bgb)
            dxbc_ref[:, bsl] = db + _dot(dgsum_t.astype(BF16), cgb)

        tri_t = upper.astype(F32)
        dadt = jnp.dot(tri_t, dacs_ref[...], preferred_element_type=F32, precision=HIGHEST)
        xall = xbc_ref[:, 0:D_INNER]
        dtall = dt_ref[...]
        dxsall = dxs_ref[...]
        dyall = dy_ref[...]
        ddt_ref[...] = dadt * a_ref[...] + _head_sums(dxsall * xall, bd)
        dxbc_ref[:, 0:D_INNER] = dxsall * dtall + dyall * dskip_ref[...]
        da_ref[...] += jnp.sum(dadt * dtall, axis=0, keepdims=True)
        dds_ref[...] += jnp.sum(dyall * xall, axis=0, keepdims=True)

        @pl.when(step == nc - 1)
        def _():
            dds_ref[...] = _head_sums(dds_ref[...], bd)

    row = lambda w: pl.BlockSpec((cl, w), lambda c: (nc - 1 - c, 0))
    vec = pl.BlockSpec((1, D_INNER), lambda c: (0, 0))
    return pl.pallas_call(
        body, name="ssd_bwd", grid=(nc,),
        in_specs=[row(CONV_DIM), row(D_INNER), row(D_INNER),
                  pl.BlockSpec((SSM_HEADS, cl), lambda c: (0, nc - 1 - c)), vec, vec,
                  pl.BlockSpec((None, SSM_STATE, D_INNER), lambda c: (nc - 1 - c, 0, 0)), row(D_INNER)],
        out_specs=[row(CONV_DIM), row(D_INNER), vec, vec],
        out_shape=[jax.ShapeDtypeStruct((t, CONV_DIM), F32), jax.ShapeDtypeStruct((t, D_INNER), F32),
                   jax.ShapeDtypeStruct((1, D_INNER), F32), jax.ShapeDtypeStruct((1, D_INNER), F32)],
        scratch_shapes=[pltpu.VMEM((SSM_STATE, D_INNER), F32), pltpu.VMEM((cl, D_INNER), F32),
                        pltpu.VMEM((cl, D_INNER), F32)],
        compiler_params=_params("arbitrary"),
    )(xbc, dt_rep, acs_rep, acs_t, dskip_rep, a_rep, hin_all, dy)


def _gate_norm_fwd(y, z, w):
    t, c = y.shape
    tm = _tile(t, 256)

    def body(y_ref, z_ref, w_ref, o_ref):
        for g in range(SSM_GROUPS):
            gsl = slice(g * GROUP_W, (g + 1) * GROUP_W)
            zv = z_ref[:, gsl]
            v = y_ref[:, gsl] * (zv * _sigmoid(zv))
            r = lax.rsqrt(jnp.mean(v * v, axis=-1, keepdims=True) + NORM_EPS)
            o_ref[:, gsl] = (v * r * w_ref[:, gsl]).astype(BF16)

    row = pl.BlockSpec((tm, c), lambda i: (i, 0))
    return pl.pallas_call(
        body, name="gate_norm_fwd", grid=(t // tm,),
        in_specs=[row, row, pl.BlockSpec((1, c), lambda i: (0, 0))], out_specs=row,
        out_shape=jax.ShapeDtypeStruct((t, c), BF16),
        compiler_params=_params("parallel"),
    )(y, z, w)


def _gate_norm_bwd(y, z, w, dout):
    t, c = y.shape
    tm = _tile(t, 256)

    def body(y_ref, z_ref, w_ref, do_ref, dy_ref, dz_ref, dw_ref):
        @pl.when(pl.program_id(0) == 0)
        def _():
            dw_ref[...] = jnp.zeros_like(dw_ref)

        for g in range(SSM_GROUPS):
            gsl = slice(g * GROUP_W, (g + 1) * GROUP_W)
            zv, yv, dov = z_ref[:, gsl], y_ref[:, gsl], do_ref[:, gsl]
            sg = _sigmoid(zv)
            sz = zv * sg
            v = yv * sz
            r = lax.rsqrt(jnp.mean(v * v, axis=-1, keepdims=True) + NORM_EPS)
            vh = v * r
            dvh = dov * w_ref[:, gsl]
            mean = jnp.mean(dvh * vh, axis=-1, keepdims=True)
            dv = r * (dvh - vh * mean)
            dy_ref[:, gsl] = dv * sz
            dz_ref[:, gsl] = (dv * yv * (sg * (1.0 + zv * (1.0 - sg)))).astype(BF16)
            dw_ref[:, gsl] += jnp.sum(dov * vh, axis=0, keepdims=True)

    row = pl.BlockSpec((tm, c), lambda i: (i, 0))
    vec = pl.BlockSpec((1, c), lambda i: (0, 0))
    return pl.pallas_call(
        body, name="gate_norm_bwd", grid=(t // tm,),
        in_specs=[row, row, vec, row], out_specs=[row, row, vec],
        out_shape=[jax.ShapeDtypeStruct((t, c), F32), jax.ShapeDtypeStruct((t, c), BF16),
                   jax.ShapeDtypeStruct((1, c), F32)],
        compiler_params=_params("arbitrary"),
    )(y, z, w, dout)


ATT_W = ATT_HEADS * ATT_HEAD_DIM
N_QKV_BLOCKS = 9
ATT_SCALE = 1.0 / math.sqrt(ATT_HEAD_DIM)


def _slope(h):
    return 2.0 ** (-8.0 * (h + 1) / ATT_HEADS)


def _qk_norm_fwd(qkv, gq, gk):
    t = qkv.shape[0]
    tm = _tile(t, 512)

    def body(x_ref, gq_ref, gk_ref, o_ref):
        cb = pl.program_id(1)
        xv = x_ref[...]

        def normed(gain):
            ms = _head_sums(xv * xv, _head_block_diag()) * (1.0 / ATT_HEAD_DIM)
            return (xv * lax.rsqrt(ms + NORM_EPS) * gain).astype(BF16)

        @pl.when(cb < 3)
        def _():
            o_ref[...] = normed(gq_ref[...])

        @pl.when(jnp.logical_and(cb >= 3, cb < 6))
        def _():
            o_ref[...] = normed(gk_ref[...])

        @pl.when(cb >= 6)
        def _():
            o_ref[...] = xv.astype(BF16)

    blk = pl.BlockSpec((tm, ATT_W), lambda i, j: (i, j))
    vec = pl.BlockSpec((1, ATT_W), lambda i, j: (0, 0))
    return pl.pallas_call(
        body, name="qk_norm_fwd", grid=(t // tm, N_QKV_BLOCKS),
        in_specs=[blk, vec, vec], out_specs=blk,
        out_shape=jax.ShapeDtypeStruct(qkv.shape, BF16),
        compiler_params=_params("parallel", "parallel"),
    )(qkv, gq, gk)


def _qk_norm_bwd(qkv, gq, gk, grads):
    t = qkv.shape[0]
    tm = _tile(t, 256)

    def body(x_ref, gq_ref, gk_ref, *rest):
        g_refs = rest[:N_QKV_BLOCKS]
        o_ref, dgq_ref, dgk_ref = rest[N_QKV_BLOCKS:]
        cb = pl.program_id(1)

        @pl.when(jnp.logical_and(pl.program_id(0) == 0, cb == 0))
        def _():
            dgq_ref[...] = jnp.zeros_like(dgq_ref)
            dgk_ref[...] = jnp.zeros_like(dgk_ref)

        def norm_bwd(dy, gain, dg_ref):
            bd = _head_block_diag()
            xv = x_ref[...]
            ms = _head_sums(xv * xv, bd) * (1.0 / ATT_HEAD_DIM)
            r = lax.rsqrt(ms + NORM_EPS)
            xh = xv * r
            dxh = dy * gain
            mean = _head_sums(dxh * xh, bd) * (1.0 / ATT_HEAD_DIM)
            o_ref[...] = (r * (dxh - xh * mean)).astype(BF16)
            dg_ref[...] += jnp.sum(dy * xh, axis=0, keepdims=True)

        for k in range(N_QKV_BLOCKS):
            @pl.when(cb == k)
            def _(k=k):
                if k < 3:
                    norm_bwd(g_refs[k][...], gq_ref[...], dgq_ref)
                elif k < 6:
                    norm_bwd(g_refs[k][...], gk_ref[...], dgk_ref)
                else:
                    o_ref[...] = g_refs[k][...].astype(BF16)

    blk = pl.BlockSpec((tm, ATT_W), lambda i, j: (i, j))
    one = pl.BlockSpec((tm, ATT_W), lambda i, j: (i, 0))
    vec = pl.BlockSpec((1, ATT_W), lambda i, j: (0, 0))
    return pl.pallas_call(
        body, name="qk_norm_bwd", grid=(t // tm, N_QKV_BLOCKS),
        in_specs=[blk, vec, vec] + [one] * N_QKV_BLOCKS, out_specs=[blk, vec, vec],
        out_shape=[jax.ShapeDtypeStruct(qkv.shape, BF16), jax.ShapeDtypeStruct((1, ATT_W), F32),
                   jax.ShapeDtypeStruct((1, ATT_W), F32)],
        compiler_params=_params("arbitrary", "arbitrary"),
    )(qkv, gq, gk, *grads)


def _attn_logits(qm, kcat, slope, dist_bias, valid):
    s = _dot_nt(qm, kcat) * ATT_SCALE - slope * dist_bias
    return jnp.where(valid, s, NEG_BIG)


def _attn_fwd(qkvn, g, dil):
    t = qkvn.shape[0]
    lu = t // dil
    nb = lu // ATT_BLOCK
    view = qkvn.reshape(lu, dil * N_QKV_BLOCKS * ATT_W)
    bq = ATT_BLOCK

    def body(q_ref, kc_ref, kp_ref, vc_ref, vp_ref, o_ref, l_ref):
        n = pl.program_id(1)
        lt64 = _lane_lt64(bq)
        qi = lax.broadcasted_iota(jnp.int32, (bq, 2 * bq), 0)
        kk = lax.broadcasted_iota(jnp.int32, (bq, 2 * bq), 1)
        dist = qi + bq - kk
        valid = (dist >= 0) & (dist <= bq) & ((kk >= bq) | (n > 0))
        dist_bias = dist.astype(F32) * float(dil)
        for pair in range(ATT_HEADS // 2):
            sl = slice(pair * LANES, (pair + 1) * LANES)
            qp = q_ref[:, sl]
            kcat = jnp.concatenate([kp_ref[:, sl], kc_ref[:, sl]], axis=0)
            vcat = jnp.concatenate([vp_ref[:, sl], vc_ref[:, sl]], axis=0)
            outs, lses = [], []
            for hh in range(2):
                mine = lt64 if hh == 0 else jnp.logical_not(lt64)
                qm = jnp.where(mine, qp, jnp.zeros_like(qp))
                s = _attn_logits(qm, kcat, _slope(pair * 2 + hh), dist_bias, valid)
                m = jnp.max(s, axis=1, keepdims=True)
                p = jnp.exp(s - m)
                l = jnp.sum(p, axis=1, keepdims=True)
                outs.append(_dot(p.astype(BF16), vcat) * (1.0 / l))
                lses.append(jnp.broadcast_to(m + jnp.log(l), (bq, LANES)))
            o_ref[:, sl] = jnp.where(lt64, outs[0], outs[1])
            l_ref[:, sl] = jnp.where(lt64, lses[0], lses[1])

    def col(j):
        return lambda r, n: (n, r * N_QKV_BLOCKS + j * 3 + g)

    def col_prev(j):
        return lambda r, n: (jnp.maximum(n - 1, 0), r * N_QKV_BLOCKS + j * 3 + g)

    blk = lambda f: pl.BlockSpec((bq, ATT_W), f)
    out = pl.BlockSpec((bq, ATT_W), lambda r, n: (n, r))
    o, lse = pl.pallas_call(
        body, name=f"attn_fwd_g{g}", grid=(dil, nb),
        in_specs=[blk(col(0)), blk(col(1)), blk(col_prev(1)), blk(col(2)), blk(col_prev(2))],
        out_specs=[out, out],
        out_shape=[jax.ShapeDtypeStruct((lu, dil * ATT_W), F32), jax.ShapeDtypeStruct((lu, dil * ATT_W), F32)],
        compiler_params=_params("parallel", "arbitrary"),
    )(view, view, view, view, view)
    return o.reshape(t, ATT_W), lse.reshape(t, ATT_W)


def _attn_combine_fwd(outs, lses):
    t = outs[0].shape[0]
    tm = _tile(t, 256)

    def body(o0, o1, o2, l0, l1, l2, ob_ref, of_ref, lt_ref):
        a, b, c = l0[...], l1[...], l2[...]
        m = jnp.maximum(jnp.maximum(a, b), c)
        ea, eb, ec = jnp.exp(a - m), jnp.exp(b - m), jnp.exp(c - m)
        ssum = ea + eb + ec
        o = (ea * o0[...] + eb * o1[...] + ec * o2[...]) / ssum
        ob_ref[...] = o.astype(BF16)
        of_ref[...] = o
        lt_ref[...] = m + jnp.log(ssum)

    row = pl.BlockSpec((tm, ATT_W), lambda i: (i, 0))
    return pl.pallas_call(
        body, name="attn_combine_fwd", grid=(t // tm,),
        in_specs=[row] * 6, out_specs=[row] * 3,
        out_shape=[jax.ShapeDtypeStruct((t, ATT_W), BF16), jax.ShapeDtypeStruct((t, ATT_W), F32),
                   jax.ShapeDtypeStruct((t, ATT_W), F32)],
        compiler_params=_params("parallel"),
    )(*outs, *lses)


def _attn_combine_bwd(do, o):
    t = do.shape[0]
    tm = _tile(t, 256)

    def body(do_ref, o_ref, dl_ref, dob_ref):
        dov = do_ref[...]
        dl_ref[...] = _head_sums(dov * o_ref[...], _head_block_diag())
        dob_ref[...] = dov.astype(BF16)

    row = pl.BlockSpec((tm, ATT_W), lambda i: (i, 0))
    return pl.pallas_call(
        body, name="attn_combine_bwd", grid=(t // tm,),
        in_specs=[row, row], out_specs=[row, row],
        out_shape=[jax.ShapeDtypeStruct((t, ATT_W), F32), jax.ShapeDtypeStruct((t, ATT_W), BF16)],
        compiler_params=_params("parallel"),
    )(do, o)


def _attn_bwd_dq(qkvn, do_b, l_rep, dl_rep, g, dil):
    t = qkvn.shape[0]
    lu = t // dil
    nb = lu // ATT_BLOCK
    view = qkvn.reshape(lu, dil * N_QKV_BLOCKS * ATT_W)
    shp = (lu, dil * ATT_W)
    bq = ATT_BLOCK

    def body(q_ref, kc_ref, kp_ref, vc_ref, vp_ref, do_ref, l_ref, dl_ref, dq_ref):
        n = pl.program_id(1)
        lt64 = _lane_lt64(bq)
        qi = lax.broadcasted_iota(jnp.int32, (bq, 2 * bq), 0)
        kk = lax.broadcasted_iota(jnp.int32, (bq, 2 * bq), 1)
        dist = qi + bq - kk
        valid = (dist >= 0) & (dist <= bq) & ((kk >= bq) | (n > 0))
        dist_bias = dist.astype(F32) * float(dil)
        for pair in range(ATT_HEADS // 2):
            sl = slice(pair * LANES, (pair + 1) * LANES)
            qp = q_ref[:, sl]
            dop = do_ref[:, sl]
            kcat = jnp.concatenate([kp_ref[:, sl], kc_ref[:, sl]], axis=0)
            vcat = jnp.concatenate([vp_ref[:, sl], vc_ref[:, sl]], axis=0)
            lcols = _head_cols(l_ref[:, sl], lt64)
            dcols = _head_cols(dl_ref[:, sl], lt64)
            dqs = []
            for hh in range(2):
                mine = lt64 if hh == 0 else jnp.logical_not(lt64)
                qm = jnp.where(mine, qp, jnp.zeros_like(qp))
                dom = jnp.where(mine, dop, jnp.zeros_like(dop))
                s = _attn_logits(qm, kcat, _slope(pair * 2 + hh), dist_bias, valid)
                p = jnp.exp(s - jnp.concatenate([lcols[hh], lcols[hh]], axis=1))
                dp = _dot_nt(dom, vcat)
                ds = p * (dp - jnp.concatenate([dcols[hh], dcols[hh]], axis=1))
                dqs.append(_dot(ds.astype(BF16), kcat) * ATT_SCALE)
            dq_ref[:, sl] = jnp.where(lt64, dqs[0], dqs[1])

    def col(j):
        return lambda r, n: (n, r * N_QKV_BLOCKS + j * 3 + g)

    def col_prev(j):
        return lambda r, n: (jnp.maximum(n - 1, 0), r * N_QKV_BLOCKS + j * 3 + g)

    blk = lambda f: pl.BlockSpec((bq, ATT_W), f)
    tok = pl.BlockSpec((bq, ATT_W), lambda r, n: (n, r))
    dq = pl.pallas_call(
        body, name=f"attn_bwd_dq_g{g}", grid=(dil, nb),
        in_specs=[blk(col(0)), blk(col(1)), blk(col_prev(1)), blk(col(2)), blk(col_prev(2)), tok, tok, tok],
        out_specs=tok, out_shape=jax.ShapeDtypeStruct(shp, F32),
        compiler_params=_params("parallel", "arbitrary"),
    )(view, view, view, view, view, do_b.reshape(shp), l_rep.reshape(shp), dl_rep.reshape(shp))
    return dq.reshape(t, ATT_W)


def _attn_bwd_dkv(qkvn, do_b, l_row, dl_row, g, dil):
    t = qkvn.shape[0]
    lu = t // dil
    nb = lu // ATT_BLOCK
    view = qkvn.reshape(lu, dil * N_QKV_BLOCKS * ATT_W)
    shp = (lu, dil * ATT_W)
    bq = ATT_BLOCK

    def body(k_ref, v_ref, qc_ref, qn_ref, doc_ref, don_ref, lc_ref, ln_ref, dc_ref, dn_ref, dk_ref, dv_ref):
        n = pl.program_id(1)
        lt64 = _lane_lt64(bq)
        ki = lax.broadcasted_iota(jnp.int32, (bq, 2 * bq), 0)
        qq = lax.broadcasted_iota(jnp.int32, (bq, 2 * bq), 1)
        dist = qq - ki
        valid = (dist >= 0) & (dist <= bq) & ((qq < bq) | (n < nb - 1))
        dist_bias = dist.astype(F32) * float(dil)
        for pair in range(ATT_HEADS // 2):
            sl = slice(pair * LANES, (pair + 1) * LANES)
            kp = k_ref[:, sl]
            vp = v_ref[:, sl]
            qcat = jnp.concatenate([qc_ref[:, sl], qn_ref[:, sl]], axis=0)
            docat = jnp.concatenate([doc_ref[:, sl], don_ref[:, sl]], axis=0)
            dks, dvs = [], []
            for hh in range(2):
                h = pair * 2 + hh
                mine = lt64 if hh == 0 else jnp.logical_not(lt64)
                km = jnp.where(mine, kp, jnp.zeros_like(kp))
                vm = jnp.where(mine, vp, jnp.zeros_like(vp))
                s_t = _attn_logits(km, qcat, _slope(h), dist_bias, valid)
                l_r = jnp.concatenate([lc_ref[h:h + 1, :], ln_ref[h:h + 1, :]], axis=1)
                d_r = jnp.concatenate([dc_ref[h:h + 1, :], dn_ref[h:h + 1, :]], axis=1)
                p_t = jnp.exp(s_t - l_r)
                dvs.append(_dot(p_t.astype(BF16), docat))
                dp_t = _dot_nt(vm, docat)
                ds_t = p_t * (dp_t - d_r)
                dks.append(_dot(ds_t.astype(BF16), qcat) * ATT_SCALE)
            dk_ref[:, sl] = jnp.where(lt64, dks[0], dks[1])
            dv_ref[:, sl] = jnp.where(lt64, dvs[0], dvs[1])

    def col(j):
        return lambda r, n: (n, r * N_QKV_BLOCKS + j * 3 + g)

    blk = lambda f: pl.BlockSpec((bq, ATT_W), f)
    nxt = lambda n: jnp.minimum(n + 1, nb - 1)
    tok = pl.BlockSpec((bq, ATT_W), lambda r, n: (n, r))
    tok_next = pl.BlockSpec((bq, ATT_W), lambda r, n: (nxt(n), r))
    rowv = pl.BlockSpec((ATT_HEADS, bq), lambda r, n: (0, r * nb + n))
    rowv_next = pl.BlockSpec((ATT_HEADS, bq), lambda r, n: (0, r * nb + nxt(n)))
    dk, dv = pl.pallas_call(
        body, name=f"attn_bwd_dkv_g{g}", grid=(dil, nb),
        in_specs=[blk(col(1)), blk(col(2)), blk(col(0)),
                  blk(lambda r, n: (nxt(n), r * N_QKV_BLOCKS + g)),
                  tok, tok_next, rowv, rowv_next, rowv, rowv_next],
        out_specs=[tok, tok],
        out_shape=[jax.ShapeDtypeStruct(shp, F32), jax.ShapeDtypeStruct(shp, F32)],
        compiler_params=_params("parallel", "arbitrary"),
    )(view, view, view, view, do_b.reshape(shp), do_b.reshape(shp), l_row, l_row, dl_row, dl_row)
    return dk.reshape(t, ATT_W), dv.reshape(t, ATT_W)


def _rows_by_residue(rep, dil):
    t = rep.shape[0]
    per_head = rep[:, ::ATT_HEAD_DIM]
    return per_head.reshape(t // dil, dil, ATT_HEADS).transpose(2, 1, 0).reshape(ATT_HEADS, t)


def _per_head(rep_row):
    return rep_row[0, ::SSM_HEAD_DIM]


def _rep_heads(v):
    return jnp.repeat(v, SSM_HEAD_DIM)[None, :]


def _pad_lanes(v):
    return jnp.pad(v, ((0, 0), (0, LANES - v.shape[1])))


def _ffn_ple_fwd(x1, p_i, prm, i):
    h = _rmsnorm_fwd(x1, prm["norm_ffn"][i:i + 1], name=f"ffn_norm_fwd_{i}")
    g, u, act = _swiglu_fwd(h, prm["ffn_w_gate"][i], prm["ffn_w_up"][i], name=f"swiglu_fwd_{i}")
    x2 = _matmul(act, prm["ffn_w_down"][i], mode="nn", addend=x1, name=f"ffn_down_{i}")
    x3 = _ple_fwd(x2, p_i, prm["ple_w_gate"][i], prm["ple_w_proj"][i], name=f"ple_fwd_{i}")
    return x3, dict(x1=x1, h=h, g=g, u=u, act=act, x2=x2)


def _ffn_ple_bwd(dx3, p_i, prm, i, sv, grads):
    ds, dple = _ple_bwd(sv["x2"], p_i, prm["ple_w_gate"][i], prm["ple_w_proj"][i], dx3, name=f"ple_bwd_{i}")
    grads["ple_w_gate"][i] = _matmul_tn(sv["x2"], ds, name=f"d_ple_w_gate_{i}")
    grads["ple_w_proj"][i] = _matmul_tn(p_i, dple, name=f"d_ple_w_proj_{i}")
    dx2 = _matmul(ds, prm["ple_w_gate"][i], mode="nt", addend=dx3, name=f"ple_dx_{i}")
    grads["ffn_w_down"][i] = _matmul_tn(sv["act"], dx2, name=f"d_ffn_w_down_{i}")
    dg, du = _swiglu_bwd(dx2, prm["ffn_w_down"][i], sv["g"], sv["u"], name=f"swiglu_bwd_{i}")
    grads["ffn_w_gate"][i] = _matmul_tn(sv["h"], dg, name=f"d_ffn_w_gate_{i}")
    grads["ffn_w_up"][i] = _matmul_tn(sv["h"], du, name=f"d_ffn_w_up_{i}")
    dh = _matmul(dg, prm["ffn_w_gate"][i], mode="nt", name=f"ffn_dh_gate_{i}")
    dh = _matmul(du, prm["ffn_w_up"][i], mode="nt", addend=dh, name=f"ffn_dh_up_{i}")
    dx1, dgain = _rmsnorm_bwd(sv["x1"], prm["norm_ffn"][i:i + 1], dh, dx2, name=f"ffn_norm_bwd_{i}")
    grads["norm_ffn"][i] = dgain[0]
    return dx1


def _mamba_fwd(x0, prm):
    h = _rmsnorm_fwd(x0, prm["norm_mix"][0:1], name="mix_norm_fwd_0")
    z = _matmul(h, prm["ssm_w_z"], mode="nn", name="ssm_in_z")
    xbc_pre = _matmul(h, prm["ssm_w_xbc"], mode="nn", name="ssm_in_xbc")
    dt_raw = _matmul(h, prm["ssm_w_dt"], mode="nn", name="ssm_in_dt")
    xbc = _conv_fwd(xbc_pre, prm["ssm_conv_w"], prm["ssm_conv_b"])
    dt_bias = _pad_lanes(prm["ssm_dt_bias"])
    a_log = _pad_lanes(prm["ssm_a_log"])
    dt, acs = _ssd_prep_fwd(dt_raw, dt_bias, a_log)
    dt_rep = jnp.repeat(dt[:, :SSM_HEADS], SSM_HEAD_DIM, axis=1)
    acs_rep = jnp.repeat(acs[:, :SSM_HEADS], SSM_HEAD_DIM, axis=1)
    acs_t = acs[:, :SSM_HEADS].T
    dskip_rep = _rep_heads(prm["ssm_d_skip"][0])
    y, hin_all = _ssd_fwd(xbc, dt_rep, acs_rep, acs_t, dskip_rep)
    yn = _gate_norm_fwd(y, z, prm["ssm_norm_w"])
    x1 = _matmul(yn, prm["ssm_w_out"], mode="nn", addend=x0, name="ssm_out")
    sv = dict(x0=x0, h=h, z=z, xbc_pre=xbc_pre, dt_raw=dt_raw, xbc=xbc, dt_bias=dt_bias, dt_rep=dt_rep,
              acs_rep=acs_rep, acs_t=acs_t, dskip_rep=dskip_rep, y=y, hin_all=hin_all, yn=yn)
    return x1, sv


def _mamba_bwd(dx1, prm, sv, grads):
    grads["ssm_w_out"] = _matmul_tn(sv["yn"], dx1, name="d_ssm_w_out")
    dyn = _matmul(dx1, prm["ssm_w_out"], mode="nt", name="ssm_out_dx")
    dy, dz, dnw = _gate_norm_bwd(sv["y"], sv["z"], prm["ssm_norm_w"], dyn)
    grads["ssm_norm_w"] = dnw
    a_rep = _rep_heads(-jnp.exp(prm["ssm_a_log"][0]))
    dxbc, ddt_rep, da_rep, dds_rep = _ssd_bwd(sv["xbc"], sv["dt_rep"], sv["acs_rep"], sv["acs_t"], sv["dskip_rep"],
                                              a_rep, sv["hin_all"], dy)
    grads["ssm_d_skip"] = _per_head(dds_rep)[None, :]
    grads["ssm_a_log"] = (_per_head(da_rep) * _per_head(a_rep))[None, :]
    ddt = _pad_lanes(ddt_rep[:, ::SSM_HEAD_DIM])
    ddt_raw, dbias = _ssd_prep_bwd(sv["dt_raw"], sv["dt_bias"], ddt)
    grads["ssm_dt_bias"] = dbias[:, :SSM_HEADS]
    du, dcw, dcb = _conv_bwd(sv["xbc_pre"], prm["ssm_conv_w"], prm["ssm_conv_b"], dxbc)
    grads["ssm_conv_w"] = dcw
    grads["ssm_conv_b"] = dcb
    h = sv["h"]
    dw_in = jnp.concatenate([_matmul_tn(h, dz, name="d_ssm_w_z"), _matmul_tn(h, du, name="d_ssm_w_xbc"),
                             _matmul_tn(h, ddt_raw, name="d_ssm_w_dt")[:, :SSM_HEADS]], axis=1)
    grads["ssm_w_in"] = dw_in
    dh = _matmul(dz, prm["ssm_w_z"], mode="nt", name="ssm_dh_z")
    dh = _matmul(du, prm["ssm_w_xbc"], mode="nt", addend=dh, name="ssm_dh_xbc")
    dh = _matmul(ddt_raw, prm["ssm_w_dt"], mode="nt", addend=dh, name="ssm_dh_dt")
    dx0, dgain = _rmsnorm_bwd(sv["x0"], prm["norm_mix"][0:1], dh, dx1, name="mix_norm_bwd_0")
    grads["norm_mix"][0] = dgain[0]
    return dx0


def _attn_mixer_fwd(x0, prm):
    h = _rmsnorm_fwd(x0, prm["norm_mix"][1:2], name="mix_norm_fwd_1")
    qkv = _matmul(h, prm["att_w_qkv"], mode="nn", name="att_qkv")
    gq = jnp.tile(prm["att_q_norm"], (1, ATT_HEADS))
    gk = jnp.tile(prm["att_k_norm"], (1, ATT_HEADS))
    qkvn = _qk_norm_fwd(qkv, gq, gk)
    outs, lses = [], []
    for g, (window, dil) in enumerate(DIL_PATTERNS):
        o_g, l_g = _attn_fwd(qkvn, g, dil)
        outs.append(o_g)
        lses.append(l_g)
    o_b, o_f, l_rep = _attn_combine_fwd(outs, lses)
    x1 = _matmul(o_b, prm["att_w_o"], mode="nn", addend=x0, name="att_out")
    sv = dict(x0=x0, h=h, qkv=qkv, gq=gq, gk=gk, qkvn=qkvn, o_b=o_b, o_f=o_f, l_rep=l_rep)
    return x1, sv


def _attn_mixer_bwd(dx1, prm, sv, grads):
    grads["att_w_o"] = _matmul_tn(sv["o_b"], dx1, name="d_att_w_o")
    do = _matmul(dx1, prm["att_w_o"], mode="nt", name="att_out_dx")
    dl_rep, do_b = _attn_combine_bwd(do, sv["o_f"])
    blocks = [None] * N_QKV_BLOCKS
    for g, (window, dil) in enumerate(DIL_PATTERNS):
        blocks[g] = _attn_bwd_dq(sv["qkvn"], do_b, sv["l_rep"], dl_rep, g, dil)
        dk, dv = _attn_bwd_dkv(sv["qkvn"], do_b, _rows_by_residue(sv["l_rep"], dil),
                               _rows_by_residue(dl_rep, dil), g, dil)
        blocks[3 + g] = dk
        blocks[6 + g] = dv
    dqkv, dgq, dgk = _qk_norm_bwd(sv["qkv"], sv["gq"], sv["gk"], blocks)
    grads["att_q_norm"] = dgq.reshape(ATT_HEADS, ATT_HEAD_DIM).sum(axis=0)[None, :]
    grads["att_k_norm"] = dgk.reshape(ATT_HEADS, ATT_HEAD_DIM).sum(axis=0)[None, :]
    grads["att_w_qkv"] = _matmul_tn(sv["h"], dqkv, name="d_att_w_qkv")
    dh = _matmul(dqkv, prm["att_w_qkv"], mode="nt", name="att_qkv_dx")
    dx0, dgain = _rmsnorm_bwd(sv["x0"], prm["norm_mix"][1:2], dh, dx1, name="mix_norm_bwd_1")
    grads["norm_mix"][1] = dgain[0]
    return dx0


def _local_step(x, p, target, prm):
    grads = {k: [None, None] for k in ("norm_mix", "norm_ffn", "ffn_w_gate", "ffn_w_up", "ffn_w_down",
                                       "ple_w_proj", "ple_w_gate")}
    x1, sv_m = _mamba_fwd(x, prm)
    x3, sv_f0 = _ffn_ple_fwd(x1, p[0], prm, 0)
    x4, sv_a = _attn_mixer_fwd(x3, prm)
    x6, sv_f1 = _ffn_ple_fwd(x4, p[1], prm, 1)
    dy, loss_row = _loss_head(x6, target)
    dx4 = _ffn_ple_bwd(dy, p[1], prm, 1, sv_f1, grads)
    dx3 = _attn_mixer_bwd(dx4, prm, sv_a, grads)
    dx1 = _ffn_ple_bwd(dx3, p[0], prm, 0, sv_f0, grads)
    dx0 = _mamba_bwd(dx1, prm, sv_m, grads)
    return loss_row, dx0, grads


PACK_W = 1024
PACK_LAYOUT = (("ssm_w_in", 1288), ("ssm_w_out", 512), ("att_w_qkv", 2304), ("att_w_o", 256),
               ("ffn_w_gate", 1408), ("ffn_w_up", 1408), ("ffn_w_down", 1408), ("ple_w_proj", 128),
               ("ple_w_gate", 512))
PACK_OFFSETS = {}
_off = 0
for _name, _rows in PACK_LAYOUT:
    PACK_OFFSETS[_name] = (_off, _rows)
    _off += _rows
CONV_W_OFFSET = _off
CONV_W_ROWS_F32 = 3
PACK_ROWS = 9472
PACK_HALF = PACK_ROWS // 2
assert CONV_W_OFFSET + 2 * CONV_W_ROWS_F32 <= PACK_ROWS


def _to_shards(name, full):
    c = N_CHIPS
    if name in ("ssm_w_in", "att_w_qkv"):
        k, n = full.shape
        v = full.reshape(k, c, n // c).transpose(1, 0, 2)
    elif name in ("ssm_w_out", "att_w_o"):
        v = full
    elif name in ("ffn_w_gate", "ffn_w_up", "ple_w_proj"):
        l, k, n = full.shape
        v = full.reshape(l, k, c, n // c).transpose(2, 0, 1, 3)
    elif name in ("ffn_w_down", "ple_w_gate"):
        l, k, n = full.shape
        v = full.reshape(l, c, k // c, n).transpose(1, 0, 2, 3)
    elif name == "ssm_conv_w":
        k, n = full.shape
        v = full.reshape(k, c, n // c).transpose(1, 0, 2)
    return v.reshape(c, -1, PACK_W)


def _from_shards(name, slabs):
    c = N_CHIPS
    if name == "ssm_w_in":
        return slabs.reshape(c, D_MODEL, -1).transpose(1, 0, 2).reshape(D_MODEL, -1)
    if name == "att_w_qkv":
        return slabs.reshape(c, D_MODEL, -1).transpose(1, 0, 2).reshape(D_MODEL, -1)
    if name in ("ssm_w_out", "att_w_o"):
        return slabs.reshape(-1, D_MODEL)
    if name in ("ffn_w_gate", "ffn_w_up"):
        return slabs.reshape(c, 2, D_MODEL, -1).transpose(1, 2, 0, 3).reshape(2, D_MODEL, -1)
    if name == "ple_w_proj":
        return slabs.reshape(c, 2, PLE_DIM, -1).transpose(1, 2, 0, 3).reshape(2, PLE_DIM, -1)
    if name in ("ffn_w_down", "ple_w_gate"):
        return slabs.reshape(c, 2, -1, D_MODEL).transpose(1, 0, 2, 3).reshape(2, -1, D_MODEL)
    if name == "ssm_conv_w":
        return slabs.reshape(c, CONV_WIDTH, -1).transpose(1, 0, 2).reshape(CONV_WIDTH, -1)
    raise KeyError(name)


def _permute_qkv_cols(w):
    k = w.shape[0]
    return w.reshape(k, 3, 3, ATT_W).transpose(0, 2, 1, 3).reshape(k, N_QKV_BLOCKS * ATT_W)


MESH = pl.DeviceIdType.MESH
ANY = pl.BlockSpec(memory_space=pl.ANY)


def _position():
    return lax.axis_index("x"), lax.axis_index("y"), lax.axis_index("c")


def _other_chips(x, y):
    return [(1 - x, y), (x, 1 - y), (1 - x, 1 - y)]


def _gather_weights(wpack):
    rows = wpack.shape[0]
    half = rows // 2

    def body(w_ref, out_ref, send_sems, recv_sems, local_sem):
        x, y, c = _position()
        me, sibling = (x, y, c), (x, y, 1 - c)
        chips = _other_chips(x, y)

        def slab(px, py, pc):
            return out_ref.at[2 * px + py, pl.ds(pc * half, half), :]

        def copy(k, src, dst, to):
            return pltpu.make_async_remote_copy(src_ref=src, dst_ref=dst, send_sem=send_sems.at[k],
                                                recv_sem=recv_sems.at[k], device_id=to, device_id_type=MESH)

        mine = pltpu.make_async_copy(w_ref, out_ref.at[2 * x + y], local_sem)
        mine.start()
        my_half = w_ref.at[pl.ds(c * half, half), :]
        first = [copy(j, my_half, slab(x, y, c), (*chip, c)) for j, chip in enumerate(chips)]
        for cp in first:
            cp.start()
        passed = [copy(3 + j, slab(*chip, c), slab(*chip, c), sibling) for j, chip in enumerate(chips)]
        for j, chip in enumerate(chips):
            copy(j, slab(*chip, c), slab(*chip, c), me).wait_recv()
            passed[j].start()
        for j, chip in enumerate(chips):
            copy(3 + j, slab(*chip, 1 - c), slab(*chip, 1 - c), me).wait_recv()
        for cp in first + passed:
            cp.wait_send()
        mine.wait()

    return pl.pallas_call(
        body, name="gather_weights", in_specs=[ANY], out_specs=ANY,
        out_shape=jax.ShapeDtypeStruct((N_CHIPS, rows, PACK_W), wpack.dtype),
        scratch_shapes=[pltpu.SemaphoreType.DMA((6,)), pltpu.SemaphoreType.DMA((6,)), pltpu.SemaphoreType.DMA],
    )(wpack)


def _sibling_swap_halves(gpack):
    half = gpack.shape[1] // 2

    def body(g_ref, recv_ref, send_sem, recv_sem):
        x, y, c = _position()
        cp = pltpu.make_async_remote_copy(src_ref=g_ref.at[:, pl.ds((1 - c) * half, half), :], dst_ref=recv_ref,
                                          send_sem=send_sem, recv_sem=recv_sem, device_id=(x, y, 1 - c),
                                          device_id_type=MESH)
        cp.start()
        cp.wait()

    return pl.pallas_call(
        body, name="grad_swap_halves", in_specs=[ANY], out_specs=ANY,
        out_shape=jax.ShapeDtypeStruct((N_CHIPS, half, PACK_W), gpack.dtype),
        scratch_shapes=[pltpu.SemaphoreType.DMA, pltpu.SemaphoreType.DMA],
    )(gpack)


def _chip_exchange(chipsum):
    h = chipsum.shape[1]

    def body(cs_ref, recv_ref, send_sems, recv_sems):
        x, y, c = _position()
        cps = [pltpu.make_async_remote_copy(src_ref=cs_ref.at[2 * tx + ty], dst_ref=recv_ref.at[j],
                                            send_sem=send_sems.at[j], recv_sem=recv_sems.at[j],
                                            device_id=(tx, ty, c), device_id_type=MESH)
               for j, (tx, ty) in enumerate(_other_chips(x, y))]
        for cp in cps:
            cp.start()
        for cp in cps:
            cp.wait()

    return pl.pallas_call(
        body, name="grad_chip_exchange", in_specs=[ANY], out_specs=ANY,
        out_shape=jax.ShapeDtypeStruct((3, h, PACK_W), chipsum.dtype),
        scratch_shapes=[pltpu.SemaphoreType.DMA((3,)), pltpu.SemaphoreType.DMA((3,))],
    )(chipsum)


def _sibling_share_half(total_half):
    h = total_half.shape[0]

    def body(t_ref, out_ref, send_sem, recv_sem, local_sem):
        x, y, c = _position()
        local = pltpu.make_async_copy(t_ref, out_ref.at[c], local_sem)
        local.start()
        cp = pltpu.make_async_remote_copy(src_ref=t_ref, dst_ref=out_ref.at[c], send_sem=send_sem,
                                          recv_sem=recv_sem, device_id=(x, y, 1 - c), device_id_type=MESH)
        cp.start()
        cp.wait_send()
        pltpu.make_async_remote_copy(src_ref=t_ref, dst_ref=out_ref.at[1 - c], send_sem=send_sem,
                                     recv_sem=recv_sem, device_id=(x, y, 1 - c), device_id_type=MESH).wait_recv()
        local.wait()

    return pl.pallas_call(
        body, name="grad_share_half", in_specs=[ANY], out_specs=ANY,
        out_shape=jax.ShapeDtypeStruct((2, h, PACK_W), total_half.dtype),
        scratch_shapes=[pltpu.SemaphoreType.DMA, pltpu.SemaphoreType.DMA, pltpu.SemaphoreType.DMA],
    )(total_half)


REDUCE_ROW_TILE = 296


def _add_sibling(gpack, recv, c_idx):
    nt = PACK_HALF // REDUCE_ROW_TILE

    def body(c_ref, g_ref, r_ref, o_ref):
        o_ref[...] = g_ref[...] + r_ref[...]

    blk = (None, REDUCE_ROW_TILE, PACK_W)
    return pl.pallas_call(
        body, name="grad_add_sibling",
        grid_spec=pltpu.PrefetchScalarGridSpec(
            num_scalar_prefetch=1, grid=(N_CHIPS, nt),
            in_specs=[pl.BlockSpec(blk, lambda s, i, c_ref: (s, c_ref[0] * nt + i, 0)),
                      pl.BlockSpec(blk, lambda s, i, c_ref: (s, i, 0))],
            out_specs=pl.BlockSpec(blk, lambda s, i, c_ref: (s, i, 0))),
        out_shape=jax.ShapeDtypeStruct((N_CHIPS, PACK_HALF, PACK_W), F32),
        compiler_params=_params("parallel", "parallel"),
    )(c_idx, gpack, recv)


def _add_chips(chipsum, recv, s_idx):
    nt = PACK_HALF // REDUCE_ROW_TILE

    def body(s_ref, own_ref, r_ref, o_ref):
        o_ref[...] = ((own_ref[...] + r_ref[0]) + r_ref[1]) + r_ref[2]

    return pl.pallas_call(
        body, name="grad_add_chips",
        grid_spec=pltpu.PrefetchScalarGridSpec(
            num_scalar_prefetch=1, grid=(nt,),
            in_specs=[pl.BlockSpec((None, REDUCE_ROW_TILE, PACK_W), lambda i, s_ref: (s_ref[0], i, 0)),
                      pl.BlockSpec((3, REDUCE_ROW_TILE, PACK_W), lambda i, s_ref: (0, i, 0))],
            out_specs=pl.BlockSpec((REDUCE_ROW_TILE, PACK_W), lambda i, s_ref: (i, 0))),
        out_shape=jax.ShapeDtypeStruct((PACK_HALF, PACK_W), F32),
        compiler_params=_params("parallel"),
    )(s_idx, chipsum, recv)


def _adamw_math(w, g, m, v):
    m = ADAM_B1 * m + (1.0 - ADAM_B1) * g
    v = ADAM_B2 * v + (1.0 - ADAM_B2) * (g * g)
    m_hat = m / (1.0 - ADAM_B1 ** ADAM_STEP)
    v_hat = v / (1.0 - ADAM_B2 ** ADAM_STEP)
    delta = -ADAM_LR * (m_hat / (jnp.sqrt(v_hat) + ADAM_EPS) + ADAM_WD * w)
    return delta, m, v


ADAM_TILE_ELEMS = 256 * 1024


def _adamw(w, g, m, v, *, name):
    shape = w.shape
    cols = shape[-1]
    rows = w.size // cols
    tr = rows
    for cand in range(8, rows, 8):
        if rows % cand == 0 and cand * cols <= ADAM_TILE_ELEMS:
            tr = cand
    if rows * cols <= ADAM_TILE_ELEMS:
        tr = rows

    def body(w_ref, g_ref, m_ref, v_ref, d_ref, nm_ref, nv_ref):
        d, nm, nv = _adamw_math(w_ref[...], g_ref[...], m_ref[...], v_ref[...])
        d_ref[...] = d
        nm_ref[...] = nm
        nv_ref[...] = nv

    blk = pl.BlockSpec((tr, cols), lambda i: (i, 0))
    sds = jax.ShapeDtypeStruct((rows, cols), F32)
    outs = pl.pallas_call(
        body, name=name, grid=(rows // tr,), in_specs=[blk] * 4, out_specs=[blk] * 3, out_shape=[sds] * 3,
        compiler_params=_params("parallel"),
    )(*[a.reshape(rows, cols) for a in (w, g, m, v)])
    return [o.reshape(shape) for o in outs]


SMALL_LAYOUT = (("loss", 1), ("norm_mix", 16), ("norm_ffn", 16), ("ssm_conv_b", 24), ("ssm_dt_bias", 1),
                ("ssm_a_log", 1), ("ssm_d_skip", 1), ("ssm_norm_w", 16), ("att_q_norm", 1), ("att_k_norm", 1))
SMALL_ROWS = 80
N_DEVICES = 8


def _small_pack(values):
    parts = []
    for name, rows in SMALL_LAYOUT:
        flat = values[name].reshape(-1).astype(F32)
        parts.append(jnp.pad(flat, (0, rows * LANES - flat.shape[0])).reshape(rows, LANES))
    used = sum(r for _, r in SMALL_LAYOUT)
    parts.append(jnp.zeros((SMALL_ROWS - used, LANES), F32))
    return jnp.concatenate(parts, axis=0)


def _small_unpack(pack, shapes):
    out, off = {}, 0
    for name, rows in SMALL_LAYOUT:
        shape = shapes[name]
        n = math.prod(shape)
        out[name] = pack[off:off + rows].reshape(-1)[:n].reshape(shape)
        off += rows
    return out


def _small_allreduce_adamw(g, w, m, v):
    def body(g_ref, w_ref, m_ref, v_ref, gs_ref, d_ref, nm_ref, nv_ref, buf, send_sems, recv_sems):
        x, y, c = _position()
        pos = (x, y, c)
        me = 4 * x + 2 * y + c
        buf[me] = g_ref[...]
        peers = []
        for k in range(1, N_DEVICES):
            bits = ((k >> 2) & 1, (k >> 1) & 1, k & 1)
            peers.append(tuple(1 - p if b else p for p, b in zip(pos, bits)))
        cps = [pltpu.make_async_remote_copy(src_ref=g_ref, dst_ref=buf.at[me], send_sem=send_sems.at[k],
                                            recv_sem=recv_sems.at[k], device_id=peer, device_id_type=MESH)
               for k, peer in enumerate(peers)]
        for cp in cps:
            cp.start()
        for k, (px, py, pc) in enumerate(peers):
            pltpu.make_async_remote_copy(src_ref=g_ref, dst_ref=buf.at[4 * px + 2 * py + pc],
                                         send_sem=send_sems.at[k], recv_sem=recv_sems.at[k],
                                         device_id=(px, py, pc), device_id_type=MESH).wait_recv()
        for cp in cps:
            cp.wait_send()
        total = buf[0]
        for dev in range(1, N_DEVICES):
            total = total + buf[dev]
        gs_ref[...] = total
        d, nm, nv = _adamw_math(w_ref[...], total, m_ref[...], v_ref[...])
        d_ref[...] = d
        nm_ref[...] = nm
        nv_ref[...] = nv

    vm = pl.BlockSpec(memory_space=pltpu.VMEM)
    sds = jax.ShapeDtypeStruct((SMALL_ROWS, LANES), F32)
    return pl.pallas_call(
        body, name="small_allreduce_adamw", in_specs=[vm] * 4, out_specs=[vm] * 4, out_shape=[sds] * 4,
        scratch_shapes=[pltpu.VMEM((N_DEVICES, SMALL_ROWS, LANES), F32),
                        pltpu.SemaphoreType.DMA((N_DEVICES - 1,)), pltpu.SemaphoreType.DMA((N_DEVICES - 1,))],
    )(g, w, m, v)


BIG = tuple(n for n, _ in PACK_LAYOUT)
SMALL = tuple(n for n, _ in SMALL_LAYOUT if n != "loss")
WEIGHTS = ("norm_mix", "norm_ffn", "ssm_w_in", "ssm_conv_w", "ssm_conv_b", "ssm_dt_bias", "ssm_a_log", "ssm_d_skip",
           "ssm_norm_w", "ssm_w_out", "att_w_qkv", "att_q_norm", "att_k_norm", "att_w_o", "ffn_w_gate", "ffn_w_up",
           "ffn_w_down", "ple_w_proj", "ple_w_gate")


def _pack_weight_shards(w):
    parts = [w[n].astype(BF16).reshape(-1, PACK_W) for n in BIG]
    conv = lax.bitcast_convert_type(w["ssm_conv_w"].reshape(CONV_W_ROWS_F32, PACK_W), BF16)
    parts.append(conv.transpose(0, 2, 1).reshape(2 * CONV_W_ROWS_F32, PACK_W))
    used = CONV_W_OFFSET + 2 * CONV_W_ROWS_F32
    parts.append(jnp.zeros((PACK_ROWS - used, PACK_W), BF16))
    return jnp.concatenate(parts, axis=0)


def _unpack_gathered(gathered, small):
    full = {}
    for n in BIG:
        off, rows = PACK_OFFSETS[n]
        full[n] = _from_shards(n, gathered[:, off:off + rows])
    conv = gathered[:, CONV_W_OFFSET:CONV_W_OFFSET + 2 * CONV_W_ROWS_F32]
    conv = conv.reshape(N_CHIPS, CONV_W_ROWS_F32, 2, PACK_W).transpose(0, 1, 3, 2)
    conv = lax.bitcast_convert_type(conv, F32)
    prm = dict(small)
    prm["ssm_conv_w"] = _from_shards("ssm_conv_w", conv)
    w_in = full["ssm_w_in"]
    prm["ssm_w_z"] = w_in[:, :D_INNER]
    prm["ssm_w_xbc"] = w_in[:, D_INNER:D_INNER + CONV_DIM]
    prm["ssm_w_dt"] = jnp.pad(w_in[:, D_INNER + CONV_DIM:], ((0, 0), (0, LANES - SSM_HEADS)))
    prm["ssm_w_out"] = full["ssm_w_out"]
    prm["att_w_qkv"] = _permute_qkv_cols(full["att_w_qkv"])
    prm["att_w_o"] = full["att_w_o"]
    for n in ("ffn_w_gate", "ffn_w_up", "ffn_w_down", "ple_w_proj", "ple_w_gate"):
        prm[n] = full[n]
    return prm


def _pack_grads(grads):
    g = dict(grads)
    g["att_w_qkv"] = _permute_qkv_cols(grads["att_w_qkv"])
    for n in ("ffn_w_gate", "ffn_w_up", "ffn_w_down", "ple_w_proj", "ple_w_gate"):
        g[n] = jnp.stack(grads[n])
    parts = [_to_shards(n, g[n]) for n in BIG]
    parts.append(_to_shards("ssm_conv_w", g["ssm_conv_w"]))
    used = CONV_W_OFFSET + CONV_W_ROWS_F32
    parts.append(jnp.zeros((N_CHIPS, PACK_ROWS - used, PACK_W), F32))
    return jnp.concatenate(parts, axis=1)


def kernel(x, p, norm_mix, norm_ffn, ssm_w_in, ssm_conv_w, ssm_conv_b, ssm_dt_bias, ssm_a_log, ssm_d_skip, ssm_norm_w, ssm_w_out, att_w_qkv, att_q_norm, att_k_norm, att_w_o, ffn_w_gate, ffn_w_up, ffn_w_down, ple_w_proj, ple_w_gate, loss_target, m_norm_mix, m_norm_ffn, m_ssm_w_in, m_ssm_conv_w, m_ssm_conv_b, m_ssm_dt_bias, m_ssm_a_log, m_ssm_d_skip, m_ssm_norm_w, m_ssm_w_out, m_att_w_qkv, m_att_q_norm, m_att_k_norm, m_att_w_o, m_ffn_w_gate, m_ffn_w_up, m_ffn_w_down, m_ple_w_proj, m_ple_w_gate, v_norm_mix, v_norm_ffn, v_ssm_w_in, v_ssm_conv_w, v_ssm_conv_b, v_ssm_dt_bias, v_ssm_a_log, v_ssm_d_skip, v_ssm_norm_w, v_ssm_w_out, v_att_w_qkv, v_att_q_norm, v_att_k_norm, v_att_w_o, v_ffn_w_gate, v_ffn_w_up, v_ffn_w_down, v_ple_w_proj, v_ple_w_gate):
    given = dict(locals())
    w = {n: given[n] for n in WEIGHTS}
    m = {n: given["m_" + n] for n in WEIGHTS}
    v = {n: given["v_" + n] for n in WEIGHTS}
    c_idx = lax.axis_index("c").astype(jnp.int32).reshape(1)
    s_idx = (2 * lax.axis_index("x") + lax.axis_index("y")).astype(jnp.int32).reshape(1)

    gathered = _gather_weights(_pack_weight_shards(w))
    prm = _unpack_gathered(gathered, {n: w[n] for n in SMALL})

    loss_row, dx, grads = _local_step(x[0], p[:, 0], loss_target[0], prm)

    gpack = _pack_grads(grads)
    chipsum = _add_sibling(gpack, _sibling_swap_halves(gpack), c_idx)
    total_half = _add_chips(chipsum, _chip_exchange(chipsum), s_idx)
    gsum = _sibling_share_half(total_half).reshape(PACK_ROWS, PACK_W)

    grad, delta, new_m, new_v = {}, {}, {}, {}
    for n in BIG + ("ssm_conv_w",):
        if n == "ssm_conv_w":
            off, rows = CONV_W_OFFSET, CONV_W_ROWS_F32
        else:
            off, rows = PACK_OFFSETS[n]
        grad[n] = gsum[off:off + rows].reshape(w[n].shape)
        delta[n], new_m[n], new_v[n] = _adamw(w[n], grad[n], m[n], v[n], name="adamw_" + n)

    small_g = {n: (jnp.stack(grads[n]) if isinstance(grads[n], list) else grads[n]) for n in SMALL}
    small_g["loss"] = loss_row
    zero = {"loss": jnp.zeros((1, LANES), F32)}
    outs = _small_allreduce_adamw(_small_pack(small_g), _small_pack({**w, **zero}), _small_pack({**m, **zero}),
                                  _small_pack({**v, **zero}))
    shapes = {n: w[n].shape for n in SMALL}
    shapes["loss"] = (1, LANES)
    sg, sd, sm, sv = [_small_unpack(o, shapes) for o in outs]
    for n in SMALL:
        grad[n], delta[n], new_m[n], new_v[n] = sg[n], sd[n], sm[n], sv[n]
    loss = sg["loss"][0, 0]

    return (loss, dx[None], *[grad[n] for n in WEIGHTS], *[delta[n] for n in WEIGHTS],
            *[new_m[n] for n in WEIGHTS], *[new_v[n] for n in WEIGHTS])
```

```python
import functools
import math

import jax
import jax.numpy as jnp
from jax import lax
from jax.experimental import pallas as pl
from jax.experimental.pallas import tpu as pltpu

F32 = jnp.float32
BF16 = jnp.bfloat16
HIGHEST = lax.Precision.HIGHEST

NORM_EPS = 1e-6
ADAM_LR, ADAM_B1, ADAM_B2, ADAM_EPS, ADAM_WD, ADAM_STEP = 0.001, 0.9, 0.999, 1e-08, 0.01, 10

D_MODEL = 1024
D_INNER = 2048
SSM_HEADS = 32
SSM_HEAD_DIM = 64
SSM_GROUPS = 4
SSM_STATE = 128
SSD_CHUNK = 128
CONV_DIM = 3072
CONV_WIDTH = 4
ATT_HEADS = 16
ATT_HEAD_DIM = 64
DIL_PATTERNS = ((128, 1), (512, 4), (2048, 16))
ATT_BLOCK = 128
FFN_HIDDEN = 2816
PLE_DIM = 256

LANES = 128
V7X_VMEM_LIMIT = 56 * 1024 * 1024
NEG_BIG = -1e30

N_CHIPS = 4


def _params(*sem):
    return pltpu.CompilerParams(dimension_semantics=sem, vmem_limit_bytes=V7X_VMEM_LIMIT)


def _tile(n, pref):
    if n <= pref:
        return n
    best = None
    for t in range(LANES, pref + 1, LANES):
        if n % t == 0:
            best = t
    assert best is not None, (n, pref)
    return best


def _sigmoid(v):
    return 1.0 / (1.0 + jnp.exp(-v))


def _dot(a, b):
    return jnp.dot(a, b, preferred_element_type=F32)


def _dot_nt(a, b):
    return lax.dot_general(a, b, (((1,), (1,)), ((), ())), preferred_element_type=F32)


def _dot_tn(a, b):
    return lax.dot_general(a, b, (((0,), (0,)), ((), ())), preferred_element_type=F32)


def _head_block_diag():
    i = lax.broadcasted_iota(jnp.int32, (LANES, LANES), 0) // ATT_HEAD_DIM
    j = lax.broadcasted_iota(jnp.int32, (LANES, LANES), 1) // ATT_HEAD_DIM
    return (i == j).astype(BF16)


def _split_dot(ones, z):
    hi = z.astype(BF16)
    lo = (z - hi.astype(F32)).astype(BF16)
    return _dot(ones, hi) + _dot(ones, lo)


def _head_sums(z, bd):
    hi = z.astype(BF16)
    lo = (z - hi.astype(F32)).astype(BF16)
    parts = []
    for t in range(z.shape[1] // LANES):
        sl = slice(t * LANES, (t + 1) * LANES)
        parts.append(_dot(hi[:, sl], bd) + _dot(lo[:, sl], bd))
    return parts[0] if len(parts) == 1 else jnp.concatenate(parts, axis=1)


def _lane_lt64(rows):
    return lax.broadcasted_iota(jnp.int32, (rows, LANES), 1) < ATT_HEAD_DIM


def _matmul(a, b, *, mode, name, out_dtype=F32, addend=None, tm=1024, tn=512, tk_max=3072):
    m, k = a.shape
    if mode == "nn":
        k2, n = b.shape
    else:
        n, k2 = b.shape
    assert k == k2, (a.shape, b.shape, mode)
    tm, tn, tk = _tile(m, tm), _tile(n, tn), _tile(k, tk_max)
    nk = k // tk
    has_add = addend is not None

    def body(*refs):
        a_ref, b_ref = refs[0], refs[1]
        add_ref = refs[2] if has_add else None
        o_ref, acc_ref = refs[-2], refs[-1]
        kk = pl.program_id(2)
        av = a_ref[...].astype(BF16)
        bv = b_ref[...].astype(BF16)
        part = _dot(av, bv) if mode == "nn" else _dot_nt(av, bv)

        @pl.when(kk == 0)
        def _():
            acc_ref[...] = part

        @pl.when(kk > 0)
        def _():
            acc_ref[...] += part

        @pl.when(kk == nk - 1)
        def _():
            res = acc_ref[...]
            if has_add:
                res = res + add_ref[...]
            o_ref[...] = res.astype(out_dtype)

    a_spec = pl.BlockSpec((tm, tk), lambda i, j, kk: (i, kk))
    if mode == "nn":
        b_spec = pl.BlockSpec((tk, tn), lambda i, j, kk: (kk, j))
    else:
        b_spec = pl.BlockSpec((tn, tk), lambda i, j, kk: (j, kk))
    in_specs = [a_spec, b_spec]
    args = [a, b]
    if has_add:
        in_specs.append(pl.BlockSpec((tm, tn), lambda i, j, kk: (i, j)))
        args.append(addend)
    return pl.pallas_call(
        body, name=name, grid=(m // tm, n // tn, nk),
        in_specs=in_specs, out_specs=pl.BlockSpec((tm, tn), lambda i, j, kk: (i, j)),
        out_shape=jax.ShapeDtypeStruct((m, n), out_dtype),
        scratch_shapes=[pltpu.VMEM((tm, tn), F32)],
        compiler_params=_params("parallel", "parallel", "arbitrary"),
    )(*args)


def _matmul_tn(a, b, *, name, tm=1408, tn=512, tk=1024):
    t, m = a.shape
    t2, n = b.shape
    assert t == t2
    tm, tn, tk = _tile(m, tm), _tile(n, tn), _tile(t, tk)

    def body(a_ref, b_ref, o_ref):
        part = _dot_tn(a_ref[...].astype(BF16), b_ref[...].astype(BF16))

        @pl.when(pl.program_id(2) == 0)
        def _():
            o_ref[...] = part

        @pl.when(pl.program_id(2) > 0)
        def _():
            o_ref[...] += part

    return pl.pallas_call(
        body, name=name, grid=(m // tm, n // tn, t // tk),
        in_specs=[pl.BlockSpec((tk, tm), lambda i, j, kk: (kk, i)),
                  pl.BlockSpec((tk, tn), lambda i, j, kk: (kk, j))],
        out_specs=pl.BlockSpec((tm, tn), lambda i, j, kk: (i, j)),
        out_shape=jax.ShapeDtypeStruct((m, n), F32),
        compiler_params=_params("parallel", "parallel", "arbitrary"),
    )(a, b)


def _rmsnorm_fwd(x, gain, *, name):
    t, d = x.shape
    tm = _tile(t, 512)

    def body(x_ref, g_ref, o_ref):
        xv = x_ref[...]
        r = lax.rsqrt(jnp.mean(xv * xv, axis=-1, keepdims=True) + NORM_EPS)
        o_ref[...] = (xv * r * g_ref[...]).astype(BF16)

    return pl.pallas_call(
        body, name=name, grid=(t // tm,),
        in_specs=[pl.BlockSpec((tm, d), lambda i: (i, 0)), pl.BlockSpec((1, d), lambda i: (0, 0))],
        out_specs=pl.BlockSpec((tm, d), lambda i: (i, 0)),
        out_shape=jax.ShapeDtypeStruct((t, d), BF16),
        compiler_params=_params("parallel"),
    )(x, gain)


def _rmsnorm_bwd(x, gain, dy, dres, *, name):
    t, d = x.shape
    tm = _tile(t, 512)

    def body(x_ref, g_ref, dy_ref, dres_ref, dx_ref, dg_ref):
        xv = x_ref[...]
        r = lax.rsqrt(jnp.mean(xv * xv, axis=-1, keepdims=True) + NORM_EPS)
        xh = xv * r
        dyv = dy_ref[...]
        dxh = dyv * g_ref[...]
        mean = jnp.mean(dxh * xh, axis=-1, keepdims=True)
        dx_ref[...] = dres_ref[...] + r * (dxh - xh * mean)
        part = jnp.sum(dyv * xh, axis=0, keepdims=True)

        @pl.when(pl.program_id(0) == 0)
        def _():
            dg_ref[...] = part

        @pl.when(pl.program_id(0) > 0)
        def _():
            dg_ref[...] += part

    row = pl.BlockSpec((tm, d), lambda i: (i, 0))
    vec = pl.BlockSpec((1, d), lambda i: (0, 0))
    return pl.pallas_call(
        body, name=name, grid=(t // tm,),
        in_specs=[row, vec, row, row], out_specs=[row, vec],
        out_shape=[jax.ShapeDtypeStruct((t, d), F32), jax.ShapeDtypeStruct((1, d), F32)],
        compiler_params=_params("arbitrary"),
    )(x, gain, dy, dres)


def _loss_head(y, target):
    t, d = y.shape
    tm = _tile(t, 512)
    steps = t // tm

    def body(y_ref, t_ref, dy_ref, l_ref, acc_ref):
        e = y_ref[...] - t_ref[...]
        dy_ref[...] = e * (1.0 / d)
        part = jnp.sum(e * e, axis=0, keepdims=True)

        @pl.when(pl.program_id(0) == 0)
        def _():
            acc_ref[...] = part

        @pl.when(pl.program_id(0) > 0)
        def _():
            acc_ref[...] += part

        @pl.when(pl.program_id(0) == steps - 1)
        def _():
            l_ref[...] = jnp.full((1, LANES), (0.5 / d), F32) * jnp.sum(acc_ref[...])

    row = pl.BlockSpec((tm, d), lambda i: (i, 0))
    return pl.pallas_call(
        body, name="loss_head", grid=(steps,),
        in_specs=[row, row], out_specs=[row, pl.BlockSpec((1, LANES), lambda i: (0, 0))],
        out_shape=[jax.ShapeDtypeStruct((t, d), F32), jax.ShapeDtypeStruct((1, LANES), F32)],
        scratch_shapes=[pltpu.VMEM((1, d), F32)],
        compiler_params=_params("arbitrary"),
    )(y, target)


def _swiglu_fwd(h, w_gate_t, w_up_t, *, name):
    t, d = h.shape
    f = w_gate_t.shape[0]
    tm, tn = _tile(t, 1024), _tile(f, 256)

    def body(h_ref, wg_ref, wu_ref, g_ref, u_ref, a_ref):
        hv = h_ref[...]
        g = _dot_nt(hv, wg_ref[...])
        u = _dot_nt(hv, wu_ref[...])
        g_ref[...] = g
        u_ref[...] = u
        a_ref[...] = (g * _sigmoid(g) * u).astype(BF16)

    wspec = pl.BlockSpec((tn, d), lambda i, j: (j, 0))
    ospec = pl.BlockSpec((tm, tn), lambda i, j: (i, j))
    return pl.pallas_call(
        body, name=name, grid=(t // tm, f // tn),
        in_specs=[pl.BlockSpec((tm, d), lambda i, j: (i, 0)), wspec, wspec],
        out_specs=[ospec, ospec, ospec],
        out_shape=[jax.ShapeDtypeStruct((t, f), F32), jax.ShapeDtypeStruct((t, f), F32),
                   jax.ShapeDtypeStruct((t, f), BF16)],
        compiler_params=_params("parallel", "parallel"),
    )(h, w_gate_t, w_up_t)


def _swiglu_bwd(dx, w_down, g, u, *, name):
    t, d = dx.shape
    f = w_down.shape[0]
    tm, tn = _tile(t, 1024), _tile(f, 256)

    def body(dx_ref, wd_ref, g_ref, u_ref, dg_ref, du_ref):
        dact = _dot_nt(dx_ref[...].astype(BF16), wd_ref[...])
        gv, uv = g_ref[...], u_ref[...]
        sg = _sigmoid(gv)
        dg_ref[...] = (dact * uv * sg * (1.0 + gv * (1.0 - sg))).astype(BF16)
        du_ref[...] = (dact * gv * sg).astype(BF16)

    ospec = pl.BlockSpec((tm, tn), lambda i, j: (i, j))
    return pl.pallas_call(
        body, name=name, grid=(t // tm, f // tn),
        in_specs=[pl.BlockSpec((tm, d), lambda i, j: (i, 0)), pl.BlockSpec((tn, d), lambda i, j: (j, 0)),
                  ospec, ospec],
        out_specs=[ospec, ospec],
        out_shape=[jax.ShapeDtypeStruct((t, f), BF16), jax.ShapeDtypeStruct((t, f), BF16)],
        compiler_params=_params("parallel", "parallel"),
    )(dx, w_down, g, u)


def _ple_fwd(x, p, w_gate, w_proj_t, *, name):
    t, d = x.shape
    e = p.shape[1]
    tm, tn = _tile(t, 1024), _tile(d, 512)

    def body(xf_ref, xr_ref, p_ref, wg_ref, wp_ref, o_ref):
        s = _dot(xf_ref[...].astype(BF16), wg_ref[...])
        ple = _dot_nt(p_ref[...].astype(BF16), wp_ref[...])
        o_ref[...] = xr_ref[...] + _sigmoid(s) * ple

    return pl.pallas_call(
        body, name=name, grid=(t // tm, d // tn),
        in_specs=[pl.BlockSpec((tm, d), lambda i, j: (i, 0)), pl.BlockSpec((tm, tn), lambda i, j: (i, j)),
                  pl.BlockSpec((tm, e), lambda i, j: (i, 0)), pl.BlockSpec((d, tn), lambda i, j: (0, j)),
                  pl.BlockSpec((tn, e), lambda i, j: (j, 0))],
        out_specs=pl.BlockSpec((tm, tn), lambda i, j: (i, j)),
        out_shape=jax.ShapeDtypeStruct((t, d), F32),
        compiler_params=_params("parallel", "parallel"),
    )(x, x, p, w_gate, w_proj_t)


def _ple_bwd(x, p, w_gate, w_proj_t, dout, *, name):
    t, d = x.shape
    e = p.shape[1]
    tm, tn = _tile(t, 1024), _tile(d, 512)

    def body(xf_ref, p_ref, wg_ref, wp_ref, do_ref, ds_ref, dple_ref):
        s = _dot(xf_ref[...].astype(BF16), wg_ref[...])
        ple = _dot_nt(p_ref[...].astype(BF16), wp_ref[...])
        gate = _sigmoid(s)
        dov = do_ref[...]
        dple_ref[...] = (dov * gate).astype(BF16)
        ds_ref[...] = (dov * ple * gate * (1.0 - gate)).astype(BF16)

    ospec = pl.BlockSpec((tm, tn), lambda i, j: (i, j))
    return pl.pallas_call(
        body, name=name, grid=(t // tm, d // tn),
        in_specs=[pl.BlockSpec((tm, d), lambda i, j: (i, 0)), pl.BlockSpec((tm, e), lambda i, j: (i, 0)),
                  pl.BlockSpec((d, tn), lambda i, j: (0, j)), pl.BlockSpec((tn, e), lambda i, j: (j, 0)), ospec],
        out_specs=[ospec, ospec],
        out_shape=[jax.ShapeDtypeStruct((t, d), BF16), jax.ShapeDtypeStruct((t, d), BF16)],
        compiler_params=_params("parallel", "parallel"),
    )(x, p, w_gate, w_proj_t, dout)


CONV_TIME_TILE = 256
CONV_HALO = 8


def _conv_taps(ext, w):
    acc = ext[CONV_HALO:, :] * w[CONV_WIDTH - 1:CONV_WIDTH, :]
    shifted = [ext[CONV_HALO:, :]]
    for j in range(1, CONV_WIDTH):
        sh = pltpu.roll(ext, j, 0)[CONV_HALO:, :]
        shifted.append(sh)
        acc = acc + sh * w[CONV_WIDTH - 1 - j:CONV_WIDTH - j, :]
    return acc, shifted


def _conv_fwd(u, w, b):
    t, c = u.shape
    tc = _tile(c, 256)
    tt = CONV_TIME_TILE

    def body(u_ref, w_ref, b_ref, o_ref):
        wv, bv = w_ref[...], b_ref[...]

        def tile(start, ext):
            pre = _conv_taps(ext, wv)[0] + bv
            o_ref[pl.ds(start, tt), :] = pre * _sigmoid(pre)

        tile(0, jnp.concatenate([jnp.zeros((CONV_HALO, tc), F32), u_ref[0:tt, :]], axis=0))

        def loop(i, carry):
            start = pl.multiple_of(i * tt, tt)
            tile(start, u_ref[pl.ds(start - CONV_HALO, tt + CONV_HALO), :])
            return carry

        lax.fori_loop(1, t // tt, loop, 0)

    col = pl.BlockSpec((t, tc), lambda j: (0, j))
    return pl.pallas_call(
        body, name="conv_fwd", grid=(c // tc,),
        in_specs=[col, pl.BlockSpec((CONV_WIDTH, tc), lambda j: (0, j)), pl.BlockSpec((1, tc), lambda j: (0, j))],
        out_specs=col, out_shape=jax.ShapeDtypeStruct((t, c), F32),
        compiler_params=_params("parallel"),
    )(u, w, b)


def _conv_bwd(u, w, b, dact):
    t, c = u.shape
    tc = _tile(c, 256)
    tt = CONV_TIME_TILE

    def body(u_ref, w_ref, b_ref, da_ref, du_ref, dw_ref, db_ref, dpre_ref):
        wv, bv = w_ref[...], b_ref[...]

        def tile(start, ext, sums):
            acc, shifted = _conv_taps(ext, wv)
            pre = acc + bv
            sg = _sigmoid(pre)
            dpre = da_ref[pl.ds(start, tt), :] * (sg * (1.0 + pre * (1.0 - sg)))
            dpre_ref[pl.ds(start, tt), :] = dpre
            new = [sums[0] + jnp.sum(dpre, axis=0, keepdims=True)]
            for j in range(CONV_WIDTH):
                new.append(sums[1 + j] + jnp.sum(dpre * shifted[j], axis=0, keepdims=True))
            return tuple(new)

        zero = jnp.zeros((1, tc), F32)
        sums = tile(0, jnp.concatenate([jnp.zeros((CONV_HALO, tc), F32), u_ref[0:tt, :]], axis=0),
                    (zero,) * (1 + CONV_WIDTH))

        def loop(i, sums):
            start = pl.multiple_of(i * tt, tt)
            return tile(start, u_ref[pl.ds(start - CONV_HALO, tt + CONV_HALO), :], sums)

        sums = lax.fori_loop(1, t // tt, loop, sums)
        db_ref[...] = sums[0]
        dw_ref[...] = jnp.concatenate([sums[1 + (CONV_WIDTH - 1 - k)] for k in range(CONV_WIDTH)], axis=0)
        dpre_ref[pl.ds(t, CONV_HALO), :] = jnp.zeros((CONV_HALO, tc), F32)

        def loop2(i, carry):
            start = pl.multiple_of(i * tt, tt)
            ext = dpre_ref[pl.ds(start, tt + CONV_HALO), :]
            acc = ext[0:tt, :] * wv[CONV_WIDTH - 1:CONV_WIDTH, :]
            for j in range(1, CONV_WIDTH):
                acc = acc + pltpu.roll(ext, tt + CONV_HALO - j, 0)[0:tt, :] * wv[CONV_WIDTH - 1 - j:CONV_WIDTH - j, :]
            du_ref[pl.ds(start, tt), :] = acc.astype(BF16)
            return carry

        lax.fori_loop(0, t // tt, loop2, 0)

    col = pl.BlockSpec((t, tc), lambda j: (0, j))
    return pl.pallas_call(
        body, name="conv_bwd", grid=(c // tc,),
        in_specs=[col, pl.BlockSpec((CONV_WIDTH, tc), lambda j: (0, j)), pl.BlockSpec((1, tc), lambda j: (0, j)), col],
        out_specs=[col, pl.BlockSpec((CONV_WIDTH, tc), lambda j: (0, j)), pl.BlockSpec((1, tc), lambda j: (0, j))],
        out_shape=[jax.ShapeDtypeStruct((t, c), BF16), jax.ShapeDtypeStruct((CONV_WIDTH, c), F32),
                   jax.ShapeDtypeStruct((1, c), F32)],
        scratch_shapes=[pltpu.VMEM((t + CONV_HALO, tc), F32)],
        compiler_params=_params("parallel"),
    )(u, w, b, dact)


def _softplus(v):
    e = jnp.exp(-jnp.abs(v))
    w = 1.0 + e
    log1p = jnp.where(w == 1.0, e, jnp.log(w) * (e / jnp.where(w == 1.0, 1.0, w - 1.0)))
    return jnp.maximum(v, 0.0) + log1p


def _ssd_prep_fwd(dt_raw, dt_bias, a_log):
    t = dt_raw.shape[0]
    cl = SSD_CHUNK

    def body(r_ref, b_ref, al_ref, dt_ref, acs_ref):
        dt = _softplus(r_ref[...] + b_ref[...])
        adt = dt * (-jnp.exp(al_ref[...]))
        li = lax.broadcasted_iota(jnp.int32, (cl, cl), 0)
        si = lax.broadcasted_iota(jnp.int32, (cl, cl), 1)
        tri = (si <= li).astype(F32)
        dt_ref[...] = dt
        acs_ref[...] = jnp.dot(tri, adt, preferred_element_type=F32, precision=HIGHEST)

    row = pl.BlockSpec((cl, LANES), lambda i: (i, 0))
    vec = pl.BlockSpec((1, LANES), lambda i: (0, 0))
    return pl.pallas_call(
        body, name="ssd_prep_fwd", grid=(t // cl,),
        in_specs=[row, vec, vec], out_specs=[row, row],
        out_shape=[jax.ShapeDtypeStruct((t, LANES), F32), jax.ShapeDtypeStruct((t, LANES), F32)],
        compiler_params=_params("parallel"),
    )(dt_raw, dt_bias, a_log)


def _ssd_prep_bwd(dt_raw, dt_bias, ddt):
    t = dt_raw.shape[0]
    tm = _tile(t, 512)

    def body(r_ref, b_ref, d_ref, o_ref, db_ref):
        g = d_ref[...] * _sigmoid(r_ref[...] + b_ref[...])
        o_ref[...] = g.astype(BF16)
        part = jnp.sum(g, axis=0, keepdims=True)

        @pl.when(pl.program_id(0) == 0)
        def _():
            db_ref[...] = part

        @pl.when(pl.program_id(0) > 0)
        def _():
            db_ref[...] += part

    row = pl.BlockSpec((tm, LANES), lambda i: (i, 0))
    vec = pl.BlockSpec((1, LANES), lambda i: (0, 0))
    return pl.pallas_call(
        body, name="ssd_prep_bwd", grid=(t // tm,),
        in_specs=[row, vec, row], out_specs=[row, vec],
        out_shape=[jax.ShapeDtypeStruct((t, LANES), BF16), jax.ShapeDtypeStruct((1, LANES), F32)],
        compiler_params=_params("arbitrary"),
    )(dt_raw, dt_bias, ddt)


GROUP_W = D_INNER // SSM_GROUPS
PAIRS_PER_GROUP = GROUP_W // LANES


def _head_cols(acs_pair, lt64):
    rolled = pltpu.roll(acs_pair, ATT_HEAD_DIM, 1)
    return jnp.where(lt64, acs_pair, rolled), jnp.where(lt64, rolled, acs_pair)


def _ssd_fwd(xbc, dt_rep, acs_rep, acs_t, dskip_rep):
    t = xbc.shape[0]
    cl = SSD_CHUNK
    nc = t // cl

    def body(xbc_ref, dt_ref, acs_ref, acst_ref, dskip_ref, y_ref, hin_ref, state_ref):
        @pl.when(pl.program_id(0) == 0)
        def _():
            state_ref[...] = jnp.zeros_like(state_ref)

        lt64 = _lane_lt64(cl)
        li = lax.broadcasted_iota(jnp.int32, (cl, cl), 0)
        si = lax.broadcasted_iota(jnp.int32, (cl, cl), 1)
        causal = li >= si
        hin_ref[...] = state_ref[...]
        for g in range(SSM_GROUPS):
            gsl = slice(g * GROUP_W, (g + 1) * GROUP_W)
            xg = xbc_ref[:, gsl]
            bg = xbc_ref[:, D_INNER + g * SSM_STATE:D_INNER + (g + 1) * SSM_STATE]
            cg = xbc_ref[:, D_INNER + SSM_GROUPS * SSM_STATE + g * SSM_STATE:
                         D_INNER + SSM_GROUPS * SSM_STATE + (g + 1) * SSM_STATE]
            acs = acs_ref[:, gsl]
            xdt = xg * dt_ref[:, gsl]
            atot = acs[cl - 1:cl, :]
            hin = state_ref[:, gsl]
            cgb = cg.astype(BF16)
            gmat = _dot_nt(cgb, bg.astype(BF16))
            yoff = _dot(cgb, hin.astype(BF16)) * jnp.exp(acs)
            snew = _dot(bg.T.astype(BF16), (xdt * jnp.exp(atot - acs)).astype(BF16))
            state_ref[:, gsl] = hin * jnp.exp(atot) + snew
            xdtb = xdt.astype(BF16)
            for pr in range(PAIRS_PER_GROUP):
                psl = slice(pr * LANES, (pr + 1) * LANES)
                cols = _head_cols(acs[:, psl], lt64)
                xp = xdtb[:, psl]
                ys = []
                for hh in range(2):
                    h = (g * PAIRS_PER_GROUP + pr) * 2 + hh
                    seg = cols[hh] - acst_ref[h:h + 1, :]
                    lm = jnp.exp(jnp.where(causal, seg, NEG_BIG))
                    ys.append(_dot((gmat * lm).astype(BF16), xp))
                ydiag = jnp.where(lt64, ys[0], ys[1])
                osl = slice(g * GROUP_W + pr * LANES, g * GROUP_W + (pr + 1) * LANES)
                y_ref[:, osl] = ydiag + yoff[:, psl] + xg[:, psl] * dskip_ref[:, osl]

    row = lambda w: pl.BlockSpec((cl, w), lambda c: (c, 0))
    return pl.pallas_call(
        body, name="ssd_fwd", grid=(nc,),
        in_specs=[row(CONV_DIM), row(D_INNER), row(D_INNER),
                  pl.BlockSpec((SSM_HEADS, cl), lambda c: (0, c)), pl.BlockSpec((1, D_INNER), lambda c: (0, 0))],
        out_specs=[row(D_INNER), pl.BlockSpec((None, SSM_STATE, D_INNER), lambda c: (c, 0, 0))],
        out_shape=[jax.ShapeDtypeStruct((t, D_INNER), F32), jax.ShapeDtypeStruct((nc, SSM_STATE, D_INNER), F32)],
        scratch_shapes=[pltpu.VMEM((SSM_STATE, D_INNER), F32)],
        compiler_params=_params("arbitrary"),
    )(xbc, dt_rep, acs_rep, acs_t, dskip_rep)


def _ssd_bwd(xbc, dt_rep, acs_rep, acs_t, dskip_rep, a_rep, hin_all, dy):
    t = xbc.shape[0]
    cl = SSD_CHUNK
    nc = t // cl

    def body(xbc_ref, dt_ref, acs_ref, acst_ref, dskip_ref, a_ref, hin_ref, dy_ref,
             dxbc_ref, ddt_ref, da_ref, dds_ref, dstate_ref, dacs_ref, dxs_ref):
        step = pl.program_id(0)

        @pl.when(step == 0)
        def _():
            dstate_ref[...] = jnp.zeros_like(dstate_ref)
            da_ref[...] = jnp.zeros_like(da_ref)
            dds_ref[...] = jnp.zeros_like(dds_ref)

        bd = _head_block_diag()
        lt64 = _lane_lt64(cl)
        li = lax.broadcasted_iota(jnp.int32, (cl, cl), 0)
        si = lax.broadcasted_iota(jnp.int32, (cl, cl), 1)
        lower = li >= si
        upper = si >= li
        last_row = lax.broadcasted_iota(jnp.int32, (cl, GROUP_W), 0) == cl - 1
        for g in range(SSM_GROUPS):
            gsl = slice(g * GROUP_W, (g + 1) * GROUP_W)
            bsl = slice(D_INNER + g * SSM_STATE, D_INNER + (g + 1) * SSM_STATE)
            csl = slice(D_INNER + SSM_GROUPS * SSM_STATE + g * SSM_STATE,
                        D_INNER + SSM_GROUPS * SSM_STATE + (g + 1) * SSM_STATE)
            xg = xbc_ref[:, gsl]
            bg = xbc_ref[:, bsl]
            cg = xbc_ref[:, csl]
            bgb, cgb = bg.astype(BF16), cg.astype(BF16)
            acs = acs_ref[:, gsl]
            xdt = xg * dt_ref[:, gsl]
            atot = acs[cl - 1:cl, :]
            eg = jnp.exp(acs)
            dk = jnp.exp(atot - acs)
            etot = jnp.exp(atot)
            hin = hin_ref[:, gsl]
            hinb = hin.astype(BF16)
            dh = dstate_ref[:, gsl]
            dhb = dh.astype(BF16)
            dyg = dy_ref[:, gsl]

            gmat = _dot_nt(cgb, bgb)
            gmat_t = _dot_nt(bgb, cgb)
            ch = _dot(cgb, hinb)
            dacs = _head_sums(dyg * ch * eg, bd)
            dye = (dyg * eg).astype(BF16)
            dc = _dot_nt(dye, hinb)
            dhin = _dot(cg.T.astype(BF16), dye)
            bdh = _dot(bgb, dhb)
            dxs = bdh * dk
            xdk = xdt * dk
            db = _dot_nt(xdk.astype(BF16), dhb)
            ddk = _head_sums(bdh * xdk, bd)
            dacs = dacs - ddk
            datot = jnp.sum(ddk, axis=0, keepdims=True) + etot * _head_sums(
                jnp.sum(dh * hin, axis=0, keepdims=True), bd)
            dacs = dacs + jnp.where(last_row, datot, 0.0)
            dstate_ref[:, gsl] = dh * etot + dhin

            xdtb = xdt.astype(BF16)
            dgsum = jnp.zeros((cl, cl), F32)
            dgsum_t = jnp.zeros((cl, cl), F32)
            for pr in range(PAIRS_PER_GROUP):
                psl = slice(pr * LANES, (pr + 1) * LANES)
                cols = _head_cols(acs[:, psl], lt64)
                xp = xdtb[:, psl]
                dyp = dyg[:, psl].astype(BF16)
                dx1, dac = [], []
                for hh in range(2):
                    h = (g * PAIRS_PER_GROUP + pr) * 2 + hh
                    mine = lt64 if hh == 0 else jnp.logical_not(lt64)
                    row = acst_ref[h:h + 1, :]
                    lm = jnp.exp(jnp.where(lower, cols[hh] - row, NEG_BIG))
                    lm_t = jnp.exp(jnp.where(upper, row - cols[hh], NEG_BIG))
                    dyh = jnp.where(mine, dyp, jnp.zeros_like(dyp))
                    xh = jnp.where(mine, xp, jnp.zeros_like(xp))
                    dm = _dot_nt(dyh, xp)
                    dm_t = _dot_nt(xh, dyp)
                    m_t = gmat_t * lm_t
                    dx1.append(_dot(m_t.astype(BF16), dyp))
                    w = dm * (gmat * lm)
                    w_t = dm_t * m_t
                    dac.append(jnp.sum(w, axis=1, keepdims=True) - jnp.sum(w_t, axis=1, keepdims=True))
                    dgsum = dgsum + dm * lm
                    dgsum_t = dgsum_t + dm_t * lm_t
                osl = slice(g * GROUP_W + pr * LANES, g * GROUP_W + (pr + 1) * LANES)
                dxs_ref[:, osl] = dxs[:, psl] + jnp.where(lt64, dx1[0], dx1[1])
                dacs_ref[:, osl] = dacs[:, psl] + jnp.where(lt64, jnp.broadcast_to(dac[0], (cl, LANES)),
                                                             jnp.broadcast_to(dac[1], (cl, LANES)))
            dxbc_ref[:, csl] = dc + _dot(dgsum.astype(BF16), bgb)
            dxbc_ref[:, bsl] = db + _dot(dgsum_t.astype(BF16), cgb)

        dadt = _split_dot(upper.astype(BF16), dacs_ref[...])
        xall = xbc_ref[:, 0:D_INNER]
        dtall = dt_ref[...]
        dxsall = dxs_ref[...]
        dyall = dy_ref[...]
        ddt_ref[...] = dadt * a_ref[...] + _head_sums(dxsall * xall, bd)
        dxbc_ref[:, 0:D_INNER] = dxsall * dtall + dyall * dskip_ref[...]
        da_ref[...] += jnp.sum(dadt * dtall, axis=0, keepdims=True)
        dds_ref[...] += jnp.sum(dyall * xall, axis=0, keepdims=True)

        @pl.when(step == nc - 1)
        def _():
            dds_ref[...] = _head_sums(dds_ref[...], bd)

    row = lambda w: pl.BlockSpec((cl, w), lambda c: (nc - 1 - c, 0))
    vec = pl.BlockSpec((1, D_INNER), lambda c: (0, 0))
    return pl.pallas_call(
        body, name="ssd_bwd", grid=(nc,),
        in_specs=[row(CONV_DIM), row(D_INNER), row(D_INNER),
                  pl.BlockSpec((SSM_HEADS, cl), lambda c: (0, nc - 1 - c)), vec, vec,
                  pl.BlockSpec((None, SSM_STATE, D_INNER), lambda c: (nc - 1 - c, 0, 0)), row(D_INNER)],
        out_specs=[row(CONV_DIM), row(D_INNER), vec, vec],
        out_shape=[jax.ShapeDtypeStruct((t, CONV_DIM), F32), jax.ShapeDtypeStruct((t, D_INNER), F32),
                   jax.ShapeDtypeStruct((1, D_INNER), F32), jax.ShapeDtypeStruct((1, D_INNER), F32)],
        scratch_shapes=[pltpu.VMEM((SSM_STATE, D_INNER), F32), pltpu.VMEM((cl, D_INNER), F32),
                        pltpu.VMEM((cl, D_INNER), F32)],
        compiler_params=_params("arbitrary"),
    )(xbc, dt_rep, acs_rep, acs_t, dskip_rep, a_rep, hin_all, dy)


def _gate_norm_fwd(y, z, w):
    t, c = y.shape
    tm = _tile(t, 256)

    def body(y_ref, z_ref, w_ref, o_ref):
        for g in range(SSM_GROUPS):
            gsl = slice(g * GROUP_W, (g + 1) * GROUP_W)
            zv = z_ref[:, gsl]
            v = y_ref[:, gsl] * (zv * _sigmoid(zv))
            r = lax.rsqrt(jnp.mean(v * v, axis=-1, keepdims=True) + NORM_EPS)
            o_ref[:, gsl] = (v * r * w_ref[:, gsl]).astype(BF16)

    row = pl.BlockSpec((tm, c), lambda i: (i, 0))
    return pl.pallas_call(
        body, name="gate_norm_fwd", grid=(t // tm,),
        in_specs=[row, row, pl.BlockSpec((1, c), lambda i: (0, 0))], out_specs=row,
        out_shape=jax.ShapeDtypeStruct((t, c), BF16),
        compiler_params=_params("parallel"),
    )(y, z, w)


def _gate_norm_bwd(y, z, w, dout):
    t, c = y.shape
    tm = _tile(t, 256)

    def body(y_ref, z_ref, w_ref, do_ref, dy_ref, dz_ref, dw_ref):
        @pl.when(pl.program_id(0) == 0)
        def _():
            dw_ref[...] = jnp.zeros_like(dw_ref)

        for g in range(SSM_GROUPS):
            gsl = slice(g * GROUP_W, (g + 1) * GROUP_W)
            zv, yv, dov = z_ref[:, gsl], y_ref[:, gsl], do_ref[:, gsl]
            sg = _sigmoid(zv)
            sz = zv * sg
            v = yv * sz
            r = lax.rsqrt(jnp.mean(v * v, axis=-1, keepdims=True) + NORM_EPS)
            vh = v * r
            dvh = dov * w_ref[:, gsl]
            mean = jnp.mean(dvh * vh, axis=-1, keepdims=True)
            dv = r * (dvh - vh * mean)
            dy_ref[:, gsl] = dv * sz
            dz_ref[:, gsl] = (dv * yv * (sg * (1.0 + zv * (1.0 - sg)))).astype(BF16)
            dw_ref[:, gsl] += jnp.sum(dov * vh, axis=0, keepdims=True)

    row = pl.BlockSpec((tm, c), lambda i: (i, 0))
    vec = pl.BlockSpec((1, c), lambda i: (0, 0))
    return pl.pallas_call(
        body, name="gate_norm_bwd", grid=(t // tm,),
        in_specs=[row, row, vec, row], out_specs=[row, row, vec],
        out_shape=[jax.ShapeDtypeStruct((t, c), F32), jax.ShapeDtypeStruct((t, c), BF16),
                   jax.ShapeDtypeStruct((1, c), F32)],
        compiler_params=_params("arbitrary"),
    )(y, z, w, dout)


ATT_W = ATT_HEADS * ATT_HEAD_DIM
N_QKV_BLOCKS = 9
ATT_SCALE = 1.0 / math.sqrt(ATT_HEAD_DIM)


def _slope(h):
    return 2.0 ** (-8.0 * (h + 1) / ATT_HEADS)


def _qk_norm_fwd(qkv, gq, gk):
    t = qkv.shape[0]
    tm = _tile(t, 512)

    def body(x_ref, gq_ref, gk_ref, o_ref):
        cb = pl.program_id(1)
        xv = x_ref[...]

        def normed(gain):
            ms = _head_sums(xv * xv, _head_block_diag()) * (1.0 / ATT_HEAD_DIM)
            return (xv * lax.rsqrt(ms + NORM_EPS) * gain).astype(BF16)

        @pl.when(cb % 3 == 0)
        def _():
            o_ref[...] = normed(gq_ref[...])

        @pl.when(cb % 3 == 1)
        def _():
            o_ref[...] = normed(gk_ref[...])

        @pl.when(cb % 3 == 2)
        def _():
            o_ref[...] = xv.astype(BF16)

    blk = pl.BlockSpec((tm, ATT_W), lambda i, j: (i, j))
    vec = pl.BlockSpec((1, ATT_W), lambda i, j: (0, 0))
    return pl.pallas_call(
        body, name="qk_norm_fwd", grid=(t // tm, N_QKV_BLOCKS),
        in_specs=[blk, vec, vec], out_specs=blk,
        out_shape=jax.ShapeDtypeStruct(qkv.shape, BF16),
        compiler_params=_params("parallel", "parallel"),
    )(qkv, gq, gk)


def _qk_norm_bwd(qkv, gq, gk, grads):
    t = qkv.shape[0]
    tm = _tile(t, 256)

    def body(x_ref, gq_ref, gk_ref, *rest):
        g_refs = rest[:N_QKV_BLOCKS]
        o_ref, dgq_ref, dgk_ref = rest[N_QKV_BLOCKS:]
        cb = pl.program_id(1)

        @pl.when(jnp.logical_and(pl.program_id(0) == 0, cb == 0))
        def _():
            dgq_ref[...] = jnp.zeros_like(dgq_ref)
            dgk_ref[...] = jnp.zeros_like(dgk_ref)

        def norm_bwd(dy, gain, dg_ref):
            bd = _head_block_diag()
            xv = x_ref[...]
            ms = _head_sums(xv * xv, bd) * (1.0 / ATT_HEAD_DIM)
            r = lax.rsqrt(ms + NORM_EPS)
            xh = xv * r
            dxh = dy * gain
            mean = _head_sums(dxh * xh, bd) * (1.0 / ATT_HEAD_DIM)
            o_ref[...] = (r * (dxh - xh * mean)).astype(BF16)
            dg_ref[...] += jnp.sum(dy * xh, axis=0, keepdims=True)

        for k in range(N_QKV_BLOCKS):
            @pl.when(cb == k)
            def _(k=k):
                if k % 3 == 0:
                    norm_bwd(g_refs[k][...], gq_ref[...], dgq_ref)
                elif k % 3 == 1:
                    norm_bwd(g_refs[k][...], gk_ref[...], dgk_ref)
                else:
                    o_ref[...] = g_refs[k][...].astype(BF16)

    blk = pl.BlockSpec((tm, ATT_W), lambda i, j: (i, j))
    one = pl.BlockSpec((tm, ATT_W), lambda i, j: (i, 0))
    vec = pl.BlockSpec((1, ATT_W), lambda i, j: (0, 0))
    return pl.pallas_call(
        body, name="qk_norm_bwd", grid=(t // tm, N_QKV_BLOCKS),
        in_specs=[blk, vec, vec] + [one] * N_QKV_BLOCKS, out_specs=[blk, vec, vec],
        out_shape=[jax.ShapeDtypeStruct(qkv.shape, BF16), jax.ShapeDtypeStruct((1, ATT_W), F32),
                   jax.ShapeDtypeStruct((1, ATT_W), F32)],
        compiler_params=_params("arbitrary", "arbitrary"),
    )(qkv, gq, gk, *grads)


def _attn_logits(qm, kcat, slope, dist_bias, valid):
    s = _dot_nt(qm, kcat) * ATT_SCALE - slope * dist_bias
    return jnp.where(valid, s, NEG_BIG)


def _attn_fwd(qkvn, g, dil):
    t = qkvn.shape[0]
    lu = t // dil
    nb = lu // ATT_BLOCK
    view = qkvn.reshape(lu, dil * N_QKV_BLOCKS * ATT_W)
    bq = ATT_BLOCK

    def body(q_ref, kc_ref, kp_ref, vc_ref, vp_ref, o_ref, l_ref):
        n = pl.program_id(1)
        lt64 = _lane_lt64(bq)
        qi = lax.broadcasted_iota(jnp.int32, (bq, 2 * bq), 0)
        kk = lax.broadcasted_iota(jnp.int32, (bq, 2 * bq), 1)
        dist = qi + bq - kk
        valid = (dist >= 0) & (dist <= bq) & ((kk >= bq) | (n > 0))
        dist_bias = dist.astype(F32) * float(dil)
        for pair in range(ATT_HEADS // 2):
            sl = slice(pair * LANES, (pair + 1) * LANES)
            qp = q_ref[:, sl]
            kcat = jnp.concatenate([kp_ref[:, sl], kc_ref[:, sl]], axis=0)
            vcat = jnp.concatenate([vp_ref[:, sl], vc_ref[:, sl]], axis=0)
            outs, lses = [], []
            for hh in range(2):
                mine = lt64 if hh == 0 else jnp.logical_not(lt64)
                qm = jnp.where(mine, qp, jnp.zeros_like(qp))
                s = _attn_logits(qm, kcat, _slope(pair * 2 + hh), dist_bias, valid)
                m = jnp.max(s, axis=1, keepdims=True)
                p = jnp.exp(s - m)
                l = jnp.sum(p, axis=1, keepdims=True)
                outs.append(_dot(p.astype(BF16), vcat) * (1.0 / l))
                lses.append(jnp.broadcast_to(m + jnp.log(l), (bq, LANES)))
            o_ref[:, sl] = jnp.where(lt64, outs[0], outs[1])
            l_ref[:, sl] = jnp.where(lt64, lses[0], lses[1])

    def col(j):
        return lambda r, n: (n, r * N_QKV_BLOCKS + g * 3 + j)

    def col_prev(j):
        return lambda r, n: (jnp.maximum(n - 1, 0), r * N_QKV_BLOCKS + g * 3 + j)

    blk = lambda f: pl.BlockSpec((bq, ATT_W), f)
    out = pl.BlockSpec((bq, ATT_W), lambda r, n: (n, r))
    o, lse = pl.pallas_call(
        body, name=f"attn_fwd_g{g}", grid=(dil, nb),
        in_specs=[blk(col(0)), blk(col(1)), blk(col_prev(1)), blk(col(2)), blk(col_prev(2))],
        out_specs=[out, out],
        out_shape=[jax.ShapeDtypeStruct((lu, dil * ATT_W), F32), jax.ShapeDtypeStruct((lu, dil * ATT_W), F32)],
        compiler_params=_params("parallel", "arbitrary"),
    )(view, view, view, view, view)
    return o.reshape(t, ATT_W), lse.reshape(t, ATT_W)


def _attn_combine_fwd(outs, lses):
    t = outs[0].shape[0]
    tm = _tile(t, 256)

    def body(o0, o1, o2, l0, l1, l2, ob_ref, of_ref, lt_ref):
        a, b, c = l0[...], l1[...], l2[...]
        m = jnp.maximum(jnp.maximum(a, b), c)
        ea, eb, ec = jnp.exp(a - m), jnp.exp(b - m), jnp.exp(c - m)
        ssum = ea + eb + ec
        o = (ea * o0[...] + eb * o1[...] + ec * o2[...]) / ssum
        ob_ref[...] = o.astype(BF16)
        of_ref[...] = o
        lt_ref[...] = m + jnp.log(ssum)

    row = pl.BlockSpec((tm, ATT_W), lambda i: (i, 0))
    return pl.pallas_call(
        body, name="attn_combine_fwd", grid=(t // tm,),
        in_specs=[row] * 6, out_specs=[row] * 3,
        out_shape=[jax.ShapeDtypeStruct((t, ATT_W), BF16), jax.ShapeDtypeStruct((t, ATT_W), F32),
                   jax.ShapeDtypeStruct((t, ATT_W), F32)],
        compiler_params=_params("parallel"),
    )(*outs, *lses)


def _attn_combine_bwd(do, o):
    t = do.shape[0]
    tm = _tile(t, 256)

    def body(do_ref, o_ref, dl_ref, dob_ref):
        dov = do_ref[...]
        dl_ref[...] = _head_sums(dov * o_ref[...], _head_block_diag())
        dob_ref[...] = dov.astype(BF16)

    row = pl.BlockSpec((tm, ATT_W), lambda i: (i, 0))
    return pl.pallas_call(
        body, name="attn_combine_bwd", grid=(t // tm,),
        in_specs=[row, row], out_specs=[row, row],
        out_shape=[jax.ShapeDtypeStruct((t, ATT_W), F32), jax.ShapeDtypeStruct((t, ATT_W), BF16)],
        compiler_params=_params("parallel"),
    )(do, o)


def _attn_bwd_dq(qkvn, do_b, l_rep, dl_rep, g, dil):
    t = qkvn.shape[0]
    lu = t // dil
    nb = lu // ATT_BLOCK
    view = qkvn.reshape(lu, dil * N_QKV_BLOCKS * ATT_W)
    shp = (lu, dil * ATT_W)
    bq = ATT_BLOCK

    def body(q_ref, kc_ref, kp_ref, vc_ref, vp_ref, do_ref, l_ref, dl_ref, dq_ref):
        n = pl.program_id(1)
        lt64 = _lane_lt64(bq)
        qi = lax.broadcasted_iota(jnp.int32, (bq, 2 * bq), 0)
        kk = lax.broadcasted_iota(jnp.int32, (bq, 2 * bq), 1)
        dist = qi + bq - kk
        valid = (dist >= 0) & (dist <= bq) & ((kk >= bq) | (n > 0))
        dist_bias = dist.astype(F32) * float(dil)
        for pair in range(ATT_HEADS // 2):
            sl = slice(pair * LANES, (pair + 1) * LANES)
            qp = q_ref[:, sl]
            dop = do_ref[:, sl]
            kcat = jnp.concatenate([kp_ref[:, sl], kc_ref[:, sl]], axis=0)
            vcat = jnp.concatenate([vp_ref[:, sl], vc_ref[:, sl]], axis=0)
            lcols = _head_cols(l_ref[:, sl], lt64)
            dcols = _head_cols(dl_ref[:, sl], lt64)
            dqs = []
            for hh in range(2):
                mine = lt64 if hh == 0 else jnp.logical_not(lt64)
                qm = jnp.where(mine, qp, jnp.zeros_like(qp))
                dom = jnp.where(mine, dop, jnp.zeros_like(dop))
                s = _attn_logits(qm, kcat, _slope(pair * 2 + hh), dist_bias, valid)
                p = jnp.exp(s - jnp.concatenate([lcols[hh], lcols[hh]], axis=1))
                dp = _dot_nt(dom, vcat)
                ds = p * (dp - jnp.concatenate([dcols[hh], dcols[hh]], axis=1))
                dqs.append(_dot(ds.astype(BF16), kcat) * ATT_SCALE)
            dq_ref[:, sl] = jnp.where(lt64, dqs[0], dqs[1])

    def col(j):
        return lambda r, n: (n, r * N_QKV_BLOCKS + g * 3 + j)

    def col_prev(j):
        return lambda r, n: (jnp.maximum(n - 1, 0), r * N_QKV_BLOCKS + g * 3 + j)

    blk = lambda f: pl.BlockSpec((bq, ATT_W), f)
    tok = pl.BlockSpec((bq, ATT_W), lambda r, n: (n, r))
    dq = pl.pallas_call(
        body, name=f"attn_bwd_dq_g{g}", grid=(dil, nb),
        in_specs=[blk(col(0)), blk(col(1)), blk(col_prev(1)), blk(col(2)), blk(col_prev(2)), tok, tok, tok],
        out_specs=tok, out_shape=jax.ShapeDtypeStruct(shp, F32),
        compiler_params=_params("parallel", "arbitrary"),
    )(view, view, view, view, view, do_b.reshape(shp), l_rep.reshape(shp), dl_rep.reshape(shp))
    return dq.reshape(t, ATT_W)


def _attn_bwd_dkv(qkvn, do_b, l_row, dl_row, g, dil):
    t = qkvn.shape[0]
    lu = t // dil
    nb = lu // ATT_BLOCK
    view = qkvn.reshape(lu, dil * N_QKV_BLOCKS * ATT_W)
    shp = (lu, dil * ATT_W)
    bq = ATT_BLOCK

    def body(k_ref, v_ref, qc_ref, qn_ref, doc_ref, don_ref, lc_ref, ln_ref, dc_ref, dn_ref, dk_ref, dv_ref):
        n = pl.program_id(1)
        lt64 = _lane_lt64(bq)
        ki = lax.broadcasted_iota(jnp.int32, (bq, 2 * bq), 0)
        qq = lax.broadcasted_iota(jnp.int32, (bq, 2 * bq), 1)
        dist = qq - ki
        valid = (dist >= 0) & (dist <= bq) & ((qq < bq) | (n < nb - 1))
        dist_bias = dist.astype(F32) * float(dil)
        for pair in range(ATT_HEADS // 2):
            sl = slice(pair * LANES, (pair + 1) * LANES)
            kp = k_ref[:, sl]
            vp = v_ref[:, sl]
            qcat = jnp.concatenate([qc_ref[:, sl], qn_ref[:, sl]], axis=0)
            docat = jnp.concatenate([doc_ref[:, sl], don_ref[:, sl]], axis=0)
            dks, dvs = [], []
            for hh in range(2):
                h = pair * 2 + hh
                mine = lt64 if hh == 0 else jnp.logical_not(lt64)
                km = jnp.where(mine, kp, jnp.zeros_like(kp))
                vm = jnp.where(mine, vp, jnp.zeros_like(vp))
                s_t = _attn_logits(km, qcat, _slope(h), dist_bias, valid)
                l_r = jnp.concatenate([lc_ref[h:h + 1, :], ln_ref[h:h + 1, :]], axis=1)
                d_r = jnp.concatenate([dc_ref[h:h + 1, :], dn_ref[h:h + 1, :]], axis=1)
                p_t = jnp.exp(s_t - l_r)
                dvs.append(_dot(p_t.astype(BF16), docat))
                dp_t = _dot_nt(vm, docat)
                ds_t = p_t * (dp_t - d_r)
                dks.append(_dot(ds_t.astype(BF16), qcat) * ATT_SCALE)
            dk_ref[:, sl] = jnp.where(lt64, dks[0], dks[1])
            dv_ref[:, sl] = jnp.where(lt64, dvs[0], dvs[1])

    def col(j):
        return lambda r, n: (n, r * N_QKV_BLOCKS + g * 3 + j)

    blk = lambda f: pl.BlockSpec((bq, ATT_W), f)
    nxt = lambda n: jnp.minimum(n + 1, nb - 1)
    tok = pl.BlockSpec((bq, ATT_W), lambda r, n: (n, r))
    tok_next = pl.BlockSpec((bq, ATT_W), lambda r, n: (nxt(n), r))
    rowv = pl.BlockSpec((ATT_HEADS, bq), lambda r, n: (0, r * nb + n))
    rowv_next = pl.BlockSpec((ATT_HEADS, bq), lambda r, n: (0, r * nb + nxt(n)))
    dk, dv = pl.pallas_call(
        body, name=f"attn_bwd_dkv_g{g}", grid=(dil, nb),
        in_specs=[blk(col(1)), blk(col(2)), blk(col(0)),
                  blk(lambda r, n: (nxt(n), r * N_QKV_BLOCKS + g * 3)),
                  tok, tok_next, rowv, rowv_next, rowv, rowv_next],
        out_specs=[tok, tok],
        out_shape=[jax.ShapeDtypeStruct(shp, F32), jax.ShapeDtypeStruct(shp, F32)],
        compiler_params=_params("parallel", "arbitrary"),
    )(view, view, view, view, do_b.reshape(shp), do_b.reshape(shp), l_row, l_row, dl_row, dl_row)
    return dk.reshape(t, ATT_W), dv.reshape(t, ATT_W)


def _rows_by_residue(rep, dil):
    t = rep.shape[0]
    per_head = rep[:, ::ATT_HEAD_DIM]
    return per_head.reshape(t // dil, dil, ATT_HEADS).transpose(2, 1, 0).reshape(ATT_HEADS, t)


def _per_head(rep_row):
    return rep_row[0, ::SSM_HEAD_DIM]


def _rep_heads(v):
    return jnp.repeat(v, SSM_HEAD_DIM)[None, :]


def _pad_lanes(v):
    return jnp.pad(v, ((0, 0), (0, LANES - v.shape[1])))


def _ffn_ple_fwd(x1, p_i, prm, i):
    h = _rmsnorm_fwd(x1, prm["norm_ffn"][i:i + 1], name=f"ffn_norm_fwd_{i}")
    g, u, act = _swiglu_fwd(h, prm["ffn_w_gate"][i], prm["ffn_w_up"][i], name=f"swiglu_fwd_{i}")
    x2 = _matmul(act, prm["ffn_w_down"][i], mode="nn", addend=x1, name=f"ffn_down_{i}")
    x3 = _ple_fwd(x2, p_i, prm["ple_w_gate"][i], prm["ple_w_proj"][i], name=f"ple_fwd_{i}")
    return x3, dict(x1=x1, h=h, g=g, u=u, act=act, x2=x2)


def _ffn_ple_bwd(dx3, p_i, prm, i, sv, grads):
    ds, dple = _ple_bwd(sv["x2"], p_i, prm["ple_w_gate"][i], prm["ple_w_proj"][i], dx3, name=f"ple_bwd_{i}")
    grads["ple_w_gate"][i] = _matmul_tn(sv["x2"], ds, name=f"d_ple_w_gate_{i}")
    grads["ple_w_proj"][i] = _matmul_tn(dple, p_i, name=f"d_ple_w_proj_{i}")
    dx2 = _matmul(ds, prm["ple_w_gate"][i], mode="nt", addend=dx3, name=f"ple_dx_{i}")
    grads["ffn_w_down"][i] = _matmul_tn(sv["act"], dx2, name=f"d_ffn_w_down_{i}")
    dg, du = _swiglu_bwd(dx2, prm["ffn_w_down"][i], sv["g"], sv["u"], name=f"swiglu_bwd_{i}")
    grads["ffn_w_gate"][i] = _matmul_tn(dg, sv["h"], name=f"d_ffn_w_gate_{i}")
    grads["ffn_w_up"][i] = _matmul_tn(du, sv["h"], name=f"d_ffn_w_up_{i}")
    dh = _matmul(dg, prm["ffn_w_gate"][i], mode="nn", name=f"ffn_dh_gate_{i}")
    dh = _matmul(du, prm["ffn_w_up"][i], mode="nn", addend=dh, name=f"ffn_dh_up_{i}")
    dx1, dgain = _rmsnorm_bwd(sv["x1"], prm["norm_ffn"][i:i + 1], dh, dx2, name=f"ffn_norm_bwd_{i}")
    grads["norm_ffn"][i] = dgain[0]
    return dx1


def _mamba_fwd(x0, prm):
    h = _rmsnorm_fwd(x0, prm["norm_mix"][0:1], name="mix_norm_fwd_0")
    z = _matmul(h, prm["ssm_w_z"], mode="nt", name="ssm_in_z")
    xbc_pre = _matmul(h, prm["ssm_w_xbc"], mode="nt", name="ssm_in_xbc")
    dt_raw = _matmul(h, prm["ssm_w_dt"], mode="nt", name="ssm_in_dt")
    xbc = _conv_fwd(xbc_pre, prm["ssm_conv_w"], prm["ssm_conv_b"])
    dt_bias = _pad_lanes(prm["ssm_dt_bias"])
    a_log = _pad_lanes(prm["ssm_a_log"])
    dt, acs = _ssd_prep_fwd(dt_raw, dt_bias, a_log)
    dt_rep = jnp.repeat(dt[:, :SSM_HEADS], SSM_HEAD_DIM, axis=1)
    acs_rep = jnp.repeat(acs[:, :SSM_HEADS], SSM_HEAD_DIM, axis=1)
    acs_t = acs[:, :SSM_HEADS].T
    dskip_rep = _rep_heads(prm["ssm_d_skip"][0])
    y, hin_all = _ssd_fwd(xbc, dt_rep, acs_rep, acs_t, dskip_rep)
    yn = _gate_norm_fwd(y, z, prm["ssm_norm_w"])
    x1 = _matmul(yn, prm["ssm_w_out"], mode="nn", addend=x0, name="ssm_out")
    sv = dict(x0=x0, h=h, z=z, xbc_pre=xbc_pre, dt_raw=dt_raw, xbc=xbc, dt_bias=dt_bias, dt_rep=dt_rep,
              acs_rep=acs_rep, acs_t=acs_t, dskip_rep=dskip_rep, y=y, hin_all=hin_all, yn=yn)
    return x1, sv


def _mamba_bwd(dx1, prm, sv, grads):
    grads["ssm_w_out"] = _matmul_tn(sv["yn"], dx1, name="d_ssm_w_out")
    dyn = _matmul(dx1, prm["ssm_w_out"], mode="nt", name="ssm_out_dx")
    dy, dz, dnw = _gate_norm_bwd(sv["y"], sv["z"], prm["ssm_norm_w"], dyn)
    grads["ssm_norm_w"] = dnw
    a_rep = _rep_heads(-jnp.exp(prm["ssm_a_log"][0]))
    dxbc, ddt_rep, da_rep, dds_rep = _ssd_bwd(sv["xbc"], sv["dt_rep"], sv["acs_rep"], sv["acs_t"], sv["dskip_rep"],
                                              a_rep, sv["hin_all"], dy)
    grads["ssm_d_skip"] = _per_head(dds_rep)[None, :]
    grads["ssm_a_log"] = (_per_head(da_rep) * _per_head(a_rep))[None, :]
    ddt = _pad_lanes(ddt_rep[:, ::SSM_HEAD_DIM])
    ddt_raw, dbias = _ssd_prep_bwd(sv["dt_raw"], sv["dt_bias"], ddt)
    grads["ssm_dt_bias"] = dbias[:, :SSM_HEADS]
    du, dcw, dcb = _conv_bwd(sv["xbc_pre"], prm["ssm_conv_w"], prm["ssm_conv_b"], dxbc)
    grads["ssm_conv_w"] = dcw
    grads["ssm_conv_b"] = dcb
    h = sv["h"]
    grads["ssm_w_in"] = jnp.concatenate(
        [_matmul_tn(dz, h, name="d_ssm_w_z"), _matmul_tn(du, h, name="d_ssm_w_xbc"),
         _matmul_tn(ddt_raw, h, name="d_ssm_w_dt")[:SSM_HEADS]], axis=0)
    dh = _matmul(dz, prm["ssm_w_z"], mode="nn", name="ssm_dh_z")
    dh = _matmul(du, prm["ssm_w_xbc"], mode="nn", addend=dh, name="ssm_dh_xbc")
    dh = _matmul(ddt_raw, prm["ssm_w_dt"], mode="nn", addend=dh, name="ssm_dh_dt")
    dx0, dgain = _rmsnorm_bwd(sv["x0"], prm["norm_mix"][0:1], dh, dx1, name="mix_norm_bwd_0")
    grads["norm_mix"][0] = dgain[0]
    return dx0


def _attn_mixer_fwd(x0, prm):
    h = _rmsnorm_fwd(x0, prm["norm_mix"][1:2], name="mix_norm_fwd_1")
    qkv = _matmul(h, prm["att_w_qkv"], mode="nt", name="att_qkv")
    gq = jnp.tile(prm["att_q_norm"], (1, ATT_HEADS))
    gk = jnp.tile(prm["att_k_norm"], (1, ATT_HEADS))
    qkvn = _qk_norm_fwd(qkv, gq, gk)
    outs, lses = [], []
    for g, (window, dil) in enumerate(DIL_PATTERNS):
        o_g, l_g = _attn_fwd(qkvn, g, dil)
        outs.append(o_g)
        lses.append(l_g)
    o_b, o_f, l_rep = _attn_combine_fwd(outs, lses)
    x1 = _matmul(o_b, prm["att_w_o"], mode="nn", addend=x0, name="att_out")
    sv = dict(x0=x0, h=h, qkv=qkv, gq=gq, gk=gk, qkvn=qkvn, o_b=o_b, o_f=o_f, l_rep=l_rep)
    return x1, sv


def _attn_mixer_bwd(dx1, prm, sv, grads):
    grads["att_w_o"] = _matmul_tn(sv["o_b"], dx1, name="d_att_w_o")
    do = _matmul(dx1, prm["att_w_o"], mode="nt", name="att_out_dx")
    dl_rep, do_b = _attn_combine_bwd(do, sv["o_f"])
    blocks = [None] * N_QKV_BLOCKS
    for g, (window, dil) in enumerate(DIL_PATTERNS):
        blocks[3 * g] = _attn_bwd_dq(sv["qkvn"], do_b, sv["l_rep"], dl_rep, g, dil)
        dk, dv = _attn_bwd_dkv(sv["qkvn"], do_b, _rows_by_residue(sv["l_rep"], dil),
                               _rows_by_residue(dl_rep, dil), g, dil)
        blocks[3 * g + 1] = dk
        blocks[3 * g + 2] = dv
    dqkv, dgq, dgk = _qk_norm_bwd(sv["qkv"], sv["gq"], sv["gk"], blocks)
    grads["att_q_norm"] = dgq.reshape(ATT_HEADS, ATT_HEAD_DIM).sum(axis=0)[None, :]
    grads["att_k_norm"] = dgk.reshape(ATT_HEADS, ATT_HEAD_DIM).sum(axis=0)[None, :]
    grads["att_w_qkv"] = _matmul_tn(dqkv, sv["h"], name="d_att_w_qkv")
    dh = _matmul(dqkv, prm["att_w_qkv"], mode="nn", name="att_qkv_dx")
    dx0, dgain = _rmsnorm_bwd(sv["x0"], prm["norm_mix"][1:2], dh, dx1, name="mix_norm_bwd_1")
    grads["norm_mix"][1] = dgain[0]
    return dx0


def _local_step(x, p, target, prm):
    grads = {k: [None, None] for k in ("norm_mix", "norm_ffn", "ffn_w_gate", "ffn_w_up", "ffn_w_down",
                                       "ple_w_proj", "ple_w_gate")}
    x1, sv_m = _mamba_fwd(x, prm)
    x3, sv_f0 = _ffn_ple_fwd(x1, p[0], prm, 0)
    x4, sv_a = _attn_mixer_fwd(x3, prm)
    x6, sv_f1 = _ffn_ple_fwd(x4, p[1], prm, 1)
    dy, loss_row = _loss_head(x6, target)
    dx4 = _ffn_ple_bwd(dy, p[1], prm, 1, sv_f1, grads)
    dx3 = _attn_mixer_bwd(dx4, prm, sv_a, grads)
    dx1 = _ffn_ple_bwd(dx3, p[0], prm, 0, sv_f0, grads)
    dx0 = _mamba_bwd(dx1, prm, sv_m, grads)
    return loss_row, dx0, grads


MESH = pl.DeviceIdType.MESH
ANY = pl.BlockSpec(memory_space=pl.ANY)
W_IN_SLAB_ROWS = 1312


def _position():
    return lax.axis_index("x"), lax.axis_index("y"), lax.axis_index("c")


def _other_chips(x, y):
    return [(1 - x, y), (x, 1 - y), (1 - x, 1 - y)]


def _gather_slabs(entries, conv_w):
    n = len(entries)

    def body(*refs):
        in_refs, conv_ref = refs[:n], refs[n]
        out_refs, conv_out = refs[n + 1:2 * n + 1], refs[2 * n + 1]
        send_sems, recv_sems = refs[2 * n + 2], refs[2 * n + 3]
        x, y, c = _position()
        me, sibling = (x, y, c), (x, y, 1 - c)
        chips = _other_chips(x, y)

        def copy(k, src, dst, to):
            return pltpu.make_async_remote_copy(src_ref=src, dst_ref=dst, send_sem=send_sems.at[k],
                                                recv_sem=recv_sems.at[k], device_id=to, device_id_type=MESH)

        started = []
        for j, chip in enumerate(chips):
            for e in range(n):
                started.append(copy(6 * e + j, in_refs[e].at[c], out_refs[e].at[2 * x + y, c], (*chip, c)))
                started[-1].start()
            started.append(copy(6 * n + j, conv_ref, conv_out.at[2 * x + y], (*chip, c)))
            started[-1].start()
        for j, (px, py) in enumerate(chips):
            for e in range(n):
                landed = out_refs[e].at[2 * px + py, c]
                copy(6 * e + j, landed, landed, me).wait_recv()
                started.append(copy(6 * e + 3 + j, landed, landed, sibling))
                started[-1].start()
            copy(6 * n + j, conv_ref, conv_out.at[2 * px + py], me).wait_recv()
        for j, (px, py) in enumerate(chips):
            for e in range(n):
                passed = out_refs[e].at[2 * px + py, 1 - c]
                copy(6 * e + 3 + j, passed, passed, me).wait_recv()
        for cp in started:
            cp.wait_send()

    outs = pl.pallas_call(
        body, name="gather_slabs", in_specs=[ANY] * (n + 1), out_specs=[ANY] * (n + 1),
        out_shape=[jax.ShapeDtypeStruct((N_CHIPS,) + e.shape, e.dtype) for e in entries]
        + [jax.ShapeDtypeStruct((N_CHIPS,) + conv_w.shape, conv_w.dtype)],
        scratch_shapes=[pltpu.SemaphoreType.DMA((6 * n + 3,)), pltpu.SemaphoreType.DMA((6 * n + 3,))],
    )(*entries, conv_w)
    return outs[:n], outs[n]


def _swap_halves(grads):
    n = len(grads)

    def body(*refs):
        g_refs, r_refs = refs[:n], refs[n:2 * n]
        send_sems, recv_sems = refs[2 * n], refs[2 * n + 1]
        x, y, c = _position()
        cps = [pltpu.make_async_remote_copy(src_ref=g_refs[e].at[:, 1 - c], dst_ref=r_refs[e],
                                            send_sem=send_sems.at[e], recv_sem=recv_sems.at[e],
                                            device_id=(x, y, 1 - c), device_id_type=MESH) for e in range(n)]
        for cp in cps:
            cp.start()
        for cp in cps:
            cp.wait()

    return pl.pallas_call(
        body, name="grad_swap_halves", in_specs=[ANY] * n, out_specs=[ANY] * n,
        out_shape=[jax.ShapeDtypeStruct((N_CHIPS,) + g.shape[2:], g.dtype) for g in grads],
        scratch_shapes=[pltpu.SemaphoreType.DMA((n,)), pltpu.SemaphoreType.DMA((n,))],
    )(*grads)


def _chip_exchange(chipsums):
    n = len(chipsums)

    def body(*refs):
        cs_refs, r_refs = refs[:n], refs[n:2 * n]
        send_sems, recv_sems = refs[2 * n], refs[2 * n + 1]
        x, y, c = _position()
        cps = []
        for j, (tx, ty) in enumerate(_other_chips(x, y)):
            for e in range(n):
                cps.append(pltpu.make_async_remote_copy(
                    src_ref=cs_refs[e].at[2 * tx + ty], dst_ref=r_refs[e].at[j], send_sem=send_sems.at[3 * e + j],
                    recv_sem=recv_sems.at[3 * e + j], device_id=(tx, ty, c), device_id_type=MESH))
                cps[-1].start()
        for cp in cps:
            cp.wait()

    return pl.pallas_call(
        body, name="grad_chip_exchange", in_specs=[ANY] * n, out_specs=[ANY] * n,
        out_shape=[jax.ShapeDtypeStruct((3,) + cs.shape[1:], cs.dtype) for cs in chipsums],
        scratch_shapes=[pltpu.SemaphoreType.DMA((3 * n,)), pltpu.SemaphoreType.DMA((3 * n,))],
    )(*chipsums)


def _share_halves(totals):
    n = len(totals)

    def body(*refs):
        t_refs, r_refs = refs[:n], refs[n:2 * n]
        send_sems, recv_sems = refs[2 * n], refs[2 * n + 1]
        x, y, c = _position()
        cps = [pltpu.make_async_remote_copy(src_ref=t_refs[e], dst_ref=r_refs[e], send_sem=send_sems.at[e],
                                            recv_sem=recv_sems.at[e], device_id=(x, y, 1 - c), device_id_type=MESH)
               for e in range(n)]
        for cp in cps:
            cp.start()
        for cp in cps:
            cp.wait()

    return pl.pallas_call(
        body, name="grad_share_halves", in_specs=[ANY] * n, out_specs=[ANY] * n,
        out_shape=[jax.ShapeDtypeStruct(t.shape, t.dtype) for t in totals],
        scratch_shapes=[pltpu.SemaphoreType.DMA((n,)), pltpu.SemaphoreType.DMA((n,))],
    )(*totals)


def _reduce_rows(h):
    return h if h <= 704 else h // 2


def _add_sibling(grad, recv, c_idx, *, name):
    _, _, h, cw = grad.shape
    th = _reduce_rows(h)

    def body(c_ref, g_ref, r_ref, o_ref):
        o_ref[...] = (g_ref[...] + r_ref[...]).astype(BF16)

    return pl.pallas_call(
        body, name=name,
        grid_spec=pltpu.PrefetchScalarGridSpec(
            num_scalar_prefetch=1, grid=(N_CHIPS, h // th),
            in_specs=[pl.BlockSpec((None, None, th, cw), lambda s, i, c_ref: (s, c_ref[0], i, 0)),
                      pl.BlockSpec((None, th, cw), lambda s, i, c_ref: (s, i, 0))],
            out_specs=pl.BlockSpec((None, th, cw), lambda s, i, c_ref: (s, i, 0))),
        out_shape=jax.ShapeDtypeStruct((N_CHIPS, h, cw), BF16),
        compiler_params=_params("parallel", "parallel"),
    )(c_idx, grad, recv)


def _add_chips(chipsum, recv, s_idx, *, name):
    _, h, cw = chipsum.shape
    th = _reduce_rows(h)

    def body(s_ref, own_ref, r_ref, o_ref):
        o_ref[...] = ((own_ref[...].astype(F32) + r_ref[0].astype(F32)) + r_ref[1].astype(F32)) + r_ref[2].astype(F32)

    return pl.pallas_call(
        body, name=name,
        grid_spec=pltpu.PrefetchScalarGridSpec(
            num_scalar_prefetch=1, grid=(h // th,),
            in_specs=[pl.BlockSpec((None, th, cw), lambda i, s_ref: (s_ref[0], i, 0)),
                      pl.BlockSpec((3, th, cw), lambda i, s_ref: (0, i, 0))],
            out_specs=pl.BlockSpec((th, cw), lambda i, s_ref: (i, 0))),
        out_shape=jax.ShapeDtypeStruct((h, cw), F32),
        compiler_params=_params("parallel"),
    )(s_idx, chipsum, recv)


def _adamw_math(w, g, m, v):
    m = ADAM_B1 * m + (1.0 - ADAM_B1) * g
    v = ADAM_B2 * v + (1.0 - ADAM_B2) * (g * g)
    m_hat = m / (1.0 - ADAM_B1 ** ADAM_STEP)
    v_hat = v / (1.0 - ADAM_B2 ** ADAM_STEP)
    delta = -ADAM_LR * (m_hat / (jnp.sqrt(v_hat) + ADAM_EPS) + ADAM_WD * w)
    return delta, m, v


ADAM_TILE_ELEMS = 256 * 1024


def _adamw(w, g, m, v, *, name):
    shape = w.shape
    cols = shape[-1]
    rows = w.size // cols
    tr = rows
    for cand in range(8, rows, 8):
        if rows % cand == 0 and cand * cols <= ADAM_TILE_ELEMS:
            tr = cand
    if rows * cols <= ADAM_TILE_ELEMS:
        tr = rows

    def body(w_ref, g_ref, m_ref, v_ref, d_ref, nm_ref, nv_ref):
        d, nm, nv = _adamw_math(w_ref[...], g_ref[...], m_ref[...], v_ref[...])
        d_ref[...] = d
        nm_ref[...] = nm
        nv_ref[...] = nv

    blk = pl.BlockSpec((tr, cols), lambda i: (i, 0))
    sds = jax.ShapeDtypeStruct((rows, cols), F32)
    outs = pl.pallas_call(
        body, name=name, grid=(rows // tr,), in_specs=[blk] * 4, out_specs=[blk] * 3, out_shape=[sds] * 3,
        compiler_params=_params("parallel"),
    )(*[a.reshape(rows, cols) for a in (w, g, m, v)])
    return [o.reshape(shape) for o in outs]


SMALL_LAYOUT = (("loss", 1), ("norm_mix", 16), ("norm_ffn", 16), ("ssm_conv_b", 24), ("ssm_dt_bias", 1),
                ("ssm_a_log", 1), ("ssm_d_skip", 1), ("ssm_norm_w", 16), ("att_q_norm", 1), ("att_k_norm", 1),
                ("conv_w_full", 96))
SMALL_ROWS = 176
N_DEVICES = 8


def _small_pack(values):
    parts = []
    for name, rows in SMALL_LAYOUT:
        flat = values[name].reshape(-1).astype(F32)
        parts.append(jnp.pad(flat, (0, rows * LANES - flat.shape[0])).reshape(rows, LANES))
    used = sum(r for _, r in SMALL_LAYOUT)
    parts.append(jnp.zeros((SMALL_ROWS - used, LANES), F32))
    return jnp.concatenate(parts, axis=0)


def _small_unpack(pack, shapes):
    out, off = {}, 0
    for name, rows in SMALL_LAYOUT:
        shape = shapes[name]
        n = math.prod(shape)
        out[name] = pack[off:off + rows].reshape(-1)[:n].reshape(shape)
        off += rows
    return out


def _small_allreduce_adamw(g, w, m, v):
    def body(g_ref, w_ref, m_ref, v_ref, gs_ref, d_ref, nm_ref, nv_ref, buf, send_sems, recv_sems):
        x, y, c = _position()
        pos = (x, y, c)
        me = 4 * x + 2 * y + c
        buf[me] = g_ref[...]
        peers = []
        for k in range(1, N_DEVICES):
            bits = ((k >> 2) & 1, (k >> 1) & 1, k & 1)
            peers.append(tuple(1 - p if b else p for p, b in zip(pos, bits)))
        cps = [pltpu.make_async_remote_copy(src_ref=g_ref, dst_ref=buf.at[me], send_sem=send_sems.at[k],
                                            recv_sem=recv_sems.at[k], device_id=peer, device_id_type=MESH)
               for k, peer in enumerate(peers)]
        for cp in cps:
            cp.start()
        for k, (px, py, pc) in enumerate(peers):
            pltpu.make_async_remote_copy(src_ref=g_ref, dst_ref=buf.at[4 * px + 2 * py + pc],
                                         send_sem=send_sems.at[k], recv_sem=recv_sems.at[k],
                                         device_id=(px, py, pc), device_id_type=MESH).wait_recv()
        for cp in cps:
            cp.wait_send()
        total = buf[0]
        for dev in range(1, N_DEVICES):
            total = total + buf[dev]
        gs_ref[...] = total
        d, nm, nv = _adamw_math(w_ref[...], total, m_ref[...], v_ref[...])
        d_ref[...] = d
        nm_ref[...] = nm
        nv_ref[...] = nv

    vm = pl.BlockSpec(memory_space=pltpu.VMEM)
    sds = jax.ShapeDtypeStruct((SMALL_ROWS, LANES), F32)
    return pl.pallas_call(
        body, name="small_allreduce_adamw", in_specs=[vm] * 4, out_specs=[vm] * 4, out_shape=[sds] * 4,
        scratch_shapes=[pltpu.VMEM((N_DEVICES, SMALL_ROWS, LANES), F32),
                        pltpu.SemaphoreType.DMA((N_DEVICES - 1,)), pltpu.SemaphoreType.DMA((N_DEVICES - 1,))],
    )(g, w, m, v)


SMALL = tuple(n for n, _ in SMALL_LAYOUT if n not in ("loss", "conv_w_full"))
WEIGHTS = ("norm_mix", "norm_ffn", "ssm_w_in", "ssm_conv_w", "ssm_conv_b", "ssm_dt_bias", "ssm_a_log", "ssm_d_skip",
           "ssm_norm_w", "ssm_w_out", "att_w_qkv", "att_q_norm", "att_k_norm", "att_w_o", "ffn_w_gate", "ffn_w_up",
           "ffn_w_down", "ple_w_proj", "ple_w_gate")
COLUMN_SHARDED = ("ssm_w_in", "att_w_qkv", "ffn_w_gate", "ffn_w_up", "ple_w_proj")
LAYERED = ("ffn_w_gate", "ffn_w_up", "ffn_w_down", "ple_w_proj", "ple_w_gate")
GATHER_ORDER = ("ssm_w_in", "ssm_w_out", "att_w_qkv", "att_w_o", "ffn_w_gate", "ffn_w_up", "ffn_w_down",
                "ple_w_proj", "ple_w_gate")


def _weight_slabs(w):
    slabs = []
    for n in GATHER_ORDER:
        a = w[n]
        if n in LAYERED:
            a = a.transpose(0, 2, 1) if n in COLUMN_SHARDED else a
        else:
            a = a[0].T if n in COLUMN_SHARDED else a[0]
            if n == "ssm_w_in":
                a = jnp.pad(a, ((0, W_IN_SLAB_ROWS - a.shape[0]), (0, 0)))
            a = a.reshape(2, a.shape[0] // 2, a.shape[1])
        slabs.append(a.astype(BF16))
    return slabs


def _full_weights(gathered, own, conv_all, conv_own, s_me, small):
    full = {}
    for n, g, o in zip(GATHER_ORDER, gathered, own):
        full[n] = lax.dynamic_update_slice(g, o[None], (s_me, 0, 0, 0))
    prm = dict(small)
    conv = lax.dynamic_update_slice(conv_all, conv_own[None], (s_me, 0, 0))
    prm["ssm_conv_w"] = conv.transpose(1, 0, 2).reshape(CONV_WIDTH, CONV_DIM)
    rows = (D_INNER + CONV_DIM + SSM_HEADS) // N_CHIPS
    w_in_t = full["ssm_w_in"].reshape(N_CHIPS, W_IN_SLAB_ROWS, D_MODEL)[:, :rows].reshape(N_CHIPS * rows, D_MODEL)
    prm["ssm_w_z"] = w_in_t[:D_INNER]
    prm["ssm_w_xbc"] = w_in_t[D_INNER:D_INNER + CONV_DIM]
    prm["ssm_w_dt"] = jnp.pad(w_in_t[D_INNER + CONV_DIM:], ((0, LANES - SSM_HEADS), (0, 0)))
    prm["ssm_w_out"] = full["ssm_w_out"].reshape(D_INNER, D_MODEL)
    prm["att_w_qkv"] = full["att_w_qkv"].reshape(N_QKV_BLOCKS * ATT_W, D_MODEL)
    prm["att_w_o"] = full["att_w_o"].reshape(ATT_W, D_MODEL)
    for n in LAYERED:
        g = full[n]
        prm[n] = [g[:, i].reshape(N_CHIPS * g.shape[2], g.shape[3]) for i in range(2)]
    return prm


def _grad_slabs(grads):
    out = []
    for n in GATHER_ORDER:
        for i in (range(2) if n in LAYERED else (None,)):
            g = grads[n] if i is None else grads[n][i]
            if n == "ssm_w_in":
                g = jnp.pad(g.reshape(N_CHIPS, g.shape[0] // N_CHIPS, D_MODEL),
                            ((0, 0), (0, W_IN_SLAB_ROWS - g.shape[0] // N_CHIPS), (0, 0)))
            rows = g.size // (N_CHIPS * g.shape[-1])
            out.append((n, i, g.reshape(N_CHIPS, 2, rows // 2, g.shape[-1])))
    return out


def _natural_shard(n, reduced, shape):
    def one(r):
        if n == "ssm_w_in":
            r = r[:shape[-1]]
        return r.T if n in COLUMN_SHARDED else r
    if n in LAYERED:
        return jnp.stack([one(r) for r in reduced]).reshape(shape)
    return one(reduced[0]).reshape(shape)


def kernel(x, p, norm_mix, norm_ffn, ssm_w_in, ssm_conv_w, ssm_conv_b, ssm_dt_bias, ssm_a_log, ssm_d_skip, ssm_norm_w, ssm_w_out, att_w_qkv, att_q_norm, att_k_norm, att_w_o, ffn_w_gate, ffn_w_up, ffn_w_down, ple_w_proj, ple_w_gate, loss_target, m_norm_mix, m_norm_ffn, m_ssm_w_in, m_ssm_conv_w, m_ssm_conv_b, m_ssm_dt_bias, m_ssm_a_log, m_ssm_d_skip, m_ssm_norm_w, m_ssm_w_out, m_att_w_qkv, m_att_q_norm, m_att_k_norm, m_att_w_o, m_ffn_w_gate, m_ffn_w_up, m_ffn_w_down, m_ple_w_proj, m_ple_w_gate, v_norm_mix, v_norm_ffn, v_ssm_w_in, v_ssm_conv_w, v_ssm_conv_b, v_ssm_dt_bias, v_ssm_a_log, v_ssm_d_skip, v_ssm_norm_w, v_ssm_w_out, v_att_w_qkv, v_att_q_norm, v_att_k_norm, v_att_w_o, v_ffn_w_gate, v_ffn_w_up, v_ffn_w_down, v_ple_w_proj, v_ple_w_gate):
    given = dict(locals())
    w = {n: given[n] for n in WEIGHTS}
    m = {n: given["m_" + n] for n in WEIGHTS}
    v = {n: given["v_" + n] for n in WEIGHTS}
    c_idx = lax.axis_index("c").astype(jnp.int32).reshape(1)
    s_idx = (2 * lax.axis_index("x") + lax.axis_index("y")).astype(jnp.int32).reshape(1)

    s_me = 2 * lax.axis_index("x") + lax.axis_index("y")
    first_core = lax.axis_index("c") == 0

    own = _weight_slabs(w)
    gathered, conv_all = _gather_slabs(own, ssm_conv_w[0])
    prm = _full_weights(gathered, own, conv_all, ssm_conv_w[0], s_me, {n: w[n] for n in SMALL})

    loss_row, dx, grads = _local_step(x[0], p[:, 0], loss_target[0], prm)

    slabs = _grad_slabs(grads)
    tags = [n if i is None else f"{n}_{i}" for n, i, _ in slabs]
    g4 = [g for _, _, g in slabs]
    from_sibling = _swap_halves(g4)
    chipsums = [_add_sibling(g, r, c_idx, name="add_sibling_" + t) for g, r, t in zip(g4, from_sibling, tags)]
    from_chips = _chip_exchange(chipsums)
    totals = [_add_chips(cs, r, s_idx, name="add_chips_" + t) for cs, r, t in zip(chipsums, from_chips, tags)]
    shared = _share_halves(totals)
    reduced = {}
    for (n, i, _), mine, theirs in zip(slabs, totals, shared):
        lo = jnp.where(first_core, mine, theirs)
        hi = jnp.where(first_core, theirs, mine)
        reduced.setdefault(n, []).append(jnp.concatenate([lo, hi], axis=0))

    grad, delta, new_m, new_v = {}, {}, {}, {}
    for n in GATHER_ORDER:
        grad[n] = _natural_shard(n, reduced[n], w[n].shape)
        delta[n], new_m[n], new_v[n] = _adamw(w[n], grad[n], m[n], v[n], name="adamw_" + n)

    small_g = {n: (jnp.stack(grads[n]) if isinstance(grads[n], list) else grads[n]) for n in SMALL}
    small_g["loss"] = loss_row
    small_g["conv_w_full"] = grads["ssm_conv_w"]
    zero = {"loss": jnp.zeros((1, LANES), F32), "conv_w_full": jnp.zeros((CONV_WIDTH, CONV_DIM), F32)}
    outs = _small_allreduce_adamw(_small_pack(small_g), _small_pack({**w, **zero}), _small_pack({**m, **zero}),
                                  _small_pack({**v, **zero}))
    shapes = {n: w[n].shape for n in SMALL}
    shapes["loss"] = (1, LANES)
    shapes["conv_w_full"] = (CONV_WIDTH, CONV_DIM)
    sg, sd, sm, sv = [_small_unpack(o, shapes) for o in outs]
    for n in SMALL:
        grad[n], delta[n], new_m[n], new_v[n] = sg[n], sd[n], sm[n], sv[n]
    loss = sg["loss"][0, 0]
    conv_cols = CONV_DIM // N_CHIPS
    grad["ssm_conv_w"] = lax.dynamic_slice(sg["conv_w_full"], (0, s_me * conv_cols), (CONV_WIDTH, conv_cols))[None]
    delta["ssm_conv_w"], new_m["ssm_conv_w"], new_v["ssm_conv_w"] = _adamw(
        ssm_conv_w, grad["ssm_conv_w"], m_ssm_conv_w, v_ssm_conv_w, name="adamw_ssm_conv_w")

    return (loss, dx[None], *[grad[n] for n in WEIGHTS], *[delta[n] for n in WEIGHTS],
            *[new_m[n] for n in WEIGHTS], *[new_v[n] for n in WEIGHTS])
```

```python
import functools
import math

import jax
import jax.numpy as jnp
from jax import lax
from jax.experimental import pallas as pl
from jax.experimental.pallas import tpu as pltpu

F32 = jnp.float32
BF16 = jnp.bfloat16
HIGHEST = lax.Precision.HIGHEST

NORM_EPS = 1e-6
ADAM_LR, ADAM_B1, ADAM_B2, ADAM_EPS, ADAM_WD, ADAM_STEP = 0.001, 0.9, 0.999, 1e-08, 0.01, 10

D_MODEL = 1024
D_INNER = 2048
SSM_HEADS = 32
SSM_HEAD_DIM = 64
SSM_GROUPS = 4
SSM_STATE = 128
SSD_CHUNK = 128
CONV_DIM = 3072
CONV_WIDTH = 4
ATT_HEADS = 16
ATT_HEAD_DIM = 64
DIL_PATTERNS = ((128, 1), (512, 4), (2048, 16))
ATT_BLOCK = 128
FFN_HIDDEN = 2816
PLE_DIM = 256

LANES = 128
V7X_VMEM_LIMIT = 56 * 1024 * 1024
NEG_BIG = -1e30

N_CHIPS = 4


def _params(*sem):
    return pltpu.CompilerParams(dimension_semantics=sem, vmem_limit_bytes=V7X_VMEM_LIMIT)


def _tile(n, pref):
    if n <= pref:
        return n
    best = None
    for t in range(LANES, pref + 1, LANES):
        if n % t == 0:
            best = t
    assert best is not None, (n, pref)
    return best


def _sigmoid(v):
    return 1.0 / (1.0 + jnp.exp(-v))


def _dot(a, b):
    return jnp.dot(a, b, preferred_element_type=F32)


def _dot_nt(a, b):
    return lax.dot_general(a, b, (((1,), (1,)), ((), ())), preferred_element_type=F32)


def _dot_tn(a, b):
    return lax.dot_general(a, b, (((0,), (0,)), ((), ())), preferred_element_type=F32)


def _head_block_diag():
    i = lax.broadcasted_iota(jnp.int32, (LANES, LANES), 0) // ATT_HEAD_DIM
    j = lax.broadcasted_iota(jnp.int32, (LANES, LANES), 1) // ATT_HEAD_DIM
    return (i == j).astype(BF16)


def _split_dot(ones, z):
    hi = z.astype(BF16)
    lo = (z - hi.astype(F32)).astype(BF16)
    return _dot(ones, hi) + _dot(ones, lo)


def _head_sums(z, bd):
    hi = z.astype(BF16)
    lo = (z - hi.astype(F32)).astype(BF16)
    parts = []
    for t in range(z.shape[1] // LANES):
        sl = slice(t * LANES, (t + 1) * LANES)
        parts.append(_dot(hi[:, sl], bd) + _dot(lo[:, sl], bd))
    return parts[0] if len(parts) == 1 else jnp.concatenate(parts, axis=1)


def _lane_lt64(rows):
    return lax.broadcasted_iota(jnp.int32, (rows, LANES), 1) < ATT_HEAD_DIM


def _matmul(a, b, *, mode, name, out_dtype=F32, addend=None, tm=1024, tn=512, tk_max=3072):
    m, k = a.shape
    if mode == "nn":
        k2, n = b.shape
    else:
        n, k2 = b.shape
    assert k == k2, (a.shape, b.shape, mode)
    tm, tn, tk = _tile(m, tm), _tile(n, tn), _tile(k, tk_max)
    nk = k // tk
    has_add = addend is not None

    def body(*refs):
        a_ref, b_ref = refs[0], refs[1]
        add_ref = refs[2] if has_add else None
        o_ref, acc_ref = refs[-2], refs[-1]
        kk = pl.program_id(2)
        av = a_ref[...].astype(BF16)
        bv = b_ref[...].astype(BF16)
        part = _dot(av, bv) if mode == "nn" else _dot_nt(av, bv)

        @pl.when(kk == 0)
        def _():
            acc_ref[...] = part

        @pl.when(kk > 0)
        def _():
            acc_ref[...] += part

        @pl.when(kk == nk - 1)
        def _():
            res = acc_ref[...]
            if has_add:
                res = res + add_ref[...]
            o_ref[...] = res.astype(out_dtype)

    a_spec = pl.BlockSpec((tm, tk), lambda i, j, kk: (i, kk))
    if mode == "nn":
        b_spec = pl.BlockSpec((tk, tn), lambda i, j, kk: (kk, j))
    else:
        b_spec = pl.BlockSpec((tn, tk), lambda i, j, kk: (j, kk))
    in_specs = [a_spec, b_spec]
    args = [a, b]
    if has_add:
        in_specs.append(pl.BlockSpec((tm, tn), lambda i, j, kk: (i, j)))
        args.append(addend)
    return pl.pallas_call(
        body, name=name, grid=(m // tm, n // tn, nk),
        in_specs=in_specs, out_specs=pl.BlockSpec((tm, tn), lambda i, j, kk: (i, j)),
        out_shape=jax.ShapeDtypeStruct((m, n), out_dtype),
        scratch_shapes=[pltpu.VMEM((tm, tn), F32)],
        compiler_params=_params("parallel", "parallel", "arbitrary"),
    )(*args)


def _matmul_tn(a, b, *, name, tm=1408, tn=512, tk=1024):
    t, m = a.shape
    t2, n = b.shape
    assert t == t2
    tm, tn, tk = _tile(m, tm), _tile(n, tn), _tile(t, tk)

    def body(a_ref, b_ref, o_ref):
        part = _dot_tn(a_ref[...].astype(BF16), b_ref[...].astype(BF16))

        @pl.when(pl.program_id(2) == 0)
        def _():
            o_ref[...] = part

        @pl.when(pl.program_id(2) > 0)
        def _():
            o_ref[...] += part

    return pl.pallas_call(
        body, name=name, grid=(m // tm, n // tn, t // tk),
        in_specs=[pl.BlockSpec((tk, tm), lambda i, j, kk: (kk, i)),
                  pl.BlockSpec((tk, tn), lambda i, j, kk: (kk, j))],
        out_specs=pl.BlockSpec((tm, tn), lambda i, j, kk: (i, j)),
        out_shape=jax.ShapeDtypeStruct((m, n), F32),
        compiler_params=_params("parallel", "parallel", "arbitrary"),
    )(a, b)


def _rmsnorm_fwd(x, gain, *, name):
    t, d = x.shape
    tm = _tile(t, 512)

    def body(x_ref, g_ref, o_ref):
        xv = x_ref[...]
        r = lax.rsqrt(jnp.mean(xv * xv, axis=-1, keepdims=True) + NORM_EPS)
        o_ref[...] = (xv * r * g_ref[...]).astype(BF16)

    return pl.pallas_call(
        body, name=name, grid=(t // tm,),
        in_specs=[pl.BlockSpec((tm, d), lambda i: (i, 0)), pl.BlockSpec((1, d), lambda i: (0, 0))],
        out_specs=pl.BlockSpec((tm, d), lambda i: (i, 0)),
        out_shape=jax.ShapeDtypeStruct((t, d), BF16),
        compiler_params=_params("parallel"),
    )(x, gain)


def _rmsnorm_bwd(x, gain, dy, dres, *, name):
    t, d = x.shape
    tm = _tile(t, 512)

    def body(x_ref, g_ref, dy_ref, dres_ref, dx_ref, dg_ref):
        xv = x_ref[...]
        r = lax.rsqrt(jnp.mean(xv * xv, axis=-1, keepdims=True) + NORM_EPS)
        xh = xv * r
        dyv = dy_ref[...]
        dxh = dyv * g_ref[...]
        mean = jnp.mean(dxh * xh, axis=-1, keepdims=True)
        dx_ref[...] = dres_ref[...] + r * (dxh - xh * mean)
        part = jnp.sum(dyv * xh, axis=0, keepdims=True)

        @pl.when(pl.program_id(0) == 0)
        def _():
            dg_ref[...] = part

        @pl.when(pl.program_id(0) > 0)
        def _():
            dg_ref[...] += part

    row = pl.BlockSpec((tm, d), lambda i: (i, 0))
    vec = pl.BlockSpec((1, d), lambda i: (0, 0))
    return pl.pallas_call(
        body, name=name, grid=(t // tm,),
        in_specs=[row, vec, row, row], out_specs=[row, vec],
        out_shape=[jax.ShapeDtypeStruct((t, d), F32), jax.ShapeDtypeStruct((1, d), F32)],
        compiler_params=_params("arbitrary"),
    )(x, gain, dy, dres)


def _loss_head(y, target):
    t, d = y.shape
    tm = _tile(t, 512)
    steps = t // tm

    def body(y_ref, t_ref, dy_ref, l_ref, acc_ref):
        e = y_ref[...] - t_ref[...]
        dy_ref[...] = e * (1.0 / d)
        part = jnp.sum(e * e, axis=0, keepdims=True)

        @pl.when(pl.program_id(0) == 0)
        def _():
            acc_ref[...] = part

        @pl.when(pl.program_id(0) > 0)
        def _():
            acc_ref[...] += part

        @pl.when(pl.program_id(0) == steps - 1)
        def _():
            l_ref[...] = jnp.full((1, LANES), (0.5 / d), F32) * jnp.sum(acc_ref[...])

    row = pl.BlockSpec((tm, d), lambda i: (i, 0))
    return pl.pallas_call(
        body, name="loss_head", grid=(steps,),
        in_specs=[row, row], out_specs=[row, pl.BlockSpec((1, LANES), lambda i: (0, 0))],
        out_shape=[jax.ShapeDtypeStruct((t, d), F32), jax.ShapeDtypeStruct((1, LANES), F32)],
        scratch_shapes=[pltpu.VMEM((1, d), F32)],
        compiler_params=_params("arbitrary"),
    )(y, target)


def _swiglu_fwd(h, w_gate_t, w_up_t, *, name):
    t, d = h.shape
    f = w_gate_t.shape[0]
    tm, tn = _tile(t, 1024), _tile(f, 256)

    def body(h_ref, wg_ref, wu_ref, g_ref, u_ref, a_ref):
        hv = h_ref[...]
        g = _dot_nt(hv, wg_ref[...])
        u = _dot_nt(hv, wu_ref[...])
        g_ref[...] = g.astype(BF16)
        u_ref[...] = u.astype(BF16)
        a_ref[...] = (g * _sigmoid(g) * u).astype(BF16)

    wspec = pl.BlockSpec((tn, d), lambda i, j: (j, 0))
    ospec = pl.BlockSpec((tm, tn), lambda i, j: (i, j))
    return pl.pallas_call(
        body, name=name, grid=(t // tm, f // tn),
        in_specs=[pl.BlockSpec((tm, d), lambda i, j: (i, 0)), wspec, wspec],
        out_specs=[ospec, ospec, ospec],
        out_shape=[jax.ShapeDtypeStruct((t, f), BF16), jax.ShapeDtypeStruct((t, f), BF16),
                   jax.ShapeDtypeStruct((t, f), BF16)],
        compiler_params=_params("parallel", "parallel"),
    )(h, w_gate_t, w_up_t)


def _swiglu_bwd(dx, w_down, g, u, *, name):
    t, d = dx.shape
    f = w_down.shape[0]
    tm, tn = _tile(t, 1024), _tile(f, 256)

    def body(dx_ref, wd_ref, g_ref, u_ref, dg_ref, du_ref):
        dact = _dot_nt(dx_ref[...].astype(BF16), wd_ref[...])
        gv, uv = g_ref[...].astype(F32), u_ref[...].astype(F32)
        sg = _sigmoid(gv)
        dg_ref[...] = (dact * uv * sg * (1.0 + gv * (1.0 - sg))).astype(BF16)
        du_ref[...] = (dact * gv * sg).astype(BF16)

    ospec = pl.BlockSpec((tm, tn), lambda i, j: (i, j))
    return pl.pallas_call(
        body, name=name, grid=(t // tm, f // tn),
        in_specs=[pl.BlockSpec((tm, d), lambda i, j: (i, 0)), pl.BlockSpec((tn, d), lambda i, j: (j, 0)),
                  ospec, ospec],
        out_specs=[ospec, ospec],
        out_shape=[jax.ShapeDtypeStruct((t, f), BF16), jax.ShapeDtypeStruct((t, f), BF16)],
        compiler_params=_params("parallel", "parallel"),
    )(dx, w_down, g, u)


def _ple_fwd(x, p, w_gate, w_proj_t, *, name):
    t, d = x.shape
    e = p.shape[1]
    tm, tn = _tile(t, 1024), _tile(d, 512)

    def body(xf_ref, xr_ref, p_ref, wg_ref, wp_ref, o_ref):
        s = _dot(xf_ref[...].astype(BF16), wg_ref[...])
        ple = _dot_nt(p_ref[...].astype(BF16), wp_ref[...])
        o_ref[...] = xr_ref[...] + _sigmoid(s) * ple

    return pl.pallas_call(
        body, name=name, grid=(t // tm, d // tn),
        in_specs=[pl.BlockSpec((tm, d), lambda i, j: (i, 0)), pl.BlockSpec((tm, tn), lambda i, j: (i, j)),
                  pl.BlockSpec((tm, e), lambda i, j: (i, 0)), pl.BlockSpec((d, tn), lambda i, j: (0, j)),
                  pl.BlockSpec((tn, e), lambda i, j: (j, 0))],
        out_specs=pl.BlockSpec((tm, tn), lambda i, j: (i, j)),
        out_shape=jax.ShapeDtypeStruct((t, d), F32),
        compiler_params=_params("parallel", "parallel"),
    )(x, x, p, w_gate, w_proj_t)


def _ple_bwd(x, p, w_gate, w_proj_t, dout, *, name):
    t, d = x.shape
    e = p.shape[1]
    tm, tn = _tile(t, 1024), _tile(d, 512)

    def body(xf_ref, p_ref, wg_ref, wp_ref, do_ref, ds_ref, dple_ref):
        s = _dot(xf_ref[...].astype(BF16), wg_ref[...])
        ple = _dot_nt(p_ref[...].astype(BF16), wp_ref[...])
        gate = _sigmoid(s)
        dov = do_ref[...]
        dple_ref[...] = (dov * gate).astype(BF16)
        ds_ref[...] = (dov * ple * gate * (1.0 - gate)).astype(BF16)

    ospec = pl.BlockSpec((tm, tn), lambda i, j: (i, j))
    return pl.pallas_call(
        body, name=name, grid=(t // tm, d // tn),
        in_specs=[pl.BlockSpec((tm, d), lambda i, j: (i, 0)), pl.BlockSpec((tm, e), lambda i, j: (i, 0)),
                  pl.BlockSpec((d, tn), lambda i, j: (0, j)), pl.BlockSpec((tn, e), lambda i, j: (j, 0)), ospec],
        out_specs=[ospec, ospec],
        out_shape=[jax.ShapeDtypeStruct((t, d), BF16), jax.ShapeDtypeStruct((t, d), BF16)],
        compiler_params=_params("parallel", "parallel"),
    )(x, p, w_gate, w_proj_t, dout)


CONV_TIME_TILE = 256
CONV_HALO = 8


def _conv_taps(ext, w):
    acc = ext[CONV_HALO:, :] * w[CONV_WIDTH - 1:CONV_WIDTH, :]
    shifted = [ext[CONV_HALO:, :]]
    for j in range(1, CONV_WIDTH):
        sh = pltpu.roll(ext, j, 0)[CONV_HALO:, :]
        shifted.append(sh)
        acc = acc + sh * w[CONV_WIDTH - 1 - j:CONV_WIDTH - j, :]
    return acc, shifted


def _conv_fwd(u, w, b):
    t, c = u.shape
    tc = _tile(c, 256)
    tt = CONV_TIME_TILE

    def body(u_ref, w_ref, b_ref, o_ref):
        wv, bv = w_ref[...], b_ref[...]

        def tile(start, ext):
            pre = _conv_taps(ext, wv)[0] + bv
            o_ref[pl.ds(start, tt), :] = pre * _sigmoid(pre)

        tile(0, jnp.concatenate([jnp.zeros((CONV_HALO, tc), F32), u_ref[0:tt, :]], axis=0))

        def loop(i, carry):
            start = pl.multiple_of(i * tt, tt)
            tile(start, u_ref[pl.ds(start - CONV_HALO, tt + CONV_HALO), :])
            return carry

        lax.fori_loop(1, t // tt, loop, 0)

    col = pl.BlockSpec((t, tc), lambda j: (0, j))
    return pl.pallas_call(
        body, name="conv_fwd", grid=(c // tc,),
        in_specs=[col, pl.BlockSpec((CONV_WIDTH, tc), lambda j: (0, j)), pl.BlockSpec((1, tc), lambda j: (0, j))],
        out_specs=col, out_shape=jax.ShapeDtypeStruct((t, c), F32),
        compiler_params=_params("parallel"),
    )(u, w, b)


def _conv_bwd(u, w, b, dact):
    t, c = u.shape
    tc = _tile(c, 256)
    tt = CONV_TIME_TILE

    def body(u_ref, w_ref, b_ref, da_ref, du_ref, dw_ref, db_ref, dpre_ref):
        wv, bv = w_ref[...], b_ref[...]

        def tile(start, ext, sums):
            acc, shifted = _conv_taps(ext, wv)
            pre = acc + bv
            sg = _sigmoid(pre)
            dpre = da_ref[pl.ds(start, tt), :] * (sg * (1.0 + pre * (1.0 - sg)))
            dpre_ref[pl.ds(start, tt), :] = dpre
            new = [sums[0] + jnp.sum(dpre, axis=0, keepdims=True)]
            for j in range(CONV_WIDTH):
                new.append(sums[1 + j] + jnp.sum(dpre * shifted[j], axis=0, keepdims=True))
            return tuple(new)

        zero = jnp.zeros((1, tc), F32)
        sums = tile(0, jnp.concatenate([jnp.zeros((CONV_HALO, tc), F32), u_ref[0:tt, :]], axis=0),
                    (zero,) * (1 + CONV_WIDTH))

        def loop(i, sums):
            start = pl.multiple_of(i * tt, tt)
            return tile(start, u_ref[pl.ds(start - CONV_HALO, tt + CONV_HALO), :], sums)

        sums = lax.fori_loop(1, t // tt, loop, sums)
        db_ref[...] = sums[0]
        dw_ref[...] = jnp.concatenate([sums[1 + (CONV_WIDTH - 1 - k)] for k in range(CONV_WIDTH)], axis=0)
        dpre_ref[pl.ds(t, CONV_HALO), :] = jnp.zeros((CONV_HALO, tc), F32)

        def loop2(i, carry):
            start = pl.multiple_of(i * tt, tt)
            ext = dpre_ref[pl.ds(start, tt + CONV_HALO), :]
            acc = ext[0:tt, :] * wv[CONV_WIDTH - 1:CONV_WIDTH, :]
            for j in range(1, CONV_WIDTH):
                acc = acc + pltpu.roll(ext, tt + CONV_HALO - j, 0)[0:tt, :] * wv[CONV_WIDTH - 1 - j:CONV_WIDTH - j, :]
            du_ref[pl.ds(start, tt), :] = acc.astype(BF16)
            return carry

        lax.fori_loop(0, t // tt, loop2, 0)

    col = pl.BlockSpec((t, tc), lambda j: (0, j))
    return pl.pallas_call(
        body, name="conv_bwd", grid=(c // tc,),
        in_specs=[col, pl.BlockSpec((CONV_WIDTH, tc), lambda j: (0, j)), pl.BlockSpec((1, tc), lambda j: (0, j)), col],
        out_specs=[col, pl.BlockSpec((CONV_WIDTH, tc), lambda j: (0, j)), pl.BlockSpec((1, tc), lambda j: (0, j))],
        out_shape=[jax.ShapeDtypeStruct((t, c), BF16), jax.ShapeDtypeStruct((CONV_WIDTH, c), F32),
                   jax.ShapeDtypeStruct((1, c), F32)],
        scratch_shapes=[pltpu.VMEM((t + CONV_HALO, tc), F32)],
        compiler_params=_params("parallel"),
    )(u, w, b, dact)


def _softplus(v):
    e = jnp.exp(-jnp.abs(v))
    w = 1.0 + e
    log1p = jnp.where(w == 1.0, e, jnp.log(w) * (e / jnp.where(w == 1.0, 1.0, w - 1.0)))
    return jnp.maximum(v, 0.0) + log1p


def _ssd_prep_fwd(dt_raw, dt_bias, a_log):
    t = dt_raw.shape[0]
    cl = SSD_CHUNK

    def body(r_ref, b_ref, al_ref, dt_ref, acs_ref):
        dt = _softplus(r_ref[...] + b_ref[...])
        adt = dt * (-jnp.exp(al_ref[...]))
        li = lax.broadcasted_iota(jnp.int32, (cl, cl), 0)
        si = lax.broadcasted_iota(jnp.int32, (cl, cl), 1)
        tri = (si <= li).astype(F32)
        dt_ref[...] = dt
        acs_ref[...] = jnp.dot(tri, adt, preferred_element_type=F32, precision=HIGHEST)

    row = pl.BlockSpec((cl, LANES), lambda i: (i, 0))
    vec = pl.BlockSpec((1, LANES), lambda i: (0, 0))
    return pl.pallas_call(
        body, name="ssd_prep_fwd", grid=(t // cl,),
        in_specs=[row, vec, vec], out_specs=[row, row],
        out_shape=[jax.ShapeDtypeStruct((t, LANES), F32), jax.ShapeDtypeStruct((t, LANES), F32)],
        compiler_params=_params("parallel"),
    )(dt_raw, dt_bias, a_log)


def _ssd_prep_bwd(dt_raw, dt_bias, ddt):
    t = dt_raw.shape[0]
    tm = _tile(t, 512)

    def body(r_ref, b_ref, d_ref, o_ref, db_ref):
        g = d_ref[...] * _sigmoid(r_ref[...] + b_ref[...])
        o_ref[...] = g.astype(BF16)
        part = jnp.sum(g, axis=0, keepdims=True)

        @pl.when(pl.program_id(0) == 0)
        def _():
            db_ref[...] = part

        @pl.when(pl.program_id(0) > 0)
        def _():
            db_ref[...] += part

    row = pl.BlockSpec((tm, LANES), lambda i: (i, 0))
    vec = pl.BlockSpec((1, LANES), lambda i: (0, 0))
    return pl.pallas_call(
        body, name="ssd_prep_bwd", grid=(t // tm,),
        in_specs=[row, vec, row], out_specs=[row, vec],
        out_shape=[jax.ShapeDtypeStruct((t, LANES), BF16), jax.ShapeDtypeStruct((1, LANES), F32)],
        compiler_params=_params("arbitrary"),
    )(dt_raw, dt_bias, ddt)


GROUP_W = D_INNER // SSM_GROUPS
PAIRS_PER_GROUP = GROUP_W // LANES


def _head_cols(acs_pair, lt64):
    rolled = pltpu.roll(acs_pair, ATT_HEAD_DIM, 1)
    return jnp.where(lt64, acs_pair, rolled), jnp.where(lt64, rolled, acs_pair)


def _ssd_fwd(xbc, dt_rep, acs_rep, acs_t, dskip_rep):
    t = xbc.shape[0]
    cl = SSD_CHUNK
    nc = t // cl

    def body(xbc_ref, dt_ref, acs_ref, acst_ref, dskip_ref, y_ref, hin_ref, state_ref):
        @pl.when(pl.program_id(0) == 0)
        def _():
            state_ref[...] = jnp.zeros_like(state_ref)

        lt64 = _lane_lt64(cl)
        li = lax.broadcasted_iota(jnp.int32, (cl, cl), 0)
        si = lax.broadcasted_iota(jnp.int32, (cl, cl), 1)
        causal = li >= si
        hin_ref[...] = state_ref[...]
        for g in range(SSM_GROUPS):
            gsl = slice(g * GROUP_W, (g + 1) * GROUP_W)
            xg = xbc_ref[:, gsl]
            bg = xbc_ref[:, D_INNER + g * SSM_STATE:D_INNER + (g + 1) * SSM_STATE]
            cg = xbc_ref[:, D_INNER + SSM_GROUPS * SSM_STATE + g * SSM_STATE:
                         D_INNER + SSM_GROUPS * SSM_STATE + (g + 1) * SSM_STATE]
            acs = acs_ref[:, gsl]
            xdt = xg * dt_ref[:, gsl]
            atot = acs[cl - 1:cl, :]
            hin = state_ref[:, gsl]
            cgb = cg.astype(BF16)
            gmat = _dot_nt(cgb, bg.astype(BF16))
            yoff = _dot(cgb, hin.astype(BF16)) * jnp.exp(acs)
            snew = _dot(bg.T.astype(BF16), (xdt * jnp.exp(atot - acs)).astype(BF16))
            state_ref[:, gsl] = hin * jnp.exp(atot) + snew
            xdtb = xdt.astype(BF16)
            for pr in range(PAIRS_PER_GROUP):
                psl = slice(pr * LANES, (pr + 1) * LANES)
                cols = _head_cols(acs[:, psl], lt64)
                xp = xdtb[:, psl]
                ys = []
                for hh in range(2):
                    h = (g * PAIRS_PER_GROUP + pr) * 2 + hh
                    seg = cols[hh] - acst_ref[h:h + 1, :]
                    lm = jnp.exp(jnp.where(causal, seg, NEG_BIG))
                    ys.append(_dot((gmat * lm).astype(BF16), xp))
                ydiag = jnp.where(lt64, ys[0], ys[1])
                osl = slice(g * GROUP_W + pr * LANES, g * GROUP_W + (pr + 1) * LANES)
                y_ref[:, osl] = ydiag + yoff[:, psl] + xg[:, psl] * dskip_ref[:, osl]

    row = lambda w: pl.BlockSpec((cl, w), lambda c: (c, 0))
    return pl.pallas_call(
        body, name="ssd_fwd", grid=(nc,),
        in_specs=[row(CONV_DIM), row(D_INNER), row(D_INNER),
                  pl.BlockSpec((SSM_HEADS, cl), lambda c: (0, c)), pl.BlockSpec((1, D_INNER), lambda c: (0, 0))],
        out_specs=[row(D_INNER), pl.BlockSpec((None, SSM_STATE, D_INNER), lambda c: (c, 0, 0))],
        out_shape=[jax.ShapeDtypeStruct((t, D_INNER), F32), jax.ShapeDtypeStruct((nc, SSM_STATE, D_INNER), F32)],
        scratch_shapes=[pltpu.VMEM((SSM_STATE, D_INNER), F32)],
        compiler_params=_params("arbitrary"),
    )(xbc, dt_rep, acs_rep, acs_t, dskip_rep)


def _ssd_bwd(xbc, dt_rep, acs_rep, acs_t, dskip_rep, a_rep, hin_all, dy):
    t = xbc.shape[0]
    cl = SSD_CHUNK
    nc = t // cl

    def body(xbc_ref, dt_ref, acs_ref, acst_ref, dskip_ref, a_ref, hin_ref, dy_ref,
             dxbc_ref, ddt_ref, da_ref, dds_ref, dstate_ref, dacs_ref, dxs_ref):
        step = pl.program_id(0)

        @pl.when(step == 0)
        def _():
            dstate_ref[...] = jnp.zeros_like(dstate_ref)
            da_ref[...] = jnp.zeros_like(da_ref)
            dds_ref[...] = jnp.zeros_like(dds_ref)

        bd = _head_block_diag()
        lt64 = _lane_lt64(cl)
        li = lax.broadcasted_iota(jnp.int32, (cl, cl), 0)
        si = lax.broadcasted_iota(jnp.int32, (cl, cl), 1)
        lower = li >= si
        upper = si >= li
        last_row = lax.broadcasted_iota(jnp.int32, (cl, GROUP_W), 0) == cl - 1
        for g in range(SSM_GROUPS):
            gsl = slice(g * GROUP_W, (g + 1) * GROUP_W)
            bsl = slice(D_INNER + g * SSM_STATE, D_INNER + (g + 1) * SSM_STATE)
            csl = slice(D_INNER + SSM_GROUPS * SSM_STATE + g * SSM_STATE,
                        D_INNER + SSM_GROUPS * SSM_STATE + (g + 1) * SSM_STATE)
            xg = xbc_ref[:, gsl]
            bg = xbc_ref[:, bsl]
            cg = xbc_ref[:, csl]
            bgb, cgb = bg.astype(BF16), cg.astype(BF16)
            acs = acs_ref[:, gsl]
            xdt = xg * dt_ref[:, gsl]
            atot = acs[cl - 1:cl, :]
            eg = jnp.exp(acs)
            dk = jnp.exp(atot - acs)
            etot = jnp.exp(atot)
            hin = hin_ref[:, gsl]
            hinb = hin.astype(BF16)
            dh = dstate_ref[:, gsl]
            dhb = dh.astype(BF16)
            dyg = dy_ref[:, gsl]

            gmat = _dot_nt(cgb, bgb)
            gmat_t = _dot_nt(bgb, cgb)
            ch = _dot(cgb, hinb)
            dacs = _head_sums(dyg * ch * eg, bd)
            dye = (dyg * eg).astype(BF16)
            dc = _dot_nt(dye, hinb)
            dhin = _dot(cg.T.astype(BF16), dye)
            bdh = _dot(bgb, dhb)
            dxs = bdh * dk
            xdk = xdt * dk
            db = _dot_nt(xdk.astype(BF16), dhb)
            ddk = _head_sums(bdh * xdk, bd)
            dacs = dacs - ddk
            datot = jnp.sum(ddk, axis=0, keepdims=True) + etot * _head_sums(
                jnp.sum(dh * hin, axis=0, keepdims=True), bd)
            dacs = dacs + jnp.where(last_row, datot, 0.0)
            dstate_ref[:, gsl] = dh * etot + dhin

            xdtb = xdt.astype(BF16)
            dgsum = jnp.zeros((cl, cl), F32)
            dgsum_t = jnp.zeros((cl, cl), F32)
            for pr in range(PAIRS_PER_GROUP):
                psl = slice(pr * LANES, (pr + 1) * LANES)
                cols = _head_cols(acs[:, psl], lt64)
                xp = xdtb[:, psl]
                dyp = dyg[:, psl].astype(BF16)
                dx1, dac = [], []
                for hh in range(2):
                    h = (g * PAIRS_PER_GROUP + pr) * 2 + hh
                    mine = lt64 if hh == 0 else jnp.logical_not(lt64)
                    row = acst_ref[h:h + 1, :]
                    lm = jnp.exp(jnp.where(lower, cols[hh] - row, NEG_BIG))
                    lm_t = jnp.exp(jnp.where(upper, row - cols[hh], NEG_BIG))
                    dyh = jnp.where(mine, dyp, jnp.zeros_like(dyp))
                    xh = jnp.where(mine, xp, jnp.zeros_like(xp))
                    dm = _dot_nt(dyh, xp)
                    dm_t = _dot_nt(xh, dyp)
                    m_t = gmat_t * lm_t
                    dx1.append(_dot(m_t.astype(BF16), dyp))
                    w = dm * (gmat * lm)
                    w_t = dm_t * m_t
                    dac.append(jnp.sum(w, axis=1, keepdims=True) - jnp.sum(w_t, axis=1, keepdims=True))
                    dgsum = dgsum + dm * lm
                    dgsum_t = dgsum_t + dm_t * lm_t
                osl = slice(g * GROUP_W + pr * LANES, g * GROUP_W + (pr + 1) * LANES)
                dxs_ref[:, osl] = dxs[:, psl] + jnp.where(lt64, dx1[0], dx1[1])
                dacs_ref[:, osl] = dacs[:, psl] + jnp.where(lt64, jnp.broadcast_to(dac[0], (cl, LANES)),
                                                             jnp.broadcast_to(dac[1], (cl, LANES)))
            dxbc_ref[:, csl] = dc + _dot(dgsum.astype(BF16), bgb)
            dxbc_ref[:, bsl] = db + _dot(dgsum_t.astype(BF16), cgb)

        dadt = _split_dot(upper.astype(BF16), dacs_ref[...])
        xall = xbc_ref[:, 0:D_INNER]
        dtall = dt_ref[...]
        dxsall = dxs_ref[...]
        dyall = dy_ref[...]
        ddt_ref[...] = dadt * a_ref[...] + _head_sums(dxsall * xall, bd)
        dxbc_ref[:, 0:D_INNER] = dxsall * dtall + dyall * dskip_ref[...]
        da_ref[...] += jnp.sum(dadt * dtall, axis=0, keepdims=True)
        dds_ref[...] += jnp.sum(dyall * xall, axis=0, keepdims=True)

        @pl.when(step == nc - 1)
        def _():
            dds_ref[...] = _head_sums(dds_ref[...], bd)

    row = lambda w: pl.BlockSpec((cl, w), lambda c: (nc - 1 - c, 0))
    vec = pl.BlockSpec((1, D_INNER), lambda c: (0, 0))
    return pl.pallas_call(
        body, name="ssd_bwd", grid=(nc,),
        in_specs=[row(CONV_DIM), row(D_INNER), row(D_INNER),
                  pl.BlockSpec((SSM_HEADS, cl), lambda c: (0, nc - 1 - c)), vec, vec,
                  pl.BlockSpec((None, SSM_STATE, D_INNER), lambda c: (nc - 1 - c, 0, 0)), row(D_INNER)],
        out_specs=[row(CONV_DIM), row(D_INNER), vec, vec],
        out_shape=[jax.ShapeDtypeStruct((t, CONV_DIM), F32), jax.ShapeDtypeStruct((t, D_INNER), F32),
                   jax.ShapeDtypeStruct((1, D_INNER), F32), jax.ShapeDtypeStruct((1, D_INNER), F32)],
        scratch_shapes=[pltpu.VMEM((SSM_STATE, D_INNER), F32), pltpu.VMEM((cl, D_INNER), F32),
                        pltpu.VMEM((cl, D_INNER), F32)],
        compiler_params=_params("arbitrary"),
    )(xbc, dt_rep, acs_rep, acs_t, dskip_rep, a_rep, hin_all, dy)


def _gate_norm_fwd(y, z, w):
    t, c = y.shape
    tm = _tile(t, 256)

    def body(y_ref, z_ref, w_ref, o_ref):
        for g in range(SSM_GROUPS):
            gsl = slice(g * GROUP_W, (g + 1) * GROUP_W)
            zv = z_ref[:, gsl]
            v = y_ref[:, gsl] * (zv * _sigmoid(zv))
            r = lax.rsqrt(jnp.mean(v * v, axis=-1, keepdims=True) + NORM_EPS)
            o_ref[:, gsl] = (v * r * w_ref[:, gsl]).astype(BF16)

    row = pl.BlockSpec((tm, c), lambda i: (i, 0))
    return pl.pallas_call(
        body, name="gate_norm_fwd", grid=(t // tm,),
        in_specs=[row, row, pl.BlockSpec((1, c), lambda i: (0, 0))], out_specs=row,
        out_shape=jax.ShapeDtypeStruct((t, c), BF16),
        compiler_params=_params("parallel"),
    )(y, z, w)


def _gate_norm_bwd(y, z, w, dout):
    t, c = y.shape
    tm = _tile(t, 256)

    def body(y_ref, z_ref, w_ref, do_ref, dy_ref, dz_ref, dw_ref):
        @pl.when(pl.program_id(0) == 0)
        def _():
            dw_ref[...] = jnp.zeros_like(dw_ref)

        for g in range(SSM_GROUPS):
            gsl = slice(g * GROUP_W, (g + 1) * GROUP_W)
            zv, yv, dov = z_ref[:, gsl], y_ref[:, gsl], do_ref[:, gsl]
            sg = _sigmoid(zv)
            sz = zv * sg
            v = yv * sz
            r = lax.rsqrt(jnp.mean(v * v, axis=-1, keepdims=True) + NORM_EPS)
            vh = v * r
            dvh = dov * w_ref[:, gsl]
            mean = jnp.mean(dvh * vh, axis=-1, keepdims=True)
            dv = r * (dvh - vh * mean)
            dy_ref[:, gsl] = dv * sz
            dz_ref[:, gsl] = (dv * yv * (sg * (1.0 + zv * (1.0 - sg)))).astype(BF16)
            dw_ref[:, gsl] += jnp.sum(dov * vh, axis=0, keepdims=True)

    row = pl.BlockSpec((tm, c), lambda i: (i, 0))
    vec = pl.BlockSpec((1, c), lambda i: (0, 0))
    return pl.pallas_call(
        body, name="gate_norm_bwd", grid=(t // tm,),
        in_specs=[row, row, vec, row], out_specs=[row, row, vec],
        out_shape=[jax.ShapeDtypeStruct((t, c), F32), jax.ShapeDtypeStruct((t, c), BF16),
                   jax.ShapeDtypeStruct((1, c), F32)],
        compiler_params=_params("arbitrary"),
    )(y, z, w, dout)


ATT_W = ATT_HEADS * ATT_HEAD_DIM
N_QKV_BLOCKS = 9
ATT_SCALE = 1.0 / math.sqrt(ATT_HEAD_DIM)


def _head_rmsnorm(x, gain, bd):
    ms = _head_sums(x * x, bd) * (1.0 / ATT_HEAD_DIM)
    return x * lax.rsqrt(ms + NORM_EPS) * gain


def _class_rows(ref, blk, r, dil):
    span = ATT_BLOCK * dil
    sub = ref.at[pl.ds(pl.multiple_of(blk * span, span), span), :]
    return sub[...] if dil == 1 else sub[pl.ds(r, ATT_BLOCK, stride=dil), :]


def _store_class_rows(ref, blk, r, dil, val):
    span = ATT_BLOCK * dil
    sub = ref.at[pl.ds(pl.multiple_of(blk * span, span), span), :]
    if dil == 1:
        sub[...] = val
    else:
        sub[pl.ds(r, ATT_BLOCK, stride=dil), :] = val


def _qk_norm_bwd(qkv, gq, gk, grads):
    t = qkv.shape[0]
    tm = _tile(t, 256)

    def body(x_ref, gq_ref, gk_ref, *rest):
        g_refs = rest[:N_QKV_BLOCKS]
        o_ref, dgq_ref, dgk_ref = rest[N_QKV_BLOCKS:]
        cb = pl.program_id(1)

        @pl.when(jnp.logical_and(pl.program_id(0) == 0, cb == 0))
        def _():
            dgq_ref[...] = jnp.zeros_like(dgq_ref)
            dgk_ref[...] = jnp.zeros_like(dgk_ref)

        def norm_bwd(dy, gain, dg_ref):
            bd = _head_block_diag()
            xv = x_ref[...]
            ms = _head_sums(xv * xv, bd) * (1.0 / ATT_HEAD_DIM)
            r = lax.rsqrt(ms + NORM_EPS)
            xh = xv * r
            dxh = dy * gain
            mean = _head_sums(dxh * xh, bd) * (1.0 / ATT_HEAD_DIM)
            o_ref[...] = (r * (dxh - xh * mean)).astype(BF16)
            dg_ref[...] += jnp.sum(dy * xh, axis=0, keepdims=True)

        for k in range(N_QKV_BLOCKS):
            @pl.when(cb == k)
            def _(k=k):
                if k % 3 == 0:
                    norm_bwd(g_refs[k][...], gq_ref[...], dgq_ref)
                elif k % 3 == 1:
                    norm_bwd(g_refs[k][...], gk_ref[...], dgk_ref)
                else:
                    o_ref[...] = g_refs[k][...].astype(BF16)

    blk = pl.BlockSpec((tm, ATT_W), lambda i, j: (i, j))
    one = pl.BlockSpec((tm, ATT_W), lambda i, j: (i, 0))
    vec = pl.BlockSpec((1, ATT_W), lambda i, j: (0, 0))
    return pl.pallas_call(
        body, name="qk_norm_bwd", grid=(t // tm, N_QKV_BLOCKS),
        in_specs=[blk, vec, vec] + [one] * N_QKV_BLOCKS, out_specs=[blk, vec, vec],
        out_shape=[jax.ShapeDtypeStruct(qkv.shape, BF16), jax.ShapeDtypeStruct((1, ATT_W), F32),
                   jax.ShapeDtypeStruct((1, ATT_W), F32)],
        compiler_params=_params("arbitrary", "arbitrary"),
    )(qkv, gq, gk, *grads)


def _attn_logits(qm, kcat, slope, dist_bias, valid):
    s = _dot_nt(qm, kcat) * ATT_SCALE - slope * dist_bias
    return jnp.where(valid, s, NEG_BIG)


PAIRS = ATT_HEADS // 2


def _pair_col(g, j):
    return lambda pair: (0, (g * 3 + j) * PAIRS + pair)


def _pair_slopes(pair):
    steps = jnp.full((1, 2 * ATT_BLOCK), 2 * pair + 1, jnp.int32).astype(F32)
    first = jnp.exp(steps * (-0.5 * math.log(2.0)))
    return first, first * (2.0 ** -0.5)


def _band(n, nb, dil, transposed):
    bq = ATT_BLOCK
    a = lax.broadcasted_iota(jnp.int32, (bq, 2 * bq), 0)
    b = lax.broadcasted_iota(jnp.int32, (bq, 2 * bq), 1)
    if transposed:
        dist = b - a
        valid = (dist >= 0) & (dist <= bq) & ((b < bq) | (n < nb - 1))
    else:
        dist = a + bq - b
        valid = (dist >= 0) & (dist <= bq) & ((b >= bq) | (n > 0))
    return valid, dist.astype(F32) * float(dil)


def _attn_fwd(qkv, gq, gk, g, dil):
    t = qkv.shape[0]
    nb = t // dil // ATT_BLOCK
    bq = ATT_BLOCK

    def body(q_ref, k_ref, v_ref, gq_ref, gk_ref, o_ref, l_ref):
        bd = _head_block_diag()
        lt64 = _lane_lt64(bq)
        pair = pl.program_id(0)
        slopes = _pair_slopes(pair)

        def step(n, carry):
            valid, dist_bias = _band(n, nb, dil, False)
            prev = jnp.maximum(n - 1, 0)
            for r in range(dil):
                qp = _head_rmsnorm(_class_rows(q_ref, n, r, dil), gq_ref[...], bd).astype(BF16)
                kcat = jnp.concatenate([_head_rmsnorm(_class_rows(k_ref, prev, r, dil), gk_ref[...], bd),
                                        _head_rmsnorm(_class_rows(k_ref, n, r, dil), gk_ref[...], bd)],
                                       axis=0).astype(BF16)
                vcat = jnp.concatenate([_class_rows(v_ref, prev, r, dil), _class_rows(v_ref, n, r, dil)],
                                       axis=0).astype(BF16)
                outs, lses = [], []
                for hh in range(2):
                    mine = lt64 if hh == 0 else jnp.logical_not(lt64)
                    qm = jnp.where(mine, qp, jnp.zeros_like(qp))
                    s = _attn_logits(qm, kcat, slopes[hh], dist_bias, valid)
                    m = jnp.max(s, axis=1, keepdims=True)
                    p = jnp.exp(s - m)
                    l = jnp.sum(p, axis=1, keepdims=True)
                    outs.append(_dot(p.astype(BF16), vcat) * (1.0 / l))
                    lses.append(jnp.broadcast_to(m + jnp.log(l), (bq, LANES)))
                _store_class_rows(o_ref, n, r, dil, jnp.where(lt64, outs[0], outs[1]))
                _store_class_rows(l_ref, n, r, dil, jnp.where(lt64, lses[0], lses[1]))
            return carry

        lax.fori_loop(0, nb, step, 0)

    col = lambda j: pl.BlockSpec((t, LANES), _pair_col(g, j))
    vec = pl.BlockSpec((1, LANES), lambda pair: (0, 0))
    out = pl.BlockSpec((t, LANES), lambda pair: (0, pair))
    return pl.pallas_call(
        body, name=f"attn_fwd_g{g}", grid=(PAIRS,),
        in_specs=[col(0), col(1), col(2), vec, vec], out_specs=[out, out],
        out_shape=[jax.ShapeDtypeStruct((t, ATT_W), F32), jax.ShapeDtypeStruct((t, ATT_W), F32)],
        compiler_params=_params("parallel"),
    )(qkv, qkv, qkv, gq, gk)


def _attn_combine_fwd(outs, lses):
    t = outs[0].shape[0]
    tm = _tile(t, 256)

    def body(o0, o1, o2, l0, l1, l2, ob_ref, of_ref, lt_ref):
        a, b, c = l0[...], l1[...], l2[...]
        m = jnp.maximum(jnp.maximum(a, b), c)
        ea, eb, ec = jnp.exp(a - m), jnp.exp(b - m), jnp.exp(c - m)
        ssum = ea + eb + ec
        o = (ea * o0[...] + eb * o1[...] + ec * o2[...]) / ssum
        ob_ref[...] = o.astype(BF16)
        of_ref[...] = o
        lt_ref[...] = m + jnp.log(ssum)

    row = pl.BlockSpec((tm, ATT_W), lambda i: (i, 0))
    return pl.pallas_call(
        body, name="attn_combine_fwd", grid=(t // tm,),
        in_specs=[row] * 6, out_specs=[row] * 3,
        out_shape=[jax.ShapeDtypeStruct((t, ATT_W), BF16), jax.ShapeDtypeStruct((t, ATT_W), F32),
                   jax.ShapeDtypeStruct((t, ATT_W), F32)],
        compiler_params=_params("parallel"),
    )(*outs, *lses)


def _attn_combine_bwd(do, o):
    t = do.shape[0]
    tm = _tile(t, 256)

    def body(do_ref, o_ref, dl_ref):
        dl_ref[...] = _head_sums(do_ref[...] * o_ref[...], _head_block_diag())

    row = pl.BlockSpec((tm, ATT_W), lambda i: (i, 0))
    return pl.pallas_call(
        body, name="attn_combine_bwd", grid=(t // tm,),
        in_specs=[row, row], out_specs=row, out_shape=jax.ShapeDtypeStruct((t, ATT_W), F32),
        compiler_params=_params("parallel"),
    )(do, o)


def _attn_bwd_dq(qkv, gq, gk, do, l_rep, dl_rep, g, dil):
    t = qkv.shape[0]
    nb = t // dil // ATT_BLOCK
    bq = ATT_BLOCK

    def body(q_ref, k_ref, v_ref, gq_ref, gk_ref, do_ref, l_ref, dl_ref, dq_ref):
        bd = _head_block_diag()
        lt64 = _lane_lt64(bq)
        slopes = _pair_slopes(pl.program_id(0))

        def step(n, carry):
            valid, dist_bias = _band(n, nb, dil, False)
            prev = jnp.maximum(n - 1, 0)
            for r in range(dil):
                qp = _head_rmsnorm(_class_rows(q_ref, n, r, dil), gq_ref[...], bd).astype(BF16)
                dop = _class_rows(do_ref, n, r, dil).astype(BF16)
                kcat = jnp.concatenate([_head_rmsnorm(_class_rows(k_ref, prev, r, dil), gk_ref[...], bd),
                                        _head_rmsnorm(_class_rows(k_ref, n, r, dil), gk_ref[...], bd)],
                                       axis=0).astype(BF16)
                vcat = jnp.concatenate([_class_rows(v_ref, prev, r, dil), _class_rows(v_ref, n, r, dil)],
                                       axis=0).astype(BF16)
                lcols = _head_cols(_class_rows(l_ref, n, r, dil), lt64)
                dcols = _head_cols(_class_rows(dl_ref, n, r, dil), lt64)
                dqs = []
                for hh in range(2):
                    mine = lt64 if hh == 0 else jnp.logical_not(lt64)
                    qm = jnp.where(mine, qp, jnp.zeros_like(qp))
                    dom = jnp.where(mine, dop, jnp.zeros_like(dop))
                    s = _attn_logits(qm, kcat, slopes[hh], dist_bias, valid)
                    p = jnp.exp(s - jnp.concatenate([lcols[hh], lcols[hh]], axis=1))
                    dp = _dot_nt(dom, vcat)
                    ds = p * (dp - jnp.concatenate([dcols[hh], dcols[hh]], axis=1))
                    dqs.append(_dot(ds.astype(BF16), kcat) * ATT_SCALE)
                _store_class_rows(dq_ref, n, r, dil, jnp.where(lt64, dqs[0], dqs[1]))
            return carry

        lax.fori_loop(0, nb, step, 0)

    col = lambda j: pl.BlockSpec((t, LANES), _pair_col(g, j))
    vec = pl.BlockSpec((1, LANES), lambda pair: (0, 0))
    tok = pl.BlockSpec((t, LANES), lambda pair: (0, pair))
    return pl.pallas_call(
        body, name=f"attn_bwd_dq_g{g}", grid=(PAIRS,),
        in_specs=[col(0), col(1), col(2), vec, vec, tok, tok, tok], out_specs=tok,
        out_shape=jax.ShapeDtypeStruct((t, ATT_W), F32),
        compiler_params=_params("parallel"),
    )(qkv, qkv, qkv, gq, gk, do, l_rep, dl_rep)


def _attn_bwd_dkv(qkv, gq, gk, do, l_row, dl_row, g, dil):
    t = qkv.shape[0]
    nb = t // dil // ATT_BLOCK
    bq = ATT_BLOCK

    def body(q_ref, k_ref, v_ref, gq_ref, gk_ref, do_ref, l_ref, dl_ref, dk_ref, dv_ref):
        bd = _head_block_diag()
        lt64 = _lane_lt64(bq)
        slopes = _pair_slopes(pl.program_id(0))

        def step(n, carry):
            valid, dist_bias = _band(n, nb, dil, True)
            nxt = jnp.minimum(n + 1, nb - 1)
            for r in range(dil):
                kp = _head_rmsnorm(_class_rows(k_ref, n, r, dil), gk_ref[...], bd).astype(BF16)
                vp = _class_rows(v_ref, n, r, dil).astype(BF16)
                qcat = jnp.concatenate([_head_rmsnorm(_class_rows(q_ref, n, r, dil), gq_ref[...], bd),
                                        _head_rmsnorm(_class_rows(q_ref, nxt, r, dil), gq_ref[...], bd)],
                                       axis=0).astype(BF16)
                docat = jnp.concatenate([_class_rows(do_ref, n, r, dil), _class_rows(do_ref, nxt, r, dil)],
                                        axis=0).astype(BF16)
                lane_c = pl.multiple_of((r * nb + n) * bq, bq)
                lane_n = pl.multiple_of((r * nb + nxt) * bq, bq)
                dks, dvs = [], []
                for hh in range(2):
                    mine = lt64 if hh == 0 else jnp.logical_not(lt64)
                    km = jnp.where(mine, kp, jnp.zeros_like(kp))
                    vm = jnp.where(mine, vp, jnp.zeros_like(vp))
                    s_t = _attn_logits(km, qcat, slopes[hh], dist_bias, valid)
                    l_r = jnp.concatenate([l_ref[hh:hh + 1, pl.ds(lane_c, bq)],
                                           l_ref[hh:hh + 1, pl.ds(lane_n, bq)]], axis=1)
                    d_r = jnp.concatenate([dl_ref[hh:hh + 1, pl.ds(lane_c, bq)],
                                           dl_ref[hh:hh + 1, pl.ds(lane_n, bq)]], axis=1)
                    p_t = jnp.exp(s_t - l_r)
                    dvs.append(_dot(p_t.astype(BF16), docat))
                    dp_t = _dot_nt(vm, docat)
                    ds_t = p_t * (dp_t - d_r)
                    dks.append(_dot(ds_t.astype(BF16), qcat) * ATT_SCALE)
                _store_class_rows(dk_ref, n, r, dil, jnp.where(lt64, dks[0], dks[1]))
                _store_class_rows(dv_ref, n, r, dil, jnp.where(lt64, dvs[0], dvs[1]))
            return carry

        lax.fori_loop(0, nb, step, 0)

    col = lambda j: pl.BlockSpec((t, LANES), _pair_col(g, j))
    vec = pl.BlockSpec((1, LANES), lambda pair: (0, 0))
    tok = pl.BlockSpec((t, LANES), lambda pair: (0, pair))
    rows = pl.BlockSpec((None, 8, t), lambda pair: (pair, 0, 0))
    return pl.pallas_call(
        body, name=f"attn_bwd_dkv_g{g}", grid=(PAIRS,),
        in_specs=[col(0), col(1), col(2), vec, vec, tok, rows, rows], out_specs=[tok, tok],
        out_shape=[jax.ShapeDtypeStruct((t, ATT_W), F32), jax.ShapeDtypeStruct((t, ATT_W), F32)],
        compiler_params=_params("parallel"),
    )(qkv, qkv, qkv, gq, gk, do, l_row, dl_row)


def _rows_by_residue(rep, dil):
    t = rep.shape[0]
    per_head = rep[:, ::ATT_HEAD_DIM]
    rows = per_head.reshape(t // dil, dil, ATT_HEADS).transpose(2, 1, 0).reshape(PAIRS, 2, t)
    return jnp.pad(rows, ((0, 0), (0, 6), (0, 0)))


def _per_head(rep_row):
    return rep_row[0, ::SSM_HEAD_DIM]


def _rep_heads(v):
    return jnp.repeat(v, SSM_HEAD_DIM)[None, :]


def _pad_lanes(v):
    return jnp.pad(v, ((0, 0), (0, LANES - v.shape[1])))


def _ffn_ple_fwd(x1, p_i, prm, i):
    h = _rmsnorm_fwd(x1, prm["norm_ffn"][i:i + 1], name=f"ffn_norm_fwd_{i}")
    g, u, act = _swiglu_fwd(h, prm["ffn_w_gate"][i], prm["ffn_w_up"][i], name=f"swiglu_fwd_{i}")
    x2 = _matmul(act, prm["ffn_w_down"][i], mode="nn", addend=x1, name=f"ffn_down_{i}")
    x3 = _ple_fwd(x2, p_i, prm["ple_w_gate"][i], prm["ple_w_proj"][i], name=f"ple_fwd_{i}")
    return x3, dict(x1=x1, h=h, g=g, u=u, act=act, x2=x2)


def _ffn_ple_bwd(dx3, p_i, prm, i, sv, grads):
    ds, dple = _ple_bwd(sv["x2"], p_i, prm["ple_w_gate"][i], prm["ple_w_proj"][i], dx3, name=f"ple_bwd_{i}")
    grads["ple_w_gate"][i] = _matmul_tn(sv["x2"], ds, name=f"d_ple_w_gate_{i}")
    grads["ple_w_proj"][i] = _matmul_tn(dple, p_i, name=f"d_ple_w_proj_{i}")
    dx2 = _matmul(ds, prm["ple_w_gate"][i], mode="nt", addend=dx3, name=f"ple_dx_{i}")
    grads["ffn_w_down"][i] = _matmul_tn(sv["act"], dx2, name=f"d_ffn_w_down_{i}")
    dg, du = _swiglu_bwd(dx2, prm["ffn_w_down"][i], sv["g"], sv["u"], name=f"swiglu_bwd_{i}")
    grads["ffn_w_gate"][i] = _matmul_tn(dg, sv["h"], name=f"d_ffn_w_gate_{i}")
    grads["ffn_w_up"][i] = _matmul_tn(du, sv["h"], name=f"d_ffn_w_up_{i}")
    dh = _matmul(dg, prm["ffn_w_gate"][i], mode="nn", name=f"ffn_dh_gate_{i}")
    dh = _matmul(du, prm["ffn_w_up"][i], mode="nn", addend=dh, name=f"ffn_dh_up_{i}")
    dx1, dgain = _rmsnorm_bwd(sv["x1"], prm["norm_ffn"][i:i + 1], dh, dx2, name=f"ffn_norm_bwd_{i}")
    grads["norm_ffn"][i] = dgain[0]
    return dx1


def _mamba_fwd(x0, prm):
    h = _rmsnorm_fwd(x0, prm["norm_mix"][0:1], name="mix_norm_fwd_0")
    z = _matmul(h, prm["ssm_w_z"], mode="nt", name="ssm_in_z")
    xbc_pre = _matmul(h, prm["ssm_w_xbc"], mode="nt", name="ssm_in_xbc")
    dt_raw = _matmul(h, prm["ssm_w_dt"], mode="nt", name="ssm_in_dt")
    xbc = _conv_fwd(xbc_pre, prm["ssm_conv_w"], prm["ssm_conv_b"])
    dt_bias = _pad_lanes(prm["ssm_dt_bias"])
    a_log = _pad_lanes(prm["ssm_a_log"])
    dt, acs = _ssd_prep_fwd(dt_raw, dt_bias, a_log)
    dt_rep = jnp.repeat(dt[:, :SSM_HEADS], SSM_HEAD_DIM, axis=1)
    acs_rep = jnp.repeat(acs[:, :SSM_HEADS], SSM_HEAD_DIM, axis=1)
    acs_t = acs[:, :SSM_HEADS].T
    dskip_rep = _rep_heads(prm["ssm_d_skip"][0])
    y, hin_all = _ssd_fwd(xbc, dt_rep, acs_rep, acs_t, dskip_rep)
    yn = _gate_norm_fwd(y, z, prm["ssm_norm_w"])
    x1 = _matmul(yn, prm["ssm_w_out"], mode="nn", addend=x0, name="ssm_out")
    sv = dict(x0=x0, h=h, z=z, xbc_pre=xbc_pre, dt_raw=dt_raw, xbc=xbc, dt_bias=dt_bias, dt_rep=dt_rep,
              acs_rep=acs_rep, acs_t=acs_t, dskip_rep=dskip_rep, y=y, hin_all=hin_all, yn=yn)
    return x1, sv


def _mamba_bwd(dx1, prm, sv, grads):
    grads["ssm_w_out"] = _matmul_tn(sv["yn"], dx1, name="d_ssm_w_out")
    dyn = _matmul(dx1, prm["ssm_w_out"], mode="nt", name="ssm_out_dx")
    dy, dz, dnw = _gate_norm_bwd(sv["y"], sv["z"], prm["ssm_norm_w"], dyn)
    grads["ssm_norm_w"] = dnw
    a_rep = _rep_heads(-jnp.exp(prm["ssm_a_log"][0]))
    dxbc, ddt_rep, da_rep, dds_rep = _ssd_bwd(sv["xbc"], sv["dt_rep"], sv["acs_rep"], sv["acs_t"], sv["dskip_rep"],
                                              a_rep, sv["hin_all"], dy)
    grads["ssm_d_skip"] = _per_head(dds_rep)[None, :]
    grads["ssm_a_log"] = (_per_head(da_rep) * _per_head(a_rep))[None, :]
    ddt = _pad_lanes(ddt_rep[:, ::SSM_HEAD_DIM])
    ddt_raw, dbias = _ssd_prep_bwd(sv["dt_raw"], sv["dt_bias"], ddt)
    grads["ssm_dt_bias"] = dbias[:, :SSM_HEADS]
    du, dcw, dcb = _conv_bwd(sv["xbc_pre"], prm["ssm_conv_w"], prm["ssm_conv_b"], dxbc)
    grads["ssm_conv_w"] = dcw
    grads["ssm_conv_b"] = dcb
    h = sv["h"]
    grads["ssm_w_in"] = jnp.concatenate(
        [_matmul_tn(dz, h, name="d_ssm_w_z"), _matmul_tn(du, h, name="d_ssm_w_xbc"),
         _matmul_tn(ddt_raw, h, name="d_ssm_w_dt")[:SSM_HEADS]], axis=0)
    dh = _matmul(dz, prm["ssm_w_z"], mode="nn", name="ssm_dh_z")
    dh = _matmul(du, prm["ssm_w_xbc"], mode="nn", addend=dh, name="ssm_dh_xbc")
    dh = _matmul(ddt_raw, prm["ssm_w_dt"], mode="nn", addend=dh, name="ssm_dh_dt")
    dx0, dgain = _rmsnorm_bwd(sv["x0"], prm["norm_mix"][0:1], dh, dx1, name="mix_norm_bwd_0")
    grads["norm_mix"][0] = dgain[0]
    return dx0


def _attn_mixer_fwd(x0, prm):
    h = _rmsnorm_fwd(x0, prm["norm_mix"][1:2], name="mix_norm_fwd_1")
    qkv = _matmul(h, prm["att_w_qkv"], mode="nt", name="att_qkv")
    gq = jnp.tile(prm["att_q_norm"], (1, ATT_HEADS))
    gk = jnp.tile(prm["att_k_norm"], (1, ATT_HEADS))
    gq2, gk2 = gq[:, :LANES], gk[:, :LANES]
    outs, lses = [], []
    for g, (window, dil) in enumerate(DIL_PATTERNS):
        o_g, l_g = _attn_fwd(qkv, gq2, gk2, g, dil)
        outs.append(o_g)
        lses.append(l_g)
    o_b, o_f, l_rep = _attn_combine_fwd(outs, lses)
    x1 = _matmul(o_b, prm["att_w_o"], mode="nn", addend=x0, name="att_out")
    sv = dict(x0=x0, h=h, qkv=qkv, gq=gq, gk=gk, gq2=gq2, gk2=gk2, o_b=o_b, o_f=o_f, l_rep=l_rep)
    return x1, sv


def _attn_mixer_bwd(dx1, prm, sv, grads):
    grads["att_w_o"] = _matmul_tn(sv["o_b"], dx1, name="d_att_w_o")
    do = _matmul(dx1, prm["att_w_o"], mode="nt", name="att_out_dx")
    dl_rep = _attn_combine_bwd(do, sv["o_f"])
    blocks = [None] * N_QKV_BLOCKS
    for g, (window, dil) in enumerate(DIL_PATTERNS):
        blocks[3 * g] = _attn_bwd_dq(sv["qkv"], sv["gq2"], sv["gk2"], do, sv["l_rep"], dl_rep, g, dil)
        dk, dv = _attn_bwd_dkv(sv["qkv"], sv["gq2"], sv["gk2"], do, _rows_by_residue(sv["l_rep"], dil),
                               _rows_by_residue(dl_rep, dil), g, dil)
        blocks[3 * g + 1] = dk
        blocks[3 * g + 2] = dv
    dqkv, dgq, dgk = _qk_norm_bwd(sv["qkv"], sv["gq"], sv["gk"], blocks)
    grads["att_q_norm"] = dgq.reshape(ATT_HEADS, ATT_HEAD_DIM).sum(axis=0)[None, :]
    grads["att_k_norm"] = dgk.reshape(ATT_HEADS, ATT_HEAD_DIM).sum(axis=0)[None, :]
    grads["att_w_qkv"] = _matmul_tn(dqkv, sv["h"], name="d_att_w_qkv")
    dh = _matmul(dqkv, prm["att_w_qkv"], mode="nn", name="att_qkv_dx")
    dx0, dgain = _rmsnorm_bwd(sv["x0"], prm["norm_mix"][1:2], dh, dx1, name="mix_norm_bwd_1")
    grads["norm_mix"][1] = dgain[0]
    return dx0


def _local_step(x, p, target, prm):
    grads = {k: [None, None] for k in ("norm_mix", "norm_ffn", "ffn_w_gate", "ffn_w_up", "ffn_w_down",
                                       "ple_w_proj", "ple_w_gate")}
    x1, sv_m = _mamba_fwd(x, prm)
    x3, sv_f0 = _ffn_ple_fwd(x1, p[0], prm, 0)
    x4, sv_a = _attn_mixer_fwd(x3, prm)
    x6, sv_f1 = _ffn_ple_fwd(x4, p[1], prm, 1)
    dy, loss_row = _loss_head(x6, target)
    dx4 = _ffn_ple_bwd(dy, p[1], prm, 1, sv_f1, grads)
    dx3 = _attn_mixer_bwd(dx4, prm, sv_a, grads)
    dx1 = _ffn_ple_bwd(dx3, p[0], prm, 0, sv_f0, grads)
    dx0 = _mamba_bwd(dx1, prm, sv_m, grads)
    return loss_row, dx0, grads


MESH = pl.DeviceIdType.MESH
ANY = pl.BlockSpec(memory_space=pl.ANY)
W_IN_SLAB_ROWS = 1312


def _position():
    return lax.axis_index("x"), lax.axis_index("y"), lax.axis_index("c")


def _other_chips(x, y):
    return [(1 - x, y), (x, 1 - y), (1 - x, 1 - y)]


def _gather_slabs(entries, conv_w):
    n = len(entries)

    def body(*refs):
        in_refs, conv_ref = refs[:n], refs[n]
        out_refs, conv_out = refs[n + 1:2 * n + 1], refs[2 * n + 1]
        send_sems, recv_sems = refs[2 * n + 2], refs[2 * n + 3]
        x, y, c = _position()
        me, sibling = (x, y, c), (x, y, 1 - c)
        chips = _other_chips(x, y)

        def copy(k, src, dst, to):
            return pltpu.make_async_remote_copy(src_ref=src, dst_ref=dst, send_sem=send_sems.at[k],
                                                recv_sem=recv_sems.at[k], device_id=to, device_id_type=MESH)

        started = []
        for j, chip in enumerate(chips):
            for e in range(n):
                started.append(copy(6 * e + j, in_refs[e].at[c], out_refs[e].at[2 * x + y, c], (*chip, c)))
                started[-1].start()
            started.append(copy(6 * n + j, conv_ref, conv_out.at[2 * x + y], (*chip, c)))
            started[-1].start()
        for j, (px, py) in enumerate(chips):
            for e in range(n):
                landed = out_refs[e].at[2 * px + py, c]
                copy(6 * e + j, landed, landed, me).wait_recv()
                started.append(copy(6 * e + 3 + j, landed, landed, sibling))
                started[-1].start()
            copy(6 * n + j, conv_ref, conv_out.at[2 * px + py], me).wait_recv()
        for j, (px, py) in enumerate(chips):
            for e in range(n):
                passed = out_refs[e].at[2 * px + py, 1 - c]
                copy(6 * e + 3 + j, passed, passed, me).wait_recv()
        for cp in started:
            cp.wait_send()

    outs = pl.pallas_call(
        body, name="gather_slabs", in_specs=[ANY] * (n + 1), out_specs=[ANY] * (n + 1),
        out_shape=[jax.ShapeDtypeStruct((N_CHIPS,) + e.shape, e.dtype) for e in entries]
        + [jax.ShapeDtypeStruct((N_CHIPS,) + conv_w.shape, conv_w.dtype)],
        scratch_shapes=[pltpu.SemaphoreType.DMA((6 * n + 3,)), pltpu.SemaphoreType.DMA((6 * n + 3,))],
    )(*entries, conv_w)
    return outs[:n], outs[n]


def _swap_halves(grads):
    n = len(grads)

    def body(*refs):
        g_refs, r_refs = refs[:n], refs[n:2 * n]
        send_sems, recv_sems = refs[2 * n], refs[2 * n + 1]
        x, y, c = _position()
        cps = [pltpu.make_async_remote_copy(src_ref=g_refs[e].at[:, 1 - c], dst_ref=r_refs[e],
                                            send_sem=send_sems.at[e], recv_sem=recv_sems.at[e],
                                            device_id=(x, y, 1 - c), device_id_type=MESH) for e in range(n)]
        for cp in cps:
            cp.start()
        for cp in cps:
            cp.wait()

    return pl.pallas_call(
        body, name="grad_swap_halves", in_specs=[ANY] * n, out_specs=[ANY] * n,
        out_shape=[jax.ShapeDtypeStruct((N_CHIPS,) + g.shape[2:], g.dtype) for g in grads],
        scratch_shapes=[pltpu.SemaphoreType.DMA((n,)), pltpu.SemaphoreType.DMA((n,))],
    )(*grads)


def _chip_exchange(chipsums):
    n = len(chipsums)

    def body(*refs):
        cs_refs, r_refs = refs[:n], refs[n:2 * n]
        send_sems, recv_sems = refs[2 * n], refs[2 * n + 1]
        x, y, c = _position()
        cps = []
        for j, (tx, ty) in enumerate(_other_chips(x, y)):
            for e in range(n):
                cps.append(pltpu.make_async_remote_copy(
                    src_ref=cs_refs[e].at[2 * tx + ty], dst_ref=r_refs[e].at[j], send_sem=send_sems.at[3 * e + j],
                    recv_sem=recv_sems.at[3 * e + j], device_id=(tx, ty, c), device_id_type=MESH))
                cps[-1].start()
        for cp in cps:
            cp.wait()

    return pl.pallas_call(
        body, name="grad_chip_exchange", in_specs=[ANY] * n, out_specs=[ANY] * n,
        out_shape=[jax.ShapeDtypeStruct((3,) + cs.shape[1:], cs.dtype) for cs in chipsums],
        scratch_shapes=[pltpu.SemaphoreType.DMA((3 * n,)), pltpu.SemaphoreType.DMA((3 * n,))],
    )(*chipsums)


def _share_halves(totals):
    n = len(totals)

    def body(*refs):
        t_refs, r_refs = refs[:n], refs[n:2 * n]
        send_sems, recv_sems = refs[2 * n], refs[2 * n + 1]
        x, y, c = _position()
        cps = [pltpu.make_async_remote_copy(src_ref=t_refs[e], dst_ref=r_refs[e], send_sem=send_sems.at[e],
                                            recv_sem=recv_sems.at[e], device_id=(x, y, 1 - c), device_id_type=MESH)
               for e in range(n)]
        for cp in cps:
            cp.start()
        for cp in cps:
            cp.wait()

    return pl.pallas_call(
        body, name="grad_share_halves", in_specs=[ANY] * n, out_specs=[ANY] * n,
        out_shape=[jax.ShapeDtypeStruct(t.shape, t.dtype) for t in totals],
        scratch_shapes=[pltpu.SemaphoreType.DMA((n,)), pltpu.SemaphoreType.DMA((n,))],
    )(*totals)


def _reduce_rows(h):
    return h if h <= 704 else h // 2


def _add_sibling(grad, recv, c_idx, *, name):
    _, _, h, cw = grad.shape
    th = _reduce_rows(h)

    def body(c_ref, g_ref, r_ref, o_ref):
        o_ref[...] = (g_ref[...] + r_ref[...]).astype(BF16)

    return pl.pallas_call(
        body, name=name,
        grid_spec=pltpu.PrefetchScalarGridSpec(
            num_scalar_prefetch=1, grid=(N_CHIPS, h // th),
            in_specs=[pl.BlockSpec((None, None, th, cw), lambda s, i, c_ref: (s, c_ref[0], i, 0)),
                      pl.BlockSpec((None, th, cw), lambda s, i, c_ref: (s, i, 0))],
            out_specs=pl.BlockSpec((None, th, cw), lambda s, i, c_ref: (s, i, 0))),
        out_shape=jax.ShapeDtypeStruct((N_CHIPS, h, cw), BF16),
        compiler_params=_params("parallel", "parallel"),
    )(c_idx, grad, recv)


def _add_chips(chipsum, recv, s_idx, *, name):
    _, h, cw = chipsum.shape
    th = _reduce_rows(h)

    def body(s_ref, own_ref, r_ref, o_ref):
        o_ref[...] = ((own_ref[...].astype(F32) + r_ref[0].astype(F32)) + r_ref[1].astype(F32)) + r_ref[2].astype(F32)

    return pl.pallas_call(
        body, name=name,
        grid_spec=pltpu.PrefetchScalarGridSpec(
            num_scalar_prefetch=1, grid=(h // th,),
            in_specs=[pl.BlockSpec((None, th, cw), lambda i, s_ref: (s_ref[0], i, 0)),
                      pl.BlockSpec((3, th, cw), lambda i, s_ref: (0, i, 0))],
            out_specs=pl.BlockSpec((th, cw), lambda i, s_ref: (i, 0))),
        out_shape=jax.ShapeDtypeStruct((h, cw), F32),
        compiler_params=_params("parallel"),
    )(s_idx, chipsum, recv)


def _adamw_math(w, g, m, v):
    m = ADAM_B1 * m + (1.0 - ADAM_B1) * g
    v = ADAM_B2 * v + (1.0 - ADAM_B2) * (g * g)
    m_hat = m / (1.0 - ADAM_B1 ** ADAM_STEP)
    v_hat = v / (1.0 - ADAM_B2 ** ADAM_STEP)
    delta = -ADAM_LR * (m_hat / (jnp.sqrt(v_hat) + ADAM_EPS) + ADAM_WD * w)
    return delta, m, v


ADAM_TILE_ELEMS = 256 * 1024


def _adamw(w, g, m, v, *, name):
    shape = w.shape
    cols = shape[-1]
    rows = w.size // cols
    tr = rows
    for cand in range(8, rows, 8):
        if rows % cand == 0 and cand * cols <= ADAM_TILE_ELEMS:
            tr = cand
    if rows * cols <= ADAM_TILE_ELEMS:
        tr = rows

    def body(w_ref, g_ref, m_ref, v_ref, d_ref, nm_ref, nv_ref):
        d, nm, nv = _adamw_math(w_ref[...], g_ref[...], m_ref[...], v_ref[...])
        d_ref[...] = d
        nm_ref[...] = nm
        nv_ref[...] = nv

    blk = pl.BlockSpec((tr, cols), lambda i: (i, 0))
    sds = jax.ShapeDtypeStruct((rows, cols), F32)
    outs = pl.pallas_call(
        body, name=name, grid=(rows // tr,), in_specs=[blk] * 4, out_specs=[blk] * 3, out_shape=[sds] * 3,
        compiler_params=_params("parallel"),
    )(*[a.reshape(rows, cols) for a in (w, g, m, v)])
    return [o.reshape(shape) for o in outs]


SMALL_LAYOUT = (("loss", 1), ("norm_mix", 16), ("norm_ffn", 16), ("ssm_conv_b", 24), ("ssm_dt_bias", 1),
                ("ssm_a_log", 1), ("ssm_d_skip", 1), ("ssm_norm_w", 16), ("att_q_norm", 1), ("att_k_norm", 1),
                ("conv_w_full", 96))
SMALL_ROWS = 176
N_DEVICES = 8


def _small_pack(values):
    parts = []
    for name, rows in SMALL_LAYOUT:
        flat = values[name].reshape(-1).astype(F32)
        parts.append(jnp.pad(flat, (0, rows * LANES - flat.shape[0])).reshape(rows, LANES))
    used = sum(r for _, r in SMALL_LAYOUT)
    parts.append(jnp.zeros((SMALL_ROWS - used, LANES), F32))
    return jnp.concatenate(parts, axis=0)


def _small_unpack(pack, shapes):
    out, off = {}, 0
    for name, rows in SMALL_LAYOUT:
        shape = shapes[name]
        n = math.prod(shape)
        out[name] = pack[off:off + rows].reshape(-1)[:n].reshape(shape)
        off += rows
    return out


def _small_allreduce_adamw(g, w, m, v):
    def body(g_ref, w_ref, m_ref, v_ref, gs_ref, d_ref, nm_ref, nv_ref, buf, send_sems, recv_sems):
        x, y, c = _position()
        pos = (x, y, c)
        me = 4 * x + 2 * y + c
        buf[me] = g_ref[...]
        peers = []
        for k in range(1, N_DEVICES):
            bits = ((k >> 2) & 1, (k >> 1) & 1, k & 1)
            peers.append(tuple(1 - p if b else p for p, b in zip(pos, bits)))
        cps = [pltpu.make_async_remote_copy(src_ref=g_ref, dst_ref=buf.at[me], send_sem=send_sems.at[k],
                                            recv_sem=recv_sems.at[k], device_id=peer, device_id_type=MESH)
               for k, peer in enumerate(peers)]
        for cp in cps:
            cp.start()
        for k, (px, py, pc) in enumerate(peers):
            pltpu.make_async_remote_copy(src_ref=g_ref, dst_ref=buf.at[4 * px + 2 * py + pc],
                                         send_sem=send_sems.at[k], recv_sem=recv_sems.at[k],
                                         device_id=(px, py, pc), device_id_type=MESH).wait_recv()
        for cp in cps:
            cp.wait_send()
        total = buf[0]
        for dev in range(1, N_DEVICES):
            total = total + buf[dev]
        gs_ref[...] = total
        d, nm, nv = _adamw_math(w_ref[...], total, m_ref[...], v_ref[...])
        d_ref[...] = d
        nm_ref[...] = nm
        nv_ref[...] = nv

    vm = pl.BlockSpec(memory_space=pltpu.VMEM)
    sds = jax.ShapeDtypeStruct((SMALL_ROWS, LANES), F32)
    return pl.pallas_call(
        body, name="small_allreduce_adamw", in_specs=[vm] * 4, out_specs=[vm] * 4, out_shape=[sds] * 4,
        scratch_shapes=[pltpu.VMEM((N_DEVICES, SMALL_ROWS, LANES), F32),
                        pltpu.SemaphoreType.DMA((N_DEVICES - 1,)), pltpu.SemaphoreType.DMA((N_DEVICES - 1,))],
    )(g, w, m, v)


SMALL = tuple(n for n, _ in SMALL_LAYOUT if n not in ("loss", "conv_w_full"))
WEIGHTS = ("norm_mix", "norm_ffn", "ssm_w_in", "ssm_conv_w", "ssm_conv_b", "ssm_dt_bias", "ssm_a_log", "ssm_d_skip",
           "ssm_norm_w", "ssm_w_out", "att_w_qkv", "att_q_norm", "att_k_norm", "att_w_o", "ffn_w_gate", "ffn_w_up",
           "ffn_w_down", "ple_w_proj", "ple_w_gate")
COLUMN_SHARDED = ("ssm_w_in", "att_w_qkv", "ffn_w_gate", "ffn_w_up", "ple_w_proj")
LAYERED = ("ffn_w_gate", "ffn_w_up", "ffn_w_down", "ple_w_proj", "ple_w_gate")
GATHER_ORDER = ("ssm_w_in", "ssm_w_out", "att_w_qkv", "att_w_o", "ffn_w_gate", "ffn_w_up", "ffn_w_down",
                "ple_w_proj", "ple_w_gate")


def _weight_slabs(w):
    slabs = []
    for n in GATHER_ORDER:
        a = w[n]
        if n in LAYERED:
            a = a.transpose(0, 2, 1) if n in COLUMN_SHARDED else a
        else:
            a = a[0].T if n in COLUMN_SHARDED else a[0]
            if n == "ssm_w_in":
                a = jnp.pad(a, ((0, W_IN_SLAB_ROWS - a.shape[0]), (0, 0)))
            a = a.reshape(2, a.shape[0] // 2, a.shape[1])
        slabs.append(a.astype(BF16))
    return slabs


def _full_weights(gathered, own, conv_all, conv_own, s_me, small):
    full = {}
    for n, g, o in zip(GATHER_ORDER, gathered, own):
        full[n] = lax.dynamic_update_slice(g, o[None], (s_me, 0, 0, 0))
    prm = dict(small)
    conv = lax.dynamic_update_slice(conv_all, conv_own[None], (s_me, 0, 0))
    prm["ssm_conv_w"] = conv.transpose(1, 0, 2).reshape(CONV_WIDTH, CONV_DIM)
    rows = (D_INNER + CONV_DIM + SSM_HEADS) // N_CHIPS
    w_in_t = full["ssm_w_in"].reshape(N_CHIPS, W_IN_SLAB_ROWS, D_MODEL)[:, :rows].reshape(N_CHIPS * rows, D_MODEL)
    prm["ssm_w_z"] = w_in_t[:D_INNER]
    prm["ssm_w_xbc"] = w_in_t[D_INNER:D_INNER + CONV_DIM]
    prm["ssm_w_dt"] = jnp.pad(w_in_t[D_INNER + CONV_DIM:], ((0, LANES - SSM_HEADS), (0, 0)))
    prm["ssm_w_out"] = full["ssm_w_out"].reshape(D_INNER, D_MODEL)
    prm["att_w_qkv"] = full["att_w_qkv"].reshape(N_QKV_BLOCKS * ATT_W, D_MODEL)
    prm["att_w_o"] = full["att_w_o"].reshape(ATT_W, D_MODEL)
    for n in LAYERED:
        g = full[n]
        prm[n] = [g[:, i].reshape(N_CHIPS * g.shape[2], g.shape[3]) for i in range(2)]
    return prm


def _grad_slabs(grads):
    out = []
    for n in GATHER_ORDER:
        for i in (range(2) if n in LAYERED else (None,)):
            g = grads[n] if i is None else grads[n][i]
            if n == "ssm_w_in":
                g = jnp.pad(g.reshape(N_CHIPS, g.shape[0] // N_CHIPS, D_MODEL),
                            ((0, 0), (0, W_IN_SLAB_ROWS - g.shape[0] // N_CHIPS), (0, 0)))
            rows = g.size // (N_CHIPS * g.shape[-1])
            out.append((n, i, g.reshape(N_CHIPS, 2, rows // 2, g.shape[-1])))
    return out


def _natural_shard(n, reduced, shape):
    def one(r):
        if n == "ssm_w_in":
            r = r[:shape[-1]]
        return r.T if n in COLUMN_SHARDED else r
    if n in LAYERED:
        return jnp.stack([one(r) for r in reduced]).reshape(shape)
    return one(reduced[0]).reshape(shape)


def kernel(x, p, norm_mix, norm_ffn, ssm_w_in, ssm_conv_w, ssm_conv_b, ssm_dt_bias, ssm_a_log, ssm_d_skip, ssm_norm_w, ssm_w_out, att_w_qkv, att_q_norm, att_k_norm, att_w_o, ffn_w_gate, ffn_w_up, ffn_w_down, ple_w_proj, ple_w_gate, loss_target, m_norm_mix, m_norm_ffn, m_ssm_w_in, m_ssm_conv_w, m_ssm_conv_b, m_ssm_dt_bias, m_ssm_a_log, m_ssm_d_skip, m_ssm_norm_w, m_ssm_w_out, m_att_w_qkv, m_att_q_norm, m_att_k_norm, m_att_w_o, m_ffn_w_gate, m_ffn_w_up, m_ffn_w_down, m_ple_w_proj, m_ple_w_gate, v_norm_mix, v_norm_ffn, v_ssm_w_in, v_ssm_conv_w, v_ssm_conv_b, v_ssm_dt_bias, v_ssm_a_log, v_ssm_d_skip, v_ssm_norm_w, v_ssm_w_out, v_att_w_qkv, v_att_q_norm, v_att_k_norm, v_att_w_o, v_ffn_w_gate, v_ffn_w_up, v_ffn_w_down, v_ple_w_proj, v_ple_w_gate):
    given = dict(locals())
    w = {n: given[n] for n in WEIGHTS}
    m = {n: given["m_" + n] for n in WEIGHTS}
    v = {n: given["v_" + n] for n in WEIGHTS}
    c_idx = lax.axis_index("c").astype(jnp.int32).reshape(1)
    s_idx = (2 * lax.axis_index("x") + lax.axis_index("y")).astype(jnp.int32).reshape(1)

    s_me = 2 * lax.axis_index("x") + lax.axis_index("y")
    first_core = lax.axis_index("c") == 0

    own = _weight_slabs(w)
    gathered, conv_all = _gather_slabs(own, ssm_conv_w[0])
    prm = _full_weights(gathered, own, conv_all, ssm_conv_w[0], s_me, {n: w[n] for n in SMALL})

    loss_row, dx, grads = _local_step(x[0], p[:, 0], loss_target[0], prm)

    slabs = _grad_slabs(grads)
    tags = [n if i is None else f"{n}_{i}" for n, i, _ in slabs]
    g4 = [g for _, _, g in slabs]
    from_sibling = _swap_halves(g4)
    chipsums = [_add_sibling(g, r, c_idx, name="add_sibling_" + t) for g, r, t in zip(g4, from_sibling, tags)]
    from_chips = _chip_exchange(chipsums)
    totals = [_add_chips(cs, r, s_idx, name="add_chips_" + t) for cs, r, t in zip(chipsums, from_chips, tags)]
    shared = _share_halves(totals)
    reduced = {}
    for (n, i, _), mine, theirs in zip(slabs, totals, shared):
        lo = jnp.where(first_core, mine, theirs)
        hi = jnp.where(first_core, theirs, mine)
        reduced.setdefault(n, []).append(jnp.concatenate([lo, hi], axis=0))

    grad, delta, new_m, new_v = {}, {}, {}, {}
    for n in GATHER_ORDER:
        grad[n] = _natural_shard(n, reduced[n], w[n].shape)
        delta[n], new_m[n], new_v[n] = _adamw(w[n], grad[n], m[n], v[n], name="adamw_" + n)

    small_g = {n: (jnp.stack(grads[n]) if isinstance(grads[n], list) else grads[n]) for n in SMALL}
    small_g["loss"] = loss_row
    small_g["conv_w_full"] = grads["ssm_conv_w"]
    zero = {"loss": jnp.zeros((1, LANES), F32), "conv_w_full": jnp.zeros((CONV_WIDTH, CONV_DIM), F32)}
    outs = _small_allreduce_adamw(_small_pack(small_g), _small_pack({**w, **zero}), _small_pack({**m, **zero}),
                                  _small_pack({**v, **zero}))
    shapes = {n: w[n].shape for n in SMALL}
    shapes["loss"] = (1, LANES)
    shapes["conv_w_full"] = (CONV_WIDTH, CONV_DIM)
    sg, sd, sm, sv = [_small_unpack(o, shapes) for o in outs]
    for n in SMALL:
        grad[n], delta[n], new_m[n], new_v[n] = sg[n], sd[n], sm[n], sv[n]
    loss = sg["loss"][0, 0]
    conv_cols = CONV_DIM // N_CHIPS
    grad["ssm_conv_w"] = lax.dynamic_slice(sg["conv_w_full"], (0, s_me * conv_cols), (CONV_WIDTH, conv_cols))[None]
    delta["ssm_conv_w"], new_m["ssm_conv_w"], new_v["ssm_conv_w"] = _adamw(
        ssm_conv_w, grad["ssm_conv_w"], m_ssm_conv_w, v_ssm_conv_w, name="adamw_ssm_conv_w")

    return (loss, dx[None], *[grad[n] for n in WEIGHTS], *[delta[n] for n in WEIGHTS],
            *[new_m[n] for n in WEIGHTS], *[new_v[n] for n in WEIGHTS])
```

```python
import functools
import math

import jax
import jax.numpy as jnp
from jax import lax
from jax.experimental import pallas as pl
from jax.experimental.pallas import tpu as pltpu

F32 = jnp.float32
BF16 = jnp.bfloat16
HIGHEST = lax.Precision.HIGHEST

NORM_EPS = 1e-6
ADAM_LR, ADAM_B1, ADAM_B2, ADAM_EPS, ADAM_WD, ADAM_STEP = 0.001, 0.9, 0.999, 1e-08, 0.01, 10

D_MODEL = 1024
D_INNER = 2048
SSM_HEADS = 32
SSM_HEAD_DIM = 64
SSM_GROUPS = 4
SSM_STATE = 128
SSD_CHUNK = 128
CONV_DIM = 3072
CONV_WIDTH = 4
ATT_HEADS = 16
ATT_HEAD_DIM = 64
DIL_PATTERNS = ((128, 1), (512, 4), (2048, 16))
ATT_BLOCK = 128
FFN_HIDDEN = 2816
PLE_DIM = 256

LANES = 128
V7X_VMEM_LIMIT = 56 * 1024 * 1024
NEG_BIG = -1e30

N_CHIPS = 4


def _params(*sem):
    return pltpu.CompilerParams(dimension_semantics=sem, vmem_limit_bytes=V7X_VMEM_LIMIT)


def _tile(n, pref):
    if n <= pref:
        return n
    best = None
    for t in range(LANES, pref + 1, LANES):
        if n % t == 0:
            best = t
    assert best is not None, (n, pref)
    return best


def _sigmoid(v):
    return 1.0 / (1.0 + jnp.exp(-v))


def _dot(a, b):
    return jnp.dot(a, b, preferred_element_type=F32)


def _dot_nt(a, b):
    return lax.dot_general(a, b, (((1,), (1,)), ((), ())), preferred_element_type=F32)


def _dot_tn(a, b):
    return lax.dot_general(a, b, (((0,), (0,)), ((), ())), preferred_element_type=F32)


def _head_block_diag():
    i = lax.broadcasted_iota(jnp.int32, (LANES, LANES), 0) // ATT_HEAD_DIM
    j = lax.broadcasted_iota(jnp.int32, (LANES, LANES), 1) // ATT_HEAD_DIM
    return (i == j).astype(BF16)


def _split_dot(ones, z):
    hi = z.astype(BF16)
    lo = (z - hi.astype(F32)).astype(BF16)
    return _dot(ones, hi) + _dot(ones, lo)


def _head_sums(z, bd):
    hi = z.astype(BF16)
    lo = (z - hi.astype(F32)).astype(BF16)
    parts = []
    for t in range(z.shape[1] // LANES):
        sl = slice(t * LANES, (t + 1) * LANES)
        parts.append(_dot(hi[:, sl], bd) + _dot(lo[:, sl], bd))
    return parts[0] if len(parts) == 1 else jnp.concatenate(parts, axis=1)


def _lane_lt64(rows):
    return lax.broadcasted_iota(jnp.int32, (rows, LANES), 1) < ATT_HEAD_DIM


def _matmul(a, b, *, mode, name, out_dtype=F32, addend=None, tm=1024, tn=512, tk_max=3072):
    m, k = a.shape
    if mode == "nn":
        k2, n = b.shape
    else:
        n, k2 = b.shape
    assert k == k2, (a.shape, b.shape, mode)
    tm, tn, tk = _tile(m, tm), _tile(n, tn), _tile(k, tk_max)
    nk = k // tk
    has_add = addend is not None

    def body(*refs):
        a_ref, b_ref = refs[0], refs[1]
        add_ref = refs[2] if has_add else None
        o_ref, acc_ref = refs[-2], refs[-1]
        kk = pl.program_id(2)
        av = a_ref[...].astype(BF16)
        bv = b_ref[...].astype(BF16)
        part = _dot(av, bv) if mode == "nn" else _dot_nt(av, bv)

        @pl.when(kk == 0)
        def _():
            acc_ref[...] = part

        @pl.when(kk > 0)
        def _():
            acc_ref[...] += part

        @pl.when(kk == nk - 1)
        def _():
            res = acc_ref[...]
            if has_add:
                res = res + add_ref[...]
            o_ref[...] = res.astype(out_dtype)

    a_spec = pl.BlockSpec((tm, tk), lambda i, j, kk: (i, kk))
    if mode == "nn":
        b_spec = pl.BlockSpec((tk, tn), lambda i, j, kk: (kk, j))
    else:
        b_spec = pl.BlockSpec((tn, tk), lambda i, j, kk: (j, kk))
    in_specs = [a_spec, b_spec]
    args = [a, b]
    if has_add:
        in_specs.append(pl.BlockSpec((tm, tn), lambda i, j, kk: (i, j)))
        args.append(addend)
    return pl.pallas_call(
        body, name=name, grid=(m // tm, n // tn, nk),
        in_specs=in_specs, out_specs=pl.BlockSpec((tm, tn), lambda i, j, kk: (i, j)),
        out_shape=jax.ShapeDtypeStruct((m, n), out_dtype),
        scratch_shapes=[pltpu.VMEM((tm, tn), F32)],
        compiler_params=_params("parallel", "parallel", "arbitrary"),
    )(*args)


def _matmul_tn(a, b, *, name, tm=1408, tn=512, tk=1024):
    t, m = a.shape
    t2, n = b.shape
    assert t == t2
    tm, tn, tk = _tile(m, tm), _tile(n, tn), _tile(t, tk)

    def body(a_ref, b_ref, o_ref):
        part = _dot_tn(a_ref[...].astype(BF16), b_ref[...].astype(BF16))

        @pl.when(pl.program_id(2) == 0)
        def _():
            o_ref[...] = part

        @pl.when(pl.program_id(2) > 0)
        def _():
            o_ref[...] += part

    return pl.pallas_call(
        body, name=name, grid=(m // tm, n // tn, t // tk),
        in_specs=[pl.BlockSpec((tk, tm), lambda i, j, kk: (kk, i)),
                  pl.BlockSpec((tk, tn), lambda i, j, kk: (kk, j))],
        out_specs=pl.BlockSpec((tm, tn), lambda i, j, kk: (i, j)),
        out_shape=jax.ShapeDtypeStruct((m, n), F32),
        compiler_params=_params("parallel", "parallel", "arbitrary"),
    )(a, b)


def _rmsnorm_fwd(x, gain, *, name):
    t, d = x.shape
    tm = _tile(t, 512)

    def body(x_ref, g_ref, o_ref):
        xv = x_ref[...]
        r = lax.rsqrt(jnp.mean(xv * xv, axis=-1, keepdims=True) + NORM_EPS)
        o_ref[...] = (xv * r * g_ref[...]).astype(BF16)

    return pl.pallas_call(
        body, name=name, grid=(t // tm,),
        in_specs=[pl.BlockSpec((tm, d), lambda i: (i, 0)), pl.BlockSpec((1, d), lambda i: (0, 0))],
        out_specs=pl.BlockSpec((tm, d), lambda i: (i, 0)),
        out_shape=jax.ShapeDtypeStruct((t, d), BF16),
        compiler_params=_params("parallel"),
    )(x, gain)


def _rmsnorm_bwd(x, gain, dy, dres, *, name):
    t, d = x.shape
    tm = _tile(t, 512)

    def body(x_ref, g_ref, dy_ref, dres_ref, dx_ref, dg_ref):
        xv = x_ref[...]
        r = lax.rsqrt(jnp.mean(xv * xv, axis=-1, keepdims=True) + NORM_EPS)
        xh = xv * r
        dyv = dy_ref[...]
        dxh = dyv * g_ref[...]
        mean = jnp.mean(dxh * xh, axis=-1, keepdims=True)
        dx_ref[...] = dres_ref[...] + r * (dxh - xh * mean)
        part = jnp.sum(dyv * xh, axis=0, keepdims=True)

        @pl.when(pl.program_id(0) == 0)
        def _():
            dg_ref[...] = part

        @pl.when(pl.program_id(0) > 0)
        def _():
            dg_ref[...] += part

    row = pl.BlockSpec((tm, d), lambda i: (i, 0))
    vec = pl.BlockSpec((1, d), lambda i: (0, 0))
    return pl.pallas_call(
        body, name=name, grid=(t // tm,),
        in_specs=[row, vec, row, row], out_specs=[row, vec],
        out_shape=[jax.ShapeDtypeStruct((t, d), F32), jax.ShapeDtypeStruct((1, d), F32)],
        compiler_params=_params("arbitrary"),
    )(x, gain, dy, dres)


def _loss_head(y, target):
    t, d = y.shape
    tm = _tile(t, 512)
    steps = t // tm

    def body(y_ref, t_ref, dy_ref, l_ref, acc_ref):
        e = y_ref[...] - t_ref[...]
        dy_ref[...] = e * (1.0 / d)
        part = jnp.sum(e * e, axis=0, keepdims=True)

        @pl.when(pl.program_id(0) == 0)
        def _():
            acc_ref[...] = part

        @pl.when(pl.program_id(0) > 0)
        def _():
            acc_ref[...] += part

        @pl.when(pl.program_id(0) == steps - 1)
        def _():
            l_ref[...] = jnp.full((1, LANES), (0.5 / d), F32) * jnp.sum(acc_ref[...])

    row = pl.BlockSpec((tm, d), lambda i: (i, 0))
    return pl.pallas_call(
        body, name="loss_head", grid=(steps,),
        in_specs=[row, row], out_specs=[row, pl.BlockSpec((1, LANES), lambda i: (0, 0))],
        out_shape=[jax.ShapeDtypeStruct((t, d), F32), jax.ShapeDtypeStruct((1, LANES), F32)],
        scratch_shapes=[pltpu.VMEM((1, d), F32)],
        compiler_params=_params("arbitrary"),
    )(y, target)


def _swiglu_fwd(h, w_gate_t, w_up_t, *, name):
    t, d = h.shape
    f = w_gate_t.shape[0]
    tm, tn = _tile(t, 1024), _tile(f, 256)

    def body(h_ref, wg_ref, wu_ref, g_ref, u_ref, a_ref):
        hv = h_ref[...]
        g = _dot_nt(hv, wg_ref[...])
        u = _dot_nt(hv, wu_ref[...])
        g_ref[...] = g.astype(BF16)
        u_ref[...] = u.astype(BF16)
        a_ref[...] = (g * _sigmoid(g) * u).astype(BF16)

    wspec = pl.BlockSpec((tn, d), lambda i, j: (j, 0))
    ospec = pl.BlockSpec((tm, tn), lambda i, j: (i, j))
    return pl.pallas_call(
        body, name=name, grid=(t // tm, f // tn),
        in_specs=[pl.BlockSpec((tm, d), lambda i, j: (i, 0)), wspec, wspec],
        out_specs=[ospec, ospec, ospec],
        out_shape=[jax.ShapeDtypeStruct((t, f), BF16), jax.ShapeDtypeStruct((t, f), BF16),
                   jax.ShapeDtypeStruct((t, f), BF16)],
        compiler_params=_params("parallel", "parallel"),
    )(h, w_gate_t, w_up_t)


def _swiglu_bwd(dx, w_down, g, u, *, name):
    t, d = dx.shape
    f = w_down.shape[0]
    tm, tn = _tile(t, 1024), _tile(f, 256)

    def body(dx_ref, wd_ref, g_ref, u_ref, dg_ref, du_ref):
        dact = _dot_nt(dx_ref[...].astype(BF16), wd_ref[...])
        gv, uv = g_ref[...].astype(F32), u_ref[...].astype(F32)
        sg = _sigmoid(gv)
        dg_ref[...] = (dact * uv * sg * (1.0 + gv * (1.0 - sg))).astype(BF16)
        du_ref[...] = (dact * gv * sg).astype(BF16)

    ospec = pl.BlockSpec((tm, tn), lambda i, j: (i, j))
    return pl.pallas_call(
        body, name=name, grid=(t // tm, f // tn),
        in_specs=[pl.BlockSpec((tm, d), lambda i, j: (i, 0)), pl.BlockSpec((tn, d), lambda i, j: (j, 0)),
                  ospec, ospec],
        out_specs=[ospec, ospec],
        out_shape=[jax.ShapeDtypeStruct((t, f), BF16), jax.ShapeDtypeStruct((t, f), BF16)],
        compiler_params=_params("parallel", "parallel"),
    )(dx, w_down, g, u)


def _ple_fwd(x, p, w_gate, w_proj_t, *, name):
    t, d = x.shape
    e = p.shape[1]
    tm, tn = _tile(t, 1024), _tile(d, 512)

    def body(xf_ref, xr_ref, p_ref, wg_ref, wp_ref, o_ref):
        s = _dot(xf_ref[...].astype(BF16), wg_ref[...])
        ple = _dot_nt(p_ref[...].astype(BF16), wp_ref[...])
        o_ref[...] = xr_ref[...] + _sigmoid(s) * ple

    return pl.pallas_call(
        body, name=name, grid=(t // tm, d // tn),
        in_specs=[pl.BlockSpec((tm, d), lambda i, j: (i, 0)), pl.BlockSpec((tm, tn), lambda i, j: (i, j)),
                  pl.BlockSpec((tm, e), lambda i, j: (i, 0)), pl.BlockSpec((d, tn), lambda i, j: (0, j)),
                  pl.BlockSpec((tn, e), lambda i, j: (j, 0))],
        out_specs=pl.BlockSpec((tm, tn), lambda i, j: (i, j)),
        out_shape=jax.ShapeDtypeStruct((t, d), F32),
        compiler_params=_params("parallel", "parallel"),
    )(x, x, p, w_gate, w_proj_t)


def _ple_bwd(x, p, w_gate, w_proj_t, dout, *, name):
    t, d = x.shape
    e = p.shape[1]
    tm, tn = _tile(t, 1024), _tile(d, 512)

    def body(xf_ref, p_ref, wg_ref, wp_ref, do_ref, ds_ref, dple_ref):
        s = _dot(xf_ref[...].astype(BF16), wg_ref[...])
        ple = _dot_nt(p_ref[...].astype(BF16), wp_ref[...])
        gate = _sigmoid(s)
        dov = do_ref[...]
        dple_ref[...] = (dov * gate).astype(BF16)
        ds_ref[...] = (dov * ple * gate * (1.0 - gate)).astype(BF16)

    ospec = pl.BlockSpec((tm, tn), lambda i, j: (i, j))
    return pl.pallas_call(
        body, name=name, grid=(t // tm, d // tn),
        in_specs=[pl.BlockSpec((tm, d), lambda i, j: (i, 0)), pl.BlockSpec((tm, e), lambda i, j: (i, 0)),
                  pl.BlockSpec((d, tn), lambda i, j: (0, j)), pl.BlockSpec((tn, e), lambda i, j: (j, 0)), ospec],
        out_specs=[ospec, ospec],
        out_shape=[jax.ShapeDtypeStruct((t, d), BF16), jax.ShapeDtypeStruct((t, d), BF16)],
        compiler_params=_params("parallel", "parallel"),
    )(x, p, w_gate, w_proj_t, dout)


CONV_TIME_TILE = 256
CONV_HALO = 8


def _conv_taps(ext, w):
    acc = ext[CONV_HALO:, :] * w[CONV_WIDTH - 1:CONV_WIDTH, :]
    shifted = [ext[CONV_HALO:, :]]
    for j in range(1, CONV_WIDTH):
        sh = pltpu.roll(ext, j, 0)[CONV_HALO:, :]
        shifted.append(sh)
        acc = acc + sh * w[CONV_WIDTH - 1 - j:CONV_WIDTH - j, :]
    return acc, shifted


def _conv_fwd(u, w, b):
    t, c = u.shape
    tc = _tile(c, 256)
    tt = CONV_TIME_TILE

    def body(u_ref, w_ref, b_ref, o_ref):
        wv, bv = w_ref[...], b_ref[...]

        def tile(start, ext):
            pre = _conv_taps(ext, wv)[0] + bv
            o_ref[pl.ds(start, tt), :] = pre * _sigmoid(pre)

        tile(0, jnp.concatenate([jnp.zeros((CONV_HALO, tc), F32), u_ref[0:tt, :]], axis=0))

        def loop(i, carry):
            start = pl.multiple_of(i * tt, tt)
            tile(start, u_ref[pl.ds(start - CONV_HALO, tt + CONV_HALO), :])
            return carry

        lax.fori_loop(1, t // tt, loop, 0)

    col = pl.BlockSpec((t, tc), lambda j: (0, j))
    return pl.pallas_call(
        body, name="conv_fwd", grid=(c // tc,),
        in_specs=[col, pl.BlockSpec((CONV_WIDTH, tc), lambda j: (0, j)), pl.BlockSpec((1, tc), lambda j: (0, j))],
        out_specs=col, out_shape=jax.ShapeDtypeStruct((t, c), F32),
        compiler_params=_params("parallel"),
    )(u, w, b)


def _conv_bwd(u, w, b, dact):
    t, c = u.shape
    tc = _tile(c, 256)
    tt = CONV_TIME_TILE

    def body(u_ref, w_ref, b_ref, da_ref, du_ref, dw_ref, db_ref, dpre_ref):
        wv, bv = w_ref[...], b_ref[...]

        def tile(start, ext, sums):
            acc, shifted = _conv_taps(ext, wv)
            pre = acc + bv
            sg = _sigmoid(pre)
            dpre = da_ref[pl.ds(start, tt), :] * (sg * (1.0 + pre * (1.0 - sg)))
            dpre_ref[pl.ds(start, tt), :] = dpre
            new = [sums[0] + jnp.sum(dpre, axis=0, keepdims=True)]
            for j in range(CONV_WIDTH):
                new.append(sums[1 + j] + jnp.sum(dpre * shifted[j], axis=0, keepdims=True))
            return tuple(new)

        zero = jnp.zeros((1, tc), F32)
        sums = tile(0, jnp.concatenate([jnp.zeros((CONV_HALO, tc), F32), u_ref[0:tt, :]], axis=0),
                    (zero,) * (1 + CONV_WIDTH))

        def loop(i, sums):
            start = pl.multiple_of(i * tt, tt)
            return tile(start, u_ref[pl.ds(start - CONV_HALO, tt + CONV_HALO), :], sums)

        sums = lax.fori_loop(1, t // tt, loop, sums)
        db_ref[...] = sums[0]
        dw_ref[...] = jnp.concatenate([sums[1 + (CONV_WIDTH - 1 - k)] for k in range(CONV_WIDTH)], axis=0)
        dpre_ref[pl.ds(t, CONV_HALO), :] = jnp.zeros((CONV_HALO, tc), F32)

        def loop2(i, carry):
            start = pl.multiple_of(i * tt, tt)
            ext = dpre_ref[pl.ds(start, tt + CONV_HALO), :]
            acc = ext[0:tt, :] * wv[CONV_WIDTH - 1:CONV_WIDTH, :]
            for j in range(1, CONV_WIDTH):
                acc = acc + pltpu.roll(ext, tt + CONV_HALO - j, 0)[0:tt, :] * wv[CONV_WIDTH - 1 - j:CONV_WIDTH - j, :]
            du_ref[pl.ds(start, tt), :] = acc.astype(BF16)
            return carry

        lax.fori_loop(0, t // tt, loop2, 0)

    col = pl.BlockSpec((t, tc), lambda j: (0, j))
    return pl.pallas_call(
        body, name="conv_bwd", grid=(c // tc,),
        in_specs=[col, pl.BlockSpec((CONV_WIDTH, tc), lambda j: (0, j)), pl.BlockSpec((1, tc), lambda j: (0, j)), col],
        out_specs=[col, pl.BlockSpec((CONV_WIDTH, tc), lambda j: (0, j)), pl.BlockSpec((1, tc), lambda j: (0, j))],
        out_shape=[jax.ShapeDtypeStruct((t, c), BF16), jax.ShapeDtypeStruct((CONV_WIDTH, c), F32),
                   jax.ShapeDtypeStruct((1, c), F32)],
        scratch_shapes=[pltpu.VMEM((t + CONV_HALO, tc), F32)],
        compiler_params=_params("parallel"),
    )(u, w, b, dact)


def _softplus(v):
    e = jnp.exp(-jnp.abs(v))
    w = 1.0 + e
    log1p = jnp.where(w == 1.0, e, jnp.log(w) * (e / jnp.where(w == 1.0, 1.0, w - 1.0)))
    return jnp.maximum(v, 0.0) + log1p


def _ssd_prep_fwd(dt_raw, dt_bias, a_log):
    t = dt_raw.shape[0]
    cl = SSD_CHUNK

    def body(r_ref, b_ref, al_ref, dt_ref, acs_ref):
        dt = _softplus(r_ref[...] + b_ref[...])
        adt = dt * (-jnp.exp(al_ref[...]))
        li = lax.broadcasted_iota(jnp.int32, (cl, cl), 0)
        si = lax.broadcasted_iota(jnp.int32, (cl, cl), 1)
        tri = (si <= li).astype(F32)
        dt_ref[...] = dt
        acs_ref[...] = jnp.dot(tri, adt, preferred_element_type=F32, precision=HIGHEST)

    row = pl.BlockSpec((cl, LANES), lambda i: (i, 0))
    vec = pl.BlockSpec((1, LANES), lambda i: (0, 0))
    return pl.pallas_call(
        body, name="ssd_prep_fwd", grid=(t // cl,),
        in_specs=[row, vec, vec], out_specs=[row, row],
        out_shape=[jax.ShapeDtypeStruct((t, LANES), F32), jax.ShapeDtypeStruct((t, LANES), F32)],
        compiler_params=_params("parallel"),
    )(dt_raw, dt_bias, a_log)


def _ssd_prep_bwd(dt_raw, dt_bias, ddt):
    t = dt_raw.shape[0]
    tm = _tile(t, 512)

    def body(r_ref, b_ref, d_ref, o_ref, db_ref):
        g = d_ref[...] * _sigmoid(r_ref[...] + b_ref[...])
        o_ref[...] = g.astype(BF16)
        part = jnp.sum(g, axis=0, keepdims=True)

        @pl.when(pl.program_id(0) == 0)
        def _():
            db_ref[...] = part

        @pl.when(pl.program_id(0) > 0)
        def _():
            db_ref[...] += part

    row = pl.BlockSpec((tm, LANES), lambda i: (i, 0))
    vec = pl.BlockSpec((1, LANES), lambda i: (0, 0))
    return pl.pallas_call(
        body, name="ssd_prep_bwd", grid=(t // tm,),
        in_specs=[row, vec, row], out_specs=[row, vec],
        out_shape=[jax.ShapeDtypeStruct((t, LANES), BF16), jax.ShapeDtypeStruct((1, LANES), F32)],
        compiler_params=_params("arbitrary"),
    )(dt_raw, dt_bias, ddt)


GROUP_W = D_INNER // SSM_GROUPS
PAIRS_PER_GROUP = GROUP_W // LANES


def _head_cols(acs_pair, lt64):
    rolled = pltpu.roll(acs_pair, ATT_HEAD_DIM, 1)
    return jnp.where(lt64, acs_pair, rolled), jnp.where(lt64, rolled, acs_pair)


def _ssd_fwd(xbc, dt_rep, acs_rep, acs_t, dskip_rep):
    t = xbc.shape[0]
    cl = SSD_CHUNK
    nc = t // cl

    def body(xbc_ref, dt_ref, acs_ref, acst_ref, dskip_ref, y_ref, hin_ref, state_ref):
        @pl.when(pl.program_id(0) == 0)
        def _():
            state_ref[...] = jnp.zeros_like(state_ref)

        lt64 = _lane_lt64(cl)
        li = lax.broadcasted_iota(jnp.int32, (cl, cl), 0)
        si = lax.broadcasted_iota(jnp.int32, (cl, cl), 1)
        causal = li >= si
        hin_ref[...] = state_ref[...]
        for g in range(SSM_GROUPS):
            gsl = slice(g * GROUP_W, (g + 1) * GROUP_W)
            xg = xbc_ref[:, gsl]
            bg = xbc_ref[:, D_INNER + g * SSM_STATE:D_INNER + (g + 1) * SSM_STATE]
            cg = xbc_ref[:, D_INNER + SSM_GROUPS * SSM_STATE + g * SSM_STATE:
                         D_INNER + SSM_GROUPS * SSM_STATE + (g + 1) * SSM_STATE]
            acs = acs_ref[:, gsl]
            xdt = xg * dt_ref[:, gsl]
            atot = acs[cl - 1:cl, :]
            hin = state_ref[:, gsl]
            cgb = cg.astype(BF16)
            gmat = _dot_nt(cgb, bg.astype(BF16))
            yoff = _dot(cgb, hin.astype(BF16)) * jnp.exp(acs)
            snew = _dot(bg.T.astype(BF16), (xdt * jnp.exp(atot - acs)).astype(BF16))
            state_ref[:, gsl] = hin * jnp.exp(atot) + snew
            xdtb = xdt.astype(BF16)
            for pr in range(PAIRS_PER_GROUP):
                psl = slice(pr * LANES, (pr + 1) * LANES)
                cols = _head_cols(acs[:, psl], lt64)
                xp = xdtb[:, psl]
                ys = []
                for hh in range(2):
                    h = (g * PAIRS_PER_GROUP + pr) * 2 + hh
                    seg = cols[hh] - acst_ref[h:h + 1, :]
                    lm = jnp.exp(jnp.where(causal, seg, NEG_BIG))
                    ys.append(_dot((gmat * lm).astype(BF16), xp))
                ydiag = jnp.where(lt64, ys[0], ys[1])
                osl = slice(g * GROUP_W + pr * LANES, g * GROUP_W + (pr + 1) * LANES)
                y_ref[:, osl] = ydiag + yoff[:, psl] + xg[:, psl] * dskip_ref[:, osl]

    row = lambda w: pl.BlockSpec((cl, w), lambda c: (c, 0))
    return pl.pallas_call(
        body, name="ssd_fwd", grid=(nc,),
        in_specs=[row(CONV_DIM), row(D_INNER), row(D_INNER),
                  pl.BlockSpec((SSM_HEADS, cl), lambda c: (0, c)), pl.BlockSpec((1, D_INNER), lambda c: (0, 0))],
        out_specs=[row(D_INNER), pl.BlockSpec((None, SSM_STATE, D_INNER), lambda c: (c, 0, 0))],
        out_shape=[jax.ShapeDtypeStruct((t, D_INNER), F32), jax.ShapeDtypeStruct((nc, SSM_STATE, D_INNER), F32)],
        scratch_shapes=[pltpu.VMEM((SSM_STATE, D_INNER), F32)],
        compiler_params=_params("arbitrary"),
    )(xbc, dt_rep, acs_rep, acs_t, dskip_rep)


def _ssd_bwd(xbc, dt_rep, acs_rep, acs_t, dskip_rep, a_rep, hin_all, dy):
    t = xbc.shape[0]
    cl = SSD_CHUNK
    nc = t // cl

    def body(xbc_ref, dt_ref, acs_ref, acst_ref, dskip_ref, a_ref, hin_ref, dy_ref,
             dxbc_ref, ddt_ref, da_ref, dds_ref, dstate_ref, dacs_ref, dxs_ref):
        step = pl.program_id(0)

        @pl.when(step == 0)
        def _():
            dstate_ref[...] = jnp.zeros_like(dstate_ref)
            da_ref[...] = jnp.zeros_like(da_ref)
            dds_ref[...] = jnp.zeros_like(dds_ref)

        bd = _head_block_diag()
        lt64 = _lane_lt64(cl)
        li = lax.broadcasted_iota(jnp.int32, (cl, cl), 0)
        si = lax.broadcasted_iota(jnp.int32, (cl, cl), 1)
        lower = li >= si
        upper = si >= li
        last_row = lax.broadcasted_iota(jnp.int32, (cl, GROUP_W), 0) == cl - 1
        for g in range(SSM_GROUPS):
            gsl = slice(g * GROUP_W, (g + 1) * GROUP_W)
            bsl = slice(D_INNER + g * SSM_STATE, D_INNER + (g + 1) * SSM_STATE)
            csl = slice(D_INNER + SSM_GROUPS * SSM_STATE + g * SSM_STATE,
                        D_INNER + SSM_GROUPS * SSM_STATE + (g + 1) * SSM_STATE)
            xg = xbc_ref[:, gsl]
            bg = xbc_ref[:, bsl]
            cg = xbc_ref[:, csl]
            bgb, cgb = bg.astype(BF16), cg.astype(BF16)
            acs = acs_ref[:, gsl]
            xdt = xg * dt_ref[:, gsl]
            atot = acs[cl - 1:cl, :]
            eg = jnp.exp(acs)
            dk = jnp.exp(atot - acs)
            etot = jnp.exp(atot)
            hin = hin_ref[:, gsl]
            hinb = hin.astype(BF16)
            dh = dstate_ref[:, gsl]
            dhb = dh.astype(BF16)
            dyg = dy_ref[:, gsl]

            gmat = _dot_nt(cgb, bgb)
            gmat_t = _dot_nt(bgb, cgb)
            ch = _dot(cgb, hinb)
            dacs = _head_sums(dyg * ch * eg, bd)
            dye = (dyg * eg).astype(BF16)
            dc = _dot_nt(dye, hinb)
            dhin = _dot(cg.T.astype(BF16), dye)
            bdh = _dot(bgb, dhb)
            dxs = bdh * dk
            xdk = xdt * dk
            db = _dot_nt(xdk.astype(BF16), dhb)
            ddk = _head_sums(bdh * xdk, bd)
            dacs = dacs - ddk
            datot = jnp.sum(ddk, axis=0, keepdims=True) + etot * _head_sums(
                jnp.sum(dh * hin, axis=0, keepdims=True), bd)
            dacs = dacs + jnp.where(last_row, datot, 0.0)
            dstate_ref[:, gsl] = dh * etot + dhin

            xdtb = xdt.astype(BF16)
            dgsum = jnp.zeros((cl, cl), F32)
            dgsum_t = jnp.zeros((cl, cl), F32)
            for pr in range(PAIRS_PER_GROUP):
                psl = slice(pr * LANES, (pr + 1) * LANES)
                cols = _head_cols(acs[:, psl], lt64)
                xp = xdtb[:, psl]
                dyp = dyg[:, psl].astype(BF16)
                dx1, dac = [], []
                for hh in range(2):
                    h = (g * PAIRS_PER_GROUP + pr) * 2 + hh
                    mine = lt64 if hh == 0 else jnp.logical_not(lt64)
                    row = acst_ref[h:h + 1, :]
                    lm = jnp.exp(jnp.where(lower, cols[hh] - row, NEG_BIG))
                    lm_t = jnp.exp(jnp.where(upper, row - cols[hh], NEG_BIG))
                    dyh = jnp.where(mine, dyp, jnp.zeros_like(dyp))
                    xh = jnp.where(mine, xp, jnp.zeros_like(xp))
                    dm = _dot_nt(dyh, xp)
                    dm_t = _dot_nt(xh, dyp)
                    m_t = gmat_t * lm_t
                    dx1.append(_dot(m_t.astype(BF16), dyp))
                    w = dm * (gmat * lm)
                    w_t = dm_t * m_t
                    dac.append(jnp.sum(w, axis=1, keepdims=True) - jnp.sum(w_t, axis=1, keepdims=True))
                    dgsum = dgsum + dm * lm
                    dgsum_t = dgsum_t + dm_t * lm_t
                osl = slice(g * GROUP_W + pr * LANES, g * GROUP_W + (pr + 1) * LANES)
                dxs_ref[:, osl] = dxs[:, psl] + jnp.where(lt64, dx1[0], dx1[1])
                dacs_ref[:, osl] = dacs[:, psl] + jnp.where(lt64, jnp.broadcast_to(dac[0], (cl, LANES)),
                                                             jnp.broadcast_to(dac[1], (cl, LANES)))
            dxbc_ref[:, csl] = dc + _dot(dgsum.astype(BF16), bgb)
            dxbc_ref[:, bsl] = db + _dot(dgsum_t.astype(BF16), cgb)

        dadt = _split_dot(upper.astype(BF16), dacs_ref[...])
        xall = xbc_ref[:, 0:D_INNER]
        dtall = dt_ref[...]
        dxsall = dxs_ref[...]
        dyall = dy_ref[...]
        ddt_ref[...] = dadt * a_ref[...] + _head_sums(dxsall * xall, bd)
        dxbc_ref[:, 0:D_INNER] = dxsall * dtall + dyall * dskip_ref[...]
        da_ref[...] += jnp.sum(dadt * dtall, axis=0, keepdims=True)
        dds_ref[...] += jnp.sum(dyall * xall, axis=0, keepdims=True)

        @pl.when(step == nc - 1)
        def _():
            dds_ref[...] = _head_sums(dds_ref[...], bd)

    row = lambda w: pl.BlockSpec((cl, w), lambda c: (nc - 1 - c, 0))
    vec = pl.BlockSpec((1, D_INNER), lambda c: (0, 0))
    return pl.pallas_call(
        body, name="ssd_bwd", grid=(nc,),
        in_specs=[row(CONV_DIM), row(D_INNER), row(D_INNER),
                  pl.BlockSpec((SSM_HEADS, cl), lambda c: (0, nc - 1 - c)), vec, vec,
                  pl.BlockSpec((None, SSM_STATE, D_INNER), lambda c: (nc - 1 - c, 0, 0)), row(D_INNER)],
        out_specs=[row(CONV_DIM), row(D_INNER), vec, vec],
        out_shape=[jax.ShapeDtypeStruct((t, CONV_DIM), F32), jax.ShapeDtypeStruct((t, D_INNER), F32),
                   jax.ShapeDtypeStruct((1, D_INNER), F32), jax.ShapeDtypeStruct((1, D_INNER), F32)],
        scratch_shapes=[pltpu.VMEM((SSM_STATE, D_INNER), F32), pltpu.VMEM((cl, D_INNER), F32),
                        pltpu.VMEM((cl, D_INNER), F32)],
        compiler_params=_params("arbitrary"),
    )(xbc, dt_rep, acs_rep, acs_t, dskip_rep, a_rep, hin_all, dy)


def _gate_norm_fwd(y, z, w):
    t, c = y.shape
    tm = _tile(t, 256)

    def body(y_ref, z_ref, w_ref, o_ref):
        for g in range(SSM_GROUPS):
            gsl = slice(g * GROUP_W, (g + 1) * GROUP_W)
            zv = z_ref[:, gsl]
            v = y_ref[:, gsl] * (zv * _sigmoid(zv))
            r = lax.rsqrt(jnp.mean(v * v, axis=-1, keepdims=True) + NORM_EPS)
            o_ref[:, gsl] = (v * r * w_ref[:, gsl]).astype(BF16)

    row = pl.BlockSpec((tm, c), lambda i: (i, 0))
    return pl.pallas_call(
        body, name="gate_norm_fwd", grid=(t // tm,),
        in_specs=[row, row, pl.BlockSpec((1, c), lambda i: (0, 0))], out_specs=row,
        out_shape=jax.ShapeDtypeStruct((t, c), BF16),
        compiler_params=_params("parallel"),
    )(y, z, w)


def _gate_norm_bwd(y, z, w, dout):
    t, c = y.shape
    tm = _tile(t, 256)

    def body(y_ref, z_ref, w_ref, do_ref, dy_ref, dz_ref, dw_ref):
        @pl.when(pl.program_id(0) == 0)
        def _():
            dw_ref[...] = jnp.zeros_like(dw_ref)

        for g in range(SSM_GROUPS):
            gsl = slice(g * GROUP_W, (g + 1) * GROUP_W)
            zv, yv, dov = z_ref[:, gsl], y_ref[:, gsl], do_ref[:, gsl]
            sg = _sigmoid(zv)
            sz = zv * sg
            v = yv * sz
            r = lax.rsqrt(jnp.mean(v * v, axis=-1, keepdims=True) + NORM_EPS)
            vh = v * r
            dvh = dov * w_ref[:, gsl]
            mean = jnp.mean(dvh * vh, axis=-1, keepdims=True)
            dv = r * (dvh - vh * mean)
            dy_ref[:, gsl] = dv * sz
            dz_ref[:, gsl] = (dv * yv * (sg * (1.0 + zv * (1.0 - sg)))).astype(BF16)
            dw_ref[:, gsl] += jnp.sum(dov * vh, axis=0, keepdims=True)

    row = pl.BlockSpec((tm, c), lambda i: (i, 0))
    vec = pl.BlockSpec((1, c), lambda i: (0, 0))
    return pl.pallas_call(
        body, name="gate_norm_bwd", grid=(t // tm,),
        in_specs=[row, row, vec, row], out_specs=[row, row, vec],
        out_shape=[jax.ShapeDtypeStruct((t, c), F32), jax.ShapeDtypeStruct((t, c), BF16),
                   jax.ShapeDtypeStruct((1, c), F32)],
        compiler_params=_params("arbitrary"),
    )(y, z, w, dout)


ATT_W = ATT_HEADS * ATT_HEAD_DIM
N_QKV_BLOCKS = 9
ATT_SCALE = 1.0 / math.sqrt(ATT_HEAD_DIM)


def _head_rmsnorm(x, gain, bd):
    ms = _head_sums(x * x, bd) * (1.0 / ATT_HEAD_DIM)
    return x * lax.rsqrt(ms + NORM_EPS) * gain


def _class_rows(ref, blk, r, dil):
    span = ATT_BLOCK * dil
    sub = ref.at[pl.ds(pl.multiple_of(blk * span, span), span), :]
    return sub[...] if dil == 1 else sub[pl.ds(r, ATT_BLOCK, stride=dil), :]


def _store_class_rows(ref, blk, r, dil, val):
    span = ATT_BLOCK * dil
    sub = ref.at[pl.ds(pl.multiple_of(blk * span, span), span), :]
    if dil == 1:
        sub[...] = val
    else:
        sub[pl.ds(r, ATT_BLOCK, stride=dil), :] = val


def _qk_norm_bwd(qkv, gq, gk, grads):
    t = qkv.shape[0]
    tm = _tile(t, 256)

    def body(x_ref, gq_ref, gk_ref, *rest):
        g_refs = rest[:N_QKV_BLOCKS]
        o_ref, dgq_ref, dgk_ref = rest[N_QKV_BLOCKS:]
        cb = pl.program_id(1)

        @pl.when(jnp.logical_and(pl.program_id(0) == 0, cb == 0))
        def _():
            dgq_ref[...] = jnp.zeros_like(dgq_ref)
            dgk_ref[...] = jnp.zeros_like(dgk_ref)

        def norm_bwd(dy, gain, dg_ref):
            bd = _head_block_diag()
            xv = x_ref[...]
            ms = _head_sums(xv * xv, bd) * (1.0 / ATT_HEAD_DIM)
            r = lax.rsqrt(ms + NORM_EPS)
            xh = xv * r
            dxh = dy * gain
            mean = _head_sums(dxh * xh, bd) * (1.0 / ATT_HEAD_DIM)
            o_ref[...] = (r * (dxh - xh * mean)).astype(BF16)
            dg_ref[...] += jnp.sum(dy * xh, axis=0, keepdims=True)

        for k in range(N_QKV_BLOCKS):
            @pl.when(cb == k)
            def _(k=k):
                if k % 3 == 0:
                    norm_bwd(g_refs[k][...], gq_ref[...], dgq_ref)
                elif k % 3 == 1:
                    norm_bwd(g_refs[k][...], gk_ref[...], dgk_ref)
                else:
                    o_ref[...] = g_refs[k][...].astype(BF16)

    blk = pl.BlockSpec((tm, ATT_W), lambda i, j: (i, j))
    one = pl.BlockSpec((tm, ATT_W), lambda i, j: (i, 0))
    vec = pl.BlockSpec((1, ATT_W), lambda i, j: (0, 0))
    return pl.pallas_call(
        body, name="qk_norm_bwd", grid=(t // tm, N_QKV_BLOCKS),
        in_specs=[blk, vec, vec] + [one] * N_QKV_BLOCKS, out_specs=[blk, vec, vec],
        out_shape=[jax.ShapeDtypeStruct(qkv.shape, BF16), jax.ShapeDtypeStruct((1, ATT_W), F32),
                   jax.ShapeDtypeStruct((1, ATT_W), F32)],
        compiler_params=_params("arbitrary", "arbitrary"),
    )(qkv, gq, gk, *grads)


PAIRS = ATT_HEADS // 2


def _pair_col(g, j):
    return lambda pair: (0, (g * 3 + j) * PAIRS + pair)


def _pair_slopes(pair):
    steps = jnp.full((1, 2 * ATT_BLOCK), 2 * pair + 1, jnp.int32).astype(F32)
    first = jnp.exp(steps * (-0.5 * math.log(2.0)))
    return first, first * (2.0 ** -0.5)


NORM_ROWS = 512


def _band2(pair, dil, transposed):
    bq = ATT_BLOCK
    a = lax.broadcasted_iota(jnp.int32, (2 * bq, 2 * bq), 0) % bq
    b = lax.broadcasted_iota(jnp.int32, (2 * bq, 2 * bq), 1)
    dist = (b - a) if transposed else (a + bq - b)
    in_band = (dist >= 0) & (dist <= bq)
    s0, s1 = _pair_slopes(pair)
    first_head = lax.broadcasted_iota(jnp.int32, (2 * bq, 2 * bq), 0) < bq
    bias = jnp.where(first_head, s0, s1) * (dist.astype(F32) * float(dil))
    return in_band, bias, b


def _stack_heads(tile):
    rows = lax.broadcasted_iota(jnp.int32, (2 * ATT_BLOCK, LANES), 0) < ATT_BLOCK
    lanes = lax.broadcasted_iota(jnp.int32, (2 * ATT_BLOCK, LANES), 1) < ATT_HEAD_DIM
    both = jnp.concatenate([tile, tile], axis=0)
    return jnp.where(rows == lanes, both, jnp.zeros_like(both))


def _unstack_heads(stacked, lt64):
    return jnp.where(lt64, stacked[:ATT_BLOCK], stacked[ATT_BLOCK:])


def _normalise_qk(q_ref, k_ref, gq_ref, gk_ref, qn_ref, kn_ref):
    bd = _head_block_diag()

    def step(i, carry):
        rows = pl.ds(pl.multiple_of(i * NORM_ROWS, NORM_ROWS), NORM_ROWS)
        qn_ref[rows, :] = _head_rmsnorm(q_ref[rows, :], gq_ref[...], bd)
        kn_ref[rows, :] = _head_rmsnorm(k_ref[rows, :], gk_ref[...], bd)
        return carry

    lax.fori_loop(0, q_ref.shape[0] // NORM_ROWS, step, 0)


def _attn_fwd(qkv, gq, gk, g, dil):
    t = qkv.shape[0]
    nb = t // dil // ATT_BLOCK
    bq = ATT_BLOCK

    def body(q_ref, k_ref, v_ref, gq_ref, gk_ref, o_ref, l_ref, qn_ref, kn_ref):
        _normalise_qk(q_ref, k_ref, gq_ref, gk_ref, qn_ref, kn_ref)
        lt64 = _lane_lt64(bq)
        in_band, bias, key = _band2(pl.program_id(0), dil, False)

        def step(n, carry):
            valid = in_band & ((key >= bq) | (n > 0))
            prev = jnp.maximum(n - 1, 0)
            for r in range(dil):
                q2 = _stack_heads(_class_rows(qn_ref, n, r, dil).astype(BF16))
                kcat = jnp.concatenate([_class_rows(kn_ref, prev, r, dil), _class_rows(kn_ref, n, r, dil)],
                                       axis=0).astype(BF16)
                vcat = jnp.concatenate([_class_rows(v_ref, prev, r, dil), _class_rows(v_ref, n, r, dil)],
                                       axis=0).astype(BF16)
                s = jnp.where(valid, _dot_nt(q2, kcat) * ATT_SCALE - bias, NEG_BIG)
                m = jnp.max(s, axis=1, keepdims=True)
                p = jnp.exp(s - m)
                l = jnp.sum(p, axis=1, keepdims=True)
                out = _dot(p.astype(BF16), vcat) * (1.0 / l)
                lse = jnp.broadcast_to(m + jnp.log(l), (2 * bq, LANES))
                _store_class_rows(o_ref, n, r, dil, _unstack_heads(out, lt64))
                _store_class_rows(l_ref, n, r, dil, _unstack_heads(lse, lt64))
            return carry

        lax.fori_loop(0, nb, step, 0)

    col = lambda j: pl.BlockSpec((t, LANES), _pair_col(g, j))
    vec = pl.BlockSpec((1, LANES), lambda pair: (0, 0))
    out = pl.BlockSpec((t, LANES), lambda pair: (0, pair))
    return pl.pallas_call(
        body, name=f"attn_fwd_g{g}", grid=(PAIRS,),
        in_specs=[col(0), col(1), col(2), vec, vec], out_specs=[out, out],
        out_shape=[jax.ShapeDtypeStruct((t, ATT_W), F32), jax.ShapeDtypeStruct((t, ATT_W), F32)],
        scratch_shapes=[pltpu.VMEM((t, LANES), F32), pltpu.VMEM((t, LANES), F32)],
        compiler_params=_params("parallel"),
    )(qkv, qkv, qkv, gq, gk)


def _attn_combine_fwd(outs, lses):
    t = outs[0].shape[0]
    tm = _tile(t, 256)

    def body(o0, o1, o2, l0, l1, l2, ob_ref, of_ref, lt_ref):
        a, b, c = l0[...], l1[...], l2[...]
        m = jnp.maximum(jnp.maximum(a, b), c)
        ea, eb, ec = jnp.exp(a - m), jnp.exp(b - m), jnp.exp(c - m)
        ssum = ea + eb + ec
        o = (ea * o0[...] + eb * o1[...] + ec * o2[...]) / ssum
        ob_ref[...] = o.astype(BF16)
        of_ref[...] = o
        lt_ref[...] = m + jnp.log(ssum)

    row = pl.BlockSpec((tm, ATT_W), lambda i: (i, 0))
    return pl.pallas_call(
        body, name="attn_combine_fwd", grid=(t // tm,),
        in_specs=[row] * 6, out_specs=[row] * 3,
        out_shape=[jax.ShapeDtypeStruct((t, ATT_W), BF16), jax.ShapeDtypeStruct((t, ATT_W), F32),
                   jax.ShapeDtypeStruct((t, ATT_W), F32)],
        compiler_params=_params("parallel"),
    )(*outs, *lses)


def _attn_combine_bwd(do, o):
    t = do.shape[0]
    tm = _tile(t, 256)

    def body(do_ref, o_ref, dl_ref):
        dl_ref[...] = _head_sums(do_ref[...] * o_ref[...], _head_block_diag())

    row = pl.BlockSpec((tm, ATT_W), lambda i: (i, 0))
    return pl.pallas_call(
        body, name="attn_combine_bwd", grid=(t // tm,),
        in_specs=[row, row], out_specs=row, out_shape=jax.ShapeDtypeStruct((t, ATT_W), F32),
        compiler_params=_params("parallel"),
    )(do, o)


def _attn_bwd_dq(qkv, gq, gk, do, l_rep, dl_rep, g, dil):
    t = qkv.shape[0]
    nb = t // dil // ATT_BLOCK
    bq = ATT_BLOCK

    def body(q_ref, k_ref, v_ref, gq_ref, gk_ref, do_ref, l_ref, dl_ref, dq_ref, qn_ref, kn_ref):
        _normalise_qk(q_ref, k_ref, gq_ref, gk_ref, qn_ref, kn_ref)
        lt64 = _lane_lt64(bq)
        in_band, bias, key = _band2(pl.program_id(0), dil, False)

        def per_row(tile):
            cols = _head_cols(tile, lt64)
            half = jnp.concatenate([cols[0], cols[1]], axis=0)
            return jnp.concatenate([half, half], axis=1)

        def step(n, carry):
            valid = in_band & ((key >= bq) | (n > 0))
            prev = jnp.maximum(n - 1, 0)
            for r in range(dil):
                q2 = _stack_heads(_class_rows(qn_ref, n, r, dil).astype(BF16))
                do2 = _stack_heads(_class_rows(do_ref, n, r, dil).astype(BF16))
                kcat = jnp.concatenate([_class_rows(kn_ref, prev, r, dil), _class_rows(kn_ref, n, r, dil)],
                                       axis=0).astype(BF16)
                vcat = jnp.concatenate([_class_rows(v_ref, prev, r, dil), _class_rows(v_ref, n, r, dil)],
                                       axis=0).astype(BF16)
                s = jnp.where(valid, _dot_nt(q2, kcat) * ATT_SCALE - bias, NEG_BIG)
                p = jnp.exp(s - per_row(_class_rows(l_ref, n, r, dil)))
                ds = p * (_dot_nt(do2, vcat) - per_row(_class_rows(dl_ref, n, r, dil)))
                dq = _dot(ds.astype(BF16), kcat) * ATT_SCALE
                _store_class_rows(dq_ref, n, r, dil, _unstack_heads(dq, lt64))
            return carry

        lax.fori_loop(0, nb, step, 0)

    col = lambda j: pl.BlockSpec((t, LANES), _pair_col(g, j))
    vec = pl.BlockSpec((1, LANES), lambda pair: (0, 0))
    tok = pl.BlockSpec((t, LANES), lambda pair: (0, pair))
    return pl.pallas_call(
        body, name=f"attn_bwd_dq_g{g}", grid=(PAIRS,),
        in_specs=[col(0), col(1), col(2), vec, vec, tok, tok, tok], out_specs=tok,
        out_shape=jax.ShapeDtypeStruct((t, ATT_W), F32),
        scratch_shapes=[pltpu.VMEM((t, LANES), F32), pltpu.VMEM((t, LANES), F32)],
        compiler_params=_params("parallel"),
    )(qkv, qkv, qkv, gq, gk, do, l_rep, dl_rep)


def _attn_bwd_dkv(qkv, gq, gk, do, l_row, dl_row, g, dil):
    t = qkv.shape[0]
    nb = t // dil // ATT_BLOCK
    bq = ATT_BLOCK

    def body(q_ref, k_ref, v_ref, gq_ref, gk_ref, do_ref, l_ref, dl_ref, dk_ref, dv_ref, qn_ref, kn_ref):
        _normalise_qk(q_ref, k_ref, gq_ref, gk_ref, qn_ref, kn_ref)
        lt64 = _lane_lt64(bq)
        in_band, bias, query = _band2(pl.program_id(0), dil, True)

        def per_query(ref, lane_c, lane_n):
            heads = [jnp.broadcast_to(jnp.concatenate([ref[hh:hh + 1, pl.ds(lane_c, bq)],
                                                        ref[hh:hh + 1, pl.ds(lane_n, bq)]], axis=1), (bq, 2 * bq))
                     for hh in range(2)]
            return jnp.concatenate(heads, axis=0)

        def step(n, carry):
            valid = in_band & ((query < bq) | (n < nb - 1))
            nxt = jnp.minimum(n + 1, nb - 1)
            for r in range(dil):
                k2 = _stack_heads(_class_rows(kn_ref, n, r, dil).astype(BF16))
                v2 = _stack_heads(_class_rows(v_ref, n, r, dil).astype(BF16))
                qcat = jnp.concatenate([_class_rows(qn_ref, n, r, dil), _class_rows(qn_ref, nxt, r, dil)],
                                       axis=0).astype(BF16)
                docat = jnp.concatenate([_class_rows(do_ref, n, r, dil), _class_rows(do_ref, nxt, r, dil)],
                                        axis=0).astype(BF16)
                lane_c = pl.multiple_of((r * nb + n) * bq, bq)
                lane_n = pl.multiple_of((r * nb + nxt) * bq, bq)
                s_t = jnp.where(valid, _dot_nt(k2, qcat) * ATT_SCALE - bias, NEG_BIG)
                p_t = jnp.exp(s_t - per_query(l_ref, lane_c, lane_n))
                dv = _dot(p_t.astype(BF16), docat)
                ds_t = p_t * (_dot_nt(v2, docat) - per_query(dl_ref, lane_c, lane_n))
                dk = _dot(ds_t.astype(BF16), qcat) * ATT_SCALE
                _store_class_rows(dk_ref, n, r, dil, _unstack_heads(dk, lt64))
                _store_class_rows(dv_ref, n, r, dil, _unstack_heads(dv, lt64))
            return carry

        lax.fori_loop(0, nb, step, 0)

    col = lambda j: pl.BlockSpec((t, LANES), _pair_col(g, j))
    vec = pl.BlockSpec((1, LANES), lambda pair: (0, 0))
    tok = pl.BlockSpec((t, LANES), lambda pair: (0, pair))
    rows = pl.BlockSpec((None, 8, t), lambda pair: (pair, 0, 0))
    return pl.pallas_call(
        body, name=f"attn_bwd_dkv_g{g}", grid=(PAIRS,),
        in_specs=[col(0), col(1), col(2), vec, vec, tok, rows, rows], out_specs=[tok, tok],
        out_shape=[jax.ShapeDtypeStruct((t, ATT_W), F32), jax.ShapeDtypeStruct((t, ATT_W), F32)],
        scratch_shapes=[pltpu.VMEM((t, LANES), F32), pltpu.VMEM((t, LANES), F32)],
        compiler_params=_params("parallel"),
    )(qkv, qkv, qkv, gq, gk, do, l_row, dl_row)


def _rows_by_residue(rep, dil):
    t = rep.shape[0]
    per_head = rep[:, ::ATT_HEAD_DIM]
    rows = per_head.reshape(t // dil, dil, ATT_HEADS).transpose(2, 1, 0).reshape(PAIRS, 2, t)
    return jnp.pad(rows, ((0, 0), (0, 6), (0, 0)))


def _per_head(rep_row):
    return rep_row[0, ::SSM_HEAD_DIM]


def _rep_heads(v):
    return jnp.repeat(v, SSM_HEAD_DIM)[None, :]


def _pad_lanes(v):
    return jnp.pad(v, ((0, 0), (0, LANES - v.shape[1])))


def _ffn_ple_fwd(x1, p_i, prm, i):
    h = _rmsnorm_fwd(x1, prm["norm_ffn"][i:i + 1], name=f"ffn_norm_fwd_{i}")
    g, u, act = _swiglu_fwd(h, prm["ffn_w_gate"][i], prm["ffn_w_up"][i], name=f"swiglu_fwd_{i}")
    x2 = _matmul(act, prm["ffn_w_down"][i], mode="nn", addend=x1, name=f"ffn_down_{i}")
    x3 = _ple_fwd(x2, p_i, prm["ple_w_gate"][i], prm["ple_w_proj"][i], name=f"ple_fwd_{i}")
    return x3, dict(x1=x1, h=h, g=g, u=u, act=act, x2=x2)


def _ffn_ple_bwd(dx3, p_i, prm, i, sv, grads):
    ds, dple = _ple_bwd(sv["x2"], p_i, prm["ple_w_gate"][i], prm["ple_w_proj"][i], dx3, name=f"ple_bwd_{i}")
    grads["ple_w_gate"][i] = _matmul_tn(sv["x2"], ds, name=f"d_ple_w_gate_{i}")
    grads["ple_w_proj"][i] = _matmul_tn(dple, p_i, name=f"d_ple_w_proj_{i}")
    dx2 = _matmul(ds, prm["ple_w_gate"][i], mode="nt", addend=dx3, name=f"ple_dx_{i}")
    grads["ffn_w_down"][i] = _matmul_tn(sv["act"], dx2, name=f"d_ffn_w_down_{i}")
    dg, du = _swiglu_bwd(dx2, prm["ffn_w_down"][i], sv["g"], sv["u"], name=f"swiglu_bwd_{i}")
    grads["ffn_w_gate"][i] = _matmul_tn(dg, sv["h"], name=f"d_ffn_w_gate_{i}")
    grads["ffn_w_up"][i] = _matmul_tn(du, sv["h"], name=f"d_ffn_w_up_{i}")
    dh = _matmul(dg, prm["ffn_w_gate"][i], mode="nn", name=f"ffn_dh_gate_{i}")
    dh = _matmul(du, prm["ffn_w_up"][i], mode="nn", addend=dh, name=f"ffn_dh_up_{i}")
    dx1, dgain = _rmsnorm_bwd(sv["x1"], prm["norm_ffn"][i:i + 1], dh, dx2, name=f"ffn_norm_bwd_{i}")
    grads["norm_ffn"][i] = dgain[0]
    return dx1


def _mamba_fwd(x0, prm):
    h = _rmsnorm_fwd(x0, prm["norm_mix"][0:1], name="mix_norm_fwd_0")
    z = _matmul(h, prm["ssm_w_z"], mode="nt", name="ssm_in_z")
    xbc_pre = _matmul(h, prm["ssm_w_xbc"], mode="nt", name="ssm_in_xbc")
    dt_raw = _matmul(h, prm["ssm_w_dt"], mode="nt", name="ssm_in_dt")
    xbc = _conv_fwd(xbc_pre, prm["ssm_conv_w"], prm["ssm_conv_b"])
    dt_bias = _pad_lanes(prm["ssm_dt_bias"])
    a_log = _pad_lanes(prm["ssm_a_log"])
    dt, acs = _ssd_prep_fwd(dt_raw, dt_bias, a_log)
    dt_rep = jnp.repeat(dt[:, :SSM_HEADS], SSM_HEAD_DIM, axis=1)
    acs_rep = jnp.repeat(acs[:, :SSM_HEADS], SSM_HEAD_DIM, axis=1)
    acs_t = acs[:, :SSM_HEADS].T
    dskip_rep = _rep_heads(prm["ssm_d_skip"][0])
    y, hin_all = _ssd_fwd(xbc, dt_rep, acs_rep, acs_t, dskip_rep)
    yn = _gate_norm_fwd(y, z, prm["ssm_norm_w"])
    x1 = _matmul(yn, prm["ssm_w_out"], mode="nn", addend=x0, name="ssm_out")
    sv = dict(x0=x0, h=h, z=z, xbc_pre=xbc_pre, dt_raw=dt_raw, xbc=xbc, dt_bias=dt_bias, dt_rep=dt_rep,
              acs_rep=acs_rep, acs_t=acs_t, dskip_rep=dskip_rep, y=y, hin_all=hin_all, yn=yn)
    return x1, sv


def _mamba_bwd(dx1, prm, sv, grads):
    grads["ssm_w_out"] = _matmul_tn(sv["yn"], dx1, name="d_ssm_w_out")
    dyn = _matmul(dx1, prm["ssm_w_out"], mode="nt", name="ssm_out_dx")
    dy, dz, dnw = _gate_norm_bwd(sv["y"], sv["z"], prm["ssm_norm_w"], dyn)
    grads["ssm_norm_w"] = dnw
    a_rep = _rep_heads(-jnp.exp(prm["ssm_a_log"][0]))
    dxbc, ddt_rep, da_rep, dds_rep = _ssd_bwd(sv["xbc"], sv["dt_rep"], sv["acs_rep"], sv["acs_t"], sv["dskip_rep"],
                                              a_rep, sv["hin_all"], dy)
    grads["ssm_d_skip"] = _per_head(dds_rep)[None, :]
    grads["ssm_a_log"] = (_per_head(da_rep) * _per_head(a_rep))[None, :]
    ddt = _pad_lanes(ddt_rep[:, ::SSM_HEAD_DIM])
    ddt_raw, dbias = _ssd_prep_bwd(sv["dt_raw"], sv["dt_bias"], ddt)
    grads["ssm_dt_bias"] = dbias[:, :SSM_HEADS]
    du, dcw, dcb = _conv_bwd(sv["xbc_pre"], prm["ssm_conv_w"], prm["ssm_conv_b"], dxbc)
    grads["ssm_conv_w"] = dcw
    grads["ssm_conv_b"] = dcb
    h = sv["h"]
    grads["ssm_w_in"] = jnp.concatenate(
        [_matmul_tn(dz, h, name="d_ssm_w_z"), _matmul_tn(du, h, name="d_ssm_w_xbc"),
         _matmul_tn(ddt_raw, h, name="d_ssm_w_dt")[:SSM_HEADS]], axis=0)
    dh = _matmul(dz, prm["ssm_w_z"], mode="nn", name="ssm_dh_z")
    dh = _matmul(du, prm["ssm_w_xbc"], mode="nn", addend=dh, name="ssm_dh_xbc")
    dh = _matmul(ddt_raw, prm["ssm_w_dt"], mode="nn", addend=dh, name="ssm_dh_dt")
    dx0, dgain = _rmsnorm_bwd(sv["x0"], prm["norm_mix"][0:1], dh, dx1, name="mix_norm_bwd_0")
    grads["norm_mix"][0] = dgain[0]
    return dx0


def _attn_mixer_fwd(x0, prm):
    h = _rmsnorm_fwd(x0, prm["norm_mix"][1:2], name="mix_norm_fwd_1")
    qkv = _matmul(h, prm["att_w_qkv"], mode="nt", name="att_qkv")
    gq = jnp.tile(prm["att_q_norm"], (1, ATT_HEADS))
    gk = jnp.tile(prm["att_k_norm"], (1, ATT_HEADS))
    gq2, gk2 = gq[:, :LANES], gk[:, :LANES]
    outs, lses = [], []
    for g, (window, dil) in enumerate(DIL_PATTERNS):
        o_g, l_g = _attn_fwd(qkv, gq2, gk2, g, dil)
        outs.append(o_g)
        lses.append(l_g)
    o_b, o_f, l_rep = _attn_combine_fwd(outs, lses)
    x1 = _matmul(o_b, prm["att_w_o"], mode="nn", addend=x0, name="att_out")
    sv = dict(x0=x0, h=h, qkv=qkv, gq=gq, gk=gk, gq2=gq2, gk2=gk2, o_b=o_b, o_f=o_f, l_rep=l_rep)
    return x1, sv


def _attn_mixer_bwd(dx1, prm, sv, grads):
    grads["att_w_o"] = _matmul_tn(sv["o_b"], dx1, name="d_att_w_o")
    do = _matmul(dx1, prm["att_w_o"], mode="nt", name="att_out_dx")
    dl_rep = _attn_combine_bwd(do, sv["o_f"])
    blocks = [None] * N_QKV_BLOCKS
    for g, (window, dil) in enumerate(DIL_PATTERNS):
        blocks[3 * g] = _attn_bwd_dq(sv["qkv"], sv["gq2"], sv["gk2"], do, sv["l_rep"], dl_rep, g, dil)
        dk, dv = _attn_bwd_dkv(sv["qkv"], sv["gq2"], sv["gk2"], do, _rows_by_residue(sv["l_rep"], dil),
                               _rows_by_residue(dl_rep, dil), g, dil)
        blocks[3 * g + 1] = dk
        blocks[3 * g + 2] = dv
    dqkv, dgq, dgk = _qk_norm_bwd(sv["qkv"], sv["gq"], sv["gk"], blocks)
    grads["att_q_norm"] = dgq.reshape(ATT_HEADS, ATT_HEAD_DIM).sum(axis=0)[None, :]
    grads["att_k_norm"] = dgk.reshape(ATT_HEADS, ATT_HEAD_DIM).sum(axis=0)[None, :]
    grads["att_w_qkv"] = _matmul_tn(dqkv, sv["h"], name="d_att_w_qkv")
    dh = _matmul(dqkv, prm["att_w_qkv"], mode="nn", name="att_qkv_dx")
    dx0, dgain = _rmsnorm_bwd(sv["x0"], prm["norm_mix"][1:2], dh, dx1, name="mix_norm_bwd_1")
    grads["norm_mix"][1] = dgain[0]
    return dx0


def _local_step(x, p, target, prm):
    grads = {k: [None, None] for k in ("norm_mix", "norm_ffn", "ffn_w_gate", "ffn_w_up", "ffn_w_down",
                                       "ple_w_proj", "ple_w_gate")}
    x1, sv_m = _mamba_fwd(x, prm)
    x3, sv_f0 = _ffn_ple_fwd(x1, p[0], prm, 0)
    x4, sv_a = _attn_mixer_fwd(x3, prm)
    x6, sv_f1 = _ffn_ple_fwd(x4, p[1], prm, 1)
    dy, loss_row = _loss_head(x6, target)
    dx4 = _ffn_ple_bwd(dy, p[1], prm, 1, sv_f1, grads)
    dx3 = _attn_mixer_bwd(dx4, prm, sv_a, grads)
    dx1 = _ffn_ple_bwd(dx3, p[0], prm, 0, sv_f0, grads)
    dx0 = _mamba_bwd(dx1, prm, sv_m, grads)
    return loss_row, dx0, grads


MESH = pl.DeviceIdType.MESH
ANY = pl.BlockSpec(memory_space=pl.ANY)
W_IN_SLAB_ROWS = 1312


def _position():
    return lax.axis_index("x"), lax.axis_index("y"), lax.axis_index("c")


def _other_chips(x, y):
    return [(1 - x, y), (x, 1 - y), (1 - x, 1 - y)]


def _gather_slabs(entries, conv_w):
    n = len(entries)

    def body(*refs):
        in_refs, conv_ref = refs[:n], refs[n]
        out_refs, conv_out = refs[n + 1:2 * n + 1], refs[2 * n + 1]
        send_sems, recv_sems = refs[2 * n + 2], refs[2 * n + 3]
        x, y, c = _position()
        me, sibling = (x, y, c), (x, y, 1 - c)
        chips = _other_chips(x, y)

        def copy(k, src, dst, to):
            return pltpu.make_async_remote_copy(src_ref=src, dst_ref=dst, send_sem=send_sems.at[k],
                                                recv_sem=recv_sems.at[k], device_id=to, device_id_type=MESH)

        started = []
        for j, chip in enumerate(chips):
            for e in range(n):
                started.append(copy(6 * e + j, in_refs[e].at[c], out_refs[e].at[2 * x + y, c], (*chip, c)))
                started[-1].start()
            started.append(copy(6 * n + j, conv_ref, conv_out.at[2 * x + y], (*chip, c)))
            started[-1].start()
        for j, (px, py) in enumerate(chips):
            for e in range(n):
                landed = out_refs[e].at[2 * px + py, c]
                copy(6 * e + j, landed, landed, me).wait_recv()
                started.append(copy(6 * e + 3 + j, landed, landed, sibling))
                started[-1].start()
            copy(6 * n + j, conv_ref, conv_out.at[2 * px + py], me).wait_recv()
        for j, (px, py) in enumerate(chips):
            for e in range(n):
                passed = out_refs[e].at[2 * px + py, 1 - c]
                copy(6 * e + 3 + j, passed, passed, me).wait_recv()
        for cp in started:
            cp.wait_send()

    outs = pl.pallas_call(
        body, name="gather_slabs", in_specs=[ANY] * (n + 1), out_specs=[ANY] * (n + 1),
        out_shape=[jax.ShapeDtypeStruct((N_CHIPS,) + e.shape, e.dtype) for e in entries]
        + [jax.ShapeDtypeStruct((N_CHIPS,) + conv_w.shape, conv_w.dtype)],
        scratch_shapes=[pltpu.SemaphoreType.DMA((6 * n + 3,)), pltpu.SemaphoreType.DMA((6 * n + 3,))],
    )(*entries, conv_w)
    return outs[:n], outs[n]


def _swap_halves(grads):
    n = len(grads)

    def body(*refs):
        g_refs, r_refs = refs[:n], refs[n:2 * n]
        send_sems, recv_sems = refs[2 * n], refs[2 * n + 1]
        x, y, c = _position()
        cps = [pltpu.make_async_remote_copy(src_ref=g_refs[e].at[:, 1 - c], dst_ref=r_refs[e],
                                            send_sem=send_sems.at[e], recv_sem=recv_sems.at[e],
                                            device_id=(x, y, 1 - c), device_id_type=MESH) for e in range(n)]
        for cp in cps:
            cp.start()
        for cp in cps:
            cp.wait()

    return pl.pallas_call(
        body, name="grad_swap_halves", in_specs=[ANY] * n, out_specs=[ANY] * n,
        out_shape=[jax.ShapeDtypeStruct((N_CHIPS,) + g.shape[2:], g.dtype) for g in grads],
        scratch_shapes=[pltpu.SemaphoreType.DMA((n,)), pltpu.SemaphoreType.DMA((n,))],
    )(*grads)


def _chip_exchange(chipsums):
    n = len(chipsums)

    def body(*refs):
        cs_refs, r_refs = refs[:n], refs[n:2 * n]
        send_sems, recv_sems = refs[2 * n], refs[2 * n + 1]
        x, y, c = _position()
        cps = []
        for j, (tx, ty) in enumerate(_other_chips(x, y)):
            for e in range(n):
                cps.append(pltpu.make_async_remote_copy(
                    src_ref=cs_refs[e].at[2 * tx + ty], dst_ref=r_refs[e].at[j], send_sem=send_sems.at[3 * e + j],
                    recv_sem=recv_sems.at[3 * e + j], device_id=(tx, ty, c), device_id_type=MESH))
                cps[-1].start()
        for cp in cps:
            cp.wait()

    return pl.pallas_call(
        body, name="grad_chip_exchange", in_specs=[ANY] * n, out_specs=[ANY] * n,
        out_shape=[jax.ShapeDtypeStruct((3,) + cs.shape[1:], cs.dtype) for cs in chipsums],
        scratch_shapes=[pltpu.SemaphoreType.DMA((3 * n,)), pltpu.SemaphoreType.DMA((3 * n,))],
    )(*chipsums)


def _share_halves(totals):
    n = len(totals)

    def body(*refs):
        t_refs, r_refs = refs[:n], refs[n:2 * n]
        send_sems, recv_sems = refs[2 * n], refs[2 * n + 1]
        x, y, c = _position()
        cps = [pltpu.make_async_remote_copy(src_ref=t_refs[e], dst_ref=r_refs[e], send_sem=send_sems.at[e],
                                            recv_sem=recv_sems.at[e], device_id=(x, y, 1 - c), device_id_type=MESH)
               for e in range(n)]
        for cp in cps:
            cp.start()
        for cp in cps:
            cp.wait()

    return pl.pallas_call(
        body, name="grad_share_halves", in_specs=[ANY] * n, out_specs=[ANY] * n,
        out_shape=[jax.ShapeDtypeStruct(t.shape, t.dtype) for t in totals],
        scratch_shapes=[pltpu.SemaphoreType.DMA((n,)), pltpu.SemaphoreType.DMA((n,))],
    )(*totals)


def _reduce_rows(h):
    return h if h <= 704 else h // 2


def _add_sibling(grad, recv, c_idx, *, name):
    _, _, h, cw = grad.shape
    th = _reduce_rows(h)

    def body(c_ref, g_ref, r_ref, o_ref):
        o_ref[...] = (g_ref[...] + r_ref[...]).astype(BF16)

    return pl.pallas_call(
        body, name=name,
        grid_spec=pltpu.PrefetchScalarGridSpec(
            num_scalar_prefetch=1, grid=(N_CHIPS, h // th),
            in_specs=[pl.BlockSpec((None, None, th, cw), lambda s, i, c_ref: (s, c_ref[0], i, 0)),
                      pl.BlockSpec((None, th, cw), lambda s, i, c_ref: (s, i, 0))],
            out_specs=pl.BlockSpec((None, th, cw), lambda s, i, c_ref: (s, i, 0))),
        out_shape=jax.ShapeDtypeStruct((N_CHIPS, h, cw), BF16),
        compiler_params=_params("parallel", "parallel"),
    )(c_idx, grad, recv)


def _add_chips(chipsum, recv, s_idx, *, name):
    _, h, cw = chipsum.shape
    th = _reduce_rows(h)

    def body(s_ref, own_ref, r_ref, o_ref):
        o_ref[...] = ((own_ref[...].astype(F32) + r_ref[0].astype(F32)) + r_ref[1].astype(F32)) + r_ref[2].astype(F32)

    return pl.pallas_call(
        body, name=name,
        grid_spec=pltpu.PrefetchScalarGridSpec(
            num_scalar_prefetch=1, grid=(h // th,),
            in_specs=[pl.BlockSpec((None, th, cw), lambda i, s_ref: (s_ref[0], i, 0)),
                      pl.BlockSpec((3, th, cw), lambda i, s_ref: (0, i, 0))],
            out_specs=pl.BlockSpec((th, cw), lambda i, s_ref: (i, 0))),
        out_shape=jax.ShapeDtypeStruct((h, cw), F32),
        compiler_params=_params("parallel"),
    )(s_idx, chipsum, recv)


def _adamw_math(w, g, m, v):
    m = ADAM_B1 * m + (1.0 - ADAM_B1) * g
    v = ADAM_B2 * v + (1.0 - ADAM_B2) * (g * g)
    m_hat = m / (1.0 - ADAM_B1 ** ADAM_STEP)
    v_hat = v / (1.0 - ADAM_B2 ** ADAM_STEP)
    delta = -ADAM_LR * (m_hat / (jnp.sqrt(v_hat) + ADAM_EPS) + ADAM_WD * w)
    return delta, m, v


ADAM_TILE_ELEMS = 256 * 1024


def _adamw(w, g, m, v, *, name):
    shape = w.shape
    cols = shape[-1]
    rows = w.size // cols
    tr = rows
    for cand in range(8, rows, 8):
        if rows % cand == 0 and cand * cols <= ADAM_TILE_ELEMS:
            tr = cand
    if rows * cols <= ADAM_TILE_ELEMS:
        tr = rows

    def body(w_ref, g_ref, m_ref, v_ref, d_ref, nm_ref, nv_ref):
        d, nm, nv = _adamw_math(w_ref[...], g_ref[...], m_ref[...], v_ref[...])
        d_ref[...] = d
        nm_ref[...] = nm
        nv_ref[...] = nv

    blk = pl.BlockSpec((tr, cols), lambda i: (i, 0))
    sds = jax.ShapeDtypeStruct((rows, cols), F32)
    outs = pl.pallas_call(
        body, name=name, grid=(rows // tr,), in_specs=[blk] * 4, out_specs=[blk] * 3, out_shape=[sds] * 3,
        compiler_params=_params("parallel"),
    )(*[a.reshape(rows, cols) for a in (w, g, m, v)])
    return [o.reshape(shape) for o in outs]


SMALL_LAYOUT = (("loss", 1), ("norm_mix", 16), ("norm_ffn", 16), ("ssm_conv_b", 24), ("ssm_dt_bias", 1),
                ("ssm_a_log", 1), ("ssm_d_skip", 1), ("ssm_norm_w", 16), ("att_q_norm", 1), ("att_k_norm", 1),
                ("conv_w_full", 96))
SMALL_ROWS = 176
N_DEVICES = 8


def _small_pack(values):
    parts = []
    for name, rows in SMALL_LAYOUT:
        flat = values[name].reshape(-1).astype(F32)
        parts.append(jnp.pad(flat, (0, rows * LANES - flat.shape[0])).reshape(rows, LANES))
    used = sum(r for _, r in SMALL_LAYOUT)
    parts.append(jnp.zeros((SMALL_ROWS - used, LANES), F32))
    return jnp.concatenate(parts, axis=0)


def _small_unpack(pack, shapes):
    out, off = {}, 0
    for name, rows in SMALL_LAYOUT:
        shape = shapes[name]
        n = math.prod(shape)
        out[name] = pack[off:off + rows].reshape(-1)[:n].reshape(shape)
        off += rows
    return out


def _small_allreduce_adamw(g, w, m, v):
    def body(g_ref, w_ref, m_ref, v_ref, gs_ref, d_ref, nm_ref, nv_ref, buf, send_sems, recv_sems):
        x, y, c = _position()
        pos = (x, y, c)
        me = 4 * x + 2 * y + c
        buf[me] = g_ref[...]
        peers = []
        for k in range(1, N_DEVICES):
            bits = ((k >> 2) & 1, (k >> 1) & 1, k & 1)
            peers.append(tuple(1 - p if b else p for p, b in zip(pos, bits)))
        cps = [pltpu.make_async_remote_copy(src_ref=g_ref, dst_ref=buf.at[me], send_sem=send_sems.at[k],
                                            recv_sem=recv_sems.at[k], device_id=peer, device_id_type=MESH)
               for k, peer in enumerate(peers)]
        for cp in cps:
            cp.start()
        for k, (px, py, pc) in enumerate(peers):
            pltpu.make_async_remote_copy(src_ref=g_ref, dst_ref=buf.at[4 * px + 2 * py + pc],
                                         send_sem=send_sems.at[k], recv_sem=recv_sems.at[k],
                                         device_id=(px, py, pc), device_id_type=MESH).wait_recv()
        for cp in cps:
            cp.wait_send()
        total = buf[0]
        for dev in range(1, N_DEVICES):
            total = total + buf[dev]
        gs_ref[...] = total
        d, nm, nv = _adamw_math(w_ref[...], total, m_ref[...], v_ref[...])
        d_ref[...] = d
        nm_ref[...] = nm
        nv_ref[...] = nv

    vm = pl.BlockSpec(memory_space=pltpu.VMEM)
    sds = jax.ShapeDtypeStruct((SMALL_ROWS, LANES), F32)
    return pl.pallas_call(
        body, name="small_allreduce_adamw", in_specs=[vm] * 4, out_specs=[vm] * 4, out_shape=[sds] * 4,
        scratch_shapes=[pltpu.VMEM((N_DEVICES, SMALL_ROWS, LANES), F32),
                        pltpu.SemaphoreType.DMA((N_DEVICES - 1,)), pltpu.SemaphoreType.DMA((N_DEVICES - 1,))],
    )(g, w, m, v)


SMALL = tuple(n for n, _ in SMALL_LAYOUT if n not in ("loss", "conv_w_full"))
WEIGHTS = ("norm_mix", "norm_ffn", "ssm_w_in", "ssm_conv_w", "ssm_conv_b", "ssm_dt_bias", "ssm_a_log", "ssm_d_skip",
           "ssm_norm_w", "ssm_w_out", "att_w_qkv", "att_q_norm", "att_k_norm", "att_w_o", "ffn_w_gate", "ffn_w_up",
           "ffn_w_down", "ple_w_proj", "ple_w_gate")
COLUMN_SHARDED = ("ssm_w_in", "att_w_qkv", "ffn_w_gate", "ffn_w_up", "ple_w_proj")
LAYERED = ("ffn_w_gate", "ffn_w_up", "ffn_w_down", "ple_w_proj", "ple_w_gate")
GATHER_ORDER = ("ssm_w_in", "ssm_w_out", "att_w_qkv", "att_w_o", "ffn_w_gate", "ffn_w_up", "ffn_w_down",
                "ple_w_proj", "ple_w_gate")


def _weight_slabs(w):
    slabs = []
    for n in GATHER_ORDER:
        a = w[n]
        if n in LAYERED:
            a = a.transpose(0, 2, 1) if n in COLUMN_SHARDED else a
        else:
            a = a[0].T if n in COLUMN_SHARDED else a[0]
            if n == "ssm_w_in":
                a = jnp.pad(a, ((0, W_IN_SLAB_ROWS - a.shape[0]), (0, 0)))
            a = a.reshape(2, a.shape[0] // 2, a.shape[1])
        slabs.append(a.astype(BF16))
    return slabs


def _full_weights(gathered, own, conv_all, conv_own, s_me, small):
    full = {}
    for n, g, o in zip(GATHER_ORDER, gathered, own):
        full[n] = lax.dynamic_update_slice(g, o[None], (s_me, 0, 0, 0))
    prm = dict(small)
    conv = lax.dynamic_update_slice(conv_all, conv_own[None], (s_me, 0, 0))
    prm["ssm_conv_w"] = conv.transpose(1, 0, 2).reshape(CONV_WIDTH, CONV_DIM)
    rows = (D_INNER + CONV_DIM + SSM_HEADS) // N_CHIPS
    w_in_t = full["ssm_w_in"].reshape(N_CHIPS, W_IN_SLAB_ROWS, D_MODEL)[:, :rows].reshape(N_CHIPS * rows, D_MODEL)
    prm["ssm_w_z"] = w_in_t[:D_INNER]
    prm["ssm_w_xbc"] = w_in_t[D_INNER:D_INNER + CONV_DIM]
    prm["ssm_w_dt"] = jnp.pad(w_in_t[D_INNER + CONV_DIM:], ((0, LANES - SSM_HEADS), (0, 0)))
    prm["ssm_w_out"] = full["ssm_w_out"].reshape(D_INNER, D_MODEL)
    prm["att_w_qkv"] = full["att_w_qkv"].reshape(N_QKV_BLOCKS * ATT_W, D_MODEL)
    prm["att_w_o"] = full["att_w_o"].reshape(ATT_W, D_MODEL)
    for n in LAYERED:
        g = full[n]
        prm[n] = [g[:, i].reshape(N_CHIPS * g.shape[2], g.shape[3]) for i in range(2)]
    return prm


def _grad_slabs(grads):
    out = []
    for n in GATHER_ORDER:
        for i in (range(2) if n in LAYERED else (None,)):
            g = grads[n] if i is None else grads[n][i]
            if n == "ssm_w_in":
                g = jnp.pad(g.reshape(N_CHIPS, g.shape[0] // N_CHIPS, D_MODEL),
                            ((0, 0), (0, W_IN_SLAB_ROWS - g.shape[0] // N_CHIPS), (0, 0)))
            rows = g.size // (N_CHIPS * g.shape[-1])
            out.append((n, i, g.reshape(N_CHIPS, 2, rows // 2, g.shape[-1])))
    return out


def _natural_shard(n, reduced, shape):
    def one(r):
        if n == "ssm_w_in":
            r = r[:shape[-1]]
        return r.T if n in COLUMN_SHARDED else r
    if n in LAYERED:
        return jnp.stack([one(r) for r in reduced]).reshape(shape)
    return one(reduced[0]).reshape(shape)


def kernel(x, p, norm_mix, norm_ffn, ssm_w_in, ssm_conv_w, ssm_conv_b, ssm_dt_bias, ssm_a_log, ssm_d_skip, ssm_norm_w, ssm_w_out, att_w_qkv, att_q_norm, att_k_norm, att_w_o, ffn_w_gate, ffn_w_up, ffn_w_down, ple_w_proj, ple_w_gate, loss_target, m_norm_mix, m_norm_ffn, m_ssm_w_in, m_ssm_conv_w, m_ssm_conv_b, m_ssm_dt_bias, m_ssm_a_log, m_ssm_d_skip, m_ssm_norm_w, m_ssm_w_out, m_att_w_qkv, m_att_q_norm, m_att_k_norm, m_att_w_o, m_ffn_w_gate, m_ffn_w_up, m_ffn_w_down, m_ple_w_proj, m_ple_w_gate, v_norm_mix, v_norm_ffn, v_ssm_w_in, v_ssm_conv_w, v_ssm_conv_b, v_ssm_dt_bias, v_ssm_a_log, v_ssm_d_skip, v_ssm_norm_w, v_ssm_w_out, v_att_w_qkv, v_att_q_norm, v_att_k_norm, v_att_w_o, v_ffn_w_gate, v_ffn_w_up, v_ffn_w_down, v_ple_w_proj, v_ple_w_gate):
    given = dict(locals())
    w = {n: given[n] for n in WEIGHTS}
    m = {n: given["m_" + n] for n in WEIGHTS}
    v = {n: given["v_" + n] for n in WEIGHTS}
    c_idx = lax.axis_index("c").astype(jnp.int32).reshape(1)
    s_idx = (2 * lax.axis_index("x") + lax.axis_index("y")).astype(jnp.int32).reshape(1)

    s_me = 2 * lax.axis_index("x") + lax.axis_index("y")
    first_core = lax.axis_index("c") == 0

    own = _weight_slabs(w)
    gathered, conv_all = _gather_slabs(own, ssm_conv_w[0])
    prm = _full_weights(gathered, own, conv_all, ssm_conv_w[0], s_me, {n: w[n] for n in SMALL})

    loss_row, dx, grads = _local_step(x[0], p[:, 0], loss_target[0], prm)

    slabs = _grad_slabs(grads)
    tags = [n if i is None else f"{n}_{i}" for n, i, _ in slabs]
    g4 = [g for _, _, g in slabs]
    from_sibling = _swap_halves(g4)
    chipsums = [_add_sibling(g, r, c_idx, name="add_sibling_" + t) for g, r, t in zip(g4, from_sibling, tags)]
    from_chips = _chip_exchange(chipsums)
    totals = [_add_chips(cs, r, s_idx, name="add_chips_" + t) for cs, r, t in zip(chipsums, from_chips, tags)]
    shared = _share_halves(totals)
    reduced = {}
    for (n, i, _), mine, theirs in zip(slabs, totals, shared):
        lo = jnp.where(first_core, mine, theirs)
        hi = jnp.where(first_core, theirs, mine)
        reduced.setdefault(n, []).append(jnp.concatenate([lo, hi], axis=0))

    grad, delta, new_m, new_v = {}, {}, {}, {}
    for n in GATHER_ORDER:
        grad[n] = _natural_shard(n, reduced[n], w[n].shape)
        delta[n], new_m[n], new_v[n] = _adamw(w[n], grad[n], m[n], v[n], name="adamw_" + n)

    small_g = {n: (jnp.stack(grads[n]) if isinstance(grads[n], list) else grads[n]) for n in SMALL}
    small_g["loss"] = loss_row
    small_g["conv_w_full"] = grads["ssm_conv_w"]
    zero = {"loss": jnp.zeros((1, LANES), F32), "conv_w_full": jnp.zeros((CONV_WIDTH, CONV_DIM), F32)}
    outs = _small_allreduce_adamw(_small_pack(small_g), _small_pack({**w, **zero}), _small_pack({**m, **zero}),
                                  _small_pack({**v, **zero}))
    shapes = {n: w[n].shape for n in SMALL}
    shapes["loss"] = (1, LANES)
    shapes["conv_w_full"] = (CONV_WIDTH, CONV_DIM)
    sg, sd, sm, sv = [_small_unpack(o, shapes) for o in outs]
    for n in SMALL:
        grad[n], delta[n], new_m[n], new_v[n] = sg[n], sd[n], sm[n], sv[n]
    loss = sg["loss"][0, 0]
    conv_cols = CONV_DIM // N_CHIPS
    grad["ssm_conv_w"] = lax.dynamic_slice(sg["conv_w_full"], (0, s_me * conv_cols), (CONV_WIDTH, conv_cols))[None]
    delta["ssm_conv_w"], new_m["ssm_conv_w"], new_v["ssm_conv_w"] = _adamw(
        ssm_conv_w, grad["ssm_conv_w"], m_ssm_conv_w, v_ssm_conv_w, name="adamw_ssm_conv_w")

    return (loss, dx[None], *[grad[n] for n in WEIGHTS], *[delta[n] for n in WEIGHTS],
            *[new_m[n] for n in WEIGHTS], *[new_v[n] for n in WEIGHTS])
```

```python
import functools
import math

import jax
import jax.numpy as jnp
from jax import lax
from jax.experimental import pallas as pl
from jax.experimental.pallas import tpu as pltpu

F32 = jnp.float32
BF16 = jnp.bfloat16
HIGHEST = lax.Precision.HIGHEST

NORM_EPS = 1e-6
ADAM_LR, ADAM_B1, ADAM_B2, ADAM_EPS, ADAM_WD, ADAM_STEP = 0.001, 0.9, 0.999, 1e-08, 0.01, 10

D_MODEL = 1024
D_INNER = 2048
SSM_HEADS = 32
SSM_HEAD_DIM = 64
SSM_GROUPS = 4
SSM_STATE = 128
SSD_CHUNK = 128
CONV_DIM = 3072
CONV_WIDTH = 4
ATT_HEADS = 16
ATT_HEAD_DIM = 64
DIL_PATTERNS = ((128, 1), (512, 4), (2048, 16))
ATT_BLOCK = 128
FFN_HIDDEN = 2816
PLE_DIM = 256

LANES = 128
V7X_VMEM_LIMIT = 56 * 1024 * 1024
NEG_BIG = -1e30

N_CHIPS = 4


def _params(*sem):
    return pltpu.CompilerParams(dimension_semantics=sem, vmem_limit_bytes=V7X_VMEM_LIMIT)


def _tile(n, pref):
    if n <= pref:
        return n
    best = None
    for t in range(LANES, pref + 1, LANES):
        if n % t == 0:
            best = t
    assert best is not None, (n, pref)
    return best


def _sigmoid(v):
    return 1.0 / (1.0 + jnp.exp(-v))


def _dot(a, b):
    return jnp.dot(a, b, preferred_element_type=F32)


def _dot_nt(a, b):
    return lax.dot_general(a, b, (((1,), (1,)), ((), ())), preferred_element_type=F32)


def _dot_tn(a, b):
    return lax.dot_general(a, b, (((0,), (0,)), ((), ())), preferred_element_type=F32)


def _head_block_diag():
    i = lax.broadcasted_iota(jnp.int32, (LANES, LANES), 0) // ATT_HEAD_DIM
    j = lax.broadcasted_iota(jnp.int32, (LANES, LANES), 1) // ATT_HEAD_DIM
    return (i == j).astype(BF16)


def _split_dot(ones, z):
    hi = z.astype(BF16)
    lo = (z - hi.astype(F32)).astype(BF16)
    return _dot(ones, hi) + _dot(ones, lo)


def _head_sums(z, bd):
    hi = z.astype(BF16)
    lo = (z - hi.astype(F32)).astype(BF16)
    parts = []
    for t in range(z.shape[1] // LANES):
        sl = slice(t * LANES, (t + 1) * LANES)
        parts.append(_dot(hi[:, sl], bd) + _dot(lo[:, sl], bd))
    return parts[0] if len(parts) == 1 else jnp.concatenate(parts, axis=1)


def _lane_lt64(rows):
    return lax.broadcasted_iota(jnp.int32, (rows, LANES), 1) < ATT_HEAD_DIM


MESH = pl.DeviceIdType.MESH
ANY = pl.BlockSpec(memory_space=pl.ANY)


class _Side:
    def __init__(self, inputs, out_shapes, n_sems, start, finish):
        self.inputs, self.out_shapes, self.n_sems = list(inputs), list(out_shapes), n_sems
        self.start, self.finish = start, finish
        self.outputs = None


def _call(body, side, *, name, grid, in_specs, out_specs, out_shape, scratch_shapes, semantics, args):
    in_specs, out_specs, out_shape = list(in_specs), list(out_specs), list(out_shape)
    scratch_shapes = list(scratch_shapes)
    if side is None:
        return pl.pallas_call(body, name=name, grid=grid, in_specs=in_specs, out_specs=out_specs,
                              out_shape=out_shape, scratch_shapes=scratch_shapes,
                              compiler_params=_params(*semantics))(*args)
    ni, no, ns = len(in_specs), len(out_specs), len(scratch_shapes)
    si, so = len(side.inputs), len(side.out_shapes)

    def hosted(*refs):
        ins, s_ins = refs[:ni], refs[ni:ni + si]
        outs, s_outs = refs[ni + si:ni + si + no], refs[ni + si + no:ni + si + no + so]
        scratch = refs[ni + si + no + so:ni + si + no + so + ns]
        send_sems, recv_sems = refs[-2], refs[-1]
        first = pl.program_id(0) == 0
        last = pl.program_id(0) == grid[0] - 1
        for axis in range(1, len(grid)):
            first = jnp.logical_and(first, pl.program_id(axis) == 0)
            last = jnp.logical_and(last, pl.program_id(axis) == grid[axis] - 1)

        @pl.when(first)
        def _():
            side.start(s_ins, s_outs, send_sems, recv_sems)

        body(*ins, *outs, *scratch)

        @pl.when(last)
        def _():
            side.finish(s_ins, s_outs, send_sems, recv_sems)

    res = pl.pallas_call(
        hosted, name=name, grid=grid, in_specs=in_specs + [ANY] * si, out_specs=out_specs + [ANY] * so,
        out_shape=out_shape + side.out_shapes,
        scratch_shapes=scratch_shapes + [pltpu.SemaphoreType.DMA((side.n_sems,)),
                                         pltpu.SemaphoreType.DMA((side.n_sems,))],
        compiler_params=_params(*["arbitrary"] * len(grid)),
    )(*args, *side.inputs)
    side.outputs = list(res[no:])
    return list(res[:no])


def _matmul(a, b, *, mode, name, out_dtype=F32, addend=None, tm=1024, tn=512, tk_max=3072, side=None):
    m, k = a.shape
    if mode == "nn":
        k2, n = b.shape
    else:
        n, k2 = b.shape
    assert k == k2, (a.shape, b.shape, mode)
    tm, tn, tk = _tile(m, tm), _tile(n, tn), _tile(k, tk_max)
    nk = k // tk
    has_add = addend is not None

    def body(*refs):
        a_ref, b_ref = refs[0], refs[1]
        add_ref = refs[2] if has_add else None
        o_ref, acc_ref = refs[-2], refs[-1]
        kk = pl.program_id(2)
        av = a_ref[...].astype(BF16)
        bv = b_ref[...].astype(BF16)
        part = _dot(av, bv) if mode == "nn" else _dot_nt(av, bv)

        @pl.when(kk == 0)
        def _():
            acc_ref[...] = part

        @pl.when(kk > 0)
        def _():
            acc_ref[...] += part

        @pl.when(kk == nk - 1)
        def _():
            res = acc_ref[...]
            if has_add:
                res = res + add_ref[...]
            o_ref[...] = res.astype(out_dtype)

    a_spec = pl.BlockSpec((tm, tk), lambda i, j, kk: (i, kk))
    if mode == "nn":
        b_spec = pl.BlockSpec((tk, tn), lambda i, j, kk: (kk, j))
    else:
        b_spec = pl.BlockSpec((tn, tk), lambda i, j, kk: (j, kk))
    in_specs = [a_spec, b_spec]
    args = [a, b]
    if has_add:
        in_specs.append(pl.BlockSpec((tm, tn), lambda i, j, kk: (i, j)))
        args.append(addend)
    return _call(
        body, side, name=name, grid=(m // tm, n // tn, nk),
        in_specs=in_specs, out_specs=[pl.BlockSpec((tm, tn), lambda i, j, kk: (i, j))],
        out_shape=[jax.ShapeDtypeStruct((m, n), out_dtype)],
        scratch_shapes=[pltpu.VMEM((tm, tn), F32)],
        semantics=("parallel", "parallel", "arbitrary"), args=args,
    )[0]


def _matmul_tn(a, b, *, name, tm=1408, tn=512, tk=1024):
    t, m = a.shape
    t2, n = b.shape
    assert t == t2
    tm, tn, tk = _tile(m, tm), _tile(n, tn), _tile(t, tk)

    def body(a_ref, b_ref, o_ref):
        part = _dot_tn(a_ref[...].astype(BF16), b_ref[...].astype(BF16))

        @pl.when(pl.program_id(2) == 0)
        def _():
            o_ref[...] = part

        @pl.when(pl.program_id(2) > 0)
        def _():
            o_ref[...] += part

    return pl.pallas_call(
        body, name=name, grid=(m // tm, n // tn, t // tk),
        in_specs=[pl.BlockSpec((tk, tm), lambda i, j, kk: (kk, i)),
                  pl.BlockSpec((tk, tn), lambda i, j, kk: (kk, j))],
        out_specs=pl.BlockSpec((tm, tn), lambda i, j, kk: (i, j)),
        out_shape=jax.ShapeDtypeStruct((m, n), F32),
        compiler_params=_params("parallel", "parallel", "arbitrary"),
    )(a, b)


def _rmsnorm_fwd(x, gain, *, name):
    t, d = x.shape
    tm = _tile(t, 512)

    def body(x_ref, g_ref, o_ref):
        xv = x_ref[...]
        r = lax.rsqrt(jnp.mean(xv * xv, axis=-1, keepdims=True) + NORM_EPS)
        o_ref[...] = (xv * r * g_ref[...]).astype(BF16)

    return pl.pallas_call(
        body, name=name, grid=(t // tm,),
        in_specs=[pl.BlockSpec((tm, d), lambda i: (i, 0)), pl.BlockSpec((1, d), lambda i: (0, 0))],
        out_specs=pl.BlockSpec((tm, d), lambda i: (i, 0)),
        out_shape=jax.ShapeDtypeStruct((t, d), BF16),
        compiler_params=_params("parallel"),
    )(x, gain)


def _rmsnorm_bwd(x, gain, dy, dres, *, name):
    t, d = x.shape
    tm = _tile(t, 512)

    def body(x_ref, g_ref, dy_ref, dres_ref, dx_ref, dg_ref):
        xv = x_ref[...]
        r = lax.rsqrt(jnp.mean(xv * xv, axis=-1, keepdims=True) + NORM_EPS)
        xh = xv * r
        dyv = dy_ref[...]
        dxh = dyv * g_ref[...]
        mean = jnp.mean(dxh * xh, axis=-1, keepdims=True)
        dx_ref[...] = dres_ref[...] + r * (dxh - xh * mean)
        part = jnp.sum(dyv * xh, axis=0, keepdims=True)

        @pl.when(pl.program_id(0) == 0)
        def _():
            dg_ref[...] = part

        @pl.when(pl.program_id(0) > 0)
        def _():
            dg_ref[...] += part

    row = pl.BlockSpec((tm, d), lambda i: (i, 0))
    vec = pl.BlockSpec((1, d), lambda i: (0, 0))
    return pl.pallas_call(
        body, name=name, grid=(t // tm,),
        in_specs=[row, vec, row, row], out_specs=[row, vec],
        out_shape=[jax.ShapeDtypeStruct((t, d), F32), jax.ShapeDtypeStruct((1, d), F32)],
        compiler_params=_params("arbitrary"),
    )(x, gain, dy, dres)


def _loss_head(y, target):
    t, d = y.shape
    tm = _tile(t, 512)
    steps = t // tm

    def body(y_ref, t_ref, dy_ref, l_ref, acc_ref):
        e = y_ref[...] - t_ref[...]
        dy_ref[...] = e * (1.0 / d)
        part = jnp.sum(e * e, axis=0, keepdims=True)

        @pl.when(pl.program_id(0) == 0)
        def _():
            acc_ref[...] = part

        @pl.when(pl.program_id(0) > 0)
        def _():
            acc_ref[...] += part

        @pl.when(pl.program_id(0) == steps - 1)
        def _():
            l_ref[...] = jnp.full((1, LANES), (0.5 / d), F32) * jnp.sum(acc_ref[...])

    row = pl.BlockSpec((tm, d), lambda i: (i, 0))
    return pl.pallas_call(
        body, name="loss_head", grid=(steps,),
        in_specs=[row, row], out_specs=[row, pl.BlockSpec((1, LANES), lambda i: (0, 0))],
        out_shape=[jax.ShapeDtypeStruct((t, d), F32), jax.ShapeDtypeStruct((1, LANES), F32)],
        scratch_shapes=[pltpu.VMEM((1, d), F32)],
        compiler_params=_params("arbitrary"),
    )(y, target)


def _swiglu_fwd(h, w_gate_t, w_up_t, *, name, side=None):
    t, d = h.shape
    f = w_gate_t.shape[0]
    tm, tn = _tile(t, 1024), _tile(f, 256)

    def body(h_ref, wg_ref, wu_ref, g_ref, u_ref, a_ref):
        hv = h_ref[...]
        g = _dot_nt(hv, wg_ref[...])
        u = _dot_nt(hv, wu_ref[...])
        g_ref[...] = g.astype(BF16)
        u_ref[...] = u.astype(BF16)
        a_ref[...] = (g * _sigmoid(g) * u).astype(BF16)

    wspec = pl.BlockSpec((tn, d), lambda i, j: (j, 0))
    ospec = pl.BlockSpec((tm, tn), lambda i, j: (i, j))
    return _call(
        body, side, name=name, grid=(t // tm, f // tn),
        in_specs=[pl.BlockSpec((tm, d), lambda i, j: (i, 0)), wspec, wspec],
        out_specs=[ospec, ospec, ospec],
        out_shape=[jax.ShapeDtypeStruct((t, f), BF16), jax.ShapeDtypeStruct((t, f), BF16),
                   jax.ShapeDtypeStruct((t, f), BF16)],
        scratch_shapes=[], semantics=("parallel", "parallel"), args=(h, w_gate_t, w_up_t),
    )


def _swiglu_bwd(dx, w_down, g, u, *, name):
    t, d = dx.shape
    f = w_down.shape[0]
    tm, tn = _tile(t, 1024), _tile(f, 256)

    def body(dx_ref, wd_ref, g_ref, u_ref, dg_ref, du_ref):
        dact = _dot_nt(dx_ref[...].astype(BF16), wd_ref[...])
        gv, uv = g_ref[...].astype(F32), u_ref[...].astype(F32)
        sg = _sigmoid(gv)
        dg_ref[...] = (dact * uv * sg * (1.0 + gv * (1.0 - sg))).astype(BF16)
        du_ref[...] = (dact * gv * sg).astype(BF16)

    ospec = pl.BlockSpec((tm, tn), lambda i, j: (i, j))
    return pl.pallas_call(
        body, name=name, grid=(t // tm, f // tn),
        in_specs=[pl.BlockSpec((tm, d), lambda i, j: (i, 0)), pl.BlockSpec((tn, d), lambda i, j: (j, 0)),
                  ospec, ospec],
        out_specs=[ospec, ospec],
        out_shape=[jax.ShapeDtypeStruct((t, f), BF16), jax.ShapeDtypeStruct((t, f), BF16)],
        compiler_params=_params("parallel", "parallel"),
    )(dx, w_down, g, u)


def _ple_fwd(x, p, w_gate, w_proj_t, *, name):
    t, d = x.shape
    e = p.shape[1]
    tm, tn = _tile(t, 1024), _tile(d, 512)

    def body(xf_ref, xr_ref, p_ref, wg_ref, wp_ref, o_ref):
        s = _dot(xf_ref[...].astype(BF16), wg_ref[...])
        ple = _dot_nt(p_ref[...].astype(BF16), wp_ref[...])
        o_ref[...] = xr_ref[...] + _sigmoid(s) * ple

    return pl.pallas_call(
        body, name=name, grid=(t // tm, d // tn),
        in_specs=[pl.BlockSpec((tm, d), lambda i, j: (i, 0)), pl.BlockSpec((tm, tn), lambda i, j: (i, j)),
                  pl.BlockSpec((tm, e), lambda i, j: (i, 0)), pl.BlockSpec((d, tn), lambda i, j: (0, j)),
                  pl.BlockSpec((tn, e), lambda i, j: (j, 0))],
        out_specs=pl.BlockSpec((tm, tn), lambda i, j: (i, j)),
        out_shape=jax.ShapeDtypeStruct((t, d), F32),
        compiler_params=_params("parallel", "parallel"),
    )(x, x, p, w_gate, w_proj_t)


def _ple_bwd(x, p, w_gate, w_proj_t, dout, *, name):
    t, d = x.shape
    e = p.shape[1]
    tm, tn = _tile(t, 1024), _tile(d, 512)

    def body(xf_ref, p_ref, wg_ref, wp_ref, do_ref, ds_ref, dple_ref):
        s = _dot(xf_ref[...].astype(BF16), wg_ref[...])
        ple = _dot_nt(p_ref[...].astype(BF16), wp_ref[...])
        gate = _sigmoid(s)
        dov = do_ref[...]
        dple_ref[...] = (dov * gate).astype(BF16)
        ds_ref[...] = (dov * ple * gate * (1.0 - gate)).astype(BF16)

    ospec = pl.BlockSpec((tm, tn), lambda i, j: (i, j))
    return pl.pallas_call(
        body, name=name, grid=(t // tm, d // tn),
        in_specs=[pl.BlockSpec((tm, d), lambda i, j: (i, 0)), pl.BlockSpec((tm, e), lambda i, j: (i, 0)),
                  pl.BlockSpec((d, tn), lambda i, j: (0, j)), pl.BlockSpec((tn, e), lambda i, j: (j, 0)), ospec],
        out_specs=[ospec, ospec],
        out_shape=[jax.ShapeDtypeStruct((t, d), BF16), jax.ShapeDtypeStruct((t, d), BF16)],
        compiler_params=_params("parallel", "parallel"),
    )(x, p, w_gate, w_proj_t, dout)


CONV_TIME_TILE = 256
CONV_HALO = 8


def _conv_taps(ext, w):
    acc = ext[CONV_HALO:, :] * w[CONV_WIDTH - 1:CONV_WIDTH, :]
    shifted = [ext[CONV_HALO:, :]]
    for j in range(1, CONV_WIDTH):
        sh = pltpu.roll(ext, j, 0)[CONV_HALO:, :]
        shifted.append(sh)
        acc = acc + sh * w[CONV_WIDTH - 1 - j:CONV_WIDTH - j, :]
    return acc, shifted


def _conv_fwd(u, w, b):
    t, c = u.shape
    tc = _tile(c, 256)
    tt = CONV_TIME_TILE

    def body(u_ref, w_ref, b_ref, o_ref):
        wv, bv = w_ref[...], b_ref[...]

        def tile(start, ext):
            pre = _conv_taps(ext, wv)[0] + bv
            o_ref[pl.ds(start, tt), :] = pre * _sigmoid(pre)

        tile(0, jnp.concatenate([jnp.zeros((CONV_HALO, tc), F32), u_ref[0:tt, :]], axis=0))

        def loop(i, carry):
            start = pl.multiple_of(i * tt, tt)
            tile(start, u_ref[pl.ds(start - CONV_HALO, tt + CONV_HALO), :])
            return carry

        lax.fori_loop(1, t // tt, loop, 0)

    col = pl.BlockSpec((t, tc), lambda j: (0, j))
    return pl.pallas_call(
        body, name="conv_fwd", grid=(c // tc,),
        in_specs=[col, pl.BlockSpec((CONV_WIDTH, tc), lambda j: (0, j)), pl.BlockSpec((1, tc), lambda j: (0, j))],
        out_specs=col, out_shape=jax.ShapeDtypeStruct((t, c), F32),
        compiler_params=_params("parallel"),
    )(u, w, b)


def _conv_bwd(u, w, b, dact):
    t, c = u.shape
    tc = _tile(c, 256)
    tt = CONV_TIME_TILE

    def body(u_ref, w_ref, b_ref, da_ref, du_ref, dw_ref, db_ref, dpre_ref):
        wv, bv = w_ref[...], b_ref[...]

        def tile(start, ext, sums):
            acc, shifted = _conv_taps(ext, wv)
            pre = acc + bv
            sg = _sigmoid(pre)
            dpre = da_ref[pl.ds(start, tt), :] * (sg * (1.0 + pre * (1.0 - sg)))
            dpre_ref[pl.ds(start, tt), :] = dpre
            new = [sums[0] + jnp.sum(dpre, axis=0, keepdims=True)]
            for j in range(CONV_WIDTH):
                new.append(sums[1 + j] + jnp.sum(dpre * shifted[j], axis=0, keepdims=True))
            return tuple(new)

        zero = jnp.zeros((1, tc), F32)
        sums = tile(0, jnp.concatenate([jnp.zeros((CONV_HALO, tc), F32), u_ref[0:tt, :]], axis=0),
                    (zero,) * (1 + CONV_WIDTH))

        def loop(i, sums):
            start = pl.multiple_of(i * tt, tt)
            return tile(start, u_ref[pl.ds(start - CONV_HALO, tt + CONV_HALO), :], sums)

        sums = lax.fori_loop(1, t // tt, loop, sums)
        db_ref[...] = sums[0]
        dw_ref[...] = jnp.concatenate([sums[1 + (CONV_WIDTH - 1 - k)] for k in range(CONV_WIDTH)], axis=0)
        dpre_ref[pl.ds(t, CONV_HALO), :] = jnp.zeros((CONV_HALO, tc), F32)

        def loop2(i, carry):
            start = pl.multiple_of(i * tt, tt)
            ext = dpre_ref[pl.ds(start, tt + CONV_HALO), :]
            acc = ext[0:tt, :] * wv[CONV_WIDTH - 1:CONV_WIDTH, :]
            for j in range(1, CONV_WIDTH):
                acc = acc + pltpu.roll(ext, tt + CONV_HALO - j, 0)[0:tt, :] * wv[CONV_WIDTH - 1 - j:CONV_WIDTH - j, :]
            du_ref[pl.ds(start, tt), :] = acc.astype(BF16)
            return carry

        lax.fori_loop(0, t // tt, loop2, 0)

    col = pl.BlockSpec((t, tc), lambda j: (0, j))
    return pl.pallas_call(
        body, name="conv_bwd", grid=(c // tc,),
        in_specs=[col, pl.BlockSpec((CONV_WIDTH, tc), lambda j: (0, j)), pl.BlockSpec((1, tc), lambda j: (0, j)), col],
        out_specs=[col, pl.BlockSpec((CONV_WIDTH, tc), lambda j: (0, j)), pl.BlockSpec((1, tc), lambda j: (0, j))],
        out_shape=[jax.ShapeDtypeStruct((t, c), BF16), jax.ShapeDtypeStruct((CONV_WIDTH, c), F32),
                   jax.ShapeDtypeStruct((1, c), F32)],
        scratch_shapes=[pltpu.VMEM((t + CONV_HALO, tc), F32)],
        compiler_params=_params("parallel"),
    )(u, w, b, dact)


def _softplus(v):
    e = jnp.exp(-jnp.abs(v))
    w = 1.0 + e
    log1p = jnp.where(w == 1.0, e, jnp.log(w) * (e / jnp.where(w == 1.0, 1.0, w - 1.0)))
    return jnp.maximum(v, 0.0) + log1p


def _ssd_prep_fwd(dt_raw, dt_bias, a_log):
    t = dt_raw.shape[0]
    cl = SSD_CHUNK

    def body(r_ref, b_ref, al_ref, dt_ref, acs_ref):
        dt = _softplus(r_ref[...] + b_ref[...])
        adt = dt * (-jnp.exp(al_ref[...]))
        li = lax.broadcasted_iota(jnp.int32, (cl, cl), 0)
        si = lax.broadcasted_iota(jnp.int32, (cl, cl), 1)
        tri = (si <= li).astype(F32)
        dt_ref[...] = dt
        acs_ref[...] = jnp.dot(tri, adt, preferred_element_type=F32, precision=HIGHEST)

    row = pl.BlockSpec((cl, LANES), lambda i: (i, 0))
    vec = pl.BlockSpec((1, LANES), lambda i: (0, 0))
    return pl.pallas_call(
        body, name="ssd_prep_fwd", grid=(t // cl,),
        in_specs=[row, vec, vec], out_specs=[row, row],
        out_shape=[jax.ShapeDtypeStruct((t, LANES), F32), jax.ShapeDtypeStruct((t, LANES), F32)],
        compiler_params=_params("parallel"),
    )(dt_raw, dt_bias, a_log)


def _ssd_prep_bwd(dt_raw, dt_bias, ddt):
    t = dt_raw.shape[0]
    tm = _tile(t, 512)

    def body(r_ref, b_ref, d_ref, o_ref, db_ref):
        g = d_ref[...] * _sigmoid(r_ref[...] + b_ref[...])
        o_ref[...] = g.astype(BF16)
        part = jnp.sum(g, axis=0, keepdims=True)

        @pl.when(pl.program_id(0) == 0)
        def _():
            db_ref[...] = part

        @pl.when(pl.program_id(0) > 0)
        def _():
            db_ref[...] += part

    row = pl.BlockSpec((tm, LANES), lambda i: (i, 0))
    vec = pl.BlockSpec((1, LANES), lambda i: (0, 0))
    return pl.pallas_call(
        body, name="ssd_prep_bwd", grid=(t // tm,),
        in_specs=[row, vec, row], out_specs=[row, vec],
        out_shape=[jax.ShapeDtypeStruct((t, LANES), BF16), jax.ShapeDtypeStruct((1, LANES), F32)],
        compiler_params=_params("arbitrary"),
    )(dt_raw, dt_bias, ddt)


GROUP_W = D_INNER // SSM_GROUPS
PAIRS_PER_GROUP = GROUP_W // LANES


def _head_cols(acs_pair, lt64):
    rolled = pltpu.roll(acs_pair, ATT_HEAD_DIM, 1)
    return jnp.where(lt64, acs_pair, rolled), jnp.where(lt64, rolled, acs_pair)


def _ssd_fwd(xbc, dt_rep, acs_rep, acs_t, dskip_rep):
    t = xbc.shape[0]
    cl = SSD_CHUNK
    nc = t // cl

    def body(xbc_ref, dt_ref, acs_ref, acst_ref, dskip_ref, y_ref, hin_ref, state_ref):
        @pl.when(pl.program_id(0) == 0)
        def _():
            state_ref[...] = jnp.zeros_like(state_ref)

        lt64 = _lane_lt64(cl)
        li = lax.broadcasted_iota(jnp.int32, (cl, cl), 0)
        si = lax.broadcasted_iota(jnp.int32, (cl, cl), 1)
        causal = li >= si
        hin_ref[...] = state_ref[...]
        for g in range(SSM_GROUPS):
            gsl = slice(g * GROUP_W, (g + 1) * GROUP_W)
            xg = xbc_ref[:, gsl]
            bg = xbc_ref[:, D_INNER + g * SSM_STATE:D_INNER + (g + 1) * SSM_STATE]
            cg = xbc_ref[:, D_INNER + SSM_GROUPS * SSM_STATE + g * SSM_STATE:
                         D_INNER + SSM_GROUPS * SSM_STATE + (g + 1) * SSM_STATE]
            acs = acs_ref[:, gsl]
            xdt = xg * dt_ref[:, gsl]
            atot = acs[cl - 1:cl, :]
            hin = state_ref[:, gsl]
            cgb = cg.astype(BF16)
            gmat = _dot_nt(cgb, bg.astype(BF16))
            yoff = _dot(cgb, hin.astype(BF16)) * jnp.exp(acs)
            snew = _dot(bg.T.astype(BF16), (xdt * jnp.exp(atot - acs)).astype(BF16))
            state_ref[:, gsl] = hin * jnp.exp(atot) + snew
            xdtb = xdt.astype(BF16)
            for pr in range(PAIRS_PER_GROUP):
                psl = slice(pr * LANES, (pr + 1) * LANES)
                cols = _head_cols(acs[:, psl], lt64)
                xp = xdtb[:, psl]
                ys = []
                for hh in range(2):
                    h = (g * PAIRS_PER_GROUP + pr) * 2 + hh
                    seg = cols[hh] - acst_ref[h:h + 1, :]
                    lm = jnp.exp(jnp.where(causal, seg, NEG_BIG))
                    ys.append(_dot((gmat * lm).astype(BF16), xp))
                ydiag = jnp.where(lt64, ys[0], ys[1])
                osl = slice(g * GROUP_W + pr * LANES, g * GROUP_W + (pr + 1) * LANES)
                y_ref[:, osl] = ydiag + yoff[:, psl] + xg[:, psl] * dskip_ref[:, osl]

    row = lambda w: pl.BlockSpec((cl, w), lambda c: (c, 0))
    return pl.pallas_call(
        body, name="ssd_fwd", grid=(nc,),
        in_specs=[row(CONV_DIM), row(D_INNER), row(D_INNER),
                  pl.BlockSpec((SSM_HEADS, cl), lambda c: (0, c)), pl.BlockSpec((1, D_INNER), lambda c: (0, 0))],
        out_specs=[row(D_INNER), pl.BlockSpec((None, SSM_STATE, D_INNER), lambda c: (c, 0, 0))],
        out_shape=[jax.ShapeDtypeStruct((t, D_INNER), F32), jax.ShapeDtypeStruct((nc, SSM_STATE, D_INNER), F32)],
        scratch_shapes=[pltpu.VMEM((SSM_STATE, D_INNER), F32)],
        compiler_params=_params("arbitrary"),
    )(xbc, dt_rep, acs_rep, acs_t, dskip_rep)


def _ssd_bwd(xbc, dt_rep, acs_rep, acs_t, dskip_rep, a_rep, hin_all, dy, side=None):
    t = xbc.shape[0]
    cl = SSD_CHUNK
    nc = t // cl

    def body(xbc_ref, dt_ref, acs_ref, acst_ref, dskip_ref, a_ref, hin_ref, dy_ref,
             dxbc_ref, ddt_ref, da_ref, dds_ref, dstate_ref, dacs_ref, dxs_ref):
        step = pl.program_id(0)

        @pl.when(step == 0)
        def _():
            dstate_ref[...] = jnp.zeros_like(dstate_ref)
            da_ref[...] = jnp.zeros_like(da_ref)
            dds_ref[...] = jnp.zeros_like(dds_ref)

        bd = _head_block_diag()
        lt64 = _lane_lt64(cl)
        li = lax.broadcasted_iota(jnp.int32, (cl, cl), 0)
        si = lax.broadcasted_iota(jnp.int32, (cl, cl), 1)
        lower = li >= si
        upper = si >= li
        last_row = lax.broadcasted_iota(jnp.int32, (cl, GROUP_W), 0) == cl - 1
        for g in range(SSM_GROUPS):
            gsl = slice(g * GROUP_W, (g + 1) * GROUP_W)
            bsl = slice(D_INNER + g * SSM_STATE, D_INNER + (g + 1) * SSM_STATE)
            csl = slice(D_INNER + SSM_GROUPS * SSM_STATE + g * SSM_STATE,
                        D_INNER + SSM_GROUPS * SSM_STATE + (g + 1) * SSM_STATE)
            xg = xbc_ref[:, gsl]
            bg = xbc_ref[:, bsl]
            cg = xbc_ref[:, csl]
            bgb, cgb = bg.astype(BF16), cg.astype(BF16)
            acs = acs_ref[:, gsl]
            xdt = xg * dt_ref[:, gsl]
            atot = acs[cl - 1:cl, :]
            eg = jnp.exp(acs)
            dk = jnp.exp(atot - acs)
            etot = jnp.exp(atot)
            hin = hin_ref[:, gsl]
            hinb = hin.astype(BF16)
            dh = dstate_ref[:, gsl]
            dhb = dh.astype(BF16)
            dyg = dy_ref[:, gsl]

            gmat = _dot_nt(cgb, bgb)
            gmat_t = _dot_nt(bgb, cgb)
            ch = _dot(cgb, hinb)
            dacs = _head_sums(dyg * ch * eg, bd)
            dye = (dyg * eg).astype(BF16)
            dc = _dot_nt(dye, hinb)
            dhin = _dot(cg.T.astype(BF16), dye)
            bdh = _dot(bgb, dhb)
            dxs = bdh * dk
            xdk = xdt * dk
            db = _dot_nt(xdk.astype(BF16), dhb)
            ddk = _head_sums(bdh * xdk, bd)
            dacs = dacs - ddk
            datot = jnp.sum(ddk, axis=0, keepdims=True) + etot * _head_sums(
                jnp.sum(dh * hin, axis=0, keepdims=True), bd)
            dacs = dacs + jnp.where(last_row, datot, 0.0)
            dstate_ref[:, gsl] = dh * etot + dhin

            xdtb = xdt.astype(BF16)
            dgsum = jnp.zeros((cl, cl), F32)
            dgsum_t = jnp.zeros((cl, cl), F32)
            for pr in range(PAIRS_PER_GROUP):
                psl = slice(pr * LANES, (pr + 1) * LANES)
                cols = _head_cols(acs[:, psl], lt64)
                xp = xdtb[:, psl]
                dyp = dyg[:, psl].astype(BF16)
                dx1, dac = [], []
                for hh in range(2):
                    h = (g * PAIRS_PER_GROUP + pr) * 2 + hh
                    mine = lt64 if hh == 0 else jnp.logical_not(lt64)
                    row = acst_ref[h:h + 1, :]
                    lm = jnp.exp(jnp.where(lower, cols[hh] - row, NEG_BIG))
                    lm_t = jnp.exp(jnp.where(upper, row - cols[hh], NEG_BIG))
                    dyh = jnp.where(mine, dyp, jnp.zeros_like(dyp))
                    xh = jnp.where(mine, xp, jnp.zeros_like(xp))
                    dm = _dot_nt(dyh, xp)
                    dm_t = _dot_nt(xh, dyp)
                    m_t = gmat_t * lm_t
                    dx1.append(_dot(m_t.astype(BF16), dyp))
                    w = dm * (gmat * lm)
                    w_t = dm_t * m_t
                    dac.append(jnp.sum(w, axis=1, keepdims=True) - jnp.sum(w_t, axis=1, keepdims=True))
                    dgsum = dgsum + dm * lm
                    dgsum_t = dgsum_t + dm_t * lm_t
                osl = slice(g * GROUP_W + pr * LANES, g * GROUP_W + (pr + 1) * LANES)
                dxs_ref[:, osl] = dxs[:, psl] + jnp.where(lt64, dx1[0], dx1[1])
                dacs_ref[:, osl] = dacs[:, psl] + jnp.where(lt64, jnp.broadcast_to(dac[0], (cl, LANES)),
                                                             jnp.broadcast_to(dac[1], (cl, LANES)))
            dxbc_ref[:, csl] = dc + _dot(dgsum.astype(BF16), bgb)
            dxbc_ref[:, bsl] = db + _dot(dgsum_t.astype(BF16), cgb)

        dadt = _split_dot(upper.astype(BF16), dacs_ref[...])
        xall = xbc_ref[:, 0:D_INNER]
        dtall = dt_ref[...]
        dxsall = dxs_ref[...]
        dyall = dy_ref[...]
        ddt_ref[...] = dadt * a_ref[...] + _head_sums(dxsall * xall, bd)
        dxbc_ref[:, 0:D_INNER] = dxsall * dtall + dyall * dskip_ref[...]
        da_ref[...] += jnp.sum(dadt * dtall, axis=0, keepdims=True)
        dds_ref[...] += jnp.sum(dyall * xall, axis=0, keepdims=True)

        @pl.when(step == nc - 1)
        def _():
            dds_ref[...] = _head_sums(dds_ref[...], bd)

    row = lambda w: pl.BlockSpec((cl, w), lambda c: (nc - 1 - c, 0))
    vec = pl.BlockSpec((1, D_INNER), lambda c: (0, 0))
    return _call(
        body, side, name="ssd_bwd", grid=(nc,),
        in_specs=[row(CONV_DIM), row(D_INNER), row(D_INNER),
                  pl.BlockSpec((SSM_HEADS, cl), lambda c: (0, nc - 1 - c)), vec, vec,
                  pl.BlockSpec((None, SSM_STATE, D_INNER), lambda c: (nc - 1 - c, 0, 0)), row(D_INNER)],
        out_specs=[row(CONV_DIM), row(D_INNER), vec, vec],
        out_shape=[jax.ShapeDtypeStruct((t, CONV_DIM), F32), jax.ShapeDtypeStruct((t, D_INNER), F32),
                   jax.ShapeDtypeStruct((1, D_INNER), F32), jax.ShapeDtypeStruct((1, D_INNER), F32)],
        scratch_shapes=[pltpu.VMEM((SSM_STATE, D_INNER), F32), pltpu.VMEM((cl, D_INNER), F32),
                        pltpu.VMEM((cl, D_INNER), F32)],
        semantics=("arbitrary",), args=(xbc, dt_rep, acs_rep, acs_t, dskip_rep, a_rep, hin_all, dy),
    )


def _gate_norm_fwd(y, z, w):
    t, c = y.shape
    tm = _tile(t, 256)

    def body(y_ref, z_ref, w_ref, o_ref):
        for g in range(SSM_GROUPS):
            gsl = slice(g * GROUP_W, (g + 1) * GROUP_W)
            zv = z_ref[:, gsl]
            v = y_ref[:, gsl] * (zv * _sigmoid(zv))
            r = lax.rsqrt(jnp.mean(v * v, axis=-1, keepdims=True) + NORM_EPS)
            o_ref[:, gsl] = (v * r * w_ref[:, gsl]).astype(BF16)

    row = pl.BlockSpec((tm, c), lambda i: (i, 0))
    return pl.pallas_call(
        body, name="gate_norm_fwd", grid=(t // tm,),
        in_specs=[row, row, pl.BlockSpec((1, c), lambda i: (0, 0))], out_specs=row,
        out_shape=jax.ShapeDtypeStruct((t, c), BF16),
        compiler_params=_params("parallel"),
    )(y, z, w)


def _gate_norm_bwd(y, z, w, dout):
    t, c = y.shape
    tm = _tile(t, 256)

    def body(y_ref, z_ref, w_ref, do_ref, dy_ref, dz_ref, dw_ref):
        @pl.when(pl.program_id(0) == 0)
        def _():
            dw_ref[...] = jnp.zeros_like(dw_ref)

        for g in range(SSM_GROUPS):
            gsl = slice(g * GROUP_W, (g + 1) * GROUP_W)
            zv, yv, dov = z_ref[:, gsl], y_ref[:, gsl], do_ref[:, gsl]
            sg = _sigmoid(zv)
            sz = zv * sg
            v = yv * sz
            r = lax.rsqrt(jnp.mean(v * v, axis=-1, keepdims=True) + NORM_EPS)
            vh = v * r
            dvh = dov * w_ref[:, gsl]
            mean = jnp.mean(dvh * vh, axis=-1, keepdims=True)
            dv = r * (dvh - vh * mean)
            dy_ref[:, gsl] = dv * sz
            dz_ref[:, gsl] = (dv * yv * (sg * (1.0 + zv * (1.0 - sg)))).astype(BF16)
            dw_ref[:, gsl] += jnp.sum(dov * vh, axis=0, keepdims=True)

    row = pl.BlockSpec((tm, c), lambda i: (i, 0))
    vec = pl.BlockSpec((1, c), lambda i: (0, 0))
    return pl.pallas_call(
        body, name="gate_norm_bwd", grid=(t // tm,),
        in_specs=[row, row, vec, row], out_specs=[row, row, vec],
        out_shape=[jax.ShapeDtypeStruct((t, c), F32), jax.ShapeDtypeStruct((t, c), BF16),
                   jax.ShapeDtypeStruct((1, c), F32)],
        compiler_params=_params("arbitrary"),
    )(y, z, w, dout)


ATT_W = ATT_HEADS * ATT_HEAD_DIM
N_QKV_BLOCKS = 9
ATT_SCALE = 1.0 / math.sqrt(ATT_HEAD_DIM)


def _head_rmsnorm(x, gain, bd):
    ms = _head_sums(x * x, bd) * (1.0 / ATT_HEAD_DIM)
    return x * lax.rsqrt(ms + NORM_EPS) * gain


def _class_rows(ref, blk, r, dil):
    span = ATT_BLOCK * dil
    sub = ref.at[pl.ds(pl.multiple_of(blk * span, span), span), :]
    return sub[...] if dil == 1 else sub[pl.ds(r, ATT_BLOCK, stride=dil), :]


def _store_class_rows(ref, blk, r, dil, val):
    span = ATT_BLOCK * dil
    sub = ref.at[pl.ds(pl.multiple_of(blk * span, span), span), :]
    if dil == 1:
        sub[...] = val
    else:
        sub[pl.ds(r, ATT_BLOCK, stride=dil), :] = val


def _qk_norm_bwd(qkv, gq, gk, grads):
    t = qkv.shape[0]
    tm = _tile(t, 256)

    def body(x_ref, gq_ref, gk_ref, *rest):
        g_refs = rest[:N_QKV_BLOCKS]
        o_ref, dgq_ref, dgk_ref = rest[N_QKV_BLOCKS:]
        cb = pl.program_id(1)

        @pl.when(jnp.logical_and(pl.program_id(0) == 0, cb == 0))
        def _():
            dgq_ref[...] = jnp.zeros_like(dgq_ref)
            dgk_ref[...] = jnp.zeros_like(dgk_ref)

        def norm_bwd(dy, gain, dg_ref):
            bd = _head_block_diag()
            xv = x_ref[...]
            ms = _head_sums(xv * xv, bd) * (1.0 / ATT_HEAD_DIM)
            r = lax.rsqrt(ms + NORM_EPS)
            xh = xv * r
            dxh = dy * gain
            mean = _head_sums(dxh * xh, bd) * (1.0 / ATT_HEAD_DIM)
            o_ref[...] = (r * (dxh - xh * mean)).astype(BF16)
            dg_ref[...] += jnp.sum(dy * xh, axis=0, keepdims=True)

        for k in range(N_QKV_BLOCKS):
            @pl.when(cb == k)
            def _(k=k):
                if k % 3 == 0:
                    norm_bwd(g_refs[k][...], gq_ref[...], dgq_ref)
                elif k % 3 == 1:
                    norm_bwd(g_refs[k][...], gk_ref[...], dgk_ref)
                else:
                    o_ref[...] = g_refs[k][...].astype(BF16)

    blk = pl.BlockSpec((tm, ATT_W), lambda i, j: (i, j))
    one = pl.BlockSpec((tm, ATT_W), lambda i, j: (i, 0))
    vec = pl.BlockSpec((1, ATT_W), lambda i, j: (0, 0))
    return pl.pallas_call(
        body, name="qk_norm_bwd", grid=(t // tm, N_QKV_BLOCKS),
        in_specs=[blk, vec, vec] + [one] * N_QKV_BLOCKS, out_specs=[blk, vec, vec],
        out_shape=[jax.ShapeDtypeStruct(qkv.shape, BF16), jax.ShapeDtypeStruct((1, ATT_W), F32),
                   jax.ShapeDtypeStruct((1, ATT_W), F32)],
        compiler_params=_params("arbitrary", "arbitrary"),
    )(qkv, gq, gk, *grads)


PAIRS = ATT_HEADS // 2


def _pair_col(g, j):
    return lambda pair: (0, (g * 3 + j) * PAIRS + pair)


def _pair_slopes(pair):
    steps = jnp.full((1, 2 * ATT_BLOCK), 2 * pair + 1, jnp.int32).astype(F32)
    first = jnp.exp(steps * (-0.5 * math.log(2.0)))
    return first, first * (2.0 ** -0.5)


NORM_ROWS = 512


def _band2(pair, dil, transposed):
    bq = ATT_BLOCK
    a = lax.broadcasted_iota(jnp.int32, (2 * bq, 2 * bq), 0) % bq
    b = lax.broadcasted_iota(jnp.int32, (2 * bq, 2 * bq), 1)
    dist = (b - a) if transposed else (a + bq - b)
    in_band = (dist >= 0) & (dist <= bq)
    s0, s1 = _pair_slopes(pair)
    first_head = lax.broadcasted_iota(jnp.int32, (2 * bq, 2 * bq), 0) < bq
    bias = jnp.where(first_head, s0, s1) * (dist.astype(F32) * float(dil))
    return in_band, bias, b


def _stack_heads(tile):
    rows = lax.broadcasted_iota(jnp.int32, (2 * ATT_BLOCK, LANES), 0) < ATT_BLOCK
    lanes = lax.broadcasted_iota(jnp.int32, (2 * ATT_BLOCK, LANES), 1) < ATT_HEAD_DIM
    both = jnp.concatenate([tile, tile], axis=0)
    return jnp.where(rows == lanes, both, jnp.zeros_like(both))


def _unstack_heads(stacked, lt64):
    return jnp.where(lt64, stacked[:ATT_BLOCK], stacked[ATT_BLOCK:])


def _block_loop(nb, dil, step):
    per_trip = 4 if dil == 1 else 1

    def trip(i, carry):
        for b in range(per_trip):
            step(i * per_trip + b, carry)
        return carry

    lax.fori_loop(0, nb // per_trip, trip, 0)


def _normalise_qk(q_ref, k_ref, gq_ref, gk_ref, qn_ref, kn_ref):
    bd = _head_block_diag()

    def step(i, carry):
        rows = pl.ds(pl.multiple_of(i * NORM_ROWS, NORM_ROWS), NORM_ROWS)
        qn_ref[rows, :] = _head_rmsnorm(q_ref[rows, :], gq_ref[...], bd)
        kn_ref[rows, :] = _head_rmsnorm(k_ref[rows, :], gk_ref[...], bd)
        return carry

    lax.fori_loop(0, q_ref.shape[0] // NORM_ROWS, step, 0)


def _attn_fwd(qkv, gq, gk, g, dil):
    t = qkv.shape[0]
    nb = t // dil // ATT_BLOCK
    bq = ATT_BLOCK

    def body(q_ref, k_ref, v_ref, gq_ref, gk_ref, o_ref, l_ref, qn_ref, kn_ref):
        _normalise_qk(q_ref, k_ref, gq_ref, gk_ref, qn_ref, kn_ref)
        lt64 = _lane_lt64(bq)
        in_band, bias, key = _band2(pl.program_id(0), dil, False)

        def step(n, carry):
            valid = in_band & ((key >= bq) | (n > 0))
            prev = jnp.maximum(n - 1, 0)
            for r in range(dil):
                q2 = _stack_heads(_class_rows(qn_ref, n, r, dil).astype(BF16))
                kcat = jnp.concatenate([_class_rows(kn_ref, prev, r, dil), _class_rows(kn_ref, n, r, dil)],
                                       axis=0).astype(BF16)
                vcat = jnp.concatenate([_class_rows(v_ref, prev, r, dil), _class_rows(v_ref, n, r, dil)],
                                       axis=0).astype(BF16)
                s = jnp.where(valid, _dot_nt(q2, kcat) * ATT_SCALE - bias, NEG_BIG)
                m = jnp.max(s, axis=1, keepdims=True)
                p = jnp.exp(s - m)
                l = jnp.sum(p, axis=1, keepdims=True)
                out = _dot(p.astype(BF16), vcat) * (1.0 / l)
                lse = jnp.broadcast_to(m + jnp.log(l), (2 * bq, LANES))
                _store_class_rows(o_ref, n, r, dil, _unstack_heads(out, lt64))
                _store_class_rows(l_ref, n, r, dil, _unstack_heads(lse, lt64))
            return carry

        _block_loop(nb, dil, step)

    col = lambda j: pl.BlockSpec((t, LANES), _pair_col(g, j))
    vec = pl.BlockSpec((1, LANES), lambda pair: (0, 0))
    out = pl.BlockSpec((t, LANES), lambda pair: (0, pair))
    return pl.pallas_call(
        body, name=f"attn_fwd_g{g}", grid=(PAIRS,),
        in_specs=[col(0), col(1), col(2), vec, vec], out_specs=[out, out],
        out_shape=[jax.ShapeDtypeStruct((t, ATT_W), F32), jax.ShapeDtypeStruct((t, ATT_W), F32)],
        scratch_shapes=[pltpu.VMEM((t, LANES), F32), pltpu.VMEM((t, LANES), F32)],
        compiler_params=_params("parallel"),
    )(qkv, qkv, qkv, gq, gk)


def _attn_combine_fwd(outs, lses):
    t = outs[0].shape[0]
    tm = _tile(t, 256)

    def body(o0, o1, o2, l0, l1, l2, ob_ref, of_ref, lt_ref):
        a, b, c = l0[...], l1[...], l2[...]
        m = jnp.maximum(jnp.maximum(a, b), c)
        ea, eb, ec = jnp.exp(a - m), jnp.exp(b - m), jnp.exp(c - m)
        ssum = ea + eb + ec
        o = (ea * o0[...] + eb * o1[...] + ec * o2[...]) / ssum
        ob_ref[...] = o.astype(BF16)
        of_ref[...] = o
        lt_ref[...] = m + jnp.log(ssum)

    row = pl.BlockSpec((tm, ATT_W), lambda i: (i, 0))
    return pl.pallas_call(
        body, name="attn_combine_fwd", grid=(t // tm,),
        in_specs=[row] * 6, out_specs=[row] * 3,
        out_shape=[jax.ShapeDtypeStruct((t, ATT_W), BF16), jax.ShapeDtypeStruct((t, ATT_W), F32),
                   jax.ShapeDtypeStruct((t, ATT_W), F32)],
        compiler_params=_params("parallel"),
    )(*outs, *lses)


def _attn_combine_bwd(do, o):
    t = do.shape[0]
    tm = _tile(t, 256)

    def body(do_ref, o_ref, dl_ref):
        dl_ref[...] = _head_sums(do_ref[...] * o_ref[...], _head_block_diag())

    row = pl.BlockSpec((tm, ATT_W), lambda i: (i, 0))
    return pl.pallas_call(
        body, name="attn_combine_bwd", grid=(t // tm,),
        in_specs=[row, row], out_specs=row, out_shape=jax.ShapeDtypeStruct((t, ATT_W), F32),
        compiler_params=_params("parallel"),
    )(do, o)


def _attn_bwd_dq(qkv, gq, gk, do, l_rep, dl_rep, g, dil):
    t = qkv.shape[0]
    nb = t // dil // ATT_BLOCK
    bq = ATT_BLOCK

    def body(q_ref, k_ref, v_ref, gq_ref, gk_ref, do_ref, l_ref, dl_ref, dq_ref, qn_ref, kn_ref):
        _normalise_qk(q_ref, k_ref, gq_ref, gk_ref, qn_ref, kn_ref)
        lt64 = _lane_lt64(bq)
        in_band, bias, key = _band2(pl.program_id(0), dil, False)

        def per_row(tile):
            cols = _head_cols(tile, lt64)
            half = jnp.concatenate([cols[0], cols[1]], axis=0)
            return jnp.concatenate([half, half], axis=1)

        def step(n, carry):
            valid = in_band & ((key >= bq) | (n > 0))
            prev = jnp.maximum(n - 1, 0)
            for r in range(dil):
                q2 = _stack_heads(_class_rows(qn_ref, n, r, dil).astype(BF16))
                do2 = _stack_heads(_class_rows(do_ref, n, r, dil).astype(BF16))
                kcat = jnp.concatenate([_class_rows(kn_ref, prev, r, dil), _class_rows(kn_ref, n, r, dil)],
                                       axis=0).astype(BF16)
                vcat = jnp.concatenate([_class_rows(v_ref, prev, r, dil), _class_rows(v_ref, n, r, dil)],
                                       axis=0).astype(BF16)
                s = jnp.where(valid, _dot_nt(q2, kcat) * ATT_SCALE - bias, NEG_BIG)
                p = jnp.exp(s - per_row(_class_rows(l_ref, n, r, dil)))
                ds = p * (_dot_nt(do2, vcat) - per_row(_class_rows(dl_ref, n, r, dil)))
                dq = _dot(ds.astype(BF16), kcat) * ATT_SCALE
                _store_class_rows(dq_ref, n, r, dil, _unstack_heads(dq, lt64))
            return carry

        _block_loop(nb, dil, step)

    col = lambda j: pl.BlockSpec((t, LANES), _pair_col(g, j))
    vec = pl.BlockSpec((1, LANES), lambda pair: (0, 0))
    tok = pl.BlockSpec((t, LANES), lambda pair: (0, pair))
    return pl.pallas_call(
        body, name=f"attn_bwd_dq_g{g}", grid=(PAIRS,),
        in_specs=[col(0), col(1), col(2), vec, vec, tok, tok, tok], out_specs=tok,
        out_shape=jax.ShapeDtypeStruct((t, ATT_W), F32),
        scratch_shapes=[pltpu.VMEM((t, LANES), F32), pltpu.VMEM((t, LANES), F32)],
        compiler_params=_params("parallel"),
    )(qkv, qkv, qkv, gq, gk, do, l_rep, dl_rep)


def _attn_bwd_dkv(qkv, gq, gk, do, l_row, dl_row, g, dil):
    t = qkv.shape[0]
    nb = t // dil // ATT_BLOCK
    bq = ATT_BLOCK

    def body(q_ref, k_ref, v_ref, gq_ref, gk_ref, do_ref, l_ref, dl_ref, dk_ref, dv_ref, qn_ref, kn_ref):
        _normalise_qk(q_ref, k_ref, gq_ref, gk_ref, qn_ref, kn_ref)
        lt64 = _lane_lt64(bq)
        in_band, bias, query = _band2(pl.program_id(0), dil, True)

        def per_query(ref, lane_c, lane_n):
            heads = [jnp.broadcast_to(jnp.concatenate([ref[hh:hh + 1, pl.ds(lane_c, bq)],
                                                        ref[hh:hh + 1, pl.ds(lane_n, bq)]], axis=1), (bq, 2 * bq))
                     for hh in range(2)]
            return jnp.concatenate(heads, axis=0)

        def step(n, carry):
            valid = in_band & ((query < bq) | (n < nb - 1))
            nxt = jnp.minimum(n + 1, nb - 1)
            for r in range(dil):
                k2 = _stack_heads(_class_rows(kn_ref, n, r, dil).astype(BF16))
                v2 = _stack_heads(_class_rows(v_ref, n, r, dil).astype(BF16))
                qcat = jnp.concatenate([_class_rows(qn_ref, n, r, dil), _class_rows(qn_ref, nxt, r, dil)],
                                       axis=0).astype(BF16)
                docat = jnp.concatenate([_class_rows(do_ref, n, r, dil), _class_rows(do_ref, nxt, r, dil)],
                                        axis=0).astype(BF16)
                lane_c = pl.multiple_of((r * nb + n) * bq, bq)
                lane_n = pl.multiple_of((r * nb + nxt) * bq, bq)
                s_t = jnp.where(valid, _dot_nt(k2, qcat) * ATT_SCALE - bias, NEG_BIG)
                p_t = jnp.exp(s_t - per_query(l_ref, lane_c, lane_n))
                dv = _dot(p_t.astype(BF16), docat)
                ds_t = p_t * (_dot_nt(v2, docat) - per_query(dl_ref, lane_c, lane_n))
                dk = _dot(ds_t.astype(BF16), qcat) * ATT_SCALE
                _store_class_rows(dk_ref, n, r, dil, _unstack_heads(dk, lt64))
                _store_class_rows(dv_ref, n, r, dil, _unstack_heads(dv, lt64))
            return carry

        _block_loop(nb, dil, step)

    col = lambda j: pl.BlockSpec((t, LANES), _pair_col(g, j))
    vec = pl.BlockSpec((1, LANES), lambda pair: (0, 0))
    tok = pl.BlockSpec((t, LANES), lambda pair: (0, pair))
    rows = pl.BlockSpec((None, 8, t), lambda pair: (pair, 0, 0))
    return pl.pallas_call(
        body, name=f"attn_bwd_dkv_g{g}", grid=(PAIRS,),
        in_specs=[col(0), col(1), col(2), vec, vec, tok, rows, rows], out_specs=[tok, tok],
        out_shape=[jax.ShapeDtypeStruct((t, ATT_W), F32), jax.ShapeDtypeStruct((t, ATT_W), F32)],
        scratch_shapes=[pltpu.VMEM((t, LANES), F32), pltpu.VMEM((t, LANES), F32)],
        compiler_params=_params("parallel"),
    )(qkv, qkv, qkv, gq, gk, do, l_row, dl_row)


def _rows_by_residue(rep, dil):
    t = rep.shape[0]
    per_head = rep[:, ::ATT_HEAD_DIM]
    rows = per_head.reshape(t // dil, dil, ATT_HEADS).transpose(2, 1, 0).reshape(PAIRS, 2, t)
    return jnp.pad(rows, ((0, 0), (0, 6), (0, 0)))


def _per_head(rep_row):
    return rep_row[0, ::SSM_HEAD_DIM]


def _rep_heads(v):
    return jnp.repeat(v, SSM_HEAD_DIM)[None, :]


def _pad_lanes(v):
    return jnp.pad(v, ((0, 0), (0, LANES - v.shape[1])))


class _NoOverlap:
    def side(self, host):
        return None

    def after(self, host, prm):
        pass

    def layer1_backward_done(self, grads):
        pass


def _ffn_ple_fwd(x1, p_i, prm, i, plan):
    h = _rmsnorm_fwd(x1, prm["norm_ffn"][i:i + 1], name=f"ffn_norm_fwd_{i}")
    g, u, act = _swiglu_fwd(h, prm["ffn_w_gate"][i], prm["ffn_w_up"][i], name=f"swiglu_fwd_{i}",
                            side=plan.side(f"swiglu_fwd_{i}"))
    plan.after(f"swiglu_fwd_{i}", prm)
    x2 = _matmul(act, prm["ffn_w_down"][i], mode="nn", addend=x1, name=f"ffn_down_{i}")
    x3 = _ple_fwd(x2, p_i, prm["ple_w_gate"][i], prm["ple_w_proj"][i], name=f"ple_fwd_{i}")
    return x3, dict(x1=x1, h=h, g=g, u=u, act=act, x2=x2)


def _ffn_ple_bwd(dx3, p_i, prm, i, sv, grads):
    ds, dple = _ple_bwd(sv["x2"], p_i, prm["ple_w_gate"][i], prm["ple_w_proj"][i], dx3, name=f"ple_bwd_{i}")
    grads["ple_w_gate"][i] = _matmul_tn(sv["x2"], ds, name=f"d_ple_w_gate_{i}")
    grads["ple_w_proj"][i] = _matmul_tn(dple, p_i, name=f"d_ple_w_proj_{i}")
    dx2 = _matmul(ds, prm["ple_w_gate"][i], mode="nt", addend=dx3, name=f"ple_dx_{i}")
    grads["ffn_w_down"][i] = _matmul_tn(sv["act"], dx2, name=f"d_ffn_w_down_{i}")
    dg, du = _swiglu_bwd(dx2, prm["ffn_w_down"][i], sv["g"], sv["u"], name=f"swiglu_bwd_{i}")
    grads["ffn_w_gate"][i] = _matmul_tn(dg, sv["h"], name=f"d_ffn_w_gate_{i}")
    grads["ffn_w_up"][i] = _matmul_tn(du, sv["h"], name=f"d_ffn_w_up_{i}")
    dh = _matmul(dg, prm["ffn_w_gate"][i], mode="nn", name=f"ffn_dh_gate_{i}")
    dh = _matmul(du, prm["ffn_w_up"][i], mode="nn", addend=dh, name=f"ffn_dh_up_{i}")
    dx1, dgain = _rmsnorm_bwd(sv["x1"], prm["norm_ffn"][i:i + 1], dh, dx2, name=f"ffn_norm_bwd_{i}")
    grads["norm_ffn"][i] = dgain[0]
    return dx1


def _mamba_fwd(x0, prm):
    h = _rmsnorm_fwd(x0, prm["norm_mix"][0:1], name="mix_norm_fwd_0")
    z = _matmul(h, prm["ssm_w_z"], mode="nt", name="ssm_in_z")
    xbc_pre = _matmul(h, prm["ssm_w_xbc"], mode="nt", name="ssm_in_xbc")
    dt_raw = _matmul(h, prm["ssm_w_dt"], mode="nt", name="ssm_in_dt")
    xbc = _conv_fwd(xbc_pre, prm["ssm_conv_w"], prm["ssm_conv_b"])
    dt_bias = _pad_lanes(prm["ssm_dt_bias"])
    a_log = _pad_lanes(prm["ssm_a_log"])
    dt, acs = _ssd_prep_fwd(dt_raw, dt_bias, a_log)
    dt_rep = jnp.repeat(dt[:, :SSM_HEADS], SSM_HEAD_DIM, axis=1)
    acs_rep = jnp.repeat(acs[:, :SSM_HEADS], SSM_HEAD_DIM, axis=1)
    acs_t = acs[:, :SSM_HEADS].T
    dskip_rep = _rep_heads(prm["ssm_d_skip"][0])
    y, hin_all = _ssd_fwd(xbc, dt_rep, acs_rep, acs_t, dskip_rep)
    yn = _gate_norm_fwd(y, z, prm["ssm_norm_w"])
    x1 = _matmul(yn, prm["ssm_w_out"], mode="nn", addend=x0, name="ssm_out")
    sv = dict(x0=x0, h=h, z=z, xbc_pre=xbc_pre, dt_raw=dt_raw, xbc=xbc, dt_bias=dt_bias, dt_rep=dt_rep,
              acs_rep=acs_rep, acs_t=acs_t, dskip_rep=dskip_rep, y=y, hin_all=hin_all, yn=yn)
    return x1, sv


def _mamba_bwd(dx1, prm, sv, grads, plan):
    grads["ssm_w_out"] = _matmul_tn(sv["yn"], dx1, name="d_ssm_w_out")
    dyn = _matmul(dx1, prm["ssm_w_out"], mode="nt", name="ssm_out_dx")
    dy, dz, dnw = _gate_norm_bwd(sv["y"], sv["z"], prm["ssm_norm_w"], dyn)
    grads["ssm_norm_w"] = dnw
    a_rep = _rep_heads(-jnp.exp(prm["ssm_a_log"][0]))
    dxbc, ddt_rep, da_rep, dds_rep = _ssd_bwd(sv["xbc"], sv["dt_rep"], sv["acs_rep"], sv["acs_t"], sv["dskip_rep"],
                                              a_rep, sv["hin_all"], dy, side=plan.side("ssd_bwd"))
    plan.after("ssd_bwd", prm)
    grads["ssm_d_skip"] = _per_head(dds_rep)[None, :]
    grads["ssm_a_log"] = (_per_head(da_rep) * _per_head(a_rep))[None, :]
    ddt = _pad_lanes(ddt_rep[:, ::SSM_HEAD_DIM])
    ddt_raw, dbias = _ssd_prep_bwd(sv["dt_raw"], sv["dt_bias"], ddt)
    grads["ssm_dt_bias"] = dbias[:, :SSM_HEADS]
    du, dcw, dcb = _conv_bwd(sv["xbc_pre"], prm["ssm_conv_w"], prm["ssm_conv_b"], dxbc)
    grads["ssm_conv_w"] = dcw
    grads["ssm_conv_b"] = dcb
    h = sv["h"]
    grads["ssm_w_in"] = jnp.concatenate(
        [_matmul_tn(dz, h, name="d_ssm_w_z"), _matmul_tn(du, h, name="d_ssm_w_xbc"),
         _matmul_tn(ddt_raw, h, name="d_ssm_w_dt")[:SSM_HEADS]], axis=0)
    dh = _matmul(dz, prm["ssm_w_z"], mode="nn", name="ssm_dh_z")
    dh = _matmul(du, prm["ssm_w_xbc"], mode="nn", addend=dh, name="ssm_dh_xbc")
    dh = _matmul(ddt_raw, prm["ssm_w_dt"], mode="nn", addend=dh, name="ssm_dh_dt")
    dx0, dgain = _rmsnorm_bwd(sv["x0"], prm["norm_mix"][0:1], dh, dx1, name="mix_norm_bwd_0")
    grads["norm_mix"][0] = dgain[0]
    return dx0


def _attn_mixer_fwd(x0, prm, plan):
    h = _rmsnorm_fwd(x0, prm["norm_mix"][1:2], name="mix_norm_fwd_1")
    qkv = _matmul(h, prm["att_w_qkv"], mode="nt", name="att_qkv", side=plan.side("att_qkv"))
    plan.after("att_qkv", prm)
    gq = jnp.tile(prm["att_q_norm"], (1, ATT_HEADS))
    gk = jnp.tile(prm["att_k_norm"], (1, ATT_HEADS))
    gq2, gk2 = gq[:, :LANES], gk[:, :LANES]
    outs, lses = [], []
    for g, (window, dil) in enumerate(DIL_PATTERNS):
        o_g, l_g = _attn_fwd(qkv, gq2, gk2, g, dil)
        outs.append(o_g)
        lses.append(l_g)
    o_b, o_f, l_rep = _attn_combine_fwd(outs, lses)
    x1 = _matmul(o_b, prm["att_w_o"], mode="nn", addend=x0, name="att_out")
    sv = dict(x0=x0, h=h, qkv=qkv, gq=gq, gk=gk, gq2=gq2, gk2=gk2, o_b=o_b, o_f=o_f, l_rep=l_rep)
    return x1, sv


def _attn_mixer_bwd(dx1, prm, sv, grads):
    grads["att_w_o"] = _matmul_tn(sv["o_b"], dx1, name="d_att_w_o")
    do = _matmul(dx1, prm["att_w_o"], mode="nt", name="att_out_dx")
    dl_rep = _attn_combine_bwd(do, sv["o_f"])
    blocks = [None] * N_QKV_BLOCKS
    for g, (window, dil) in enumerate(DIL_PATTERNS):
        blocks[3 * g] = _attn_bwd_dq(sv["qkv"], sv["gq2"], sv["gk2"], do, sv["l_rep"], dl_rep, g, dil)
        dk, dv = _attn_bwd_dkv(sv["qkv"], sv["gq2"], sv["gk2"], do, _rows_by_residue(sv["l_rep"], dil),
                               _rows_by_residue(dl_rep, dil), g, dil)
        blocks[3 * g + 1] = dk
        blocks[3 * g + 2] = dv
    dqkv, dgq, dgk = _qk_norm_bwd(sv["qkv"], sv["gq"], sv["gk"], blocks)
    grads["att_q_norm"] = dgq.reshape(ATT_HEADS, ATT_HEAD_DIM).sum(axis=0)[None, :]
    grads["att_k_norm"] = dgk.reshape(ATT_HEADS, ATT_HEAD_DIM).sum(axis=0)[None, :]
    grads["att_w_qkv"] = _matmul_tn(dqkv, sv["h"], name="d_att_w_qkv")
    dh = _matmul(dqkv, prm["att_w_qkv"], mode="nn", name="att_qkv_dx")
    dx0, dgain = _rmsnorm_bwd(sv["x0"], prm["norm_mix"][1:2], dh, dx1, name="mix_norm_bwd_1")
    grads["norm_mix"][1] = dgain[0]
    return dx0


def _local_step(x, p, target, prm, plan=None):
    plan = plan or _NoOverlap()
    grads = {k: [None, None] for k in ("norm_mix", "norm_ffn", "ffn_w_gate", "ffn_w_up", "ffn_w_down",
                                       "ple_w_proj", "ple_w_gate")}
    x1, sv_m = _mamba_fwd(x, prm)
    x3, sv_f0 = _ffn_ple_fwd(x1, p[0], prm, 0, plan)
    x4, sv_a = _attn_mixer_fwd(x3, prm, plan)
    x6, sv_f1 = _ffn_ple_fwd(x4, p[1], prm, 1, plan)
    dy, loss_row = _loss_head(x6, target)
    dx4 = _ffn_ple_bwd(dy, p[1], prm, 1, sv_f1, grads)
    dx3 = _attn_mixer_bwd(dx4, prm, sv_a, grads)
    plan.layer1_backward_done(grads)
    dx1 = _ffn_ple_bwd(dx3, p[0], prm, 0, sv_f0, grads)
    dx0 = _mamba_bwd(dx1, prm, sv_m, grads, plan)
    return loss_row, dx0, grads


W_IN_SLAB_ROWS = 1312


def _position():
    return lax.axis_index("x"), lax.axis_index("y"), lax.axis_index("c")


def _other_chips(x, y):
    return [(1 - x, y), (x, 1 - y), (1 - x, 1 - y)]


def _remote(send_sems, recv_sems, k, src, dst, to):
    return pltpu.make_async_remote_copy(src_ref=src, dst_ref=dst, send_sem=send_sems.at[k], recv_sem=recv_sems.at[k],
                                        device_id=to, device_id_type=MESH)


def _gather_side(entries, whole=()):
    n, nw = len(entries), len(whole)

    def first_hop(ins, outs, send_sems, recv_sems):
        x, y, c = _position()
        cps = []
        for j, chip in enumerate(_other_chips(x, y)):
            for e in range(n):
                cps.append(_remote(send_sems, recv_sems, 6 * e + j, ins[e].at[c], outs[e].at[2 * x + y, c], (*chip, c)))
            for e in range(nw):
                cps.append(_remote(send_sems, recv_sems, 6 * n + 3 * e + j, ins[n + e], outs[n + e].at[2 * x + y],
                                   (*chip, c)))
        return cps

    def start(ins, outs, send_sems, recv_sems):
        for cp in first_hop(ins, outs, send_sems, recv_sems):
            cp.start()

    def finish(ins, outs, send_sems, recv_sems):
        x, y, c = _position()
        me, sibling = (x, y, c), (x, y, 1 - c)
        chips = _other_chips(x, y)
        passed_on = []
        for j, (px, py) in enumerate(chips):
            for e in range(n):
                landed = outs[e].at[2 * px + py, c]
                _remote(send_sems, recv_sems, 6 * e + j, landed, landed, me).wait_recv()
                passed_on.append(_remote(send_sems, recv_sems, 6 * e + 3 + j, landed, landed, sibling))
                passed_on[-1].start()
            for e in range(nw):
                landed = outs[n + e].at[2 * px + py]
                _remote(send_sems, recv_sems, 6 * n + 3 * e + j, landed, landed, me).wait_recv()
        for j, (px, py) in enumerate(chips):
            for e in range(n):
                passed = outs[e].at[2 * px + py, 1 - c]
                _remote(send_sems, recv_sems, 6 * e + 3 + j, passed, passed, me).wait_recv()
        for cp in first_hop(ins, outs, send_sems, recv_sems) + passed_on:
            cp.wait_send()

    shapes = [jax.ShapeDtypeStruct((N_CHIPS,) + a.shape, a.dtype) for a in list(entries) + list(whole)]
    return _Side(list(entries) + list(whole), shapes, 6 * n + 3 * nw, start, finish)


def _run_side(side, name):
    si, so = len(side.inputs), len(side.out_shapes)

    def body(*refs):
        ins, outs, send_sems, recv_sems = refs[:si], refs[si:si + so], refs[-2], refs[-1]
        side.start(ins, outs, send_sems, recv_sems)
        side.finish(ins, outs, send_sems, recv_sems)

    side.outputs = list(pl.pallas_call(
        body, name=name, in_specs=[ANY] * si, out_specs=[ANY] * so, out_shape=side.out_shapes,
        scratch_shapes=[pltpu.SemaphoreType.DMA((side.n_sems,)), pltpu.SemaphoreType.DMA((side.n_sems,))],
    )(*side.inputs))
    return side.outputs


def _swap_halves(grads, *, name):
    n = len(grads)

    def body(*refs):
        g_refs, r_refs = refs[:n], refs[n:2 * n]
        send_sems, recv_sems = refs[2 * n], refs[2 * n + 1]
        x, y, c = _position()
        cps = [pltpu.make_async_remote_copy(src_ref=g_refs[e].at[:, 1 - c], dst_ref=r_refs[e],
                                            send_sem=send_sems.at[e], recv_sem=recv_sems.at[e],
                                            device_id=(x, y, 1 - c), device_id_type=MESH) for e in range(n)]
        for cp in cps:
            cp.start()
        for cp in cps:
            cp.wait()

    return pl.pallas_call(
        body, name=name, in_specs=[ANY] * n, out_specs=[ANY] * n,
        out_shape=[jax.ShapeDtypeStruct((N_CHIPS,) + g.shape[2:], g.dtype) for g in grads],
        scratch_shapes=[pltpu.SemaphoreType.DMA((n,)), pltpu.SemaphoreType.DMA((n,))],
    )(*grads)


def _chip_exchange_side(chipsums):
    n = len(chipsums)

    def copies(ins, outs, send_sems, recv_sems):
        x, y, c = _position()
        return [_remote(send_sems, recv_sems, 3 * e + j, ins[e].at[2 * tx + ty], outs[e].at[j], (tx, ty, c))
                for j, (tx, ty) in enumerate(_other_chips(x, y)) for e in range(n)]

    def start(ins, outs, send_sems, recv_sems):
        for cp in copies(ins, outs, send_sems, recv_sems):
            cp.start()

    def finish(ins, outs, send_sems, recv_sems):
        for cp in copies(ins, outs, send_sems, recv_sems):
            cp.wait()

    shapes = [jax.ShapeDtypeStruct((3,) + cs.shape[1:], cs.dtype) for cs in chipsums]
    return _Side(chipsums, shapes, 3 * n, start, finish)


def _share_halves(totals):
    n = len(totals)

    def body(*refs):
        t_refs, r_refs = refs[:n], refs[n:2 * n]
        send_sems, recv_sems = refs[2 * n], refs[2 * n + 1]
        x, y, c = _position()
        cps = [pltpu.make_async_remote_copy(src_ref=t_refs[e], dst_ref=r_refs[e], send_sem=send_sems.at[e],
                                            recv_sem=recv_sems.at[e], device_id=(x, y, 1 - c), device_id_type=MESH)
               for e in range(n)]
        for cp in cps:
            cp.start()
        for cp in cps:
            cp.wait()

    return pl.pallas_call(
        body, name="grad_share_halves", in_specs=[ANY] * n, out_specs=[ANY] * n,
        out_shape=[jax.ShapeDtypeStruct(t.shape, t.dtype) for t in totals],
        scratch_shapes=[pltpu.SemaphoreType.DMA((n,)), pltpu.SemaphoreType.DMA((n,))],
    )(*totals)


def _reduce_rows(h):
    return h if h <= 704 else h // 2


def _add_sibling(grad, recv, c_idx, *, name):
    _, _, h, cw = grad.shape
    th = _reduce_rows(h)

    def body(c_ref, g_ref, r_ref, o_ref):
        o_ref[...] = (g_ref[...] + r_ref[...]).astype(BF16)

    return pl.pallas_call(
        body, name=name,
        grid_spec=pltpu.PrefetchScalarGridSpec(
            num_scalar_prefetch=1, grid=(N_CHIPS, h // th),
            in_specs=[pl.BlockSpec((None, None, th, cw), lambda s, i, c_ref: (s, c_ref[0], i, 0)),
                      pl.BlockSpec((None, th, cw), lambda s, i, c_ref: (s, i, 0))],
            out_specs=pl.BlockSpec((None, th, cw), lambda s, i, c_ref: (s, i, 0))),
        out_shape=jax.ShapeDtypeStruct((N_CHIPS, h, cw), BF16),
        compiler_params=_params("parallel", "parallel"),
    )(c_idx, grad, recv)


def _add_chips(chipsum, recv, s_idx, *, name):
    _, h, cw = chipsum.shape
    th = _reduce_rows(h)

    def body(s_ref, own_ref, r_ref, o_ref):
        o_ref[...] = ((own_ref[...].astype(F32) + r_ref[0].astype(F32)) + r_ref[1].astype(F32)) + r_ref[2].astype(F32)

    return pl.pallas_call(
        body, name=name,
        grid_spec=pltpu.PrefetchScalarGridSpec(
            num_scalar_prefetch=1, grid=(h // th,),
            in_specs=[pl.BlockSpec((None, th, cw), lambda i, s_ref: (s_ref[0], i, 0)),
                      pl.BlockSpec((3, th, cw), lambda i, s_ref: (0, i, 0))],
            out_specs=pl.BlockSpec((th, cw), lambda i, s_ref: (i, 0))),
        out_shape=jax.ShapeDtypeStruct((h, cw), F32),
        compiler_params=_params("parallel"),
    )(s_idx, chipsum, recv)


def _adamw_math(w, g, m, v):
    m = ADAM_B1 * m + (1.0 - ADAM_B1) * g
    v = ADAM_B2 * v + (1.0 - ADAM_B2) * (g * g)
    m_hat = m / (1.0 - ADAM_B1 ** ADAM_STEP)
    v_hat = v / (1.0 - ADAM_B2 ** ADAM_STEP)
    delta = -ADAM_LR * (m_hat / (jnp.sqrt(v_hat) + ADAM_EPS) + ADAM_WD * w)
    return delta, m, v


ADAM_TILE_ELEMS = 256 * 1024


def _adamw(w, g, m, v, *, name):
    shape = w.shape
    cols = shape[-1]
    rows = w.size // cols
    tr = rows
    for cand in range(8, rows, 8):
        if rows % cand == 0 and cand * cols <= ADAM_TILE_ELEMS:
            tr = cand
    if rows * cols <= ADAM_TILE_ELEMS:
        tr = rows

    def body(w_ref, g_ref, m_ref, v_ref, d_ref, nm_ref, nv_ref):
        d, nm, nv = _adamw_math(w_ref[...], g_ref[...], m_ref[...], v_ref[...])
        d_ref[...] = d
        nm_ref[...] = nm
        nv_ref[...] = nv

    blk = pl.BlockSpec((tr, cols), lambda i: (i, 0))
    sds = jax.ShapeDtypeStruct((rows, cols), F32)
    outs = pl.pallas_call(
        body, name=name, grid=(rows // tr,), in_specs=[blk] * 4, out_specs=[blk] * 3, out_shape=[sds] * 3,
        compiler_params=_params("parallel"),
    )(*[a.reshape(rows, cols) for a in (w, g, m, v)])
    return [o.reshape(shape) for o in outs]


SMALL_LAYOUT = (("loss", 1), ("norm_mix", 16), ("norm_ffn", 16), ("ssm_conv_b", 24), ("ssm_dt_bias", 1),
                ("ssm_a_log", 1), ("ssm_d_skip", 1), ("ssm_norm_w", 16), ("att_q_norm", 1), ("att_k_norm", 1),
                ("conv_w_full", 96))
SMALL_ROWS = 176
N_DEVICES = 8


def _small_pack(values):
    parts = []
    for name, rows in SMALL_LAYOUT:
        flat = values[name].reshape(-1).astype(F32)
        parts.append(jnp.pad(flat, (0, rows * LANES - flat.shape[0])).reshape(rows, LANES))
    used = sum(r for _, r in SMALL_LAYOUT)
    parts.append(jnp.zeros((SMALL_ROWS - used, LANES), F32))
    return jnp.concatenate(parts, axis=0)


def _small_unpack(pack, shapes):
    out, off = {}, 0
    for name, rows in SMALL_LAYOUT:
        shape = shapes[name]
        n = math.prod(shape)
        out[name] = pack[off:off + rows].reshape(-1)[:n].reshape(shape)
        off += rows
    return out


def _small_allreduce_adamw(g, w, m, v):
    def body(g_ref, w_ref, m_ref, v_ref, gs_ref, d_ref, nm_ref, nv_ref, buf, send_sems, recv_sems):
        x, y, c = _position()
        pos = (x, y, c)
        me = 4 * x + 2 * y + c
        buf[me] = g_ref[...]
        peers = []
        for k in range(1, N_DEVICES):
            bits = ((k >> 2) & 1, (k >> 1) & 1, k & 1)
            peers.append(tuple(1 - p if b else p for p, b in zip(pos, bits)))
        cps = [pltpu.make_async_remote_copy(src_ref=g_ref, dst_ref=buf.at[me], send_sem=send_sems.at[k],
                                            recv_sem=recv_sems.at[k], device_id=peer, device_id_type=MESH)
               for k, peer in enumerate(peers)]
        for cp in cps:
            cp.start()
        for k, (px, py, pc) in enumerate(peers):
            pltpu.make_async_remote_copy(src_ref=g_ref, dst_ref=buf.at[4 * px + 2 * py + pc],
                                         send_sem=send_sems.at[k], recv_sem=recv_sems.at[k],
                                         device_id=(px, py, pc), device_id_type=MESH).wait_recv()
        for cp in cps:
            cp.wait_send()
        total = buf[0]
        for dev in range(1, N_DEVICES):
            total = total + buf[dev]
        gs_ref[...] = total
        d, nm, nv = _adamw_math(w_ref[...], total, m_ref[...], v_ref[...])
        d_ref[...] = d
        nm_ref[...] = nm
        nv_ref[...] = nv

    vm = pl.BlockSpec(memory_space=pltpu.VMEM)
    sds = jax.ShapeDtypeStruct((SMALL_ROWS, LANES), F32)
    return pl.pallas_call(
        body, name="small_allreduce_adamw", in_specs=[vm] * 4, out_specs=[vm] * 4, out_shape=[sds] * 4,
        scratch_shapes=[pltpu.VMEM((N_DEVICES, SMALL_ROWS, LANES), F32),
                        pltpu.SemaphoreType.DMA((N_DEVICES - 1,)), pltpu.SemaphoreType.DMA((N_DEVICES - 1,))],
    )(g, w, m, v)


SMALL = tuple(n for n, _ in SMALL_LAYOUT if n not in ("loss", "conv_w_full"))
WEIGHTS = ("norm_mix", "norm_ffn", "ssm_w_in", "ssm_conv_w", "ssm_conv_b", "ssm_dt_bias", "ssm_a_log", "ssm_d_skip",
           "ssm_norm_w", "ssm_w_out", "att_w_qkv", "att_q_norm", "att_k_norm", "att_w_o", "ffn_w_gate", "ffn_w_up",
           "ffn_w_down", "ple_w_proj", "ple_w_gate")
COLUMN_SHARDED = ("ssm_w_in", "att_w_qkv", "ffn_w_gate", "ffn_w_up", "ple_w_proj")
LAYERED = ("ffn_w_gate", "ffn_w_up", "ffn_w_down", "ple_w_proj", "ple_w_gate")
GATHER_ORDER = ("ssm_w_in", "ssm_w_out", "att_w_qkv", "att_w_o", "ffn_w_gate", "ffn_w_up", "ffn_w_down",
                "ple_w_proj", "ple_w_gate")


def _layers(n):
    return (0, 1) if n in LAYERED else (None,)


def _tag(key):
    return key[0] if key[1] is None else f"{key[0]}_{key[1]}"


def _weight_slab(w, key):
    n, i = key
    a = w[n][0 if i is None else i]
    a = a.T if n in COLUMN_SHARDED else a
    if n == "ssm_w_in":
        a = jnp.pad(a, ((0, W_IN_SLAB_ROWS - a.shape[0]), (0, 0)))
    return a.reshape(2, a.shape[0] // 2, a.shape[1]).astype(BF16)


def _install(prm, key, gathered, own, s_me):
    n, i = key
    full = lax.dynamic_update_slice(gathered, own[None], (s_me, 0, 0, 0))
    full = full.reshape(N_CHIPS, 2 * full.shape[2], full.shape[3])
    if n == "ssm_w_in":
        rows = (D_INNER + CONV_DIM + SSM_HEADS) // N_CHIPS
        w_in_t = full[:, :rows].reshape(N_CHIPS * rows, D_MODEL)
        prm["ssm_w_z"] = w_in_t[:D_INNER]
        prm["ssm_w_xbc"] = w_in_t[D_INNER:D_INNER + CONV_DIM]
        prm["ssm_w_dt"] = jnp.pad(w_in_t[D_INNER + CONV_DIM:], ((0, LANES - SSM_HEADS), (0, 0)))
        return
    full = full.reshape(N_CHIPS * full.shape[1], full.shape[2])
    if i is None:
        prm[n] = full
    else:
        prm.setdefault(n, [None, None])[i] = full


def _grad_slab(grads, key):
    n, i = key
    g = grads[n] if i is None else grads[n][i]
    if n == "ssm_w_in":
        g = jnp.pad(g.reshape(N_CHIPS, g.shape[0] // N_CHIPS, D_MODEL),
                    ((0, 0), (0, W_IN_SLAB_ROWS - g.shape[0] // N_CHIPS), (0, 0)))
    rows = g.size // (N_CHIPS * g.shape[-1])
    return g.reshape(N_CHIPS, 2, rows // 2, g.shape[-1])


def _natural_shard(n, reduced, shape):
    def one(r):
        if n == "ssm_w_in":
            r = r[:shape[-1]]
        return r.T if n in COLUMN_SHARDED else r
    if n in LAYERED:
        return jnp.stack([one(r) for r in reduced]).reshape(shape)
    return one(reduced[0]).reshape(shape)


def kernel(x, p, norm_mix, norm_ffn, ssm_w_in, ssm_conv_w, ssm_conv_b, ssm_dt_bias, ssm_a_log, ssm_d_skip, ssm_norm_w, ssm_w_out, att_w_qkv, att_q_norm, att_k_norm, att_w_o, ffn_w_gate, ffn_w_up, ffn_w_down, ple_w_proj, ple_w_gate, loss_target, m_norm_mix, m_norm_ffn, m_ssm_w_in, m_ssm_conv_w, m_ssm_conv_b, m_ssm_dt_bias, m_ssm_a_log, m_ssm_d_skip, m_ssm_norm_w, m_ssm_w_out, m_att_w_qkv, m_att_q_norm, m_att_k_norm, m_att_w_o, m_ffn_w_gate, m_ffn_w_up, m_ffn_w_down, m_ple_w_proj, m_ple_w_gate, v_norm_mix, v_norm_ffn, v_ssm_w_in, v_ssm_conv_w, v_ssm_conv_b, v_ssm_dt_bias, v_ssm_a_log, v_ssm_d_skip, v_ssm_norm_w, v_ssm_w_out, v_att_w_qkv, v_att_q_norm, v_att_k_norm, v_att_w_o, v_ffn_w_gate, v_ffn_w_up, v_ffn_w_down, v_ple_w_proj, v_ple_w_gate):
    given = dict(locals())
    w = {n: given[n] for n in WEIGHTS}
    m = {n: given["m_" + n] for n in WEIGHTS}
    v = {n: given["v_" + n] for n in WEIGHTS}
    c_idx = lax.axis_index("c").astype(jnp.int32).reshape(1)
    s_idx = (2 * lax.axis_index("x") + lax.axis_index("y")).astype(jnp.int32).reshape(1)

    s_me = 2 * lax.axis_index("x") + lax.axis_index("y")
    first_core = lax.axis_index("c") == 0

    def keys(*names):
        return [(n, i) for n in names for i in _layers(n)]

    layer0 = [k for k in keys("ssm_w_in", "ssm_w_out", "ffn_w_gate", "ffn_w_up", "ffn_w_down", "ple_w_proj",
                              "ple_w_gate") if k[1] != 1]
    attention = keys("att_w_qkv", "att_w_o")
    layer1 = [k for k in keys("ffn_w_gate", "ffn_w_up", "ffn_w_down", "ple_w_proj", "ple_w_gate") if k[1] == 1]
    own = {k: _weight_slab(w, k) for k in layer0 + attention + layer1}
    prm = {n: w[n] for n in SMALL}

    def land(group, outputs):
        for k, g in zip(group, outputs):
            _install(prm, k, g, own[k], s_me)

    first = _gather_side([own[k] for k in layer0], whole=[ssm_conv_w[0]])
    _run_side(first, "gather_layer0")
    land(layer0, first.outputs)
    conv = lax.dynamic_update_slice(first.outputs[-1], ssm_conv_w, (s_me, 0, 0))
    prm["ssm_conv_w"] = conv.transpose(1, 0, 2).reshape(CONV_WIDTH, CONV_DIM)

    def reduce_start(grads, group):
        g4 = [_grad_slab(grads, k) for k in group]
        from_sibling = _swap_halves(g4, name="grad_swap_" + _tag(group[0]))
        chipsums = [_add_sibling(g, r, c_idx, name="add_sibling_" + _tag(k))
                    for g, r, k in zip(g4, from_sibling, group)]
        return chipsums, _chip_exchange_side(chipsums)

    def reduce_finish(group, chipsums, from_chips):
        return {k: _add_chips(cs, r, s_idx, name="add_chips_" + _tag(k))
                for k, cs, r in zip(group, chipsums, from_chips)}

    second = attention + layer1

    class Plan(_NoOverlap):
        carried = {"swiglu_fwd_0": _gather_side([own[k] for k in attention]),
                   "att_qkv": _gather_side([own[k] for k in layer1])}
        groups = {"swiglu_fwd_0": attention, "att_qkv": layer1}

        def side(self, host):
            return self.carried.get(host)

        def after(self, host, prm_):
            if host in self.groups:
                land(self.groups[host], self.carried[host].outputs)

        def layer1_backward_done(self, grads):
            self.chipsums, self.carried["ssd_bwd"] = reduce_start(grads, second)

    plan = Plan()
    loss_row, dx, grads = _local_step(x[0], p[:, 0], loss_target[0], prm, plan)

    totals = reduce_finish(second, plan.chipsums, plan.carried["ssd_bwd"].outputs)
    chipsums, exchange = reduce_start(grads, layer0)
    totals.update(reduce_finish(layer0, chipsums, _run_side(exchange, "grad_chip_exchange")))
    order = layer0 + second
    shared = _share_halves([totals[k] for k in order])
    reduced = {}
    for k, theirs in zip(order, shared):
        lo = jnp.where(first_core, totals[k], theirs)
        hi = jnp.where(first_core, theirs, totals[k])
        reduced.setdefault(k[0], {})[k[1]] = jnp.concatenate([lo, hi], axis=0)
    reduced = {n: [by_layer[i] for i in _layers(n)] for n, by_layer in reduced.items()}

    grad, delta, new_m, new_v = {}, {}, {}, {}
    for n in GATHER_ORDER:
        grad[n] = _natural_shard(n, reduced[n], w[n].shape)
        delta[n], new_m[n], new_v[n] = _adamw(w[n], grad[n], m[n], v[n], name="adamw_" + n)

    small_g = {n: (jnp.stack(grads[n]) if isinstance(grads[n], list) else grads[n]) for n in SMALL}
    small_g["loss"] = loss_row
    small_g["conv_w_full"] = grads["ssm_conv_w"]
    zero = {"loss": jnp.zeros((1, LANES), F32), "conv_w_full": jnp.zeros((CONV_WIDTH, CONV_DIM), F32)}
    outs = _small_allreduce_adamw(_small_pack(small_g), _small_pack({**w, **zero}), _small_pack({**m, **zero}),
                                  _small_pack({**v, **zero}))
    shapes = {n: w[n].shape for n in SMALL}
    shapes["loss"] = (1, LANES)
    shapes["conv_w_full"] = (CONV_WIDTH, CONV_DIM)
    sg, sd, sm, sv = [_small_unpack(o, shapes) for o in outs]
    for n in SMALL:
        grad[n], delta[n], new_m[n], new_v[n] = sg[n], sd[n], sm[n], sv[n]
    loss = sg["loss"][0, 0]
    conv_cols = CONV_DIM // N_CHIPS
    grad["ssm_conv_w"] = lax.dynamic_slice(sg["conv_w_full"], (0, s_me * conv_cols), (CONV_WIDTH, conv_cols))[None]
    delta["ssm_conv_w"], new_m["ssm_conv_w"], new_v["ssm_conv_w"] = _adamw(
        ssm_conv_w, grad["ssm_conv_w"], m_ssm_conv_w, v_ssm_conv_w, name="adamw_ssm_conv_w")

    return (loss, dx[None], *[grad[n] for n in WEIGHTS], *[delta[n] for n in WEIGHTS],
            *[new_m[n] for n in WEIGHTS], *[new_v[n] for n in WEIGHTS])
```

```python
import functools
import math

import jax
import jax.numpy as jnp
from jax import lax
from jax.experimental import pallas as pl
from jax.experimental.pallas import tpu as pltpu

F32 = jnp.float32
BF16 = jnp.bfloat16
HIGHEST = lax.Precision.HIGHEST

NORM_EPS = 1e-6
ADAM_LR, ADAM_B1, ADAM_B2, ADAM_EPS, ADAM_WD, ADAM_STEP = 0.001, 0.9, 0.999, 1e-08, 0.01, 10

D_MODEL = 1024
D_INNER = 2048
SSM_HEADS = 32
SSM_HEAD_DIM = 64
SSM_GROUPS = 4
SSM_STATE = 128
SSD_CHUNK = 128
CONV_DIM = 3072
CONV_WIDTH = 4
ATT_HEADS = 16
ATT_HEAD_DIM = 64
DIL_PATTERNS = ((128, 1), (512, 4), (2048, 16))
ATT_BLOCK = 128
FFN_HIDDEN = 2816
PLE_DIM = 256

LANES = 128
V7X_VMEM_LIMIT = 56 * 1024 * 1024
NEG_BIG = -1e30

N_CHIPS = 4


def _params(*sem):
    return pltpu.CompilerParams(dimension_semantics=sem, vmem_limit_bytes=V7X_VMEM_LIMIT)


def _tile(n, pref):
    if n <= pref:
        return n
    best = None
    for t in range(LANES, pref + 1, LANES):
        if n % t == 0:
            best = t
    assert best is not None, (n, pref)
    return best


def _sigmoid(v):
    return 1.0 / (1.0 + jnp.exp(-v))


def _dot(a, b):
    return jnp.dot(a, b, preferred_element_type=F32)


def _dot_nt(a, b):
    return lax.dot_general(a, b, (((1,), (1,)), ((), ())), preferred_element_type=F32)


def _dot_tn(a, b):
    return lax.dot_general(a, b, (((0,), (0,)), ((), ())), preferred_element_type=F32)


def _head_block_diag():
    i = lax.broadcasted_iota(jnp.int32, (LANES, LANES), 0) // ATT_HEAD_DIM
    j = lax.broadcasted_iota(jnp.int32, (LANES, LANES), 1) // ATT_HEAD_DIM
    return (i == j).astype(BF16)


def _split_dot(ones, z):
    hi = z.astype(BF16)
    lo = (z - hi.astype(F32)).astype(BF16)
    return _dot(ones, hi) + _dot(ones, lo)


def _head_sums(z, bd):
    hi = z.astype(BF16)
    lo = (z - hi.astype(F32)).astype(BF16)
    parts = []
    for t in range(z.shape[1] // LANES):
        sl = slice(t * LANES, (t + 1) * LANES)
        parts.append(_dot(hi[:, sl], bd) + _dot(lo[:, sl], bd))
    return parts[0] if len(parts) == 1 else jnp.concatenate(parts, axis=1)


def _lane_lt64(rows):
    return lax.broadcasted_iota(jnp.int32, (rows, LANES), 1) < ATT_HEAD_DIM


MESH = pl.DeviceIdType.MESH
ANY = pl.BlockSpec(memory_space=pl.ANY)


class _Side:
    def __init__(self, inputs, out_shapes, n_sems, start, finish):
        self.inputs, self.out_shapes, self.n_sems = list(inputs), list(out_shapes), n_sems
        self.start, self.finish = start, finish
        self.outputs = None


def _call(body, side, *, name, grid, in_specs, out_specs, out_shape, scratch_shapes, semantics, args):
    in_specs, out_specs, out_shape = list(in_specs), list(out_specs), list(out_shape)
    scratch_shapes = list(scratch_shapes)
    if side is None:
        return pl.pallas_call(body, name=name, grid=grid, in_specs=in_specs, out_specs=out_specs,
                              out_shape=out_shape, scratch_shapes=scratch_shapes,
                              compiler_params=_params(*semantics))(*args)
    ni, no, ns = len(in_specs), len(out_specs), len(scratch_shapes)
    si, so = len(side.inputs), len(side.out_shapes)

    def hosted(*refs):
        ins, s_ins = refs[:ni], refs[ni:ni + si]
        outs, s_outs = refs[ni + si:ni + si + no], refs[ni + si + no:ni + si + no + so]
        scratch = refs[ni + si + no + so:ni + si + no + so + ns]
        send_sems, recv_sems = refs[-2], refs[-1]
        first = pl.program_id(0) == 0
        last = pl.program_id(0) == grid[0] - 1
        for axis in range(1, len(grid)):
            first = jnp.logical_and(first, pl.program_id(axis) == 0)
            last = jnp.logical_and(last, pl.program_id(axis) == grid[axis] - 1)

        @pl.when(first)
        def _():
            side.start(s_ins, s_outs, send_sems, recv_sems)

        body(*ins, *outs, *scratch)

        @pl.when(last)
        def _():
            side.finish(s_ins, s_outs, send_sems, recv_sems)

    res = pl.pallas_call(
        hosted, name=name, grid=grid, in_specs=in_specs + [ANY] * si, out_specs=out_specs + [ANY] * so,
        out_shape=out_shape + side.out_shapes,
        scratch_shapes=scratch_shapes + [pltpu.SemaphoreType.DMA((side.n_sems,)),
                                         pltpu.SemaphoreType.DMA((side.n_sems,))],
        compiler_params=_params(*["arbitrary"] * len(grid)),
    )(*args, *side.inputs)
    side.outputs = list(res[no:])
    return list(res[:no])


def _matmul(a, b, *, mode, name, out_dtype=F32, addend=None, tm=1024, tn=512, tk_max=3072, side=None):
    m, k = a.shape
    if mode == "nn":
        k2, n = b.shape
    else:
        n, k2 = b.shape
    assert k == k2, (a.shape, b.shape, mode)
    tm, tn, tk = _tile(m, tm), _tile(n, tn), _tile(k, tk_max)
    nk = k // tk
    has_add = addend is not None

    def body(*refs):
        a_ref, b_ref = refs[0], refs[1]
        add_ref = refs[2] if has_add else None
        o_ref, acc_ref = refs[-2], refs[-1]
        kk = pl.program_id(2)
        av = a_ref[...].astype(BF16)
        bv = b_ref[...].astype(BF16)
        part = _dot(av, bv) if mode == "nn" else _dot_nt(av, bv)

        @pl.when(kk == 0)
        def _():
            acc_ref[...] = part

        @pl.when(kk > 0)
        def _():
            acc_ref[...] += part

        @pl.when(kk == nk - 1)
        def _():
            res = acc_ref[...]
            if has_add:
                res = res + add_ref[...]
            o_ref[...] = res.astype(out_dtype)

    a_spec = pl.BlockSpec((tm, tk), lambda i, j, kk: (i, kk))
    if mode == "nn":
        b_spec = pl.BlockSpec((tk, tn), lambda i, j, kk: (kk, j))
    else:
        b_spec = pl.BlockSpec((tn, tk), lambda i, j, kk: (j, kk))
    in_specs = [a_spec, b_spec]
    args = [a, b]
    if has_add:
        in_specs.append(pl.BlockSpec((tm, tn), lambda i, j, kk: (i, j)))
        args.append(addend)
    return _call(
        body, side, name=name, grid=(m // tm, n // tn, nk),
        in_specs=in_specs, out_specs=[pl.BlockSpec((tm, tn), lambda i, j, kk: (i, j))],
        out_shape=[jax.ShapeDtypeStruct((m, n), out_dtype)],
        scratch_shapes=[pltpu.VMEM((tm, tn), F32)],
        semantics=("parallel", "parallel", "arbitrary"), args=args,
    )[0]


def _matmul_tn(a, b, *, name, tm=1408, tn=512, tk=1024):
    t, m = a.shape
    t2, n = b.shape
    assert t == t2
    tm, tn, tk = _tile(m, tm), _tile(n, tn), _tile(t, tk)

    def body(a_ref, b_ref, o_ref):
        part = _dot_tn(a_ref[...].astype(BF16), b_ref[...].astype(BF16))

        @pl.when(pl.program_id(2) == 0)
        def _():
            o_ref[...] = part

        @pl.when(pl.program_id(2) > 0)
        def _():
            o_ref[...] += part

    return pl.pallas_call(
        body, name=name, grid=(m // tm, n // tn, t // tk),
        in_specs=[pl.BlockSpec((tk, tm), lambda i, j, kk: (kk, i)),
                  pl.BlockSpec((tk, tn), lambda i, j, kk: (kk, j))],
        out_specs=pl.BlockSpec((tm, tn), lambda i, j, kk: (i, j)),
        out_shape=jax.ShapeDtypeStruct((m, n), F32),
        compiler_params=_params("parallel", "parallel", "arbitrary"),
    )(a, b)


def _rmsnorm_fwd(x, gain, *, name):
    t, d = x.shape
    tm = _tile(t, 512)

    def body(x_ref, g_ref, o_ref):
        xv = x_ref[...]
        r = lax.rsqrt(jnp.mean(xv * xv, axis=-1, keepdims=True) + NORM_EPS)
        o_ref[...] = (xv * r * g_ref[...]).astype(BF16)

    return pl.pallas_call(
        body, name=name, grid=(t // tm,),
        in_specs=[pl.BlockSpec((tm, d), lambda i: (i, 0)), pl.BlockSpec((1, d), lambda i: (0, 0))],
        out_specs=pl.BlockSpec((tm, d), lambda i: (i, 0)),
        out_shape=jax.ShapeDtypeStruct((t, d), BF16),
        compiler_params=_params("parallel"),
    )(x, gain)


def _rmsnorm_bwd(x, gain, dy, dres, *, name):
    t, d = x.shape
    tm = _tile(t, 512)

    def body(x_ref, g_ref, dy_ref, dres_ref, dx_ref, dg_ref):
        xv = x_ref[...]
        r = lax.rsqrt(jnp.mean(xv * xv, axis=-1, keepdims=True) + NORM_EPS)
        xh = xv * r
        dyv = dy_ref[...]
        dxh = dyv * g_ref[...]
        mean = jnp.mean(dxh * xh, axis=-1, keepdims=True)
        dx_ref[...] = dres_ref[...] + r * (dxh - xh * mean)
        part = jnp.sum(dyv * xh, axis=0, keepdims=True)

        @pl.when(pl.program_id(0) == 0)
        def _():
            dg_ref[...] = part

        @pl.when(pl.program_id(0) > 0)
        def _():
            dg_ref[...] += part

    row = pl.BlockSpec((tm, d), lambda i: (i, 0))
    vec = pl.BlockSpec((1, d), lambda i: (0, 0))
    return pl.pallas_call(
        body, name=name, grid=(t // tm,),
        in_specs=[row, vec, row, row], out_specs=[row, vec],
        out_shape=[jax.ShapeDtypeStruct((t, d), F32), jax.ShapeDtypeStruct((1, d), F32)],
        compiler_params=_params("arbitrary"),
    )(x, gain, dy, dres)


def _loss_head(y, target):
    t, d = y.shape
    tm = _tile(t, 512)
    steps = t // tm

    def body(y_ref, t_ref, dy_ref, l_ref, acc_ref):
        e = y_ref[...] - t_ref[...]
        dy_ref[...] = e * (1.0 / d)
        part = jnp.sum(e * e, axis=0, keepdims=True)

        @pl.when(pl.program_id(0) == 0)
        def _():
            acc_ref[...] = part

        @pl.when(pl.program_id(0) > 0)
        def _():
            acc_ref[...] += part

        @pl.when(pl.program_id(0) == steps - 1)
        def _():
            l_ref[...] = jnp.full((1, LANES), (0.5 / d), F32) * jnp.sum(acc_ref[...])

    row = pl.BlockSpec((tm, d), lambda i: (i, 0))
    return pl.pallas_call(
        body, name="loss_head", grid=(steps,),
        in_specs=[row, row], out_specs=[row, pl.BlockSpec((1, LANES), lambda i: (0, 0))],
        out_shape=[jax.ShapeDtypeStruct((t, d), F32), jax.ShapeDtypeStruct((1, LANES), F32)],
        scratch_shapes=[pltpu.VMEM((1, d), F32)],
        compiler_params=_params("arbitrary"),
    )(y, target)


def _swiglu_fwd(h, w_gate_t, w_up_t, *, name, side=None):
    t, d = h.shape
    f = w_gate_t.shape[0]
    tm, tn = _tile(t, 1024), _tile(f, 256)

    def body(h_ref, wg_ref, wu_ref, g_ref, u_ref, a_ref):
        hv = h_ref[...]
        g = _dot_nt(hv, wg_ref[...])
        u = _dot_nt(hv, wu_ref[...])
        g_ref[...] = g.astype(BF16)
        u_ref[...] = u.astype(BF16)
        a_ref[...] = (g * _sigmoid(g) * u).astype(BF16)

    wspec = pl.BlockSpec((tn, d), lambda i, j: (j, 0))
    ospec = pl.BlockSpec((tm, tn), lambda i, j: (i, j))
    return _call(
        body, side, name=name, grid=(t // tm, f // tn),
        in_specs=[pl.BlockSpec((tm, d), lambda i, j: (i, 0)), wspec, wspec],
        out_specs=[ospec, ospec, ospec],
        out_shape=[jax.ShapeDtypeStruct((t, f), BF16), jax.ShapeDtypeStruct((t, f), BF16),
                   jax.ShapeDtypeStruct((t, f), BF16)],
        scratch_shapes=[], semantics=("parallel", "parallel"), args=(h, w_gate_t, w_up_t),
    )


def _swiglu_bwd(dx, w_down, g, u, *, name, side=None):
    t, d = dx.shape
    f = w_down.shape[0]
    tm, tn = _tile(t, 1024), _tile(f, 256)

    def body(dx_ref, wd_ref, g_ref, u_ref, dg_ref, du_ref):
        dact = _dot_nt(dx_ref[...].astype(BF16), wd_ref[...])
        gv, uv = g_ref[...].astype(F32), u_ref[...].astype(F32)
        sg = _sigmoid(gv)
        dg_ref[...] = (dact * uv * sg * (1.0 + gv * (1.0 - sg))).astype(BF16)
        du_ref[...] = (dact * gv * sg).astype(BF16)

    ospec = pl.BlockSpec((tm, tn), lambda i, j: (i, j))
    return _call(
        body, side, name=name, grid=(t // tm, f // tn),
        in_specs=[pl.BlockSpec((tm, d), lambda i, j: (i, 0)), pl.BlockSpec((tn, d), lambda i, j: (j, 0)),
                  ospec, ospec],
        out_specs=[ospec, ospec],
        out_shape=[jax.ShapeDtypeStruct((t, f), BF16), jax.ShapeDtypeStruct((t, f), BF16)],
        scratch_shapes=[], semantics=("parallel", "parallel"), args=(dx, w_down, g, u),
    )


def _ple_fwd(x, p, w_gate, w_proj_t, *, name):
    t, d = x.shape
    e = p.shape[1]
    tm, tn = _tile(t, 1024), _tile(d, 512)

    def body(xf_ref, xr_ref, p_ref, wg_ref, wp_ref, o_ref):
        s = _dot(xf_ref[...].astype(BF16), wg_ref[...])
        ple = _dot_nt(p_ref[...].astype(BF16), wp_ref[...])
        o_ref[...] = xr_ref[...] + _sigmoid(s) * ple

    return pl.pallas_call(
        body, name=name, grid=(t // tm, d // tn),
        in_specs=[pl.BlockSpec((tm, d), lambda i, j: (i, 0)), pl.BlockSpec((tm, tn), lambda i, j: (i, j)),
                  pl.BlockSpec((tm, e), lambda i, j: (i, 0)), pl.BlockSpec((d, tn), lambda i, j: (0, j)),
                  pl.BlockSpec((tn, e), lambda i, j: (j, 0))],
        out_specs=pl.BlockSpec((tm, tn), lambda i, j: (i, j)),
        out_shape=jax.ShapeDtypeStruct((t, d), F32),
        compiler_params=_params("parallel", "parallel"),
    )(x, x, p, w_gate, w_proj_t)


def _ple_bwd(x, p, w_gate, w_proj_t, dout, *, name):
    t, d = x.shape
    e = p.shape[1]
    tm, tn = _tile(t, 1024), _tile(d, 512)

    def body(xf_ref, p_ref, wg_ref, wp_ref, do_ref, ds_ref, dple_ref):
        s = _dot(xf_ref[...].astype(BF16), wg_ref[...])
        ple = _dot_nt(p_ref[...].astype(BF16), wp_ref[...])
        gate = _sigmoid(s)
        dov = do_ref[...]
        dple_ref[...] = (dov * gate).astype(BF16)
        ds_ref[...] = (dov * ple * gate * (1.0 - gate)).astype(BF16)

    ospec = pl.BlockSpec((tm, tn), lambda i, j: (i, j))
    return pl.pallas_call(
        body, name=name, grid=(t // tm, d // tn),
        in_specs=[pl.BlockSpec((tm, d), lambda i, j: (i, 0)), pl.BlockSpec((tm, e), lambda i, j: (i, 0)),
                  pl.BlockSpec((d, tn), lambda i, j: (0, j)), pl.BlockSpec((tn, e), lambda i, j: (j, 0)), ospec],
        out_specs=[ospec, ospec],
        out_shape=[jax.ShapeDtypeStruct((t, d), BF16), jax.ShapeDtypeStruct((t, d), BF16)],
        compiler_params=_params("parallel", "parallel"),
    )(x, p, w_gate, w_proj_t, dout)


CONV_TIME_TILE = 256
CONV_HALO = 8


def _conv_taps(ext, w):
    acc = ext[CONV_HALO:, :] * w[CONV_WIDTH - 1:CONV_WIDTH, :]
    shifted = [ext[CONV_HALO:, :]]
    for j in range(1, CONV_WIDTH):
        sh = pltpu.roll(ext, j, 0)[CONV_HALO:, :]
        shifted.append(sh)
        acc = acc + sh * w[CONV_WIDTH - 1 - j:CONV_WIDTH - j, :]
    return acc, shifted


def _conv_fwd(u, w, b, side=None):
    t, c = u.shape
    tc = _tile(c, 256)
    tt = CONV_TIME_TILE

    def body(u_ref, w_ref, b_ref, o_ref):
        wv, bv = w_ref[...], b_ref[...]

        def tile(start, ext):
            pre = _conv_taps(ext, wv)[0] + bv
            o_ref[pl.ds(start, tt), :] = pre * _sigmoid(pre)

        tile(0, jnp.concatenate([jnp.zeros((CONV_HALO, tc), F32), u_ref[0:tt, :]], axis=0))

        def loop(i, carry):
            start = pl.multiple_of(i * tt, tt)
            tile(start, u_ref[pl.ds(start - CONV_HALO, tt + CONV_HALO), :])
            return carry

        lax.fori_loop(1, t // tt, loop, 0)

    col = pl.BlockSpec((t, tc), lambda j: (0, j))
    return _call(
        body, side, name="conv_fwd", grid=(c // tc,),
        in_specs=[col, pl.BlockSpec((CONV_WIDTH, tc), lambda j: (0, j)), pl.BlockSpec((1, tc), lambda j: (0, j))],
        out_specs=[col], out_shape=[jax.ShapeDtypeStruct((t, c), F32)],
        scratch_shapes=[], semantics=("parallel",), args=(u, w, b),
    )[0]


def _conv_bwd(u, w, b, dact, side=None):
    t, c = u.shape
    tc = _tile(c, 256)
    tt = CONV_TIME_TILE

    def body(u_ref, w_ref, b_ref, da_ref, du_ref, dw_ref, db_ref, dpre_ref):
        wv, bv = w_ref[...], b_ref[...]

        def tile(start, ext, sums):
            acc, shifted = _conv_taps(ext, wv)
            pre = acc + bv
            sg = _sigmoid(pre)
            dpre = da_ref[pl.ds(start, tt), :] * (sg * (1.0 + pre * (1.0 - sg)))
            dpre_ref[pl.ds(start, tt), :] = dpre
            new = [sums[0] + jnp.sum(dpre, axis=0, keepdims=True)]
            for j in range(CONV_WIDTH):
                new.append(sums[1 + j] + jnp.sum(dpre * shifted[j], axis=0, keepdims=True))
            return tuple(new)

        zero = jnp.zeros((1, tc), F32)
        sums = tile(0, jnp.concatenate([jnp.zeros((CONV_HALO, tc), F32), u_ref[0:tt, :]], axis=0),
                    (zero,) * (1 + CONV_WIDTH))

        def loop(i, sums):
            start = pl.multiple_of(i * tt, tt)
            return tile(start, u_ref[pl.ds(start - CONV_HALO, tt + CONV_HALO), :], sums)

        sums = lax.fori_loop(1, t // tt, loop, sums)
        db_ref[...] = sums[0]
        dw_ref[...] = jnp.concatenate([sums[1 + (CONV_WIDTH - 1 - k)] for k in range(CONV_WIDTH)], axis=0)
        dpre_ref[pl.ds(t, CONV_HALO), :] = jnp.zeros((CONV_HALO, tc), F32)

        def loop2(i, carry):
            start = pl.multiple_of(i * tt, tt)
            ext = dpre_ref[pl.ds(start, tt + CONV_HALO), :]
            acc = ext[0:tt, :] * wv[CONV_WIDTH - 1:CONV_WIDTH, :]
            for j in range(1, CONV_WIDTH):
                acc = acc + pltpu.roll(ext, tt + CONV_HALO - j, 0)[0:tt, :] * wv[CONV_WIDTH - 1 - j:CONV_WIDTH - j, :]
            du_ref[pl.ds(start, tt), :] = acc.astype(BF16)
            return carry

        lax.fori_loop(0, t // tt, loop2, 0)

    col = pl.BlockSpec((t, tc), lambda j: (0, j))
    return _call(
        body, side, name="conv_bwd", grid=(c // tc,),
        in_specs=[col, pl.BlockSpec((CONV_WIDTH, tc), lambda j: (0, j)), pl.BlockSpec((1, tc), lambda j: (0, j)), col],
        out_specs=[col, pl.BlockSpec((CONV_WIDTH, tc), lambda j: (0, j)), pl.BlockSpec((1, tc), lambda j: (0, j))],
        out_shape=[jax.ShapeDtypeStruct((t, c), BF16), jax.ShapeDtypeStruct((CONV_WIDTH, c), F32),
                   jax.ShapeDtypeStruct((1, c), F32)],
        scratch_shapes=[pltpu.VMEM((t + CONV_HALO, tc), F32)],
        semantics=("parallel",), args=(u, w, b, dact),
    )


def _softplus(v):
    e = jnp.exp(-jnp.abs(v))
    w = 1.0 + e
    log1p = jnp.where(w == 1.0, e, jnp.log(w) * (e / jnp.where(w == 1.0, 1.0, w - 1.0)))
    return jnp.maximum(v, 0.0) + log1p


def _ssd_prep_fwd(dt_raw, dt_bias, a_log):
    t = dt_raw.shape[0]
    cl = SSD_CHUNK

    def body(r_ref, b_ref, al_ref, dt_ref, acs_ref):
        dt = _softplus(r_ref[...] + b_ref[...])
        adt = dt * (-jnp.exp(al_ref[...]))
        li = lax.broadcasted_iota(jnp.int32, (cl, cl), 0)
        si = lax.broadcasted_iota(jnp.int32, (cl, cl), 1)
        tri = (si <= li).astype(F32)
        dt_ref[...] = dt
        acs_ref[...] = jnp.dot(tri, adt, preferred_element_type=F32, precision=HIGHEST)

    row = pl.BlockSpec((cl, LANES), lambda i: (i, 0))
    vec = pl.BlockSpec((1, LANES), lambda i: (0, 0))
    return pl.pallas_call(
        body, name="ssd_prep_fwd", grid=(t // cl,),
        in_specs=[row, vec, vec], out_specs=[row, row],
        out_shape=[jax.ShapeDtypeStruct((t, LANES), F32), jax.ShapeDtypeStruct((t, LANES), F32)],
        compiler_params=_params("parallel"),
    )(dt_raw, dt_bias, a_log)


def _ssd_prep_bwd(dt_raw, dt_bias, ddt):
    t = dt_raw.shape[0]
    tm = _tile(t, 512)

    def body(r_ref, b_ref, d_ref, o_ref, db_ref):
        g = d_ref[...] * _sigmoid(r_ref[...] + b_ref[...])
        o_ref[...] = g.astype(BF16)
        part = jnp.sum(g, axis=0, keepdims=True)

        @pl.when(pl.program_id(0) == 0)
        def _():
            db_ref[...] = part

        @pl.when(pl.program_id(0) > 0)
        def _():
            db_ref[...] += part

    row = pl.BlockSpec((tm, LANES), lambda i: (i, 0))
    vec = pl.BlockSpec((1, LANES), lambda i: (0, 0))
    return pl.pallas_call(
        body, name="ssd_prep_bwd", grid=(t // tm,),
        in_specs=[row, vec, row], out_specs=[row, vec],
        out_shape=[jax.ShapeDtypeStruct((t, LANES), BF16), jax.ShapeDtypeStruct((1, LANES), F32)],
        compiler_params=_params("arbitrary"),
    )(dt_raw, dt_bias, ddt)


GROUP_W = D_INNER // SSM_GROUPS
PAIRS_PER_GROUP = GROUP_W // LANES


def _head_cols(acs_pair, lt64):
    rolled = pltpu.roll(acs_pair, ATT_HEAD_DIM, 1)
    return jnp.where(lt64, acs_pair, rolled), jnp.where(lt64, rolled, acs_pair)


def _ssd_fwd(xbc, dt_rep, acs_rep, acs_t, dskip_rep, side=None):
    t = xbc.shape[0]
    cl = SSD_CHUNK
    nc = t // cl

    def body(xbc_ref, dt_ref, acs_ref, acst_ref, dskip_ref, y_ref, hin_ref, state_ref):
        @pl.when(pl.program_id(0) == 0)
        def _():
            state_ref[...] = jnp.zeros_like(state_ref)

        lt64 = _lane_lt64(cl)
        li = lax.broadcasted_iota(jnp.int32, (cl, cl), 0)
        si = lax.broadcasted_iota(jnp.int32, (cl, cl), 1)
        causal = li >= si
        hin_ref[...] = state_ref[...]
        for g in range(SSM_GROUPS):
            gsl = slice(g * GROUP_W, (g + 1) * GROUP_W)
            xg = xbc_ref[:, gsl]
            bg = xbc_ref[:, D_INNER + g * SSM_STATE:D_INNER + (g + 1) * SSM_STATE]
            cg = xbc_ref[:, D_INNER + SSM_GROUPS * SSM_STATE + g * SSM_STATE:
                         D_INNER + SSM_GROUPS * SSM_STATE + (g + 1) * SSM_STATE]
            acs = acs_ref[:, gsl]
            xdt = xg * dt_ref[:, gsl]
            atot = acs[cl - 1:cl, :]
            hin = state_ref[:, gsl]
            cgb = cg.astype(BF16)
            gmat = _dot_nt(cgb, bg.astype(BF16))
            yoff = _dot(cgb, hin.astype(BF16)) * jnp.exp(acs)
            snew = _dot(bg.T.astype(BF16), (xdt * jnp.exp(atot - acs)).astype(BF16))
            state_ref[:, gsl] = hin * jnp.exp(atot) + snew
            xdtb = xdt.astype(BF16)
            for pr in range(PAIRS_PER_GROUP):
                psl = slice(pr * LANES, (pr + 1) * LANES)
                cols = _head_cols(acs[:, psl], lt64)
                xp = xdtb[:, psl]
                ys = []
                for hh in range(2):
                    h = (g * PAIRS_PER_GROUP + pr) * 2 + hh
                    seg = cols[hh] - acst_ref[h:h + 1, :]
                    lm = jnp.exp(jnp.where(causal, seg, NEG_BIG))
                    ys.append(_dot((gmat * lm).astype(BF16), xp))
                ydiag = jnp.where(lt64, ys[0], ys[1])
                osl = slice(g * GROUP_W + pr * LANES, g * GROUP_W + (pr + 1) * LANES)
                y_ref[:, osl] = ydiag + yoff[:, psl] + xg[:, psl] * dskip_ref[:, osl]

    row = lambda w: pl.BlockSpec((cl, w), lambda c: (c, 0))
    return _call(
        body, side, name="ssd_fwd", grid=(nc,),
        in_specs=[row(CONV_DIM), row(D_INNER), row(D_INNER),
                  pl.BlockSpec((SSM_HEADS, cl), lambda c: (0, c)), pl.BlockSpec((1, D_INNER), lambda c: (0, 0))],
        out_specs=[row(D_INNER), pl.BlockSpec((None, SSM_STATE, D_INNER), lambda c: (c, 0, 0))],
        out_shape=[jax.ShapeDtypeStruct((t, D_INNER), F32), jax.ShapeDtypeStruct((nc, SSM_STATE, D_INNER), F32)],
        scratch_shapes=[pltpu.VMEM((SSM_STATE, D_INNER), F32)],
        semantics=("arbitrary",), args=(xbc, dt_rep, acs_rep, acs_t, dskip_rep),
    )


def _ssd_bwd(xbc, dt_rep, acs_rep, acs_t, dskip_rep, a_rep, hin_all, dy, side=None):
    t = xbc.shape[0]
    cl = SSD_CHUNK
    nc = t // cl

    def body(xbc_ref, dt_ref, acs_ref, acst_ref, dskip_ref, a_ref, hin_ref, dy_ref,
             dxbc_ref, ddt_ref, da_ref, dds_ref, dstate_ref, dacs_ref, dxs_ref):
        step = pl.program_id(0)

        @pl.when(step == 0)
        def _():
            dstate_ref[...] = jnp.zeros_like(dstate_ref)
            da_ref[...] = jnp.zeros_like(da_ref)
            dds_ref[...] = jnp.zeros_like(dds_ref)

        bd = _head_block_diag()
        lt64 = _lane_lt64(cl)
        li = lax.broadcasted_iota(jnp.int32, (cl, cl), 0)
        si = lax.broadcasted_iota(jnp.int32, (cl, cl), 1)
        lower = li >= si
        upper = si >= li
        last_row = lax.broadcasted_iota(jnp.int32, (cl, GROUP_W), 0) == cl - 1
        for g in range(SSM_GROUPS):
            gsl = slice(g * GROUP_W, (g + 1) * GROUP_W)
            bsl = slice(D_INNER + g * SSM_STATE, D_INNER + (g + 1) * SSM_STATE)
            csl = slice(D_INNER + SSM_GROUPS * SSM_STATE + g * SSM_STATE,
                        D_INNER + SSM_GROUPS * SSM_STATE + (g + 1) * SSM_STATE)
            xg = xbc_ref[:, gsl]
            bg = xbc_ref[:, bsl]
            cg = xbc_ref[:, csl]
            bgb, cgb = bg.astype(BF16), cg.astype(BF16)
            acs = acs_ref[:, gsl]
            xdt = xg * dt_ref[:, gsl]
            atot = acs[cl - 1:cl, :]
            eg = jnp.exp(acs)
            dk = jnp.exp(atot - acs)
            etot = jnp.exp(atot)
            hin = hin_ref[:, gsl]
            hinb = hin.astype(BF16)
            dh = dstate_ref[:, gsl]
            dhb = dh.astype(BF16)
            dyg = dy_ref[:, gsl]

            gmat = _dot_nt(cgb, bgb)
            gmat_t = _dot_nt(bgb, cgb)
            ch = _dot(cgb, hinb)
            dacs = _head_sums(dyg * ch * eg, bd)
            dye = (dyg * eg).astype(BF16)
            dc = _dot_nt(dye, hinb)
            dhin = _dot(cg.T.astype(BF16), dye)
            bdh = _dot(bgb, dhb)
            dxs = bdh * dk
            xdk = xdt * dk
            db = _dot_nt(xdk.astype(BF16), dhb)
            ddk = _head_sums(bdh * xdk, bd)
            dacs = dacs - ddk
            datot = jnp.sum(ddk, axis=0, keepdims=True) + etot * _head_sums(
                jnp.sum(dh * hin, axis=0, keepdims=True), bd)
            dacs = dacs + jnp.where(last_row, datot, 0.0)
            dstate_ref[:, gsl] = dh * etot + dhin

            xdtb = xdt.astype(BF16)
            dgsum = jnp.zeros((cl, cl), F32)
            dgsum_t = jnp.zeros((cl, cl), F32)
            for pr in range(PAIRS_PER_GROUP):
                psl = slice(pr * LANES, (pr + 1) * LANES)
                cols = _head_cols(acs[:, psl], lt64)
                xp = xdtb[:, psl]
                dyp = dyg[:, psl].astype(BF16)
                dx1, dac = [], []
                for hh in range(2):
                    h = (g * PAIRS_PER_GROUP + pr) * 2 + hh
                    mine = lt64 if hh == 0 else jnp.logical_not(lt64)
                    row = acst_ref[h:h + 1, :]
                    lm = jnp.exp(jnp.where(lower, cols[hh] - row, NEG_BIG))
                    lm_t = jnp.exp(jnp.where(upper, row - cols[hh], NEG_BIG))
                    dyh = jnp.where(mine, dyp, jnp.zeros_like(dyp))
                    xh = jnp.where(mine, xp, jnp.zeros_like(xp))
                    dm = _dot_nt(dyh, xp)
                    dm_t = _dot_nt(xh, dyp)
                    m_t = gmat_t * lm_t
                    dx1.append(_dot(m_t.astype(BF16), dyp))
                    w = dm * (gmat * lm)
                    w_t = dm_t * m_t
                    dac.append(jnp.sum(w, axis=1, keepdims=True) - jnp.sum(w_t, axis=1, keepdims=True))
                    dgsum = dgsum + dm * lm
                    dgsum_t = dgsum_t + dm_t * lm_t
                osl = slice(g * GROUP_W + pr * LANES, g * GROUP_W + (pr + 1) * LANES)
                dxs_ref[:, osl] = dxs[:, psl] + jnp.where(lt64, dx1[0], dx1[1])
                dacs_ref[:, osl] = dacs[:, psl] + jnp.where(lt64, jnp.broadcast_to(dac[0], (cl, LANES)),
                                                             jnp.broadcast_to(dac[1], (cl, LANES)))
            dxbc_ref[:, csl] = dc + _dot(dgsum.astype(BF16), bgb)
            dxbc_ref[:, bsl] = db + _dot(dgsum_t.astype(BF16), cgb)

        dadt = _split_dot(upper.astype(BF16), dacs_ref[...])
        xall = xbc_ref[:, 0:D_INNER]
        dtall = dt_ref[...]
        dxsall = dxs_ref[...]
        dyall = dy_ref[...]
        ddt_ref[...] = dadt * a_ref[...] + _head_sums(dxsall * xall, bd)
        dxbc_ref[:, 0:D_INNER] = dxsall * dtall + dyall * dskip_ref[...]
        da_ref[...] += jnp.sum(dadt * dtall, axis=0, keepdims=True)
        dds_ref[...] += jnp.sum(dyall * xall, axis=0, keepdims=True)

        @pl.when(step == nc - 1)
        def _():
            dds_ref[...] = _head_sums(dds_ref[...], bd)

    row = lambda w: pl.BlockSpec((cl, w), lambda c: (nc - 1 - c, 0))
    vec = pl.BlockSpec((1, D_INNER), lambda c: (0, 0))
    return _call(
        body, side, name="ssd_bwd", grid=(nc,),
        in_specs=[row(CONV_DIM), row(D_INNER), row(D_INNER),
                  pl.BlockSpec((SSM_HEADS, cl), lambda c: (0, nc - 1 - c)), vec, vec,
                  pl.BlockSpec((None, SSM_STATE, D_INNER), lambda c: (nc - 1 - c, 0, 0)), row(D_INNER)],
        out_specs=[row(CONV_DIM), row(D_INNER), vec, vec],
        out_shape=[jax.ShapeDtypeStruct((t, CONV_DIM), F32), jax.ShapeDtypeStruct((t, D_INNER), F32),
                   jax.ShapeDtypeStruct((1, D_INNER), F32), jax.ShapeDtypeStruct((1, D_INNER), F32)],
        scratch_shapes=[pltpu.VMEM((SSM_STATE, D_INNER), F32), pltpu.VMEM((cl, D_INNER), F32),
                        pltpu.VMEM((cl, D_INNER), F32)],
        semantics=("arbitrary",), args=(xbc, dt_rep, acs_rep, acs_t, dskip_rep, a_rep, hin_all, dy),
    )


def _gate_norm_fwd(y, z, w):
    t, c = y.shape
    tm = _tile(t, 256)

    def body(y_ref, z_ref, w_ref, o_ref):
        for g in range(SSM_GROUPS):
            gsl = slice(g * GROUP_W, (g + 1) * GROUP_W)
            zv = z_ref[:, gsl]
            v = y_ref[:, gsl] * (zv * _sigmoid(zv))
            r = lax.rsqrt(jnp.mean(v * v, axis=-1, keepdims=True) + NORM_EPS)
            o_ref[:, gsl] = (v * r * w_ref[:, gsl]).astype(BF16)

    row = pl.BlockSpec((tm, c), lambda i: (i, 0))
    return pl.pallas_call(
        body, name="gate_norm_fwd", grid=(t // tm,),
        in_specs=[row, row, pl.BlockSpec((1, c), lambda i: (0, 0))], out_specs=row,
        out_shape=jax.ShapeDtypeStruct((t, c), BF16),
        compiler_params=_params("parallel"),
    )(y, z, w)


def _gate_norm_bwd(y, z, w, dout, side=None):
    t, c = y.shape
    tm = _tile(t, 256)

    def body(y_ref, z_ref, w_ref, do_ref, dy_ref, dz_ref, dw_ref):
        @pl.when(pl.program_id(0) == 0)
        def _():
            dw_ref[...] = jnp.zeros_like(dw_ref)

        for g in range(SSM_GROUPS):
            gsl = slice(g * GROUP_W, (g + 1) * GROUP_W)
            zv, yv, dov = z_ref[:, gsl], y_ref[:, gsl], do_ref[:, gsl]
            sg = _sigmoid(zv)
            sz = zv * sg
            v = yv * sz
            r = lax.rsqrt(jnp.mean(v * v, axis=-1, keepdims=True) + NORM_EPS)
            vh = v * r
            dvh = dov * w_ref[:, gsl]
            mean = jnp.mean(dvh * vh, axis=-1, keepdims=True)
            dv = r * (dvh - vh * mean)
            dy_ref[:, gsl] = dv * sz
            dz_ref[:, gsl] = (dv * yv * (sg * (1.0 + zv * (1.0 - sg)))).astype(BF16)
            dw_ref[:, gsl] += jnp.sum(dov * vh, axis=0, keepdims=True)

    row = pl.BlockSpec((tm, c), lambda i: (i, 0))
    vec = pl.BlockSpec((1, c), lambda i: (0, 0))
    return _call(
        body, side, name="gate_norm_bwd", grid=(t // tm,),
        in_specs=[row, row, vec, row], out_specs=[row, row, vec],
        out_shape=[jax.ShapeDtypeStruct((t, c), F32), jax.ShapeDtypeStruct((t, c), BF16),
                   jax.ShapeDtypeStruct((1, c), F32)],
        scratch_shapes=[], semantics=("arbitrary",), args=(y, z, w, dout),
    )


ATT_W = ATT_HEADS * ATT_HEAD_DIM
N_QKV_BLOCKS = 9
ATT_SCALE = 1.0 / math.sqrt(ATT_HEAD_DIM)


def _head_rmsnorm(x, gain, bd):
    ms = _head_sums(x * x, bd) * (1.0 / ATT_HEAD_DIM)
    return x * lax.rsqrt(ms + NORM_EPS) * gain


def _class_rows(ref, blk, r, dil):
    span = ATT_BLOCK * dil
    sub = ref.at[pl.ds(pl.multiple_of(blk * span, span), span), :]
    return sub[...] if dil == 1 else sub[pl.ds(r, ATT_BLOCK, stride=dil), :]


def _store_class_rows(ref, blk, r, dil, val):
    span = ATT_BLOCK * dil
    sub = ref.at[pl.ds(pl.multiple_of(blk * span, span), span), :]
    if dil == 1:
        sub[...] = val
    else:
        sub[pl.ds(r, ATT_BLOCK, stride=dil), :] = val


def _qk_norm_bwd(qkv, gq, gk, grads):
    t = qkv.shape[0]
    tm = _tile(t, 256)

    def body(x_ref, gq_ref, gk_ref, *rest):
        g_refs = rest[:N_QKV_BLOCKS]
        o_ref, dgq_ref, dgk_ref = rest[N_QKV_BLOCKS:]
        cb = pl.program_id(1)

        @pl.when(jnp.logical_and(pl.program_id(0) == 0, cb == 0))
        def _():
            dgq_ref[...] = jnp.zeros_like(dgq_ref)
            dgk_ref[...] = jnp.zeros_like(dgk_ref)

        def norm_bwd(dy, gain, dg_ref):
            bd = _head_block_diag()
            xv = x_ref[...]
            ms = _head_sums(xv * xv, bd) * (1.0 / ATT_HEAD_DIM)
            r = lax.rsqrt(ms + NORM_EPS)
            xh = xv * r
            dxh = dy * gain
            mean = _head_sums(dxh * xh, bd) * (1.0 / ATT_HEAD_DIM)
            o_ref[...] = (r * (dxh - xh * mean)).astype(BF16)
            dg_ref[...] += jnp.sum(dy * xh, axis=0, keepdims=True)

        for k in range(N_QKV_BLOCKS):
            @pl.when(cb == k)
            def _(k=k):
                if k % 3 == 0:
                    norm_bwd(g_refs[k][...], gq_ref[...], dgq_ref)
                elif k % 3 == 1:
                    norm_bwd(g_refs[k][...], gk_ref[...], dgk_ref)
                else:
                    o_ref[...] = g_refs[k][...].astype(BF16)

    blk = pl.BlockSpec((tm, ATT_W), lambda i, j: (i, j))
    one = pl.BlockSpec((tm, ATT_W), lambda i, j: (i, 0))
    vec = pl.BlockSpec((1, ATT_W), lambda i, j: (0, 0))
    return pl.pallas_call(
        body, name="qk_norm_bwd", grid=(t // tm, N_QKV_BLOCKS),
        in_specs=[blk, vec, vec] + [one] * N_QKV_BLOCKS, out_specs=[blk, vec, vec],
        out_shape=[jax.ShapeDtypeStruct(qkv.shape, BF16), jax.ShapeDtypeStruct((1, ATT_W), F32),
                   jax.ShapeDtypeStruct((1, ATT_W), F32)],
        compiler_params=_params("arbitrary", "arbitrary"),
    )(qkv, gq, gk, *grads)


PAIRS = ATT_HEADS // 2


def _pair_col(g, j):
    return lambda pair: (0, (g * 3 + j) * PAIRS + pair)


def _pair_slopes(pair):
    steps = jnp.full((1, 2 * ATT_BLOCK), 2 * pair + 1, jnp.int32).astype(F32)
    first = jnp.exp(steps * (-0.5 * math.log(2.0)))
    return first, first * (2.0 ** -0.5)


NORM_ROWS = 512


def _band2(pair, dil, transposed):
    bq = ATT_BLOCK
    a = lax.broadcasted_iota(jnp.int32, (2 * bq, 2 * bq), 0) % bq
    b = lax.broadcasted_iota(jnp.int32, (2 * bq, 2 * bq), 1)
    dist = (b - a) if transposed else (a + bq - b)
    in_band = (dist >= 0) & (dist <= bq)
    s0, s1 = _pair_slopes(pair)
    first_head = lax.broadcasted_iota(jnp.int32, (2 * bq, 2 * bq), 0) < bq
    bias = jnp.where(first_head, s0, s1) * (dist.astype(F32) * float(dil))
    return in_band, bias, b


def _stack_heads(tile):
    rows = lax.broadcasted_iota(jnp.int32, (2 * ATT_BLOCK, LANES), 0) < ATT_BLOCK
    lanes = lax.broadcasted_iota(jnp.int32, (2 * ATT_BLOCK, LANES), 1) < ATT_HEAD_DIM
    both = jnp.concatenate([tile, tile], axis=0)
    return jnp.where(rows == lanes, both, jnp.zeros_like(both))


def _unstack_heads(stacked, lt64):
    return jnp.where(lt64, stacked[:ATT_BLOCK], stacked[ATT_BLOCK:])


def _block_loop(nb, dil, step):
    per_trip = 4 if dil == 1 else 1

    def trip(i, carry):
        for b in range(per_trip):
            step(i * per_trip + b, carry)
        return carry

    lax.fori_loop(0, nb // per_trip, trip, 0)


def _normalise_qk(q_ref, k_ref, gq_ref, gk_ref, qn_ref, kn_ref):
    bd = _head_block_diag()

    def step(i, carry):
        rows = pl.ds(pl.multiple_of(i * NORM_ROWS, NORM_ROWS), NORM_ROWS)
        qn_ref[rows, :] = _head_rmsnorm(q_ref[rows, :], gq_ref[...], bd)
        kn_ref[rows, :] = _head_rmsnorm(k_ref[rows, :], gk_ref[...], bd)
        return carry

    lax.fori_loop(0, q_ref.shape[0] // NORM_ROWS, step, 0)


def _attn_fwd(qkv, gq, gk, g, dil):
    t = qkv.shape[0]
    nb = t // dil // ATT_BLOCK
    bq = ATT_BLOCK

    def body(q_ref, k_ref, v_ref, gq_ref, gk_ref, o_ref, l_ref, qn_ref, kn_ref):
        _normalise_qk(q_ref, k_ref, gq_ref, gk_ref, qn_ref, kn_ref)
        lt64 = _lane_lt64(bq)
        in_band, bias, key = _band2(pl.program_id(0), dil, False)

        def step(n, carry):
            valid = in_band & ((key >= bq) | (n > 0))
            prev = jnp.maximum(n - 1, 0)
            for r in range(dil):
                q2 = _stack_heads(_class_rows(qn_ref, n, r, dil).astype(BF16))
                kcat = jnp.concatenate([_class_rows(kn_ref, prev, r, dil), _class_rows(kn_ref, n, r, dil)],
                                       axis=0).astype(BF16)
                vcat = jnp.concatenate([_class_rows(v_ref, prev, r, dil), _class_rows(v_ref, n, r, dil)],
                                       axis=0).astype(BF16)
                s = jnp.where(valid, _dot_nt(q2, kcat) * ATT_SCALE - bias, NEG_BIG)
                m = jnp.max(s, axis=1, keepdims=True)
                p = jnp.exp(s - m)
                l = jnp.sum(p, axis=1, keepdims=True)
                out = _dot(p.astype(BF16), vcat) * (1.0 / l)
                lse = jnp.broadcast_to(m + jnp.log(l), (2 * bq, LANES))
                _store_class_rows(o_ref, n, r, dil, _unstack_heads(out, lt64))
                _store_class_rows(l_ref, n, r, dil, _unstack_heads(lse, lt64))
            return carry

        _block_loop(nb, dil, step)

    col = lambda j: pl.BlockSpec((t, LANES), _pair_col(g, j))
    vec = pl.BlockSpec((1, LANES), lambda pair: (0, 0))
    out = pl.BlockSpec((t, LANES), lambda pair: (0, pair))
    return pl.pallas_call(
        body, name=f"attn_fwd_g{g}", grid=(PAIRS,),
        in_specs=[col(0), col(1), col(2), vec, vec], out_specs=[out, out],
        out_shape=[jax.ShapeDtypeStruct((t, ATT_W), F32), jax.ShapeDtypeStruct((t, ATT_W), F32)],
        scratch_shapes=[pltpu.VMEM((t, LANES), F32), pltpu.VMEM((t, LANES), F32)],
        compiler_params=_params("parallel"),
    )(qkv, qkv, qkv, gq, gk)


def _attn_combine_fwd(outs, lses):
    t = outs[0].shape[0]
    tm = _tile(t, 256)

    def body(o0, o1, o2, l0, l1, l2, ob_ref, of_ref, lt_ref):
        a, b, c = l0[...], l1[...], l2[...]
        m = jnp.maximum(jnp.maximum(a, b), c)
        ea, eb, ec = jnp.exp(a - m), jnp.exp(b - m), jnp.exp(c - m)
        ssum = ea + eb + ec
        o = (ea * o0[...] + eb * o1[...] + ec * o2[...]) / ssum
        ob_ref[...] = o.astype(BF16)
        of_ref[...] = o
        lt_ref[...] = m + jnp.log(ssum)

    row = pl.BlockSpec((tm, ATT_W), lambda i: (i, 0))
    return pl.pallas_call(
        body, name="attn_combine_fwd", grid=(t // tm,),
        in_specs=[row] * 6, out_specs=[row] * 3,
        out_shape=[jax.ShapeDtypeStruct((t, ATT_W), BF16), jax.ShapeDtypeStruct((t, ATT_W), F32),
                   jax.ShapeDtypeStruct((t, ATT_W), F32)],
        compiler_params=_params("parallel"),
    )(*outs, *lses)


def _attn_combine_bwd(do, o):
    t = do.shape[0]
    tm = _tile(t, 256)

    def body(do_ref, o_ref, dl_ref):
        dl_ref[...] = _head_sums(do_ref[...] * o_ref[...], _head_block_diag())

    row = pl.BlockSpec((tm, ATT_W), lambda i: (i, 0))
    return pl.pallas_call(
        body, name="attn_combine_bwd", grid=(t // tm,),
        in_specs=[row, row], out_specs=row, out_shape=jax.ShapeDtypeStruct((t, ATT_W), F32),
        compiler_params=_params("parallel"),
    )(do, o)


def _attn_bwd_dq(qkv, gq, gk, do, l_rep, dl_rep, g, dil):
    t = qkv.shape[0]
    nb = t // dil // ATT_BLOCK
    bq = ATT_BLOCK

    def body(q_ref, k_ref, v_ref, gq_ref, gk_ref, do_ref, l_ref, dl_ref, dq_ref, qn_ref, kn_ref):
        _normalise_qk(q_ref, k_ref, gq_ref, gk_ref, qn_ref, kn_ref)
        lt64 = _lane_lt64(bq)
        in_band, bias, key = _band2(pl.program_id(0), dil, False)

        def per_row(tile):
            cols = _head_cols(tile, lt64)
            half = jnp.concatenate([cols[0], cols[1]], axis=0)
            return jnp.concatenate([half, half], axis=1)

        def step(n, carry):
            valid = in_band & ((key >= bq) | (n > 0))
            prev = jnp.maximum(n - 1, 0)
            for r in range(dil):
                q2 = _stack_heads(_class_rows(qn_ref, n, r, dil).astype(BF16))
                do2 = _stack_heads(_class_rows(do_ref, n, r, dil).astype(BF16))
                kcat = jnp.concatenate([_class_rows(kn_ref, prev, r, dil), _class_rows(kn_ref, n, r, dil)],
                                       axis=0).astype(BF16)
                vcat = jnp.concatenate([_class_rows(v_ref, prev, r, dil), _class_rows(v_ref, n, r, dil)],
                                       axis=0).astype(BF16)
                s = jnp.where(valid, _dot_nt(q2, kcat) * ATT_SCALE - bias, NEG_BIG)
                p = jnp.exp(s - per_row(_class_rows(l_ref, n, r, dil)))
                ds = p * (_dot_nt(do2, vcat) - per_row(_class_rows(dl_ref, n, r, dil)))
                dq = _dot(ds.astype(BF16), kcat) * ATT_SCALE
                _store_class_rows(dq_ref, n, r, dil, _unstack_heads(dq, lt64))
            return carry

        _block_loop(nb, dil, step)

    col = lambda j: pl.BlockSpec((t, LANES), _pair_col(g, j))
    vec = pl.BlockSpec((1, LANES), lambda pair: (0, 0))
    tok = pl.BlockSpec((t, LANES), lambda pair: (0, pair))
    return pl.pallas_call(
        body, name=f"attn_bwd_dq_g{g}", grid=(PAIRS,),
        in_specs=[col(0), col(1), col(2), vec, vec, tok, tok, tok], out_specs=tok,
        out_shape=jax.ShapeDtypeStruct((t, ATT_W), F32),
        scratch_shapes=[pltpu.VMEM((t, LANES), F32), pltpu.VMEM((t, LANES), F32)],
        compiler_params=_params("parallel"),
    )(qkv, qkv, qkv, gq, gk, do, l_rep, dl_rep)


def _attn_bwd_dkv(qkv, gq, gk, do, l_row, dl_row, g, dil):
    t = qkv.shape[0]
    nb = t // dil // ATT_BLOCK
    bq = ATT_BLOCK

    def body(q_ref, k_ref, v_ref, gq_ref, gk_ref, do_ref, l_ref, dl_ref, dk_ref, dv_ref, qn_ref, kn_ref):
        _normalise_qk(q_ref, k_ref, gq_ref, gk_ref, qn_ref, kn_ref)
        lt64 = _lane_lt64(bq)
        in_band, bias, query = _band2(pl.program_id(0), dil, True)

        def per_query(ref, lane_c, lane_n):
            heads = [jnp.broadcast_to(jnp.concatenate([ref[hh:hh + 1, pl.ds(lane_c, bq)],
                                                        ref[hh:hh + 1, pl.ds(lane_n, bq)]], axis=1), (bq, 2 * bq))
                     for hh in range(2)]
            return jnp.concatenate(heads, axis=0)

        def step(n, carry):
            valid = in_band & ((query < bq) | (n < nb - 1))
            nxt = jnp.minimum(n + 1, nb - 1)
            for r in range(dil):
                k2 = _stack_heads(_class_rows(kn_ref, n, r, dil).astype(BF16))
                v2 = _stack_heads(_class_rows(v_ref, n, r, dil).astype(BF16))
                qcat = jnp.concatenate([_class_rows(qn_ref, n, r, dil), _class_rows(qn_ref, nxt, r, dil)],
                                       axis=0).astype(BF16)
                docat = jnp.concatenate([_class_rows(do_ref, n, r, dil), _class_rows(do_ref, nxt, r, dil)],
                                        axis=0).astype(BF16)
                lane_c = pl.multiple_of((r * nb + n) * bq, bq)
                lane_n = pl.multiple_of((r * nb + nxt) * bq, bq)
                s_t = jnp.where(valid, _dot_nt(k2, qcat) * ATT_SCALE - bias, NEG_BIG)
                p_t = jnp.exp(s_t - per_query(l_ref, lane_c, lane_n))
                dv = _dot(p_t.astype(BF16), docat)
                ds_t = p_t * (_dot_nt(v2, docat) - per_query(dl_ref, lane_c, lane_n))
                dk = _dot(ds_t.astype(BF16), qcat) * ATT_SCALE
                _store_class_rows(dk_ref, n, r, dil, _unstack_heads(dk, lt64))
                _store_class_rows(dv_ref, n, r, dil, _unstack_heads(dv, lt64))
            return carry

        _block_loop(nb, dil, step)

    col = lambda j: pl.BlockSpec((t, LANES), _pair_col(g, j))
    vec = pl.BlockSpec((1, LANES), lambda pair: (0, 0))
    tok = pl.BlockSpec((t, LANES), lambda pair: (0, pair))
    rows = pl.BlockSpec((None, 8, t), lambda pair: (pair, 0, 0))
    return pl.pallas_call(
        body, name=f"attn_bwd_dkv_g{g}", grid=(PAIRS,),
        in_specs=[col(0), col(1), col(2), vec, vec, tok, rows, rows], out_specs=[tok, tok],
        out_shape=[jax.ShapeDtypeStruct((t, ATT_W), F32), jax.ShapeDtypeStruct((t, ATT_W), F32)],
        scratch_shapes=[pltpu.VMEM((t, LANES), F32), pltpu.VMEM((t, LANES), F32)],
        compiler_params=_params("parallel"),
    )(qkv, qkv, qkv, gq, gk, do, l_row, dl_row)


def _rows_by_residue(rep, dil):
    t = rep.shape[0]
    per_head = rep[:, ::ATT_HEAD_DIM]
    rows = per_head.reshape(t // dil, dil, ATT_HEADS).transpose(2, 1, 0).reshape(PAIRS, 2, t)
    return jnp.pad(rows, ((0, 0), (0, 6), (0, 0)))


def _per_head(rep_row):
    return rep_row[0, ::SSM_HEAD_DIM]


def _rep_heads(v):
    return jnp.repeat(v, SSM_HEAD_DIM)[None, :]


def _pad_lanes(v):
    return jnp.pad(v, ((0, 0), (0, LANES - v.shape[1])))


class _NoOverlap:
    def side(self, host):
        return None

    def after(self, host):
        pass

    def begin_backward(self, grads):
        pass


def _hosted(plan, host, fn, *args, **kwargs):
    out = fn(*args, side=plan.side(host), **kwargs)
    plan.after(host)
    return out


def _ffn_ple_fwd(x1, p_i, prm, i, plan):
    h = _rmsnorm_fwd(x1, prm["norm_ffn"][i:i + 1], name=f"ffn_norm_fwd_{i}")
    g, u, act = _hosted(plan, f"swiglu_fwd_{i}", _swiglu_fwd, h, prm["ffn_w_gate"][i], prm["ffn_w_up"][i],
                        name=f"swiglu_fwd_{i}")
    x2 = _hosted(plan, f"ffn_down_{i}", _matmul, act, prm["ffn_w_down"][i], mode="nn", addend=x1,
                 name=f"ffn_down_{i}")
    x3 = _ple_fwd(x2, p_i, prm["ple_w_gate"][i], prm["ple_w_proj"][i], name=f"ple_fwd_{i}")
    return x3, dict(x1=x1, h=h, g=g, u=u, act=act, x2=x2)


def _ffn_ple_bwd(dx3, p_i, prm, i, sv, grads, plan):
    ds, dple = _ple_bwd(sv["x2"], p_i, prm["ple_w_gate"][i], prm["ple_w_proj"][i], dx3, name=f"ple_bwd_{i}")
    grads["ple_w_gate"][i] = _matmul_tn(sv["x2"], ds, name=f"d_ple_w_gate_{i}")
    grads["ple_w_proj"][i] = _matmul_tn(dple, p_i, name=f"d_ple_w_proj_{i}")
    dx2 = _matmul(ds, prm["ple_w_gate"][i], mode="nt", addend=dx3, name=f"ple_dx_{i}")
    grads["ffn_w_down"][i] = _matmul_tn(sv["act"], dx2, name=f"d_ffn_w_down_{i}")
    dg, du = _hosted(plan, f"swiglu_bwd_{i}", _swiglu_bwd, dx2, prm["ffn_w_down"][i], sv["g"], sv["u"],
                     name=f"swiglu_bwd_{i}")
    grads["ffn_w_gate"][i] = _matmul_tn(dg, sv["h"], name=f"d_ffn_w_gate_{i}")
    grads["ffn_w_up"][i] = _matmul_tn(du, sv["h"], name=f"d_ffn_w_up_{i}")
    dh = _matmul(dg, prm["ffn_w_gate"][i], mode="nn", name=f"ffn_dh_gate_{i}")
    dh = _matmul(du, prm["ffn_w_up"][i], mode="nn", addend=dh, name=f"ffn_dh_up_{i}")
    dx1, dgain = _rmsnorm_bwd(sv["x1"], prm["norm_ffn"][i:i + 1], dh, dx2, name=f"ffn_norm_bwd_{i}")
    grads["norm_ffn"][i] = dgain[0]
    return dx1


def _mamba_fwd(x0, prm, plan):
    h = _rmsnorm_fwd(x0, prm["norm_mix"][0:1], name="mix_norm_fwd_0")
    z = _matmul(h, prm["ssm_w_z"], mode="nt", name="ssm_in_z")
    xbc_pre = _hosted(plan, "ssm_in_xbc", _matmul, h, prm["ssm_w_xbc"], mode="nt", name="ssm_in_xbc")
    dt_raw = _matmul(h, prm["ssm_w_dt"], mode="nt", name="ssm_in_dt")
    xbc = _hosted(plan, "conv_fwd", _conv_fwd, xbc_pre, prm["ssm_conv_w"], prm["ssm_conv_b"])
    dt_bias = _pad_lanes(prm["ssm_dt_bias"])
    a_log = _pad_lanes(prm["ssm_a_log"])
    dt, acs = _ssd_prep_fwd(dt_raw, dt_bias, a_log)
    dt_rep = jnp.repeat(dt[:, :SSM_HEADS], SSM_HEAD_DIM, axis=1)
    acs_rep = jnp.repeat(acs[:, :SSM_HEADS], SSM_HEAD_DIM, axis=1)
    acs_t = acs[:, :SSM_HEADS].T
    dskip_rep = _rep_heads(prm["ssm_d_skip"][0])
    y, hin_all = _hosted(plan, "ssd_fwd", _ssd_fwd, xbc, dt_rep, acs_rep, acs_t, dskip_rep)
    yn = _gate_norm_fwd(y, z, prm["ssm_norm_w"])
    x1 = _matmul(yn, prm["ssm_w_out"], mode="nn", addend=x0, name="ssm_out")
    sv = dict(x0=x0, h=h, z=z, xbc_pre=xbc_pre, dt_raw=dt_raw, xbc=xbc, dt_bias=dt_bias, dt_rep=dt_rep,
              acs_rep=acs_rep, acs_t=acs_t, dskip_rep=dskip_rep, y=y, hin_all=hin_all, yn=yn)
    return x1, sv


def _mamba_bwd(dx1, prm, sv, grads, plan):
    grads["ssm_w_out"] = _matmul_tn(sv["yn"], dx1, name="d_ssm_w_out")
    dyn = _matmul(dx1, prm["ssm_w_out"], mode="nt", name="ssm_out_dx")
    dy, dz, dnw = _hosted(plan, "gate_norm_bwd", _gate_norm_bwd, sv["y"], sv["z"], prm["ssm_norm_w"], dyn)
    grads["ssm_norm_w"] = dnw
    a_rep = _rep_heads(-jnp.exp(prm["ssm_a_log"][0]))
    dxbc, ddt_rep, da_rep, dds_rep = _hosted(plan, "ssd_bwd", _ssd_bwd, sv["xbc"], sv["dt_rep"], sv["acs_rep"],
                                             sv["acs_t"], sv["dskip_rep"], a_rep, sv["hin_all"], dy)
    grads["ssm_d_skip"] = _per_head(dds_rep)[None, :]
    grads["ssm_a_log"] = (_per_head(da_rep) * _per_head(a_rep))[None, :]
    ddt = _pad_lanes(ddt_rep[:, ::SSM_HEAD_DIM])
    ddt_raw, dbias = _ssd_prep_bwd(sv["dt_raw"], sv["dt_bias"], ddt)
    grads["ssm_dt_bias"] = dbias[:, :SSM_HEADS]
    du, dcw, dcb = _hosted(plan, "conv_bwd", _conv_bwd, sv["xbc_pre"], prm["ssm_conv_w"], prm["ssm_conv_b"], dxbc)
    grads["ssm_conv_w"] = dcw
    grads["ssm_conv_b"] = dcb
    h = sv["h"]
    grads["ssm_w_in"] = jnp.concatenate(
        [_matmul_tn(dz, h, name="d_ssm_w_z"), _matmul_tn(du, h, name="d_ssm_w_xbc"),
         _matmul_tn(ddt_raw, h, name="d_ssm_w_dt")[:SSM_HEADS]], axis=0)
    dh = _matmul(dz, prm["ssm_w_z"], mode="nn", name="ssm_dh_z")
    dh = _matmul(du, prm["ssm_w_xbc"], mode="nn", addend=dh, name="ssm_dh_xbc")
    dh = _matmul(ddt_raw, prm["ssm_w_dt"], mode="nn", addend=dh, name="ssm_dh_dt")
    dx0, dgain = _rmsnorm_bwd(sv["x0"], prm["norm_mix"][0:1], dh, dx1, name="mix_norm_bwd_0")
    grads["norm_mix"][0] = dgain[0]
    return dx0


def _attn_mixer_fwd(x0, prm, plan):
    h = _rmsnorm_fwd(x0, prm["norm_mix"][1:2], name="mix_norm_fwd_1")
    qkv = _hosted(plan, "att_qkv", _matmul, h, prm["att_w_qkv"], mode="nt", name="att_qkv")
    gq = jnp.tile(prm["att_q_norm"], (1, ATT_HEADS))
    gk = jnp.tile(prm["att_k_norm"], (1, ATT_HEADS))
    gq2, gk2 = gq[:, :LANES], gk[:, :LANES]
    outs, lses = [], []
    for g, (window, dil) in enumerate(DIL_PATTERNS):
        o_g, l_g = _attn_fwd(qkv, gq2, gk2, g, dil)
        outs.append(o_g)
        lses.append(l_g)
    o_b, o_f, l_rep = _attn_combine_fwd(outs, lses)
    x1 = _matmul(o_b, prm["att_w_o"], mode="nn", addend=x0, name="att_out")
    sv = dict(x0=x0, h=h, qkv=qkv, gq=gq, gk=gk, gq2=gq2, gk2=gk2, o_b=o_b, o_f=o_f, l_rep=l_rep)
    return x1, sv


def _attn_mixer_bwd(dx1, prm, sv, grads):
    grads["att_w_o"] = _matmul_tn(sv["o_b"], dx1, name="d_att_w_o")
    do = _matmul(dx1, prm["att_w_o"], mode="nt", name="att_out_dx")
    dl_rep = _attn_combine_bwd(do, sv["o_f"])
    blocks = [None] * N_QKV_BLOCKS
    for g, (window, dil) in enumerate(DIL_PATTERNS):
        blocks[3 * g] = _attn_bwd_dq(sv["qkv"], sv["gq2"], sv["gk2"], do, sv["l_rep"], dl_rep, g, dil)
        dk, dv = _attn_bwd_dkv(sv["qkv"], sv["gq2"], sv["gk2"], do, _rows_by_residue(sv["l_rep"], dil),
                               _rows_by_residue(dl_rep, dil), g, dil)
        blocks[3 * g + 1] = dk
        blocks[3 * g + 2] = dv
    dqkv, dgq, dgk = _qk_norm_bwd(sv["qkv"], sv["gq"], sv["gk"], blocks)
    grads["att_q_norm"] = dgq.reshape(ATT_HEADS, ATT_HEAD_DIM).sum(axis=0)[None, :]
    grads["att_k_norm"] = dgk.reshape(ATT_HEADS, ATT_HEAD_DIM).sum(axis=0)[None, :]
    grads["att_w_qkv"] = _matmul_tn(dqkv, sv["h"], name="d_att_w_qkv")
    dh = _matmul(dqkv, prm["att_w_qkv"], mode="nn", name="att_qkv_dx")
    dx0, dgain = _rmsnorm_bwd(sv["x0"], prm["norm_mix"][1:2], dh, dx1, name="mix_norm_bwd_1")
    grads["norm_mix"][1] = dgain[0]
    return dx0


def _local_step(x, p, target, prm, plan=None):
    plan = plan or _NoOverlap()
    grads = {k: [None, None] for k in ("norm_mix", "norm_ffn", "ffn_w_gate", "ffn_w_up", "ffn_w_down",
                                       "ple_w_proj", "ple_w_gate")}
    plan.begin_backward(grads)
    x1, sv_m = _mamba_fwd(x, prm, plan)
    x3, sv_f0 = _ffn_ple_fwd(x1, p[0], prm, 0, plan)
    x4, sv_a = _attn_mixer_fwd(x3, prm, plan)
    x6, sv_f1 = _ffn_ple_fwd(x4, p[1], prm, 1, plan)
    dy, loss_row = _loss_head(x6, target)
    dx4 = _ffn_ple_bwd(dy, p[1], prm, 1, sv_f1, grads, plan)
    dx3 = _attn_mixer_bwd(dx4, prm, sv_a, grads)
    dx1 = _ffn_ple_bwd(dx3, p[0], prm, 0, sv_f0, grads, plan)
    dx0 = _mamba_bwd(dx1, prm, sv_m, grads, plan)
    return loss_row, dx0, grads


W_IN_SLAB_ROWS = 1312


def _position():
    return lax.axis_index("x"), lax.axis_index("y"), lax.axis_index("c")


def _other_chips(x, y):
    return [(1 - x, y), (x, 1 - y), (1 - x, 1 - y)]


def _remote(send_sems, recv_sems, k, src, dst, to):
    return pltpu.make_async_remote_copy(src_ref=src, dst_ref=dst, send_sem=send_sems.at[k], recv_sem=recv_sems.at[k],
                                        device_id=to, device_id_type=MESH)


def _gather_side(entries, whole=()):
    n, nw = len(entries), len(whole)

    def first_hop(ins, outs, send_sems, recv_sems):
        x, y, c = _position()
        cps = []
        for j, chip in enumerate(_other_chips(x, y)):
            for e in range(n):
                cps.append(_remote(send_sems, recv_sems, 6 * e + j, ins[e].at[c], outs[e].at[2 * x + y, c], (*chip, c)))
            for e in range(nw):
                cps.append(_remote(send_sems, recv_sems, 6 * n + 3 * e + j, ins[n + e], outs[n + e].at[2 * x + y],
                                   (*chip, c)))
        return cps

    def start(ins, outs, send_sems, recv_sems):
        for cp in first_hop(ins, outs, send_sems, recv_sems):
            cp.start()

    def finish(ins, outs, send_sems, recv_sems):
        x, y, c = _position()
        me, sibling = (x, y, c), (x, y, 1 - c)
        chips = _other_chips(x, y)
        passed_on = []
        for j, (px, py) in enumerate(chips):
            for e in range(n):
                landed = outs[e].at[2 * px + py, c]
                _remote(send_sems, recv_sems, 6 * e + j, landed, landed, me).wait_recv()
                passed_on.append(_remote(send_sems, recv_sems, 6 * e + 3 + j, landed, landed, sibling))
                passed_on[-1].start()
            for e in range(nw):
                landed = outs[n + e].at[2 * px + py]
                _remote(send_sems, recv_sems, 6 * n + 3 * e + j, landed, landed, me).wait_recv()
        for j, (px, py) in enumerate(chips):
            for e in range(n):
                passed = outs[e].at[2 * px + py, 1 - c]
                _remote(send_sems, recv_sems, 6 * e + 3 + j, passed, passed, me).wait_recv()
        for cp in first_hop(ins, outs, send_sems, recv_sems) + passed_on:
            cp.wait_send()

    shapes = [jax.ShapeDtypeStruct((N_CHIPS,) + a.shape, a.dtype) for a in list(entries) + list(whole)]
    return _Side(list(entries) + list(whole), shapes, 6 * n + 3 * nw, start, finish)


def _run_side(side, name):
    si, so = len(side.inputs), len(side.out_shapes)

    def body(*refs):
        ins, outs, send_sems, recv_sems = refs[:si], refs[si:si + so], refs[-2], refs[-1]
        side.start(ins, outs, send_sems, recv_sems)
        side.finish(ins, outs, send_sems, recv_sems)

    side.outputs = list(pl.pallas_call(
        body, name=name, in_specs=[ANY] * si, out_specs=[ANY] * so, out_shape=side.out_shapes,
        scratch_shapes=[pltpu.SemaphoreType.DMA((side.n_sems,)), pltpu.SemaphoreType.DMA((side.n_sems,))],
    )(*side.inputs))
    return side.outputs


def _swap_side(grads):
    n = len(grads)

    def copies(ins, outs, send_sems, recv_sems):
        x, y, c = _position()
        return [_remote(send_sems, recv_sems, e, ins[e].at[:, 1 - c], outs[e], (x, y, 1 - c)) for e in range(n)]

    def start(ins, outs, send_sems, recv_sems):
        for cp in copies(ins, outs, send_sems, recv_sems):
            cp.start()

    def finish(ins, outs, send_sems, recv_sems):
        for cp in copies(ins, outs, send_sems, recv_sems):
            cp.wait()

    shapes = [jax.ShapeDtypeStruct((N_CHIPS,) + g.shape[2:], g.dtype) for g in grads]
    return _Side(grads, shapes, n, start, finish)


def _chip_exchange_side(chipsums):
    n = len(chipsums)

    def copies(ins, outs, send_sems, recv_sems):
        x, y, c = _position()
        return [_remote(send_sems, recv_sems, 3 * e + j, ins[e].at[2 * tx + ty], outs[e].at[j], (tx, ty, c))
                for j, (tx, ty) in enumerate(_other_chips(x, y)) for e in range(n)]

    def start(ins, outs, send_sems, recv_sems):
        for cp in copies(ins, outs, send_sems, recv_sems):
            cp.start()

    def finish(ins, outs, send_sems, recv_sems):
        for cp in copies(ins, outs, send_sems, recv_sems):
            cp.wait()

    shapes = [jax.ShapeDtypeStruct((3,) + cs.shape[1:], cs.dtype) for cs in chipsums]
    return _Side(chipsums, shapes, 3 * n, start, finish)


def _share_halves(totals):
    n = len(totals)

    def body(*refs):
        t_refs, r_refs = refs[:n], refs[n:2 * n]
        send_sems, recv_sems = refs[2 * n], refs[2 * n + 1]
        x, y, c = _position()
        cps = [pltpu.make_async_remote_copy(src_ref=t_refs[e], dst_ref=r_refs[e], send_sem=send_sems.at[e],
                                            recv_sem=recv_sems.at[e], device_id=(x, y, 1 - c), device_id_type=MESH)
               for e in range(n)]
        for cp in cps:
            cp.start()
        for cp in cps:
            cp.wait()

    return pl.pallas_call(
        body, name="grad_share_halves", in_specs=[ANY] * n, out_specs=[ANY] * n,
        out_shape=[jax.ShapeDtypeStruct(t.shape, t.dtype) for t in totals],
        scratch_shapes=[pltpu.SemaphoreType.DMA((n,)), pltpu.SemaphoreType.DMA((n,))],
    )(*totals)


def _reduce_rows(h):
    return h if h <= 704 else h // 2


def _add_sibling(grad, recv, c_idx, *, name):
    _, _, h, cw = grad.shape
    th = _reduce_rows(h)

    def body(c_ref, g_ref, r_ref, o_ref):
        o_ref[...] = (g_ref[...] + r_ref[...]).astype(BF16)

    return pl.pallas_call(
        body, name=name,
        grid_spec=pltpu.PrefetchScalarGridSpec(
            num_scalar_prefetch=1, grid=(N_CHIPS, h // th),
            in_specs=[pl.BlockSpec((None, None, th, cw), lambda s, i, c_ref: (s, c_ref[0], i, 0)),
                      pl.BlockSpec((None, th, cw), lambda s, i, c_ref: (s, i, 0))],
            out_specs=pl.BlockSpec((None, th, cw), lambda s, i, c_ref: (s, i, 0))),
        out_shape=jax.ShapeDtypeStruct((N_CHIPS, h, cw), BF16),
        compiler_params=_params("parallel", "parallel"),
    )(c_idx, grad, recv)


def _add_chips(chipsum, recv, s_idx, *, name):
    _, h, cw = chipsum.shape
    th = _reduce_rows(h)

    def body(s_ref, own_ref, r_ref, o_ref):
        o_ref[...] = ((own_ref[...].astype(F32) + r_ref[0].astype(F32)) + r_ref[1].astype(F32)) + r_ref[2].astype(F32)

    return pl.pallas_call(
        body, name=name,
        grid_spec=pltpu.PrefetchScalarGridSpec(
            num_scalar_prefetch=1, grid=(h // th,),
            in_specs=[pl.BlockSpec((None, th, cw), lambda i, s_ref: (s_ref[0], i, 0)),
                      pl.BlockSpec((3, th, cw), lambda i, s_ref: (0, i, 0))],
            out_specs=pl.BlockSpec((th, cw), lambda i, s_ref: (i, 0))),
        out_shape=jax.ShapeDtypeStruct((h, cw), F32),
        compiler_params=_params("parallel"),
    )(s_idx, chipsum, recv)


def _adamw_math(w, g, m, v):
    m = ADAM_B1 * m + (1.0 - ADAM_B1) * g
    v = ADAM_B2 * v + (1.0 - ADAM_B2) * (g * g)
    m_hat = m / (1.0 - ADAM_B1 ** ADAM_STEP)
    v_hat = v / (1.0 - ADAM_B2 ** ADAM_STEP)
    delta = -ADAM_LR * (m_hat / (jnp.sqrt(v_hat) + ADAM_EPS) + ADAM_WD * w)
    return delta, m, v


ADAM_TILE_ELEMS = 256 * 1024


def _adamw(w, g, m, v, *, name):
    shape = w.shape
    cols = shape[-1]
    rows = w.size // cols
    tr = rows
    for cand in range(8, rows, 8):
        if rows % cand == 0 and cand * cols <= ADAM_TILE_ELEMS:
            tr = cand
    if rows * cols <= ADAM_TILE_ELEMS:
        tr = rows

    def body(w_ref, g_ref, m_ref, v_ref, d_ref, nm_ref, nv_ref):
        d, nm, nv = _adamw_math(w_ref[...], g_ref[...], m_ref[...], v_ref[...])
        d_ref[...] = d
        nm_ref[...] = nm
        nv_ref[...] = nv

    blk = pl.BlockSpec((tr, cols), lambda i: (i, 0))
    sds = jax.ShapeDtypeStruct((rows, cols), F32)
    outs = pl.pallas_call(
        body, name=name, grid=(rows // tr,), in_specs=[blk] * 4, out_specs=[blk] * 3, out_shape=[sds] * 3,
        compiler_params=_params("parallel"),
    )(*[a.reshape(rows, cols) for a in (w, g, m, v)])
    return [o.reshape(shape) for o in outs]


SMALL_LAYOUT = (("loss", 1), ("norm_mix", 16), ("norm_ffn", 16), ("ssm_conv_b", 24), ("ssm_dt_bias", 1),
                ("ssm_a_log", 1), ("ssm_d_skip", 1), ("ssm_norm_w", 16), ("att_q_norm", 1), ("att_k_norm", 1),
                ("conv_w_full", 96))
SMALL_ROWS = 176
N_DEVICES = 8


def _small_pack(values):
    parts = []
    for name, rows in SMALL_LAYOUT:
        flat = values[name].reshape(-1).astype(F32)
        parts.append(jnp.pad(flat, (0, rows * LANES - flat.shape[0])).reshape(rows, LANES))
    used = sum(r for _, r in SMALL_LAYOUT)
    parts.append(jnp.zeros((SMALL_ROWS - used, LANES), F32))
    return jnp.concatenate(parts, axis=0)


def _small_unpack(pack, shapes):
    out, off = {}, 0
    for name, rows in SMALL_LAYOUT:
        shape = shapes[name]
        n = math.prod(shape)
        out[name] = pack[off:off + rows].reshape(-1)[:n].reshape(shape)
        off += rows
    return out


def _small_allreduce_adamw(g, w, m, v):
    def body(g_ref, w_ref, m_ref, v_ref, gs_ref, d_ref, nm_ref, nv_ref, buf, send_sems, recv_sems):
        x, y, c = _position()
        pos = (x, y, c)
        me = 4 * x + 2 * y + c
        buf[me] = g_ref[...]
        peers = []
        for k in range(1, N_DEVICES):
            bits = ((k >> 2) & 1, (k >> 1) & 1, k & 1)
            peers.append(tuple(1 - p if b else p for p, b in zip(pos, bits)))
        cps = [pltpu.make_async_remote_copy(src_ref=g_ref, dst_ref=buf.at[me], send_sem=send_sems.at[k],
                                            recv_sem=recv_sems.at[k], device_id=peer, device_id_type=MESH)
               for k, peer in enumerate(peers)]
        for cp in cps:
            cp.start()
        for k, (px, py, pc) in enumerate(peers):
            pltpu.make_async_remote_copy(src_ref=g_ref, dst_ref=buf.at[4 * px + 2 * py + pc],
                                         send_sem=send_sems.at[k], recv_sem=recv_sems.at[k],
                                         device_id=(px, py, pc), device_id_type=MESH).wait_recv()
        for cp in cps:
            cp.wait_send()
        total = buf[0]
        for dev in range(1, N_DEVICES):
            total = total + buf[dev]
        gs_ref[...] = total
        d, nm, nv = _adamw_math(w_ref[...], total, m_ref[...], v_ref[...])
        d_ref[...] = d
        nm_ref[...] = nm
        nv_ref[...] = nv

    vm = pl.BlockSpec(memory_space=pltpu.VMEM)
    sds = jax.ShapeDtypeStruct((SMALL_ROWS, LANES), F32)
    return pl.pallas_call(
        body, name="small_allreduce_adamw", in_specs=[vm] * 4, out_specs=[vm] * 4, out_shape=[sds] * 4,
        scratch_shapes=[pltpu.VMEM((N_DEVICES, SMALL_ROWS, LANES), F32),
                        pltpu.SemaphoreType.DMA((N_DEVICES - 1,)), pltpu.SemaphoreType.DMA((N_DEVICES - 1,))],
    )(g, w, m, v)


SMALL = tuple(n for n, _ in SMALL_LAYOUT if n not in ("loss", "conv_w_full"))
WEIGHTS = ("norm_mix", "norm_ffn", "ssm_w_in", "ssm_conv_w", "ssm_conv_b", "ssm_dt_bias", "ssm_a_log", "ssm_d_skip",
           "ssm_norm_w", "ssm_w_out", "att_w_qkv", "att_q_norm", "att_k_norm", "att_w_o", "ffn_w_gate", "ffn_w_up",
           "ffn_w_down", "ple_w_proj", "ple_w_gate")
COLUMN_SHARDED = ("ssm_w_in", "att_w_qkv", "ffn_w_gate", "ffn_w_up", "ple_w_proj")
LAYERED = ("ffn_w_gate", "ffn_w_up", "ffn_w_down", "ple_w_proj", "ple_w_gate")
GATHER_ORDER = ("ssm_w_in", "ssm_w_out", "att_w_qkv", "att_w_o", "ffn_w_gate", "ffn_w_up", "ffn_w_down",
                "ple_w_proj", "ple_w_gate")


def _layers(n):
    return (0, 1) if n in LAYERED else (None,)


def _tag(key):
    return key[0] if key[1] is None else f"{key[0]}_{key[1]}"


QKV_PARTS = 3


def _weight_slab(w, key):
    n, i = key
    if n == "att_w_qkv":
        a = w[n][0].T
        rows = a.shape[0] // QKV_PARTS
        a = a[i * rows:(i + 1) * rows]
    else:
        a = w[n][0 if i is None else i]
        a = a.T if n in COLUMN_SHARDED else a
    if n == "ssm_w_in":
        a = jnp.pad(a, ((0, W_IN_SLAB_ROWS - a.shape[0]), (0, 0)))
    return a.reshape(2, a.shape[0] // 2, a.shape[1]).astype(BF16)


def _install(prm, key, gathered, own, s_me):
    n, i = key
    full = lax.dynamic_update_slice(gathered, own[None], (s_me, 0, 0, 0))
    full = full.reshape(N_CHIPS, 2 * full.shape[2], full.shape[3])
    if n == "att_w_qkv":
        parts = prm.setdefault("att_w_qkv_parts", {})
        parts[i] = full
        if len(parts) == QKV_PARTS:
            prm[n] = jnp.stack([parts[j] for j in range(QKV_PARTS)], axis=1).reshape(-1, D_MODEL)
        return
    if n == "ssm_w_in":
        rows = (D_INNER + CONV_DIM + SSM_HEADS) // N_CHIPS
        w_in_t = full[:, :rows].reshape(N_CHIPS * rows, D_MODEL)
        prm["ssm_w_z"] = w_in_t[:D_INNER]
        prm["ssm_w_xbc"] = w_in_t[D_INNER:D_INNER + CONV_DIM]
        prm["ssm_w_dt"] = jnp.pad(w_in_t[D_INNER + CONV_DIM:], ((0, LANES - SSM_HEADS), (0, 0)))
        return
    full = full.reshape(N_CHIPS * full.shape[1], full.shape[2])
    if i is None:
        prm[n] = full
    else:
        prm.setdefault(n, [None, None])[i] = full


def _grad_slab(grads, key):
    n, i = key
    g = grads[n] if i is None else grads[n][i]
    if n == "ssm_w_in":
        g = jnp.pad(g.reshape(N_CHIPS, g.shape[0] // N_CHIPS, D_MODEL),
                    ((0, 0), (0, W_IN_SLAB_ROWS - g.shape[0] // N_CHIPS), (0, 0)))
    rows = g.size // (N_CHIPS * g.shape[-1])
    return g.reshape(N_CHIPS, 2, rows // 2, g.shape[-1])


def _natural_shard(n, reduced, shape):
    def one(r):
        if n == "ssm_w_in":
            r = r[:shape[-1]]
        return r.T if n in COLUMN_SHARDED else r
    if n in LAYERED:
        return jnp.stack([one(r) for r in reduced]).reshape(shape)
    return one(reduced[0]).reshape(shape)


def kernel(x, p, norm_mix, norm_ffn, ssm_w_in, ssm_conv_w, ssm_conv_b, ssm_dt_bias, ssm_a_log, ssm_d_skip, ssm_norm_w, ssm_w_out, att_w_qkv, att_q_norm, att_k_norm, att_w_o, ffn_w_gate, ffn_w_up, ffn_w_down, ple_w_proj, ple_w_gate, loss_target, m_norm_mix, m_norm_ffn, m_ssm_w_in, m_ssm_conv_w, m_ssm_conv_b, m_ssm_dt_bias, m_ssm_a_log, m_ssm_d_skip, m_ssm_norm_w, m_ssm_w_out, m_att_w_qkv, m_att_q_norm, m_att_k_norm, m_att_w_o, m_ffn_w_gate, m_ffn_w_up, m_ffn_w_down, m_ple_w_proj, m_ple_w_gate, v_norm_mix, v_norm_ffn, v_ssm_w_in, v_ssm_conv_w, v_ssm_conv_b, v_ssm_dt_bias, v_ssm_a_log, v_ssm_d_skip, v_ssm_norm_w, v_ssm_w_out, v_att_w_qkv, v_att_q_norm, v_att_k_norm, v_att_w_o, v_ffn_w_gate, v_ffn_w_up, v_ffn_w_down, v_ple_w_proj, v_ple_w_gate):
    given = dict(locals())
    w = {n: given[n] for n in WEIGHTS}
    m = {n: given["m_" + n] for n in WEIGHTS}
    v = {n: given["v_" + n] for n in WEIGHTS}
    c_idx = lax.axis_index("c").astype(jnp.int32).reshape(1)
    s_idx = (2 * lax.axis_index("x") + lax.axis_index("y")).astype(jnp.int32).reshape(1)

    s_me = 2 * lax.axis_index("x") + lax.axis_index("y")
    first_core = lax.axis_index("c") == 0

    qkv_parts = [("att_w_qkv", j) for j in range(QKV_PARTS)]
    gather_plan = {
        "ssm_in_xbc": [("ffn_w_gate", 0)],
        "conv_fwd": [("ffn_w_up", 0)],
        "ssd_fwd": [("ffn_w_down", 0), ("ple_w_proj", 0), ("ple_w_gate", 0), ("att_w_o", None)],
        "swiglu_fwd_0": qkv_parts[:2],
        "ffn_down_0": qkv_parts[2:],
        "att_qkv": [(n, 1) for n in LAYERED],
    }
    mamba = [("ssm_w_in", None), ("ssm_w_out", None)]
    own = {k: _weight_slab(w, k) for k in mamba + sum(gather_plan.values(), [])}
    prm = {n: w[n] for n in SMALL}

    def land(group, outputs):
        for k, g in zip(group, outputs):
            _install(prm, k, g, own[k], s_me)

    first = _gather_side([own[k] for k in mamba], whole=[ssm_conv_w[0]])
    _run_side(first, "gather_mamba")
    land(mamba, first.outputs)
    conv = lax.dynamic_update_slice(first.outputs[-1], ssm_conv_w, (s_me, 0, 0))
    prm["ssm_conv_w"] = conv.transpose(1, 0, 2).reshape(CONV_WIDTH, CONV_DIM)

    layer1 = [("att_w_qkv", None), ("att_w_o", None)] + [(n, 1) for n in LAYERED]
    ffn0 = [(n, 0) for n in LAYERED]
    reduce_plan = {"swiglu_bwd_0": ("swap", layer1), "ssd_bwd": ("exchange", layer1),
                   "gate_norm_bwd": ("swap", ffn0), "conv_bwd": ("exchange", ffn0)}
    state = {}

    def swap_side(group):
        state[_tag(group[0]), "g4"] = g4 = [_grad_slab(state["grads"], k) for k in group]
        return _swap_side(g4)

    def add_siblings(group, from_sibling):
        state[_tag(group[0]), "chipsums"] = [
            _add_sibling(g, r, c_idx, name="add_sibling_" + _tag(k))
            for g, r, k in zip(state[_tag(group[0]), "g4"], from_sibling, group)]

    def exchange_side(group):
        return _chip_exchange_side(state[_tag(group[0]), "chipsums"])

    def add_chips(group, from_chips):
        for k, cs, r in zip(group, state[_tag(group[0]), "chipsums"], from_chips):
            state["total", k] = _add_chips(cs, r, s_idx, name="add_chips_" + _tag(k))

    class Plan(_NoOverlap):
        def __init__(self):
            self.carried = {host: _gather_side([own[k] for k in group]) for host, group in gather_plan.items()}

        def begin_backward(self, grads):
            state["grads"] = grads

        def side(self, host):
            if host in reduce_plan:
                step, group = reduce_plan[host]
                self.carried[host] = swap_side(group) if step == "swap" else exchange_side(group)
            return self.carried.get(host)

        def after(self, host):
            if host in gather_plan:
                land(gather_plan[host], self.carried[host].outputs)
            elif host in reduce_plan:
                step, group = reduce_plan[host]
                (add_siblings if step == "swap" else add_chips)(group, self.carried[host].outputs)

    loss_row, dx, grads = _local_step(x[0], p[:, 0], loss_target[0], prm, Plan())

    add_siblings(mamba, _run_side(swap_side(mamba), "grad_swap_mamba"))
    add_chips(mamba, _run_side(exchange_side(mamba), "grad_exchange_mamba"))
    order = mamba + ffn0 + layer1
    shared = _share_halves([state["total", k] for k in order])
    reduced = {}
    for k, theirs in zip(order, shared):
        lo = jnp.where(first_core, state["total", k], theirs)
        hi = jnp.where(first_core, theirs, state["total", k])
        reduced.setdefault(k[0], {})[k[1]] = jnp.concatenate([lo, hi], axis=0)
    reduced = {n: [by_layer[i] for i in _layers(n)] for n, by_layer in reduced.items()}

    grad, delta, new_m, new_v = {}, {}, {}, {}
    for n in GATHER_ORDER:
        grad[n] = _natural_shard(n, reduced[n], w[n].shape)
        delta[n], new_m[n], new_v[n] = _adamw(w[n], grad[n], m[n], v[n], name="adamw_" + n)

    small_g = {n: (jnp.stack(grads[n]) if isinstance(grads[n], list) else grads[n]) for n in SMALL}
    small_g["loss"] = loss_row
    small_g["conv_w_full"] = grads["ssm_conv_w"]
    zero = {"loss": jnp.zeros((1, LANES), F32), "conv_w_full": jnp.zeros((CONV_WIDTH, CONV_DIM), F32)}
    outs = _small_allreduce_adamw(_small_pack(small_g), _small_pack({**w, **zero}), _small_pack({**m, **zero}),
                                  _small_pack({**v, **zero}))
    shapes = {n: w[n].shape for n in SMALL}
    shapes["loss"] = (1, LANES)
    shapes["conv_w_full"] = (CONV_WIDTH, CONV_DIM)
    sg, sd, sm, sv = [_small_unpack(o, shapes) for o in outs]
    for n in SMALL:
        grad[n], delta[n], new_m[n], new_v[n] = sg[n], sd[n], sm[n], sv[n]
    loss = sg["loss"][0, 0]
    conv_cols = CONV_DIM // N_CHIPS
    grad["ssm_conv_w"] = lax.dynamic_slice(sg["conv_w_full"], (0, s_me * conv_cols), (CONV_WIDTH, conv_cols))[None]
    delta["ssm_conv_w"], new_m["ssm_conv_w"], new_v["ssm_conv_w"] = _adamw(
        ssm_conv_w, grad["ssm_conv_w"], m_ssm_conv_w, v_ssm_conv_w, name="adamw_ssm_conv_w")

    return (loss, dx[None], *[grad[n] for n in WEIGHTS], *[delta[n] for n in WEIGHTS],
            *[new_m[n] for n in WEIGHTS], *[new_v[n] for n in WEIGHTS])
```

```python
import functools
import math

import jax
import jax.numpy as jnp
from jax import lax
from jax.experimental import pallas as pl
from jax.experimental.pallas import tpu as pltpu

F32 = jnp.float32
BF16 = jnp.bfloat16
HIGHEST = lax.Precision.HIGHEST

NORM_EPS = 1e-6
ADAM_LR, ADAM_B1, ADAM_B2, ADAM_EPS, ADAM_WD, ADAM_STEP = 0.001, 0.9, 0.999, 1e-08, 0.01, 10

D_MODEL = 1024
D_INNER = 2048
SSM_HEADS = 32
SSM_HEAD_DIM = 64
SSM_GROUPS = 4
SSM_STATE = 128
SSD_CHUNK = 128
CONV_DIM = 3072
CONV_WIDTH = 4
ATT_HEADS = 16
ATT_HEAD_DIM = 64
DIL_PATTERNS = ((128, 1), (512, 4), (2048, 16))
ATT_BLOCK = 128
FFN_HIDDEN = 2816
PLE_DIM = 256

LANES = 128
V7X_VMEM_LIMIT = 56 * 1024 * 1024
NEG_BIG = -1e30

N_CHIPS = 4


def _params(*sem):
    return pltpu.CompilerParams(dimension_semantics=sem, vmem_limit_bytes=V7X_VMEM_LIMIT)


def _tile(n, pref):
    if n <= pref:
        return n
    best = None
    for t in range(LANES, pref + 1, LANES):
        if n % t == 0:
            best = t
    assert best is not None, (n, pref)
    return best


def _sigmoid(v):
    return 1.0 / (1.0 + jnp.exp(-v))


def _dot(a, b):
    return jnp.dot(a, b, preferred_element_type=F32)


def _dot_nt(a, b):
    return lax.dot_general(a, b, (((1,), (1,)), ((), ())), preferred_element_type=F32)


def _dot_tn(a, b):
    return lax.dot_general(a, b, (((0,), (0,)), ((), ())), preferred_element_type=F32)


def _head_block_diag():
    i = lax.broadcasted_iota(jnp.int32, (LANES, LANES), 0) // ATT_HEAD_DIM
    j = lax.broadcasted_iota(jnp.int32, (LANES, LANES), 1) // ATT_HEAD_DIM
    return (i == j).astype(BF16)


def _split_dot(ones, z):
    hi = z.astype(BF16)
    lo = (z - hi.astype(F32)).astype(BF16)
    return _dot(ones, hi) + _dot(ones, lo)


def _head_sums(z, bd):
    hi = z.astype(BF16)
    lo = (z - hi.astype(F32)).astype(BF16)
    parts = []
    for t in range(z.shape[1] // LANES):
        sl = slice(t * LANES, (t + 1) * LANES)
        parts.append(_dot(hi[:, sl], bd) + _dot(lo[:, sl], bd))
    return parts[0] if len(parts) == 1 else jnp.concatenate(parts, axis=1)


def _lane_lt64(rows):
    return lax.broadcasted_iota(jnp.int32, (rows, LANES), 1) < ATT_HEAD_DIM


MESH = pl.DeviceIdType.MESH
ANY = pl.BlockSpec(memory_space=pl.ANY)


class _Side:
    def __init__(self, inputs, out_shapes, n_sems, start, finish):
        self.inputs, self.out_shapes, self.n_sems = list(inputs), list(out_shapes), n_sems
        self.start, self.finish = start, finish
        self.outputs = None


def _call(body, side, *, name, grid, in_specs, out_specs, out_shape, scratch_shapes, semantics, args):
    in_specs, out_specs, out_shape = list(in_specs), list(out_specs), list(out_shape)
    scratch_shapes = list(scratch_shapes)
    if side is None:
        return pl.pallas_call(body, name=name, grid=grid, in_specs=in_specs, out_specs=out_specs,
                              out_shape=out_shape, scratch_shapes=scratch_shapes,
                              compiler_params=_params(*semantics))(*args)
    ni, no, ns = len(in_specs), len(out_specs), len(scratch_shapes)
    si, so = len(side.inputs), len(side.out_shapes)

    def hosted(*refs):
        ins, s_ins = refs[:ni], refs[ni:ni + si]
        outs, s_outs = refs[ni + si:ni + si + no], refs[ni + si + no:ni + si + no + so]
        scratch = refs[ni + si + no + so:ni + si + no + so + ns]
        send_sems, recv_sems = refs[-2], refs[-1]
        first = pl.program_id(0) == 0
        last = pl.program_id(0) == grid[0] - 1
        for axis in range(1, len(grid)):
            first = jnp.logical_and(first, pl.program_id(axis) == 0)
            last = jnp.logical_and(last, pl.program_id(axis) == grid[axis] - 1)

        @pl.when(first)
        def _():
            side.start(s_ins, s_outs, send_sems, recv_sems)

        body(*ins, *outs, *scratch)

        @pl.when(last)
        def _():
            side.finish(s_ins, s_outs, send_sems, recv_sems)

    res = pl.pallas_call(
        hosted, name=name, grid=grid, in_specs=in_specs + [ANY] * si, out_specs=out_specs + [ANY] * so,
        out_shape=out_shape + side.out_shapes,
        scratch_shapes=scratch_shapes + [pltpu.SemaphoreType.DMA((side.n_sems,)),
                                         pltpu.SemaphoreType.DMA((side.n_sems,))],
        compiler_params=_params(*["arbitrary"] * len(grid)),
    )(*args, *side.inputs)
    side.outputs = list(res[no:])
    return list(res[:no])


def _matmul(a, b, *, mode, name, out_dtype=F32, addend=None, tm=1024, tn=512, tk_max=3072, side=None):
    m, k = a.shape
    if mode == "nn":
        k2, n = b.shape
    else:
        n, k2 = b.shape
    assert k == k2, (a.shape, b.shape, mode)
    tm, tn, tk = _tile(m, tm), _tile(n, tn), _tile(k, tk_max)
    nk = k // tk
    has_add = addend is not None

    def body(*refs):
        a_ref, b_ref = refs[0], refs[1]
        add_ref = refs[2] if has_add else None
        o_ref, acc_ref = refs[-2], refs[-1]
        kk = pl.program_id(2)
        av = a_ref[...].astype(BF16)
        bv = b_ref[...].astype(BF16)
        part = _dot(av, bv) if mode == "nn" else _dot_nt(av, bv)

        @pl.when(kk == 0)
        def _():
            acc_ref[...] = part

        @pl.when(kk > 0)
        def _():
            acc_ref[...] += part

        @pl.when(kk == nk - 1)
        def _():
            res = acc_ref[...]
            if has_add:
                res = res + add_ref[...]
            o_ref[...] = res.astype(out_dtype)

    a_spec = pl.BlockSpec((tm, tk), lambda i, j, kk: (i, kk))
    if mode == "nn":
        b_spec = pl.BlockSpec((tk, tn), lambda i, j, kk: (kk, j))
    else:
        b_spec = pl.BlockSpec((tn, tk), lambda i, j, kk: (j, kk))
    in_specs = [a_spec, b_spec]
    args = [a, b]
    if has_add:
        in_specs.append(pl.BlockSpec((tm, tn), lambda i, j, kk: (i, j)))
        args.append(addend)
    return _call(
        body, side, name=name, grid=(m // tm, n // tn, nk),
        in_specs=in_specs, out_specs=[pl.BlockSpec((tm, tn), lambda i, j, kk: (i, j))],
        out_shape=[jax.ShapeDtypeStruct((m, n), out_dtype)],
        scratch_shapes=[pltpu.VMEM((tm, tn), F32)],
        semantics=("parallel", "parallel", "arbitrary"), args=args,
    )[0]


def _matmul_tn(a, b, *, name, tm=1408, tn=512, tk=1024):
    t, m = a.shape
    t2, n = b.shape
    assert t == t2
    tm, tn, tk = _tile(m, tm), _tile(n, tn), _tile(t, tk)

    def body(a_ref, b_ref, o_ref):
        part = _dot_tn(a_ref[...].astype(BF16), b_ref[...].astype(BF16))

        @pl.when(pl.program_id(2) == 0)
        def _():
            o_ref[...] = part

        @pl.when(pl.program_id(2) > 0)
        def _():
            o_ref[...] += part

    return pl.pallas_call(
        body, name=name, grid=(m // tm, n // tn, t // tk),
        in_specs=[pl.BlockSpec((tk, tm), lambda i, j, kk: (kk, i)),
                  pl.BlockSpec((tk, tn), lambda i, j, kk: (kk, j))],
        out_specs=pl.BlockSpec((tm, tn), lambda i, j, kk: (i, j)),
        out_shape=jax.ShapeDtypeStruct((m, n), F32),
        compiler_params=_params("parallel", "parallel", "arbitrary"),
    )(a, b)


def _rmsnorm_fwd(x, gain, *, name):
    t, d = x.shape
    tm = _tile(t, 512)

    def body(x_ref, g_ref, o_ref):
        xv = x_ref[...]
        r = lax.rsqrt(jnp.mean(xv * xv, axis=-1, keepdims=True) + NORM_EPS)
        o_ref[...] = (xv * r * g_ref[...]).astype(BF16)

    return pl.pallas_call(
        body, name=name, grid=(t // tm,),
        in_specs=[pl.BlockSpec((tm, d), lambda i: (i, 0)), pl.BlockSpec((1, d), lambda i: (0, 0))],
        out_specs=pl.BlockSpec((tm, d), lambda i: (i, 0)),
        out_shape=jax.ShapeDtypeStruct((t, d), BF16),
        compiler_params=_params("parallel"),
    )(x, gain)


def _rmsnorm_bwd(x, gain, dy, dres, *, name):
    t, d = x.shape
    tm = _tile(t, 512)

    def body(x_ref, g_ref, dy_ref, dres_ref, dx_ref, dg_ref):
        xv = x_ref[...]
        r = lax.rsqrt(jnp.mean(xv * xv, axis=-1, keepdims=True) + NORM_EPS)
        xh = xv * r
        dyv = dy_ref[...]
        dxh = dyv * g_ref[...]
        mean = jnp.mean(dxh * xh, axis=-1, keepdims=True)
        dx_ref[...] = dres_ref[...] + r * (dxh - xh * mean)
        part = jnp.sum(dyv * xh, axis=0, keepdims=True)

        @pl.when(pl.program_id(0) == 0)
        def _():
            dg_ref[...] = part

        @pl.when(pl.program_id(0) > 0)
        def _():
            dg_ref[...] += part

    row = pl.BlockSpec((tm, d), lambda i: (i, 0))
    vec = pl.BlockSpec((1, d), lambda i: (0, 0))
    return pl.pallas_call(
        body, name=name, grid=(t // tm,),
        in_specs=[row, vec, row, row], out_specs=[row, vec],
        out_shape=[jax.ShapeDtypeStruct((t, d), F32), jax.ShapeDtypeStruct((1, d), F32)],
        compiler_params=_params("arbitrary"),
    )(x, gain, dy, dres)


def _loss_head(y, target):
    t, d = y.shape
    tm = _tile(t, 512)
    steps = t // tm

    def body(y_ref, t_ref, dy_ref, l_ref, acc_ref):
        e = y_ref[...] - t_ref[...]
        dy_ref[...] = e * (1.0 / d)
        part = jnp.sum(e * e, axis=0, keepdims=True)

        @pl.when(pl.program_id(0) == 0)
        def _():
            acc_ref[...] = part

        @pl.when(pl.program_id(0) > 0)
        def _():
            acc_ref[...] += part

        @pl.when(pl.program_id(0) == steps - 1)
        def _():
            l_ref[...] = jnp.full((1, LANES), (0.5 / d), F32) * jnp.sum(acc_ref[...])

    row = pl.BlockSpec((tm, d), lambda i: (i, 0))
    return pl.pallas_call(
        body, name="loss_head", grid=(steps,),
        in_specs=[row, row], out_specs=[row, pl.BlockSpec((1, LANES), lambda i: (0, 0))],
        out_shape=[jax.ShapeDtypeStruct((t, d), F32), jax.ShapeDtypeStruct((1, LANES), F32)],
        scratch_shapes=[pltpu.VMEM((1, d), F32)],
        compiler_params=_params("arbitrary"),
    )(y, target)


def _swiglu_fwd(h, w_gate_t, w_up_t, *, name, side=None):
    t, d = h.shape
    f = w_gate_t.shape[0]
    tm, tn = _tile(t, 1024), _tile(f, 256)

    def body(h_ref, wg_ref, wu_ref, g_ref, u_ref, a_ref):
        hv = h_ref[...]
        g = _dot_nt(hv, wg_ref[...])
        u = _dot_nt(hv, wu_ref[...])
        g_ref[...] = g.astype(BF16)
        u_ref[...] = u.astype(BF16)
        a_ref[...] = (g * _sigmoid(g) * u).astype(BF16)

    wspec = pl.BlockSpec((tn, d), lambda i, j: (j, 0))
    ospec = pl.BlockSpec((tm, tn), lambda i, j: (i, j))
    return _call(
        body, side, name=name, grid=(t // tm, f // tn),
        in_specs=[pl.BlockSpec((tm, d), lambda i, j: (i, 0)), wspec, wspec],
        out_specs=[ospec, ospec, ospec],
        out_shape=[jax.ShapeDtypeStruct((t, f), BF16), jax.ShapeDtypeStruct((t, f), BF16),
                   jax.ShapeDtypeStruct((t, f), BF16)],
        scratch_shapes=[], semantics=("parallel", "parallel"), args=(h, w_gate_t, w_up_t),
    )


def _swiglu_bwd(dx, w_down, g, u, *, name, side=None):
    t, d = dx.shape
    f = w_down.shape[0]
    tm, tn = _tile(t, 1024), _tile(f, 256)

    def body(dx_ref, wd_ref, g_ref, u_ref, dg_ref, du_ref):
        dact = _dot_nt(dx_ref[...].astype(BF16), wd_ref[...])
        gv, uv = g_ref[...].astype(F32), u_ref[...].astype(F32)
        sg = _sigmoid(gv)
        dg_ref[...] = (dact * uv * sg * (1.0 + gv * (1.0 - sg))).astype(BF16)
        du_ref[...] = (dact * gv * sg).astype(BF16)

    ospec = pl.BlockSpec((tm, tn), lambda i, j: (i, j))
    return _call(
        body, side, name=name, grid=(t // tm, f // tn),
        in_specs=[pl.BlockSpec((tm, d), lambda i, j: (i, 0)), pl.BlockSpec((tn, d), lambda i, j: (j, 0)),
                  ospec, ospec],
        out_specs=[ospec, ospec],
        out_shape=[jax.ShapeDtypeStruct((t, f), BF16), jax.ShapeDtypeStruct((t, f), BF16)],
        scratch_shapes=[], semantics=("parallel", "parallel"), args=(dx, w_down, g, u),
    )


def _ple_fwd(x, p, w_gate, w_proj_t, *, name):
    t, d = x.shape
    e = p.shape[1]
    tm, tn = _tile(t, 1024), _tile(d, 512)

    def body(xf_ref, xr_ref, p_ref, wg_ref, wp_ref, o_ref):
        s = _dot(xf_ref[...].astype(BF16), wg_ref[...])
        ple = _dot_nt(p_ref[...].astype(BF16), wp_ref[...])
        o_ref[...] = xr_ref[...] + _sigmoid(s) * ple

    return pl.pallas_call(
        body, name=name, grid=(t // tm, d // tn),
        in_specs=[pl.BlockSpec((tm, d), lambda i, j: (i, 0)), pl.BlockSpec((tm, tn), lambda i, j: (i, j)),
                  pl.BlockSpec((tm, e), lambda i, j: (i, 0)), pl.BlockSpec((d, tn), lambda i, j: (0, j)),
                  pl.BlockSpec((tn, e), lambda i, j: (j, 0))],
        out_specs=pl.BlockSpec((tm, tn), lambda i, j: (i, j)),
        out_shape=jax.ShapeDtypeStruct((t, d), F32),
        compiler_params=_params("parallel", "parallel"),
    )(x, x, p, w_gate, w_proj_t)


def _ple_bwd(x, p, w_gate, w_proj_t, dout, *, name):
    t, d = x.shape
    e = p.shape[1]
    tm, tn = _tile(t, 1024), _tile(d, 512)

    def body(xf_ref, p_ref, wg_ref, wp_ref, do_ref, ds_ref, dple_ref):
        s = _dot(xf_ref[...].astype(BF16), wg_ref[...])
        ple = _dot_nt(p_ref[...].astype(BF16), wp_ref[...])
        gate = _sigmoid(s)
        dov = do_ref[...]
        dple_ref[...] = (dov * gate).astype(BF16)
        ds_ref[...] = (dov * ple * gate * (1.0 - gate)).astype(BF16)

    ospec = pl.BlockSpec((tm, tn), lambda i, j: (i, j))
    return pl.pallas_call(
        body, name=name, grid=(t // tm, d // tn),
        in_specs=[pl.BlockSpec((tm, d), lambda i, j: (i, 0)), pl.BlockSpec((tm, e), lambda i, j: (i, 0)),
                  pl.BlockSpec((d, tn), lambda i, j: (0, j)), pl.BlockSpec((tn, e), lambda i, j: (j, 0)), ospec],
        out_specs=[ospec, ospec],
        out_shape=[jax.ShapeDtypeStruct((t, d), BF16), jax.ShapeDtypeStruct((t, d), BF16)],
        compiler_params=_params("parallel", "parallel"),
    )(x, p, w_gate, w_proj_t, dout)


CONV_TIME_TILE = 256
CONV_HALO = 8


def _conv_taps(ext, w):
    acc = ext[CONV_HALO:, :] * w[CONV_WIDTH - 1:CONV_WIDTH, :]
    shifted = [ext[CONV_HALO:, :]]
    for j in range(1, CONV_WIDTH):
        sh = pltpu.roll(ext, j, 0)[CONV_HALO:, :]
        shifted.append(sh)
        acc = acc + sh * w[CONV_WIDTH - 1 - j:CONV_WIDTH - j, :]
    return acc, shifted


def _conv_fwd(u, w, b, side=None):
    t, c = u.shape
    tc = _tile(c, 256)
    tt = CONV_TIME_TILE

    def body(u_ref, w_ref, b_ref, o_ref):
        wv, bv = w_ref[...], b_ref[...]

        def tile(start, ext):
            pre = _conv_taps(ext, wv)[0] + bv
            o_ref[pl.ds(start, tt), :] = pre * _sigmoid(pre)

        tile(0, jnp.concatenate([jnp.zeros((CONV_HALO, tc), F32), u_ref[0:tt, :]], axis=0))

        def loop(i, carry):
            start = pl.multiple_of(i * tt, tt)
            tile(start, u_ref[pl.ds(start - CONV_HALO, tt + CONV_HALO), :])
            return carry

        lax.fori_loop(1, t // tt, loop, 0)

    col = pl.BlockSpec((t, tc), lambda j: (0, j))
    return _call(
        body, side, name="conv_fwd", grid=(c // tc,),
        in_specs=[col, pl.BlockSpec((CONV_WIDTH, tc), lambda j: (0, j)), pl.BlockSpec((1, tc), lambda j: (0, j))],
        out_specs=[col], out_shape=[jax.ShapeDtypeStruct((t, c), F32)],
        scratch_shapes=[], semantics=("parallel",), args=(u, w, b),
    )[0]


def _conv_bwd(u, w, b, dact, side=None):
    t, c = u.shape
    tc = _tile(c, 256)
    tt = CONV_TIME_TILE

    def body(u_ref, w_ref, b_ref, da_ref, du_ref, dw_ref, db_ref, dpre_ref):
        wv, bv = w_ref[...], b_ref[...]

        def tile(start, ext, sums):
            acc, shifted = _conv_taps(ext, wv)
            pre = acc + bv
            sg = _sigmoid(pre)
            dpre = da_ref[pl.ds(start, tt), :] * (sg * (1.0 + pre * (1.0 - sg)))
            dpre_ref[pl.ds(start, tt), :] = dpre
            new = [sums[0] + jnp.sum(dpre, axis=0, keepdims=True)]
            for j in range(CONV_WIDTH):
                new.append(sums[1 + j] + jnp.sum(dpre * shifted[j], axis=0, keepdims=True))
            return tuple(new)

        zero = jnp.zeros((1, tc), F32)
        sums = tile(0, jnp.concatenate([jnp.zeros((CONV_HALO, tc), F32), u_ref[0:tt, :]], axis=0),
                    (zero,) * (1 + CONV_WIDTH))

        def loop(i, sums):
            start = pl.multiple_of(i * tt, tt)
            return tile(start, u_ref[pl.ds(start - CONV_HALO, tt + CONV_HALO), :], sums)

        sums = lax.fori_loop(1, t // tt, loop, sums)
        db_ref[...] = sums[0]
        dw_ref[...] = jnp.concatenate([sums[1 + (CONV_WIDTH - 1 - k)] for k in range(CONV_WIDTH)], axis=0)
        dpre_ref[pl.ds(t, CONV_HALO), :] = jnp.zeros((CONV_HALO, tc), F32)

        def loop2(i, carry):
            start = pl.multiple_of(i * tt, tt)
            ext = dpre_ref[pl.ds(start, tt + CONV_HALO), :]
            acc = ext[0:tt, :] * wv[CONV_WIDTH - 1:CONV_WIDTH, :]
            for j in range(1, CONV_WIDTH):
                acc = acc + pltpu.roll(ext, tt + CONV_HALO - j, 0)[0:tt, :] * wv[CONV_WIDTH - 1 - j:CONV_WIDTH - j, :]
            du_ref[pl.ds(start, tt), :] = acc.astype(BF16)
            return carry

        lax.fori_loop(0, t // tt, loop2, 0)

    col = pl.BlockSpec((t, tc), lambda j: (0, j))
    return _call(
        body, side, name="conv_bwd", grid=(c // tc,),
        in_specs=[col, pl.BlockSpec((CONV_WIDTH, tc), lambda j: (0, j)), pl.BlockSpec((1, tc), lambda j: (0, j)), col],
        out_specs=[col, pl.BlockSpec((CONV_WIDTH, tc), lambda j: (0, j)), pl.BlockSpec((1, tc), lambda j: (0, j))],
        out_shape=[jax.ShapeDtypeStruct((t, c), BF16), jax.ShapeDtypeStruct((CONV_WIDTH, c), F32),
                   jax.ShapeDtypeStruct((1, c), F32)],
        scratch_shapes=[pltpu.VMEM((t + CONV_HALO, tc), F32)],
        semantics=("parallel",), args=(u, w, b, dact),
    )


def _softplus(v):
    e = jnp.exp(-jnp.abs(v))
    w = 1.0 + e
    log1p = jnp.where(w == 1.0, e, jnp.log(w) * (e / jnp.where(w == 1.0, 1.0, w - 1.0)))
    return jnp.maximum(v, 0.0) + log1p


def _ssd_prep_fwd(dt_raw, dt_bias, a_log):
    t = dt_raw.shape[0]
    cl = SSD_CHUNK

    def body(r_ref, b_ref, al_ref, dt_ref, acs_ref):
        dt = _softplus(r_ref[...] + b_ref[...])
        adt = dt * (-jnp.exp(al_ref[...]))
        li = lax.broadcasted_iota(jnp.int32, (cl, cl), 0)
        si = lax.broadcasted_iota(jnp.int32, (cl, cl), 1)
        tri = (si <= li).astype(F32)
        dt_ref[...] = dt
        acs_ref[...] = jnp.dot(tri, adt, preferred_element_type=F32, precision=HIGHEST)

    row = pl.BlockSpec((cl, LANES), lambda i: (i, 0))
    vec = pl.BlockSpec((1, LANES), lambda i: (0, 0))
    return pl.pallas_call(
        body, name="ssd_prep_fwd", grid=(t // cl,),
        in_specs=[row, vec, vec], out_specs=[row, row],
        out_shape=[jax.ShapeDtypeStruct((t, LANES), F32), jax.ShapeDtypeStruct((t, LANES), F32)],
        compiler_params=_params("parallel"),
    )(dt_raw, dt_bias, a_log)


def _ssd_prep_bwd(dt_raw, dt_bias, ddt):
    t = dt_raw.shape[0]
    tm = _tile(t, 512)

    def body(r_ref, b_ref, d_ref, o_ref, db_ref):
        g = d_ref[...] * _sigmoid(r_ref[...] + b_ref[...])
        o_ref[...] = g.astype(BF16)
        part = jnp.sum(g, axis=0, keepdims=True)

        @pl.when(pl.program_id(0) == 0)
        def _():
            db_ref[...] = part

        @pl.when(pl.program_id(0) > 0)
        def _():
            db_ref[...] += part

    row = pl.BlockSpec((tm, LANES), lambda i: (i, 0))
    vec = pl.BlockSpec((1, LANES), lambda i: (0, 0))
    return pl.pallas_call(
        body, name="ssd_prep_bwd", grid=(t // tm,),
        in_specs=[row, vec, row], out_specs=[row, vec],
        out_shape=[jax.ShapeDtypeStruct((t, LANES), BF16), jax.ShapeDtypeStruct((1, LANES), F32)],
        compiler_params=_params("arbitrary"),
    )(dt_raw, dt_bias, ddt)


GROUP_W = D_INNER // SSM_GROUPS
PAIRS_PER_GROUP = GROUP_W // LANES


def _head_cols(acs_pair, lt64):
    rolled = pltpu.roll(acs_pair, ATT_HEAD_DIM, 1)
    return jnp.where(lt64, acs_pair, rolled), jnp.where(lt64, rolled, acs_pair)


def _ssd_fwd(xbc, dt_rep, acs_rep, acs_t, dskip_rep, side=None):
    t = xbc.shape[0]
    cl = SSD_CHUNK
    nc = t // cl

    def body(xbc_ref, dt_ref, acs_ref, acst_ref, dskip_ref, y_ref, hin_ref, state_ref):
        @pl.when(pl.program_id(0) == 0)
        def _():
            state_ref[...] = jnp.zeros_like(state_ref)

        lt64 = _lane_lt64(cl)
        li = lax.broadcasted_iota(jnp.int32, (cl, cl), 0)
        si = lax.broadcasted_iota(jnp.int32, (cl, cl), 1)
        causal = li >= si
        hin_ref[...] = state_ref[...]
        for g in range(SSM_GROUPS):
            gsl = slice(g * GROUP_W, (g + 1) * GROUP_W)
            xg = xbc_ref[:, gsl]
            bg = xbc_ref[:, D_INNER + g * SSM_STATE:D_INNER + (g + 1) * SSM_STATE]
            cg = xbc_ref[:, D_INNER + SSM_GROUPS * SSM_STATE + g * SSM_STATE:
                         D_INNER + SSM_GROUPS * SSM_STATE + (g + 1) * SSM_STATE]
            acs = acs_ref[:, gsl]
            xdt = xg * dt_ref[:, gsl]
            atot = acs[cl - 1:cl, :]
            hin = state_ref[:, gsl]
            cgb = cg.astype(BF16)
            gmat = _dot_nt(cgb, bg.astype(BF16))
            yoff = _dot(cgb, hin.astype(BF16)) * jnp.exp(acs)
            snew = _dot(bg.T.astype(BF16), (xdt * jnp.exp(atot - acs)).astype(BF16))
            state_ref[:, gsl] = hin * jnp.exp(atot) + snew
            xdtb = xdt.astype(BF16)
            for pr in range(PAIRS_PER_GROUP):
                psl = slice(pr * LANES, (pr + 1) * LANES)
                cols = _head_cols(acs[:, psl], lt64)
                xp = xdtb[:, psl]
                ys = []
                for hh in range(2):
                    h = (g * PAIRS_PER_GROUP + pr) * 2 + hh
                    seg = cols[hh] - acst_ref[h:h + 1, :]
                    lm = jnp.exp(jnp.where(causal, seg, NEG_BIG))
                    ys.append(_dot((gmat * lm).astype(BF16), xp))
                ydiag = jnp.where(lt64, ys[0], ys[1])
                osl = slice(g * GROUP_W + pr * LANES, g * GROUP_W + (pr + 1) * LANES)
                y_ref[:, osl] = ydiag + yoff[:, psl] + xg[:, psl] * dskip_ref[:, osl]

    row = lambda w: pl.BlockSpec((cl, w), lambda c: (c, 0))
    return _call(
        body, side, name="ssd_fwd", grid=(nc,),
        in_specs=[row(CONV_DIM), row(D_INNER), row(D_INNER),
                  pl.BlockSpec((SSM_HEADS, cl), lambda c: (0, c)), pl.BlockSpec((1, D_INNER), lambda c: (0, 0))],
        out_specs=[row(D_INNER), pl.BlockSpec((None, SSM_STATE, D_INNER), lambda c: (c, 0, 0))],
        out_shape=[jax.ShapeDtypeStruct((t, D_INNER), F32), jax.ShapeDtypeStruct((nc, SSM_STATE, D_INNER), F32)],
        scratch_shapes=[pltpu.VMEM((SSM_STATE, D_INNER), F32)],
        semantics=("arbitrary",), args=(xbc, dt_rep, acs_rep, acs_t, dskip_rep),
    )


def _ssd_bwd(xbc, dt_rep, acs_rep, acs_t, dskip_rep, a_rep, hin_all, dy, side=None):
    t = xbc.shape[0]
    cl = SSD_CHUNK
    nc = t // cl

    def body(xbc_ref, dt_ref, acs_ref, acst_ref, dskip_ref, a_ref, hin_ref, dy_ref,
             dxbc_ref, ddt_ref, da_ref, dds_ref, dstate_ref, dacs_ref, dxs_ref):
        step = pl.program_id(0)

        @pl.when(step == 0)
        def _():
            dstate_ref[...] = jnp.zeros_like(dstate_ref)
            da_ref[...] = jnp.zeros_like(da_ref)
            dds_ref[...] = jnp.zeros_like(dds_ref)

        bd = _head_block_diag()
        lt64 = _lane_lt64(cl)
        li = lax.broadcasted_iota(jnp.int32, (cl, cl), 0)
        si = lax.broadcasted_iota(jnp.int32, (cl, cl), 1)
        lower = li >= si
        upper = si >= li
        last_row = lax.broadcasted_iota(jnp.int32, (cl, GROUP_W), 0) == cl - 1
        for g in range(SSM_GROUPS):
            gsl = slice(g * GROUP_W, (g + 1) * GROUP_W)
            bsl = slice(D_INNER + g * SSM_STATE, D_INNER + (g + 1) * SSM_STATE)
            csl = slice(D_INNER + SSM_GROUPS * SSM_STATE + g * SSM_STATE,
                        D_INNER + SSM_GROUPS * SSM_STATE + (g + 1) * SSM_STATE)
            xg = xbc_ref[:, gsl]
            bg = xbc_ref[:, bsl]
            cg = xbc_ref[:, csl]
            bgb, cgb = bg.astype(BF16), cg.astype(BF16)
            acs = acs_ref[:, gsl]
            xdt = xg * dt_ref[:, gsl]
            atot = acs[cl - 1:cl, :]
            eg = jnp.exp(acs)
            dk = jnp.exp(atot - acs)
            etot = jnp.exp(atot)
            hin = hin_ref[:, gsl]
            hinb = hin.astype(BF16)
            dh = dstate_ref[:, gsl]
            dhb = dh.astype(BF16)
            dyg = dy_ref[:, gsl]

            gmat = _dot_nt(cgb, bgb)
            gmat_t = _dot_nt(bgb, cgb)
            ch = _dot(cgb, hinb)
            dacs = _head_sums(dyg * ch * eg, bd)
            dye = (dyg * eg).astype(BF16)
            dc = _dot_nt(dye, hinb)
            dhin = _dot(cg.T.astype(BF16), dye)
            bdh = _dot(bgb, dhb)
            dxs = bdh * dk
            xdk = xdt * dk
            db = _dot_nt(xdk.astype(BF16), dhb)
            ddk = _head_sums(bdh * xdk, bd)
            dacs = dacs - ddk
            datot = jnp.sum(ddk, axis=0, keepdims=True) + etot * _head_sums(
                jnp.sum(dh * hin, axis=0, keepdims=True), bd)
            dacs = dacs + jnp.where(last_row, datot, 0.0)
            dstate_ref[:, gsl] = dh * etot + dhin

            xdtb = xdt.astype(BF16)
            dgsum = jnp.zeros((cl, cl), F32)
            dgsum_t = jnp.zeros((cl, cl), F32)
            for pr in range(PAIRS_PER_GROUP):
                psl = slice(pr * LANES, (pr + 1) * LANES)
                cols = _head_cols(acs[:, psl], lt64)
                xp = xdtb[:, psl]
                dyp = dyg[:, psl].astype(BF16)
                dx1, dac = [], []
                for hh in range(2):
                    h = (g * PAIRS_PER_GROUP + pr) * 2 + hh
                    mine = lt64 if hh == 0 else jnp.logical_not(lt64)
                    row = acst_ref[h:h + 1, :]
                    lm = jnp.exp(jnp.where(lower, cols[hh] - row, NEG_BIG))
                    lm_t = jnp.exp(jnp.where(upper, row - cols[hh], NEG_BIG))
                    dyh = jnp.where(mine, dyp, jnp.zeros_like(dyp))
                    xh = jnp.where(mine, xp, jnp.zeros_like(xp))
                    dm = _dot_nt(dyh, xp)
                    dm_t = _dot_nt(xh, dyp)
                    m_t = gmat_t * lm_t
                    dx1.append(_dot(m_t.astype(BF16), dyp))
                    w = dm * (gmat * lm)
                    w_t = dm_t * m_t
                    dac.append(jnp.sum(w, axis=1, keepdims=True) - jnp.sum(w_t, axis=1, keepdims=True))
                    dgsum = dgsum + dm * lm
                    dgsum_t = dgsum_t + dm_t * lm_t
                osl = slice(g * GROUP_W + pr * LANES, g * GROUP_W + (pr + 1) * LANES)
                dxs_ref[:, osl] = dxs[:, psl] + jnp.where(lt64, dx1[0], dx1[1])
                dacs_ref[:, osl] = dacs[:, psl] + jnp.where(lt64, jnp.broadcast_to(dac[0], (cl, LANES)),
                                                             jnp.broadcast_to(dac[1], (cl, LANES)))
            dxbc_ref[:, csl] = dc + _dot(dgsum.astype(BF16), bgb)
            dxbc_ref[:, bsl] = db + _dot(dgsum_t.astype(BF16), cgb)

        dadt = _split_dot(upper.astype(BF16), dacs_ref[...])
        xall = xbc_ref[:, 0:D_INNER]
        dtall = dt_ref[...]
        dxsall = dxs_ref[...]
        dyall = dy_ref[...]
        ddt_ref[...] = dadt * a_ref[...] + _head_sums(dxsall * xall, bd)
        dxbc_ref[:, 0:D_INNER] = dxsall * dtall + dyall * dskip_ref[...]
        da_ref[...] += jnp.sum(dadt * dtall, axis=0, keepdims=True)
        dds_ref[...] += jnp.sum(dyall * xall, axis=0, keepdims=True)

        @pl.when(step == nc - 1)
        def _():
            dds_ref[...] = _head_sums(dds_ref[...], bd)

    row = lambda w: pl.BlockSpec((cl, w), lambda c: (nc - 1 - c, 0))
    vec = pl.BlockSpec((1, D_INNER), lambda c: (0, 0))
    return _call(
        body, side, name="ssd_bwd", grid=(nc,),
        in_specs=[row(CONV_DIM), row(D_INNER), row(D_INNER),
                  pl.BlockSpec((SSM_HEADS, cl), lambda c: (0, nc - 1 - c)), vec, vec,
                  pl.BlockSpec((None, SSM_STATE, D_INNER), lambda c: (nc - 1 - c, 0, 0)), row(D_INNER)],
        out_specs=[row(CONV_DIM), row(D_INNER), vec, vec],
        out_shape=[jax.ShapeDtypeStruct((t, CONV_DIM), F32), jax.ShapeDtypeStruct((t, D_INNER), F32),
                   jax.ShapeDtypeStruct((1, D_INNER), F32), jax.ShapeDtypeStruct((1, D_INNER), F32)],
        scratch_shapes=[pltpu.VMEM((SSM_STATE, D_INNER), F32), pltpu.VMEM((cl, D_INNER), F32),
                        pltpu.VMEM((cl, D_INNER), F32)],
        semantics=("arbitrary",), args=(xbc, dt_rep, acs_rep, acs_t, dskip_rep, a_rep, hin_all, dy),
    )


def _gate_norm_fwd(y, z, w):
    t, c = y.shape
    tm = _tile(t, 256)

    def body(y_ref, z_ref, w_ref, o_ref):
        for g in range(SSM_GROUPS):
            gsl = slice(g * GROUP_W, (g + 1) * GROUP_W)
            zv = z_ref[:, gsl]
            v = y_ref[:, gsl] * (zv * _sigmoid(zv))
            r = lax.rsqrt(jnp.mean(v * v, axis=-1, keepdims=True) + NORM_EPS)
            o_ref[:, gsl] = (v * r * w_ref[:, gsl]).astype(BF16)

    row = pl.BlockSpec((tm, c), lambda i: (i, 0))
    return pl.pallas_call(
        body, name="gate_norm_fwd", grid=(t // tm,),
        in_specs=[row, row, pl.BlockSpec((1, c), lambda i: (0, 0))], out_specs=row,
        out_shape=jax.ShapeDtypeStruct((t, c), BF16),
        compiler_params=_params("parallel"),
    )(y, z, w)


def _gate_norm_bwd(y, z, w, dout, side=None):
    t, c = y.shape
    tm = _tile(t, 256)

    def body(y_ref, z_ref, w_ref, do_ref, dy_ref, dz_ref, dw_ref):
        @pl.when(pl.program_id(0) == 0)
        def _():
            dw_ref[...] = jnp.zeros_like(dw_ref)

        for g in range(SSM_GROUPS):
            gsl = slice(g * GROUP_W, (g + 1) * GROUP_W)
            zv, yv, dov = z_ref[:, gsl], y_ref[:, gsl], do_ref[:, gsl]
            sg = _sigmoid(zv)
            sz = zv * sg
            v = yv * sz
            r = lax.rsqrt(jnp.mean(v * v, axis=-1, keepdims=True) + NORM_EPS)
            vh = v * r
            dvh = dov * w_ref[:, gsl]
            mean = jnp.mean(dvh * vh, axis=-1, keepdims=True)
            dv = r * (dvh - vh * mean)
            dy_ref[:, gsl] = dv * sz
            dz_ref[:, gsl] = (dv * yv * (sg * (1.0 + zv * (1.0 - sg)))).astype(BF16)
            dw_ref[:, gsl] += jnp.sum(dov * vh, axis=0, keepdims=True)

    row = pl.BlockSpec((tm, c), lambda i: (i, 0))
    vec = pl.BlockSpec((1, c), lambda i: (0, 0))
    return _call(
        body, side, name="gate_norm_bwd", grid=(t // tm,),
        in_specs=[row, row, vec, row], out_specs=[row, row, vec],
        out_shape=[jax.ShapeDtypeStruct((t, c), F32), jax.ShapeDtypeStruct((t, c), BF16),
                   jax.ShapeDtypeStruct((1, c), F32)],
        scratch_shapes=[], semantics=("arbitrary",), args=(y, z, w, dout),
    )


ATT_W = ATT_HEADS * ATT_HEAD_DIM
N_QKV_BLOCKS = 9
ATT_SCALE = 1.0 / math.sqrt(ATT_HEAD_DIM)


def _head_rmsnorm(x, gain, bd):
    ms = _head_sums(x * x, bd) * (1.0 / ATT_HEAD_DIM)
    return x * lax.rsqrt(ms + NORM_EPS) * gain


def _class_rows(ref, blk, r, dil):
    span = ATT_BLOCK * dil
    sub = ref.at[pl.ds(pl.multiple_of(blk * span, span), span), :]
    return sub[...] if dil == 1 else sub[pl.ds(r, ATT_BLOCK, stride=dil), :]


def _store_class_rows(ref, blk, r, dil, val):
    span = ATT_BLOCK * dil
    sub = ref.at[pl.ds(pl.multiple_of(blk * span, span), span), :]
    if dil == 1:
        sub[...] = val
    else:
        sub[pl.ds(r, ATT_BLOCK, stride=dil), :] = val


def _qk_norm_bwd(qkv, gq, gk, grads):
    t = qkv.shape[0]
    tm = _tile(t, 256)

    def body(x_ref, gq_ref, gk_ref, *rest):
        g_refs = rest[:N_QKV_BLOCKS]
        o_ref, dgq_ref, dgk_ref = rest[N_QKV_BLOCKS:]
        cb = pl.program_id(1)

        @pl.when(jnp.logical_and(pl.program_id(0) == 0, cb == 0))
        def _():
            dgq_ref[...] = jnp.zeros_like(dgq_ref)
            dgk_ref[...] = jnp.zeros_like(dgk_ref)

        def norm_bwd(dy, gain, dg_ref):
            bd = _head_block_diag()
            xv = x_ref[...]
            ms = _head_sums(xv * xv, bd) * (1.0 / ATT_HEAD_DIM)
            r = lax.rsqrt(ms + NORM_EPS)
            xh = xv * r
            dxh = dy * gain
            mean = _head_sums(dxh * xh, bd) * (1.0 / ATT_HEAD_DIM)
            o_ref[...] = (r * (dxh - xh * mean)).astype(BF16)
            dg_ref[...] += jnp.sum(dy * xh, axis=0, keepdims=True)

        for k in range(N_QKV_BLOCKS):
            @pl.when(cb == k)
            def _(k=k):
                if k % 3 == 0:
                    norm_bwd(g_refs[k][...], gq_ref[...], dgq_ref)
                elif k % 3 == 1:
                    norm_bwd(g_refs[k][...], gk_ref[...], dgk_ref)
                else:
                    o_ref[...] = g_refs[k][...].astype(BF16)

    blk = pl.BlockSpec((tm, ATT_W), lambda i, j: (i, j))
    one = pl.BlockSpec((tm, ATT_W), lambda i, j: (i, 0))
    vec = pl.BlockSpec((1, ATT_W), lambda i, j: (0, 0))
    return pl.pallas_call(
        body, name="qk_norm_bwd", grid=(t // tm, N_QKV_BLOCKS),
        in_specs=[blk, vec, vec] + [one] * N_QKV_BLOCKS, out_specs=[blk, vec, vec],
        out_shape=[jax.ShapeDtypeStruct(qkv.shape, BF16), jax.ShapeDtypeStruct((1, ATT_W), F32),
                   jax.ShapeDtypeStruct((1, ATT_W), F32)],
        compiler_params=_params("arbitrary", "arbitrary"),
    )(qkv, gq, gk, *grads)


PAIRS = ATT_HEADS // 2


def _pair_col(g, j):
    return lambda pair: (0, (g * 3 + j) * PAIRS + pair)


def _pair_slopes(pair):
    steps = jnp.full((1, 2 * ATT_BLOCK), 2 * pair + 1, jnp.int32).astype(F32)
    first = jnp.exp(steps * (-0.5 * math.log(2.0)))
    return first, first * (2.0 ** -0.5)


NORM_ROWS = 512


ROW_SLICES = 4
SLICE_ROWS = 2 * ATT_BLOCK // ROW_SLICES


def _fill_band_bias(bias_ref, pair, dil, transposed):
    bq = ATT_BLOCK
    a = lax.broadcasted_iota(jnp.int32, (2 * bq, 2 * bq), 0) % bq
    b = lax.broadcasted_iota(jnp.int32, (2 * bq, 2 * bq), 1)
    dist = (b - a) if transposed else (a + bq - b)
    in_band = (dist >= 0) & (dist <= bq)
    s0, s1 = _pair_slopes(pair)
    first_head = lax.broadcasted_iota(jnp.int32, (2 * bq, 2 * bq), 0) < bq
    bias = jnp.where(first_head, s0, s1) * (dist.astype(F32) * float(dil))
    inside = (b < bq) if transposed else (b >= bq)
    bias_ref[1] = jnp.where(in_band, bias, -NEG_BIG)
    bias_ref[0] = jnp.where(in_band & inside, bias, -NEG_BIG)


def _row_slices():
    return [slice(i * SLICE_ROWS, (i + 1) * SLICE_ROWS) for i in range(ROW_SLICES)]


def _stack_heads(tile):
    rows = lax.broadcasted_iota(jnp.int32, (2 * ATT_BLOCK, LANES), 0) < ATT_BLOCK
    lanes = lax.broadcasted_iota(jnp.int32, (2 * ATT_BLOCK, LANES), 1) < ATT_HEAD_DIM
    both = jnp.concatenate([tile, tile], axis=0)
    return jnp.where(rows == lanes, both, jnp.zeros_like(both))


def _unstack_heads(stacked, lt64):
    return jnp.where(lt64, stacked[:ATT_BLOCK], stacked[ATT_BLOCK:])


ITEMS_PER_PASS = 4


def _item_loop(nb, dil, work):
    if dil == 1:
        def trip(i, carry):
            work([(i * ITEMS_PER_PASS + b, 0) for b in range(ITEMS_PER_PASS)])
            return carry

        lax.fori_loop(0, nb // ITEMS_PER_PASS, trip, 0)
    else:
        def trip(n, carry):
            for r0 in range(0, dil, ITEMS_PER_PASS):
                pl.when(n >= 0)(functools.partial(work, [(n, r0 + j) for j in range(ITEMS_PER_PASS)]))
            return carry

        lax.fori_loop(0, nb, trip, 0)


def _normalise_qk(q_ref, k_ref, gq_ref, gk_ref, qn_ref, kn_ref):
    bd = _head_block_diag()
    gq_scaled = gq_ref[...] * ATT_SCALE

    def step(i, carry):
        rows = pl.ds(pl.multiple_of(i * NORM_ROWS, NORM_ROWS), NORM_ROWS)
        qn_ref[rows, :] = _head_rmsnorm(q_ref[rows, :], gq_scaled, bd)
        kn_ref[rows, :] = _head_rmsnorm(k_ref[rows, :], gk_ref[...], bd)
        return carry

    lax.fori_loop(0, q_ref.shape[0] // NORM_ROWS, step, 0)


def _attn_fwd(qkv, gq, gk, g, dil):
    t = qkv.shape[0]
    nb = t // dil // ATT_BLOCK
    bq = ATT_BLOCK

    def body(q_ref, k_ref, v_ref, gq_ref, gk_ref, o_ref, l_ref, qn_ref, kn_ref, bias_ref):
        _normalise_qk(q_ref, k_ref, gq_ref, gk_ref, qn_ref, kn_ref)
        _fill_band_bias(bias_ref, pl.program_id(0), dil, False)
        lt64 = _lane_lt64(bq)

        def work(items):
            scores, values, probs = [], [], []
            for n, r in items:
                prev = jnp.maximum(n - 1, 0)
                q2 = _stack_heads(_class_rows(qn_ref, n, r, dil).astype(BF16))
                kcat = jnp.concatenate([_class_rows(kn_ref, prev, r, dil), _class_rows(kn_ref, n, r, dil)],
                                       axis=0).astype(BF16)
                values.append(jnp.concatenate([_class_rows(v_ref, prev, r, dil), _class_rows(v_ref, n, r, dil)],
                                              axis=0).astype(BF16))
                scores.append(_dot_nt(q2, kcat))
            for (n, r), sc in zip(items, scores):
                bias = bias_ref.at[jnp.minimum(n, 1)]
                ps, inv, lses = [], [], []
                for rows in _row_slices():
                    s = sc[rows] - bias[rows, :]
                    m = jnp.max(s, axis=1, keepdims=True)
                    p = jnp.exp(s - m)
                    l = jnp.sum(p, axis=1, keepdims=True)
                    ps.append(p.astype(BF16))
                    inv.append(jnp.broadcast_to(1.0 / l, (SLICE_ROWS, LANES)))
                    lses.append(jnp.broadcast_to(m + jnp.log(l), (SLICE_ROWS, LANES)))
                probs.append((jnp.concatenate(ps, axis=0), jnp.concatenate(inv, axis=0)))
                _store_class_rows(l_ref, n, r, dil, _unstack_heads(jnp.concatenate(lses, axis=0), lt64))
            for (n, r), (p, inv), vcat in zip(items, probs, values):
                _store_class_rows(o_ref, n, r, dil, _unstack_heads(_dot(p, vcat) * inv, lt64))

        _item_loop(nb, dil, work)

    col = lambda j: pl.BlockSpec((t, LANES), _pair_col(g, j))
    vec = pl.BlockSpec((1, LANES), lambda pair: (0, 0))
    out = pl.BlockSpec((t, LANES), lambda pair: (0, pair))
    return pl.pallas_call(
        body, name=f"attn_fwd_g{g}", grid=(PAIRS,),
        in_specs=[col(0), col(1), col(2), vec, vec], out_specs=[out, out],
        out_shape=[jax.ShapeDtypeStruct((t, ATT_W), F32), jax.ShapeDtypeStruct((t, ATT_W), F32)],
        scratch_shapes=[pltpu.VMEM((t, LANES), F32), pltpu.VMEM((t, LANES), F32),
                        pltpu.VMEM((2, 2 * bq, 2 * bq), F32)],
        compiler_params=_params("parallel"),
    )(qkv, qkv, qkv, gq, gk)


def _attn_combine_fwd(outs, lses):
    t = outs[0].shape[0]
    tm = _tile(t, 256)

    def body(o0, o1, o2, l0, l1, l2, ob_ref, of_ref, lt_ref):
        a, b, c = l0[...], l1[...], l2[...]
        m = jnp.maximum(jnp.maximum(a, b), c)
        ea, eb, ec = jnp.exp(a - m), jnp.exp(b - m), jnp.exp(c - m)
        ssum = ea + eb + ec
        o = (ea * o0[...] + eb * o1[...] + ec * o2[...]) / ssum
        ob_ref[...] = o.astype(BF16)
        of_ref[...] = o
        lt_ref[...] = m + jnp.log(ssum)

    row = pl.BlockSpec((tm, ATT_W), lambda i: (i, 0))
    return pl.pallas_call(
        body, name="attn_combine_fwd", grid=(t // tm,),
        in_specs=[row] * 6, out_specs=[row] * 3,
        out_shape=[jax.ShapeDtypeStruct((t, ATT_W), BF16), jax.ShapeDtypeStruct((t, ATT_W), F32),
                   jax.ShapeDtypeStruct((t, ATT_W), F32)],
        compiler_params=_params("parallel"),
    )(*outs, *lses)


def _attn_combine_bwd(do, o):
    t = do.shape[0]
    tm = _tile(t, 256)

    def body(do_ref, o_ref, dl_ref):
        dl_ref[...] = _head_sums(do_ref[...] * o_ref[...], _head_block_diag())

    row = pl.BlockSpec((tm, ATT_W), lambda i: (i, 0))
    return pl.pallas_call(
        body, name="attn_combine_bwd", grid=(t // tm,),
        in_specs=[row, row], out_specs=row, out_shape=jax.ShapeDtypeStruct((t, ATT_W), F32),
        compiler_params=_params("parallel"),
    )(do, o)


def _attn_bwd_dq(qkv, gq, gk, do, l_rep, dl_rep, g, dil):
    t = qkv.shape[0]
    nb = t // dil // ATT_BLOCK
    bq = ATT_BLOCK

    def body(q_ref, k_ref, v_ref, gq_ref, gk_ref, do_ref, l_ref, dl_ref, dq_ref, qn_ref, kn_ref, bias_ref):
        _normalise_qk(q_ref, k_ref, gq_ref, gk_ref, qn_ref, kn_ref)
        _fill_band_bias(bias_ref, pl.program_id(0), dil, False)
        lt64 = _lane_lt64(bq)

        def per_row(tile):
            cols = _head_cols(tile, lt64)
            half = jnp.concatenate([cols[0], cols[1]], axis=0)
            return jnp.concatenate([half, half], axis=1)

        def work(items):
            products, keys, dscores = [], [], []
            for n, r in items:
                prev = jnp.maximum(n - 1, 0)
                q2 = _stack_heads(_class_rows(qn_ref, n, r, dil).astype(BF16))
                do2 = _stack_heads(_class_rows(do_ref, n, r, dil).astype(BF16))
                kcat = jnp.concatenate([_class_rows(kn_ref, prev, r, dil), _class_rows(kn_ref, n, r, dil)],
                                       axis=0).astype(BF16)
                vcat = jnp.concatenate([_class_rows(v_ref, prev, r, dil), _class_rows(v_ref, n, r, dil)],
                                       axis=0).astype(BF16)
                keys.append(kcat)
                products.append((_dot_nt(q2, kcat), _dot_nt(do2, vcat)))
            for (n, r), (scores, dps) in zip(items, products):
                bias = bias_ref.at[jnp.minimum(n, 1)]
                lse = per_row(_class_rows(l_ref, n, r, dil))
                dl = per_row(_class_rows(dl_ref, n, r, dil))
                dss = []
                for rows in _row_slices():
                    p = jnp.exp(scores[rows] - bias[rows, :] - lse[rows])
                    dss.append((p * (dps[rows] - dl[rows])).astype(BF16))
                dscores.append(jnp.concatenate(dss, axis=0))
            for (n, r), ds, kcat in zip(items, dscores, keys):
                _store_class_rows(dq_ref, n, r, dil, _unstack_heads(_dot(ds, kcat) * ATT_SCALE, lt64))

        _item_loop(nb, dil, work)

    col = lambda j: pl.BlockSpec((t, LANES), _pair_col(g, j))
    vec = pl.BlockSpec((1, LANES), lambda pair: (0, 0))
    tok = pl.BlockSpec((t, LANES), lambda pair: (0, pair))
    return pl.pallas_call(
        body, name=f"attn_bwd_dq_g{g}", grid=(PAIRS,),
        in_specs=[col(0), col(1), col(2), vec, vec, tok, tok, tok], out_specs=tok,
        out_shape=jax.ShapeDtypeStruct((t, ATT_W), F32),
        scratch_shapes=[pltpu.VMEM((t, LANES), F32), pltpu.VMEM((t, LANES), F32),
                        pltpu.VMEM((2, 2 * bq, 2 * bq), F32)],
        compiler_params=_params("parallel"),
    )(qkv, qkv, qkv, gq, gk, do, l_rep, dl_rep)


def _attn_bwd_dkv(qkv, gq, gk, do, l_row, dl_row, g, dil):
    t = qkv.shape[0]
    nb = t // dil // ATT_BLOCK
    bq = ATT_BLOCK

    def body(q_ref, k_ref, v_ref, gq_ref, gk_ref, do_ref, l_ref, dl_ref, dk_ref, dv_ref, qn_ref, kn_ref, bias_ref):
        _normalise_qk(q_ref, k_ref, gq_ref, gk_ref, qn_ref, kn_ref)
        _fill_band_bias(bias_ref, pl.program_id(0), dil, True)
        lt64 = _lane_lt64(bq)

        def per_query(ref, hh, lane_c, lane_n):
            return jnp.concatenate([ref[hh:hh + 1, pl.ds(lane_c, bq)], ref[hh:hh + 1, pl.ds(lane_n, bq)]], axis=1)

        def work(items):
            products, operands, weights = [], [], []
            for n, r in items:
                nxt = jnp.minimum(n + 1, nb - 1)
                k2 = _stack_heads(_class_rows(kn_ref, n, r, dil).astype(BF16))
                v2 = _stack_heads(_class_rows(v_ref, n, r, dil).astype(BF16))
                qcat = jnp.concatenate([_class_rows(qn_ref, n, r, dil), _class_rows(qn_ref, nxt, r, dil)],
                                       axis=0).astype(BF16)
                docat = jnp.concatenate([_class_rows(do_ref, n, r, dil), _class_rows(do_ref, nxt, r, dil)],
                                        axis=0).astype(BF16)
                operands.append((qcat, docat))
                products.append((_dot_nt(k2, qcat), _dot_nt(v2, docat)))
            for (n, r), (scores, dps) in zip(items, products):
                nxt = jnp.minimum(n + 1, nb - 1)
                bias = bias_ref.at[jnp.where(n == nb - 1, 0, 1)]
                lane_c = pl.multiple_of((r * nb + n) * bq, bq)
                lane_n = pl.multiple_of((r * nb + nxt) * bq, bq)
                lse = [per_query(l_ref, hh, lane_c, lane_n) for hh in range(2)]
                dl = [per_query(dl_ref, hh, lane_c, lane_n) for hh in range(2)]
                pts, dss = [], []
                for i, rows in enumerate(_row_slices()):
                    hh = i * SLICE_ROWS // bq
                    p_t = jnp.exp(scores[rows] - bias[rows, :] - lse[hh])
                    pts.append(p_t.astype(BF16))
                    dss.append((p_t * (dps[rows] - dl[hh])).astype(BF16))
                weights.append((jnp.concatenate(pts, axis=0), jnp.concatenate(dss, axis=0)))
            for (n, r), (p_t, ds_t), (qcat, docat) in zip(items, weights, operands):
                _store_class_rows(dv_ref, n, r, dil, _unstack_heads(_dot(p_t, docat), lt64))
                _store_class_rows(dk_ref, n, r, dil, _unstack_heads(_dot(ds_t, qcat), lt64))

        _item_loop(nb, dil, work)

    col = lambda j: pl.BlockSpec((t, LANES), _pair_col(g, j))
    vec = pl.BlockSpec((1, LANES), lambda pair: (0, 0))
    tok = pl.BlockSpec((t, LANES), lambda pair: (0, pair))
    rows = pl.BlockSpec((None, 8, t), lambda pair: (pair, 0, 0))
    return pl.pallas_call(
        body, name=f"attn_bwd_dkv_g{g}", grid=(PAIRS,),
        in_specs=[col(0), col(1), col(2), vec, vec, tok, rows, rows], out_specs=[tok, tok],
        out_shape=[jax.ShapeDtypeStruct((t, ATT_W), F32), jax.ShapeDtypeStruct((t, ATT_W), F32)],
        scratch_shapes=[pltpu.VMEM((t, LANES), F32), pltpu.VMEM((t, LANES), F32),
                        pltpu.VMEM((2, 2 * bq, 2 * bq), F32)],
        compiler_params=_params("parallel"),
    )(qkv, qkv, qkv, gq, gk, do, l_row, dl_row)


def _rows_by_residue(rep, dil):
    t = rep.shape[0]
    per_head = rep[:, ::ATT_HEAD_DIM]
    rows = per_head.reshape(t // dil, dil, ATT_HEADS).transpose(2, 1, 0).reshape(PAIRS, 2, t)
    return jnp.pad(rows, ((0, 0), (0, 6), (0, 0)))


def _per_head(rep_row):
    return rep_row[0, ::SSM_HEAD_DIM]


def _rep_heads(v):
    return jnp.repeat(v, SSM_HEAD_DIM)[None, :]


def _pad_lanes(v):
    return jnp.pad(v, ((0, 0), (0, LANES - v.shape[1])))


class _NoOverlap:
    def side(self, host):
        return None

    def after(self, host):
        pass

    def begin_backward(self, grads):
        pass


def _hosted(plan, host, fn, *args, **kwargs):
    out = fn(*args, side=plan.side(host), **kwargs)
    plan.after(host)
    return out


def _ffn_ple_fwd(x1, p_i, prm, i, plan):
    h = _rmsnorm_fwd(x1, prm["norm_ffn"][i:i + 1], name=f"ffn_norm_fwd_{i}")
    g, u, act = _hosted(plan, f"swiglu_fwd_{i}", _swiglu_fwd, h, prm["ffn_w_gate"][i], prm["ffn_w_up"][i],
                        name=f"swiglu_fwd_{i}")
    x2 = _hosted(plan, f"ffn_down_{i}", _matmul, act, prm["ffn_w_down"][i], mode="nn", addend=x1,
                 name=f"ffn_down_{i}")
    x3 = _ple_fwd(x2, p_i, prm["ple_w_gate"][i], prm["ple_w_proj"][i], name=f"ple_fwd_{i}")
    return x3, dict(x1=x1, h=h, g=g, u=u, act=act, x2=x2)


def _ffn_ple_bwd(dx3, p_i, prm, i, sv, grads, plan):
    ds, dple = _ple_bwd(sv["x2"], p_i, prm["ple_w_gate"][i], prm["ple_w_proj"][i], dx3, name=f"ple_bwd_{i}")
    grads["ple_w_gate"][i] = _matmul_tn(sv["x2"], ds, name=f"d_ple_w_gate_{i}")
    grads["ple_w_proj"][i] = _matmul_tn(dple, p_i, name=f"d_ple_w_proj_{i}")
    dx2 = _matmul(ds, prm["ple_w_gate"][i], mode="nt", addend=dx3, name=f"ple_dx_{i}")
    grads["ffn_w_down"][i] = _matmul_tn(sv["act"], dx2, name=f"d_ffn_w_down_{i}")
    dg, du = _hosted(plan, f"swiglu_bwd_{i}", _swiglu_bwd, dx2, prm["ffn_w_down"][i], sv["g"], sv["u"],
                     name=f"swiglu_bwd_{i}")
    grads["ffn_w_gate"][i] = _matmul_tn(dg, sv["h"], name=f"d_ffn_w_gate_{i}")
    grads["ffn_w_up"][i] = _matmul_tn(du, sv["h"], name=f"d_ffn_w_up_{i}")
    dh = _matmul(dg, prm["ffn_w_gate"][i], mode="nn", name=f"ffn_dh_gate_{i}")
    dh = _matmul(du, prm["ffn_w_up"][i], mode="nn", addend=dh, name=f"ffn_dh_up_{i}")
    dx1, dgain = _rmsnorm_bwd(sv["x1"], prm["norm_ffn"][i:i + 1], dh, dx2, name=f"ffn_norm_bwd_{i}")
    grads["norm_ffn"][i] = dgain[0]
    return dx1


def _mamba_fwd(x0, prm, plan):
    h = _rmsnorm_fwd(x0, prm["norm_mix"][0:1], name="mix_norm_fwd_0")
    z = _matmul(h, prm["ssm_w_z"], mode="nt", name="ssm_in_z")
    xbc_pre = _hosted(plan, "ssm_in_xbc", _matmul, h, prm["ssm_w_xbc"], mode="nt", name="ssm_in_xbc")
    dt_raw = _matmul(h, prm["ssm_w_dt"], mode="nt", name="ssm_in_dt")
    xbc = _hosted(plan, "conv_fwd", _conv_fwd, xbc_pre, prm["ssm_conv_w"], prm["ssm_conv_b"])
    dt_bias = _pad_lanes(prm["ssm_dt_bias"])
    a_log = _pad_lanes(prm["ssm_a_log"])
    dt, acs = _ssd_prep_fwd(dt_raw, dt_bias, a_log)
    dt_rep = jnp.repeat(dt[:, :SSM_HEADS], SSM_HEAD_DIM, axis=1)
    acs_rep = jnp.repeat(acs[:, :SSM_HEADS], SSM_HEAD_DIM, axis=1)
    acs_t = acs[:, :SSM_HEADS].T
    dskip_rep = _rep_heads(prm["ssm_d_skip"][0])
    y, hin_all = _hosted(plan, "ssd_fwd", _ssd_fwd, xbc, dt_rep, acs_rep, acs_t, dskip_rep)
    yn = _gate_norm_fwd(y, z, prm["ssm_norm_w"])
    x1 = _matmul(yn, prm["ssm_w_out"], mode="nn", addend=x0, name="ssm_out")
    sv = dict(x0=x0, h=h, z=z, xbc_pre=xbc_pre, dt_raw=dt_raw, xbc=xbc, dt_bias=dt_bias, dt_rep=dt_rep,
              acs_rep=acs_rep, acs_t=acs_t, dskip_rep=dskip_rep, y=y, hin_all=hin_all, yn=yn)
    return x1, sv


def _mamba_bwd(dx1, prm, sv, grads, plan):
    grads["ssm_w_out"] = _matmul_tn(sv["yn"], dx1, name="d_ssm_w_out")
    dyn = _matmul(dx1, prm["ssm_w_out"], mode="nt", name="ssm_out_dx")
    dy, dz, dnw = _hosted(plan, "gate_norm_bwd", _gate_norm_bwd, sv["y"], sv["z"], prm["ssm_norm_w"], dyn)
    grads["ssm_norm_w"] = dnw
    a_rep = _rep_heads(-jnp.exp(prm["ssm_a_log"][0]))
    dxbc, ddt_rep, da_rep, dds_rep = _hosted(plan, "ssd_bwd", _ssd_bwd, sv["xbc"], sv["dt_rep"], sv["acs_rep"],
                                             sv["acs_t"], sv["dskip_rep"], a_rep, sv["hin_all"], dy)
    grads["ssm_d_skip"] = _per_head(dds_rep)[None, :]
    grads["ssm_a_log"] = (_per_head(da_rep) * _per_head(a_rep))[None, :]
    ddt = _pad_lanes(ddt_rep[:, ::SSM_HEAD_DIM])
    ddt_raw, dbias = _ssd_prep_bwd(sv["dt_raw"], sv["dt_bias"], ddt)
    grads["ssm_dt_bias"] = dbias[:, :SSM_HEADS]
    du, dcw, dcb = _hosted(plan, "conv_bwd", _conv_bwd, sv["xbc_pre"], prm["ssm_conv_w"], prm["ssm_conv_b"], dxbc)
    grads["ssm_conv_w"] = dcw
    grads["ssm_conv_b"] = dcb
    h = sv["h"]
    grads["ssm_w_in"] = jnp.concatenate(
        [_matmul_tn(dz, h, name="d_ssm_w_z"), _matmul_tn(du, h, name="d_ssm_w_xbc"),
         _matmul_tn(ddt_raw, h, name="d_ssm_w_dt")[:SSM_HEADS]], axis=0)
    dh = _matmul(dz, prm["ssm_w_z"], mode="nn", name="ssm_dh_z")
    dh = _matmul(du, prm["ssm_w_xbc"], mode="nn", addend=dh, name="ssm_dh_xbc")
    dh = _matmul(ddt_raw, prm["ssm_w_dt"], mode="nn", addend=dh, name="ssm_dh_dt")
    dx0, dgain = _rmsnorm_bwd(sv["x0"], prm["norm_mix"][0:1], dh, dx1, name="mix_norm_bwd_0")
    grads["norm_mix"][0] = dgain[0]
    return dx0


def _attn_mixer_fwd(x0, prm, plan):
    h = _rmsnorm_fwd(x0, prm["norm_mix"][1:2], name="mix_norm_fwd_1")
    qkv = _hosted(plan, "att_qkv", _matmul, h, prm["att_w_qkv"], mode="nt", name="att_qkv")
    gq = jnp.tile(prm["att_q_norm"], (1, ATT_HEADS))
    gk = jnp.tile(prm["att_k_norm"], (1, ATT_HEADS))
    gq2, gk2 = gq[:, :LANES], gk[:, :LANES]
    outs, lses = [], []
    for g, (window, dil) in enumerate(DIL_PATTERNS):
        o_g, l_g = _attn_fwd(qkv, gq2, gk2, g, dil)
        outs.append(o_g)
        lses.append(l_g)
    o_b, o_f, l_rep = _attn_combine_fwd(outs, lses)
    x1 = _matmul(o_b, prm["att_w_o"], mode="nn", addend=x0, name="att_out")
    sv = dict(x0=x0, h=h, qkv=qkv, gq=gq, gk=gk, gq2=gq2, gk2=gk2, o_b=o_b, o_f=o_f, l_rep=l_rep)
    return x1, sv


def _attn_mixer_bwd(dx1, prm, sv, grads):
    grads["att_w_o"] = _matmul_tn(sv["o_b"], dx1, name="d_att_w_o")
    do = _matmul(dx1, prm["att_w_o"], mode="nt", name="att_out_dx")
    dl_rep = _attn_combine_bwd(do, sv["o_f"])
    blocks = [None] * N_QKV_BLOCKS
    for g, (window, dil) in enumerate(DIL_PATTERNS):
        blocks[3 * g] = _attn_bwd_dq(sv["qkv"], sv["gq2"], sv["gk2"], do, sv["l_rep"], dl_rep, g, dil)
        dk, dv = _attn_bwd_dkv(sv["qkv"], sv["gq2"], sv["gk2"], do, _rows_by_residue(sv["l_rep"], dil),
                               _rows_by_residue(dl_rep, dil), g, dil)
        blocks[3 * g + 1] = dk
        blocks[3 * g + 2] = dv
    dqkv, dgq, dgk = _qk_norm_bwd(sv["qkv"], sv["gq"], sv["gk"], blocks)
    grads["att_q_norm"] = dgq.reshape(ATT_HEADS, ATT_HEAD_DIM).sum(axis=0)[None, :]
    grads["att_k_norm"] = dgk.reshape(ATT_HEADS, ATT_HEAD_DIM).sum(axis=0)[None, :]
    grads["att_w_qkv"] = _matmul_tn(dqkv, sv["h"], name="d_att_w_qkv")
    dh = _matmul(dqkv, prm["att_w_qkv"], mode="nn", name="att_qkv_dx")
    dx0, dgain = _rmsnorm_bwd(sv["x0"], prm["norm_mix"][1:2], dh, dx1, name="mix_norm_bwd_1")
    grads["norm_mix"][1] = dgain[0]
    return dx0


def _local_step(x, p, target, prm, plan=None):
    plan = plan or _NoOverlap()
    grads = {k: [None, None] for k in ("norm_mix", "norm_ffn", "ffn_w_gate", "ffn_w_up", "ffn_w_down",
                                       "ple_w_proj", "ple_w_gate")}
    plan.begin_backward(grads)
    x1, sv_m = _mamba_fwd(x, prm, plan)
    x3, sv_f0 = _ffn_ple_fwd(x1, p[0], prm, 0, plan)
    x4, sv_a = _attn_mixer_fwd(x3, prm, plan)
    x6, sv_f1 = _ffn_ple_fwd(x4, p[1], prm, 1, plan)
    dy, loss_row = _loss_head(x6, target)
    dx4 = _ffn_ple_bwd(dy, p[1], prm, 1, sv_f1, grads, plan)
    dx3 = _attn_mixer_bwd(dx4, prm, sv_a, grads)
    dx1 = _ffn_ple_bwd(dx3, p[0], prm, 0, sv_f0, grads, plan)
    dx0 = _mamba_bwd(dx1, prm, sv_m, grads, plan)
    return loss_row, dx0, grads


W_IN_SLAB_ROWS = 1312


def _position():
    return lax.axis_index("x"), lax.axis_index("y"), lax.axis_index("c")


def _other_chips(x, y):
    return [(1 - x, y), (x, 1 - y), (1 - x, 1 - y)]


def _remote(send_sems, recv_sems, k, src, dst, to):
    return pltpu.make_async_remote_copy(src_ref=src, dst_ref=dst, send_sem=send_sems.at[k], recv_sem=recv_sems.at[k],
                                        device_id=to, device_id_type=MESH)


def _gather_side(entries, whole=()):
    n, nw = len(entries), len(whole)

    def first_hop(ins, outs, send_sems, recv_sems):
        x, y, c = _position()
        cps = []
        for j, chip in enumerate(_other_chips(x, y)):
            for e in range(n):
                cps.append(_remote(send_sems, recv_sems, 6 * e + j, ins[e].at[c], outs[e].at[2 * x + y, c], (*chip, c)))
            for e in range(nw):
                cps.append(_remote(send_sems, recv_sems, 6 * n + 3 * e + j, ins[n + e], outs[n + e].at[2 * x + y],
                                   (*chip, c)))
        return cps

    def start(ins, outs, send_sems, recv_sems):
        for cp in first_hop(ins, outs, send_sems, recv_sems):
            cp.start()

    def finish(ins, outs, send_sems, recv_sems):
        x, y, c = _position()
        me, sibling = (x, y, c), (x, y, 1 - c)
        chips = _other_chips(x, y)
        passed_on = []
        for j, (px, py) in enumerate(chips):
            for e in range(n):
                landed = outs[e].at[2 * px + py, c]
                _remote(send_sems, recv_sems, 6 * e + j, landed, landed, me).wait_recv()
                passed_on.append(_remote(send_sems, recv_sems, 6 * e + 3 + j, landed, landed, sibling))
                passed_on[-1].start()
            for e in range(nw):
                landed = outs[n + e].at[2 * px + py]
                _remote(send_sems, recv_sems, 6 * n + 3 * e + j, landed, landed, me).wait_recv()
        for j, (px, py) in enumerate(chips):
            for e in range(n):
                passed = outs[e].at[2 * px + py, 1 - c]
                _remote(send_sems, recv_sems, 6 * e + 3 + j, passed, passed, me).wait_recv()
        for cp in first_hop(ins, outs, send_sems, recv_sems) + passed_on:
            cp.wait_send()

    shapes = [jax.ShapeDtypeStruct((N_CHIPS,) + a.shape, a.dtype) for a in list(entries) + list(whole)]
    return _Side(list(entries) + list(whole), shapes, 6 * n + 3 * nw, start, finish)


def _run_side(side, name):
    si, so = len(side.inputs), len(side.out_shapes)

    def body(*refs):
        ins, outs, send_sems, recv_sems = refs[:si], refs[si:si + so], refs[-2], refs[-1]
        side.start(ins, outs, send_sems, recv_sems)
        side.finish(ins, outs, send_sems, recv_sems)

    side.outputs = list(pl.pallas_call(
        body, name=name, in_specs=[ANY] * si, out_specs=[ANY] * so, out_shape=side.out_shapes,
        scratch_shapes=[pltpu.SemaphoreType.DMA((side.n_sems,)), pltpu.SemaphoreType.DMA((side.n_sems,))],
    )(*side.inputs))
    return side.outputs


def _swap_side(grads):
    n = len(grads)

    def copies(ins, outs, send_sems, recv_sems):
        x, y, c = _position()
        return [_remote(send_sems, recv_sems, e, ins[e].at[:, 1 - c], outs[e], (x, y, 1 - c)) for e in range(n)]

    def start(ins, outs, send_sems, recv_sems):
        for cp in copies(ins, outs, send_sems, recv_sems):
            cp.start()

    def finish(ins, outs, send_sems, recv_sems):
        for cp in copies(ins, outs, send_sems, recv_sems):
            cp.wait()

    shapes = [jax.ShapeDtypeStruct((N_CHIPS,) + g.shape[2:], g.dtype) for g in grads]
    return _Side(grads, shapes, n, start, finish)


def _chip_exchange_side(chipsums):
    n = len(chipsums)

    def copies(ins, outs, send_sems, recv_sems):
        x, y, c = _position()
        return [_remote(send_sems, recv_sems, 3 * e + j, ins[e].at[2 * tx + ty], outs[e].at[j], (tx, ty, c))
                for j, (tx, ty) in enumerate(_other_chips(x, y)) for e in range(n)]

    def start(ins, outs, send_sems, recv_sems):
        for cp in copies(ins, outs, send_sems, recv_sems):
            cp.start()

    def finish(ins, outs, send_sems, recv_sems):
        for cp in copies(ins, outs, send_sems, recv_sems):
            cp.wait()

    shapes = [jax.ShapeDtypeStruct((3,) + cs.shape[1:], cs.dtype) for cs in chipsums]
    return _Side(chipsums, shapes, 3 * n, start, finish)


def _share_halves(totals):
    n = len(totals)

    def body(*refs):
        t_refs, r_refs = refs[:n], refs[n:2 * n]
        send_sems, recv_sems = refs[2 * n], refs[2 * n + 1]
        x, y, c = _position()
        cps = [pltpu.make_async_remote_copy(src_ref=t_refs[e], dst_ref=r_refs[e], send_sem=send_sems.at[e],
                                            recv_sem=recv_sems.at[e], device_id=(x, y, 1 - c), device_id_type=MESH)
               for e in range(n)]
        for cp in cps:
            cp.start()
        for cp in cps:
            cp.wait()

    return pl.pallas_call(
        body, name="grad_share_halves", in_specs=[ANY] * n, out_specs=[ANY] * n,
        out_shape=[jax.ShapeDtypeStruct(t.shape, t.dtype) for t in totals],
        scratch_shapes=[pltpu.SemaphoreType.DMA((n,)), pltpu.SemaphoreType.DMA((n,))],
    )(*totals)


def _reduce_rows(h):
    return h if h <= 704 else h // 2


def _add_sibling(grad, recv, c_idx, *, name):
    _, _, h, cw = grad.shape
    th = _reduce_rows(h)

    def body(c_ref, g_ref, r_ref, o_ref):
        o_ref[...] = (g_ref[...] + r_ref[...]).astype(BF16)

    return pl.pallas_call(
        body, name=name,
        grid_spec=pltpu.PrefetchScalarGridSpec(
            num_scalar_prefetch=1, grid=(N_CHIPS, h // th),
            in_specs=[pl.BlockSpec((None, None, th, cw), lambda s, i, c_ref: (s, c_ref[0], i, 0)),
                      pl.BlockSpec((None, th, cw), lambda s, i, c_ref: (s, i, 0))],
            out_specs=pl.BlockSpec((None, th, cw), lambda s, i, c_ref: (s, i, 0))),
        out_shape=jax.ShapeDtypeStruct((N_CHIPS, h, cw), BF16),
        compiler_params=_params("parallel", "parallel"),
    )(c_idx, grad, recv)


def _add_chips(chipsum, recv, s_idx, *, name):
    _, h, cw = chipsum.shape
    th = _reduce_rows(h)

    def body(s_ref, own_ref, r_ref, o_ref):
        o_ref[...] = ((own_ref[...].astype(F32) + r_ref[0].astype(F32)) + r_ref[1].astype(F32)) + r_ref[2].astype(F32)

    return pl.pallas_call(
        body, name=name,
        grid_spec=pltpu.PrefetchScalarGridSpec(
            num_scalar_prefetch=1, grid=(h // th,),
            in_specs=[pl.BlockSpec((None, th, cw), lambda i, s_ref: (s_ref[0], i, 0)),
                      pl.BlockSpec((3, th, cw), lambda i, s_ref: (0, i, 0))],
            out_specs=pl.BlockSpec((th, cw), lambda i, s_ref: (i, 0))),
        out_shape=jax.ShapeDtypeStruct((h, cw), F32),
        compiler_params=_params("parallel"),
    )(s_idx, chipsum, recv)


def _adamw_math(w, g, m, v):
    m = ADAM_B1 * m + (1.0 - ADAM_B1) * g
    v = ADAM_B2 * v + (1.0 - ADAM_B2) * (g * g)
    m_hat = m / (1.0 - ADAM_B1 ** ADAM_STEP)
    v_hat = v / (1.0 - ADAM_B2 ** ADAM_STEP)
    delta = -ADAM_LR * (m_hat / (jnp.sqrt(v_hat) + ADAM_EPS) + ADAM_WD * w)
    return delta, m, v


ADAM_TILE_ELEMS = 256 * 1024


def _adamw(w, g, m, v, *, name):
    shape = w.shape
    cols = shape[-1]
    rows = w.size // cols
    tr = rows
    for cand in range(8, rows, 8):
        if rows % cand == 0 and cand * cols <= ADAM_TILE_ELEMS:
            tr = cand
    if rows * cols <= ADAM_TILE_ELEMS:
        tr = rows

    def body(w_ref, g_ref, m_ref, v_ref, d_ref, nm_ref, nv_ref):
        d, nm, nv = _adamw_math(w_ref[...], g_ref[...], m_ref[...], v_ref[...])
        d_ref[...] = d
        nm_ref[...] = nm
        nv_ref[...] = nv

    blk = pl.BlockSpec((tr, cols), lambda i: (i, 0))
    sds = jax.ShapeDtypeStruct((rows, cols), F32)
    outs = pl.pallas_call(
        body, name=name, grid=(rows // tr,), in_specs=[blk] * 4, out_specs=[blk] * 3, out_shape=[sds] * 3,
        compiler_params=_params("parallel"),
    )(*[a.reshape(rows, cols) for a in (w, g, m, v)])
    return [o.reshape(shape) for o in outs]


SMALL_LAYOUT = (("loss", 1), ("norm_mix", 16), ("norm_ffn", 16), ("ssm_conv_b", 24), ("ssm_dt_bias", 1),
                ("ssm_a_log", 1), ("ssm_d_skip", 1), ("ssm_norm_w", 16), ("att_q_norm", 1), ("att_k_norm", 1),
                ("conv_w_full", 96))
SMALL_ROWS = 176
N_DEVICES = 8


def _small_pack(values):
    parts = []
    for name, rows in SMALL_LAYOUT:
        flat = values[name].reshape(-1).astype(F32)
        parts.append(jnp.pad(flat, (0, rows * LANES - flat.shape[0])).reshape(rows, LANES))
    used = sum(r for _, r in SMALL_LAYOUT)
    parts.append(jnp.zeros((SMALL_ROWS - used, LANES), F32))
    return jnp.concatenate(parts, axis=0)


def _small_unpack(pack, shapes):
    out, off = {}, 0
    for name, rows in SMALL_LAYOUT:
        shape = shapes[name]
        n = math.prod(shape)
        out[name] = pack[off:off + rows].reshape(-1)[:n].reshape(shape)
        off += rows
    return out


def _small_allreduce_adamw(g, w, m, v):
    def body(g_ref, w_ref, m_ref, v_ref, gs_ref, d_ref, nm_ref, nv_ref, buf, send_sems, recv_sems):
        x, y, c = _position()
        pos = (x, y, c)
        me = 4 * x + 2 * y + c
        buf[me] = g_ref[...]
        peers = []
        for k in range(1, N_DEVICES):
            bits = ((k >> 2) & 1, (k >> 1) & 1, k & 1)
            peers.append(tuple(1 - p if b else p for p, b in zip(pos, bits)))
        cps = [pltpu.make_async_remote_copy(src_ref=g_ref, dst_ref=buf.at[me], send_sem=send_sems.at[k],
                                            recv_sem=recv_sems.at[k], device_id=peer, device_id_type=MESH)
               for k, peer in enumerate(peers)]
        for cp in cps:
            cp.start()
        for k, (px, py, pc) in enumerate(peers):
            pltpu.make_async_remote_copy(src_ref=g_ref, dst_ref=buf.at[4 * px + 2 * py + pc],
                                         send_sem=send_sems.at[k], recv_sem=recv_sems.at[k],
                                         device_id=(px, py, pc), device_id_type=MESH).wait_recv()
        for cp in cps:
            cp.wait_send()
        total = buf[0]
        for dev in range(1, N_DEVICES):
            total = total + buf[dev]
        gs_ref[...] = total
        d, nm, nv = _adamw_math(w_ref[...], total, m_ref[...], v_ref[...])
        d_ref[...] = d
        nm_ref[...] = nm
        nv_ref[...] = nv

    vm = pl.BlockSpec(memory_space=pltpu.VMEM)
    sds = jax.ShapeDtypeStruct((SMALL_ROWS, LANES), F32)
    return pl.pallas_call(
        body, name="small_allreduce_adamw", in_specs=[vm] * 4, out_specs=[vm] * 4, out_shape=[sds] * 4,
        scratch_shapes=[pltpu.VMEM((N_DEVICES, SMALL_ROWS, LANES), F32),
                        pltpu.SemaphoreType.DMA((N_DEVICES - 1,)), pltpu.SemaphoreType.DMA((N_DEVICES - 1,))],
    )(g, w, m, v)


SMALL = tuple(n for n, _ in SMALL_LAYOUT if n not in ("loss", "conv_w_full"))
WEIGHTS = ("norm_mix", "norm_ffn", "ssm_w_in", "ssm_conv_w", "ssm_conv_b", "ssm_dt_bias", "ssm_a_log", "ssm_d_skip",
           "ssm_norm_w", "ssm_w_out", "att_w_qkv", "att_q_norm", "att_k_norm", "att_w_o", "ffn_w_gate", "ffn_w_up",
           "ffn_w_down", "ple_w_proj", "ple_w_gate")
COLUMN_SHARDED = ("ssm_w_in", "att_w_qkv", "ffn_w_gate", "ffn_w_up", "ple_w_proj")
LAYERED = ("ffn_w_gate", "ffn_w_up", "ffn_w_down", "ple_w_proj", "ple_w_gate")
GATHER_ORDER = ("ssm_w_in", "ssm_w_out", "att_w_qkv", "att_w_o", "ffn_w_gate", "ffn_w_up", "ffn_w_down",
                "ple_w_proj", "ple_w_gate")


def _layers(n):
    return (0, 1) if n in LAYERED else (None,)


def _tag(key):
    return key[0] if key[1] is None else f"{key[0]}_{key[1]}"


QKV_PARTS = 3


def _weight_slab(w, key):
    n, i = key
    if n == "att_w_qkv":
        a = w[n][0].T
        rows = a.shape[0] // QKV_PARTS
        a = a[i * rows:(i + 1) * rows]
    else:
        a = w[n][0 if i is None else i]
        a = a.T if n in COLUMN_SHARDED else a
    if n == "ssm_w_in":
        a = jnp.pad(a, ((0, W_IN_SLAB_ROWS - a.shape[0]), (0, 0)))
    return a.reshape(2, a.shape[0] // 2, a.shape[1]).astype(BF16)


def _install(prm, key, gathered, own, s_me):
    n, i = key
    full = lax.dynamic_update_slice(gathered, own[None], (s_me, 0, 0, 0))
    full = full.reshape(N_CHIPS, 2 * full.shape[2], full.shape[3])
    if n == "att_w_qkv":
        parts = prm.setdefault("att_w_qkv_parts", {})
        parts[i] = full
        if len(parts) == QKV_PARTS:
            prm[n] = jnp.stack([parts[j] for j in range(QKV_PARTS)], axis=1).reshape(-1, D_MODEL)
        return
    if n == "ssm_w_in":
        rows = (D_INNER + CONV_DIM + SSM_HEADS) // N_CHIPS
        w_in_t = full[:, :rows].reshape(N_CHIPS * rows, D_MODEL)
        prm["ssm_w_z"] = w_in_t[:D_INNER]
        prm["ssm_w_xbc"] = w_in_t[D_INNER:D_INNER + CONV_DIM]
        prm["ssm_w_dt"] = jnp.pad(w_in_t[D_INNER + CONV_DIM:], ((0, LANES - SSM_HEADS), (0, 0)))
        return
    full = full.reshape(N_CHIPS * full.shape[1], full.shape[2])
    if i is None:
        prm[n] = full
    else:
        prm.setdefault(n, [None, None])[i] = full


def _grad_slab(grads, key):
    n, i = key
    g = grads[n] if i is None else grads[n][i]
    if n == "ssm_w_in":
        g = jnp.pad(g.reshape(N_CHIPS, g.shape[0] // N_CHIPS, D_MODEL),
                    ((0, 0), (0, W_IN_SLAB_ROWS - g.shape[0] // N_CHIPS), (0, 0)))
    rows = g.size // (N_CHIPS * g.shape[-1])
    return g.reshape(N_CHIPS, 2, rows // 2, g.shape[-1])


def _natural_shard(n, reduced, shape):
    def one(r):
        if n == "ssm_w_in":
            r = r[:shape[-1]]
        return r.T if n in COLUMN_SHARDED else r
    if n in LAYERED:
        return jnp.stack([one(r) for r in reduced]).reshape(shape)
    return one(reduced[0]).reshape(shape)


def kernel(x, p, norm_mix, norm_ffn, ssm_w_in, ssm_conv_w, ssm_conv_b, ssm_dt_bias, ssm_a_log, ssm_d_skip, ssm_norm_w, ssm_w_out, att_w_qkv, att_q_norm, att_k_norm, att_w_o, ffn_w_gate, ffn_w_up, ffn_w_down, ple_w_proj, ple_w_gate, loss_target, m_norm_mix, m_norm_ffn, m_ssm_w_in, m_ssm_conv_w, m_ssm_conv_b, m_ssm_dt_bias, m_ssm_a_log, m_ssm_d_skip, m_ssm_norm_w, m_ssm_w_out, m_att_w_qkv, m_att_q_norm, m_att_k_norm, m_att_w_o, m_ffn_w_gate, m_ffn_w_up, m_ffn_w_down, m_ple_w_proj, m_ple_w_gate, v_norm_mix, v_norm_ffn, v_ssm_w_in, v_ssm_conv_w, v_ssm_conv_b, v_ssm_dt_bias, v_ssm_a_log, v_ssm_d_skip, v_ssm_norm_w, v_ssm_w_out, v_att_w_qkv, v_att_q_norm, v_att_k_norm, v_att_w_o, v_ffn_w_gate, v_ffn_w_up, v_ffn_w_down, v_ple_w_proj, v_ple_w_gate):
    given = dict(locals())
    w = {n: given[n] for n in WEIGHTS}
    m = {n: given["m_" + n] for n in WEIGHTS}
    v = {n: given["v_" + n] for n in WEIGHTS}
    c_idx = lax.axis_index("c").astype(jnp.int32).reshape(1)
    s_idx = (2 * lax.axis_index("x") + lax.axis_index("y")).astype(jnp.int32).reshape(1)

    s_me = 2 * lax.axis_index("x") + lax.axis_index("y")
    first_core = lax.axis_index("c") == 0

    qkv_parts = [("att_w_qkv", j) for j in range(QKV_PARTS)]
    gather_plan = {
        "ssm_in_xbc": [("ffn_w_gate", 0)],
        "conv_fwd": [("ffn_w_up", 0)],
        "ssd_fwd": [("ffn_w_down", 0), ("ple_w_proj", 0), ("ple_w_gate", 0), ("att_w_o", None)],
        "swiglu_fwd_0": qkv_parts[:2],
        "ffn_down_0": qkv_parts[2:],
        "att_qkv": [(n, 1) for n in LAYERED],
    }
    mamba = [("ssm_w_in", None), ("ssm_w_out", None)]
    own = {k: _weight_slab(w, k) for k in mamba + sum(gather_plan.values(), [])}
    prm = {n: w[n] for n in SMALL}

    def land(group, outputs):
        for k, g in zip(group, outputs):
            _install(prm, k, g, own[k], s_me)

    first = _gather_side([own[k] for k in mamba], whole=[ssm_conv_w[0]])
    _run_side(first, "gather_mamba")
    land(mamba, first.outputs)
    conv = lax.dynamic_update_slice(first.outputs[-1], ssm_conv_w, (s_me, 0, 0))
    prm["ssm_conv_w"] = conv.transpose(1, 0, 2).reshape(CONV_WIDTH, CONV_DIM)

    layer1 = [("att_w_qkv", None), ("att_w_o", None)] + [(n, 1) for n in LAYERED]
    ffn0 = [(n, 0) for n in LAYERED]
    reduce_plan = {"swiglu_bwd_0": ("swap", layer1), "ssd_bwd": ("exchange", layer1),
                   "gate_norm_bwd": ("swap", ffn0), "conv_bwd": ("exchange", ffn0)}
    state = {}

    def swap_side(group):
        state[_tag(group[0]), "g4"] = g4 = [_grad_slab(state["grads"], k) for k in group]
        return _swap_side(g4)

    def add_siblings(group, from_sibling):
        state[_tag(group[0]), "chipsums"] = [
            _add_sibling(g, r, c_idx, name="add_sibling_" + _tag(k))
            for g, r, k in zip(state[_tag(group[0]), "g4"], from_sibling, group)]

    def exchange_side(group):
        return _chip_exchange_side(state[_tag(group[0]), "chipsums"])

    def add_chips(group, from_chips):
        for k, cs, r in zip(group, state[_tag(group[0]), "chipsums"], from_chips):
            state["total", k] = _add_chips(cs, r, s_idx, name="add_chips_" + _tag(k))

    class Plan(_NoOverlap):
        def __init__(self):
            self.carried = {host: _gather_side([own[k] for k in group]) for host, group in gather_plan.items()}

        def begin_backward(self, grads):
            state["grads"] = grads

        def side(self, host):
            if host in reduce_plan:
                step, group = reduce_plan[host]
                self.carried[host] = swap_side(group) if step == "swap" else exchange_side(group)
            return self.carried.get(host)

        def after(self, host):
            if host in gather_plan:
                land(gather_plan[host], self.carried[host].outputs)
            elif host in reduce_plan:
                step, group = reduce_plan[host]
                (add_siblings if step == "swap" else add_chips)(group, self.carried[host].outputs)

    loss_row, dx, grads = _local_step(x[0], p[:, 0], loss_target[0], prm, Plan())

    add_siblings(mamba, _run_side(swap_side(mamba), "grad_swap_mamba"))
    add_chips(mamba, _run_side(exchange_side(mamba), "grad_exchange_mamba"))
    order = mamba + ffn0 + layer1
    shared = _share_halves([state["total", k] for k in order])
    reduced = {}
    for k, theirs in zip(order, shared):
        lo = jnp.where(first_core, state["total", k], theirs)
        hi = jnp.where(first_core, theirs, state["total", k])
        reduced.setdefault(k[0], {})[k[1]] = jnp.concatenate([lo, hi], axis=0)
    reduced = {n: [by_layer[i] for i in _layers(n)] for n, by_layer in reduced.items()}

    grad, delta, new_m, new_v = {}, {}, {}, {}
    for n in GATHER_ORDER:
        grad[n] = _natural_shard(n, reduced[n], w[n].shape)
        delta[n], new_m[n], new_v[n] = _adamw(w[n], grad[n], m[n], v[n], name="adamw_" + n)

    small_g = {n: (jnp.stack(grads[n]) if isinstance(grads[n], list) else grads[n]) for n in SMALL}
    small_g["loss"] = loss_row
    small_g["conv_w_full"] = grads["ssm_conv_w"]
    zero = {"loss": jnp.zeros((1, LANES), F32), "conv_w_full": jnp.zeros((CONV_WIDTH, CONV_DIM), F32)}
    outs = _small_allreduce_adamw(_small_pack(small_g), _small_pack({**w, **zero}), _small_pack({**m, **zero}),
                                  _small_pack({**v, **zero}))
    shapes = {n: w[n].shape for n in SMALL}
    shapes["loss"] = (1, LANES)
    shapes["conv_w_full"] = (CONV_WIDTH, CONV_DIM)
    sg, sd, sm, sv = [_small_unpack(o, shapes) for o in outs]
    for n in SMALL:
        grad[n], delta[n], new_m[n], new_v[n] = sg[n], sd[n], sm[n], sv[n]
    loss = sg["loss"][0, 0]
    conv_cols = CONV_DIM // N_CHIPS
    grad["ssm_conv_w"] = lax.dynamic_slice(sg["conv_w_full"], (0, s_me * conv_cols), (CONV_WIDTH, conv_cols))[None]
    delta["ssm_conv_w"], new_m["ssm_conv_w"], new_v["ssm_conv_w"] = _adamw(
        ssm_conv_w, grad["ssm_conv_w"], m_ssm_conv_w, v_ssm_conv_w, name="adamw_ssm_conv_w")

    return (loss, dx[None], *[grad[n] for n in WEIGHTS], *[delta[n] for n in WEIGHTS],
            *[new_m[n] for n in WEIGHTS], *[new_v[n] for n in WEIGHTS])
```

```python
import functools
import math

import jax
import jax.numpy as jnp
from jax import lax
from jax.experimental import pallas as pl
from jax.experimental.pallas import tpu as pltpu

F32 = jnp.float32
BF16 = jnp.bfloat16
HIGHEST = lax.Precision.HIGHEST

NORM_EPS = 1e-6
ADAM_LR, ADAM_B1, ADAM_B2, ADAM_EPS, ADAM_WD, ADAM_STEP = 0.001, 0.9, 0.999, 1e-08, 0.01, 10

D_MODEL = 1024
D_INNER = 2048
SSM_HEADS = 32
SSM_HEAD_DIM = 64
SSM_GROUPS = 4
SSM_STATE = 128
SSD_CHUNK = 128
CONV_DIM = 3072
CONV_WIDTH = 4
ATT_HEADS = 16
ATT_HEAD_DIM = 64
DIL_PATTERNS = ((128, 1), (512, 4), (2048, 16))
ATT_BLOCK = 128
FFN_HIDDEN = 2816
PLE_DIM = 256

LANES = 128
V7X_VMEM_LIMIT = 56 * 1024 * 1024
NEG_BIG = -1e30

N_CHIPS = 4


def _params(*sem):
    return pltpu.CompilerParams(dimension_semantics=sem, vmem_limit_bytes=V7X_VMEM_LIMIT)


def _tile(n, pref):
    if n <= pref:
        return n
    best = None
    for t in range(LANES, pref + 1, LANES):
        if n % t == 0:
            best = t
    assert best is not None, (n, pref)
    return best


def _sigmoid(v):
    return 1.0 / (1.0 + jnp.exp(-v))


def _dot(a, b):
    return jnp.dot(a, b, preferred_element_type=F32)


def _dot_nt(a, b):
    return lax.dot_general(a, b, (((1,), (1,)), ((), ())), preferred_element_type=F32)


def _dot_tn(a, b):
    return lax.dot_general(a, b, (((0,), (0,)), ((), ())), preferred_element_type=F32)


def _head_block_diag():
    i = lax.broadcasted_iota(jnp.int32, (LANES, LANES), 0) // ATT_HEAD_DIM
    j = lax.broadcasted_iota(jnp.int32, (LANES, LANES), 1) // ATT_HEAD_DIM
    return (i == j).astype(BF16)


def _split_dot(ones, z):
    hi = z.astype(BF16)
    lo = (z - hi.astype(F32)).astype(BF16)
    return _dot(ones, hi) + _dot(ones, lo)


def _head_sums(z, bd):
    hi = z.astype(BF16)
    lo = (z - hi.astype(F32)).astype(BF16)
    parts = []
    for t in range(z.shape[1] // LANES):
        sl = slice(t * LANES, (t + 1) * LANES)
        parts.append(_dot(hi[:, sl], bd) + _dot(lo[:, sl], bd))
    return parts[0] if len(parts) == 1 else jnp.concatenate(parts, axis=1)


def _lane_lt64(rows):
    return lax.broadcasted_iota(jnp.int32, (rows, LANES), 1) < ATT_HEAD_DIM


MESH = pl.DeviceIdType.MESH
ANY = pl.BlockSpec(memory_space=pl.ANY)


class _Side:
    def __init__(self, inputs, out_shapes, n_sems, start, finish):
        self.inputs, self.out_shapes, self.n_sems = list(inputs), list(out_shapes), n_sems
        self.start, self.finish = start, finish
        self.outputs = None


def _call(body, side, *, name, grid, in_specs, out_specs, out_shape, scratch_shapes, semantics, args):
    in_specs, out_specs, out_shape = list(in_specs), list(out_specs), list(out_shape)
    scratch_shapes = list(scratch_shapes)
    if side is None:
        return pl.pallas_call(body, name=name, grid=grid, in_specs=in_specs, out_specs=out_specs,
                              out_shape=out_shape, scratch_shapes=scratch_shapes,
                              compiler_params=_params(*semantics))(*args)
    ni, no, ns = len(in_specs), len(out_specs), len(scratch_shapes)
    si, so = len(side.inputs), len(side.out_shapes)

    def hosted(*refs):
        ins, s_ins = refs[:ni], refs[ni:ni + si]
        outs, s_outs = refs[ni + si:ni + si + no], refs[ni + si + no:ni + si + no + so]
        scratch = refs[ni + si + no + so:ni + si + no + so + ns]
        send_sems, recv_sems = refs[-2], refs[-1]
        first = pl.program_id(0) == 0
        last = pl.program_id(0) == grid[0] - 1
        for axis in range(1, len(grid)):
            first = jnp.logical_and(first, pl.program_id(axis) == 0)
            last = jnp.logical_and(last, pl.program_id(axis) == grid[axis] - 1)

        @pl.when(first)
        def _():
            side.start(s_ins, s_outs, send_sems, recv_sems)

        body(*ins, *outs, *scratch)

        @pl.when(last)
        def _():
            side.finish(s_ins, s_outs, send_sems, recv_sems)

    res = pl.pallas_call(
        hosted, name=name, grid=grid, in_specs=in_specs + [ANY] * si, out_specs=out_specs + [ANY] * so,
        out_shape=out_shape + side.out_shapes,
        scratch_shapes=scratch_shapes + [pltpu.SemaphoreType.DMA((side.n_sems,)),
                                         pltpu.SemaphoreType.DMA((side.n_sems,))],
        compiler_params=_params(*["arbitrary"] * len(grid)),
    )(*args, *side.inputs)
    side.outputs = list(res[no:])
    return list(res[:no])


def _matmul(a, b, *, mode, name, out_dtype=F32, addend=None, tm=1024, tn=512, tk_max=3072, side=None):
    m, k = a.shape
    if mode == "nn":
        k2, n = b.shape
    else:
        n, k2 = b.shape
    assert k == k2, (a.shape, b.shape, mode)
    tm, tn, tk = _tile(m, tm), _tile(n, tn), _tile(k, tk_max)
    nk = k // tk
    has_add = addend is not None

    def body(*refs):
        a_ref, b_ref = refs[0], refs[1]
        add_ref = refs[2] if has_add else None
        o_ref, acc_ref = refs[-2], refs[-1]
        kk = pl.program_id(2)
        av = a_ref[...].astype(BF16)
        bv = b_ref[...].astype(BF16)
        part = _dot(av, bv) if mode == "nn" else _dot_nt(av, bv)

        @pl.when(kk == 0)
        def _():
            acc_ref[...] = part

        @pl.when(kk > 0)
        def _():
            acc_ref[...] += part

        @pl.when(kk == nk - 1)
        def _():
            res = acc_ref[...]
            if has_add:
                res = res + add_ref[...]
            o_ref[...] = res.astype(out_dtype)

    a_spec = pl.BlockSpec((tm, tk), lambda i, j, kk: (i, kk))
    if mode == "nn":
        b_spec = pl.BlockSpec((tk, tn), lambda i, j, kk: (kk, j))
    else:
        b_spec = pl.BlockSpec((tn, tk), lambda i, j, kk: (j, kk))
    in_specs = [a_spec, b_spec]
    args = [a, b]
    if has_add:
        in_specs.append(pl.BlockSpec((tm, tn), lambda i, j, kk: (i, j)))
        args.append(addend)
    return _call(
        body, side, name=name, grid=(m // tm, n // tn, nk),
        in_specs=in_specs, out_specs=[pl.BlockSpec((tm, tn), lambda i, j, kk: (i, j))],
        out_shape=[jax.ShapeDtypeStruct((m, n), out_dtype)],
        scratch_shapes=[pltpu.VMEM((tm, tn), F32)],
        semantics=("parallel", "parallel", "arbitrary"), args=args,
    )[0]


def _matmul_tn(a, b, *, name, tm=1408, tn=512, tk=1024):
    t, m = a.shape
    t2, n = b.shape
    assert t == t2
    tm, tn, tk = _tile(m, tm), _tile(n, tn), _tile(t, tk)

    def body(a_ref, b_ref, o_ref):
        part = _dot_tn(a_ref[...].astype(BF16), b_ref[...].astype(BF16))

        @pl.when(pl.program_id(2) == 0)
        def _():
            o_ref[...] = part

        @pl.when(pl.program_id(2) > 0)
        def _():
            o_ref[...] += part

    return pl.pallas_call(
        body, name=name, grid=(m // tm, n // tn, t // tk),
        in_specs=[pl.BlockSpec((tk, tm), lambda i, j, kk: (kk, i)),
                  pl.BlockSpec((tk, tn), lambda i, j, kk: (kk, j))],
        out_specs=pl.BlockSpec((tm, tn), lambda i, j, kk: (i, j)),
        out_shape=jax.ShapeDtypeStruct((m, n), F32),
        compiler_params=_params("parallel", "parallel", "arbitrary"),
    )(a, b)


def _rmsnorm_fwd(x, gain, *, name):
    t, d = x.shape
    tm = _tile(t, 512)

    def body(x_ref, g_ref, o_ref):
        xv = x_ref[...]
        r = lax.rsqrt(jnp.mean(xv * xv, axis=-1, keepdims=True) + NORM_EPS)
        o_ref[...] = (xv * r * g_ref[...]).astype(BF16)

    return pl.pallas_call(
        body, name=name, grid=(t // tm,),
        in_specs=[pl.BlockSpec((tm, d), lambda i: (i, 0)), pl.BlockSpec((1, d), lambda i: (0, 0))],
        out_specs=pl.BlockSpec((tm, d), lambda i: (i, 0)),
        out_shape=jax.ShapeDtypeStruct((t, d), BF16),
        compiler_params=_params("parallel"),
    )(x, gain)


def _rmsnorm_bwd(x, gain, dy, dres, *, name):
    t, d = x.shape
    tm = _tile(t, 512)

    def body(x_ref, g_ref, dy_ref, dres_ref, dx_ref, dg_ref):
        xv = x_ref[...]
        r = lax.rsqrt(jnp.mean(xv * xv, axis=-1, keepdims=True) + NORM_EPS)
        xh = xv * r
        dyv = dy_ref[...]
        dxh = dyv * g_ref[...]
        mean = jnp.mean(dxh * xh, axis=-1, keepdims=True)
        dx_ref[...] = dres_ref[...] + r * (dxh - xh * mean)
        part = jnp.sum(dyv * xh, axis=0, keepdims=True)

        @pl.when(pl.program_id(0) == 0)
        def _():
            dg_ref[...] = part

        @pl.when(pl.program_id(0) > 0)
        def _():
            dg_ref[...] += part

    row = pl.BlockSpec((tm, d), lambda i: (i, 0))
    vec = pl.BlockSpec((1, d), lambda i: (0, 0))
    return pl.pallas_call(
        body, name=name, grid=(t // tm,),
        in_specs=[row, vec, row, row], out_specs=[row, vec],
        out_shape=[jax.ShapeDtypeStruct((t, d), F32), jax.ShapeDtypeStruct((1, d), F32)],
        compiler_params=_params("arbitrary"),
    )(x, gain, dy, dres)


def _loss_head(y, target):
    t, d = y.shape
    tm = _tile(t, 512)
    steps = t // tm

    def body(y_ref, t_ref, dy_ref, l_ref, acc_ref):
        e = y_ref[...] - t_ref[...]
        dy_ref[...] = e * (1.0 / d)
        part = jnp.sum(e * e, axis=0, keepdims=True)

        @pl.when(pl.program_id(0) == 0)
        def _():
            acc_ref[...] = part

        @pl.when(pl.program_id(0) > 0)
        def _():
            acc_ref[...] += part

        @pl.when(pl.program_id(0) == steps - 1)
        def _():
            l_ref[...] = jnp.full((1, LANES), (0.5 / d), F32) * jnp.sum(acc_ref[...])

    row = pl.BlockSpec((tm, d), lambda i: (i, 0))
    return pl.pallas_call(
        body, name="loss_head", grid=(steps,),
        in_specs=[row, row], out_specs=[row, pl.BlockSpec((1, LANES), lambda i: (0, 0))],
        out_shape=[jax.ShapeDtypeStruct((t, d), F32), jax.ShapeDtypeStruct((1, LANES), F32)],
        scratch_shapes=[pltpu.VMEM((1, d), F32)],
        compiler_params=_params("arbitrary"),
    )(y, target)


def _swiglu_fwd(h, w_gate_t, w_up_t, *, name, side=None):
    t, d = h.shape
    f = w_gate_t.shape[0]
    tm, tn = _tile(t, 1024), _tile(f, 256)

    def body(h_ref, wg_ref, wu_ref, g_ref, u_ref, a_ref):
        hv = h_ref[...]
        g = _dot_nt(hv, wg_ref[...])
        u = _dot_nt(hv, wu_ref[...])
        g_ref[...] = g.astype(BF16)
        u_ref[...] = u.astype(BF16)
        a_ref[...] = (g * _sigmoid(g) * u).astype(BF16)

    wspec = pl.BlockSpec((tn, d), lambda i, j: (j, 0))
    ospec = pl.BlockSpec((tm, tn), lambda i, j: (i, j))
    return _call(
        body, side, name=name, grid=(t // tm, f // tn),
        in_specs=[pl.BlockSpec((tm, d), lambda i, j: (i, 0)), wspec, wspec],
        out_specs=[ospec, ospec, ospec],
        out_shape=[jax.ShapeDtypeStruct((t, f), BF16), jax.ShapeDtypeStruct((t, f), BF16),
                   jax.ShapeDtypeStruct((t, f), BF16)],
        scratch_shapes=[], semantics=("parallel", "parallel"), args=(h, w_gate_t, w_up_t),
    )


def _swiglu_bwd(dx, w_down, g, u, *, name, side=None):
    t, d = dx.shape
    f = w_down.shape[0]
    tm, tn = _tile(t, 1024), _tile(f, 256)

    def body(dx_ref, wd_ref, g_ref, u_ref, dg_ref, du_ref):
        dact = _dot_nt(dx_ref[...].astype(BF16), wd_ref[...])
        gv, uv = g_ref[...].astype(F32), u_ref[...].astype(F32)
        sg = _sigmoid(gv)
        dg_ref[...] = (dact * uv * sg * (1.0 + gv * (1.0 - sg))).astype(BF16)
        du_ref[...] = (dact * gv * sg).astype(BF16)

    ospec = pl.BlockSpec((tm, tn), lambda i, j: (i, j))
    return _call(
        body, side, name=name, grid=(t // tm, f // tn),
        in_specs=[pl.BlockSpec((tm, d), lambda i, j: (i, 0)), pl.BlockSpec((tn, d), lambda i, j: (j, 0)),
                  ospec, ospec],
        out_specs=[ospec, ospec],
        out_shape=[jax.ShapeDtypeStruct((t, f), BF16), jax.ShapeDtypeStruct((t, f), BF16)],
        scratch_shapes=[], semantics=("parallel", "parallel"), args=(dx, w_down, g, u),
    )


def _ple_fwd(x, p, w_gate, w_proj_t, *, name):
    t, d = x.shape
    e = p.shape[1]
    tm, tn = _tile(t, 1024), _tile(d, 512)

    def body(xf_ref, xr_ref, p_ref, wg_ref, wp_ref, o_ref):
        s = _dot(xf_ref[...].astype(BF16), wg_ref[...])
        ple = _dot_nt(p_ref[...].astype(BF16), wp_ref[...])
        o_ref[...] = xr_ref[...] + _sigmoid(s) * ple

    return pl.pallas_call(
        body, name=name, grid=(t // tm, d // tn),
        in_specs=[pl.BlockSpec((tm, d), lambda i, j: (i, 0)), pl.BlockSpec((tm, tn), lambda i, j: (i, j)),
                  pl.BlockSpec((tm, e), lambda i, j: (i, 0)), pl.BlockSpec((d, tn), lambda i, j: (0, j)),
                  pl.BlockSpec((tn, e), lambda i, j: (j, 0))],
        out_specs=pl.BlockSpec((tm, tn), lambda i, j: (i, j)),
        out_shape=jax.ShapeDtypeStruct((t, d), F32),
        compiler_params=_params("parallel", "parallel"),
    )(x, x, p, w_gate, w_proj_t)


def _ple_bwd(x, p, w_gate, w_proj_t, dout, *, name):
    t, d = x.shape
    e = p.shape[1]
    tm, tn = _tile(t, 1024), _tile(d, 512)

    def body(xf_ref, p_ref, wg_ref, wp_ref, do_ref, ds_ref, dple_ref):
        s = _dot(xf_ref[...].astype(BF16), wg_ref[...])
        ple = _dot_nt(p_ref[...].astype(BF16), wp_ref[...])
        gate = _sigmoid(s)
        dov = do_ref[...]
        dple_ref[...] = (dov * gate).astype(BF16)
        ds_ref[...] = (dov * ple * gate * (1.0 - gate)).astype(BF16)

    ospec = pl.BlockSpec((tm, tn), lambda i, j: (i, j))
    return pl.pallas_call(
        body, name=name, grid=(t // tm, d // tn),
        in_specs=[pl.BlockSpec((tm, d), lambda i, j: (i, 0)), pl.BlockSpec((tm, e), lambda i, j: (i, 0)),
                  pl.BlockSpec((d, tn), lambda i, j: (0, j)), pl.BlockSpec((tn, e), lambda i, j: (j, 0)), ospec],
        out_specs=[ospec, ospec],
        out_shape=[jax.ShapeDtypeStruct((t, d), BF16), jax.ShapeDtypeStruct((t, d), BF16)],
        compiler_params=_params("parallel", "parallel"),
    )(x, p, w_gate, w_proj_t, dout)


CONV_TIME_TILE = 256
CONV_HALO = 8


def _conv_taps(ext, w):
    acc = ext[CONV_HALO:, :] * w[CONV_WIDTH - 1:CONV_WIDTH, :]
    shifted = [ext[CONV_HALO:, :]]
    for j in range(1, CONV_WIDTH):
        sh = pltpu.roll(ext, j, 0)[CONV_HALO:, :]
        shifted.append(sh)
        acc = acc + sh * w[CONV_WIDTH - 1 - j:CONV_WIDTH - j, :]
    return acc, shifted


def _conv_fwd(u, w, b, side=None):
    t, c = u.shape
    tc = _tile(c, 256)
    tt = CONV_TIME_TILE

    def body(u_ref, w_ref, b_ref, o_ref):
        wv, bv = w_ref[...], b_ref[...]

        def tile(start, ext):
            pre = _conv_taps(ext, wv)[0] + bv
            o_ref[pl.ds(start, tt), :] = pre * _sigmoid(pre)

        tile(0, jnp.concatenate([jnp.zeros((CONV_HALO, tc), F32), u_ref[0:tt, :]], axis=0))

        def loop(i, carry):
            start = pl.multiple_of(i * tt, tt)
            tile(start, u_ref[pl.ds(start - CONV_HALO, tt + CONV_HALO), :])
            return carry

        lax.fori_loop(1, t // tt, loop, 0)

    col = pl.BlockSpec((t, tc), lambda j: (0, j))
    return _call(
        body, side, name="conv_fwd", grid=(c // tc,),
        in_specs=[col, pl.BlockSpec((CONV_WIDTH, tc), lambda j: (0, j)), pl.BlockSpec((1, tc), lambda j: (0, j))],
        out_specs=[col], out_shape=[jax.ShapeDtypeStruct((t, c), F32)],
        scratch_shapes=[], semantics=("parallel",), args=(u, w, b),
    )[0]


def _conv_bwd(u, w, b, dact, side=None):
    t, c = u.shape
    tc = _tile(c, 256)
    tt = CONV_TIME_TILE

    def body(u_ref, w_ref, b_ref, da_ref, du_ref, dw_ref, db_ref, dpre_ref):
        wv, bv = w_ref[...], b_ref[...]

        def tile(start, ext, sums):
            acc, shifted = _conv_taps(ext, wv)
            pre = acc + bv
            sg = _sigmoid(pre)
            dpre = da_ref[pl.ds(start, tt), :] * (sg * (1.0 + pre * (1.0 - sg)))
            dpre_ref[pl.ds(start, tt), :] = dpre
            new = [sums[0] + jnp.sum(dpre, axis=0, keepdims=True)]
            for j in range(CONV_WIDTH):
                new.append(sums[1 + j] + jnp.sum(dpre * shifted[j], axis=0, keepdims=True))
            return tuple(new)

        zero = jnp.zeros((1, tc), F32)
        sums = tile(0, jnp.concatenate([jnp.zeros((CONV_HALO, tc), F32), u_ref[0:tt, :]], axis=0),
                    (zero,) * (1 + CONV_WIDTH))

        def loop(i, sums):
            start = pl.multiple_of(i * tt, tt)
            return tile(start, u_ref[pl.ds(start - CONV_HALO, tt + CONV_HALO), :], sums)

        sums = lax.fori_loop(1, t // tt, loop, sums)
        db_ref[...] = sums[0]
        dw_ref[...] = jnp.concatenate([sums[1 + (CONV_WIDTH - 1 - k)] for k in range(CONV_WIDTH)], axis=0)
        dpre_ref[pl.ds(t, CONV_HALO), :] = jnp.zeros((CONV_HALO, tc), F32)

        def loop2(i, carry):
            start = pl.multiple_of(i * tt, tt)
            ext = dpre_ref[pl.ds(start, tt + CONV_HALO), :]
            acc = ext[0:tt, :] * wv[CONV_WIDTH - 1:CONV_WIDTH, :]
            for j in range(1, CONV_WIDTH):
                acc = acc + pltpu.roll(ext, tt + CONV_HALO - j, 0)[0:tt, :] * wv[CONV_WIDTH - 1 - j:CONV_WIDTH - j, :]
            du_ref[pl.ds(start, tt), :] = acc.astype(BF16)
            return carry

        lax.fori_loop(0, t // tt, loop2, 0)

    col = pl.BlockSpec((t, tc), lambda j: (0, j))
    return _call(
        body, side, name="conv_bwd", grid=(c // tc,),
        in_specs=[col, pl.BlockSpec((CONV_WIDTH, tc), lambda j: (0, j)), pl.BlockSpec((1, tc), lambda j: (0, j)), col],
        out_specs=[col, pl.BlockSpec((CONV_WIDTH, tc), lambda j: (0, j)), pl.BlockSpec((1, tc), lambda j: (0, j))],
        out_shape=[jax.ShapeDtypeStruct((t, c), BF16), jax.ShapeDtypeStruct((CONV_WIDTH, c), F32),
                   jax.ShapeDtypeStruct((1, c), F32)],
        scratch_shapes=[pltpu.VMEM((t + CONV_HALO, tc), F32)],
        semantics=("parallel",), args=(u, w, b, dact),
    )


def _softplus(v):
    e = jnp.exp(-jnp.abs(v))
    w = 1.0 + e
    log1p = jnp.where(w == 1.0, e, jnp.log(w) * (e / jnp.where(w == 1.0, 1.0, w - 1.0)))
    return jnp.maximum(v, 0.0) + log1p


def _ssd_prep_fwd(dt_raw, dt_bias, a_log):
    t = dt_raw.shape[0]
    cl = SSD_CHUNK

    def body(r_ref, b_ref, al_ref, dt_ref, acs_ref):
        dt = _softplus(r_ref[...] + b_ref[...])
        adt = dt * (-jnp.exp(al_ref[...]))
        li = lax.broadcasted_iota(jnp.int32, (cl, cl), 0)
        si = lax.broadcasted_iota(jnp.int32, (cl, cl), 1)
        tri = (si <= li).astype(F32)
        dt_ref[...] = dt
        acs_ref[...] = jnp.dot(tri, adt, preferred_element_type=F32, precision=HIGHEST)

    row = pl.BlockSpec((cl, LANES), lambda i: (i, 0))
    vec = pl.BlockSpec((1, LANES), lambda i: (0, 0))
    return pl.pallas_call(
        body, name="ssd_prep_fwd", grid=(t // cl,),
        in_specs=[row, vec, vec], out_specs=[row, row],
        out_shape=[jax.ShapeDtypeStruct((t, LANES), F32), jax.ShapeDtypeStruct((t, LANES), F32)],
        compiler_params=_params("parallel"),
    )(dt_raw, dt_bias, a_log)


def _ssd_prep_bwd(dt_raw, dt_bias, ddt):
    t = dt_raw.shape[0]
    tm = _tile(t, 512)

    def body(r_ref, b_ref, d_ref, o_ref, db_ref):
        g = d_ref[...] * _sigmoid(r_ref[...] + b_ref[...])
        o_ref[...] = g.astype(BF16)
        part = jnp.sum(g, axis=0, keepdims=True)

        @pl.when(pl.program_id(0) == 0)
        def _():
            db_ref[...] = part

        @pl.when(pl.program_id(0) > 0)
        def _():
            db_ref[...] += part

    row = pl.BlockSpec((tm, LANES), lambda i: (i, 0))
    vec = pl.BlockSpec((1, LANES), lambda i: (0, 0))
    return pl.pallas_call(
        body, name="ssd_prep_bwd", grid=(t // tm,),
        in_specs=[row, vec, row], out_specs=[row, vec],
        out_shape=[jax.ShapeDtypeStruct((t, LANES), BF16), jax.ShapeDtypeStruct((1, LANES), F32)],
        compiler_params=_params("arbitrary"),
    )(dt_raw, dt_bias, ddt)


GROUP_W = D_INNER // SSM_GROUPS
PAIRS_PER_GROUP = GROUP_W // LANES


def _head_cols(acs_pair, lt64):
    rolled = pltpu.roll(acs_pair, ATT_HEAD_DIM, 1)
    return jnp.where(lt64, acs_pair, rolled), jnp.where(lt64, rolled, acs_pair)


def _ssd_fwd(xbc, dt_rep, acs_rep, acs_t, dskip_rep, side=None):
    t = xbc.shape[0]
    cl = SSD_CHUNK
    nc = t // cl

    def body(xbc_ref, dt_ref, acs_ref, acst_ref, dskip_ref, y_ref, hin_ref, state_ref):
        @pl.when(pl.program_id(0) == 0)
        def _():
            state_ref[...] = jnp.zeros_like(state_ref)

        lt64 = _lane_lt64(cl)
        li = lax.broadcasted_iota(jnp.int32, (cl, cl), 0)
        si = lax.broadcasted_iota(jnp.int32, (cl, cl), 1)
        causal = li >= si
        hin_ref[...] = state_ref[...]
        for g in range(SSM_GROUPS):
            gsl = slice(g * GROUP_W, (g + 1) * GROUP_W)
            xg = xbc_ref[:, gsl]
            bg = xbc_ref[:, D_INNER + g * SSM_STATE:D_INNER + (g + 1) * SSM_STATE]
            cg = xbc_ref[:, D_INNER + SSM_GROUPS * SSM_STATE + g * SSM_STATE:
                         D_INNER + SSM_GROUPS * SSM_STATE + (g + 1) * SSM_STATE]
            acs = acs_ref[:, gsl]
            xdt = xg * dt_ref[:, gsl]
            atot = acs[cl - 1:cl, :]
            hin = state_ref[:, gsl]
            cgb = cg.astype(BF16)
            gmat = _dot_nt(cgb, bg.astype(BF16))
            yoff = _dot(cgb, hin.astype(BF16)) * jnp.exp(acs)
            snew = _dot(bg.T.astype(BF16), (xdt * jnp.exp(atot - acs)).astype(BF16))
            state_ref[:, gsl] = hin * jnp.exp(atot) + snew
            xdtb = xdt.astype(BF16)
            for pr in range(PAIRS_PER_GROUP):
                psl = slice(pr * LANES, (pr + 1) * LANES)
                cols = _head_cols(acs[:, psl], lt64)
                xp = xdtb[:, psl]
                ys = []
                for hh in range(2):
                    h = (g * PAIRS_PER_GROUP + pr) * 2 + hh
                    seg = cols[hh] - acst_ref[h:h + 1, :]
                    lm = jnp.exp(jnp.where(causal, seg, NEG_BIG))
                    ys.append(_dot((gmat * lm).astype(BF16), xp))
                ydiag = jnp.where(lt64, ys[0], ys[1])
                osl = slice(g * GROUP_W + pr * LANES, g * GROUP_W + (pr + 1) * LANES)
                y_ref[:, osl] = ydiag + yoff[:, psl] + xg[:, psl] * dskip_ref[:, osl]

    row = lambda w: pl.BlockSpec((cl, w), lambda c: (c, 0))
    return _call(
        body, side, name="ssd_fwd", grid=(nc,),
        in_specs=[row(CONV_DIM), row(D_INNER), row(D_INNER),
                  pl.BlockSpec((SSM_HEADS, cl), lambda c: (0, c)), pl.BlockSpec((1, D_INNER), lambda c: (0, 0))],
        out_specs=[row(D_INNER), pl.BlockSpec((None, SSM_STATE, D_INNER), lambda c: (c, 0, 0))],
        out_shape=[jax.ShapeDtypeStruct((t, D_INNER), F32), jax.ShapeDtypeStruct((nc, SSM_STATE, D_INNER), F32)],
        scratch_shapes=[pltpu.VMEM((SSM_STATE, D_INNER), F32)],
        semantics=("arbitrary",), args=(xbc, dt_rep, acs_rep, acs_t, dskip_rep),
    )


def _ssd_bwd(xbc, dt_rep, acs_rep, acs_t, dskip_rep, a_rep, hin_all, dy, side=None):
    t = xbc.shape[0]
    cl = SSD_CHUNK
    nc = t // cl

    def body(xbc_ref, dt_ref, acs_ref, acst_ref, dskip_ref, a_ref, hin_ref, dy_ref,
             dxbc_ref, ddt_ref, da_ref, dds_ref, dstate_ref, dacs_ref, dxs_ref):
        step = pl.program_id(0)

        @pl.when(step == 0)
        def _():
            dstate_ref[...] = jnp.zeros_like(dstate_ref)
            da_ref[...] = jnp.zeros_like(da_ref)
            dds_ref[...] = jnp.zeros_like(dds_ref)

        bd = _head_block_diag()
        lt64 = _lane_lt64(cl)
        li = lax.broadcasted_iota(jnp.int32, (cl, cl), 0)
        si = lax.broadcasted_iota(jnp.int32, (cl, cl), 1)
        lower = li >= si
        upper = si >= li
        last_row = lax.broadcasted_iota(jnp.int32, (cl, GROUP_W), 0) == cl - 1
        for g in range(SSM_GROUPS):
            gsl = slice(g * GROUP_W, (g + 1) * GROUP_W)
            bsl = slice(D_INNER + g * SSM_STATE, D_INNER + (g + 1) * SSM_STATE)
            csl = slice(D_INNER + SSM_GROUPS * SSM_STATE + g * SSM_STATE,
                        D_INNER + SSM_GROUPS * SSM_STATE + (g + 1) * SSM_STATE)
            xg = xbc_ref[:, gsl]
            bg = xbc_ref[:, bsl]
            cg = xbc_ref[:, csl]
            bgb, cgb = bg.astype(BF16), cg.astype(BF16)
            acs = acs_ref[:, gsl]
            xdt = xg * dt_ref[:, gsl]
            atot = acs[cl - 1:cl, :]
            eg = jnp.exp(acs)
            dk = jnp.exp(atot - acs)
            etot = jnp.exp(atot)
            hin = hin_ref[:, gsl]
            hinb = hin.astype(BF16)
            dh = dstate_ref[:, gsl]
            dhb = dh.astype(BF16)
            dyg = dy_ref[:, gsl]

            gmat = _dot_nt(cgb, bgb)
            gmat_t = _dot_nt(bgb, cgb)
            ch = _dot(cgb, hinb)
            dacs = _head_sums(dyg * ch * eg, bd)
            dye = (dyg * eg).astype(BF16)
            dc = _dot_nt(dye, hinb)
            dhin = _dot(cg.T.astype(BF16), dye)
            bdh = _dot(bgb, dhb)
            dxs = bdh * dk
            xdk = xdt * dk
            db = _dot_nt(xdk.astype(BF16), dhb)
            ddk = _head_sums(bdh * xdk, bd)
            dacs = dacs - ddk
            datot = jnp.sum(ddk, axis=0, keepdims=True) + etot * _head_sums(
                jnp.sum(dh * hin, axis=0, keepdims=True), bd)
            dacs = dacs + jnp.where(last_row, datot, 0.0)
            dstate_ref[:, gsl] = dh * etot + dhin

            xdtb = xdt.astype(BF16)
            dgsum = jnp.zeros((cl, cl), F32)
            dgsum_t = jnp.zeros((cl, cl), F32)
            for pr in range(PAIRS_PER_GROUP):
                psl = slice(pr * LANES, (pr + 1) * LANES)
                cols = _head_cols(acs[:, psl], lt64)
                xp = xdtb[:, psl]
                dyp = dyg[:, psl].astype(BF16)
                dx1, dac = [], []
                for hh in range(2):
                    h = (g * PAIRS_PER_GROUP + pr) * 2 + hh
                    mine = lt64 if hh == 0 else jnp.logical_not(lt64)
                    row = acst_ref[h:h + 1, :]
                    lm = jnp.exp(jnp.where(lower, cols[hh] - row, NEG_BIG))
                    lm_t = jnp.exp(jnp.where(upper, row - cols[hh], NEG_BIG))
                    dyh = jnp.where(mine, dyp, jnp.zeros_like(dyp))
                    xh = jnp.where(mine, xp, jnp.zeros_like(xp))
                    dm = _dot_nt(dyh, xp)
                    dm_t = _dot_nt(xh, dyp)
                    m_t = gmat_t * lm_t
                    dx1.append(_dot(m_t.astype(BF16), dyp))
                    w = dm * (gmat * lm)
                    w_t = dm_t * m_t
                    dac.append(jnp.sum(w, axis=1, keepdims=True) - jnp.sum(w_t, axis=1, keepdims=True))
                    dgsum = dgsum + dm * lm
                    dgsum_t = dgsum_t + dm_t * lm_t
                osl = slice(g * GROUP_W + pr * LANES, g * GROUP_W + (pr + 1) * LANES)
                dxs_ref[:, osl] = dxs[:, psl] + jnp.where(lt64, dx1[0], dx1[1])
                dacs_ref[:, osl] = dacs[:, psl] + jnp.where(lt64, jnp.broadcast_to(dac[0], (cl, LANES)),
                                                             jnp.broadcast_to(dac[1], (cl, LANES)))
            dxbc_ref[:, csl] = dc + _dot(dgsum.astype(BF16), bgb)
            dxbc_ref[:, bsl] = db + _dot(dgsum_t.astype(BF16), cgb)

        dadt = _split_dot(upper.astype(BF16), dacs_ref[...])
        xall = xbc_ref[:, 0:D_INNER]
        dtall = dt_ref[...]
        dxsall = dxs_ref[...]
        dyall = dy_ref[...]
        ddt_ref[...] = dadt * a_ref[...] + _head_sums(dxsall * xall, bd)
        dxbc_ref[:, 0:D_INNER] = dxsall * dtall + dyall * dskip_ref[...]
        da_ref[...] += jnp.sum(dadt * dtall, axis=0, keepdims=True)
        dds_ref[...] += jnp.sum(dyall * xall, axis=0, keepdims=True)

        @pl.when(step == nc - 1)
        def _():
            dds_ref[...] = _head_sums(dds_ref[...], bd)

    row = lambda w: pl.BlockSpec((cl, w), lambda c: (nc - 1 - c, 0))
    vec = pl.BlockSpec((1, D_INNER), lambda c: (0, 0))
    return _call(
        body, side, name="ssd_bwd", grid=(nc,),
        in_specs=[row(CONV_DIM), row(D_INNER), row(D_INNER),
                  pl.BlockSpec((SSM_HEADS, cl), lambda c: (0, nc - 1 - c)), vec, vec,
                  pl.BlockSpec((None, SSM_STATE, D_INNER), lambda c: (nc - 1 - c, 0, 0)), row(D_INNER)],
        out_specs=[row(CONV_DIM), row(D_INNER), vec, vec],
        out_shape=[jax.ShapeDtypeStruct((t, CONV_DIM), F32), jax.ShapeDtypeStruct((t, D_INNER), F32),
                   jax.ShapeDtypeStruct((1, D_INNER), F32), jax.ShapeDtypeStruct((1, D_INNER), F32)],
        scratch_shapes=[pltpu.VMEM((SSM_STATE, D_INNER), F32), pltpu.VMEM((cl, D_INNER), F32),
                        pltpu.VMEM((cl, D_INNER), F32)],
        semantics=("arbitrary",), args=(xbc, dt_rep, acs_rep, acs_t, dskip_rep, a_rep, hin_all, dy),
    )


def _gate_norm_fwd(y, z, w):
    t, c = y.shape
    tm = _tile(t, 256)

    def body(y_ref, z_ref, w_ref, o_ref):
        for g in range(SSM_GROUPS):
            gsl = slice(g * GROUP_W, (g + 1) * GROUP_W)
            zv = z_ref[:, gsl]
            v = y_ref[:, gsl] * (zv * _sigmoid(zv))
            r = lax.rsqrt(jnp.mean(v * v, axis=-1, keepdims=True) + NORM_EPS)
            o_ref[:, gsl] = (v * r * w_ref[:, gsl]).astype(BF16)

    row = pl.BlockSpec((tm, c), lambda i: (i, 0))
    return pl.pallas_call(
        body, name="gate_norm_fwd", grid=(t // tm,),
        in_specs=[row, row, pl.BlockSpec((1, c), lambda i: (0, 0))], out_specs=row,
        out_shape=jax.ShapeDtypeStruct((t, c), BF16),
        compiler_params=_params("parallel"),
    )(y, z, w)


def _gate_norm_bwd(y, z, w, dout, side=None):
    t, c = y.shape
    tm = _tile(t, 256)

    def body(y_ref, z_ref, w_ref, do_ref, dy_ref, dz_ref, dw_ref):
        @pl.when(pl.program_id(0) == 0)
        def _():
            dw_ref[...] = jnp.zeros_like(dw_ref)

        for g in range(SSM_GROUPS):
            gsl = slice(g * GROUP_W, (g + 1) * GROUP_W)
            zv, yv, dov = z_ref[:, gsl], y_ref[:, gsl], do_ref[:, gsl]
            sg = _sigmoid(zv)
            sz = zv * sg
            v = yv * sz
            r = lax.rsqrt(jnp.mean(v * v, axis=-1, keepdims=True) + NORM_EPS)
            vh = v * r
            dvh = dov * w_ref[:, gsl]
            mean = jnp.mean(dvh * vh, axis=-1, keepdims=True)
            dv = r * (dvh - vh * mean)
            dy_ref[:, gsl] = dv * sz
            dz_ref[:, gsl] = (dv * yv * (sg * (1.0 + zv * (1.0 - sg)))).astype(BF16)
            dw_ref[:, gsl] += jnp.sum(dov * vh, axis=0, keepdims=True)

    row = pl.BlockSpec((tm, c), lambda i: (i, 0))
    vec = pl.BlockSpec((1, c), lambda i: (0, 0))
    return _call(
        body, side, name="gate_norm_bwd", grid=(t // tm,),
        in_specs=[row, row, vec, row], out_specs=[row, row, vec],
        out_shape=[jax.ShapeDtypeStruct((t, c), F32), jax.ShapeDtypeStruct((t, c), BF16),
                   jax.ShapeDtypeStruct((1, c), F32)],
        scratch_shapes=[], semantics=("arbitrary",), args=(y, z, w, dout),
    )


ATT_W = ATT_HEADS * ATT_HEAD_DIM
N_QKV_BLOCKS = 9
ATT_SCALE = 1.0 / math.sqrt(ATT_HEAD_DIM)


def _head_rmsnorm(x, gain, bd):
    ms = _head_sums(x * x, bd) * (1.0 / ATT_HEAD_DIM)
    return x * lax.rsqrt(ms + NORM_EPS) * gain


def _class_rows(ref, blk, r, dil):
    span = ATT_BLOCK * dil
    sub = ref.at[pl.ds(pl.multiple_of(blk * span, span), span), :]
    return sub[...] if dil == 1 else sub[pl.ds(r, ATT_BLOCK, stride=dil), :]


def _store_class_rows(ref, blk, r, dil, val):
    span = ATT_BLOCK * dil
    sub = ref.at[pl.ds(pl.multiple_of(blk * span, span), span), :]
    if dil == 1:
        sub[...] = val
    else:
        sub[pl.ds(r, ATT_BLOCK, stride=dil), :] = val


PAIRS = ATT_HEADS // 2


def _pair_col(g, j):
    return lambda pair: (0, (g * 3 + j) * PAIRS + pair)


def _pair_slopes(pair):
    steps = jnp.full((1, 2 * ATT_BLOCK), 2 * pair + 1, jnp.int32).astype(F32)
    first = jnp.exp(steps * (-0.5 * math.log(2.0)))
    return first, first * (2.0 ** -0.5)


NORM_ROWS = 512


ROW_SLICES = 4
SLICE_ROWS = 2 * ATT_BLOCK // ROW_SLICES


def _fill_band_bias(bias_ref, pair, dil, transposed):
    bq = ATT_BLOCK
    a = lax.broadcasted_iota(jnp.int32, (2 * bq, 2 * bq), 0) % bq
    b = lax.broadcasted_iota(jnp.int32, (2 * bq, 2 * bq), 1)
    dist = (b - a) if transposed else (a + bq - b)
    in_band = (dist >= 0) & (dist <= bq)
    s0, s1 = _pair_slopes(pair)
    first_head = lax.broadcasted_iota(jnp.int32, (2 * bq, 2 * bq), 0) < bq
    bias = jnp.where(first_head, s0, s1) * (dist.astype(F32) * float(dil))
    inside = (b < bq) if transposed else (b >= bq)
    bias_ref[1] = jnp.where(in_band, bias, -NEG_BIG)
    bias_ref[0] = jnp.where(in_band & inside, bias, -NEG_BIG)


def _row_slices():
    return [slice(i * SLICE_ROWS, (i + 1) * SLICE_ROWS) for i in range(ROW_SLICES)]


def _stack_heads(tile):
    rows = lax.broadcasted_iota(jnp.int32, (2 * ATT_BLOCK, LANES), 0) < ATT_BLOCK
    lanes = lax.broadcasted_iota(jnp.int32, (2 * ATT_BLOCK, LANES), 1) < ATT_HEAD_DIM
    both = jnp.concatenate([tile, tile], axis=0)
    return jnp.where(rows == lanes, both, jnp.zeros_like(both))


def _unstack_heads(stacked, lt64):
    return jnp.where(lt64, stacked[:ATT_BLOCK], stacked[ATT_BLOCK:])


ITEMS_PER_PASS = 4


def _item_loop(nb, dil, work):
    if dil == 1:
        def trip(i, carry):
            work([(i * ITEMS_PER_PASS + b, 0) for b in range(ITEMS_PER_PASS)])
            return carry

        lax.fori_loop(0, nb // ITEMS_PER_PASS, trip, 0)
    else:
        def trip(n, carry):
            for r0 in range(0, dil, ITEMS_PER_PASS):
                pl.when(n >= 0)(functools.partial(work, [(n, r0 + j) for j in range(ITEMS_PER_PASS)]))
            return carry

        lax.fori_loop(0, nb, trip, 0)


def _normalise_qk(q_ref, k_ref, gq_ref, gk_ref, qn_ref, kn_ref):
    bd = _head_block_diag()
    gq_scaled = gq_ref[...] * ATT_SCALE

    def step(i, carry):
        rows = pl.ds(pl.multiple_of(i * NORM_ROWS, NORM_ROWS), NORM_ROWS)
        qn_ref[rows, :] = _head_rmsnorm(q_ref[rows, :], gq_scaled, bd)
        kn_ref[rows, :] = _head_rmsnorm(k_ref[rows, :], gk_ref[...], bd)
        return carry

    lax.fori_loop(0, q_ref.shape[0] // NORM_ROWS, step, 0)


def _attn_fwd(qkv, gq, gk, g, dil):
    t = qkv.shape[0]
    nb = t // dil // ATT_BLOCK
    bq = ATT_BLOCK

    def body(q_ref, k_ref, v_ref, gq_ref, gk_ref, o_ref, l_ref, qn_ref, kn_ref, bias_ref):
        _normalise_qk(q_ref, k_ref, gq_ref, gk_ref, qn_ref, kn_ref)
        _fill_band_bias(bias_ref, pl.program_id(0), dil, False)
        lt64 = _lane_lt64(bq)

        def work(items):
            scores, values, probs = [], [], []
            for n, r in items:
                prev = jnp.maximum(n - 1, 0)
                q2 = _stack_heads(_class_rows(qn_ref, n, r, dil).astype(BF16))
                kcat = jnp.concatenate([_class_rows(kn_ref, prev, r, dil), _class_rows(kn_ref, n, r, dil)],
                                       axis=0).astype(BF16)
                values.append(jnp.concatenate([_class_rows(v_ref, prev, r, dil), _class_rows(v_ref, n, r, dil)],
                                              axis=0).astype(BF16))
                scores.append(_dot_nt(q2, kcat))
            for (n, r), sc in zip(items, scores):
                bias = bias_ref.at[jnp.minimum(n, 1)]
                ps, inv, lses = [], [], []
                for rows in _row_slices():
                    s = sc[rows] - bias[rows, :]
                    m = jnp.max(s, axis=1, keepdims=True)
                    p = jnp.exp(s - m)
                    l = jnp.sum(p, axis=1, keepdims=True)
                    ps.append(p.astype(BF16))
                    inv.append(jnp.broadcast_to(1.0 / l, (SLICE_ROWS, LANES)))
                    lses.append(jnp.broadcast_to(m + jnp.log(l), (SLICE_ROWS, LANES)))
                probs.append((jnp.concatenate(ps, axis=0), jnp.concatenate(inv, axis=0)))
                _store_class_rows(l_ref, n, r, dil, _unstack_heads(jnp.concatenate(lses, axis=0), lt64))
            for (n, r), (p, inv), vcat in zip(items, probs, values):
                _store_class_rows(o_ref, n, r, dil, _unstack_heads(_dot(p, vcat) * inv, lt64))

        _item_loop(nb, dil, work)

    col = lambda j: pl.BlockSpec((t, LANES), _pair_col(g, j))
    vec = pl.BlockSpec((1, LANES), lambda pair: (0, 0))
    out = pl.BlockSpec((t, LANES), lambda pair: (0, pair))
    return pl.pallas_call(
        body, name=f"attn_fwd_g{g}", grid=(PAIRS,),
        in_specs=[col(0), col(1), col(2), vec, vec], out_specs=[out, out],
        out_shape=[jax.ShapeDtypeStruct((t, ATT_W), F32), jax.ShapeDtypeStruct((t, ATT_W), F32)],
        scratch_shapes=[pltpu.VMEM((t, LANES), F32), pltpu.VMEM((t, LANES), F32),
                        pltpu.VMEM((2, 2 * bq, 2 * bq), F32)],
        compiler_params=_params("parallel"),
    )(qkv, qkv, qkv, gq, gk)


def _attn_combine_fwd(outs, lses):
    t = outs[0].shape[0]
    tm = _tile(t, 256)

    def body(o0, o1, o2, l0, l1, l2, ob_ref, of_ref, lt_ref):
        a, b, c = l0[...], l1[...], l2[...]
        m = jnp.maximum(jnp.maximum(a, b), c)
        ea, eb, ec = jnp.exp(a - m), jnp.exp(b - m), jnp.exp(c - m)
        ssum = ea + eb + ec
        o = (ea * o0[...] + eb * o1[...] + ec * o2[...]) / ssum
        ob_ref[...] = o.astype(BF16)
        of_ref[...] = o
        lt_ref[...] = m + jnp.log(ssum)

    row = pl.BlockSpec((tm, ATT_W), lambda i: (i, 0))
    return pl.pallas_call(
        body, name="attn_combine_fwd", grid=(t // tm,),
        in_specs=[row] * 6, out_specs=[row] * 3,
        out_shape=[jax.ShapeDtypeStruct((t, ATT_W), BF16), jax.ShapeDtypeStruct((t, ATT_W), F32),
                   jax.ShapeDtypeStruct((t, ATT_W), F32)],
        compiler_params=_params("parallel"),
    )(*outs, *lses)


def _attn_combine_bwd(do, o):
    t = do.shape[0]
    tm = _tile(t, 256)

    def body(do_ref, o_ref, dl_ref):
        dl_ref[...] = _head_sums(do_ref[...] * o_ref[...], _head_block_diag())

    row = pl.BlockSpec((tm, ATT_W), lambda i: (i, 0))
    return pl.pallas_call(
        body, name="attn_combine_bwd", grid=(t // tm,),
        in_specs=[row, row], out_specs=row, out_shape=jax.ShapeDtypeStruct((t, ATT_W), F32),
        compiler_params=_params("parallel"),
    )(do, o)


def _head_rmsnorm_bwd(x_ref, dy_ref, gain_ref, dx_ref, dgain_ref):
    bd = _head_block_diag()
    gain = gain_ref[...]

    def step(i, acc):
        rows = pl.ds(pl.multiple_of(i * NORM_ROWS, NORM_ROWS), NORM_ROWS)
        x, dy = x_ref[rows, :], dy_ref[rows, :]
        r = lax.rsqrt(_head_sums(x * x, bd) * (1.0 / ATT_HEAD_DIM) + NORM_EPS)
        xh = x * r
        dxh = dy * gain
        mean = _head_sums(dxh * xh, bd) * (1.0 / ATT_HEAD_DIM)
        dx_ref[rows, :] = (r * (dxh - xh * mean)).astype(BF16)
        return acc + jnp.sum(dy * xh, axis=0, keepdims=True)

    acc = lax.fori_loop(0, x_ref.shape[0] // NORM_ROWS, step, jnp.zeros((1, LANES), F32))
    dgain_ref[...] = jnp.broadcast_to(acc, dgain_ref.shape)


def _attn_bwd_dq(qkv, gq, gk, do, l_rep, dl_rep, g, dil):
    t = qkv.shape[0]
    nb = t // dil // ATT_BLOCK
    bq = ATT_BLOCK

    def body(q_ref, k_ref, v_ref, gq_ref, gk_ref, do_ref, l_ref, dl_ref, dx_ref, dgain_ref, qn_ref, kn_ref, bias_ref,
             dq_ref):
        _normalise_qk(q_ref, k_ref, gq_ref, gk_ref, qn_ref, kn_ref)
        _fill_band_bias(bias_ref, pl.program_id(0), dil, False)
        lt64 = _lane_lt64(bq)

        def per_row(tile):
            cols = _head_cols(tile, lt64)
            half = jnp.concatenate([cols[0], cols[1]], axis=0)
            return jnp.concatenate([half, half], axis=1)

        def work(items):
            products, keys, dscores = [], [], []
            for n, r in items:
                prev = jnp.maximum(n - 1, 0)
                q2 = _stack_heads(_class_rows(qn_ref, n, r, dil).astype(BF16))
                do2 = _stack_heads(_class_rows(do_ref, n, r, dil).astype(BF16))
                kcat = jnp.concatenate([_class_rows(kn_ref, prev, r, dil), _class_rows(kn_ref, n, r, dil)],
                                       axis=0).astype(BF16)
                vcat = jnp.concatenate([_class_rows(v_ref, prev, r, dil), _class_rows(v_ref, n, r, dil)],
                                       axis=0).astype(BF16)
                keys.append(kcat)
                products.append((_dot_nt(q2, kcat), _dot_nt(do2, vcat)))
            for (n, r), (scores, dps) in zip(items, products):
                bias = bias_ref.at[jnp.minimum(n, 1)]
                lse = per_row(_class_rows(l_ref, n, r, dil))
                dl = per_row(_class_rows(dl_ref, n, r, dil))
                dss = []
                for rows in _row_slices():
                    p = jnp.exp(scores[rows] - bias[rows, :] - lse[rows])
                    dss.append((p * (dps[rows] - dl[rows])).astype(BF16))
                dscores.append(jnp.concatenate(dss, axis=0))
            for (n, r), ds, kcat in zip(items, dscores, keys):
                _store_class_rows(dq_ref, n, r, dil, _unstack_heads(_dot(ds, kcat) * ATT_SCALE, lt64))

        _item_loop(nb, dil, work)
        _head_rmsnorm_bwd(q_ref, dq_ref, gq_ref, dx_ref, dgain_ref)

    col = lambda j: pl.BlockSpec((t, LANES), _pair_col(g, j))
    vec = pl.BlockSpec((1, LANES), lambda pair: (0, 0))
    tok = pl.BlockSpec((t, LANES), lambda pair: (0, pair))
    return pl.pallas_call(
        body, name=f"attn_bwd_dq_g{g}", grid=(PAIRS,),
        in_specs=[col(0), col(1), col(2), vec, vec, tok, tok, tok],
        out_specs=[tok, pl.BlockSpec((None, 8, LANES), lambda pair: (pair, 0, 0))],
        out_shape=[jax.ShapeDtypeStruct((t, ATT_W), BF16), jax.ShapeDtypeStruct((PAIRS, 8, LANES), F32)],
        scratch_shapes=[pltpu.VMEM((t, LANES), F32), pltpu.VMEM((t, LANES), F32),
                        pltpu.VMEM((2, 2 * bq, 2 * bq), F32), pltpu.VMEM((t, LANES), F32)],
        compiler_params=_params("parallel"),
    )(qkv, qkv, qkv, gq, gk, do, l_rep, dl_rep)


def _attn_bwd_dkv(qkv, gq, gk, do, l_row, dl_row, g, dil):
    t = qkv.shape[0]
    nb = t // dil // ATT_BLOCK
    bq = ATT_BLOCK

    def body(q_ref, k_ref, v_ref, gq_ref, gk_ref, do_ref, l_ref, dl_ref, dkx_ref, dvx_ref, dgain_ref, qn_ref, kn_ref,
             bias_ref, dk_ref, dv_ref):
        _normalise_qk(q_ref, k_ref, gq_ref, gk_ref, qn_ref, kn_ref)
        _fill_band_bias(bias_ref, pl.program_id(0), dil, True)
        lt64 = _lane_lt64(bq)

        def per_query(ref, hh, lane_c, lane_n):
            return jnp.concatenate([ref[hh:hh + 1, pl.ds(lane_c, bq)], ref[hh:hh + 1, pl.ds(lane_n, bq)]], axis=1)

        def work(items):
            products, operands, weights = [], [], []
            for n, r in items:
                nxt = jnp.minimum(n + 1, nb - 1)
                k2 = _stack_heads(_class_rows(kn_ref, n, r, dil).astype(BF16))
                v2 = _stack_heads(_class_rows(v_ref, n, r, dil).astype(BF16))
                qcat = jnp.concatenate([_class_rows(qn_ref, n, r, dil), _class_rows(qn_ref, nxt, r, dil)],
                                       axis=0).astype(BF16)
                docat = jnp.concatenate([_class_rows(do_ref, n, r, dil), _class_rows(do_ref, nxt, r, dil)],
                                        axis=0).astype(BF16)
                operands.append((qcat, docat))
                products.append((_dot_nt(k2, qcat), _dot_nt(v2, docat)))
            for (n, r), (scores, dps) in zip(items, products):
                nxt = jnp.minimum(n + 1, nb - 1)
                bias = bias_ref.at[jnp.where(n == nb - 1, 0, 1)]
                lane_c = pl.multiple_of((r * nb + n) * bq, bq)
                lane_n = pl.multiple_of((r * nb + nxt) * bq, bq)
                lse = [per_query(l_ref, hh, lane_c, lane_n) for hh in range(2)]
                dl = [per_query(dl_ref, hh, lane_c, lane_n) for hh in range(2)]
                pts, dss = [], []
                for i, rows in enumerate(_row_slices()):
                    hh = i * SLICE_ROWS // bq
                    p_t = jnp.exp(scores[rows] - bias[rows, :] - lse[hh])
                    pts.append(p_t.astype(BF16))
                    dss.append((p_t * (dps[rows] - dl[hh])).astype(BF16))
                weights.append((jnp.concatenate(pts, axis=0), jnp.concatenate(dss, axis=0)))
            for (n, r), (p_t, ds_t), (qcat, docat) in zip(items, weights, operands):
                _store_class_rows(dv_ref, n, r, dil, _unstack_heads(_dot(p_t, docat), lt64))
                _store_class_rows(dk_ref, n, r, dil, _unstack_heads(_dot(ds_t, qcat), lt64))

        _item_loop(nb, dil, work)
        _head_rmsnorm_bwd(k_ref, dk_ref, gk_ref, dkx_ref, dgain_ref)

        def cast_rows(i, carry):
            rows = pl.ds(pl.multiple_of(i * NORM_ROWS, NORM_ROWS), NORM_ROWS)
            dvx_ref[rows, :] = dv_ref[rows, :].astype(BF16)
            return carry

        lax.fori_loop(0, t // NORM_ROWS, cast_rows, 0)

    col = lambda j: pl.BlockSpec((t, LANES), _pair_col(g, j))
    vec = pl.BlockSpec((1, LANES), lambda pair: (0, 0))
    tok = pl.BlockSpec((t, LANES), lambda pair: (0, pair))
    rows = pl.BlockSpec((None, 8, t), lambda pair: (pair, 0, 0))
    return pl.pallas_call(
        body, name=f"attn_bwd_dkv_g{g}", grid=(PAIRS,),
        in_specs=[col(0), col(1), col(2), vec, vec, tok, rows, rows],
        out_specs=[tok, tok, pl.BlockSpec((None, 8, LANES), lambda pair: (pair, 0, 0))],
        out_shape=[jax.ShapeDtypeStruct((t, ATT_W), BF16), jax.ShapeDtypeStruct((t, ATT_W), BF16),
                   jax.ShapeDtypeStruct((PAIRS, 8, LANES), F32)],
        scratch_shapes=[pltpu.VMEM((t, LANES), F32), pltpu.VMEM((t, LANES), F32),
                        pltpu.VMEM((2, 2 * bq, 2 * bq), F32), pltpu.VMEM((t, LANES), F32),
                        pltpu.VMEM((t, LANES), F32)],
        compiler_params=_params("parallel"),
    )(qkv, qkv, qkv, gq, gk, do, l_row, dl_row)


def _rows_by_residue(rep, dil):
    t = rep.shape[0]
    per_head = rep[:, ::ATT_HEAD_DIM]
    rows = per_head.reshape(t // dil, dil, ATT_HEADS).transpose(2, 1, 0).reshape(PAIRS, 2, t)
    return jnp.pad(rows, ((0, 0), (0, 6), (0, 0)))


def _per_head(rep_row):
    return rep_row[0, ::SSM_HEAD_DIM]


def _rep_heads(v):
    return jnp.repeat(v, SSM_HEAD_DIM)[None, :]


def _pad_lanes(v):
    return jnp.pad(v, ((0, 0), (0, LANES - v.shape[1])))


class _NoOverlap:
    def side(self, host):
        return None

    def after(self, host):
        pass

    def begin_backward(self, grads):
        pass


def _hosted(plan, host, fn, *args, **kwargs):
    out = fn(*args, side=plan.side(host), **kwargs)
    plan.after(host)
    return out


def _ffn_ple_fwd(x1, p_i, prm, i, plan):
    h = _rmsnorm_fwd(x1, prm["norm_ffn"][i:i + 1], name=f"ffn_norm_fwd_{i}")
    g, u, act = _hosted(plan, f"swiglu_fwd_{i}", _swiglu_fwd, h, prm["ffn_w_gate"][i], prm["ffn_w_up"][i],
                        name=f"swiglu_fwd_{i}")
    x2 = _hosted(plan, f"ffn_down_{i}", _matmul, act, prm["ffn_w_down"][i], mode="nn", addend=x1,
                 name=f"ffn_down_{i}")
    x3 = _ple_fwd(x2, p_i, prm["ple_w_gate"][i], prm["ple_w_proj"][i], name=f"ple_fwd_{i}")
    return x3, dict(x1=x1, h=h, g=g, u=u, act=act, x2=x2)


def _ffn_ple_bwd(dx3, p_i, prm, i, sv, grads, plan):
    ds, dple = _ple_bwd(sv["x2"], p_i, prm["ple_w_gate"][i], prm["ple_w_proj"][i], dx3, name=f"ple_bwd_{i}")
    grads["ple_w_gate"][i] = _matmul_tn(sv["x2"], ds, name=f"d_ple_w_gate_{i}")
    grads["ple_w_proj"][i] = _matmul_tn(dple, p_i, name=f"d_ple_w_proj_{i}")
    dx2 = _matmul(ds, prm["ple_w_gate"][i], mode="nt", addend=dx3, name=f"ple_dx_{i}")
    grads["ffn_w_down"][i] = _matmul_tn(sv["act"], dx2, name=f"d_ffn_w_down_{i}")
    dg, du = _hosted(plan, f"swiglu_bwd_{i}", _swiglu_bwd, dx2, prm["ffn_w_down"][i], sv["g"], sv["u"],
                     name=f"swiglu_bwd_{i}")
    grads["ffn_w_gate"][i] = _matmul_tn(dg, sv["h"], name=f"d_ffn_w_gate_{i}")
    grads["ffn_w_up"][i] = _matmul_tn(du, sv["h"], name=f"d_ffn_w_up_{i}")
    dh = _matmul(dg, prm["ffn_w_gate"][i], mode="nn", name=f"ffn_dh_gate_{i}")
    dh = _matmul(du, prm["ffn_w_up"][i], mode="nn", addend=dh, name=f"ffn_dh_up_{i}")
    dx1, dgain = _rmsnorm_bwd(sv["x1"], prm["norm_ffn"][i:i + 1], dh, dx2, name=f"ffn_norm_bwd_{i}")
    grads["norm_ffn"][i] = dgain[0]
    return dx1


def _mamba_fwd(x0, prm, plan):
    h = _rmsnorm_fwd(x0, prm["norm_mix"][0:1], name="mix_norm_fwd_0")
    z = _hosted(plan, "ssm_in_z", _matmul, h, prm["ssm_w_z"], mode="nt", name="ssm_in_z")
    xbc_pre = _hosted(plan, "ssm_in_xbc", _matmul, h, prm["ssm_w_xbc"], mode="nt", name="ssm_in_xbc")
    dt_raw = _matmul(h, prm["ssm_w_dt"], mode="nt", name="ssm_in_dt")
    xbc = _hosted(plan, "conv_fwd", _conv_fwd, xbc_pre, prm["ssm_conv_w"], prm["ssm_conv_b"])
    dt_bias = _pad_lanes(prm["ssm_dt_bias"])
    a_log = _pad_lanes(prm["ssm_a_log"])
    dt, acs = _ssd_prep_fwd(dt_raw, dt_bias, a_log)
    dt_rep = jnp.repeat(dt[:, :SSM_HEADS], SSM_HEAD_DIM, axis=1)
    acs_rep = jnp.repeat(acs[:, :SSM_HEADS], SSM_HEAD_DIM, axis=1)
    acs_t = acs[:, :SSM_HEADS].T
    dskip_rep = _rep_heads(prm["ssm_d_skip"][0])
    y, hin_all = _hosted(plan, "ssd_fwd", _ssd_fwd, xbc, dt_rep, acs_rep, acs_t, dskip_rep)
    yn = _gate_norm_fwd(y, z, prm["ssm_norm_w"])
    x1 = _matmul(yn, prm["ssm_w_out"], mode="nn", addend=x0, name="ssm_out")
    sv = dict(x0=x0, h=h, z=z, xbc_pre=xbc_pre, dt_raw=dt_raw, xbc=xbc, dt_bias=dt_bias, dt_rep=dt_rep,
              acs_rep=acs_rep, acs_t=acs_t, dskip_rep=dskip_rep, y=y, hin_all=hin_all, yn=yn)
    return x1, sv


def _mamba_bwd(dx1, prm, sv, grads, plan):
    grads["ssm_w_out"] = _matmul_tn(sv["yn"], dx1, name="d_ssm_w_out")
    dyn = _matmul(dx1, prm["ssm_w_out"], mode="nt", name="ssm_out_dx")
    dy, dz, dnw = _hosted(plan, "gate_norm_bwd", _gate_norm_bwd, sv["y"], sv["z"], prm["ssm_norm_w"], dyn)
    grads["ssm_norm_w"] = dnw
    a_rep = _rep_heads(-jnp.exp(prm["ssm_a_log"][0]))
    dxbc, ddt_rep, da_rep, dds_rep = _hosted(plan, "ssd_bwd", _ssd_bwd, sv["xbc"], sv["dt_rep"], sv["acs_rep"],
                                             sv["acs_t"], sv["dskip_rep"], a_rep, sv["hin_all"], dy)
    grads["ssm_d_skip"] = _per_head(dds_rep)[None, :]
    grads["ssm_a_log"] = (_per_head(da_rep) * _per_head(a_rep))[None, :]
    ddt = _pad_lanes(ddt_rep[:, ::SSM_HEAD_DIM])
    ddt_raw, dbias = _ssd_prep_bwd(sv["dt_raw"], sv["dt_bias"], ddt)
    grads["ssm_dt_bias"] = dbias[:, :SSM_HEADS]
    du, dcw, dcb = _hosted(plan, "conv_bwd", _conv_bwd, sv["xbc_pre"], prm["ssm_conv_w"], prm["ssm_conv_b"], dxbc)
    grads["ssm_conv_w"] = dcw
    grads["ssm_conv_b"] = dcb
    h = sv["h"]
    grads["ssm_w_in"] = jnp.concatenate(
        [_matmul_tn(dz, h, name="d_ssm_w_z"), _matmul_tn(du, h, name="d_ssm_w_xbc"),
         _matmul_tn(ddt_raw, h, name="d_ssm_w_dt")[:SSM_HEADS]], axis=0)
    dh = _matmul(dz, prm["ssm_w_z"], mode="nn", name="ssm_dh_z")
    dh = _matmul(du, prm["ssm_w_xbc"], mode="nn", addend=dh, name="ssm_dh_xbc")
    dh = _matmul(ddt_raw, prm["ssm_w_dt"], mode="nn", addend=dh, name="ssm_dh_dt")
    dx0, dgain = _rmsnorm_bwd(sv["x0"], prm["norm_mix"][0:1], dh, dx1, name="mix_norm_bwd_0")
    grads["norm_mix"][0] = dgain[0]
    return dx0


def _attn_mixer_fwd(x0, prm, plan):
    h = _rmsnorm_fwd(x0, prm["norm_mix"][1:2], name="mix_norm_fwd_1")
    qkv = _hosted(plan, "att_qkv", _matmul, h, prm["att_w_qkv"], mode="nt", name="att_qkv")
    gq = jnp.tile(prm["att_q_norm"], (1, ATT_HEADS))
    gk = jnp.tile(prm["att_k_norm"], (1, ATT_HEADS))
    gq2, gk2 = gq[:, :LANES], gk[:, :LANES]
    outs, lses = [], []
    for g, (window, dil) in enumerate(DIL_PATTERNS):
        o_g, l_g = _attn_fwd(qkv, gq2, gk2, g, dil)
        outs.append(o_g)
        lses.append(l_g)
    o_b, o_f, l_rep = _attn_combine_fwd(outs, lses)
    x1 = _matmul(o_b, prm["att_w_o"], mode="nn", addend=x0, name="att_out")
    sv = dict(x0=x0, h=h, qkv=qkv, gq=gq, gk=gk, gq2=gq2, gk2=gk2, o_b=o_b, o_f=o_f, l_rep=l_rep)
    return x1, sv


def _attn_mixer_bwd(dx1, prm, sv, grads):
    grads["att_w_o"] = _matmul_tn(sv["o_b"], dx1, name="d_att_w_o")
    do = _matmul(dx1, prm["att_w_o"], mode="nt", name="att_out_dx")
    dl_rep = _attn_combine_bwd(do, sv["o_f"])
    blocks, dgq, dgk = [], [], []
    for g, (window, dil) in enumerate(DIL_PATTERNS):
        dq, dgq_g = _attn_bwd_dq(sv["qkv"], sv["gq2"], sv["gk2"], do, sv["l_rep"], dl_rep, g, dil)
        dk, dv, dgk_g = _attn_bwd_dkv(sv["qkv"], sv["gq2"], sv["gk2"], do, _rows_by_residue(sv["l_rep"], dil),
                                      _rows_by_residue(dl_rep, dil), g, dil)
        blocks += [dq, dk, dv]
        dgq.append(dgq_g)
        dgk.append(dgk_g)
    dqkv = jnp.concatenate(blocks, axis=1)

    def fold(parts):
        return jnp.stack(parts)[:, :, 0].reshape(-1, ATT_HEAD_DIM).sum(axis=0)[None, :]

    grads["att_q_norm"] = fold(dgq)
    grads["att_k_norm"] = fold(dgk)
    grads["att_w_qkv"] = _matmul_tn(dqkv, sv["h"], name="d_att_w_qkv")
    dh = _matmul(dqkv, prm["att_w_qkv"], mode="nn", name="att_qkv_dx")
    dx0, dgain = _rmsnorm_bwd(sv["x0"], prm["norm_mix"][1:2], dh, dx1, name="mix_norm_bwd_1")
    grads["norm_mix"][1] = dgain[0]
    return dx0


def _local_step(x, p, target, prm, plan=None):
    plan = plan or _NoOverlap()
    grads = {k: [None, None] for k in ("norm_mix", "norm_ffn", "ffn_w_gate", "ffn_w_up", "ffn_w_down",
                                       "ple_w_proj", "ple_w_gate")}
    plan.begin_backward(grads)
    x1, sv_m = _mamba_fwd(x, prm, plan)
    x3, sv_f0 = _ffn_ple_fwd(x1, p[0], prm, 0, plan)
    x4, sv_a = _attn_mixer_fwd(x3, prm, plan)
    x6, sv_f1 = _ffn_ple_fwd(x4, p[1], prm, 1, plan)
    dy, loss_row = _loss_head(x6, target)
    dx4 = _ffn_ple_bwd(dy, p[1], prm, 1, sv_f1, grads, plan)
    dx3 = _attn_mixer_bwd(dx4, prm, sv_a, grads)
    dx1 = _ffn_ple_bwd(dx3, p[0], prm, 0, sv_f0, grads, plan)
    dx0 = _mamba_bwd(dx1, prm, sv_m, grads, plan)
    return loss_row, dx0, grads


W_IN_SLAB_ROWS = 1312


def _position():
    return lax.axis_index("x"), lax.axis_index("y"), lax.axis_index("c")


def _other_chips(x, y):
    return [(1 - x, y), (x, 1 - y), (1 - x, 1 - y)]


def _remote(send_sems, recv_sems, k, src, dst, to):
    return pltpu.make_async_remote_copy(src_ref=src, dst_ref=dst, send_sem=send_sems.at[k], recv_sem=recv_sems.at[k],
                                        device_id=to, device_id_type=MESH)


def _gather_side(entries, whole=()):
    n, nw = len(entries), len(whole)

    def first_hop(ins, outs, send_sems, recv_sems):
        x, y, c = _position()
        cps = []
        for j, chip in enumerate(_other_chips(x, y)):
            for e in range(n):
                cps.append(_remote(send_sems, recv_sems, 6 * e + j, ins[e].at[c], outs[e].at[2 * x + y, c], (*chip, c)))
            for e in range(nw):
                cps.append(_remote(send_sems, recv_sems, 6 * n + 3 * e + j, ins[n + e], outs[n + e].at[2 * x + y],
                                   (*chip, c)))
        return cps

    def start(ins, outs, send_sems, recv_sems):
        for cp in first_hop(ins, outs, send_sems, recv_sems):
            cp.start()

    def finish(ins, outs, send_sems, recv_sems):
        x, y, c = _position()
        me, sibling = (x, y, c), (x, y, 1 - c)
        chips = _other_chips(x, y)
        passed_on = []
        for j, (px, py) in enumerate(chips):
            for e in range(n):
                landed = outs[e].at[2 * px + py, c]
                _remote(send_sems, recv_sems, 6 * e + j, landed, landed, me).wait_recv()
                passed_on.append(_remote(send_sems, recv_sems, 6 * e + 3 + j, landed, landed, sibling))
                passed_on[-1].start()
            for e in range(nw):
                landed = outs[n + e].at[2 * px + py]
                _remote(send_sems, recv_sems, 6 * n + 3 * e + j, landed, landed, me).wait_recv()
        for j, (px, py) in enumerate(chips):
            for e in range(n):
                passed = outs[e].at[2 * px + py, 1 - c]
                _remote(send_sems, recv_sems, 6 * e + 3 + j, passed, passed, me).wait_recv()
        for cp in first_hop(ins, outs, send_sems, recv_sems) + passed_on:
            cp.wait_send()

    shapes = [jax.ShapeDtypeStruct((N_CHIPS,) + a.shape, a.dtype) for a in list(entries) + list(whole)]
    return _Side(list(entries) + list(whole), shapes, 6 * n + 3 * nw, start, finish)


def _run_side(side, name):
    si, so = len(side.inputs), len(side.out_shapes)

    def body(*refs):
        ins, outs, send_sems, recv_sems = refs[:si], refs[si:si + so], refs[-2], refs[-1]
        side.start(ins, outs, send_sems, recv_sems)
        side.finish(ins, outs, send_sems, recv_sems)

    side.outputs = list(pl.pallas_call(
        body, name=name, in_specs=[ANY] * si, out_specs=[ANY] * so, out_shape=side.out_shapes,
        scratch_shapes=[pltpu.SemaphoreType.DMA((side.n_sems,)), pltpu.SemaphoreType.DMA((side.n_sems,))],
    )(*side.inputs))
    return side.outputs


def _swap_side(grads):
    n = len(grads)

    def copies(ins, outs, send_sems, recv_sems):
        x, y, c = _position()
        return [_remote(send_sems, recv_sems, e, ins[e].at[:, 1 - c], outs[e], (x, y, 1 - c)) for e in range(n)]

    def start(ins, outs, send_sems, recv_sems):
        for cp in copies(ins, outs, send_sems, recv_sems):
            cp.start()

    def finish(ins, outs, send_sems, recv_sems):
        for cp in copies(ins, outs, send_sems, recv_sems):
            cp.wait()

    shapes = [jax.ShapeDtypeStruct((N_CHIPS,) + g.shape[2:], g.dtype) for g in grads]
    return _Side(grads, shapes, n, start, finish)


def _chip_exchange_side(chipsums):
    n = len(chipsums)

    def copies(ins, outs, send_sems, recv_sems):
        x, y, c = _position()
        return [_remote(send_sems, recv_sems, 3 * e + j, ins[e].at[2 * tx + ty], outs[e].at[j], (tx, ty, c))
                for j, (tx, ty) in enumerate(_other_chips(x, y)) for e in range(n)]

    def start(ins, outs, send_sems, recv_sems):
        for cp in copies(ins, outs, send_sems, recv_sems):
            cp.start()

    def finish(ins, outs, send_sems, recv_sems):
        for cp in copies(ins, outs, send_sems, recv_sems):
            cp.wait()

    shapes = [jax.ShapeDtypeStruct((3,) + cs.shape[1:], cs.dtype) for cs in chipsums]
    return _Side(chipsums, shapes, 3 * n, start, finish)


def _share_halves(totals):
    n = len(totals)

    def body(*refs):
        t_refs, r_refs = refs[:n], refs[n:2 * n]
        send_sems, recv_sems = refs[2 * n], refs[2 * n + 1]
        x, y, c = _position()
        cps = [pltpu.make_async_remote_copy(src_ref=t_refs[e], dst_ref=r_refs[e], send_sem=send_sems.at[e],
                                            recv_sem=recv_sems.at[e], device_id=(x, y, 1 - c), device_id_type=MESH)
               for e in range(n)]
        for cp in cps:
            cp.start()
        for cp in cps:
            cp.wait()

    return pl.pallas_call(
        body, name="grad_share_halves", in_specs=[ANY] * n, out_specs=[ANY] * n,
        out_shape=[jax.ShapeDtypeStruct(t.shape, t.dtype) for t in totals],
        scratch_shapes=[pltpu.SemaphoreType.DMA((n,)), pltpu.SemaphoreType.DMA((n,))],
    )(*totals)


def _reduce_rows(h):
    return h if h <= 704 else h // 2


def _add_sibling(grad, recv, c_idx, *, name):
    _, _, h, cw = grad.shape
    th = _reduce_rows(h)

    def body(c_ref, g_ref, r_ref, o_ref):
        o_ref[...] = (g_ref[...] + r_ref[...]).astype(BF16)

    return pl.pallas_call(
        body, name=name,
        grid_spec=pltpu.PrefetchScalarGridSpec(
            num_scalar_prefetch=1, grid=(N_CHIPS, h // th),
            in_specs=[pl.BlockSpec((None, None, th, cw), lambda s, i, c_ref: (s, c_ref[0], i, 0)),
                      pl.BlockSpec((None, th, cw), lambda s, i, c_ref: (s, i, 0))],
            out_specs=pl.BlockSpec((None, th, cw), lambda s, i, c_ref: (s, i, 0))),
        out_shape=jax.ShapeDtypeStruct((N_CHIPS, h, cw), BF16),
        compiler_params=_params("parallel", "parallel"),
    )(c_idx, grad, recv)


def _add_chips(chipsum, recv, s_idx, *, name):
    _, h, cw = chipsum.shape
    th = _reduce_rows(h)

    def body(s_ref, own_ref, r_ref, o_ref):
        o_ref[...] = ((own_ref[...].astype(F32) + r_ref[0].astype(F32)) + r_ref[1].astype(F32)) + r_ref[2].astype(F32)

    return pl.pallas_call(
        body, name=name,
        grid_spec=pltpu.PrefetchScalarGridSpec(
            num_scalar_prefetch=1, grid=(h // th,),
            in_specs=[pl.BlockSpec((None, th, cw), lambda i, s_ref: (s_ref[0], i, 0)),
                      pl.BlockSpec((3, th, cw), lambda i, s_ref: (0, i, 0))],
            out_specs=pl.BlockSpec((th, cw), lambda i, s_ref: (i, 0))),
        out_shape=jax.ShapeDtypeStruct((h, cw), F32),
        compiler_params=_params("parallel"),
    )(s_idx, chipsum, recv)


def _adamw_math(w, g, m, v):
    m = ADAM_B1 * m + (1.0 - ADAM_B1) * g
    v = ADAM_B2 * v + (1.0 - ADAM_B2) * (g * g)
    m_hat = m / (1.0 - ADAM_B1 ** ADAM_STEP)
    v_hat = v / (1.0 - ADAM_B2 ** ADAM_STEP)
    delta = -ADAM_LR * (m_hat / (jnp.sqrt(v_hat) + ADAM_EPS) + ADAM_WD * w)
    return delta, m, v


ADAM_TILE_ELEMS = 256 * 1024


def _adamw(w, g, m, v, *, name):
    shape = w.shape
    cols = shape[-1]
    rows = w.size // cols
    tr = rows
    for cand in range(8, rows, 8):
        if rows % cand == 0 and cand * cols <= ADAM_TILE_ELEMS:
            tr = cand
    if rows * cols <= ADAM_TILE_ELEMS:
        tr = rows

    def body(w_ref, g_ref, m_ref, v_ref, d_ref, nm_ref, nv_ref):
        d, nm, nv = _adamw_math(w_ref[...], g_ref[...], m_ref[...], v_ref[...])
        d_ref[...] = d
        nm_ref[...] = nm
        nv_ref[...] = nv

    blk = pl.BlockSpec((tr, cols), lambda i: (i, 0))
    sds = jax.ShapeDtypeStruct((rows, cols), F32)
    outs = pl.pallas_call(
        body, name=name, grid=(rows // tr,), in_specs=[blk] * 4, out_specs=[blk] * 3, out_shape=[sds] * 3,
        compiler_params=_params("parallel"),
    )(*[a.reshape(rows, cols) for a in (w, g, m, v)])
    return [o.reshape(shape) for o in outs]


SMALL_LAYOUT = (("loss", 1), ("norm_mix", 16), ("norm_ffn", 16), ("ssm_conv_b", 24), ("ssm_dt_bias", 1),
                ("ssm_a_log", 1), ("ssm_d_skip", 1), ("ssm_norm_w", 16), ("att_q_norm", 1), ("att_k_norm", 1),
                ("conv_w_full", 96))
SMALL_ROWS = 176
N_DEVICES = 8


def _small_pack(values):
    parts = []
    for name, rows in SMALL_LAYOUT:
        flat = values[name].reshape(-1).astype(F32)
        parts.append(jnp.pad(flat, (0, rows * LANES - flat.shape[0])).reshape(rows, LANES))
    used = sum(r for _, r in SMALL_LAYOUT)
    parts.append(jnp.zeros((SMALL_ROWS - used, LANES), F32))
    return jnp.concatenate(parts, axis=0)


def _small_unpack(pack, shapes):
    out, off = {}, 0
    for name, rows in SMALL_LAYOUT:
        shape = shapes[name]
        n = math.prod(shape)
        out[name] = pack[off:off + rows].reshape(-1)[:n].reshape(shape)
        off += rows
    return out


def _small_allreduce_adamw(g, w, m, v):
    def body(g_ref, w_ref, m_ref, v_ref, gs_ref, d_ref, nm_ref, nv_ref, buf, send_sems, recv_sems):
        x, y, c = _position()
        pos = (x, y, c)
        me = 4 * x + 2 * y + c
        buf[me] = g_ref[...]
        peers = []
        for k in range(1, N_DEVICES):
            bits = ((k >> 2) & 1, (k >> 1) & 1, k & 1)
            peers.append(tuple(1 - p if b else p for p, b in zip(pos, bits)))
        cps = [pltpu.make_async_remote_copy(src_ref=g_ref, dst_ref=buf.at[me], send_sem=send_sems.at[k],
                                            recv_sem=recv_sems.at[k], device_id=peer, device_id_type=MESH)
               for k, peer in enumerate(peers)]
        for cp in cps:
            cp.start()
        for k, (px, py, pc) in enumerate(peers):
            pltpu.make_async_remote_copy(src_ref=g_ref, dst_ref=buf.at[4 * px + 2 * py + pc],
                                         send_sem=send_sems.at[k], recv_sem=recv_sems.at[k],
                                         device_id=(px, py, pc), device_id_type=MESH).wait_recv()
        for cp in cps:
            cp.wait_send()
        total = buf[0]
        for dev in range(1, N_DEVICES):
            total = total + buf[dev]
        gs_ref[...] = total
        d, nm, nv = _adamw_math(w_ref[...], total, m_ref[...], v_ref[...])
        d_ref[...] = d
        nm_ref[...] = nm
        nv_ref[...] = nv

    vm = pl.BlockSpec(memory_space=pltpu.VMEM)
    sds = jax.ShapeDtypeStruct((SMALL_ROWS, LANES), F32)
    return pl.pallas_call(
        body, name="small_allreduce_adamw", in_specs=[vm] * 4, out_specs=[vm] * 4, out_shape=[sds] * 4,
        scratch_shapes=[pltpu.VMEM((N_DEVICES, SMALL_ROWS, LANES), F32),
                        pltpu.SemaphoreType.DMA((N_DEVICES - 1,)), pltpu.SemaphoreType.DMA((N_DEVICES - 1,))],
    )(g, w, m, v)


SMALL = tuple(n for n, _ in SMALL_LAYOUT if n not in ("loss", "conv_w_full"))
WEIGHTS = ("norm_mix", "norm_ffn", "ssm_w_in", "ssm_conv_w", "ssm_conv_b", "ssm_dt_bias", "ssm_a_log", "ssm_d_skip",
           "ssm_norm_w", "ssm_w_out", "att_w_qkv", "att_q_norm", "att_k_norm", "att_w_o", "ffn_w_gate", "ffn_w_up",
           "ffn_w_down", "ple_w_proj", "ple_w_gate")
COLUMN_SHARDED = ("ssm_w_in", "att_w_qkv", "ffn_w_gate", "ffn_w_up", "ple_w_proj")
LAYERED = ("ffn_w_gate", "ffn_w_up", "ffn_w_down", "ple_w_proj", "ple_w_gate")
GATHER_ORDER = ("ssm_w_in", "ssm_w_out", "att_w_qkv", "att_w_o", "ffn_w_gate", "ffn_w_up", "ffn_w_down",
                "ple_w_proj", "ple_w_gate")


def _layers(n):
    return (0, 1) if n in LAYERED else (None,)


def _tag(key):
    return key[0] if key[1] is None else f"{key[0]}_{key[1]}"


QKV_PARTS = 3


def _weight_slab(w, key):
    n, i = key
    if n == "att_w_qkv":
        a = w[n][0].T
        rows = a.shape[0] // QKV_PARTS
        a = a[i * rows:(i + 1) * rows]
    else:
        a = w[n][0 if i is None else i]
        a = a.T if n in COLUMN_SHARDED else a
    if n == "ssm_w_in":
        a = jnp.pad(a, ((0, W_IN_SLAB_ROWS - a.shape[0]), (0, 0)))
    return a.reshape(2, a.shape[0] // 2, a.shape[1]).astype(BF16)


def _install(prm, key, gathered, own, s_me):
    n, i = key
    full = lax.dynamic_update_slice(gathered, own[None], (s_me, 0, 0, 0))
    full = full.reshape(N_CHIPS, 2 * full.shape[2], full.shape[3])
    if n == "att_w_qkv":
        parts = prm.setdefault("att_w_qkv_parts", {})
        parts[i] = full
        if len(parts) == QKV_PARTS:
            prm[n] = jnp.stack([parts[j] for j in range(QKV_PARTS)], axis=1).reshape(-1, D_MODEL)
        return
    if n == "ssm_w_in":
        rows = (D_INNER + CONV_DIM + SSM_HEADS) // N_CHIPS
        w_in_t = full[:, :rows].reshape(N_CHIPS * rows, D_MODEL)
        prm["ssm_w_z"] = w_in_t[:D_INNER]
        prm["ssm_w_xbc"] = w_in_t[D_INNER:D_INNER + CONV_DIM]
        prm["ssm_w_dt"] = jnp.pad(w_in_t[D_INNER + CONV_DIM:], ((0, LANES - SSM_HEADS), (0, 0)))
        return
    full = full.reshape(N_CHIPS * full.shape[1], full.shape[2])
    if i is None:
        prm[n] = full
    else:
        prm.setdefault(n, [None, None])[i] = full


def _grad_slab(grads, key):
    n, i = key
    g = grads[n] if i is None else grads[n][i]
    if n == "ssm_w_in":
        g = jnp.pad(g.reshape(N_CHIPS, g.shape[0] // N_CHIPS, D_MODEL),
                    ((0, 0), (0, W_IN_SLAB_ROWS - g.shape[0] // N_CHIPS), (0, 0)))
    rows = g.size // (N_CHIPS * g.shape[-1])
    return g.reshape(N_CHIPS, 2, rows // 2, g.shape[-1])


def _natural_shard(n, reduced, shape):
    def one(r):
        if n == "ssm_w_in":
            r = r[:shape[-1]]
        return r.T if n in COLUMN_SHARDED else r
    if n in LAYERED:
        return jnp.stack([one(r) for r in reduced]).reshape(shape)
    return one(reduced[0]).reshape(shape)


def kernel(x, p, norm_mix, norm_ffn, ssm_w_in, ssm_conv_w, ssm_conv_b, ssm_dt_bias, ssm_a_log, ssm_d_skip, ssm_norm_w, ssm_w_out, att_w_qkv, att_q_norm, att_k_norm, att_w_o, ffn_w_gate, ffn_w_up, ffn_w_down, ple_w_proj, ple_w_gate, loss_target, m_norm_mix, m_norm_ffn, m_ssm_w_in, m_ssm_conv_w, m_ssm_conv_b, m_ssm_dt_bias, m_ssm_a_log, m_ssm_d_skip, m_ssm_norm_w, m_ssm_w_out, m_att_w_qkv, m_att_q_norm, m_att_k_norm, m_att_w_o, m_ffn_w_gate, m_ffn_w_up, m_ffn_w_down, m_ple_w_proj, m_ple_w_gate, v_norm_mix, v_norm_ffn, v_ssm_w_in, v_ssm_conv_w, v_ssm_conv_b, v_ssm_dt_bias, v_ssm_a_log, v_ssm_d_skip, v_ssm_norm_w, v_ssm_w_out, v_att_w_qkv, v_att_q_norm, v_att_k_norm, v_att_w_o, v_ffn_w_gate, v_ffn_w_up, v_ffn_w_down, v_ple_w_proj, v_ple_w_gate):
    given = dict(locals())
    w = {n: given[n] for n in WEIGHTS}
    m = {n: given["m_" + n] for n in WEIGHTS}
    v = {n: given["v_" + n] for n in WEIGHTS}
    c_idx = lax.axis_index("c").astype(jnp.int32).reshape(1)
    s_idx = (2 * lax.axis_index("x") + lax.axis_index("y")).astype(jnp.int32).reshape(1)

    s_me = 2 * lax.axis_index("x") + lax.axis_index("y")
    first_core = lax.axis_index("c") == 0

    qkv_parts = [("att_w_qkv", j) for j in range(QKV_PARTS)]
    gather_plan = {
        "ssm_in_z": [("ssm_w_out", None)],
        "ssm_in_xbc": [("ffn_w_gate", 0)],
        "conv_fwd": [("ffn_w_up", 0)],
        "ssd_fwd": [("ffn_w_down", 0), ("ple_w_proj", 0), ("ple_w_gate", 0), ("att_w_o", None)],
        "swiglu_fwd_0": qkv_parts[:2],
        "ffn_down_0": qkv_parts[2:],
        "att_qkv": [(n, 1) for n in LAYERED],
    }
    mamba = [("ssm_w_in", None)]
    own = {k: _weight_slab(w, k) for k in mamba + sum(gather_plan.values(), [])}
    prm = {n: w[n] for n in SMALL}

    def land(group, outputs):
        for k, g in zip(group, outputs):
            _install(prm, k, g, own[k], s_me)

    first = _gather_side([own[k] for k in mamba], whole=[ssm_conv_w[0]])
    _run_side(first, "gather_mamba")
    land(mamba, first.outputs)
    conv = lax.dynamic_update_slice(first.outputs[-1], ssm_conv_w, (s_me, 0, 0))
    prm["ssm_conv_w"] = conv.transpose(1, 0, 2).reshape(CONV_WIDTH, CONV_DIM)

    layer1 = [("att_w_qkv", None), ("att_w_o", None)] + [(n, 1) for n in LAYERED]
    ffn0 = [(n, 0) for n in LAYERED] + [("ssm_w_out", None)]
    reduce_plan = {"swiglu_bwd_0": ("swap", layer1), "ssd_bwd": ("exchange", layer1),
                   "gate_norm_bwd": ("swap", ffn0), "conv_bwd": ("exchange", ffn0)}
    state = {}

    def swap_side(group):
        state[_tag(group[0]), "g4"] = g4 = [_grad_slab(state["grads"], k) for k in group]
        return _swap_side(g4)

    def add_siblings(group, from_sibling):
        state[_tag(group[0]), "chipsums"] = [
            _add_sibling(g, r, c_idx, name="add_sibling_" + _tag(k))
            for g, r, k in zip(state[_tag(group[0]), "g4"], from_sibling, group)]

    def exchange_side(group):
        return _chip_exchange_side(state[_tag(group[0]), "chipsums"])

    def add_chips(group, from_chips):
        for k, cs, r in zip(group, state[_tag(group[0]), "chipsums"], from_chips):
            state["total", k] = _add_chips(cs, r, s_idx, name="add_chips_" + _tag(k))

    class Plan(_NoOverlap):
        def __init__(self):
            self.carried = {host: _gather_side([own[k] for k in group]) for host, group in gather_plan.items()}

        def begin_backward(self, grads):
            state["grads"] = grads

        def side(self, host):
            if host in reduce_plan:
                step, group = reduce_plan[host]
                self.carried[host] = swap_side(group) if step == "swap" else exchange_side(group)
            return self.carried.get(host)

        def after(self, host):
            if host in gather_plan:
                land(gather_plan[host], self.carried[host].outputs)
            elif host in reduce_plan:
                step, group = reduce_plan[host]
                (add_siblings if step == "swap" else add_chips)(group, self.carried[host].outputs)

    loss_row, dx, grads = _local_step(x[0], p[:, 0], loss_target[0], prm, Plan())

    add_siblings(mamba, _run_side(swap_side(mamba), "grad_swap_mamba"))
    add_chips(mamba, _run_side(exchange_side(mamba), "grad_exchange_mamba"))
    order = mamba + ffn0 + layer1
    shared = _share_halves([state["total", k] for k in order])
    reduced = {}
    for k, theirs in zip(order, shared):
        lo = jnp.where(first_core, state["total", k], theirs)
        hi = jnp.where(first_core, theirs, state["total", k])
        reduced.setdefault(k[0], {})[k[1]] = jnp.concatenate([lo, hi], axis=0)
    reduced = {n: [by_layer[i] for i in _layers(n)] for n, by_layer in reduced.items()}

    grad, delta, new_m, new_v = {}, {}, {}, {}
    for n in GATHER_ORDER:
        grad[n] = _natural_shard(n, reduced[n], w[n].shape)
        delta[n], new_m[n], new_v[n] = _adamw(w[n], grad[n], m[n], v[n], name="adamw_" + n)

    small_g = {n: (jnp.stack(grads[n]) if isinstance(grads[n], list) else grads[n]) for n in SMALL}
    small_g["loss"] = loss_row
    small_g["conv_w_full"] = grads["ssm_conv_w"]
    zero = {"loss": jnp.zeros((1, LANES), F32), "conv_w_full": jnp.zeros((CONV_WIDTH, CONV_DIM), F32)}
    outs = _small_allreduce_adamw(_small_pack(small_g), _small_pack({**w, **zero}), _small_pack({**m, **zero}),
                                  _small_pack({**v, **zero}))
    shapes = {n: w[n].shape for n in SMALL}
    shapes["loss"] = (1, LANES)
    shapes["conv_w_full"] = (CONV_WIDTH, CONV_DIM)
    sg, sd, sm, sv = [_small_unpack(o, shapes) for o in outs]
    for n in SMALL:
        grad[n], delta[n], new_m[n], new_v[n] = sg[n], sd[n], sm[n], sv[n]
    loss = sg["loss"][0, 0]
    conv_cols = CONV_DIM // N_CHIPS
    grad["ssm_conv_w"] = lax.dynamic_slice(sg["conv_w_full"], (0, s_me * conv_cols), (CONV_WIDTH, conv_cols))[None]
    delta["ssm_conv_w"], new_m["ssm_conv_w"], new_v["ssm_conv_w"] = _adamw(
        ssm_conv_w, grad["ssm_conv_w"], m_ssm_conv_w, v_ssm_conv_w, name="adamw_ssm_conv_w")

    return (loss, dx[None], *[grad[n] for n in WEIGHTS], *[delta[n] for n in WEIGHTS],
            *[new_m[n] for n in WEIGHTS], *[new_v[n] for n in WEIGHTS])
```

```python
import functools
import math

import jax
import jax.numpy as jnp
from jax import lax
from jax.experimental import pallas as pl
from jax.experimental.pallas import tpu as pltpu

F32 = jnp.float32
BF16 = jnp.bfloat16
HIGHEST = lax.Precision.HIGHEST

NORM_EPS = 1e-6
ADAM_LR, ADAM_B1, ADAM_B2, ADAM_EPS, ADAM_WD, ADAM_STEP = 0.001, 0.9, 0.999, 1e-08, 0.01, 10

D_MODEL = 1024
D_INNER = 2048
SSM_HEADS = 32
SSM_HEAD_DIM = 64
SSM_GROUPS = 4
SSM_STATE = 128
SSD_CHUNK = 128
CONV_DIM = 3072
CONV_WIDTH = 4
ATT_HEADS = 16
ATT_HEAD_DIM = 64
DIL_PATTERNS = ((128, 1), (512, 4), (2048, 16))
ATT_BLOCK = 128
FFN_HIDDEN = 2816
PLE_DIM = 256

LANES = 128
V7X_VMEM_LIMIT = 56 * 1024 * 1024
NEG_BIG = -1e30

N_CHIPS = 4


def _params(*sem):
    return pltpu.CompilerParams(dimension_semantics=sem, vmem_limit_bytes=V7X_VMEM_LIMIT)


def _tile(n, pref):
    if n <= pref:
        return n
    best = None
    for t in range(LANES, pref + 1, LANES):
        if n % t == 0:
            best = t
    assert best is not None, (n, pref)
    return best


def _sigmoid(v):
    return 1.0 / (1.0 + jnp.exp(-v))


def _dot(a, b):
    return jnp.dot(a, b, preferred_element_type=F32)


def _dot_nt(a, b):
    return lax.dot_general(a, b, (((1,), (1,)), ((), ())), preferred_element_type=F32)


def _dot_tn(a, b):
    return lax.dot_general(a, b, (((0,), (0,)), ((), ())), preferred_element_type=F32)


def _head_block_diag():
    i = lax.broadcasted_iota(jnp.int32, (LANES, LANES), 0) // ATT_HEAD_DIM
    j = lax.broadcasted_iota(jnp.int32, (LANES, LANES), 1) // ATT_HEAD_DIM
    return (i == j).astype(BF16)


def _split_dot(ones, z):
    hi = z.astype(BF16)
    lo = (z - hi.astype(F32)).astype(BF16)
    return _dot(ones, hi) + _dot(ones, lo)


def _head_sums(z, bd):
    hi = z.astype(BF16)
    lo = (z - hi.astype(F32)).astype(BF16)
    parts = []
    for t in range(z.shape[1] // LANES):
        sl = slice(t * LANES, (t + 1) * LANES)
        parts.append(_dot(hi[:, sl], bd) + _dot(lo[:, sl], bd))
    return parts[0] if len(parts) == 1 else jnp.concatenate(parts, axis=1)


def _lane_lt64(rows):
    return lax.broadcasted_iota(jnp.int32, (rows, LANES), 1) < ATT_HEAD_DIM


MESH = pl.DeviceIdType.MESH
ANY = pl.BlockSpec(memory_space=pl.ANY)


class _Side:
    def __init__(self, inputs, out_shapes, n_sems, start, finish):
        self.inputs, self.out_shapes, self.n_sems = list(inputs), list(out_shapes), n_sems
        self.start, self.finish = start, finish
        self.outputs = None


def _call(body, side, *, name, grid, in_specs, out_specs, out_shape, scratch_shapes, semantics, args):
    in_specs, out_specs, out_shape = list(in_specs), list(out_specs), list(out_shape)
    scratch_shapes = list(scratch_shapes)
    if side is None:
        return pl.pallas_call(body, name=name, grid=grid, in_specs=in_specs, out_specs=out_specs,
                              out_shape=out_shape, scratch_shapes=scratch_shapes,
                              compiler_params=_params(*semantics))(*args)
    ni, no, ns = len(in_specs), len(out_specs), len(scratch_shapes)
    si, so = len(side.inputs), len(side.out_shapes)

    def hosted(*refs):
        ins, s_ins = refs[:ni], refs[ni:ni + si]
        outs, s_outs = refs[ni + si:ni + si + no], refs[ni + si + no:ni + si + no + so]
        scratch = refs[ni + si + no + so:ni + si + no + so + ns]
        send_sems, recv_sems = refs[-2], refs[-1]
        first = pl.program_id(0) == 0
        last = pl.program_id(0) == grid[0] - 1
        for axis in range(1, len(grid)):
            first = jnp.logical_and(first, pl.program_id(axis) == 0)
            last = jnp.logical_and(last, pl.program_id(axis) == grid[axis] - 1)

        @pl.when(first)
        def _():
            side.start(s_ins, s_outs, send_sems, recv_sems)

        body(*ins, *outs, *scratch)

        @pl.when(last)
        def _():
            side.finish(s_ins, s_outs, send_sems, recv_sems)

    res = pl.pallas_call(
        hosted, name=name, grid=grid, in_specs=in_specs + [ANY] * si, out_specs=out_specs + [ANY] * so,
        out_shape=out_shape + side.out_shapes,
        scratch_shapes=scratch_shapes + [pltpu.SemaphoreType.DMA((side.n_sems,)),
                                         pltpu.SemaphoreType.DMA((side.n_sems,))],
        compiler_params=_params(*["arbitrary"] * len(grid)),
    )(*args, *side.inputs)
    side.outputs = list(res[no:])
    return list(res[:no])


def _matmul(a, b, *, mode, name, out_dtype=F32, addend=None, tm=1024, tn=512, tk_max=3072, side=None, second=None):
    m, k = a.shape
    if mode == "nn":
        k2, n = b.shape
    else:
        n, k2 = b.shape
    assert k == k2, (a.shape, b.shape, mode)
    tm, tn, tk = _tile(m, tm), _tile(n, tn), _tile(k, tk_max)
    nk = k // tk
    has_add = addend is not None
    n_rows = len(second[1]) if second else 0
    n_out = 2 if second else 1

    def body(*refs):
        a_ref, b_ref = refs[0], refs[1]
        add_ref = refs[2] if has_add else None
        row_refs = refs[2 + has_add:2 + has_add + n_rows]
        o_ref, acc_ref = refs[-1 - n_out], refs[-1]
        kk = pl.program_id(2)
        col_tile = pl.program_id(1)
        av = a_ref[...].astype(BF16)
        bv = b_ref[...].astype(BF16)
        part = _dot(av, bv) if mode == "nn" else _dot_nt(av, bv)

        @pl.when(kk == 0)
        def _():
            acc_ref[...] = part

        @pl.when(kk > 0)
        def _():
            acc_ref[...] += part

        @pl.when(kk == nk - 1)
        def _():
            res = acc_ref[...]
            if has_add:
                res = res + add_ref[...]
            o_ref[...] = res.astype(out_dtype)
            if second:
                refs[-2][...] = second[0](res, col_tile, *row_refs)

    a_spec = pl.BlockSpec((tm, tk), lambda i, j, kk: (i, kk))
    if mode == "nn":
        b_spec = pl.BlockSpec((tk, tn), lambda i, j, kk: (kk, j))
    else:
        b_spec = pl.BlockSpec((tn, tk), lambda i, j, kk: (j, kk))
    tile = pl.BlockSpec((tm, tn), lambda i, j, kk: (i, j))
    in_specs = [a_spec, b_spec]
    args = [a, b]
    if has_add:
        in_specs.append(tile)
        args.append(addend)
    if second:
        in_specs += [pl.BlockSpec((1, tn), lambda i, j, kk: (0, j))] * n_rows
        args += list(second[1])
    outs = _call(
        body, side, name=name, grid=(m // tm, n // tn, nk),
        in_specs=in_specs, out_specs=[tile] * n_out,
        out_shape=[jax.ShapeDtypeStruct((m, n), out_dtype)] + [jax.ShapeDtypeStruct((m, n), F32)] * (n_out - 1),
        scratch_shapes=[pltpu.VMEM((tm, tn), F32)],
        semantics=("parallel", "parallel", "arbitrary"), args=args,
    )
    return outs if second else outs[0]


def _matmul_tn(a, b, *, name, tm=1408, tn=512, tk=1024):
    t, m = a.shape
    t2, n = b.shape
    assert t == t2
    tm, tn, tk = _tile(m, tm), _tile(n, tn), _tile(t, tk)

    def body(a_ref, b_ref, o_ref):
        part = _dot_tn(a_ref[...].astype(BF16), b_ref[...].astype(BF16))

        @pl.when(pl.program_id(2) == 0)
        def _():
            o_ref[...] = part

        @pl.when(pl.program_id(2) > 0)
        def _():
            o_ref[...] += part

    return pl.pallas_call(
        body, name=name, grid=(m // tm, n // tn, t // tk),
        in_specs=[pl.BlockSpec((tk, tm), lambda i, j, kk: (kk, i)),
                  pl.BlockSpec((tk, tn), lambda i, j, kk: (kk, j))],
        out_specs=pl.BlockSpec((tm, tn), lambda i, j, kk: (i, j)),
        out_shape=jax.ShapeDtypeStruct((m, n), F32),
        compiler_params=_params("parallel", "parallel", "arbitrary"),
    )(a, b)


def _rmsnorm_fwd(x, gain, *, name):
    t, d = x.shape
    tm = _tile(t, 512)

    def body(x_ref, g_ref, o_ref):
        xv = x_ref[...]
        r = lax.rsqrt(jnp.mean(xv * xv, axis=-1, keepdims=True) + NORM_EPS)
        o_ref[...] = (xv * r * g_ref[...]).astype(BF16)

    return pl.pallas_call(
        body, name=name, grid=(t // tm,),
        in_specs=[pl.BlockSpec((tm, d), lambda i: (i, 0)), pl.BlockSpec((1, d), lambda i: (0, 0))],
        out_specs=pl.BlockSpec((tm, d), lambda i: (i, 0)),
        out_shape=jax.ShapeDtypeStruct((t, d), BF16),
        compiler_params=_params("parallel"),
    )(x, gain)


def _rmsnorm_bwd(x, gain, dy, dres, *, name):
    t, d = x.shape
    tm = _tile(t, 512)

    def body(x_ref, g_ref, dy_ref, dres_ref, dx_ref, dg_ref):
        xv = x_ref[...]
        r = lax.rsqrt(jnp.mean(xv * xv, axis=-1, keepdims=True) + NORM_EPS)
        xh = xv * r
        dyv = dy_ref[...]
        dxh = dyv * g_ref[...]
        mean = jnp.mean(dxh * xh, axis=-1, keepdims=True)
        dx_ref[...] = dres_ref[...] + r * (dxh - xh * mean)
        part = jnp.sum(dyv * xh, axis=0, keepdims=True)

        @pl.when(pl.program_id(0) == 0)
        def _():
            dg_ref[...] = part

        @pl.when(pl.program_id(0) > 0)
        def _():
            dg_ref[...] += part

    row = pl.BlockSpec((tm, d), lambda i: (i, 0))
    vec = pl.BlockSpec((1, d), lambda i: (0, 0))
    return pl.pallas_call(
        body, name=name, grid=(t // tm,),
        in_specs=[row, vec, row, row], out_specs=[row, vec],
        out_shape=[jax.ShapeDtypeStruct((t, d), F32), jax.ShapeDtypeStruct((1, d), F32)],
        compiler_params=_params("arbitrary"),
    )(x, gain, dy, dres)


def _loss_head(y, target):
    t, d = y.shape
    tm = _tile(t, 512)
    steps = t // tm

    def body(y_ref, t_ref, dy_ref, l_ref, acc_ref):
        e = y_ref[...] - t_ref[...]
        dy_ref[...] = e * (1.0 / d)
        part = jnp.sum(e * e, axis=0, keepdims=True)

        @pl.when(pl.program_id(0) == 0)
        def _():
            acc_ref[...] = part

        @pl.when(pl.program_id(0) > 0)
        def _():
            acc_ref[...] += part

        @pl.when(pl.program_id(0) == steps - 1)
        def _():
            l_ref[...] = jnp.full((1, LANES), (0.5 / d), F32) * jnp.sum(acc_ref[...])

    row = pl.BlockSpec((tm, d), lambda i: (i, 0))
    return pl.pallas_call(
        body, name="loss_head", grid=(steps,),
        in_specs=[row, row], out_specs=[row, pl.BlockSpec((1, LANES), lambda i: (0, 0))],
        out_shape=[jax.ShapeDtypeStruct((t, d), F32), jax.ShapeDtypeStruct((1, LANES), F32)],
        scratch_shapes=[pltpu.VMEM((1, d), F32)],
        compiler_params=_params("arbitrary"),
    )(y, target)


def _swiglu_fwd(h, w_gate_t, w_up_t, *, name, side=None):
    t, d = h.shape
    f = w_gate_t.shape[0]
    tm, tn = _tile(t, 1024), _tile(f, 256)

    def body(h_ref, wg_ref, wu_ref, g_ref, u_ref, a_ref):
        hv = h_ref[...]
        g = _dot_nt(hv, wg_ref[...])
        u = _dot_nt(hv, wu_ref[...])
        g_ref[...] = g.astype(BF16)
        u_ref[...] = u.astype(BF16)
        a_ref[...] = (g * _sigmoid(g) * u).astype(BF16)

    wspec = pl.BlockSpec((tn, d), lambda i, j: (j, 0))
    ospec = pl.BlockSpec((tm, tn), lambda i, j: (i, j))
    return _call(
        body, side, name=name, grid=(t // tm, f // tn),
        in_specs=[pl.BlockSpec((tm, d), lambda i, j: (i, 0)), wspec, wspec],
        out_specs=[ospec, ospec, ospec],
        out_shape=[jax.ShapeDtypeStruct((t, f), BF16), jax.ShapeDtypeStruct((t, f), BF16),
                   jax.ShapeDtypeStruct((t, f), BF16)],
        scratch_shapes=[], semantics=("parallel", "parallel"), args=(h, w_gate_t, w_up_t),
    )


def _swiglu_bwd(dx, w_down, g, u, *, name, side=None):
    t, d = dx.shape
    f = w_down.shape[0]
    tm, tn = _tile(t, 1024), _tile(f, 256)

    def body(dx_ref, wd_ref, g_ref, u_ref, dg_ref, du_ref):
        dact = _dot_nt(dx_ref[...].astype(BF16), wd_ref[...])
        gv, uv = g_ref[...].astype(F32), u_ref[...].astype(F32)
        sg = _sigmoid(gv)
        dg_ref[...] = (dact * uv * sg * (1.0 + gv * (1.0 - sg))).astype(BF16)
        du_ref[...] = (dact * gv * sg).astype(BF16)

    ospec = pl.BlockSpec((tm, tn), lambda i, j: (i, j))
    return _call(
        body, side, name=name, grid=(t // tm, f // tn),
        in_specs=[pl.BlockSpec((tm, d), lambda i, j: (i, 0)), pl.BlockSpec((tn, d), lambda i, j: (j, 0)),
                  ospec, ospec],
        out_specs=[ospec, ospec],
        out_shape=[jax.ShapeDtypeStruct((t, f), BF16), jax.ShapeDtypeStruct((t, f), BF16)],
        scratch_shapes=[], semantics=("parallel", "parallel"), args=(dx, w_down, g, u),
    )


def _ple_fwd(x, p, w_gate, w_proj_t, *, name):
    t, d = x.shape
    e = p.shape[1]
    tm, tn = _tile(t, 1024), _tile(d, 512)

    def body(xf_ref, xr_ref, p_ref, wg_ref, wp_ref, o_ref):
        s = _dot(xf_ref[...].astype(BF16), wg_ref[...])
        ple = _dot_nt(p_ref[...].astype(BF16), wp_ref[...])
        o_ref[...] = xr_ref[...] + _sigmoid(s) * ple

    return pl.pallas_call(
        body, name=name, grid=(t // tm, d // tn),
        in_specs=[pl.BlockSpec((tm, d), lambda i, j: (i, 0)), pl.BlockSpec((tm, tn), lambda i, j: (i, j)),
                  pl.BlockSpec((tm, e), lambda i, j: (i, 0)), pl.BlockSpec((d, tn), lambda i, j: (0, j)),
                  pl.BlockSpec((tn, e), lambda i, j: (j, 0))],
        out_specs=pl.BlockSpec((tm, tn), lambda i, j: (i, j)),
        out_shape=jax.ShapeDtypeStruct((t, d), F32),
        compiler_params=_params("parallel", "parallel"),
    )(x, x, p, w_gate, w_proj_t)


def _ple_bwd(x, p, w_gate, w_proj_t, dout, *, name):
    t, d = x.shape
    e = p.shape[1]
    tm, tn = _tile(t, 1024), _tile(d, 512)

    def body(xf_ref, p_ref, wg_ref, wp_ref, do_ref, ds_ref, dple_ref):
        s = _dot(xf_ref[...].astype(BF16), wg_ref[...])
        ple = _dot_nt(p_ref[...].astype(BF16), wp_ref[...])
        gate = _sigmoid(s)
        dov = do_ref[...]
        dple_ref[...] = (dov * gate).astype(BF16)
        ds_ref[...] = (dov * ple * gate * (1.0 - gate)).astype(BF16)

    ospec = pl.BlockSpec((tm, tn), lambda i, j: (i, j))
    return pl.pallas_call(
        body, name=name, grid=(t // tm, d // tn),
        in_specs=[pl.BlockSpec((tm, d), lambda i, j: (i, 0)), pl.BlockSpec((tm, e), lambda i, j: (i, 0)),
                  pl.BlockSpec((d, tn), lambda i, j: (0, j)), pl.BlockSpec((tn, e), lambda i, j: (j, 0)), ospec],
        out_specs=[ospec, ospec],
        out_shape=[jax.ShapeDtypeStruct((t, d), BF16), jax.ShapeDtypeStruct((t, d), BF16)],
        compiler_params=_params("parallel", "parallel"),
    )(x, p, w_gate, w_proj_t, dout)


CONV_TIME_TILE = 256
CONV_HALO = 8


def _conv_taps(ext, w):
    acc = ext[CONV_HALO:, :] * w[CONV_WIDTH - 1:CONV_WIDTH, :]
    shifted = [ext[CONV_HALO:, :]]
    for j in range(1, CONV_WIDTH):
        sh = pltpu.roll(ext, j, 0)[CONV_HALO:, :]
        shifted.append(sh)
        acc = acc + sh * w[CONV_WIDTH - 1 - j:CONV_WIDTH - j, :]
    return acc, shifted


def _conv_fwd(u, w, b, side=None):
    t, c = u.shape
    tc = _tile(c, 256)
    tt = CONV_TIME_TILE

    def body(u_ref, w_ref, b_ref, o_ref):
        wv, bv = w_ref[...], b_ref[...]

        def tile(start, ext):
            pre = _conv_taps(ext, wv)[0] + bv
            o_ref[pl.ds(start, tt), :] = pre * _sigmoid(pre)

        tile(0, jnp.concatenate([jnp.zeros((CONV_HALO, tc), F32), u_ref[0:tt, :]], axis=0))

        def loop(i, carry):
            start = pl.multiple_of(i * tt, tt)
            tile(start, u_ref[pl.ds(start - CONV_HALO, tt + CONV_HALO), :])
            return carry

        lax.fori_loop(1, t // tt, loop, 0)

    col = pl.BlockSpec((t, tc), lambda j: (0, j))
    return _call(
        body, side, name="conv_fwd", grid=(c // tc,),
        in_specs=[col, pl.BlockSpec((CONV_WIDTH, tc), lambda j: (0, j)), pl.BlockSpec((1, tc), lambda j: (0, j))],
        out_specs=[col], out_shape=[jax.ShapeDtypeStruct((t, c), F32)],
        scratch_shapes=[], semantics=("parallel",), args=(u, w, b),
    )[0]


def _conv_bwd(u, w, b, dact, side=None):
    t, c = u.shape
    tc = _tile(c, 256)
    tt = CONV_TIME_TILE

    def body(u_ref, w_ref, b_ref, da_ref, du_ref, dw_ref, db_ref, dpre_ref):
        wv, bv = w_ref[...], b_ref[...]

        def tile(start, ext, sums):
            acc, shifted = _conv_taps(ext, wv)
            pre = acc + bv
            sg = _sigmoid(pre)
            dpre = da_ref[pl.ds(start, tt), :] * (sg * (1.0 + pre * (1.0 - sg)))
            dpre_ref[pl.ds(start, tt), :] = dpre
            new = [sums[0] + jnp.sum(dpre, axis=0, keepdims=True)]
            for j in range(CONV_WIDTH):
                new.append(sums[1 + j] + jnp.sum(dpre * shifted[j], axis=0, keepdims=True))
            return tuple(new)

        zero = jnp.zeros((1, tc), F32)
        sums = tile(0, jnp.concatenate([jnp.zeros((CONV_HALO, tc), F32), u_ref[0:tt, :]], axis=0),
                    (zero,) * (1 + CONV_WIDTH))

        def loop(i, sums):
            start = pl.multiple_of(i * tt, tt)
            return tile(start, u_ref[pl.ds(start - CONV_HALO, tt + CONV_HALO), :], sums)

        sums = lax.fori_loop(1, t // tt, loop, sums)
        db_ref[...] = sums[0]
        dw_ref[...] = jnp.concatenate([sums[1 + (CONV_WIDTH - 1 - k)] for k in range(CONV_WIDTH)], axis=0)
        dpre_ref[pl.ds(t, CONV_HALO), :] = jnp.zeros((CONV_HALO, tc), F32)

        def loop2(i, carry):
            start = pl.multiple_of(i * tt, tt)
            ext = dpre_ref[pl.ds(start, tt + CONV_HALO), :]
            acc = ext[0:tt, :] * wv[CONV_WIDTH - 1:CONV_WIDTH, :]
            for j in range(1, CONV_WIDTH):
                acc = acc + pltpu.roll(ext, tt + CONV_HALO - j, 0)[0:tt, :] * wv[CONV_WIDTH - 1 - j:CONV_WIDTH - j, :]
            du_ref[pl.ds(start, tt), :] = acc.astype(BF16)
            return carry

        lax.fori_loop(0, t // tt, loop2, 0)

    col = pl.BlockSpec((t, tc), lambda j: (0, j))
    return _call(
        body, side, name="conv_bwd", grid=(c // tc,),
        in_specs=[col, pl.BlockSpec((CONV_WIDTH, tc), lambda j: (0, j)), pl.BlockSpec((1, tc), lambda j: (0, j)), col],
        out_specs=[col, pl.BlockSpec((CONV_WIDTH, tc), lambda j: (0, j)), pl.BlockSpec((1, tc), lambda j: (0, j))],
        out_shape=[jax.ShapeDtypeStruct((t, c), BF16), jax.ShapeDtypeStruct((CONV_WIDTH, c), F32),
                   jax.ShapeDtypeStruct((1, c), F32)],
        scratch_shapes=[pltpu.VMEM((t + CONV_HALO, tc), F32)],
        semantics=("parallel",), args=(u, w, b, dact),
    )


def _softplus(v):
    e = jnp.exp(-jnp.abs(v))
    w = 1.0 + e
    log1p = jnp.where(w == 1.0, e, jnp.log(w) * (e / jnp.where(w == 1.0, 1.0, w - 1.0)))
    return jnp.maximum(v, 0.0) + log1p


def _ssd_prep_fwd(dt_raw, dt_bias, a_log):
    t = dt_raw.shape[0]
    cl = SSD_CHUNK

    def body(r_ref, b_ref, al_ref, dt_ref, acs_ref):
        dt = _softplus(r_ref[...] + b_ref[...])
        adt = dt * (-jnp.exp(al_ref[...]))
        li = lax.broadcasted_iota(jnp.int32, (cl, cl), 0)
        si = lax.broadcasted_iota(jnp.int32, (cl, cl), 1)
        tri = (si <= li).astype(F32)
        dt_ref[...] = dt
        acs_ref[...] = jnp.dot(tri, adt, preferred_element_type=F32, precision=HIGHEST)

    row = pl.BlockSpec((cl, LANES), lambda i: (i, 0))
    vec = pl.BlockSpec((1, LANES), lambda i: (0, 0))
    return pl.pallas_call(
        body, name="ssd_prep_fwd", grid=(t // cl,),
        in_specs=[row, vec, vec], out_specs=[row, row],
        out_shape=[jax.ShapeDtypeStruct((t, LANES), F32), jax.ShapeDtypeStruct((t, LANES), F32)],
        compiler_params=_params("parallel"),
    )(dt_raw, dt_bias, a_log)


def _ssd_prep_bwd(dt_raw, dt_bias, ddt):
    t = dt_raw.shape[0]
    tm = _tile(t, 512)

    def body(r_ref, b_ref, d_ref, o_ref, db_ref):
        g = d_ref[...] * _sigmoid(r_ref[...] + b_ref[...])
        o_ref[...] = g.astype(BF16)
        part = jnp.sum(g, axis=0, keepdims=True)

        @pl.when(pl.program_id(0) == 0)
        def _():
            db_ref[...] = part

        @pl.when(pl.program_id(0) > 0)
        def _():
            db_ref[...] += part

    row = pl.BlockSpec((tm, LANES), lambda i: (i, 0))
    vec = pl.BlockSpec((1, LANES), lambda i: (0, 0))
    return pl.pallas_call(
        body, name="ssd_prep_bwd", grid=(t // tm,),
        in_specs=[row, vec, row], out_specs=[row, vec],
        out_shape=[jax.ShapeDtypeStruct((t, LANES), BF16), jax.ShapeDtypeStruct((1, LANES), F32)],
        compiler_params=_params("arbitrary"),
    )(dt_raw, dt_bias, ddt)


GROUP_W = D_INNER // SSM_GROUPS
PAIRS_PER_GROUP = GROUP_W // LANES


def _head_cols(acs_pair, lt64):
    rolled = pltpu.roll(acs_pair, ATT_HEAD_DIM, 1)
    return jnp.where(lt64, acs_pair, rolled), jnp.where(lt64, rolled, acs_pair)


def _ssd_fwd(xbc, dt_rep, acs_rep, acs_t, dskip_rep, side=None):
    t = xbc.shape[0]
    cl = SSD_CHUNK
    nc = t // cl

    def body(xbc_ref, dt_ref, acs_ref, acst_ref, dskip_ref, y_ref, hin_ref, state_ref):
        @pl.when(pl.program_id(0) == 0)
        def _():
            state_ref[...] = jnp.zeros_like(state_ref)

        lt64 = _lane_lt64(cl)
        li = lax.broadcasted_iota(jnp.int32, (cl, cl), 0)
        si = lax.broadcasted_iota(jnp.int32, (cl, cl), 1)
        causal = li >= si
        hin_ref[...] = state_ref[...]
        for g in range(SSM_GROUPS):
            gsl = slice(g * GROUP_W, (g + 1) * GROUP_W)
            xg = xbc_ref[:, gsl]
            bg = xbc_ref[:, D_INNER + g * SSM_STATE:D_INNER + (g + 1) * SSM_STATE]
            cg = xbc_ref[:, D_INNER + SSM_GROUPS * SSM_STATE + g * SSM_STATE:
                         D_INNER + SSM_GROUPS * SSM_STATE + (g + 1) * SSM_STATE]
            acs = acs_ref[:, gsl]
            xdt = xg * dt_ref[:, gsl]
            atot = acs[cl - 1:cl, :]
            hin = state_ref[:, gsl]
            cgb = cg.astype(BF16)
            gmat = _dot_nt(cgb, bg.astype(BF16))
            yoff = _dot(cgb, hin.astype(BF16)) * jnp.exp(acs)
            snew = _dot(bg.T.astype(BF16), (xdt * jnp.exp(atot - acs)).astype(BF16))
            state_ref[:, gsl] = hin * jnp.exp(atot) + snew
            xdtb = xdt.astype(BF16)
            for pr in range(PAIRS_PER_GROUP):
                psl = slice(pr * LANES, (pr + 1) * LANES)
                cols = _head_cols(acs[:, psl], lt64)
                xp = xdtb[:, psl]
                ys = []
                for hh in range(2):
                    h = (g * PAIRS_PER_GROUP + pr) * 2 + hh
                    seg = cols[hh] - acst_ref[h:h + 1, :]
                    lm = jnp.exp(jnp.where(causal, seg, NEG_BIG))
                    ys.append(_dot((gmat * lm).astype(BF16), xp))
                ydiag = jnp.where(lt64, ys[0], ys[1])
                osl = slice(g * GROUP_W + pr * LANES, g * GROUP_W + (pr + 1) * LANES)
                y_ref[:, osl] = ydiag + yoff[:, psl] + xg[:, psl] * dskip_ref[:, osl]

    row = lambda w: pl.BlockSpec((cl, w), lambda c: (c, 0))
    return _call(
        body, side, name="ssd_fwd", grid=(nc,),
        in_specs=[row(CONV_DIM), row(D_INNER), row(D_INNER),
                  pl.BlockSpec((SSM_HEADS, cl), lambda c: (0, c)), pl.BlockSpec((1, D_INNER), lambda c: (0, 0))],
        out_specs=[row(D_INNER), pl.BlockSpec((None, SSM_STATE, D_INNER), lambda c: (c, 0, 0))],
        out_shape=[jax.ShapeDtypeStruct((t, D_INNER), F32), jax.ShapeDtypeStruct((nc, SSM_STATE, D_INNER), F32)],
        scratch_shapes=[pltpu.VMEM((SSM_STATE, D_INNER), F32)],
        semantics=("arbitrary",), args=(xbc, dt_rep, acs_rep, acs_t, dskip_rep),
    )


def _ssd_bwd(xbc, dt_rep, acs_rep, acs_t, dskip_rep, a_rep, hin_all, dy, side=None):
    t = xbc.shape[0]
    cl = SSD_CHUNK
    nc = t // cl

    def body(xbc_ref, dt_ref, acs_ref, acst_ref, dskip_ref, a_ref, hin_ref, dy_ref,
             dxbc_ref, ddt_ref, da_ref, dds_ref, dstate_ref, dacs_ref, dxs_ref):
        step = pl.program_id(0)

        @pl.when(step == 0)
        def _():
            dstate_ref[...] = jnp.zeros_like(dstate_ref)
            da_ref[...] = jnp.zeros_like(da_ref)
            dds_ref[...] = jnp.zeros_like(dds_ref)

        bd = _head_block_diag()
        lt64 = _lane_lt64(cl)
        li = lax.broadcasted_iota(jnp.int32, (cl, cl), 0)
        si = lax.broadcasted_iota(jnp.int32, (cl, cl), 1)
        lower = li >= si
        upper = si >= li
        last_row = lax.broadcasted_iota(jnp.int32, (cl, GROUP_W), 0) == cl - 1
        for g in range(SSM_GROUPS):
            gsl = slice(g * GROUP_W, (g + 1) * GROUP_W)
            bsl = slice(D_INNER + g * SSM_STATE, D_INNER + (g + 1) * SSM_STATE)
            csl = slice(D_INNER + SSM_GROUPS * SSM_STATE + g * SSM_STATE,
                        D_INNER + SSM_GROUPS * SSM_STATE + (g + 1) * SSM_STATE)
            xg = xbc_ref[:, gsl]
            bg = xbc_ref[:, bsl]
            cg = xbc_ref[:, csl]
            bgb, cgb = bg.astype(BF16), cg.astype(BF16)
            acs = acs_ref[:, gsl]
            xdt = xg * dt_ref[:, gsl]
            atot = acs[cl - 1:cl, :]
            eg = jnp.exp(acs)
            dk = jnp.exp(atot - acs)
            etot = jnp.exp(atot)
            hin = hin_ref[:, gsl]
            hinb = hin.astype(BF16)
            dh = dstate_ref[:, gsl]
            dhb = dh.astype(BF16)
            dyg = dy_ref[:, gsl]

            gmat = _dot_nt(cgb, bgb)
            gmat_t = _dot_nt(bgb, cgb)
            ch = _dot(cgb, hinb)
            dacs = _head_sums(dyg * ch * eg, bd)
            dye = (dyg * eg).astype(BF16)
            dc = _dot_nt(dye, hinb)
            dhin = _dot(cg.T.astype(BF16), dye)
            bdh = _dot(bgb, dhb)
            dxs = bdh * dk
            xdk = xdt * dk
            db = _dot_nt(xdk.astype(BF16), dhb)
            ddk = _head_sums(bdh * xdk, bd)
            dacs = dacs - ddk
            datot = jnp.sum(ddk, axis=0, keepdims=True) + etot * _head_sums(
                jnp.sum(dh * hin, axis=0, keepdims=True), bd)
            dacs = dacs + jnp.where(last_row, datot, 0.0)
            dstate_ref[:, gsl] = dh * etot + dhin

            xdtb = xdt.astype(BF16)
            dgsum = jnp.zeros((cl, cl), F32)
            dgsum_t = jnp.zeros((cl, cl), F32)
            for pr in range(PAIRS_PER_GROUP):
                psl = slice(pr * LANES, (pr + 1) * LANES)
                cols = _head_cols(acs[:, psl], lt64)
                xp = xdtb[:, psl]
                dyp = dyg[:, psl].astype(BF16)
                dx1, dac = [], []
                for hh in range(2):
                    h = (g * PAIRS_PER_GROUP + pr) * 2 + hh
                    mine = lt64 if hh == 0 else jnp.logical_not(lt64)
                    row = acst_ref[h:h + 1, :]
                    lm = jnp.exp(jnp.where(lower, cols[hh] - row, NEG_BIG))
                    lm_t = jnp.exp(jnp.where(upper, row - cols[hh], NEG_BIG))
                    dyh = jnp.where(mine, dyp, jnp.zeros_like(dyp))
                    xh = jnp.where(mine, xp, jnp.zeros_like(xp))
                    dm = _dot_nt(dyh, xp)
                    dm_t = _dot_nt(xh, dyp)
                    m_t = gmat_t * lm_t
                    dx1.append(_dot(m_t.astype(BF16), dyp))
                    w = dm * (gmat * lm)
                    w_t = dm_t * m_t
                    dac.append(jnp.sum(w, axis=1, keepdims=True) - jnp.sum(w_t, axis=1, keepdims=True))
                    dgsum = dgsum + dm * lm
                    dgsum_t = dgsum_t + dm_t * lm_t
                osl = slice(g * GROUP_W + pr * LANES, g * GROUP_W + (pr + 1) * LANES)
                dxs_ref[:, osl] = dxs[:, psl] + jnp.where(lt64, dx1[0], dx1[1])
                dacs_ref[:, osl] = dacs[:, psl] + jnp.where(lt64, jnp.broadcast_to(dac[0], (cl, LANES)),
                                                             jnp.broadcast_to(dac[1], (cl, LANES)))
            dxbc_ref[:, csl] = dc + _dot(dgsum.astype(BF16), bgb)
            dxbc_ref[:, bsl] = db + _dot(dgsum_t.astype(BF16), cgb)

        dadt = _split_dot(upper.astype(BF16), dacs_ref[...])
        xall = xbc_ref[:, 0:D_INNER]
        dtall = dt_ref[...]
        dxsall = dxs_ref[...]
        dyall = dy_ref[...]
        ddt_ref[...] = dadt * a_ref[...] + _head_sums(dxsall * xall, bd)
        dxbc_ref[:, 0:D_INNER] = dxsall * dtall + dyall * dskip_ref[...]
        da_ref[...] += jnp.sum(dadt * dtall, axis=0, keepdims=True)
        dds_ref[...] += jnp.sum(dyall * xall, axis=0, keepdims=True)

        @pl.when(step == nc - 1)
        def _():
            dds_ref[...] = _head_sums(dds_ref[...], bd)

    row = lambda w: pl.BlockSpec((cl, w), lambda c: (nc - 1 - c, 0))
    vec = pl.BlockSpec((1, D_INNER), lambda c: (0, 0))
    return _call(
        body, side, name="ssd_bwd", grid=(nc,),
        in_specs=[row(CONV_DIM), row(D_INNER), row(D_INNER),
                  pl.BlockSpec((SSM_HEADS, cl), lambda c: (0, nc - 1 - c)), vec, vec,
                  pl.BlockSpec((None, SSM_STATE, D_INNER), lambda c: (nc - 1 - c, 0, 0)), row(D_INNER)],
        out_specs=[row(CONV_DIM), row(D_INNER), vec, vec],
        out_shape=[jax.ShapeDtypeStruct((t, CONV_DIM), F32), jax.ShapeDtypeStruct((t, D_INNER), F32),
                   jax.ShapeDtypeStruct((1, D_INNER), F32), jax.ShapeDtypeStruct((1, D_INNER), F32)],
        scratch_shapes=[pltpu.VMEM((SSM_STATE, D_INNER), F32), pltpu.VMEM((cl, D_INNER), F32),
                        pltpu.VMEM((cl, D_INNER), F32)],
        semantics=("arbitrary",), args=(xbc, dt_rep, acs_rep, acs_t, dskip_rep, a_rep, hin_all, dy),
    )


def _gate_norm_fwd(y, z, w):
    t, c = y.shape
    tm = _tile(t, 256)

    def body(y_ref, z_ref, w_ref, o_ref):
        for g in range(SSM_GROUPS):
            gsl = slice(g * GROUP_W, (g + 1) * GROUP_W)
            zv = z_ref[:, gsl]
            v = y_ref[:, gsl] * (zv * _sigmoid(zv))
            r = lax.rsqrt(jnp.mean(v * v, axis=-1, keepdims=True) + NORM_EPS)
            o_ref[:, gsl] = (v * r * w_ref[:, gsl]).astype(BF16)

    row = pl.BlockSpec((tm, c), lambda i: (i, 0))
    return pl.pallas_call(
        body, name="gate_norm_fwd", grid=(t // tm,),
        in_specs=[row, row, pl.BlockSpec((1, c), lambda i: (0, 0))], out_specs=row,
        out_shape=jax.ShapeDtypeStruct((t, c), BF16),
        compiler_params=_params("parallel"),
    )(y, z, w)


def _gate_norm_bwd(y, z, w, dout, side=None):
    t, c = y.shape
    tm = _tile(t, 256)

    def body(y_ref, z_ref, w_ref, do_ref, dy_ref, dz_ref, dw_ref):
        @pl.when(pl.program_id(0) == 0)
        def _():
            dw_ref[...] = jnp.zeros_like(dw_ref)

        for g in range(SSM_GROUPS):
            gsl = slice(g * GROUP_W, (g + 1) * GROUP_W)
            zv, yv, dov = z_ref[:, gsl], y_ref[:, gsl], do_ref[:, gsl]
            sg = _sigmoid(zv)
            sz = zv * sg
            v = yv * sz
            r = lax.rsqrt(jnp.mean(v * v, axis=-1, keepdims=True) + NORM_EPS)
            vh = v * r
            dvh = dov * w_ref[:, gsl]
            mean = jnp.mean(dvh * vh, axis=-1, keepdims=True)
            dv = r * (dvh - vh * mean)
            dy_ref[:, gsl] = dv * sz
            dz_ref[:, gsl] = (dv * yv * (sg * (1.0 + zv * (1.0 - sg)))).astype(BF16)
            dw_ref[:, gsl] += jnp.sum(dov * vh, axis=0, keepdims=True)

    row = pl.BlockSpec((tm, c), lambda i: (i, 0))
    vec = pl.BlockSpec((1, c), lambda i: (0, 0))
    return _call(
        body, side, name="gate_norm_bwd", grid=(t // tm,),
        in_specs=[row, row, vec, row], out_specs=[row, row, vec],
        out_shape=[jax.ShapeDtypeStruct((t, c), F32), jax.ShapeDtypeStruct((t, c), BF16),
                   jax.ShapeDtypeStruct((1, c), F32)],
        scratch_shapes=[], semantics=("arbitrary",), args=(y, z, w, dout),
    )


ATT_W = ATT_HEADS * ATT_HEAD_DIM
N_QKV_BLOCKS = 9
ATT_SCALE = 1.0 / math.sqrt(ATT_HEAD_DIM)


def _head_rmsnorm(x, gain, bd):
    ms = _head_sums(x * x, bd) * (1.0 / ATT_HEAD_DIM)
    return x * lax.rsqrt(ms + NORM_EPS) * gain


def _class_rows(ref, blk, r, dil):
    span = ATT_BLOCK * dil
    sub = ref.at[pl.ds(pl.multiple_of(blk * span, span), span), :]
    return sub[...] if dil == 1 else sub[pl.ds(r, ATT_BLOCK, stride=dil), :]


def _store_class_rows(ref, blk, r, dil, val):
    span = ATT_BLOCK * dil
    sub = ref.at[pl.ds(pl.multiple_of(blk * span, span), span), :]
    if dil == 1:
        sub[...] = val
    else:
        sub[pl.ds(r, ATT_BLOCK, stride=dil), :] = val


PAIRS = ATT_HEADS // 2


def _pair_col(g, j):
    return lambda pair: (0, (g * 3 + j) * PAIRS + pair)


def _pair_slopes(pair):
    steps = jnp.full((1, 2 * ATT_BLOCK), 2 * pair + 1, jnp.int32).astype(F32)
    first = jnp.exp(steps * (-0.5 * math.log(2.0)))
    return first, first * (2.0 ** -0.5)


NORM_ROWS = 512


ROW_SLICES = 4
SLICE_ROWS = 2 * ATT_BLOCK // ROW_SLICES


def _fill_band_bias(bias_ref, pair, dil, transposed):
    bq = ATT_BLOCK
    a = lax.broadcasted_iota(jnp.int32, (2 * bq, 2 * bq), 0) % bq
    b = lax.broadcasted_iota(jnp.int32, (2 * bq, 2 * bq), 1)
    dist = (b - a) if transposed else (a + bq - b)
    in_band = (dist >= 0) & (dist <= bq)
    s0, s1 = _pair_slopes(pair)
    first_head = lax.broadcasted_iota(jnp.int32, (2 * bq, 2 * bq), 0) < bq
    bias = jnp.where(first_head, s0, s1) * (dist.astype(F32) * float(dil))
    inside = (b < bq) if transposed else (b >= bq)
    bias_ref[1] = jnp.where(in_band, bias, -NEG_BIG)
    bias_ref[0] = jnp.where(in_band & inside, bias, -NEG_BIG)


def _row_slices():
    return [slice(i * SLICE_ROWS, (i + 1) * SLICE_ROWS) for i in range(ROW_SLICES)]


def _stack_heads(tile):
    rows = lax.broadcasted_iota(jnp.int32, (2 * ATT_BLOCK, LANES), 0) < ATT_BLOCK
    lanes = lax.broadcasted_iota(jnp.int32, (2 * ATT_BLOCK, LANES), 1) < ATT_HEAD_DIM
    both = jnp.concatenate([tile, tile], axis=0)
    return jnp.where(rows == lanes, both, jnp.zeros_like(both))


def _unstack_heads(stacked, lt64):
    return jnp.where(lt64, stacked[:ATT_BLOCK], stacked[ATT_BLOCK:])


ITEMS_PER_PASS = 4


def _item_loop(nb, dil, work):
    if dil == 1:
        def trip(i, carry):
            work([(i * ITEMS_PER_PASS + b, 0) for b in range(ITEMS_PER_PASS)])
            return carry

        lax.fori_loop(0, nb // ITEMS_PER_PASS, trip, 0)
    else:
        def trip(n, carry):
            for r0 in range(0, dil, ITEMS_PER_PASS):
                pl.when(n >= 0)(functools.partial(work, [(n, r0 + j) for j in range(ITEMS_PER_PASS)]))
            return carry

        lax.fori_loop(0, nb, trip, 0)


def _qk_normalised(tile, j, gq_ref, gk_ref):
    kind = (j // (ATT_W // tile.shape[1])) % 3
    gain = jnp.where(kind == 0, gq_ref[...] * ATT_SCALE, gk_ref[...])
    return jnp.where(kind == 2, tile, _head_rmsnorm(tile, gain, _head_block_diag()))


def _attn_fwd(qkn, g, dil):
    t = qkn.shape[0]
    nb = t // dil // ATT_BLOCK
    bq = ATT_BLOCK

    def body(qn_ref, kn_ref, v_ref, o_ref, l_ref, bias_ref):
        _fill_band_bias(bias_ref, pl.program_id(0), dil, False)
        lt64 = _lane_lt64(bq)

        def work(items):
            scores, values, probs = [], [], []
            for n, r in items:
                prev = jnp.maximum(n - 1, 0)
                q2 = _stack_heads(_class_rows(qn_ref, n, r, dil).astype(BF16))
                kcat = jnp.concatenate([_class_rows(kn_ref, prev, r, dil), _class_rows(kn_ref, n, r, dil)],
                                       axis=0).astype(BF16)
                values.append(jnp.concatenate([_class_rows(v_ref, prev, r, dil), _class_rows(v_ref, n, r, dil)],
                                              axis=0).astype(BF16))
                scores.append(_dot_nt(q2, kcat))
            for (n, r), sc in zip(items, scores):
                bias = bias_ref.at[jnp.minimum(n, 1)]
                ps, inv, lses = [], [], []
                for rows in _row_slices():
                    s = sc[rows] - bias[rows, :]
                    m = jnp.max(s, axis=1, keepdims=True)
                    p = jnp.exp(s - m)
                    l = jnp.sum(p, axis=1, keepdims=True)
                    ps.append(p.astype(BF16))
                    inv.append(jnp.broadcast_to(1.0 / l, (SLICE_ROWS, LANES)))
                    lses.append(jnp.broadcast_to(m + jnp.log(l), (SLICE_ROWS, LANES)))
                probs.append((jnp.concatenate(ps, axis=0), jnp.concatenate(inv, axis=0)))
                _store_class_rows(l_ref, n, r, dil, _unstack_heads(jnp.concatenate(lses, axis=0), lt64))
            for (n, r), (p, inv), vcat in zip(items, probs, values):
                _store_class_rows(o_ref, n, r, dil, _unstack_heads(_dot(p, vcat) * inv, lt64))

        _item_loop(nb, dil, work)

    col = lambda j: pl.BlockSpec((t, LANES), _pair_col(g, j))
    out = pl.BlockSpec((t, LANES), lambda pair: (0, pair))
    return pl.pallas_call(
        body, name=f"attn_fwd_g{g}", grid=(PAIRS,),
        in_specs=[col(0), col(1), col(2)], out_specs=[out, out],
        out_shape=[jax.ShapeDtypeStruct((t, ATT_W), F32), jax.ShapeDtypeStruct((t, ATT_W), F32)],
        scratch_shapes=[pltpu.VMEM((2, 2 * bq, 2 * bq), F32)],
        compiler_params=_params("parallel"),
    )(qkn, qkn, qkn)


def _attn_combine_fwd(outs, lses):
    t = outs[0].shape[0]
    tm = _tile(t, 256)

    def body(o0, o1, o2, l0, l1, l2, ob_ref, of_ref, lt_ref):
        a, b, c = l0[...], l1[...], l2[...]
        m = jnp.maximum(jnp.maximum(a, b), c)
        ea, eb, ec = jnp.exp(a - m), jnp.exp(b - m), jnp.exp(c - m)
        ssum = ea + eb + ec
        o = (ea * o0[...] + eb * o1[...] + ec * o2[...]) / ssum
        ob_ref[...] = o.astype(BF16)
        of_ref[...] = o
        lt_ref[...] = m + jnp.log(ssum)

    row = pl.BlockSpec((tm, ATT_W), lambda i: (i, 0))
    return pl.pallas_call(
        body, name="attn_combine_fwd", grid=(t // tm,),
        in_specs=[row] * 6, out_specs=[row] * 3,
        out_shape=[jax.ShapeDtypeStruct((t, ATT_W), BF16), jax.ShapeDtypeStruct((t, ATT_W), F32),
                   jax.ShapeDtypeStruct((t, ATT_W), F32)],
        compiler_params=_params("parallel"),
    )(*outs, *lses)


def _attn_combine_bwd(do, o):
    t = do.shape[0]
    tm = _tile(t, 256)

    def body(do_ref, o_ref, dl_ref):
        dl_ref[...] = _head_sums(do_ref[...] * o_ref[...], _head_block_diag())

    row = pl.BlockSpec((tm, ATT_W), lambda i: (i, 0))
    return pl.pallas_call(
        body, name="attn_combine_bwd", grid=(t // tm,),
        in_specs=[row, row], out_specs=row, out_shape=jax.ShapeDtypeStruct((t, ATT_W), F32),
        compiler_params=_params("parallel"),
    )(do, o)


def _head_rmsnorm_bwd(x_ref, dy_ref, gain_ref, dx_ref, dgain_ref):
    bd = _head_block_diag()
    gain = gain_ref[...]

    def step(i, acc):
        rows = pl.ds(pl.multiple_of(i * NORM_ROWS, NORM_ROWS), NORM_ROWS)
        x, dy = x_ref[rows, :], dy_ref[rows, :]
        r = lax.rsqrt(_head_sums(x * x, bd) * (1.0 / ATT_HEAD_DIM) + NORM_EPS)
        xh = x * r
        dxh = dy * gain
        mean = _head_sums(dxh * xh, bd) * (1.0 / ATT_HEAD_DIM)
        dx_ref[rows, :] = (r * (dxh - xh * mean)).astype(BF16)
        return acc + jnp.sum(dy * xh, axis=0, keepdims=True)

    acc = lax.fori_loop(0, x_ref.shape[0] // NORM_ROWS, step, jnp.zeros((1, LANES), F32))
    dgain_ref[...] = jnp.broadcast_to(acc, dgain_ref.shape)


def _attn_bwd_dq(qkv, qkn, gq, do, l_rep, dl_rep, g, dil):
    t = qkv.shape[0]
    nb = t // dil // ATT_BLOCK
    bq = ATT_BLOCK

    def body(q_ref, qn_ref, kn_ref, v_ref, gq_ref, do_ref, l_ref, dl_ref, dx_ref, dgain_ref, bias_ref, dq_ref):
        _fill_band_bias(bias_ref, pl.program_id(0), dil, False)
        lt64 = _lane_lt64(bq)

        def per_row(tile):
            cols = _head_cols(tile, lt64)
            half = jnp.concatenate([cols[0], cols[1]], axis=0)
            return jnp.concatenate([half, half], axis=1)

        def work(items):
            products, keys, dscores = [], [], []
            for n, r in items:
                prev = jnp.maximum(n - 1, 0)
                q2 = _stack_heads(_class_rows(qn_ref, n, r, dil).astype(BF16))
                do2 = _stack_heads(_class_rows(do_ref, n, r, dil).astype(BF16))
                kcat = jnp.concatenate([_class_rows(kn_ref, prev, r, dil), _class_rows(kn_ref, n, r, dil)],
                                       axis=0).astype(BF16)
                vcat = jnp.concatenate([_class_rows(v_ref, prev, r, dil), _class_rows(v_ref, n, r, dil)],
                                       axis=0).astype(BF16)
                keys.append(kcat)
                products.append((_dot_nt(q2, kcat), _dot_nt(do2, vcat)))
            for (n, r), (scores, dps) in zip(items, products):
                bias = bias_ref.at[jnp.minimum(n, 1)]
                lse = per_row(_class_rows(l_ref, n, r, dil))
                dl = per_row(_class_rows(dl_ref, n, r, dil))
                dss = []
                for rows in _row_slices():
                    p = jnp.exp(scores[rows] - bias[rows, :] - lse[rows])
                    dss.append((p * (dps[rows] - dl[rows])).astype(BF16))
                dscores.append(jnp.concatenate(dss, axis=0))
            for (n, r), ds, kcat in zip(items, dscores, keys):
                _store_class_rows(dq_ref, n, r, dil, _unstack_heads(_dot(ds, kcat) * ATT_SCALE, lt64))

        _item_loop(nb, dil, work)
        _head_rmsnorm_bwd(q_ref, dq_ref, gq_ref, dx_ref, dgain_ref)

    col = lambda j: pl.BlockSpec((t, LANES), _pair_col(g, j))
    vec = pl.BlockSpec((1, LANES), lambda pair: (0, 0))
    tok = pl.BlockSpec((t, LANES), lambda pair: (0, pair))
    return pl.pallas_call(
        body, name=f"attn_bwd_dq_g{g}", grid=(PAIRS,),
        in_specs=[col(0), col(0), col(1), col(2), vec, tok, tok, tok],
        out_specs=[tok, pl.BlockSpec((None, 8, LANES), lambda pair: (pair, 0, 0))],
        out_shape=[jax.ShapeDtypeStruct((t, ATT_W), BF16), jax.ShapeDtypeStruct((PAIRS, 8, LANES), F32)],
        scratch_shapes=[pltpu.VMEM((2, 2 * bq, 2 * bq), F32), pltpu.VMEM((t, LANES), F32)],
        compiler_params=_params("parallel"),
    )(qkv, qkn, qkn, qkn, gq, do, l_rep, dl_rep)


def _attn_bwd_dkv(qkv, qkn, gk, do, l_row, dl_row, g, dil):
    t = qkv.shape[0]
    nb = t // dil // ATT_BLOCK
    bq = ATT_BLOCK

    def body(k_ref, qn_ref, kn_ref, v_ref, gk_ref, do_ref, l_ref, dl_ref, dkx_ref, dvx_ref, dgain_ref, bias_ref,
             dk_ref, dv_ref):
        _fill_band_bias(bias_ref, pl.program_id(0), dil, True)
        lt64 = _lane_lt64(bq)

        def per_query(ref, hh, lane_c, lane_n):
            return jnp.concatenate([ref[hh:hh + 1, pl.ds(lane_c, bq)], ref[hh:hh + 1, pl.ds(lane_n, bq)]], axis=1)

        def work(items):
            products, operands, weights = [], [], []
            for n, r in items:
                nxt = jnp.minimum(n + 1, nb - 1)
                k2 = _stack_heads(_class_rows(kn_ref, n, r, dil).astype(BF16))
                v2 = _stack_heads(_class_rows(v_ref, n, r, dil).astype(BF16))
                qcat = jnp.concatenate([_class_rows(qn_ref, n, r, dil), _class_rows(qn_ref, nxt, r, dil)],
                                       axis=0).astype(BF16)
                docat = jnp.concatenate([_class_rows(do_ref, n, r, dil), _class_rows(do_ref, nxt, r, dil)],
                                        axis=0).astype(BF16)
                operands.append((qcat, docat))
                products.append((_dot_nt(k2, qcat), _dot_nt(v2, docat)))
            for (n, r), (scores, dps) in zip(items, products):
                nxt = jnp.minimum(n + 1, nb - 1)
                bias = bias_ref.at[jnp.where(n == nb - 1, 0, 1)]
                lane_c = pl.multiple_of((r * nb + n) * bq, bq)
                lane_n = pl.multiple_of((r * nb + nxt) * bq, bq)
                lse = [per_query(l_ref, hh, lane_c, lane_n) for hh in range(2)]
                dl = [per_query(dl_ref, hh, lane_c, lane_n) for hh in range(2)]
                pts, dss = [], []
                for i, rows in enumerate(_row_slices()):
                    hh = i * SLICE_ROWS // bq
                    p_t = jnp.exp(scores[rows] - bias[rows, :] - lse[hh])
                    pts.append(p_t.astype(BF16))
                    dss.append((p_t * (dps[rows] - dl[hh])).astype(BF16))
                weights.append((jnp.concatenate(pts, axis=0), jnp.concatenate(dss, axis=0)))
            for (n, r), (p_t, ds_t), (qcat, docat) in zip(items, weights, operands):
                _store_class_rows(dv_ref, n, r, dil, _unstack_heads(_dot(p_t, docat), lt64))
                _store_class_rows(dk_ref, n, r, dil, _unstack_heads(_dot(ds_t, qcat), lt64))

        _item_loop(nb, dil, work)
        _head_rmsnorm_bwd(k_ref, dk_ref, gk_ref, dkx_ref, dgain_ref)

        def cast_rows(i, carry):
            rows = pl.ds(pl.multiple_of(i * NORM_ROWS, NORM_ROWS), NORM_ROWS)
            dvx_ref[rows, :] = dv_ref[rows, :].astype(BF16)
            return carry

        lax.fori_loop(0, t // NORM_ROWS, cast_rows, 0)

    col = lambda j: pl.BlockSpec((t, LANES), _pair_col(g, j))
    vec = pl.BlockSpec((1, LANES), lambda pair: (0, 0))
    tok = pl.BlockSpec((t, LANES), lambda pair: (0, pair))
    rows = pl.BlockSpec((None, 8, t), lambda pair: (pair, 0, 0))
    return pl.pallas_call(
        body, name=f"attn_bwd_dkv_g{g}", grid=(PAIRS,),
        in_specs=[col(1), col(0), col(1), col(2), vec, tok, rows, rows],
        out_specs=[tok, tok, pl.BlockSpec((None, 8, LANES), lambda pair: (pair, 0, 0))],
        out_shape=[jax.ShapeDtypeStruct((t, ATT_W), BF16), jax.ShapeDtypeStruct((t, ATT_W), BF16),
                   jax.ShapeDtypeStruct((PAIRS, 8, LANES), F32)],
        scratch_shapes=[pltpu.VMEM((2, 2 * bq, 2 * bq), F32), pltpu.VMEM((t, LANES), F32),
                        pltpu.VMEM((t, LANES), F32)],
        compiler_params=_params("parallel"),
    )(qkv, qkn, qkn, qkn, gk, do, l_row, dl_row)


def _rows_by_residue(rep, dil):
    t = rep.shape[0]
    per_head = rep[:, ::ATT_HEAD_DIM]
    rows = per_head.reshape(t // dil, dil, ATT_HEADS).transpose(2, 1, 0).reshape(PAIRS, 2, t)
    return jnp.pad(rows, ((0, 0), (0, 6), (0, 0)))


def _per_head(rep_row):
    return rep_row[0, ::SSM_HEAD_DIM]


def _rep_heads(v):
    return jnp.repeat(v, SSM_HEAD_DIM)[None, :]


def _pad_lanes(v):
    return jnp.pad(v, ((0, 0), (0, LANES - v.shape[1])))


class _NoOverlap:
    def side(self, host):
        return None

    def after(self, host):
        pass

    def begin_backward(self, grads):
        pass


def _hosted(plan, host, fn, *args, **kwargs):
    out = fn(*args, side=plan.side(host), **kwargs)
    plan.after(host)
    return out


def _ffn_ple_fwd(x1, p_i, prm, i, plan):
    h = _rmsnorm_fwd(x1, prm["norm_ffn"][i:i + 1], name=f"ffn_norm_fwd_{i}")
    g, u, act = _hosted(plan, f"swiglu_fwd_{i}", _swiglu_fwd, h, prm["ffn_w_gate"][i], prm["ffn_w_up"][i],
                        name=f"swiglu_fwd_{i}")
    x2 = _hosted(plan, f"ffn_down_{i}", _matmul, act, prm["ffn_w_down"][i], mode="nn", addend=x1,
                 name=f"ffn_down_{i}")
    x3 = _ple_fwd(x2, p_i, prm["ple_w_gate"][i], prm["ple_w_proj"][i], name=f"ple_fwd_{i}")
    return x3, dict(x1=x1, h=h, g=g, u=u, act=act, x2=x2)


def _ffn_ple_bwd(dx3, p_i, prm, i, sv, grads, plan):
    ds, dple = _ple_bwd(sv["x2"], p_i, prm["ple_w_gate"][i], prm["ple_w_proj"][i], dx3, name=f"ple_bwd_{i}")
    grads["ple_w_gate"][i] = _matmul_tn(sv["x2"], ds, name=f"d_ple_w_gate_{i}")
    grads["ple_w_proj"][i] = _matmul_tn(dple, p_i, name=f"d_ple_w_proj_{i}")
    dx2 = _matmul(ds, prm["ple_w_gate"][i], mode="nt", addend=dx3, name=f"ple_dx_{i}")
    grads["ffn_w_down"][i] = _matmul_tn(sv["act"], dx2, name=f"d_ffn_w_down_{i}")
    dg, du = _hosted(plan, f"swiglu_bwd_{i}", _swiglu_bwd, dx2, prm["ffn_w_down"][i], sv["g"], sv["u"],
                     name=f"swiglu_bwd_{i}")
    grads["ffn_w_gate"][i] = _matmul_tn(dg, sv["h"], name=f"d_ffn_w_gate_{i}")
    grads["ffn_w_up"][i] = _matmul_tn(du, sv["h"], name=f"d_ffn_w_up_{i}")
    dh = _matmul(dg, prm["ffn_w_gate"][i], mode="nn", name=f"ffn_dh_gate_{i}")
    dh = _matmul(du, prm["ffn_w_up"][i], mode="nn", addend=dh, name=f"ffn_dh_up_{i}")
    dx1, dgain = _rmsnorm_bwd(sv["x1"], prm["norm_ffn"][i:i + 1], dh, dx2, name=f"ffn_norm_bwd_{i}")
    grads["norm_ffn"][i] = dgain[0]
    return dx1


def _mamba_fwd(x0, prm, plan):
    h = _rmsnorm_fwd(x0, prm["norm_mix"][0:1], name="mix_norm_fwd_0")
    z = _hosted(plan, "ssm_in_z", _matmul, h, prm["ssm_w_z"], mode="nt", name="ssm_in_z")
    xbc_pre = _hosted(plan, "ssm_in_xbc", _matmul, h, prm["ssm_w_xbc"], mode="nt", name="ssm_in_xbc")
    dt_raw = _matmul(h, prm["ssm_w_dt"], mode="nt", name="ssm_in_dt")
    xbc = _hosted(plan, "conv_fwd", _conv_fwd, xbc_pre, prm["ssm_conv_w"], prm["ssm_conv_b"])
    dt_bias = _pad_lanes(prm["ssm_dt_bias"])
    a_log = _pad_lanes(prm["ssm_a_log"])
    dt, acs = _ssd_prep_fwd(dt_raw, dt_bias, a_log)
    dt_rep = jnp.repeat(dt[:, :SSM_HEADS], SSM_HEAD_DIM, axis=1)
    acs_rep = jnp.repeat(acs[:, :SSM_HEADS], SSM_HEAD_DIM, axis=1)
    acs_t = acs[:, :SSM_HEADS].T
    dskip_rep = _rep_heads(prm["ssm_d_skip"][0])
    y, hin_all = _hosted(plan, "ssd_fwd", _ssd_fwd, xbc, dt_rep, acs_rep, acs_t, dskip_rep)
    yn = _gate_norm_fwd(y, z, prm["ssm_norm_w"])
    x1 = _matmul(yn, prm["ssm_w_out"], mode="nn", addend=x0, name="ssm_out")
    sv = dict(x0=x0, h=h, z=z, xbc_pre=xbc_pre, dt_raw=dt_raw, xbc=xbc, dt_bias=dt_bias, dt_rep=dt_rep,
              acs_rep=acs_rep, acs_t=acs_t, dskip_rep=dskip_rep, y=y, hin_all=hin_all, yn=yn)
    return x1, sv


def _mamba_bwd(dx1, prm, sv, grads, plan):
    grads["ssm_w_out"] = _matmul_tn(sv["yn"], dx1, name="d_ssm_w_out")
    dyn = _matmul(dx1, prm["ssm_w_out"], mode="nt", name="ssm_out_dx")
    dy, dz, dnw = _hosted(plan, "gate_norm_bwd", _gate_norm_bwd, sv["y"], sv["z"], prm["ssm_norm_w"], dyn)
    grads["ssm_norm_w"] = dnw
    a_rep = _rep_heads(-jnp.exp(prm["ssm_a_log"][0]))
    dxbc, ddt_rep, da_rep, dds_rep = _hosted(plan, "ssd_bwd", _ssd_bwd, sv["xbc"], sv["dt_rep"], sv["acs_rep"],
                                             sv["acs_t"], sv["dskip_rep"], a_rep, sv["hin_all"], dy)
    grads["ssm_d_skip"] = _per_head(dds_rep)[None, :]
    grads["ssm_a_log"] = (_per_head(da_rep) * _per_head(a_rep))[None, :]
    ddt = _pad_lanes(ddt_rep[:, ::SSM_HEAD_DIM])
    ddt_raw, dbias = _ssd_prep_bwd(sv["dt_raw"], sv["dt_bias"], ddt)
    grads["ssm_dt_bias"] = dbias[:, :SSM_HEADS]
    du, dcw, dcb = _hosted(plan, "conv_bwd", _conv_bwd, sv["xbc_pre"], prm["ssm_conv_w"], prm["ssm_conv_b"], dxbc)
    grads["ssm_conv_w"] = dcw
    grads["ssm_conv_b"] = dcb
    h = sv["h"]
    grads["ssm_w_in"] = jnp.concatenate(
        [_matmul_tn(dz, h, name="d_ssm_w_z"), _matmul_tn(du, h, name="d_ssm_w_xbc"),
         _matmul_tn(ddt_raw, h, name="d_ssm_w_dt")[:SSM_HEADS]], axis=0)
    dh = _matmul(dz, prm["ssm_w_z"], mode="nn", name="ssm_dh_z")
    dh = _matmul(du, prm["ssm_w_xbc"], mode="nn", addend=dh, name="ssm_dh_xbc")
    dh = _matmul(ddt_raw, prm["ssm_w_dt"], mode="nn", addend=dh, name="ssm_dh_dt")
    dx0, dgain = _rmsnorm_bwd(sv["x0"], prm["norm_mix"][0:1], dh, dx1, name="mix_norm_bwd_0")
    grads["norm_mix"][0] = dgain[0]
    return dx0


def _attn_mixer_fwd(x0, prm, plan):
    h = _rmsnorm_fwd(x0, prm["norm_mix"][1:2], name="mix_norm_fwd_1")
    n_heads = N_QKV_BLOCKS * ATT_HEADS
    gq = jnp.tile(prm["att_q_norm"], (1, n_heads))
    gk = jnp.tile(prm["att_k_norm"], (1, n_heads))
    qkv, qkn = _hosted(plan, "att_qkv", _matmul, h, prm["att_w_qkv"], mode="nt", name="att_qkv",
                       second=(_qk_normalised, [gq, gk]))
    outs, lses = [], []
    for g, (window, dil) in enumerate(DIL_PATTERNS):
        o_g, l_g = _attn_fwd(qkn, g, dil)
        outs.append(o_g)
        lses.append(l_g)
    o_b, o_f, l_rep = _attn_combine_fwd(outs, lses)
    x1 = _matmul(o_b, prm["att_w_o"], mode="nn", addend=x0, name="att_out")
    sv = dict(x0=x0, h=h, qkv=qkv, qkn=qkn, gq2=gq[:, :LANES], gk2=gk[:, :LANES], o_b=o_b, o_f=o_f, l_rep=l_rep)
    return x1, sv


def _attn_mixer_bwd(dx1, prm, sv, grads):
    grads["att_w_o"] = _matmul_tn(sv["o_b"], dx1, name="d_att_w_o")
    do = _matmul(dx1, prm["att_w_o"], mode="nt", name="att_out_dx")
    dl_rep = _attn_combine_bwd(do, sv["o_f"])
    blocks, dgq, dgk = [], [], []
    for g, (window, dil) in enumerate(DIL_PATTERNS):
        dq, dgq_g = _attn_bwd_dq(sv["qkv"], sv["qkn"], sv["gq2"], do, sv["l_rep"], dl_rep, g, dil)
        dk, dv, dgk_g = _attn_bwd_dkv(sv["qkv"], sv["qkn"], sv["gk2"], do, _rows_by_residue(sv["l_rep"], dil),
                                      _rows_by_residue(dl_rep, dil), g, dil)
        blocks += [dq, dk, dv]
        dgq.append(dgq_g)
        dgk.append(dgk_g)
    dqkv = jnp.concatenate(blocks, axis=1)

    def fold(parts):
        return jnp.stack(parts)[:, :, 0].reshape(-1, ATT_HEAD_DIM).sum(axis=0)[None, :]

    grads["att_q_norm"] = fold(dgq)
    grads["att_k_norm"] = fold(dgk)
    grads["att_w_qkv"] = _matmul_tn(dqkv, sv["h"], name="d_att_w_qkv")
    dh = _matmul(dqkv, prm["att_w_qkv"], mode="nn", name="att_qkv_dx")
    dx0, dgain = _rmsnorm_bwd(sv["x0"], prm["norm_mix"][1:2], dh, dx1, name="mix_norm_bwd_1")
    grads["norm_mix"][1] = dgain[0]
    return dx0


def _local_step(x, p, target, prm, plan=None):
    plan = plan or _NoOverlap()
    grads = {k: [None, None] for k in ("norm_mix", "norm_ffn", "ffn_w_gate", "ffn_w_up", "ffn_w_down",
                                       "ple_w_proj", "ple_w_gate")}
    plan.begin_backward(grads)
    x1, sv_m = _mamba_fwd(x, prm, plan)
    x3, sv_f0 = _ffn_ple_fwd(x1, p[0], prm, 0, plan)
    x4, sv_a = _attn_mixer_fwd(x3, prm, plan)
    x6, sv_f1 = _ffn_ple_fwd(x4, p[1], prm, 1, plan)
    dy, loss_row = _loss_head(x6, target)
    dx4 = _ffn_ple_bwd(dy, p[1], prm, 1, sv_f1, grads, plan)
    dx3 = _attn_mixer_bwd(dx4, prm, sv_a, grads)
    dx1 = _ffn_ple_bwd(dx3, p[0], prm, 0, sv_f0, grads, plan)
    dx0 = _mamba_bwd(dx1, prm, sv_m, grads, plan)
    return loss_row, dx0, grads


W_IN_SLAB_ROWS = 1312


def _position():
    return lax.axis_index("x"), lax.axis_index("y"), lax.axis_index("c")


def _other_chips(x, y):
    return [(1 - x, y), (x, 1 - y), (1 - x, 1 - y)]


def _remote(send_sems, recv_sems, k, src, dst, to):
    return pltpu.make_async_remote_copy(src_ref=src, dst_ref=dst, send_sem=send_sems.at[k], recv_sem=recv_sems.at[k],
                                        device_id=to, device_id_type=MESH)


def _gather_side(entries, whole=()):
    n, nw = len(entries), len(whole)

    def first_hop(ins, outs, send_sems, recv_sems):
        x, y, c = _position()
        cps = []
        for j, chip in enumerate(_other_chips(x, y)):
            for e in range(n):
                cps.append(_remote(send_sems, recv_sems, 6 * e + j, ins[e].at[c], outs[e].at[2 * x + y, c], (*chip, c)))
            for e in range(nw):
                cps.append(_remote(send_sems, recv_sems, 6 * n + 3 * e + j, ins[n + e], outs[n + e].at[2 * x + y],
                                   (*chip, c)))
        return cps

    def start(ins, outs, send_sems, recv_sems):
        for cp in first_hop(ins, outs, send_sems, recv_sems):
            cp.start()

    def finish(ins, outs, send_sems, recv_sems):
        x, y, c = _position()
        me, sibling = (x, y, c), (x, y, 1 - c)
        chips = _other_chips(x, y)
        passed_on = []
        for j, (px, py) in enumerate(chips):
            for e in range(n):
                landed = outs[e].at[2 * px + py, c]
                _remote(send_sems, recv_sems, 6 * e + j, landed, landed, me).wait_recv()
                passed_on.append(_remote(send_sems, recv_sems, 6 * e + 3 + j, landed, landed, sibling))
                passed_on[-1].start()
            for e in range(nw):
                landed = outs[n + e].at[2 * px + py]
                _remote(send_sems, recv_sems, 6 * n + 3 * e + j, landed, landed, me).wait_recv()
        for j, (px, py) in enumerate(chips):
            for e in range(n):
                passed = outs[e].at[2 * px + py, 1 - c]
                _remote(send_sems, recv_sems, 6 * e + 3 + j, passed, passed, me).wait_recv()
        for cp in first_hop(ins, outs, send_sems, recv_sems) + passed_on:
            cp.wait_send()

    shapes = [jax.ShapeDtypeStruct((N_CHIPS,) + a.shape, a.dtype) for a in list(entries) + list(whole)]
    return _Side(list(entries) + list(whole), shapes, 6 * n + 3 * nw, start, finish)


def _run_side(side, name):
    si, so = len(side.inputs), len(side.out_shapes)

    def body(*refs):
        ins, outs, send_sems, recv_sems = refs[:si], refs[si:si + so], refs[-2], refs[-1]
        side.start(ins, outs, send_sems, recv_sems)
        side.finish(ins, outs, send_sems, recv_sems)

    side.outputs = list(pl.pallas_call(
        body, name=name, in_specs=[ANY] * si, out_specs=[ANY] * so, out_shape=side.out_shapes,
        scratch_shapes=[pltpu.SemaphoreType.DMA((side.n_sems,)), pltpu.SemaphoreType.DMA((side.n_sems,))],
    )(*side.inputs))
    return side.outputs


def _swap_side(grads):
    n = len(grads)

    def copies(ins, outs, send_sems, recv_sems):
        x, y, c = _position()
        return [_remote(send_sems, recv_sems, e, ins[e].at[:, 1 - c], outs[e], (x, y, 1 - c)) for e in range(n)]

    def start(ins, outs, send_sems, recv_sems):
        for cp in copies(ins, outs, send_sems, recv_sems):
            cp.start()

    def finish(ins, outs, send_sems, recv_sems):
        for cp in copies(ins, outs, send_sems, recv_sems):
            cp.wait()

    shapes = [jax.ShapeDtypeStruct((N_CHIPS,) + g.shape[2:], g.dtype) for g in grads]
    return _Side(grads, shapes, n, start, finish)


def _chip_exchange_side(chipsums):
    n = len(chipsums)

    def copies(ins, outs, send_sems, recv_sems):
        x, y, c = _position()
        return [_remote(send_sems, recv_sems, 3 * e + j, ins[e].at[2 * tx + ty], outs[e].at[j], (tx, ty, c))
                for j, (tx, ty) in enumerate(_other_chips(x, y)) for e in range(n)]

    def start(ins, outs, send_sems, recv_sems):
        for cp in copies(ins, outs, send_sems, recv_sems):
            cp.start()

    def finish(ins, outs, send_sems, recv_sems):
        for cp in copies(ins, outs, send_sems, recv_sems):
            cp.wait()

    shapes = [jax.ShapeDtypeStruct((3,) + cs.shape[1:], cs.dtype) for cs in chipsums]
    return _Side(chipsums, shapes, 3 * n, start, finish)


def _share_halves(totals):
    n = len(totals)

    def body(*refs):
        t_refs, r_refs = refs[:n], refs[n:2 * n]
        send_sems, recv_sems = refs[2 * n], refs[2 * n + 1]
        x, y, c = _position()
        cps = [pltpu.make_async_remote_copy(src_ref=t_refs[e], dst_ref=r_refs[e], send_sem=send_sems.at[e],
                                            recv_sem=recv_sems.at[e], device_id=(x, y, 1 - c), device_id_type=MESH)
               for e in range(n)]
        for cp in cps:
            cp.start()
        for cp in cps:
            cp.wait()

    return pl.pallas_call(
        body, name="grad_share_halves", in_specs=[ANY] * n, out_specs=[ANY] * n,
        out_shape=[jax.ShapeDtypeStruct(t.shape, t.dtype) for t in totals],
        scratch_shapes=[pltpu.SemaphoreType.DMA((n,)), pltpu.SemaphoreType.DMA((n,))],
    )(*totals)


def _reduce_rows(h):
    return h if h <= 704 else h // 2


def _add_sibling(grad, recv, c_idx, *, name):
    _, _, h, cw = grad.shape
    th = _reduce_rows(h)

    def body(c_ref, g_ref, r_ref, o_ref):
        o_ref[...] = (g_ref[...] + r_ref[...]).astype(BF16)

    return pl.pallas_call(
        body, name=name,
        grid_spec=pltpu.PrefetchScalarGridSpec(
            num_scalar_prefetch=1, grid=(N_CHIPS, h // th),
            in_specs=[pl.BlockSpec((None, None, th, cw), lambda s, i, c_ref: (s, c_ref[0], i, 0)),
                      pl.BlockSpec((None, th, cw), lambda s, i, c_ref: (s, i, 0))],
            out_specs=pl.BlockSpec((None, th, cw), lambda s, i, c_ref: (s, i, 0))),
        out_shape=jax.ShapeDtypeStruct((N_CHIPS, h, cw), BF16),
        compiler_params=_params("parallel", "parallel"),
    )(c_idx, grad, recv)


def _add_chips(chipsum, recv, s_idx, *, name):
    _, h, cw = chipsum.shape
    th = _reduce_rows(h)

    def body(s_ref, own_ref, r_ref, o_ref):
        o_ref[...] = ((own_ref[...].astype(F32) + r_ref[0].astype(F32)) + r_ref[1].astype(F32)) + r_ref[2].astype(F32)

    return pl.pallas_call(
        body, name=name,
        grid_spec=pltpu.PrefetchScalarGridSpec(
            num_scalar_prefetch=1, grid=(h // th,),
            in_specs=[pl.BlockSpec((None, th, cw), lambda i, s_ref: (s_ref[0], i, 0)),
                      pl.BlockSpec((3, th, cw), lambda i, s_ref: (0, i, 0))],
            out_specs=pl.BlockSpec((th, cw), lambda i, s_ref: (i, 0))),
        out_shape=jax.ShapeDtypeStruct((h, cw), F32),
        compiler_params=_params("parallel"),
    )(s_idx, chipsum, recv)


def _adamw_math(w, g, m, v):
    m = ADAM_B1 * m + (1.0 - ADAM_B1) * g
    v = ADAM_B2 * v + (1.0 - ADAM_B2) * (g * g)
    m_hat = m / (1.0 - ADAM_B1 ** ADAM_STEP)
    v_hat = v / (1.0 - ADAM_B2 ** ADAM_STEP)
    delta = -ADAM_LR * (m_hat / (jnp.sqrt(v_hat) + ADAM_EPS) + ADAM_WD * w)
    return delta, m, v


ADAM_TILE_ELEMS = 256 * 1024


def _adamw(w, g, m, v, *, name):
    shape = w.shape
    cols = shape[-1]
    rows = w.size // cols
    tr = rows
    for cand in range(8, rows, 8):
        if rows % cand == 0 and cand * cols <= ADAM_TILE_ELEMS:
            tr = cand
    if rows * cols <= ADAM_TILE_ELEMS:
        tr = rows

    def body(w_ref, g_ref, m_ref, v_ref, d_ref, nm_ref, nv_ref):
        d, nm, nv = _adamw_math(w_ref[...], g_ref[...], m_ref[...], v_ref[...])
        d_ref[...] = d
        nm_ref[...] = nm
        nv_ref[...] = nv

    blk = pl.BlockSpec((tr, cols), lambda i: (i, 0))
    sds = jax.ShapeDtypeStruct((rows, cols), F32)
    outs = pl.pallas_call(
        body, name=name, grid=(rows // tr,), in_specs=[blk] * 4, out_specs=[blk] * 3, out_shape=[sds] * 3,
        compiler_params=_params("parallel"),
    )(*[a.reshape(rows, cols) for a in (w, g, m, v)])
    return [o.reshape(shape) for o in outs]


SMALL_LAYOUT = (("loss", 1), ("norm_mix", 16), ("norm_ffn", 16), ("ssm_conv_b", 24), ("ssm_dt_bias", 1),
                ("ssm_a_log", 1), ("ssm_d_skip", 1), ("ssm_norm_w", 16), ("att_q_norm", 1), ("att_k_norm", 1),
                ("conv_w_full", 96))
SMALL_ROWS = 176
N_DEVICES = 8


def _small_pack(values):
    parts = []
    for name, rows in SMALL_LAYOUT:
        flat = values[name].reshape(-1).astype(F32)
        parts.append(jnp.pad(flat, (0, rows * LANES - flat.shape[0])).reshape(rows, LANES))
    used = sum(r for _, r in SMALL_LAYOUT)
    parts.append(jnp.zeros((SMALL_ROWS - used, LANES), F32))
    return jnp.concatenate(parts, axis=0)


def _small_unpack(pack, shapes):
    out, off = {}, 0
    for name, rows in SMALL_LAYOUT:
        shape = shapes[name]
        n = math.prod(shape)
        out[name] = pack[off:off + rows].reshape(-1)[:n].reshape(shape)
        off += rows
    return out


def _small_allreduce_adamw(g, w, m, v):
    def body(g_ref, w_ref, m_ref, v_ref, gs_ref, d_ref, nm_ref, nv_ref, buf, send_sems, recv_sems):
        x, y, c = _position()
        pos = (x, y, c)
        me = 4 * x + 2 * y + c
        buf[me] = g_ref[...]
        peers = []
        for k in range(1, N_DEVICES):
            bits = ((k >> 2) & 1, (k >> 1) & 1, k & 1)
            peers.append(tuple(1 - p if b else p for p, b in zip(pos, bits)))
        cps = [pltpu.make_async_remote_copy(src_ref=g_ref, dst_ref=buf.at[me], send_sem=send_sems.at[k],
                                            recv_sem=recv_sems.at[k], device_id=peer, device_id_type=MESH)
               for k, peer in enumerate(peers)]
        for cp in cps:
            cp.start()
        for k, (px, py, pc) in enumerate(peers):
            pltpu.make_async_remote_copy(src_ref=g_ref, dst_ref=buf.at[4 * px + 2 * py + pc],
                                         send_sem=send_sems.at[k], recv_sem=recv_sems.at[k],
                                         device_id=(px, py, pc), device_id_type=MESH).wait_recv()
        for cp in cps:
            cp.wait_send()
        total = buf[0]
        for dev in range(1, N_DEVICES):
            total = total + buf[dev]
        gs_ref[...] = total
        d, nm, nv = _adamw_math(w_ref[...], total, m_ref[...], v_ref[...])
        d_ref[...] = d
        nm_ref[...] = nm
        nv_ref[...] = nv

    vm = pl.BlockSpec(memory_space=pltpu.VMEM)
    sds = jax.ShapeDtypeStruct((SMALL_ROWS, LANES), F32)
    return pl.pallas_call(
        body, name="small_allreduce_adamw", in_specs=[vm] * 4, out_specs=[vm] * 4, out_shape=[sds] * 4,
        scratch_shapes=[pltpu.VMEM((N_DEVICES, SMALL_ROWS, LANES), F32),
                        pltpu.SemaphoreType.DMA((N_DEVICES - 1,)), pltpu.SemaphoreType.DMA((N_DEVICES - 1,))],
    )(g, w, m, v)


SMALL = tuple(n for n, _ in SMALL_LAYOUT if n not in ("loss", "conv_w_full"))
WEIGHTS = ("norm_mix", "norm_ffn", "ssm_w_in", "ssm_conv_w", "ssm_conv_b", "ssm_dt_bias", "ssm_a_log", "ssm_d_skip",
           "ssm_norm_w", "ssm_w_out", "att_w_qkv", "att_q_norm", "att_k_norm", "att_w_o", "ffn_w_gate", "ffn_w_up",
           "ffn_w_down", "ple_w_proj", "ple_w_gate")
COLUMN_SHARDED = ("ssm_w_in", "att_w_qkv", "ffn_w_gate", "ffn_w_up", "ple_w_proj")
LAYERED = ("ffn_w_gate", "ffn_w_up", "ffn_w_down", "ple_w_proj", "ple_w_gate")
GATHER_ORDER = ("ssm_w_in", "ssm_w_out", "att_w_qkv", "att_w_o", "ffn_w_gate", "ffn_w_up", "ffn_w_down",
                "ple_w_proj", "ple_w_gate")


def _layers(n):
    return (0, 1) if n in LAYERED else (None,)


def _tag(key):
    return key[0] if key[1] is None else f"{key[0]}_{key[1]}"


QKV_PARTS = 3


def _weight_slab(w, key):
    n, i = key
    if n == "att_w_qkv":
        a = w[n][0].T
        rows = a.shape[0] // QKV_PARTS
        a = a[i * rows:(i + 1) * rows]
    else:
        a = w[n][0 if i is None else i]
        a = a.T if n in COLUMN_SHARDED else a
    if n == "ssm_w_in":
        a = jnp.pad(a, ((0, W_IN_SLAB_ROWS - a.shape[0]), (0, 0)))
    return a.reshape(2, a.shape[0] // 2, a.shape[1]).astype(BF16)


def _install(prm, key, gathered, own, s_me):
    n, i = key
    full = lax.dynamic_update_slice(gathered, own[None], (s_me, 0, 0, 0))
    full = full.reshape(N_CHIPS, 2 * full.shape[2], full.shape[3])
    if n == "att_w_qkv":
        parts = prm.setdefault("att_w_qkv_parts", {})
        parts[i] = full
        if len(parts) == QKV_PARTS:
            prm[n] = jnp.stack([parts[j] for j in range(QKV_PARTS)], axis=1).reshape(-1, D_MODEL)
        return
    if n == "ssm_w_in":
        rows = (D_INNER + CONV_DIM + SSM_HEADS) // N_CHIPS
        w_in_t = full[:, :rows].reshape(N_CHIPS * rows, D_MODEL)
        prm["ssm_w_z"] = w_in_t[:D_INNER]
        prm["ssm_w_xbc"] = w_in_t[D_INNER:D_INNER + CONV_DIM]
        prm["ssm_w_dt"] = jnp.pad(w_in_t[D_INNER + CONV_DIM:], ((0, LANES - SSM_HEADS), (0, 0)))
        return
    full = full.reshape(N_CHIPS * full.shape[1], full.shape[2])
    if i is None:
        prm[n] = full
    else:
        prm.setdefault(n, [None, None])[i] = full


def _grad_slab(grads, key):
    n, i = key
    g = grads[n] if i is None else grads[n][i]
    if n == "ssm_w_in":
        g = jnp.pad(g.reshape(N_CHIPS, g.shape[0] // N_CHIPS, D_MODEL),
                    ((0, 0), (0, W_IN_SLAB_ROWS - g.shape[0] // N_CHIPS), (0, 0)))
    rows = g.size // (N_CHIPS * g.shape[-1])
    return g.reshape(N_CHIPS, 2, rows // 2, g.shape[-1])


def _natural_shard(n, reduced, shape):
    def one(r):
        if n == "ssm_w_in":
            r = r[:shape[-1]]
        return r.T if n in COLUMN_SHARDED else r
    if n in LAYERED:
        return jnp.stack([one(r) for r in reduced]).reshape(shape)
    return one(reduced[0]).reshape(shape)


def kernel(x, p, norm_mix, norm_ffn, ssm_w_in, ssm_conv_w, ssm_conv_b, ssm_dt_bias, ssm_a_log, ssm_d_skip, ssm_norm_w, ssm_w_out, att_w_qkv, att_q_norm, att_k_norm, att_w_o, ffn_w_gate, ffn_w_up, ffn_w_down, ple_w_proj, ple_w_gate, loss_target, m_norm_mix, m_norm_ffn, m_ssm_w_in, m_ssm_conv_w, m_ssm_conv_b, m_ssm_dt_bias, m_ssm_a_log, m_ssm_d_skip, m_ssm_norm_w, m_ssm_w_out, m_att_w_qkv, m_att_q_norm, m_att_k_norm, m_att_w_o, m_ffn_w_gate, m_ffn_w_up, m_ffn_w_down, m_ple_w_proj, m_ple_w_gate, v_norm_mix, v_norm_ffn, v_ssm_w_in, v_ssm_conv_w, v_ssm_conv_b, v_ssm_dt_bias, v_ssm_a_log, v_ssm_d_skip, v_ssm_norm_w, v_ssm_w_out, v_att_w_qkv, v_att_q_norm, v_att_k_norm, v_att_w_o, v_ffn_w_gate, v_ffn_w_up, v_ffn_w_down, v_ple_w_proj, v_ple_w_gate):
    given = dict(locals())
    w = {n: given[n] for n in WEIGHTS}
    m = {n: given["m_" + n] for n in WEIGHTS}
    v = {n: given["v_" + n] for n in WEIGHTS}
    c_idx = lax.axis_index("c").astype(jnp.int32).reshape(1)
    s_idx = (2 * lax.axis_index("x") + lax.axis_index("y")).astype(jnp.int32).reshape(1)

    s_me = 2 * lax.axis_index("x") + lax.axis_index("y")
    first_core = lax.axis_index("c") == 0

    qkv_parts = [("att_w_qkv", j) for j in range(QKV_PARTS)]
    gather_plan = {
        "ssm_in_z": [("ssm_w_out", None)],
        "ssm_in_xbc": [("ffn_w_gate", 0)],
        "conv_fwd": [("ffn_w_up", 0)],
        "ssd_fwd": [("ffn_w_down", 0), ("ple_w_proj", 0), ("ple_w_gate", 0), ("att_w_o", None)],
        "swiglu_fwd_0": qkv_parts[:2],
        "ffn_down_0": qkv_parts[2:],
        "att_qkv": [(n, 1) for n in LAYERED],
    }
    mamba = [("ssm_w_in", None)]
    own = {k: _weight_slab(w, k) for k in mamba + sum(gather_plan.values(), [])}
    prm = {n: w[n] for n in SMALL}

    def land(group, outputs):
        for k, g in zip(group, outputs):
            _install(prm, k, g, own[k], s_me)

    first = _gather_side([own[k] for k in mamba], whole=[ssm_conv_w[0]])
    _run_side(first, "gather_mamba")
    land(mamba, first.outputs)
    conv = lax.dynamic_update_slice(first.outputs[-1], ssm_conv_w, (s_me, 0, 0))
    prm["ssm_conv_w"] = conv.transpose(1, 0, 2).reshape(CONV_WIDTH, CONV_DIM)

    layer1 = [("att_w_qkv", None), ("att_w_o", None)] + [(n, 1) for n in LAYERED]
    ffn0 = [(n, 0) for n in LAYERED] + [("ssm_w_out", None)]
    reduce_plan = {"swiglu_bwd_0": ("swap", layer1), "ssd_bwd": ("exchange", layer1),
                   "gate_norm_bwd": ("swap", ffn0), "conv_bwd": ("exchange", ffn0)}
    state = {}

    def swap_side(group):
        state[_tag(group[0]), "g4"] = g4 = [_grad_slab(state["grads"], k) for k in group]
        return _swap_side(g4)

    def add_siblings(group, from_sibling):
        state[_tag(group[0]), "chipsums"] = [
            _add_sibling(g, r, c_idx, name="add_sibling_" + _tag(k))
            for g, r, k in zip(state[_tag(group[0]), "g4"], from_sibling, group)]

    def exchange_side(group):
        return _chip_exchange_side(state[_tag(group[0]), "chipsums"])

    def add_chips(group, from_chips):
        for k, cs, r in zip(group, state[_tag(group[0]), "chipsums"], from_chips):
            state["total", k] = _add_chips(cs, r, s_idx, name="add_chips_" + _tag(k))

    class Plan(_NoOverlap):
        def __init__(self):
            self.carried = {host: _gather_side([own[k] for k in group]) for host, group in gather_plan.items()}

        def begin_backward(self, grads):
            state["grads"] = grads

        def side(self, host):
            if host in reduce_plan:
                step, group = reduce_plan[host]
                self.carried[host] = swap_side(group) if step == "swap" else exchange_side(group)
            return self.carried.get(host)

        def after(self, host):
            if host in gather_plan:
                land(gather_plan[host], self.carried[host].outputs)
            elif host in reduce_plan:
                step, group = reduce_plan[host]
                (add_siblings if step == "swap" else add_chips)(group, self.carried[host].outputs)

    loss_row, dx, grads = _local_step(x[0], p[:, 0], loss_target[0], prm, Plan())

    add_siblings(mamba, _run_side(swap_side(mamba), "grad_swap_mamba"))
    add_chips(mamba, _run_side(exchange_side(mamba), "grad_exchange_mamba"))
    order = mamba + ffn0 + layer1
    shared = _share_halves([state["total", k] for k in order])
    reduced = {}
    for k, theirs in zip(order, shared):
        lo = jnp.where(first_core, state["total", k], theirs)
        hi = jnp.where(first_core, theirs, state["total", k])
        reduced.setdefault(k[0], {})[k[1]] = jnp.concatenate([lo, hi], axis=0)
    reduced = {n: [by_layer[i] for i in _layers(n)] for n, by_layer in reduced.items()}

    grad, delta, new_m, new_v = {}, {}, {}, {}
    for n in GATHER_ORDER:
        grad[n] = _natural_shard(n, reduced[n], w[n].shape)
        delta[n], new_m[n], new_v[n] = _adamw(w[n], grad[n], m[n], v[n], name="adamw_" + n)

    small_g = {n: (jnp.stack(grads[n]) if isinstance(grads[n], list) else grads[n]) for n in SMALL}
    small_g["loss"] = loss_row
    small_g["conv_w_full"] = grads["ssm_conv_w"]
    zero = {"loss": jnp.zeros((1, LANES), F32), "conv_w_full": jnp.zeros((CONV_WIDTH, CONV_DIM), F32)}
    outs = _small_allreduce_adamw(_small_pack(small_g), _small_pack({**w, **zero}), _small_pack({**m, **zero}),
                                  _small_pack({**v, **zero}))
    shapes = {n: w[n].shape for n in SMALL}
    shapes["loss"] = (1, LANES)
    shapes["conv_w_full"] = (CONV_WIDTH, CONV_DIM)
    sg, sd, sm, sv = [_small_unpack(o, shapes) for o in outs]
    for n in SMALL:
        grad[n], delta[n], new_m[n], new_v[n] = sg[n], sd[n], sm[n], sv[n]
    loss = sg["loss"][0, 0]
    conv_cols = CONV_DIM // N_CHIPS
    grad["ssm_conv_w"] = lax.dynamic_slice(sg["conv_w_full"], (0, s_me * conv_cols), (CONV_WIDTH, conv_cols))[None]
    delta["ssm_conv_w"], new_m["ssm_conv_w"], new_v["ssm_conv_w"] = _adamw(
        ssm_conv_w, grad["ssm_conv_w"], m_ssm_conv_w, v_ssm_conv_w, name="adamw_ssm_conv_w")

    return (loss, dx[None], *[grad[n] for n in WEIGHTS], *[delta[n] for n in WEIGHTS],
            *[new_m[n] for n in WEIGHTS], *[new_v[n] for n in WEIGHTS])
```

```python
import functools
import math

import jax
import jax.numpy as jnp
from jax import lax
from jax.experimental import pallas as pl
from jax.experimental.pallas import tpu as pltpu

F32 = jnp.float32
BF16 = jnp.bfloat16
HIGHEST = lax.Precision.HIGHEST

NORM_EPS = 1e-6
ADAM_LR, ADAM_B1, ADAM_B2, ADAM_EPS, ADAM_WD, ADAM_STEP = 0.001, 0.9, 0.999, 1e-08, 0.01, 10

D_MODEL = 1024
D_INNER = 2048
SSM_HEADS = 32
SSM_HEAD_DIM = 64
SSM_GROUPS = 4
SSM_STATE = 128
SSD_CHUNK = 128
CONV_DIM = 3072
CONV_WIDTH = 4
ATT_HEADS = 16
ATT_HEAD_DIM = 64
DIL_PATTERNS = ((128, 1), (512, 4), (2048, 16))
ATT_BLOCK = 128
FFN_HIDDEN = 2816
PLE_DIM = 256

LANES = 128
V7X_VMEM_LIMIT = 56 * 1024 * 1024
NEG_BIG = -1e30

N_CHIPS = 4


def _params(*sem):
    return pltpu.CompilerParams(dimension_semantics=sem, vmem_limit_bytes=V7X_VMEM_LIMIT)


def _tile(n, pref):
    if n <= pref:
        return n
    best = None
    for t in range(LANES, pref + 1, LANES):
        if n % t == 0:
            best = t
    assert best is not None, (n, pref)
    return best


def _sigmoid(v):
    return 1.0 / (1.0 + jnp.exp(-v))


def _dot(a, b):
    return jnp.dot(a, b, preferred_element_type=F32)


def _dot_nt(a, b):
    return lax.dot_general(a, b, (((1,), (1,)), ((), ())), preferred_element_type=F32)


def _dot_tn(a, b):
    return lax.dot_general(a, b, (((0,), (0,)), ((), ())), preferred_element_type=F32)


def _head_block_diag():
    i = lax.broadcasted_iota(jnp.int32, (LANES, LANES), 0) // ATT_HEAD_DIM
    j = lax.broadcasted_iota(jnp.int32, (LANES, LANES), 1) // ATT_HEAD_DIM
    return (i == j).astype(BF16)


def _split_dot(ones, z):
    hi = z.astype(BF16)
    lo = (z - hi.astype(F32)).astype(BF16)
    return _dot(ones, hi) + _dot(ones, lo)


def _head_sums(z, bd, terms=2):
    hi = z.astype(BF16)
    lo = (z - hi.astype(F32)).astype(BF16) if terms == 2 else None
    parts = []
    for t in range(z.shape[1] // LANES):
        sl = slice(t * LANES, (t + 1) * LANES)
        part = _dot(hi[:, sl], bd)
        parts.append(part + _dot(lo[:, sl], bd) if terms == 2 else part)
    return parts[0] if len(parts) == 1 else jnp.concatenate(parts, axis=1)


def _lane_lt64(rows):
    return lax.broadcasted_iota(jnp.int32, (rows, LANES), 1) < ATT_HEAD_DIM


MESH = pl.DeviceIdType.MESH
ANY = pl.BlockSpec(memory_space=pl.ANY)


class _Side:
    def __init__(self, inputs, out_shapes, n_sems, start, finish):
        self.inputs, self.out_shapes, self.n_sems = list(inputs), list(out_shapes), n_sems
        self.start, self.finish = start, finish
        self.outputs = None


def _call(body, side, *, name, grid, in_specs, out_specs, out_shape, scratch_shapes, semantics, args):
    in_specs, out_specs, out_shape = list(in_specs), list(out_specs), list(out_shape)
    scratch_shapes = list(scratch_shapes)
    if side is None:
        return pl.pallas_call(body, name=name, grid=grid, in_specs=in_specs, out_specs=out_specs,
                              out_shape=out_shape, scratch_shapes=scratch_shapes,
                              compiler_params=_params(*semantics))(*args)
    ni, no, ns = len(in_specs), len(out_specs), len(scratch_shapes)
    si, so = len(side.inputs), len(side.out_shapes)

    def hosted(*refs):
        ins, s_ins = refs[:ni], refs[ni:ni + si]
        outs, s_outs = refs[ni + si:ni + si + no], refs[ni + si + no:ni + si + no + so]
        scratch = refs[ni + si + no + so:ni + si + no + so + ns]
        send_sems, recv_sems = refs[-2], refs[-1]
        first = pl.program_id(0) == 0
        last = pl.program_id(0) == grid[0] - 1
        for axis in range(1, len(grid)):
            first = jnp.logical_and(first, pl.program_id(axis) == 0)
            last = jnp.logical_and(last, pl.program_id(axis) == grid[axis] - 1)

        @pl.when(first)
        def _():
            side.start(s_ins, s_outs, send_sems, recv_sems)

        body(*ins, *outs, *scratch)

        @pl.when(last)
        def _():
            side.finish(s_ins, s_outs, send_sems, recv_sems)

    res = pl.pallas_call(
        hosted, name=name, grid=grid, in_specs=in_specs + [ANY] * si, out_specs=out_specs + [ANY] * so,
        out_shape=out_shape + side.out_shapes,
        scratch_shapes=scratch_shapes + [pltpu.SemaphoreType.DMA((side.n_sems,)),
                                         pltpu.SemaphoreType.DMA((side.n_sems,))],
        compiler_params=_params(*["arbitrary"] * len(grid)),
    )(*args, *side.inputs)
    side.outputs = list(res[no:])
    return list(res[:no])


def _matmul(a, b, *, mode, name, out_dtype=F32, addend=None, tm=1024, tn=512, tk_max=3072, side=None, second=None):
    m, k = a.shape
    if mode == "nn":
        k2, n = b.shape
    else:
        n, k2 = b.shape
    assert k == k2, (a.shape, b.shape, mode)
    tm, tn, tk = _tile(m, tm), _tile(n, tn), _tile(k, tk_max)
    nk = k // tk
    has_add = addend is not None
    n_rows = len(second[1]) if second else 0
    n_out = 2 if second else 1

    def body(*refs):
        a_ref, b_ref = refs[0], refs[1]
        add_ref = refs[2] if has_add else None
        row_refs = refs[2 + has_add:2 + has_add + n_rows]
        o_ref, acc_ref = refs[-1 - n_out], refs[-1]
        kk = pl.program_id(2)
        col_tile = pl.program_id(1)
        av = a_ref[...].astype(BF16)
        bv = b_ref[...].astype(BF16)
        part = _dot(av, bv) if mode == "nn" else _dot_nt(av, bv)

        @pl.when(kk == 0)
        def _():
            acc_ref[...] = part

        @pl.when(kk > 0)
        def _():
            acc_ref[...] += part

        @pl.when(kk == nk - 1)
        def _():
            res = acc_ref[...]
            if has_add:
                res = res + add_ref[...]
            o_ref[...] = res.astype(out_dtype)
            if second:
                refs[-2][...] = second[0](res, col_tile, *row_refs)

    a_spec = pl.BlockSpec((tm, tk), lambda i, j, kk: (i, kk))
    if mode == "nn":
        b_spec = pl.BlockSpec((tk, tn), lambda i, j, kk: (kk, j))
    else:
        b_spec = pl.BlockSpec((tn, tk), lambda i, j, kk: (j, kk))
    tile = pl.BlockSpec((tm, tn), lambda i, j, kk: (i, j))
    in_specs = [a_spec, b_spec]
    args = [a, b]
    if has_add:
        in_specs.append(tile)
        args.append(addend)
    if second:
        in_specs += [pl.BlockSpec((1, tn), lambda i, j, kk: (0, j))] * n_rows
        args += list(second[1])
    outs = _call(
        body, side, name=name, grid=(m // tm, n // tn, nk),
        in_specs=in_specs, out_specs=[tile] * n_out,
        out_shape=[jax.ShapeDtypeStruct((m, n), out_dtype)] + [jax.ShapeDtypeStruct((m, n), F32)] * (n_out - 1),
        scratch_shapes=[pltpu.VMEM((tm, tn), F32)],
        semantics=("parallel", "parallel", "arbitrary"), args=args,
    )
    return outs if second else outs[0]


def _matmul_tn(a, b, *, name, tm=1408, tn=512, tk=1024):
    t, m = a.shape
    t2, n = b.shape
    assert t == t2
    tm, tn, tk = _tile(m, tm), _tile(n, tn), _tile(t, tk)

    def body(a_ref, b_ref, o_ref):
        part = _dot_tn(a_ref[...].astype(BF16), b_ref[...].astype(BF16))

        @pl.when(pl.program_id(2) == 0)
        def _():
            o_ref[...] = part

        @pl.when(pl.program_id(2) > 0)
        def _():
            o_ref[...] += part

    return pl.pallas_call(
        body, name=name, grid=(m // tm, n // tn, t // tk),
        in_specs=[pl.BlockSpec((tk, tm), lambda i, j, kk: (kk, i)),
                  pl.BlockSpec((tk, tn), lambda i, j, kk: (kk, j))],
        out_specs=pl.BlockSpec((tm, tn), lambda i, j, kk: (i, j)),
        out_shape=jax.ShapeDtypeStruct((m, n), F32),
        compiler_params=_params("parallel", "parallel", "arbitrary"),
    )(a, b)


def _rmsnorm_fwd(x, gain, *, name):
    t, d = x.shape
    tm = _tile(t, 512)

    def body(x_ref, g_ref, o_ref):
        xv = x_ref[...]
        r = lax.rsqrt(jnp.mean(xv * xv, axis=-1, keepdims=True) + NORM_EPS)
        o_ref[...] = (xv * r * g_ref[...]).astype(BF16)

    return pl.pallas_call(
        body, name=name, grid=(t // tm,),
        in_specs=[pl.BlockSpec((tm, d), lambda i: (i, 0)), pl.BlockSpec((1, d), lambda i: (0, 0))],
        out_specs=pl.BlockSpec((tm, d), lambda i: (i, 0)),
        out_shape=jax.ShapeDtypeStruct((t, d), BF16),
        compiler_params=_params("parallel"),
    )(x, gain)


def _rmsnorm_bwd(x, gain, dy, dres, *, name):
    t, d = x.shape
    tm = _tile(t, 512)

    def body(x_ref, g_ref, dy_ref, dres_ref, dx_ref, dg_ref):
        xv = x_ref[...]
        r = lax.rsqrt(jnp.mean(xv * xv, axis=-1, keepdims=True) + NORM_EPS)
        xh = xv * r
        dyv = dy_ref[...]
        dxh = dyv * g_ref[...]
        mean = jnp.mean(dxh * xh, axis=-1, keepdims=True)
        dx_ref[...] = dres_ref[...] + r * (dxh - xh * mean)
        part = jnp.sum(dyv * xh, axis=0, keepdims=True)

        @pl.when(pl.program_id(0) == 0)
        def _():
            dg_ref[...] = part

        @pl.when(pl.program_id(0) > 0)
        def _():
            dg_ref[...] += part

    row = pl.BlockSpec((tm, d), lambda i: (i, 0))
    vec = pl.BlockSpec((1, d), lambda i: (0, 0))
    return pl.pallas_call(
        body, name=name, grid=(t // tm,),
        in_specs=[row, vec, row, row], out_specs=[row, vec],
        out_shape=[jax.ShapeDtypeStruct((t, d), F32), jax.ShapeDtypeStruct((1, d), F32)],
        compiler_params=_params("arbitrary"),
    )(x, gain, dy, dres)


def _loss_head(y, target):
    t, d = y.shape
    tm = _tile(t, 512)
    steps = t // tm

    def body(y_ref, t_ref, dy_ref, l_ref, acc_ref):
        e = y_ref[...] - t_ref[...]
        dy_ref[...] = e * (1.0 / d)
        part = jnp.sum(e * e, axis=0, keepdims=True)

        @pl.when(pl.program_id(0) == 0)
        def _():
            acc_ref[...] = part

        @pl.when(pl.program_id(0) > 0)
        def _():
            acc_ref[...] += part

        @pl.when(pl.program_id(0) == steps - 1)
        def _():
            l_ref[...] = jnp.full((1, LANES), (0.5 / d), F32) * jnp.sum(acc_ref[...])

    row = pl.BlockSpec((tm, d), lambda i: (i, 0))
    return pl.pallas_call(
        body, name="loss_head", grid=(steps,),
        in_specs=[row, row], out_specs=[row, pl.BlockSpec((1, LANES), lambda i: (0, 0))],
        out_shape=[jax.ShapeDtypeStruct((t, d), F32), jax.ShapeDtypeStruct((1, LANES), F32)],
        scratch_shapes=[pltpu.VMEM((1, d), F32)],
        compiler_params=_params("arbitrary"),
    )(y, target)


def _swiglu_fwd(h, w_gate_t, w_up_t, *, name, side=None):
    t, d = h.shape
    f = w_gate_t.shape[0]
    tm, tn = _tile(t, 1024), _tile(f, 256)

    def body(h_ref, wg_ref, wu_ref, g_ref, u_ref, a_ref):
        hv = h_ref[...]
        g = _dot_nt(hv, wg_ref[...])
        u = _dot_nt(hv, wu_ref[...])
        g_ref[...] = g.astype(BF16)
        u_ref[...] = u.astype(BF16)
        a_ref[...] = (g * _sigmoid(g) * u).astype(BF16)

    wspec = pl.BlockSpec((tn, d), lambda i, j: (j, 0))
    ospec = pl.BlockSpec((tm, tn), lambda i, j: (i, j))
    return _call(
        body, side, name=name, grid=(t // tm, f // tn),
        in_specs=[pl.BlockSpec((tm, d), lambda i, j: (i, 0)), wspec, wspec],
        out_specs=[ospec, ospec, ospec],
        out_shape=[jax.ShapeDtypeStruct((t, f), BF16), jax.ShapeDtypeStruct((t, f), BF16),
                   jax.ShapeDtypeStruct((t, f), BF16)],
        scratch_shapes=[], semantics=("parallel", "parallel"), args=(h, w_gate_t, w_up_t),
    )


def _swiglu_bwd(dx, w_down, g, u, *, name, side=None):
    t, d = dx.shape
    f = w_down.shape[0]
    tm, tn = _tile(t, 1024), _tile(f, 256)

    def body(dx_ref, wd_ref, g_ref, u_ref, dg_ref, du_ref):
        dact = _dot_nt(dx_ref[...].astype(BF16), wd_ref[...])
        gv, uv = g_ref[...].astype(F32), u_ref[...].astype(F32)
        sg = _sigmoid(gv)
        dg_ref[...] = (dact * uv * sg * (1.0 + gv * (1.0 - sg))).astype(BF16)
        du_ref[...] = (dact * gv * sg).astype(BF16)

    ospec = pl.BlockSpec((tm, tn), lambda i, j: (i, j))
    return _call(
        body, side, name=name, grid=(t // tm, f // tn),
        in_specs=[pl.BlockSpec((tm, d), lambda i, j: (i, 0)), pl.BlockSpec((tn, d), lambda i, j: (j, 0)),
                  ospec, ospec],
        out_specs=[ospec, ospec],
        out_shape=[jax.ShapeDtypeStruct((t, f), BF16), jax.ShapeDtypeStruct((t, f), BF16)],
        scratch_shapes=[], semantics=("parallel", "parallel"), args=(dx, w_down, g, u),
    )


def _ple_fwd(x, p, w_gate, w_proj_t, *, name):
    t, d = x.shape
    e = p.shape[1]
    tm, tn = _tile(t, 1024), _tile(d, 512)

    def body(xf_ref, xr_ref, p_ref, wg_ref, wp_ref, o_ref):
        s = _dot(xf_ref[...].astype(BF16), wg_ref[...])
        ple = _dot_nt(p_ref[...].astype(BF16), wp_ref[...])
        o_ref[...] = xr_ref[...] + _sigmoid(s) * ple

    return pl.pallas_call(
        body, name=name, grid=(t // tm, d // tn),
        in_specs=[pl.BlockSpec((tm, d), lambda i, j: (i, 0)), pl.BlockSpec((tm, tn), lambda i, j: (i, j)),
                  pl.BlockSpec((tm, e), lambda i, j: (i, 0)), pl.BlockSpec((d, tn), lambda i, j: (0, j)),
                  pl.BlockSpec((tn, e), lambda i, j: (j, 0))],
        out_specs=pl.BlockSpec((tm, tn), lambda i, j: (i, j)),
        out_shape=jax.ShapeDtypeStruct((t, d), F32),
        compiler_params=_params("parallel", "parallel"),
    )(x, x, p, w_gate, w_proj_t)


def _ple_bwd(x, p, w_gate, w_proj_t, dout, *, name):
    t, d = x.shape
    e = p.shape[1]
    tm, tn = _tile(t, 1024), _tile(d, 512)

    def body(xf_ref, p_ref, wg_ref, wp_ref, do_ref, ds_ref, dple_ref):
        s = _dot(xf_ref[...].astype(BF16), wg_ref[...])
        ple = _dot_nt(p_ref[...].astype(BF16), wp_ref[...])
        gate = _sigmoid(s)
        dov = do_ref[...]
        dple_ref[...] = (dov * gate).astype(BF16)
        ds_ref[...] = (dov * ple * gate * (1.0 - gate)).astype(BF16)

    ospec = pl.BlockSpec((tm, tn), lambda i, j: (i, j))
    return pl.pallas_call(
        body, name=name, grid=(t // tm, d // tn),
        in_specs=[pl.BlockSpec((tm, d), lambda i, j: (i, 0)), pl.BlockSpec((tm, e), lambda i, j: (i, 0)),
                  pl.BlockSpec((d, tn), lambda i, j: (0, j)), pl.BlockSpec((tn, e), lambda i, j: (j, 0)), ospec],
        out_specs=[ospec, ospec],
        out_shape=[jax.ShapeDtypeStruct((t, d), BF16), jax.ShapeDtypeStruct((t, d), BF16)],
        compiler_params=_params("parallel", "parallel"),
    )(x, p, w_gate, w_proj_t, dout)


CONV_TIME_TILE = 256
CONV_HALO = 8


def _conv_taps(ext, w):
    acc = ext[CONV_HALO:, :] * w[CONV_WIDTH - 1:CONV_WIDTH, :]
    shifted = [ext[CONV_HALO:, :]]
    for j in range(1, CONV_WIDTH):
        sh = pltpu.roll(ext, j, 0)[CONV_HALO:, :]
        shifted.append(sh)
        acc = acc + sh * w[CONV_WIDTH - 1 - j:CONV_WIDTH - j, :]
    return acc, shifted


def _conv_fwd(u, w, b, side=None):
    t, c = u.shape
    tc = _tile(c, 256)
    tt = CONV_TIME_TILE

    def body(u_ref, w_ref, b_ref, o_ref):
        wv, bv = w_ref[...], b_ref[...]

        def tile(start, ext):
            pre = _conv_taps(ext, wv)[0] + bv
            o_ref[pl.ds(start, tt), :] = pre * _sigmoid(pre)

        tile(0, jnp.concatenate([jnp.zeros((CONV_HALO, tc), F32), u_ref[0:tt, :]], axis=0))

        def loop(i, carry):
            start = pl.multiple_of(i * tt, tt)
            tile(start, u_ref[pl.ds(start - CONV_HALO, tt + CONV_HALO), :])
            return carry

        lax.fori_loop(1, t // tt, loop, 0)

    col = pl.BlockSpec((t, tc), lambda j: (0, j))
    return _call(
        body, side, name="conv_fwd", grid=(c // tc,),
        in_specs=[col, pl.BlockSpec((CONV_WIDTH, tc), lambda j: (0, j)), pl.BlockSpec((1, tc), lambda j: (0, j))],
        out_specs=[col], out_shape=[jax.ShapeDtypeStruct((t, c), F32)],
        scratch_shapes=[], semantics=("parallel",), args=(u, w, b),
    )[0]


def _conv_bwd(u, w, b, dact, side=None):
    t, c = u.shape
    tc = _tile(c, 256)
    tt = CONV_TIME_TILE

    def body(u_ref, w_ref, b_ref, da_ref, du_ref, dw_ref, db_ref, dpre_ref):
        wv, bv = w_ref[...], b_ref[...]

        def tile(start, ext, sums):
            acc, shifted = _conv_taps(ext, wv)
            pre = acc + bv
            sg = _sigmoid(pre)
            dpre = da_ref[pl.ds(start, tt), :] * (sg * (1.0 + pre * (1.0 - sg)))
            dpre_ref[pl.ds(start, tt), :] = dpre
            new = [sums[0] + jnp.sum(dpre, axis=0, keepdims=True)]
            for j in range(CONV_WIDTH):
                new.append(sums[1 + j] + jnp.sum(dpre * shifted[j], axis=0, keepdims=True))
            return tuple(new)

        zero = jnp.zeros((1, tc), F32)
        sums = tile(0, jnp.concatenate([jnp.zeros((CONV_HALO, tc), F32), u_ref[0:tt, :]], axis=0),
                    (zero,) * (1 + CONV_WIDTH))

        def loop(i, sums):
            start = pl.multiple_of(i * tt, tt)
            return tile(start, u_ref[pl.ds(start - CONV_HALO, tt + CONV_HALO), :], sums)

        sums = lax.fori_loop(1, t // tt, loop, sums)
        db_ref[...] = sums[0]
        dw_ref[...] = jnp.concatenate([sums[1 + (CONV_WIDTH - 1 - k)] for k in range(CONV_WIDTH)], axis=0)
        dpre_ref[pl.ds(t, CONV_HALO), :] = jnp.zeros((CONV_HALO, tc), F32)

        def loop2(i, carry):
            start = pl.multiple_of(i * tt, tt)
            ext = dpre_ref[pl.ds(start, tt + CONV_HALO), :]
            acc = ext[0:tt, :] * wv[CONV_WIDTH - 1:CONV_WIDTH, :]
            for j in range(1, CONV_WIDTH):
                acc = acc + pltpu.roll(ext, tt + CONV_HALO - j, 0)[0:tt, :] * wv[CONV_WIDTH - 1 - j:CONV_WIDTH - j, :]
            du_ref[pl.ds(start, tt), :] = acc.astype(BF16)
            return carry

        lax.fori_loop(0, t // tt, loop2, 0)

    col = pl.BlockSpec((t, tc), lambda j: (0, j))
    return _call(
        body, side, name="conv_bwd", grid=(c // tc,),
        in_specs=[col, pl.BlockSpec((CONV_WIDTH, tc), lambda j: (0, j)), pl.BlockSpec((1, tc), lambda j: (0, j)), col],
        out_specs=[col, pl.BlockSpec((CONV_WIDTH, tc), lambda j: (0, j)), pl.BlockSpec((1, tc), lambda j: (0, j))],
        out_shape=[jax.ShapeDtypeStruct((t, c), BF16), jax.ShapeDtypeStruct((CONV_WIDTH, c), F32),
                   jax.ShapeDtypeStruct((1, c), F32)],
        scratch_shapes=[pltpu.VMEM((t + CONV_HALO, tc), F32)],
        semantics=("parallel",), args=(u, w, b, dact),
    )


def _softplus(v):
    e = jnp.exp(-jnp.abs(v))
    w = 1.0 + e
    log1p = jnp.where(w == 1.0, e, jnp.log(w) * (e / jnp.where(w == 1.0, 1.0, w - 1.0)))
    return jnp.maximum(v, 0.0) + log1p


def _split3(z):
    hi = z.astype(BF16)
    rest = z - hi.astype(F32)
    mid = rest.astype(BF16)
    return hi, mid, (rest - mid.astype(F32)).astype(BF16)


def _select_dot(z, ones):
    return sum(_dot(term, ones) for term in _split3(z))


def _ssd_prep_fwd(dt_raw, dt_bias, a_log):
    t = dt_raw.shape[0]
    cl = SSD_CHUNK

    def body(r_ref, b_ref, al_ref, acs_ref, dt_rep_ref, acs_rep_ref):
        dt = _softplus(r_ref[...] + b_ref[...])
        adt = dt * (-jnp.exp(al_ref[...]))
        li = lax.broadcasted_iota(jnp.int32, (cl, cl), 0)
        si = lax.broadcasted_iota(jnp.int32, (cl, cl), 1)
        tri = (si <= li).astype(F32)
        acs = jnp.dot(tri, adt, preferred_element_type=F32, precision=HIGHEST)
        acs_ref[...] = acs
        head = lax.broadcasted_iota(jnp.int32, (LANES, D_INNER), 0)
        chan = lax.broadcasted_iota(jnp.int32, (LANES, D_INNER), 1) // SSM_HEAD_DIM
        spread = (head == chan).astype(BF16)
        dt_rep_ref[...] = _select_dot(dt, spread)
        acs_rep_ref[...] = _select_dot(acs, spread)

    row = pl.BlockSpec((cl, LANES), lambda i: (i, 0))
    wide = pl.BlockSpec((cl, D_INNER), lambda i: (i, 0))
    vec = pl.BlockSpec((1, LANES), lambda i: (0, 0))
    return pl.pallas_call(
        body, name="ssd_prep_fwd", grid=(t // cl,),
        in_specs=[row, vec, vec], out_specs=[row, wide, wide],
        out_shape=[jax.ShapeDtypeStruct((t, LANES), F32), jax.ShapeDtypeStruct((t, D_INNER), F32),
                   jax.ShapeDtypeStruct((t, D_INNER), F32)],
        compiler_params=_params("parallel"),
    )(dt_raw, dt_bias, a_log)


def _ssd_prep_bwd(dt_raw, dt_bias, ddt):
    t = dt_raw.shape[0]
    tm = _tile(t, 512)

    def body(r_ref, b_ref, d_ref, o_ref, db_ref):
        g = d_ref[...] * _sigmoid(r_ref[...] + b_ref[...])
        o_ref[...] = g.astype(BF16)
        part = jnp.sum(g, axis=0, keepdims=True)

        @pl.when(pl.program_id(0) == 0)
        def _():
            db_ref[...] = part

        @pl.when(pl.program_id(0) > 0)
        def _():
            db_ref[...] += part

    row = pl.BlockSpec((tm, LANES), lambda i: (i, 0))
    vec = pl.BlockSpec((1, LANES), lambda i: (0, 0))
    return pl.pallas_call(
        body, name="ssd_prep_bwd", grid=(t // tm,),
        in_specs=[row, vec, row], out_specs=[row, vec],
        out_shape=[jax.ShapeDtypeStruct((t, LANES), BF16), jax.ShapeDtypeStruct((1, LANES), F32)],
        compiler_params=_params("arbitrary"),
    )(dt_raw, dt_bias, ddt)


GROUP_W = D_INNER // SSM_GROUPS
PAIRS_PER_GROUP = GROUP_W // LANES


def _head_cols(acs_pair, lt64):
    rolled = pltpu.roll(acs_pair, ATT_HEAD_DIM, 1)
    return jnp.where(lt64, acs_pair, rolled), jnp.where(lt64, rolled, acs_pair)


def _ssd_fwd(xbc, dt_rep, acs_rep, acs_t, dskip_rep, side=None):
    t = xbc.shape[0]
    cl = SSD_CHUNK
    nc = t // cl

    def body(xbc_ref, dt_ref, acs_ref, acst_ref, dskip_ref, y_ref, hin_ref, state_ref):
        @pl.when(pl.program_id(0) == 0)
        def _():
            state_ref[...] = jnp.zeros_like(state_ref)

        lt64 = _lane_lt64(cl)
        li = lax.broadcasted_iota(jnp.int32, (cl, cl), 0)
        si = lax.broadcasted_iota(jnp.int32, (cl, cl), 1)
        causal = li >= si
        hin_ref[...] = state_ref[...]
        for g in range(SSM_GROUPS):
            gsl = slice(g * GROUP_W, (g + 1) * GROUP_W)
            xg = xbc_ref[:, gsl]
            bg = xbc_ref[:, D_INNER + g * SSM_STATE:D_INNER + (g + 1) * SSM_STATE]
            cg = xbc_ref[:, D_INNER + SSM_GROUPS * SSM_STATE + g * SSM_STATE:
                         D_INNER + SSM_GROUPS * SSM_STATE + (g + 1) * SSM_STATE]
            acs = acs_ref[:, gsl]
            xdt = xg * dt_ref[:, gsl]
            atot = acs[cl - 1:cl, :]
            hin = state_ref[:, gsl]
            cgb = cg.astype(BF16)
            gmat = _dot_nt(cgb, bg.astype(BF16))
            yoff = _dot(cgb, hin.astype(BF16)) * jnp.exp(acs)
            snew = _dot(bg.T.astype(BF16), (xdt * jnp.exp(atot - acs)).astype(BF16))
            state_ref[:, gsl] = hin * jnp.exp(atot) + snew
            xdtb = xdt.astype(BF16)
            for pr in range(PAIRS_PER_GROUP):
                psl = slice(pr * LANES, (pr + 1) * LANES)
                cols = _head_cols(acs[:, psl], lt64)
                xp = xdtb[:, psl]
                ys = []
                for hh in range(2):
                    h = (g * PAIRS_PER_GROUP + pr) * 2 + hh
                    seg = cols[hh] - acst_ref[h:h + 1, :]
                    lm = jnp.exp(jnp.where(causal, seg, NEG_BIG))
                    ys.append(_dot((gmat * lm).astype(BF16), xp))
                ydiag = jnp.where(lt64, ys[0], ys[1])
                osl = slice(g * GROUP_W + pr * LANES, g * GROUP_W + (pr + 1) * LANES)
                y_ref[:, osl] = ydiag + yoff[:, psl] + xg[:, psl] * dskip_ref[:, osl]

    row = lambda w: pl.BlockSpec((cl, w), lambda c: (c, 0))
    return _call(
        body, side, name="ssd_fwd", grid=(nc,),
        in_specs=[row(CONV_DIM), row(D_INNER), row(D_INNER),
                  pl.BlockSpec((SSM_HEADS, cl), lambda c: (0, c)), pl.BlockSpec((1, D_INNER), lambda c: (0, 0))],
        out_specs=[row(D_INNER), pl.BlockSpec((None, SSM_STATE, D_INNER), lambda c: (c, 0, 0))],
        out_shape=[jax.ShapeDtypeStruct((t, D_INNER), F32), jax.ShapeDtypeStruct((nc, SSM_STATE, D_INNER), F32)],
        scratch_shapes=[pltpu.VMEM((SSM_STATE, D_INNER), F32)],
        semantics=("arbitrary",), args=(xbc, dt_rep, acs_rep, acs_t, dskip_rep),
    )


def _ssd_bwd(xbc, dt_rep, acs_rep, acs_t, dskip_rep, a_rep, hin_all, dy, side=None):
    t = xbc.shape[0]
    cl = SSD_CHUNK
    nc = t // cl

    def body(xbc_ref, dt_ref, acs_ref, acst_ref, dskip_ref, a_ref, hin_ref, dy_ref,
             dxbc_ref, ddt_ref, da_ref, dds_ref, dstate_ref, dacs_ref, dxs_ref):
        step = pl.program_id(0)

        @pl.when(step == 0)
        def _():
            dstate_ref[...] = jnp.zeros_like(dstate_ref)
            da_ref[...] = jnp.zeros_like(da_ref)
            dds_ref[...] = jnp.zeros_like(dds_ref)

        bd = _head_block_diag()
        lt64 = _lane_lt64(cl)
        li = lax.broadcasted_iota(jnp.int32, (cl, cl), 0)
        si = lax.broadcasted_iota(jnp.int32, (cl, cl), 1)
        lower = li >= si
        upper = si >= li
        last_row = lax.broadcasted_iota(jnp.int32, (cl, GROUP_W), 0) == cl - 1
        for g in range(SSM_GROUPS):
            gsl = slice(g * GROUP_W, (g + 1) * GROUP_W)
            bsl = slice(D_INNER + g * SSM_STATE, D_INNER + (g + 1) * SSM_STATE)
            csl = slice(D_INNER + SSM_GROUPS * SSM_STATE + g * SSM_STATE,
                        D_INNER + SSM_GROUPS * SSM_STATE + (g + 1) * SSM_STATE)
            xg = xbc_ref[:, gsl]
            bg = xbc_ref[:, bsl]
            cg = xbc_ref[:, csl]
            bgb, cgb = bg.astype(BF16), cg.astype(BF16)
            acs = acs_ref[:, gsl]
            xdt = xg * dt_ref[:, gsl]
            atot = acs[cl - 1:cl, :]
            eg = jnp.exp(acs)
            dk = jnp.exp(atot - acs)
            etot = jnp.exp(atot)
            hin = hin_ref[:, gsl]
            hinb = hin.astype(BF16)
            dh = dstate_ref[:, gsl]
            dhb = dh.astype(BF16)
            dyg = dy_ref[:, gsl]

            gmat = _dot_nt(cgb, bgb)
            gmat_t = _dot_nt(bgb, cgb)
            ch = _dot(cgb, hinb)
            dacs = _head_sums(dyg * ch * eg, bd)
            dye = (dyg * eg).astype(BF16)
            dc = _dot_nt(dye, hinb)
            dhin = _dot(cg.T.astype(BF16), dye)
            bdh = _dot(bgb, dhb)
            dxs = bdh * dk
            xdk = xdt * dk
            db = _dot_nt(xdk.astype(BF16), dhb)
            ddk = _head_sums(bdh * xdk, bd)
            dacs = dacs - ddk
            datot = jnp.sum(ddk, axis=0, keepdims=True) + etot * _head_sums(
                jnp.sum(dh * hin, axis=0, keepdims=True), bd)
            dacs = dacs + jnp.where(last_row, datot, 0.0)
            dstate_ref[:, gsl] = dh * etot + dhin

            xdtb = xdt.astype(BF16)
            dgsum = jnp.zeros((cl, cl), F32)
            dgsum_t = jnp.zeros((cl, cl), F32)
            for pr in range(PAIRS_PER_GROUP):
                psl = slice(pr * LANES, (pr + 1) * LANES)
                cols = _head_cols(acs[:, psl], lt64)
                xp = xdtb[:, psl]
                dyp = dyg[:, psl].astype(BF16)
                dx1, dac = [], []
                for hh in range(2):
                    h = (g * PAIRS_PER_GROUP + pr) * 2 + hh
                    mine = lt64 if hh == 0 else jnp.logical_not(lt64)
                    row = acst_ref[h:h + 1, :]
                    lm = jnp.exp(jnp.where(lower, cols[hh] - row, NEG_BIG))
                    lm_t = jnp.exp(jnp.where(upper, row - cols[hh], NEG_BIG))
                    dyh = jnp.where(mine, dyp, jnp.zeros_like(dyp))
                    xh = jnp.where(mine, xp, jnp.zeros_like(xp))
                    dm = _dot_nt(dyh, xp)
                    dm_t = _dot_nt(xh, dyp)
                    m_t = gmat_t * lm_t
                    dx1.append(_dot(m_t.astype(BF16), dyp))
                    w = dm * (gmat * lm)
                    w_t = dm_t * m_t
                    dac.append(jnp.sum(w, axis=1, keepdims=True) - jnp.sum(w_t, axis=1, keepdims=True))
                    dgsum = dgsum + dm * lm
                    dgsum_t = dgsum_t + dm_t * lm_t
                osl = slice(g * GROUP_W + pr * LANES, g * GROUP_W + (pr + 1) * LANES)
                dxs_ref[:, osl] = dxs[:, psl] + jnp.where(lt64, dx1[0], dx1[1])
                dacs_ref[:, osl] = dacs[:, psl] + jnp.where(lt64, jnp.broadcast_to(dac[0], (cl, LANES)),
                                                             jnp.broadcast_to(dac[1], (cl, LANES)))
            dxbc_ref[:, csl] = dc + _dot(dgsum.astype(BF16), bgb)
            dxbc_ref[:, bsl] = db + _dot(dgsum_t.astype(BF16), cgb)

        dadt = _split_dot(upper.astype(BF16), dacs_ref[...])
        xall = xbc_ref[:, 0:D_INNER]
        dtall = dt_ref[...]
        dxsall = dxs_ref[...]
        dyall = dy_ref[...]
        ddt_rep = dadt * a_ref[...] + _head_sums(dxsall * xall, bd)
        chan = lax.broadcasted_iota(jnp.int32, (D_INNER, LANES), 0)
        head = lax.broadcasted_iota(jnp.int32, (D_INNER, LANES), 1)
        ddt_ref[...] = _select_dot(ddt_rep, (chan == head * SSM_HEAD_DIM).astype(BF16))
        dxbc_ref[:, 0:D_INNER] = dxsall * dtall + dyall * dskip_ref[...]
        da_ref[...] += jnp.sum(dadt * dtall, axis=0, keepdims=True)
        dds_ref[...] += jnp.sum(dyall * xall, axis=0, keepdims=True)

        @pl.when(step == nc - 1)
        def _():
            dds_ref[...] = _head_sums(dds_ref[...], bd)

    row = lambda w: pl.BlockSpec((cl, w), lambda c: (nc - 1 - c, 0))
    vec = pl.BlockSpec((1, D_INNER), lambda c: (0, 0))
    return _call(
        body, side, name="ssd_bwd", grid=(nc,),
        in_specs=[row(CONV_DIM), row(D_INNER), row(D_INNER),
                  pl.BlockSpec((SSM_HEADS, cl), lambda c: (0, nc - 1 - c)), vec, vec,
                  pl.BlockSpec((None, SSM_STATE, D_INNER), lambda c: (nc - 1 - c, 0, 0)), row(D_INNER)],
        out_specs=[row(CONV_DIM), row(LANES), vec, vec],
        out_shape=[jax.ShapeDtypeStruct((t, CONV_DIM), F32), jax.ShapeDtypeStruct((t, LANES), F32),
                   jax.ShapeDtypeStruct((1, D_INNER), F32), jax.ShapeDtypeStruct((1, D_INNER), F32)],
        scratch_shapes=[pltpu.VMEM((SSM_STATE, D_INNER), F32), pltpu.VMEM((cl, D_INNER), F32),
                        pltpu.VMEM((cl, D_INNER), F32)],
        semantics=("arbitrary",), args=(xbc, dt_rep, acs_rep, acs_t, dskip_rep, a_rep, hin_all, dy),
    )


def _gate_norm_fwd(y, z, w):
    t, c = y.shape
    tm = _tile(t, 256)

    def body(y_ref, z_ref, w_ref, o_ref):
        for g in range(SSM_GROUPS):
            gsl = slice(g * GROUP_W, (g + 1) * GROUP_W)
            zv = z_ref[:, gsl]
            v = y_ref[:, gsl] * (zv * _sigmoid(zv))
            r = lax.rsqrt(jnp.mean(v * v, axis=-1, keepdims=True) + NORM_EPS)
            o_ref[:, gsl] = (v * r * w_ref[:, gsl]).astype(BF16)

    row = pl.BlockSpec((tm, c), lambda i: (i, 0))
    return pl.pallas_call(
        body, name="gate_norm_fwd", grid=(t // tm,),
        in_specs=[row, row, pl.BlockSpec((1, c), lambda i: (0, 0))], out_specs=row,
        out_shape=jax.ShapeDtypeStruct((t, c), BF16),
        compiler_params=_params("parallel"),
    )(y, z, w)


def _gate_norm_bwd(y, z, w, dout, side=None):
    t, c = y.shape
    tm = _tile(t, 256)

    def body(y_ref, z_ref, w_ref, do_ref, dy_ref, dz_ref, dw_ref):
        @pl.when(pl.program_id(0) == 0)
        def _():
            dw_ref[...] = jnp.zeros_like(dw_ref)

        for g in range(SSM_GROUPS):
            gsl = slice(g * GROUP_W, (g + 1) * GROUP_W)
            zv, yv, dov = z_ref[:, gsl], y_ref[:, gsl], do_ref[:, gsl]
            sg = _sigmoid(zv)
            sz = zv * sg
            v = yv * sz
            r = lax.rsqrt(jnp.mean(v * v, axis=-1, keepdims=True) + NORM_EPS)
            vh = v * r
            dvh = dov * w_ref[:, gsl]
            mean = jnp.mean(dvh * vh, axis=-1, keepdims=True)
            dv = r * (dvh - vh * mean)
            dy_ref[:, gsl] = dv * sz
            dz_ref[:, gsl] = (dv * yv * (sg * (1.0 + zv * (1.0 - sg)))).astype(BF16)
            dw_ref[:, gsl] += jnp.sum(dov * vh, axis=0, keepdims=True)

    row = pl.BlockSpec((tm, c), lambda i: (i, 0))
    vec = pl.BlockSpec((1, c), lambda i: (0, 0))
    return _call(
        body, side, name="gate_norm_bwd", grid=(t // tm,),
        in_specs=[row, row, vec, row], out_specs=[row, row, vec],
        out_shape=[jax.ShapeDtypeStruct((t, c), F32), jax.ShapeDtypeStruct((t, c), BF16),
                   jax.ShapeDtypeStruct((1, c), F32)],
        scratch_shapes=[], semantics=("arbitrary",), args=(y, z, w, dout),
    )


ATT_W = ATT_HEADS * ATT_HEAD_DIM
N_QKV_BLOCKS = 9
ATT_SCALE = 1.0 / math.sqrt(ATT_HEAD_DIM)


def _head_rmsnorm(x, gain, bd):
    ms = _head_sums(x * x, bd, terms=1) * (1.0 / ATT_HEAD_DIM)
    return x * lax.rsqrt(ms + NORM_EPS) * gain


def _class_rows(ref, blk, r, dil):
    span = ATT_BLOCK * dil
    sub = ref.at[pl.ds(pl.multiple_of(blk * span, span), span), :]
    return sub[...] if dil == 1 else sub[pl.ds(r, ATT_BLOCK, stride=dil), :]


def _store_class_rows(ref, blk, r, dil, val):
    span = ATT_BLOCK * dil
    sub = ref.at[pl.ds(pl.multiple_of(blk * span, span), span), :]
    if dil == 1:
        sub[...] = val
    else:
        sub[pl.ds(r, ATT_BLOCK, stride=dil), :] = val


PAIRS = ATT_HEADS // 2


def _pair_col(g, j):
    return lambda pair: (0, (g * 3 + j) * PAIRS + pair)


def _pair_slopes(pair):
    steps = jnp.full((1, 2 * ATT_BLOCK), 2 * pair + 1, jnp.int32).astype(F32)
    first = jnp.exp(steps * (-0.5 * math.log(2.0)))
    return first, first * (2.0 ** -0.5)


NORM_ROWS = 512


ROW_SLICES = 4
SLICE_ROWS = 2 * ATT_BLOCK // ROW_SLICES


def _fill_band_bias(bias_ref, pair, dil, transposed):
    bq = ATT_BLOCK
    a = lax.broadcasted_iota(jnp.int32, (2 * bq, 2 * bq), 0) % bq
    b = lax.broadcasted_iota(jnp.int32, (2 * bq, 2 * bq), 1)
    dist = (b - a) if transposed else (a + bq - b)
    in_band = (dist >= 0) & (dist <= bq)
    s0, s1 = _pair_slopes(pair)
    first_head = lax.broadcasted_iota(jnp.int32, (2 * bq, 2 * bq), 0) < bq
    bias = jnp.where(first_head, s0, s1) * (dist.astype(F32) * float(dil))
    inside = (b < bq) if transposed else (b >= bq)
    bias_ref[1] = jnp.where(in_band, bias, -NEG_BIG)
    bias_ref[0] = jnp.where(in_band & inside, bias, -NEG_BIG)


def _row_slices():
    return [slice(i * SLICE_ROWS, (i + 1) * SLICE_ROWS) for i in range(ROW_SLICES)]


def _stack_heads(tile):
    rows = lax.broadcasted_iota(jnp.int32, (2 * ATT_BLOCK, LANES), 0) < ATT_BLOCK
    lanes = lax.broadcasted_iota(jnp.int32, (2 * ATT_BLOCK, LANES), 1) < ATT_HEAD_DIM
    both = jnp.concatenate([tile, tile], axis=0)
    return jnp.where(rows == lanes, both, jnp.zeros_like(both))


def _unstack_heads(stacked, lt64):
    return jnp.where(lt64, stacked[:ATT_BLOCK], stacked[ATT_BLOCK:])


ITEMS_PER_PASS = 4


def _item_loop(nb, dil, work):
    if dil == 1:
        def trip(i, carry):
            work([(i * ITEMS_PER_PASS + b, 0) for b in range(ITEMS_PER_PASS)])
            return carry

        lax.fori_loop(0, nb // ITEMS_PER_PASS, trip, 0)
    else:
        def trip(n, carry):
            for r0 in range(0, dil, ITEMS_PER_PASS):
                pl.when(n >= 0)(functools.partial(work, [(n, r0 + j) for j in range(ITEMS_PER_PASS)]))
            return carry

        lax.fori_loop(0, nb, trip, 0)


def _qk_normalised(tile, j, gq_ref, gk_ref):
    kind = (j // (ATT_W // tile.shape[1])) % 3
    gain = jnp.where(kind == 0, gq_ref[...] * ATT_SCALE, gk_ref[...])
    return jnp.where(kind == 2, tile, _head_rmsnorm(tile, gain, _head_block_diag()))


def _attn_fwd(qkn, g, dil):
    t = qkn.shape[0]
    nb = t // dil // ATT_BLOCK
    bq = ATT_BLOCK

    def body(qn_ref, kn_ref, v_ref, o_ref, l_ref, bias_ref):
        _fill_band_bias(bias_ref, pl.program_id(0), dil, False)
        lt64 = _lane_lt64(bq)

        def work(items):
            scores, values, probs = [], [], []
            for n, r in items:
                prev = jnp.maximum(n - 1, 0)
                q2 = _stack_heads(_class_rows(qn_ref, n, r, dil).astype(BF16))
                kcat = jnp.concatenate([_class_rows(kn_ref, prev, r, dil), _class_rows(kn_ref, n, r, dil)],
                                       axis=0).astype(BF16)
                values.append(jnp.concatenate([_class_rows(v_ref, prev, r, dil), _class_rows(v_ref, n, r, dil)],
                                              axis=0).astype(BF16))
                scores.append(_dot_nt(q2, kcat))
            for (n, r), sc in zip(items, scores):
                bias = bias_ref.at[jnp.minimum(n, 1)]
                ps, inv, lses = [], [], []
                for rows in _row_slices():
                    s = sc[rows] - bias[rows, :]
                    m = jnp.max(s, axis=1, keepdims=True)
                    p = jnp.exp(s - m)
                    l = jnp.sum(p, axis=1, keepdims=True)
                    ps.append(p.astype(BF16))
                    inv.append(jnp.broadcast_to(1.0 / l, (SLICE_ROWS, LANES)))
                    lses.append(jnp.broadcast_to(m + jnp.log(l), (SLICE_ROWS, LANES)))
                probs.append((jnp.concatenate(ps, axis=0), jnp.concatenate(inv, axis=0)))
                _store_class_rows(l_ref, n, r, dil, _unstack_heads(jnp.concatenate(lses, axis=0), lt64))
            for (n, r), (p, inv), vcat in zip(items, probs, values):
                _store_class_rows(o_ref, n, r, dil, _unstack_heads(_dot(p, vcat) * inv, lt64))

        _item_loop(nb, dil, work)

    col = lambda j: pl.BlockSpec((t, LANES), _pair_col(g, j))
    out = pl.BlockSpec((t, LANES), lambda pair: (0, pair))
    return pl.pallas_call(
        body, name=f"attn_fwd_g{g}", grid=(PAIRS,),
        in_specs=[col(0), col(1), col(2)], out_specs=[out, out],
        out_shape=[jax.ShapeDtypeStruct((t, ATT_W), F32), jax.ShapeDtypeStruct((t, ATT_W), F32)],
        scratch_shapes=[pltpu.VMEM((2, 2 * bq, 2 * bq), F32)],
        compiler_params=_params("parallel"),
    )(qkn, qkn, qkn)


def _attn_combine_fwd(outs, lses):
    t = outs[0].shape[0]
    tm = _tile(t, 256)

    def body(o0, o1, o2, l0, l1, l2, ob_ref, of_ref, lt_ref):
        a, b, c = l0[...], l1[...], l2[...]
        m = jnp.maximum(jnp.maximum(a, b), c)
        ea, eb, ec = jnp.exp(a - m), jnp.exp(b - m), jnp.exp(c - m)
        ssum = ea + eb + ec
        o = (ea * o0[...] + eb * o1[...] + ec * o2[...]) / ssum
        ob_ref[...] = o.astype(BF16)
        of_ref[...] = o
        lt_ref[...] = m + jnp.log(ssum)

    row = pl.BlockSpec((tm, ATT_W), lambda i: (i, 0))
    return pl.pallas_call(
        body, name="attn_combine_fwd", grid=(t // tm,),
        in_specs=[row] * 6, out_specs=[row] * 3,
        out_shape=[jax.ShapeDtypeStruct((t, ATT_W), BF16), jax.ShapeDtypeStruct((t, ATT_W), F32),
                   jax.ShapeDtypeStruct((t, ATT_W), F32)],
        compiler_params=_params("parallel"),
    )(*outs, *lses)


def _attn_combine_bwd(do, o):
    t = do.shape[0]
    tm = _tile(t, 256)

    def body(do_ref, o_ref, dl_ref):
        dl_ref[...] = _head_sums(do_ref[...] * o_ref[...], _head_block_diag())

    row = pl.BlockSpec((tm, ATT_W), lambda i: (i, 0))
    return pl.pallas_call(
        body, name="attn_combine_bwd", grid=(t // tm,),
        in_specs=[row, row], out_specs=row, out_shape=jax.ShapeDtypeStruct((t, ATT_W), F32),
        compiler_params=_params("parallel"),
    )(do, o)


def _head_rmsnorm_bwd(x_ref, dy_ref, gain_ref, dx_ref, dgain_ref):
    bd = _head_block_diag()
    gain = gain_ref[...]

    def step(i, acc):
        rows = pl.ds(pl.multiple_of(i * NORM_ROWS, NORM_ROWS), NORM_ROWS)
        x, dy = x_ref[rows, :], dy_ref[rows, :]
        r = lax.rsqrt(_head_sums(x * x, bd, terms=1) * (1.0 / ATT_HEAD_DIM) + NORM_EPS)
        xh = x * r
        dxh = dy * gain
        mean = _head_sums(dxh * xh, bd, terms=1) * (1.0 / ATT_HEAD_DIM)
        dx_ref[rows, :] = (r * (dxh - xh * mean)).astype(BF16)
        return acc + jnp.sum(dy * xh, axis=0, keepdims=True)

    acc = lax.fori_loop(0, x_ref.shape[0] // NORM_ROWS, step, jnp.zeros((1, LANES), F32))
    dgain_ref[...] = jnp.broadcast_to(acc, dgain_ref.shape)


def _attn_bwd_dq(qkv, qkn, gq, do, l_rep, dl_rep, g, dil):
    t = qkv.shape[0]
    nb = t // dil // ATT_BLOCK
    bq = ATT_BLOCK

    def body(q_ref, qn_ref, kn_ref, v_ref, gq_ref, do_ref, l_ref, dl_ref, dx_ref, dgain_ref, bias_ref, dq_ref):
        _fill_band_bias(bias_ref, pl.program_id(0), dil, False)
        lt64 = _lane_lt64(bq)

        def per_row(tile):
            cols = _head_cols(tile, lt64)
            half = jnp.concatenate([cols[0], cols[1]], axis=0)
            return jnp.concatenate([half, half], axis=1)

        def work(items):
            products, keys, dscores = [], [], []
            for n, r in items:
                prev = jnp.maximum(n - 1, 0)
                q2 = _stack_heads(_class_rows(qn_ref, n, r, dil).astype(BF16))
                do2 = _stack_heads(_class_rows(do_ref, n, r, dil).astype(BF16))
                kcat = jnp.concatenate([_class_rows(kn_ref, prev, r, dil), _class_rows(kn_ref, n, r, dil)],
                                       axis=0).astype(BF16)
                vcat = jnp.concatenate([_class_rows(v_ref, prev, r, dil), _class_rows(v_ref, n, r, dil)],
                                       axis=0).astype(BF16)
                keys.append(kcat)
                products.append((_dot_nt(q2, kcat), _dot_nt(do2, vcat)))
            for (n, r), (scores, dps) in zip(items, products):
                bias = bias_ref.at[jnp.minimum(n, 1)]
                lse = per_row(_class_rows(l_ref, n, r, dil))
                dl = per_row(_class_rows(dl_ref, n, r, dil))
                dss = []
                for rows in _row_slices():
                    p = jnp.exp(scores[rows] - bias[rows, :] - lse[rows])
                    dss.append((p * (dps[rows] - dl[rows])).astype(BF16))
                dscores.append(jnp.concatenate(dss, axis=0))
            for (n, r), ds, kcat in zip(items, dscores, keys):
                _store_class_rows(dq_ref, n, r, dil, _unstack_heads(_dot(ds, kcat) * ATT_SCALE, lt64))

        _item_loop(nb, dil, work)
        _head_rmsnorm_bwd(q_ref, dq_ref, gq_ref, dx_ref, dgain_ref)

    col = lambda j: pl.BlockSpec((t, LANES), _pair_col(g, j))
    vec = pl.BlockSpec((1, LANES), lambda pair: (0, 0))
    tok = pl.BlockSpec((t, LANES), lambda pair: (0, pair))
    return pl.pallas_call(
        body, name=f"attn_bwd_dq_g{g}", grid=(PAIRS,),
        in_specs=[col(0), col(0), col(1), col(2), vec, tok, tok, tok],
        out_specs=[tok, pl.BlockSpec((None, 8, LANES), lambda pair: (pair, 0, 0))],
        out_shape=[jax.ShapeDtypeStruct((t, ATT_W), BF16), jax.ShapeDtypeStruct((PAIRS, 8, LANES), F32)],
        scratch_shapes=[pltpu.VMEM((2, 2 * bq, 2 * bq), F32), pltpu.VMEM((t, LANES), F32)],
        compiler_params=_params("parallel"),
    )(qkv, qkn, qkn, qkn, gq, do, l_rep, dl_rep)


def _attn_bwd_dkv(qkv, qkn, gk, do, l_row, dl_row, g, dil):
    t = qkv.shape[0]
    nb = t // dil // ATT_BLOCK
    bq = ATT_BLOCK

    def body(k_ref, qn_ref, kn_ref, v_ref, gk_ref, do_ref, l_ref, dl_ref, dkx_ref, dvx_ref, dgain_ref, bias_ref,
             dk_ref, dv_ref):
        _fill_band_bias(bias_ref, pl.program_id(0), dil, True)
        lt64 = _lane_lt64(bq)

        def per_query(ref, hh, lane_c, lane_n):
            return jnp.concatenate([ref[hh:hh + 1, pl.ds(lane_c, bq)], ref[hh:hh + 1, pl.ds(lane_n, bq)]], axis=1)

        def work(items):
            products, operands, weights = [], [], []
            for n, r in items:
                nxt = jnp.minimum(n + 1, nb - 1)
                k2 = _stack_heads(_class_rows(kn_ref, n, r, dil).astype(BF16))
                v2 = _stack_heads(_class_rows(v_ref, n, r, dil).astype(BF16))
                qcat = jnp.concatenate([_class_rows(qn_ref, n, r, dil), _class_rows(qn_ref, nxt, r, dil)],
                                       axis=0).astype(BF16)
                docat = jnp.concatenate([_class_rows(do_ref, n, r, dil), _class_rows(do_ref, nxt, r, dil)],
                                        axis=0).astype(BF16)
                operands.append((qcat, docat))
                products.append((_dot_nt(k2, qcat), _dot_nt(v2, docat)))
            for (n, r), (scores, dps) in zip(items, products):
                nxt = jnp.minimum(n + 1, nb - 1)
                bias = bias_ref.at[jnp.where(n == nb - 1, 0, 1)]
                lane_c = pl.multiple_of((r * nb + n) * bq, bq)
                lane_n = pl.multiple_of((r * nb + nxt) * bq, bq)
                lse = [per_query(l_ref, hh, lane_c, lane_n) for hh in range(2)]
                dl = [per_query(dl_ref, hh, lane_c, lane_n) for hh in range(2)]
                pts, dss = [], []
                for i, rows in enumerate(_row_slices()):
                    hh = i * SLICE_ROWS // bq
                    p_t = jnp.exp(scores[rows] - bias[rows, :] - lse[hh])
                    pts.append(p_t.astype(BF16))
                    dss.append((p_t * (dps[rows] - dl[hh])).astype(BF16))
                weights.append((jnp.concatenate(pts, axis=0), jnp.concatenate(dss, axis=0)))
            for (n, r), (p_t, ds_t), (qcat, docat) in zip(items, weights, operands):
                _store_class_rows(dv_ref, n, r, dil, _unstack_heads(_dot(p_t, docat), lt64))
                _store_class_rows(dk_ref, n, r, dil, _unstack_heads(_dot(ds_t, qcat), lt64))

        _item_loop(nb, dil, work)
        _head_rmsnorm_bwd(k_ref, dk_ref, gk_ref, dkx_ref, dgain_ref)

        def cast_rows(i, carry):
            rows = pl.ds(pl.multiple_of(i * NORM_ROWS, NORM_ROWS), NORM_ROWS)
            dvx_ref[rows, :] = dv_ref[rows, :].astype(BF16)
            return carry

        lax.fori_loop(0, t // NORM_ROWS, cast_rows, 0)

    col = lambda j: pl.BlockSpec((t, LANES), _pair_col(g, j))
    vec = pl.BlockSpec((1, LANES), lambda pair: (0, 0))
    tok = pl.BlockSpec((t, LANES), lambda pair: (0, pair))
    rows = pl.BlockSpec((None, 8, t), lambda pair: (pair, 0, 0))
    return pl.pallas_call(
        body, name=f"attn_bwd_dkv_g{g}", grid=(PAIRS,),
        in_specs=[col(1), col(0), col(1), col(2), vec, tok, rows, rows],
        out_specs=[tok, tok, pl.BlockSpec((None, 8, LANES), lambda pair: (pair, 0, 0))],
        out_shape=[jax.ShapeDtypeStruct((t, ATT_W), BF16), jax.ShapeDtypeStruct((t, ATT_W), BF16),
                   jax.ShapeDtypeStruct((PAIRS, 8, LANES), F32)],
        scratch_shapes=[pltpu.VMEM((2, 2 * bq, 2 * bq), F32), pltpu.VMEM((t, LANES), F32),
                        pltpu.VMEM((t, LANES), F32)],
        compiler_params=_params("parallel"),
    )(qkv, qkn, qkn, qkn, gk, do, l_row, dl_row)


def _rows_by_residue(rep, dil):
    t = rep.shape[0]
    per_head = rep[:, ::ATT_HEAD_DIM]
    rows = per_head.reshape(t // dil, dil, ATT_HEADS).transpose(2, 1, 0).reshape(PAIRS, 2, t)
    return jnp.pad(rows, ((0, 0), (0, 6), (0, 0)))


def _per_head(rep_row):
    return rep_row[0, ::SSM_HEAD_DIM]


def _rep_heads(v):
    return jnp.repeat(v, SSM_HEAD_DIM)[None, :]


def _pad_lanes(v):
    return jnp.pad(v, ((0, 0), (0, LANES - v.shape[1])))


class _NoOverlap:
    def side(self, host):
        return None

    def after(self, host):
        pass

    def begin_backward(self, grads):
        pass


def _hosted(plan, host, fn, *args, **kwargs):
    out = fn(*args, side=plan.side(host), **kwargs)
    plan.after(host)
    return out


def _ffn_ple_fwd(x1, p_i, prm, i, plan):
    h = _rmsnorm_fwd(x1, prm["norm_ffn"][i:i + 1], name=f"ffn_norm_fwd_{i}")
    g, u, act = _hosted(plan, f"swiglu_fwd_{i}", _swiglu_fwd, h, prm["ffn_w_gate"][i], prm["ffn_w_up"][i],
                        name=f"swiglu_fwd_{i}")
    x2 = _hosted(plan, f"ffn_down_{i}", _matmul, act, prm["ffn_w_down"][i], mode="nn", addend=x1,
                 name=f"ffn_down_{i}")
    x3 = _ple_fwd(x2, p_i, prm["ple_w_gate"][i], prm["ple_w_proj"][i], name=f"ple_fwd_{i}")
    return x3, dict(x1=x1, h=h, g=g, u=u, act=act, x2=x2)


def _ffn_ple_bwd(dx3, p_i, prm, i, sv, grads, plan):
    ds, dple = _ple_bwd(sv["x2"], p_i, prm["ple_w_gate"][i], prm["ple_w_proj"][i], dx3, name=f"ple_bwd_{i}")
    grads["ple_w_gate"][i] = _matmul_tn(sv["x2"], ds, name=f"d_ple_w_gate_{i}")
    grads["ple_w_proj"][i] = _matmul_tn(dple, p_i, name=f"d_ple_w_proj_{i}")
    dx2 = _matmul(ds, prm["ple_w_gate"][i], mode="nt", addend=dx3, name=f"ple_dx_{i}")
    grads["ffn_w_down"][i] = _matmul_tn(sv["act"], dx2, name=f"d_ffn_w_down_{i}")
    dg, du = _hosted(plan, f"swiglu_bwd_{i}", _swiglu_bwd, dx2, prm["ffn_w_down"][i], sv["g"], sv["u"],
                     name=f"swiglu_bwd_{i}")
    grads["ffn_w_gate"][i] = _matmul_tn(dg, sv["h"], name=f"d_ffn_w_gate_{i}")
    grads["ffn_w_up"][i] = _matmul_tn(du, sv["h"], name=f"d_ffn_w_up_{i}")
    dh = _matmul(dg, prm["ffn_w_gate"][i], mode="nn", name=f"ffn_dh_gate_{i}")
    dh = _matmul(du, prm["ffn_w_up"][i], mode="nn", addend=dh, name=f"ffn_dh_up_{i}")
    dx1, dgain = _rmsnorm_bwd(sv["x1"], prm["norm_ffn"][i:i + 1], dh, dx2, name=f"ffn_norm_bwd_{i}")
    grads["norm_ffn"][i] = dgain[0]
    return dx1


def _mamba_fwd(x0, prm, plan):
    h = _rmsnorm_fwd(x0, prm["norm_mix"][0:1], name="mix_norm_fwd_0")
    z = _hosted(plan, "ssm_in_z", _matmul, h, prm["ssm_w_z"], mode="nt", name="ssm_in_z")
    xbc_pre = _hosted(plan, "ssm_in_xbc", _matmul, h, prm["ssm_w_xbc"], mode="nt", name="ssm_in_xbc")
    dt_raw = _matmul(h, prm["ssm_w_dt"], mode="nt", name="ssm_in_dt")
    xbc = _hosted(plan, "conv_fwd", _conv_fwd, xbc_pre, prm["ssm_conv_w"], prm["ssm_conv_b"])
    dt_bias = _pad_lanes(prm["ssm_dt_bias"])
    a_log = _pad_lanes(prm["ssm_a_log"])
    acs, dt_rep, acs_rep = _ssd_prep_fwd(dt_raw, dt_bias, a_log)
    acs_t = acs[:, :SSM_HEADS].T
    dskip_rep = _rep_heads(prm["ssm_d_skip"][0])
    y, hin_all = _hosted(plan, "ssd_fwd", _ssd_fwd, xbc, dt_rep, acs_rep, acs_t, dskip_rep)
    yn = _gate_norm_fwd(y, z, prm["ssm_norm_w"])
    x1 = _matmul(yn, prm["ssm_w_out"], mode="nn", addend=x0, name="ssm_out")
    sv = dict(x0=x0, h=h, z=z, xbc_pre=xbc_pre, dt_raw=dt_raw, xbc=xbc, dt_bias=dt_bias, dt_rep=dt_rep,
              acs_rep=acs_rep, acs_t=acs_t, dskip_rep=dskip_rep, y=y, hin_all=hin_all, yn=yn)
    return x1, sv


def _mamba_bwd(dx1, prm, sv, grads, plan):
    grads["ssm_w_out"] = _matmul_tn(sv["yn"], dx1, name="d_ssm_w_out")
    dyn = _matmul(dx1, prm["ssm_w_out"], mode="nt", name="ssm_out_dx")
    dy, dz, dnw = _hosted(plan, "gate_norm_bwd", _gate_norm_bwd, sv["y"], sv["z"], prm["ssm_norm_w"], dyn)
    grads["ssm_norm_w"] = dnw
    a_rep = _rep_heads(-jnp.exp(prm["ssm_a_log"][0]))
    dxbc, ddt, da_rep, dds_rep = _hosted(plan, "ssd_bwd", _ssd_bwd, sv["xbc"], sv["dt_rep"], sv["acs_rep"],
                                             sv["acs_t"], sv["dskip_rep"], a_rep, sv["hin_all"], dy)
    grads["ssm_d_skip"] = _per_head(dds_rep)[None, :]
    grads["ssm_a_log"] = (_per_head(da_rep) * _per_head(a_rep))[None, :]
    ddt_raw, dbias = _ssd_prep_bwd(sv["dt_raw"], sv["dt_bias"], ddt)
    grads["ssm_dt_bias"] = dbias[:, :SSM_HEADS]
    du, dcw, dcb = _hosted(plan, "conv_bwd", _conv_bwd, sv["xbc_pre"], prm["ssm_conv_w"], prm["ssm_conv_b"], dxbc)
    grads["ssm_conv_w"] = dcw
    grads["ssm_conv_b"] = dcb
    h = sv["h"]
    grads["ssm_w_in"] = jnp.concatenate(
        [_matmul_tn(dz, h, name="d_ssm_w_z"), _matmul_tn(du, h, name="d_ssm_w_xbc"),
         _matmul_tn(ddt_raw, h, name="d_ssm_w_dt")[:SSM_HEADS]], axis=0)
    dh = _matmul(dz, prm["ssm_w_z"], mode="nn", name="ssm_dh_z")
    dh = _matmul(du, prm["ssm_w_xbc"], mode="nn", addend=dh, name="ssm_dh_xbc")
    dh = _matmul(ddt_raw, prm["ssm_w_dt"], mode="nn", addend=dh, name="ssm_dh_dt")
    dx0, dgain = _rmsnorm_bwd(sv["x0"], prm["norm_mix"][0:1], dh, dx1, name="mix_norm_bwd_0")
    grads["norm_mix"][0] = dgain[0]
    return dx0


def _attn_mixer_fwd(x0, prm, plan):
    h = _rmsnorm_fwd(x0, prm["norm_mix"][1:2], name="mix_norm_fwd_1")
    n_heads = N_QKV_BLOCKS * ATT_HEADS
    gq = jnp.tile(prm["att_q_norm"], (1, n_heads))
    gk = jnp.tile(prm["att_k_norm"], (1, n_heads))
    qkv, qkn = _hosted(plan, "att_qkv", _matmul, h, prm["att_w_qkv"], mode="nt", name="att_qkv",
                       second=(_qk_normalised, [gq, gk]))
    outs, lses = [], []
    for g, (window, dil) in enumerate(DIL_PATTERNS):
        o_g, l_g = _attn_fwd(qkn, g, dil)
        outs.append(o_g)
        lses.append(l_g)
    o_b, o_f, l_rep = _attn_combine_fwd(outs, lses)
    x1 = _matmul(o_b, prm["att_w_o"], mode="nn", addend=x0, name="att_out")
    sv = dict(x0=x0, h=h, qkv=qkv, qkn=qkn, gq2=gq[:, :LANES], gk2=gk[:, :LANES], o_b=o_b, o_f=o_f, l_rep=l_rep)
    return x1, sv


def _attn_mixer_bwd(dx1, prm, sv, grads):
    grads["att_w_o"] = _matmul_tn(sv["o_b"], dx1, name="d_att_w_o")
    do = _matmul(dx1, prm["att_w_o"], mode="nt", name="att_out_dx")
    dl_rep = _attn_combine_bwd(do, sv["o_f"])
    blocks, dgq, dgk = [], [], []
    for g, (window, dil) in enumerate(DIL_PATTERNS):
        dq, dgq_g = _attn_bwd_dq(sv["qkv"], sv["qkn"], sv["gq2"], do, sv["l_rep"], dl_rep, g, dil)
        dk, dv, dgk_g = _attn_bwd_dkv(sv["qkv"], sv["qkn"], sv["gk2"], do, _rows_by_residue(sv["l_rep"], dil),
                                      _rows_by_residue(dl_rep, dil), g, dil)
        blocks += [dq, dk, dv]
        dgq.append(dgq_g)
        dgk.append(dgk_g)
    dqkv = jnp.concatenate(blocks, axis=1)

    def fold(parts):
        return jnp.stack(parts)[:, :, 0].reshape(-1, ATT_HEAD_DIM).sum(axis=0)[None, :]

    grads["att_q_norm"] = fold(dgq)
    grads["att_k_norm"] = fold(dgk)
    grads["att_w_qkv"] = _matmul_tn(dqkv, sv["h"], name="d_att_w_qkv")
    dh = _matmul(dqkv, prm["att_w_qkv"], mode="nn", name="att_qkv_dx")
    dx0, dgain = _rmsnorm_bwd(sv["x0"], prm["norm_mix"][1:2], dh, dx1, name="mix_norm_bwd_1")
    grads["norm_mix"][1] = dgain[0]
    return dx0


def _local_step(x, p, target, prm, plan=None):
    plan = plan or _NoOverlap()
    grads = {k: [None, None] for k in ("norm_mix", "norm_ffn", "ffn_w_gate", "ffn_w_up", "ffn_w_down",
                                       "ple_w_proj", "ple_w_gate")}
    plan.begin_backward(grads)
    x1, sv_m = _mamba_fwd(x, prm, plan)
    x3, sv_f0 = _ffn_ple_fwd(x1, p[0], prm, 0, plan)
    x4, sv_a = _attn_mixer_fwd(x3, prm, plan)
    x6, sv_f1 = _ffn_ple_fwd(x4, p[1], prm, 1, plan)
    dy, loss_row = _loss_head(x6, target)
    dx4 = _ffn_ple_bwd(dy, p[1], prm, 1, sv_f1, grads, plan)
    dx3 = _attn_mixer_bwd(dx4, prm, sv_a, grads)
    dx1 = _ffn_ple_bwd(dx3, p[0], prm, 0, sv_f0, grads, plan)
    dx0 = _mamba_bwd(dx1, prm, sv_m, grads, plan)
    return loss_row, dx0, grads


W_IN_SLAB_ROWS = 1312


def _position():
    return lax.axis_index("x"), lax.axis_index("y"), lax.axis_index("c")


def _other_chips(x, y):
    return [(1 - x, y), (x, 1 - y), (1 - x, 1 - y)]


def _remote(send_sems, recv_sems, k, src, dst, to):
    return pltpu.make_async_remote_copy(src_ref=src, dst_ref=dst, send_sem=send_sems.at[k], recv_sem=recv_sems.at[k],
                                        device_id=to, device_id_type=MESH)


def _gather_side(entries, whole=()):
    n, nw = len(entries), len(whole)

    def first_hop(ins, outs, send_sems, recv_sems):
        x, y, c = _position()
        cps = []
        for j, chip in enumerate(_other_chips(x, y)):
            for e in range(n):
                cps.append(_remote(send_sems, recv_sems, 6 * e + j, ins[e].at[c], outs[e].at[2 * x + y, c], (*chip, c)))
            for e in range(nw):
                cps.append(_remote(send_sems, recv_sems, 6 * n + 3 * e + j, ins[n + e], outs[n + e].at[2 * x + y],
                                   (*chip, c)))
        return cps

    def start(ins, outs, send_sems, recv_sems):
        for cp in first_hop(ins, outs, send_sems, recv_sems):
            cp.start()

    def finish(ins, outs, send_sems, recv_sems):
        x, y, c = _position()
        me, sibling = (x, y, c), (x, y, 1 - c)
        chips = _other_chips(x, y)
        passed_on = []
        for j, (px, py) in enumerate(chips):
            for e in range(n):
                landed = outs[e].at[2 * px + py, c]
                _remote(send_sems, recv_sems, 6 * e + j, landed, landed, me).wait_recv()
                passed_on.append(_remote(send_sems, recv_sems, 6 * e + 3 + j, landed, landed, sibling))
                passed_on[-1].start()
            for e in range(nw):
                landed = outs[n + e].at[2 * px + py]
                _remote(send_sems, recv_sems, 6 * n + 3 * e + j, landed, landed, me).wait_recv()
        for j, (px, py) in enumerate(chips):
            for e in range(n):
                passed = outs[e].at[2 * px + py, 1 - c]
                _remote(send_sems, recv_sems, 6 * e + 3 + j, passed, passed, me).wait_recv()
        for cp in first_hop(ins, outs, send_sems, recv_sems) + passed_on:
            cp.wait_send()

    shapes = [jax.ShapeDtypeStruct((N_CHIPS,) + a.shape, a.dtype) for a in list(entries) + list(whole)]
    return _Side(list(entries) + list(whole), shapes, 6 * n + 3 * nw, start, finish)


def _run_side(side, name):
    si, so = len(side.inputs), len(side.out_shapes)

    def body(*refs):
        ins, outs, send_sems, recv_sems = refs[:si], refs[si:si + so], refs[-2], refs[-1]
        side.start(ins, outs, send_sems, recv_sems)
        side.finish(ins, outs, send_sems, recv_sems)

    side.outputs = list(pl.pallas_call(
        body, name=name, in_specs=[ANY] * si, out_specs=[ANY] * so, out_shape=side.out_shapes,
        scratch_shapes=[pltpu.SemaphoreType.DMA((side.n_sems,)), pltpu.SemaphoreType.DMA((side.n_sems,))],
    )(*side.inputs))
    return side.outputs


def _swap_side(grads):
    n = len(grads)

    def copies(ins, outs, send_sems, recv_sems):
        x, y, c = _position()
        return [_remote(send_sems, recv_sems, e, ins[e].at[:, 1 - c], outs[e], (x, y, 1 - c)) for e in range(n)]

    def start(ins, outs, send_sems, recv_sems):
        for cp in copies(ins, outs, send_sems, recv_sems):
            cp.start()

    def finish(ins, outs, send_sems, recv_sems):
        for cp in copies(ins, outs, send_sems, recv_sems):
            cp.wait()

    shapes = [jax.ShapeDtypeStruct((N_CHIPS,) + g.shape[2:], g.dtype) for g in grads]
    return _Side(grads, shapes, n, start, finish)


def _chip_exchange_side(chipsums):
    n = len(chipsums)

    def copies(ins, outs, send_sems, recv_sems):
        x, y, c = _position()
        return [_remote(send_sems, recv_sems, 3 * e + j, ins[e].at[2 * tx + ty], outs[e].at[j], (tx, ty, c))
                for j, (tx, ty) in enumerate(_other_chips(x, y)) for e in range(n)]

    def start(ins, outs, send_sems, recv_sems):
        for cp in copies(ins, outs, send_sems, recv_sems):
            cp.start()

    def finish(ins, outs, send_sems, recv_sems):
        for cp in copies(ins, outs, send_sems, recv_sems):
            cp.wait()

    shapes = [jax.ShapeDtypeStruct((3,) + cs.shape[1:], cs.dtype) for cs in chipsums]
    return _Side(chipsums, shapes, 3 * n, start, finish)


def _share_halves(totals):
    n = len(totals)

    def body(*refs):
        t_refs, r_refs = refs[:n], refs[n:2 * n]
        send_sems, recv_sems = refs[2 * n], refs[2 * n + 1]
        x, y, c = _position()
        cps = [pltpu.make_async_remote_copy(src_ref=t_refs[e], dst_ref=r_refs[e], send_sem=send_sems.at[e],
                                            recv_sem=recv_sems.at[e], device_id=(x, y, 1 - c), device_id_type=MESH)
               for e in range(n)]
        for cp in cps:
            cp.start()
        for cp in cps:
            cp.wait()

    return pl.pallas_call(
        body, name="grad_share_halves", in_specs=[ANY] * n, out_specs=[ANY] * n,
        out_shape=[jax.ShapeDtypeStruct(t.shape, t.dtype) for t in totals],
        scratch_shapes=[pltpu.SemaphoreType.DMA((n,)), pltpu.SemaphoreType.DMA((n,))],
    )(*totals)


def _reduce_rows(h):
    return h if h <= 704 else h // 2


def _add_sibling(grad, recv, c_idx, *, name):
    _, _, h, cw = grad.shape
    th = _reduce_rows(h)

    def body(c_ref, g_ref, r_ref, o_ref):
        o_ref[...] = (g_ref[...] + r_ref[...]).astype(BF16)

    return pl.pallas_call(
        body, name=name,
        grid_spec=pltpu.PrefetchScalarGridSpec(
            num_scalar_prefetch=1, grid=(N_CHIPS, h // th),
            in_specs=[pl.BlockSpec((None, None, th, cw), lambda s, i, c_ref: (s, c_ref[0], i, 0)),
                      pl.BlockSpec((None, th, cw), lambda s, i, c_ref: (s, i, 0))],
            out_specs=pl.BlockSpec((None, th, cw), lambda s, i, c_ref: (s, i, 0))),
        out_shape=jax.ShapeDtypeStruct((N_CHIPS, h, cw), BF16),
        compiler_params=_params("parallel", "parallel"),
    )(c_idx, grad, recv)


def _add_chips(chipsum, recv, s_idx, *, name):
    _, h, cw = chipsum.shape
    th = _reduce_rows(h)

    def body(s_ref, own_ref, r_ref, o_ref):
        o_ref[...] = ((own_ref[...].astype(F32) + r_ref[0].astype(F32)) + r_ref[1].astype(F32)) + r_ref[2].astype(F32)

    return pl.pallas_call(
        body, name=name,
        grid_spec=pltpu.PrefetchScalarGridSpec(
            num_scalar_prefetch=1, grid=(h // th,),
            in_specs=[pl.BlockSpec((None, th, cw), lambda i, s_ref: (s_ref[0], i, 0)),
                      pl.BlockSpec((3, th, cw), lambda i, s_ref: (0, i, 0))],
            out_specs=pl.BlockSpec((th, cw), lambda i, s_ref: (i, 0))),
        out_shape=jax.ShapeDtypeStruct((h, cw), F32),
        compiler_params=_params("parallel"),
    )(s_idx, chipsum, recv)


def _adamw_math(w, g, m, v):
    m = ADAM_B1 * m + (1.0 - ADAM_B1) * g
    v = ADAM_B2 * v + (1.0 - ADAM_B2) * (g * g)
    m_hat = m / (1.0 - ADAM_B1 ** ADAM_STEP)
    v_hat = v / (1.0 - ADAM_B2 ** ADAM_STEP)
    delta = -ADAM_LR * (m_hat / (jnp.sqrt(v_hat) + ADAM_EPS) + ADAM_WD * w)
    return delta, m, v


ADAM_TILE_ELEMS = 256 * 1024


def _adamw(w, g, m, v, *, name):
    layers, rows, cols = w.shape
    tr = rows
    for cand in range(8, rows, 8):
        if rows % cand == 0 and cand * cols <= ADAM_TILE_ELEMS:
            tr = cand
    if rows * cols <= ADAM_TILE_ELEMS:
        tr = rows

    def body(w_ref, g_ref, m_ref, v_ref, d_ref, nm_ref, nv_ref):
        d, nm, nv = _adamw_math(w_ref[...], g_ref[...], m_ref[...], v_ref[...])
        d_ref[...] = d
        nm_ref[...] = nm
        nv_ref[...] = nv

    blk = pl.BlockSpec((None, tr, cols), lambda l, i: (l, i, 0))
    sds = jax.ShapeDtypeStruct(w.shape, F32)
    return pl.pallas_call(
        body, name=name, grid=(layers, rows // tr), in_specs=[blk] * 4, out_specs=[blk] * 3, out_shape=[sds] * 3,
        compiler_params=_params("parallel", "parallel"),
    )(w, g, m, v)


SMALL_LAYOUT = (("loss", 1), ("norm_mix", 16), ("norm_ffn", 16), ("ssm_conv_b", 24), ("ssm_dt_bias", 1),
                ("ssm_a_log", 1), ("ssm_d_skip", 1), ("ssm_norm_w", 16), ("att_q_norm", 1), ("att_k_norm", 1),
                ("conv_w_full", 96))
SMALL_ROWS = 176
N_DEVICES = 8


def _small_pack(values):
    parts = []
    for name, rows in SMALL_LAYOUT:
        flat = values[name].reshape(-1).astype(F32)
        parts.append(jnp.pad(flat, (0, rows * LANES - flat.shape[0])).reshape(rows, LANES))
    used = sum(r for _, r in SMALL_LAYOUT)
    parts.append(jnp.zeros((SMALL_ROWS - used, LANES), F32))
    return jnp.concatenate(parts, axis=0)


def _small_unpack(pack, shapes):
    out, off = {}, 0
    for name, rows in SMALL_LAYOUT:
        shape = shapes[name]
        n = math.prod(shape)
        out[name] = pack[off:off + rows].reshape(-1)[:n].reshape(shape)
        off += rows
    return out


def _small_allreduce_adamw(g, w, m, v):
    def body(g_ref, w_ref, m_ref, v_ref, gs_ref, d_ref, nm_ref, nv_ref, buf, send_sems, recv_sems):
        x, y, c = _position()
        pos = (x, y, c)
        me = 4 * x + 2 * y + c
        buf[me] = g_ref[...]
        peers = []
        for k in range(1, N_DEVICES):
            bits = ((k >> 2) & 1, (k >> 1) & 1, k & 1)
            peers.append(tuple(1 - p if b else p for p, b in zip(pos, bits)))
        cps = [pltpu.make_async_remote_copy(src_ref=g_ref, dst_ref=buf.at[me], send_sem=send_sems.at[k],
                                            recv_sem=recv_sems.at[k], device_id=peer, device_id_type=MESH)
               for k, peer in enumerate(peers)]
        for cp in cps:
            cp.start()
        for k, (px, py, pc) in enumerate(peers):
            pltpu.make_async_remote_copy(src_ref=g_ref, dst_ref=buf.at[4 * px + 2 * py + pc],
                                         send_sem=send_sems.at[k], recv_sem=recv_sems.at[k],
                                         device_id=(px, py, pc), device_id_type=MESH).wait_recv()
        for cp in cps:
            cp.wait_send()
        total = buf[0]
        for dev in range(1, N_DEVICES):
            total = total + buf[dev]
        gs_ref[...] = total
        d, nm, nv = _adamw_math(w_ref[...], total, m_ref[...], v_ref[...])
        d_ref[...] = d
        nm_ref[...] = nm
        nv_ref[...] = nv

    vm = pl.BlockSpec(memory_space=pltpu.VMEM)
    sds = jax.ShapeDtypeStruct((SMALL_ROWS, LANES), F32)
    return pl.pallas_call(
        body, name="small_allreduce_adamw", in_specs=[vm] * 4, out_specs=[vm] * 4, out_shape=[sds] * 4,
        scratch_shapes=[pltpu.VMEM((N_DEVICES, SMALL_ROWS, LANES), F32),
                        pltpu.SemaphoreType.DMA((N_DEVICES - 1,)), pltpu.SemaphoreType.DMA((N_DEVICES - 1,))],
    )(g, w, m, v)


SMALL = tuple(n for n, _ in SMALL_LAYOUT if n not in ("loss", "conv_w_full"))
WEIGHTS = ("norm_mix", "norm_ffn", "ssm_w_in", "ssm_conv_w", "ssm_conv_b", "ssm_dt_bias", "ssm_a_log", "ssm_d_skip",
           "ssm_norm_w", "ssm_w_out", "att_w_qkv", "att_q_norm", "att_k_norm", "att_w_o", "ffn_w_gate", "ffn_w_up",
           "ffn_w_down", "ple_w_proj", "ple_w_gate")
COLUMN_SHARDED = ("ssm_w_in", "att_w_qkv", "ffn_w_gate", "ffn_w_up", "ple_w_proj")
LAYERED = ("ffn_w_gate", "ffn_w_up", "ffn_w_down", "ple_w_proj", "ple_w_gate")
GATHER_ORDER = ("ssm_w_in", "ssm_w_out", "att_w_qkv", "att_w_o", "ffn_w_gate", "ffn_w_up", "ffn_w_down",
                "ple_w_proj", "ple_w_gate")


def _layers(n):
    return (0, 1) if n in LAYERED else (None,)


def _tag(key):
    return key[0] if key[1] is None else f"{key[0]}_{key[1]}"


QKV_PARTS = 3


def _weight_slab(w, key):
    n, i = key
    if n == "att_w_qkv":
        a = w[n][0].T
        rows = a.shape[0] // QKV_PARTS
        a = a[i * rows:(i + 1) * rows]
    else:
        a = w[n][0 if i is None else i]
        a = a.T if n in COLUMN_SHARDED else a
    if n == "ssm_w_in":
        a = jnp.pad(a, ((0, W_IN_SLAB_ROWS - a.shape[0]), (0, 0)))
    return a.reshape(2, a.shape[0] // 2, a.shape[1]).astype(BF16)


def _install(prm, key, gathered, own, s_me):
    n, i = key
    full = lax.dynamic_update_slice(gathered, own[None], (s_me, 0, 0, 0))
    full = full.reshape(N_CHIPS, 2 * full.shape[2], full.shape[3])
    if n == "att_w_qkv":
        parts = prm.setdefault("att_w_qkv_parts", {})
        parts[i] = full
        if len(parts) == QKV_PARTS:
            prm[n] = jnp.stack([parts[j] for j in range(QKV_PARTS)], axis=1).reshape(-1, D_MODEL)
        return
    if n == "ssm_w_in":
        rows = (D_INNER + CONV_DIM + SSM_HEADS) // N_CHIPS
        w_in_t = full[:, :rows].reshape(N_CHIPS * rows, D_MODEL)
        prm["ssm_w_z"] = w_in_t[:D_INNER]
        prm["ssm_w_xbc"] = w_in_t[D_INNER:D_INNER + CONV_DIM]
        prm["ssm_w_dt"] = jnp.pad(w_in_t[D_INNER + CONV_DIM:], ((0, LANES - SSM_HEADS), (0, 0)))
        return
    full = full.reshape(N_CHIPS * full.shape[1], full.shape[2])
    if i is None:
        prm[n] = full
    else:
        prm.setdefault(n, [None, None])[i] = full


def _grad_slab(grads, key):
    n, i = key
    g = grads[n] if i is None else grads[n][i]
    if n == "ssm_w_in":
        g = jnp.pad(g.reshape(N_CHIPS, g.shape[0] // N_CHIPS, D_MODEL),
                    ((0, 0), (0, W_IN_SLAB_ROWS - g.shape[0] // N_CHIPS), (0, 0)))
    rows = g.size // (N_CHIPS * g.shape[-1])
    return g.reshape(N_CHIPS, 2, rows // 2, g.shape[-1])


def _natural_shard(n, reduced, shape):
    def one(r):
        if n == "ssm_w_in":
            r = r[:shape[-1]]
        return r.T if n in COLUMN_SHARDED else r
    if n in LAYERED:
        return jnp.stack([one(r) for r in reduced]).reshape(shape)
    return one(reduced[0]).reshape(shape)


def kernel(x, p, norm_mix, norm_ffn, ssm_w_in, ssm_conv_w, ssm_conv_b, ssm_dt_bias, ssm_a_log, ssm_d_skip, ssm_norm_w, ssm_w_out, att_w_qkv, att_q_norm, att_k_norm, att_w_o, ffn_w_gate, ffn_w_up, ffn_w_down, ple_w_proj, ple_w_gate, loss_target, m_norm_mix, m_norm_ffn, m_ssm_w_in, m_ssm_conv_w, m_ssm_conv_b, m_ssm_dt_bias, m_ssm_a_log, m_ssm_d_skip, m_ssm_norm_w, m_ssm_w_out, m_att_w_qkv, m_att_q_norm, m_att_k_norm, m_att_w_o, m_ffn_w_gate, m_ffn_w_up, m_ffn_w_down, m_ple_w_proj, m_ple_w_gate, v_norm_mix, v_norm_ffn, v_ssm_w_in, v_ssm_conv_w, v_ssm_conv_b, v_ssm_dt_bias, v_ssm_a_log, v_ssm_d_skip, v_ssm_norm_w, v_ssm_w_out, v_att_w_qkv, v_att_q_norm, v_att_k_norm, v_att_w_o, v_ffn_w_gate, v_ffn_w_up, v_ffn_w_down, v_ple_w_proj, v_ple_w_gate):
    given = dict(locals())
    w = {n: given[n] for n in WEIGHTS}
    m = {n: given["m_" + n] for n in WEIGHTS}
    v = {n: given["v_" + n] for n in WEIGHTS}
    c_idx = lax.axis_index("c").astype(jnp.int32).reshape(1)
    s_idx = (2 * lax.axis_index("x") + lax.axis_index("y")).astype(jnp.int32).reshape(1)

    s_me = 2 * lax.axis_index("x") + lax.axis_index("y")
    first_core = lax.axis_index("c") == 0

    qkv_parts = [("att_w_qkv", j) for j in range(QKV_PARTS)]
    gather_plan = {
        "ssm_in_z": [("ssm_w_out", None)],
        "ssm_in_xbc": [("ffn_w_gate", 0)],
        "conv_fwd": [("ffn_w_up", 0)],
        "ssd_fwd": [("ffn_w_down", 0), ("ple_w_proj", 0), ("ple_w_gate", 0), ("att_w_o", None)],
        "swiglu_fwd_0": qkv_parts[:2],
        "ffn_down_0": qkv_parts[2:],
        "att_qkv": [(n, 1) for n in LAYERED],
    }
    mamba = [("ssm_w_in", None)]
    own = {k: _weight_slab(w, k) for k in mamba + sum(gather_plan.values(), [])}
    prm = {n: w[n] for n in SMALL}

    def land(group, outputs):
        for k, g in zip(group, outputs):
            _install(prm, k, g, own[k], s_me)

    first = _gather_side([own[k] for k in mamba], whole=[ssm_conv_w[0]])
    _run_side(first, "gather_mamba")
    land(mamba, first.outputs)
    conv = lax.dynamic_update_slice(first.outputs[-1], ssm_conv_w, (s_me, 0, 0))
    prm["ssm_conv_w"] = conv.transpose(1, 0, 2).reshape(CONV_WIDTH, CONV_DIM)

    layer1 = [("att_w_qkv", None), ("att_w_o", None)] + [(n, 1) for n in LAYERED]
    ffn0 = [(n, 0) for n in LAYERED] + [("ssm_w_out", None)]
    reduce_plan = {"swiglu_bwd_0": ("swap", layer1), "ssd_bwd": ("exchange", layer1),
                   "gate_norm_bwd": ("swap", ffn0), "conv_bwd": ("exchange", ffn0)}
    state = {}

    def swap_side(group):
        state[_tag(group[0]), "g4"] = g4 = [_grad_slab(state["grads"], k) for k in group]
        return _swap_side(g4)

    def add_siblings(group, from_sibling):
        state[_tag(group[0]), "chipsums"] = [
            _add_sibling(g, r, c_idx, name="add_sibling_" + _tag(k))
            for g, r, k in zip(state[_tag(group[0]), "g4"], from_sibling, group)]

    def exchange_side(group):
        return _chip_exchange_side(state[_tag(group[0]), "chipsums"])

    def add_chips(group, from_chips):
        for k, cs, r in zip(group, state[_tag(group[0]), "chipsums"], from_chips):
            state["total", k] = _add_chips(cs, r, s_idx, name="add_chips_" + _tag(k))

    class Plan(_NoOverlap):
        def __init__(self):
            self.carried = {host: _gather_side([own[k] for k in group]) for host, group in gather_plan.items()}

        def begin_backward(self, grads):
            state["grads"] = grads

        def side(self, host):
            if host in reduce_plan:
                step, group = reduce_plan[host]
                self.carried[host] = swap_side(group) if step == "swap" else exchange_side(group)
            return self.carried.get(host)

        def after(self, host):
            if host in gather_plan:
                land(gather_plan[host], self.carried[host].outputs)
            elif host in reduce_plan:
                step, group = reduce_plan[host]
                (add_siblings if step == "swap" else add_chips)(group, self.carried[host].outputs)

    loss_row, dx, grads = _local_step(x[0], p[:, 0], loss_target[0], prm, Plan())

    add_siblings(mamba, _run_side(swap_side(mamba), "grad_swap_mamba"))
    add_chips(mamba, _run_side(exchange_side(mamba), "grad_exchange_mamba"))
    order = mamba + ffn0 + layer1
    shared = _share_halves([state["total", k] for k in order])
    reduced = {}
    for k, theirs in zip(order, shared):
        lo = jnp.where(first_core, state["total", k], theirs)
        hi = jnp.where(first_core, theirs, state["total", k])
        reduced.setdefault(k[0], {})[k[1]] = jnp.concatenate([lo, hi], axis=0)
    reduced = {n: [by_layer[i] for i in _layers(n)] for n, by_layer in reduced.items()}

    grad, delta, new_m, new_v = {}, {}, {}, {}
    for n in GATHER_ORDER:
        grad[n] = _natural_shard(n, reduced[n], w[n].shape)
        delta[n], new_m[n], new_v[n] = _adamw(w[n], grad[n], m[n], v[n], name="adamw_" + n)

    small_g = {n: (jnp.stack(grads[n]) if isinstance(grads[n], list) else grads[n]) for n in SMALL}
    small_g["loss"] = loss_row
    small_g["conv_w_full"] = grads["ssm_conv_w"]
    zero = {"loss": jnp.zeros((1, LANES), F32), "conv_w_full": jnp.zeros((CONV_WIDTH, CONV_DIM), F32)}
    outs = _small_allreduce_adamw(_small_pack(small_g), _small_pack({**w, **zero}), _small_pack({**m, **zero}),
                                  _small_pack({**v, **zero}))
    shapes = {n: w[n].shape for n in SMALL}
    shapes["loss"] = (1, LANES)
    shapes["conv_w_full"] = (CONV_WIDTH, CONV_DIM)
    sg, sd, sm, sv = [_small_unpack(o, shapes) for o in outs]
    for n in SMALL:
        grad[n], delta[n], new_m[n], new_v[n] = sg[n], sd[n], sm[n], sv[n]
    loss = sg["loss"][0, 0]
    conv_cols = CONV_DIM // N_CHIPS
    grad["ssm_conv_w"] = lax.dynamic_slice(sg["conv_w_full"], (0, s_me * conv_cols), (CONV_WIDTH, conv_cols))[None]
    delta["ssm_conv_w"], new_m["ssm_conv_w"], new_v["ssm_conv_w"] = _adamw(
        ssm_conv_w, grad["ssm_conv_w"], m_ssm_conv_w, v_ssm_conv_w, name="adamw_ssm_conv_w")

    return (loss, dx[None], *[grad[n] for n in WEIGHTS], *[delta[n] for n in WEIGHTS],
            *[new_m[n] for n in WEIGHTS], *[new_v[n] for n in WEIGHTS])
```

```python
import functools
import math

import jax
import jax.numpy as jnp
from jax import lax
from jax.experimental import pallas as pl
from jax.experimental.pallas import tpu as pltpu

F32 = jnp.float32
BF16 = jnp.bfloat16
HIGHEST = lax.Precision.HIGHEST

NORM_EPS = 1e-6
ADAM_LR, ADAM_B1, ADAM_B2, ADAM_EPS, ADAM_WD, ADAM_STEP = 0.001, 0.9, 0.999, 1e-08, 0.01, 10

D_MODEL = 1024
D_INNER = 2048
SSM_HEADS = 32
SSM_HEAD_DIM = 64
SSM_GROUPS = 4
SSM_STATE = 128
SSD_CHUNK = 128
CONV_DIM = 3072
CONV_WIDTH = 4
ATT_HEADS = 16
ATT_HEAD_DIM = 64
DIL_PATTERNS = ((128, 1), (512, 4), (2048, 16))
ATT_BLOCK = 128
FFN_HIDDEN = 2816
PLE_DIM = 256

LANES = 128
V7X_VMEM_LIMIT = 56 * 1024 * 1024
NEG_BIG = -1e30

N_CHIPS = 4


def _params(*sem):
    return pltpu.CompilerParams(dimension_semantics=sem, vmem_limit_bytes=V7X_VMEM_LIMIT)


def _tile(n, pref):
    if n <= pref:
        return n
    best = None
    for t in range(LANES, pref + 1, LANES):
        if n % t == 0:
            best = t
    assert best is not None, (n, pref)
    return best


def _sigmoid(v):
    return 1.0 / (1.0 + jnp.exp(-v))


def _dot(a, b):
    return jnp.dot(a, b, preferred_element_type=F32)


def _dot_nt(a, b):
    return lax.dot_general(a, b, (((1,), (1,)), ((), ())), preferred_element_type=F32)


def _dot_tn(a, b):
    return lax.dot_general(a, b, (((0,), (0,)), ((), ())), preferred_element_type=F32)


def _head_block_diag():
    i = lax.broadcasted_iota(jnp.int32, (LANES, LANES), 0) // ATT_HEAD_DIM
    j = lax.broadcasted_iota(jnp.int32, (LANES, LANES), 1) // ATT_HEAD_DIM
    return (i == j).astype(BF16)


def _split_dot(ones, z):
    hi = z.astype(BF16)
    lo = (z - hi.astype(F32)).astype(BF16)
    return _dot(ones, hi) + _dot(ones, lo)


def _head_sums(z, bd, terms=2):
    hi = z.astype(BF16)
    lo = (z - hi.astype(F32)).astype(BF16) if terms == 2 else None
    parts = []
    for t in range(z.shape[1] // LANES):
        sl = slice(t * LANES, (t + 1) * LANES)
        part = _dot(hi[:, sl], bd)
        parts.append(part + _dot(lo[:, sl], bd) if terms == 2 else part)
    return parts[0] if len(parts) == 1 else jnp.concatenate(parts, axis=1)


def _lane_lt64(rows):
    return lax.broadcasted_iota(jnp.int32, (rows, LANES), 1) < ATT_HEAD_DIM


MESH = pl.DeviceIdType.MESH
ANY = pl.BlockSpec(memory_space=pl.ANY)


class _Side:
    def __init__(self, inputs, out_shapes, n_sems, start, finish):
        self.inputs, self.out_shapes, self.n_sems = list(inputs), list(out_shapes), n_sems
        self.start, self.finish = start, finish
        self.outputs = None


def _call(body, side, *, name, grid, in_specs, out_specs, out_shape, scratch_shapes, semantics, args):
    in_specs, out_specs, out_shape = list(in_specs), list(out_specs), list(out_shape)
    scratch_shapes = list(scratch_shapes)
    if side is None:
        return pl.pallas_call(body, name=name, grid=grid, in_specs=in_specs, out_specs=out_specs,
                              out_shape=out_shape, scratch_shapes=scratch_shapes,
                              compiler_params=_params(*semantics))(*args)
    ni, no, ns = len(in_specs), len(out_specs), len(scratch_shapes)
    si, so = len(side.inputs), len(side.out_shapes)

    def hosted(*refs):
        ins, s_ins = refs[:ni], refs[ni:ni + si]
        outs, s_outs = refs[ni + si:ni + si + no], refs[ni + si + no:ni + si + no + so]
        scratch = refs[ni + si + no + so:ni + si + no + so + ns]
        send_sems, recv_sems = refs[-2], refs[-1]
        first = pl.program_id(0) == 0
        last = pl.program_id(0) == grid[0] - 1
        for axis in range(1, len(grid)):
            first = jnp.logical_and(first, pl.program_id(axis) == 0)
            last = jnp.logical_and(last, pl.program_id(axis) == grid[axis] - 1)

        @pl.when(first)
        def _():
            side.start(s_ins, s_outs, send_sems, recv_sems)

        body(*ins, *outs, *scratch)

        @pl.when(last)
        def _():
            side.finish(s_ins, s_outs, send_sems, recv_sems)

    res = pl.pallas_call(
        hosted, name=name, grid=grid, in_specs=in_specs + [ANY] * si, out_specs=out_specs + [ANY] * so,
        out_shape=out_shape + side.out_shapes,
        scratch_shapes=scratch_shapes + [pltpu.SemaphoreType.DMA((side.n_sems,)),
                                         pltpu.SemaphoreType.DMA((side.n_sems,))],
        compiler_params=_params(*["arbitrary"] * len(grid)),
    )(*args, *side.inputs)
    side.outputs = list(res[no:])
    return list(res[:no])


def _matmul(a, b, *, mode, name, out_dtype=F32, addend=None, tm=1024, tn=512, tk_max=3072, side=None, second=None):
    m, k = a.shape
    if mode == "nn":
        k2, n = b.shape
    else:
        n, k2 = b.shape
    assert k == k2, (a.shape, b.shape, mode)
    tm, tn, tk = _tile(m, tm), _tile(n, tn), _tile(k, tk_max)
    nk = k // tk
    has_add = addend is not None
    n_rows = len(second[1]) if second else 0
    n_out = 2 if second else 1

    def body(*refs):
        a_ref, b_ref = refs[0], refs[1]
        add_ref = refs[2] if has_add else None
        row_refs = refs[2 + has_add:2 + has_add + n_rows]
        o_ref, acc_ref = refs[-1 - n_out], refs[-1]
        kk = pl.program_id(2)
        col_tile = pl.program_id(1)
        av = a_ref[...].astype(BF16)
        bv = b_ref[...].astype(BF16)
        part = _dot(av, bv) if mode == "nn" else _dot_nt(av, bv)

        @pl.when(kk == 0)
        def _():
            acc_ref[...] = part

        @pl.when(kk > 0)
        def _():
            acc_ref[...] += part

        @pl.when(kk == nk - 1)
        def _():
            res = acc_ref[...]
            if has_add:
                res = res + add_ref[...]
            o_ref[...] = res.astype(out_dtype)
            if second:
                refs[-2][...] = second[0](res, col_tile, *row_refs)

    a_spec = pl.BlockSpec((tm, tk), lambda i, j, kk: (i, kk))
    if mode == "nn":
        b_spec = pl.BlockSpec((tk, tn), lambda i, j, kk: (kk, j))
    else:
        b_spec = pl.BlockSpec((tn, tk), lambda i, j, kk: (j, kk))
    tile = pl.BlockSpec((tm, tn), lambda i, j, kk: (i, j))
    in_specs = [a_spec, b_spec]
    args = [a, b]
    if has_add:
        in_specs.append(tile)
        args.append(addend)
    if second:
        in_specs += [pl.BlockSpec((1, tn), lambda i, j, kk: (0, j))] * n_rows
        args += list(second[1])
    outs = _call(
        body, side, name=name, grid=(m // tm, n // tn, nk),
        in_specs=in_specs, out_specs=[tile] * n_out,
        out_shape=[jax.ShapeDtypeStruct((m, n), out_dtype)] + [jax.ShapeDtypeStruct((m, n), F32)] * (n_out - 1),
        scratch_shapes=[pltpu.VMEM((tm, tn), F32)],
        semantics=("parallel", "parallel", "arbitrary"), args=args,
    )
    return outs if second else outs[0]


def _matmul_tn(a, b, *, name, tm=1408, tn=512, tk=1024):
    t, m = a.shape
    t2, n = b.shape
    assert t == t2
    tm, tn, tk = _tile(m, tm), _tile(n, tn), _tile(t, tk)

    def body(a_ref, b_ref, o_ref):
        part = _dot_tn(a_ref[...].astype(BF16), b_ref[...].astype(BF16))

        @pl.when(pl.program_id(2) == 0)
        def _():
            o_ref[...] = part

        @pl.when(pl.program_id(2) > 0)
        def _():
            o_ref[...] += part

    return pl.pallas_call(
        body, name=name, grid=(m // tm, n // tn, t // tk),
        in_specs=[pl.BlockSpec((tk, tm), lambda i, j, kk: (kk, i)),
                  pl.BlockSpec((tk, tn), lambda i, j, kk: (kk, j))],
        out_specs=pl.BlockSpec((tm, tn), lambda i, j, kk: (i, j)),
        out_shape=jax.ShapeDtypeStruct((m, n), F32),
        compiler_params=_params("parallel", "parallel", "arbitrary"),
    )(a, b)


def _rmsnorm_fwd(x, gain, *, name):
    t, d = x.shape
    tm = _tile(t, 512)

    def body(x_ref, g_ref, o_ref):
        xv = x_ref[...]
        r = lax.rsqrt(jnp.mean(xv * xv, axis=-1, keepdims=True) + NORM_EPS)
        o_ref[...] = (xv * r * g_ref[...]).astype(BF16)

    return pl.pallas_call(
        body, name=name, grid=(t // tm,),
        in_specs=[pl.BlockSpec((tm, d), lambda i: (i, 0)), pl.BlockSpec((1, d), lambda i: (0, 0))],
        out_specs=pl.BlockSpec((tm, d), lambda i: (i, 0)),
        out_shape=jax.ShapeDtypeStruct((t, d), BF16),
        compiler_params=_params("parallel"),
    )(x, gain)


def _rmsnorm_bwd(x, gain, dy, dres, *, name):
    t, d = x.shape
    tm = _tile(t, 512)

    def body(x_ref, g_ref, dy_ref, dres_ref, dx_ref, dg_ref):
        xv = x_ref[...]
        r = lax.rsqrt(jnp.mean(xv * xv, axis=-1, keepdims=True) + NORM_EPS)
        xh = xv * r
        dyv = dy_ref[...]
        dxh = dyv * g_ref[...]
        mean = jnp.mean(dxh * xh, axis=-1, keepdims=True)
        dx_ref[...] = dres_ref[...] + r * (dxh - xh * mean)
        part = jnp.sum(dyv * xh, axis=0, keepdims=True)

        @pl.when(pl.program_id(0) == 0)
        def _():
            dg_ref[...] = part

        @pl.when(pl.program_id(0) > 0)
        def _():
            dg_ref[...] += part

    row = pl.BlockSpec((tm, d), lambda i: (i, 0))
    vec = pl.BlockSpec((1, d), lambda i: (0, 0))
    return pl.pallas_call(
        body, name=name, grid=(t // tm,),
        in_specs=[row, vec, row, row], out_specs=[row, vec],
        out_shape=[jax.ShapeDtypeStruct((t, d), F32), jax.ShapeDtypeStruct((1, d), F32)],
        compiler_params=_params("arbitrary"),
    )(x, gain, dy, dres)


def _loss_head(y, target):
    t, d = y.shape
    tm = _tile(t, 512)
    steps = t // tm

    def body(y_ref, t_ref, dy_ref, l_ref, acc_ref):
        e = y_ref[...] - t_ref[...]
        dy_ref[...] = e * (1.0 / d)
        part = jnp.sum(e * e, axis=0, keepdims=True)

        @pl.when(pl.program_id(0) == 0)
        def _():
            acc_ref[...] = part

        @pl.when(pl.program_id(0) > 0)
        def _():
            acc_ref[...] += part

        @pl.when(pl.program_id(0) == steps - 1)
        def _():
            l_ref[...] = jnp.full((1, LANES), (0.5 / d), F32) * jnp.sum(acc_ref[...])

    row = pl.BlockSpec((tm, d), lambda i: (i, 0))
    return pl.pallas_call(
        body, name="loss_head", grid=(steps,),
        in_specs=[row, row], out_specs=[row, pl.BlockSpec((1, LANES), lambda i: (0, 0))],
        out_shape=[jax.ShapeDtypeStruct((t, d), F32), jax.ShapeDtypeStruct((1, LANES), F32)],
        scratch_shapes=[pltpu.VMEM((1, d), F32)],
        compiler_params=_params("arbitrary"),
    )(y, target)


def _swiglu_fwd(h, w_gate_t, w_up_t, *, name, side=None):
    t, d = h.shape
    f = w_gate_t.shape[0]
    tm, tn = _tile(t, 1024), _tile(f, 256)

    def body(h_ref, wg_ref, wu_ref, g_ref, u_ref, a_ref):
        hv = h_ref[...]
        g = _dot_nt(hv, wg_ref[...])
        u = _dot_nt(hv, wu_ref[...])
        g_ref[...] = g.astype(BF16)
        u_ref[...] = u.astype(BF16)
        a_ref[...] = (g * _sigmoid(g) * u).astype(BF16)

    wspec = pl.BlockSpec((tn, d), lambda i, j: (j, 0))
    ospec = pl.BlockSpec((tm, tn), lambda i, j: (i, j))
    return _call(
        body, side, name=name, grid=(t // tm, f // tn),
        in_specs=[pl.BlockSpec((tm, d), lambda i, j: (i, 0)), wspec, wspec],
        out_specs=[ospec, ospec, ospec],
        out_shape=[jax.ShapeDtypeStruct((t, f), BF16), jax.ShapeDtypeStruct((t, f), BF16),
                   jax.ShapeDtypeStruct((t, f), BF16)],
        scratch_shapes=[], semantics=("parallel", "parallel"), args=(h, w_gate_t, w_up_t),
    )


def _swiglu_bwd(dx, w_down, g, u, *, name, side=None):
    t, d = dx.shape
    f = w_down.shape[0]
    tm, tn = _tile(t, 1024), _tile(f, 256)

    def body(dx_ref, wd_ref, g_ref, u_ref, dg_ref, du_ref):
        dact = _dot_nt(dx_ref[...].astype(BF16), wd_ref[...])
        gv, uv = g_ref[...].astype(F32), u_ref[...].astype(F32)
        sg = _sigmoid(gv)
        dg_ref[...] = (dact * uv * sg * (1.0 + gv * (1.0 - sg))).astype(BF16)
        du_ref[...] = (dact * gv * sg).astype(BF16)

    ospec = pl.BlockSpec((tm, tn), lambda i, j: (i, j))
    return _call(
        body, side, name=name, grid=(t // tm, f // tn),
        in_specs=[pl.BlockSpec((tm, d), lambda i, j: (i, 0)), pl.BlockSpec((tn, d), lambda i, j: (j, 0)),
                  ospec, ospec],
        out_specs=[ospec, ospec],
        out_shape=[jax.ShapeDtypeStruct((t, f), BF16), jax.ShapeDtypeStruct((t, f), BF16)],
        scratch_shapes=[], semantics=("parallel", "parallel"), args=(dx, w_down, g, u),
    )


def _ple_fwd(x, p, w_gate, w_proj_t, *, name):
    t, d = x.shape
    e = p.shape[1]
    tm, tn = _tile(t, 1024), _tile(d, 512)

    def body(xf_ref, xr_ref, p_ref, wg_ref, wp_ref, o_ref):
        s = _dot(xf_ref[...].astype(BF16), wg_ref[...])
        ple = _dot_nt(p_ref[...].astype(BF16), wp_ref[...])
        o_ref[...] = xr_ref[...] + _sigmoid(s) * ple

    return pl.pallas_call(
        body, name=name, grid=(t // tm, d // tn),
        in_specs=[pl.BlockSpec((tm, d), lambda i, j: (i, 0)), pl.BlockSpec((tm, tn), lambda i, j: (i, j)),
                  pl.BlockSpec((tm, e), lambda i, j: (i, 0)), pl.BlockSpec((d, tn), lambda i, j: (0, j)),
                  pl.BlockSpec((tn, e), lambda i, j: (j, 0))],
        out_specs=pl.BlockSpec((tm, tn), lambda i, j: (i, j)),
        out_shape=jax.ShapeDtypeStruct((t, d), F32),
        compiler_params=_params("parallel", "parallel"),
    )(x, x, p, w_gate, w_proj_t)


def _ple_bwd(x, p, w_gate, w_proj_t, dout, *, name):
    t, d = x.shape
    e = p.shape[1]
    tm, tn = _tile(t, 1024), _tile(d, 512)

    def body(xf_ref, p_ref, wg_ref, wp_ref, do_ref, ds_ref, dple_ref):
        s = _dot(xf_ref[...].astype(BF16), wg_ref[...])
        ple = _dot_nt(p_ref[...].astype(BF16), wp_ref[...])
        gate = _sigmoid(s)
        dov = do_ref[...]
        dple_ref[...] = (dov * gate).astype(BF16)
        ds_ref[...] = (dov * ple * gate * (1.0 - gate)).astype(BF16)

    ospec = pl.BlockSpec((tm, tn), lambda i, j: (i, j))
    return pl.pallas_call(
        body, name=name, grid=(t // tm, d // tn),
        in_specs=[pl.BlockSpec((tm, d), lambda i, j: (i, 0)), pl.BlockSpec((tm, e), lambda i, j: (i, 0)),
                  pl.BlockSpec((d, tn), lambda i, j: (0, j)), pl.BlockSpec((tn, e), lambda i, j: (j, 0)), ospec],
        out_specs=[ospec, ospec],
        out_shape=[jax.ShapeDtypeStruct((t, d), BF16), jax.ShapeDtypeStruct((t, d), BF16)],
        compiler_params=_params("parallel", "parallel"),
    )(x, p, w_gate, w_proj_t, dout)


CONV_TIME_TILE = 256
CONV_HALO = 8


def _conv_taps(ext, w):
    acc = ext[CONV_HALO:, :] * w[CONV_WIDTH - 1:CONV_WIDTH, :]
    shifted = [ext[CONV_HALO:, :]]
    for j in range(1, CONV_WIDTH):
        sh = pltpu.roll(ext, j, 0)[CONV_HALO:, :]
        shifted.append(sh)
        acc = acc + sh * w[CONV_WIDTH - 1 - j:CONV_WIDTH - j, :]
    return acc, shifted


def _conv_fwd(u, w, b, side=None):
    t, c = u.shape
    tc = _tile(c, 256)
    tt = CONV_TIME_TILE

    def body(u_ref, w_ref, b_ref, o_ref):
        wv, bv = w_ref[...], b_ref[...]

        def tile(start, ext):
            pre = _conv_taps(ext, wv)[0] + bv
            o_ref[pl.ds(start, tt), :] = pre * _sigmoid(pre)

        tile(0, jnp.concatenate([jnp.zeros((CONV_HALO, tc), F32), u_ref[0:tt, :]], axis=0))

        def loop(i, carry):
            start = pl.multiple_of(i * tt, tt)
            tile(start, u_ref[pl.ds(start - CONV_HALO, tt + CONV_HALO), :])
            return carry

        lax.fori_loop(1, t // tt, loop, 0)

    col = pl.BlockSpec((t, tc), lambda j: (0, j))
    return _call(
        body, side, name="conv_fwd", grid=(c // tc,),
        in_specs=[col, pl.BlockSpec((CONV_WIDTH, tc), lambda j: (0, j)), pl.BlockSpec((1, tc), lambda j: (0, j))],
        out_specs=[col], out_shape=[jax.ShapeDtypeStruct((t, c), F32)],
        scratch_shapes=[], semantics=("parallel",), args=(u, w, b),
    )[0]


def _conv_bwd(u, w, b, dact, side=None):
    t, c = u.shape
    tc = _tile(c, 256)
    tt = CONV_TIME_TILE

    def body(u_ref, w_ref, b_ref, da_ref, du_ref, dw_ref, db_ref, dpre_ref):
        wv, bv = w_ref[...], b_ref[...]

        def tile(start, ext, sums):
            acc, shifted = _conv_taps(ext, wv)
            pre = acc + bv
            sg = _sigmoid(pre)
            dpre = da_ref[pl.ds(start, tt), :] * (sg * (1.0 + pre * (1.0 - sg)))
            dpre_ref[pl.ds(start, tt), :] = dpre
            new = [sums[0] + jnp.sum(dpre, axis=0, keepdims=True)]
            for j in range(CONV_WIDTH):
                new.append(sums[1 + j] + jnp.sum(dpre * shifted[j], axis=0, keepdims=True))
            return tuple(new)

        zero = jnp.zeros((1, tc), F32)
        sums = tile(0, jnp.concatenate([jnp.zeros((CONV_HALO, tc), F32), u_ref[0:tt, :]], axis=0),
                    (zero,) * (1 + CONV_WIDTH))

        def loop(i, sums):
            start = pl.multiple_of(i * tt, tt)
            return tile(start, u_ref[pl.ds(start - CONV_HALO, tt + CONV_HALO), :], sums)

        sums = lax.fori_loop(1, t // tt, loop, sums)
        db_ref[...] = sums[0]
        dw_ref[...] = jnp.concatenate([sums[1 + (CONV_WIDTH - 1 - k)] for k in range(CONV_WIDTH)], axis=0)
        dpre_ref[pl.ds(t, CONV_HALO), :] = jnp.zeros((CONV_HALO, tc), F32)

        def loop2(i, carry):
            start = pl.multiple_of(i * tt, tt)
            ext = dpre_ref[pl.ds(start, tt + CONV_HALO), :]
            acc = ext[0:tt, :] * wv[CONV_WIDTH - 1:CONV_WIDTH, :]
            for j in range(1, CONV_WIDTH):
                acc = acc + pltpu.roll(ext, tt + CONV_HALO - j, 0)[0:tt, :] * wv[CONV_WIDTH - 1 - j:CONV_WIDTH - j, :]
            du_ref[pl.ds(start, tt), :] = acc.astype(BF16)
            return carry

        lax.fori_loop(0, t // tt, loop2, 0)

    col = pl.BlockSpec((t, tc), lambda j: (0, j))
    return _call(
        body, side, name="conv_bwd", grid=(c // tc,),
        in_specs=[col, pl.BlockSpec((CONV_WIDTH, tc), lambda j: (0, j)), pl.BlockSpec((1, tc), lambda j: (0, j)), col],
        out_specs=[col, pl.BlockSpec((CONV_WIDTH, tc), lambda j: (0, j)), pl.BlockSpec((1, tc), lambda j: (0, j))],
        out_shape=[jax.ShapeDtypeStruct((t, c), BF16), jax.ShapeDtypeStruct((CONV_WIDTH, c), F32),
                   jax.ShapeDtypeStruct((1, c), F32)],
        scratch_shapes=[pltpu.VMEM((t + CONV_HALO, tc), F32)],
        semantics=("parallel",), args=(u, w, b, dact),
    )


def _softplus(v):
    e = jnp.exp(-jnp.abs(v))
    w = 1.0 + e
    log1p = jnp.where(w == 1.0, e, jnp.log(w) * (e / jnp.where(w == 1.0, 1.0, w - 1.0)))
    return jnp.maximum(v, 0.0) + log1p


def _split3(z):
    hi = z.astype(BF16)
    rest = z - hi.astype(F32)
    mid = rest.astype(BF16)
    return hi, mid, (rest - mid.astype(F32)).astype(BF16)


def _select_dot(z, ones):
    return sum(_dot(term, ones) for term in _split3(z))


def _ssd_prep_fwd(dt_raw, dt_bias, a_log):
    t = dt_raw.shape[0]
    cl = SSD_CHUNK

    def body(r_ref, b_ref, al_ref, acs_ref, dt_rep_ref, acs_rep_ref):
        dt = _softplus(r_ref[...] + b_ref[...])
        adt = dt * (-jnp.exp(al_ref[...]))
        li = lax.broadcasted_iota(jnp.int32, (cl, cl), 0)
        si = lax.broadcasted_iota(jnp.int32, (cl, cl), 1)
        tri = (si <= li).astype(F32)
        acs = jnp.dot(tri, adt, preferred_element_type=F32, precision=HIGHEST)
        acs_ref[...] = acs
        head = lax.broadcasted_iota(jnp.int32, (LANES, D_INNER), 0)
        chan = lax.broadcasted_iota(jnp.int32, (LANES, D_INNER), 1) // SSM_HEAD_DIM
        spread = (head == chan).astype(BF16)
        dt_rep_ref[...] = _select_dot(dt, spread)
        acs_rep_ref[...] = _select_dot(acs, spread)

    row = pl.BlockSpec((cl, LANES), lambda i: (i, 0))
    wide = pl.BlockSpec((cl, D_INNER), lambda i: (i, 0))
    vec = pl.BlockSpec((1, LANES), lambda i: (0, 0))
    return pl.pallas_call(
        body, name="ssd_prep_fwd", grid=(t // cl,),
        in_specs=[row, vec, vec], out_specs=[row, wide, wide],
        out_shape=[jax.ShapeDtypeStruct((t, LANES), F32), jax.ShapeDtypeStruct((t, D_INNER), F32),
                   jax.ShapeDtypeStruct((t, D_INNER), F32)],
        compiler_params=_params("parallel"),
    )(dt_raw, dt_bias, a_log)


def _ssd_prep_bwd(dt_raw, dt_bias, ddt):
    t = dt_raw.shape[0]
    tm = _tile(t, 512)

    def body(r_ref, b_ref, d_ref, o_ref, db_ref):
        g = d_ref[...] * _sigmoid(r_ref[...] + b_ref[...])
        o_ref[...] = g.astype(BF16)
        part = jnp.sum(g, axis=0, keepdims=True)

        @pl.when(pl.program_id(0) == 0)
        def _():
            db_ref[...] = part

        @pl.when(pl.program_id(0) > 0)
        def _():
            db_ref[...] += part

    row = pl.BlockSpec((tm, LANES), lambda i: (i, 0))
    vec = pl.BlockSpec((1, LANES), lambda i: (0, 0))
    return pl.pallas_call(
        body, name="ssd_prep_bwd", grid=(t // tm,),
        in_specs=[row, vec, row], out_specs=[row, vec],
        out_shape=[jax.ShapeDtypeStruct((t, LANES), BF16), jax.ShapeDtypeStruct((1, LANES), F32)],
        compiler_params=_params("arbitrary"),
    )(dt_raw, dt_bias, ddt)


GROUP_W = D_INNER // SSM_GROUPS
PAIRS_PER_GROUP = GROUP_W // LANES


def _head_cols(acs_pair, lt64):
    rolled = pltpu.roll(acs_pair, ATT_HEAD_DIM, 1)
    return jnp.where(lt64, acs_pair, rolled), jnp.where(lt64, rolled, acs_pair)


def _ssd_fwd(xbc, dt_rep, acs_rep, acs_t, dskip_rep, side=None):
    t = xbc.shape[0]
    cl = SSD_CHUNK
    nc = t // cl

    def body(xbc_ref, dt_ref, acs_ref, acst_ref, dskip_ref, y_ref, hin_ref, state_ref):
        @pl.when(pl.program_id(0) == 0)
        def _():
            state_ref[...] = jnp.zeros_like(state_ref)

        lt64 = _lane_lt64(cl)
        li = lax.broadcasted_iota(jnp.int32, (cl, cl), 0)
        si = lax.broadcasted_iota(jnp.int32, (cl, cl), 1)
        causal = li >= si
        hin_ref[...] = state_ref[...]
        for g in range(SSM_GROUPS):
            gsl = slice(g * GROUP_W, (g + 1) * GROUP_W)
            xg = xbc_ref[:, gsl]
            bg = xbc_ref[:, D_INNER + g * SSM_STATE:D_INNER + (g + 1) * SSM_STATE]
            cg = xbc_ref[:, D_INNER + SSM_GROUPS * SSM_STATE + g * SSM_STATE:
                         D_INNER + SSM_GROUPS * SSM_STATE + (g + 1) * SSM_STATE]
            acs = acs_ref[:, gsl]
            xdt = xg * dt_ref[:, gsl]
            atot = acs[cl - 1:cl, :]
            hin = state_ref[:, gsl]
            cgb = cg.astype(BF16)
            gmat = _dot_nt(cgb, bg.astype(BF16))
            yoff = _dot(cgb, hin.astype(BF16)) * jnp.exp(acs)
            snew = _dot(bg.T.astype(BF16), (xdt * jnp.exp(atot - acs)).astype(BF16))
            state_ref[:, gsl] = hin * jnp.exp(atot) + snew
            xdtb = xdt.astype(BF16)
            for pr in range(PAIRS_PER_GROUP):
                psl = slice(pr * LANES, (pr + 1) * LANES)
                cols = _head_cols(acs[:, psl], lt64)
                xp = xdtb[:, psl]
                ys = []
                for hh in range(2):
                    h = (g * PAIRS_PER_GROUP + pr) * 2 + hh
                    seg = cols[hh] - acst_ref[h:h + 1, :]
                    lm = jnp.exp(jnp.where(causal, seg, NEG_BIG))
                    ys.append(_dot((gmat * lm).astype(BF16), xp))
                ydiag = jnp.where(lt64, ys[0], ys[1])
                osl = slice(g * GROUP_W + pr * LANES, g * GROUP_W + (pr + 1) * LANES)
                y_ref[:, osl] = ydiag + yoff[:, psl] + xg[:, psl] * dskip_ref[:, osl]

    row = lambda w: pl.BlockSpec((cl, w), lambda c: (c, 0))
    return _call(
        body, side, name="ssd_fwd", grid=(nc,),
        in_specs=[row(CONV_DIM), row(D_INNER), row(D_INNER),
                  pl.BlockSpec((SSM_HEADS, cl), lambda c: (0, c)), pl.BlockSpec((1, D_INNER), lambda c: (0, 0))],
        out_specs=[row(D_INNER), pl.BlockSpec((None, SSM_STATE, D_INNER), lambda c: (c, 0, 0))],
        out_shape=[jax.ShapeDtypeStruct((t, D_INNER), F32), jax.ShapeDtypeStruct((nc, SSM_STATE, D_INNER), F32)],
        scratch_shapes=[pltpu.VMEM((SSM_STATE, D_INNER), F32)],
        semantics=("arbitrary",), args=(xbc, dt_rep, acs_rep, acs_t, dskip_rep),
    )


def _ssd_bwd(xbc, dt_rep, acs_rep, acs_t, dskip_rep, a_rep, hin_all, dy, side=None):
    t = xbc.shape[0]
    cl = SSD_CHUNK
    nc = t // cl

    def body(xbc_ref, dt_ref, acs_ref, acst_ref, dskip_ref, a_ref, hin_ref, dy_ref,
             dxbc_ref, ddt_ref, da_ref, dds_ref, dstate_ref, dacs_ref, dxs_ref):
        step = pl.program_id(0)

        @pl.when(step == 0)
        def _():
            dstate_ref[...] = jnp.zeros_like(dstate_ref)
            da_ref[...] = jnp.zeros_like(da_ref)
            dds_ref[...] = jnp.zeros_like(dds_ref)

        bd = _head_block_diag()
        lt64 = _lane_lt64(cl)
        li = lax.broadcasted_iota(jnp.int32, (cl, cl), 0)
        si = lax.broadcasted_iota(jnp.int32, (cl, cl), 1)
        lower = li >= si
        upper = si >= li
        last_row = lax.broadcasted_iota(jnp.int32, (cl, GROUP_W), 0) == cl - 1
        for g in range(SSM_GROUPS):
            gsl = slice(g * GROUP_W, (g + 1) * GROUP_W)
            bsl = slice(D_INNER + g * SSM_STATE, D_INNER + (g + 1) * SSM_STATE)
            csl = slice(D_INNER + SSM_GROUPS * SSM_STATE + g * SSM_STATE,
                        D_INNER + SSM_GROUPS * SSM_STATE + (g + 1) * SSM_STATE)
            xg = xbc_ref[:, gsl]
            bg = xbc_ref[:, bsl]
            cg = xbc_ref[:, csl]
            bgb, cgb = bg.astype(BF16), cg.astype(BF16)
            acs = acs_ref[:, gsl]
            xdt = xg * dt_ref[:, gsl]
            atot = acs[cl - 1:cl, :]
            eg = jnp.exp(acs)
            dk = jnp.exp(atot - acs)
            etot = jnp.exp(atot)
            hin = hin_ref[:, gsl]
            hinb = hin.astype(BF16)
            dh = dstate_ref[:, gsl]
            dhb = dh.astype(BF16)
            dyg = dy_ref[:, gsl]

            gmat = _dot_nt(cgb, bgb)
            gmat_t = _dot_nt(bgb, cgb)
            ch = _dot(cgb, hinb)
            dacs = _head_sums(dyg * ch * eg, bd)
            dye = (dyg * eg).astype(BF16)
            dc = _dot_nt(dye, hinb)
            dhin = _dot(cg.T.astype(BF16), dye)
            bdh = _dot(bgb, dhb)
            dxs = bdh * dk
            xdk = xdt * dk
            db = _dot_nt(xdk.astype(BF16), dhb)
            ddk = _head_sums(bdh * xdk, bd)
            dacs = dacs - ddk
            datot = jnp.sum(ddk, axis=0, keepdims=True) + etot * _head_sums(
                jnp.sum(dh * hin, axis=0, keepdims=True), bd)
            dacs = dacs + jnp.where(last_row, datot, 0.0)
            dstate_ref[:, gsl] = dh * etot + dhin

            xdtb = xdt.astype(BF16)
            dgsum = jnp.zeros((cl, cl), F32)
            dgsum_t = jnp.zeros((cl, cl), F32)
            for pr in range(PAIRS_PER_GROUP):
                psl = slice(pr * LANES, (pr + 1) * LANES)
                cols = _head_cols(acs[:, psl], lt64)
                xp = xdtb[:, psl]
                dyp = dyg[:, psl].astype(BF16)
                dx1, dac = [], []
                for hh in range(2):
                    h = (g * PAIRS_PER_GROUP + pr) * 2 + hh
                    mine = lt64 if hh == 0 else jnp.logical_not(lt64)
                    row = acst_ref[h:h + 1, :]
                    lm = jnp.exp(jnp.where(lower, cols[hh] - row, NEG_BIG))
                    lm_t = jnp.exp(jnp.where(upper, row - cols[hh], NEG_BIG))
                    dyh = jnp.where(mine, dyp, jnp.zeros_like(dyp))
                    xh = jnp.where(mine, xp, jnp.zeros_like(xp))
                    dm = _dot_nt(dyh, xp)
                    dm_t = _dot_nt(xh, dyp)
                    m_t = gmat_t * lm_t
                    dx1.append(_dot(m_t.astype(BF16), dyp))
                    w = dm * (gmat * lm)
                    w_t = dm_t * m_t
                    dac.append(jnp.sum(w, axis=1, keepdims=True) - jnp.sum(w_t, axis=1, keepdims=True))
                    dgsum = dgsum + dm * lm
                    dgsum_t = dgsum_t + dm_t * lm_t
                osl = slice(g * GROUP_W + pr * LANES, g * GROUP_W + (pr + 1) * LANES)
                dxs_ref[:, osl] = dxs[:, psl] + jnp.where(lt64, dx1[0], dx1[1])
                dacs_ref[:, osl] = dacs[:, psl] + jnp.where(lt64, jnp.broadcast_to(dac[0], (cl, LANES)),
                                                             jnp.broadcast_to(dac[1], (cl, LANES)))
            dxbc_ref[:, csl] = dc + _dot(dgsum.astype(BF16), bgb)
            dxbc_ref[:, bsl] = db + _dot(dgsum_t.astype(BF16), cgb)

        dadt = _split_dot(upper.astype(BF16), dacs_ref[...])
        xall = xbc_ref[:, 0:D_INNER]
        dtall = dt_ref[...]
        dxsall = dxs_ref[...]
        dyall = dy_ref[...]
        ddt_rep = dadt * a_ref[...] + _head_sums(dxsall * xall, bd)
        chan = lax.broadcasted_iota(jnp.int32, (D_INNER, LANES), 0)
        head = lax.broadcasted_iota(jnp.int32, (D_INNER, LANES), 1)
        ddt_ref[...] = _select_dot(ddt_rep, (chan == head * SSM_HEAD_DIM).astype(BF16))
        dxbc_ref[:, 0:D_INNER] = dxsall * dtall + dyall * dskip_ref[...]
        da_ref[...] += jnp.sum(dadt * dtall, axis=0, keepdims=True)
        dds_ref[...] += jnp.sum(dyall * xall, axis=0, keepdims=True)

        @pl.when(step == nc - 1)
        def _():
            dds_ref[...] = _head_sums(dds_ref[...], bd)

    row = lambda w: pl.BlockSpec((cl, w), lambda c: (nc - 1 - c, 0))
    vec = pl.BlockSpec((1, D_INNER), lambda c: (0, 0))
    return _call(
        body, side, name="ssd_bwd", grid=(nc,),
        in_specs=[row(CONV_DIM), row(D_INNER), row(D_INNER),
                  pl.BlockSpec((SSM_HEADS, cl), lambda c: (0, nc - 1 - c)), vec, vec,
                  pl.BlockSpec((None, SSM_STATE, D_INNER), lambda c: (nc - 1 - c, 0, 0)), row(D_INNER)],
        out_specs=[row(CONV_DIM), row(LANES), vec, vec],
        out_shape=[jax.ShapeDtypeStruct((t, CONV_DIM), F32), jax.ShapeDtypeStruct((t, LANES), F32),
                   jax.ShapeDtypeStruct((1, D_INNER), F32), jax.ShapeDtypeStruct((1, D_INNER), F32)],
        scratch_shapes=[pltpu.VMEM((SSM_STATE, D_INNER), F32), pltpu.VMEM((cl, D_INNER), F32),
                        pltpu.VMEM((cl, D_INNER), F32)],
        semantics=("arbitrary",), args=(xbc, dt_rep, acs_rep, acs_t, dskip_rep, a_rep, hin_all, dy),
    )


def _gate_norm_fwd(y, z, w):
    t, c = y.shape
    tm = _tile(t, 256)

    def body(y_ref, z_ref, w_ref, o_ref):
        for g in range(SSM_GROUPS):
            gsl = slice(g * GROUP_W, (g + 1) * GROUP_W)
            zv = z_ref[:, gsl]
            v = y_ref[:, gsl] * (zv * _sigmoid(zv))
            r = lax.rsqrt(jnp.mean(v * v, axis=-1, keepdims=True) + NORM_EPS)
            o_ref[:, gsl] = (v * r * w_ref[:, gsl]).astype(BF16)

    row = pl.BlockSpec((tm, c), lambda i: (i, 0))
    return pl.pallas_call(
        body, name="gate_norm_fwd", grid=(t // tm,),
        in_specs=[row, row, pl.BlockSpec((1, c), lambda i: (0, 0))], out_specs=row,
        out_shape=jax.ShapeDtypeStruct((t, c), BF16),
        compiler_params=_params("parallel"),
    )(y, z, w)


def _gate_norm_bwd(y, z, w, dout, side=None):
    t, c = y.shape
    tm = _tile(t, 256)

    def body(y_ref, z_ref, w_ref, do_ref, dy_ref, dz_ref, dw_ref):
        @pl.when(pl.program_id(0) == 0)
        def _():
            dw_ref[...] = jnp.zeros_like(dw_ref)

        for g in range(SSM_GROUPS):
            gsl = slice(g * GROUP_W, (g + 1) * GROUP_W)
            zv, yv, dov = z_ref[:, gsl], y_ref[:, gsl], do_ref[:, gsl]
            sg = _sigmoid(zv)
            sz = zv * sg
            v = yv * sz
            r = lax.rsqrt(jnp.mean(v * v, axis=-1, keepdims=True) + NORM_EPS)
            vh = v * r
            dvh = dov * w_ref[:, gsl]
            mean = jnp.mean(dvh * vh, axis=-1, keepdims=True)
            dv = r * (dvh - vh * mean)
            dy_ref[:, gsl] = dv * sz
            dz_ref[:, gsl] = (dv * yv * (sg * (1.0 + zv * (1.0 - sg)))).astype(BF16)
            dw_ref[:, gsl] += jnp.sum(dov * vh, axis=0, keepdims=True)

    row = pl.BlockSpec((tm, c), lambda i: (i, 0))
    vec = pl.BlockSpec((1, c), lambda i: (0, 0))
    return _call(
        body, side, name="gate_norm_bwd", grid=(t // tm,),
        in_specs=[row, row, vec, row], out_specs=[row, row, vec],
        out_shape=[jax.ShapeDtypeStruct((t, c), F32), jax.ShapeDtypeStruct((t, c), BF16),
                   jax.ShapeDtypeStruct((1, c), F32)],
        scratch_shapes=[], semantics=("arbitrary",), args=(y, z, w, dout),
    )


ATT_W = ATT_HEADS * ATT_HEAD_DIM
N_QKV_BLOCKS = 9
ATT_SCALE = 1.0 / math.sqrt(ATT_HEAD_DIM)


def _head_rmsnorm(x, gain, bd):
    ms = _head_sums(x * x, bd, terms=1) * (1.0 / ATT_HEAD_DIM)
    return x * lax.rsqrt(ms + NORM_EPS) * gain


def _class_rows(ref, blk, r, dil):
    span = ATT_BLOCK * dil
    sub = ref.at[pl.ds(pl.multiple_of(blk * span, span), span), :]
    return sub[...] if dil == 1 else sub[pl.ds(r, ATT_BLOCK, stride=dil), :]


def _store_class_rows(ref, blk, r, dil, val):
    span = ATT_BLOCK * dil
    sub = ref.at[pl.ds(pl.multiple_of(blk * span, span), span), :]
    if dil == 1:
        sub[...] = val
    else:
        sub[pl.ds(r, ATT_BLOCK, stride=dil), :] = val


PAIRS = ATT_HEADS // 2


def _pair_col(g, j):
    return lambda pair: (0, (g * 3 + j) * PAIRS + pair)


def _pair_slopes(pair):
    steps = jnp.full((1, 2 * ATT_BLOCK), 2 * pair + 1, jnp.int32).astype(F32)
    first = jnp.exp(steps * (-0.5 * math.log(2.0)))
    return first, first * (2.0 ** -0.5)


NORM_ROWS = 512


ROW_SLICES = 4
SLICE_ROWS = 2 * ATT_BLOCK // ROW_SLICES


def _fill_band_bias(bias_ref, pair, dil, transposed):
    bq = ATT_BLOCK
    a = lax.broadcasted_iota(jnp.int32, (2 * bq, 2 * bq), 0) % bq
    b = lax.broadcasted_iota(jnp.int32, (2 * bq, 2 * bq), 1)
    dist = (b - a) if transposed else (a + bq - b)
    in_band = (dist >= 0) & (dist <= bq)
    s0, s1 = _pair_slopes(pair)
    first_head = lax.broadcasted_iota(jnp.int32, (2 * bq, 2 * bq), 0) < bq
    bias = jnp.where(first_head, s0, s1) * (dist.astype(F32) * float(dil))
    inside = (b < bq) if transposed else (b >= bq)
    bias_ref[1] = jnp.where(in_band, bias, -NEG_BIG)
    bias_ref[0] = jnp.where(in_band & inside, bias, -NEG_BIG)


def _row_slices():
    return [slice(i * SLICE_ROWS, (i + 1) * SLICE_ROWS) for i in range(ROW_SLICES)]


def _stack_heads(tile):
    rows = lax.broadcasted_iota(jnp.int32, (2 * ATT_BLOCK, LANES), 0) < ATT_BLOCK
    lanes = lax.broadcasted_iota(jnp.int32, (2 * ATT_BLOCK, LANES), 1) < ATT_HEAD_DIM
    both = jnp.concatenate([tile, tile], axis=0)
    return jnp.where(rows == lanes, both, jnp.zeros_like(both))


def _unstack_heads(stacked, lt64):
    return jnp.where(lt64, stacked[:ATT_BLOCK], stacked[ATT_BLOCK:])


ITEMS_PER_PASS = 4


def _item_loop(nb, dil, work):
    if dil == 1:
        def trip(i, carry):
            work([(i * ITEMS_PER_PASS + b, 0) for b in range(ITEMS_PER_PASS)])
            return carry

        lax.fori_loop(0, nb // ITEMS_PER_PASS, trip, 0)
    else:
        def trip(n, carry):
            for r0 in range(0, dil, ITEMS_PER_PASS):
                work([(n, r0 + j) for j in range(ITEMS_PER_PASS)])
            return carry

        lax.fori_loop(0, nb, trip, 0)


def _qk_normalised(tile, j, gq_ref, gk_ref):
    kind = (j // (ATT_W // tile.shape[1])) % 3
    gain = jnp.where(kind == 0, gq_ref[...] * ATT_SCALE, gk_ref[...])
    return jnp.where(kind == 2, tile, _head_rmsnorm(tile, gain, _head_block_diag()))


def _attn_fwd(qkn, g, dil):
    t = qkn.shape[0]
    nb = t // dil // ATT_BLOCK
    bq = ATT_BLOCK

    def body(qn_ref, kn_ref, v_ref, o_ref, l_ref, bias_ref):
        _fill_band_bias(bias_ref, pl.program_id(0), dil, False)
        lt64 = _lane_lt64(bq)

        def work(items):
            scores, values, probs = [], [], []
            for n, r in items:
                prev = jnp.maximum(n - 1, 0)
                q2 = _stack_heads(_class_rows(qn_ref, n, r, dil).astype(BF16))
                kcat = jnp.concatenate([_class_rows(kn_ref, prev, r, dil), _class_rows(kn_ref, n, r, dil)],
                                       axis=0).astype(BF16)
                values.append(jnp.concatenate([_class_rows(v_ref, prev, r, dil), _class_rows(v_ref, n, r, dil)],
                                              axis=0).astype(BF16))
                scores.append(_dot_nt(q2, kcat))
            for (n, r), sc in zip(items, scores):
                bias = bias_ref.at[jnp.minimum(n, 1)]
                ps, inv, lses = [], [], []
                for rows in _row_slices():
                    s = sc[rows] - bias[rows, :]
                    m = jnp.max(s, axis=1, keepdims=True)
                    p = jnp.exp(s - m)
                    l = jnp.sum(p, axis=1, keepdims=True)
                    ps.append(p.astype(BF16))
                    inv.append(jnp.broadcast_to(1.0 / l, (SLICE_ROWS, LANES)))
                    lses.append(jnp.broadcast_to(m + jnp.log(l), (SLICE_ROWS, LANES)))
                probs.append((jnp.concatenate(ps, axis=0), jnp.concatenate(inv, axis=0)))
                _store_class_rows(l_ref, n, r, dil, _unstack_heads(jnp.concatenate(lses, axis=0), lt64))
            for (n, r), (p, inv), vcat in zip(items, probs, values):
                _store_class_rows(o_ref, n, r, dil, _unstack_heads(_dot(p, vcat) * inv, lt64))

        _item_loop(nb, dil, work)

    col = lambda j: pl.BlockSpec((t, LANES), _pair_col(g, j))
    out = pl.BlockSpec((t, LANES), lambda pair: (0, pair))
    return pl.pallas_call(
        body, name=f"attn_fwd_g{g}", grid=(PAIRS,),
        in_specs=[col(0), col(1), col(2)], out_specs=[out, out],
        out_shape=[jax.ShapeDtypeStruct((t, ATT_W), F32), jax.ShapeDtypeStruct((t, ATT_W), F32)],
        scratch_shapes=[pltpu.VMEM((2, 2 * bq, 2 * bq), F32)],
        compiler_params=_params("parallel"),
    )(qkn, qkn, qkn)


def _one_per_head(rep):
    chan = lax.broadcasted_iota(jnp.int32, (ATT_W, LANES), 0)
    head = lax.broadcasted_iota(jnp.int32, (ATT_W, LANES), 1)
    return _select_dot(rep, (chan == head * ATT_HEAD_DIM).astype(BF16))


def _attn_combine_fwd(outs, lses):
    t = outs[0].shape[0]
    tm = _tile(t, 256)

    def body(o0, o1, o2, l0, l1, l2, ob_ref, of_ref, lt_ref, lc_ref):
        a, b, c = l0[...], l1[...], l2[...]
        m = jnp.maximum(jnp.maximum(a, b), c)
        ea, eb, ec = jnp.exp(a - m), jnp.exp(b - m), jnp.exp(c - m)
        ssum = ea + eb + ec
        o = (ea * o0[...] + eb * o1[...] + ec * o2[...]) / ssum
        ob_ref[...] = o.astype(BF16)
        of_ref[...] = o
        lse = m + jnp.log(ssum)
        lt_ref[...] = lse
        lc_ref[...] = _one_per_head(lse)

    row = pl.BlockSpec((tm, ATT_W), lambda i: (i, 0))
    return pl.pallas_call(
        body, name="attn_combine_fwd", grid=(t // tm,),
        in_specs=[row] * 6, out_specs=[row] * 3 + [pl.BlockSpec((tm, LANES), lambda i: (i, 0))],
        out_shape=[jax.ShapeDtypeStruct((t, ATT_W), BF16), jax.ShapeDtypeStruct((t, ATT_W), F32),
                   jax.ShapeDtypeStruct((t, ATT_W), F32), jax.ShapeDtypeStruct((t, LANES), F32)],
        compiler_params=_params("parallel"),
    )(*outs, *lses)


def _attn_combine_bwd(do, o):
    t = do.shape[0]
    tm = _tile(t, 256)

    def body(do_ref, o_ref, dl_ref, dc_ref):
        dl = _head_sums(do_ref[...] * o_ref[...], _head_block_diag())
        dl_ref[...] = dl
        dc_ref[...] = _one_per_head(dl)

    row = pl.BlockSpec((tm, ATT_W), lambda i: (i, 0))
    return pl.pallas_call(
        body, name="attn_combine_bwd", grid=(t // tm,),
        in_specs=[row, row], out_specs=[row, pl.BlockSpec((tm, LANES), lambda i: (i, 0))],
        out_shape=[jax.ShapeDtypeStruct((t, ATT_W), F32), jax.ShapeDtypeStruct((t, LANES), F32)],
        compiler_params=_params("parallel"),
    )(do, o)


def _head_rmsnorm_bwd(x_ref, dy_ref, gain_ref, dx_ref, dgain_ref):
    bd = _head_block_diag()
    gain = gain_ref[...]

    def step(i, acc):
        rows = pl.ds(pl.multiple_of(i * NORM_ROWS, NORM_ROWS), NORM_ROWS)
        x, dy = x_ref[rows, :], dy_ref[rows, :]
        r = lax.rsqrt(_head_sums(x * x, bd, terms=1) * (1.0 / ATT_HEAD_DIM) + NORM_EPS)
        xh = x * r
        dxh = dy * gain
        mean = _head_sums(dxh * xh, bd, terms=1) * (1.0 / ATT_HEAD_DIM)
        dx_ref[rows, :] = (r * (dxh - xh * mean)).astype(BF16)
        return acc + jnp.sum(dy * xh, axis=0, keepdims=True)

    acc = lax.fori_loop(0, x_ref.shape[0] // NORM_ROWS, step, jnp.zeros((1, LANES), F32))
    dgain_ref[...] = jnp.broadcast_to(acc, dgain_ref.shape)


def _attn_bwd_dq(qkv, qkn, gq, do, l_rep, dl_rep, g, dil):
    t = qkv.shape[0]
    nb = t // dil // ATT_BLOCK
    bq = ATT_BLOCK

    def body(q_ref, qn_ref, kn_ref, v_ref, gq_ref, do_ref, l_ref, dl_ref, dx_ref, dgain_ref, bias_ref, dq_ref):
        _fill_band_bias(bias_ref, pl.program_id(0), dil, False)
        lt64 = _lane_lt64(bq)

        def per_row(tile):
            cols = _head_cols(tile, lt64)
            half = jnp.concatenate([cols[0], cols[1]], axis=0)
            return jnp.concatenate([half, half], axis=1)

        def work(items):
            products, keys, dscores = [], [], []
            for n, r in items:
                prev = jnp.maximum(n - 1, 0)
                q2 = _stack_heads(_class_rows(qn_ref, n, r, dil).astype(BF16))
                do2 = _stack_heads(_class_rows(do_ref, n, r, dil).astype(BF16))
                kcat = jnp.concatenate([_class_rows(kn_ref, prev, r, dil), _class_rows(kn_ref, n, r, dil)],
                                       axis=0).astype(BF16)
                vcat = jnp.concatenate([_class_rows(v_ref, prev, r, dil), _class_rows(v_ref, n, r, dil)],
                                       axis=0).astype(BF16)
                keys.append(kcat)
                products.append((_dot_nt(q2, kcat), _dot_nt(do2, vcat)))
            for (n, r), (scores, dps) in zip(items, products):
                bias = bias_ref.at[jnp.minimum(n, 1)]
                lse = per_row(_class_rows(l_ref, n, r, dil))
                dl = per_row(_class_rows(dl_ref, n, r, dil))
                dss = []
                for rows in _row_slices():
                    p = jnp.exp(scores[rows] - bias[rows, :] - lse[rows])
                    dss.append((p * (dps[rows] - dl[rows])).astype(BF16))
                dscores.append(jnp.concatenate(dss, axis=0))
            for (n, r), ds, kcat in zip(items, dscores, keys):
                _store_class_rows(dq_ref, n, r, dil, _unstack_heads(_dot(ds, kcat) * ATT_SCALE, lt64))

        _item_loop(nb, dil, work)
        _head_rmsnorm_bwd(q_ref, dq_ref, gq_ref, dx_ref, dgain_ref)

    col = lambda j: pl.BlockSpec((t, LANES), _pair_col(g, j))
    vec = pl.BlockSpec((1, LANES), lambda pair: (0, 0))
    tok = pl.BlockSpec((t, LANES), lambda pair: (0, pair))
    return pl.pallas_call(
        body, name=f"attn_bwd_dq_g{g}", grid=(PAIRS,),
        in_specs=[col(0), col(0), col(1), col(2), vec, tok, tok, tok],
        out_specs=[tok, pl.BlockSpec((None, 8, LANES), lambda pair: (pair, 0, 0))],
        out_shape=[jax.ShapeDtypeStruct((t, ATT_W), BF16), jax.ShapeDtypeStruct((PAIRS, 8, LANES), F32)],
        scratch_shapes=[pltpu.VMEM((2, 2 * bq, 2 * bq), F32), pltpu.VMEM((t, LANES), F32)],
        compiler_params=_params("parallel"),
    )(qkv, qkn, qkn, qkn, gq, do, l_rep, dl_rep)


def _attn_bwd_dkv(qkv, qkn, gk, do, l_row, dl_row, g, dil):
    t = qkv.shape[0]
    nb = t // dil // ATT_BLOCK
    bq = ATT_BLOCK

    def body(k_ref, qn_ref, kn_ref, v_ref, gk_ref, do_ref, l_ref, dl_ref, dkx_ref, dvx_ref, dgain_ref, bias_ref,
             dk_ref, dv_ref):
        _fill_band_bias(bias_ref, pl.program_id(0), dil, True)
        lt64 = _lane_lt64(bq)

        def per_query(ref, hh, lane_c, lane_n):
            return jnp.concatenate([ref[hh:hh + 1, pl.ds(lane_c, bq)], ref[hh:hh + 1, pl.ds(lane_n, bq)]], axis=1)

        def work(items):
            products, operands, weights = [], [], []
            for n, r in items:
                nxt = jnp.minimum(n + 1, nb - 1)
                k2 = _stack_heads(_class_rows(kn_ref, n, r, dil).astype(BF16))
                v2 = _stack_heads(_class_rows(v_ref, n, r, dil).astype(BF16))
                qcat = jnp.concatenate([_class_rows(qn_ref, n, r, dil), _class_rows(qn_ref, nxt, r, dil)],
                                       axis=0).astype(BF16)
                docat = jnp.concatenate([_class_rows(do_ref, n, r, dil), _class_rows(do_ref, nxt, r, dil)],
                                        axis=0).astype(BF16)
                operands.append((qcat, docat))
                products.append((_dot_nt(k2, qcat), _dot_nt(v2, docat)))
            for (n, r), (scores, dps) in zip(items, products):
                nxt = jnp.minimum(n + 1, nb - 1)
                bias = bias_ref.at[jnp.where(n == nb - 1, 0, 1)]
                lane_c = pl.multiple_of((r * nb + n) * bq, bq)
                lane_n = pl.multiple_of((r * nb + nxt) * bq, bq)
                lse = [per_query(l_ref, hh, lane_c, lane_n) for hh in range(2)]
                dl = [per_query(dl_ref, hh, lane_c, lane_n) for hh in range(2)]
                pts, dss = [], []
                for i, rows in enumerate(_row_slices()):
                    hh = i * SLICE_ROWS // bq
                    p_t = jnp.exp(scores[rows] - bias[rows, :] - lse[hh])
                    pts.append(p_t.astype(BF16))
                    dss.append((p_t * (dps[rows] - dl[hh])).astype(BF16))
                weights.append((jnp.concatenate(pts, axis=0), jnp.concatenate(dss, axis=0)))
            for (n, r), (p_t, ds_t), (qcat, docat) in zip(items, weights, operands):
                _store_class_rows(dv_ref, n, r, dil, _unstack_heads(_dot(p_t, docat), lt64))
                _store_class_rows(dk_ref, n, r, dil, _unstack_heads(_dot(ds_t, qcat), lt64))

        _item_loop(nb, dil, work)
        _head_rmsnorm_bwd(k_ref, dk_ref, gk_ref, dkx_ref, dgain_ref)

        def cast_rows(i, carry):
            rows = pl.ds(pl.multiple_of(i * NORM_ROWS, NORM_ROWS), NORM_ROWS)
            dvx_ref[rows, :] = dv_ref[rows, :].astype(BF16)
            return carry

        lax.fori_loop(0, t // NORM_ROWS, cast_rows, 0)

    col = lambda j: pl.BlockSpec((t, LANES), _pair_col(g, j))
    vec = pl.BlockSpec((1, LANES), lambda pair: (0, 0))
    tok = pl.BlockSpec((t, LANES), lambda pair: (0, pair))
    rows = pl.BlockSpec((None, 8, t), lambda pair: (pair, 0, 0))
    return pl.pallas_call(
        body, name=f"attn_bwd_dkv_g{g}", grid=(PAIRS,),
        in_specs=[col(1), col(0), col(1), col(2), vec, tok, rows, rows],
        out_specs=[tok, tok, pl.BlockSpec((None, 8, LANES), lambda pair: (pair, 0, 0))],
        out_shape=[jax.ShapeDtypeStruct((t, ATT_W), BF16), jax.ShapeDtypeStruct((t, ATT_W), BF16),
                   jax.ShapeDtypeStruct((PAIRS, 8, LANES), F32)],
        scratch_shapes=[pltpu.VMEM((2, 2 * bq, 2 * bq), F32), pltpu.VMEM((t, LANES), F32),
                        pltpu.VMEM((t, LANES), F32)],
        compiler_params=_params("parallel"),
    )(qkv, qkn, qkn, qkn, gk, do, l_row, dl_row)


def _rows_by_residue(one_per_head, dil):
    t = one_per_head.shape[0]
    per_head = one_per_head[:, :ATT_HEADS]
    rows = per_head.reshape(t // dil, dil, ATT_HEADS).transpose(2, 1, 0).reshape(PAIRS, 2, t)
    return jnp.pad(rows, ((0, 0), (0, 6), (0, 0)))


def _per_head(rep_row):
    return rep_row[0, ::SSM_HEAD_DIM]


def _rep_heads(v):
    return jnp.repeat(v, SSM_HEAD_DIM)[None, :]


def _pad_lanes(v):
    return jnp.pad(v, ((0, 0), (0, LANES - v.shape[1])))


class _NoOverlap:
    def side(self, host):
        return None

    def after(self, host):
        pass

    def begin_backward(self, grads):
        pass


def _hosted(plan, host, fn, *args, **kwargs):
    out = fn(*args, side=plan.side(host), **kwargs)
    plan.after(host)
    return out


def _ffn_ple_fwd(x1, p_i, prm, i, plan):
    h = _rmsnorm_fwd(x1, prm["norm_ffn"][i:i + 1], name=f"ffn_norm_fwd_{i}")
    g, u, act = _hosted(plan, f"swiglu_fwd_{i}", _swiglu_fwd, h, prm["ffn_w_gate"][i], prm["ffn_w_up"][i],
                        name=f"swiglu_fwd_{i}")
    x2 = _hosted(plan, f"ffn_down_{i}", _matmul, act, prm["ffn_w_down"][i], mode="nn", addend=x1,
                 name=f"ffn_down_{i}")
    x3 = _ple_fwd(x2, p_i, prm["ple_w_gate"][i], prm["ple_w_proj"][i], name=f"ple_fwd_{i}")
    return x3, dict(x1=x1, h=h, g=g, u=u, act=act, x2=x2)


def _ffn_ple_bwd(dx3, p_i, prm, i, sv, grads, plan):
    ds, dple = _ple_bwd(sv["x2"], p_i, prm["ple_w_gate"][i], prm["ple_w_proj"][i], dx3, name=f"ple_bwd_{i}")
    grads["ple_w_gate"][i] = _matmul_tn(sv["x2"], ds, name=f"d_ple_w_gate_{i}")
    grads["ple_w_proj"][i] = _matmul_tn(dple, p_i, name=f"d_ple_w_proj_{i}")
    dx2 = _matmul(ds, prm["ple_w_gate"][i], mode="nt", addend=dx3, name=f"ple_dx_{i}")
    grads["ffn_w_down"][i] = _matmul_tn(sv["act"], dx2, name=f"d_ffn_w_down_{i}")
    dg, du = _hosted(plan, f"swiglu_bwd_{i}", _swiglu_bwd, dx2, prm["ffn_w_down"][i], sv["g"], sv["u"],
                     name=f"swiglu_bwd_{i}")
    grads["ffn_w_gate"][i] = _matmul_tn(dg, sv["h"], name=f"d_ffn_w_gate_{i}")
    grads["ffn_w_up"][i] = _matmul_tn(du, sv["h"], name=f"d_ffn_w_up_{i}")
    dh = _matmul(dg, prm["ffn_w_gate"][i], mode="nn", name=f"ffn_dh_gate_{i}")
    dh = _matmul(du, prm["ffn_w_up"][i], mode="nn", addend=dh, name=f"ffn_dh_up_{i}")
    dx1, dgain = _rmsnorm_bwd(sv["x1"], prm["norm_ffn"][i:i + 1], dh, dx2, name=f"ffn_norm_bwd_{i}")
    grads["norm_ffn"][i] = dgain[0]
    return dx1


def _mamba_fwd(x0, prm, plan):
    h = _rmsnorm_fwd(x0, prm["norm_mix"][0:1], name="mix_norm_fwd_0")
    z = _hosted(plan, "ssm_in_z", _matmul, h, prm["ssm_w_z"], mode="nt", name="ssm_in_z")
    xbc_pre = _hosted(plan, "ssm_in_xbc", _matmul, h, prm["ssm_w_xbc"], mode="nt", name="ssm_in_xbc")
    dt_raw = _matmul(h, prm["ssm_w_dt"], mode="nt", name="ssm_in_dt")
    xbc = _hosted(plan, "conv_fwd", _conv_fwd, xbc_pre, prm["ssm_conv_w"], prm["ssm_conv_b"])
    dt_bias = _pad_lanes(prm["ssm_dt_bias"])
    a_log = _pad_lanes(prm["ssm_a_log"])
    acs, dt_rep, acs_rep = _ssd_prep_fwd(dt_raw, dt_bias, a_log)
    acs_t = acs[:, :SSM_HEADS].T
    dskip_rep = _rep_heads(prm["ssm_d_skip"][0])
    y, hin_all = _hosted(plan, "ssd_fwd", _ssd_fwd, xbc, dt_rep, acs_rep, acs_t, dskip_rep)
    yn = _gate_norm_fwd(y, z, prm["ssm_norm_w"])
    x1 = _matmul(yn, prm["ssm_w_out"], mode="nn", addend=x0, name="ssm_out")
    sv = dict(x0=x0, h=h, z=z, xbc_pre=xbc_pre, dt_raw=dt_raw, xbc=xbc, dt_bias=dt_bias, dt_rep=dt_rep,
              acs_rep=acs_rep, acs_t=acs_t, dskip_rep=dskip_rep, y=y, hin_all=hin_all, yn=yn)
    return x1, sv


def _mamba_bwd(dx1, prm, sv, grads, plan):
    grads["ssm_w_out"] = _matmul_tn(sv["yn"], dx1, name="d_ssm_w_out")
    dyn = _matmul(dx1, prm["ssm_w_out"], mode="nt", name="ssm_out_dx")
    dy, dz, dnw = _hosted(plan, "gate_norm_bwd", _gate_norm_bwd, sv["y"], sv["z"], prm["ssm_norm_w"], dyn)
    grads["ssm_norm_w"] = dnw
    a_rep = _rep_heads(-jnp.exp(prm["ssm_a_log"][0]))
    dxbc, ddt, da_rep, dds_rep = _hosted(plan, "ssd_bwd", _ssd_bwd, sv["xbc"], sv["dt_rep"], sv["acs_rep"],
                                             sv["acs_t"], sv["dskip_rep"], a_rep, sv["hin_all"], dy)
    grads["ssm_d_skip"] = _per_head(dds_rep)[None, :]
    grads["ssm_a_log"] = (_per_head(da_rep) * _per_head(a_rep))[None, :]
    ddt_raw, dbias = _ssd_prep_bwd(sv["dt_raw"], sv["dt_bias"], ddt)
    grads["ssm_dt_bias"] = dbias[:, :SSM_HEADS]
    du, dcw, dcb = _hosted(plan, "conv_bwd", _conv_bwd, sv["xbc_pre"], prm["ssm_conv_w"], prm["ssm_conv_b"], dxbc)
    grads["ssm_conv_w"] = dcw
    grads["ssm_conv_b"] = dcb
    h = sv["h"]
    grads["ssm_w_in"] = jnp.concatenate(
        [_matmul_tn(dz, h, name="d_ssm_w_z"), _matmul_tn(du, h, name="d_ssm_w_xbc"),
         _matmul_tn(ddt_raw, h, name="d_ssm_w_dt")[:SSM_HEADS]], axis=0)
    dh = _matmul(dz, prm["ssm_w_z"], mode="nn", name="ssm_dh_z")
    dh = _matmul(du, prm["ssm_w_xbc"], mode="nn", addend=dh, name="ssm_dh_xbc")
    dh = _matmul(ddt_raw, prm["ssm_w_dt"], mode="nn", addend=dh, name="ssm_dh_dt")
    dx0, dgain = _rmsnorm_bwd(sv["x0"], prm["norm_mix"][0:1], dh, dx1, name="mix_norm_bwd_0")
    grads["norm_mix"][0] = dgain[0]
    return dx0


def _attn_mixer_fwd(x0, prm, plan):
    h = _rmsnorm_fwd(x0, prm["norm_mix"][1:2], name="mix_norm_fwd_1")
    n_heads = N_QKV_BLOCKS * ATT_HEADS
    gq = jnp.tile(prm["att_q_norm"], (1, n_heads))
    gk = jnp.tile(prm["att_k_norm"], (1, n_heads))
    qkv, qkn = _hosted(plan, "att_qkv", _matmul, h, prm["att_w_qkv"], mode="nt", name="att_qkv",
                       second=(_qk_normalised, [gq, gk]))
    outs, lses = [], []
    for g, (window, dil) in enumerate(DIL_PATTERNS):
        o_g, l_g = _attn_fwd(qkn, g, dil)
        outs.append(o_g)
        lses.append(l_g)
    o_b, o_f, l_rep, l_one = _attn_combine_fwd(outs, lses)
    x1 = _matmul(o_b, prm["att_w_o"], mode="nn", addend=x0, name="att_out")
    sv = dict(x0=x0, h=h, qkv=qkv, qkn=qkn, gq2=gq[:, :LANES], gk2=gk[:, :LANES], o_b=o_b, o_f=o_f, l_rep=l_rep,
              l_one=l_one)
    return x1, sv


def _attn_mixer_bwd(dx1, prm, sv, grads):
    grads["att_w_o"] = _matmul_tn(sv["o_b"], dx1, name="d_att_w_o")
    do = _matmul(dx1, prm["att_w_o"], mode="nt", name="att_out_dx")
    dl_rep, dl_one = _attn_combine_bwd(do, sv["o_f"])
    blocks, dgq, dgk = [], [], []
    for g, (window, dil) in enumerate(DIL_PATTERNS):
        dq, dgq_g = _attn_bwd_dq(sv["qkv"], sv["qkn"], sv["gq2"], do, sv["l_rep"], dl_rep, g, dil)
        dk, dv, dgk_g = _attn_bwd_dkv(sv["qkv"], sv["qkn"], sv["gk2"], do, _rows_by_residue(sv["l_one"], dil),
                                      _rows_by_residue(dl_one, dil), g, dil)
        blocks += [dq, dk, dv]
        dgq.append(dgq_g)
        dgk.append(dgk_g)
    dqkv = jnp.concatenate(blocks, axis=1)

    def fold(parts):
        return jnp.stack(parts)[:, :, 0].reshape(-1, ATT_HEAD_DIM).sum(axis=0)[None, :]

    grads["att_q_norm"] = fold(dgq)
    grads["att_k_norm"] = fold(dgk)
    grads["att_w_qkv"] = _matmul_tn(dqkv, sv["h"], name="d_att_w_qkv")
    dh = _matmul(dqkv, prm["att_w_qkv"], mode="nn", name="att_qkv_dx")
    dx0, dgain = _rmsnorm_bwd(sv["x0"], prm["norm_mix"][1:2], dh, dx1, name="mix_norm_bwd_1")
    grads["norm_mix"][1] = dgain[0]
    return dx0


def _local_step(x, p, target, prm, plan=None):
    plan = plan or _NoOverlap()
    grads = {k: [None, None] for k in ("norm_mix", "norm_ffn", "ffn_w_gate", "ffn_w_up", "ffn_w_down",
                                       "ple_w_proj", "ple_w_gate")}
    plan.begin_backward(grads)
    x1, sv_m = _mamba_fwd(x, prm, plan)
    x3, sv_f0 = _ffn_ple_fwd(x1, p[0], prm, 0, plan)
    x4, sv_a = _attn_mixer_fwd(x3, prm, plan)
    x6, sv_f1 = _ffn_ple_fwd(x4, p[1], prm, 1, plan)
    dy, loss_row = _loss_head(x6, target)
    dx4 = _ffn_ple_bwd(dy, p[1], prm, 1, sv_f1, grads, plan)
    dx3 = _attn_mixer_bwd(dx4, prm, sv_a, grads)
    dx1 = _ffn_ple_bwd(dx3, p[0], prm, 0, sv_f0, grads, plan)
    dx0 = _mamba_bwd(dx1, prm, sv_m, grads, plan)
    return loss_row, dx0, grads


W_IN_SLAB_ROWS = 1312


def _position():
    return lax.axis_index("x"), lax.axis_index("y"), lax.axis_index("c")


def _other_chips(x, y):
    return [(1 - x, y), (x, 1 - y), (1 - x, 1 - y)]


def _remote(send_sems, recv_sems, k, src, dst, to):
    return pltpu.make_async_remote_copy(src_ref=src, dst_ref=dst, send_sem=send_sems.at[k], recv_sem=recv_sems.at[k],
                                        device_id=to, device_id_type=MESH)


def _gather_side(entries, whole=()):
    n, nw = len(entries), len(whole)

    def first_hop(ins, outs, send_sems, recv_sems):
        x, y, c = _position()
        cps = []
        for j, chip in enumerate(_other_chips(x, y)):
            for e in range(n):
                cps.append(_remote(send_sems, recv_sems, 6 * e + j, ins[e].at[c], outs[e].at[2 * x + y, c], (*chip, c)))
            for e in range(nw):
                cps.append(_remote(send_sems, recv_sems, 6 * n + 3 * e + j, ins[n + e], outs[n + e].at[2 * x + y],
                                   (*chip, c)))
        return cps

    def start(ins, outs, send_sems, recv_sems):
        for cp in first_hop(ins, outs, send_sems, recv_sems):
            cp.start()

    def finish(ins, outs, send_sems, recv_sems):
        x, y, c = _position()
        me, sibling = (x, y, c), (x, y, 1 - c)
        chips = _other_chips(x, y)
        passed_on = []
        for j, (px, py) in enumerate(chips):
            for e in range(n):
                landed = outs[e].at[2 * px + py, c]
                _remote(send_sems, recv_sems, 6 * e + j, landed, landed, me).wait_recv()
                passed_on.append(_remote(send_sems, recv_sems, 6 * e + 3 + j, landed, landed, sibling))
                passed_on[-1].start()
            for e in range(nw):
                landed = outs[n + e].at[2 * px + py]
                _remote(send_sems, recv_sems, 6 * n + 3 * e + j, landed, landed, me).wait_recv()
        for j, (px, py) in enumerate(chips):
            for e in range(n):
                passed = outs[e].at[2 * px + py, 1 - c]
                _remote(send_sems, recv_sems, 6 * e + 3 + j, passed, passed, me).wait_recv()
        for cp in first_hop(ins, outs, send_sems, recv_sems) + passed_on:
            cp.wait_send()

    shapes = [jax.ShapeDtypeStruct((N_CHIPS,) + a.shape, a.dtype) for a in list(entries) + list(whole)]
    return _Side(list(entries) + list(whole), shapes, 6 * n + 3 * nw, start, finish)


def _run_side(side, name):
    si, so = len(side.inputs), len(side.out_shapes)

    def body(*refs):
        ins, outs, send_sems, recv_sems = refs[:si], refs[si:si + so], refs[-2], refs[-1]
        side.start(ins, outs, send_sems, recv_sems)
        side.finish(ins, outs, send_sems, recv_sems)

    side.outputs = list(pl.pallas_call(
        body, name=name, in_specs=[ANY] * si, out_specs=[ANY] * so, out_shape=side.out_shapes,
        scratch_shapes=[pltpu.SemaphoreType.DMA((side.n_sems,)), pltpu.SemaphoreType.DMA((side.n_sems,))],
    )(*side.inputs))
    return side.outputs


def _swap_side(grads):
    n = len(grads)

    def copies(ins, outs, send_sems, recv_sems):
        x, y, c = _position()
        return [_remote(send_sems, recv_sems, e, ins[e].at[:, 1 - c], outs[e], (x, y, 1 - c)) for e in range(n)]

    def start(ins, outs, send_sems, recv_sems):
        for cp in copies(ins, outs, send_sems, recv_sems):
            cp.start()

    def finish(ins, outs, send_sems, recv_sems):
        for cp in copies(ins, outs, send_sems, recv_sems):
            cp.wait()

    shapes = [jax.ShapeDtypeStruct((N_CHIPS,) + g.shape[2:], g.dtype) for g in grads]
    return _Side(grads, shapes, n, start, finish)


def _chip_exchange_side(chipsums):
    n = len(chipsums)

    def copies(ins, outs, send_sems, recv_sems):
        x, y, c = _position()
        return [_remote(send_sems, recv_sems, 3 * e + j, ins[e].at[2 * tx + ty], outs[e].at[j], (tx, ty, c))
                for j, (tx, ty) in enumerate(_other_chips(x, y)) for e in range(n)]

    def start(ins, outs, send_sems, recv_sems):
        for cp in copies(ins, outs, send_sems, recv_sems):
            cp.start()

    def finish(ins, outs, send_sems, recv_sems):
        for cp in copies(ins, outs, send_sems, recv_sems):
            cp.wait()

    shapes = [jax.ShapeDtypeStruct((3,) + cs.shape[1:], cs.dtype) for cs in chipsums]
    return _Side(chipsums, shapes, 3 * n, start, finish)


def _share_halves(totals):
    n = len(totals)

    def body(*refs):
        t_refs, r_refs = refs[:n], refs[n:2 * n]
        send_sems, recv_sems = refs[2 * n], refs[2 * n + 1]
        x, y, c = _position()
        cps = [pltpu.make_async_remote_copy(src_ref=t_refs[e], dst_ref=r_refs[e], send_sem=send_sems.at[e],
                                            recv_sem=recv_sems.at[e], device_id=(x, y, 1 - c), device_id_type=MESH)
               for e in range(n)]
        for cp in cps:
            cp.start()
        for cp in cps:
            cp.wait()

    return pl.pallas_call(
        body, name="grad_share_halves", in_specs=[ANY] * n, out_specs=[ANY] * n,
        out_shape=[jax.ShapeDtypeStruct(t.shape, t.dtype) for t in totals],
        scratch_shapes=[pltpu.SemaphoreType.DMA((n,)), pltpu.SemaphoreType.DMA((n,))],
    )(*totals)


def _reduce_rows(h):
    return h if h <= 704 else h // 2


def _add_sibling(grad, recv, c_idx, *, name):
    _, _, h, cw = grad.shape
    th = _reduce_rows(h)

    def body(c_ref, g_ref, r_ref, o_ref):
        o_ref[...] = (g_ref[...] + r_ref[...]).astype(BF16)

    return pl.pallas_call(
        body, name=name,
        grid_spec=pltpu.PrefetchScalarGridSpec(
            num_scalar_prefetch=1, grid=(N_CHIPS, h // th),
            in_specs=[pl.BlockSpec((None, None, th, cw), lambda s, i, c_ref: (s, c_ref[0], i, 0)),
                      pl.BlockSpec((None, th, cw), lambda s, i, c_ref: (s, i, 0))],
            out_specs=pl.BlockSpec((None, th, cw), lambda s, i, c_ref: (s, i, 0))),
        out_shape=jax.ShapeDtypeStruct((N_CHIPS, h, cw), BF16),
        compiler_params=_params("parallel", "parallel"),
    )(c_idx, grad, recv)


def _add_chips(chipsum, recv, s_idx, *, name):
    _, h, cw = chipsum.shape
    th = _reduce_rows(h)

    def body(s_ref, own_ref, r_ref, o_ref):
        o_ref[...] = ((own_ref[...].astype(F32) + r_ref[0].astype(F32)) + r_ref[1].astype(F32)) + r_ref[2].astype(F32)

    return pl.pallas_call(
        body, name=name,
        grid_spec=pltpu.PrefetchScalarGridSpec(
            num_scalar_prefetch=1, grid=(h // th,),
            in_specs=[pl.BlockSpec((None, th, cw), lambda i, s_ref: (s_ref[0], i, 0)),
                      pl.BlockSpec((3, th, cw), lambda i, s_ref: (0, i, 0))],
            out_specs=pl.BlockSpec((th, cw), lambda i, s_ref: (i, 0))),
        out_shape=jax.ShapeDtypeStruct((h, cw), F32),
        compiler_params=_params("parallel"),
    )(s_idx, chipsum, recv)


def _adamw_math(w, g, m, v):
    m = ADAM_B1 * m + (1.0 - ADAM_B1) * g
    v = ADAM_B2 * v + (1.0 - ADAM_B2) * (g * g)
    m_hat = m / (1.0 - ADAM_B1 ** ADAM_STEP)
    v_hat = v / (1.0 - ADAM_B2 ** ADAM_STEP)
    delta = -ADAM_LR * (m_hat / (jnp.sqrt(v_hat) + ADAM_EPS) + ADAM_WD * w)
    return delta, m, v


ADAM_TILE_ELEMS = 256 * 1024


def _adamw(w, g, m, v, *, name):
    layers, rows, cols = w.shape
    tr = rows
    for cand in range(8, rows, 8):
        if rows % cand == 0 and cand * cols <= ADAM_TILE_ELEMS:
            tr = cand
    if rows * cols <= ADAM_TILE_ELEMS:
        tr = rows

    def body(w_ref, g_ref, m_ref, v_ref, d_ref, nm_ref, nv_ref):
        d, nm, nv = _adamw_math(w_ref[...], g_ref[...], m_ref[...], v_ref[...])
        d_ref[...] = d
        nm_ref[...] = nm
        nv_ref[...] = nv

    blk = pl.BlockSpec((None, tr, cols), lambda l, i: (l, i, 0))
    sds = jax.ShapeDtypeStruct(w.shape, F32)
    return pl.pallas_call(
        body, name=name, grid=(layers, rows // tr), in_specs=[blk] * 4, out_specs=[blk] * 3, out_shape=[sds] * 3,
        compiler_params=_params("parallel", "parallel"),
    )(w, g, m, v)


SMALL_LAYOUT = (("loss", 1), ("norm_mix", 16), ("norm_ffn", 16), ("ssm_conv_b", 24), ("ssm_dt_bias", 1),
                ("ssm_a_log", 1), ("ssm_d_skip", 1), ("ssm_norm_w", 16), ("att_q_norm", 1), ("att_k_norm", 1),
                ("conv_w_full", 96))
SMALL_ROWS = 176
N_DEVICES = 8


def _small_pack(values):
    parts = []
    for name, rows in SMALL_LAYOUT:
        flat = values[name].reshape(-1).astype(F32)
        parts.append(jnp.pad(flat, (0, rows * LANES - flat.shape[0])).reshape(rows, LANES))
    used = sum(r for _, r in SMALL_LAYOUT)
    parts.append(jnp.zeros((SMALL_ROWS - used, LANES), F32))
    return jnp.concatenate(parts, axis=0)


def _small_unpack(pack, shapes):
    out, off = {}, 0
    for name, rows in SMALL_LAYOUT:
        shape = shapes[name]
        n = math.prod(shape)
        out[name] = pack[off:off + rows].reshape(-1)[:n].reshape(shape)
        off += rows
    return out


def _small_allreduce_adamw(g, w, m, v):
    def body(g_ref, w_ref, m_ref, v_ref, gs_ref, d_ref, nm_ref, nv_ref, buf, send_sems, recv_sems):
        x, y, c = _position()
        pos = (x, y, c)
        me = 4 * x + 2 * y + c
        buf[me] = g_ref[...]
        peers = []
        for k in range(1, N_DEVICES):
            bits = ((k >> 2) & 1, (k >> 1) & 1, k & 1)
            peers.append(tuple(1 - p if b else p for p, b in zip(pos, bits)))
        cps = [pltpu.make_async_remote_copy(src_ref=g_ref, dst_ref=buf.at[me], send_sem=send_sems.at[k],
                                            recv_sem=recv_sems.at[k], device_id=peer, device_id_type=MESH)
               for k, peer in enumerate(peers)]
        for cp in cps:
            cp.start()
        for k, (px, py, pc) in enumerate(peers):
            pltpu.make_async_remote_copy(src_ref=g_ref, dst_ref=buf.at[4 * px + 2 * py + pc],
                                         send_sem=send_sems.at[k], recv_sem=recv_sems.at[k],
                                         device_id=(px, py, pc), device_id_type=MESH).wait_recv()
        for cp in cps:
            cp.wait_send()
        total = buf[0]
        for dev in range(1, N_DEVICES):
            total = total + buf[dev]
        gs_ref[...] = total
        d, nm, nv = _adamw_math(w_ref[...], total, m_ref[...], v_ref[...])
        d_ref[...] = d
        nm_ref[...] = nm
        nv_ref[...] = nv

    vm = pl.BlockSpec(memory_space=pltpu.VMEM)
    sds = jax.ShapeDtypeStruct((SMALL_ROWS, LANES), F32)
    return pl.pallas_call(
        body, name="small_allreduce_adamw", in_specs=[vm] * 4, out_specs=[vm] * 4, out_shape=[sds] * 4,
        scratch_shapes=[pltpu.VMEM((N_DEVICES, SMALL_ROWS, LANES), F32),
                        pltpu.SemaphoreType.DMA((N_DEVICES - 1,)), pltpu.SemaphoreType.DMA((N_DEVICES - 1,))],
    )(g, w, m, v)


SMALL = tuple(n for n, _ in SMALL_LAYOUT if n not in ("loss", "conv_w_full"))
WEIGHTS = ("norm_mix", "norm_ffn", "ssm_w_in", "ssm_conv_w", "ssm_conv_b", "ssm_dt_bias", "ssm_a_log", "ssm_d_skip",
           "ssm_norm_w", "ssm_w_out", "att_w_qkv", "att_q_norm", "att_k_norm", "att_w_o", "ffn_w_gate", "ffn_w_up",
           "ffn_w_down", "ple_w_proj", "ple_w_gate")
COLUMN_SHARDED = ("ssm_w_in", "att_w_qkv", "ffn_w_gate", "ffn_w_up", "ple_w_proj")
LAYERED = ("ffn_w_gate", "ffn_w_up", "ffn_w_down", "ple_w_proj", "ple_w_gate")
UPDATED_TRANSPOSED = ("ssm_w_in", "ffn_w_gate", "ffn_w_up")
GATHER_ORDER = ("ssm_w_in", "ssm_w_out", "att_w_qkv", "att_w_o", "ffn_w_gate", "ffn_w_up", "ffn_w_down",
                "ple_w_proj", "ple_w_gate")


def _layers(n):
    return (0, 1) if n in LAYERED else (None,)


def _tag(key):
    return key[0] if key[1] is None else f"{key[0]}_{key[1]}"


QKV_PARTS = 3


def _weight_slab(w, key):
    n, i = key
    if n == "att_w_qkv":
        a = w[n][0].T
        rows = a.shape[0] // QKV_PARTS
        a = a[i * rows:(i + 1) * rows]
    else:
        a = w[n][0 if i is None else i]
        a = a.T if n in COLUMN_SHARDED else a
    if n == "ssm_w_in":
        a = jnp.pad(a, ((0, W_IN_SLAB_ROWS - a.shape[0]), (0, 0)))
    return a.reshape(2, a.shape[0] // 2, a.shape[1]).astype(BF16)


def _install(prm, key, gathered, own, s_me):
    n, i = key
    full = lax.dynamic_update_slice(gathered, own[None], (s_me, 0, 0, 0))
    full = full.reshape(N_CHIPS, 2 * full.shape[2], full.shape[3])
    if n == "att_w_qkv":
        parts = prm.setdefault("att_w_qkv_parts", {})
        parts[i] = full
        if len(parts) == QKV_PARTS:
            prm[n] = jnp.stack([parts[j] for j in range(QKV_PARTS)], axis=1).reshape(-1, D_MODEL)
        return
    if n == "ssm_w_in":
        rows = (D_INNER + CONV_DIM + SSM_HEADS) // N_CHIPS
        w_in_t = full[:, :rows].reshape(N_CHIPS * rows, D_MODEL)
        prm["ssm_w_z"] = w_in_t[:D_INNER]
        prm["ssm_w_xbc"] = w_in_t[D_INNER:D_INNER + CONV_DIM]
        prm["ssm_w_dt"] = jnp.pad(w_in_t[D_INNER + CONV_DIM:], ((0, LANES - SSM_HEADS), (0, 0)))
        return
    full = full.reshape(N_CHIPS * full.shape[1], full.shape[2])
    if i is None:
        prm[n] = full
    else:
        prm.setdefault(n, [None, None])[i] = full


def _grad_slab(grads, key):
    n, i = key
    g = grads[n] if i is None else grads[n][i]
    if n == "ssm_w_in":
        g = jnp.pad(g.reshape(N_CHIPS, g.shape[0] // N_CHIPS, D_MODEL),
                    ((0, 0), (0, W_IN_SLAB_ROWS - g.shape[0] // N_CHIPS), (0, 0)))
    rows = g.size // (N_CHIPS * g.shape[-1])
    return g.reshape(N_CHIPS, 2, rows // 2, g.shape[-1])


def _natural_shard(n, reduced, shape):
    def one(r):
        if n == "ssm_w_in":
            r = r[:shape[-1]]
        return r.T if n in COLUMN_SHARDED else r
    if n in LAYERED:
        return jnp.stack([one(r) for r in reduced]).reshape(shape)
    return one(reduced[0]).reshape(shape)


def kernel(x, p, norm_mix, norm_ffn, ssm_w_in, ssm_conv_w, ssm_conv_b, ssm_dt_bias, ssm_a_log, ssm_d_skip, ssm_norm_w, ssm_w_out, att_w_qkv, att_q_norm, att_k_norm, att_w_o, ffn_w_gate, ffn_w_up, ffn_w_down, ple_w_proj, ple_w_gate, loss_target, m_norm_mix, m_norm_ffn, m_ssm_w_in, m_ssm_conv_w, m_ssm_conv_b, m_ssm_dt_bias, m_ssm_a_log, m_ssm_d_skip, m_ssm_norm_w, m_ssm_w_out, m_att_w_qkv, m_att_q_norm, m_att_k_norm, m_att_w_o, m_ffn_w_gate, m_ffn_w_up, m_ffn_w_down, m_ple_w_proj, m_ple_w_gate, v_norm_mix, v_norm_ffn, v_ssm_w_in, v_ssm_conv_w, v_ssm_conv_b, v_ssm_dt_bias, v_ssm_a_log, v_ssm_d_skip, v_ssm_norm_w, v_ssm_w_out, v_att_w_qkv, v_att_q_norm, v_att_k_norm, v_att_w_o, v_ffn_w_gate, v_ffn_w_up, v_ffn_w_down, v_ple_w_proj, v_ple_w_gate):
    given = dict(locals())
    w = {n: given[n] for n in WEIGHTS}
    m = {n: given["m_" + n] for n in WEIGHTS}
    v = {n: given["v_" + n] for n in WEIGHTS}
    c_idx = lax.axis_index("c").astype(jnp.int32).reshape(1)
    s_idx = (2 * lax.axis_index("x") + lax.axis_index("y")).astype(jnp.int32).reshape(1)

    s_me = 2 * lax.axis_index("x") + lax.axis_index("y")
    first_core = lax.axis_index("c") == 0

    qkv_parts = [("att_w_qkv", j) for j in range(QKV_PARTS)]
    gather_plan = {
        "ssm_in_z": [("ssm_w_out", None)],
        "ssm_in_xbc": [("ffn_w_gate", 0)],
        "conv_fwd": [("ffn_w_up", 0)],
        "ssd_fwd": [("ffn_w_down", 0), ("ple_w_proj", 0), ("ple_w_gate", 0), ("att_w_o", None)],
        "swiglu_fwd_0": qkv_parts[:2],
        "ffn_down_0": qkv_parts[2:],
        "att_qkv": [(n, 1) for n in LAYERED],
    }
    mamba = [("ssm_w_in", None)]
    own = {k: _weight_slab(w, k) for k in mamba + sum(gather_plan.values(), [])}
    prm = {n: w[n] for n in SMALL}

    def land(group, outputs):
        for k, g in zip(group, outputs):
            _install(prm, k, g, own[k], s_me)

    first = _gather_side([own[k] for k in mamba], whole=[ssm_conv_w[0]])
    _run_side(first, "gather_mamba")
    land(mamba, first.outputs)
    conv = lax.dynamic_update_slice(first.outputs[-1], ssm_conv_w, (s_me, 0, 0))
    prm["ssm_conv_w"] = conv.transpose(1, 0, 2).reshape(CONV_WIDTH, CONV_DIM)

    layer1 = [("att_w_qkv", None), ("att_w_o", None)] + [(n, 1) for n in LAYERED]
    ffn0 = [(n, 0) for n in LAYERED] + [("ssm_w_out", None)]
    reduce_plan = {"swiglu_bwd_0": ("swap", layer1), "ssd_bwd": ("exchange", layer1),
                   "gate_norm_bwd": ("swap", ffn0), "conv_bwd": ("exchange", ffn0)}
    state = {}

    def swap_side(group):
        state[_tag(group[0]), "g4"] = g4 = [_grad_slab(state["grads"], k) for k in group]
        return _swap_side(g4)

    def add_siblings(group, from_sibling):
        state[_tag(group[0]), "chipsums"] = [
            _add_sibling(g, r, c_idx, name="add_sibling_" + _tag(k))
            for g, r, k in zip(state[_tag(group[0]), "g4"], from_sibling, group)]

    def exchange_side(group):
        return _chip_exchange_side(state[_tag(group[0]), "chipsums"])

    def add_chips(group, from_chips):
        for k, cs, r in zip(group, state[_tag(group[0]), "chipsums"], from_chips):
            state["total", k] = _add_chips(cs, r, s_idx, name="add_chips_" + _tag(k))

    class Plan(_NoOverlap):
        def __init__(self):
            self.carried = {host: _gather_side([own[k] for k in group]) for host, group in gather_plan.items()}

        def begin_backward(self, grads):
            state["grads"] = grads

        def side(self, host):
            if host in reduce_plan:
                step, group = reduce_plan[host]
                self.carried[host] = swap_side(group) if step == "swap" else exchange_side(group)
            return self.carried.get(host)

        def after(self, host):
            if host in gather_plan:
                land(gather_plan[host], self.carried[host].outputs)
            elif host in reduce_plan:
                step, group = reduce_plan[host]
                (add_siblings if step == "swap" else add_chips)(group, self.carried[host].outputs)

    loss_row, dx, grads = _local_step(x[0], p[:, 0], loss_target[0], prm, Plan())

    add_siblings(mamba, _run_side(swap_side(mamba), "grad_swap_mamba"))
    add_chips(mamba, _run_side(exchange_side(mamba), "grad_exchange_mamba"))
    order = mamba + ffn0 + layer1
    shared = _share_halves([state["total", k] for k in order])
    reduced = {}
    for k, theirs in zip(order, shared):
        lo = jnp.where(first_core, state["total", k], theirs)
        hi = jnp.where(first_core, theirs, state["total", k])
        reduced.setdefault(k[0], {})[k[1]] = jnp.concatenate([lo, hi], axis=0)
    reduced = {n: [by_layer[i] for i in _layers(n)] for n, by_layer in reduced.items()}

    grad, delta, new_m, new_v = {}, {}, {}, {}
    for n in GATHER_ORDER:
        if n in UPDATED_TRANSPOSED:
            flip = lambda a: a.transpose(0, 2, 1)
            cols = w[n].shape[-1]
            g_t = jnp.stack([r[:cols] for r in reduced[n]])
            grad[n] = flip(g_t)
            delta[n], new_m[n], new_v[n] = [flip(o) for o in _adamw(flip(w[n]), g_t, flip(m[n]), flip(v[n]),
                                                                    name="adamw_" + n)]
            continue
        grad[n] = _natural_shard(n, reduced[n], w[n].shape)
        delta[n], new_m[n], new_v[n] = _adamw(w[n], grad[n], m[n], v[n], name="adamw_" + n)

    small_g = {n: (jnp.stack(grads[n]) if isinstance(grads[n], list) else grads[n]) for n in SMALL}
    small_g["loss"] = loss_row
    small_g["conv_w_full"] = grads["ssm_conv_w"]
    zero = {"loss": jnp.zeros((1, LANES), F32), "conv_w_full": jnp.zeros((CONV_WIDTH, CONV_DIM), F32)}
    outs = _small_allreduce_adamw(_small_pack(small_g), _small_pack({**w, **zero}), _small_pack({**m, **zero}),
                                  _small_pack({**v, **zero}))
    shapes = {n: w[n].shape for n in SMALL}
    shapes["loss"] = (1, LANES)
    shapes["conv_w_full"] = (CONV_WIDTH, CONV_DIM)
    sg, sd, sm, sv = [_small_unpack(o, shapes) for o in outs]
    for n in SMALL:
        grad[n], delta[n], new_m[n], new_v[n] = sg[n], sd[n], sm[n], sv[n]
    loss = sg["loss"][0, 0]
    conv_cols = CONV_DIM // N_CHIPS
    grad["ssm_conv_w"] = lax.dynamic_slice(sg["conv_w_full"], (0, s_me * conv_cols), (CONV_WIDTH, conv_cols))[None]
    delta["ssm_conv_w"], new_m["ssm_conv_w"], new_v["ssm_conv_w"] = _adamw(
        ssm_conv_w, grad["ssm_conv_w"], m_ssm_conv_w, v_ssm_conv_w, name="adamw_ssm_conv_w")

    return (loss, dx[None], *[grad[n] for n in WEIGHTS], *[delta[n] for n in WEIGHTS],
            *[new_m[n] for n in WEIGHTS], *[new_v[n] for n in WEIGHTS])
```

```python
import functools
import math

import jax
import jax.numpy as jnp
from jax import lax
from jax.experimental import pallas as pl
from jax.experimental.pallas import tpu as pltpu

F32 = jnp.float32
BF16 = jnp.bfloat16
HIGHEST = lax.Precision.HIGHEST

NORM_EPS = 1e-6
ADAM_LR, ADAM_B1, ADAM_B2, ADAM_EPS, ADAM_WD, ADAM_STEP = 0.001, 0.9, 0.999, 1e-08, 0.01, 10

D_MODEL = 1024
D_INNER = 2048
SSM_HEADS = 32
SSM_HEAD_DIM = 64
SSM_GROUPS = 4
SSM_STATE = 128
SSD_CHUNK = 128
CONV_DIM = 3072
CONV_WIDTH = 4
ATT_HEADS = 16
ATT_HEAD_DIM = 64
DIL_PATTERNS = ((128, 1), (512, 4), (2048, 16))
ATT_BLOCK = 128
FFN_HIDDEN = 2816
PLE_DIM = 256

LANES = 128
V7X_VMEM_LIMIT = 56 * 1024 * 1024
NEG_BIG = -1e30

N_CHIPS = 4


def _params(*sem):
    return pltpu.CompilerParams(dimension_semantics=sem, vmem_limit_bytes=V7X_VMEM_LIMIT)


def _tile(n, pref):
    if n <= pref:
        return n
    best = None
    for t in range(LANES, pref + 1, LANES):
        if n % t == 0:
            best = t
    assert best is not None, (n, pref)
    return best


def _sigmoid(v):
    return 1.0 / (1.0 + jnp.exp(-v))


def _dot(a, b):
    return jnp.dot(a, b, preferred_element_type=F32)


def _dot_nt(a, b):
    return lax.dot_general(a, b, (((1,), (1,)), ((), ())), preferred_element_type=F32)


def _dot_tn(a, b):
    return lax.dot_general(a, b, (((0,), (0,)), ((), ())), preferred_element_type=F32)


def _head_block_diag():
    i = lax.broadcasted_iota(jnp.int32, (LANES, LANES), 0) // ATT_HEAD_DIM
    j = lax.broadcasted_iota(jnp.int32, (LANES, LANES), 1) // ATT_HEAD_DIM
    return (i == j).astype(BF16)


def _split_dot(ones, z):
    hi = z.astype(BF16)
    lo = (z - hi.astype(F32)).astype(BF16)
    return _dot(ones, hi) + _dot(ones, lo)


def _head_sums(z, bd, terms=2):
    hi = z.astype(BF16)
    lo = (z - hi.astype(F32)).astype(BF16) if terms == 2 else None
    parts = []
    for t in range(z.shape[1] // LANES):
        sl = slice(t * LANES, (t + 1) * LANES)
        part = _dot(hi[:, sl], bd)
        parts.append(part + _dot(lo[:, sl], bd) if terms == 2 else part)
    return parts[0] if len(parts) == 1 else jnp.concatenate(parts, axis=1)


def _lane_lt64(rows):
    return lax.broadcasted_iota(jnp.int32, (rows, LANES), 1) < ATT_HEAD_DIM


MESH = pl.DeviceIdType.MESH
ANY = pl.BlockSpec(memory_space=pl.ANY)


class _Side:
    def __init__(self, inputs, out_shapes, n_sems, start, finish):
        self.inputs, self.out_shapes, self.n_sems = list(inputs), list(out_shapes), n_sems
        self.start, self.finish = start, finish
        self.outputs = None


def _call(body, side, *, name, grid, in_specs, out_specs, out_shape, scratch_shapes, semantics, args):
    in_specs, out_specs, out_shape = list(in_specs), list(out_specs), list(out_shape)
    scratch_shapes = list(scratch_shapes)
    if side is None:
        return pl.pallas_call(body, name=name, grid=grid, in_specs=in_specs, out_specs=out_specs,
                              out_shape=out_shape, scratch_shapes=scratch_shapes,
                              compiler_params=_params(*semantics))(*args)
    ni, no, ns = len(in_specs), len(out_specs), len(scratch_shapes)
    si, so = len(side.inputs), len(side.out_shapes)

    def hosted(*refs):
        ins, s_ins = refs[:ni], refs[ni:ni + si]
        outs, s_outs = refs[ni + si:ni + si + no], refs[ni + si + no:ni + si + no + so]
        scratch = refs[ni + si + no + so:ni + si + no + so + ns]
        send_sems, recv_sems = refs[-2], refs[-1]
        first = pl.program_id(0) == 0
        last = pl.program_id(0) == grid[0] - 1
        for axis in range(1, len(grid)):
            first = jnp.logical_and(first, pl.program_id(axis) == 0)
            last = jnp.logical_and(last, pl.program_id(axis) == grid[axis] - 1)

        @pl.when(first)
        def _():
            side.start(s_ins, s_outs, send_sems, recv_sems)

        body(*ins, *outs, *scratch)

        @pl.when(last)
        def _():
            side.finish(s_ins, s_outs, send_sems, recv_sems)

    res = pl.pallas_call(
        hosted, name=name, grid=grid, in_specs=in_specs + [ANY] * si, out_specs=out_specs + [ANY] * so,
        out_shape=out_shape + side.out_shapes,
        scratch_shapes=scratch_shapes + [pltpu.SemaphoreType.DMA((side.n_sems,)),
                                         pltpu.SemaphoreType.DMA((side.n_sems,))],
        compiler_params=_params(*["arbitrary"] * len(grid)),
    )(*args, *side.inputs)
    side.outputs = list(res[no:])
    return list(res[:no])


def _matmul(a, b, *, mode, name, out_dtype=F32, addend=None, tm=1024, tn=512, tk_max=3072, side=None, second=None):
    m, k = a.shape
    if mode == "nn":
        k2, n = b.shape
    else:
        n, k2 = b.shape
    assert k == k2, (a.shape, b.shape, mode)
    tm, tn, tk = _tile(m, tm), _tile(n, tn), _tile(k, tk_max)
    nk = k // tk
    has_add = addend is not None
    n_rows = len(second[1]) if second else 0
    n_out = 2 if second else 1

    def body(*refs):
        a_ref, b_ref = refs[0], refs[1]
        add_ref = refs[2] if has_add else None
        row_refs = refs[2 + has_add:2 + has_add + n_rows]
        o_ref, acc_ref = refs[-1 - n_out], refs[-1]
        kk = pl.program_id(2)
        col_tile = pl.program_id(1)
        av = a_ref[...].astype(BF16)
        bv = b_ref[...].astype(BF16)
        part = _dot(av, bv) if mode == "nn" else _dot_nt(av, bv)

        @pl.when(kk == 0)
        def _():
            acc_ref[...] = part

        @pl.when(kk > 0)
        def _():
            acc_ref[...] += part

        @pl.when(kk == nk - 1)
        def _():
            res = acc_ref[...]
            if has_add:
                res = res + add_ref[...]
            o_ref[...] = res.astype(out_dtype)
            if second:
                refs[-2][...] = second[0](res, col_tile, *row_refs)

    a_spec = pl.BlockSpec((tm, tk), lambda i, j, kk: (i, kk))
    if mode == "nn":
        b_spec = pl.BlockSpec((tk, tn), lambda i, j, kk: (kk, j))
    else:
        b_spec = pl.BlockSpec((tn, tk), lambda i, j, kk: (j, kk))
    tile = pl.BlockSpec((tm, tn), lambda i, j, kk: (i, j))
    in_specs = [a_spec, b_spec]
    args = [a, b]
    if has_add:
        in_specs.append(tile)
        args.append(addend)
    if second:
        in_specs += [pl.BlockSpec((1, tn), lambda i, j, kk: (0, j))] * n_rows
        args += list(second[1])
    outs = _call(
        body, side, name=name, grid=(m // tm, n // tn, nk),
        in_specs=in_specs, out_specs=[tile] * n_out,
        out_shape=[jax.ShapeDtypeStruct((m, n), out_dtype)] + [jax.ShapeDtypeStruct((m, n), F32)] * (n_out - 1),
        scratch_shapes=[pltpu.VMEM((tm, tn), F32)],
        semantics=("parallel", "parallel", "arbitrary"), args=args,
    )
    return outs if second else outs[0]


def _matmul_tn(a, b, *, name, tm=1408, tn=512, tk=1024):
    t, m = a.shape
    t2, n = b.shape
    assert t == t2
    tm, tn, tk = _tile(m, tm), _tile(n, tn), _tile(t, tk)

    def body(a_ref, b_ref, o_ref):
        part = _dot_tn(a_ref[...].astype(BF16), b_ref[...].astype(BF16))

        @pl.when(pl.program_id(2) == 0)
        def _():
            o_ref[...] = part

        @pl.when(pl.program_id(2) > 0)
        def _():
            o_ref[...] += part

    return pl.pallas_call(
        body, name=name, grid=(m // tm, n // tn, t // tk),
        in_specs=[pl.BlockSpec((tk, tm), lambda i, j, kk: (kk, i)),
                  pl.BlockSpec((tk, tn), lambda i, j, kk: (kk, j))],
        out_specs=pl.BlockSpec((tm, tn), lambda i, j, kk: (i, j)),
        out_shape=jax.ShapeDtypeStruct((m, n), F32),
        compiler_params=_params("parallel", "parallel", "arbitrary"),
    )(a, b)


def _rmsnorm_fwd(x, gain, *, name):
    t, d = x.shape
    tm = _tile(t, 512)

    def body(x_ref, g_ref, o_ref):
        xv = x_ref[...]
        r = lax.rsqrt(jnp.mean(xv * xv, axis=-1, keepdims=True) + NORM_EPS)
        o_ref[...] = (xv * r * g_ref[...]).astype(BF16)

    return pl.pallas_call(
        body, name=name, grid=(t // tm,),
        in_specs=[pl.BlockSpec((tm, d), lambda i: (i, 0)), pl.BlockSpec((1, d), lambda i: (0, 0))],
        out_specs=pl.BlockSpec((tm, d), lambda i: (i, 0)),
        out_shape=jax.ShapeDtypeStruct((t, d), BF16),
        compiler_params=_params("parallel"),
    )(x, gain)


def _rmsnorm_bwd(x, gain, dy, dres, *, name):
    t, d = x.shape
    tm = _tile(t, 512)

    def body(x_ref, g_ref, dy_ref, dres_ref, dx_ref, dg_ref):
        xv = x_ref[...]
        r = lax.rsqrt(jnp.mean(xv * xv, axis=-1, keepdims=True) + NORM_EPS)
        xh = xv * r
        dyv = dy_ref[...]
        dxh = dyv * g_ref[...]
        mean = jnp.mean(dxh * xh, axis=-1, keepdims=True)
        dx_ref[...] = dres_ref[...] + r * (dxh - xh * mean)
        part = jnp.sum(dyv * xh, axis=0, keepdims=True)

        @pl.when(pl.program_id(0) == 0)
        def _():
            dg_ref[...] = part

        @pl.when(pl.program_id(0) > 0)
        def _():
            dg_ref[...] += part

    row = pl.BlockSpec((tm, d), lambda i: (i, 0))
    vec = pl.BlockSpec((1, d), lambda i: (0, 0))
    return pl.pallas_call(
        body, name=name, grid=(t // tm,),
        in_specs=[row, vec, row, row], out_specs=[row, vec],
        out_shape=[jax.ShapeDtypeStruct((t, d), F32), jax.ShapeDtypeStruct((1, d), F32)],
        compiler_params=_params("arbitrary"),
    )(x, gain, dy, dres)


def _loss_head(y, target):
    t, d = y.shape
    tm = _tile(t, 512)
    steps = t // tm

    def body(y_ref, t_ref, dy_ref, l_ref, acc_ref):
        e = y_ref[...] - t_ref[...]
        dy_ref[...] = e * (1.0 / d)
        part = jnp.sum(e * e, axis=0, keepdims=True)

        @pl.when(pl.program_id(0) == 0)
        def _():
            acc_ref[...] = part

        @pl.when(pl.program_id(0) > 0)
        def _():
            acc_ref[...] += part

        @pl.when(pl.program_id(0) == steps - 1)
        def _():
            l_ref[...] = jnp.full((1, LANES), (0.5 / d), F32) * jnp.sum(acc_ref[...])

    row = pl.BlockSpec((tm, d), lambda i: (i, 0))
    return pl.pallas_call(
        body, name="loss_head", grid=(steps,),
        in_specs=[row, row], out_specs=[row, pl.BlockSpec((1, LANES), lambda i: (0, 0))],
        out_shape=[jax.ShapeDtypeStruct((t, d), F32), jax.ShapeDtypeStruct((1, LANES), F32)],
        scratch_shapes=[pltpu.VMEM((1, d), F32)],
        compiler_params=_params("arbitrary"),
    )(y, target)


def _swiglu_fwd(h, w_gate_t, w_up_t, *, name, side=None):
    t, d = h.shape
    f = w_gate_t.shape[0]
    tm, tn = _tile(t, 1024), _tile(f, 256)

    def body(h_ref, wg_ref, wu_ref, g_ref, u_ref, a_ref):
        hv = h_ref[...]
        g = _dot_nt(hv, wg_ref[...])
        u = _dot_nt(hv, wu_ref[...])
        g_ref[...] = g.astype(BF16)
        u_ref[...] = u.astype(BF16)
        a_ref[...] = (g * _sigmoid(g) * u).astype(BF16)

    wspec = pl.BlockSpec((tn, d), lambda i, j: (j, 0))
    ospec = pl.BlockSpec((tm, tn), lambda i, j: (i, j))
    return _call(
        body, side, name=name, grid=(t // tm, f // tn),
        in_specs=[pl.BlockSpec((tm, d), lambda i, j: (i, 0)), wspec, wspec],
        out_specs=[ospec, ospec, ospec],
        out_shape=[jax.ShapeDtypeStruct((t, f), BF16), jax.ShapeDtypeStruct((t, f), BF16),
                   jax.ShapeDtypeStruct((t, f), BF16)],
        scratch_shapes=[], semantics=("parallel", "parallel"), args=(h, w_gate_t, w_up_t),
    )


def _swiglu_bwd(dx, w_down, g, u, *, name, side=None):
    t, d = dx.shape
    f = w_down.shape[0]
    tm, tn = _tile(t, 1024), _tile(f, 256)

    def body(dx_ref, wd_ref, g_ref, u_ref, dg_ref, du_ref):
        dact = _dot_nt(dx_ref[...].astype(BF16), wd_ref[...])
        gv, uv = g_ref[...].astype(F32), u_ref[...].astype(F32)
        sg = _sigmoid(gv)
        dg_ref[...] = (dact * uv * sg * (1.0 + gv * (1.0 - sg))).astype(BF16)
        du_ref[...] = (dact * gv * sg).astype(BF16)

    ospec = pl.BlockSpec((tm, tn), lambda i, j: (i, j))
    return _call(
        body, side, name=name, grid=(t // tm, f // tn),
        in_specs=[pl.BlockSpec((tm, d), lambda i, j: (i, 0)), pl.BlockSpec((tn, d), lambda i, j: (j, 0)),
                  ospec, ospec],
        out_specs=[ospec, ospec],
        out_shape=[jax.ShapeDtypeStruct((t, f), BF16), jax.ShapeDtypeStruct((t, f), BF16)],
        scratch_shapes=[], semantics=("parallel", "parallel"), args=(dx, w_down, g, u),
    )


def _ple_fwd(x, p, w_gate, w_proj_t, *, name):
    t, d = x.shape
    e = p.shape[1]
    tm, tn = _tile(t, 1024), _tile(d, 512)

    def body(xf_ref, xr_ref, p_ref, wg_ref, wp_ref, o_ref):
        s = _dot(xf_ref[...].astype(BF16), wg_ref[...])
        ple = _dot_nt(p_ref[...].astype(BF16), wp_ref[...])
        o_ref[...] = xr_ref[...] + _sigmoid(s) * ple

    return pl.pallas_call(
        body, name=name, grid=(t // tm, d // tn),
        in_specs=[pl.BlockSpec((tm, d), lambda i, j: (i, 0)), pl.BlockSpec((tm, tn), lambda i, j: (i, j)),
                  pl.BlockSpec((tm, e), lambda i, j: (i, 0)), pl.BlockSpec((d, tn), lambda i, j: (0, j)),
                  pl.BlockSpec((tn, e), lambda i, j: (j, 0))],
        out_specs=pl.BlockSpec((tm, tn), lambda i, j: (i, j)),
        out_shape=jax.ShapeDtypeStruct((t, d), F32),
        compiler_params=_params("parallel", "parallel"),
    )(x, x, p, w_gate, w_proj_t)


def _ple_bwd(x, p, w_gate, w_proj_t, dout, *, name):
    t, d = x.shape
    e = p.shape[1]
    tm, tn = _tile(t, 1024), _tile(d, 512)

    def body(xf_ref, p_ref, wg_ref, wp_ref, do_ref, ds_ref, dple_ref):
        s = _dot(xf_ref[...].astype(BF16), wg_ref[...])
        ple = _dot_nt(p_ref[...].astype(BF16), wp_ref[...])
        gate = _sigmoid(s)
        dov = do_ref[...]
        dple_ref[...] = (dov * gate).astype(BF16)
        ds_ref[...] = (dov * ple * gate * (1.0 - gate)).astype(BF16)

    ospec = pl.BlockSpec((tm, tn), lambda i, j: (i, j))
    return pl.pallas_call(
        body, name=name, grid=(t // tm, d // tn),
        in_specs=[pl.BlockSpec((tm, d), lambda i, j: (i, 0)), pl.BlockSpec((tm, e), lambda i, j: (i, 0)),
                  pl.BlockSpec((d, tn), lambda i, j: (0, j)), pl.BlockSpec((tn, e), lambda i, j: (j, 0)), ospec],
        out_specs=[ospec, ospec],
        out_shape=[jax.ShapeDtypeStruct((t, d), BF16), jax.ShapeDtypeStruct((t, d), BF16)],
        compiler_params=_params("parallel", "parallel"),
    )(x, p, w_gate, w_proj_t, dout)


CONV_TIME_TILE = 256
CONV_HALO = 8


def _conv_taps(ext, w):
    acc = ext[CONV_HALO:, :] * w[CONV_WIDTH - 1:CONV_WIDTH, :]
    shifted = [ext[CONV_HALO:, :]]
    for j in range(1, CONV_WIDTH):
        sh = pltpu.roll(ext, j, 0)[CONV_HALO:, :]
        shifted.append(sh)
        acc = acc + sh * w[CONV_WIDTH - 1 - j:CONV_WIDTH - j, :]
    return acc, shifted


def _conv_fwd(u, w, b, side=None):
    t, c = u.shape
    tc = _tile(c, 256)
    tt = CONV_TIME_TILE

    def body(u_ref, w_ref, b_ref, o_ref):
        wv, bv = w_ref[...], b_ref[...]

        def tile(start, ext):
            pre = _conv_taps(ext, wv)[0] + bv
            o_ref[pl.ds(start, tt), :] = pre * _sigmoid(pre)

        tile(0, jnp.concatenate([jnp.zeros((CONV_HALO, tc), F32), u_ref[0:tt, :]], axis=0))

        def loop(i, carry):
            start = pl.multiple_of(i * tt, tt)
            tile(start, u_ref[pl.ds(start - CONV_HALO, tt + CONV_HALO), :])
            return carry

        lax.fori_loop(1, t // tt, loop, 0)

    col = pl.BlockSpec((t, tc), lambda j: (0, j))
    return _call(
        body, side, name="conv_fwd", grid=(c // tc,),
        in_specs=[col, pl.BlockSpec((CONV_WIDTH, tc), lambda j: (0, j)), pl.BlockSpec((1, tc), lambda j: (0, j))],
        out_specs=[col], out_shape=[jax.ShapeDtypeStruct((t, c), F32)],
        scratch_shapes=[], semantics=("parallel",), args=(u, w, b),
    )[0]


def _conv_bwd(u, w, b, dact, side=None):
    t, c = u.shape
    tc = _tile(c, 256)
    tt = CONV_TIME_TILE

    def body(u_ref, w_ref, b_ref, da_ref, du_ref, dw_ref, db_ref, dpre_ref):
        wv, bv = w_ref[...], b_ref[...]

        def tile(start, ext, sums):
            acc, shifted = _conv_taps(ext, wv)
            pre = acc + bv
            sg = _sigmoid(pre)
            dpre = da_ref[pl.ds(start, tt), :] * (sg * (1.0 + pre * (1.0 - sg)))
            dpre_ref[pl.ds(start, tt), :] = dpre
            new = [sums[0] + jnp.sum(dpre, axis=0, keepdims=True)]
            for j in range(CONV_WIDTH):
                new.append(sums[1 + j] + jnp.sum(dpre * shifted[j], axis=0, keepdims=True))
            return tuple(new)

        zero = jnp.zeros((1, tc), F32)
        sums = tile(0, jnp.concatenate([jnp.zeros((CONV_HALO, tc), F32), u_ref[0:tt, :]], axis=0),
                    (zero,) * (1 + CONV_WIDTH))

        def loop(i, sums):
            start = pl.multiple_of(i * tt, tt)
            return tile(start, u_ref[pl.ds(start - CONV_HALO, tt + CONV_HALO), :], sums)

        sums = lax.fori_loop(1, t // tt, loop, sums)
        db_ref[...] = sums[0]
        dw_ref[...] = jnp.concatenate([sums[1 + (CONV_WIDTH - 1 - k)] for k in range(CONV_WIDTH)], axis=0)
        dpre_ref[pl.ds(t, CONV_HALO), :] = jnp.zeros((CONV_HALO, tc), F32)

        def loop2(i, carry):
            start = pl.multiple_of(i * tt, tt)
            ext = dpre_ref[pl.ds(start, tt + CONV_HALO), :]
            acc = ext[0:tt, :] * wv[CONV_WIDTH - 1:CONV_WIDTH, :]
            for j in range(1, CONV_WIDTH):
                acc = acc + pltpu.roll(ext, tt + CONV_HALO - j, 0)[0:tt, :] * wv[CONV_WIDTH - 1 - j:CONV_WIDTH - j, :]
            du_ref[pl.ds(start, tt), :] = acc.astype(BF16)
            return carry

        lax.fori_loop(0, t // tt, loop2, 0)

    col = pl.BlockSpec((t, tc), lambda j: (0, j))
    return _call(
        body, side, name="conv_bwd", grid=(c // tc,),
        in_specs=[col, pl.BlockSpec((CONV_WIDTH, tc), lambda j: (0, j)), pl.BlockSpec((1, tc), lambda j: (0, j)), col],
        out_specs=[col, pl.BlockSpec((CONV_WIDTH, tc), lambda j: (0, j)), pl.BlockSpec((1, tc), lambda j: (0, j))],
        out_shape=[jax.ShapeDtypeStruct((t, c), BF16), jax.ShapeDtypeStruct((CONV_WIDTH, c), F32),
                   jax.ShapeDtypeStruct((1, c), F32)],
        scratch_shapes=[pltpu.VMEM((t + CONV_HALO, tc), F32)],
        semantics=("parallel",), args=(u, w, b, dact),
    )


def _softplus(v):
    e = jnp.exp(-jnp.abs(v))
    w = 1.0 + e
    log1p = jnp.where(w == 1.0, e, jnp.log(w) * (e / jnp.where(w == 1.0, 1.0, w - 1.0)))
    return jnp.maximum(v, 0.0) + log1p


def _split3(z):
    hi = z.astype(BF16)
    rest = z - hi.astype(F32)
    mid = rest.astype(BF16)
    return hi, mid, (rest - mid.astype(F32)).astype(BF16)


def _select_dot(z, ones):
    return sum(_dot(term, ones) for term in _split3(z))


def _ssd_prep_fwd(dt_raw, dt_bias, a_log):
    t = dt_raw.shape[0]
    cl = SSD_CHUNK

    def body(r_ref, b_ref, al_ref, acs_ref, dt_rep_ref, acs_rep_ref):
        dt = _softplus(r_ref[...] + b_ref[...])
        adt = dt * (-jnp.exp(al_ref[...]))
        li = lax.broadcasted_iota(jnp.int32, (cl, cl), 0)
        si = lax.broadcasted_iota(jnp.int32, (cl, cl), 1)
        tri = (si <= li).astype(F32)
        acs = jnp.dot(tri, adt, preferred_element_type=F32, precision=HIGHEST)
        acs_ref[...] = acs
        head = lax.broadcasted_iota(jnp.int32, (LANES, D_INNER), 0)
        chan = lax.broadcasted_iota(jnp.int32, (LANES, D_INNER), 1) // SSM_HEAD_DIM
        spread = (head == chan).astype(BF16)
        dt_rep_ref[...] = _select_dot(dt, spread)
        acs_rep_ref[...] = _select_dot(acs, spread)

    row = pl.BlockSpec((cl, LANES), lambda i: (i, 0))
    wide = pl.BlockSpec((cl, D_INNER), lambda i: (i, 0))
    vec = pl.BlockSpec((1, LANES), lambda i: (0, 0))
    return pl.pallas_call(
        body, name="ssd_prep_fwd", grid=(t // cl,),
        in_specs=[row, vec, vec], out_specs=[row, wide, wide],
        out_shape=[jax.ShapeDtypeStruct((t, LANES), F32), jax.ShapeDtypeStruct((t, D_INNER), F32),
                   jax.ShapeDtypeStruct((t, D_INNER), F32)],
        compiler_params=_params("parallel"),
    )(dt_raw, dt_bias, a_log)


def _ssd_prep_bwd(dt_raw, dt_bias, ddt):
    t = dt_raw.shape[0]
    tm = _tile(t, 512)

    def body(r_ref, b_ref, d_ref, o_ref, db_ref):
        g = d_ref[...] * _sigmoid(r_ref[...] + b_ref[...])
        o_ref[...] = g.astype(BF16)
        part = jnp.sum(g, axis=0, keepdims=True)

        @pl.when(pl.program_id(0) == 0)
        def _():
            db_ref[...] = part

        @pl.when(pl.program_id(0) > 0)
        def _():
            db_ref[...] += part

    row = pl.BlockSpec((tm, LANES), lambda i: (i, 0))
    vec = pl.BlockSpec((1, LANES), lambda i: (0, 0))
    return pl.pallas_call(
        body, name="ssd_prep_bwd", grid=(t // tm,),
        in_specs=[row, vec, row], out_specs=[row, vec],
        out_shape=[jax.ShapeDtypeStruct((t, LANES), BF16), jax.ShapeDtypeStruct((1, LANES), F32)],
        compiler_params=_params("arbitrary"),
    )(dt_raw, dt_bias, ddt)


GROUP_W = D_INNER // SSM_GROUPS
PAIRS_PER_GROUP = GROUP_W // LANES


def _head_cols(acs_pair, lt64):
    rolled = pltpu.roll(acs_pair, ATT_HEAD_DIM, 1)
    return jnp.where(lt64, acs_pair, rolled), jnp.where(lt64, rolled, acs_pair)


def _ssd_fwd(xbc, dt_rep, acs_rep, acs_t, dskip_rep, side=None):
    t = xbc.shape[0]
    cl = SSD_CHUNK
    nc = t // cl

    def body(xbc_ref, dt_ref, acs_ref, acst_ref, dskip_ref, y_ref, hin_ref, state_ref):
        @pl.when(pl.program_id(0) == 0)
        def _():
            state_ref[...] = jnp.zeros_like(state_ref)

        lt64 = _lane_lt64(cl)
        li = lax.broadcasted_iota(jnp.int32, (cl, cl), 0)
        si = lax.broadcasted_iota(jnp.int32, (cl, cl), 1)
        causal = li >= si
        hin_ref[...] = state_ref[...]
        for g in range(SSM_GROUPS):
            gsl = slice(g * GROUP_W, (g + 1) * GROUP_W)
            xg = xbc_ref[:, gsl]
            bg = xbc_ref[:, D_INNER + g * SSM_STATE:D_INNER + (g + 1) * SSM_STATE]
            cg = xbc_ref[:, D_INNER + SSM_GROUPS * SSM_STATE + g * SSM_STATE:
                         D_INNER + SSM_GROUPS * SSM_STATE + (g + 1) * SSM_STATE]
            acs = acs_ref[:, gsl]
            xdt = xg * dt_ref[:, gsl]
            atot = acs[cl - 1:cl, :]
            hin = state_ref[:, gsl]
            cgb = cg.astype(BF16)
            gmat = _dot_nt(cgb, bg.astype(BF16))
            yoff = _dot(cgb, hin.astype(BF16)) * jnp.exp(acs)
            snew = _dot(bg.T.astype(BF16), (xdt * jnp.exp(atot - acs)).astype(BF16))
            state_ref[:, gsl] = hin * jnp.exp(atot) + snew
            xdtb = xdt.astype(BF16)
            for pr in range(PAIRS_PER_GROUP):
                psl = slice(pr * LANES, (pr + 1) * LANES)
                cols = _head_cols(acs[:, psl], lt64)
                xp = xdtb[:, psl]
                ys = []
                for hh in range(2):
                    h = (g * PAIRS_PER_GROUP + pr) * 2 + hh
                    seg = cols[hh] - acst_ref[h:h + 1, :]
                    lm = jnp.exp(jnp.where(causal, seg, NEG_BIG))
                    ys.append(_dot((gmat * lm).astype(BF16), xp))
                ydiag = jnp.where(lt64, ys[0], ys[1])
                osl = slice(g * GROUP_W + pr * LANES, g * GROUP_W + (pr + 1) * LANES)
                y_ref[:, osl] = ydiag + yoff[:, psl] + xg[:, psl] * dskip_ref[:, osl]

    row = lambda w: pl.BlockSpec((cl, w), lambda c: (c, 0))
    return _call(
        body, side, name="ssd_fwd", grid=(nc,),
        in_specs=[row(CONV_DIM), row(D_INNER), row(D_INNER),
                  pl.BlockSpec((SSM_HEADS, cl), lambda c: (0, c)), pl.BlockSpec((1, D_INNER), lambda c: (0, 0))],
        out_specs=[row(D_INNER), pl.BlockSpec((None, SSM_STATE, D_INNER), lambda c: (c, 0, 0))],
        out_shape=[jax.ShapeDtypeStruct((t, D_INNER), F32), jax.ShapeDtypeStruct((nc, SSM_STATE, D_INNER), F32)],
        scratch_shapes=[pltpu.VMEM((SSM_STATE, D_INNER), F32)],
        semantics=("arbitrary",), args=(xbc, dt_rep, acs_rep, acs_t, dskip_rep),
    )


def _ssd_bwd(xbc, dt_rep, acs_rep, acs_t, dskip_rep, a_rep, hin_all, dy, side=None):
    t = xbc.shape[0]
    cl = SSD_CHUNK
    nc = t // cl

    def body(xbc_ref, dt_ref, acs_ref, acst_ref, dskip_ref, a_ref, hin_ref, dy_ref,
             dxbc_ref, ddt_ref, da_ref, dds_ref, dstate_ref, dacs_ref, dxs_ref):
        step = pl.program_id(0)

        @pl.when(step == 0)
        def _():
            dstate_ref[...] = jnp.zeros_like(dstate_ref)
            da_ref[...] = jnp.zeros_like(da_ref)
            dds_ref[...] = jnp.zeros_like(dds_ref)

        bd = _head_block_diag()
        lt64 = _lane_lt64(cl)
        li = lax.broadcasted_iota(jnp.int32, (cl, cl), 0)
        si = lax.broadcasted_iota(jnp.int32, (cl, cl), 1)
        lower = li >= si
        upper = si >= li
        last_row = lax.broadcasted_iota(jnp.int32, (cl, GROUP_W), 0) == cl - 1
        for g in range(SSM_GROUPS):
            gsl = slice(g * GROUP_W, (g + 1) * GROUP_W)
            bsl = slice(D_INNER + g * SSM_STATE, D_INNER + (g + 1) * SSM_STATE)
            csl = slice(D_INNER + SSM_GROUPS * SSM_STATE + g * SSM_STATE,
                        D_INNER + SSM_GROUPS * SSM_STATE + (g + 1) * SSM_STATE)
            xg = xbc_ref[:, gsl]
            bg = xbc_ref[:, bsl]
            cg = xbc_ref[:, csl]
            bgb, cgb = bg.astype(BF16), cg.astype(BF16)
            acs = acs_ref[:, gsl]
            xdt = xg * dt_ref[:, gsl]
            atot = acs[cl - 1:cl, :]
            eg = jnp.exp(acs)
            dk = jnp.exp(atot - acs)
            etot = jnp.exp(atot)
            hin = hin_ref[:, gsl]
            hinb = hin.astype(BF16)
            dh = dstate_ref[:, gsl]
            dhb = dh.astype(BF16)
            dyg = dy_ref[:, gsl]

            gmat = _dot_nt(cgb, bgb)
            gmat_t = _dot_nt(bgb, cgb)
            ch = _dot(cgb, hinb)
            dacs = _head_sums(dyg * ch * eg, bd)
            dye = (dyg * eg).astype(BF16)
            dc = _dot_nt(dye, hinb)
            dhin = _dot(cg.T.astype(BF16), dye)
            bdh = _dot(bgb, dhb)
            dxs = bdh * dk
            xdk = xdt * dk
            db = _dot_nt(xdk.astype(BF16), dhb)
            ddk = _head_sums(bdh * xdk, bd)
            dacs = dacs - ddk
            datot = jnp.sum(ddk, axis=0, keepdims=True) + etot * _head_sums(
                jnp.sum(dh * hin, axis=0, keepdims=True), bd)
            dacs = dacs + jnp.where(last_row, datot, 0.0)
            dstate_ref[:, gsl] = dh * etot + dhin

            xdtb = xdt.astype(BF16)
            dgsum = jnp.zeros((cl, cl), F32)
            dgsum_t = jnp.zeros((cl, cl), F32)
            for pr in range(PAIRS_PER_GROUP):
                psl = slice(pr * LANES, (pr + 1) * LANES)
                cols = _head_cols(acs[:, psl], lt64)
                xp = xdtb[:, psl]
                dyp = dyg[:, psl].astype(BF16)
                dx1, dac = [], []
                for hh in range(2):
                    h = (g * PAIRS_PER_GROUP + pr) * 2 + hh
                    mine = lt64 if hh == 0 else jnp.logical_not(lt64)
                    row = acst_ref[h:h + 1, :]
                    lm = jnp.exp(jnp.where(lower, cols[hh] - row, NEG_BIG))
                    lm_t = jnp.exp(jnp.where(upper, row - cols[hh], NEG_BIG))
                    dyh = jnp.where(mine, dyp, jnp.zeros_like(dyp))
                    xh = jnp.where(mine, xp, jnp.zeros_like(xp))
                    dm = _dot_nt(dyh, xp)
                    dm_t = _dot_nt(xh, dyp)
                    m_t = gmat_t * lm_t
                    dx1.append(_dot(m_t.astype(BF16), dyp))
                    w = dm * (gmat * lm)
                    w_t = dm_t * m_t
                    dac.append(jnp.sum(w, axis=1, keepdims=True) - jnp.sum(w_t, axis=1, keepdims=True))
                    dgsum = dgsum + dm * lm
                    dgsum_t = dgsum_t + dm_t * lm_t
                osl = slice(g * GROUP_W + pr * LANES, g * GROUP_W + (pr + 1) * LANES)
                dxs_ref[:, osl] = dxs[:, psl] + jnp.where(lt64, dx1[0], dx1[1])
                dacs_ref[:, osl] = dacs[:, psl] + jnp.where(lt64, jnp.broadcast_to(dac[0], (cl, LANES)),
                                                             jnp.broadcast_to(dac[1], (cl, LANES)))
            dxbc_ref[:, csl] = dc + _dot(dgsum.astype(BF16), bgb)
            dxbc_ref[:, bsl] = db + _dot(dgsum_t.astype(BF16), cgb)

        dadt = _split_dot(upper.astype(BF16), dacs_ref[...])
        xall = xbc_ref[:, 0:D_INNER]
        dtall = dt_ref[...]
        dxsall = dxs_ref[...]
        dyall = dy_ref[...]
        ddt_rep = dadt * a_ref[...] + _head_sums(dxsall * xall, bd)
        chan = lax.broadcasted_iota(jnp.int32, (D_INNER, LANES), 0)
        head = lax.broadcasted_iota(jnp.int32, (D_INNER, LANES), 1)
        ddt_ref[...] = _select_dot(ddt_rep, (chan == head * SSM_HEAD_DIM).astype(BF16))
        dxbc_ref[:, 0:D_INNER] = dxsall * dtall + dyall * dskip_ref[...]
        da_ref[...] += jnp.sum(dadt * dtall, axis=0, keepdims=True)
        dds_ref[...] += jnp.sum(dyall * xall, axis=0, keepdims=True)

        @pl.when(step == nc - 1)
        def _():
            dds_ref[...] = _head_sums(dds_ref[...], bd)

    row = lambda w: pl.BlockSpec((cl, w), lambda c: (nc - 1 - c, 0))
    vec = pl.BlockSpec((1, D_INNER), lambda c: (0, 0))
    return _call(
        body, side, name="ssd_bwd", grid=(nc,),
        in_specs=[row(CONV_DIM), row(D_INNER), row(D_INNER),
                  pl.BlockSpec((SSM_HEADS, cl), lambda c: (0, nc - 1 - c)), vec, vec,
                  pl.BlockSpec((None, SSM_STATE, D_INNER), lambda c: (nc - 1 - c, 0, 0)), row(D_INNER)],
        out_specs=[row(CONV_DIM), row(LANES), vec, vec],
        out_shape=[jax.ShapeDtypeStruct((t, CONV_DIM), F32), jax.ShapeDtypeStruct((t, LANES), F32),
                   jax.ShapeDtypeStruct((1, D_INNER), F32), jax.ShapeDtypeStruct((1, D_INNER), F32)],
        scratch_shapes=[pltpu.VMEM((SSM_STATE, D_INNER), F32), pltpu.VMEM((cl, D_INNER), F32),
                        pltpu.VMEM((cl, D_INNER), F32)],
        semantics=("arbitrary",), args=(xbc, dt_rep, acs_rep, acs_t, dskip_rep, a_rep, hin_all, dy),
    )


def _gate_norm_fwd(y, z, w):
    t, c = y.shape
    tm = _tile(t, 256)

    def body(y_ref, z_ref, w_ref, o_ref):
        for g in range(SSM_GROUPS):
            gsl = slice(g * GROUP_W, (g + 1) * GROUP_W)
            zv = z_ref[:, gsl]
            v = y_ref[:, gsl] * (zv * _sigmoid(zv))
            r = lax.rsqrt(jnp.mean(v * v, axis=-1, keepdims=True) + NORM_EPS)
            o_ref[:, gsl] = (v * r * w_ref[:, gsl]).astype(BF16)

    row = pl.BlockSpec((tm, c), lambda i: (i, 0))
    return pl.pallas_call(
        body, name="gate_norm_fwd", grid=(t // tm,),
        in_specs=[row, row, pl.BlockSpec((1, c), lambda i: (0, 0))], out_specs=row,
        out_shape=jax.ShapeDtypeStruct((t, c), BF16),
        compiler_params=_params("parallel"),
    )(y, z, w)


def _gate_norm_bwd(y, z, w, dout, side=None):
    t, c = y.shape
    tm = _tile(t, 256)

    def body(y_ref, z_ref, w_ref, do_ref, dy_ref, dz_ref, dw_ref):
        @pl.when(pl.program_id(0) == 0)
        def _():
            dw_ref[...] = jnp.zeros_like(dw_ref)

        for g in range(SSM_GROUPS):
            gsl = slice(g * GROUP_W, (g + 1) * GROUP_W)
            zv, yv, dov = z_ref[:, gsl], y_ref[:, gsl], do_ref[:, gsl]
            sg = _sigmoid(zv)
            sz = zv * sg
            v = yv * sz
            r = lax.rsqrt(jnp.mean(v * v, axis=-1, keepdims=True) + NORM_EPS)
            vh = v * r
            dvh = dov * w_ref[:, gsl]
            mean = jnp.mean(dvh * vh, axis=-1, keepdims=True)
            dv = r * (dvh - vh * mean)
            dy_ref[:, gsl] = dv * sz
            dz_ref[:, gsl] = (dv * yv * (sg * (1.0 + zv * (1.0 - sg)))).astype(BF16)
            dw_ref[:, gsl] += jnp.sum(dov * vh, axis=0, keepdims=True)

    row = pl.BlockSpec((tm, c), lambda i: (i, 0))
    vec = pl.BlockSpec((1, c), lambda i: (0, 0))
    return _call(
        body, side, name="gate_norm_bwd", grid=(t // tm,),
        in_specs=[row, row, vec, row], out_specs=[row, row, vec],
        out_shape=[jax.ShapeDtypeStruct((t, c), F32), jax.ShapeDtypeStruct((t, c), BF16),
                   jax.ShapeDtypeStruct((1, c), F32)],
        scratch_shapes=[], semantics=("arbitrary",), args=(y, z, w, dout),
    )


ATT_W = ATT_HEADS * ATT_HEAD_DIM
N_QKV_BLOCKS = 9
ATT_SCALE = 1.0 / math.sqrt(ATT_HEAD_DIM)


def _head_rmsnorm(x, gain, bd):
    ms = _head_sums(x * x, bd, terms=1) * (1.0 / ATT_HEAD_DIM)
    return x * lax.rsqrt(ms + NORM_EPS) * gain


def _class_rows(ref, blk, r, dil):
    span = ATT_BLOCK * dil
    sub = ref.at[pl.ds(pl.multiple_of(blk * span, span), span), :]
    return sub[...] if dil == 1 else sub[pl.ds(r, ATT_BLOCK, stride=dil), :]


def _store_class_rows(ref, blk, r, dil, val):
    span = ATT_BLOCK * dil
    sub = ref.at[pl.ds(pl.multiple_of(blk * span, span), span), :]
    if dil == 1:
        sub[...] = val
    else:
        sub[pl.ds(r, ATT_BLOCK, stride=dil), :] = val


PAIRS = ATT_HEADS // 2


def _pair_col(g, j):
    return lambda pair: (0, (g * 3 + j) * PAIRS + pair)


def _pair_slopes(pair):
    steps = jnp.full((1, 2 * ATT_BLOCK), 2 * pair + 1, jnp.int32).astype(F32)
    first = jnp.exp(steps * (-0.5 * math.log(2.0)))
    return first, first * (2.0 ** -0.5)


NORM_ROWS = 512


ROW_SLICES = 4
SLICE_ROWS = 2 * ATT_BLOCK // ROW_SLICES


def _fill_band_bias(bias_ref, pair, dil, transposed):
    bq = ATT_BLOCK
    a = lax.broadcasted_iota(jnp.int32, (2 * bq, 2 * bq), 0) % bq
    b = lax.broadcasted_iota(jnp.int32, (2 * bq, 2 * bq), 1)
    dist = (b - a) if transposed else (a + bq - b)
    in_band = (dist >= 0) & (dist <= bq)
    s0, s1 = _pair_slopes(pair)
    first_head = lax.broadcasted_iota(jnp.int32, (2 * bq, 2 * bq), 0) < bq
    bias = jnp.where(first_head, s0, s1) * (dist.astype(F32) * float(dil))
    inside = (b < bq) if transposed else (b >= bq)
    bias_ref[1] = jnp.where(in_band, bias, -NEG_BIG)
    bias_ref[0] = jnp.where(in_band & inside, bias, -NEG_BIG)


def _row_slices():
    return [slice(i * SLICE_ROWS, (i + 1) * SLICE_ROWS) for i in range(ROW_SLICES)]


def _stack_heads(tile):
    rows = lax.broadcasted_iota(jnp.int32, (2 * ATT_BLOCK, LANES), 0) < ATT_BLOCK
    lanes = lax.broadcasted_iota(jnp.int32, (2 * ATT_BLOCK, LANES), 1) < ATT_HEAD_DIM
    both = jnp.concatenate([tile, tile], axis=0)
    return jnp.where(rows == lanes, both, jnp.zeros_like(both))


def _unstack_heads(stacked, lt64):
    return jnp.where(lt64, stacked[:ATT_BLOCK], stacked[ATT_BLOCK:])


ITEMS_PER_PASS = 4


def _item_loop(nb, dil, work):
    if dil == 1:
        def trip(i, carry):
            work([(i * ITEMS_PER_PASS + b, 0) for b in range(ITEMS_PER_PASS)])
            return carry

        lax.fori_loop(0, nb // ITEMS_PER_PASS, trip, 0)
    else:
        def trip(n, carry):
            for r0 in range(0, dil, ITEMS_PER_PASS):
                work([(n, r0 + j) for j in range(ITEMS_PER_PASS)])
            return carry

        lax.fori_loop(0, nb, trip, 0)


def _qk_normalised(tile, j, gq_ref, gk_ref):
    kind = (j // (ATT_W // tile.shape[1])) % 3
    gain = jnp.where(kind == 0, gq_ref[...] * ATT_SCALE, gk_ref[...])
    return jnp.where(kind == 2, tile, _head_rmsnorm(tile, gain, _head_block_diag()))


def _attn_fwd(qkn, g, dil):
    t = qkn.shape[0]
    nb = t // dil // ATT_BLOCK
    bq = ATT_BLOCK

    def body(qn_ref, kn_ref, v_ref, o_ref, l_ref, bias_ref):
        _fill_band_bias(bias_ref, pl.program_id(0), dil, False)
        lt64 = _lane_lt64(bq)

        def work(items):
            scores, values, probs = [], [], []
            for n, r in items:
                prev = jnp.maximum(n - 1, 0)
                q2 = _stack_heads(_class_rows(qn_ref, n, r, dil).astype(BF16))
                kcat = jnp.concatenate([_class_rows(kn_ref, prev, r, dil), _class_rows(kn_ref, n, r, dil)],
                                       axis=0).astype(BF16)
                values.append(jnp.concatenate([_class_rows(v_ref, prev, r, dil), _class_rows(v_ref, n, r, dil)],
                                              axis=0).astype(BF16))
                scores.append(_dot_nt(q2, kcat))
            for (n, r), sc in zip(items, scores):
                bias = bias_ref.at[jnp.minimum(n, 1)]
                ps, inv, lses = [], [], []
                for rows in _row_slices():
                    s = sc[rows] - bias[rows, :]
                    m = jnp.max(s, axis=1, keepdims=True)
                    p = jnp.exp(s - m)
                    l = jnp.sum(p, axis=1, keepdims=True)
                    ps.append(p.astype(BF16))
                    inv.append(jnp.broadcast_to(1.0 / l, (SLICE_ROWS, LANES)))
                    lses.append(jnp.broadcast_to(m + jnp.log(l), (SLICE_ROWS, LANES)))
                probs.append((jnp.concatenate(ps, axis=0), jnp.concatenate(inv, axis=0)))
                _store_class_rows(l_ref, n, r, dil, _unstack_heads(jnp.concatenate(lses, axis=0), lt64))
            for (n, r), (p, inv), vcat in zip(items, probs, values):
                _store_class_rows(o_ref, n, r, dil, _unstack_heads(_dot(p, vcat) * inv, lt64))

        _item_loop(nb, dil, work)

    col = lambda j: pl.BlockSpec((t, LANES), _pair_col(g, j))
    out = pl.BlockSpec((t, LANES), lambda pair: (0, pair))
    return pl.pallas_call(
        body, name=f"attn_fwd_g{g}", grid=(PAIRS,),
        in_specs=[col(0), col(1), col(2)], out_specs=[out, out],
        out_shape=[jax.ShapeDtypeStruct((t, ATT_W), F32), jax.ShapeDtypeStruct((t, ATT_W), F32)],
        scratch_shapes=[pltpu.VMEM((2, 2 * bq, 2 * bq), F32)],
        compiler_params=_params("parallel"),
    )(qkn, qkn, qkn)


def _one_per_head(rep):
    chan = lax.broadcasted_iota(jnp.int32, (ATT_W, LANES), 0)
    head = lax.broadcasted_iota(jnp.int32, (ATT_W, LANES), 1)
    return _select_dot(rep, (chan == head * ATT_HEAD_DIM).astype(BF16))


def _attn_combine_fwd(outs, lses):
    t = outs[0].shape[0]
    tm = _tile(t, 256)

    def body(o0, o1, o2, l0, l1, l2, ob_ref, of_ref, lt_ref, lc_ref):
        a, b, c = l0[...], l1[...], l2[...]
        m = jnp.maximum(jnp.maximum(a, b), c)
        ea, eb, ec = jnp.exp(a - m), jnp.exp(b - m), jnp.exp(c - m)
        ssum = ea + eb + ec
        o = (ea * o0[...] + eb * o1[...] + ec * o2[...]) / ssum
        ob_ref[...] = o.astype(BF16)
        of_ref[...] = o
        lse = m + jnp.log(ssum)
        lt_ref[...] = lse
        lc_ref[...] = _one_per_head(lse)

    row = pl.BlockSpec((tm, ATT_W), lambda i: (i, 0))
    return pl.pallas_call(
        body, name="attn_combine_fwd", grid=(t // tm,),
        in_specs=[row] * 6, out_specs=[row] * 3 + [pl.BlockSpec((tm, LANES), lambda i: (i, 0))],
        out_shape=[jax.ShapeDtypeStruct((t, ATT_W), BF16), jax.ShapeDtypeStruct((t, ATT_W), F32),
                   jax.ShapeDtypeStruct((t, ATT_W), F32), jax.ShapeDtypeStruct((t, LANES), F32)],
        compiler_params=_params("parallel"),
    )(*outs, *lses)


def _attn_combine_bwd(do, o):
    t = do.shape[0]
    tm = _tile(t, 256)

    def body(do_ref, o_ref, dl_ref, dc_ref):
        dl = _head_sums(do_ref[...] * o_ref[...], _head_block_diag())
        dl_ref[...] = dl
        dc_ref[...] = _one_per_head(dl)

    row = pl.BlockSpec((tm, ATT_W), lambda i: (i, 0))
    return pl.pallas_call(
        body, name="attn_combine_bwd", grid=(t // tm,),
        in_specs=[row, row], out_specs=[row, pl.BlockSpec((tm, LANES), lambda i: (i, 0))],
        out_shape=[jax.ShapeDtypeStruct((t, ATT_W), F32), jax.ShapeDtypeStruct((t, LANES), F32)],
        compiler_params=_params("parallel"),
    )(do, o)


def _head_rmsnorm_bwd(x_ref, dy_ref, gain_ref, dx_ref, dgain_ref):
    bd = _head_block_diag()
    gain = gain_ref[...]

    def step(i, acc):
        rows = pl.ds(pl.multiple_of(i * NORM_ROWS, NORM_ROWS), NORM_ROWS)
        x, dy = x_ref[rows, :], dy_ref[rows, :]
        r = lax.rsqrt(_head_sums(x * x, bd, terms=1) * (1.0 / ATT_HEAD_DIM) + NORM_EPS)
        xh = x * r
        dxh = dy * gain
        mean = _head_sums(dxh * xh, bd, terms=1) * (1.0 / ATT_HEAD_DIM)
        dx_ref[rows, :] = (r * (dxh - xh * mean)).astype(BF16)
        return acc + jnp.sum(dy * xh, axis=0, keepdims=True)

    acc = lax.fori_loop(0, x_ref.shape[0] // NORM_ROWS, step, jnp.zeros((1, LANES), F32))
    dgain_ref[...] = jnp.broadcast_to(acc, dgain_ref.shape)


def _attn_bwd_dq(qkv, qkn, gq, do, l_rep, dl_rep, g, dil):
    t = qkv.shape[0]
    nb = t // dil // ATT_BLOCK
    bq = ATT_BLOCK

    def body(q_ref, qn_ref, kn_ref, v_ref, gq_ref, do_ref, l_ref, dl_ref, dx_ref, dgain_ref, bias_ref, dq_ref):
        _fill_band_bias(bias_ref, pl.program_id(0), dil, False)
        lt64 = _lane_lt64(bq)

        def per_row(tile):
            cols = _head_cols(tile, lt64)
            half = jnp.concatenate([cols[0], cols[1]], axis=0)
            return jnp.concatenate([half, half], axis=1)

        def work(items):
            products, keys, dscores = [], [], []
            for n, r in items:
                prev = jnp.maximum(n - 1, 0)
                q2 = _stack_heads(_class_rows(qn_ref, n, r, dil).astype(BF16))
                do2 = _stack_heads(_class_rows(do_ref, n, r, dil).astype(BF16))
                kcat = jnp.concatenate([_class_rows(kn_ref, prev, r, dil), _class_rows(kn_ref, n, r, dil)],
                                       axis=0).astype(BF16)
                vcat = jnp.concatenate([_class_rows(v_ref, prev, r, dil), _class_rows(v_ref, n, r, dil)],
                                       axis=0).astype(BF16)
                keys.append(kcat)
                products.append((_dot_nt(q2, kcat), _dot_nt(do2, vcat)))
            for (n, r), (scores, dps) in zip(items, products):
                bias = bias_ref.at[jnp.minimum(n, 1)]
                lse = per_row(_class_rows(l_ref, n, r, dil))
                dl = per_row(_class_rows(dl_ref, n, r, dil))
                dss = []
                for rows in _row_slices():
                    p = jnp.exp(scores[rows] - bias[rows, :] - lse[rows])
                    dss.append((p * (dps[rows] - dl[rows])).astype(BF16))
                dscores.append(jnp.concatenate(dss, axis=0))
            for (n, r), ds, kcat in zip(items, dscores, keys):
                _store_class_rows(dq_ref, n, r, dil, _unstack_heads(_dot(ds, kcat) * ATT_SCALE, lt64))

        _item_loop(nb, dil, work)
        _head_rmsnorm_bwd(q_ref, dq_ref, gq_ref, dx_ref, dgain_ref)

    col = lambda j: pl.BlockSpec((t, LANES), _pair_col(g, j))
    vec = pl.BlockSpec((1, LANES), lambda pair: (0, 0))
    tok = pl.BlockSpec((t, LANES), lambda pair: (0, pair))
    return pl.pallas_call(
        body, name=f"attn_bwd_dq_g{g}", grid=(PAIRS,),
        in_specs=[col(0), col(0), col(1), col(2), vec, tok, tok, tok],
        out_specs=[tok, pl.BlockSpec((None, 8, LANES), lambda pair: (pair, 0, 0))],
        out_shape=[jax.ShapeDtypeStruct((t, ATT_W), BF16), jax.ShapeDtypeStruct((PAIRS, 8, LANES), F32)],
        scratch_shapes=[pltpu.VMEM((2, 2 * bq, 2 * bq), F32), pltpu.VMEM((t, LANES), F32)],
        compiler_params=_params("parallel"),
    )(qkv, qkn, qkn, qkn, gq, do, l_rep, dl_rep)


def _attn_bwd_dkv(qkv, qkn, gk, do, l_row, dl_row, g, dil):
    t = qkv.shape[0]
    nb = t // dil // ATT_BLOCK
    bq = ATT_BLOCK

    def body(k_ref, qn_ref, kn_ref, v_ref, gk_ref, do_ref, l_ref, dl_ref, dkx_ref, dvx_ref, dgain_ref, bias_ref,
             dk_ref, dv_ref):
        _fill_band_bias(bias_ref, pl.program_id(0), dil, True)
        lt64 = _lane_lt64(bq)

        def per_query(ref, hh, lane_c, lane_n):
            return jnp.concatenate([ref[hh:hh + 1, pl.ds(lane_c, bq)], ref[hh:hh + 1, pl.ds(lane_n, bq)]], axis=1)

        def work(items):
            products, operands, weights = [], [], []
            for n, r in items:
                nxt = jnp.minimum(n + 1, nb - 1)
                k2 = _stack_heads(_class_rows(kn_ref, n, r, dil).astype(BF16))
                v2 = _stack_heads(_class_rows(v_ref, n, r, dil).astype(BF16))
                qcat = jnp.concatenate([_class_rows(qn_ref, n, r, dil), _class_rows(qn_ref, nxt, r, dil)],
                                       axis=0).astype(BF16)
                docat = jnp.concatenate([_class_rows(do_ref, n, r, dil), _class_rows(do_ref, nxt, r, dil)],
                                        axis=0).astype(BF16)
                operands.append((qcat, docat))
                products.append((_dot_nt(k2, qcat), _dot_nt(v2, docat)))
            for (n, r), (scores, dps) in zip(items, products):
                nxt = jnp.minimum(n + 1, nb - 1)
                bias = bias_ref.at[jnp.where(n == nb - 1, 0, 1)]
                lane_c = pl.multiple_of((r * nb + n) * bq, bq)
                lane_n = pl.multiple_of((r * nb + nxt) * bq, bq)
                lse = [per_query(l_ref, hh, lane_c, lane_n) for hh in range(2)]
                dl = [per_query(dl_ref, hh, lane_c, lane_n) for hh in range(2)]
                pts, dss = [], []
                for i, rows in enumerate(_row_slices()):
                    hh = i * SLICE_ROWS // bq
                    p_t = jnp.exp(scores[rows] - bias[rows, :] - lse[hh])
                    pts.append(p_t.astype(BF16))
                    dss.append((p_t * (dps[rows] - dl[hh])).astype(BF16))
                weights.append((jnp.concatenate(pts, axis=0), jnp.concatenate(dss, axis=0)))
            for (n, r), (p_t, ds_t), (qcat, docat) in zip(items, weights, operands):
                _store_class_rows(dv_ref, n, r, dil, _unstack_heads(_dot(p_t, docat), lt64))
                _store_class_rows(dk_ref, n, r, dil, _unstack_heads(_dot(ds_t, qcat), lt64))

        _item_loop(nb, dil, work)
        _head_rmsnorm_bwd(k_ref, dk_ref, gk_ref, dkx_ref, dgain_ref)

        def cast_rows(i, carry):
            rows = pl.ds(pl.multiple_of(i * NORM_ROWS, NORM_ROWS), NORM_ROWS)
            dvx_ref[rows, :] = dv_ref[rows, :].astype(BF16)
            return carry

        lax.fori_loop(0, t // NORM_ROWS, cast_rows, 0)

    col = lambda j: pl.BlockSpec((t, LANES), _pair_col(g, j))
    vec = pl.BlockSpec((1, LANES), lambda pair: (0, 0))
    tok = pl.BlockSpec((t, LANES), lambda pair: (0, pair))
    rows = pl.BlockSpec((None, 8, t), lambda pair: (pair, 0, 0))
    return pl.pallas_call(
        body, name=f"attn_bwd_dkv_g{g}", grid=(PAIRS,),
        in_specs=[col(1), col(0), col(1), col(2), vec, tok, rows, rows],
        out_specs=[tok, tok, pl.BlockSpec((None, 8, LANES), lambda pair: (pair, 0, 0))],
        out_shape=[jax.ShapeDtypeStruct((t, ATT_W), BF16), jax.ShapeDtypeStruct((t, ATT_W), BF16),
                   jax.ShapeDtypeStruct((PAIRS, 8, LANES), F32)],
        scratch_shapes=[pltpu.VMEM((2, 2 * bq, 2 * bq), F32), pltpu.VMEM((t, LANES), F32),
                        pltpu.VMEM((t, LANES), F32)],
        compiler_params=_params("parallel"),
    )(qkv, qkn, qkn, qkn, gk, do, l_row, dl_row)


def _rows_by_residue(one_per_head, dil):
    t = one_per_head.shape[0]
    per_head = one_per_head[:, :ATT_HEADS]
    rows = per_head.reshape(t // dil, dil, ATT_HEADS).transpose(2, 1, 0).reshape(PAIRS, 2, t)
    return jnp.pad(rows, ((0, 0), (0, 6), (0, 0)))


def _per_head(rep_row):
    return rep_row[0, ::SSM_HEAD_DIM]


def _rep_heads(v):
    return jnp.repeat(v, SSM_HEAD_DIM)[None, :]


def _pad_lanes(v):
    return jnp.pad(v, ((0, 0), (0, LANES - v.shape[1])))


class _NoOverlap:
    def side(self, host):
        return None

    def after(self, host):
        pass

    def begin_backward(self, grads):
        pass


def _hosted(plan, host, fn, *args, **kwargs):
    out = fn(*args, side=plan.side(host), **kwargs)
    plan.after(host)
    return out


def _ffn_ple_fwd(x1, p_i, prm, i, plan):
    h = _rmsnorm_fwd(x1, prm["norm_ffn"][i:i + 1], name=f"ffn_norm_fwd_{i}")
    g, u, act = _hosted(plan, f"swiglu_fwd_{i}", _swiglu_fwd, h, prm["ffn_w_gate"][i], prm["ffn_w_up"][i],
                        name=f"swiglu_fwd_{i}")
    x2 = _hosted(plan, f"ffn_down_{i}", _matmul, act, prm["ffn_w_down"][i], mode="nn", addend=x1,
                 name=f"ffn_down_{i}")
    x3 = _ple_fwd(x2, p_i, prm["ple_w_gate"][i], prm["ple_w_proj"][i], name=f"ple_fwd_{i}")
    return x3, dict(x1=x1, h=h, g=g, u=u, act=act, x2=x2)


def _ffn_ple_bwd(dx3, p_i, prm, i, sv, grads, plan):
    ds, dple = _ple_bwd(sv["x2"], p_i, prm["ple_w_gate"][i], prm["ple_w_proj"][i], dx3, name=f"ple_bwd_{i}")
    grads["ple_w_gate"][i] = _matmul_tn(sv["x2"], ds, name=f"d_ple_w_gate_{i}")
    grads["ple_w_proj"][i] = _matmul_tn(dple, p_i, name=f"d_ple_w_proj_{i}")
    dx2 = _matmul(ds, prm["ple_w_gate"][i], mode="nt", addend=dx3, name=f"ple_dx_{i}")
    grads["ffn_w_down"][i] = _matmul_tn(sv["act"], dx2, name=f"d_ffn_w_down_{i}")
    dg, du = _hosted(plan, f"swiglu_bwd_{i}", _swiglu_bwd, dx2, prm["ffn_w_down"][i], sv["g"], sv["u"],
                     name=f"swiglu_bwd_{i}")
    grads["ffn_w_gate"][i] = _matmul_tn(dg, sv["h"], name=f"d_ffn_w_gate_{i}")
    grads["ffn_w_up"][i] = _matmul_tn(du, sv["h"], name=f"d_ffn_w_up_{i}")
    dh = _matmul(dg, prm["ffn_w_gate"][i], mode="nn", name=f"ffn_dh_gate_{i}")
    dh = _matmul(du, prm["ffn_w_up"][i], mode="nn", addend=dh, name=f"ffn_dh_up_{i}")
    dx1, dgain = _rmsnorm_bwd(sv["x1"], prm["norm_ffn"][i:i + 1], dh, dx2, name=f"ffn_norm_bwd_{i}")
    grads["norm_ffn"][i] = dgain[0]
    return dx1


def _mamba_fwd(x0, prm, plan):
    h = _rmsnorm_fwd(x0, prm["norm_mix"][0:1], name="mix_norm_fwd_0")
    z = _hosted(plan, "ssm_in_z", _matmul, h, prm["ssm_w_z"], mode="nt", name="ssm_in_z")
    xbc_pre = _hosted(plan, "ssm_in_xbc", _matmul, h, prm["ssm_w_xbc"], mode="nt", name="ssm_in_xbc")
    dt_raw = _matmul(h, prm["ssm_w_dt"], mode="nt", name="ssm_in_dt")
    xbc = _hosted(plan, "conv_fwd", _conv_fwd, xbc_pre, prm["ssm_conv_w"], prm["ssm_conv_b"])
    dt_bias = _pad_lanes(prm["ssm_dt_bias"])
    a_log = _pad_lanes(prm["ssm_a_log"])
    acs, dt_rep, acs_rep = _ssd_prep_fwd(dt_raw, dt_bias, a_log)
    acs_t = acs[:, :SSM_HEADS].T
    dskip_rep = _rep_heads(prm["ssm_d_skip"][0])
    y, hin_all = _hosted(plan, "ssd_fwd", _ssd_fwd, xbc, dt_rep, acs_rep, acs_t, dskip_rep)
    yn = _gate_norm_fwd(y, z, prm["ssm_norm_w"])
    x1 = _matmul(yn, prm["ssm_w_out"], mode="nn", addend=x0, name="ssm_out")
    sv = dict(x0=x0, h=h, z=z, xbc_pre=xbc_pre, dt_raw=dt_raw, xbc=xbc, dt_bias=dt_bias, dt_rep=dt_rep,
              acs_rep=acs_rep, acs_t=acs_t, dskip_rep=dskip_rep, y=y, hin_all=hin_all, yn=yn)
    return x1, sv


def _mamba_bwd(dx1, prm, sv, grads, plan):
    grads["ssm_w_out"] = _matmul_tn(sv["yn"], dx1, name="d_ssm_w_out")
    dyn = _matmul(dx1, prm["ssm_w_out"], mode="nt", name="ssm_out_dx")
    dy, dz, dnw = _hosted(plan, "gate_norm_bwd", _gate_norm_bwd, sv["y"], sv["z"], prm["ssm_norm_w"], dyn)
    grads["ssm_norm_w"] = dnw
    a_rep = _rep_heads(-jnp.exp(prm["ssm_a_log"][0]))
    dxbc, ddt, da_rep, dds_rep = _hosted(plan, "ssd_bwd", _ssd_bwd, sv["xbc"], sv["dt_rep"], sv["acs_rep"],
                                             sv["acs_t"], sv["dskip_rep"], a_rep, sv["hin_all"], dy)
    grads["ssm_d_skip"] = _per_head(dds_rep)[None, :]
    grads["ssm_a_log"] = (_per_head(da_rep) * _per_head(a_rep))[None, :]
    ddt_raw, dbias = _ssd_prep_bwd(sv["dt_raw"], sv["dt_bias"], ddt)
    grads["ssm_dt_bias"] = dbias[:, :SSM_HEADS]
    du, dcw, dcb = _hosted(plan, "conv_bwd", _conv_bwd, sv["xbc_pre"], prm["ssm_conv_w"], prm["ssm_conv_b"], dxbc)
    grads["ssm_conv_w"] = dcw
    grads["ssm_conv_b"] = dcb
    h = sv["h"]
    grads["ssm_w_in"] = jnp.concatenate(
        [_matmul_tn(dz, h, name="d_ssm_w_z"), _matmul_tn(du, h, name="d_ssm_w_xbc"),
         _matmul_tn(ddt_raw, h, name="d_ssm_w_dt")[:SSM_HEADS]], axis=0)
    dh = _hosted(plan, "ssm_dh_z", _matmul, dz, prm["ssm_w_z"], mode="nn", name="ssm_dh_z")
    dh = _hosted(plan, "ssm_dh_xbc", _matmul, du, prm["ssm_w_xbc"], mode="nn", addend=dh, name="ssm_dh_xbc")
    dh = _matmul(ddt_raw, prm["ssm_w_dt"], mode="nn", addend=dh, name="ssm_dh_dt")
    dx0, dgain = _rmsnorm_bwd(sv["x0"], prm["norm_mix"][0:1], dh, dx1, name="mix_norm_bwd_0")
    grads["norm_mix"][0] = dgain[0]
    return dx0


def _attn_mixer_fwd(x0, prm, plan):
    h = _rmsnorm_fwd(x0, prm["norm_mix"][1:2], name="mix_norm_fwd_1")
    n_heads = N_QKV_BLOCKS * ATT_HEADS
    gq = jnp.tile(prm["att_q_norm"], (1, n_heads))
    gk = jnp.tile(prm["att_k_norm"], (1, n_heads))
    qkv, qkn = _hosted(plan, "att_qkv", _matmul, h, prm["att_w_qkv"], mode="nt", name="att_qkv",
                       second=(_qk_normalised, [gq, gk]))
    outs, lses = [], []
    for g, (window, dil) in enumerate(DIL_PATTERNS):
        o_g, l_g = _attn_fwd(qkn, g, dil)
        outs.append(o_g)
        lses.append(l_g)
    o_b, o_f, l_rep, l_one = _attn_combine_fwd(outs, lses)
    x1 = _matmul(o_b, prm["att_w_o"], mode="nn", addend=x0, name="att_out")
    sv = dict(x0=x0, h=h, qkv=qkv, qkn=qkn, gq2=gq[:, :LANES], gk2=gk[:, :LANES], o_b=o_b, o_f=o_f, l_rep=l_rep,
              l_one=l_one)
    return x1, sv


def _attn_mixer_bwd(dx1, prm, sv, grads, plan):
    grads["att_w_o"] = _matmul_tn(sv["o_b"], dx1, name="d_att_w_o")
    do = _hosted(plan, "att_out_dx", _matmul, dx1, prm["att_w_o"], mode="nt", name="att_out_dx")
    dl_rep, dl_one = _attn_combine_bwd(do, sv["o_f"])
    blocks, dgq, dgk = [], [], []
    for g, (window, dil) in enumerate(DIL_PATTERNS):
        dq, dgq_g = _attn_bwd_dq(sv["qkv"], sv["qkn"], sv["gq2"], do, sv["l_rep"], dl_rep, g, dil)
        dk, dv, dgk_g = _attn_bwd_dkv(sv["qkv"], sv["qkn"], sv["gk2"], do, _rows_by_residue(sv["l_one"], dil),
                                      _rows_by_residue(dl_one, dil), g, dil)
        blocks += [dq, dk, dv]
        dgq.append(dgq_g)
        dgk.append(dgk_g)
    dqkv = jnp.concatenate(blocks, axis=1)

    def fold(parts):
        return jnp.stack(parts)[:, :, 0].reshape(-1, ATT_HEAD_DIM).sum(axis=0)[None, :]

    grads["att_q_norm"] = fold(dgq)
    grads["att_k_norm"] = fold(dgk)
    grads["att_w_qkv"] = _matmul_tn(dqkv, sv["h"], name="d_att_w_qkv")
    dh = _hosted(plan, "att_qkv_dx", _matmul, dqkv, prm["att_w_qkv"], mode="nn", name="att_qkv_dx")
    dx0, dgain = _rmsnorm_bwd(sv["x0"], prm["norm_mix"][1:2], dh, dx1, name="mix_norm_bwd_1")
    grads["norm_mix"][1] = dgain[0]
    return dx0


def _local_step(x, p, target, prm, plan=None):
    plan = plan or _NoOverlap()
    grads = {k: [None, None] for k in ("norm_mix", "norm_ffn", "ffn_w_gate", "ffn_w_up", "ffn_w_down",
                                       "ple_w_proj", "ple_w_gate")}
    plan.begin_backward(grads)
    x1, sv_m = _mamba_fwd(x, prm, plan)
    x3, sv_f0 = _ffn_ple_fwd(x1, p[0], prm, 0, plan)
    x4, sv_a = _attn_mixer_fwd(x3, prm, plan)
    x6, sv_f1 = _ffn_ple_fwd(x4, p[1], prm, 1, plan)
    dy, loss_row = _loss_head(x6, target)
    dx4 = _ffn_ple_bwd(dy, p[1], prm, 1, sv_f1, grads, plan)
    dx3 = _attn_mixer_bwd(dx4, prm, sv_a, grads, plan)
    dx1 = _ffn_ple_bwd(dx3, p[0], prm, 0, sv_f0, grads, plan)
    dx0 = _mamba_bwd(dx1, prm, sv_m, grads, plan)
    return loss_row, dx0, grads


W_IN_SLAB_ROWS = 1312


def _position():
    return lax.axis_index("x"), lax.axis_index("y"), lax.axis_index("c")


def _other_chips(x, y):
    return [(1 - x, y), (x, 1 - y), (1 - x, 1 - y)]


def _remote(send_sems, recv_sems, k, src, dst, to):
    return pltpu.make_async_remote_copy(src_ref=src, dst_ref=dst, send_sem=send_sems.at[k], recv_sem=recv_sems.at[k],
                                        device_id=to, device_id_type=MESH)


def _gather_side(entries, whole=()):
    n, nw = len(entries), len(whole)

    def first_hop(ins, outs, send_sems, recv_sems):
        x, y, c = _position()
        cps = []
        for j, chip in enumerate(_other_chips(x, y)):
            for e in range(n):
                cps.append(_remote(send_sems, recv_sems, 6 * e + j, ins[e].at[c], outs[e].at[2 * x + y, c], (*chip, c)))
            for e in range(nw):
                cps.append(_remote(send_sems, recv_sems, 6 * n + 3 * e + j, ins[n + e], outs[n + e].at[2 * x + y],
                                   (*chip, c)))
        return cps

    def start(ins, outs, send_sems, recv_sems):
        for cp in first_hop(ins, outs, send_sems, recv_sems):
            cp.start()

    def finish(ins, outs, send_sems, recv_sems):
        x, y, c = _position()
        me, sibling = (x, y, c), (x, y, 1 - c)
        chips = _other_chips(x, y)
        passed_on = []
        for j, (px, py) in enumerate(chips):
            for e in range(n):
                landed = outs[e].at[2 * px + py, c]
                _remote(send_sems, recv_sems, 6 * e + j, landed, landed, me).wait_recv()
                passed_on.append(_remote(send_sems, recv_sems, 6 * e + 3 + j, landed, landed, sibling))
                passed_on[-1].start()
            for e in range(nw):
                landed = outs[n + e].at[2 * px + py]
                _remote(send_sems, recv_sems, 6 * n + 3 * e + j, landed, landed, me).wait_recv()
        for j, (px, py) in enumerate(chips):
            for e in range(n):
                passed = outs[e].at[2 * px + py, 1 - c]
                _remote(send_sems, recv_sems, 6 * e + 3 + j, passed, passed, me).wait_recv()
        for cp in first_hop(ins, outs, send_sems, recv_sems) + passed_on:
            cp.wait_send()

    shapes = [jax.ShapeDtypeStruct((N_CHIPS,) + a.shape, a.dtype) for a in list(entries) + list(whole)]
    return _Side(list(entries) + list(whole), shapes, 6 * n + 3 * nw, start, finish)


def _run_side(side, name):
    si, so = len(side.inputs), len(side.out_shapes)

    def body(*refs):
        ins, outs, send_sems, recv_sems = refs[:si], refs[si:si + so], refs[-2], refs[-1]
        side.start(ins, outs, send_sems, recv_sems)
        side.finish(ins, outs, send_sems, recv_sems)

    side.outputs = list(pl.pallas_call(
        body, name=name, in_specs=[ANY] * si, out_specs=[ANY] * so, out_shape=side.out_shapes,
        scratch_shapes=[pltpu.SemaphoreType.DMA((side.n_sems,)), pltpu.SemaphoreType.DMA((side.n_sems,))],
    )(*side.inputs))
    return side.outputs


def _swap_side(grads):
    n = len(grads)

    def copies(ins, outs, send_sems, recv_sems):
        x, y, c = _position()
        return [_remote(send_sems, recv_sems, e, ins[e].at[:, 1 - c], outs[e], (x, y, 1 - c)) for e in range(n)]

    def start(ins, outs, send_sems, recv_sems):
        for cp in copies(ins, outs, send_sems, recv_sems):
            cp.start()

    def finish(ins, outs, send_sems, recv_sems):
        for cp in copies(ins, outs, send_sems, recv_sems):
            cp.wait()

    shapes = [jax.ShapeDtypeStruct((N_CHIPS,) + g.shape[2:], g.dtype) for g in grads]
    return _Side(grads, shapes, n, start, finish)


def _chip_exchange_side(chipsums):
    n = len(chipsums)

    def copies(ins, outs, send_sems, recv_sems):
        x, y, c = _position()
        return [_remote(send_sems, recv_sems, 3 * e + j, ins[e].at[2 * tx + ty], outs[e].at[j], (tx, ty, c))
                for j, (tx, ty) in enumerate(_other_chips(x, y)) for e in range(n)]

    def start(ins, outs, send_sems, recv_sems):
        for cp in copies(ins, outs, send_sems, recv_sems):
            cp.start()

    def finish(ins, outs, send_sems, recv_sems):
        for cp in copies(ins, outs, send_sems, recv_sems):
            cp.wait()

    shapes = [jax.ShapeDtypeStruct((3,) + cs.shape[1:], cs.dtype) for cs in chipsums]
    return _Side(chipsums, shapes, 3 * n, start, finish)


def _share_halves(totals):
    n = len(totals)

    def body(*refs):
        t_refs, r_refs = refs[:n], refs[n:2 * n]
        send_sems, recv_sems = refs[2 * n], refs[2 * n + 1]
        x, y, c = _position()
        cps = [pltpu.make_async_remote_copy(src_ref=t_refs[e], dst_ref=r_refs[e], send_sem=send_sems.at[e],
                                            recv_sem=recv_sems.at[e], device_id=(x, y, 1 - c), device_id_type=MESH)
               for e in range(n)]
        for cp in cps:
            cp.start()
        for cp in cps:
            cp.wait()

    return pl.pallas_call(
        body, name="grad_share_halves", in_specs=[ANY] * n, out_specs=[ANY] * n,
        out_shape=[jax.ShapeDtypeStruct(t.shape, t.dtype) for t in totals],
        scratch_shapes=[pltpu.SemaphoreType.DMA((n,)), pltpu.SemaphoreType.DMA((n,))],
    )(*totals)


def _reduce_rows(h):
    return h if h <= 704 else h // 2


def _add_sibling(grad, recv, c_idx, *, name):
    _, _, h, cw = grad.shape
    th = _reduce_rows(h)

    def body(c_ref, g_ref, r_ref, o_ref):
        o_ref[...] = (g_ref[...] + r_ref[...]).astype(BF16)

    return pl.pallas_call(
        body, name=name,
        grid_spec=pltpu.PrefetchScalarGridSpec(
            num_scalar_prefetch=1, grid=(N_CHIPS, h // th),
            in_specs=[pl.BlockSpec((None, None, th, cw), lambda s, i, c_ref: (s, c_ref[0], i, 0)),
                      pl.BlockSpec((None, th, cw), lambda s, i, c_ref: (s, i, 0))],
            out_specs=pl.BlockSpec((None, th, cw), lambda s, i, c_ref: (s, i, 0))),
        out_shape=jax.ShapeDtypeStruct((N_CHIPS, h, cw), BF16),
        compiler_params=_params("parallel", "parallel"),
    )(c_idx, grad, recv)


def _add_chips(chipsum, recv, s_idx, *, name):
    _, h, cw = chipsum.shape
    th = _reduce_rows(h)

    def body(s_ref, own_ref, r_ref, o_ref):
        o_ref[...] = ((own_ref[...].astype(F32) + r_ref[0].astype(F32)) + r_ref[1].astype(F32)) + r_ref[2].astype(F32)

    return pl.pallas_call(
        body, name=name,
        grid_spec=pltpu.PrefetchScalarGridSpec(
            num_scalar_prefetch=1, grid=(h // th,),
            in_specs=[pl.BlockSpec((None, th, cw), lambda i, s_ref: (s_ref[0], i, 0)),
                      pl.BlockSpec((3, th, cw), lambda i, s_ref: (0, i, 0))],
            out_specs=pl.BlockSpec((th, cw), lambda i, s_ref: (i, 0))),
        out_shape=jax.ShapeDtypeStruct((h, cw), F32),
        compiler_params=_params("parallel"),
    )(s_idx, chipsum, recv)


def _adamw_math(w, g, m, v):
    m = ADAM_B1 * m + (1.0 - ADAM_B1) * g
    v = ADAM_B2 * v + (1.0 - ADAM_B2) * (g * g)
    m_hat = m / (1.0 - ADAM_B1 ** ADAM_STEP)
    v_hat = v / (1.0 - ADAM_B2 ** ADAM_STEP)
    delta = -ADAM_LR * (m_hat / (jnp.sqrt(v_hat) + ADAM_EPS) + ADAM_WD * w)
    return delta, m, v


ADAM_TILE_ELEMS = 256 * 1024


def _adamw(w, g, m, v, *, name):
    layers, rows, cols = w.shape
    tr = rows
    for cand in range(8, rows, 8):
        if rows % cand == 0 and cand * cols <= ADAM_TILE_ELEMS:
            tr = cand
    if rows * cols <= ADAM_TILE_ELEMS:
        tr = rows

    def body(w_ref, g_ref, m_ref, v_ref, d_ref, nm_ref, nv_ref):
        d, nm, nv = _adamw_math(w_ref[...], g_ref[...], m_ref[...], v_ref[...])
        d_ref[...] = d
        nm_ref[...] = nm
        nv_ref[...] = nv

    blk = pl.BlockSpec((None, tr, cols), lambda l, i: (l, i, 0))
    sds = jax.ShapeDtypeStruct(w.shape, F32)
    return pl.pallas_call(
        body, name=name, grid=(layers, rows // tr), in_specs=[blk] * 4, out_specs=[blk] * 3, out_shape=[sds] * 3,
        compiler_params=_params("parallel", "parallel"),
    )(w, g, m, v)


SMALL_LAYOUT = (("loss", 1), ("norm_mix", 16), ("norm_ffn", 16), ("ssm_conv_b", 24), ("ssm_dt_bias", 1),
                ("ssm_a_log", 1), ("ssm_d_skip", 1), ("ssm_norm_w", 16), ("att_q_norm", 1), ("att_k_norm", 1),
                ("conv_w_full", 96))
SMALL_ROWS = 176
N_DEVICES = 8


def _small_pack(values):
    parts = []
    for name, rows in SMALL_LAYOUT:
        flat = values[name].reshape(-1).astype(F32)
        parts.append(jnp.pad(flat, (0, rows * LANES - flat.shape[0])).reshape(rows, LANES))
    used = sum(r for _, r in SMALL_LAYOUT)
    parts.append(jnp.zeros((SMALL_ROWS - used, LANES), F32))
    return jnp.concatenate(parts, axis=0)


def _small_unpack(pack, shapes):
    out, off = {}, 0
    for name, rows in SMALL_LAYOUT:
        shape = shapes[name]
        n = math.prod(shape)
        out[name] = pack[off:off + rows].reshape(-1)[:n].reshape(shape)
        off += rows
    return out


def _small_allreduce_adamw(g, w, m, v):
    def body(g_ref, w_ref, m_ref, v_ref, gs_ref, d_ref, nm_ref, nv_ref, buf, send_sems, recv_sems):
        x, y, c = _position()
        pos = (x, y, c)
        me = 4 * x + 2 * y + c
        buf[me] = g_ref[...]
        peers = []
        for k in range(1, N_DEVICES):
            bits = ((k >> 2) & 1, (k >> 1) & 1, k & 1)
            peers.append(tuple(1 - p if b else p for p, b in zip(pos, bits)))
        cps = [pltpu.make_async_remote_copy(src_ref=g_ref, dst_ref=buf.at[me], send_sem=send_sems.at[k],
                                            recv_sem=recv_sems.at[k], device_id=peer, device_id_type=MESH)
               for k, peer in enumerate(peers)]
        for cp in cps:
            cp.start()
        for k, (px, py, pc) in enumerate(peers):
            pltpu.make_async_remote_copy(src_ref=g_ref, dst_ref=buf.at[4 * px + 2 * py + pc],
                                         send_sem=send_sems.at[k], recv_sem=recv_sems.at[k],
                                         device_id=(px, py, pc), device_id_type=MESH).wait_recv()
        for cp in cps:
            cp.wait_send()
        total = buf[0]
        for dev in range(1, N_DEVICES):
            total = total + buf[dev]
        gs_ref[...] = total
        d, nm, nv = _adamw_math(w_ref[...], total, m_ref[...], v_ref[...])
        d_ref[...] = d
        nm_ref[...] = nm
        nv_ref[...] = nv

    vm = pl.BlockSpec(memory_space=pltpu.VMEM)
    sds = jax.ShapeDtypeStruct((SMALL_ROWS, LANES), F32)
    return pl.pallas_call(
        body, name="small_allreduce_adamw", in_specs=[vm] * 4, out_specs=[vm] * 4, out_shape=[sds] * 4,
        scratch_shapes=[pltpu.VMEM((N_DEVICES, SMALL_ROWS, LANES), F32),
                        pltpu.SemaphoreType.DMA((N_DEVICES - 1,)), pltpu.SemaphoreType.DMA((N_DEVICES - 1,))],
    )(g, w, m, v)


SMALL = tuple(n for n, _ in SMALL_LAYOUT if n not in ("loss", "conv_w_full"))
WEIGHTS = ("norm_mix", "norm_ffn", "ssm_w_in", "ssm_conv_w", "ssm_conv_b", "ssm_dt_bias", "ssm_a_log", "ssm_d_skip",
           "ssm_norm_w", "ssm_w_out", "att_w_qkv", "att_q_norm", "att_k_norm", "att_w_o", "ffn_w_gate", "ffn_w_up",
           "ffn_w_down", "ple_w_proj", "ple_w_gate")
COLUMN_SHARDED = ("ssm_w_in", "att_w_qkv", "ffn_w_gate", "ffn_w_up", "ple_w_proj")
LAYERED = ("ffn_w_gate", "ffn_w_up", "ffn_w_down", "ple_w_proj", "ple_w_gate")
UPDATED_TRANSPOSED = ("ssm_w_in", "ffn_w_gate", "ffn_w_up")
GATHER_ORDER = ("ssm_w_in", "ssm_w_out", "att_w_qkv", "att_w_o", "ffn_w_gate", "ffn_w_up", "ffn_w_down",
                "ple_w_proj", "ple_w_gate")


def _layers(n):
    return (0, 1) if n in LAYERED else (None,)


def _tag(key):
    return key[0] if key[1] is None else f"{key[0]}_{key[1]}"


QKV_PARTS = 3


def _weight_slab(w, key):
    n, i = key
    if n == "att_w_qkv":
        a = w[n][0].T
        rows = a.shape[0] // QKV_PARTS
        a = a[i * rows:(i + 1) * rows]
    else:
        a = w[n][0 if i is None else i]
        a = a.T if n in COLUMN_SHARDED else a
    if n == "ssm_w_in":
        a = jnp.pad(a, ((0, W_IN_SLAB_ROWS - a.shape[0]), (0, 0)))
    return a.reshape(2, a.shape[0] // 2, a.shape[1]).astype(BF16)


def _install(prm, key, gathered, own, s_me):
    n, i = key
    full = lax.dynamic_update_slice(gathered, own[None], (s_me, 0, 0, 0))
    full = full.reshape(N_CHIPS, 2 * full.shape[2], full.shape[3])
    if n == "att_w_qkv":
        parts = prm.setdefault("att_w_qkv_parts", {})
        parts[i] = full
        if len(parts) == QKV_PARTS:
            prm[n] = jnp.stack([parts[j] for j in range(QKV_PARTS)], axis=1).reshape(-1, D_MODEL)
        return
    if n == "ssm_w_in":
        rows = (D_INNER + CONV_DIM + SSM_HEADS) // N_CHIPS
        w_in_t = full[:, :rows].reshape(N_CHIPS * rows, D_MODEL)
        prm["ssm_w_z"] = w_in_t[:D_INNER]
        prm["ssm_w_xbc"] = w_in_t[D_INNER:D_INNER + CONV_DIM]
        prm["ssm_w_dt"] = jnp.pad(w_in_t[D_INNER + CONV_DIM:], ((0, LANES - SSM_HEADS), (0, 0)))
        return
    full = full.reshape(N_CHIPS * full.shape[1], full.shape[2])
    if i is None:
        prm[n] = full
    else:
        prm.setdefault(n, [None, None])[i] = full


def _grad_slab(grads, key):
    n, i = key
    g = grads[n] if i is None else grads[n][i]
    if n == "ssm_w_in":
        g = jnp.pad(g.reshape(N_CHIPS, g.shape[0] // N_CHIPS, D_MODEL),
                    ((0, 0), (0, W_IN_SLAB_ROWS - g.shape[0] // N_CHIPS), (0, 0)))
    rows = g.size // (N_CHIPS * g.shape[-1])
    return g.reshape(N_CHIPS, 2, rows // 2, g.shape[-1])


def _natural_shard(n, reduced, shape):
    def one(r):
        if n == "ssm_w_in":
            r = r[:shape[-1]]
        return r.T if n in COLUMN_SHARDED else r
    if n in LAYERED:
        return jnp.stack([one(r) for r in reduced]).reshape(shape)
    return one(reduced[0]).reshape(shape)


def kernel(x, p, norm_mix, norm_ffn, ssm_w_in, ssm_conv_w, ssm_conv_b, ssm_dt_bias, ssm_a_log, ssm_d_skip, ssm_norm_w, ssm_w_out, att_w_qkv, att_q_norm, att_k_norm, att_w_o, ffn_w_gate, ffn_w_up, ffn_w_down, ple_w_proj, ple_w_gate, loss_target, m_norm_mix, m_norm_ffn, m_ssm_w_in, m_ssm_conv_w, m_ssm_conv_b, m_ssm_dt_bias, m_ssm_a_log, m_ssm_d_skip, m_ssm_norm_w, m_ssm_w_out, m_att_w_qkv, m_att_q_norm, m_att_k_norm, m_att_w_o, m_ffn_w_gate, m_ffn_w_up, m_ffn_w_down, m_ple_w_proj, m_ple_w_gate, v_norm_mix, v_norm_ffn, v_ssm_w_in, v_ssm_conv_w, v_ssm_conv_b, v_ssm_dt_bias, v_ssm_a_log, v_ssm_d_skip, v_ssm_norm_w, v_ssm_w_out, v_att_w_qkv, v_att_q_norm, v_att_k_norm, v_att_w_o, v_ffn_w_gate, v_ffn_w_up, v_ffn_w_down, v_ple_w_proj, v_ple_w_gate):
    given = dict(locals())
    w = {n: given[n] for n in WEIGHTS}
    m = {n: given["m_" + n] for n in WEIGHTS}
    v = {n: given["v_" + n] for n in WEIGHTS}
    c_idx = lax.axis_index("c").astype(jnp.int32).reshape(1)
    s_idx = (2 * lax.axis_index("x") + lax.axis_index("y")).astype(jnp.int32).reshape(1)

    s_me = 2 * lax.axis_index("x") + lax.axis_index("y")
    first_core = lax.axis_index("c") == 0

    qkv_parts = [("att_w_qkv", j) for j in range(QKV_PARTS)]
    gather_plan = {
        "ssm_in_z": [("ssm_w_out", None)],
        "ssm_in_xbc": [("ffn_w_gate", 0)],
        "conv_fwd": [("ffn_w_up", 0)],
        "ssd_fwd": [("ffn_w_down", 0), ("ple_w_proj", 0), ("ple_w_gate", 0), ("att_w_o", None)],
        "swiglu_fwd_0": qkv_parts[:2],
        "ffn_down_0": qkv_parts[2:],
        "att_qkv": [(n, 1) for n in LAYERED],
    }
    mamba = [("ssm_w_in", None)]
    own = {k: _weight_slab(w, k) for k in mamba + sum(gather_plan.values(), [])}
    prm = {n: w[n] for n in SMALL}

    def land(group, outputs):
        for k, g in zip(group, outputs):
            _install(prm, k, g, own[k], s_me)

    first = _gather_side([own[k] for k in mamba], whole=[ssm_conv_w[0]])
    _run_side(first, "gather_mamba")
    land(mamba, first.outputs)
    conv = lax.dynamic_update_slice(first.outputs[-1], ssm_conv_w, (s_me, 0, 0))
    prm["ssm_conv_w"] = conv.transpose(1, 0, 2).reshape(CONV_WIDTH, CONV_DIM)

    ffn1 = [(n, 1) for n in LAYERED]
    attention = [("att_w_qkv", None), ("att_w_o", None)]
    ffn0 = [(n, 0) for n in LAYERED] + [("ssm_w_out", None)]
    reduce_plan = {"att_out_dx": ("swap", ffn1), "att_qkv_dx": ("exchange", ffn1),
                   "swiglu_bwd_0": ("swap", attention), "ssd_bwd": ("exchange", attention),
                   "gate_norm_bwd": ("swap", ffn0), "conv_bwd": ("exchange", ffn0),
                   "ssm_dh_z": ("swap", mamba), "ssm_dh_xbc": ("exchange", mamba)}
    state = {}

    def swap_side(group):
        state[_tag(group[0]), "g4"] = g4 = [_grad_slab(state["grads"], k) for k in group]
        return _swap_side(g4)

    def add_siblings(group, from_sibling):
        state[_tag(group[0]), "chipsums"] = [
            _add_sibling(g, r, c_idx, name="add_sibling_" + _tag(k))
            for g, r, k in zip(state[_tag(group[0]), "g4"], from_sibling, group)]

    def exchange_side(group):
        return _chip_exchange_side(state[_tag(group[0]), "chipsums"])

    def add_chips(group, from_chips):
        for k, cs, r in zip(group, state[_tag(group[0]), "chipsums"], from_chips):
            state["total", k] = _add_chips(cs, r, s_idx, name="add_chips_" + _tag(k))

    class Plan(_NoOverlap):
        def __init__(self):
            self.carried = {host: _gather_side([own[k] for k in group]) for host, group in gather_plan.items()}

        def begin_backward(self, grads):
            state["grads"] = grads

        def side(self, host):
            if host in reduce_plan:
                step, group = reduce_plan[host]
                self.carried[host] = swap_side(group) if step == "swap" else exchange_side(group)
            return self.carried.get(host)

        def after(self, host):
            if host in gather_plan:
                land(gather_plan[host], self.carried[host].outputs)
            elif host in reduce_plan:
                step, group = reduce_plan[host]
                (add_siblings if step == "swap" else add_chips)(group, self.carried[host].outputs)

    loss_row, dx, grads = _local_step(x[0], p[:, 0], loss_target[0], prm, Plan())

    order = mamba + ffn0 + attention + ffn1
    shared = _share_halves([state["total", k] for k in order])
    reduced = {}
    for k, theirs in zip(order, shared):
        lo = jnp.where(first_core, state["total", k], theirs)
        hi = jnp.where(first_core, theirs, state["total", k])
        reduced.setdefault(k[0], {})[k[1]] = jnp.concatenate([lo, hi], axis=0)
    reduced = {n: [by_layer[i] for i in _layers(n)] for n, by_layer in reduced.items()}

    grad, delta, new_m, new_v = {}, {}, {}, {}
    for n in GATHER_ORDER:
        if n in UPDATED_TRANSPOSED:
            flip = lambda a: a.transpose(0, 2, 1)
            cols = w[n].shape[-1]
            g_t = jnp.stack([r[:cols] for r in reduced[n]])
            grad[n] = flip(g_t)
            delta[n], new_m[n], new_v[n] = [flip(o) for o in _adamw(flip(w[n]), g_t, flip(m[n]), flip(v[n]),
                                                                    name="adamw_" + n)]
            continue
        grad[n] = _natural_shard(n, reduced[n], w[n].shape)
        delta[n], new_m[n], new_v[n] = _adamw(w[n], grad[n], m[n], v[n], name="adamw_" + n)

    small_g = {n: (jnp.stack(grads[n]) if isinstance(grads[n], list) else grads[n]) for n in SMALL}
    small_g["loss"] = loss_row
    small_g["conv_w_full"] = grads["ssm_conv_w"]
    zero = {"loss": jnp.zeros((1, LANES), F32), "conv_w_full": jnp.zeros((CONV_WIDTH, CONV_DIM), F32)}
    outs = _small_allreduce_adamw(_small_pack(small_g), _small_pack({**w, **zero}), _small_pack({**m, **zero}),
                                  _small_pack({**v, **zero}))
    shapes = {n: w[n].shape for n in SMALL}
    shapes["loss"] = (1, LANES)
    shapes["conv_w_full"] = (CONV_WIDTH, CONV_DIM)
    sg, sd, sm, sv = [_small_unpack(o, shapes) for o in outs]
    for n in SMALL:
        grad[n], delta[n], new_m[n], new_v[n] = sg[n], sd[n], sm[n], sv[n]
    loss = sg["loss"][0, 0]
    conv_cols = CONV_DIM // N_CHIPS
    grad["ssm_conv_w"] = lax.dynamic_slice(sg["conv_w_full"], (0, s_me * conv_cols), (CONV_WIDTH, conv_cols))[None]
    delta["ssm_conv_w"], new_m["ssm_conv_w"], new_v["ssm_conv_w"] = _adamw(
        ssm_conv_w, grad["ssm_conv_w"], m_ssm_conv_w, v_ssm_conv_w, name="adamw_ssm_conv_w")

    return (loss, dx[None], *[grad[n] for n in WEIGHTS], *[delta[n] for n in WEIGHTS],
            *[new_m[n] for n in WEIGHTS], *[new_v[n] for n in WEIGHTS])
```

```python
import functools
import math

import jax
import jax.numpy as jnp
from jax import lax
from jax.experimental import pallas as pl
from jax.experimental.pallas import tpu as pltpu

F32 = jnp.float32
BF16 = jnp.bfloat16
HIGHEST = lax.Precision.HIGHEST

NORM_EPS = 1e-6
ADAM_LR, ADAM_B1, ADAM_B2, ADAM_EPS, ADAM_WD, ADAM_STEP = 0.001, 0.9, 0.999, 1e-08, 0.01, 10

D_MODEL = 1024
D_INNER = 2048
SSM_HEADS = 32
SSM_HEAD_DIM = 64
SSM_GROUPS = 4
SSM_STATE = 128
SSD_CHUNK = 128
CONV_DIM = 3072
CONV_WIDTH = 4
ATT_HEADS = 16
ATT_HEAD_DIM = 64
DIL_PATTERNS = ((128, 1), (512, 4), (2048, 16))
ATT_BLOCK = 128
FFN_HIDDEN = 2816
PLE_DIM = 256

LANES = 128
V7X_VMEM_LIMIT = 56 * 1024 * 1024
NEG_BIG = -1e30

N_CHIPS = 4


def _params(*sem):
    return pltpu.CompilerParams(dimension_semantics=sem, vmem_limit_bytes=V7X_VMEM_LIMIT)


def _tile(n, pref):
    if n <= pref:
        return n
    best = None
    for t in range(LANES, pref + 1, LANES):
        if n % t == 0:
            best = t
    assert best is not None, (n, pref)
    return best


def _sigmoid(v):
    return 1.0 / (1.0 + jnp.exp(-v))


def _dot(a, b):
    return jnp.dot(a, b, preferred_element_type=F32)


def _dot_nt(a, b):
    return lax.dot_general(a, b, (((1,), (1,)), ((), ())), preferred_element_type=F32)


def _dot_tn(a, b):
    return lax.dot_general(a, b, (((0,), (0,)), ((), ())), preferred_element_type=F32)


def _head_block_diag():
    i = lax.broadcasted_iota(jnp.int32, (LANES, LANES), 0) // ATT_HEAD_DIM
    j = lax.broadcasted_iota(jnp.int32, (LANES, LANES), 1) // ATT_HEAD_DIM
    return (i == j).astype(BF16)


def _split_dot(ones, z):
    hi = z.astype(BF16)
    lo = (z - hi.astype(F32)).astype(BF16)
    return _dot(ones, hi) + _dot(ones, lo)


def _head_sums(z, bd, terms=2):
    hi = z.astype(BF16)
    lo = (z - hi.astype(F32)).astype(BF16) if terms == 2 else None
    parts = []
    for t in range(z.shape[1] // LANES):
        sl = slice(t * LANES, (t + 1) * LANES)
        part = _dot(hi[:, sl], bd)
        parts.append(part + _dot(lo[:, sl], bd) if terms == 2 else part)
    return parts[0] if len(parts) == 1 else jnp.concatenate(parts, axis=1)


def _lane_lt64(rows):
    return lax.broadcasted_iota(jnp.int32, (rows, LANES), 1) < ATT_HEAD_DIM


MESH = pl.DeviceIdType.MESH
ANY = pl.BlockSpec(memory_space=pl.ANY)


class _Side:
    def __init__(self, inputs, out_shapes, n_sems, start, finish):
        self.inputs, self.out_shapes, self.n_sems = list(inputs), list(out_shapes), n_sems
        self.start, self.finish = start, finish
        self.outputs = None


class _SemaphoresFrom:
    def __init__(self, sems, first):
        self.sems, self.first = sems, first

    @property
    def at(self):
        return self

    def __getitem__(self, k):
        return self.sems.at[self.first + k]


def _sides_together(sides):
    def run(step):
        def both(ins, outs, send_sems, recv_sems):
            i = o = k = 0
            for s in sides:
                ni, no = len(s.inputs), len(s.out_shapes)
                getattr(s, step)(ins[i:i + ni], outs[o:o + no], _SemaphoresFrom(send_sems, k),
                                 _SemaphoresFrom(recv_sems, k))
                i, o, k = i + ni, o + no, k + s.n_sems
        return both

    return _Side(sum([s.inputs for s in sides], []), sum([s.out_shapes for s in sides], []),
                 sum(s.n_sems for s in sides), run("start"), run("finish"))


def _share_out(together, sides):
    o = 0
    for s in sides:
        s.outputs = together.outputs[o:o + len(s.out_shapes)]
        o += len(s.out_shapes)


def _call(body, side, *, name, grid, in_specs, out_specs, out_shape, scratch_shapes, semantics, args):
    in_specs, out_specs, out_shape = list(in_specs), list(out_specs), list(out_shape)
    scratch_shapes = list(scratch_shapes)
    if side is None:
        return pl.pallas_call(body, name=name, grid=grid, in_specs=in_specs, out_specs=out_specs,
                              out_shape=out_shape, scratch_shapes=scratch_shapes,
                              compiler_params=_params(*semantics))(*args)
    ni, no, ns = len(in_specs), len(out_specs), len(scratch_shapes)
    si, so = len(side.inputs), len(side.out_shapes)

    def hosted(*refs):
        ins, s_ins = refs[:ni], refs[ni:ni + si]
        outs, s_outs = refs[ni + si:ni + si + no], refs[ni + si + no:ni + si + no + so]
        scratch = refs[ni + si + no + so:ni + si + no + so + ns]
        send_sems, recv_sems = refs[-2], refs[-1]
        first = pl.program_id(0) == 0
        last = pl.program_id(0) == grid[0] - 1
        for axis in range(1, len(grid)):
            first = jnp.logical_and(first, pl.program_id(axis) == 0)
            last = jnp.logical_and(last, pl.program_id(axis) == grid[axis] - 1)

        @pl.when(first)
        def _():
            side.start(s_ins, s_outs, send_sems, recv_sems)

        body(*ins, *outs, *scratch)

        @pl.when(last)
        def _():
            side.finish(s_ins, s_outs, send_sems, recv_sems)

    res = pl.pallas_call(
        hosted, name=name, grid=grid, in_specs=in_specs + [ANY] * si, out_specs=out_specs + [ANY] * so,
        out_shape=out_shape + side.out_shapes,
        scratch_shapes=scratch_shapes + [pltpu.SemaphoreType.DMA((side.n_sems,)),
                                         pltpu.SemaphoreType.DMA((side.n_sems,))],
        compiler_params=_params(*["arbitrary"] * len(grid)),
    )(*args, *side.inputs)
    side.outputs = list(res[no:])
    return list(res[:no])


def _matmul(a, b, *, mode, name, out_dtype=F32, addend=None, tm=1024, tn=512, tk_max=3072, side=None, second=None):
    m, k = a.shape
    if mode == "nn":
        k2, n = b.shape
    else:
        n, k2 = b.shape
    assert k == k2, (a.shape, b.shape, mode)
    tm, tn, tk = _tile(m, tm), _tile(n, tn), _tile(k, tk_max)
    nk = k // tk
    has_add = addend is not None
    n_rows = len(second[1]) if second else 0
    n_out = 2 if second else 1

    def body(*refs):
        a_ref, b_ref = refs[0], refs[1]
        add_ref = refs[2] if has_add else None
        row_refs = refs[2 + has_add:2 + has_add + n_rows]
        o_ref, acc_ref = refs[-1 - n_out], refs[-1]
        kk = pl.program_id(2)
        col_tile = pl.program_id(1)
        av = a_ref[...].astype(BF16)
        bv = b_ref[...].astype(BF16)
        part = _dot(av, bv) if mode == "nn" else _dot_nt(av, bv)

        @pl.when(kk == 0)
        def _():
            acc_ref[...] = part

        @pl.when(kk > 0)
        def _():
            acc_ref[...] += part

        @pl.when(kk == nk - 1)
        def _():
            res = acc_ref[...]
            if has_add:
                res = res + add_ref[...]
            o_ref[...] = res.astype(out_dtype)
            if second:
                refs[-2][...] = second[0](res, col_tile, *row_refs)

    a_spec = pl.BlockSpec((tm, tk), lambda i, j, kk: (i, kk))
    if mode == "nn":
        b_spec = pl.BlockSpec((tk, tn), lambda i, j, kk: (kk, j))
    else:
        b_spec = pl.BlockSpec((tn, tk), lambda i, j, kk: (j, kk))
    tile = pl.BlockSpec((tm, tn), lambda i, j, kk: (i, j))
    in_specs = [a_spec, b_spec]
    args = [a, b]
    if has_add:
        in_specs.append(tile)
        args.append(addend)
    if second:
        in_specs += [pl.BlockSpec((1, tn), lambda i, j, kk: (0, j))] * n_rows
        args += list(second[1])
    outs = _call(
        body, side, name=name, grid=(m // tm, n // tn, nk),
        in_specs=in_specs, out_specs=[tile] * n_out,
        out_shape=[jax.ShapeDtypeStruct((m, n), out_dtype)] + [jax.ShapeDtypeStruct((m, n), F32)] * (n_out - 1),
        scratch_shapes=[pltpu.VMEM((tm, tn), F32)],
        semantics=("parallel", "parallel", "arbitrary"), args=args,
    )
    return outs if second else outs[0]


def _matmul_tn(a, b, *, name, tm=1408, tn=512, tk=1024):
    t, m = a.shape
    t2, n = b.shape
    assert t == t2
    tm, tn, tk = _tile(m, tm), _tile(n, tn), _tile(t, tk)

    def body(a_ref, b_ref, o_ref):
        part = _dot_tn(a_ref[...].astype(BF16), b_ref[...].astype(BF16))

        @pl.when(pl.program_id(2) == 0)
        def _():
            o_ref[...] = part

        @pl.when(pl.program_id(2) > 0)
        def _():
            o_ref[...] += part

    return pl.pallas_call(
        body, name=name, grid=(m // tm, n // tn, t // tk),
        in_specs=[pl.BlockSpec((tk, tm), lambda i, j, kk: (kk, i)),
                  pl.BlockSpec((tk, tn), lambda i, j, kk: (kk, j))],
        out_specs=pl.BlockSpec((tm, tn), lambda i, j, kk: (i, j)),
        out_shape=jax.ShapeDtypeStruct((m, n), F32),
        compiler_params=_params("parallel", "parallel", "arbitrary"),
    )(a, b)


def _rmsnorm_fwd(x, gain, *, name):
    t, d = x.shape
    tm = _tile(t, 512)

    def body(x_ref, g_ref, o_ref):
        xv = x_ref[...]
        r = lax.rsqrt(jnp.mean(xv * xv, axis=-1, keepdims=True) + NORM_EPS)
        o_ref[...] = (xv * r * g_ref[...]).astype(BF16)

    return pl.pallas_call(
        body, name=name, grid=(t // tm,),
        in_specs=[pl.BlockSpec((tm, d), lambda i: (i, 0)), pl.BlockSpec((1, d), lambda i: (0, 0))],
        out_specs=pl.BlockSpec((tm, d), lambda i: (i, 0)),
        out_shape=jax.ShapeDtypeStruct((t, d), BF16),
        compiler_params=_params("parallel"),
    )(x, gain)


def _rmsnorm_bwd(x, gain, dy, dres, *, name):
    t, d = x.shape
    tm = _tile(t, 512)

    def body(x_ref, g_ref, dy_ref, dres_ref, dx_ref, dg_ref):
        xv = x_ref[...]
        r = lax.rsqrt(jnp.mean(xv * xv, axis=-1, keepdims=True) + NORM_EPS)
        xh = xv * r
        dyv = dy_ref[...]
        dxh = dyv * g_ref[...]
        mean = jnp.mean(dxh * xh, axis=-1, keepdims=True)
        dx_ref[...] = dres_ref[...] + r * (dxh - xh * mean)
        part = jnp.sum(dyv * xh, axis=0, keepdims=True)

        @pl.when(pl.program_id(0) == 0)
        def _():
            dg_ref[...] = part

        @pl.when(pl.program_id(0) > 0)
        def _():
            dg_ref[...] += part

    row = pl.BlockSpec((tm, d), lambda i: (i, 0))
    vec = pl.BlockSpec((1, d), lambda i: (0, 0))
    return pl.pallas_call(
        body, name=name, grid=(t // tm,),
        in_specs=[row, vec, row, row], out_specs=[row, vec],
        out_shape=[jax.ShapeDtypeStruct((t, d), F32), jax.ShapeDtypeStruct((1, d), F32)],
        compiler_params=_params("arbitrary"),
    )(x, gain, dy, dres)


def _loss_head(y, target):
    t, d = y.shape
    tm = _tile(t, 512)
    steps = t // tm

    def body(y_ref, t_ref, dy_ref, l_ref, acc_ref):
        e = y_ref[...] - t_ref[...]
        dy_ref[...] = e * (1.0 / d)
        part = jnp.sum(e * e, axis=0, keepdims=True)

        @pl.when(pl.program_id(0) == 0)
        def _():
            acc_ref[...] = part

        @pl.when(pl.program_id(0) > 0)
        def _():
            acc_ref[...] += part

        @pl.when(pl.program_id(0) == steps - 1)
        def _():
            l_ref[...] = jnp.full((1, LANES), (0.5 / d), F32) * jnp.sum(acc_ref[...])

    row = pl.BlockSpec((tm, d), lambda i: (i, 0))
    return pl.pallas_call(
        body, name="loss_head", grid=(steps,),
        in_specs=[row, row], out_specs=[row, pl.BlockSpec((1, LANES), lambda i: (0, 0))],
        out_shape=[jax.ShapeDtypeStruct((t, d), F32), jax.ShapeDtypeStruct((1, LANES), F32)],
        scratch_shapes=[pltpu.VMEM((1, d), F32)],
        compiler_params=_params("arbitrary"),
    )(y, target)


def _swiglu_fwd(h, w_gate_t, w_up_t, *, name, side=None):
    t, d = h.shape
    f = w_gate_t.shape[0]
    tm, tn = _tile(t, 1024), _tile(f, 256)

    def body(h_ref, wg_ref, wu_ref, g_ref, u_ref, a_ref):
        hv = h_ref[...]
        g = _dot_nt(hv, wg_ref[...])
        u = _dot_nt(hv, wu_ref[...])
        g_ref[...] = g.astype(BF16)
        u_ref[...] = u.astype(BF16)
        a_ref[...] = (g * _sigmoid(g) * u).astype(BF16)

    wspec = pl.BlockSpec((tn, d), lambda i, j: (j, 0))
    ospec = pl.BlockSpec((tm, tn), lambda i, j: (i, j))
    return _call(
        body, side, name=name, grid=(t // tm, f // tn),
        in_specs=[pl.BlockSpec((tm, d), lambda i, j: (i, 0)), wspec, wspec],
        out_specs=[ospec, ospec, ospec],
        out_shape=[jax.ShapeDtypeStruct((t, f), BF16), jax.ShapeDtypeStruct((t, f), BF16),
                   jax.ShapeDtypeStruct((t, f), BF16)],
        scratch_shapes=[], semantics=("parallel", "parallel"), args=(h, w_gate_t, w_up_t),
    )


def _swiglu_bwd(dx, w_down, g, u, *, name, side=None):
    t, d = dx.shape
    f = w_down.shape[0]
    tm, tn = _tile(t, 1024), _tile(f, 256)

    def body(dx_ref, wd_ref, g_ref, u_ref, dg_ref, du_ref):
        dact = _dot_nt(dx_ref[...].astype(BF16), wd_ref[...])
        gv, uv = g_ref[...].astype(F32), u_ref[...].astype(F32)
        sg = _sigmoid(gv)
        dg_ref[...] = (dact * uv * sg * (1.0 + gv * (1.0 - sg))).astype(BF16)
        du_ref[...] = (dact * gv * sg).astype(BF16)

    ospec = pl.BlockSpec((tm, tn), lambda i, j: (i, j))
    return _call(
        body, side, name=name, grid=(t // tm, f // tn),
        in_specs=[pl.BlockSpec((tm, d), lambda i, j: (i, 0)), pl.BlockSpec((tn, d), lambda i, j: (j, 0)),
                  ospec, ospec],
        out_specs=[ospec, ospec],
        out_shape=[jax.ShapeDtypeStruct((t, f), BF16), jax.ShapeDtypeStruct((t, f), BF16)],
        scratch_shapes=[], semantics=("parallel", "parallel"), args=(dx, w_down, g, u),
    )


def _ple_fwd(x, p, w_gate, w_proj_t, *, name):
    t, d = x.shape
    e = p.shape[1]
    tm, tn = _tile(t, 1024), _tile(d, 512)

    def body(xf_ref, xr_ref, p_ref, wg_ref, wp_ref, o_ref):
        s = _dot(xf_ref[...].astype(BF16), wg_ref[...])
        ple = _dot_nt(p_ref[...].astype(BF16), wp_ref[...])
        o_ref[...] = xr_ref[...] + _sigmoid(s) * ple

    return pl.pallas_call(
        body, name=name, grid=(t // tm, d // tn),
        in_specs=[pl.BlockSpec((tm, d), lambda i, j: (i, 0)), pl.BlockSpec((tm, tn), lambda i, j: (i, j)),
                  pl.BlockSpec((tm, e), lambda i, j: (i, 0)), pl.BlockSpec((d, tn), lambda i, j: (0, j)),
                  pl.BlockSpec((tn, e), lambda i, j: (j, 0))],
        out_specs=pl.BlockSpec((tm, tn), lambda i, j: (i, j)),
        out_shape=jax.ShapeDtypeStruct((t, d), F32),
        compiler_params=_params("parallel", "parallel"),
    )(x, x, p, w_gate, w_proj_t)


def _ple_bwd(x, p, w_gate, w_proj_t, dout, *, name):
    t, d = x.shape
    e = p.shape[1]
    tm, tn = _tile(t, 1024), _tile(d, 512)

    def body(xf_ref, p_ref, wg_ref, wp_ref, do_ref, ds_ref, dple_ref):
        s = _dot(xf_ref[...].astype(BF16), wg_ref[...])
        ple = _dot_nt(p_ref[...].astype(BF16), wp_ref[...])
        gate = _sigmoid(s)
        dov = do_ref[...]
        dple_ref[...] = (dov * gate).astype(BF16)
        ds_ref[...] = (dov * ple * gate * (1.0 - gate)).astype(BF16)

    ospec = pl.BlockSpec((tm, tn), lambda i, j: (i, j))
    return pl.pallas_call(
        body, name=name, grid=(t // tm, d // tn),
        in_specs=[pl.BlockSpec((tm, d), lambda i, j: (i, 0)), pl.BlockSpec((tm, e), lambda i, j: (i, 0)),
                  pl.BlockSpec((d, tn), lambda i, j: (0, j)), pl.BlockSpec((tn, e), lambda i, j: (j, 0)), ospec],
        out_specs=[ospec, ospec],
        out_shape=[jax.ShapeDtypeStruct((t, d), BF16), jax.ShapeDtypeStruct((t, d), BF16)],
        compiler_params=_params("parallel", "parallel"),
    )(x, p, w_gate, w_proj_t, dout)


CONV_TIME_TILE = 256
CONV_HALO = 8


def _conv_taps(ext, w):
    acc = ext[CONV_HALO:, :] * w[CONV_WIDTH - 1:CONV_WIDTH, :]
    shifted = [ext[CONV_HALO:, :]]
    for j in range(1, CONV_WIDTH):
        sh = pltpu.roll(ext, j, 0)[CONV_HALO:, :]
        shifted.append(sh)
        acc = acc + sh * w[CONV_WIDTH - 1 - j:CONV_WIDTH - j, :]
    return acc, shifted


def _conv_fwd(u, w, b, side=None):
    t, c = u.shape
    tc = _tile(c, 256)
    tt = CONV_TIME_TILE

    def body(u_ref, w_ref, b_ref, o_ref):
        wv, bv = w_ref[...], b_ref[...]

        def tile(start, ext):
            pre = _conv_taps(ext, wv)[0] + bv
            o_ref[pl.ds(start, tt), :] = pre * _sigmoid(pre)

        tile(0, jnp.concatenate([jnp.zeros((CONV_HALO, tc), F32), u_ref[0:tt, :]], axis=0))

        def loop(i, carry):
            start = pl.multiple_of(i * tt, tt)
            tile(start, u_ref[pl.ds(start - CONV_HALO, tt + CONV_HALO), :])
            return carry

        lax.fori_loop(1, t // tt, loop, 0)

    col = pl.BlockSpec((t, tc), lambda j: (0, j))
    return _call(
        body, side, name="conv_fwd", grid=(c // tc,),
        in_specs=[col, pl.BlockSpec((CONV_WIDTH, tc), lambda j: (0, j)), pl.BlockSpec((1, tc), lambda j: (0, j))],
        out_specs=[col], out_shape=[jax.ShapeDtypeStruct((t, c), F32)],
        scratch_shapes=[], semantics=("parallel",), args=(u, w, b),
    )[0]


def _conv_bwd(u, w, b, dact, side=None):
    t, c = u.shape
    tc = _tile(c, 256)
    tt = CONV_TIME_TILE

    def body(u_ref, w_ref, b_ref, da_ref, du_ref, dw_ref, db_ref, dpre_ref):
        wv, bv = w_ref[...], b_ref[...]

        def tile(start, ext, sums):
            acc, shifted = _conv_taps(ext, wv)
            pre = acc + bv
            sg = _sigmoid(pre)
            dpre = da_ref[pl.ds(start, tt), :] * (sg * (1.0 + pre * (1.0 - sg)))
            dpre_ref[pl.ds(start, tt), :] = dpre
            new = [sums[0] + jnp.sum(dpre, axis=0, keepdims=True)]
            for j in range(CONV_WIDTH):
                new.append(sums[1 + j] + jnp.sum(dpre * shifted[j], axis=0, keepdims=True))
            return tuple(new)

        zero = jnp.zeros((1, tc), F32)
        sums = tile(0, jnp.concatenate([jnp.zeros((CONV_HALO, tc), F32), u_ref[0:tt, :]], axis=0),
                    (zero,) * (1 + CONV_WIDTH))

        def loop(i, sums):
            start = pl.multiple_of(i * tt, tt)
            return tile(start, u_ref[pl.ds(start - CONV_HALO, tt + CONV_HALO), :], sums)

        sums = lax.fori_loop(1, t // tt, loop, sums)
        db_ref[...] = sums[0]
        dw_ref[...] = jnp.concatenate([sums[1 + (CONV_WIDTH - 1 - k)] for k in range(CONV_WIDTH)], axis=0)
        dpre_ref[pl.ds(t, CONV_HALO), :] = jnp.zeros((CONV_HALO, tc), F32)

        def loop2(i, carry):
            start = pl.multiple_of(i * tt, tt)
            ext = dpre_ref[pl.ds(start, tt + CONV_HALO), :]
            acc = ext[0:tt, :] * wv[CONV_WIDTH - 1:CONV_WIDTH, :]
            for j in range(1, CONV_WIDTH):
                acc = acc + pltpu.roll(ext, tt + CONV_HALO - j, 0)[0:tt, :] * wv[CONV_WIDTH - 1 - j:CONV_WIDTH - j, :]
            du_ref[pl.ds(start, tt), :] = acc.astype(BF16)
            return carry

        lax.fori_loop(0, t // tt, loop2, 0)

    col = pl.BlockSpec((t, tc), lambda j: (0, j))
    return _call(
        body, side, name="conv_bwd", grid=(c // tc,),
        in_specs=[col, pl.BlockSpec((CONV_WIDTH, tc), lambda j: (0, j)), pl.BlockSpec((1, tc), lambda j: (0, j)), col],
        out_specs=[col, pl.BlockSpec((CONV_WIDTH, tc), lambda j: (0, j)), pl.BlockSpec((1, tc), lambda j: (0, j))],
        out_shape=[jax.ShapeDtypeStruct((t, c), BF16), jax.ShapeDtypeStruct((CONV_WIDTH, c), F32),
                   jax.ShapeDtypeStruct((1, c), F32)],
        scratch_shapes=[pltpu.VMEM((t + CONV_HALO, tc), F32)],
        semantics=("parallel",), args=(u, w, b, dact),
    )


def _softplus(v):
    e = jnp.exp(-jnp.abs(v))
    w = 1.0 + e
    log1p = jnp.where(w == 1.0, e, jnp.log(w) * (e / jnp.where(w == 1.0, 1.0, w - 1.0)))
    return jnp.maximum(v, 0.0) + log1p


def _split3(z):
    hi = z.astype(BF16)
    rest = z - hi.astype(F32)
    mid = rest.astype(BF16)
    return hi, mid, (rest - mid.astype(F32)).astype(BF16)


def _select_dot(z, ones):
    return sum(_dot(term, ones) for term in _split3(z))


def _ssd_prep_fwd(dt_raw, dt_bias, a_log):
    t = dt_raw.shape[0]
    cl = SSD_CHUNK

    def body(r_ref, b_ref, al_ref, acs_ref, dt_rep_ref, acs_rep_ref):
        dt = _softplus(r_ref[...] + b_ref[...])
        adt = dt * (-jnp.exp(al_ref[...]))
        li = lax.broadcasted_iota(jnp.int32, (cl, cl), 0)
        si = lax.broadcasted_iota(jnp.int32, (cl, cl), 1)
        tri = (si <= li).astype(F32)
        acs = jnp.dot(tri, adt, preferred_element_type=F32, precision=HIGHEST)
        acs_ref[...] = acs
        head = lax.broadcasted_iota(jnp.int32, (LANES, D_INNER), 0)
        chan = lax.broadcasted_iota(jnp.int32, (LANES, D_INNER), 1) // SSM_HEAD_DIM
        spread = (head == chan).astype(BF16)
        dt_rep_ref[...] = _select_dot(dt, spread)
        acs_rep_ref[...] = _select_dot(acs, spread)

    row = pl.BlockSpec((cl, LANES), lambda i: (i, 0))
    wide = pl.BlockSpec((cl, D_INNER), lambda i: (i, 0))
    vec = pl.BlockSpec((1, LANES), lambda i: (0, 0))
    return pl.pallas_call(
        body, name="ssd_prep_fwd", grid=(t // cl,),
        in_specs=[row, vec, vec], out_specs=[row, wide, wide],
        out_shape=[jax.ShapeDtypeStruct((t, LANES), F32), jax.ShapeDtypeStruct((t, D_INNER), F32),
                   jax.ShapeDtypeStruct((t, D_INNER), F32)],
        compiler_params=_params("parallel"),
    )(dt_raw, dt_bias, a_log)


def _ssd_prep_bwd(dt_raw, dt_bias, ddt):
    t = dt_raw.shape[0]
    tm = _tile(t, 512)

    def body(r_ref, b_ref, d_ref, o_ref, db_ref):
        g = d_ref[...] * _sigmoid(r_ref[...] + b_ref[...])
        o_ref[...] = g.astype(BF16)
        part = jnp.sum(g, axis=0, keepdims=True)

        @pl.when(pl.program_id(0) == 0)
        def _():
            db_ref[...] = part

        @pl.when(pl.program_id(0) > 0)
        def _():
            db_ref[...] += part

    row = pl.BlockSpec((tm, LANES), lambda i: (i, 0))
    vec = pl.BlockSpec((1, LANES), lambda i: (0, 0))
    return pl.pallas_call(
        body, name="ssd_prep_bwd", grid=(t // tm,),
        in_specs=[row, vec, row], out_specs=[row, vec],
        out_shape=[jax.ShapeDtypeStruct((t, LANES), BF16), jax.ShapeDtypeStruct((1, LANES), F32)],
        compiler_params=_params("arbitrary"),
    )(dt_raw, dt_bias, ddt)


GROUP_W = D_INNER // SSM_GROUPS
PAIRS_PER_GROUP = GROUP_W // LANES


def _head_cols(acs_pair, lt64):
    rolled = pltpu.roll(acs_pair, ATT_HEAD_DIM, 1)
    return jnp.where(lt64, acs_pair, rolled), jnp.where(lt64, rolled, acs_pair)


def _ssd_fwd(xbc, dt_rep, acs_rep, acs_t, dskip_rep, side=None):
    t = xbc.shape[0]
    cl = SSD_CHUNK
    nc = t // cl

    def body(xbc_ref, dt_ref, acs_ref, acst_ref, dskip_ref, y_ref, hin_ref, state_ref):
        @pl.when(pl.program_id(0) == 0)
        def _():
            state_ref[...] = jnp.zeros_like(state_ref)

        lt64 = _lane_lt64(cl)
        li = lax.broadcasted_iota(jnp.int32, (cl, cl), 0)
        si = lax.broadcasted_iota(jnp.int32, (cl, cl), 1)
        causal = li >= si
        hin_ref[...] = state_ref[...]
        for g in range(SSM_GROUPS):
            gsl = slice(g * GROUP_W, (g + 1) * GROUP_W)
            xg = xbc_ref[:, gsl]
            bg = xbc_ref[:, D_INNER + g * SSM_STATE:D_INNER + (g + 1) * SSM_STATE]
            cg = xbc_ref[:, D_INNER + SSM_GROUPS * SSM_STATE + g * SSM_STATE:
                         D_INNER + SSM_GROUPS * SSM_STATE + (g + 1) * SSM_STATE]
            acs = acs_ref[:, gsl]
            xdt = xg * dt_ref[:, gsl]
            atot = acs[cl - 1:cl, :]
            hin = state_ref[:, gsl]
            cgb = cg.astype(BF16)
            gmat = _dot_nt(cgb, bg.astype(BF16))
            yoff = _dot(cgb, hin.astype(BF16)) * jnp.exp(acs)
            snew = _dot(bg.T.astype(BF16), (xdt * jnp.exp(atot - acs)).astype(BF16))
            state_ref[:, gsl] = hin * jnp.exp(atot) + snew
            xdtb = xdt.astype(BF16)
            for pr in range(PAIRS_PER_GROUP):
                psl = slice(pr * LANES, (pr + 1) * LANES)
                cols = _head_cols(acs[:, psl], lt64)
                xp = xdtb[:, psl]
                ys = []
                for hh in range(2):
                    h = (g * PAIRS_PER_GROUP + pr) * 2 + hh
                    seg = cols[hh] - acst_ref[h:h + 1, :]
                    lm = jnp.exp(jnp.where(causal, seg, NEG_BIG))
                    ys.append(_dot((gmat * lm).astype(BF16), xp))
                ydiag = jnp.where(lt64, ys[0], ys[1])
                osl = slice(g * GROUP_W + pr * LANES, g * GROUP_W + (pr + 1) * LANES)
                y_ref[:, osl] = ydiag + yoff[:, psl] + xg[:, psl] * dskip_ref[:, osl]

    row = lambda w: pl.BlockSpec((cl, w), lambda c: (c, 0))
    return _call(
        body, side, name="ssd_fwd", grid=(nc,),
        in_specs=[row(CONV_DIM), row(D_INNER), row(D_INNER),
                  pl.BlockSpec((SSM_HEADS, cl), lambda c: (0, c)), pl.BlockSpec((1, D_INNER), lambda c: (0, 0))],
        out_specs=[row(D_INNER), pl.BlockSpec((None, SSM_STATE, D_INNER), lambda c: (c, 0, 0))],
        out_shape=[jax.ShapeDtypeStruct((t, D_INNER), F32), jax.ShapeDtypeStruct((nc, SSM_STATE, D_INNER), F32)],
        scratch_shapes=[pltpu.VMEM((SSM_STATE, D_INNER), F32)],
        semantics=("arbitrary",), args=(xbc, dt_rep, acs_rep, acs_t, dskip_rep),
    )


def _ssd_bwd(xbc, dt_rep, acs_rep, acs_t, dskip_rep, a_rep, hin_all, dy, side=None):
    t = xbc.shape[0]
    cl = SSD_CHUNK
    nc = t // cl

    def body(xbc_ref, dt_ref, acs_ref, acst_ref, dskip_ref, a_ref, hin_ref, dy_ref,
             dxbc_ref, ddt_ref, da_ref, dds_ref, dstate_ref, dacs_ref, dxs_ref):
        step = pl.program_id(0)

        @pl.when(step == 0)
        def _():
            dstate_ref[...] = jnp.zeros_like(dstate_ref)
            da_ref[...] = jnp.zeros_like(da_ref)
            dds_ref[...] = jnp.zeros_like(dds_ref)

        bd = _head_block_diag()
        lt64 = _lane_lt64(cl)
        li = lax.broadcasted_iota(jnp.int32, (cl, cl), 0)
        si = lax.broadcasted_iota(jnp.int32, (cl, cl), 1)
        lower = li >= si
        upper = si >= li
        last_row = lax.broadcasted_iota(jnp.int32, (cl, GROUP_W), 0) == cl - 1
        for g in range(SSM_GROUPS):
            gsl = slice(g * GROUP_W, (g + 1) * GROUP_W)
            bsl = slice(D_INNER + g * SSM_STATE, D_INNER + (g + 1) * SSM_STATE)
            csl = slice(D_INNER + SSM_GROUPS * SSM_STATE + g * SSM_STATE,
                        D_INNER + SSM_GROUPS * SSM_STATE + (g + 1) * SSM_STATE)
            xg = xbc_ref[:, gsl]
            bg = xbc_ref[:, bsl]
            cg = xbc_ref[:, csl]
            bgb, cgb = bg.astype(BF16), cg.astype(BF16)
            acs = acs_ref[:, gsl]
            xdt = xg * dt_ref[:, gsl]
            atot = acs[cl - 1:cl, :]
            eg = jnp.exp(acs)
            dk = jnp.exp(atot - acs)
            etot = jnp.exp(atot)
            hin = hin_ref[:, gsl]
            hinb = hin.astype(BF16)
            dh = dstate_ref[:, gsl]
            dhb = dh.astype(BF16)
            dyg = dy_ref[:, gsl]

            gmat = _dot_nt(cgb, bgb)
            gmat_t = _dot_nt(bgb, cgb)
            ch = _dot(cgb, hinb)
            dacs = _head_sums(dyg * ch * eg, bd)
            dye = (dyg * eg).astype(BF16)
            dc = _dot_nt(dye, hinb)
            dhin = _dot(cg.T.astype(BF16), dye)
            bdh = _dot(bgb, dhb)
            dxs = bdh * dk
            xdk = xdt * dk
            db = _dot_nt(xdk.astype(BF16), dhb)
            ddk = _head_sums(bdh * xdk, bd)
            dacs = dacs - ddk
            datot = jnp.sum(ddk, axis=0, keepdims=True) + etot * _head_sums(
                jnp.sum(dh * hin, axis=0, keepdims=True), bd)
            dacs = dacs + jnp.where(last_row, datot, 0.0)
            dstate_ref[:, gsl] = dh * etot + dhin

            xdtb = xdt.astype(BF16)
            dgsum = jnp.zeros((cl, cl), F32)
            dgsum_t = jnp.zeros((cl, cl), F32)
            for pr in range(PAIRS_PER_GROUP):
                psl = slice(pr * LANES, (pr + 1) * LANES)
                cols = _head_cols(acs[:, psl], lt64)
                xp = xdtb[:, psl]
                dyp = dyg[:, psl].astype(BF16)
                dx1, dac = [], []
                for hh in range(2):
                    h = (g * PAIRS_PER_GROUP + pr) * 2 + hh
                    mine = lt64 if hh == 0 else jnp.logical_not(lt64)
                    row = acst_ref[h:h + 1, :]
                    lm = jnp.exp(jnp.where(lower, cols[hh] - row, NEG_BIG))
                    lm_t = jnp.exp(jnp.where(upper, row - cols[hh], NEG_BIG))
                    dyh = jnp.where(mine, dyp, jnp.zeros_like(dyp))
                    xh = jnp.where(mine, xp, jnp.zeros_like(xp))
                    dm = _dot_nt(dyh, xp)
                    dm_t = _dot_nt(xh, dyp)
                    m_t = gmat_t * lm_t
                    dx1.append(_dot(m_t.astype(BF16), dyp))
                    w = dm * (gmat * lm)
                    w_t = dm_t * m_t
                    dac.append(jnp.sum(w, axis=1, keepdims=True) - jnp.sum(w_t, axis=1, keepdims=True))
                    dgsum = dgsum + dm * lm
                    dgsum_t = dgsum_t + dm_t * lm_t
                osl = slice(g * GROUP_W + pr * LANES, g * GROUP_W + (pr + 1) * LANES)
                dxs_ref[:, osl] = dxs[:, psl] + jnp.where(lt64, dx1[0], dx1[1])
                dacs_ref[:, osl] = dacs[:, psl] + jnp.where(lt64, jnp.broadcast_to(dac[0], (cl, LANES)),
                                                             jnp.broadcast_to(dac[1], (cl, LANES)))
            dxbc_ref[:, csl] = dc + _dot(dgsum.astype(BF16), bgb)
            dxbc_ref[:, bsl] = db + _dot(dgsum_t.astype(BF16), cgb)

        dadt = _split_dot(upper.astype(BF16), dacs_ref[...])
        xall = xbc_ref[:, 0:D_INNER]
        dtall = dt_ref[...]
        dxsall = dxs_ref[...]
        dyall = dy_ref[...]
        ddt_rep = dadt * a_ref[...] + _head_sums(dxsall * xall, bd)
        chan = lax.broadcasted_iota(jnp.int32, (D_INNER, LANES), 0)
        head = lax.broadcasted_iota(jnp.int32, (D_INNER, LANES), 1)
        ddt_ref[...] = _select_dot(ddt_rep, (chan == head * SSM_HEAD_DIM).astype(BF16))
        dxbc_ref[:, 0:D_INNER] = dxsall * dtall + dyall * dskip_ref[...]
        da_ref[...] += jnp.sum(dadt * dtall, axis=0, keepdims=True)
        dds_ref[...] += jnp.sum(dyall * xall, axis=0, keepdims=True)

        @pl.when(step == nc - 1)
        def _():
            dds_ref[...] = _head_sums(dds_ref[...], bd)

    row = lambda w: pl.BlockSpec((cl, w), lambda c: (nc - 1 - c, 0))
    vec = pl.BlockSpec((1, D_INNER), lambda c: (0, 0))
    return _call(
        body, side, name="ssd_bwd", grid=(nc,),
        in_specs=[row(CONV_DIM), row(D_INNER), row(D_INNER),
                  pl.BlockSpec((SSM_HEADS, cl), lambda c: (0, nc - 1 - c)), vec, vec,
                  pl.BlockSpec((None, SSM_STATE, D_INNER), lambda c: (nc - 1 - c, 0, 0)), row(D_INNER)],
        out_specs=[row(CONV_DIM), row(LANES), vec, vec],
        out_shape=[jax.ShapeDtypeStruct((t, CONV_DIM), F32), jax.ShapeDtypeStruct((t, LANES), F32),
                   jax.ShapeDtypeStruct((1, D_INNER), F32), jax.ShapeDtypeStruct((1, D_INNER), F32)],
        scratch_shapes=[pltpu.VMEM((SSM_STATE, D_INNER), F32), pltpu.VMEM((cl, D_INNER), F32),
                        pltpu.VMEM((cl, D_INNER), F32)],
        semantics=("arbitrary",), args=(xbc, dt_rep, acs_rep, acs_t, dskip_rep, a_rep, hin_all, dy),
    )


def _gate_norm_fwd(y, z, w):
    t, c = y.shape
    tm = _tile(t, 256)

    def body(y_ref, z_ref, w_ref, o_ref):
        for g in range(SSM_GROUPS):
            gsl = slice(g * GROUP_W, (g + 1) * GROUP_W)
            zv = z_ref[:, gsl]
            v = y_ref[:, gsl] * (zv * _sigmoid(zv))
            r = lax.rsqrt(jnp.mean(v * v, axis=-1, keepdims=True) + NORM_EPS)
            o_ref[:, gsl] = (v * r * w_ref[:, gsl]).astype(BF16)

    row = pl.BlockSpec((tm, c), lambda i: (i, 0))
    return pl.pallas_call(
        body, name="gate_norm_fwd", grid=(t // tm,),
        in_specs=[row, row, pl.BlockSpec((1, c), lambda i: (0, 0))], out_specs=row,
        out_shape=jax.ShapeDtypeStruct((t, c), BF16),
        compiler_params=_params("parallel"),
    )(y, z, w)


def _gate_norm_bwd(y, z, w, dout, side=None):
    t, c = y.shape
    tm = _tile(t, 256)

    def body(y_ref, z_ref, w_ref, do_ref, dy_ref, dz_ref, dw_ref):
        @pl.when(pl.program_id(0) == 0)
        def _():
            dw_ref[...] = jnp.zeros_like(dw_ref)

        for g in range(SSM_GROUPS):
            gsl = slice(g * GROUP_W, (g + 1) * GROUP_W)
            zv, yv, dov = z_ref[:, gsl], y_ref[:, gsl], do_ref[:, gsl]
            sg = _sigmoid(zv)
            sz = zv * sg
            v = yv * sz
            r = lax.rsqrt(jnp.mean(v * v, axis=-1, keepdims=True) + NORM_EPS)
            vh = v * r
            dvh = dov * w_ref[:, gsl]
            mean = jnp.mean(dvh * vh, axis=-1, keepdims=True)
            dv = r * (dvh - vh * mean)
            dy_ref[:, gsl] = dv * sz
            dz_ref[:, gsl] = (dv * yv * (sg * (1.0 + zv * (1.0 - sg)))).astype(BF16)
            dw_ref[:, gsl] += jnp.sum(dov * vh, axis=0, keepdims=True)

    row = pl.BlockSpec((tm, c), lambda i: (i, 0))
    vec = pl.BlockSpec((1, c), lambda i: (0, 0))
    return _call(
        body, side, name="gate_norm_bwd", grid=(t // tm,),
        in_specs=[row, row, vec, row], out_specs=[row, row, vec],
        out_shape=[jax.ShapeDtypeStruct((t, c), F32), jax.ShapeDtypeStruct((t, c), BF16),
                   jax.ShapeDtypeStruct((1, c), F32)],
        scratch_shapes=[], semantics=("arbitrary",), args=(y, z, w, dout),
    )


ATT_W = ATT_HEADS * ATT_HEAD_DIM
N_QKV_BLOCKS = 9
ATT_SCALE = 1.0 / math.sqrt(ATT_HEAD_DIM)


def _head_rmsnorm(x, gain, bd):
    ms = _head_sums(x * x, bd, terms=1) * (1.0 / ATT_HEAD_DIM)
    return x * lax.rsqrt(ms + NORM_EPS) * gain


def _class_rows(ref, blk, r, dil):
    span = ATT_BLOCK * dil
    sub = ref.at[pl.ds(pl.multiple_of(blk * span, span), span), :]
    return sub[...] if dil == 1 else sub[pl.ds(r, ATT_BLOCK, stride=dil), :]


def _store_class_rows(ref, blk, r, dil, val):
    span = ATT_BLOCK * dil
    sub = ref.at[pl.ds(pl.multiple_of(blk * span, span), span), :]
    if dil == 1:
        sub[...] = val
    else:
        sub[pl.ds(r, ATT_BLOCK, stride=dil), :] = val


PAIRS = ATT_HEADS // 2


def _pair_col(g, j):
    return lambda pair: (0, (g * 3 + j) * PAIRS + pair)


def _pair_slopes(pair):
    steps = jnp.full((1, 2 * ATT_BLOCK), 2 * pair + 1, jnp.int32).astype(F32)
    first = jnp.exp(steps * (-0.5 * math.log(2.0)))
    return first, first * (2.0 ** -0.5)


NORM_ROWS = 512


ROW_SLICES = 4
SLICE_ROWS = 2 * ATT_BLOCK // ROW_SLICES


def _fill_band_bias(bias_ref, pair, dil, transposed):
    bq = ATT_BLOCK
    a = lax.broadcasted_iota(jnp.int32, (2 * bq, 2 * bq), 0) % bq
    b = lax.broadcasted_iota(jnp.int32, (2 * bq, 2 * bq), 1)
    dist = (b - a) if transposed else (a + bq - b)
    in_band = (dist >= 0) & (dist <= bq)
    s0, s1 = _pair_slopes(pair)
    first_head = lax.broadcasted_iota(jnp.int32, (2 * bq, 2 * bq), 0) < bq
    bias = jnp.where(first_head, s0, s1) * (dist.astype(F32) * float(dil))
    inside = (b < bq) if transposed else (b >= bq)
    bias_ref[1] = jnp.where(in_band, bias, -NEG_BIG)
    bias_ref[0] = jnp.where(in_band & inside, bias, -NEG_BIG)


def _row_slices():
    return [slice(i * SLICE_ROWS, (i + 1) * SLICE_ROWS) for i in range(ROW_SLICES)]


def _stack_heads(tile):
    rows = lax.broadcasted_iota(jnp.int32, (2 * ATT_BLOCK, LANES), 0) < ATT_BLOCK
    lanes = lax.broadcasted_iota(jnp.int32, (2 * ATT_BLOCK, LANES), 1) < ATT_HEAD_DIM
    both = jnp.concatenate([tile, tile], axis=0)
    return jnp.where(rows == lanes, both, jnp.zeros_like(both))


def _unstack_heads(stacked, lt64):
    return jnp.where(lt64, stacked[:ATT_BLOCK], stacked[ATT_BLOCK:])


ITEMS_PER_PASS = 4


def _item_loop(nb, dil, work):
    if dil == 1:
        def trip(i, carry):
            work([(i * ITEMS_PER_PASS + b, 0) for b in range(ITEMS_PER_PASS)])
            return carry

        lax.fori_loop(0, nb // ITEMS_PER_PASS, trip, 0)
    else:
        def trip(n, carry):
            for r0 in range(0, dil, ITEMS_PER_PASS):
                work([(n, r0 + j) for j in range(ITEMS_PER_PASS)])
            return carry

        lax.fori_loop(0, nb, trip, 0)


def _qk_normalised(tile, j, gq_ref, gk_ref):
    kind = (j // (ATT_W // tile.shape[1])) % 3
    gain = jnp.where(kind == 0, gq_ref[...] * ATT_SCALE, gk_ref[...])
    return jnp.where(kind == 2, tile, _head_rmsnorm(tile, gain, _head_block_diag()))


def _attn_fwd(qkn, g, dil):
    t = qkn.shape[0]
    nb = t // dil // ATT_BLOCK
    bq = ATT_BLOCK

    def body(qn_ref, kn_ref, v_ref, o_ref, l_ref, bias_ref):
        _fill_band_bias(bias_ref, pl.program_id(0), dil, False)
        lt64 = _lane_lt64(bq)

        def work(items):
            scores, values, probs = [], [], []
            for n, r in items:
                prev = jnp.maximum(n - 1, 0)
                q2 = _stack_heads(_class_rows(qn_ref, n, r, dil).astype(BF16))
                kcat = jnp.concatenate([_class_rows(kn_ref, prev, r, dil), _class_rows(kn_ref, n, r, dil)],
                                       axis=0).astype(BF16)
                values.append(jnp.concatenate([_class_rows(v_ref, prev, r, dil), _class_rows(v_ref, n, r, dil)],
                                              axis=0).astype(BF16))
                scores.append(_dot_nt(q2, kcat))
            for (n, r), sc in zip(items, scores):
                bias = bias_ref.at[jnp.minimum(n, 1)]
                ps, inv, lses = [], [], []
                for rows in _row_slices():
                    s = sc[rows] - bias[rows, :]
                    m = jnp.max(s, axis=1, keepdims=True)
                    p = jnp.exp(s - m)
                    l = jnp.sum(p, axis=1, keepdims=True)
                    ps.append(p.astype(BF16))
                    inv.append(jnp.broadcast_to(1.0 / l, (SLICE_ROWS, LANES)))
                    lses.append(jnp.broadcast_to(m + jnp.log(l), (SLICE_ROWS, LANES)))
                probs.append((jnp.concatenate(ps, axis=0), jnp.concatenate(inv, axis=0)))
                _store_class_rows(l_ref, n, r, dil, _unstack_heads(jnp.concatenate(lses, axis=0), lt64))
            for (n, r), (p, inv), vcat in zip(items, probs, values):
                _store_class_rows(o_ref, n, r, dil, _unstack_heads(_dot(p, vcat) * inv, lt64))

        _item_loop(nb, dil, work)

    col = lambda j: pl.BlockSpec((t, LANES), _pair_col(g, j))
    out = pl.BlockSpec((t, LANES), lambda pair: (0, pair))
    return pl.pallas_call(
        body, name=f"attn_fwd_g{g}", grid=(PAIRS,),
        in_specs=[col(0), col(1), col(2)], out_specs=[out, out],
        out_shape=[jax.ShapeDtypeStruct((t, ATT_W), F32), jax.ShapeDtypeStruct((t, ATT_W), F32)],
        scratch_shapes=[pltpu.VMEM((2, 2 * bq, 2 * bq), F32)],
        compiler_params=_params("parallel"),
    )(qkn, qkn, qkn)


def _one_per_head(rep):
    chan = lax.broadcasted_iota(jnp.int32, (ATT_W, LANES), 0)
    head = lax.broadcasted_iota(jnp.int32, (ATT_W, LANES), 1)
    return _select_dot(rep, (chan == head * ATT_HEAD_DIM).astype(BF16))


def _attn_combine_fwd(outs, lses):
    t = outs[0].shape[0]
    tm = _tile(t, 256)

    def body(o0, o1, o2, l0, l1, l2, ob_ref, of_ref, lt_ref, lc_ref):
        a, b, c = l0[...], l1[...], l2[...]
        m = jnp.maximum(jnp.maximum(a, b), c)
        ea, eb, ec = jnp.exp(a - m), jnp.exp(b - m), jnp.exp(c - m)
        ssum = ea + eb + ec
        o = (ea * o0[...] + eb * o1[...] + ec * o2[...]) / ssum
        ob_ref[...] = o.astype(BF16)
        of_ref[...] = o
        lse = m + jnp.log(ssum)
        lt_ref[...] = lse
        lc_ref[...] = _one_per_head(lse)

    row = pl.BlockSpec((tm, ATT_W), lambda i: (i, 0))
    return pl.pallas_call(
        body, name="attn_combine_fwd", grid=(t // tm,),
        in_specs=[row] * 6, out_specs=[row] * 3 + [pl.BlockSpec((tm, LANES), lambda i: (i, 0))],
        out_shape=[jax.ShapeDtypeStruct((t, ATT_W), BF16), jax.ShapeDtypeStruct((t, ATT_W), F32),
                   jax.ShapeDtypeStruct((t, ATT_W), F32), jax.ShapeDtypeStruct((t, LANES), F32)],
        compiler_params=_params("parallel"),
    )(*outs, *lses)


def _attn_combine_bwd(do, o):
    t = do.shape[0]
    tm = _tile(t, 256)

    def body(do_ref, o_ref, dl_ref, dc_ref):
        dl = _head_sums(do_ref[...] * o_ref[...], _head_block_diag())
        dl_ref[...] = dl
        dc_ref[...] = _one_per_head(dl)

    row = pl.BlockSpec((tm, ATT_W), lambda i: (i, 0))
    return pl.pallas_call(
        body, name="attn_combine_bwd", grid=(t // tm,),
        in_specs=[row, row], out_specs=[row, pl.BlockSpec((tm, LANES), lambda i: (i, 0))],
        out_shape=[jax.ShapeDtypeStruct((t, ATT_W), F32), jax.ShapeDtypeStruct((t, LANES), F32)],
        compiler_params=_params("parallel"),
    )(do, o)


def _head_rmsnorm_bwd(x_ref, dy_ref, gain_ref, dx_ref, dgain_ref):
    bd = _head_block_diag()
    gain = gain_ref[...]

    def step(i, acc):
        rows = pl.ds(pl.multiple_of(i * NORM_ROWS, NORM_ROWS), NORM_ROWS)
        x, dy = x_ref[rows, :], dy_ref[rows, :]
        r = lax.rsqrt(_head_sums(x * x, bd, terms=1) * (1.0 / ATT_HEAD_DIM) + NORM_EPS)
        xh = x * r
        dxh = dy * gain
        mean = _head_sums(dxh * xh, bd, terms=1) * (1.0 / ATT_HEAD_DIM)
        dx_ref[rows, :] = (r * (dxh - xh * mean)).astype(BF16)
        return acc + jnp.sum(dy * xh, axis=0, keepdims=True)

    acc = lax.fori_loop(0, x_ref.shape[0] // NORM_ROWS, step, jnp.zeros((1, LANES), F32))
    dgain_ref[...] = jnp.broadcast_to(acc, dgain_ref.shape)


def _attn_bwd_dq(qkv, qkn, gq, do, l_rep, dl_rep, g, dil):
    t = qkv.shape[0]
    nb = t // dil // ATT_BLOCK
    bq = ATT_BLOCK

    def body(q_ref, qn_ref, kn_ref, v_ref, gq_ref, do_ref, l_ref, dl_ref, dx_ref, dgain_ref, bias_ref, dq_ref):
        _fill_band_bias(bias_ref, pl.program_id(0), dil, False)
        lt64 = _lane_lt64(bq)

        def per_row(tile):
            cols = _head_cols(tile, lt64)
            half = jnp.concatenate([cols[0], cols[1]], axis=0)
            return jnp.concatenate([half, half], axis=1)

        def work(items):
            products, keys, dscores = [], [], []
            for n, r in items:
                prev = jnp.maximum(n - 1, 0)
                q2 = _stack_heads(_class_rows(qn_ref, n, r, dil).astype(BF16))
                do2 = _stack_heads(_class_rows(do_ref, n, r, dil).astype(BF16))
                kcat = jnp.concatenate([_class_rows(kn_ref, prev, r, dil), _class_rows(kn_ref, n, r, dil)],
                                       axis=0).astype(BF16)
                vcat = jnp.concatenate([_class_rows(v_ref, prev, r, dil), _class_rows(v_ref, n, r, dil)],
                                       axis=0).astype(BF16)
                keys.append(kcat)
                products.append((_dot_nt(q2, kcat), _dot_nt(do2, vcat)))
            for (n, r), (scores, dps) in zip(items, products):
                bias = bias_ref.at[jnp.minimum(n, 1)]
                lse = per_row(_class_rows(l_ref, n, r, dil))
                dl = per_row(_class_rows(dl_ref, n, r, dil))
                dss = []
                for rows in _row_slices():
                    p = jnp.exp(scores[rows] - bias[rows, :] - lse[rows])
                    dss.append((p * (dps[rows] - dl[rows])).astype(BF16))
                dscores.append(jnp.concatenate(dss, axis=0))
            for (n, r), ds, kcat in zip(items, dscores, keys):
                _store_class_rows(dq_ref, n, r, dil, _unstack_heads(_dot(ds, kcat) * ATT_SCALE, lt64))

        _item_loop(nb, dil, work)
        _head_rmsnorm_bwd(q_ref, dq_ref, gq_ref, dx_ref, dgain_ref)

    col = lambda j: pl.BlockSpec((t, LANES), _pair_col(g, j))
    vec = pl.BlockSpec((1, LANES), lambda pair: (0, 0))
    tok = pl.BlockSpec((t, LANES), lambda pair: (0, pair))
    return pl.pallas_call(
        body, name=f"attn_bwd_dq_g{g}", grid=(PAIRS,),
        in_specs=[col(0), col(0), col(1), col(2), vec, tok, tok, tok],
        out_specs=[tok, pl.BlockSpec((None, 8, LANES), lambda pair: (pair, 0, 0))],
        out_shape=[jax.ShapeDtypeStruct((t, ATT_W), BF16), jax.ShapeDtypeStruct((PAIRS, 8, LANES), F32)],
        scratch_shapes=[pltpu.VMEM((2, 2 * bq, 2 * bq), F32), pltpu.VMEM((t, LANES), F32)],
        compiler_params=_params("parallel"),
    )(qkv, qkn, qkn, qkn, gq, do, l_rep, dl_rep)


def _attn_bwd_dkv(qkv, qkn, gk, do, l_row, dl_row, g, dil):
    t = qkv.shape[0]
    nb = t // dil // ATT_BLOCK
    bq = ATT_BLOCK

    def body(k_ref, qn_ref, kn_ref, v_ref, gk_ref, do_ref, l_ref, dl_ref, dkx_ref, dvx_ref, dgain_ref, bias_ref,
             dk_ref, dv_ref):
        _fill_band_bias(bias_ref, pl.program_id(0), dil, True)
        lt64 = _lane_lt64(bq)

        def per_query(ref, hh, lane_c, lane_n):
            return jnp.concatenate([ref[hh:hh + 1, pl.ds(lane_c, bq)], ref[hh:hh + 1, pl.ds(lane_n, bq)]], axis=1)

        def work(items):
            products, operands, weights = [], [], []
            for n, r in items:
                nxt = jnp.minimum(n + 1, nb - 1)
                k2 = _stack_heads(_class_rows(kn_ref, n, r, dil).astype(BF16))
                v2 = _stack_heads(_class_rows(v_ref, n, r, dil).astype(BF16))
                qcat = jnp.concatenate([_class_rows(qn_ref, n, r, dil), _class_rows(qn_ref, nxt, r, dil)],
                                       axis=0).astype(BF16)
                docat = jnp.concatenate([_class_rows(do_ref, n, r, dil), _class_rows(do_ref, nxt, r, dil)],
                                        axis=0).astype(BF16)
                operands.append((qcat, docat))
                products.append((_dot_nt(k2, qcat), _dot_nt(v2, docat)))
            for (n, r), (scores, dps) in zip(items, products):
                nxt = jnp.minimum(n + 1, nb - 1)
                bias = bias_ref.at[jnp.where(n == nb - 1, 0, 1)]
                lane_c = pl.multiple_of((r * nb + n) * bq, bq)
                lane_n = pl.multiple_of((r * nb + nxt) * bq, bq)
                lse = [per_query(l_ref, hh, lane_c, lane_n) for hh in range(2)]
                dl = [per_query(dl_ref, hh, lane_c, lane_n) for hh in range(2)]
                pts, dss = [], []
                for i, rows in enumerate(_row_slices()):
                    hh = i * SLICE_ROWS // bq
                    p_t = jnp.exp(scores[rows] - bias[rows, :] - lse[hh])
                    pts.append(p_t.astype(BF16))
                    dss.append((p_t * (dps[rows] - dl[hh])).astype(BF16))
                weights.append((jnp.concatenate(pts, axis=0), jnp.concatenate(dss, axis=0)))
            for (n, r), (p_t, ds_t), (qcat, docat) in zip(items, weights, operands):
                _store_class_rows(dv_ref, n, r, dil, _unstack_heads(_dot(p_t, docat), lt64))
                _store_class_rows(dk_ref, n, r, dil, _unstack_heads(_dot(ds_t, qcat), lt64))

        _item_loop(nb, dil, work)
        _head_rmsnorm_bwd(k_ref, dk_ref, gk_ref, dkx_ref, dgain_ref)

        def cast_rows(i, carry):
            rows = pl.ds(pl.multiple_of(i * NORM_ROWS, NORM_ROWS), NORM_ROWS)
            dvx_ref[rows, :] = dv_ref[rows, :].astype(BF16)
            return carry

        lax.fori_loop(0, t // NORM_ROWS, cast_rows, 0)

    col = lambda j: pl.BlockSpec((t, LANES), _pair_col(g, j))
    vec = pl.BlockSpec((1, LANES), lambda pair: (0, 0))
    tok = pl.BlockSpec((t, LANES), lambda pair: (0, pair))
    rows = pl.BlockSpec((None, 8, t), lambda pair: (pair, 0, 0))
    return pl.pallas_call(
        body, name=f"attn_bwd_dkv_g{g}", grid=(PAIRS,),
        in_specs=[col(1), col(0), col(1), col(2), vec, tok, rows, rows],
        out_specs=[tok, tok, pl.BlockSpec((None, 8, LANES), lambda pair: (pair, 0, 0))],
        out_shape=[jax.ShapeDtypeStruct((t, ATT_W), BF16), jax.ShapeDtypeStruct((t, ATT_W), BF16),
                   jax.ShapeDtypeStruct((PAIRS, 8, LANES), F32)],
        scratch_shapes=[pltpu.VMEM((2, 2 * bq, 2 * bq), F32), pltpu.VMEM((t, LANES), F32),
                        pltpu.VMEM((t, LANES), F32)],
        compiler_params=_params("parallel"),
    )(qkv, qkn, qkn, qkn, gk, do, l_row, dl_row)


def _rows_by_residue(one_per_head, dil):
    t = one_per_head.shape[0]
    per_head = one_per_head[:, :ATT_HEADS]
    rows = per_head.reshape(t // dil, dil, ATT_HEADS).transpose(2, 1, 0).reshape(PAIRS, 2, t)
    return jnp.pad(rows, ((0, 0), (0, 6), (0, 0)))


def _per_head(rep_row):
    return rep_row[0, ::SSM_HEAD_DIM]


def _rep_heads(v):
    return jnp.repeat(v, SSM_HEAD_DIM)[None, :]


def _pad_lanes(v):
    return jnp.pad(v, ((0, 0), (0, LANES - v.shape[1])))


class _NoOverlap:
    def side(self, host):
        return None

    def after(self, host):
        pass

    def begin_backward(self, grads):
        pass


def _hosted(plan, host, fn, *args, **kwargs):
    out = fn(*args, side=plan.side(host), **kwargs)
    plan.after(host)
    return out


def _ffn_ple_fwd(x1, p_i, prm, i, plan):
    h = _rmsnorm_fwd(x1, prm["norm_ffn"][i:i + 1], name=f"ffn_norm_fwd_{i}")
    g, u, act = _hosted(plan, f"swiglu_fwd_{i}", _swiglu_fwd, h, prm["ffn_w_gate"][i], prm["ffn_w_up"][i],
                        name=f"swiglu_fwd_{i}")
    x2 = _hosted(plan, f"ffn_down_{i}", _matmul, act, prm["ffn_w_down"][i], mode="nn", addend=x1,
                 name=f"ffn_down_{i}")
    x3 = _ple_fwd(x2, p_i, prm["ple_w_gate"][i], prm["ple_w_proj"][i], name=f"ple_fwd_{i}")
    return x3, dict(x1=x1, h=h, g=g, u=u, act=act, x2=x2)


def _ffn_ple_bwd(dx3, p_i, prm, i, sv, grads, plan):
    ds, dple = _ple_bwd(sv["x2"], p_i, prm["ple_w_gate"][i], prm["ple_w_proj"][i], dx3, name=f"ple_bwd_{i}")
    grads["ple_w_gate"][i] = _matmul_tn(sv["x2"], ds, name=f"d_ple_w_gate_{i}")
    grads["ple_w_proj"][i] = _matmul_tn(dple, p_i, name=f"d_ple_w_proj_{i}")
    dx2 = _matmul(ds, prm["ple_w_gate"][i], mode="nt", addend=dx3, name=f"ple_dx_{i}")
    grads["ffn_w_down"][i] = _matmul_tn(sv["act"], dx2, name=f"d_ffn_w_down_{i}")
    dg, du = _hosted(plan, f"swiglu_bwd_{i}", _swiglu_bwd, dx2, prm["ffn_w_down"][i], sv["g"], sv["u"],
                     name=f"swiglu_bwd_{i}")
    grads["ffn_w_gate"][i] = _matmul_tn(dg, sv["h"], name=f"d_ffn_w_gate_{i}")
    grads["ffn_w_up"][i] = _matmul_tn(du, sv["h"], name=f"d_ffn_w_up_{i}")
    dh = _matmul(dg, prm["ffn_w_gate"][i], mode="nn", name=f"ffn_dh_gate_{i}")
    dh = _matmul(du, prm["ffn_w_up"][i], mode="nn", addend=dh, name=f"ffn_dh_up_{i}")
    dx1, dgain = _rmsnorm_bwd(sv["x1"], prm["norm_ffn"][i:i + 1], dh, dx2, name=f"ffn_norm_bwd_{i}")
    grads["norm_ffn"][i] = dgain[0]
    return dx1


def _mamba_fwd(x0, prm, plan):
    h = _rmsnorm_fwd(x0, prm["norm_mix"][0:1], name="mix_norm_fwd_0")
    z = _hosted(plan, "ssm_in_z", _matmul, h, prm["ssm_w_z"], mode="nt", name="ssm_in_z")
    xbc_pre = _hosted(plan, "ssm_in_xbc", _matmul, h, prm["ssm_w_xbc"], mode="nt", name="ssm_in_xbc")
    dt_raw = _matmul(h, prm["ssm_w_dt"], mode="nt", name="ssm_in_dt")
    xbc = _hosted(plan, "conv_fwd", _conv_fwd, xbc_pre, prm["ssm_conv_w"], prm["ssm_conv_b"])
    dt_bias = _pad_lanes(prm["ssm_dt_bias"])
    a_log = _pad_lanes(prm["ssm_a_log"])
    acs, dt_rep, acs_rep = _ssd_prep_fwd(dt_raw, dt_bias, a_log)
    acs_t = acs[:, :SSM_HEADS].T
    dskip_rep = _rep_heads(prm["ssm_d_skip"][0])
    y, hin_all = _hosted(plan, "ssd_fwd", _ssd_fwd, xbc, dt_rep, acs_rep, acs_t, dskip_rep)
    yn = _gate_norm_fwd(y, z, prm["ssm_norm_w"])
    x1 = _matmul(yn, prm["ssm_w_out"], mode="nn", addend=x0, name="ssm_out")
    sv = dict(x0=x0, h=h, z=z, xbc_pre=xbc_pre, dt_raw=dt_raw, xbc=xbc, dt_bias=dt_bias, dt_rep=dt_rep,
              acs_rep=acs_rep, acs_t=acs_t, dskip_rep=dskip_rep, y=y, hin_all=hin_all, yn=yn)
    return x1, sv


def _mamba_bwd(dx1, prm, sv, grads, plan):
    grads["ssm_w_out"] = _matmul_tn(sv["yn"], dx1, name="d_ssm_w_out")
    dyn = _matmul(dx1, prm["ssm_w_out"], mode="nt", name="ssm_out_dx")
    dy, dz, dnw = _hosted(plan, "gate_norm_bwd", _gate_norm_bwd, sv["y"], sv["z"], prm["ssm_norm_w"], dyn)
    grads["ssm_norm_w"] = dnw
    a_rep = _rep_heads(-jnp.exp(prm["ssm_a_log"][0]))
    dxbc, ddt, da_rep, dds_rep = _hosted(plan, "ssd_bwd", _ssd_bwd, sv["xbc"], sv["dt_rep"], sv["acs_rep"],
                                             sv["acs_t"], sv["dskip_rep"], a_rep, sv["hin_all"], dy)
    grads["ssm_d_skip"] = _per_head(dds_rep)[None, :]
    grads["ssm_a_log"] = (_per_head(da_rep) * _per_head(a_rep))[None, :]
    ddt_raw, dbias = _ssd_prep_bwd(sv["dt_raw"], sv["dt_bias"], ddt)
    grads["ssm_dt_bias"] = dbias[:, :SSM_HEADS]
    du, dcw, dcb = _hosted(plan, "conv_bwd", _conv_bwd, sv["xbc_pre"], prm["ssm_conv_w"], prm["ssm_conv_b"], dxbc)
    grads["ssm_conv_w"] = dcw
    grads["ssm_conv_b"] = dcb
    h = sv["h"]
    grads["ssm_w_in"] = jnp.concatenate(
        [_matmul_tn(dz, h, name="d_ssm_w_z"), _matmul_tn(du, h, name="d_ssm_w_xbc"),
         _matmul_tn(ddt_raw, h, name="d_ssm_w_dt")[:SSM_HEADS]], axis=0)
    dh = _hosted(plan, "ssm_dh_z", _matmul, dz, prm["ssm_w_z"], mode="nn", name="ssm_dh_z")
    dh = _hosted(plan, "ssm_dh_xbc", _matmul, du, prm["ssm_w_xbc"], mode="nn", addend=dh, name="ssm_dh_xbc")
    dh = _matmul(ddt_raw, prm["ssm_w_dt"], mode="nn", addend=dh, name="ssm_dh_dt")
    dx0, dgain = _rmsnorm_bwd(sv["x0"], prm["norm_mix"][0:1], dh, dx1, name="mix_norm_bwd_0")
    grads["norm_mix"][0] = dgain[0]
    return dx0


def _attn_mixer_fwd(x0, prm, plan):
    h = _rmsnorm_fwd(x0, prm["norm_mix"][1:2], name="mix_norm_fwd_1")
    n_heads = N_QKV_BLOCKS * ATT_HEADS
    gq = jnp.tile(prm["att_q_norm"], (1, n_heads))
    gk = jnp.tile(prm["att_k_norm"], (1, n_heads))
    qkv, qkn = _hosted(plan, "att_qkv", _matmul, h, prm["att_w_qkv"], mode="nt", name="att_qkv",
                       second=(_qk_normalised, [gq, gk]))
    outs, lses = [], []
    for g, (window, dil) in enumerate(DIL_PATTERNS):
        o_g, l_g = _attn_fwd(qkn, g, dil)
        outs.append(o_g)
        lses.append(l_g)
    o_b, o_f, l_rep, l_one = _attn_combine_fwd(outs, lses)
    x1 = _matmul(o_b, prm["att_w_o"], mode="nn", addend=x0, name="att_out")
    sv = dict(x0=x0, h=h, qkv=qkv, qkn=qkn, gq2=gq[:, :LANES], gk2=gk[:, :LANES], o_b=o_b, o_f=o_f, l_rep=l_rep,
              l_one=l_one)
    return x1, sv


def _attn_mixer_bwd(dx1, prm, sv, grads, plan):
    grads["att_w_o"] = _matmul_tn(sv["o_b"], dx1, name="d_att_w_o")
    do = _hosted(plan, "att_out_dx", _matmul, dx1, prm["att_w_o"], mode="nt", name="att_out_dx")
    dl_rep, dl_one = _attn_combine_bwd(do, sv["o_f"])
    blocks, dgq, dgk = [], [], []
    for g, (window, dil) in enumerate(DIL_PATTERNS):
        dq, dgq_g = _attn_bwd_dq(sv["qkv"], sv["qkn"], sv["gq2"], do, sv["l_rep"], dl_rep, g, dil)
        dk, dv, dgk_g = _attn_bwd_dkv(sv["qkv"], sv["qkn"], sv["gk2"], do, _rows_by_residue(sv["l_one"], dil),
                                      _rows_by_residue(dl_one, dil), g, dil)
        blocks += [dq, dk, dv]
        dgq.append(dgq_g)
        dgk.append(dgk_g)
    dqkv = jnp.concatenate(blocks, axis=1)

    def fold(parts):
        return jnp.stack(parts)[:, :, 0].reshape(-1, ATT_HEAD_DIM).sum(axis=0)[None, :]

    grads["att_q_norm"] = fold(dgq)
    grads["att_k_norm"] = fold(dgk)
    grads["att_w_qkv"] = _matmul_tn(dqkv, sv["h"], name="d_att_w_qkv")
    dh = _hosted(plan, "att_qkv_dx", _matmul, dqkv, prm["att_w_qkv"], mode="nn", name="att_qkv_dx")
    dx0, dgain = _rmsnorm_bwd(sv["x0"], prm["norm_mix"][1:2], dh, dx1, name="mix_norm_bwd_1")
    grads["norm_mix"][1] = dgain[0]
    return dx0


def _local_step(x, p, target, prm, plan=None):
    plan = plan or _NoOverlap()
    grads = {k: [None, None] for k in ("norm_mix", "norm_ffn", "ffn_w_gate", "ffn_w_up", "ffn_w_down",
                                       "ple_w_proj", "ple_w_gate")}
    plan.begin_backward(grads)
    x1, sv_m = _mamba_fwd(x, prm, plan)
    x3, sv_f0 = _ffn_ple_fwd(x1, p[0], prm, 0, plan)
    x4, sv_a = _attn_mixer_fwd(x3, prm, plan)
    x6, sv_f1 = _ffn_ple_fwd(x4, p[1], prm, 1, plan)
    dy, loss_row = _loss_head(x6, target)
    dx4 = _ffn_ple_bwd(dy, p[1], prm, 1, sv_f1, grads, plan)
    dx3 = _attn_mixer_bwd(dx4, prm, sv_a, grads, plan)
    dx1 = _ffn_ple_bwd(dx3, p[0], prm, 0, sv_f0, grads, plan)
    dx0 = _mamba_bwd(dx1, prm, sv_m, grads, plan)
    return loss_row, dx0, grads


W_IN_SLAB_ROWS = 1312


def _position():
    return lax.axis_index("x"), lax.axis_index("y"), lax.axis_index("c")


def _other_chips(x, y):
    return [(1 - x, y), (x, 1 - y), (1 - x, 1 - y)]


def _remote(send_sems, recv_sems, k, src, dst, to):
    return pltpu.make_async_remote_copy(src_ref=src, dst_ref=dst, send_sem=send_sems.at[k], recv_sem=recv_sems.at[k],
                                        device_id=to, device_id_type=MESH)


def _gather_side(entries, whole=()):
    n, nw = len(entries), len(whole)

    def first_hop(ins, outs, send_sems, recv_sems):
        x, y, c = _position()
        cps = []
        for j, chip in enumerate(_other_chips(x, y)):
            for e in range(n):
                cps.append(_remote(send_sems, recv_sems, 6 * e + j, ins[e].at[c], outs[e].at[2 * x + y, c], (*chip, c)))
            for e in range(nw):
                cps.append(_remote(send_sems, recv_sems, 6 * n + 3 * e + j, ins[n + e], outs[n + e].at[2 * x + y],
                                   (*chip, c)))
        return cps

    def start(ins, outs, send_sems, recv_sems):
        for cp in first_hop(ins, outs, send_sems, recv_sems):
            cp.start()

    def finish(ins, outs, send_sems, recv_sems):
        x, y, c = _position()
        me, sibling = (x, y, c), (x, y, 1 - c)
        chips = _other_chips(x, y)
        passed_on = []
        for j, (px, py) in enumerate(chips):
            for e in range(n):
                landed = outs[e].at[2 * px + py, c]
                _remote(send_sems, recv_sems, 6 * e + j, landed, landed, me).wait_recv()
                passed_on.append(_remote(send_sems, recv_sems, 6 * e + 3 + j, landed, landed, sibling))
                passed_on[-1].start()
            for e in range(nw):
                landed = outs[n + e].at[2 * px + py]
                _remote(send_sems, recv_sems, 6 * n + 3 * e + j, landed, landed, me).wait_recv()
        for j, (px, py) in enumerate(chips):
            for e in range(n):
                passed = outs[e].at[2 * px + py, 1 - c]
                _remote(send_sems, recv_sems, 6 * e + 3 + j, passed, passed, me).wait_recv()
        for cp in first_hop(ins, outs, send_sems, recv_sems) + passed_on:
            cp.wait_send()

    shapes = [jax.ShapeDtypeStruct((N_CHIPS,) + a.shape, a.dtype) for a in list(entries) + list(whole)]
    return _Side(list(entries) + list(whole), shapes, 6 * n + 3 * nw, start, finish)


def _run_side(side, name):
    si, so = len(side.inputs), len(side.out_shapes)

    def body(*refs):
        ins, outs, send_sems, recv_sems = refs[:si], refs[si:si + so], refs[-2], refs[-1]
        side.start(ins, outs, send_sems, recv_sems)
        side.finish(ins, outs, send_sems, recv_sems)

    side.outputs = list(pl.pallas_call(
        body, name=name, in_specs=[ANY] * si, out_specs=[ANY] * so, out_shape=side.out_shapes,
        scratch_shapes=[pltpu.SemaphoreType.DMA((side.n_sems,)), pltpu.SemaphoreType.DMA((side.n_sems,))],
    )(*side.inputs))
    return side.outputs


def _swap_side(grads):
    n = len(grads)

    def copies(ins, outs, send_sems, recv_sems):
        x, y, c = _position()
        return [_remote(send_sems, recv_sems, e, ins[e].at[:, 1 - c], outs[e], (x, y, 1 - c)) for e in range(n)]

    def start(ins, outs, send_sems, recv_sems):
        for cp in copies(ins, outs, send_sems, recv_sems):
            cp.start()

    def finish(ins, outs, send_sems, recv_sems):
        for cp in copies(ins, outs, send_sems, recv_sems):
            cp.wait()

    shapes = [jax.ShapeDtypeStruct((N_CHIPS,) + g.shape[2:], g.dtype) for g in grads]
    return _Side(grads, shapes, n, start, finish)


def _chip_exchange_side(chipsums):
    n = len(chipsums)

    def copies(ins, outs, send_sems, recv_sems):
        x, y, c = _position()
        return [_remote(send_sems, recv_sems, 3 * e + j, ins[e].at[2 * tx + ty], outs[e].at[j], (tx, ty, c))
                for j, (tx, ty) in enumerate(_other_chips(x, y)) for e in range(n)]

    def start(ins, outs, send_sems, recv_sems):
        for cp in copies(ins, outs, send_sems, recv_sems):
            cp.start()

    def finish(ins, outs, send_sems, recv_sems):
        for cp in copies(ins, outs, send_sems, recv_sems):
            cp.wait()

    shapes = [jax.ShapeDtypeStruct((3,) + cs.shape[1:], cs.dtype) for cs in chipsums]
    return _Side(chipsums, shapes, 3 * n, start, finish)


def _share_halves(totals):
    n = len(totals)

    def body(*refs):
        t_refs, r_refs = refs[:n], refs[n:2 * n]
        send_sems, recv_sems = refs[2 * n], refs[2 * n + 1]
        x, y, c = _position()
        cps = [pltpu.make_async_remote_copy(src_ref=t_refs[e], dst_ref=r_refs[e], send_sem=send_sems.at[e],
                                            recv_sem=recv_sems.at[e], device_id=(x, y, 1 - c), device_id_type=MESH)
               for e in range(n)]
        for cp in cps:
            cp.start()
        for cp in cps:
            cp.wait()

    return pl.pallas_call(
        body, name="grad_share_halves", in_specs=[ANY] * n, out_specs=[ANY] * n,
        out_shape=[jax.ShapeDtypeStruct(t.shape, t.dtype) for t in totals],
        scratch_shapes=[pltpu.SemaphoreType.DMA((n,)), pltpu.SemaphoreType.DMA((n,))],
    )(*totals)


def _reduce_rows(h):
    return h if h <= 704 else h // 2


def _add_sibling(grad, recv, c_idx, *, name):
    _, _, h, cw = grad.shape
    th = _reduce_rows(h)

    def body(c_ref, g_ref, r_ref, o_ref):
        o_ref[...] = (g_ref[...] + r_ref[...]).astype(BF16)

    return pl.pallas_call(
        body, name=name,
        grid_spec=pltpu.PrefetchScalarGridSpec(
            num_scalar_prefetch=1, grid=(N_CHIPS, h // th),
            in_specs=[pl.BlockSpec((None, None, th, cw), lambda s, i, c_ref: (s, c_ref[0], i, 0)),
                      pl.BlockSpec((None, th, cw), lambda s, i, c_ref: (s, i, 0))],
            out_specs=pl.BlockSpec((None, th, cw), lambda s, i, c_ref: (s, i, 0))),
        out_shape=jax.ShapeDtypeStruct((N_CHIPS, h, cw), BF16),
        compiler_params=_params("parallel", "parallel"),
    )(c_idx, grad, recv)


def _add_chips(chipsum, recv, s_idx, *, name):
    _, h, cw = chipsum.shape
    th = _reduce_rows(h)

    def body(s_ref, own_ref, r_ref, o_ref):
        o_ref[...] = ((own_ref[...].astype(F32) + r_ref[0].astype(F32)) + r_ref[1].astype(F32)) + r_ref[2].astype(F32)

    return pl.pallas_call(
        body, name=name,
        grid_spec=pltpu.PrefetchScalarGridSpec(
            num_scalar_prefetch=1, grid=(h // th,),
            in_specs=[pl.BlockSpec((None, th, cw), lambda i, s_ref: (s_ref[0], i, 0)),
                      pl.BlockSpec((3, th, cw), lambda i, s_ref: (0, i, 0))],
            out_specs=pl.BlockSpec((th, cw), lambda i, s_ref: (i, 0))),
        out_shape=jax.ShapeDtypeStruct((h, cw), F32),
        compiler_params=_params("parallel"),
    )(s_idx, chipsum, recv)


def _adamw_math(w, g, m, v):
    m = ADAM_B1 * m + (1.0 - ADAM_B1) * g
    v = ADAM_B2 * v + (1.0 - ADAM_B2) * (g * g)
    m_hat = m / (1.0 - ADAM_B1 ** ADAM_STEP)
    v_hat = v / (1.0 - ADAM_B2 ** ADAM_STEP)
    delta = -ADAM_LR * (m_hat / (jnp.sqrt(v_hat) + ADAM_EPS) + ADAM_WD * w)
    return delta, m, v


ADAM_TILE_ELEMS = 256 * 1024


def _adamw(w, g, m, v, *, name):
    layers, rows, cols = w.shape
    tr = rows
    for cand in range(8, rows, 8):
        if rows % cand == 0 and cand * cols <= ADAM_TILE_ELEMS:
            tr = cand
    if rows * cols <= ADAM_TILE_ELEMS:
        tr = rows

    def body(w_ref, g_ref, m_ref, v_ref, d_ref, nm_ref, nv_ref):
        d, nm, nv = _adamw_math(w_ref[...], g_ref[...], m_ref[...], v_ref[...])
        d_ref[...] = d
        nm_ref[...] = nm
        nv_ref[...] = nv

    blk = pl.BlockSpec((None, tr, cols), lambda l, i: (l, i, 0))
    sds = jax.ShapeDtypeStruct(w.shape, F32)
    return pl.pallas_call(
        body, name=name, grid=(layers, rows // tr), in_specs=[blk] * 4, out_specs=[blk] * 3, out_shape=[sds] * 3,
        compiler_params=_params("parallel", "parallel"),
    )(w, g, m, v)


SMALL_LAYOUT = (("loss", 1), ("norm_mix", 16), ("norm_ffn", 16), ("ssm_conv_b", 24), ("ssm_dt_bias", 1),
                ("ssm_a_log", 1), ("ssm_d_skip", 1), ("ssm_norm_w", 16), ("att_q_norm", 1), ("att_k_norm", 1),
                ("conv_w_full", 96))
SMALL_ROWS = 176
N_DEVICES = 8


def _small_packs(dicts):
    parts = []
    for values in dicts:
        for name, rows in SMALL_LAYOUT:
            flat = values[name].reshape(-1).astype(F32)
            parts.append(jnp.pad(flat, (0, rows * LANES - flat.shape[0])).reshape(rows, LANES))
        used = sum(r for _, r in SMALL_LAYOUT)
        parts.append(jnp.zeros((SMALL_ROWS - used, LANES), F32))
    return jnp.concatenate(parts, axis=0).reshape(len(dicts), SMALL_ROWS, LANES)


def _small_unpack(pack, shapes):
    out, off = {}, 0
    for name, rows in SMALL_LAYOUT:
        shape = shapes[name]
        n = math.prod(shape)
        out[name] = pack[off:off + rows].reshape(-1)[:n].reshape(shape)
        off += rows
    return out


def _small_allreduce_adamw(g, w, m, v):
    def body(g_ref, w_ref, m_ref, v_ref, gs_ref, d_ref, nm_ref, nv_ref, buf, send_sems, recv_sems):
        x, y, c = _position()
        pos = (x, y, c)
        me = 4 * x + 2 * y + c
        buf[me] = g_ref[...]
        peers = []
        for k in range(1, N_DEVICES):
            bits = ((k >> 2) & 1, (k >> 1) & 1, k & 1)
            peers.append(tuple(1 - p if b else p for p, b in zip(pos, bits)))
        cps = [pltpu.make_async_remote_copy(src_ref=g_ref, dst_ref=buf.at[me], send_sem=send_sems.at[k],
                                            recv_sem=recv_sems.at[k], device_id=peer, device_id_type=MESH)
               for k, peer in enumerate(peers)]
        for cp in cps:
            cp.start()
        for k, (px, py, pc) in enumerate(peers):
            pltpu.make_async_remote_copy(src_ref=g_ref, dst_ref=buf.at[4 * px + 2 * py + pc],
                                         send_sem=send_sems.at[k], recv_sem=recv_sems.at[k],
                                         device_id=(px, py, pc), device_id_type=MESH).wait_recv()
        for cp in cps:
            cp.wait_send()
        total = buf[0]
        for dev in range(1, N_DEVICES):
            total = total + buf[dev]
        gs_ref[...] = total
        d, nm, nv = _adamw_math(w_ref[...], total, m_ref[...], v_ref[...])
        d_ref[...] = d
        nm_ref[...] = nm
        nv_ref[...] = nv

    vm = pl.BlockSpec(memory_space=pltpu.VMEM)
    sds = jax.ShapeDtypeStruct((SMALL_ROWS, LANES), F32)
    return pl.pallas_call(
        body, name="small_allreduce_adamw", in_specs=[vm] * 4, out_specs=[vm] * 4, out_shape=[sds] * 4,
        scratch_shapes=[pltpu.VMEM((N_DEVICES, SMALL_ROWS, LANES), F32),
                        pltpu.SemaphoreType.DMA((N_DEVICES - 1,)), pltpu.SemaphoreType.DMA((N_DEVICES - 1,))],
    )(g, w, m, v)


SMALL = tuple(n for n, _ in SMALL_LAYOUT if n not in ("loss", "conv_w_full"))
WEIGHTS = ("norm_mix", "norm_ffn", "ssm_w_in", "ssm_conv_w", "ssm_conv_b", "ssm_dt_bias", "ssm_a_log", "ssm_d_skip",
           "ssm_norm_w", "ssm_w_out", "att_w_qkv", "att_q_norm", "att_k_norm", "att_w_o", "ffn_w_gate", "ffn_w_up",
           "ffn_w_down", "ple_w_proj", "ple_w_gate")
COLUMN_SHARDED = ("ssm_w_in", "att_w_qkv", "ffn_w_gate", "ffn_w_up", "ple_w_proj")
LAYERED = ("ffn_w_gate", "ffn_w_up", "ffn_w_down", "ple_w_proj", "ple_w_gate")
UPDATED_TRANSPOSED = ("ssm_w_in", "ffn_w_gate", "ffn_w_up")
GATHER_ORDER = ("ssm_w_in", "ssm_w_out", "att_w_qkv", "att_w_o", "ffn_w_gate", "ffn_w_up", "ffn_w_down",
                "ple_w_proj", "ple_w_gate")


def _layers(n):
    return (0, 1) if n in LAYERED else (None,)


def _tag(key):
    return key[0] if key[1] is None else f"{key[0]}_{key[1]}"


QKV_PARTS = 3


def _weight_slab(w, key):
    n, i = key
    if n == "att_w_qkv":
        a = w[n][0].T
        rows = a.shape[0] // QKV_PARTS
        a = a[i * rows:(i + 1) * rows]
    else:
        a = w[n][0 if i is None else i]
        a = a.T if n in COLUMN_SHARDED else a
    if n == "ssm_w_in":
        a = jnp.pad(a, ((0, W_IN_SLAB_ROWS - a.shape[0]), (0, 0)))
    return a.reshape(2, a.shape[0] // 2, a.shape[1]).astype(BF16)


def _install(prm, key, gathered, own, s_me):
    n, i = key
    full = lax.dynamic_update_slice(gathered, own[None], (s_me, 0, 0, 0))
    full = full.reshape(N_CHIPS, 2 * full.shape[2], full.shape[3])
    if n == "att_w_qkv":
        parts = prm.setdefault("att_w_qkv_parts", {})
        parts[i] = full
        if len(parts) == QKV_PARTS:
            prm[n] = jnp.stack([parts[j] for j in range(QKV_PARTS)], axis=1).reshape(-1, D_MODEL)
        return
    if n == "ssm_w_in":
        rows = (D_INNER + CONV_DIM + SSM_HEADS) // N_CHIPS
        w_in_t = full[:, :rows].reshape(N_CHIPS * rows, D_MODEL)
        prm["ssm_w_z"] = w_in_t[:D_INNER]
        prm["ssm_w_xbc"] = w_in_t[D_INNER:D_INNER + CONV_DIM]
        prm["ssm_w_dt"] = jnp.pad(w_in_t[D_INNER + CONV_DIM:], ((0, LANES - SSM_HEADS), (0, 0)))
        return
    full = full.reshape(N_CHIPS * full.shape[1], full.shape[2])
    if i is None:
        prm[n] = full
    else:
        prm.setdefault(n, [None, None])[i] = full


def _grad_slab(grads, key):
    n, i = key
    g = grads[n] if i is None else grads[n][i]
    if n == "ssm_w_in":
        g = jnp.pad(g.reshape(N_CHIPS, g.shape[0] // N_CHIPS, D_MODEL),
                    ((0, 0), (0, W_IN_SLAB_ROWS - g.shape[0] // N_CHIPS), (0, 0)))
    rows = g.size // (N_CHIPS * g.shape[-1])
    return g.reshape(N_CHIPS, 2, rows // 2, g.shape[-1])


def _natural_shard(n, reduced, shape):
    def one(r):
        if n == "ssm_w_in":
            r = r[:shape[-1]]
        return r.T if n in COLUMN_SHARDED else r
    if n in LAYERED:
        return jnp.stack([one(r) for r in reduced]).reshape(shape)
    return one(reduced[0]).reshape(shape)


def kernel(x, p, norm_mix, norm_ffn, ssm_w_in, ssm_conv_w, ssm_conv_b, ssm_dt_bias, ssm_a_log, ssm_d_skip, ssm_norm_w, ssm_w_out, att_w_qkv, att_q_norm, att_k_norm, att_w_o, ffn_w_gate, ffn_w_up, ffn_w_down, ple_w_proj, ple_w_gate, loss_target, m_norm_mix, m_norm_ffn, m_ssm_w_in, m_ssm_conv_w, m_ssm_conv_b, m_ssm_dt_bias, m_ssm_a_log, m_ssm_d_skip, m_ssm_norm_w, m_ssm_w_out, m_att_w_qkv, m_att_q_norm, m_att_k_norm, m_att_w_o, m_ffn_w_gate, m_ffn_w_up, m_ffn_w_down, m_ple_w_proj, m_ple_w_gate, v_norm_mix, v_norm_ffn, v_ssm_w_in, v_ssm_conv_w, v_ssm_conv_b, v_ssm_dt_bias, v_ssm_a_log, v_ssm_d_skip, v_ssm_norm_w, v_ssm_w_out, v_att_w_qkv, v_att_q_norm, v_att_k_norm, v_att_w_o, v_ffn_w_gate, v_ffn_w_up, v_ffn_w_down, v_ple_w_proj, v_ple_w_gate):
    given = dict(locals())
    w = {n: given[n] for n in WEIGHTS}
    m = {n: given["m_" + n] for n in WEIGHTS}
    v = {n: given["v_" + n] for n in WEIGHTS}
    c_idx = lax.axis_index("c").astype(jnp.int32).reshape(1)
    s_idx = (2 * lax.axis_index("x") + lax.axis_index("y")).astype(jnp.int32).reshape(1)

    s_me = 2 * lax.axis_index("x") + lax.axis_index("y")
    first_core = lax.axis_index("c") == 0

    qkv_parts = [("att_w_qkv", j) for j in range(QKV_PARTS)]
    gather_plan = {
        "ssm_in_z": [("ssm_w_out", None)],
        "ssm_in_xbc": [("ffn_w_gate", 0)],
        "conv_fwd": [("ffn_w_up", 0)],
        "ssd_fwd": [("ffn_w_down", 0), ("ple_w_proj", 0), ("ple_w_gate", 0), ("att_w_o", None)],
        "swiglu_fwd_0": qkv_parts[:2],
        "ffn_down_0": qkv_parts[2:],
        "att_qkv": [(n, 1) for n in LAYERED],
    }
    mamba = [("ssm_w_in", None)]
    own = {k: _weight_slab(w, k) for k in mamba + sum(gather_plan.values(), [])}
    prm = {n: w[n] for n in SMALL}

    def land(group, outputs):
        for k, g in zip(group, outputs):
            _install(prm, k, g, own[k], s_me)

    first = _gather_side([own[k] for k in mamba], whole=[ssm_conv_w[0]])
    _run_side(first, "gather_mamba")
    land(mamba, first.outputs)
    conv = lax.dynamic_update_slice(first.outputs[-1], ssm_conv_w, (s_me, 0, 0))
    prm["ssm_conv_w"] = conv.transpose(1, 0, 2).reshape(CONV_WIDTH, CONV_DIM)

    ffn1 = [(n, 1) for n in LAYERED]
    attention = [("att_w_qkv", None), ("att_w_o", None)]
    ffn0 = [(n, 0) for n in LAYERED] + [("ssm_w_out", None)]
    reduce_plan = {"att_out_dx": [("swap", ffn1)], "att_qkv_dx": [("exchange", ffn1)],
                   "swiglu_bwd_0": [("swap", attention)], "gate_norm_bwd": [("swap", ffn0)],
                   "ssd_bwd": [("exchange", attention), ("exchange", ffn0)],
                   "ssm_dh_z": [("swap", mamba)], "ssm_dh_xbc": [("exchange", mamba)]}
    state = {}

    def swap_side(group):
        state[_tag(group[0]), "g4"] = g4 = [_grad_slab(state["grads"], k) for k in group]
        return _swap_side(g4)

    def add_siblings(group, from_sibling):
        state[_tag(group[0]), "chipsums"] = [
            _add_sibling(g, r, c_idx, name="add_sibling_" + _tag(k))
            for g, r, k in zip(state[_tag(group[0]), "g4"], from_sibling, group)]

    def exchange_side(group):
        return _chip_exchange_side(state[_tag(group[0]), "chipsums"])

    def add_chips(group, from_chips):
        for k, cs, r in zip(group, state[_tag(group[0]), "chipsums"], from_chips):
            state["total", k] = _add_chips(cs, r, s_idx, name="add_chips_" + _tag(k))

    class Plan(_NoOverlap):
        def __init__(self):
            self.carried = {host: _gather_side([own[k] for k in group]) for host, group in gather_plan.items()}

        def begin_backward(self, grads):
            state["grads"] = grads

        def side(self, host):
            if host in reduce_plan:
                self.parts = [swap_side(group) if step == "swap" else exchange_side(group)
                              for step, group in reduce_plan[host]]
                self.carried[host] = _sides_together(self.parts)
            return self.carried.get(host)

        def after(self, host):
            if host in gather_plan:
                land(gather_plan[host], self.carried[host].outputs)
            elif host in reduce_plan:
                _share_out(self.carried[host], self.parts)
                for (step, group), part in zip(reduce_plan[host], self.parts):
                    (add_siblings if step == "swap" else add_chips)(group, part.outputs)

    loss_row, dx, grads = _local_step(x[0], p[:, 0], loss_target[0], prm, Plan())

    order = mamba + ffn0 + attention + ffn1
    shared = _share_halves([state["total", k] for k in order])
    reduced = {}
    for k, theirs in zip(order, shared):
        lo = jnp.where(first_core, state["total", k], theirs)
        hi = jnp.where(first_core, theirs, state["total", k])
        reduced.setdefault(k[0], {})[k[1]] = jnp.concatenate([lo, hi], axis=0)
    reduced = {n: [by_layer[i] for i in _layers(n)] for n, by_layer in reduced.items()}

    grad, delta, new_m, new_v = {}, {}, {}, {}
    for n in GATHER_ORDER:
        if n in UPDATED_TRANSPOSED:
            flip = lambda a: a.transpose(0, 2, 1)
            cols = w[n].shape[-1]
            g_t = jnp.stack([r[:cols] for r in reduced[n]])
            grad[n] = flip(g_t)
            delta[n], new_m[n], new_v[n] = [flip(o) for o in _adamw(flip(w[n]), g_t, flip(m[n]), flip(v[n]),
                                                                    name="adamw_" + n)]
            continue
        grad[n] = _natural_shard(n, reduced[n], w[n].shape)
        delta[n], new_m[n], new_v[n] = _adamw(w[n], grad[n], m[n], v[n], name="adamw_" + n)

    small_g = {n: (jnp.stack(grads[n]) if isinstance(grads[n], list) else grads[n]) for n in SMALL}
    small_g["loss"] = loss_row
    small_g["conv_w_full"] = grads["ssm_conv_w"]
    zero = {"loss": jnp.zeros((1, LANES), F32), "conv_w_full": jnp.zeros((CONV_WIDTH, CONV_DIM), F32)}
    packs = _small_packs([small_g, {**w, **zero}, {**m, **zero}, {**v, **zero}])
    outs = _small_allreduce_adamw(packs[0], packs[1], packs[2], packs[3])
    shapes = {n: w[n].shape for n in SMALL}
    shapes["loss"] = (1, LANES)
    shapes["conv_w_full"] = (CONV_WIDTH, CONV_DIM)
    sg, sd, sm, sv = [_small_unpack(o, shapes) for o in outs]
    for n in SMALL:
        grad[n], delta[n], new_m[n], new_v[n] = sg[n], sd[n], sm[n], sv[n]
    loss = sg["loss"][0, 0]
    conv_cols = CONV_DIM // N_CHIPS
    grad["ssm_conv_w"] = lax.dynamic_slice(sg["conv_w_full"], (0, s_me * conv_cols), (CONV_WIDTH, conv_cols))[None]
    delta["ssm_conv_w"], new_m["ssm_conv_w"], new_v["ssm_conv_w"] = _adamw(
        ssm_conv_w, grad["ssm_conv_w"], m_ssm_conv_w, v_ssm_conv_w, name="adamw_ssm_conv_w")

    return (loss, dx[None], *[grad[n] for n in WEIGHTS], *[delta[n] for n in WEIGHTS],
            *[new_m[n] for n in WEIGHTS], *[new_v[n] for n in WEIGHTS])
```

```python
import functools
import math

import jax
import jax.numpy as jnp
from jax import lax
from jax.experimental import pallas as pl
from jax.experimental.pallas import tpu as pltpu

F32 = jnp.float32
BF16 = jnp.bfloat16
HIGHEST = lax.Precision.HIGHEST

NORM_EPS = 1e-6
ADAM_LR, ADAM_B1, ADAM_B2, ADAM_EPS, ADAM_WD, ADAM_STEP = 0.001, 0.9, 0.999, 1e-08, 0.01, 10

D_MODEL = 1024
D_INNER = 2048
SSM_HEADS = 32
SSM_HEAD_DIM = 64
SSM_GROUPS = 4
SSM_STATE = 128
SSD_CHUNK = 128
CONV_DIM = 3072
CONV_WIDTH = 4
ATT_HEADS = 16
ATT_HEAD_DIM = 64
DIL_PATTERNS = ((128, 1), (512, 4), (2048, 16))
ATT_BLOCK = 128
FFN_HIDDEN = 2816
PLE_DIM = 256

LANES = 128
V7X_VMEM_LIMIT = 56 * 1024 * 1024
NEG_BIG = -1e30

N_CHIPS = 4


def _params(*sem):
    return pltpu.CompilerParams(dimension_semantics=sem, vmem_limit_bytes=V7X_VMEM_LIMIT)


def _tile(n, pref):
    if n <= pref:
        return n
    best = None
    for t in range(LANES, pref + 1, LANES):
        if n % t == 0:
            best = t
    assert best is not None, (n, pref)
    return best


def _sigmoid(v):
    return 1.0 / (1.0 + jnp.exp(-v))


def _dot(a, b):
    return jnp.dot(a, b, preferred_element_type=F32)


def _dot_nt(a, b):
    return lax.dot_general(a, b, (((1,), (1,)), ((), ())), preferred_element_type=F32)


def _dot_tn(a, b):
    return lax.dot_general(a, b, (((0,), (0,)), ((), ())), preferred_element_type=F32)


def _head_block_diag():
    i = lax.broadcasted_iota(jnp.int32, (LANES, LANES), 0) // ATT_HEAD_DIM
    j = lax.broadcasted_iota(jnp.int32, (LANES, LANES), 1) // ATT_HEAD_DIM
    return (i == j).astype(BF16)


def _split_dot(ones, z):
    hi = z.astype(BF16)
    lo = (z - hi.astype(F32)).astype(BF16)
    return _dot(ones, hi) + _dot(ones, lo)


def _head_sums(z, bd, terms=2):
    hi = z.astype(BF16)
    lo = (z - hi.astype(F32)).astype(BF16) if terms == 2 else None
    parts = []
    for t in range(z.shape[1] // LANES):
        sl = slice(t * LANES, (t + 1) * LANES)
        part = _dot(hi[:, sl], bd)
        parts.append(part + _dot(lo[:, sl], bd) if terms == 2 else part)
    return parts[0] if len(parts) == 1 else jnp.concatenate(parts, axis=1)


def _lane_lt64(rows):
    return lax.broadcasted_iota(jnp.int32, (rows, LANES), 1) < ATT_HEAD_DIM


MESH = pl.DeviceIdType.MESH
ANY = pl.BlockSpec(memory_space=pl.ANY)


class _Side:
    def __init__(self, inputs, out_shapes, n_sems, start, finish):
        self.inputs, self.out_shapes, self.n_sems = list(inputs), list(out_shapes), n_sems
        self.start, self.finish = start, finish
        self.outputs = None


class _SemaphoresFrom:
    def __init__(self, sems, first):
        self.sems, self.first = sems, first

    @property
    def at(self):
        return self

    def __getitem__(self, k):
        return self.sems.at[self.first + k]


def _sides_together(sides):
    def run(step):
        def both(ins, outs, send_sems, recv_sems):
            i = o = k = 0
            for s in sides:
                ni, no = len(s.inputs), len(s.out_shapes)
                getattr(s, step)(ins[i:i + ni], outs[o:o + no], _SemaphoresFrom(send_sems, k),
                                 _SemaphoresFrom(recv_sems, k))
                i, o, k = i + ni, o + no, k + s.n_sems
        return both

    return _Side(sum([s.inputs for s in sides], []), sum([s.out_shapes for s in sides], []),
                 sum(s.n_sems for s in sides), run("start"), run("finish"))


def _share_out(together, sides):
    o = 0
    for s in sides:
        s.outputs = together.outputs[o:o + len(s.out_shapes)]
        o += len(s.out_shapes)


def _call(body, side, *, name, grid, in_specs, out_specs, out_shape, scratch_shapes, semantics, args):
    in_specs, out_specs, out_shape = list(in_specs), list(out_specs), list(out_shape)
    scratch_shapes = list(scratch_shapes)
    if side is None:
        return pl.pallas_call(body, name=name, grid=grid, in_specs=in_specs, out_specs=out_specs,
                              out_shape=out_shape, scratch_shapes=scratch_shapes,
                              compiler_params=_params(*semantics))(*args)
    ni, no, ns = len(in_specs), len(out_specs), len(scratch_shapes)
    si, so = len(side.inputs), len(side.out_shapes)

    def hosted(*refs):
        ins, s_ins = refs[:ni], refs[ni:ni + si]
        outs, s_outs = refs[ni + si:ni + si + no], refs[ni + si + no:ni + si + no + so]
        scratch = refs[ni + si + no + so:ni + si + no + so + ns]
        send_sems, recv_sems = refs[-2], refs[-1]
        first = pl.program_id(0) == 0
        last = pl.program_id(0) == grid[0] - 1
        for axis in range(1, len(grid)):
            first = jnp.logical_and(first, pl.program_id(axis) == 0)
            last = jnp.logical_and(last, pl.program_id(axis) == grid[axis] - 1)

        @pl.when(first)
        def _():
            side.start(s_ins, s_outs, send_sems, recv_sems)

        body(*ins, *outs, *scratch)

        @pl.when(last)
        def _():
            side.finish(s_ins, s_outs, send_sems, recv_sems)

    res = pl.pallas_call(
        hosted, name=name, grid=grid, in_specs=in_specs + [ANY] * si, out_specs=out_specs + [ANY] * so,
        out_shape=out_shape + side.out_shapes,
        scratch_shapes=scratch_shapes + [pltpu.SemaphoreType.DMA((side.n_sems,)),
                                         pltpu.SemaphoreType.DMA((side.n_sems,))],
        compiler_params=_params(*["arbitrary"] * len(grid)),
    )(*args, *side.inputs)
    side.outputs = list(res[no:])
    return list(res[:no])


def _matmul(a, b, *, mode, name, out_dtype=F32, addend=None, tm=1024, tn=512, tk_max=3072, side=None, second=None):
    m, k = a.shape
    if mode == "nn":
        k2, n = b.shape
    else:
        n, k2 = b.shape
    assert k == k2, (a.shape, b.shape, mode)
    tm, tn, tk = _tile(m, tm), _tile(n, tn), _tile(k, tk_max)
    nk = k // tk
    has_add = addend is not None
    n_rows = len(second[1]) if second else 0
    n_out = 2 if second else 1

    def body(*refs):
        a_ref, b_ref = refs[0], refs[1]
        add_ref = refs[2] if has_add else None
        row_refs = refs[2 + has_add:2 + has_add + n_rows]
        o_ref, acc_ref = refs[-1 - n_out], refs[-1]
        kk = pl.program_id(2)
        col_tile = pl.program_id(1)
        av = a_ref[...].astype(BF16)
        bv = b_ref[...].astype(BF16)
        part = _dot(av, bv) if mode == "nn" else _dot_nt(av, bv)

        @pl.when(kk == 0)
        def _():
            acc_ref[...] = part

        @pl.when(kk > 0)
        def _():
            acc_ref[...] += part

        @pl.when(kk == nk - 1)
        def _():
            res = acc_ref[...]
            if has_add:
                res = res + add_ref[...]
            o_ref[...] = res.astype(out_dtype)
            if second:
                refs[-2][...] = second[0](res, col_tile, *row_refs).astype(second[2])

    a_spec = pl.BlockSpec((tm, tk), lambda i, j, kk: (i, kk))
    if mode == "nn":
        b_spec = pl.BlockSpec((tk, tn), lambda i, j, kk: (kk, j))
    else:
        b_spec = pl.BlockSpec((tn, tk), lambda i, j, kk: (j, kk))
    tile = pl.BlockSpec((tm, tn), lambda i, j, kk: (i, j))
    in_specs = [a_spec, b_spec]
    args = [a, b]
    if has_add:
        in_specs.append(tile)
        args.append(addend)
    if second:
        in_specs += [pl.BlockSpec((1, tn), lambda i, j, kk: (0, j))] * n_rows
        args += list(second[1])
    outs = _call(
        body, side, name=name, grid=(m // tm, n // tn, nk),
        in_specs=in_specs, out_specs=[tile] * n_out,
        out_shape=[jax.ShapeDtypeStruct((m, n), out_dtype)] + ([jax.ShapeDtypeStruct((m, n), second[2])] if second
                                                                 else []),
        scratch_shapes=[pltpu.VMEM((tm, tn), F32)],
        semantics=("parallel", "parallel", "arbitrary"), args=args,
    )
    return outs if second else outs[0]


def _matmul_tn(a, b, *, name, tm=1408, tn=512, tk=1024):
    t, m = a.shape
    t2, n = b.shape
    assert t == t2
    tm, tn, tk = _tile(m, tm), _tile(n, tn), _tile(t, tk)

    def body(a_ref, b_ref, o_ref):
        part = _dot_tn(a_ref[...].astype(BF16), b_ref[...].astype(BF16))

        @pl.when(pl.program_id(2) == 0)
        def _():
            o_ref[...] = part

        @pl.when(pl.program_id(2) > 0)
        def _():
            o_ref[...] += part

    return pl.pallas_call(
        body, name=name, grid=(m // tm, n // tn, t // tk),
        in_specs=[pl.BlockSpec((tk, tm), lambda i, j, kk: (kk, i)),
                  pl.BlockSpec((tk, tn), lambda i, j, kk: (kk, j))],
        out_specs=pl.BlockSpec((tm, tn), lambda i, j, kk: (i, j)),
        out_shape=jax.ShapeDtypeStruct((m, n), F32),
        compiler_params=_params("parallel", "parallel", "arbitrary"),
    )(a, b)


def _rmsnorm_rows(tile, j, gain_ref):
    r = lax.rsqrt(jnp.mean(tile * tile, axis=-1, keepdims=True) + NORM_EPS)
    return tile * r * gain_ref[...]


def _rmsnorm_fwd(x, gain, *, name):
    t, d = x.shape
    tm = _tile(t, 512)

    def body(x_ref, g_ref, o_ref):
        xv = x_ref[...]
        r = lax.rsqrt(jnp.mean(xv * xv, axis=-1, keepdims=True) + NORM_EPS)
        o_ref[...] = (xv * r * g_ref[...]).astype(BF16)

    return pl.pallas_call(
        body, name=name, grid=(t // tm,),
        in_specs=[pl.BlockSpec((tm, d), lambda i: (i, 0)), pl.BlockSpec((1, d), lambda i: (0, 0))],
        out_specs=pl.BlockSpec((tm, d), lambda i: (i, 0)),
        out_shape=jax.ShapeDtypeStruct((t, d), BF16),
        compiler_params=_params("parallel"),
    )(x, gain)


def _rmsnorm_bwd(x, gain, dy, dres, *, name):
    t, d = x.shape
    tm = _tile(t, 512)

    def body(x_ref, g_ref, dy_ref, dres_ref, dx_ref, dg_ref):
        xv = x_ref[...]
        r = lax.rsqrt(jnp.mean(xv * xv, axis=-1, keepdims=True) + NORM_EPS)
        xh = xv * r
        dyv = dy_ref[...]
        dxh = dyv * g_ref[...]
        mean = jnp.mean(dxh * xh, axis=-1, keepdims=True)
        dx_ref[...] = dres_ref[...] + r * (dxh - xh * mean)
        part = jnp.sum(dyv * xh, axis=0, keepdims=True)

        @pl.when(pl.program_id(0) == 0)
        def _():
            dg_ref[...] = part

        @pl.when(pl.program_id(0) > 0)
        def _():
            dg_ref[...] += part

    row = pl.BlockSpec((tm, d), lambda i: (i, 0))
    vec = pl.BlockSpec((1, d), lambda i: (0, 0))
    return pl.pallas_call(
        body, name=name, grid=(t // tm,),
        in_specs=[row, vec, row, row], out_specs=[row, vec],
        out_shape=[jax.ShapeDtypeStruct((t, d), F32), jax.ShapeDtypeStruct((1, d), F32)],
        compiler_params=_params("arbitrary"),
    )(x, gain, dy, dres)


def _loss_head(y, target):
    t, d = y.shape
    tm = _tile(t, 512)
    steps = t // tm

    def body(y_ref, t_ref, dy_ref, l_ref, acc_ref):
        e = y_ref[...] - t_ref[...]
        dy_ref[...] = e * (1.0 / d)
        part = jnp.sum(e * e, axis=0, keepdims=True)

        @pl.when(pl.program_id(0) == 0)
        def _():
            acc_ref[...] = part

        @pl.when(pl.program_id(0) > 0)
        def _():
            acc_ref[...] += part

        @pl.when(pl.program_id(0) == steps - 1)
        def _():
            l_ref[...] = jnp.full((1, LANES), (0.5 / d), F32) * jnp.sum(acc_ref[...])

    row = pl.BlockSpec((tm, d), lambda i: (i, 0))
    return pl.pallas_call(
        body, name="loss_head", grid=(steps,),
        in_specs=[row, row], out_specs=[row, pl.BlockSpec((1, LANES), lambda i: (0, 0))],
        out_shape=[jax.ShapeDtypeStruct((t, d), F32), jax.ShapeDtypeStruct((1, LANES), F32)],
        scratch_shapes=[pltpu.VMEM((1, d), F32)],
        compiler_params=_params("arbitrary"),
    )(y, target)


def _swiglu_fwd(h, w_gate_t, w_up_t, *, name, side=None):
    t, d = h.shape
    f = w_gate_t.shape[0]
    tm, tn = _tile(t, 1024), _tile(f, 256)

    def body(h_ref, wg_ref, wu_ref, g_ref, u_ref, a_ref):
        hv = h_ref[...]
        g = _dot_nt(hv, wg_ref[...])
        u = _dot_nt(hv, wu_ref[...])
        g_ref[...] = g.astype(BF16)
        u_ref[...] = u.astype(BF16)
        a_ref[...] = (g * _sigmoid(g) * u).astype(BF16)

    wspec = pl.BlockSpec((tn, d), lambda i, j: (j, 0))
    ospec = pl.BlockSpec((tm, tn), lambda i, j: (i, j))
    return _call(
        body, side, name=name, grid=(t // tm, f // tn),
        in_specs=[pl.BlockSpec((tm, d), lambda i, j: (i, 0)), wspec, wspec],
        out_specs=[ospec, ospec, ospec],
        out_shape=[jax.ShapeDtypeStruct((t, f), BF16), jax.ShapeDtypeStruct((t, f), BF16),
                   jax.ShapeDtypeStruct((t, f), BF16)],
        scratch_shapes=[], semantics=("parallel", "parallel"), args=(h, w_gate_t, w_up_t),
    )


def _swiglu_bwd(dx, w_down, g, u, *, name, side=None):
    t, d = dx.shape
    f = w_down.shape[0]
    tm, tn = _tile(t, 1024), _tile(f, 256)

    def body(dx_ref, wd_ref, g_ref, u_ref, dg_ref, du_ref):
        dact = _dot_nt(dx_ref[...].astype(BF16), wd_ref[...])
        gv, uv = g_ref[...].astype(F32), u_ref[...].astype(F32)
        sg = _sigmoid(gv)
        dg_ref[...] = (dact * uv * sg * (1.0 + gv * (1.0 - sg))).astype(BF16)
        du_ref[...] = (dact * gv * sg).astype(BF16)

    ospec = pl.BlockSpec((tm, tn), lambda i, j: (i, j))
    return _call(
        body, side, name=name, grid=(t // tm, f // tn),
        in_specs=[pl.BlockSpec((tm, d), lambda i, j: (i, 0)), pl.BlockSpec((tn, d), lambda i, j: (j, 0)),
                  ospec, ospec],
        out_specs=[ospec, ospec],
        out_shape=[jax.ShapeDtypeStruct((t, f), BF16), jax.ShapeDtypeStruct((t, f), BF16)],
        scratch_shapes=[], semantics=("parallel", "parallel"), args=(dx, w_down, g, u),
    )


def _ple_fwd(x, p, w_gate, w_proj_t, *, name):
    t, d = x.shape
    e = p.shape[1]
    tm, tn = _tile(t, 1024), _tile(d, 512)

    def body(xf_ref, xr_ref, p_ref, wg_ref, wp_ref, o_ref):
        s = _dot(xf_ref[...].astype(BF16), wg_ref[...])
        ple = _dot_nt(p_ref[...].astype(BF16), wp_ref[...])
        o_ref[...] = xr_ref[...] + _sigmoid(s) * ple

    return pl.pallas_call(
        body, name=name, grid=(t // tm, d // tn),
        in_specs=[pl.BlockSpec((tm, d), lambda i, j: (i, 0)), pl.BlockSpec((tm, tn), lambda i, j: (i, j)),
                  pl.BlockSpec((tm, e), lambda i, j: (i, 0)), pl.BlockSpec((d, tn), lambda i, j: (0, j)),
                  pl.BlockSpec((tn, e), lambda i, j: (j, 0))],
        out_specs=pl.BlockSpec((tm, tn), lambda i, j: (i, j)),
        out_shape=jax.ShapeDtypeStruct((t, d), F32),
        compiler_params=_params("parallel", "parallel"),
    )(x, x, p, w_gate, w_proj_t)


def _ple_bwd(x, p, w_gate, w_proj_t, dout, *, name):
    t, d = x.shape
    e = p.shape[1]
    tm, tn = _tile(t, 1024), _tile(d, 512)

    def body(xf_ref, p_ref, wg_ref, wp_ref, do_ref, ds_ref, dple_ref):
        s = _dot(xf_ref[...].astype(BF16), wg_ref[...])
        ple = _dot_nt(p_ref[...].astype(BF16), wp_ref[...])
        gate = _sigmoid(s)
        dov = do_ref[...]
        dple_ref[...] = (dov * gate).astype(BF16)
        ds_ref[...] = (dov * ple * gate * (1.0 - gate)).astype(BF16)

    ospec = pl.BlockSpec((tm, tn), lambda i, j: (i, j))
    return pl.pallas_call(
        body, name=name, grid=(t // tm, d // tn),
        in_specs=[pl.BlockSpec((tm, d), lambda i, j: (i, 0)), pl.BlockSpec((tm, e), lambda i, j: (i, 0)),
                  pl.BlockSpec((d, tn), lambda i, j: (0, j)), pl.BlockSpec((tn, e), lambda i, j: (j, 0)), ospec],
        out_specs=[ospec, ospec],
        out_shape=[jax.ShapeDtypeStruct((t, d), BF16), jax.ShapeDtypeStruct((t, d), BF16)],
        compiler_params=_params("parallel", "parallel"),
    )(x, p, w_gate, w_proj_t, dout)


CONV_TIME_TILE = 256
CONV_HALO = 8


def _conv_taps(ext, w):
    acc = ext[CONV_HALO:, :] * w[CONV_WIDTH - 1:CONV_WIDTH, :]
    shifted = [ext[CONV_HALO:, :]]
    for j in range(1, CONV_WIDTH):
        sh = pltpu.roll(ext, j, 0)[CONV_HALO:, :]
        shifted.append(sh)
        acc = acc + sh * w[CONV_WIDTH - 1 - j:CONV_WIDTH - j, :]
    return acc, shifted


def _conv_fwd(u, w, b, side=None):
    t, c = u.shape
    tc = _tile(c, 256)
    tt = CONV_TIME_TILE

    def body(u_ref, w_ref, b_ref, o_ref):
        wv, bv = w_ref[...], b_ref[...]

        def tile(start, ext):
            pre = _conv_taps(ext, wv)[0] + bv
            o_ref[pl.ds(start, tt), :] = pre * _sigmoid(pre)

        tile(0, jnp.concatenate([jnp.zeros((CONV_HALO, tc), F32), u_ref[0:tt, :]], axis=0))

        def loop(i, carry):
            start = pl.multiple_of(i * tt, tt)
            tile(start, u_ref[pl.ds(start - CONV_HALO, tt + CONV_HALO), :])
            return carry

        lax.fori_loop(1, t // tt, loop, 0)

    col = pl.BlockSpec((t, tc), lambda j: (0, j))
    return _call(
        body, side, name="conv_fwd", grid=(c // tc,),
        in_specs=[col, pl.BlockSpec((CONV_WIDTH, tc), lambda j: (0, j)), pl.BlockSpec((1, tc), lambda j: (0, j))],
        out_specs=[col], out_shape=[jax.ShapeDtypeStruct((t, c), F32)],
        scratch_shapes=[], semantics=("parallel",), args=(u, w, b),
    )[0]


def _conv_bwd(u, w, b, dact, side=None):
    t, c = u.shape
    tc = _tile(c, 256)
    tt = CONV_TIME_TILE

    def body(u_ref, w_ref, b_ref, da_ref, du_ref, dw_ref, db_ref, dpre_ref):
        wv, bv = w_ref[...], b_ref[...]

        def tile(start, ext, sums):
            acc, shifted = _conv_taps(ext, wv)
            pre = acc + bv
            sg = _sigmoid(pre)
            dpre = da_ref[pl.ds(start, tt), :] * (sg * (1.0 + pre * (1.0 - sg)))
            dpre_ref[pl.ds(start, tt), :] = dpre
            new = [sums[0] + jnp.sum(dpre, axis=0, keepdims=True)]
            for j in range(CONV_WIDTH):
                new.append(sums[1 + j] + jnp.sum(dpre * shifted[j], axis=0, keepdims=True))
            return tuple(new)

        zero = jnp.zeros((1, tc), F32)
        sums = tile(0, jnp.concatenate([jnp.zeros((CONV_HALO, tc), F32), u_ref[0:tt, :]], axis=0),
                    (zero,) * (1 + CONV_WIDTH))

        def loop(i, sums):
            start = pl.multiple_of(i * tt, tt)
            return tile(start, u_ref[pl.ds(start - CONV_HALO, tt + CONV_HALO), :], sums)

        sums = lax.fori_loop(1, t // tt, loop, sums)
        db_ref[...] = sums[0]
        dw_ref[...] = jnp.concatenate([sums[1 + (CONV_WIDTH - 1 - k)] for k in range(CONV_WIDTH)], axis=0)
        dpre_ref[pl.ds(t, CONV_HALO), :] = jnp.zeros((CONV_HALO, tc), F32)

        def loop2(i, carry):
            start = pl.multiple_of(i * tt, tt)
            ext = dpre_ref[pl.ds(start, tt + CONV_HALO), :]
            acc = ext[0:tt, :] * wv[CONV_WIDTH - 1:CONV_WIDTH, :]
            for j in range(1, CONV_WIDTH):
                acc = acc + pltpu.roll(ext, tt + CONV_HALO - j, 0)[0:tt, :] * wv[CONV_WIDTH - 1 - j:CONV_WIDTH - j, :]
            du_ref[pl.ds(start, tt), :] = acc.astype(BF16)
            return carry

        lax.fori_loop(0, t // tt, loop2, 0)

    col = pl.BlockSpec((t, tc), lambda j: (0, j))
    return _call(
        body, side, name="conv_bwd", grid=(c // tc,),
        in_specs=[col, pl.BlockSpec((CONV_WIDTH, tc), lambda j: (0, j)), pl.BlockSpec((1, tc), lambda j: (0, j)), col],
        out_specs=[col, pl.BlockSpec((CONV_WIDTH, tc), lambda j: (0, j)), pl.BlockSpec((1, tc), lambda j: (0, j))],
        out_shape=[jax.ShapeDtypeStruct((t, c), BF16), jax.ShapeDtypeStruct((CONV_WIDTH, c), F32),
                   jax.ShapeDtypeStruct((1, c), F32)],
        scratch_shapes=[pltpu.VMEM((t + CONV_HALO, tc), F32)],
        semantics=("parallel",), args=(u, w, b, dact),
    )


def _softplus(v):
    e = jnp.exp(-jnp.abs(v))
    w = 1.0 + e
    log1p = jnp.where(w == 1.0, e, jnp.log(w) * (e / jnp.where(w == 1.0, 1.0, w - 1.0)))
    return jnp.maximum(v, 0.0) + log1p


def _split3(z):
    hi = z.astype(BF16)
    rest = z - hi.astype(F32)
    mid = rest.astype(BF16)
    return hi, mid, (rest - mid.astype(F32)).astype(BF16)


def _select_dot(z, ones):
    return sum(_dot(term, ones) for term in _split3(z))


def _ssd_prep_fwd(dt_raw, dt_bias, a_log):
    t = dt_raw.shape[0]
    cl = SSD_CHUNK

    def body(r_ref, b_ref, al_ref, acs_ref, dt_rep_ref, acs_rep_ref):
        dt = _softplus(r_ref[...] + b_ref[...])
        adt = dt * (-jnp.exp(al_ref[...]))
        li = lax.broadcasted_iota(jnp.int32, (cl, cl), 0)
        si = lax.broadcasted_iota(jnp.int32, (cl, cl), 1)
        tri = (si <= li).astype(F32)
        acs = jnp.dot(tri, adt, preferred_element_type=F32, precision=HIGHEST)
        acs_ref[...] = acs
        head = lax.broadcasted_iota(jnp.int32, (LANES, D_INNER), 0)
        chan = lax.broadcasted_iota(jnp.int32, (LANES, D_INNER), 1) // SSM_HEAD_DIM
        spread = (head == chan).astype(BF16)
        dt_rep_ref[...] = _select_dot(dt, spread)
        acs_rep_ref[...] = _select_dot(acs, spread)

    row = pl.BlockSpec((cl, LANES), lambda i: (i, 0))
    wide = pl.BlockSpec((cl, D_INNER), lambda i: (i, 0))
    vec = pl.BlockSpec((1, LANES), lambda i: (0, 0))
    return pl.pallas_call(
        body, name="ssd_prep_fwd", grid=(t // cl,),
        in_specs=[row, vec, vec], out_specs=[row, wide, wide],
        out_shape=[jax.ShapeDtypeStruct((t, LANES), F32), jax.ShapeDtypeStruct((t, D_INNER), F32),
                   jax.ShapeDtypeStruct((t, D_INNER), F32)],
        compiler_params=_params("parallel"),
    )(dt_raw, dt_bias, a_log)


def _ssd_prep_bwd(dt_raw, dt_bias, ddt):
    t = dt_raw.shape[0]
    tm = _tile(t, 512)

    def body(r_ref, b_ref, d_ref, o_ref, db_ref):
        g = d_ref[...] * _sigmoid(r_ref[...] + b_ref[...])
        o_ref[...] = g.astype(BF16)
        part = jnp.sum(g, axis=0, keepdims=True)

        @pl.when(pl.program_id(0) == 0)
        def _():
            db_ref[...] = part

        @pl.when(pl.program_id(0) > 0)
        def _():
            db_ref[...] += part

    row = pl.BlockSpec((tm, LANES), lambda i: (i, 0))
    vec = pl.BlockSpec((1, LANES), lambda i: (0, 0))
    return pl.pallas_call(
        body, name="ssd_prep_bwd", grid=(t // tm,),
        in_specs=[row, vec, row], out_specs=[row, vec],
        out_shape=[jax.ShapeDtypeStruct((t, LANES), BF16), jax.ShapeDtypeStruct((1, LANES), F32)],
        compiler_params=_params("arbitrary"),
    )(dt_raw, dt_bias, ddt)


GROUP_W = D_INNER // SSM_GROUPS
PAIRS_PER_GROUP = GROUP_W // LANES


def _head_cols(acs_pair, lt64):
    rolled = pltpu.roll(acs_pair, ATT_HEAD_DIM, 1)
    return jnp.where(lt64, acs_pair, rolled), jnp.where(lt64, rolled, acs_pair)


def _ssd_fwd(xbc, dt_rep, acs_rep, acs_t, dskip_rep, side=None):
    t = xbc.shape[0]
    cl = SSD_CHUNK
    nc = t // cl

    def body(xbc_ref, dt_ref, acs_ref, acst_ref, dskip_ref, y_ref, hin_ref, state_ref):
        @pl.when(pl.program_id(0) == 0)
        def _():
            state_ref[...] = jnp.zeros_like(state_ref)

        lt64 = _lane_lt64(cl)
        li = lax.broadcasted_iota(jnp.int32, (cl, cl), 0)
        si = lax.broadcasted_iota(jnp.int32, (cl, cl), 1)
        causal = li >= si
        hin_ref[...] = state_ref[...]
        for g in range(SSM_GROUPS):
            gsl = slice(g * GROUP_W, (g + 1) * GROUP_W)
            xg = xbc_ref[:, gsl]
            bg = xbc_ref[:, D_INNER + g * SSM_STATE:D_INNER + (g + 1) * SSM_STATE]
            cg = xbc_ref[:, D_INNER + SSM_GROUPS * SSM_STATE + g * SSM_STATE:
                         D_INNER + SSM_GROUPS * SSM_STATE + (g + 1) * SSM_STATE]
            acs = acs_ref[:, gsl]
            xdt = xg * dt_ref[:, gsl]
            atot = acs[cl - 1:cl, :]
            hin = state_ref[:, gsl]
            cgb = cg.astype(BF16)
            gmat = _dot_nt(cgb, bg.astype(BF16))
            yoff = _dot(cgb, hin.astype(BF16)) * jnp.exp(acs)
            snew = _dot(bg.T.astype(BF16), (xdt * jnp.exp(atot - acs)).astype(BF16))
            state_ref[:, gsl] = hin * jnp.exp(atot) + snew
            xdtb = xdt.astype(BF16)
            for pr in range(PAIRS_PER_GROUP):
                psl = slice(pr * LANES, (pr + 1) * LANES)
                cols = _head_cols(acs[:, psl], lt64)
                xp = xdtb[:, psl]
                ys = []
                for hh in range(2):
                    h = (g * PAIRS_PER_GROUP + pr) * 2 + hh
                    seg = cols[hh] - acst_ref[h:h + 1, :]
                    lm = jnp.exp(jnp.where(causal, seg, NEG_BIG))
                    ys.append(_dot((gmat * lm).astype(BF16), xp))
                ydiag = jnp.where(lt64, ys[0], ys[1])
                osl = slice(g * GROUP_W + pr * LANES, g * GROUP_W + (pr + 1) * LANES)
                y_ref[:, osl] = ydiag + yoff[:, psl] + xg[:, psl] * dskip_ref[:, osl]

    row = lambda w: pl.BlockSpec((cl, w), lambda c: (c, 0))
    return _call(
        body, side, name="ssd_fwd", grid=(nc,),
        in_specs=[row(CONV_DIM), row(D_INNER), row(D_INNER),
                  pl.BlockSpec((SSM_HEADS, cl), lambda c: (0, c)), pl.BlockSpec((1, D_INNER), lambda c: (0, 0))],
        out_specs=[row(D_INNER), pl.BlockSpec((None, SSM_STATE, D_INNER), lambda c: (c, 0, 0))],
        out_shape=[jax.ShapeDtypeStruct((t, D_INNER), F32), jax.ShapeDtypeStruct((nc, SSM_STATE, D_INNER), F32)],
        scratch_shapes=[pltpu.VMEM((SSM_STATE, D_INNER), F32)],
        semantics=("arbitrary",), args=(xbc, dt_rep, acs_rep, acs_t, dskip_rep),
    )


def _ssd_bwd(xbc, dt_rep, acs_rep, acs_t, dskip_rep, a_rep, hin_all, dy, side=None):
    t = xbc.shape[0]
    cl = SSD_CHUNK
    nc = t // cl

    def body(xbc_ref, dt_ref, acs_ref, acst_ref, dskip_ref, a_ref, hin_ref, dy_ref,
             dxbc_ref, ddt_ref, da_ref, dds_ref, dstate_ref, dacs_ref, dxs_ref):
        step = pl.program_id(0)

        @pl.when(step == 0)
        def _():
            dstate_ref[...] = jnp.zeros_like(dstate_ref)
            da_ref[...] = jnp.zeros_like(da_ref)
            dds_ref[...] = jnp.zeros_like(dds_ref)

        bd = _head_block_diag()
        lt64 = _lane_lt64(cl)
        li = lax.broadcasted_iota(jnp.int32, (cl, cl), 0)
        si = lax.broadcasted_iota(jnp.int32, (cl, cl), 1)
        lower = li >= si
        upper = si >= li
        last_row = lax.broadcasted_iota(jnp.int32, (cl, GROUP_W), 0) == cl - 1
        for g in range(SSM_GROUPS):
            gsl = slice(g * GROUP_W, (g + 1) * GROUP_W)
            bsl = slice(D_INNER + g * SSM_STATE, D_INNER + (g + 1) * SSM_STATE)
            csl = slice(D_INNER + SSM_GROUPS * SSM_STATE + g * SSM_STATE,
                        D_INNER + SSM_GROUPS * SSM_STATE + (g + 1) * SSM_STATE)
            xg = xbc_ref[:, gsl]
            bg = xbc_ref[:, bsl]
            cg = xbc_ref[:, csl]
            bgb, cgb = bg.astype(BF16), cg.astype(BF16)
            acs = acs_ref[:, gsl]
            xdt = xg * dt_ref[:, gsl]
            atot = acs[cl - 1:cl, :]
            eg = jnp.exp(acs)
            dk = jnp.exp(atot - acs)
            etot = jnp.exp(atot)
            hin = hin_ref[:, gsl]
            hinb = hin.astype(BF16)
            dh = dstate_ref[:, gsl]
            dhb = dh.astype(BF16)
            dyg = dy_ref[:, gsl]

            gmat = _dot_nt(cgb, bgb)
            gmat_t = _dot_nt(bgb, cgb)
            ch = _dot(cgb, hinb)
            dacs = _head_sums(dyg * ch * eg, bd)
            dye = (dyg * eg).astype(BF16)
            dc = _dot_nt(dye, hinb)
            dhin = _dot(cg.T.astype(BF16), dye)
            bdh = _dot(bgb, dhb)
            dxs = bdh * dk
            xdk = xdt * dk
            db = _dot_nt(xdk.astype(BF16), dhb)
            ddk = _head_sums(bdh * xdk, bd)
            dacs = dacs - ddk
            datot = jnp.sum(ddk, axis=0, keepdims=True) + etot * _head_sums(
                jnp.sum(dh * hin, axis=0, keepdims=True), bd)
            dacs = dacs + jnp.where(last_row, datot, 0.0)
            dstate_ref[:, gsl] = dh * etot + dhin

            xdtb = xdt.astype(BF16)
            dgsum = jnp.zeros((cl, cl), F32)
            dgsum_t = jnp.zeros((cl, cl), F32)
            for pr in range(PAIRS_PER_GROUP):
                psl = slice(pr * LANES, (pr + 1) * LANES)
                cols = _head_cols(acs[:, psl], lt64)
                xp = xdtb[:, psl]
                dyp = dyg[:, psl].astype(BF16)
                dx1, dac = [], []
                for hh in range(2):
                    h = (g * PAIRS_PER_GROUP + pr) * 2 + hh
                    mine = lt64 if hh == 0 else jnp.logical_not(lt64)
                    row = acst_ref[h:h + 1, :]
                    lm = jnp.exp(jnp.where(lower, cols[hh] - row, NEG_BIG))
                    lm_t = jnp.exp(jnp.where(upper, row - cols[hh], NEG_BIG))
                    dyh = jnp.where(mine, dyp, jnp.zeros_like(dyp))
                    xh = jnp.where(mine, xp, jnp.zeros_like(xp))
                    dm = _dot_nt(dyh, xp)
                    dm_t = _dot_nt(xh, dyp)
                    m_t = gmat_t * lm_t
                    dx1.append(_dot(m_t.astype(BF16), dyp))
                    w = dm * (gmat * lm)
                    w_t = dm_t * m_t
                    dac.append(jnp.sum(w, axis=1, keepdims=True) - jnp.sum(w_t, axis=1, keepdims=True))
                    dgsum = dgsum + dm * lm
                    dgsum_t = dgsum_t + dm_t * lm_t
                osl = slice(g * GROUP_W + pr * LANES, g * GROUP_W + (pr + 1) * LANES)
                dxs_ref[:, osl] = dxs[:, psl] + jnp.where(lt64, dx1[0], dx1[1])
                dacs_ref[:, osl] = dacs[:, psl] + jnp.where(lt64, jnp.broadcast_to(dac[0], (cl, LANES)),
                                                             jnp.broadcast_to(dac[1], (cl, LANES)))
            dxbc_ref[:, csl] = dc + _dot(dgsum.astype(BF16), bgb)
            dxbc_ref[:, bsl] = db + _dot(dgsum_t.astype(BF16), cgb)

        dadt = _split_dot(upper.astype(BF16), dacs_ref[...])
        xall = xbc_ref[:, 0:D_INNER]
        dtall = dt_ref[...]
        dxsall = dxs_ref[...]
        dyall = dy_ref[...]
        ddt_rep = dadt * a_ref[...] + _head_sums(dxsall * xall, bd)
        chan = lax.broadcasted_iota(jnp.int32, (D_INNER, LANES), 0)
        head = lax.broadcasted_iota(jnp.int32, (D_INNER, LANES), 1)
        ddt_ref[...] = _select_dot(ddt_rep, (chan == head * SSM_HEAD_DIM).astype(BF16))
        dxbc_ref[:, 0:D_INNER] = dxsall * dtall + dyall * dskip_ref[...]
        da_ref[...] += jnp.sum(dadt * dtall, axis=0, keepdims=True)
        dds_ref[...] += jnp.sum(dyall * xall, axis=0, keepdims=True)

        @pl.when(step == nc - 1)
        def _():
            dds_ref[...] = _head_sums(dds_ref[...], bd)

    row = lambda w: pl.BlockSpec((cl, w), lambda c: (nc - 1 - c, 0))
    vec = pl.BlockSpec((1, D_INNER), lambda c: (0, 0))
    return _call(
        body, side, name="ssd_bwd", grid=(nc,),
        in_specs=[row(CONV_DIM), row(D_INNER), row(D_INNER),
                  pl.BlockSpec((SSM_HEADS, cl), lambda c: (0, nc - 1 - c)), vec, vec,
                  pl.BlockSpec((None, SSM_STATE, D_INNER), lambda c: (nc - 1 - c, 0, 0)), row(D_INNER)],
        out_specs=[row(CONV_DIM), row(LANES), vec, vec],
        out_shape=[jax.ShapeDtypeStruct((t, CONV_DIM), F32), jax.ShapeDtypeStruct((t, LANES), F32),
                   jax.ShapeDtypeStruct((1, D_INNER), F32), jax.ShapeDtypeStruct((1, D_INNER), F32)],
        scratch_shapes=[pltpu.VMEM((SSM_STATE, D_INNER), F32), pltpu.VMEM((cl, D_INNER), F32),
                        pltpu.VMEM((cl, D_INNER), F32)],
        semantics=("arbitrary",), args=(xbc, dt_rep, acs_rep, acs_t, dskip_rep, a_rep, hin_all, dy),
    )


def _gate_norm_fwd(y, z, w):
    t, c = y.shape
    tm = _tile(t, 256)

    def body(y_ref, z_ref, w_ref, o_ref):
        for g in range(SSM_GROUPS):
            gsl = slice(g * GROUP_W, (g + 1) * GROUP_W)
            zv = z_ref[:, gsl]
            v = y_ref[:, gsl] * (zv * _sigmoid(zv))
            r = lax.rsqrt(jnp.mean(v * v, axis=-1, keepdims=True) + NORM_EPS)
            o_ref[:, gsl] = (v * r * w_ref[:, gsl]).astype(BF16)

    row = pl.BlockSpec((tm, c), lambda i: (i, 0))
    return pl.pallas_call(
        body, name="gate_norm_fwd", grid=(t // tm,),
        in_specs=[row, row, pl.BlockSpec((1, c), lambda i: (0, 0))], out_specs=row,
        out_shape=jax.ShapeDtypeStruct((t, c), BF16),
        compiler_params=_params("parallel"),
    )(y, z, w)


def _gate_norm_bwd(y, z, w, dout, side=None):
    t, c = y.shape
    tm = _tile(t, 256)

    def body(y_ref, z_ref, w_ref, do_ref, dy_ref, dz_ref, dw_ref):
        @pl.when(pl.program_id(0) == 0)
        def _():
            dw_ref[...] = jnp.zeros_like(dw_ref)

        for g in range(SSM_GROUPS):
            gsl = slice(g * GROUP_W, (g + 1) * GROUP_W)
            zv, yv, dov = z_ref[:, gsl], y_ref[:, gsl], do_ref[:, gsl]
            sg = _sigmoid(zv)
            sz = zv * sg
            v = yv * sz
            r = lax.rsqrt(jnp.mean(v * v, axis=-1, keepdims=True) + NORM_EPS)
            vh = v * r
            dvh = dov * w_ref[:, gsl]
            mean = jnp.mean(dvh * vh, axis=-1, keepdims=True)
            dv = r * (dvh - vh * mean)
            dy_ref[:, gsl] = dv * sz
            dz_ref[:, gsl] = (dv * yv * (sg * (1.0 + zv * (1.0 - sg)))).astype(BF16)
            dw_ref[:, gsl] += jnp.sum(dov * vh, axis=0, keepdims=True)

    row = pl.BlockSpec((tm, c), lambda i: (i, 0))
    vec = pl.BlockSpec((1, c), lambda i: (0, 0))
    return _call(
        body, side, name="gate_norm_bwd", grid=(t // tm,),
        in_specs=[row, row, vec, row], out_specs=[row, row, vec],
        out_shape=[jax.ShapeDtypeStruct((t, c), F32), jax.ShapeDtypeStruct((t, c), BF16),
                   jax.ShapeDtypeStruct((1, c), F32)],
        scratch_shapes=[], semantics=("arbitrary",), args=(y, z, w, dout),
    )


ATT_W = ATT_HEADS * ATT_HEAD_DIM
N_QKV_BLOCKS = 9
ATT_SCALE = 1.0 / math.sqrt(ATT_HEAD_DIM)


def _head_rmsnorm(x, gain, bd):
    ms = _head_sums(x * x, bd, terms=1) * (1.0 / ATT_HEAD_DIM)
    return x * lax.rsqrt(ms + NORM_EPS) * gain


def _class_rows(ref, blk, r, dil):
    span = ATT_BLOCK * dil
    sub = ref.at[pl.ds(pl.multiple_of(blk * span, span), span), :]
    return sub[...] if dil == 1 else sub[pl.ds(r, ATT_BLOCK, stride=dil), :]


def _store_class_rows(ref, blk, r, dil, val):
    span = ATT_BLOCK * dil
    sub = ref.at[pl.ds(pl.multiple_of(blk * span, span), span), :]
    if dil == 1:
        sub[...] = val
    else:
        sub[pl.ds(r, ATT_BLOCK, stride=dil), :] = val


PAIRS = ATT_HEADS // 2


def _pair_col(g, j):
    return lambda pair: (0, (g * 3 + j) * PAIRS + pair)


def _pair_slopes(pair):
    steps = jnp.full((1, 2 * ATT_BLOCK), 2 * pair + 1, jnp.int32).astype(F32)
    first = jnp.exp(steps * (-0.5 * math.log(2.0)))
    return first, first * (2.0 ** -0.5)


NORM_ROWS = 512


ROW_SLICES = 4
SLICE_ROWS = 2 * ATT_BLOCK // ROW_SLICES


def _fill_band_bias(bias_ref, pair, dil, transposed):
    bq = ATT_BLOCK
    a = lax.broadcasted_iota(jnp.int32, (2 * bq, 2 * bq), 0) % bq
    b = lax.broadcasted_iota(jnp.int32, (2 * bq, 2 * bq), 1)
    dist = (b - a) if transposed else (a + bq - b)
    in_band = (dist >= 0) & (dist <= bq)
    s0, s1 = _pair_slopes(pair)
    first_head = lax.broadcasted_iota(jnp.int32, (2 * bq, 2 * bq), 0) < bq
    bias = jnp.where(first_head, s0, s1) * (dist.astype(F32) * float(dil))
    inside = (b < bq) if transposed else (b >= bq)
    bias_ref[1] = jnp.where(in_band, bias, -NEG_BIG)
    bias_ref[0] = jnp.where(in_band & inside, bias, -NEG_BIG)


def _row_slices():
    return [slice(i * SLICE_ROWS, (i + 1) * SLICE_ROWS) for i in range(ROW_SLICES)]


def _stack_heads(tile):
    rows = lax.broadcasted_iota(jnp.int32, (2 * ATT_BLOCK, LANES), 0) < ATT_BLOCK
    lanes = lax.broadcasted_iota(jnp.int32, (2 * ATT_BLOCK, LANES), 1) < ATT_HEAD_DIM
    both = jnp.concatenate([tile, tile], axis=0)
    return jnp.where(rows == lanes, both, jnp.zeros_like(both))


def _unstack_heads(stacked, lt64):
    return jnp.where(lt64, stacked[:ATT_BLOCK], stacked[ATT_BLOCK:])


ITEMS_PER_PASS = 4


def _item_loop(nb, dil, work):
    if dil == 1:
        def trip(i, carry):
            work([(i * ITEMS_PER_PASS + b, 0) for b in range(ITEMS_PER_PASS)])
            return carry

        lax.fori_loop(0, nb // ITEMS_PER_PASS, trip, 0)
    else:
        def trip(n, carry):
            for r0 in range(0, dil, ITEMS_PER_PASS):
                work([(n, r0 + j) for j in range(ITEMS_PER_PASS)])
            return carry

        lax.fori_loop(0, nb, trip, 0)


def _qk_normalised(tile, j, gq_ref, gk_ref):
    kind = (j // (ATT_W // tile.shape[1])) % 3
    gain = jnp.where(kind == 0, gq_ref[...] * ATT_SCALE, gk_ref[...])
    return jnp.where(kind == 2, tile, _head_rmsnorm(tile, gain, _head_block_diag()))


def _attn_fwd(qkn, g, dil):
    t = qkn.shape[0]
    nb = t // dil // ATT_BLOCK
    bq = ATT_BLOCK

    def body(qn_ref, kn_ref, v_ref, o_ref, l_ref, bias_ref):
        _fill_band_bias(bias_ref, pl.program_id(0), dil, False)
        lt64 = _lane_lt64(bq)

        def work(items):
            scores, values, probs = [], [], []
            for n, r in items:
                prev = jnp.maximum(n - 1, 0)
                q2 = _stack_heads(_class_rows(qn_ref, n, r, dil).astype(BF16))
                kcat = jnp.concatenate([_class_rows(kn_ref, prev, r, dil), _class_rows(kn_ref, n, r, dil)],
                                       axis=0).astype(BF16)
                values.append(jnp.concatenate([_class_rows(v_ref, prev, r, dil), _class_rows(v_ref, n, r, dil)],
                                              axis=0).astype(BF16))
                scores.append(_dot_nt(q2, kcat))
            for (n, r), sc in zip(items, scores):
                bias = bias_ref.at[jnp.minimum(n, 1)]
                ps, inv, lses = [], [], []
                for rows in _row_slices():
                    s = sc[rows] - bias[rows, :]
                    m = jnp.max(s, axis=1, keepdims=True)
                    p = jnp.exp(s - m)
                    l = jnp.sum(p, axis=1, keepdims=True)
                    ps.append(p.astype(BF16))
                    inv.append(jnp.broadcast_to(1.0 / l, (SLICE_ROWS, LANES)))
                    lses.append(jnp.broadcast_to(m + jnp.log(l), (SLICE_ROWS, LANES)))
                probs.append((jnp.concatenate(ps, axis=0), jnp.concatenate(inv, axis=0)))
                _store_class_rows(l_ref, n, r, dil, _unstack_heads(jnp.concatenate(lses, axis=0), lt64))
            for (n, r), (p, inv), vcat in zip(items, probs, values):
                _store_class_rows(o_ref, n, r, dil, _unstack_heads(_dot(p, vcat) * inv, lt64))

        _item_loop(nb, dil, work)

    col = lambda j: pl.BlockSpec((t, LANES), _pair_col(g, j))
    out = pl.BlockSpec((t, LANES), lambda pair: (0, pair))
    return pl.pallas_call(
        body, name=f"attn_fwd_g{g}", grid=(PAIRS,),
        in_specs=[col(0), col(1), col(2)], out_specs=[out, out],
        out_shape=[jax.ShapeDtypeStruct((t, ATT_W), F32), jax.ShapeDtypeStruct((t, ATT_W), F32)],
        scratch_shapes=[pltpu.VMEM((2, 2 * bq, 2 * bq), F32)],
        compiler_params=_params("parallel"),
    )(qkn, qkn, qkn)


def _one_per_head(rep):
    chan = lax.broadcasted_iota(jnp.int32, (ATT_W, LANES), 0)
    head = lax.broadcasted_iota(jnp.int32, (ATT_W, LANES), 1)
    return _select_dot(rep, (chan == head * ATT_HEAD_DIM).astype(BF16))


def _attn_combine_fwd(outs, lses):
    t = outs[0].shape[0]
    tm = _tile(t, 256)

    def body(o0, o1, o2, l0, l1, l2, ob_ref, of_ref, lt_ref, lc_ref):
        a, b, c = l0[...], l1[...], l2[...]
        m = jnp.maximum(jnp.maximum(a, b), c)
        ea, eb, ec = jnp.exp(a - m), jnp.exp(b - m), jnp.exp(c - m)
        ssum = ea + eb + ec
        o = (ea * o0[...] + eb * o1[...] + ec * o2[...]) / ssum
        ob_ref[...] = o.astype(BF16)
        of_ref[...] = o
        lse = m + jnp.log(ssum)
        lt_ref[...] = lse
        lc_ref[...] = _one_per_head(lse)

    row = pl.BlockSpec((tm, ATT_W), lambda i: (i, 0))
    return pl.pallas_call(
        body, name="attn_combine_fwd", grid=(t // tm,),
        in_specs=[row] * 6, out_specs=[row] * 3 + [pl.BlockSpec((tm, LANES), lambda i: (i, 0))],
        out_shape=[jax.ShapeDtypeStruct((t, ATT_W), BF16), jax.ShapeDtypeStruct((t, ATT_W), F32),
                   jax.ShapeDtypeStruct((t, ATT_W), F32), jax.ShapeDtypeStruct((t, LANES), F32)],
        compiler_params=_params("parallel"),
    )(*outs, *lses)


def _attn_combine_bwd(do, o):
    t = do.shape[0]
    tm = _tile(t, 256)

    def body(do_ref, o_ref, dl_ref, dc_ref):
        dl = _head_sums(do_ref[...] * o_ref[...], _head_block_diag())
        dl_ref[...] = dl
        dc_ref[...] = _one_per_head(dl)

    row = pl.BlockSpec((tm, ATT_W), lambda i: (i, 0))
    return pl.pallas_call(
        body, name="attn_combine_bwd", grid=(t // tm,),
        in_specs=[row, row], out_specs=[row, pl.BlockSpec((tm, LANES), lambda i: (i, 0))],
        out_shape=[jax.ShapeDtypeStruct((t, ATT_W), F32), jax.ShapeDtypeStruct((t, LANES), F32)],
        compiler_params=_params("parallel"),
    )(do, o)


def _head_rmsnorm_bwd(x_ref, dy_ref, gain_ref, dx_ref, dgain_ref):
    bd = _head_block_diag()
    gain = gain_ref[...]

    def step(i, acc):
        rows = pl.ds(pl.multiple_of(i * NORM_ROWS, NORM_ROWS), NORM_ROWS)
        x, dy = x_ref[rows, :], dy_ref[rows, :]
        r = lax.rsqrt(_head_sums(x * x, bd, terms=1) * (1.0 / ATT_HEAD_DIM) + NORM_EPS)
        xh = x * r
        dxh = dy * gain
        mean = _head_sums(dxh * xh, bd, terms=1) * (1.0 / ATT_HEAD_DIM)
        dx_ref[rows, :] = (r * (dxh - xh * mean)).astype(BF16)
        return acc + jnp.sum(dy * xh, axis=0, keepdims=True)

    acc = lax.fori_loop(0, x_ref.shape[0] // NORM_ROWS, step, jnp.zeros((1, LANES), F32))
    dgain_ref[...] = jnp.broadcast_to(acc, dgain_ref.shape)


def _attn_bwd_dq(qkv, qkn, gq, do, l_rep, dl_rep, g, dil):
    t = qkv.shape[0]
    nb = t // dil // ATT_BLOCK
    bq = ATT_BLOCK

    def body(q_ref, qn_ref, kn_ref, v_ref, gq_ref, do_ref, l_ref, dl_ref, dx_ref, dgain_ref, bias_ref, dq_ref):
        _fill_band_bias(bias_ref, pl.program_id(0), dil, False)
        lt64 = _lane_lt64(bq)

        def per_row(tile):
            cols = _head_cols(tile, lt64)
            half = jnp.concatenate([cols[0], cols[1]], axis=0)
            return jnp.concatenate([half, half], axis=1)

        def work(items):
            products, keys, dscores = [], [], []
            for n, r in items:
                prev = jnp.maximum(n - 1, 0)
                q2 = _stack_heads(_class_rows(qn_ref, n, r, dil).astype(BF16))
                do2 = _stack_heads(_class_rows(do_ref, n, r, dil).astype(BF16))
                kcat = jnp.concatenate([_class_rows(kn_ref, prev, r, dil), _class_rows(kn_ref, n, r, dil)],
                                       axis=0).astype(BF16)
                vcat = jnp.concatenate([_class_rows(v_ref, prev, r, dil), _class_rows(v_ref, n, r, dil)],
                                       axis=0).astype(BF16)
                keys.append(kcat)
                products.append((_dot_nt(q2, kcat), _dot_nt(do2, vcat)))
            for (n, r), (scores, dps) in zip(items, products):
                bias = bias_ref.at[jnp.minimum(n, 1)]
                lse = per_row(_class_rows(l_ref, n, r, dil))
                dl = per_row(_class_rows(dl_ref, n, r, dil))
                dss = []
                for rows in _row_slices():
                    p = jnp.exp(scores[rows] - bias[rows, :] - lse[rows])
                    dss.append((p * (dps[rows] - dl[rows])).astype(BF16))
                dscores.append(jnp.concatenate(dss, axis=0))
            for (n, r), ds, kcat in zip(items, dscores, keys):
                _store_class_rows(dq_ref, n, r, dil, _unstack_heads(_dot(ds, kcat) * ATT_SCALE, lt64))

        _item_loop(nb, dil, work)
        _head_rmsnorm_bwd(q_ref, dq_ref, gq_ref, dx_ref, dgain_ref)

    col = lambda j: pl.BlockSpec((t, LANES), _pair_col(g, j))
    vec = pl.BlockSpec((1, LANES), lambda pair: (0, 0))
    tok = pl.BlockSpec((t, LANES), lambda pair: (0, pair))
    return pl.pallas_call(
        body, name=f"attn_bwd_dq_g{g}", grid=(PAIRS,),
        in_specs=[col(0), col(0), col(1), col(2), vec, tok, tok, tok],
        out_specs=[tok, pl.BlockSpec((None, 8, LANES), lambda pair: (pair, 0, 0))],
        out_shape=[jax.ShapeDtypeStruct((t, ATT_W), BF16), jax.ShapeDtypeStruct((PAIRS, 8, LANES), F32)],
        scratch_shapes=[pltpu.VMEM((2, 2 * bq, 2 * bq), F32), pltpu.VMEM((t, LANES), F32)],
        compiler_params=_params("parallel"),
    )(qkv, qkn, qkn, qkn, gq, do, l_rep, dl_rep)


def _attn_bwd_dkv(qkv, qkn, gk, do, l_row, dl_row, g, dil):
    t = qkv.shape[0]
    nb = t // dil // ATT_BLOCK
    bq = ATT_BLOCK

    def body(k_ref, qn_ref, kn_ref, v_ref, gk_ref, do_ref, l_ref, dl_ref, dkx_ref, dvx_ref, dgain_ref, bias_ref,
             dk_ref, dv_ref):
        _fill_band_bias(bias_ref, pl.program_id(0), dil, True)
        lt64 = _lane_lt64(bq)

        def per_query(ref, hh, lane_c, lane_n):
            return jnp.concatenate([ref[hh:hh + 1, pl.ds(lane_c, bq)], ref[hh:hh + 1, pl.ds(lane_n, bq)]], axis=1)

        def work(items):
            products, operands, weights = [], [], []
            for n, r in items:
                nxt = jnp.minimum(n + 1, nb - 1)
                k2 = _stack_heads(_class_rows(kn_ref, n, r, dil).astype(BF16))
                v2 = _stack_heads(_class_rows(v_ref, n, r, dil).astype(BF16))
                qcat = jnp.concatenate([_class_rows(qn_ref, n, r, dil), _class_rows(qn_ref, nxt, r, dil)],
                                       axis=0).astype(BF16)
                docat = jnp.concatenate([_class_rows(do_ref, n, r, dil), _class_rows(do_ref, nxt, r, dil)],
                                        axis=0).astype(BF16)
                operands.append((qcat, docat))
                products.append((_dot_nt(k2, qcat), _dot_nt(v2, docat)))
            for (n, r), (scores, dps) in zip(items, products):
                nxt = jnp.minimum(n + 1, nb - 1)
                bias = bias_ref.at[jnp.where(n == nb - 1, 0, 1)]
                lane_c = pl.multiple_of((r * nb + n) * bq, bq)
                lane_n = pl.multiple_of((r * nb + nxt) * bq, bq)
                lse = [per_query(l_ref, hh, lane_c, lane_n) for hh in range(2)]
                dl = [per_query(dl_ref, hh, lane_c, lane_n) for hh in range(2)]
                pts, dss = [], []
                for i, rows in enumerate(_row_slices()):
                    hh = i * SLICE_ROWS // bq
                    p_t = jnp.exp(scores[rows] - bias[rows, :] - lse[hh])
                    pts.append(p_t.astype(BF16))
                    dss.append((p_t * (dps[rows] - dl[hh])).astype(BF16))
                weights.append((jnp.concatenate(pts, axis=0), jnp.concatenate(dss, axis=0)))
            for (n, r), (p_t, ds_t), (qcat, docat) in zip(items, weights, operands):
                _store_class_rows(dv_ref, n, r, dil, _unstack_heads(_dot(p_t, docat), lt64))
                _store_class_rows(dk_ref, n, r, dil, _unstack_heads(_dot(ds_t, qcat), lt64))

        _item_loop(nb, dil, work)
        _head_rmsnorm_bwd(k_ref, dk_ref, gk_ref, dkx_ref, dgain_ref)

        def cast_rows(i, carry):
            rows = pl.ds(pl.multiple_of(i * NORM_ROWS, NORM_ROWS), NORM_ROWS)
            dvx_ref[rows, :] = dv_ref[rows, :].astype(BF16)
            return carry

        lax.fori_loop(0, t // NORM_ROWS, cast_rows, 0)

    col = lambda j: pl.BlockSpec((t, LANES), _pair_col(g, j))
    vec = pl.BlockSpec((1, LANES), lambda pair: (0, 0))
    tok = pl.BlockSpec((t, LANES), lambda pair: (0, pair))
    rows = pl.BlockSpec((None, 8, t), lambda pair: (pair, 0, 0))
    return pl.pallas_call(
        body, name=f"attn_bwd_dkv_g{g}", grid=(PAIRS,),
        in_specs=[col(1), col(0), col(1), col(2), vec, tok, rows, rows],
        out_specs=[tok, tok, pl.BlockSpec((None, 8, LANES), lambda pair: (pair, 0, 0))],
        out_shape=[jax.ShapeDtypeStruct((t, ATT_W), BF16), jax.ShapeDtypeStruct((t, ATT_W), BF16),
                   jax.ShapeDtypeStruct((PAIRS, 8, LANES), F32)],
        scratch_shapes=[pltpu.VMEM((2, 2 * bq, 2 * bq), F32), pltpu.VMEM((t, LANES), F32),
                        pltpu.VMEM((t, LANES), F32)],
        compiler_params=_params("parallel"),
    )(qkv, qkn, qkn, qkn, gk, do, l_row, dl_row)


def _rows_by_residue(one_per_head, dil):
    t = one_per_head.shape[0]
    per_head = one_per_head[:, :ATT_HEADS]
    rows = per_head.reshape(t // dil, dil, ATT_HEADS).transpose(2, 1, 0).reshape(PAIRS, 2, t)
    return jnp.pad(rows, ((0, 0), (0, 6), (0, 0)))


def _per_head(rep_row):
    return rep_row[0, ::SSM_HEAD_DIM]


def _rep_heads(v):
    return jnp.repeat(v, SSM_HEAD_DIM)[None, :]


def _pad_lanes(v):
    return jnp.pad(v, ((0, 0), (0, LANES - v.shape[1])))


class _NoOverlap:
    def side(self, host):
        return None

    def after(self, host):
        pass

    def begin_backward(self, grads):
        pass


def _hosted(plan, host, fn, *args, **kwargs):
    out = fn(*args, side=plan.side(host), **kwargs)
    plan.after(host)
    return out


def _ffn_ple_fwd(x1, h, p_i, prm, i, plan):
    g, u, act = _hosted(plan, f"swiglu_fwd_{i}", _swiglu_fwd, h, prm["ffn_w_gate"][i], prm["ffn_w_up"][i],
                        name=f"swiglu_fwd_{i}")
    x2 = _hosted(plan, f"ffn_down_{i}", _matmul, act, prm["ffn_w_down"][i], mode="nn", addend=x1,
                 name=f"ffn_down_{i}")
    x3 = _ple_fwd(x2, p_i, prm["ple_w_gate"][i], prm["ple_w_proj"][i], name=f"ple_fwd_{i}")
    return x3, dict(x1=x1, h=h, g=g, u=u, act=act, x2=x2)


def _ffn_ple_bwd(dx3, p_i, prm, i, sv, grads, plan):
    ds, dple = _ple_bwd(sv["x2"], p_i, prm["ple_w_gate"][i], prm["ple_w_proj"][i], dx3, name=f"ple_bwd_{i}")
    grads["ple_w_gate"][i] = _matmul_tn(sv["x2"], ds, name=f"d_ple_w_gate_{i}")
    grads["ple_w_proj"][i] = _matmul_tn(dple, p_i, name=f"d_ple_w_proj_{i}")
    dx2 = _matmul(ds, prm["ple_w_gate"][i], mode="nt", addend=dx3, name=f"ple_dx_{i}")
    grads["ffn_w_down"][i] = _matmul_tn(sv["act"], dx2, name=f"d_ffn_w_down_{i}")
    dg, du = _hosted(plan, f"swiglu_bwd_{i}", _swiglu_bwd, dx2, prm["ffn_w_down"][i], sv["g"], sv["u"],
                     name=f"swiglu_bwd_{i}")
    grads["ffn_w_gate"][i] = _matmul_tn(dg, sv["h"], name=f"d_ffn_w_gate_{i}")
    grads["ffn_w_up"][i] = _matmul_tn(du, sv["h"], name=f"d_ffn_w_up_{i}")
    dh = _matmul(dg, prm["ffn_w_gate"][i], mode="nn", name=f"ffn_dh_gate_{i}")
    dh = _matmul(du, prm["ffn_w_up"][i], mode="nn", addend=dh, name=f"ffn_dh_up_{i}")
    dx1, dgain = _rmsnorm_bwd(sv["x1"], prm["norm_ffn"][i:i + 1], dh, dx2, name=f"ffn_norm_bwd_{i}")
    grads["norm_ffn"][i] = dgain[0]
    return dx1


def _mamba_fwd(x0, prm, plan):
    h = _rmsnorm_fwd(x0, prm["norm_mix"][0:1], name="mix_norm_fwd_0")
    z = _hosted(plan, "ssm_in_z", _matmul, h, prm["ssm_w_z"], mode="nt", name="ssm_in_z")
    xbc_pre = _hosted(plan, "ssm_in_xbc", _matmul, h, prm["ssm_w_xbc"], mode="nt", name="ssm_in_xbc")
    dt_raw = _matmul(h, prm["ssm_w_dt"], mode="nt", name="ssm_in_dt")
    xbc = _hosted(plan, "conv_fwd", _conv_fwd, xbc_pre, prm["ssm_conv_w"], prm["ssm_conv_b"])
    dt_bias = _pad_lanes(prm["ssm_dt_bias"])
    a_log = _pad_lanes(prm["ssm_a_log"])
    acs, dt_rep, acs_rep = _ssd_prep_fwd(dt_raw, dt_bias, a_log)
    acs_t = acs[:, :SSM_HEADS].T
    dskip_rep = _rep_heads(prm["ssm_d_skip"][0])
    y, hin_all = _hosted(plan, "ssd_fwd", _ssd_fwd, xbc, dt_rep, acs_rep, acs_t, dskip_rep)
    yn = _gate_norm_fwd(y, z, prm["ssm_norm_w"])
    x1, h_ffn = _matmul(yn, prm["ssm_w_out"], mode="nn", addend=x0, name="ssm_out", tm=512, tn=D_MODEL,
                        second=(_rmsnorm_rows, [prm["norm_ffn"][0:1]], BF16))
    sv = dict(x0=x0, h=h, z=z, xbc_pre=xbc_pre, dt_raw=dt_raw, xbc=xbc, dt_bias=dt_bias, dt_rep=dt_rep,
              acs_rep=acs_rep, acs_t=acs_t, dskip_rep=dskip_rep, y=y, hin_all=hin_all, yn=yn)
    return x1, h_ffn, sv


def _mamba_bwd(dx1, prm, sv, grads, plan):
    grads["ssm_w_out"] = _matmul_tn(sv["yn"], dx1, name="d_ssm_w_out")
    dyn = _matmul(dx1, prm["ssm_w_out"], mode="nt", name="ssm_out_dx")
    dy, dz, dnw = _hosted(plan, "gate_norm_bwd", _gate_norm_bwd, sv["y"], sv["z"], prm["ssm_norm_w"], dyn)
    grads["ssm_norm_w"] = dnw
    a_rep = _rep_heads(-jnp.exp(prm["ssm_a_log"][0]))
    dxbc, ddt, da_rep, dds_rep = _hosted(plan, "ssd_bwd", _ssd_bwd, sv["xbc"], sv["dt_rep"], sv["acs_rep"],
                                             sv["acs_t"], sv["dskip_rep"], a_rep, sv["hin_all"], dy)
    grads["ssm_d_skip"] = _per_head(dds_rep)[None, :]
    grads["ssm_a_log"] = (_per_head(da_rep) * _per_head(a_rep))[None, :]
    ddt_raw, dbias = _ssd_prep_bwd(sv["dt_raw"], sv["dt_bias"], ddt)
    grads["ssm_dt_bias"] = dbias[:, :SSM_HEADS]
    du, dcw, dcb = _hosted(plan, "conv_bwd", _conv_bwd, sv["xbc_pre"], prm["ssm_conv_w"], prm["ssm_conv_b"], dxbc)
    grads["ssm_conv_w"] = dcw
    grads["ssm_conv_b"] = dcb
    h = sv["h"]
    grads["ssm_w_in"] = jnp.concatenate(
        [_matmul_tn(dz, h, name="d_ssm_w_z"), _matmul_tn(du, h, name="d_ssm_w_xbc"),
         _matmul_tn(ddt_raw, h, name="d_ssm_w_dt")[:SSM_HEADS]], axis=0)
    dh = _hosted(plan, "ssm_dh_z", _matmul, dz, prm["ssm_w_z"], mode="nn", name="ssm_dh_z")
    dh = _hosted(plan, "ssm_dh_xbc", _matmul, du, prm["ssm_w_xbc"], mode="nn", addend=dh, name="ssm_dh_xbc")
    dh = _hosted(plan, "ssm_dh_dt", _matmul, ddt_raw, prm["ssm_w_dt"], mode="nn", addend=dh, name="ssm_dh_dt")
    dx0, dgain = _rmsnorm_bwd(sv["x0"], prm["norm_mix"][0:1], dh, dx1, name="mix_norm_bwd_0")
    grads["norm_mix"][0] = dgain[0]
    return dx0


def _attn_mixer_fwd(x0, prm, plan):
    h = _rmsnorm_fwd(x0, prm["norm_mix"][1:2], name="mix_norm_fwd_1")
    n_heads = N_QKV_BLOCKS * ATT_HEADS
    gq = jnp.tile(prm["att_q_norm"], (1, n_heads))
    gk = jnp.tile(prm["att_k_norm"], (1, n_heads))
    qkv, qkn = _hosted(plan, "att_qkv", _matmul, h, prm["att_w_qkv"], mode="nt", name="att_qkv",
                       second=(_qk_normalised, [gq, gk], F32))
    outs, lses = [], []
    for g, (window, dil) in enumerate(DIL_PATTERNS):
        o_g, l_g = _attn_fwd(qkn, g, dil)
        outs.append(o_g)
        lses.append(l_g)
    o_b, o_f, l_rep, l_one = _attn_combine_fwd(outs, lses)
    x1, h_ffn = _matmul(o_b, prm["att_w_o"], mode="nn", addend=x0, name="att_out", tm=512, tn=D_MODEL,
                        second=(_rmsnorm_rows, [prm["norm_ffn"][1:2]], BF16))
    sv = dict(x0=x0, h=h, qkv=qkv, qkn=qkn, gq2=gq[:, :LANES], gk2=gk[:, :LANES], o_b=o_b, o_f=o_f, l_rep=l_rep,
              l_one=l_one)
    return x1, h_ffn, sv


def _attn_mixer_bwd(dx1, prm, sv, grads, plan):
    grads["att_w_o"] = _matmul_tn(sv["o_b"], dx1, name="d_att_w_o")
    do = _hosted(plan, "att_out_dx", _matmul, dx1, prm["att_w_o"], mode="nt", name="att_out_dx")
    dl_rep, dl_one = _attn_combine_bwd(do, sv["o_f"])
    blocks, dgq, dgk = [], [], []
    for g, (window, dil) in enumerate(DIL_PATTERNS):
        dq, dgq_g = _attn_bwd_dq(sv["qkv"], sv["qkn"], sv["gq2"], do, sv["l_rep"], dl_rep, g, dil)
        dk, dv, dgk_g = _attn_bwd_dkv(sv["qkv"], sv["qkn"], sv["gk2"], do, _rows_by_residue(sv["l_one"], dil),
                                      _rows_by_residue(dl_one, dil), g, dil)
        blocks += [dq, dk, dv]
        dgq.append(dgq_g)
        dgk.append(dgk_g)
    dqkv = jnp.concatenate(blocks, axis=1)

    def fold(parts):
        return jnp.stack(parts)[:, :, 0].reshape(-1, ATT_HEAD_DIM).sum(axis=0)[None, :]

    grads["att_q_norm"] = fold(dgq)
    grads["att_k_norm"] = fold(dgk)
    grads["att_w_qkv"] = _matmul_tn(dqkv, sv["h"], name="d_att_w_qkv")
    dh = _hosted(plan, "att_qkv_dx", _matmul, dqkv, prm["att_w_qkv"], mode="nn", name="att_qkv_dx")
    dx0, dgain = _rmsnorm_bwd(sv["x0"], prm["norm_mix"][1:2], dh, dx1, name="mix_norm_bwd_1")
    grads["norm_mix"][1] = dgain[0]
    return dx0


def _local_step(x, p, target, prm, plan=None):
    plan = plan or _NoOverlap()
    grads = {k: [None, None] for k in ("norm_mix", "norm_ffn", "ffn_w_gate", "ffn_w_up", "ffn_w_down",
                                       "ple_w_proj", "ple_w_gate")}
    plan.begin_backward(grads)
    x1, h1, sv_m = _mamba_fwd(x, prm, plan)
    x3, sv_f0 = _ffn_ple_fwd(x1, h1, p[0], prm, 0, plan)
    x4, h4, sv_a = _attn_mixer_fwd(x3, prm, plan)
    x6, sv_f1 = _ffn_ple_fwd(x4, h4, p[1], prm, 1, plan)
    dy, loss_row = _loss_head(x6, target)
    dx4 = _ffn_ple_bwd(dy, p[1], prm, 1, sv_f1, grads, plan)
    dx3 = _attn_mixer_bwd(dx4, prm, sv_a, grads, plan)
    dx1 = _ffn_ple_bwd(dx3, p[0], prm, 0, sv_f0, grads, plan)
    dx0 = _mamba_bwd(dx1, prm, sv_m, grads, plan)
    return loss_row, dx0, grads


W_IN_SLAB_ROWS = 1312


def _position():
    return lax.axis_index("x"), lax.axis_index("y"), lax.axis_index("c")


def _other_chips(x, y):
    return [(1 - x, y), (x, 1 - y), (1 - x, 1 - y)]


def _remote(send_sems, recv_sems, k, src, dst, to):
    return pltpu.make_async_remote_copy(src_ref=src, dst_ref=dst, send_sem=send_sems.at[k], recv_sem=recv_sems.at[k],
                                        device_id=to, device_id_type=MESH)


def _gather_side(entries, whole=()):
    n, nw = len(entries), len(whole)

    def first_hop(ins, outs, send_sems, recv_sems):
        x, y, c = _position()
        cps = []
        for j, chip in enumerate(_other_chips(x, y)):
            for e in range(n):
                cps.append(_remote(send_sems, recv_sems, 6 * e + j, ins[e].at[c], outs[e].at[2 * x + y, c], (*chip, c)))
            for e in range(nw):
                cps.append(_remote(send_sems, recv_sems, 6 * n + 3 * e + j, ins[n + e], outs[n + e].at[2 * x + y],
                                   (*chip, c)))
        return cps

    def start(ins, outs, send_sems, recv_sems):
        for cp in first_hop(ins, outs, send_sems, recv_sems):
            cp.start()

    def finish(ins, outs, send_sems, recv_sems):
        x, y, c = _position()
        me, sibling = (x, y, c), (x, y, 1 - c)
        chips = _other_chips(x, y)
        passed_on = []
        for j, (px, py) in enumerate(chips):
            for e in range(n):
                landed = outs[e].at[2 * px + py, c]
                _remote(send_sems, recv_sems, 6 * e + j, landed, landed, me).wait_recv()
                passed_on.append(_remote(send_sems, recv_sems, 6 * e + 3 + j, landed, landed, sibling))
                passed_on[-1].start()
            for e in range(nw):
                landed = outs[n + e].at[2 * px + py]
                _remote(send_sems, recv_sems, 6 * n + 3 * e + j, landed, landed, me).wait_recv()
        for j, (px, py) in enumerate(chips):
            for e in range(n):
                passed = outs[e].at[2 * px + py, 1 - c]
                _remote(send_sems, recv_sems, 6 * e + 3 + j, passed, passed, me).wait_recv()
        for cp in first_hop(ins, outs, send_sems, recv_sems) + passed_on:
            cp.wait_send()

    shapes = [jax.ShapeDtypeStruct((N_CHIPS,) + a.shape, a.dtype) for a in list(entries) + list(whole)]
    return _Side(list(entries) + list(whole), shapes, 6 * n + 3 * nw, start, finish)


def _run_side(side, name):
    si, so = len(side.inputs), len(side.out_shapes)

    def body(*refs):
        ins, outs, send_sems, recv_sems = refs[:si], refs[si:si + so], refs[-2], refs[-1]
        side.start(ins, outs, send_sems, recv_sems)
        side.finish(ins, outs, send_sems, recv_sems)

    side.outputs = list(pl.pallas_call(
        body, name=name, in_specs=[ANY] * si, out_specs=[ANY] * so, out_shape=side.out_shapes,
        scratch_shapes=[pltpu.SemaphoreType.DMA((side.n_sems,)), pltpu.SemaphoreType.DMA((side.n_sems,))],
    )(*side.inputs))
    return side.outputs


def _swap_side(grads):
    n = len(grads)

    def copies(ins, outs, send_sems, recv_sems):
        x, y, c = _position()
        return [_remote(send_sems, recv_sems, e, ins[e].at[:, 1 - c], outs[e], (x, y, 1 - c)) for e in range(n)]

    def start(ins, outs, send_sems, recv_sems):
        for cp in copies(ins, outs, send_sems, recv_sems):
            cp.start()

    def finish(ins, outs, send_sems, recv_sems):
        for cp in copies(ins, outs, send_sems, recv_sems):
            cp.wait()

    shapes = [jax.ShapeDtypeStruct((N_CHIPS,) + g.shape[2:], g.dtype) for g in grads]
    return _Side(grads, shapes, n, start, finish)


def _chip_exchange_side(chipsums):
    n = len(chipsums)

    def copies(ins, outs, send_sems, recv_sems):
        x, y, c = _position()
        return [_remote(send_sems, recv_sems, 3 * e + j, ins[e].at[2 * tx + ty], outs[e].at[j], (tx, ty, c))
                for j, (tx, ty) in enumerate(_other_chips(x, y)) for e in range(n)]

    def start(ins, outs, send_sems, recv_sems):
        for cp in copies(ins, outs, send_sems, recv_sems):
            cp.start()

    def finish(ins, outs, send_sems, recv_sems):
        for cp in copies(ins, outs, send_sems, recv_sems):
            cp.wait()

    shapes = [jax.ShapeDtypeStruct((3,) + cs.shape[1:], cs.dtype) for cs in chipsums]
    return _Side(chipsums, shapes, 3 * n, start, finish)


def _share_side(totals):
    n = len(totals)

    def copies(ins, outs, send_sems, recv_sems):
        x, y, c = _position()
        return [_remote(send_sems, recv_sems, e, ins[e], outs[e], (x, y, 1 - c)) for e in range(n)]

    def start(ins, outs, send_sems, recv_sems):
        for cp in copies(ins, outs, send_sems, recv_sems):
            cp.start()

    def finish(ins, outs, send_sems, recv_sems):
        for cp in copies(ins, outs, send_sems, recv_sems):
            cp.wait()

    return _Side(totals, [jax.ShapeDtypeStruct(t.shape, t.dtype) for t in totals], n, start, finish)


def _reduce_rows(h):
    return h if h <= 704 else h // 2


def _add_sibling(grad, recv, c_idx, *, name):
    _, _, h, cw = grad.shape
    th = _reduce_rows(h)

    def body(c_ref, g_ref, r_ref, o_ref):
        o_ref[...] = (g_ref[...] + r_ref[...]).astype(BF16)

    return pl.pallas_call(
        body, name=name,
        grid_spec=pltpu.PrefetchScalarGridSpec(
            num_scalar_prefetch=1, grid=(N_CHIPS, h // th),
            in_specs=[pl.BlockSpec((None, None, th, cw), lambda s, i, c_ref: (s, c_ref[0], i, 0)),
                      pl.BlockSpec((None, th, cw), lambda s, i, c_ref: (s, i, 0))],
            out_specs=pl.BlockSpec((None, th, cw), lambda s, i, c_ref: (s, i, 0))),
        out_shape=jax.ShapeDtypeStruct((N_CHIPS, h, cw), BF16),
        compiler_params=_params("parallel", "parallel"),
    )(c_idx, grad, recv)


def _add_chips(chipsum, recv, s_idx, *, name):
    _, h, cw = chipsum.shape
    th = _reduce_rows(h)

    def body(s_ref, own_ref, r_ref, o_ref):
        o_ref[...] = ((own_ref[...].astype(F32) + r_ref[0].astype(F32)) + r_ref[1].astype(F32)) + r_ref[2].astype(F32)

    return pl.pallas_call(
        body, name=name,
        grid_spec=pltpu.PrefetchScalarGridSpec(
            num_scalar_prefetch=1, grid=(h // th,),
            in_specs=[pl.BlockSpec((None, th, cw), lambda i, s_ref: (s_ref[0], i, 0)),
                      pl.BlockSpec((3, th, cw), lambda i, s_ref: (0, i, 0))],
            out_specs=pl.BlockSpec((th, cw), lambda i, s_ref: (i, 0))),
        out_shape=jax.ShapeDtypeStruct((h, cw), F32),
        compiler_params=_params("parallel"),
    )(s_idx, chipsum, recv)


def _adamw_math(w, g, m, v):
    m = ADAM_B1 * m + (1.0 - ADAM_B1) * g
    v = ADAM_B2 * v + (1.0 - ADAM_B2) * (g * g)
    m_hat = m / (1.0 - ADAM_B1 ** ADAM_STEP)
    v_hat = v / (1.0 - ADAM_B2 ** ADAM_STEP)
    delta = -ADAM_LR * (m_hat / (jnp.sqrt(v_hat) + ADAM_EPS) + ADAM_WD * w)
    return delta, m, v


ADAM_TILE_ELEMS = 256 * 1024


def _adamw(w, g, m, v, *, name):
    layers, rows, cols = w.shape
    tr = rows
    for cand in range(8, rows, 8):
        if rows % cand == 0 and cand * cols <= ADAM_TILE_ELEMS:
            tr = cand
    if rows * cols <= ADAM_TILE_ELEMS:
        tr = rows

    def body(w_ref, g_ref, m_ref, v_ref, d_ref, nm_ref, nv_ref):
        d, nm, nv = _adamw_math(w_ref[...], g_ref[...], m_ref[...], v_ref[...])
        d_ref[...] = d
        nm_ref[...] = nm
        nv_ref[...] = nv

    blk = pl.BlockSpec((None, tr, cols), lambda l, i: (l, i, 0))
    sds = jax.ShapeDtypeStruct(w.shape, F32)
    return pl.pallas_call(
        body, name=name, grid=(layers, rows // tr), in_specs=[blk] * 4, out_specs=[blk] * 3, out_shape=[sds] * 3,
        compiler_params=_params("parallel", "parallel"),
    )(w, g, m, v)


SMALL_LAYOUT = (("loss", 1), ("norm_mix", 16), ("norm_ffn", 16), ("ssm_conv_b", 24), ("ssm_dt_bias", 1),
                ("ssm_a_log", 1), ("ssm_d_skip", 1), ("ssm_norm_w", 16), ("att_q_norm", 1), ("att_k_norm", 1),
                ("conv_w_full", 96))
SMALL_ROWS = 176
N_DEVICES = 8


def _small_packs(dicts):
    parts = []
    for values in dicts:
        for name, rows in SMALL_LAYOUT:
            flat = values[name].reshape(-1).astype(F32)
            parts.append(jnp.pad(flat, (0, rows * LANES - flat.shape[0])).reshape(rows, LANES))
        used = sum(r for _, r in SMALL_LAYOUT)
        parts.append(jnp.zeros((SMALL_ROWS - used, LANES), F32))
    return jnp.concatenate(parts, axis=0).reshape(len(dicts), SMALL_ROWS, LANES)


def _small_unpack(pack, shapes):
    out, off = {}, 0
    for name, rows in SMALL_LAYOUT:
        shape = shapes[name]
        n = math.prod(shape)
        out[name] = pack[off:off + rows].reshape(-1)[:n].reshape(shape)
        off += rows
    return out


def _small_allreduce_adamw(g, w, m, v):
    def body(g_ref, w_ref, m_ref, v_ref, gs_ref, d_ref, nm_ref, nv_ref, buf, send_sems, recv_sems):
        x, y, c = _position()
        pos = (x, y, c)
        me = 4 * x + 2 * y + c
        buf[me] = g_ref[...]
        peers = []
        for k in range(1, N_DEVICES):
            bits = ((k >> 2) & 1, (k >> 1) & 1, k & 1)
            peers.append(tuple(1 - p if b else p for p, b in zip(pos, bits)))
        cps = [pltpu.make_async_remote_copy(src_ref=g_ref, dst_ref=buf.at[me], send_sem=send_sems.at[k],
                                            recv_sem=recv_sems.at[k], device_id=peer, device_id_type=MESH)
               for k, peer in enumerate(peers)]
        for cp in cps:
            cp.start()
        for k, (px, py, pc) in enumerate(peers):
            pltpu.make_async_remote_copy(src_ref=g_ref, dst_ref=buf.at[4 * px + 2 * py + pc],
                                         send_sem=send_sems.at[k], recv_sem=recv_sems.at[k],
                                         device_id=(px, py, pc), device_id_type=MESH).wait_recv()
        for cp in cps:
            cp.wait_send()
        total = buf[0]
        for dev in range(1, N_DEVICES):
            total = total + buf[dev]
        gs_ref[...] = total
        d, nm, nv = _adamw_math(w_ref[...], total, m_ref[...], v_ref[...])
        d_ref[...] = d
        nm_ref[...] = nm
        nv_ref[...] = nv

    vm = pl.BlockSpec(memory_space=pltpu.VMEM)
    sds = jax.ShapeDtypeStruct((SMALL_ROWS, LANES), F32)
    return pl.pallas_call(
        body, name="small_allreduce_adamw", in_specs=[vm] * 4, out_specs=[vm] * 4, out_shape=[sds] * 4,
        scratch_shapes=[pltpu.VMEM((N_DEVICES, SMALL_ROWS, LANES), F32),
                        pltpu.SemaphoreType.DMA((N_DEVICES - 1,)), pltpu.SemaphoreType.DMA((N_DEVICES - 1,))],
    )(g, w, m, v)


SMALL = tuple(n for n, _ in SMALL_LAYOUT if n not in ("loss", "conv_w_full"))
WEIGHTS = ("norm_mix", "norm_ffn", "ssm_w_in", "ssm_conv_w", "ssm_conv_b", "ssm_dt_bias", "ssm_a_log", "ssm_d_skip",
           "ssm_norm_w", "ssm_w_out", "att_w_qkv", "att_q_norm", "att_k_norm", "att_w_o", "ffn_w_gate", "ffn_w_up",
           "ffn_w_down", "ple_w_proj", "ple_w_gate")
COLUMN_SHARDED = ("ssm_w_in", "att_w_qkv", "ffn_w_gate", "ffn_w_up", "ple_w_proj")
LAYERED = ("ffn_w_gate", "ffn_w_up", "ffn_w_down", "ple_w_proj", "ple_w_gate")
UPDATED_TRANSPOSED = ("ssm_w_in", "ffn_w_gate", "ffn_w_up")
GATHER_ORDER = ("ssm_w_in", "ssm_w_out", "att_w_qkv", "att_w_o", "ffn_w_gate", "ffn_w_up", "ffn_w_down",
                "ple_w_proj", "ple_w_gate")


def _layers(n):
    return (0, 1) if n in LAYERED else (None,)


def _tag(key):
    return key[0] if key[1] is None else f"{key[0]}_{key[1]}"


QKV_PARTS = 3


def _weight_slab(w, key):
    n, i = key
    if n == "att_w_qkv":
        a = w[n][0].T
        rows = a.shape[0] // QKV_PARTS
        a = a[i * rows:(i + 1) * rows]
    else:
        a = w[n][0 if i is None else i]
        a = a.T if n in COLUMN_SHARDED else a
    if n == "ssm_w_in":
        a = jnp.pad(a, ((0, W_IN_SLAB_ROWS - a.shape[0]), (0, 0)))
    return a.reshape(2, a.shape[0] // 2, a.shape[1]).astype(BF16)


def _install(prm, key, gathered, own, s_me):
    n, i = key
    full = lax.dynamic_update_slice(gathered, own[None], (s_me, 0, 0, 0))
    full = full.reshape(N_CHIPS, 2 * full.shape[2], full.shape[3])
    if n == "att_w_qkv":
        parts = prm.setdefault("att_w_qkv_parts", {})
        parts[i] = full
        if len(parts) == QKV_PARTS:
            prm[n] = jnp.stack([parts[j] for j in range(QKV_PARTS)], axis=1).reshape(-1, D_MODEL)
        return
    if n == "ssm_w_in":
        rows = (D_INNER + CONV_DIM + SSM_HEADS) // N_CHIPS
        w_in_t = full[:, :rows].reshape(N_CHIPS * rows, D_MODEL)
        prm["ssm_w_z"] = w_in_t[:D_INNER]
        prm["ssm_w_xbc"] = w_in_t[D_INNER:D_INNER + CONV_DIM]
        prm["ssm_w_dt"] = jnp.pad(w_in_t[D_INNER + CONV_DIM:], ((0, LANES - SSM_HEADS), (0, 0)))
        return
    full = full.reshape(N_CHIPS * full.shape[1], full.shape[2])
    if i is None:
        prm[n] = full
    else:
        prm.setdefault(n, [None, None])[i] = full


def _grad_slab(grads, key):
    n, i = key
    g = grads[n] if i is None else grads[n][i]
    if n == "ssm_w_in":
        g = jnp.pad(g.reshape(N_CHIPS, g.shape[0] // N_CHIPS, D_MODEL),
                    ((0, 0), (0, W_IN_SLAB_ROWS - g.shape[0] // N_CHIPS), (0, 0)))
    rows = g.size // (N_CHIPS * g.shape[-1])
    return g.reshape(N_CHIPS, 2, rows // 2, g.shape[-1])


def _natural_shard(n, reduced, shape):
    def one(r):
        if n == "ssm_w_in":
            r = r[:shape[-1]]
        return r.T if n in COLUMN_SHARDED else r
    if n in LAYERED:
        return jnp.stack([one(r) for r in reduced]).reshape(shape)
    return one(reduced[0]).reshape(shape)


def kernel(x, p, norm_mix, norm_ffn, ssm_w_in, ssm_conv_w, ssm_conv_b, ssm_dt_bias, ssm_a_log, ssm_d_skip, ssm_norm_w, ssm_w_out, att_w_qkv, att_q_norm, att_k_norm, att_w_o, ffn_w_gate, ffn_w_up, ffn_w_down, ple_w_proj, ple_w_gate, loss_target, m_norm_mix, m_norm_ffn, m_ssm_w_in, m_ssm_conv_w, m_ssm_conv_b, m_ssm_dt_bias, m_ssm_a_log, m_ssm_d_skip, m_ssm_norm_w, m_ssm_w_out, m_att_w_qkv, m_att_q_norm, m_att_k_norm, m_att_w_o, m_ffn_w_gate, m_ffn_w_up, m_ffn_w_down, m_ple_w_proj, m_ple_w_gate, v_norm_mix, v_norm_ffn, v_ssm_w_in, v_ssm_conv_w, v_ssm_conv_b, v_ssm_dt_bias, v_ssm_a_log, v_ssm_d_skip, v_ssm_norm_w, v_ssm_w_out, v_att_w_qkv, v_att_q_norm, v_att_k_norm, v_att_w_o, v_ffn_w_gate, v_ffn_w_up, v_ffn_w_down, v_ple_w_proj, v_ple_w_gate):
    given = dict(locals())
    w = {n: given[n] for n in WEIGHTS}
    m = {n: given["m_" + n] for n in WEIGHTS}
    v = {n: given["v_" + n] for n in WEIGHTS}
    c_idx = lax.axis_index("c").astype(jnp.int32).reshape(1)
    s_idx = (2 * lax.axis_index("x") + lax.axis_index("y")).astype(jnp.int32).reshape(1)

    s_me = 2 * lax.axis_index("x") + lax.axis_index("y")
    first_core = lax.axis_index("c") == 0

    qkv_parts = [("att_w_qkv", j) for j in range(QKV_PARTS)]
    gather_plan = {
        "ssm_in_z": [("ssm_w_out", None)],
        "ssm_in_xbc": [("ffn_w_gate", 0)],
        "conv_fwd": [("ffn_w_up", 0)],
        "ssd_fwd": [("ffn_w_down", 0), ("ple_w_proj", 0), ("ple_w_gate", 0), ("att_w_o", None)],
        "swiglu_fwd_0": qkv_parts[:2],
        "ffn_down_0": qkv_parts[2:],
        "att_qkv": [(n, 1) for n in LAYERED],
    }
    mamba = [("ssm_w_in", None)]
    own = {k: _weight_slab(w, k) for k in mamba + sum(gather_plan.values(), [])}
    prm = {n: w[n] for n in SMALL}

    def land(group, outputs):
        for k, g in zip(group, outputs):
            _install(prm, k, g, own[k], s_me)

    first = _gather_side([own[k] for k in mamba], whole=[ssm_conv_w[0]])
    _run_side(first, "gather_mamba")
    land(mamba, first.outputs)
    conv = lax.dynamic_update_slice(first.outputs[-1], ssm_conv_w, (s_me, 0, 0))
    prm["ssm_conv_w"] = conv.transpose(1, 0, 2).reshape(CONV_WIDTH, CONV_DIM)

    ffn1 = [(n, 1) for n in LAYERED]
    attention = [("att_w_qkv", None), ("att_w_o", None)]
    ffn0 = [(n, 0) for n in LAYERED] + [("ssm_w_out", None)]
    reduce_plan = {"att_out_dx": [("swap", ffn1)], "att_qkv_dx": [("exchange", ffn1)],
                   "swiglu_bwd_0": [("swap", attention)], "gate_norm_bwd": [("swap", ffn0)],
                   "ssd_bwd": [("exchange", attention), ("exchange", ffn0)],
                   "ssm_dh_z": [("swap", mamba)], "ssm_dh_xbc": [("exchange", mamba)]}
    state = {}

    def swap_side(group):
        state[_tag(group[0]), "g4"] = g4 = [_grad_slab(state["grads"], k) for k in group]
        return _swap_side(g4)

    def add_siblings(group, from_sibling):
        state[_tag(group[0]), "chipsums"] = [
            _add_sibling(g, r, c_idx, name="add_sibling_" + _tag(k))
            for g, r, k in zip(state[_tag(group[0]), "g4"], from_sibling, group)]

    def exchange_side(group):
        return _chip_exchange_side(state[_tag(group[0]), "chipsums"])

    def add_chips(group, from_chips):
        for k, cs, r in zip(group, state[_tag(group[0]), "chipsums"], from_chips):
            state["total", k] = _add_chips(cs, r, s_idx, name="add_chips_" + _tag(k))

    class Plan(_NoOverlap):
        def __init__(self):
            self.carried = {host: _gather_side([own[k] for k in group]) for host, group in gather_plan.items()}

        def begin_backward(self, grads):
            state["grads"] = grads

        def side(self, host):
            if host in reduce_plan:
                self.parts = [swap_side(group) if step == "swap" else exchange_side(group)
                              for step, group in reduce_plan[host]]
                self.carried[host] = _sides_together(self.parts)
            elif host == share_host:
                self.carried[host] = _share_side([state["total", k] for k in order])
            return self.carried.get(host)

        def after(self, host):
            if host in gather_plan:
                land(gather_plan[host], self.carried[host].outputs)
            elif host in reduce_plan:
                _share_out(self.carried[host], self.parts)
                for (step, group), part in zip(reduce_plan[host], self.parts):
                    (add_siblings if step == "swap" else add_chips)(group, part.outputs)
            elif host == share_host:
                state["shared"] = self.carried[host].outputs

    order = mamba + ffn0 + attention + ffn1
    share_host = "ssm_dh_dt"
    loss_row, dx, grads = _local_step(x[0], p[:, 0], loss_target[0], prm, Plan())

    reduced = {}
    for k, theirs in zip(order, state["shared"]):
        lo = jnp.where(first_core, state["total", k], theirs)
        hi = jnp.where(first_core, theirs, state["total", k])
        reduced.setdefault(k[0], {})[k[1]] = jnp.concatenate([lo, hi], axis=0)
    reduced = {n: [by_layer[i] for i in _layers(n)] for n, by_layer in reduced.items()}

    grad, delta, new_m, new_v = {}, {}, {}, {}
    for n in GATHER_ORDER:
        if n in UPDATED_TRANSPOSED:
            flip = lambda a: a.transpose(0, 2, 1)
            cols = w[n].shape[-1]
            g_t = jnp.stack([r[:cols] for r in reduced[n]])
            grad[n] = flip(g_t)
            delta[n], new_m[n], new_v[n] = [flip(o) for o in _adamw(flip(w[n]), g_t, flip(m[n]), flip(v[n]),
                                                                    name="adamw_" + n)]
            continue
        grad[n] = _natural_shard(n, reduced[n], w[n].shape)
        delta[n], new_m[n], new_v[n] = _adamw(w[n], grad[n], m[n], v[n], name="adamw_" + n)

    small_g = {n: (jnp.stack(grads[n]) if isinstance(grads[n], list) else grads[n]) for n in SMALL}
    small_g["loss"] = loss_row
    small_g["conv_w_full"] = grads["ssm_conv_w"]
    zero = {"loss": jnp.zeros((1, LANES), F32), "conv_w_full": jnp.zeros((CONV_WIDTH, CONV_DIM), F32)}
    packs = _small_packs([small_g, {**w, **zero}, {**m, **zero}, {**v, **zero}])
    outs = _small_allreduce_adamw(packs[0], packs[1], packs[2], packs[3])
    shapes = {n: w[n].shape for n in SMALL}
    shapes["loss"] = (1, LANES)
    shapes["conv_w_full"] = (CONV_WIDTH, CONV_DIM)
    sg, sd, sm, sv = [_small_unpack(o, shapes) for o in outs]
    for n in SMALL:
        grad[n], delta[n], new_m[n], new_v[n] = sg[n], sd[n], sm[n], sv[n]
    loss = sg["loss"][0, 0]
    conv_cols = CONV_DIM // N_CHIPS
    grad["ssm_conv_w"] = lax.dynamic_slice(sg["conv_w_full"], (0, s_me * conv_cols), (CONV_WIDTH, conv_cols))[None]
    delta["ssm_conv_w"], new_m["ssm_conv_w"], new_v["ssm_conv_w"] = _adamw(
        ssm_conv_w, grad["ssm_conv_w"], m_ssm_conv_w, v_ssm_conv_w, name="adamw_ssm_conv_w")

    return (loss, dx[None], *[grad[n] for n in WEIGHTS], *[delta[n] for n in WEIGHTS],
            *[new_m[n] for n in WEIGHTS], *[new_v[n] for n in WEIGHTS])
```

```python
import math

import jax
import jax.numpy as jnp
from jax import lax
from jax.experimental import pallas as pl
from jax.experimental.pallas import tpu as pltpu

F32 = jnp.float32
BF16 = jnp.bfloat16
HIGHEST = lax.Precision.HIGHEST

NORM_EPS = 1e-6
ADAM_LR, ADAM_B1, ADAM_B2, ADAM_EPS, ADAM_WD, ADAM_STEP = 0.001, 0.9, 0.999, 1e-08, 0.01, 10

D_MODEL = 1024
D_INNER = 2048
SSM_HEADS = 32
SSM_HEAD_DIM = 64
SSM_GROUPS = 4
SSM_STATE = 128
SSD_CHUNK = 128
CONV_DIM = 3072
CONV_WIDTH = 4
ATT_HEADS = 16
ATT_HEAD_DIM = 64
DIL_PATTERNS = ((128, 1), (512, 4), (2048, 16))
ATT_BLOCK = 128
FFN_HIDDEN = 2816
PLE_DIM = 256

LANES = 128
V7X_VMEM_LIMIT = 56 * 1024 * 1024
NEG_BIG = -1e30

N_CHIPS = 4


def _params(*sem):
    return pltpu.CompilerParams(dimension_semantics=sem, vmem_limit_bytes=V7X_VMEM_LIMIT)


def _tile(n, pref):
    if n <= pref:
        return n
    best = None
    for t in range(LANES, pref + 1, LANES):
        if n % t == 0:
            best = t
    assert best is not None, (n, pref)
    return best


def _sigmoid(v):
    return 1.0 / (1.0 + jnp.exp(-v))


def _dot(a, b):
    return jnp.dot(a, b, preferred_element_type=F32)


def _dot_nt(a, b):
    return lax.dot_general(a, b, (((1,), (1,)), ((), ())), preferred_element_type=F32)


def _dot_tn(a, b):
    return lax.dot_general(a, b, (((0,), (0,)), ((), ())), preferred_element_type=F32)


def _head_block_diag():
    i = lax.broadcasted_iota(jnp.int32, (LANES, LANES), 0) // ATT_HEAD_DIM
    j = lax.broadcasted_iota(jnp.int32, (LANES, LANES), 1) // ATT_HEAD_DIM
    return (i == j).astype(BF16)


def _split_dot(ones, z):
    hi = z.astype(BF16)
    lo = (z - hi.astype(F32)).astype(BF16)
    return _dot(ones, hi) + _dot(ones, lo)


def _head_sums(z, bd, terms=2):
    hi = z.astype(BF16)
    lo = (z - hi.astype(F32)).astype(BF16) if terms == 2 else None
    parts = []
    for t in range(z.shape[1] // LANES):
        sl = slice(t * LANES, (t + 1) * LANES)
        part = _dot(hi[:, sl], bd)
        parts.append(part + _dot(lo[:, sl], bd) if terms == 2 else part)
    return parts[0] if len(parts) == 1 else jnp.concatenate(parts, axis=1)


def _lane_lt64(rows):
    return lax.broadcasted_iota(jnp.int32, (rows, LANES), 1) < ATT_HEAD_DIM


MESH = pl.DeviceIdType.MESH
ANY = pl.BlockSpec(memory_space=pl.ANY)


class _Side:
    def __init__(self, inputs, out_shapes, n_sems, start, finish):
        self.inputs, self.out_shapes, self.n_sems = list(inputs), list(out_shapes), n_sems
        self.start, self.finish = start, finish
        self.outputs = None


class _SemaphoresFrom:
    def __init__(self, sems, first):
        self.sems, self.first = sems, first

    @property
    def at(self):
        return self

    def __getitem__(self, k):
        return self.sems.at[self.first + k]


def _sides_together(sides):
    def run(step):
        def both(ins, outs, send_sems, recv_sems):
            i = o = k = 0
            for s in sides:
                ni, no = len(s.inputs), len(s.out_shapes)
                getattr(s, step)(ins[i:i + ni], outs[o:o + no], _SemaphoresFrom(send_sems, k),
                                 _SemaphoresFrom(recv_sems, k))
                i, o, k = i + ni, o + no, k + s.n_sems
        return both

    return _Side(sum([s.inputs for s in sides], []), sum([s.out_shapes for s in sides], []),
                 sum(s.n_sems for s in sides), run("start"), run("finish"))


def _share_out(together, sides):
    o = 0
    for s in sides:
        s.outputs = together.outputs[o:o + len(s.out_shapes)]
        o += len(s.out_shapes)


def _call(body, side, *, name, grid, in_specs, out_specs, out_shape, scratch_shapes, semantics, args):
    in_specs, out_specs, out_shape = list(in_specs), list(out_specs), list(out_shape)
    scratch_shapes = list(scratch_shapes)
    if side is None:
        return pl.pallas_call(body, name=name, grid=grid, in_specs=in_specs, out_specs=out_specs,
                              out_shape=out_shape, scratch_shapes=scratch_shapes,
                              compiler_params=_params(*semantics))(*args)
    ni, no, ns = len(in_specs), len(out_specs), len(scratch_shapes)
    si, so = len(side.inputs), len(side.out_shapes)

    def hosted(*refs):
        ins, s_ins = refs[:ni], refs[ni:ni + si]
        outs, s_outs = refs[ni + si:ni + si + no], refs[ni + si + no:ni + si + no + so]
        scratch = refs[ni + si + no + so:ni + si + no + so + ns]
        send_sems, recv_sems = refs[-2], refs[-1]
        first = pl.program_id(0) == 0
        last = pl.program_id(0) == grid[0] - 1
        for axis in range(1, len(grid)):
            first = jnp.logical_and(first, pl.program_id(axis) == 0)
            last = jnp.logical_and(last, pl.program_id(axis) == grid[axis] - 1)

        @pl.when(first)
        def _():
            side.start(s_ins, s_outs, send_sems, recv_sems)

        body(*ins, *outs, *scratch)

        @pl.when(last)
        def _():
            side.finish(s_ins, s_outs, send_sems, recv_sems)

    res = pl.pallas_call(
        hosted, name=name, grid=grid, in_specs=in_specs + [ANY] * si, out_specs=out_specs + [ANY] * so,
        out_shape=out_shape + side.out_shapes,
        scratch_shapes=scratch_shapes + [pltpu.SemaphoreType.DMA((side.n_sems,)),
                                         pltpu.SemaphoreType.DMA((side.n_sems,))],
        compiler_params=_params(*["arbitrary"] * len(grid)),
    )(*args, *side.inputs)
    side.outputs = list(res[no:])
    return list(res[:no])


def _matmul(a, b, *, mode, name, out_dtype=F32, addend=None, tm=1024, tn=512, tk_max=3072, side=None, second=None):
    m, k = a.shape
    if mode == "nn":
        k2, n = b.shape
    else:
        n, k2 = b.shape
    assert k == k2, (a.shape, b.shape, mode)
    tm, tn, tk = _tile(m, tm), _tile(n, tn), _tile(k, tk_max)
    nk = k // tk
    has_add = addend is not None
    n_rows = len(second[1]) if second else 0
    n_out = 2 if second else 1

    def body(*refs):
        a_ref, b_ref = refs[0], refs[1]
        add_ref = refs[2] if has_add else None
        row_refs = refs[2 + has_add:2 + has_add + n_rows]
        o_ref, acc_ref = refs[-1 - n_out], refs[-1]
        kk = pl.program_id(2)
        col_tile = pl.program_id(1)
        av = a_ref[...].astype(BF16)
        bv = b_ref[...].astype(BF16)
        part = _dot(av, bv) if mode == "nn" else _dot_nt(av, bv)

        @pl.when(kk == 0)
        def _():
            acc_ref[...] = part

        @pl.when(kk > 0)
        def _():
            acc_ref[...] += part

        @pl.when(kk == nk - 1)
        def _():
            res = acc_ref[...]
            if has_add:
                res = res + add_ref[...]
            o_ref[...] = res.astype(out_dtype)
            if second:
                refs[-2][...] = second[0](res, col_tile, *row_refs).astype(second[2])

    a_spec = pl.BlockSpec((tm, tk), lambda i, j, kk: (i, kk))
    if mode == "nn":
        b_spec = pl.BlockSpec((tk, tn), lambda i, j, kk: (kk, j))
    else:
        b_spec = pl.BlockSpec((tn, tk), lambda i, j, kk: (j, kk))
    tile = pl.BlockSpec((tm, tn), lambda i, j, kk: (i, j))
    in_specs = [a_spec, b_spec]
    args = [a, b]
    if has_add:
        in_specs.append(tile)
        args.append(addend)
    if second:
        in_specs += [pl.BlockSpec((1, tn), lambda i, j, kk: (0, j))] * n_rows
        args += list(second[1])
    outs = _call(
        body, side, name=name, grid=(m // tm, n // tn, nk),
        in_specs=in_specs, out_specs=[tile] * n_out,
        out_shape=[jax.ShapeDtypeStruct((m, n), out_dtype)] + ([jax.ShapeDtypeStruct((m, n), second[2])] if second
                                                                 else []),
        scratch_shapes=[pltpu.VMEM((tm, tn), F32)],
        semantics=("parallel", "parallel", "arbitrary"), args=args,
    )
    return outs if second else outs[0]


def _matmul_tn(a, b, *, name, tm=1408, tn=512, tk=1024):
    t, m = a.shape
    t2, n = b.shape
    assert t == t2
    tm, tn, tk = _tile(m, tm), _tile(n, tn), _tile(t, tk)

    def body(a_ref, b_ref, o_ref):
        part = _dot_tn(a_ref[...].astype(BF16), b_ref[...].astype(BF16))

        @pl.when(pl.program_id(2) == 0)
        def _():
            o_ref[...] = part

        @pl.when(pl.program_id(2) > 0)
        def _():
            o_ref[...] += part

    return pl.pallas_call(
        body, name=name, grid=(m // tm, n // tn, t // tk),
        in_specs=[pl.BlockSpec((tk, tm), lambda i, j, kk: (kk, i)),
                  pl.BlockSpec((tk, tn), lambda i, j, kk: (kk, j))],
        out_specs=pl.BlockSpec((tm, tn), lambda i, j, kk: (i, j)),
        out_shape=jax.ShapeDtypeStruct((m, n), F32),
        compiler_params=_params("parallel", "parallel", "arbitrary"),
    )(a, b)


def _rmsnorm_rows(tile, j, gain_ref):
    r = lax.rsqrt(jnp.mean(tile * tile, axis=-1, keepdims=True) + NORM_EPS)
    return tile * r * gain_ref[...]


def _rmsnorm_fwd(x, gain, *, name):
    t, d = x.shape
    tm = _tile(t, 512)

    def body(x_ref, g_ref, o_ref):
        xv = x_ref[...]
        r = lax.rsqrt(jnp.mean(xv * xv, axis=-1, keepdims=True) + NORM_EPS)
        o_ref[...] = (xv * r * g_ref[...]).astype(BF16)

    return pl.pallas_call(
        body, name=name, grid=(t // tm,),
        in_specs=[pl.BlockSpec((tm, d), lambda i: (i, 0)), pl.BlockSpec((1, d), lambda i: (0, 0))],
        out_specs=pl.BlockSpec((tm, d), lambda i: (i, 0)),
        out_shape=jax.ShapeDtypeStruct((t, d), BF16),
        compiler_params=_params("parallel"),
    )(x, gain)


def _matmul_rmsnorm_bwd(a, b, addend, x, gain, dres, *, name, side=None, tm=512, tk_max=3072):
    m, k = a.shape
    d = b.shape[1]
    tm, tk = _tile(m, tm), _tile(k, tk_max)
    nk = k // tk

    def body(a_ref, b_ref, *rest):
        add_ref = rest[0] if addend is not None else None
        x_ref, g_ref, dres_ref, dx_ref, dg_ref, acc_ref = rest[-6:]
        i, kk = pl.program_id(0), pl.program_id(1)
        part = _dot(a_ref[...].astype(BF16), b_ref[...].astype(BF16))

        @pl.when(kk == 0)
        def _():
            acc_ref[...] = part

        @pl.when(kk > 0)
        def _():
            acc_ref[...] += part

        @pl.when(kk == nk - 1)
        def _():
            dyv = acc_ref[...] if addend is None else acc_ref[...] + add_ref[...]
            xv = x_ref[...]
            r = lax.rsqrt(jnp.mean(xv * xv, axis=-1, keepdims=True) + NORM_EPS)
            xh = xv * r
            dxh = dyv * g_ref[...]
            mean = jnp.mean(dxh * xh, axis=-1, keepdims=True)
            dx_ref[...] = dres_ref[...] + r * (dxh - xh * mean)
            gain_part = jnp.sum(dyv * xh, axis=0, keepdims=True)

            @pl.when(i == 0)
            def _():
                dg_ref[...] = gain_part

            @pl.when(i > 0)
            def _():
                dg_ref[...] += gain_part

    row = pl.BlockSpec((tm, d), lambda i, kk: (i, 0))
    vec = pl.BlockSpec((1, d), lambda i, kk: (0, 0))
    return _call(
        body, side, name=name, grid=(m // tm, nk),
        in_specs=[pl.BlockSpec((tm, tk), lambda i, kk: (i, kk)), pl.BlockSpec((tk, d), lambda i, kk: (kk, 0))]
        + ([row] if addend is not None else []) + [row, vec, row],
        out_specs=[row, vec],
        out_shape=[jax.ShapeDtypeStruct((m, d), F32), jax.ShapeDtypeStruct((1, d), F32)],
        scratch_shapes=[pltpu.VMEM((tm, d), F32)],
        semantics=("arbitrary", "arbitrary"),
        args=(a, b) + ((addend,) if addend is not None else ()) + (x, gain, dres),
    )


def _loss_head(y, target):
    t, d = y.shape
    tm = _tile(t, 512)
    steps = t // tm

    def body(y_ref, t_ref, dy_ref, l_ref, acc_ref):
        e = y_ref[...] - t_ref[...]
        dy_ref[...] = e * (1.0 / d)
        part = jnp.sum(e * e, axis=0, keepdims=True)

        @pl.when(pl.program_id(0) == 0)
        def _():
            acc_ref[...] = part

        @pl.when(pl.program_id(0) > 0)
        def _():
            acc_ref[...] += part

        @pl.when(pl.program_id(0) == steps - 1)
        def _():
            l_ref[...] = jnp.full((1, LANES), (0.5 / d), F32) * jnp.sum(acc_ref[...])

    row = pl.BlockSpec((tm, d), lambda i: (i, 0))
    return pl.pallas_call(
        body, name="loss_head", grid=(steps,),
        in_specs=[row, row], out_specs=[row, pl.BlockSpec((1, LANES), lambda i: (0, 0))],
        out_shape=[jax.ShapeDtypeStruct((t, d), F32), jax.ShapeDtypeStruct((1, LANES), F32)],
        scratch_shapes=[pltpu.VMEM((1, d), F32)],
        compiler_params=_params("arbitrary"),
    )(y, target)


def _swiglu_fwd(h, w_gate_t, w_up_t, *, name, side=None):
    t, d = h.shape
    f = w_gate_t.shape[0]
    tm, tn = _tile(t, 1024), _tile(f, 256)

    def body(h_ref, wg_ref, wu_ref, g_ref, u_ref, a_ref):
        hv = h_ref[...]
        g = _dot_nt(hv, wg_ref[...])
        u = _dot_nt(hv, wu_ref[...])
        g_ref[...] = g.astype(BF16)
        u_ref[...] = u.astype(BF16)
        a_ref[...] = (g * _sigmoid(g) * u).astype(BF16)

    wspec = pl.BlockSpec((tn, d), lambda i, j: (j, 0))
    ospec = pl.BlockSpec((tm, tn), lambda i, j: (i, j))
    return _call(
        body, side, name=name, grid=(t // tm, f // tn),
        in_specs=[pl.BlockSpec((tm, d), lambda i, j: (i, 0)), wspec, wspec],
        out_specs=[ospec, ospec, ospec],
        out_shape=[jax.ShapeDtypeStruct((t, f), BF16), jax.ShapeDtypeStruct((t, f), BF16),
                   jax.ShapeDtypeStruct((t, f), BF16)],
        scratch_shapes=[], semantics=("parallel", "parallel"), args=(h, w_gate_t, w_up_t),
    )


def _swiglu_bwd(dx, w_down, g, u, *, name, side=None):
    t, d = dx.shape
    f = w_down.shape[0]
    tm, tn = _tile(t, 1024), _tile(f, 256)

    def body(dx_ref, wd_ref, g_ref, u_ref, dg_ref, du_ref):
        dact = _dot_nt(dx_ref[...].astype(BF16), wd_ref[...])
        gv, uv = g_ref[...].astype(F32), u_ref[...].astype(F32)
        sg = _sigmoid(gv)
        dg_ref[...] = (dact * uv * sg * (1.0 + gv * (1.0 - sg))).astype(BF16)
        du_ref[...] = (dact * gv * sg).astype(BF16)

    ospec = pl.BlockSpec((tm, tn), lambda i, j: (i, j))
    return _call(
        body, side, name=name, grid=(t // tm, f // tn),
        in_specs=[pl.BlockSpec((tm, d), lambda i, j: (i, 0)), pl.BlockSpec((tn, d), lambda i, j: (j, 0)),
                  ospec, ospec],
        out_specs=[ospec, ospec],
        out_shape=[jax.ShapeDtypeStruct((t, f), BF16), jax.ShapeDtypeStruct((t, f), BF16)],
        scratch_shapes=[], semantics=("parallel", "parallel"), args=(dx, w_down, g, u),
    )


def _ple_fwd(x, p, w_gate, w_proj_t, *, name):
    t, d = x.shape
    e = p.shape[1]
    tm, tn = _tile(t, 1024), _tile(d, 512)

    def body(xf_ref, xr_ref, p_ref, wg_ref, wp_ref, o_ref):
        s = _dot(xf_ref[...].astype(BF16), wg_ref[...])
        ple = _dot_nt(p_ref[...].astype(BF16), wp_ref[...])
        o_ref[...] = xr_ref[...] + _sigmoid(s) * ple

    return pl.pallas_call(
        body, name=name, grid=(t // tm, d // tn),
        in_specs=[pl.BlockSpec((tm, d), lambda i, j: (i, 0)), pl.BlockSpec((tm, tn), lambda i, j: (i, j)),
                  pl.BlockSpec((tm, e), lambda i, j: (i, 0)), pl.BlockSpec((d, tn), lambda i, j: (0, j)),
                  pl.BlockSpec((tn, e), lambda i, j: (j, 0))],
        out_specs=pl.BlockSpec((tm, tn), lambda i, j: (i, j)),
        out_shape=jax.ShapeDtypeStruct((t, d), F32),
        compiler_params=_params("parallel", "parallel"),
    )(x, x, p, w_gate, w_proj_t)


def _ple_bwd(x, p, w_gate, w_proj_t, dout, *, name):
    t, d = x.shape
    e = p.shape[1]
    tm, tn = _tile(t, 1024), _tile(d, 512)

    def body(xf_ref, p_ref, wg_ref, wp_ref, do_ref, ds_ref, dple_ref):
        s = _dot(xf_ref[...].astype(BF16), wg_ref[...])
        ple = _dot_nt(p_ref[...].astype(BF16), wp_ref[...])
        gate = _sigmoid(s)
        dov = do_ref[...]
        dple_ref[...] = (dov * gate).astype(BF16)
        ds_ref[...] = (dov * ple * gate * (1.0 - gate)).astype(BF16)

    ospec = pl.BlockSpec((tm, tn), lambda i, j: (i, j))
    return pl.pallas_call(
        body, name=name, grid=(t // tm, d // tn),
        in_specs=[pl.BlockSpec((tm, d), lambda i, j: (i, 0)), pl.BlockSpec((tm, e), lambda i, j: (i, 0)),
                  pl.BlockSpec((d, tn), lambda i, j: (0, j)), pl.BlockSpec((tn, e), lambda i, j: (j, 0)), ospec],
        out_specs=[ospec, ospec],
        out_shape=[jax.ShapeDtypeStruct((t, d), BF16), jax.ShapeDtypeStruct((t, d), BF16)],
        compiler_params=_params("parallel", "parallel"),
    )(x, p, w_gate, w_proj_t, dout)


CONV_TIME_TILE = 256
CONV_HALO = 8


def _conv_taps(ext, w):
    acc = ext[CONV_HALO:, :] * w[CONV_WIDTH - 1:CONV_WIDTH, :]
    shifted = [ext[CONV_HALO:, :]]
    for j in range(1, CONV_WIDTH):
        sh = pltpu.roll(ext, j, 0)[CONV_HALO:, :]
        shifted.append(sh)
        acc = acc + sh * w[CONV_WIDTH - 1 - j:CONV_WIDTH - j, :]
    return acc, shifted


def _conv_fwd(u, w, b, side=None):
    t, c = u.shape
    tc = _tile(c, 256)
    tt = CONV_TIME_TILE

    def body(u_ref, w_ref, b_ref, o_ref):
        wv, bv = w_ref[...], b_ref[...]

        def tile(start, ext):
            pre = _conv_taps(ext, wv)[0] + bv
            o_ref[pl.ds(start, tt), :] = pre * _sigmoid(pre)

        tile(0, jnp.concatenate([jnp.zeros((CONV_HALO, tc), F32), u_ref[0:tt, :]], axis=0))

        def loop(i, carry):
            start = pl.multiple_of(i * tt, tt)
            tile(start, u_ref[pl.ds(start - CONV_HALO, tt + CONV_HALO), :])
            return carry

        lax.fori_loop(1, t // tt, loop, 0)

    col = pl.BlockSpec((t, tc), lambda j: (0, j))
    return _call(
        body, side, name="conv_fwd", grid=(c // tc,),
        in_specs=[col, pl.BlockSpec((CONV_WIDTH, tc), lambda j: (0, j)), pl.BlockSpec((1, tc), lambda j: (0, j))],
        out_specs=[col], out_shape=[jax.ShapeDtypeStruct((t, c), F32)],
        scratch_shapes=[], semantics=("parallel",), args=(u, w, b),
    )[0]


def _conv_bwd(u, w, b, dact, side=None):
    t, c = u.shape
    tc = _tile(c, 256)
    tt = CONV_TIME_TILE

    def body(u_ref, w_ref, b_ref, da_ref, du_ref, dw_ref, db_ref, dpre_ref):
        wv, bv = w_ref[...], b_ref[...]

        def tile(start, ext, sums):
            acc, shifted = _conv_taps(ext, wv)
            pre = acc + bv
            sg = _sigmoid(pre)
            dpre = da_ref[pl.ds(start, tt), :] * (sg * (1.0 + pre * (1.0 - sg)))
            dpre_ref[pl.ds(start, tt), :] = dpre
            new = [sums[0] + jnp.sum(dpre, axis=0, keepdims=True)]
            for j in range(CONV_WIDTH):
                new.append(sums[1 + j] + jnp.sum(dpre * shifted[j], axis=0, keepdims=True))
            return tuple(new)

        zero = jnp.zeros((1, tc), F32)
        sums = tile(0, jnp.concatenate([jnp.zeros((CONV_HALO, tc), F32), u_ref[0:tt, :]], axis=0),
                    (zero,) * (1 + CONV_WIDTH))

        def loop(i, sums):
            start = pl.multiple_of(i * tt, tt)
            return tile(start, u_ref[pl.ds(start - CONV_HALO, tt + CONV_HALO), :], sums)

        sums = lax.fori_loop(1, t // tt, loop, sums)
        db_ref[...] = sums[0]
        dw_ref[...] = jnp.concatenate([sums[1 + (CONV_WIDTH - 1 - k)] for k in range(CONV_WIDTH)], axis=0)
        dpre_ref[pl.ds(t, CONV_HALO), :] = jnp.zeros((CONV_HALO, tc), F32)

        def loop2(i, carry):
            start = pl.multiple_of(i * tt, tt)
            ext = dpre_ref[pl.ds(start, tt + CONV_HALO), :]
            acc = ext[0:tt, :] * wv[CONV_WIDTH - 1:CONV_WIDTH, :]
            for j in range(1, CONV_WIDTH):
                acc = acc + pltpu.roll(ext, tt + CONV_HALO - j, 0)[0:tt, :] * wv[CONV_WIDTH - 1 - j:CONV_WIDTH - j, :]
            du_ref[pl.ds(start, tt), :] = acc.astype(BF16)
            return carry

        lax.fori_loop(0, t // tt, loop2, 0)

    col = pl.BlockSpec((t, tc), lambda j: (0, j))
    return _call(
        body, side, name="conv_bwd", grid=(c // tc,),
        in_specs=[col, pl.BlockSpec((CONV_WIDTH, tc), lambda j: (0, j)), pl.BlockSpec((1, tc), lambda j: (0, j)), col],
        out_specs=[col, pl.BlockSpec((CONV_WIDTH, tc), lambda j: (0, j)), pl.BlockSpec((1, tc), lambda j: (0, j))],
        out_shape=[jax.ShapeDtypeStruct((t, c), BF16), jax.ShapeDtypeStruct((CONV_WIDTH, c), F32),
                   jax.ShapeDtypeStruct((1, c), F32)],
        scratch_shapes=[pltpu.VMEM((t + CONV_HALO, tc), F32)],
        semantics=("parallel",), args=(u, w, b, dact),
    )


def _softplus(v):
    e = jnp.exp(-jnp.abs(v))
    w = 1.0 + e
    log1p = jnp.where(w == 1.0, e, jnp.log(w) * (e / jnp.where(w == 1.0, 1.0, w - 1.0)))
    return jnp.maximum(v, 0.0) + log1p


def _split3(z):
    hi = z.astype(BF16)
    rest = z - hi.astype(F32)
    mid = rest.astype(BF16)
    return hi, mid, (rest - mid.astype(F32)).astype(BF16)


def _select_dot(z, ones):
    return sum(_dot(term, ones) for term in _split3(z))


def _ssd_prep_fwd(dt_raw, dt_bias, a_log):
    t = dt_raw.shape[0]
    cl = SSD_CHUNK

    def body(r_ref, b_ref, al_ref, acs_ref, dt_rep_ref, acs_rep_ref):
        dt = _softplus(r_ref[...] + b_ref[...])
        adt = dt * (-jnp.exp(al_ref[...]))
        li = lax.broadcasted_iota(jnp.int32, (cl, cl), 0)
        si = lax.broadcasted_iota(jnp.int32, (cl, cl), 1)
        tri = (si <= li).astype(F32)
        acs = jnp.dot(tri, adt, preferred_element_type=F32, precision=HIGHEST)
        acs_ref[...] = acs
        head = lax.broadcasted_iota(jnp.int32, (LANES, D_INNER), 0)
        chan = lax.broadcasted_iota(jnp.int32, (LANES, D_INNER), 1) // SSM_HEAD_DIM
        spread = (head == chan).astype(BF16)
        dt_rep_ref[...] = _select_dot(dt, spread)
        acs_rep_ref[...] = _select_dot(acs, spread)

    row = pl.BlockSpec((cl, LANES), lambda i: (i, 0))
    wide = pl.BlockSpec((cl, D_INNER), lambda i: (i, 0))
    vec = pl.BlockSpec((1, LANES), lambda i: (0, 0))
    return pl.pallas_call(
        body, name="ssd_prep_fwd", grid=(t // cl,),
        in_specs=[row, vec, vec], out_specs=[row, wide, wide],
        out_shape=[jax.ShapeDtypeStruct((t, LANES), F32), jax.ShapeDtypeStruct((t, D_INNER), F32),
                   jax.ShapeDtypeStruct((t, D_INNER), F32)],
        compiler_params=_params("parallel"),
    )(dt_raw, dt_bias, a_log)


def _ssd_prep_bwd(dt_raw, dt_bias, ddt):
    t = dt_raw.shape[0]
    tm = _tile(t, 512)

    def body(r_ref, b_ref, d_ref, o_ref, db_ref):
        g = d_ref[...] * _sigmoid(r_ref[...] + b_ref[...])
        o_ref[...] = g.astype(BF16)
        part = jnp.sum(g, axis=0, keepdims=True)

        @pl.when(pl.program_id(0) == 0)
        def _():
            db_ref[...] = part

        @pl.when(pl.program_id(0) > 0)
        def _():
            db_ref[...] += part

    row = pl.BlockSpec((tm, LANES), lambda i: (i, 0))
    vec = pl.BlockSpec((1, LANES), lambda i: (0, 0))
    return pl.pallas_call(
        body, name="ssd_prep_bwd", grid=(t // tm,),
        in_specs=[row, vec, row], out_specs=[row, vec],
        out_shape=[jax.ShapeDtypeStruct((t, LANES), BF16), jax.ShapeDtypeStruct((1, LANES), F32)],
        compiler_params=_params("arbitrary"),
    )(dt_raw, dt_bias, ddt)


GROUP_W = D_INNER // SSM_GROUPS
PAIRS_PER_GROUP = GROUP_W // LANES


def _head_cols(acs_pair, lt64):
    rolled = pltpu.roll(acs_pair, ATT_HEAD_DIM, 1)
    return jnp.where(lt64, acs_pair, rolled), jnp.where(lt64, rolled, acs_pair)


def _ssd_fwd(xbc, dt_rep, acs_rep, acs_t, dskip_rep, side=None):
    t = xbc.shape[0]
    cl = SSD_CHUNK
    nc = t // cl

    def body(xbc_ref, dt_ref, acs_ref, acst_ref, dskip_ref, y_ref, hin_ref, state_ref):
        @pl.when(pl.program_id(0) == 0)
        def _():
            state_ref[...] = jnp.zeros_like(state_ref)

        lt64 = _lane_lt64(cl)
        li = lax.broadcasted_iota(jnp.int32, (cl, cl), 0)
        si = lax.broadcasted_iota(jnp.int32, (cl, cl), 1)
        causal = li >= si
        hin_ref[...] = state_ref[...]
        for g in range(SSM_GROUPS):
            gsl = slice(g * GROUP_W, (g + 1) * GROUP_W)
            xg = xbc_ref[:, gsl]
            bg = xbc_ref[:, D_INNER + g * SSM_STATE:D_INNER + (g + 1) * SSM_STATE]
            cg = xbc_ref[:, D_INNER + SSM_GROUPS * SSM_STATE + g * SSM_STATE:
                         D_INNER + SSM_GROUPS * SSM_STATE + (g + 1) * SSM_STATE]
            acs = acs_ref[:, gsl]
            xdt = xg * dt_ref[:, gsl]
            atot = acs[cl - 1:cl, :]
            hin = state_ref[:, gsl]
            cgb = cg.astype(BF16)
            gmat = _dot_nt(cgb, bg.astype(BF16))
            yoff = _dot(cgb, hin.astype(BF16)) * jnp.exp(acs)
            snew = _dot(bg.T.astype(BF16), (xdt * jnp.exp(atot - acs)).astype(BF16))
            state_ref[:, gsl] = hin * jnp.exp(atot) + snew
            xdtb = xdt.astype(BF16)
            for pr in range(PAIRS_PER_GROUP):
                psl = slice(pr * LANES, (pr + 1) * LANES)
                cols = _head_cols(acs[:, psl], lt64)
                xp = xdtb[:, psl]
                ys = []
                for hh in range(2):
                    h = (g * PAIRS_PER_GROUP + pr) * 2 + hh
                    seg = cols[hh] - acst_ref[h:h + 1, :]
                    lm = jnp.exp(jnp.where(causal, seg, NEG_BIG))
                    ys.append(_dot((gmat * lm).astype(BF16), xp))
                ydiag = jnp.where(lt64, ys[0], ys[1])
                osl = slice(g * GROUP_W + pr * LANES, g * GROUP_W + (pr + 1) * LANES)
                y_ref[:, osl] = ydiag + yoff[:, psl] + xg[:, psl] * dskip_ref[:, osl]

    row = lambda w: pl.BlockSpec((cl, w), lambda c: (c, 0))
    return _call(
        body, side, name="ssd_fwd", grid=(nc,),
        in_specs=[row(CONV_DIM), row(D_INNER), row(D_INNER),
                  pl.BlockSpec((SSM_HEADS, cl), lambda c: (0, c)), pl.BlockSpec((1, D_INNER), lambda c: (0, 0))],
        out_specs=[row(D_INNER), pl.BlockSpec((None, SSM_STATE, D_INNER), lambda c: (c, 0, 0))],
        out_shape=[jax.ShapeDtypeStruct((t, D_INNER), F32), jax.ShapeDtypeStruct((nc, SSM_STATE, D_INNER), F32)],
        scratch_shapes=[pltpu.VMEM((SSM_STATE, D_INNER), F32)],
        semantics=("arbitrary",), args=(xbc, dt_rep, acs_rep, acs_t, dskip_rep),
    )


def _ssd_bwd(xbc, dt_rep, acs_rep, acs_t, dskip_rep, a_rep, hin_all, dy, side=None):
    t = xbc.shape[0]
    cl = SSD_CHUNK
    nc = t // cl

    def body(xbc_ref, dt_ref, acs_ref, acst_ref, dskip_ref, a_ref, hin_ref, dy_ref,
             dxbc_ref, ddt_ref, da_ref, dds_ref, dstate_ref, dacs_ref, dxs_ref):
        step = pl.program_id(0)

        @pl.when(step == 0)
        def _():
            dstate_ref[...] = jnp.zeros_like(dstate_ref)
            da_ref[...] = jnp.zeros_like(da_ref)
            dds_ref[...] = jnp.zeros_like(dds_ref)

        bd = _head_block_diag()
        lt64 = _lane_lt64(cl)
        li = lax.broadcasted_iota(jnp.int32, (cl, cl), 0)
        si = lax.broadcasted_iota(jnp.int32, (cl, cl), 1)
        lower = li >= si
        upper = si >= li
        last_row = lax.broadcasted_iota(jnp.int32, (cl, GROUP_W), 0) == cl - 1
        for g in range(SSM_GROUPS):
            gsl = slice(g * GROUP_W, (g + 1) * GROUP_W)
            bsl = slice(D_INNER + g * SSM_STATE, D_INNER + (g + 1) * SSM_STATE)
            csl = slice(D_INNER + SSM_GROUPS * SSM_STATE + g * SSM_STATE,
                        D_INNER + SSM_GROUPS * SSM_STATE + (g + 1) * SSM_STATE)
            xg = xbc_ref[:, gsl]
            bg = xbc_ref[:, bsl]
            cg = xbc_ref[:, csl]
            bgb, cgb = bg.astype(BF16), cg.astype(BF16)
            acs = acs_ref[:, gsl]
            xdt = xg * dt_ref[:, gsl]
            atot = acs[cl - 1:cl, :]
            eg = jnp.exp(acs)
            dk = jnp.exp(atot - acs)
            etot = jnp.exp(atot)
            hin = hin_ref[:, gsl]
            hinb = hin.astype(BF16)
            dh = dstate_ref[:, gsl]
            dhb = dh.astype(BF16)
            dyg = dy_ref[:, gsl]

            gmat = _dot_nt(cgb, bgb)
            gmat_t = _dot_nt(bgb, cgb)
            ch = _dot(cgb, hinb)
            dacs = _head_sums(dyg * ch * eg, bd)
            dye = (dyg * eg).astype(BF16)
            dc = _dot_nt(dye, hinb)
            dhin = _dot(cg.T.astype(BF16), dye)
            bdh = _dot(bgb, dhb)
            dxs = bdh * dk
            xdk = xdt * dk
            db = _dot_nt(xdk.astype(BF16), dhb)
            ddk = _head_sums(bdh * xdk, bd)
            dacs = dacs - ddk
            datot = jnp.sum(ddk, axis=0, keepdims=True) + etot * _head_sums(
                jnp.sum(dh * hin, axis=0, keepdims=True), bd)
            dacs = dacs + jnp.where(last_row, datot, 0.0)
            dstate_ref[:, gsl] = dh * etot + dhin

            xdtb = xdt.astype(BF16)
            dgsum = jnp.zeros((cl, cl), F32)
            dgsum_t = jnp.zeros((cl, cl), F32)
            for pr in range(PAIRS_PER_GROUP):
                psl = slice(pr * LANES, (pr + 1) * LANES)
                cols = _head_cols(acs[:, psl], lt64)
                xp = xdtb[:, psl]
                dyp = dyg[:, psl].astype(BF16)
                dx1, dac = [], []
                for hh in range(2):
                    h = (g * PAIRS_PER_GROUP + pr) * 2 + hh
                    mine = lt64 if hh == 0 else jnp.logical_not(lt64)
                    row = acst_ref[h:h + 1, :]
                    lm = jnp.exp(jnp.where(lower, cols[hh] - row, NEG_BIG))
                    lm_t = jnp.exp(jnp.where(upper, row - cols[hh], NEG_BIG))
                    dyh = jnp.where(mine, dyp, jnp.zeros_like(dyp))
                    xh = jnp.where(mine, xp, jnp.zeros_like(xp))
                    dm = _dot_nt(dyh, xp)
                    dm_t = _dot_nt(xh, dyp)
                    m_t = gmat_t * lm_t
                    dx1.append(_dot(m_t.astype(BF16), dyp))
                    w = dm * (gmat * lm)
                    w_t = dm_t * m_t
                    dac.append(jnp.sum(w, axis=1, keepdims=True) - jnp.sum(w_t, axis=1, keepdims=True))
                    dgsum = dgsum + dm * lm
                    dgsum_t = dgsum_t + dm_t * lm_t
                osl = slice(g * GROUP_W + pr * LANES, g * GROUP_W + (pr + 1) * LANES)
                dxs_ref[:, osl] = dxs[:, psl] + jnp.where(lt64, dx1[0], dx1[1])
                dacs_ref[:, osl] = dacs[:, psl] + jnp.where(lt64, jnp.broadcast_to(dac[0], (cl, LANES)),
                                                             jnp.broadcast_to(dac[1], (cl, LANES)))
            dxbc_ref[:, csl] = dc + _dot(dgsum.astype(BF16), bgb)
            dxbc_ref[:, bsl] = db + _dot(dgsum_t.astype(BF16), cgb)

        dadt = _split_dot(upper.astype(BF16), dacs_ref[...])
        xall = xbc_ref[:, 0:D_INNER]
        dtall = dt_ref[...]
        dxsall = dxs_ref[...]
        dyall = dy_ref[...]
        ddt_rep = dadt * a_ref[...] + _head_sums(dxsall * xall, bd)
        chan = lax.broadcasted_iota(jnp.int32, (D_INNER, LANES), 0)
        head = lax.broadcasted_iota(jnp.int32, (D_INNER, LANES), 1)
        ddt_ref[...] = _select_dot(ddt_rep, (chan == head * SSM_HEAD_DIM).astype(BF16))
        dxbc_ref[:, 0:D_INNER] = dxsall * dtall + dyall * dskip_ref[...]
        da_ref[...] += jnp.sum(dadt * dtall, axis=0, keepdims=True)
        dds_ref[...] += jnp.sum(dyall * xall, axis=0, keepdims=True)

        @pl.when(step == nc - 1)
        def _():
            dds_ref[...] = _head_sums(dds_ref[...], bd)

    row = lambda w: pl.BlockSpec((cl, w), lambda c: (nc - 1 - c, 0))
    vec = pl.BlockSpec((1, D_INNER), lambda c: (0, 0))
    return _call(
        body, side, name="ssd_bwd", grid=(nc,),
        in_specs=[row(CONV_DIM), row(D_INNER), row(D_INNER),
                  pl.BlockSpec((SSM_HEADS, cl), lambda c: (0, nc - 1 - c)), vec, vec,
                  pl.BlockSpec((None, SSM_STATE, D_INNER), lambda c: (nc - 1 - c, 0, 0)), row(D_INNER)],
        out_specs=[row(CONV_DIM), row(LANES), vec, vec],
        out_shape=[jax.ShapeDtypeStruct((t, CONV_DIM), F32), jax.ShapeDtypeStruct((t, LANES), F32),
                   jax.ShapeDtypeStruct((1, D_INNER), F32), jax.ShapeDtypeStruct((1, D_INNER), F32)],
        scratch_shapes=[pltpu.VMEM((SSM_STATE, D_INNER), F32), pltpu.VMEM((cl, D_INNER), F32),
                        pltpu.VMEM((cl, D_INNER), F32)],
        semantics=("arbitrary",), args=(xbc, dt_rep, acs_rep, acs_t, dskip_rep, a_rep, hin_all, dy),
    )


def _gate_norm_fwd(y, z, w):
    t, c = y.shape
    tm = _tile(t, 256)

    def body(y_ref, z_ref, w_ref, o_ref):
        for g in range(SSM_GROUPS):
            gsl = slice(g * GROUP_W, (g + 1) * GROUP_W)
            zv = z_ref[:, gsl]
            v = y_ref[:, gsl] * (zv * _sigmoid(zv))
            r = lax.rsqrt(jnp.mean(v * v, axis=-1, keepdims=True) + NORM_EPS)
            o_ref[:, gsl] = (v * r * w_ref[:, gsl]).astype(BF16)

    row = pl.BlockSpec((tm, c), lambda i: (i, 0))
    return pl.pallas_call(
        body, name="gate_norm_fwd", grid=(t // tm,),
        in_specs=[row, row, pl.BlockSpec((1, c), lambda i: (0, 0))], out_specs=row,
        out_shape=jax.ShapeDtypeStruct((t, c), BF16),
        compiler_params=_params("parallel"),
    )(y, z, w)


def _gate_norm_bwd(y, z, w, dout, side=None):
    t, c = y.shape
    tm = _tile(t, 256)

    def body(y_ref, z_ref, w_ref, do_ref, dy_ref, dz_ref, dw_ref):
        @pl.when(pl.program_id(0) == 0)
        def _():
            dw_ref[...] = jnp.zeros_like(dw_ref)

        for g in range(SSM_GROUPS):
            gsl = slice(g * GROUP_W, (g + 1) * GROUP_W)
            zv, yv, dov = z_ref[:, gsl], y_ref[:, gsl], do_ref[:, gsl]
            sg = _sigmoid(zv)
            sz = zv * sg
            v = yv * sz
            r = lax.rsqrt(jnp.mean(v * v, axis=-1, keepdims=True) + NORM_EPS)
            vh = v * r
            dvh = dov * w_ref[:, gsl]
            mean = jnp.mean(dvh * vh, axis=-1, keepdims=True)
            dv = r * (dvh - vh * mean)
            dy_ref[:, gsl] = dv * sz
            dz_ref[:, gsl] = (dv * yv * (sg * (1.0 + zv * (1.0 - sg)))).astype(BF16)
            dw_ref[:, gsl] += jnp.sum(dov * vh, axis=0, keepdims=True)

    row = pl.BlockSpec((tm, c), lambda i: (i, 0))
    vec = pl.BlockSpec((1, c), lambda i: (0, 0))
    return _call(
        body, side, name="gate_norm_bwd", grid=(t // tm,),
        in_specs=[row, row, vec, row], out_specs=[row, row, vec],
        out_shape=[jax.ShapeDtypeStruct((t, c), F32), jax.ShapeDtypeStruct((t, c), BF16),
                   jax.ShapeDtypeStruct((1, c), F32)],
        scratch_shapes=[], semantics=("arbitrary",), args=(y, z, w, dout),
    )


ATT_W = ATT_HEADS * ATT_HEAD_DIM
N_QKV_BLOCKS = 9
ATT_SCALE = 1.0 / math.sqrt(ATT_HEAD_DIM)


def _head_rmsnorm(x, gain, bd):
    ms = _head_sums(x * x, bd, terms=1) * (1.0 / ATT_HEAD_DIM)
    return x * lax.rsqrt(ms + NORM_EPS) * gain


def _class_rows(ref, blk, r, dil):
    span = ATT_BLOCK * dil
    sub = ref.at[pl.ds(pl.multiple_of(blk * span, span), span), :]
    return sub[...] if dil == 1 else sub[pl.ds(r, ATT_BLOCK, stride=dil), :]


def _store_class_rows(ref, blk, r, dil, val):
    span = ATT_BLOCK * dil
    sub = ref.at[pl.ds(pl.multiple_of(blk * span, span), span), :]
    if dil == 1:
        sub[...] = val
    else:
        sub[pl.ds(r, ATT_BLOCK, stride=dil), :] = val


PAIRS = ATT_HEADS // 2


def _pair_col(g, j):
    return lambda pair: (0, (g * 3 + j) * PAIRS + pair)


def _pair_slopes(pair):
    steps = jnp.full((1, 2 * ATT_BLOCK), 2 * pair + 1, jnp.int32).astype(F32)
    first = jnp.exp(steps * (-0.5 * math.log(2.0)))
    return first, first * (2.0 ** -0.5)


NORM_ROWS = 512


ROW_SLICES = 4
SLICE_ROWS = 2 * ATT_BLOCK // ROW_SLICES


def _fill_band_bias(bias_ref, pair, dil, transposed):
    bq = ATT_BLOCK
    a = lax.broadcasted_iota(jnp.int32, (2 * bq, 2 * bq), 0) % bq
    b = lax.broadcasted_iota(jnp.int32, (2 * bq, 2 * bq), 1)
    dist = (b - a) if transposed else (a + bq - b)
    in_band = (dist >= 0) & (dist <= bq)
    s0, s1 = _pair_slopes(pair)
    first_head = lax.broadcasted_iota(jnp.int32, (2 * bq, 2 * bq), 0) < bq
    bias = jnp.where(first_head, s0, s1) * (dist.astype(F32) * float(dil))
    inside = (b < bq) if transposed else (b >= bq)
    bias_ref[1] = jnp.where(in_band, bias, -NEG_BIG)
    bias_ref[0] = jnp.where(in_band & inside, bias, -NEG_BIG)


def _row_slices():
    return [slice(i * SLICE_ROWS, (i + 1) * SLICE_ROWS) for i in range(ROW_SLICES)]


def _stack_heads(tile):
    rows = lax.broadcasted_iota(jnp.int32, (2 * ATT_BLOCK, LANES), 0) < ATT_BLOCK
    lanes = lax.broadcasted_iota(jnp.int32, (2 * ATT_BLOCK, LANES), 1) < ATT_HEAD_DIM
    both = jnp.concatenate([tile, tile], axis=0)
    return jnp.where(rows == lanes, both, jnp.zeros_like(both))


def _unstack_heads(stacked, lt64):
    return jnp.where(lt64, stacked[:ATT_BLOCK], stacked[ATT_BLOCK:])


ITEMS_PER_PASS = 4


def _item_loop(nb, dil, work):
    if dil == 1:
        def trip(i, carry):
            work([(i * ITEMS_PER_PASS + b, 0) for b in range(ITEMS_PER_PASS)])
            return carry

        lax.fori_loop(0, nb // ITEMS_PER_PASS, trip, 0)
    else:
        def trip(n, carry):
            for r0 in range(0, dil, ITEMS_PER_PASS):
                work([(n, r0 + j) for j in range(ITEMS_PER_PASS)])
            return carry

        lax.fori_loop(0, nb, trip, 0)


def _qk_normalised(tile, j, gq_ref, gk_ref):
    kind = (j // (ATT_W // tile.shape[1])) % 3
    gain = jnp.where(kind == 0, gq_ref[...] * ATT_SCALE, gk_ref[...])
    return jnp.where(kind == 2, tile, _head_rmsnorm(tile, gain, _head_block_diag()))


def _attn_fwd(qkn, g, dil):
    t = qkn.shape[0]
    nb = t // dil // ATT_BLOCK
    bq = ATT_BLOCK

    def body(qn_ref, kn_ref, v_ref, o_ref, l_ref, bias_ref):
        _fill_band_bias(bias_ref, pl.program_id(0), dil, False)
        lt64 = _lane_lt64(bq)

        def work(items):
            scores, values, probs = [], [], []
            for n, r in items:
                prev = jnp.maximum(n - 1, 0)
                q2 = _stack_heads(_class_rows(qn_ref, n, r, dil).astype(BF16))
                kcat = jnp.concatenate([_class_rows(kn_ref, prev, r, dil), _class_rows(kn_ref, n, r, dil)],
                                       axis=0).astype(BF16)
                values.append(jnp.concatenate([_class_rows(v_ref, prev, r, dil), _class_rows(v_ref, n, r, dil)],
                                              axis=0).astype(BF16))
                scores.append(_dot_nt(q2, kcat))
            for (n, r), sc in zip(items, scores):
                bias = bias_ref.at[jnp.minimum(n, 1)]
                ps, inv, lses = [], [], []
                for rows in _row_slices():
                    s = sc[rows] - bias[rows, :]
                    m = jnp.max(s, axis=1, keepdims=True)
                    p = jnp.exp(s - m)
                    l = jnp.sum(p, axis=1, keepdims=True)
                    ps.append(p.astype(BF16))
                    inv.append(jnp.broadcast_to(1.0 / l, (SLICE_ROWS, LANES)))
                    lses.append(jnp.broadcast_to(m + jnp.log(l), (SLICE_ROWS, LANES)))
                probs.append((jnp.concatenate(ps, axis=0), jnp.concatenate(inv, axis=0)))
                _store_class_rows(l_ref, n, r, dil, _unstack_heads(jnp.concatenate(lses, axis=0), lt64))
            for (n, r), (p, inv), vcat in zip(items, probs, values):
                _store_class_rows(o_ref, n, r, dil, _unstack_heads(_dot(p, vcat) * inv, lt64))

        _item_loop(nb, dil, work)

    col = lambda j: pl.BlockSpec((t, LANES), _pair_col(g, j))
    out = pl.BlockSpec((t, LANES), lambda pair: (0, pair))
    return pl.pallas_call(
        body, name=f"attn_fwd_g{g}", grid=(PAIRS,),
        in_specs=[col(0), col(1), col(2)], out_specs=[out, out],
        out_shape=[jax.ShapeDtypeStruct((t, ATT_W), F32), jax.ShapeDtypeStruct((t, ATT_W), F32)],
        scratch_shapes=[pltpu.VMEM((2, 2 * bq, 2 * bq), F32)],
        compiler_params=_params("parallel"),
    )(qkn, qkn, qkn)


def _one_per_head(rep):
    chan = lax.broadcasted_iota(jnp.int32, (ATT_W, LANES), 0)
    head = lax.broadcasted_iota(jnp.int32, (ATT_W, LANES), 1)
    return _select_dot(rep, (chan == head * ATT_HEAD_DIM).astype(BF16))


def _attn_combine_fwd(outs, lses):
    t = outs[0].shape[0]
    tm = _tile(t, 256)

    def body(o0, o1, o2, l0, l1, l2, ob_ref, of_ref, lt_ref, lc_ref):
        a, b, c = l0[...], l1[...], l2[...]
        m = jnp.maximum(jnp.maximum(a, b), c)
        ea, eb, ec = jnp.exp(a - m), jnp.exp(b - m), jnp.exp(c - m)
        ssum = ea + eb + ec
        o = (ea * o0[...] + eb * o1[...] + ec * o2[...]) / ssum
        ob_ref[...] = o.astype(BF16)
        of_ref[...] = o
        lse = m + jnp.log(ssum)
        lt_ref[...] = lse
        lc_ref[...] = _one_per_head(lse)

    row = pl.BlockSpec((tm, ATT_W), lambda i: (i, 0))
    return pl.pallas_call(
        body, name="attn_combine_fwd", grid=(t // tm,),
        in_specs=[row] * 6, out_specs=[row] * 3 + [pl.BlockSpec((tm, LANES), lambda i: (i, 0))],
        out_shape=[jax.ShapeDtypeStruct((t, ATT_W), BF16), jax.ShapeDtypeStruct((t, ATT_W), F32),
                   jax.ShapeDtypeStruct((t, ATT_W), F32), jax.ShapeDtypeStruct((t, LANES), F32)],
        compiler_params=_params("parallel"),
    )(*outs, *lses)


def _attn_combine_bwd(do, o):
    t = do.shape[0]
    tm = _tile(t, 256)

    def body(do_ref, o_ref, dl_ref, dc_ref):
        dl = _head_sums(do_ref[...] * o_ref[...], _head_block_diag())
        dl_ref[...] = dl
        dc_ref[...] = _one_per_head(dl)

    row = pl.BlockSpec((tm, ATT_W), lambda i: (i, 0))
    return pl.pallas_call(
        body, name="attn_combine_bwd", grid=(t // tm,),
        in_specs=[row, row], out_specs=[row, pl.BlockSpec((tm, LANES), lambda i: (i, 0))],
        out_shape=[jax.ShapeDtypeStruct((t, ATT_W), F32), jax.ShapeDtypeStruct((t, LANES), F32)],
        compiler_params=_params("parallel"),
    )(do, o)


def _head_rmsnorm_bwd(x_ref, dy_ref, gain_ref, dx_ref, dgain_ref):
    bd = _head_block_diag()
    gain = gain_ref[...]

    def step(i, acc):
        rows = pl.ds(pl.multiple_of(i * NORM_ROWS, NORM_ROWS), NORM_ROWS)
        x, dy = x_ref[rows, :], dy_ref[rows, :]
        r = lax.rsqrt(_head_sums(x * x, bd, terms=1) * (1.0 / ATT_HEAD_DIM) + NORM_EPS)
        xh = x * r
        dxh = dy * gain
        mean = _head_sums(dxh * xh, bd, terms=1) * (1.0 / ATT_HEAD_DIM)
        dx_ref[rows, :] = (r * (dxh - xh * mean)).astype(BF16)
        return acc + jnp.sum(dy * xh, axis=0, keepdims=True)

    acc = lax.fori_loop(0, x_ref.shape[0] // NORM_ROWS, step, jnp.zeros((1, LANES), F32))
    dgain_ref[...] = jnp.broadcast_to(acc, dgain_ref.shape)


def _attn_bwd_dq(qkv, qkn, gq, do, l_rep, dl_rep, g, dil):
    t = qkv.shape[0]
    nb = t // dil // ATT_BLOCK
    bq = ATT_BLOCK

    def body(q_ref, qn_ref, kn_ref, v_ref, gq_ref, do_ref, l_ref, dl_ref, dx_ref, dgain_ref, bias_ref, dq_ref):
        _fill_band_bias(bias_ref, pl.program_id(0), dil, False)
        lt64 = _lane_lt64(bq)

        def per_row(tile):
            cols = _head_cols(tile, lt64)
            half = jnp.concatenate([cols[0], cols[1]], axis=0)
            return jnp.concatenate([half, half], axis=1)

        def work(items):
            products, keys, dscores = [], [], []
            for n, r in items:
                prev = jnp.maximum(n - 1, 0)
                q2 = _stack_heads(_class_rows(qn_ref, n, r, dil).astype(BF16))
                do2 = _stack_heads(_class_rows(do_ref, n, r, dil).astype(BF16))
                kcat = jnp.concatenate([_class_rows(kn_ref, prev, r, dil), _class_rows(kn_ref, n, r, dil)],
                                       axis=0).astype(BF16)
                vcat = jnp.concatenate([_class_rows(v_ref, prev, r, dil), _class_rows(v_ref, n, r, dil)],
                                       axis=0).astype(BF16)
                keys.append(kcat)
                products.append((_dot_nt(q2, kcat), _dot_nt(do2, vcat)))
            for (n, r), (scores, dps) in zip(items, products):
                bias = bias_ref.at[jnp.minimum(n, 1)]
                lse = per_row(_class_rows(l_ref, n, r, dil))
                dl = per_row(_class_rows(dl_ref, n, r, dil))
                dss = []
                for rows in _row_slices():
                    p = jnp.exp(scores[rows] - bias[rows, :] - lse[rows])
                    dss.append((p * (dps[rows] - dl[rows])).astype(BF16))
                dscores.append(jnp.concatenate(dss, axis=0))
            for (n, r), ds, kcat in zip(items, dscores, keys):
                _store_class_rows(dq_ref, n, r, dil, _unstack_heads(_dot(ds, kcat) * ATT_SCALE, lt64))

        _item_loop(nb, dil, work)
        _head_rmsnorm_bwd(q_ref, dq_ref, gq_ref, dx_ref, dgain_ref)

    col = lambda j: pl.BlockSpec((t, LANES), _pair_col(g, j))
    vec = pl.BlockSpec((1, LANES), lambda pair: (0, 0))
    tok = pl.BlockSpec((t, LANES), lambda pair: (0, pair))
    return pl.pallas_call(
        body, name=f"attn_bwd_dq_g{g}", grid=(PAIRS,),
        in_specs=[col(0), col(0), col(1), col(2), vec, tok, tok, tok],
        out_specs=[tok, pl.BlockSpec((None, 8, LANES), lambda pair: (pair, 0, 0))],
        out_shape=[jax.ShapeDtypeStruct((t, ATT_W), BF16), jax.ShapeDtypeStruct((PAIRS, 8, LANES), F32)],
        scratch_shapes=[pltpu.VMEM((2, 2 * bq, 2 * bq), F32), pltpu.VMEM((t, LANES), F32)],
        compiler_params=_params("parallel"),
    )(qkv, qkn, qkn, qkn, gq, do, l_rep, dl_rep)


def _attn_bwd_dkv(qkv, qkn, gk, do, l_row, dl_row, g, dil):
    t = qkv.shape[0]
    nb = t // dil // ATT_BLOCK
    bq = ATT_BLOCK

    def body(k_ref, qn_ref, kn_ref, v_ref, gk_ref, do_ref, l_ref, dl_ref, dkx_ref, dvx_ref, dgain_ref, bias_ref,
             dk_ref, dv_ref):
        _fill_band_bias(bias_ref, pl.program_id(0), dil, True)
        lt64 = _lane_lt64(bq)

        def per_query(ref, hh, lane_c, lane_n):
            return jnp.concatenate([ref[hh:hh + 1, pl.ds(lane_c, bq)], ref[hh:hh + 1, pl.ds(lane_n, bq)]], axis=1)

        def work(items):
            products, operands, weights = [], [], []
            for n, r in items:
                nxt = jnp.minimum(n + 1, nb - 1)
                k2 = _stack_heads(_class_rows(kn_ref, n, r, dil).astype(BF16))
                v2 = _stack_heads(_class_rows(v_ref, n, r, dil).astype(BF16))
                qcat = jnp.concatenate([_class_rows(qn_ref, n, r, dil), _class_rows(qn_ref, nxt, r, dil)],
                                       axis=0).astype(BF16)
                docat = jnp.concatenate([_class_rows(do_ref, n, r, dil), _class_rows(do_ref, nxt, r, dil)],
                                        axis=0).astype(BF16)
                operands.append((qcat, docat))
                products.append((_dot_nt(k2, qcat), _dot_nt(v2, docat)))
            for (n, r), (scores, dps) in zip(items, products):
                nxt = jnp.minimum(n + 1, nb - 1)
                bias = bias_ref.at[jnp.where(n == nb - 1, 0, 1)]
                lane_c = pl.multiple_of((r * nb + n) * bq, bq)
                lane_n = pl.multiple_of((r * nb + nxt) * bq, bq)
                lse = [per_query(l_ref, hh, lane_c, lane_n) for hh in range(2)]
                dl = [per_query(dl_ref, hh, lane_c, lane_n) for hh in range(2)]
                pts, dss = [], []
                for i, rows in enumerate(_row_slices()):
                    hh = i * SLICE_ROWS // bq
                    p_t = jnp.exp(scores[rows] - bias[rows, :] - lse[hh])
                    pts.append(p_t.astype(BF16))
                    dss.append((p_t * (dps[rows] - dl[hh])).astype(BF16))
                weights.append((jnp.concatenate(pts, axis=0), jnp.concatenate(dss, axis=0)))
            for (n, r), (p_t, ds_t), (qcat, docat) in zip(items, weights, operands):
                _store_class_rows(dv_ref, n, r, dil, _unstack_heads(_dot(p_t, docat), lt64))
                _store_class_rows(dk_ref, n, r, dil, _unstack_heads(_dot(ds_t, qcat), lt64))

        _item_loop(nb, dil, work)
        _head_rmsnorm_bwd(k_ref, dk_ref, gk_ref, dkx_ref, dgain_ref)

        def cast_rows(i, carry):
            rows = pl.ds(pl.multiple_of(i * NORM_ROWS, NORM_ROWS), NORM_ROWS)
            dvx_ref[rows, :] = dv_ref[rows, :].astype(BF16)
            return carry

        lax.fori_loop(0, t // NORM_ROWS, cast_rows, 0)

    col = lambda j: pl.BlockSpec((t, LANES), _pair_col(g, j))
    vec = pl.BlockSpec((1, LANES), lambda pair: (0, 0))
    tok = pl.BlockSpec((t, LANES), lambda pair: (0, pair))
    rows = pl.BlockSpec((None, 8, t), lambda pair: (pair, 0, 0))
    return pl.pallas_call(
        body, name=f"attn_bwd_dkv_g{g}", grid=(PAIRS,),
        in_specs=[col(1), col(0), col(1), col(2), vec, tok, rows, rows],
        out_specs=[tok, tok, pl.BlockSpec((None, 8, LANES), lambda pair: (pair, 0, 0))],
        out_shape=[jax.ShapeDtypeStruct((t, ATT_W), BF16), jax.ShapeDtypeStruct((t, ATT_W), BF16),
                   jax.ShapeDtypeStruct((PAIRS, 8, LANES), F32)],
        scratch_shapes=[pltpu.VMEM((2, 2 * bq, 2 * bq), F32), pltpu.VMEM((t, LANES), F32),
                        pltpu.VMEM((t, LANES), F32)],
        compiler_params=_params("parallel"),
    )(qkv, qkn, qkn, qkn, gk, do, l_row, dl_row)


def _rows_by_residue(one_per_head, dil):
    t = one_per_head.shape[0]
    per_head = one_per_head[:, :ATT_HEADS]
    rows = per_head.reshape(t // dil, dil, ATT_HEADS).transpose(2, 1, 0).reshape(PAIRS, 2, t)
    return jnp.pad(rows, ((0, 0), (0, 6), (0, 0)))


def _per_head(rep_row):
    return rep_row[0, ::SSM_HEAD_DIM]


def _rep_heads(v):
    return jnp.repeat(v, SSM_HEAD_DIM)[None, :]


def _pad_lanes(v):
    return jnp.pad(v, ((0, 0), (0, LANES - v.shape[1])))


class _NoOverlap:
    def side(self, host):
        return None

    def after(self, host):
        pass

    def begin_backward(self, grads):
        pass


def _hosted(plan, host, fn, *args, **kwargs):
    out = fn(*args, side=plan.side(host), **kwargs)
    plan.after(host)
    return out


def _ffn_ple_fwd(x1, h, p_i, prm, i, plan):
    g, u, act = _hosted(plan, f"swiglu_fwd_{i}", _swiglu_fwd, h, prm["ffn_w_gate"][i], prm["ffn_w_up"][i],
                        name=f"swiglu_fwd_{i}")
    x2 = _hosted(plan, f"ffn_down_{i}", _matmul, act, prm["ffn_w_down"][i], mode="nn", addend=x1,
                 name=f"ffn_down_{i}")
    x3 = _ple_fwd(x2, p_i, prm["ple_w_gate"][i], prm["ple_w_proj"][i], name=f"ple_fwd_{i}")
    return x3, dict(x1=x1, h=h, g=g, u=u, act=act, x2=x2)


def _ffn_ple_bwd(dx3, p_i, prm, i, sv, grads, plan):
    ds, dple = _ple_bwd(sv["x2"], p_i, prm["ple_w_gate"][i], prm["ple_w_proj"][i], dx3, name=f"ple_bwd_{i}")
    grads["ple_w_gate"][i] = _matmul_tn(sv["x2"], ds, name=f"d_ple_w_gate_{i}")
    grads["ple_w_proj"][i] = _matmul_tn(dple, p_i, name=f"d_ple_w_proj_{i}")
    dx2 = _matmul(ds, prm["ple_w_gate"][i], mode="nt", addend=dx3, name=f"ple_dx_{i}")
    grads["ffn_w_down"][i] = _matmul_tn(sv["act"], dx2, name=f"d_ffn_w_down_{i}")
    dg, du = _hosted(plan, f"swiglu_bwd_{i}", _swiglu_bwd, dx2, prm["ffn_w_down"][i], sv["g"], sv["u"],
                     name=f"swiglu_bwd_{i}")
    grads["ffn_w_gate"][i] = _matmul_tn(dg, sv["h"], name=f"d_ffn_w_gate_{i}")
    grads["ffn_w_up"][i] = _matmul_tn(du, sv["h"], name=f"d_ffn_w_up_{i}")
    dh = _matmul(dg, prm["ffn_w_gate"][i], mode="nn", name=f"ffn_dh_gate_{i}")
    dx1, dgain = _matmul_rmsnorm_bwd(du, prm["ffn_w_up"][i], dh, sv["x1"], prm["norm_ffn"][i:i + 1], dx2,
                                     name=f"ffn_dh_up_{i}")
    grads["norm_ffn"][i] = dgain[0]
    return dx1


def _mamba_fwd(x0, prm, plan):
    h = _rmsnorm_fwd(x0, prm["norm_mix"][0:1], name="mix_norm_fwd_0")
    z = _hosted(plan, "ssm_in_z", _matmul, h, prm["ssm_w_z"], mode="nt", name="ssm_in_z")
    xbc_pre = _hosted(plan, "ssm_in_xbc", _matmul, h, prm["ssm_w_xbc"], mode="nt", name="ssm_in_xbc")
    dt_raw = _matmul(h, prm["ssm_w_dt"], mode="nt", name="ssm_in_dt")
    xbc = _hosted(plan, "conv_fwd", _conv_fwd, xbc_pre, prm["ssm_conv_w"], prm["ssm_conv_b"])
    dt_bias = _pad_lanes(prm["ssm_dt_bias"])
    a_log = _pad_lanes(prm["ssm_a_log"])
    acs, dt_rep, acs_rep = _ssd_prep_fwd(dt_raw, dt_bias, a_log)
    acs_t = acs[:, :SSM_HEADS].T
    dskip_rep = _rep_heads(prm["ssm_d_skip"][0])
    y, hin_all = _hosted(plan, "ssd_fwd", _ssd_fwd, xbc, dt_rep, acs_rep, acs_t, dskip_rep)
    yn = _gate_norm_fwd(y, z, prm["ssm_norm_w"])
    x1, h_ffn = _matmul(yn, prm["ssm_w_out"], mode="nn", addend=x0, name="ssm_out", tm=512, tn=D_MODEL,
                        second=(_rmsnorm_rows, [prm["norm_ffn"][0:1]], BF16))
    sv = dict(x0=x0, h=h, z=z, xbc_pre=xbc_pre, dt_raw=dt_raw, xbc=xbc, dt_bias=dt_bias, dt_rep=dt_rep,
              acs_rep=acs_rep, acs_t=acs_t, dskip_rep=dskip_rep, y=y, hin_all=hin_all, yn=yn)
    return x1, h_ffn, sv


def _mamba_bwd(dx1, prm, sv, grads, plan):
    grads["ssm_w_out"] = _matmul_tn(sv["yn"], dx1, name="d_ssm_w_out")
    dyn = _matmul(dx1, prm["ssm_w_out"], mode="nt", name="ssm_out_dx")
    dy, dz, dnw = _hosted(plan, "gate_norm_bwd", _gate_norm_bwd, sv["y"], sv["z"], prm["ssm_norm_w"], dyn)
    grads["ssm_norm_w"] = dnw
    a_rep = _rep_heads(-jnp.exp(prm["ssm_a_log"][0]))
    dxbc, ddt, da_rep, dds_rep = _hosted(plan, "ssd_bwd", _ssd_bwd, sv["xbc"], sv["dt_rep"], sv["acs_rep"],
                                             sv["acs_t"], sv["dskip_rep"], a_rep, sv["hin_all"], dy)
    grads["ssm_d_skip"] = _per_head(dds_rep)[None, :]
    grads["ssm_a_log"] = (_per_head(da_rep) * _per_head(a_rep))[None, :]
    ddt_raw, dbias = _ssd_prep_bwd(sv["dt_raw"], sv["dt_bias"], ddt)
    grads["ssm_dt_bias"] = dbias[:, :SSM_HEADS]
    du, dcw, dcb = _hosted(plan, "conv_bwd", _conv_bwd, sv["xbc_pre"], prm["ssm_conv_w"], prm["ssm_conv_b"], dxbc)
    grads["ssm_conv_w"] = dcw
    grads["ssm_conv_b"] = dcb
    h = sv["h"]
    grads["ssm_w_in"] = jnp.concatenate(
        [_matmul_tn(dz, h, name="d_ssm_w_z"), _matmul_tn(du, h, name="d_ssm_w_xbc"),
         _matmul_tn(ddt_raw, h, name="d_ssm_w_dt")[:SSM_HEADS]], axis=0)
    dh = _hosted(plan, "ssm_dh_z", _matmul, dz, prm["ssm_w_z"], mode="nn", name="ssm_dh_z")
    dh = _hosted(plan, "ssm_dh_xbc", _matmul, du, prm["ssm_w_xbc"], mode="nn", addend=dh, name="ssm_dh_xbc")
    dx0, dgain = _hosted(plan, "ssm_dh_dt", _matmul_rmsnorm_bwd, ddt_raw, prm["ssm_w_dt"], dh, sv["x0"],
                         prm["norm_mix"][0:1], dx1, name="ssm_dh_dt")
    grads["norm_mix"][0] = dgain[0]
    return dx0


def _attn_mixer_fwd(x0, prm, plan):
    h = _rmsnorm_fwd(x0, prm["norm_mix"][1:2], name="mix_norm_fwd_1")
    n_heads = N_QKV_BLOCKS * ATT_HEADS
    gq = jnp.tile(prm["att_q_norm"], (1, n_heads))
    gk = jnp.tile(prm["att_k_norm"], (1, n_heads))
    qkv, qkn = _hosted(plan, "att_qkv", _matmul, h, prm["att_w_qkv"], mode="nt", name="att_qkv",
                       second=(_qk_normalised, [gq, gk], F32))
    outs, lses = [], []
    for g, (window, dil) in enumerate(DIL_PATTERNS):
        o_g, l_g = _attn_fwd(qkn, g, dil)
        outs.append(o_g)
        lses.append(l_g)
    o_b, o_f, l_rep, l_one = _attn_combine_fwd(outs, lses)
    x1, h_ffn = _matmul(o_b, prm["att_w_o"], mode="nn", addend=x0, name="att_out", tm=512, tn=D_MODEL,
                        second=(_rmsnorm_rows, [prm["norm_ffn"][1:2]], BF16))
    sv = dict(x0=x0, h=h, qkv=qkv, qkn=qkn, gq2=gq[:, :LANES], gk2=gk[:, :LANES], o_b=o_b, o_f=o_f, l_rep=l_rep,
              l_one=l_one)
    return x1, h_ffn, sv


def _attn_mixer_bwd(dx1, prm, sv, grads, plan):
    grads["att_w_o"] = _matmul_tn(sv["o_b"], dx1, name="d_att_w_o")
    do = _hosted(plan, "att_out_dx", _matmul, dx1, prm["att_w_o"], mode="nt", name="att_out_dx")
    dl_rep, dl_one = _attn_combine_bwd(do, sv["o_f"])
    blocks, dgq, dgk = [], [], []
    for g, (window, dil) in enumerate(DIL_PATTERNS):
        dq, dgq_g = _attn_bwd_dq(sv["qkv"], sv["qkn"], sv["gq2"], do, sv["l_rep"], dl_rep, g, dil)
        dk, dv, dgk_g = _attn_bwd_dkv(sv["qkv"], sv["qkn"], sv["gk2"], do, _rows_by_residue(sv["l_one"], dil),
                                      _rows_by_residue(dl_one, dil), g, dil)
        blocks += [dq, dk, dv]
        dgq.append(dgq_g)
        dgk.append(dgk_g)
    dqkv = jnp.concatenate(blocks, axis=1)

    def fold(parts):
        return jnp.stack(parts)[:, :, 0].reshape(-1, ATT_HEAD_DIM).sum(axis=0)[None, :]

    grads["att_q_norm"] = fold(dgq)
    grads["att_k_norm"] = fold(dgk)
    grads["att_w_qkv"] = _matmul_tn(dqkv, sv["h"], name="d_att_w_qkv")
    dx0, dgain = _hosted(plan, "att_qkv_dx", _matmul_rmsnorm_bwd, dqkv, prm["att_w_qkv"], None, sv["x0"],
                         prm["norm_mix"][1:2], dx1, name="att_qkv_dx")
    grads["norm_mix"][1] = dgain[0]
    return dx0


def _local_step(x, p, target, prm, plan=None):
    plan = plan or _NoOverlap()
    grads = {k: [None, None] for k in ("norm_mix", "norm_ffn", "ffn_w_gate", "ffn_w_up", "ffn_w_down",
                                       "ple_w_proj", "ple_w_gate")}
    plan.begin_backward(grads)
    x1, h1, sv_m = _mamba_fwd(x, prm, plan)
    x3, sv_f0 = _ffn_ple_fwd(x1, h1, p[0], prm, 0, plan)
    x4, h4, sv_a = _attn_mixer_fwd(x3, prm, plan)
    x6, sv_f1 = _ffn_ple_fwd(x4, h4, p[1], prm, 1, plan)
    dy, loss_row = _loss_head(x6, target)
    dx4 = _ffn_ple_bwd(dy, p[1], prm, 1, sv_f1, grads, plan)
    dx3 = _attn_mixer_bwd(dx4, prm, sv_a, grads, plan)
    dx1 = _ffn_ple_bwd(dx3, p[0], prm, 0, sv_f0, grads, plan)
    dx0 = _mamba_bwd(dx1, prm, sv_m, grads, plan)
    return loss_row, dx0, grads


W_IN_SLAB_ROWS = 1312


def _position():
    return lax.axis_index("x"), lax.axis_index("y"), lax.axis_index("c")


def _other_chips(x, y):
    return [(1 - x, y), (x, 1 - y), (1 - x, 1 - y)]


def _remote(send_sems, recv_sems, k, src, dst, to):
    return pltpu.make_async_remote_copy(src_ref=src, dst_ref=dst, send_sem=send_sems.at[k], recv_sem=recv_sems.at[k],
                                        device_id=to, device_id_type=MESH)


def _gather_side(entries, whole=()):
    n, nw = len(entries), len(whole)

    def first_hop(ins, outs, send_sems, recv_sems):
        x, y, c = _position()
        cps = []
        for j, chip in enumerate(_other_chips(x, y)):
            for e in range(n):
                cps.append(_remote(send_sems, recv_sems, 6 * e + j, ins[e].at[c], outs[e].at[2 * x + y, c], (*chip, c)))
            for e in range(nw):
                cps.append(_remote(send_sems, recv_sems, 6 * n + 3 * e + j, ins[n + e], outs[n + e].at[2 * x + y],
                                   (*chip, c)))
        return cps

    def start(ins, outs, send_sems, recv_sems):
        for cp in first_hop(ins, outs, send_sems, recv_sems):
            cp.start()

    def finish(ins, outs, send_sems, recv_sems):
        x, y, c = _position()
        me, sibling = (x, y, c), (x, y, 1 - c)
        chips = _other_chips(x, y)
        passed_on = []
        for j, (px, py) in enumerate(chips):
            for e in range(n):
                landed = outs[e].at[2 * px + py, c]
                _remote(send_sems, recv_sems, 6 * e + j, landed, landed, me).wait_recv()
                passed_on.append(_remote(send_sems, recv_sems, 6 * e + 3 + j, landed, landed, sibling))
                passed_on[-1].start()
            for e in range(nw):
                landed = outs[n + e].at[2 * px + py]
                _remote(send_sems, recv_sems, 6 * n + 3 * e + j, landed, landed, me).wait_recv()
        for j, (px, py) in enumerate(chips):
            for e in range(n):
                passed = outs[e].at[2 * px + py, 1 - c]
                _remote(send_sems, recv_sems, 6 * e + 3 + j, passed, passed, me).wait_recv()
        for cp in first_hop(ins, outs, send_sems, recv_sems) + passed_on:
            cp.wait_send()

    shapes = [jax.ShapeDtypeStruct((N_CHIPS,) + a.shape, a.dtype) for a in list(entries) + list(whole)]
    return _Side(list(entries) + list(whole), shapes, 6 * n + 3 * nw, start, finish)


def _run_side(side, name):
    si, so = len(side.inputs), len(side.out_shapes)

    def body(*refs):
        ins, outs, send_sems, recv_sems = refs[:si], refs[si:si + so], refs[-2], refs[-1]
        side.start(ins, outs, send_sems, recv_sems)
        side.finish(ins, outs, send_sems, recv_sems)

    side.outputs = list(pl.pallas_call(
        body, name=name, in_specs=[ANY] * si, out_specs=[ANY] * so, out_shape=side.out_shapes,
        scratch_shapes=[pltpu.SemaphoreType.DMA((side.n_sems,)), pltpu.SemaphoreType.DMA((side.n_sems,))],
    )(*side.inputs))
    return side.outputs


def _swap_side(grads):
    n = len(grads)

    def copies(ins, outs, send_sems, recv_sems):
        x, y, c = _position()
        return [_remote(send_sems, recv_sems, e, ins[e].at[:, 1 - c], outs[e], (x, y, 1 - c)) for e in range(n)]

    def start(ins, outs, send_sems, recv_sems):
        for cp in copies(ins, outs, send_sems, recv_sems):
            cp.start()

    def finish(ins, outs, send_sems, recv_sems):
        for cp in copies(ins, outs, send_sems, recv_sems):
            cp.wait()

    shapes = [jax.ShapeDtypeStruct((N_CHIPS,) + g.shape[2:], g.dtype) for g in grads]
    return _Side(grads, shapes, n, start, finish)


def _chip_exchange_side(chipsums):
    n = len(chipsums)

    def copies(ins, outs, send_sems, recv_sems):
        x, y, c = _position()
        return [_remote(send_sems, recv_sems, 3 * e + j, ins[e].at[2 * tx + ty], outs[e].at[j], (tx, ty, c))
                for j, (tx, ty) in enumerate(_other_chips(x, y)) for e in range(n)]

    def start(ins, outs, send_sems, recv_sems):
        for cp in copies(ins, outs, send_sems, recv_sems):
            cp.start()

    def finish(ins, outs, send_sems, recv_sems):
        for cp in copies(ins, outs, send_sems, recv_sems):
            cp.wait()

    shapes = [jax.ShapeDtypeStruct((3,) + cs.shape[1:], cs.dtype) for cs in chipsums]
    return _Side(chipsums, shapes, 3 * n, start, finish)


def _share_side(totals):
    n = len(totals)

    def copies(ins, outs, send_sems, recv_sems):
        x, y, c = _position()
        return [_remote(send_sems, recv_sems, e, ins[e], outs[e], (x, y, 1 - c)) for e in range(n)]

    def start(ins, outs, send_sems, recv_sems):
        for cp in copies(ins, outs, send_sems, recv_sems):
            cp.start()

    def finish(ins, outs, send_sems, recv_sems):
        for cp in copies(ins, outs, send_sems, recv_sems):
            cp.wait()

    return _Side(totals, [jax.ShapeDtypeStruct(t.shape, t.dtype) for t in totals], n, start, finish)


def _reduce_rows(h):
    return h if h <= 704 else h // 2


def _add_sibling(grad, recv, c_idx, *, name):
    _, _, h, cw = grad.shape
    th = _reduce_rows(h)

    def body(c_ref, g_ref, r_ref, o_ref):
        o_ref[...] = (g_ref[...] + r_ref[...]).astype(BF16)

    return pl.pallas_call(
        body, name=name,
        grid_spec=pltpu.PrefetchScalarGridSpec(
            num_scalar_prefetch=1, grid=(N_CHIPS, h // th),
            in_specs=[pl.BlockSpec((None, None, th, cw), lambda s, i, c_ref: (s, c_ref[0], i, 0)),
                      pl.BlockSpec((None, th, cw), lambda s, i, c_ref: (s, i, 0))],
            out_specs=pl.BlockSpec((None, th, cw), lambda s, i, c_ref: (s, i, 0))),
        out_shape=jax.ShapeDtypeStruct((N_CHIPS, h, cw), BF16),
        compiler_params=_params("parallel", "parallel"),
    )(c_idx, grad, recv)


def _add_chips(chipsum, recv, s_idx, *, name):
    _, h, cw = chipsum.shape
    th = _reduce_rows(h)

    def body(s_ref, own_ref, r_ref, o_ref):
        o_ref[...] = ((own_ref[...].astype(F32) + r_ref[0].astype(F32)) + r_ref[1].astype(F32)) + r_ref[2].astype(F32)

    return pl.pallas_call(
        body, name=name,
        grid_spec=pltpu.PrefetchScalarGridSpec(
            num_scalar_prefetch=1, grid=(h // th,),
            in_specs=[pl.BlockSpec((None, th, cw), lambda i, s_ref: (s_ref[0], i, 0)),
                      pl.BlockSpec((3, th, cw), lambda i, s_ref: (0, i, 0))],
            out_specs=pl.BlockSpec((th, cw), lambda i, s_ref: (i, 0))),
        out_shape=jax.ShapeDtypeStruct((h, cw), F32),
        compiler_params=_params("parallel"),
    )(s_idx, chipsum, recv)


def _adamw_math(w, g, m, v):
    m = ADAM_B1 * m + (1.0 - ADAM_B1) * g
    v = ADAM_B2 * v + (1.0 - ADAM_B2) * (g * g)
    m_hat = m / (1.0 - ADAM_B1 ** ADAM_STEP)
    v_hat = v / (1.0 - ADAM_B2 ** ADAM_STEP)
    delta = -ADAM_LR * (m_hat / (jnp.sqrt(v_hat) + ADAM_EPS) + ADAM_WD * w)
    return delta, m, v


ADAM_TILE_ELEMS = 256 * 1024


def _adamw(w, g, m, v, *, name):
    layers, rows, cols = w.shape
    tr = rows
    for cand in range(8, rows, 8):
        if rows % cand == 0 and cand * cols <= ADAM_TILE_ELEMS:
            tr = cand
    if rows * cols <= ADAM_TILE_ELEMS:
        tr = rows

    def body(w_ref, g_ref, m_ref, v_ref, d_ref, nm_ref, nv_ref):
        d, nm, nv = _adamw_math(w_ref[...], g_ref[...], m_ref[...], v_ref[...])
        d_ref[...] = d
        nm_ref[...] = nm
        nv_ref[...] = nv

    blk = pl.BlockSpec((None, tr, cols), lambda l, i: (l, i, 0))
    sds = jax.ShapeDtypeStruct(w.shape, F32)
    return pl.pallas_call(
        body, name=name, grid=(layers, rows // tr), in_specs=[blk] * 4, out_specs=[blk] * 3, out_shape=[sds] * 3,
        compiler_params=_params("parallel", "parallel"),
    )(w, g, m, v)


SMALL_LAYOUT = (("loss", 1), ("norm_mix", 16), ("norm_ffn", 16), ("ssm_conv_b", 24), ("ssm_dt_bias", 1),
                ("ssm_a_log", 1), ("ssm_d_skip", 1), ("ssm_norm_w", 16), ("att_q_norm", 1), ("att_k_norm", 1),
                ("conv_w_full", 96))
SMALL_ROWS = 176
N_DEVICES = 8


def _small_packs(dicts):
    parts = []
    for values in dicts:
        for name, rows in SMALL_LAYOUT:
            flat = values[name].reshape(-1).astype(F32)
            parts.append(jnp.pad(flat, (0, rows * LANES - flat.shape[0])).reshape(rows, LANES))
        used = sum(r for _, r in SMALL_LAYOUT)
        parts.append(jnp.zeros((SMALL_ROWS - used, LANES), F32))
    return jnp.concatenate(parts, axis=0).reshape(len(dicts), SMALL_ROWS, LANES)


def _small_unpack(pack, shapes):
    out, off = {}, 0
    for name, rows in SMALL_LAYOUT:
        shape = shapes[name]
        n = math.prod(shape)
        out[name] = pack[off:off + rows].reshape(-1)[:n].reshape(shape)
        off += rows
    return out


def _small_allreduce_adamw(g, w, m, v):
    def body(g_ref, w_ref, m_ref, v_ref, gs_ref, d_ref, nm_ref, nv_ref, buf, send_sems, recv_sems):
        x, y, c = _position()
        pos = (x, y, c)
        me = 4 * x + 2 * y + c
        buf[me] = g_ref[...]
        peers = []
        for k in range(1, N_DEVICES):
            bits = ((k >> 2) & 1, (k >> 1) & 1, k & 1)
            peers.append(tuple(1 - p if b else p for p, b in zip(pos, bits)))
        cps = [pltpu.make_async_remote_copy(src_ref=g_ref, dst_ref=buf.at[me], send_sem=send_sems.at[k],
                                            recv_sem=recv_sems.at[k], device_id=peer, device_id_type=MESH)
               for k, peer in enumerate(peers)]
        for cp in cps:
            cp.start()
        for k, (px, py, pc) in enumerate(peers):
            pltpu.make_async_remote_copy(src_ref=g_ref, dst_ref=buf.at[4 * px + 2 * py + pc],
                                         send_sem=send_sems.at[k], recv_sem=recv_sems.at[k],
                                         device_id=(px, py, pc), device_id_type=MESH).wait_recv()
        for cp in cps:
            cp.wait_send()
        total = buf[0]
        for dev in range(1, N_DEVICES):
            total = total + buf[dev]
        gs_ref[...] = total
        d, nm, nv = _adamw_math(w_ref[...], total, m_ref[...], v_ref[...])
        d_ref[...] = d
        nm_ref[...] = nm
        nv_ref[...] = nv

    vm = pl.BlockSpec(memory_space=pltpu.VMEM)
    sds = jax.ShapeDtypeStruct((SMALL_ROWS, LANES), F32)
    return pl.pallas_call(
        body, name="small_allreduce_adamw", in_specs=[vm] * 4, out_specs=[vm] * 4, out_shape=[sds] * 4,
        scratch_shapes=[pltpu.VMEM((N_DEVICES, SMALL_ROWS, LANES), F32),
                        pltpu.SemaphoreType.DMA((N_DEVICES - 1,)), pltpu.SemaphoreType.DMA((N_DEVICES - 1,))],
    )(g, w, m, v)


SMALL = tuple(n for n, _ in SMALL_LAYOUT if n not in ("loss", "conv_w_full"))
WEIGHTS = ("norm_mix", "norm_ffn", "ssm_w_in", "ssm_conv_w", "ssm_conv_b", "ssm_dt_bias", "ssm_a_log", "ssm_d_skip",
           "ssm_norm_w", "ssm_w_out", "att_w_qkv", "att_q_norm", "att_k_norm", "att_w_o", "ffn_w_gate", "ffn_w_up",
           "ffn_w_down", "ple_w_proj", "ple_w_gate")
COLUMN_SHARDED = ("ssm_w_in", "att_w_qkv", "ffn_w_gate", "ffn_w_up", "ple_w_proj")
LAYERED = ("ffn_w_gate", "ffn_w_up", "ffn_w_down", "ple_w_proj", "ple_w_gate")
UPDATED_TRANSPOSED = ("ssm_w_in", "ffn_w_gate", "ffn_w_up")
GATHER_ORDER = ("ssm_w_in", "ssm_w_out", "att_w_qkv", "att_w_o", "ffn_w_gate", "ffn_w_up", "ffn_w_down",
                "ple_w_proj", "ple_w_gate")


def _layers(n):
    return (0, 1) if n in LAYERED else (None,)


def _tag(key):
    return key[0] if key[1] is None else f"{key[0]}_{key[1]}"


QKV_PARTS = 3


def _weight_slab(w, key):
    n, i = key
    if n == "att_w_qkv":
        a = w[n][0].T
        rows = a.shape[0] // QKV_PARTS
        a = a[i * rows:(i + 1) * rows]
    else:
        a = w[n][0 if i is None else i]
        a = a.T if n in COLUMN_SHARDED else a
    if n == "ssm_w_in":
        a = jnp.pad(a, ((0, W_IN_SLAB_ROWS - a.shape[0]), (0, 0)))
    return a.reshape(2, a.shape[0] // 2, a.shape[1]).astype(BF16)


def _install(prm, key, gathered, own, s_me):
    n, i = key
    full = lax.dynamic_update_slice(gathered, own[None], (s_me, 0, 0, 0))
    full = full.reshape(N_CHIPS, 2 * full.shape[2], full.shape[3])
    if n == "att_w_qkv":
        parts = prm.setdefault("att_w_qkv_parts", {})
        parts[i] = full
        if len(parts) == QKV_PARTS:
            prm[n] = jnp.stack([parts[j] for j in range(QKV_PARTS)], axis=1).reshape(-1, D_MODEL)
        return
    if n == "ssm_w_in":
        rows = (D_INNER + CONV_DIM + SSM_HEADS) // N_CHIPS
        w_in_t = full[:, :rows].reshape(N_CHIPS * rows, D_MODEL)
        prm["ssm_w_z"] = w_in_t[:D_INNER]
        prm["ssm_w_xbc"] = w_in_t[D_INNER:D_INNER + CONV_DIM]
        prm["ssm_w_dt"] = jnp.pad(w_in_t[D_INNER + CONV_DIM:], ((0, LANES - SSM_HEADS), (0, 0)))
        return
    full = full.reshape(N_CHIPS * full.shape[1], full.shape[2])
    if i is None:
        prm[n] = full
    else:
        prm.setdefault(n, [None, None])[i] = full


def _grad_slab(grads, key):
    n, i = key
    g = grads[n] if i is None else grads[n][i]
    if n == "ssm_w_in":
        g = jnp.pad(g.reshape(N_CHIPS, g.shape[0] // N_CHIPS, D_MODEL),
                    ((0, 0), (0, W_IN_SLAB_ROWS - g.shape[0] // N_CHIPS), (0, 0)))
    rows = g.size // (N_CHIPS * g.shape[-1])
    return g.reshape(N_CHIPS, 2, rows // 2, g.shape[-1])


def _natural_shard(n, reduced, shape):
    def one(r):
        if n == "ssm_w_in":
            r = r[:shape[-1]]
        return r.T if n in COLUMN_SHARDED else r
    if n in LAYERED:
        return jnp.stack([one(r) for r in reduced]).reshape(shape)
    return one(reduced[0]).reshape(shape)


def kernel(x, p, norm_mix, norm_ffn, ssm_w_in, ssm_conv_w, ssm_conv_b, ssm_dt_bias, ssm_a_log, ssm_d_skip, ssm_norm_w, ssm_w_out, att_w_qkv, att_q_norm, att_k_norm, att_w_o, ffn_w_gate, ffn_w_up, ffn_w_down, ple_w_proj, ple_w_gate, loss_target, m_norm_mix, m_norm_ffn, m_ssm_w_in, m_ssm_conv_w, m_ssm_conv_b, m_ssm_dt_bias, m_ssm_a_log, m_ssm_d_skip, m_ssm_norm_w, m_ssm_w_out, m_att_w_qkv, m_att_q_norm, m_att_k_norm, m_att_w_o, m_ffn_w_gate, m_ffn_w_up, m_ffn_w_down, m_ple_w_proj, m_ple_w_gate, v_norm_mix, v_norm_ffn, v_ssm_w_in, v_ssm_conv_w, v_ssm_conv_b, v_ssm_dt_bias, v_ssm_a_log, v_ssm_d_skip, v_ssm_norm_w, v_ssm_w_out, v_att_w_qkv, v_att_q_norm, v_att_k_norm, v_att_w_o, v_ffn_w_gate, v_ffn_w_up, v_ffn_w_down, v_ple_w_proj, v_ple_w_gate):
    given = dict(locals())
    w = {n: given[n] for n in WEIGHTS}
    m = {n: given["m_" + n] for n in WEIGHTS}
    v = {n: given["v_" + n] for n in WEIGHTS}
    c_idx = lax.axis_index("c").astype(jnp.int32).reshape(1)
    s_idx = (2 * lax.axis_index("x") + lax.axis_index("y")).astype(jnp.int32).reshape(1)

    s_me = 2 * lax.axis_index("x") + lax.axis_index("y")
    first_core = lax.axis_index("c") == 0

    qkv_parts = [("att_w_qkv", j) for j in range(QKV_PARTS)]
    gather_plan = {
        "ssm_in_z": [("ssm_w_out", None)],
        "ssm_in_xbc": [("ffn_w_gate", 0)],
        "conv_fwd": [("ffn_w_up", 0)],
        "ssd_fwd": [("ffn_w_down", 0), ("ple_w_proj", 0), ("ple_w_gate", 0), ("att_w_o", None)],
        "swiglu_fwd_0": qkv_parts[:2],
        "ffn_down_0": qkv_parts[2:],
        "att_qkv": [(n, 1) for n in LAYERED],
    }
    mamba = [("ssm_w_in", None)]
    own = {k: _weight_slab(w, k) for k in mamba + sum(gather_plan.values(), [])}
    prm = {n: w[n] for n in SMALL}

    def land(group, outputs):
        for k, g in zip(group, outputs):
            _install(prm, k, g, own[k], s_me)

    first = _gather_side([own[k] for k in mamba], whole=[ssm_conv_w[0]])
    _run_side(first, "gather_mamba")
    land(mamba, first.outputs)
    conv = lax.dynamic_update_slice(first.outputs[-1], ssm_conv_w, (s_me, 0, 0))
    prm["ssm_conv_w"] = conv.transpose(1, 0, 2).reshape(CONV_WIDTH, CONV_DIM)

    ffn1 = [(n, 1) for n in LAYERED]
    attention = [("att_w_qkv", None), ("att_w_o", None)]
    ffn0 = [(n, 0) for n in LAYERED] + [("ssm_w_out", None)]
    reduce_plan = {"att_out_dx": [("swap", ffn1)], "att_qkv_dx": [("exchange", ffn1)],
                   "swiglu_bwd_0": [("swap", attention)], "gate_norm_bwd": [("swap", ffn0)],
                   "ssd_bwd": [("exchange", attention), ("exchange", ffn0)],
                   "ssm_dh_z": [("swap", mamba)], "ssm_dh_xbc": [("exchange", mamba)]}
    state = {}

    def swap_side(group):
        state[_tag(group[0]), "g4"] = g4 = [_grad_slab(state["grads"], k) for k in group]
        return _swap_side(g4)

    def add_siblings(group, from_sibling):
        state[_tag(group[0]), "chipsums"] = [
            _add_sibling(g, r, c_idx, name="add_sibling_" + _tag(k))
            for g, r, k in zip(state[_tag(group[0]), "g4"], from_sibling, group)]

    def exchange_side(group):
        return _chip_exchange_side(state[_tag(group[0]), "chipsums"])

    def add_chips(group, from_chips):
        for k, cs, r in zip(group, state[_tag(group[0]), "chipsums"], from_chips):
            state["total", k] = _add_chips(cs, r, s_idx, name="add_chips_" + _tag(k))

    class Plan(_NoOverlap):
        def __init__(self):
            self.carried = {host: _gather_side([own[k] for k in group]) for host, group in gather_plan.items()}

        def begin_backward(self, grads):
            state["grads"] = grads

        def side(self, host):
            if host in reduce_plan:
                self.parts = [swap_side(group) if step == "swap" else exchange_side(group)
                              for step, group in reduce_plan[host]]
                self.carried[host] = _sides_together(self.parts)
            elif host == share_host:
                self.carried[host] = _share_side([state["total", k] for k in order])
            return self.carried.get(host)

        def after(self, host):
            if host in gather_plan:
                land(gather_plan[host], self.carried[host].outputs)
            elif host in reduce_plan:
                _share_out(self.carried[host], self.parts)
                for (step, group), part in zip(reduce_plan[host], self.parts):
                    (add_siblings if step == "swap" else add_chips)(group, part.outputs)
            elif host == share_host:
                state["shared"] = self.carried[host].outputs

    order = mamba + ffn0 + attention + ffn1
    share_host = "ssm_dh_dt"
    loss_row, dx, grads = _local_step(x[0], p[:, 0], loss_target[0], prm, Plan())

    reduced = {}
    for k, theirs in zip(order, state["shared"]):
        lo = jnp.where(first_core, state["total", k], theirs)
        hi = jnp.where(first_core, theirs, state["total", k])
        reduced.setdefault(k[0], {})[k[1]] = jnp.concatenate([lo, hi], axis=0)
    reduced = {n: [by_layer[i] for i in _layers(n)] for n, by_layer in reduced.items()}

    grad, delta, new_m, new_v = {}, {}, {}, {}
    for n in GATHER_ORDER:
        if n in UPDATED_TRANSPOSED:
            flip = lambda a: a.transpose(0, 2, 1)
            cols = w[n].shape[-1]
            g_t = jnp.stack([r[:cols] for r in reduced[n]])
            grad[n] = flip(g_t)
            delta[n], new_m[n], new_v[n] = [flip(o) for o in _adamw(flip(w[n]), g_t, flip(m[n]), flip(v[n]),
                                                                    name="adamw_" + n)]
            continue
        grad[n] = _natural_shard(n, reduced[n], w[n].shape)
        delta[n], new_m[n], new_v[n] = _adamw(w[n], grad[n], m[n], v[n], name="adamw_" + n)

    small_g = {n: (jnp.stack(grads[n]) if isinstance(grads[n], list) else grads[n]) for n in SMALL}
    small_g["loss"] = loss_row
    small_g["conv_w_full"] = grads["ssm_conv_w"]
    zero = {"loss": jnp.zeros((1, LANES), F32), "conv_w_full": jnp.zeros((CONV_WIDTH, CONV_DIM), F32)}
    packs = _small_packs([small_g, {**w, **zero}, {**m, **zero}, {**v, **zero}])
    outs = _small_allreduce_adamw(packs[0], packs[1], packs[2], packs[3])
    shapes = {n: w[n].shape for n in SMALL}
    shapes["loss"] = (1, LANES)
    shapes["conv_w_full"] = (CONV_WIDTH, CONV_DIM)
    sg, sd, sm, sv = [_small_unpack(o, shapes) for o in outs]
    for n in SMALL:
        grad[n], delta[n], new_m[n], new_v[n] = sg[n], sd[n], sm[n], sv[n]
    loss = sg["loss"][0, 0]
    conv_cols = CONV_DIM // N_CHIPS
    grad["ssm_conv_w"] = lax.dynamic_slice(sg["conv_w_full"], (0, s_me * conv_cols), (CONV_WIDTH, conv_cols))[None]
    delta["ssm_conv_w"], new_m["ssm_conv_w"], new_v["ssm_conv_w"] = _adamw(
        ssm_conv_w, grad["ssm_conv_w"], m_ssm_conv_w, v_ssm_conv_w, name="adamw_ssm_conv_w")

    return (loss, dx[None], *[grad[n] for n in WEIGHTS], *[delta[n] for n in WEIGHTS],
            *[new_m[n] for n in WEIGHTS], *[new_v[n] for n in WEIGHTS])
```

```python
import math

import jax
import jax.numpy as jnp
from jax import lax
from jax.experimental import pallas as pl
from jax.experimental.pallas import tpu as pltpu

F32 = jnp.float32
BF16 = jnp.bfloat16
HIGHEST = lax.Precision.HIGHEST

NORM_EPS = 1e-6
ADAM_LR, ADAM_B1, ADAM_B2, ADAM_EPS, ADAM_WD, ADAM_STEP = 0.001, 0.9, 0.999, 1e-08, 0.01, 10

D_MODEL = 1024
D_INNER = 2048
SSM_HEADS = 32
SSM_HEAD_DIM = 64
SSM_GROUPS = 4
SSM_STATE = 128
SSD_CHUNK = 128
CONV_DIM = 3072
CONV_WIDTH = 4
ATT_HEADS = 16
ATT_HEAD_DIM = 64
DIL_PATTERNS = ((128, 1), (512, 4), (2048, 16))
ATT_BLOCK = 128
FFN_HIDDEN = 2816
PLE_DIM = 256

LANES = 128
V7X_VMEM_LIMIT = 56 * 1024 * 1024
NEG_BIG = -1e30

N_CHIPS = 4


def _params(*sem):
    return pltpu.CompilerParams(dimension_semantics=sem, vmem_limit_bytes=V7X_VMEM_LIMIT)


def _tile(n, pref):
    if n <= pref:
        return n
    best = None
    for t in range(LANES, pref + 1, LANES):
        if n % t == 0:
            best = t
    assert best is not None, (n, pref)
    return best


def _sigmoid(v):
    return 1.0 / (1.0 + jnp.exp(-v))


def _dot(a, b):
    return jnp.dot(a, b, preferred_element_type=F32)


def _dot_nt(a, b):
    return lax.dot_general(a, b, (((1,), (1,)), ((), ())), preferred_element_type=F32)


def _dot_tn(a, b):
    return lax.dot_general(a, b, (((0,), (0,)), ((), ())), preferred_element_type=F32)


def _head_block_diag():
    i = lax.broadcasted_iota(jnp.int32, (LANES, LANES), 0) // ATT_HEAD_DIM
    j = lax.broadcasted_iota(jnp.int32, (LANES, LANES), 1) // ATT_HEAD_DIM
    return (i == j).astype(BF16)


def _split_dot(ones, z):
    hi = z.astype(BF16)
    lo = (z - hi.astype(F32)).astype(BF16)
    return _dot(ones, hi) + _dot(ones, lo)


def _head_sums(z, bd, terms=2):
    hi = z.astype(BF16)
    lo = (z - hi.astype(F32)).astype(BF16) if terms == 2 else None
    parts = []
    for t in range(z.shape[1] // LANES):
        sl = slice(t * LANES, (t + 1) * LANES)
        part = _dot(hi[:, sl], bd)
        parts.append(part + _dot(lo[:, sl], bd) if terms == 2 else part)
    return parts[0] if len(parts) == 1 else jnp.concatenate(parts, axis=1)


def _lane_lt64(rows):
    return lax.broadcasted_iota(jnp.int32, (rows, LANES), 1) < ATT_HEAD_DIM


MESH = pl.DeviceIdType.MESH
ANY = pl.BlockSpec(memory_space=pl.ANY)


class _Side:
    def __init__(self, inputs, out_shapes, n_sems, start, finish):
        self.inputs, self.out_shapes, self.n_sems = list(inputs), list(out_shapes), n_sems
        self.start, self.finish = start, finish
        self.outputs = None


class _SemaphoresFrom:
    def __init__(self, sems, first):
        self.sems, self.first = sems, first

    @property
    def at(self):
        return self

    def __getitem__(self, k):
        return self.sems.at[self.first + k]


def _sides_together(sides):
    def run(step):
        def both(ins, outs, send_sems, recv_sems):
            i = o = k = 0
            for s in sides:
                ni, no = len(s.inputs), len(s.out_shapes)
                getattr(s, step)(ins[i:i + ni], outs[o:o + no], _SemaphoresFrom(send_sems, k),
                                 _SemaphoresFrom(recv_sems, k))
                i, o, k = i + ni, o + no, k + s.n_sems
        return both

    return _Side(sum([s.inputs for s in sides], []), sum([s.out_shapes for s in sides], []),
                 sum(s.n_sems for s in sides), run("start"), run("finish"))


def _share_out(together, sides):
    o = 0
    for s in sides:
        s.outputs = together.outputs[o:o + len(s.out_shapes)]
        o += len(s.out_shapes)


def _call(body, side, *, name, grid, in_specs, out_specs, out_shape, scratch_shapes, semantics, args):
    in_specs, out_specs, out_shape = list(in_specs), list(out_specs), list(out_shape)
    scratch_shapes = list(scratch_shapes)
    if side is None:
        return pl.pallas_call(body, name=name, grid=grid, in_specs=in_specs, out_specs=out_specs,
                              out_shape=out_shape, scratch_shapes=scratch_shapes,
                              compiler_params=_params(*semantics))(*args)
    ni, no, ns = len(in_specs), len(out_specs), len(scratch_shapes)
    si, so = len(side.inputs), len(side.out_shapes)

    def hosted(*refs):
        ins, s_ins = refs[:ni], refs[ni:ni + si]
        outs, s_outs = refs[ni + si:ni + si + no], refs[ni + si + no:ni + si + no + so]
        scratch = refs[ni + si + no + so:ni + si + no + so + ns]
        send_sems, recv_sems = refs[-2], refs[-1]
        first = pl.program_id(0) == 0
        last = pl.program_id(0) == grid[0] - 1
        for axis in range(1, len(grid)):
            first = jnp.logical_and(first, pl.program_id(axis) == 0)
            last = jnp.logical_and(last, pl.program_id(axis) == grid[axis] - 1)

        @pl.when(first)
        def _():
            side.start(s_ins, s_outs, send_sems, recv_sems)

        body(*ins, *outs, *scratch)

        @pl.when(last)
        def _():
            side.finish(s_ins, s_outs, send_sems, recv_sems)

    res = pl.pallas_call(
        hosted, name=name, grid=grid, in_specs=in_specs + [ANY] * si, out_specs=out_specs + [ANY] * so,
        out_shape=out_shape + side.out_shapes,
        scratch_shapes=scratch_shapes + [pltpu.SemaphoreType.DMA((side.n_sems,)),
                                         pltpu.SemaphoreType.DMA((side.n_sems,))],
        compiler_params=_params(*["arbitrary"] * len(grid)),
    )(*args, *side.inputs)
    side.outputs = list(res[no:])
    return list(res[:no])


def _matmul(a, b, *, mode, name, out_dtype=F32, addend=None, tm=1024, tn=512, tk_max=3072, side=None, second=None):
    m, k = a.shape
    if mode == "nn":
        k2, n = b.shape
    else:
        n, k2 = b.shape
    assert k == k2, (a.shape, b.shape, mode)
    tm, tn, tk = _tile(m, tm), _tile(n, tn), _tile(k, tk_max)
    nk = k // tk
    has_add = addend is not None
    n_rows = len(second[1]) if second else 0
    n_out = 2 if second else 1

    def body(*refs):
        a_ref, b_ref = refs[0], refs[1]
        add_ref = refs[2] if has_add else None
        row_refs = refs[2 + has_add:2 + has_add + n_rows]
        o_ref, acc_ref = refs[-1 - n_out], refs[-1]
        kk = pl.program_id(2)
        col_tile = pl.program_id(1)
        av = a_ref[...].astype(BF16)
        bv = b_ref[...].astype(BF16)
        part = _dot(av, bv) if mode == "nn" else _dot_nt(av, bv)

        @pl.when(kk == 0)
        def _():
            acc_ref[...] = part

        @pl.when(kk > 0)
        def _():
            acc_ref[...] += part

        @pl.when(kk == nk - 1)
        def _():
            res = acc_ref[...]
            if has_add:
                res = res + add_ref[...]
            o_ref[...] = res.astype(out_dtype)
            if second:
                refs[-2][...] = second[0](res, col_tile, *row_refs).astype(second[2])

    a_spec = pl.BlockSpec((tm, tk), lambda i, j, kk: (i, kk))
    if mode == "nn":
        b_spec = pl.BlockSpec((tk, tn), lambda i, j, kk: (kk, j))
    else:
        b_spec = pl.BlockSpec((tn, tk), lambda i, j, kk: (j, kk))
    tile = pl.BlockSpec((tm, tn), lambda i, j, kk: (i, j))
    in_specs = [a_spec, b_spec]
    args = [a, b]
    if has_add:
        in_specs.append(tile)
        args.append(addend)
    if second:
        in_specs += [pl.BlockSpec((1, tn), lambda i, j, kk: (0, j))] * n_rows
        args += list(second[1])
    outs = _call(
        body, side, name=name, grid=(m // tm, n // tn, nk),
        in_specs=in_specs, out_specs=[tile] * n_out,
        out_shape=[jax.ShapeDtypeStruct((m, n), out_dtype)] + ([jax.ShapeDtypeStruct((m, n), second[2])] if second
                                                                 else []),
        scratch_shapes=[pltpu.VMEM((tm, tn), F32)],
        semantics=("parallel", "parallel", "arbitrary"), args=args,
    )
    return outs if second else outs[0]


def _matmul_tn(a, b, *, name, tm=1408, tn=512, tk=1024):
    t, m = a.shape
    t2, n = b.shape
    assert t == t2
    tm, tn, tk = _tile(m, tm), _tile(n, tn), _tile(t, tk)

    def body(a_ref, b_ref, o_ref):
        part = _dot_tn(a_ref[...].astype(BF16), b_ref[...].astype(BF16))

        @pl.when(pl.program_id(2) == 0)
        def _():
            o_ref[...] = part

        @pl.when(pl.program_id(2) > 0)
        def _():
            o_ref[...] += part

    return pl.pallas_call(
        body, name=name, grid=(m // tm, n // tn, t // tk),
        in_specs=[pl.BlockSpec((tk, tm), lambda i, j, kk: (kk, i)),
                  pl.BlockSpec((tk, tn), lambda i, j, kk: (kk, j))],
        out_specs=pl.BlockSpec((tm, tn), lambda i, j, kk: (i, j)),
        out_shape=jax.ShapeDtypeStruct((m, n), F32),
        compiler_params=_params("parallel", "parallel", "arbitrary"),
    )(a, b)


def _rmsnorm_rows(tile, j, gain_ref):
    r = lax.rsqrt(jnp.mean(tile * tile, axis=-1, keepdims=True) + NORM_EPS)
    return tile * r * gain_ref[...]


def _rmsnorm_fwd(x, gain, *, name):
    t, d = x.shape
    tm = _tile(t, 512)

    def body(x_ref, g_ref, o_ref):
        xv = x_ref[...]
        r = lax.rsqrt(jnp.mean(xv * xv, axis=-1, keepdims=True) + NORM_EPS)
        o_ref[...] = (xv * r * g_ref[...]).astype(BF16)

    return pl.pallas_call(
        body, name=name, grid=(t // tm,),
        in_specs=[pl.BlockSpec((tm, d), lambda i: (i, 0)), pl.BlockSpec((1, d), lambda i: (0, 0))],
        out_specs=pl.BlockSpec((tm, d), lambda i: (i, 0)),
        out_shape=jax.ShapeDtypeStruct((t, d), BF16),
        compiler_params=_params("parallel"),
    )(x, gain)


def _matmul_rmsnorm_bwd(a, b, addend, x, gain, dres, *, name, side=None, tm=512, tk_max=3072):
    m, k = a.shape
    d = b.shape[1]
    tm, tk = _tile(m, tm), _tile(k, tk_max)
    nk = k // tk

    def body(a_ref, b_ref, *rest):
        add_ref = rest[0] if addend is not None else None
        x_ref, g_ref, dres_ref, dx_ref, dg_ref, acc_ref = rest[-6:]
        i, kk = pl.program_id(0), pl.program_id(1)
        part = _dot(a_ref[...].astype(BF16), b_ref[...].astype(BF16))

        @pl.when(kk == 0)
        def _():
            acc_ref[...] = part

        @pl.when(kk > 0)
        def _():
            acc_ref[...] += part

        @pl.when(kk == nk - 1)
        def _():
            dyv = acc_ref[...] if addend is None else acc_ref[...] + add_ref[...]
            xv = x_ref[...]
            r = lax.rsqrt(jnp.mean(xv * xv, axis=-1, keepdims=True) + NORM_EPS)
            xh = xv * r
            dxh = dyv * g_ref[...]
            mean = jnp.mean(dxh * xh, axis=-1, keepdims=True)
            dx_ref[...] = dres_ref[...] + r * (dxh - xh * mean)
            gain_part = jnp.sum(dyv * xh, axis=0, keepdims=True)

            @pl.when(i == 0)
            def _():
                dg_ref[...] = gain_part

            @pl.when(i > 0)
            def _():
                dg_ref[...] += gain_part

    row = pl.BlockSpec((tm, d), lambda i, kk: (i, 0))
    vec = pl.BlockSpec((1, d), lambda i, kk: (0, 0))
    return _call(
        body, side, name=name, grid=(m // tm, nk),
        in_specs=[pl.BlockSpec((tm, tk), lambda i, kk: (i, kk)), pl.BlockSpec((tk, d), lambda i, kk: (kk, 0))]
        + ([row] if addend is not None else []) + [row, vec, row],
        out_specs=[row, vec],
        out_shape=[jax.ShapeDtypeStruct((m, d), F32), jax.ShapeDtypeStruct((1, d), F32)],
        scratch_shapes=[pltpu.VMEM((tm, d), F32)],
        semantics=("arbitrary", "arbitrary"),
        args=(a, b) + ((addend,) if addend is not None else ()) + (x, gain, dres),
    )


def _swiglu_fwd(h, w_gate_t, w_up_t, *, name, side=None):
    t, d = h.shape
    f = w_gate_t.shape[0]
    tm, tn = _tile(t, 1024), _tile(f, 256)

    def body(h_ref, wg_ref, wu_ref, g_ref, u_ref, a_ref):
        hv = h_ref[...]
        g = _dot_nt(hv, wg_ref[...])
        u = _dot_nt(hv, wu_ref[...])
        g_ref[...] = g.astype(BF16)
        u_ref[...] = u.astype(BF16)
        a_ref[...] = (g * _sigmoid(g) * u).astype(BF16)

    wspec = pl.BlockSpec((tn, d), lambda i, j: (j, 0))
    ospec = pl.BlockSpec((tm, tn), lambda i, j: (i, j))
    return _call(
        body, side, name=name, grid=(t // tm, f // tn),
        in_specs=[pl.BlockSpec((tm, d), lambda i, j: (i, 0)), wspec, wspec],
        out_specs=[ospec, ospec, ospec],
        out_shape=[jax.ShapeDtypeStruct((t, f), BF16), jax.ShapeDtypeStruct((t, f), BF16),
                   jax.ShapeDtypeStruct((t, f), BF16)],
        scratch_shapes=[], semantics=("parallel", "parallel"), args=(h, w_gate_t, w_up_t),
    )


def _swiglu_bwd(dx, w_down, g, u, *, name, side=None):
    t, d = dx.shape
    f = w_down.shape[0]
    tm, tn = _tile(t, 1024), _tile(f, 256)

    def body(dx_ref, wd_ref, g_ref, u_ref, dg_ref, du_ref):
        dact = _dot_nt(dx_ref[...].astype(BF16), wd_ref[...])
        gv, uv = g_ref[...].astype(F32), u_ref[...].astype(F32)
        sg = _sigmoid(gv)
        dg_ref[...] = (dact * uv * sg * (1.0 + gv * (1.0 - sg))).astype(BF16)
        du_ref[...] = (dact * gv * sg).astype(BF16)

    ospec = pl.BlockSpec((tm, tn), lambda i, j: (i, j))
    return _call(
        body, side, name=name, grid=(t // tm, f // tn),
        in_specs=[pl.BlockSpec((tm, d), lambda i, j: (i, 0)), pl.BlockSpec((tn, d), lambda i, j: (j, 0)),
                  ospec, ospec],
        out_specs=[ospec, ospec],
        out_shape=[jax.ShapeDtypeStruct((t, f), BF16), jax.ShapeDtypeStruct((t, f), BF16)],
        scratch_shapes=[], semantics=("parallel", "parallel"), args=(dx, w_down, g, u),
    )


def _ple_fwd(x, p, w_gate, w_proj_t, *, name, next_gain=None, target=None):
    t, d = x.shape
    e = p.shape[1]
    tm = _tile(t, 512)
    steps = t // tm

    def body(x_ref, p_ref, wg_ref, wp_ref, *rest):
        xv = x_ref[...]
        s = _dot(xv.astype(BF16), wg_ref[...])
        ple = _dot_nt(p_ref[...].astype(BF16), wp_ref[...])
        y = xv + _sigmoid(s) * ple
        if target is None:
            gain_ref, y_ref, h_ref = rest
            y_ref[...] = y
            r = lax.rsqrt(jnp.mean(y * y, axis=-1, keepdims=True) + NORM_EPS)
            h_ref[...] = (y * r * gain_ref[...]).astype(BF16)
        else:
            t_ref, dy_ref, l_ref, acc_ref = rest
            err = y - t_ref[...]
            dy_ref[...] = err * (1.0 / d)
            part = jnp.sum(err * err, axis=0, keepdims=True)

            @pl.when(pl.program_id(0) == 0)
            def _():
                acc_ref[...] = part

            @pl.when(pl.program_id(0) > 0)
            def _():
                acc_ref[...] += part

            @pl.when(pl.program_id(0) == steps - 1)
            def _():
                l_ref[...] = jnp.full((1, LANES), (0.5 / d), F32) * jnp.sum(acc_ref[...])

    row = pl.BlockSpec((tm, d), lambda i: (i, 0))
    fixed = lambda shape: pl.BlockSpec(shape, lambda i: (0, 0))
    in_specs = [row, pl.BlockSpec((tm, e), lambda i: (i, 0)), fixed((d, d)), fixed((d, e))]
    if target is None:
        return pl.pallas_call(
            body, name=name, grid=(steps,), in_specs=in_specs + [fixed((1, d))], out_specs=[row, row],
            out_shape=[jax.ShapeDtypeStruct((t, d), F32), jax.ShapeDtypeStruct((t, d), BF16)],
            compiler_params=_params("parallel"),
        )(x, p, w_gate, w_proj_t, next_gain)
    return pl.pallas_call(
        body, name=name, grid=(steps,), in_specs=in_specs + [row], out_specs=[row, fixed((1, LANES))],
        out_shape=[jax.ShapeDtypeStruct((t, d), F32), jax.ShapeDtypeStruct((1, LANES), F32)],
        scratch_shapes=[pltpu.VMEM((1, d), F32)],
        compiler_params=_params("arbitrary"),
    )(x, p, w_gate, w_proj_t, target)


def _ple_bwd(x, p, w_gate, w_proj_t, dout, *, name):
    t, d = x.shape
    e = p.shape[1]
    tm, tn = _tile(t, 1024), _tile(d, 512)

    def body(xf_ref, p_ref, wg_ref, wp_ref, do_ref, ds_ref, dple_ref):
        s = _dot(xf_ref[...].astype(BF16), wg_ref[...])
        ple = _dot_nt(p_ref[...].astype(BF16), wp_ref[...])
        gate = _sigmoid(s)
        dov = do_ref[...]
        dple_ref[...] = (dov * gate).astype(BF16)
        ds_ref[...] = (dov * ple * gate * (1.0 - gate)).astype(BF16)

    ospec = pl.BlockSpec((tm, tn), lambda i, j: (i, j))
    return pl.pallas_call(
        body, name=name, grid=(t // tm, d // tn),
        in_specs=[pl.BlockSpec((tm, d), lambda i, j: (i, 0)), pl.BlockSpec((tm, e), lambda i, j: (i, 0)),
                  pl.BlockSpec((d, tn), lambda i, j: (0, j)), pl.BlockSpec((tn, e), lambda i, j: (j, 0)), ospec],
        out_specs=[ospec, ospec],
        out_shape=[jax.ShapeDtypeStruct((t, d), BF16), jax.ShapeDtypeStruct((t, d), BF16)],
        compiler_params=_params("parallel", "parallel"),
    )(x, p, w_gate, w_proj_t, dout)


CONV_TIME_TILE = 256
CONV_HALO = 8


def _conv_taps(ext, w):
    acc = ext[CONV_HALO:, :] * w[CONV_WIDTH - 1:CONV_WIDTH, :]
    shifted = [ext[CONV_HALO:, :]]
    for j in range(1, CONV_WIDTH):
        sh = pltpu.roll(ext, j, 0)[CONV_HALO:, :]
        shifted.append(sh)
        acc = acc + sh * w[CONV_WIDTH - 1 - j:CONV_WIDTH - j, :]
    return acc, shifted


def _conv_fwd(u, w, b, side=None):
    t, c = u.shape
    tc = _tile(c, 256)
    tt = CONV_TIME_TILE

    def body(u_ref, w_ref, b_ref, o_ref):
        wv, bv = w_ref[...], b_ref[...]

        def tile(start, ext):
            pre = _conv_taps(ext, wv)[0] + bv
            o_ref[pl.ds(start, tt), :] = pre * _sigmoid(pre)

        tile(0, jnp.concatenate([jnp.zeros((CONV_HALO, tc), F32), u_ref[0:tt, :]], axis=0))

        def loop(i, carry):
            start = pl.multiple_of(i * tt, tt)
            tile(start, u_ref[pl.ds(start - CONV_HALO, tt + CONV_HALO), :])
            return carry

        lax.fori_loop(1, t // tt, loop, 0)

    col = pl.BlockSpec((t, tc), lambda j: (0, j))
    return _call(
        body, side, name="conv_fwd", grid=(c // tc,),
        in_specs=[col, pl.BlockSpec((CONV_WIDTH, tc), lambda j: (0, j)), pl.BlockSpec((1, tc), lambda j: (0, j))],
        out_specs=[col], out_shape=[jax.ShapeDtypeStruct((t, c), F32)],
        scratch_shapes=[], semantics=("parallel",), args=(u, w, b),
    )[0]


def _conv_bwd(u, w, b, dact, side=None):
    t, c = u.shape
    tc = _tile(c, 256)
    tt = CONV_TIME_TILE

    def body(u_ref, w_ref, b_ref, da_ref, du_ref, dw_ref, db_ref, dpre_ref):
        wv, bv = w_ref[...], b_ref[...]

        def tile(start, ext, sums):
            acc, shifted = _conv_taps(ext, wv)
            pre = acc + bv
            sg = _sigmoid(pre)
            dpre = da_ref[pl.ds(start, tt), :] * (sg * (1.0 + pre * (1.0 - sg)))
            dpre_ref[pl.ds(start, tt), :] = dpre
            new = [sums[0] + jnp.sum(dpre, axis=0, keepdims=True)]
            for j in range(CONV_WIDTH):
                new.append(sums[1 + j] + jnp.sum(dpre * shifted[j], axis=0, keepdims=True))
            return tuple(new)

        zero = jnp.zeros((1, tc), F32)
        sums = tile(0, jnp.concatenate([jnp.zeros((CONV_HALO, tc), F32), u_ref[0:tt, :]], axis=0),
                    (zero,) * (1 + CONV_WIDTH))

        def loop(i, sums):
            start = pl.multiple_of(i * tt, tt)
            return tile(start, u_ref[pl.ds(start - CONV_HALO, tt + CONV_HALO), :], sums)

        sums = lax.fori_loop(1, t // tt, loop, sums)
        db_ref[...] = sums[0]
        dw_ref[...] = jnp.concatenate([sums[1 + (CONV_WIDTH - 1 - k)] for k in range(CONV_WIDTH)], axis=0)
        dpre_ref[pl.ds(t, CONV_HALO), :] = jnp.zeros((CONV_HALO, tc), F32)

        def loop2(i, carry):
            start = pl.multiple_of(i * tt, tt)
            ext = dpre_ref[pl.ds(start, tt + CONV_HALO), :]
            acc = ext[0:tt, :] * wv[CONV_WIDTH - 1:CONV_WIDTH, :]
            for j in range(1, CONV_WIDTH):
                acc = acc + pltpu.roll(ext, tt + CONV_HALO - j, 0)[0:tt, :] * wv[CONV_WIDTH - 1 - j:CONV_WIDTH - j, :]
            du_ref[pl.ds(start, tt), :] = acc.astype(BF16)
            return carry

        lax.fori_loop(0, t // tt, loop2, 0)

    col = pl.BlockSpec((t, tc), lambda j: (0, j))
    return _call(
        body, side, name="conv_bwd", grid=(c // tc,),
        in_specs=[col, pl.BlockSpec((CONV_WIDTH, tc), lambda j: (0, j)), pl.BlockSpec((1, tc), lambda j: (0, j)), col],
        out_specs=[col, pl.BlockSpec((CONV_WIDTH, tc), lambda j: (0, j)), pl.BlockSpec((1, tc), lambda j: (0, j))],
        out_shape=[jax.ShapeDtypeStruct((t, c), BF16), jax.ShapeDtypeStruct((CONV_WIDTH, c), F32),
                   jax.ShapeDtypeStruct((1, c), F32)],
        scratch_shapes=[pltpu.VMEM((t + CONV_HALO, tc), F32)],
        semantics=("parallel",), args=(u, w, b, dact),
    )


def _softplus(v):
    e = jnp.exp(-jnp.abs(v))
    w = 1.0 + e
    log1p = jnp.where(w == 1.0, e, jnp.log(w) * (e / jnp.where(w == 1.0, 1.0, w - 1.0)))
    return jnp.maximum(v, 0.0) + log1p


def _split3(z):
    hi = z.astype(BF16)
    rest = z - hi.astype(F32)
    mid = rest.astype(BF16)
    return hi, mid, (rest - mid.astype(F32)).astype(BF16)


def _select_dot(z, ones):
    return sum(_dot(term, ones) for term in _split3(z))


def _ssd_prep_fwd(dt_raw, dt_bias, a_log):
    t = dt_raw.shape[0]
    cl = SSD_CHUNK

    def body(r_ref, b_ref, al_ref, acs_ref, dt_rep_ref, acs_rep_ref):
        dt = _softplus(r_ref[...] + b_ref[...])
        adt = dt * (-jnp.exp(al_ref[...]))
        li = lax.broadcasted_iota(jnp.int32, (cl, cl), 0)
        si = lax.broadcasted_iota(jnp.int32, (cl, cl), 1)
        tri = (si <= li).astype(F32)
        acs = jnp.dot(tri, adt, preferred_element_type=F32, precision=HIGHEST)
        acs_ref[...] = acs
        head = lax.broadcasted_iota(jnp.int32, (LANES, D_INNER), 0)
        chan = lax.broadcasted_iota(jnp.int32, (LANES, D_INNER), 1) // SSM_HEAD_DIM
        spread = (head == chan).astype(BF16)
        dt_rep_ref[...] = _select_dot(dt, spread)
        acs_rep_ref[...] = _select_dot(acs, spread)

    row = pl.BlockSpec((cl, LANES), lambda i: (i, 0))
    wide = pl.BlockSpec((cl, D_INNER), lambda i: (i, 0))
    vec = pl.BlockSpec((1, LANES), lambda i: (0, 0))
    return pl.pallas_call(
        body, name="ssd_prep_fwd", grid=(t // cl,),
        in_specs=[row, vec, vec], out_specs=[row, wide, wide],
        out_shape=[jax.ShapeDtypeStruct((t, LANES), F32), jax.ShapeDtypeStruct((t, D_INNER), F32),
                   jax.ShapeDtypeStruct((t, D_INNER), F32)],
        compiler_params=_params("parallel"),
    )(dt_raw, dt_bias, a_log)


def _ssd_prep_bwd(dt_raw, dt_bias, ddt):
    t = dt_raw.shape[0]
    tm = _tile(t, 512)

    def body(r_ref, b_ref, d_ref, o_ref, db_ref):
        g = d_ref[...] * _sigmoid(r_ref[...] + b_ref[...])
        o_ref[...] = g.astype(BF16)
        part = jnp.sum(g, axis=0, keepdims=True)

        @pl.when(pl.program_id(0) == 0)
        def _():
            db_ref[...] = part

        @pl.when(pl.program_id(0) > 0)
        def _():
            db_ref[...] += part

    row = pl.BlockSpec((tm, LANES), lambda i: (i, 0))
    vec = pl.BlockSpec((1, LANES), lambda i: (0, 0))
    return pl.pallas_call(
        body, name="ssd_prep_bwd", grid=(t // tm,),
        in_specs=[row, vec, row], out_specs=[row, vec],
        out_shape=[jax.ShapeDtypeStruct((t, LANES), BF16), jax.ShapeDtypeStruct((1, LANES), F32)],
        compiler_params=_params("arbitrary"),
    )(dt_raw, dt_bias, ddt)


GROUP_W = D_INNER // SSM_GROUPS
PAIRS_PER_GROUP = GROUP_W // LANES


def _head_cols(acs_pair, lt64):
    rolled = pltpu.roll(acs_pair, ATT_HEAD_DIM, 1)
    return jnp.where(lt64, acs_pair, rolled), jnp.where(lt64, rolled, acs_pair)


def _ssd_fwd(xbc, dt_rep, acs_rep, acs_t, dskip_rep, side=None):
    t = xbc.shape[0]
    cl = SSD_CHUNK
    nc = t // cl

    def body(xbc_ref, dt_ref, acs_ref, acst_ref, dskip_ref, y_ref, hin_ref, state_ref):
        @pl.when(pl.program_id(0) == 0)
        def _():
            state_ref[...] = jnp.zeros_like(state_ref)

        lt64 = _lane_lt64(cl)
        li = lax.broadcasted_iota(jnp.int32, (cl, cl), 0)
        si = lax.broadcasted_iota(jnp.int32, (cl, cl), 1)
        causal = li >= si
        hin_ref[...] = state_ref[...]
        for g in range(SSM_GROUPS):
            gsl = slice(g * GROUP_W, (g + 1) * GROUP_W)
            xg = xbc_ref[:, gsl]
            bg = xbc_ref[:, D_INNER + g * SSM_STATE:D_INNER + (g + 1) * SSM_STATE]
            cg = xbc_ref[:, D_INNER + SSM_GROUPS * SSM_STATE + g * SSM_STATE:
                         D_INNER + SSM_GROUPS * SSM_STATE + (g + 1) * SSM_STATE]
            acs = acs_ref[:, gsl]
            xdt = xg * dt_ref[:, gsl]
            atot = acs[cl - 1:cl, :]
            hin = state_ref[:, gsl]
            cgb = cg.astype(BF16)
            gmat = _dot_nt(cgb, bg.astype(BF16))
            yoff = _dot(cgb, hin.astype(BF16)) * jnp.exp(acs)
            snew = _dot(bg.T.astype(BF16), (xdt * jnp.exp(atot - acs)).astype(BF16))
            state_ref[:, gsl] = hin * jnp.exp(atot) + snew
            xdtb = xdt.astype(BF16)
            for pr in range(PAIRS_PER_GROUP):
                psl = slice(pr * LANES, (pr + 1) * LANES)
                cols = _head_cols(acs[:, psl], lt64)
                xp = xdtb[:, psl]
                ys = []
                for hh in range(2):
                    h = (g * PAIRS_PER_GROUP + pr) * 2 + hh
                    seg = cols[hh] - acst_ref[h:h + 1, :]
                    lm = jnp.exp(jnp.where(causal, seg, NEG_BIG))
                    ys.append(_dot((gmat * lm).astype(BF16), xp))
                ydiag = jnp.where(lt64, ys[0], ys[1])
                osl = slice(g * GROUP_W + pr * LANES, g * GROUP_W + (pr + 1) * LANES)
                y_ref[:, osl] = ydiag + yoff[:, psl] + xg[:, psl] * dskip_ref[:, osl]

    row = lambda w: pl.BlockSpec((cl, w), lambda c: (c, 0))
    return _call(
        body, side, name="ssd_fwd", grid=(nc,),
        in_specs=[row(CONV_DIM), row(D_INNER), row(D_INNER),
                  pl.BlockSpec((SSM_HEADS, cl), lambda c: (0, c)), pl.BlockSpec((1, D_INNER), lambda c: (0, 0))],
        out_specs=[row(D_INNER), pl.BlockSpec((None, SSM_STATE, D_INNER), lambda c: (c, 0, 0))],
        out_shape=[jax.ShapeDtypeStruct((t, D_INNER), F32), jax.ShapeDtypeStruct((nc, SSM_STATE, D_INNER), F32)],
        scratch_shapes=[pltpu.VMEM((SSM_STATE, D_INNER), F32)],
        semantics=("arbitrary",), args=(xbc, dt_rep, acs_rep, acs_t, dskip_rep),
    )


def _ssd_bwd(xbc, dt_rep, acs_rep, acs_t, dskip_rep, a_rep, hin_all, dy, side=None):
    t = xbc.shape[0]
    cl = SSD_CHUNK
    nc = t // cl

    def body(xbc_ref, dt_ref, acs_ref, acst_ref, dskip_ref, a_ref, hin_ref, dy_ref,
             dxbc_ref, ddt_ref, da_ref, dds_ref, dstate_ref, dacs_ref, dxs_ref):
        step = pl.program_id(0)

        @pl.when(step == 0)
        def _():
            dstate_ref[...] = jnp.zeros_like(dstate_ref)
            da_ref[...] = jnp.zeros_like(da_ref)
            dds_ref[...] = jnp.zeros_like(dds_ref)

        bd = _head_block_diag()
        lt64 = _lane_lt64(cl)
        li = lax.broadcasted_iota(jnp.int32, (cl, cl), 0)
        si = lax.broadcasted_iota(jnp.int32, (cl, cl), 1)
        lower = li >= si
        upper = si >= li
        last_row = lax.broadcasted_iota(jnp.int32, (cl, GROUP_W), 0) == cl - 1
        for g in range(SSM_GROUPS):
            gsl = slice(g * GROUP_W, (g + 1) * GROUP_W)
            bsl = slice(D_INNER + g * SSM_STATE, D_INNER + (g + 1) * SSM_STATE)
            csl = slice(D_INNER + SSM_GROUPS * SSM_STATE + g * SSM_STATE,
                        D_INNER + SSM_GROUPS * SSM_STATE + (g + 1) * SSM_STATE)
            xg = xbc_ref[:, gsl]
            bg = xbc_ref[:, bsl]
            cg = xbc_ref[:, csl]
            bgb, cgb = bg.astype(BF16), cg.astype(BF16)
            acs = acs_ref[:, gsl]
            xdt = xg * dt_ref[:, gsl]
            atot = acs[cl - 1:cl, :]
            eg = jnp.exp(acs)
            dk = jnp.exp(atot - acs)
            etot = jnp.exp(atot)
            hin = hin_ref[:, gsl]
            hinb = hin.astype(BF16)
            dh = dstate_ref[:, gsl]
            dhb = dh.astype(BF16)
            dyg = dy_ref[:, gsl]

            gmat = _dot_nt(cgb, bgb)
            gmat_t = _dot_nt(bgb, cgb)
            ch = _dot(cgb, hinb)
            dacs = _head_sums(dyg * ch * eg, bd)
            dye = (dyg * eg).astype(BF16)
            dc = _dot_nt(dye, hinb)
            dhin = _dot(cg.T.astype(BF16), dye)
            bdh = _dot(bgb, dhb)
            dxs = bdh * dk
            xdk = xdt * dk
            db = _dot_nt(xdk.astype(BF16), dhb)
            ddk = _head_sums(bdh * xdk, bd)
            dacs = dacs - ddk
            datot = jnp.sum(ddk, axis=0, keepdims=True) + etot * _head_sums(
                jnp.sum(dh * hin, axis=0, keepdims=True), bd)
            dacs = dacs + jnp.where(last_row, datot, 0.0)
            dstate_ref[:, gsl] = dh * etot + dhin

            xdtb = xdt.astype(BF16)
            dgsum = jnp.zeros((cl, cl), F32)
            dgsum_t = jnp.zeros((cl, cl), F32)
            for pr in range(PAIRS_PER_GROUP):
                psl = slice(pr * LANES, (pr + 1) * LANES)
                cols = _head_cols(acs[:, psl], lt64)
                xp = xdtb[:, psl]
                dyp = dyg[:, psl].astype(BF16)
                dx1, dac = [], []
                for hh in range(2):
                    h = (g * PAIRS_PER_GROUP + pr) * 2 + hh
                    mine = lt64 if hh == 0 else jnp.logical_not(lt64)
                    row = acst_ref[h:h + 1, :]
                    lm = jnp.exp(jnp.where(lower, cols[hh] - row, NEG_BIG))
                    lm_t = jnp.exp(jnp.where(upper, row - cols[hh], NEG_BIG))
                    dyh = jnp.where(mine, dyp, jnp.zeros_like(dyp))
                    xh = jnp.where(mine, xp, jnp.zeros_like(xp))
                    dm = _dot_nt(dyh, xp)
                    dm_t = _dot_nt(xh, dyp)
                    m_t = gmat_t * lm_t
                    dx1.append(_dot(m_t.astype(BF16), dyp))
                    w = dm * (gmat * lm)
                    w_t = dm_t * m_t
                    dac.append(jnp.sum(w, axis=1, keepdims=True) - jnp.sum(w_t, axis=1, keepdims=True))
                    dgsum = dgsum + dm * lm
                    dgsum_t = dgsum_t + dm_t * lm_t
                osl = slice(g * GROUP_W + pr * LANES, g * GROUP_W + (pr + 1) * LANES)
                dxs_ref[:, osl] = dxs[:, psl] + jnp.where(lt64, dx1[0], dx1[1])
                dacs_ref[:, osl] = dacs[:, psl] + jnp.where(lt64, jnp.broadcast_to(dac[0], (cl, LANES)),
                                                             jnp.broadcast_to(dac[1], (cl, LANES)))
            dxbc_ref[:, csl] = dc + _dot(dgsum.astype(BF16), bgb)
            dxbc_ref[:, bsl] = db + _dot(dgsum_t.astype(BF16), cgb)

        dadt = _split_dot(upper.astype(BF16), dacs_ref[...])
        xall = xbc_ref[:, 0:D_INNER]
        dtall = dt_ref[...]
        dxsall = dxs_ref[...]
        dyall = dy_ref[...]
        ddt_rep = dadt * a_ref[...] + _head_sums(dxsall * xall, bd)
        chan = lax.broadcasted_iota(jnp.int32, (D_INNER, LANES), 0)
        head = lax.broadcasted_iota(jnp.int32, (D_INNER, LANES), 1)
        ddt_ref[...] = _select_dot(ddt_rep, (chan == head * SSM_HEAD_DIM).astype(BF16))
        dxbc_ref[:, 0:D_INNER] = dxsall * dtall + dyall * dskip_ref[...]
        da_ref[...] += jnp.sum(dadt * dtall, axis=0, keepdims=True)
        dds_ref[...] += jnp.sum(dyall * xall, axis=0, keepdims=True)

        @pl.when(step == nc - 1)
        def _():
            dds_ref[...] = _head_sums(dds_ref[...], bd)

    row = lambda w: pl.BlockSpec((cl, w), lambda c: (nc - 1 - c, 0))
    vec = pl.BlockSpec((1, D_INNER), lambda c: (0, 0))
    return _call(
        body, side, name="ssd_bwd", grid=(nc,),
        in_specs=[row(CONV_DIM), row(D_INNER), row(D_INNER),
                  pl.BlockSpec((SSM_HEADS, cl), lambda c: (0, nc - 1 - c)), vec, vec,
                  pl.BlockSpec((None, SSM_STATE, D_INNER), lambda c: (nc - 1 - c, 0, 0)), row(D_INNER)],
        out_specs=[row(CONV_DIM), row(LANES), vec, vec],
        out_shape=[jax.ShapeDtypeStruct((t, CONV_DIM), F32), jax.ShapeDtypeStruct((t, LANES), F32),
                   jax.ShapeDtypeStruct((1, D_INNER), F32), jax.ShapeDtypeStruct((1, D_INNER), F32)],
        scratch_shapes=[pltpu.VMEM((SSM_STATE, D_INNER), F32), pltpu.VMEM((cl, D_INNER), F32),
                        pltpu.VMEM((cl, D_INNER), F32)],
        semantics=("arbitrary",), args=(xbc, dt_rep, acs_rep, acs_t, dskip_rep, a_rep, hin_all, dy),
    )


def _gate_norm_fwd(y, z, w):
    t, c = y.shape
    tm = _tile(t, 256)

    def body(y_ref, z_ref, w_ref, o_ref):
        for g in range(SSM_GROUPS):
            gsl = slice(g * GROUP_W, (g + 1) * GROUP_W)
            zv = z_ref[:, gsl]
            v = y_ref[:, gsl] * (zv * _sigmoid(zv))
            r = lax.rsqrt(jnp.mean(v * v, axis=-1, keepdims=True) + NORM_EPS)
            o_ref[:, gsl] = (v * r * w_ref[:, gsl]).astype(BF16)

    row = pl.BlockSpec((tm, c), lambda i: (i, 0))
    return pl.pallas_call(
        body, name="gate_norm_fwd", grid=(t // tm,),
        in_specs=[row, row, pl.BlockSpec((1, c), lambda i: (0, 0))], out_specs=row,
        out_shape=jax.ShapeDtypeStruct((t, c), BF16),
        compiler_params=_params("parallel"),
    )(y, z, w)


def _gate_norm_bwd(y, z, w, dout, side=None):
    t, c = y.shape
    tm = _tile(t, 256)

    def body(y_ref, z_ref, w_ref, do_ref, dy_ref, dz_ref, dw_ref):
        @pl.when(pl.program_id(0) == 0)
        def _():
            dw_ref[...] = jnp.zeros_like(dw_ref)

        for g in range(SSM_GROUPS):
            gsl = slice(g * GROUP_W, (g + 1) * GROUP_W)
            zv, yv, dov = z_ref[:, gsl], y_ref[:, gsl], do_ref[:, gsl]
            sg = _sigmoid(zv)
            sz = zv * sg
            v = yv * sz
            r = lax.rsqrt(jnp.mean(v * v, axis=-1, keepdims=True) + NORM_EPS)
            vh = v * r
            dvh = dov * w_ref[:, gsl]
            mean = jnp.mean(dvh * vh, axis=-1, keepdims=True)
            dv = r * (dvh - vh * mean)
            dy_ref[:, gsl] = dv * sz
            dz_ref[:, gsl] = (dv * yv * (sg * (1.0 + zv * (1.0 - sg)))).astype(BF16)
            dw_ref[:, gsl] += jnp.sum(dov * vh, axis=0, keepdims=True)

    row = pl.BlockSpec((tm, c), lambda i: (i, 0))
    vec = pl.BlockSpec((1, c), lambda i: (0, 0))
    return _call(
        body, side, name="gate_norm_bwd", grid=(t // tm,),
        in_specs=[row, row, vec, row], out_specs=[row, row, vec],
        out_shape=[jax.ShapeDtypeStruct((t, c), F32), jax.ShapeDtypeStruct((t, c), BF16),
                   jax.ShapeDtypeStruct((1, c), F32)],
        scratch_shapes=[], semantics=("arbitrary",), args=(y, z, w, dout),
    )


ATT_W = ATT_HEADS * ATT_HEAD_DIM
N_QKV_BLOCKS = 9
ATT_SCALE = 1.0 / math.sqrt(ATT_HEAD_DIM)


def _head_rmsnorm(x, gain, bd):
    ms = _head_sums(x * x, bd, terms=1) * (1.0 / ATT_HEAD_DIM)
    return x * lax.rsqrt(ms + NORM_EPS) * gain


def _class_rows(ref, blk, r, dil):
    span = ATT_BLOCK * dil
    sub = ref.at[pl.ds(pl.multiple_of(blk * span, span), span), :]
    return sub[...] if dil == 1 else sub[pl.ds(r, ATT_BLOCK, stride=dil), :]


def _store_class_rows(ref, blk, r, dil, val):
    span = ATT_BLOCK * dil
    sub = ref.at[pl.ds(pl.multiple_of(blk * span, span), span), :]
    if dil == 1:
        sub[...] = val
    else:
        sub[pl.ds(r, ATT_BLOCK, stride=dil), :] = val


PAIRS = ATT_HEADS // 2


def _pair_col(g, j):
    return lambda pair: (0, (g * 3 + j) * PAIRS + pair)


def _pair_slopes(pair):
    steps = jnp.full((1, 2 * ATT_BLOCK), 2 * pair + 1, jnp.int32).astype(F32)
    first = jnp.exp(steps * (-0.5 * math.log(2.0)))
    return first, first * (2.0 ** -0.5)


NORM_ROWS = 512


ROW_SLICES = 4
SLICE_ROWS = 2 * ATT_BLOCK // ROW_SLICES


def _fill_band_bias(bias_ref, pair, dil, transposed):
    bq = ATT_BLOCK
    a = lax.broadcasted_iota(jnp.int32, (2 * bq, 2 * bq), 0) % bq
    b = lax.broadcasted_iota(jnp.int32, (2 * bq, 2 * bq), 1)
    dist = (b - a) if transposed else (a + bq - b)
    in_band = (dist >= 0) & (dist <= bq)
    s0, s1 = _pair_slopes(pair)
    first_head = lax.broadcasted_iota(jnp.int32, (2 * bq, 2 * bq), 0) < bq
    bias = jnp.where(first_head, s0, s1) * (dist.astype(F32) * float(dil))
    inside = (b < bq) if transposed else (b >= bq)
    bias_ref[1] = jnp.where(in_band, bias, -NEG_BIG)
    bias_ref[0] = jnp.where(in_band & inside, bias, -NEG_BIG)


def _row_slices():
    return [slice(i * SLICE_ROWS, (i + 1) * SLICE_ROWS) for i in range(ROW_SLICES)]


def _stack_heads(tile):
    rows = lax.broadcasted_iota(jnp.int32, (2 * ATT_BLOCK, LANES), 0) < ATT_BLOCK
    lanes = lax.broadcasted_iota(jnp.int32, (2 * ATT_BLOCK, LANES), 1) < ATT_HEAD_DIM
    both = jnp.concatenate([tile, tile], axis=0)
    return jnp.where(rows == lanes, both, jnp.zeros_like(both))


def _unstack_heads(stacked, lt64):
    return jnp.where(lt64, stacked[:ATT_BLOCK], stacked[ATT_BLOCK:])


ITEMS_PER_PASS = 4


def _item_loop(nb, dil, work):
    if dil == 1:
        def trip(i, carry):
            work([(i * ITEMS_PER_PASS + b, 0) for b in range(ITEMS_PER_PASS)])
            return carry

        lax.fori_loop(0, nb // ITEMS_PER_PASS, trip, 0)
    else:
        def trip(n, carry):
            for r0 in range(0, dil, ITEMS_PER_PASS):
                work([(n, r0 + j) for j in range(ITEMS_PER_PASS)])
            return carry

        lax.fori_loop(0, nb, trip, 0)


def _qk_normalised(tile, j, gq_ref, gk_ref):
    kind = (j // (ATT_W // tile.shape[1])) % 3
    gain = jnp.where(kind == 0, gq_ref[...] * ATT_SCALE, gk_ref[...])
    return jnp.where(kind == 2, tile, _head_rmsnorm(tile, gain, _head_block_diag()))


def _attn_fwd(qkn, g, dil):
    t = qkn.shape[0]
    nb = t // dil // ATT_BLOCK
    bq = ATT_BLOCK

    def body(qn_ref, kn_ref, v_ref, o_ref, l_ref, bias_ref):
        _fill_band_bias(bias_ref, pl.program_id(0), dil, False)
        lt64 = _lane_lt64(bq)

        def work(items):
            scores, values, probs = [], [], []
            for n, r in items:
                prev = jnp.maximum(n - 1, 0)
                q2 = _stack_heads(_class_rows(qn_ref, n, r, dil).astype(BF16))
                kcat = jnp.concatenate([_class_rows(kn_ref, prev, r, dil), _class_rows(kn_ref, n, r, dil)],
                                       axis=0).astype(BF16)
                values.append(jnp.concatenate([_class_rows(v_ref, prev, r, dil), _class_rows(v_ref, n, r, dil)],
                                              axis=0).astype(BF16))
                scores.append(_dot_nt(q2, kcat))
            for (n, r), sc in zip(items, scores):
                bias = bias_ref.at[jnp.minimum(n, 1)]
                ps, inv, lses = [], [], []
                for rows in _row_slices():
                    s = sc[rows] - bias[rows, :]
                    m = jnp.max(s, axis=1, keepdims=True)
                    p = jnp.exp(s - m)
                    l = jnp.sum(p, axis=1, keepdims=True)
                    ps.append(p.astype(BF16))
                    inv.append(jnp.broadcast_to(1.0 / l, (SLICE_ROWS, LANES)))
                    lses.append(jnp.broadcast_to(m + jnp.log(l), (SLICE_ROWS, LANES)))
                probs.append((jnp.concatenate(ps, axis=0), jnp.concatenate(inv, axis=0)))
                _store_class_rows(l_ref, n, r, dil, _unstack_heads(jnp.concatenate(lses, axis=0), lt64))
            for (n, r), (p, inv), vcat in zip(items, probs, values):
                _store_class_rows(o_ref, n, r, dil, _unstack_heads(_dot(p, vcat) * inv, lt64))

        _item_loop(nb, dil, work)

    col = lambda j: pl.BlockSpec((t, LANES), _pair_col(g, j))
    out = pl.BlockSpec((t, LANES), lambda pair: (0, pair))
    return pl.pallas_call(
        body, name=f"attn_fwd_g{g}", grid=(PAIRS,),
        in_specs=[col(0), col(1), col(2)], out_specs=[out, out],
        out_shape=[jax.ShapeDtypeStruct((t, ATT_W), F32), jax.ShapeDtypeStruct((t, ATT_W), F32)],
        scratch_shapes=[pltpu.VMEM((2, 2 * bq, 2 * bq), F32)],
        compiler_params=_params("parallel"),
    )(qkn, qkn, qkn)


def _one_per_head(rep):
    chan = lax.broadcasted_iota(jnp.int32, (ATT_W, LANES), 0)
    head = lax.broadcasted_iota(jnp.int32, (ATT_W, LANES), 1)
    return _select_dot(rep, (chan == head * ATT_HEAD_DIM).astype(BF16))


def _attn_combine_fwd(outs, lses):
    t = outs[0].shape[0]
    tm = _tile(t, 256)

    def body(o0, o1, o2, l0, l1, l2, ob_ref, of_ref, lt_ref, lc_ref):
        a, b, c = l0[...], l1[...], l2[...]
        m = jnp.maximum(jnp.maximum(a, b), c)
        ea, eb, ec = jnp.exp(a - m), jnp.exp(b - m), jnp.exp(c - m)
        ssum = ea + eb + ec
        o = (ea * o0[...] + eb * o1[...] + ec * o2[...]) / ssum
        ob_ref[...] = o.astype(BF16)
        of_ref[...] = o
        lse = m + jnp.log(ssum)
        lt_ref[...] = lse
        lc_ref[...] = _one_per_head(lse)

    row = pl.BlockSpec((tm, ATT_W), lambda i: (i, 0))
    return pl.pallas_call(
        body, name="attn_combine_fwd", grid=(t // tm,),
        in_specs=[row] * 6, out_specs=[row] * 3 + [pl.BlockSpec((tm, LANES), lambda i: (i, 0))],
        out_shape=[jax.ShapeDtypeStruct((t, ATT_W), BF16), jax.ShapeDtypeStruct((t, ATT_W), F32),
                   jax.ShapeDtypeStruct((t, ATT_W), F32), jax.ShapeDtypeStruct((t, LANES), F32)],
        compiler_params=_params("parallel"),
    )(*outs, *lses)


def _attn_combine_bwd(do, o):
    t = do.shape[0]
    tm = _tile(t, 256)

    def body(do_ref, o_ref, dl_ref, dc_ref):
        dl = _head_sums(do_ref[...] * o_ref[...], _head_block_diag())
        dl_ref[...] = dl
        dc_ref[...] = _one_per_head(dl)

    row = pl.BlockSpec((tm, ATT_W), lambda i: (i, 0))
    return pl.pallas_call(
        body, name="attn_combine_bwd", grid=(t // tm,),
        in_specs=[row, row], out_specs=[row, pl.BlockSpec((tm, LANES), lambda i: (i, 0))],
        out_shape=[jax.ShapeDtypeStruct((t, ATT_W), F32), jax.ShapeDtypeStruct((t, LANES), F32)],
        compiler_params=_params("parallel"),
    )(do, o)


def _head_rmsnorm_bwd(x_ref, dy_ref, gain_ref, dx_ref, dgain_ref):
    bd = _head_block_diag()
    gain = gain_ref[...]

    def step(i, acc):
        rows = pl.ds(pl.multiple_of(i * NORM_ROWS, NORM_ROWS), NORM_ROWS)
        x, dy = x_ref[rows, :], dy_ref[rows, :]
        r = lax.rsqrt(_head_sums(x * x, bd, terms=1) * (1.0 / ATT_HEAD_DIM) + NORM_EPS)
        xh = x * r
        dxh = dy * gain
        mean = _head_sums(dxh * xh, bd, terms=1) * (1.0 / ATT_HEAD_DIM)
        dx_ref[rows, :] = (r * (dxh - xh * mean)).astype(BF16)
        return acc + jnp.sum(dy * xh, axis=0, keepdims=True)

    acc = lax.fori_loop(0, x_ref.shape[0] // NORM_ROWS, step, jnp.zeros((1, LANES), F32))
    dgain_ref[...] = jnp.broadcast_to(acc, dgain_ref.shape)


def _attn_bwd_dq(qkv, qkn, gq, do, l_rep, dl_rep, g, dil):
    t = qkv.shape[0]
    nb = t // dil // ATT_BLOCK
    bq = ATT_BLOCK

    def body(q_ref, qn_ref, kn_ref, v_ref, gq_ref, do_ref, l_ref, dl_ref, dx_ref, dgain_ref, bias_ref, dq_ref):
        _fill_band_bias(bias_ref, pl.program_id(0), dil, False)
        lt64 = _lane_lt64(bq)

        def per_row(tile):
            cols = _head_cols(tile, lt64)
            half = jnp.concatenate([cols[0], cols[1]], axis=0)
            return jnp.concatenate([half, half], axis=1)

        def work(items):
            products, keys, dscores = [], [], []
            for n, r in items:
                prev = jnp.maximum(n - 1, 0)
                q2 = _stack_heads(_class_rows(qn_ref, n, r, dil).astype(BF16))
                do2 = _stack_heads(_class_rows(do_ref, n, r, dil).astype(BF16))
                kcat = jnp.concatenate([_class_rows(kn_ref, prev, r, dil), _class_rows(kn_ref, n, r, dil)],
                                       axis=0).astype(BF16)
                vcat = jnp.concatenate([_class_rows(v_ref, prev, r, dil), _class_rows(v_ref, n, r, dil)],
                                       axis=0).astype(BF16)
                keys.append(kcat)
                products.append((_dot_nt(q2, kcat), _dot_nt(do2, vcat)))
            for (n, r), (scores, dps) in zip(items, products):
                bias = bias_ref.at[jnp.minimum(n, 1)]
                lse = per_row(_class_rows(l_ref, n, r, dil))
                dl = per_row(_class_rows(dl_ref, n, r, dil))
                dss = []
                for rows in _row_slices():
                    p = jnp.exp(scores[rows] - bias[rows, :] - lse[rows])
                    dss.append((p * (dps[rows] - dl[rows])).astype(BF16))
                dscores.append(jnp.concatenate(dss, axis=0))
            for (n, r), ds, kcat in zip(items, dscores, keys):
                _store_class_rows(dq_ref, n, r, dil, _unstack_heads(_dot(ds, kcat) * ATT_SCALE, lt64))

        _item_loop(nb, dil, work)
        _head_rmsnorm_bwd(q_ref, dq_ref, gq_ref, dx_ref, dgain_ref)

    col = lambda j: pl.BlockSpec((t, LANES), _pair_col(g, j))
    vec = pl.BlockSpec((1, LANES), lambda pair: (0, 0))
    tok = pl.BlockSpec((t, LANES), lambda pair: (0, pair))
    return pl.pallas_call(
        body, name=f"attn_bwd_dq_g{g}", grid=(PAIRS,),
        in_specs=[col(0), col(0), col(1), col(2), vec, tok, tok, tok],
        out_specs=[tok, pl.BlockSpec((None, 8, LANES), lambda pair: (pair, 0, 0))],
        out_shape=[jax.ShapeDtypeStruct((t, ATT_W), BF16), jax.ShapeDtypeStruct((PAIRS, 8, LANES), F32)],
        scratch_shapes=[pltpu.VMEM((2, 2 * bq, 2 * bq), F32), pltpu.VMEM((t, LANES), F32)],
        compiler_params=_params("parallel"),
    )(qkv, qkn, qkn, qkn, gq, do, l_rep, dl_rep)


def _attn_bwd_dkv(qkv, qkn, gk, do, l_row, dl_row, g, dil):
    t = qkv.shape[0]
    nb = t // dil // ATT_BLOCK
    bq = ATT_BLOCK

    def body(k_ref, qn_ref, kn_ref, v_ref, gk_ref, do_ref, l_ref, dl_ref, dkx_ref, dvx_ref, dgain_ref, bias_ref,
             dk_ref, dv_ref):
        _fill_band_bias(bias_ref, pl.program_id(0), dil, True)
        lt64 = _lane_lt64(bq)

        def per_query(ref, hh, lane_c, lane_n):
            return jnp.concatenate([ref[hh:hh + 1, pl.ds(lane_c, bq)], ref[hh:hh + 1, pl.ds(lane_n, bq)]], axis=1)

        def work(items):
            products, operands, weights = [], [], []
            for n, r in items:
                nxt = jnp.minimum(n + 1, nb - 1)
                k2 = _stack_heads(_class_rows(kn_ref, n, r, dil).astype(BF16))
                v2 = _stack_heads(_class_rows(v_ref, n, r, dil).astype(BF16))
                qcat = jnp.concatenate([_class_rows(qn_ref, n, r, dil), _class_rows(qn_ref, nxt, r, dil)],
                                       axis=0).astype(BF16)
                docat = jnp.concatenate([_class_rows(do_ref, n, r, dil), _class_rows(do_ref, nxt, r, dil)],
                                        axis=0).astype(BF16)
                operands.append((qcat, docat))
                products.append((_dot_nt(k2, qcat), _dot_nt(v2, docat)))
            for (n, r), (scores, dps) in zip(items, products):
                nxt = jnp.minimum(n + 1, nb - 1)
                bias = bias_ref.at[jnp.where(n == nb - 1, 0, 1)]
                lane_c = pl.multiple_of((r * nb + n) * bq, bq)
                lane_n = pl.multiple_of((r * nb + nxt) * bq, bq)
                lse = [per_query(l_ref, hh, lane_c, lane_n) for hh in range(2)]
                dl = [per_query(dl_ref, hh, lane_c, lane_n) for hh in range(2)]
                pts, dss = [], []
                for i, rows in enumerate(_row_slices()):
                    hh = i * SLICE_ROWS // bq
                    p_t = jnp.exp(scores[rows] - bias[rows, :] - lse[hh])
                    pts.append(p_t.astype(BF16))
                    dss.append((p_t * (dps[rows] - dl[hh])).astype(BF16))
                weights.append((jnp.concatenate(pts, axis=0), jnp.concatenate(dss, axis=0)))
            for (n, r), (p_t, ds_t), (qcat, docat) in zip(items, weights, operands):
                _store_class_rows(dv_ref, n, r, dil, _unstack_heads(_dot(p_t, docat), lt64))
                _store_class_rows(dk_ref, n, r, dil, _unstack_heads(_dot(ds_t, qcat), lt64))

        _item_loop(nb, dil, work)
        _head_rmsnorm_bwd(k_ref, dk_ref, gk_ref, dkx_ref, dgain_ref)

        def cast_rows(i, carry):
            rows = pl.ds(pl.multiple_of(i * NORM_ROWS, NORM_ROWS), NORM_ROWS)
            dvx_ref[rows, :] = dv_ref[rows, :].astype(BF16)
            return carry

        lax.fori_loop(0, t // NORM_ROWS, cast_rows, 0)

    col = lambda j: pl.BlockSpec((t, LANES), _pair_col(g, j))
    vec = pl.BlockSpec((1, LANES), lambda pair: (0, 0))
    tok = pl.BlockSpec((t, LANES), lambda pair: (0, pair))
    rows = pl.BlockSpec((None, 8, t), lambda pair: (pair, 0, 0))
    return pl.pallas_call(
        body, name=f"attn_bwd_dkv_g{g}", grid=(PAIRS,),
        in_specs=[col(1), col(0), col(1), col(2), vec, tok, rows, rows],
        out_specs=[tok, tok, pl.BlockSpec((None, 8, LANES), lambda pair: (pair, 0, 0))],
        out_shape=[jax.ShapeDtypeStruct((t, ATT_W), BF16), jax.ShapeDtypeStruct((t, ATT_W), BF16),
                   jax.ShapeDtypeStruct((PAIRS, 8, LANES), F32)],
        scratch_shapes=[pltpu.VMEM((2, 2 * bq, 2 * bq), F32), pltpu.VMEM((t, LANES), F32),
                        pltpu.VMEM((t, LANES), F32)],
        compiler_params=_params("parallel"),
    )(qkv, qkn, qkn, qkn, gk, do, l_row, dl_row)


def _rows_by_residue(one_per_head, dil):
    t = one_per_head.shape[0]
    per_head = one_per_head[:, :ATT_HEADS]
    rows = per_head.reshape(t // dil, dil, ATT_HEADS).transpose(2, 1, 0).reshape(PAIRS, 2, t)
    return jnp.pad(rows, ((0, 0), (0, 6), (0, 0)))


def _per_head(rep_row):
    return rep_row[0, ::SSM_HEAD_DIM]


def _rep_heads(v):
    return jnp.repeat(v, SSM_HEAD_DIM)[None, :]


def _pad_lanes(v):
    return jnp.pad(v, ((0, 0), (0, LANES - v.shape[1])))


class _NoOverlap:
    def side(self, host):
        return None

    def after(self, host):
        pass

    def begin_backward(self, grads):
        pass


def _hosted(plan, host, fn, *args, **kwargs):
    out = fn(*args, side=plan.side(host), **kwargs)
    plan.after(host)
    return out


def _ffn_ple_fwd(x1, h, p_i, prm, i, plan, next_gain=None, target=None):
    g, u, act = _hosted(plan, f"swiglu_fwd_{i}", _swiglu_fwd, h, prm["ffn_w_gate"][i], prm["ffn_w_up"][i],
                        name=f"swiglu_fwd_{i}")
    x2 = _hosted(plan, f"ffn_down_{i}", _matmul, act, prm["ffn_w_down"][i], mode="nn", addend=x1,
                 name=f"ffn_down_{i}")
    outs = _ple_fwd(x2, p_i, prm["ple_w_gate"][i], prm["ple_w_proj"][i], name=f"ple_fwd_{i}", next_gain=next_gain,
                    target=target)
    return outs, dict(x1=x1, h=h, g=g, u=u, act=act, x2=x2)


def _ffn_ple_bwd(dx3, p_i, prm, i, sv, grads, plan):
    ds, dple = _ple_bwd(sv["x2"], p_i, prm["ple_w_gate"][i], prm["ple_w_proj"][i], dx3, name=f"ple_bwd_{i}")
    grads["ple_w_gate"][i] = _matmul_tn(sv["x2"], ds, name=f"d_ple_w_gate_{i}")
    grads["ple_w_proj"][i] = _matmul_tn(dple, p_i, name=f"d_ple_w_proj_{i}")
    dx2 = _matmul(ds, prm["ple_w_gate"][i], mode="nt", addend=dx3, name=f"ple_dx_{i}")
    grads["ffn_w_down"][i] = _matmul_tn(sv["act"], dx2, name=f"d_ffn_w_down_{i}")
    dg, du = _hosted(plan, f"swiglu_bwd_{i}", _swiglu_bwd, dx2, prm["ffn_w_down"][i], sv["g"], sv["u"],
                     name=f"swiglu_bwd_{i}")
    grads["ffn_w_gate"][i] = _matmul_tn(dg, sv["h"], name=f"d_ffn_w_gate_{i}")
    grads["ffn_w_up"][i] = _matmul_tn(du, sv["h"], name=f"d_ffn_w_up_{i}")
    dh = _matmul(dg, prm["ffn_w_gate"][i], mode="nn", name=f"ffn_dh_gate_{i}")
    dx1, dgain = _matmul_rmsnorm_bwd(du, prm["ffn_w_up"][i], dh, sv["x1"], prm["norm_ffn"][i:i + 1], dx2,
                                     name=f"ffn_dh_up_{i}")
    grads["norm_ffn"][i] = dgain[0]
    return dx1


def _mamba_fwd(x0, prm, plan):
    h = _rmsnorm_fwd(x0, prm["norm_mix"][0:1], name="mix_norm_fwd_0")
    z = _hosted(plan, "ssm_in_z", _matmul, h, prm["ssm_w_z"], mode="nt", name="ssm_in_z")
    xbc_pre = _hosted(plan, "ssm_in_xbc", _matmul, h, prm["ssm_w_xbc"], mode="nt", name="ssm_in_xbc")
    dt_raw = _matmul(h, prm["ssm_w_dt"], mode="nt", name="ssm_in_dt")
    xbc = _hosted(plan, "conv_fwd", _conv_fwd, xbc_pre, prm["ssm_conv_w"], prm["ssm_conv_b"])
    dt_bias = _pad_lanes(prm["ssm_dt_bias"])
    a_log = _pad_lanes(prm["ssm_a_log"])
    acs, dt_rep, acs_rep = _ssd_prep_fwd(dt_raw, dt_bias, a_log)
    acs_t = acs[:, :SSM_HEADS].T
    dskip_rep = _rep_heads(prm["ssm_d_skip"][0])
    y, hin_all = _hosted(plan, "ssd_fwd", _ssd_fwd, xbc, dt_rep, acs_rep, acs_t, dskip_rep)
    yn = _gate_norm_fwd(y, z, prm["ssm_norm_w"])
    x1, h_ffn = _matmul(yn, prm["ssm_w_out"], mode="nn", addend=x0, name="ssm_out", tm=512, tn=D_MODEL,
                        second=(_rmsnorm_rows, [prm["norm_ffn"][0:1]], BF16))
    sv = dict(x0=x0, h=h, z=z, xbc_pre=xbc_pre, dt_raw=dt_raw, xbc=xbc, dt_bias=dt_bias, dt_rep=dt_rep,
              acs_rep=acs_rep, acs_t=acs_t, dskip_rep=dskip_rep, y=y, hin_all=hin_all, yn=yn)
    return x1, h_ffn, sv


def _mamba_bwd(dx1, prm, sv, grads, plan):
    grads["ssm_w_out"] = _matmul_tn(sv["yn"], dx1, name="d_ssm_w_out")
    dyn = _matmul(dx1, prm["ssm_w_out"], mode="nt", name="ssm_out_dx")
    dy, dz, dnw = _hosted(plan, "gate_norm_bwd", _gate_norm_bwd, sv["y"], sv["z"], prm["ssm_norm_w"], dyn)
    grads["ssm_norm_w"] = dnw
    a_rep = _rep_heads(-jnp.exp(prm["ssm_a_log"][0]))
    dxbc, ddt, da_rep, dds_rep = _hosted(plan, "ssd_bwd", _ssd_bwd, sv["xbc"], sv["dt_rep"], sv["acs_rep"],
                                             sv["acs_t"], sv["dskip_rep"], a_rep, sv["hin_all"], dy)
    grads["ssm_d_skip"] = _per_head(dds_rep)[None, :]
    grads["ssm_a_log"] = (_per_head(da_rep) * _per_head(a_rep))[None, :]
    ddt_raw, dbias = _ssd_prep_bwd(sv["dt_raw"], sv["dt_bias"], ddt)
    grads["ssm_dt_bias"] = dbias[:, :SSM_HEADS]
    du, dcw, dcb = _hosted(plan, "conv_bwd", _conv_bwd, sv["xbc_pre"], prm["ssm_conv_w"], prm["ssm_conv_b"], dxbc)
    grads["ssm_conv_w"] = dcw
    grads["ssm_conv_b"] = dcb
    h = sv["h"]
    grads["ssm_w_in"] = jnp.concatenate(
        [_matmul_tn(dz, h, name="d_ssm_w_z"), _matmul_tn(du, h, name="d_ssm_w_xbc"),
         _matmul_tn(ddt_raw, h, name="d_ssm_w_dt")[:SSM_HEADS]], axis=0)
    dh = _hosted(plan, "ssm_dh_z", _matmul, dz, prm["ssm_w_z"], mode="nn", name="ssm_dh_z")
    dh = _hosted(plan, "ssm_dh_xbc", _matmul, du, prm["ssm_w_xbc"], mode="nn", addend=dh, name="ssm_dh_xbc")
    dx0, dgain = _hosted(plan, "ssm_dh_dt", _matmul_rmsnorm_bwd, ddt_raw, prm["ssm_w_dt"], dh, sv["x0"],
                         prm["norm_mix"][0:1], dx1, name="ssm_dh_dt")
    grads["norm_mix"][0] = dgain[0]
    return dx0


def _attn_mixer_fwd(x0, h, prm, plan):
    n_heads = N_QKV_BLOCKS * ATT_HEADS
    gq = jnp.tile(prm["att_q_norm"], (1, n_heads))
    gk = jnp.tile(prm["att_k_norm"], (1, n_heads))
    qkv, qkn = _hosted(plan, "att_qkv", _matmul, h, prm["att_w_qkv"], mode="nt", name="att_qkv",
                       second=(_qk_normalised, [gq, gk], F32))
    outs, lses = [], []
    for g, (window, dil) in enumerate(DIL_PATTERNS):
        o_g, l_g = _attn_fwd(qkn, g, dil)
        outs.append(o_g)
        lses.append(l_g)
    o_b, o_f, l_rep, l_one = _attn_combine_fwd(outs, lses)
    x1, h_ffn = _matmul(o_b, prm["att_w_o"], mode="nn", addend=x0, name="att_out", tm=512, tn=D_MODEL,
                        second=(_rmsnorm_rows, [prm["norm_ffn"][1:2]], BF16))
    sv = dict(x0=x0, h=h, qkv=qkv, qkn=qkn, gq2=gq[:, :LANES], gk2=gk[:, :LANES], o_b=o_b, o_f=o_f, l_rep=l_rep,
              l_one=l_one)
    return x1, h_ffn, sv


def _attn_mixer_bwd(dx1, prm, sv, grads, plan):
    grads["att_w_o"] = _matmul_tn(sv["o_b"], dx1, name="d_att_w_o")
    do = _hosted(plan, "att_out_dx", _matmul, dx1, prm["att_w_o"], mode="nt", name="att_out_dx")
    dl_rep, dl_one = _attn_combine_bwd(do, sv["o_f"])
    blocks, dgq, dgk = [], [], []
    for g, (window, dil) in enumerate(DIL_PATTERNS):
        dq, dgq_g = _attn_bwd_dq(sv["qkv"], sv["qkn"], sv["gq2"], do, sv["l_rep"], dl_rep, g, dil)
        dk, dv, dgk_g = _attn_bwd_dkv(sv["qkv"], sv["qkn"], sv["gk2"], do, _rows_by_residue(sv["l_one"], dil),
                                      _rows_by_residue(dl_one, dil), g, dil)
        blocks += [dq, dk, dv]
        dgq.append(dgq_g)
        dgk.append(dgk_g)
    dqkv = jnp.concatenate(blocks, axis=1)

    def fold(parts):
        return jnp.stack(parts)[:, :, 0].reshape(-1, ATT_HEAD_DIM).sum(axis=0)[None, :]

    grads["att_q_norm"] = fold(dgq)
    grads["att_k_norm"] = fold(dgk)
    grads["att_w_qkv"] = _matmul_tn(dqkv, sv["h"], name="d_att_w_qkv")
    dx0, dgain = _hosted(plan, "att_qkv_dx", _matmul_rmsnorm_bwd, dqkv, prm["att_w_qkv"], None, sv["x0"],
                         prm["norm_mix"][1:2], dx1, name="att_qkv_dx")
    grads["norm_mix"][1] = dgain[0]
    return dx0


def _local_step(x, p, target, prm, plan=None):
    plan = plan or _NoOverlap()
    grads = {k: [None, None] for k in ("norm_mix", "norm_ffn", "ffn_w_gate", "ffn_w_up", "ffn_w_down",
                                       "ple_w_proj", "ple_w_gate")}
    plan.begin_backward(grads)
    x1, h1, sv_m = _mamba_fwd(x, prm, plan)
    (x3, h3), sv_f0 = _ffn_ple_fwd(x1, h1, p[0], prm, 0, plan, next_gain=prm["norm_mix"][1:2])
    x4, h4, sv_a = _attn_mixer_fwd(x3, h3, prm, plan)
    (dy, loss_row), sv_f1 = _ffn_ple_fwd(x4, h4, p[1], prm, 1, plan, target=target)
    dx4 = _ffn_ple_bwd(dy, p[1], prm, 1, sv_f1, grads, plan)
    dx3 = _attn_mixer_bwd(dx4, prm, sv_a, grads, plan)
    dx1 = _ffn_ple_bwd(dx3, p[0], prm, 0, sv_f0, grads, plan)
    dx0 = _mamba_bwd(dx1, prm, sv_m, grads, plan)
    return loss_row, dx0, grads


W_IN_SLAB_ROWS = 1312


def _position():
    return lax.axis_index("x"), lax.axis_index("y"), lax.axis_index("c")


def _other_chips(x, y):
    return [(1 - x, y), (x, 1 - y), (1 - x, 1 - y)]


def _remote(send_sems, recv_sems, k, src, dst, to):
    return pltpu.make_async_remote_copy(src_ref=src, dst_ref=dst, send_sem=send_sems.at[k], recv_sem=recv_sems.at[k],
                                        device_id=to, device_id_type=MESH)


def _gather_side(entries, whole=()):
    n, nw = len(entries), len(whole)

    def first_hop(ins, outs, send_sems, recv_sems):
        x, y, c = _position()
        cps = []
        for j, chip in enumerate(_other_chips(x, y)):
            for e in range(n):
                cps.append(_remote(send_sems, recv_sems, 6 * e + j, ins[e].at[c], outs[e].at[2 * x + y, c], (*chip, c)))
            for e in range(nw):
                cps.append(_remote(send_sems, recv_sems, 6 * n + 3 * e + j, ins[n + e], outs[n + e].at[2 * x + y],
                                   (*chip, c)))
        return cps

    def start(ins, outs, send_sems, recv_sems):
        for cp in first_hop(ins, outs, send_sems, recv_sems):
            cp.start()

    def finish(ins, outs, send_sems, recv_sems):
        x, y, c = _position()
        me, sibling = (x, y, c), (x, y, 1 - c)
        chips = _other_chips(x, y)
        passed_on = []
        for j, (px, py) in enumerate(chips):
            for e in range(n):
                landed = outs[e].at[2 * px + py, c]
                _remote(send_sems, recv_sems, 6 * e + j, landed, landed, me).wait_recv()
                passed_on.append(_remote(send_sems, recv_sems, 6 * e + 3 + j, landed, landed, sibling))
                passed_on[-1].start()
            for e in range(nw):
                landed = outs[n + e].at[2 * px + py]
                _remote(send_sems, recv_sems, 6 * n + 3 * e + j, landed, landed, me).wait_recv()
        for j, (px, py) in enumerate(chips):
            for e in range(n):
                passed = outs[e].at[2 * px + py, 1 - c]
                _remote(send_sems, recv_sems, 6 * e + 3 + j, passed, passed, me).wait_recv()
        for cp in first_hop(ins, outs, send_sems, recv_sems) + passed_on:
            cp.wait_send()

    shapes = [jax.ShapeDtypeStruct((N_CHIPS,) + a.shape, a.dtype) for a in list(entries) + list(whole)]
    return _Side(list(entries) + list(whole), shapes, 6 * n + 3 * nw, start, finish)


def _run_side(side, name):
    si, so = len(side.inputs), len(side.out_shapes)

    def body(*refs):
        ins, outs, send_sems, recv_sems = refs[:si], refs[si:si + so], refs[-2], refs[-1]
        side.start(ins, outs, send_sems, recv_sems)
        side.finish(ins, outs, send_sems, recv_sems)

    side.outputs = list(pl.pallas_call(
        body, name=name, in_specs=[ANY] * si, out_specs=[ANY] * so, out_shape=side.out_shapes,
        scratch_shapes=[pltpu.SemaphoreType.DMA((side.n_sems,)), pltpu.SemaphoreType.DMA((side.n_sems,))],
    )(*side.inputs))
    return side.outputs


def _swap_side(grads):
    n = len(grads)

    def copies(ins, outs, send_sems, recv_sems):
        x, y, c = _position()
        return [_remote(send_sems, recv_sems, e, ins[e].at[:, 1 - c], outs[e], (x, y, 1 - c)) for e in range(n)]

    def start(ins, outs, send_sems, recv_sems):
        for cp in copies(ins, outs, send_sems, recv_sems):
            cp.start()

    def finish(ins, outs, send_sems, recv_sems):
        for cp in copies(ins, outs, send_sems, recv_sems):
            cp.wait()

    shapes = [jax.ShapeDtypeStruct((N_CHIPS,) + g.shape[2:], g.dtype) for g in grads]
    return _Side(grads, shapes, n, start, finish)


def _chip_exchange_side(chipsums):
    n = len(chipsums)

    def copies(ins, outs, send_sems, recv_sems):
        x, y, c = _position()
        return [_remote(send_sems, recv_sems, 3 * e + j, ins[e].at[2 * tx + ty], outs[e].at[j], (tx, ty, c))
                for j, (tx, ty) in enumerate(_other_chips(x, y)) for e in range(n)]

    def start(ins, outs, send_sems, recv_sems):
        for cp in copies(ins, outs, send_sems, recv_sems):
            cp.start()

    def finish(ins, outs, send_sems, recv_sems):
        for cp in copies(ins, outs, send_sems, recv_sems):
            cp.wait()

    shapes = [jax.ShapeDtypeStruct((3,) + cs.shape[1:], cs.dtype) for cs in chipsums]
    return _Side(chipsums, shapes, 3 * n, start, finish)


def _share_side(totals):
    n = len(totals)

    def copies(ins, outs, send_sems, recv_sems):
        x, y, c = _position()
        return [_remote(send_sems, recv_sems, e, ins[e], outs[e], (x, y, 1 - c)) for e in range(n)]

    def start(ins, outs, send_sems, recv_sems):
        for cp in copies(ins, outs, send_sems, recv_sems):
            cp.start()

    def finish(ins, outs, send_sems, recv_sems):
        for cp in copies(ins, outs, send_sems, recv_sems):
            cp.wait()

    return _Side(totals, [jax.ShapeDtypeStruct(t.shape, t.dtype) for t in totals], n, start, finish)


def _reduce_rows(h):
    return h if h <= 704 else h // 2


def _add_sibling(grad, recv, c_idx, *, name):
    _, _, h, cw = grad.shape
    th = _reduce_rows(h)

    def body(c_ref, g_ref, r_ref, o_ref):
        o_ref[...] = (g_ref[...] + r_ref[...]).astype(BF16)

    return pl.pallas_call(
        body, name=name,
        grid_spec=pltpu.PrefetchScalarGridSpec(
            num_scalar_prefetch=1, grid=(N_CHIPS, h // th),
            in_specs=[pl.BlockSpec((None, None, th, cw), lambda s, i, c_ref: (s, c_ref[0], i, 0)),
                      pl.BlockSpec((None, th, cw), lambda s, i, c_ref: (s, i, 0))],
            out_specs=pl.BlockSpec((None, th, cw), lambda s, i, c_ref: (s, i, 0))),
        out_shape=jax.ShapeDtypeStruct((N_CHIPS, h, cw), BF16),
        compiler_params=_params("parallel", "parallel"),
    )(c_idx, grad, recv)


def _add_chips(chipsum, recv, s_idx, *, name):
    _, h, cw = chipsum.shape
    th = _reduce_rows(h)

    def body(s_ref, own_ref, r_ref, o_ref):
        o_ref[...] = ((own_ref[...].astype(F32) + r_ref[0].astype(F32)) + r_ref[1].astype(F32)) + r_ref[2].astype(F32)

    return pl.pallas_call(
        body, name=name,
        grid_spec=pltpu.PrefetchScalarGridSpec(
            num_scalar_prefetch=1, grid=(h // th,),
            in_specs=[pl.BlockSpec((None, th, cw), lambda i, s_ref: (s_ref[0], i, 0)),
                      pl.BlockSpec((3, th, cw), lambda i, s_ref: (0, i, 0))],
            out_specs=pl.BlockSpec((th, cw), lambda i, s_ref: (i, 0))),
        out_shape=jax.ShapeDtypeStruct((h, cw), F32),
        compiler_params=_params("parallel"),
    )(s_idx, chipsum, recv)


def _adamw_math(w, g, m, v):
    m = ADAM_B1 * m + (1.0 - ADAM_B1) * g
    v = ADAM_B2 * v + (1.0 - ADAM_B2) * (g * g)
    m_hat = m / (1.0 - ADAM_B1 ** ADAM_STEP)
    v_hat = v / (1.0 - ADAM_B2 ** ADAM_STEP)
    delta = -ADAM_LR * (m_hat / (jnp.sqrt(v_hat) + ADAM_EPS) + ADAM_WD * w)
    return delta, m, v


ADAM_TILE_ELEMS = 256 * 1024


def _adamw(w, g, m, v, *, name):
    layers, rows, cols = w.shape
    tr = rows
    for cand in range(8, rows, 8):
        if rows % cand == 0 and cand * cols <= ADAM_TILE_ELEMS:
            tr = cand
    if rows * cols <= ADAM_TILE_ELEMS:
        tr = rows

    def body(w_ref, g_ref, m_ref, v_ref, d_ref, nm_ref, nv_ref):
        d, nm, nv = _adamw_math(w_ref[...], g_ref[...], m_ref[...], v_ref[...])
        d_ref[...] = d
        nm_ref[...] = nm
        nv_ref[...] = nv

    blk = pl.BlockSpec((None, tr, cols), lambda l, i: (l, i, 0))
    sds = jax.ShapeDtypeStruct(w.shape, F32)
    return pl.pallas_call(
        body, name=name, grid=(layers, rows // tr), in_specs=[blk] * 4, out_specs=[blk] * 3, out_shape=[sds] * 3,
        compiler_params=_params("parallel", "parallel"),
    )(w, g, m, v)


SMALL_LAYOUT = (("loss", 1), ("norm_mix", 16), ("norm_ffn", 16), ("ssm_conv_b", 24), ("ssm_dt_bias", 1),
                ("ssm_a_log", 1), ("ssm_d_skip", 1), ("ssm_norm_w", 16), ("att_q_norm", 1), ("att_k_norm", 1),
                ("conv_w_full", 96))
SMALL_ROWS = 176
N_DEVICES = 8


def _small_packs(dicts):
    parts = []
    for values in dicts:
        for name, rows in SMALL_LAYOUT:
            flat = values[name].reshape(-1).astype(F32)
            parts.append(jnp.pad(flat, (0, rows * LANES - flat.shape[0])).reshape(rows, LANES))
        used = sum(r for _, r in SMALL_LAYOUT)
        parts.append(jnp.zeros((SMALL_ROWS - used, LANES), F32))
    return jnp.concatenate(parts, axis=0).reshape(len(dicts), SMALL_ROWS, LANES)


def _small_unpack(pack, shapes):
    out, off = {}, 0
    for name, rows in SMALL_LAYOUT:
        shape = shapes[name]
        n = math.prod(shape)
        out[name] = pack[off:off + rows].reshape(-1)[:n].reshape(shape)
        off += rows
    return out


def _small_allreduce_adamw(g, w, m, v):
    def body(g_ref, w_ref, m_ref, v_ref, gs_ref, d_ref, nm_ref, nv_ref, buf, send_sems, recv_sems):
        x, y, c = _position()
        pos = (x, y, c)
        me = 4 * x + 2 * y + c
        buf[me] = g_ref[...]
        peers = []
        for k in range(1, N_DEVICES):
            bits = ((k >> 2) & 1, (k >> 1) & 1, k & 1)
            peers.append(tuple(1 - p if b else p for p, b in zip(pos, bits)))
        cps = [pltpu.make_async_remote_copy(src_ref=g_ref, dst_ref=buf.at[me], send_sem=send_sems.at[k],
                                            recv_sem=recv_sems.at[k], device_id=peer, device_id_type=MESH)
               for k, peer in enumerate(peers)]
        for cp in cps:
            cp.start()
        for k, (px, py, pc) in enumerate(peers):
            pltpu.make_async_remote_copy(src_ref=g_ref, dst_ref=buf.at[4 * px + 2 * py + pc],
                                         send_sem=send_sems.at[k], recv_sem=recv_sems.at[k],
                                         device_id=(px, py, pc), device_id_type=MESH).wait_recv()
        for cp in cps:
            cp.wait_send()
        total = buf[0]
        for dev in range(1, N_DEVICES):
            total = total + buf[dev]
        gs_ref[...] = total
        d, nm, nv = _adamw_math(w_ref[...], total, m_ref[...], v_ref[...])
        d_ref[...] = d
        nm_ref[...] = nm
        nv_ref[...] = nv

    vm = pl.BlockSpec(memory_space=pltpu.VMEM)
    sds = jax.ShapeDtypeStruct((SMALL_ROWS, LANES), F32)
    return pl.pallas_call(
        body, name="small_allreduce_adamw", in_specs=[vm] * 4, out_specs=[vm] * 4, out_shape=[sds] * 4,
        scratch_shapes=[pltpu.VMEM((N_DEVICES, SMALL_ROWS, LANES), F32),
                        pltpu.SemaphoreType.DMA((N_DEVICES - 1,)), pltpu.SemaphoreType.DMA((N_DEVICES - 1,))],
    )(g, w, m, v)


SMALL = tuple(n for n, _ in SMALL_LAYOUT if n not in ("loss", "conv_w_full"))
WEIGHTS = ("norm_mix", "norm_ffn", "ssm_w_in", "ssm_conv_w", "ssm_conv_b", "ssm_dt_bias", "ssm_a_log", "ssm_d_skip",
           "ssm_norm_w", "ssm_w_out", "att_w_qkv", "att_q_norm", "att_k_norm", "att_w_o", "ffn_w_gate", "ffn_w_up",
           "ffn_w_down", "ple_w_proj", "ple_w_gate")
COLUMN_SHARDED = ("ssm_w_in", "att_w_qkv", "ffn_w_gate", "ffn_w_up", "ple_w_proj")
LAYERED = ("ffn_w_gate", "ffn_w_up", "ffn_w_down", "ple_w_proj", "ple_w_gate")
UPDATED_TRANSPOSED = ("ssm_w_in", "ffn_w_gate", "ffn_w_up")
GATHER_ORDER = ("ssm_w_in", "ssm_w_out", "att_w_qkv", "att_w_o", "ffn_w_gate", "ffn_w_up", "ffn_w_down",
                "ple_w_proj", "ple_w_gate")


def _layers(n):
    return (0, 1) if n in LAYERED else (None,)


def _tag(key):
    return key[0] if key[1] is None else f"{key[0]}_{key[1]}"


QKV_PARTS = 3


def _weight_slab(w, key):
    n, i = key
    if n == "att_w_qkv":
        a = w[n][0].T
        rows = a.shape[0] // QKV_PARTS
        a = a[i * rows:(i + 1) * rows]
    else:
        a = w[n][0 if i is None else i]
        a = a.T if n in COLUMN_SHARDED else a
    if n == "ssm_w_in":
        a = jnp.pad(a, ((0, W_IN_SLAB_ROWS - a.shape[0]), (0, 0)))
    return a.reshape(2, a.shape[0] // 2, a.shape[1]).astype(BF16)


def _install(prm, key, gathered, own, s_me):
    n, i = key
    full = lax.dynamic_update_slice(gathered, own[None], (s_me, 0, 0, 0))
    full = full.reshape(N_CHIPS, 2 * full.shape[2], full.shape[3])
    if n == "att_w_qkv":
        parts = prm.setdefault("att_w_qkv_parts", {})
        parts[i] = full
        if len(parts) == QKV_PARTS:
            prm[n] = jnp.stack([parts[j] for j in range(QKV_PARTS)], axis=1).reshape(-1, D_MODEL)
        return
    if n == "ssm_w_in":
        rows = (D_INNER + CONV_DIM + SSM_HEADS) // N_CHIPS
        w_in_t = full[:, :rows].reshape(N_CHIPS * rows, D_MODEL)
        prm["ssm_w_z"] = w_in_t[:D_INNER]
        prm["ssm_w_xbc"] = w_in_t[D_INNER:D_INNER + CONV_DIM]
        prm["ssm_w_dt"] = jnp.pad(w_in_t[D_INNER + CONV_DIM:], ((0, LANES - SSM_HEADS), (0, 0)))
        return
    full = full.reshape(N_CHIPS * full.shape[1], full.shape[2])
    if i is None:
        prm[n] = full
    else:
        prm.setdefault(n, [None, None])[i] = full


def _grad_slab(grads, key):
    n, i = key
    g = grads[n] if i is None else grads[n][i]
    if n == "ssm_w_in":
        g = jnp.pad(g.reshape(N_CHIPS, g.shape[0] // N_CHIPS, D_MODEL),
                    ((0, 0), (0, W_IN_SLAB_ROWS - g.shape[0] // N_CHIPS), (0, 0)))
    rows = g.size // (N_CHIPS * g.shape[-1])
    return g.reshape(N_CHIPS, 2, rows // 2, g.shape[-1])


def _natural_shard(n, reduced, shape):
    def one(r):
        if n == "ssm_w_in":
            r = r[:shape[-1]]
        return r.T if n in COLUMN_SHARDED else r
    if n in LAYERED:
        return jnp.stack([one(r) for r in reduced]).reshape(shape)
    return one(reduced[0]).reshape(shape)


def kernel(x, p, norm_mix, norm_ffn, ssm_w_in, ssm_conv_w, ssm_conv_b, ssm_dt_bias, ssm_a_log, ssm_d_skip, ssm_norm_w, ssm_w_out, att_w_qkv, att_q_norm, att_k_norm, att_w_o, ffn_w_gate, ffn_w_up, ffn_w_down, ple_w_proj, ple_w_gate, loss_target, m_norm_mix, m_norm_ffn, m_ssm_w_in, m_ssm_conv_w, m_ssm_conv_b, m_ssm_dt_bias, m_ssm_a_log, m_ssm_d_skip, m_ssm_norm_w, m_ssm_w_out, m_att_w_qkv, m_att_q_norm, m_att_k_norm, m_att_w_o, m_ffn_w_gate, m_ffn_w_up, m_ffn_w_down, m_ple_w_proj, m_ple_w_gate, v_norm_mix, v_norm_ffn, v_ssm_w_in, v_ssm_conv_w, v_ssm_conv_b, v_ssm_dt_bias, v_ssm_a_log, v_ssm_d_skip, v_ssm_norm_w, v_ssm_w_out, v_att_w_qkv, v_att_q_norm, v_att_k_norm, v_att_w_o, v_ffn_w_gate, v_ffn_w_up, v_ffn_w_down, v_ple_w_proj, v_ple_w_gate):
    given = dict(locals())
    w = {n: given[n] for n in WEIGHTS}
    m = {n: given["m_" + n] for n in WEIGHTS}
    v = {n: given["v_" + n] for n in WEIGHTS}
    c_idx = lax.axis_index("c").astype(jnp.int32).reshape(1)
    s_idx = (2 * lax.axis_index("x") + lax.axis_index("y")).astype(jnp.int32).reshape(1)

    s_me = 2 * lax.axis_index("x") + lax.axis_index("y")
    first_core = lax.axis_index("c") == 0

    qkv_parts = [("att_w_qkv", j) for j in range(QKV_PARTS)]
    gather_plan = {
        "ssm_in_z": [("ssm_w_out", None)],
        "ssm_in_xbc": [("ffn_w_gate", 0)],
        "conv_fwd": [("ffn_w_up", 0)],
        "ssd_fwd": [("ffn_w_down", 0), ("ple_w_proj", 0), ("ple_w_gate", 0), ("att_w_o", None)],
        "swiglu_fwd_0": qkv_parts[:2],
        "ffn_down_0": qkv_parts[2:],
        "att_qkv": [(n, 1) for n in LAYERED],
    }
    mamba = [("ssm_w_in", None)]
    own = {k: _weight_slab(w, k) for k in mamba + sum(gather_plan.values(), [])}
    prm = {n: w[n] for n in SMALL}

    def land(group, outputs):
        for k, g in zip(group, outputs):
            _install(prm, k, g, own[k], s_me)

    first = _gather_side([own[k] for k in mamba], whole=[ssm_conv_w[0]])
    _run_side(first, "gather_mamba")
    land(mamba, first.outputs)
    conv = lax.dynamic_update_slice(first.outputs[-1], ssm_conv_w, (s_me, 0, 0))
    prm["ssm_conv_w"] = conv.transpose(1, 0, 2).reshape(CONV_WIDTH, CONV_DIM)

    ffn1 = [(n, 1) for n in LAYERED]
    attention = [("att_w_qkv", None), ("att_w_o", None)]
    ffn0 = [(n, 0) for n in LAYERED] + [("ssm_w_out", None)]
    reduce_plan = {"att_out_dx": [("swap", ffn1)], "att_qkv_dx": [("exchange", ffn1)],
                   "swiglu_bwd_0": [("swap", attention)], "gate_norm_bwd": [("swap", ffn0)],
                   "ssd_bwd": [("exchange", attention), ("exchange", ffn0)],
                   "ssm_dh_z": [("swap", mamba)], "ssm_dh_xbc": [("exchange", mamba)]}
    state = {}

    def swap_side(group):
        state[_tag(group[0]), "g4"] = g4 = [_grad_slab(state["grads"], k) for k in group]
        return _swap_side(g4)

    def add_siblings(group, from_sibling):
        state[_tag(group[0]), "chipsums"] = [
            _add_sibling(g, r, c_idx, name="add_sibling_" + _tag(k))
            for g, r, k in zip(state[_tag(group[0]), "g4"], from_sibling, group)]

    def exchange_side(group):
        return _chip_exchange_side(state[_tag(group[0]), "chipsums"])

    def add_chips(group, from_chips):
        for k, cs, r in zip(group, state[_tag(group[0]), "chipsums"], from_chips):
            state["total", k] = _add_chips(cs, r, s_idx, name="add_chips_" + _tag(k))

    class Plan(_NoOverlap):
        def __init__(self):
            self.carried = {host: _gather_side([own[k] for k in group]) for host, group in gather_plan.items()}

        def begin_backward(self, grads):
            state["grads"] = grads

        def side(self, host):
            if host in reduce_plan:
                self.parts = [swap_side(group) if step == "swap" else exchange_side(group)
                              for step, group in reduce_plan[host]]
                self.carried[host] = _sides_together(self.parts)
            elif host == share_host:
                self.carried[host] = _share_side([state["total", k] for k in order])
            return self.carried.get(host)

        def after(self, host):
            if host in gather_plan:
                land(gather_plan[host], self.carried[host].outputs)
            elif host in reduce_plan:
                _share_out(self.carried[host], self.parts)
                for (step, group), part in zip(reduce_plan[host], self.parts):
                    (add_siblings if step == "swap" else add_chips)(group, part.outputs)
            elif host == share_host:
                state["shared"] = self.carried[host].outputs

    order = mamba + ffn0 + attention + ffn1
    share_host = "ssm_dh_dt"
    loss_row, dx, grads = _local_step(x[0], p[:, 0], loss_target[0], prm, Plan())

    reduced = {}
    for k, theirs in zip(order, state["shared"]):
        lo = jnp.where(first_core, state["total", k], theirs)
        hi = jnp.where(first_core, theirs, state["total", k])
        reduced.setdefault(k[0], {})[k[1]] = jnp.concatenate([lo, hi], axis=0)
    reduced = {n: [by_layer[i] for i in _layers(n)] for n, by_layer in reduced.items()}

    grad, delta, new_m, new_v = {}, {}, {}, {}
    for n in GATHER_ORDER:
        if n in UPDATED_TRANSPOSED:
            flip = lambda a: a.transpose(0, 2, 1)
            cols = w[n].shape[-1]
            g_t = jnp.stack([r[:cols] for r in reduced[n]])
            grad[n] = flip(g_t)
            delta[n], new_m[n], new_v[n] = [flip(o) for o in _adamw(flip(w[n]), g_t, flip(m[n]), flip(v[n]),
                                                                    name="adamw_" + n)]
            continue
        grad[n] = _natural_shard(n, reduced[n], w[n].shape)
        delta[n], new_m[n], new_v[n] = _adamw(w[n], grad[n], m[n], v[n], name="adamw_" + n)

    small_g = {n: (jnp.stack(grads[n]) if isinstance(grads[n], list) else grads[n]) for n in SMALL}
    small_g["loss"] = loss_row
    small_g["conv_w_full"] = grads["ssm_conv_w"]
    zero = {"loss": jnp.zeros((1, LANES), F32), "conv_w_full": jnp.zeros((CONV_WIDTH, CONV_DIM), F32)}
    packs = _small_packs([small_g, {**w, **zero}, {**m, **zero}, {**v, **zero}])
    outs = _small_allreduce_adamw(packs[0], packs[1], packs[2], packs[3])
    shapes = {n: w[n].shape for n in SMALL}
    shapes["loss"] = (1, LANES)
    shapes["conv_w_full"] = (CONV_WIDTH, CONV_DIM)
    sg, sd, sm, sv = [_small_unpack(o, shapes) for o in outs]
    for n in SMALL:
        grad[n], delta[n], new_m[n], new_v[n] = sg[n], sd[n], sm[n], sv[n]
    loss = sg["loss"][0, 0]
    conv_cols = CONV_DIM // N_CHIPS
    grad["ssm_conv_w"] = lax.dynamic_slice(sg["conv_w_full"], (0, s_me * conv_cols), (CONV_WIDTH, conv_cols))[None]
    delta["ssm_conv_w"], new_m["ssm_conv_w"], new_v["ssm_conv_w"] = _adamw(
        ssm_conv_w, grad["ssm_conv_w"], m_ssm_conv_w, v_ssm_conv_w, name="adamw_ssm_conv_w")

    return (loss, dx[None], *[grad[n] for n in WEIGHTS], *[delta[n] for n in WEIGHTS],
            *[new_m[n] for n in WEIGHTS], *[new_v[n] for n in WEIGHTS])
```

```python
import math

import jax
import jax.numpy as jnp
from jax import lax
from jax.experimental import pallas as pl
from jax.experimental.pallas import tpu as pltpu

F32 = jnp.float32
BF16 = jnp.bfloat16
HIGHEST = lax.Precision.HIGHEST

NORM_EPS = 1e-6
ADAM_LR, ADAM_B1, ADAM_B2, ADAM_EPS, ADAM_WD, ADAM_STEP = 0.001, 0.9, 0.999, 1e-08, 0.01, 10

D_MODEL = 1024
D_INNER = 2048
SSM_HEADS = 32
SSM_HEAD_DIM = 64
SSM_GROUPS = 4
SSM_STATE = 128
SSD_CHUNK = 128
CONV_DIM = 3072
CONV_WIDTH = 4
ATT_HEADS = 16
ATT_HEAD_DIM = 64
DIL_PATTERNS = ((128, 1), (512, 4), (2048, 16))
ATT_BLOCK = 128
FFN_HIDDEN = 2816
PLE_DIM = 256

LANES = 128
V7X_VMEM_LIMIT = 56 * 1024 * 1024
NEG_BIG = -1e30

N_CHIPS = 4


def _params(*sem):
    return pltpu.CompilerParams(dimension_semantics=sem, vmem_limit_bytes=V7X_VMEM_LIMIT)


def _tile(n, pref):
    if n <= pref:
        return n
    best = None
    for t in range(LANES, pref + 1, LANES):
        if n % t == 0:
            best = t
    assert best is not None, (n, pref)
    return best


def _sigmoid(v):
    return 1.0 / (1.0 + jnp.exp(-v))


def _dot(a, b):
    return jnp.dot(a, b, preferred_element_type=F32)


def _dot_nt(a, b):
    return lax.dot_general(a, b, (((1,), (1,)), ((), ())), preferred_element_type=F32)


def _dot_tn(a, b):
    return lax.dot_general(a, b, (((0,), (0,)), ((), ())), preferred_element_type=F32)


def _head_block_diag():
    i = lax.broadcasted_iota(jnp.int32, (LANES, LANES), 0) // ATT_HEAD_DIM
    j = lax.broadcasted_iota(jnp.int32, (LANES, LANES), 1) // ATT_HEAD_DIM
    return (i == j).astype(BF16)


def _split_dot(ones, z):
    hi = z.astype(BF16)
    lo = (z - hi.astype(F32)).astype(BF16)
    return _dot(ones, hi) + _dot(ones, lo)


def _head_sums(z, bd, terms=2):
    hi = z.astype(BF16)
    lo = (z - hi.astype(F32)).astype(BF16) if terms == 2 else None
    parts = []
    for t in range(z.shape[1] // LANES):
        sl = slice(t * LANES, (t + 1) * LANES)
        part = _dot(hi[:, sl], bd)
        parts.append(part + _dot(lo[:, sl], bd) if terms == 2 else part)
    return parts[0] if len(parts) == 1 else jnp.concatenate(parts, axis=1)


def _lane_lt64(rows):
    return lax.broadcasted_iota(jnp.int32, (rows, LANES), 1) < ATT_HEAD_DIM


MESH = pl.DeviceIdType.MESH
ANY = pl.BlockSpec(memory_space=pl.ANY)


class _Side:
    def __init__(self, inputs, out_shapes, n_sems, start, finish):
        self.inputs, self.out_shapes, self.n_sems = list(inputs), list(out_shapes), n_sems
        self.start, self.finish = start, finish
        self.outputs = None


class _SemaphoresFrom:
    def __init__(self, sems, first):
        self.sems, self.first = sems, first

    @property
    def at(self):
        return self

    def __getitem__(self, k):
        return self.sems.at[self.first + k]


def _sides_together(sides):
    def run(step):
        def both(ins, outs, send_sems, recv_sems):
            i = o = k = 0
            for s in sides:
                ni, no = len(s.inputs), len(s.out_shapes)
                getattr(s, step)(ins[i:i + ni], outs[o:o + no], _SemaphoresFrom(send_sems, k),
                                 _SemaphoresFrom(recv_sems, k))
                i, o, k = i + ni, o + no, k + s.n_sems
        return both

    return _Side(sum([s.inputs for s in sides], []), sum([s.out_shapes for s in sides], []),
                 sum(s.n_sems for s in sides), run("start"), run("finish"))


def _share_out(together, sides):
    o = 0
    for s in sides:
        s.outputs = together.outputs[o:o + len(s.out_shapes)]
        o += len(s.out_shapes)


def _call(body, side, *, name, grid, in_specs, out_specs, out_shape, scratch_shapes, semantics, args):
    in_specs, out_specs, out_shape = list(in_specs), list(out_specs), list(out_shape)
    scratch_shapes = list(scratch_shapes)
    if side is None:
        return pl.pallas_call(body, name=name, grid=grid, in_specs=in_specs, out_specs=out_specs,
                              out_shape=out_shape, scratch_shapes=scratch_shapes,
                              compiler_params=_params(*semantics))(*args)
    ni, no, ns = len(in_specs), len(out_specs), len(scratch_shapes)
    si, so = len(side.inputs), len(side.out_shapes)

    def hosted(*refs):
        ins, s_ins = refs[:ni], refs[ni:ni + si]
        outs, s_outs = refs[ni + si:ni + si + no], refs[ni + si + no:ni + si + no + so]
        scratch = refs[ni + si + no + so:ni + si + no + so + ns]
        send_sems, recv_sems = refs[-2], refs[-1]
        first = pl.program_id(0) == 0
        last = pl.program_id(0) == grid[0] - 1
        for axis in range(1, len(grid)):
            first = jnp.logical_and(first, pl.program_id(axis) == 0)
            last = jnp.logical_and(last, pl.program_id(axis) == grid[axis] - 1)

        @pl.when(first)
        def _():
            side.start(s_ins, s_outs, send_sems, recv_sems)

        body(*ins, *outs, *scratch)

        @pl.when(last)
        def _():
            side.finish(s_ins, s_outs, send_sems, recv_sems)

    res = pl.pallas_call(
        hosted, name=name, grid=grid, in_specs=in_specs + [ANY] * si, out_specs=out_specs + [ANY] * so,
        out_shape=out_shape + side.out_shapes,
        scratch_shapes=scratch_shapes + [pltpu.SemaphoreType.DMA((side.n_sems,)),
                                         pltpu.SemaphoreType.DMA((side.n_sems,))],
        compiler_params=_params(*["arbitrary"] * len(grid)),
    )(*args, *side.inputs)
    side.outputs = list(res[no:])
    return list(res[:no])


def _matmul(a, b, *, mode, name, out_dtype=F32, addend=None, tm=1024, tn=512, tk_max=3072, side=None, second=None):
    m, k = a.shape
    if mode == "nn":
        k2, n = b.shape
    else:
        n, k2 = b.shape
    assert k == k2, (a.shape, b.shape, mode)
    tm, tn, tk = _tile(m, tm), _tile(n, tn), _tile(k, tk_max)
    nk = k // tk
    has_add = addend is not None
    n_rows = len(second[1]) if second else 0
    n_out = 2 if second else 1

    def body(*refs):
        a_ref, b_ref = refs[0], refs[1]
        add_ref = refs[2] if has_add else None
        row_refs = refs[2 + has_add:2 + has_add + n_rows]
        o_ref, acc_ref = refs[-1 - n_out], refs[-1]
        kk = pl.program_id(2)
        col_tile = pl.program_id(1)
        av = a_ref[...].astype(BF16)
        bv = b_ref[...].astype(BF16)
        part = _dot(av, bv) if mode == "nn" else _dot_nt(av, bv)

        @pl.when(kk == 0)
        def _():
            acc_ref[...] = part

        @pl.when(kk > 0)
        def _():
            acc_ref[...] += part

        @pl.when(kk == nk - 1)
        def _():
            res = acc_ref[...]
            if has_add:
                res = res + add_ref[...]
            o_ref[...] = res.astype(out_dtype)
            if second:
                refs[-2][...] = second[0](res, col_tile, *row_refs).astype(second[2])

    a_spec = pl.BlockSpec((tm, tk), lambda i, j, kk: (i, kk))
    if mode == "nn":
        b_spec = pl.BlockSpec((tk, tn), lambda i, j, kk: (kk, j))
    else:
        b_spec = pl.BlockSpec((tn, tk), lambda i, j, kk: (j, kk))
    tile = pl.BlockSpec((tm, tn), lambda i, j, kk: (i, j))
    in_specs = [a_spec, b_spec]
    args = [a, b]
    if has_add:
        in_specs.append(tile)
        args.append(addend)
    if second:
        in_specs += [pl.BlockSpec((1, tn), lambda i, j, kk: (0, j))] * n_rows
        args += list(second[1])
    outs = _call(
        body, side, name=name, grid=(m // tm, n // tn, nk),
        in_specs=in_specs, out_specs=[tile] * n_out,
        out_shape=[jax.ShapeDtypeStruct((m, n), out_dtype)] + ([jax.ShapeDtypeStruct((m, n), second[2])] if second
                                                                 else []),
        scratch_shapes=[pltpu.VMEM((tm, tn), F32)],
        semantics=("parallel", "parallel", "arbitrary"), args=args,
    )
    return outs if second else outs[0]


def _matmul_tn(a, b, *, name, tm=1408, tn=512, tk=1024):
    t, m = a.shape
    t2, n = b.shape
    assert t == t2
    tm, tn, tk = _tile(m, tm), _tile(n, tn), _tile(t, tk)

    def body(a_ref, b_ref, o_ref):
        part = _dot_tn(a_ref[...].astype(BF16), b_ref[...].astype(BF16))

        @pl.when(pl.program_id(2) == 0)
        def _():
            o_ref[...] = part

        @pl.when(pl.program_id(2) > 0)
        def _():
            o_ref[...] += part

    return pl.pallas_call(
        body, name=name, grid=(m // tm, n // tn, t // tk),
        in_specs=[pl.BlockSpec((tk, tm), lambda i, j, kk: (kk, i)),
                  pl.BlockSpec((tk, tn), lambda i, j, kk: (kk, j))],
        out_specs=pl.BlockSpec((tm, tn), lambda i, j, kk: (i, j)),
        out_shape=jax.ShapeDtypeStruct((m, n), F32),
        compiler_params=_params("parallel", "parallel", "arbitrary"),
    )(a, b)


def _rmsnorm_rows(tile, j, gain_ref):
    r = lax.rsqrt(jnp.mean(tile * tile, axis=-1, keepdims=True) + NORM_EPS)
    return tile * r * gain_ref[...]


def _rmsnorm_fwd(x, gain, *, name):
    t, d = x.shape
    tm = _tile(t, 512)

    def body(x_ref, g_ref, o_ref):
        xv = x_ref[...]
        r = lax.rsqrt(jnp.mean(xv * xv, axis=-1, keepdims=True) + NORM_EPS)
        o_ref[...] = (xv * r * g_ref[...]).astype(BF16)

    return pl.pallas_call(
        body, name=name, grid=(t // tm,),
        in_specs=[pl.BlockSpec((tm, d), lambda i: (i, 0)), pl.BlockSpec((1, d), lambda i: (0, 0))],
        out_specs=pl.BlockSpec((tm, d), lambda i: (i, 0)),
        out_shape=jax.ShapeDtypeStruct((t, d), BF16),
        compiler_params=_params("parallel"),
    )(x, gain)


def _matmul_rmsnorm_bwd(a, b, addend, x, gain, dres, *, name, side=None, tm=512, tk_max=3072):
    m, k = a.shape
    d = b.shape[1]
    tm, tk = _tile(m, tm), _tile(k, tk_max)
    nk = k // tk

    def body(a_ref, b_ref, *rest):
        add_ref = rest[0] if addend is not None else None
        x_ref, g_ref, dres_ref, dx_ref, dg_ref, acc_ref = rest[-6:]
        i, kk = pl.program_id(0), pl.program_id(1)
        part = _dot(a_ref[...].astype(BF16), b_ref[...].astype(BF16))

        @pl.when(kk == 0)
        def _():
            acc_ref[...] = part

        @pl.when(kk > 0)
        def _():
            acc_ref[...] += part

        @pl.when(kk == nk - 1)
        def _():
            dyv = acc_ref[...] if addend is None else acc_ref[...] + add_ref[...]
            xv = x_ref[...]
            r = lax.rsqrt(jnp.mean(xv * xv, axis=-1, keepdims=True) + NORM_EPS)
            xh = xv * r
            dxh = dyv * g_ref[...]
            mean = jnp.mean(dxh * xh, axis=-1, keepdims=True)
            dx_ref[...] = dres_ref[...] + r * (dxh - xh * mean)
            gain_part = jnp.sum(dyv * xh, axis=0, keepdims=True)

            @pl.when(i == 0)
            def _():
                dg_ref[...] = gain_part

            @pl.when(i > 0)
            def _():
                dg_ref[...] += gain_part

    row = pl.BlockSpec((tm, d), lambda i, kk: (i, 0))
    vec = pl.BlockSpec((1, d), lambda i, kk: (0, 0))
    return _call(
        body, side, name=name, grid=(m // tm, nk),
        in_specs=[pl.BlockSpec((tm, tk), lambda i, kk: (i, kk)), pl.BlockSpec((tk, d), lambda i, kk: (kk, 0))]
        + ([row] if addend is not None else []) + [row, vec, row],
        out_specs=[row, vec],
        out_shape=[jax.ShapeDtypeStruct((m, d), F32), jax.ShapeDtypeStruct((1, d), F32)],
        scratch_shapes=[pltpu.VMEM((tm, d), F32)],
        semantics=("arbitrary", "arbitrary"),
        args=(a, b) + ((addend,) if addend is not None else ()) + (x, gain, dres),
    )


def _swiglu_fwd(h, w_gate_t, w_up_t, *, name, side=None):
    t, d = h.shape
    f = w_gate_t.shape[0]
    tm, tn = _tile(t, 1024), _tile(f, 256)

    def body(h_ref, wg_ref, wu_ref, g_ref, u_ref, a_ref):
        hv = h_ref[...]
        g = _dot_nt(hv, wg_ref[...])
        u = _dot_nt(hv, wu_ref[...])
        g_ref[...] = g.astype(BF16)
        u_ref[...] = u.astype(BF16)
        a_ref[...] = (g * _sigmoid(g) * u).astype(BF16)

    wspec = pl.BlockSpec((tn, d), lambda i, j: (j, 0))
    ospec = pl.BlockSpec((tm, tn), lambda i, j: (i, j))
    return _call(
        body, side, name=name, grid=(t // tm, f // tn),
        in_specs=[pl.BlockSpec((tm, d), lambda i, j: (i, 0)), wspec, wspec],
        out_specs=[ospec, ospec, ospec],
        out_shape=[jax.ShapeDtypeStruct((t, f), BF16), jax.ShapeDtypeStruct((t, f), BF16),
                   jax.ShapeDtypeStruct((t, f), BF16)],
        scratch_shapes=[], semantics=("parallel", "parallel"), args=(h, w_gate_t, w_up_t),
    )


def _swiglu_bwd(dx, w_down, g, u, *, name, side=None):
    t, d = dx.shape
    f = w_down.shape[0]
    tm, tn = _tile(t, 1024), _tile(f, 256)

    def body(dx_ref, wd_ref, g_ref, u_ref, dg_ref, du_ref):
        dact = _dot_nt(dx_ref[...].astype(BF16), wd_ref[...])
        gv, uv = g_ref[...].astype(F32), u_ref[...].astype(F32)
        sg = _sigmoid(gv)
        dg_ref[...] = (dact * uv * sg * (1.0 + gv * (1.0 - sg))).astype(BF16)
        du_ref[...] = (dact * gv * sg).astype(BF16)

    ospec = pl.BlockSpec((tm, tn), lambda i, j: (i, j))
    return _call(
        body, side, name=name, grid=(t // tm, f // tn),
        in_specs=[pl.BlockSpec((tm, d), lambda i, j: (i, 0)), pl.BlockSpec((tn, d), lambda i, j: (j, 0)),
                  ospec, ospec],
        out_specs=[ospec, ospec],
        out_shape=[jax.ShapeDtypeStruct((t, f), BF16), jax.ShapeDtypeStruct((t, f), BF16)],
        scratch_shapes=[], semantics=("parallel", "parallel"), args=(dx, w_down, g, u),
    )


def _ple_fwd(x, p, w_gate, w_proj_t, *, name, next_gain=None, target=None):
    t, d = x.shape
    e = p.shape[1]
    tm = _tile(t, 512)
    steps = t // tm

    def body(x_ref, p_ref, wg_ref, wp_ref, *rest):
        xv = x_ref[...]
        s = _dot(xv.astype(BF16), wg_ref[...])
        ple = _dot_nt(p_ref[...].astype(BF16), wp_ref[...])
        y = xv + _sigmoid(s) * ple
        if target is None:
            gain_ref, y_ref, h_ref = rest
            y_ref[...] = y
            r = lax.rsqrt(jnp.mean(y * y, axis=-1, keepdims=True) + NORM_EPS)
            h_ref[...] = (y * r * gain_ref[...]).astype(BF16)
        else:
            t_ref, dy_ref, l_ref, acc_ref = rest
            err = y - t_ref[...]
            dy_ref[...] = err * (1.0 / d)
            part = jnp.sum(err * err, axis=0, keepdims=True)

            @pl.when(pl.program_id(0) == 0)
            def _():
                acc_ref[...] = part

            @pl.when(pl.program_id(0) > 0)
            def _():
                acc_ref[...] += part

            @pl.when(pl.program_id(0) == steps - 1)
            def _():
                l_ref[...] = jnp.full((1, LANES), (0.5 / d), F32) * jnp.sum(acc_ref[...])

    row = pl.BlockSpec((tm, d), lambda i: (i, 0))
    fixed = lambda shape: pl.BlockSpec(shape, lambda i: (0, 0))
    in_specs = [row, pl.BlockSpec((tm, e), lambda i: (i, 0)), fixed((d, d)), fixed((d, e))]
    if target is None:
        return pl.pallas_call(
            body, name=name, grid=(steps,), in_specs=in_specs + [fixed((1, d))], out_specs=[row, row],
            out_shape=[jax.ShapeDtypeStruct((t, d), F32), jax.ShapeDtypeStruct((t, d), BF16)],
            compiler_params=_params("parallel"),
        )(x, p, w_gate, w_proj_t, next_gain)
    return pl.pallas_call(
        body, name=name, grid=(steps,), in_specs=in_specs + [row], out_specs=[row, fixed((1, LANES))],
        out_shape=[jax.ShapeDtypeStruct((t, d), F32), jax.ShapeDtypeStruct((1, LANES), F32)],
        scratch_shapes=[pltpu.VMEM((1, d), F32)],
        compiler_params=_params("arbitrary"),
    )(x, p, w_gate, w_proj_t, target)


def _ple_bwd(x, p, w_gate, w_proj_t, dout, *, name):
    t, d = x.shape
    e = p.shape[1]
    tm = _tile(t, 512)

    def body(x_ref, p_ref, wg_ref, wp_ref, do_ref, ds_ref, dple_ref, dx_ref):
        wg = wg_ref[...]
        s = _dot(x_ref[...].astype(BF16), wg)
        ple = _dot_nt(p_ref[...].astype(BF16), wp_ref[...])
        gate = _sigmoid(s)
        dov = do_ref[...]
        dple_ref[...] = (dov * gate).astype(BF16)
        ds = (dov * ple * gate * (1.0 - gate)).astype(BF16)
        ds_ref[...] = ds
        dx_ref[...] = dov + _dot_nt(ds, wg)

    row = pl.BlockSpec((tm, d), lambda i: (i, 0))
    fixed = lambda shape: pl.BlockSpec(shape, lambda i: (0, 0))
    return pl.pallas_call(
        body, name=name, grid=(t // tm,),
        in_specs=[row, pl.BlockSpec((tm, e), lambda i: (i, 0)), fixed((d, d)), fixed((d, e)), row],
        out_specs=[row, row, row],
        out_shape=[jax.ShapeDtypeStruct((t, d), BF16), jax.ShapeDtypeStruct((t, d), BF16),
                   jax.ShapeDtypeStruct((t, d), F32)],
        compiler_params=_params("parallel"),
    )(x, p, w_gate, w_proj_t, dout)


CONV_TIME_TILE = 256
CONV_HALO = 8


def _conv_taps(ext, w):
    acc = ext[CONV_HALO:, :] * w[CONV_WIDTH - 1:CONV_WIDTH, :]
    shifted = [ext[CONV_HALO:, :]]
    for j in range(1, CONV_WIDTH):
        sh = pltpu.roll(ext, j, 0)[CONV_HALO:, :]
        shifted.append(sh)
        acc = acc + sh * w[CONV_WIDTH - 1 - j:CONV_WIDTH - j, :]
    return acc, shifted


def _conv_fwd(u, w, b, side=None):
    t, c = u.shape
    tc = _tile(c, 256)
    tt = CONV_TIME_TILE

    def body(u_ref, w_ref, b_ref, o_ref):
        wv, bv = w_ref[...], b_ref[...]

        def tile(start, ext):
            pre = _conv_taps(ext, wv)[0] + bv
            o_ref[pl.ds(start, tt), :] = pre * _sigmoid(pre)

        tile(0, jnp.concatenate([jnp.zeros((CONV_HALO, tc), F32), u_ref[0:tt, :]], axis=0))

        def loop(i, carry):
            start = pl.multiple_of(i * tt, tt)
            tile(start, u_ref[pl.ds(start - CONV_HALO, tt + CONV_HALO), :])
            return carry

        lax.fori_loop(1, t // tt, loop, 0)

    col = pl.BlockSpec((t, tc), lambda j: (0, j))
    return _call(
        body, side, name="conv_fwd", grid=(c // tc,),
        in_specs=[col, pl.BlockSpec((CONV_WIDTH, tc), lambda j: (0, j)), pl.BlockSpec((1, tc), lambda j: (0, j))],
        out_specs=[col], out_shape=[jax.ShapeDtypeStruct((t, c), F32)],
        scratch_shapes=[], semantics=("parallel",), args=(u, w, b),
    )[0]


def _conv_bwd(u, w, b, dact, side=None):
    t, c = u.shape
    tc = _tile(c, 256)
    tt = CONV_TIME_TILE

    def body(u_ref, w_ref, b_ref, da_ref, du_ref, dw_ref, db_ref, dpre_ref):
        wv, bv = w_ref[...], b_ref[...]

        def tile(start, ext, sums):
            acc, shifted = _conv_taps(ext, wv)
            pre = acc + bv
            sg = _sigmoid(pre)
            dpre = da_ref[pl.ds(start, tt), :] * (sg * (1.0 + pre * (1.0 - sg)))
            dpre_ref[pl.ds(start, tt), :] = dpre
            new = [sums[0] + jnp.sum(dpre, axis=0, keepdims=True)]
            for j in range(CONV_WIDTH):
                new.append(sums[1 + j] + jnp.sum(dpre * shifted[j], axis=0, keepdims=True))
            return tuple(new)

        zero = jnp.zeros((1, tc), F32)
        sums = tile(0, jnp.concatenate([jnp.zeros((CONV_HALO, tc), F32), u_ref[0:tt, :]], axis=0),
                    (zero,) * (1 + CONV_WIDTH))

        def loop(i, sums):
            start = pl.multiple_of(i * tt, tt)
            return tile(start, u_ref[pl.ds(start - CONV_HALO, tt + CONV_HALO), :], sums)

        sums = lax.fori_loop(1, t // tt, loop, sums)
        db_ref[...] = sums[0]
        dw_ref[...] = jnp.concatenate([sums[1 + (CONV_WIDTH - 1 - k)] for k in range(CONV_WIDTH)], axis=0)
        dpre_ref[pl.ds(t, CONV_HALO), :] = jnp.zeros((CONV_HALO, tc), F32)

        def loop2(i, carry):
            start = pl.multiple_of(i * tt, tt)
            ext = dpre_ref[pl.ds(start, tt + CONV_HALO), :]
            acc = ext[0:tt, :] * wv[CONV_WIDTH - 1:CONV_WIDTH, :]
            for j in range(1, CONV_WIDTH):
                acc = acc + pltpu.roll(ext, tt + CONV_HALO - j, 0)[0:tt, :] * wv[CONV_WIDTH - 1 - j:CONV_WIDTH - j, :]
            du_ref[pl.ds(start, tt), :] = acc.astype(BF16)
            return carry

        lax.fori_loop(0, t // tt, loop2, 0)

    col = pl.BlockSpec((t, tc), lambda j: (0, j))
    return _call(
        body, side, name="conv_bwd", grid=(c // tc,),
        in_specs=[col, pl.BlockSpec((CONV_WIDTH, tc), lambda j: (0, j)), pl.BlockSpec((1, tc), lambda j: (0, j)), col],
        out_specs=[col, pl.BlockSpec((CONV_WIDTH, tc), lambda j: (0, j)), pl.BlockSpec((1, tc), lambda j: (0, j))],
        out_shape=[jax.ShapeDtypeStruct((t, c), BF16), jax.ShapeDtypeStruct((CONV_WIDTH, c), F32),
                   jax.ShapeDtypeStruct((1, c), F32)],
        scratch_shapes=[pltpu.VMEM((t + CONV_HALO, tc), F32)],
        semantics=("parallel",), args=(u, w, b, dact),
    )


def _softplus(v):
    e = jnp.exp(-jnp.abs(v))
    w = 1.0 + e
    log1p = jnp.where(w == 1.0, e, jnp.log(w) * (e / jnp.where(w == 1.0, 1.0, w - 1.0)))
    return jnp.maximum(v, 0.0) + log1p


def _split3(z):
    hi = z.astype(BF16)
    rest = z - hi.astype(F32)
    mid = rest.astype(BF16)
    return hi, mid, (rest - mid.astype(F32)).astype(BF16)


def _select_dot(z, ones):
    return sum(_dot(term, ones) for term in _split3(z))


def _ssd_prep_fwd(dt_raw, dt_bias, a_log):
    t = dt_raw.shape[0]
    cl = SSD_CHUNK

    def body(r_ref, b_ref, al_ref, acs_ref, dt_rep_ref, acs_rep_ref):
        dt = _softplus(r_ref[...] + b_ref[...])
        adt = dt * (-jnp.exp(al_ref[...]))
        li = lax.broadcasted_iota(jnp.int32, (cl, cl), 0)
        si = lax.broadcasted_iota(jnp.int32, (cl, cl), 1)
        tri = (si <= li).astype(F32)
        acs = jnp.dot(tri, adt, preferred_element_type=F32, precision=HIGHEST)
        acs_ref[...] = acs
        head = lax.broadcasted_iota(jnp.int32, (LANES, D_INNER), 0)
        chan = lax.broadcasted_iota(jnp.int32, (LANES, D_INNER), 1) // SSM_HEAD_DIM
        spread = (head == chan).astype(BF16)
        dt_rep_ref[...] = _select_dot(dt, spread)
        acs_rep_ref[...] = _select_dot(acs, spread)

    row = pl.BlockSpec((cl, LANES), lambda i: (i, 0))
    wide = pl.BlockSpec((cl, D_INNER), lambda i: (i, 0))
    vec = pl.BlockSpec((1, LANES), lambda i: (0, 0))
    return pl.pallas_call(
        body, name="ssd_prep_fwd", grid=(t // cl,),
        in_specs=[row, vec, vec], out_specs=[row, wide, wide],
        out_shape=[jax.ShapeDtypeStruct((t, LANES), F32), jax.ShapeDtypeStruct((t, D_INNER), F32),
                   jax.ShapeDtypeStruct((t, D_INNER), F32)],
        compiler_params=_params("parallel"),
    )(dt_raw, dt_bias, a_log)


def _ssd_prep_bwd(dt_raw, dt_bias, ddt):
    t = dt_raw.shape[0]
    tm = _tile(t, 512)

    def body(r_ref, b_ref, d_ref, o_ref, db_ref):
        g = d_ref[...] * _sigmoid(r_ref[...] + b_ref[...])
        o_ref[...] = g.astype(BF16)
        part = jnp.sum(g, axis=0, keepdims=True)

        @pl.when(pl.program_id(0) == 0)
        def _():
            db_ref[...] = part

        @pl.when(pl.program_id(0) > 0)
        def _():
            db_ref[...] += part

    row = pl.BlockSpec((tm, LANES), lambda i: (i, 0))
    vec = pl.BlockSpec((1, LANES), lambda i: (0, 0))
    return pl.pallas_call(
        body, name="ssd_prep_bwd", grid=(t // tm,),
        in_specs=[row, vec, row], out_specs=[row, vec],
        out_shape=[jax.ShapeDtypeStruct((t, LANES), BF16), jax.ShapeDtypeStruct((1, LANES), F32)],
        compiler_params=_params("arbitrary"),
    )(dt_raw, dt_bias, ddt)


GROUP_W = D_INNER // SSM_GROUPS
PAIRS_PER_GROUP = GROUP_W // LANES


def _head_cols(acs_pair, lt64):
    rolled = pltpu.roll(acs_pair, ATT_HEAD_DIM, 1)
    return jnp.where(lt64, acs_pair, rolled), jnp.where(lt64, rolled, acs_pair)


def _ssd_fwd(xbc, dt_rep, acs_rep, acs_t, dskip_rep, z, norm_w, side=None):
    t = xbc.shape[0]
    cl = SSD_CHUNK
    nc = t // cl

    def body(xbc_ref, dt_ref, acs_ref, acst_ref, dskip_ref, z_ref, nw_ref, y_ref, hin_ref, yn_ref, state_ref):
        @pl.when(pl.program_id(0) == 0)
        def _():
            state_ref[...] = jnp.zeros_like(state_ref)

        lt64 = _lane_lt64(cl)
        li = lax.broadcasted_iota(jnp.int32, (cl, cl), 0)
        si = lax.broadcasted_iota(jnp.int32, (cl, cl), 1)
        causal = li >= si
        hin_ref[...] = state_ref[...]
        for g in range(SSM_GROUPS):
            gsl = slice(g * GROUP_W, (g + 1) * GROUP_W)
            xg = xbc_ref[:, gsl]
            bg = xbc_ref[:, D_INNER + g * SSM_STATE:D_INNER + (g + 1) * SSM_STATE]
            cg = xbc_ref[:, D_INNER + SSM_GROUPS * SSM_STATE + g * SSM_STATE:
                         D_INNER + SSM_GROUPS * SSM_STATE + (g + 1) * SSM_STATE]
            acs = acs_ref[:, gsl]
            xdt = xg * dt_ref[:, gsl]
            atot = acs[cl - 1:cl, :]
            hin = state_ref[:, gsl]
            cgb = cg.astype(BF16)
            gmat = _dot_nt(cgb, bg.astype(BF16))
            yoff = _dot(cgb, hin.astype(BF16)) * jnp.exp(acs)
            snew = _dot(bg.T.astype(BF16), (xdt * jnp.exp(atot - acs)).astype(BF16))
            state_ref[:, gsl] = hin * jnp.exp(atot) + snew
            xdtb = xdt.astype(BF16)
            for pr in range(PAIRS_PER_GROUP):
                psl = slice(pr * LANES, (pr + 1) * LANES)
                cols = _head_cols(acs[:, psl], lt64)
                xp = xdtb[:, psl]
                ys = []
                for hh in range(2):
                    h = (g * PAIRS_PER_GROUP + pr) * 2 + hh
                    seg = cols[hh] - acst_ref[h:h + 1, :]
                    lm = jnp.exp(jnp.where(causal, seg, NEG_BIG))
                    ys.append(_dot((gmat * lm).astype(BF16), xp))
                ydiag = jnp.where(lt64, ys[0], ys[1])
                osl = slice(g * GROUP_W + pr * LANES, g * GROUP_W + (pr + 1) * LANES)
                y_ref[:, osl] = ydiag + yoff[:, psl] + xg[:, psl] * dskip_ref[:, osl]
            zv = z_ref[:, gsl]
            v = y_ref[:, gsl] * (zv * _sigmoid(zv))
            r = lax.rsqrt(jnp.mean(v * v, axis=-1, keepdims=True) + NORM_EPS)
            yn_ref[:, gsl] = (v * r * nw_ref[:, gsl]).astype(BF16)

    row = lambda w: pl.BlockSpec((cl, w), lambda c: (c, 0))
    vec = pl.BlockSpec((1, D_INNER), lambda c: (0, 0))
    return _call(
        body, side, name="ssd_fwd", grid=(nc,),
        in_specs=[row(CONV_DIM), row(D_INNER), row(D_INNER),
                  pl.BlockSpec((SSM_HEADS, cl), lambda c: (0, c)), vec, row(D_INNER), vec],
        out_specs=[row(D_INNER), pl.BlockSpec((None, SSM_STATE, D_INNER), lambda c: (c, 0, 0)), row(D_INNER)],
        out_shape=[jax.ShapeDtypeStruct((t, D_INNER), F32), jax.ShapeDtypeStruct((nc, SSM_STATE, D_INNER), F32),
                   jax.ShapeDtypeStruct((t, D_INNER), BF16)],
        scratch_shapes=[pltpu.VMEM((SSM_STATE, D_INNER), F32)],
        semantics=("arbitrary",), args=(xbc, dt_rep, acs_rep, acs_t, dskip_rep, z, norm_w),
    )


def _ssd_bwd(xbc, dt_rep, acs_rep, acs_t, dskip_rep, a_rep, hin_all, dy, side=None):
    t = xbc.shape[0]
    cl = SSD_CHUNK
    nc = t // cl

    def body(xbc_ref, dt_ref, acs_ref, acst_ref, dskip_ref, a_ref, hin_ref, dy_ref,
             dxbc_ref, ddt_ref, da_ref, dds_ref, dstate_ref, dacs_ref, dxs_ref):
        step = pl.program_id(0)

        @pl.when(step == 0)
        def _():
            dstate_ref[...] = jnp.zeros_like(dstate_ref)
            da_ref[...] = jnp.zeros_like(da_ref)
            dds_ref[...] = jnp.zeros_like(dds_ref)

        bd = _head_block_diag()
        lt64 = _lane_lt64(cl)
        li = lax.broadcasted_iota(jnp.int32, (cl, cl), 0)
        si = lax.broadcasted_iota(jnp.int32, (cl, cl), 1)
        lower = li >= si
        upper = si >= li
        last_row = lax.broadcasted_iota(jnp.int32, (cl, GROUP_W), 0) == cl - 1
        for g in range(SSM_GROUPS):
            gsl = slice(g * GROUP_W, (g + 1) * GROUP_W)
            bsl = slice(D_INNER + g * SSM_STATE, D_INNER + (g + 1) * SSM_STATE)
            csl = slice(D_INNER + SSM_GROUPS * SSM_STATE + g * SSM_STATE,
                        D_INNER + SSM_GROUPS * SSM_STATE + (g + 1) * SSM_STATE)
            xg = xbc_ref[:, gsl]
            bg = xbc_ref[:, bsl]
            cg = xbc_ref[:, csl]
            bgb, cgb = bg.astype(BF16), cg.astype(BF16)
            acs = acs_ref[:, gsl]
            xdt = xg * dt_ref[:, gsl]
            atot = acs[cl - 1:cl, :]
            eg = jnp.exp(acs)
            dk = jnp.exp(atot - acs)
            etot = jnp.exp(atot)
            hin = hin_ref[:, gsl]
            hinb = hin.astype(BF16)
            dh = dstate_ref[:, gsl]
            dhb = dh.astype(BF16)
            dyg = dy_ref[:, gsl]

            gmat = _dot_nt(cgb, bgb)
            gmat_t = _dot_nt(bgb, cgb)
            ch = _dot(cgb, hinb)
            dacs = _head_sums(dyg * ch * eg, bd)
            dye = (dyg * eg).astype(BF16)
            dc = _dot_nt(dye, hinb)
            dhin = _dot(cg.T.astype(BF16), dye)
            bdh = _dot(bgb, dhb)
            dxs = bdh * dk
            xdk = xdt * dk
            db = _dot_nt(xdk.astype(BF16), dhb)
            ddk = _head_sums(bdh * xdk, bd)
            dacs = dacs - ddk
            datot = jnp.sum(ddk, axis=0, keepdims=True) + etot * _head_sums(
                jnp.sum(dh * hin, axis=0, keepdims=True), bd)
            dacs = dacs + jnp.where(last_row, datot, 0.0)
            dstate_ref[:, gsl] = dh * etot + dhin

            xdtb = xdt.astype(BF16)
            dgsum = jnp.zeros((cl, cl), F32)
            dgsum_t = jnp.zeros((cl, cl), F32)
            for pr in range(PAIRS_PER_GROUP):
                psl = slice(pr * LANES, (pr + 1) * LANES)
                cols = _head_cols(acs[:, psl], lt64)
                xp = xdtb[:, psl]
                dyp = dyg[:, psl].astype(BF16)
                dx1, dac = [], []
                for hh in range(2):
                    h = (g * PAIRS_PER_GROUP + pr) * 2 + hh
                    mine = lt64 if hh == 0 else jnp.logical_not(lt64)
                    row = acst_ref[h:h + 1, :]
                    lm = jnp.exp(jnp.where(lower, cols[hh] - row, NEG_BIG))
                    lm_t = jnp.exp(jnp.where(upper, row - cols[hh], NEG_BIG))
                    dyh = jnp.where(mine, dyp, jnp.zeros_like(dyp))
                    xh = jnp.where(mine, xp, jnp.zeros_like(xp))
                    dm = _dot_nt(dyh, xp)
                    dm_t = _dot_nt(xh, dyp)
                    m_t = gmat_t * lm_t
                    dx1.append(_dot(m_t.astype(BF16), dyp))
                    w = dm * (gmat * lm)
                    w_t = dm_t * m_t
                    dac.append(jnp.sum(w, axis=1, keepdims=True) - jnp.sum(w_t, axis=1, keepdims=True))
                    dgsum = dgsum + dm * lm
                    dgsum_t = dgsum_t + dm_t * lm_t
                osl = slice(g * GROUP_W + pr * LANES, g * GROUP_W + (pr + 1) * LANES)
                dxs_ref[:, osl] = dxs[:, psl] + jnp.where(lt64, dx1[0], dx1[1])
                dacs_ref[:, osl] = dacs[:, psl] + jnp.where(lt64, jnp.broadcast_to(dac[0], (cl, LANES)),
                                                             jnp.broadcast_to(dac[1], (cl, LANES)))
            dxbc_ref[:, csl] = dc + _dot(dgsum.astype(BF16), bgb)
            dxbc_ref[:, bsl] = db + _dot(dgsum_t.astype(BF16), cgb)

        dadt = _split_dot(upper.astype(BF16), dacs_ref[...])
        xall = xbc_ref[:, 0:D_INNER]
        dtall = dt_ref[...]
        dxsall = dxs_ref[...]
        dyall = dy_ref[...]
        ddt_rep = dadt * a_ref[...] + _head_sums(dxsall * xall, bd)
        chan = lax.broadcasted_iota(jnp.int32, (D_INNER, LANES), 0)
        head = lax.broadcasted_iota(jnp.int32, (D_INNER, LANES), 1)
        ddt_ref[...] = _select_dot(ddt_rep, (chan == head * SSM_HEAD_DIM).astype(BF16))
        dxbc_ref[:, 0:D_INNER] = dxsall * dtall + dyall * dskip_ref[...]
        da_ref[...] += jnp.sum(dadt * dtall, axis=0, keepdims=True)
        dds_ref[...] += jnp.sum(dyall * xall, axis=0, keepdims=True)

        @pl.when(step == nc - 1)
        def _():
            dds_ref[...] = _head_sums(dds_ref[...], bd)

    row = lambda w: pl.BlockSpec((cl, w), lambda c: (nc - 1 - c, 0))
    vec = pl.BlockSpec((1, D_INNER), lambda c: (0, 0))
    return _call(
        body, side, name="ssd_bwd", grid=(nc,),
        in_specs=[row(CONV_DIM), row(D_INNER), row(D_INNER),
                  pl.BlockSpec((SSM_HEADS, cl), lambda c: (0, nc - 1 - c)), vec, vec,
                  pl.BlockSpec((None, SSM_STATE, D_INNER), lambda c: (nc - 1 - c, 0, 0)), row(D_INNER)],
        out_specs=[row(CONV_DIM), row(LANES), vec, vec],
        out_shape=[jax.ShapeDtypeStruct((t, CONV_DIM), F32), jax.ShapeDtypeStruct((t, LANES), F32),
                   jax.ShapeDtypeStruct((1, D_INNER), F32), jax.ShapeDtypeStruct((1, D_INNER), F32)],
        scratch_shapes=[pltpu.VMEM((SSM_STATE, D_INNER), F32), pltpu.VMEM((cl, D_INNER), F32),
                        pltpu.VMEM((cl, D_INNER), F32)],
        semantics=("arbitrary",), args=(xbc, dt_rep, acs_rep, acs_t, dskip_rep, a_rep, hin_all, dy),
    )


def _gate_norm_bwd(y, z, w, dout, side=None):
    t, c = y.shape
    tm = _tile(t, 256)

    def body(y_ref, z_ref, w_ref, do_ref, dy_ref, dz_ref, dw_ref):
        @pl.when(pl.program_id(0) == 0)
        def _():
            dw_ref[...] = jnp.zeros_like(dw_ref)

        for g in range(SSM_GROUPS):
            gsl = slice(g * GROUP_W, (g + 1) * GROUP_W)
            zv, yv, dov = z_ref[:, gsl], y_ref[:, gsl], do_ref[:, gsl]
            sg = _sigmoid(zv)
            sz = zv * sg
            v = yv * sz
            r = lax.rsqrt(jnp.mean(v * v, axis=-1, keepdims=True) + NORM_EPS)
            vh = v * r
            dvh = dov * w_ref[:, gsl]
            mean = jnp.mean(dvh * vh, axis=-1, keepdims=True)
            dv = r * (dvh - vh * mean)
            dy_ref[:, gsl] = dv * sz
            dz_ref[:, gsl] = (dv * yv * (sg * (1.0 + zv * (1.0 - sg)))).astype(BF16)
            dw_ref[:, gsl] += jnp.sum(dov * vh, axis=0, keepdims=True)

    row = pl.BlockSpec((tm, c), lambda i: (i, 0))
    vec = pl.BlockSpec((1, c), lambda i: (0, 0))
    return _call(
        body, side, name="gate_norm_bwd", grid=(t // tm,),
        in_specs=[row, row, vec, row], out_specs=[row, row, vec],
        out_shape=[jax.ShapeDtypeStruct((t, c), F32), jax.ShapeDtypeStruct((t, c), BF16),
                   jax.ShapeDtypeStruct((1, c), F32)],
        scratch_shapes=[], semantics=("arbitrary",), args=(y, z, w, dout),
    )


ATT_W = ATT_HEADS * ATT_HEAD_DIM
N_QKV_BLOCKS = 9
ATT_SCALE = 1.0 / math.sqrt(ATT_HEAD_DIM)


def _head_rmsnorm(x, gain, bd):
    ms = _head_sums(x * x, bd, terms=1) * (1.0 / ATT_HEAD_DIM)
    return x * lax.rsqrt(ms + NORM_EPS) * gain


def _class_rows(ref, blk, r, dil):
    span = ATT_BLOCK * dil
    sub = ref.at[pl.ds(pl.multiple_of(blk * span, span), span), :]
    return sub[...] if dil == 1 else sub[pl.ds(r, ATT_BLOCK, stride=dil), :]


def _store_class_rows(ref, blk, r, dil, val):
    span = ATT_BLOCK * dil
    sub = ref.at[pl.ds(pl.multiple_of(blk * span, span), span), :]
    if dil == 1:
        sub[...] = val
    else:
        sub[pl.ds(r, ATT_BLOCK, stride=dil), :] = val


PAIRS = ATT_HEADS // 2


def _pair_col(g, j):
    return lambda pair: (0, (g * 3 + j) * PAIRS + pair)


def _pair_slopes(pair):
    steps = jnp.full((1, 2 * ATT_BLOCK), 2 * pair + 1, jnp.int32).astype(F32)
    first = jnp.exp(steps * (-0.5 * math.log(2.0)))
    return first, first * (2.0 ** -0.5)


NORM_ROWS = 512


ROW_SLICES = 4
SLICE_ROWS = 2 * ATT_BLOCK // ROW_SLICES


def _fill_band_bias(bias_ref, pair, dil, transposed):
    bq = ATT_BLOCK
    a = lax.broadcasted_iota(jnp.int32, (2 * bq, 2 * bq), 0) % bq
    b = lax.broadcasted_iota(jnp.int32, (2 * bq, 2 * bq), 1)
    dist = (b - a) if transposed else (a + bq - b)
    in_band = (dist >= 0) & (dist <= bq)
    s0, s1 = _pair_slopes(pair)
    first_head = lax.broadcasted_iota(jnp.int32, (2 * bq, 2 * bq), 0) < bq
    bias = jnp.where(first_head, s0, s1) * (dist.astype(F32) * float(dil))
    inside = (b < bq) if transposed else (b >= bq)
    bias_ref[1] = jnp.where(in_band, bias, -NEG_BIG)
    bias_ref[0] = jnp.where(in_band & inside, bias, -NEG_BIG)


def _row_slices():
    return [slice(i * SLICE_ROWS, (i + 1) * SLICE_ROWS) for i in range(ROW_SLICES)]


def _stack_heads(tile):
    rows = lax.broadcasted_iota(jnp.int32, (2 * ATT_BLOCK, LANES), 0) < ATT_BLOCK
    lanes = lax.broadcasted_iota(jnp.int32, (2 * ATT_BLOCK, LANES), 1) < ATT_HEAD_DIM
    both = jnp.concatenate([tile, tile], axis=0)
    return jnp.where(rows == lanes, both, jnp.zeros_like(both))


def _unstack_heads(stacked, lt64):
    return jnp.where(lt64, stacked[:ATT_BLOCK], stacked[ATT_BLOCK:])


ITEMS_PER_PASS = 4


def _item_loop(nb, dil, work):
    if dil == 1:
        def trip(i, carry):
            work([(i * ITEMS_PER_PASS + b, 0) for b in range(ITEMS_PER_PASS)])
            return carry

        lax.fori_loop(0, nb // ITEMS_PER_PASS, trip, 0)
    else:
        def trip(n, carry):
            for r0 in range(0, dil, ITEMS_PER_PASS):
                work([(n, r0 + j) for j in range(ITEMS_PER_PASS)])
            return carry

        lax.fori_loop(0, nb, trip, 0)


def _qk_normalised(tile, j, gq_ref, gk_ref):
    kind = (j // (ATT_W // tile.shape[1])) % 3
    gain = jnp.where(kind == 0, gq_ref[...] * ATT_SCALE, gk_ref[...])
    return jnp.where(kind == 2, tile, _head_rmsnorm(tile, gain, _head_block_diag()))


def _attn_fwd(qkn, g, dil):
    t = qkn.shape[0]
    nb = t // dil // ATT_BLOCK
    bq = ATT_BLOCK

    def body(qn_ref, kn_ref, v_ref, o_ref, l_ref, bias_ref):
        _fill_band_bias(bias_ref, pl.program_id(0), dil, False)
        lt64 = _lane_lt64(bq)

        def work(items):
            scores, values, probs = [], [], []
            for n, r in items:
                prev = jnp.maximum(n - 1, 0)
                q2 = _stack_heads(_class_rows(qn_ref, n, r, dil).astype(BF16))
                kcat = jnp.concatenate([_class_rows(kn_ref, prev, r, dil), _class_rows(kn_ref, n, r, dil)],
                                       axis=0).astype(BF16)
                values.append(jnp.concatenate([_class_rows(v_ref, prev, r, dil), _class_rows(v_ref, n, r, dil)],
                                              axis=0).astype(BF16))
                scores.append(_dot_nt(q2, kcat))
            for (n, r), sc in zip(items, scores):
                bias = bias_ref.at[jnp.minimum(n, 1)]
                ps, inv, lses = [], [], []
                for rows in _row_slices():
                    s = sc[rows] - bias[rows, :]
                    m = jnp.max(s, axis=1, keepdims=True)
                    p = jnp.exp(s - m)
                    l = jnp.sum(p, axis=1, keepdims=True)
                    ps.append(p.astype(BF16))
                    inv.append(jnp.broadcast_to(1.0 / l, (SLICE_ROWS, LANES)))
                    lses.append(jnp.broadcast_to(m + jnp.log(l), (SLICE_ROWS, LANES)))
                probs.append((jnp.concatenate(ps, axis=0), jnp.concatenate(inv, axis=0)))
                _store_class_rows(l_ref, n, r, dil, _unstack_heads(jnp.concatenate(lses, axis=0), lt64))
            for (n, r), (p, inv), vcat in zip(items, probs, values):
                _store_class_rows(o_ref, n, r, dil, _unstack_heads(_dot(p, vcat) * inv, lt64))

        _item_loop(nb, dil, work)

    col = lambda j: pl.BlockSpec((t, LANES), _pair_col(g, j))
    out = pl.BlockSpec((t, LANES), lambda pair: (0, pair))
    return pl.pallas_call(
        body, name=f"attn_fwd_g{g}", grid=(PAIRS,),
        in_specs=[col(0), col(1), col(2)], out_specs=[out, out],
        out_shape=[jax.ShapeDtypeStruct((t, ATT_W), F32), jax.ShapeDtypeStruct((t, ATT_W), F32)],
        scratch_shapes=[pltpu.VMEM((2, 2 * bq, 2 * bq), F32)],
        compiler_params=_params("parallel"),
    )(qkn, qkn, qkn)


def _one_per_head(rep):
    chan = lax.broadcasted_iota(jnp.int32, (ATT_W, LANES), 0)
    head = lax.broadcasted_iota(jnp.int32, (ATT_W, LANES), 1)
    return _select_dot(rep, (chan == head * ATT_HEAD_DIM).astype(BF16))


def _attn_combine_fwd(outs, lses):
    t = outs[0].shape[0]
    tm = _tile(t, 256)

    def body(o0, o1, o2, l0, l1, l2, ob_ref, of_ref, lt_ref, lc_ref):
        a, b, c = l0[...], l1[...], l2[...]
        m = jnp.maximum(jnp.maximum(a, b), c)
        ea, eb, ec = jnp.exp(a - m), jnp.exp(b - m), jnp.exp(c - m)
        ssum = ea + eb + ec
        o = (ea * o0[...] + eb * o1[...] + ec * o2[...]) / ssum
        ob_ref[...] = o.astype(BF16)
        of_ref[...] = o
        lse = m + jnp.log(ssum)
        lt_ref[...] = lse
        lc_ref[...] = _one_per_head(lse)

    row = pl.BlockSpec((tm, ATT_W), lambda i: (i, 0))
    return pl.pallas_call(
        body, name="attn_combine_fwd", grid=(t // tm,),
        in_specs=[row] * 6, out_specs=[row] * 3 + [pl.BlockSpec((tm, LANES), lambda i: (i, 0))],
        out_shape=[jax.ShapeDtypeStruct((t, ATT_W), BF16), jax.ShapeDtypeStruct((t, ATT_W), F32),
                   jax.ShapeDtypeStruct((t, ATT_W), F32), jax.ShapeDtypeStruct((t, LANES), F32)],
        compiler_params=_params("parallel"),
    )(*outs, *lses)


def _attn_combine_bwd(do, o):
    t = do.shape[0]
    tm = _tile(t, 256)

    def body(do_ref, o_ref, dl_ref, dc_ref):
        dl = _head_sums(do_ref[...] * o_ref[...], _head_block_diag())
        dl_ref[...] = dl
        dc_ref[...] = _one_per_head(dl)

    row = pl.BlockSpec((tm, ATT_W), lambda i: (i, 0))
    return pl.pallas_call(
        body, name="attn_combine_bwd", grid=(t // tm,),
        in_specs=[row, row], out_specs=[row, pl.BlockSpec((tm, LANES), lambda i: (i, 0))],
        out_shape=[jax.ShapeDtypeStruct((t, ATT_W), F32), jax.ShapeDtypeStruct((t, LANES), F32)],
        compiler_params=_params("parallel"),
    )(do, o)


def _head_rmsnorm_bwd(x_ref, dy_ref, gain_ref, dx_ref, dgain_ref):
    bd = _head_block_diag()
    gain = gain_ref[...]

    def step(i, acc):
        rows = pl.ds(pl.multiple_of(i * NORM_ROWS, NORM_ROWS), NORM_ROWS)
        x, dy = x_ref[rows, :], dy_ref[rows, :]
        r = lax.rsqrt(_head_sums(x * x, bd, terms=1) * (1.0 / ATT_HEAD_DIM) + NORM_EPS)
        xh = x * r
        dxh = dy * gain
        mean = _head_sums(dxh * xh, bd, terms=1) * (1.0 / ATT_HEAD_DIM)
        dx_ref[rows, :] = (r * (dxh - xh * mean)).astype(BF16)
        return acc + jnp.sum(dy * xh, axis=0, keepdims=True)

    acc = lax.fori_loop(0, x_ref.shape[0] // NORM_ROWS, step, jnp.zeros((1, LANES), F32))
    dgain_ref[...] = jnp.broadcast_to(acc, dgain_ref.shape)


def _attn_bwd_dq(qkv, qkn, gq, do, l_rep, dl_rep, g, dil):
    t = qkv.shape[0]
    nb = t // dil // ATT_BLOCK
    bq = ATT_BLOCK

    def body(q_ref, qn_ref, kn_ref, v_ref, gq_ref, do_ref, l_ref, dl_ref, dx_ref, dgain_ref, bias_ref, dq_ref):
        _fill_band_bias(bias_ref, pl.program_id(0), dil, False)
        lt64 = _lane_lt64(bq)

        def per_row(tile):
            cols = _head_cols(tile, lt64)
            half = jnp.concatenate([cols[0], cols[1]], axis=0)
            return jnp.concatenate([half, half], axis=1)

        def work(items):
            products, keys, dscores = [], [], []
            for n, r in items:
                prev = jnp.maximum(n - 1, 0)
                q2 = _stack_heads(_class_rows(qn_ref, n, r, dil).astype(BF16))
                do2 = _stack_heads(_class_rows(do_ref, n, r, dil).astype(BF16))
                kcat = jnp.concatenate([_class_rows(kn_ref, prev, r, dil), _class_rows(kn_ref, n, r, dil)],
                                       axis=0).astype(BF16)
                vcat = jnp.concatenate([_class_rows(v_ref, prev, r, dil), _class_rows(v_ref, n, r, dil)],
                                       axis=0).astype(BF16)
                keys.append(kcat)
                products.append((_dot_nt(q2, kcat), _dot_nt(do2, vcat)))
            for (n, r), (scores, dps) in zip(items, products):
                bias = bias_ref.at[jnp.minimum(n, 1)]
                lse = per_row(_class_rows(l_ref, n, r, dil))
                dl = per_row(_class_rows(dl_ref, n, r, dil))
                dss = []
                for rows in _row_slices():
                    p = jnp.exp(scores[rows] - bias[rows, :] - lse[rows])
                    dss.append((p * (dps[rows] - dl[rows])).astype(BF16))
                dscores.append(jnp.concatenate(dss, axis=0))
            for (n, r), ds, kcat in zip(items, dscores, keys):
                _store_class_rows(dq_ref, n, r, dil, _unstack_heads(_dot(ds, kcat) * ATT_SCALE, lt64))

        _item_loop(nb, dil, work)
        _head_rmsnorm_bwd(q_ref, dq_ref, gq_ref, dx_ref, dgain_ref)

    col = lambda j: pl.BlockSpec((t, LANES), _pair_col(g, j))
    vec = pl.BlockSpec((1, LANES), lambda pair: (0, 0))
    tok = pl.BlockSpec((t, LANES), lambda pair: (0, pair))
    return pl.pallas_call(
        body, name=f"attn_bwd_dq_g{g}", grid=(PAIRS,),
        in_specs=[col(0), col(0), col(1), col(2), vec, tok, tok, tok],
        out_specs=[tok, pl.BlockSpec((None, 8, LANES), lambda pair: (pair, 0, 0))],
        out_shape=[jax.ShapeDtypeStruct((t, ATT_W), BF16), jax.ShapeDtypeStruct((PAIRS, 8, LANES), F32)],
        scratch_shapes=[pltpu.VMEM((2, 2 * bq, 2 * bq), F32), pltpu.VMEM((t, LANES), F32)],
        compiler_params=_params("parallel"),
    )(qkv, qkn, qkn, qkn, gq, do, l_rep, dl_rep)


def _attn_bwd_dkv(qkv, qkn, gk, do, l_row, dl_row, g, dil):
    t = qkv.shape[0]
    nb = t // dil // ATT_BLOCK
    bq = ATT_BLOCK

    def body(k_ref, qn_ref, kn_ref, v_ref, gk_ref, do_ref, l_ref, dl_ref, dkx_ref, dvx_ref, dgain_ref, bias_ref,
             dk_ref, dv_ref):
        _fill_band_bias(bias_ref, pl.program_id(0), dil, True)
        lt64 = _lane_lt64(bq)

        def per_query(ref, hh, lane_c, lane_n):
            return jnp.concatenate([ref[hh:hh + 1, pl.ds(lane_c, bq)], ref[hh:hh + 1, pl.ds(lane_n, bq)]], axis=1)

        def work(items):
            products, operands, weights = [], [], []
            for n, r in items:
                nxt = jnp.minimum(n + 1, nb - 1)
                k2 = _stack_heads(_class_rows(kn_ref, n, r, dil).astype(BF16))
                v2 = _stack_heads(_class_rows(v_ref, n, r, dil).astype(BF16))
                qcat = jnp.concatenate([_class_rows(qn_ref, n, r, dil), _class_rows(qn_ref, nxt, r, dil)],
                                       axis=0).astype(BF16)
                docat = jnp.concatenate([_class_rows(do_ref, n, r, dil), _class_rows(do_ref, nxt, r, dil)],
                                        axis=0).astype(BF16)
                operands.append((qcat, docat))
                products.append((_dot_nt(k2, qcat), _dot_nt(v2, docat)))
            for (n, r), (scores, dps) in zip(items, products):
                nxt = jnp.minimum(n + 1, nb - 1)
                bias = bias_ref.at[jnp.where(n == nb - 1, 0, 1)]
                lane_c = pl.multiple_of((r * nb + n) * bq, bq)
                lane_n = pl.multiple_of((r * nb + nxt) * bq, bq)
                lse = [per_query(l_ref, hh, lane_c, lane_n) for hh in range(2)]
                dl = [per_query(dl_ref, hh, lane_c, lane_n) for hh in range(2)]
                pts, dss = [], []
                for i, rows in enumerate(_row_slices()):
                    hh = i * SLICE_ROWS // bq
                    p_t = jnp.exp(scores[rows] - bias[rows, :] - lse[hh])
                    pts.append(p_t.astype(BF16))
                    dss.append((p_t * (dps[rows] - dl[hh])).astype(BF16))
                weights.append((jnp.concatenate(pts, axis=0), jnp.concatenate(dss, axis=0)))
            for (n, r), (p_t, ds_t), (qcat, docat) in zip(items, weights, operands):
                _store_class_rows(dv_ref, n, r, dil, _unstack_heads(_dot(p_t, docat), lt64))
                _store_class_rows(dk_ref, n, r, dil, _unstack_heads(_dot(ds_t, qcat), lt64))

        _item_loop(nb, dil, work)
        _head_rmsnorm_bwd(k_ref, dk_ref, gk_ref, dkx_ref, dgain_ref)

        def cast_rows(i, carry):
            rows = pl.ds(pl.multiple_of(i * NORM_ROWS, NORM_ROWS), NORM_ROWS)
            dvx_ref[rows, :] = dv_ref[rows, :].astype(BF16)
            return carry

        lax.fori_loop(0, t // NORM_ROWS, cast_rows, 0)

    col = lambda j: pl.BlockSpec((t, LANES), _pair_col(g, j))
    vec = pl.BlockSpec((1, LANES), lambda pair: (0, 0))
    tok = pl.BlockSpec((t, LANES), lambda pair: (0, pair))
    rows = pl.BlockSpec((None, 8, t), lambda pair: (pair, 0, 0))
    return pl.pallas_call(
        body, name=f"attn_bwd_dkv_g{g}", grid=(PAIRS,),
        in_specs=[col(1), col(0), col(1), col(2), vec, tok, rows, rows],
        out_specs=[tok, tok, pl.BlockSpec((None, 8, LANES), lambda pair: (pair, 0, 0))],
        out_shape=[jax.ShapeDtypeStruct((t, ATT_W), BF16), jax.ShapeDtypeStruct((t, ATT_W), BF16),
                   jax.ShapeDtypeStruct((PAIRS, 8, LANES), F32)],
        scratch_shapes=[pltpu.VMEM((2, 2 * bq, 2 * bq), F32), pltpu.VMEM((t, LANES), F32),
                        pltpu.VMEM((t, LANES), F32)],
        compiler_params=_params("parallel"),
    )(qkv, qkn, qkn, qkn, gk, do, l_row, dl_row)


def _rows_by_residue(one_per_head, dil):
    t = one_per_head.shape[0]
    per_head = one_per_head[:, :ATT_HEADS]
    rows = per_head.reshape(t // dil, dil, ATT_HEADS).transpose(2, 1, 0).reshape(PAIRS, 2, t)
    return jnp.pad(rows, ((0, 0), (0, 6), (0, 0)))


def _per_head(rep_row):
    return rep_row[0, ::SSM_HEAD_DIM]


def _rep_heads(v):
    return jnp.repeat(v, SSM_HEAD_DIM)[None, :]


def _pad_lanes(v):
    return jnp.pad(v, ((0, 0), (0, LANES - v.shape[1])))


class _NoOverlap:
    def side(self, host):
        return None

    def after(self, host):
        pass

    def begin_backward(self, grads):
        pass


def _hosted(plan, host, fn, *args, **kwargs):
    out = fn(*args, side=plan.side(host), **kwargs)
    plan.after(host)
    return out


def _ffn_ple_fwd(x1, h, p_i, prm, i, plan, next_gain=None, target=None):
    g, u, act = _hosted(plan, f"swiglu_fwd_{i}", _swiglu_fwd, h, prm["ffn_w_gate"][i], prm["ffn_w_up"][i],
                        name=f"swiglu_fwd_{i}")
    x2 = _hosted(plan, f"ffn_down_{i}", _matmul, act, prm["ffn_w_down"][i], mode="nn", addend=x1,
                 name=f"ffn_down_{i}")
    outs = _ple_fwd(x2, p_i, prm["ple_w_gate"][i], prm["ple_w_proj"][i], name=f"ple_fwd_{i}", next_gain=next_gain,
                    target=target)
    return outs, dict(x1=x1, h=h, g=g, u=u, act=act, x2=x2)


def _ffn_ple_bwd(dx3, p_i, prm, i, sv, grads, plan):
    ds, dple, dx2 = _ple_bwd(sv["x2"], p_i, prm["ple_w_gate"][i], prm["ple_w_proj"][i], dx3, name=f"ple_bwd_{i}")
    grads["ple_w_gate"][i] = _matmul_tn(sv["x2"], ds, name=f"d_ple_w_gate_{i}")
    grads["ple_w_proj"][i] = _matmul_tn(dple, p_i, name=f"d_ple_w_proj_{i}")
    grads["ffn_w_down"][i] = _matmul_tn(sv["act"], dx2, name=f"d_ffn_w_down_{i}")
    dg, du = _hosted(plan, f"swiglu_bwd_{i}", _swiglu_bwd, dx2, prm["ffn_w_down"][i], sv["g"], sv["u"],
                     name=f"swiglu_bwd_{i}")
    grads["ffn_w_gate"][i] = _matmul_tn(dg, sv["h"], name=f"d_ffn_w_gate_{i}")
    grads["ffn_w_up"][i] = _matmul_tn(du, sv["h"], name=f"d_ffn_w_up_{i}")
    dh = _matmul(dg, prm["ffn_w_gate"][i], mode="nn", name=f"ffn_dh_gate_{i}")
    dx1, dgain = _matmul_rmsnorm_bwd(du, prm["ffn_w_up"][i], dh, sv["x1"], prm["norm_ffn"][i:i + 1], dx2,
                                     name=f"ffn_dh_up_{i}")
    grads["norm_ffn"][i] = dgain[0]
    return dx1


def _mamba_fwd(x0, prm, plan):
    h = _rmsnorm_fwd(x0, prm["norm_mix"][0:1], name="mix_norm_fwd_0")
    z = _hosted(plan, "ssm_in_z", _matmul, h, prm["ssm_w_z"], mode="nt", name="ssm_in_z")
    xbc_pre = _hosted(plan, "ssm_in_xbc", _matmul, h, prm["ssm_w_xbc"], mode="nt", name="ssm_in_xbc")
    dt_raw = _matmul(h, prm["ssm_w_dt"], mode="nt", name="ssm_in_dt")
    xbc = _hosted(plan, "conv_fwd", _conv_fwd, xbc_pre, prm["ssm_conv_w"], prm["ssm_conv_b"])
    dt_bias = _pad_lanes(prm["ssm_dt_bias"])
    a_log = _pad_lanes(prm["ssm_a_log"])
    acs, dt_rep, acs_rep = _ssd_prep_fwd(dt_raw, dt_bias, a_log)
    acs_t = acs[:, :SSM_HEADS].T
    dskip_rep = _rep_heads(prm["ssm_d_skip"][0])
    y, hin_all, yn = _hosted(plan, "ssd_fwd", _ssd_fwd, xbc, dt_rep, acs_rep, acs_t, dskip_rep, z,
                             prm["ssm_norm_w"])
    x1, h_ffn = _matmul(yn, prm["ssm_w_out"], mode="nn", addend=x0, name="ssm_out", tm=512, tn=D_MODEL,
                        second=(_rmsnorm_rows, [prm["norm_ffn"][0:1]], BF16))
    sv = dict(x0=x0, h=h, z=z, xbc_pre=xbc_pre, dt_raw=dt_raw, xbc=xbc, dt_bias=dt_bias, dt_rep=dt_rep,
              acs_rep=acs_rep, acs_t=acs_t, dskip_rep=dskip_rep, y=y, hin_all=hin_all, yn=yn)
    return x1, h_ffn, sv


def _mamba_bwd(dx1, prm, sv, grads, plan):
    grads["ssm_w_out"] = _matmul_tn(sv["yn"], dx1, name="d_ssm_w_out")
    dyn = _matmul(dx1, prm["ssm_w_out"], mode="nt", name="ssm_out_dx")
    dy, dz, dnw = _hosted(plan, "gate_norm_bwd", _gate_norm_bwd, sv["y"], sv["z"], prm["ssm_norm_w"], dyn)
    grads["ssm_norm_w"] = dnw
    a_rep = _rep_heads(-jnp.exp(prm["ssm_a_log"][0]))
    dxbc, ddt, da_rep, dds_rep = _hosted(plan, "ssd_bwd", _ssd_bwd, sv["xbc"], sv["dt_rep"], sv["acs_rep"],
                                             sv["acs_t"], sv["dskip_rep"], a_rep, sv["hin_all"], dy)
    grads["ssm_d_skip"] = _per_head(dds_rep)[None, :]
    grads["ssm_a_log"] = (_per_head(da_rep) * _per_head(a_rep))[None, :]
    ddt_raw, dbias = _ssd_prep_bwd(sv["dt_raw"], sv["dt_bias"], ddt)
    grads["ssm_dt_bias"] = dbias[:, :SSM_HEADS]
    du, dcw, dcb = _hosted(plan, "conv_bwd", _conv_bwd, sv["xbc_pre"], prm["ssm_conv_w"], prm["ssm_conv_b"], dxbc)
    grads["ssm_conv_w"] = dcw
    grads["ssm_conv_b"] = dcb
    h = sv["h"]
    grads["ssm_w_in"] = jnp.concatenate(
        [_matmul_tn(dz, h, name="d_ssm_w_z"), _matmul_tn(du, h, name="d_ssm_w_xbc"),
         _matmul_tn(ddt_raw, h, name="d_ssm_w_dt")[:SSM_HEADS]], axis=0)
    dh = _hosted(plan, "ssm_dh_z", _matmul, dz, prm["ssm_w_z"], mode="nn", name="ssm_dh_z")
    dh = _hosted(plan, "ssm_dh_xbc", _matmul, du, prm["ssm_w_xbc"], mode="nn", addend=dh, name="ssm_dh_xbc")
    dx0, dgain = _hosted(plan, "ssm_dh_dt", _matmul_rmsnorm_bwd, ddt_raw, prm["ssm_w_dt"], dh, sv["x0"],
                         prm["norm_mix"][0:1], dx1, name="ssm_dh_dt")
    grads["norm_mix"][0] = dgain[0]
    return dx0


def _attn_mixer_fwd(x0, h, prm, plan):
    n_heads = N_QKV_BLOCKS * ATT_HEADS
    gq = jnp.tile(prm["att_q_norm"], (1, n_heads))
    gk = jnp.tile(prm["att_k_norm"], (1, n_heads))
    qkv, qkn = _hosted(plan, "att_qkv", _matmul, h, prm["att_w_qkv"], mode="nt", name="att_qkv",
                       second=(_qk_normalised, [gq, gk], F32))
    outs, lses = [], []
    for g, (window, dil) in enumerate(DIL_PATTERNS):
        o_g, l_g = _attn_fwd(qkn, g, dil)
        outs.append(o_g)
        lses.append(l_g)
    o_b, o_f, l_rep, l_one = _attn_combine_fwd(outs, lses)
    x1, h_ffn = _matmul(o_b, prm["att_w_o"], mode="nn", addend=x0, name="att_out", tm=512, tn=D_MODEL,
                        second=(_rmsnorm_rows, [prm["norm_ffn"][1:2]], BF16))
    sv = dict(x0=x0, h=h, qkv=qkv, qkn=qkn, gq2=gq[:, :LANES], gk2=gk[:, :LANES], o_b=o_b, o_f=o_f, l_rep=l_rep,
              l_one=l_one)
    return x1, h_ffn, sv


def _attn_mixer_bwd(dx1, prm, sv, grads, plan):
    grads["att_w_o"] = _matmul_tn(sv["o_b"], dx1, name="d_att_w_o")
    do = _hosted(plan, "att_out_dx", _matmul, dx1, prm["att_w_o"], mode="nt", name="att_out_dx")
    dl_rep, dl_one = _attn_combine_bwd(do, sv["o_f"])
    blocks, dgq, dgk = [], [], []
    for g, (window, dil) in enumerate(DIL_PATTERNS):
        dq, dgq_g = _attn_bwd_dq(sv["qkv"], sv["qkn"], sv["gq2"], do, sv["l_rep"], dl_rep, g, dil)
        dk, dv, dgk_g = _attn_bwd_dkv(sv["qkv"], sv["qkn"], sv["gk2"], do, _rows_by_residue(sv["l_one"], dil),
                                      _rows_by_residue(dl_one, dil), g, dil)
        blocks += [dq, dk, dv]
        dgq.append(dgq_g)
        dgk.append(dgk_g)
    dqkv = jnp.concatenate(blocks, axis=1)

    def fold(parts):
        return jnp.stack(parts)[:, :, 0].reshape(-1, ATT_HEAD_DIM).sum(axis=0)[None, :]

    grads["att_q_norm"] = fold(dgq)
    grads["att_k_norm"] = fold(dgk)
    grads["att_w_qkv"] = _matmul_tn(dqkv, sv["h"], name="d_att_w_qkv")
    dx0, dgain = _hosted(plan, "att_qkv_dx", _matmul_rmsnorm_bwd, dqkv, prm["att_w_qkv"], None, sv["x0"],
                         prm["norm_mix"][1:2], dx1, name="att_qkv_dx")
    grads["norm_mix"][1] = dgain[0]
    return dx0


def _local_step(x, p, target, prm, plan=None):
    plan = plan or _NoOverlap()
    grads = {k: [None, None] for k in ("norm_mix", "norm_ffn", "ffn_w_gate", "ffn_w_up", "ffn_w_down",
                                       "ple_w_proj", "ple_w_gate")}
    plan.begin_backward(grads)
    x1, h1, sv_m = _mamba_fwd(x, prm, plan)
    (x3, h3), sv_f0 = _ffn_ple_fwd(x1, h1, p[0], prm, 0, plan, next_gain=prm["norm_mix"][1:2])
    x4, h4, sv_a = _attn_mixer_fwd(x3, h3, prm, plan)
    (dy, loss_row), sv_f1 = _ffn_ple_fwd(x4, h4, p[1], prm, 1, plan, target=target)
    dx4 = _ffn_ple_bwd(dy, p[1], prm, 1, sv_f1, grads, plan)
    dx3 = _attn_mixer_bwd(dx4, prm, sv_a, grads, plan)
    dx1 = _ffn_ple_bwd(dx3, p[0], prm, 0, sv_f0, grads, plan)
    dx0 = _mamba_bwd(dx1, prm, sv_m, grads, plan)
    return loss_row, dx0, grads


W_IN_SLAB_ROWS = 1312


def _position():
    return lax.axis_index("x"), lax.axis_index("y"), lax.axis_index("c")


def _other_chips(x, y):
    return [(1 - x, y), (x, 1 - y), (1 - x, 1 - y)]


def _remote(send_sems, recv_sems, k, src, dst, to):
    return pltpu.make_async_remote_copy(src_ref=src, dst_ref=dst, send_sem=send_sems.at[k], recv_sem=recv_sems.at[k],
                                        device_id=to, device_id_type=MESH)


def _gather_side(entries, whole=()):
    n, nw = len(entries), len(whole)

    def first_hop(ins, outs, send_sems, recv_sems):
        x, y, c = _position()
        cps = []
        for j, chip in enumerate(_other_chips(x, y)):
            for e in range(n):
                cps.append(_remote(send_sems, recv_sems, 6 * e + j, ins[e].at[c], outs[e].at[2 * x + y, c], (*chip, c)))
            for e in range(nw):
                cps.append(_remote(send_sems, recv_sems, 6 * n + 3 * e + j, ins[n + e], outs[n + e].at[2 * x + y],
                                   (*chip, c)))
        return cps

    def start(ins, outs, send_sems, recv_sems):
        for cp in first_hop(ins, outs, send_sems, recv_sems):
            cp.start()

    def finish(ins, outs, send_sems, recv_sems):
        x, y, c = _position()
        me, sibling = (x, y, c), (x, y, 1 - c)
        chips = _other_chips(x, y)
        passed_on = []
        for j, (px, py) in enumerate(chips):
            for e in range(n):
                landed = outs[e].at[2 * px + py, c]
                _remote(send_sems, recv_sems, 6 * e + j, landed, landed, me).wait_recv()
                passed_on.append(_remote(send_sems, recv_sems, 6 * e + 3 + j, landed, landed, sibling))
                passed_on[-1].start()
            for e in range(nw):
                landed = outs[n + e].at[2 * px + py]
                _remote(send_sems, recv_sems, 6 * n + 3 * e + j, landed, landed, me).wait_recv()
        for j, (px, py) in enumerate(chips):
            for e in range(n):
                passed = outs[e].at[2 * px + py, 1 - c]
                _remote(send_sems, recv_sems, 6 * e + 3 + j, passed, passed, me).wait_recv()
        for cp in first_hop(ins, outs, send_sems, recv_sems) + passed_on:
            cp.wait_send()

    shapes = [jax.ShapeDtypeStruct((N_CHIPS,) + a.shape, a.dtype) for a in list(entries) + list(whole)]
    return _Side(list(entries) + list(whole), shapes, 6 * n + 3 * nw, start, finish)


def _run_side(side, name):
    si, so = len(side.inputs), len(side.out_shapes)

    def body(*refs):
        ins, outs, send_sems, recv_sems = refs[:si], refs[si:si + so], refs[-2], refs[-1]
        side.start(ins, outs, send_sems, recv_sems)
        side.finish(ins, outs, send_sems, recv_sems)

    side.outputs = list(pl.pallas_call(
        body, name=name, in_specs=[ANY] * si, out_specs=[ANY] * so, out_shape=side.out_shapes,
        scratch_shapes=[pltpu.SemaphoreType.DMA((side.n_sems,)), pltpu.SemaphoreType.DMA((side.n_sems,))],
    )(*side.inputs))
    return side.outputs


def _swap_side(grads):
    n = len(grads)

    def copies(ins, outs, send_sems, recv_sems):
        x, y, c = _position()
        return [_remote(send_sems, recv_sems, e, ins[e].at[:, 1 - c], outs[e], (x, y, 1 - c)) for e in range(n)]

    def start(ins, outs, send_sems, recv_sems):
        for cp in copies(ins, outs, send_sems, recv_sems):
            cp.start()

    def finish(ins, outs, send_sems, recv_sems):
        for cp in copies(ins, outs, send_sems, recv_sems):
            cp.wait()

    shapes = [jax.ShapeDtypeStruct((N_CHIPS,) + g.shape[2:], g.dtype) for g in grads]
    return _Side(grads, shapes, n, start, finish)


def _chip_exchange_side(chipsums):
    n = len(chipsums)

    def copies(ins, outs, send_sems, recv_sems):
        x, y, c = _position()
        return [_remote(send_sems, recv_sems, 3 * e + j, ins[e].at[2 * tx + ty], outs[e].at[j], (tx, ty, c))
                for j, (tx, ty) in enumerate(_other_chips(x, y)) for e in range(n)]

    def start(ins, outs, send_sems, recv_sems):
        for cp in copies(ins, outs, send_sems, recv_sems):
            cp.start()

    def finish(ins, outs, send_sems, recv_sems):
        for cp in copies(ins, outs, send_sems, recv_sems):
            cp.wait()

    shapes = [jax.ShapeDtypeStruct((3,) + cs.shape[1:], cs.dtype) for cs in chipsums]
    return _Side(chipsums, shapes, 3 * n, start, finish)


def _share_side(totals):
    n = len(totals)

    def copies(ins, outs, send_sems, recv_sems):
        x, y, c = _position()
        return [_remote(send_sems, recv_sems, e, ins[e], outs[e], (x, y, 1 - c)) for e in range(n)]

    def start(ins, outs, send_sems, recv_sems):
        for cp in copies(ins, outs, send_sems, recv_sems):
            cp.start()

    def finish(ins, outs, send_sems, recv_sems):
        for cp in copies(ins, outs, send_sems, recv_sems):
            cp.wait()

    return _Side(totals, [jax.ShapeDtypeStruct(t.shape, t.dtype) for t in totals], n, start, finish)


def _reduce_rows(h):
    return h if h <= 704 else h // 2


def _add_sibling(grad, recv, c_idx, *, name):
    _, _, h, cw = grad.shape
    th = _reduce_rows(h)

    def body(c_ref, g_ref, r_ref, o_ref):
        o_ref[...] = (g_ref[...] + r_ref[...]).astype(BF16)

    return pl.pallas_call(
        body, name=name,
        grid_spec=pltpu.PrefetchScalarGridSpec(
            num_scalar_prefetch=1, grid=(N_CHIPS, h // th),
            in_specs=[pl.BlockSpec((None, None, th, cw), lambda s, i, c_ref: (s, c_ref[0], i, 0)),
                      pl.BlockSpec((None, th, cw), lambda s, i, c_ref: (s, i, 0))],
            out_specs=pl.BlockSpec((None, th, cw), lambda s, i, c_ref: (s, i, 0))),
        out_shape=jax.ShapeDtypeStruct((N_CHIPS, h, cw), BF16),
        compiler_params=_params("parallel", "parallel"),
    )(c_idx, grad, recv)


def _add_chips(chipsum, recv, s_idx, *, name):
    _, h, cw = chipsum.shape
    th = _reduce_rows(h)

    def body(s_ref, own_ref, r_ref, o_ref):
        o_ref[...] = ((own_ref[...].astype(F32) + r_ref[0].astype(F32)) + r_ref[1].astype(F32)) + r_ref[2].astype(F32)

    return pl.pallas_call(
        body, name=name,
        grid_spec=pltpu.PrefetchScalarGridSpec(
            num_scalar_prefetch=1, grid=(h // th,),
            in_specs=[pl.BlockSpec((None, th, cw), lambda i, s_ref: (s_ref[0], i, 0)),
                      pl.BlockSpec((3, th, cw), lambda i, s_ref: (0, i, 0))],
            out_specs=pl.BlockSpec((th, cw), lambda i, s_ref: (i, 0))),
        out_shape=jax.ShapeDtypeStruct((h, cw), F32),
        compiler_params=_params("parallel"),
    )(s_idx, chipsum, recv)


def _adamw_math(w, g, m, v):
    m = ADAM_B1 * m + (1.0 - ADAM_B1) * g
    v = ADAM_B2 * v + (1.0 - ADAM_B2) * (g * g)
    m_hat = m / (1.0 - ADAM_B1 ** ADAM_STEP)
    v_hat = v / (1.0 - ADAM_B2 ** ADAM_STEP)
    delta = -ADAM_LR * (m_hat / (jnp.sqrt(v_hat) + ADAM_EPS) + ADAM_WD * w)
    return delta, m, v


ADAM_TILE_ELEMS = 256 * 1024


def _adamw(w, g, m, v, *, name):
    layers, rows, cols = w.shape
    tr = rows
    for cand in range(8, rows, 8):
        if rows % cand == 0 and cand * cols <= ADAM_TILE_ELEMS:
            tr = cand
    if rows * cols <= ADAM_TILE_ELEMS:
        tr = rows

    def body(w_ref, g_ref, m_ref, v_ref, d_ref, nm_ref, nv_ref):
        d, nm, nv = _adamw_math(w_ref[...], g_ref[...], m_ref[...], v_ref[...])
        d_ref[...] = d
        nm_ref[...] = nm
        nv_ref[...] = nv

    blk = pl.BlockSpec((None, tr, cols), lambda l, i: (l, i, 0))
    sds = jax.ShapeDtypeStruct(w.shape, F32)
    return pl.pallas_call(
        body, name=name, grid=(layers, rows // tr), in_specs=[blk] * 4, out_specs=[blk] * 3, out_shape=[sds] * 3,
        compiler_params=_params("parallel", "parallel"),
    )(w, g, m, v)


SMALL_LAYOUT = (("loss", 1), ("norm_mix", 16), ("norm_ffn", 16), ("ssm_conv_b", 24), ("ssm_dt_bias", 1),
                ("ssm_a_log", 1), ("ssm_d_skip", 1), ("ssm_norm_w", 16), ("att_q_norm", 1), ("att_k_norm", 1),
                ("conv_w_full", 96))
SMALL_ROWS = 176
N_DEVICES = 8


def _small_packs(dicts):
    parts = []
    for values in dicts:
        for name, rows in SMALL_LAYOUT:
            flat = values[name].reshape(-1).astype(F32)
            parts.append(jnp.pad(flat, (0, rows * LANES - flat.shape[0])).reshape(rows, LANES))
        used = sum(r for _, r in SMALL_LAYOUT)
        parts.append(jnp.zeros((SMALL_ROWS - used, LANES), F32))
    return jnp.concatenate(parts, axis=0).reshape(len(dicts), SMALL_ROWS, LANES)


def _small_unpack(pack, shapes):
    out, off = {}, 0
    for name, rows in SMALL_LAYOUT:
        shape = shapes[name]
        n = math.prod(shape)
        out[name] = pack[off:off + rows].reshape(-1)[:n].reshape(shape)
        off += rows
    return out


def _small_allreduce_adamw(g, w, m, v):
    def body(g_ref, w_ref, m_ref, v_ref, gs_ref, d_ref, nm_ref, nv_ref, buf, send_sems, recv_sems):
        x, y, c = _position()
        pos = (x, y, c)
        me = 4 * x + 2 * y + c
        buf[me] = g_ref[...]
        peers = []
        for k in range(1, N_DEVICES):
            bits = ((k >> 2) & 1, (k >> 1) & 1, k & 1)
            peers.append(tuple(1 - p if b else p for p, b in zip(pos, bits)))
        cps = [pltpu.make_async_remote_copy(src_ref=g_ref, dst_ref=buf.at[me], send_sem=send_sems.at[k],
                                            recv_sem=recv_sems.at[k], device_id=peer, device_id_type=MESH)
               for k, peer in enumerate(peers)]
        for cp in cps:
            cp.start()
        for k, (px, py, pc) in enumerate(peers):
            pltpu.make_async_remote_copy(src_ref=g_ref, dst_ref=buf.at[4 * px + 2 * py + pc],
                                         send_sem=send_sems.at[k], recv_sem=recv_sems.at[k],
                                         device_id=(px, py, pc), device_id_type=MESH).wait_recv()
        for cp in cps:
            cp.wait_send()
        total = buf[0]
        for dev in range(1, N_DEVICES):
            total = total + buf[dev]
        gs_ref[...] = total
        d, nm, nv = _adamw_math(w_ref[...], total, m_ref[...], v_ref[...])
        d_ref[...] = d
        nm_ref[...] = nm
        nv_ref[...] = nv

    vm = pl.BlockSpec(memory_space=pltpu.VMEM)
    sds = jax.ShapeDtypeStruct((SMALL_ROWS, LANES), F32)
    return pl.pallas_call(
        body, name="small_allreduce_adamw", in_specs=[vm] * 4, out_specs=[vm] * 4, out_shape=[sds] * 4,
        scratch_shapes=[pltpu.VMEM((N_DEVICES, SMALL_ROWS, LANES), F32),
                        pltpu.SemaphoreType.DMA((N_DEVICES - 1,)), pltpu.SemaphoreType.DMA((N_DEVICES - 1,))],
    )(g, w, m, v)


SMALL = tuple(n for n, _ in SMALL_LAYOUT if n not in ("loss", "conv_w_full"))
WEIGHTS = ("norm_mix", "norm_ffn", "ssm_w_in", "ssm_conv_w", "ssm_conv_b", "ssm_dt_bias", "ssm_a_log", "ssm_d_skip",
           "ssm_norm_w", "ssm_w_out", "att_w_qkv", "att_q_norm", "att_k_norm", "att_w_o", "ffn_w_gate", "ffn_w_up",
           "ffn_w_down", "ple_w_proj", "ple_w_gate")
COLUMN_SHARDED = ("ssm_w_in", "att_w_qkv", "ffn_w_gate", "ffn_w_up", "ple_w_proj")
LAYERED = ("ffn_w_gate", "ffn_w_up", "ffn_w_down", "ple_w_proj", "ple_w_gate")
UPDATED_TRANSPOSED = ("ssm_w_in", "ffn_w_gate", "ffn_w_up")
GATHER_ORDER = ("ssm_w_in", "ssm_w_out", "att_w_qkv", "att_w_o", "ffn_w_gate", "ffn_w_up", "ffn_w_down",
                "ple_w_proj", "ple_w_gate")


def _layers(n):
    return (0, 1) if n in LAYERED else (None,)


def _tag(key):
    return key[0] if key[1] is None else f"{key[0]}_{key[1]}"


QKV_PARTS = 3


def _weight_slab(w, key):
    n, i = key
    if n == "att_w_qkv":
        a = w[n][0].T
        rows = a.shape[0] // QKV_PARTS
        a = a[i * rows:(i + 1) * rows]
    else:
        a = w[n][0 if i is None else i]
        a = a.T if n in COLUMN_SHARDED else a
    if n == "ssm_w_in":
        a = jnp.pad(a, ((0, W_IN_SLAB_ROWS - a.shape[0]), (0, 0)))
    return a.reshape(2, a.shape[0] // 2, a.shape[1]).astype(BF16)


def _install(prm, key, gathered, own, s_me):
    n, i = key
    full = lax.dynamic_update_slice(gathered, own[None], (s_me, 0, 0, 0))
    full = full.reshape(N_CHIPS, 2 * full.shape[2], full.shape[3])
    if n == "att_w_qkv":
        parts = prm.setdefault("att_w_qkv_parts", {})
        parts[i] = full
        if len(parts) == QKV_PARTS:
            prm[n] = jnp.stack([parts[j] for j in range(QKV_PARTS)], axis=1).reshape(-1, D_MODEL)
        return
    if n == "ssm_w_in":
        rows = (D_INNER + CONV_DIM + SSM_HEADS) // N_CHIPS
        w_in_t = full[:, :rows].reshape(N_CHIPS * rows, D_MODEL)
        prm["ssm_w_z"] = w_in_t[:D_INNER]
        prm["ssm_w_xbc"] = w_in_t[D_INNER:D_INNER + CONV_DIM]
        prm["ssm_w_dt"] = jnp.pad(w_in_t[D_INNER + CONV_DIM:], ((0, LANES - SSM_HEADS), (0, 0)))
        return
    full = full.reshape(N_CHIPS * full.shape[1], full.shape[2])
    if i is None:
        prm[n] = full
    else:
        prm.setdefault(n, [None, None])[i] = full


def _grad_slab(grads, key):
    n, i = key
    g = grads[n] if i is None else grads[n][i]
    if n == "ssm_w_in":
        g = jnp.pad(g.reshape(N_CHIPS, g.shape[0] // N_CHIPS, D_MODEL),
                    ((0, 0), (0, W_IN_SLAB_ROWS - g.shape[0] // N_CHIPS), (0, 0)))
    rows = g.size // (N_CHIPS * g.shape[-1])
    return g.reshape(N_CHIPS, 2, rows // 2, g.shape[-1])


def _natural_shard(n, reduced, shape):
    def one(r):
        if n == "ssm_w_in":
            r = r[:shape[-1]]
        return r.T if n in COLUMN_SHARDED else r
    if n in LAYERED:
        return jnp.stack([one(r) for r in reduced]).reshape(shape)
    return one(reduced[0]).reshape(shape)


def kernel(x, p, norm_mix, norm_ffn, ssm_w_in, ssm_conv_w, ssm_conv_b, ssm_dt_bias, ssm_a_log, ssm_d_skip, ssm_norm_w, ssm_w_out, att_w_qkv, att_q_norm, att_k_norm, att_w_o, ffn_w_gate, ffn_w_up, ffn_w_down, ple_w_proj, ple_w_gate, loss_target, m_norm_mix, m_norm_ffn, m_ssm_w_in, m_ssm_conv_w, m_ssm_conv_b, m_ssm_dt_bias, m_ssm_a_log, m_ssm_d_skip, m_ssm_norm_w, m_ssm_w_out, m_att_w_qkv, m_att_q_norm, m_att_k_norm, m_att_w_o, m_ffn_w_gate, m_ffn_w_up, m_ffn_w_down, m_ple_w_proj, m_ple_w_gate, v_norm_mix, v_norm_ffn, v_ssm_w_in, v_ssm_conv_w, v_ssm_conv_b, v_ssm_dt_bias, v_ssm_a_log, v_ssm_d_skip, v_ssm_norm_w, v_ssm_w_out, v_att_w_qkv, v_att_q_norm, v_att_k_norm, v_att_w_o, v_ffn_w_gate, v_ffn_w_up, v_ffn_w_down, v_ple_w_proj, v_ple_w_gate):
    given = dict(locals())
    w = {n: given[n] for n in WEIGHTS}
    m = {n: given["m_" + n] for n in WEIGHTS}
    v = {n: given["v_" + n] for n in WEIGHTS}
    c_idx = lax.axis_index("c").astype(jnp.int32).reshape(1)
    s_idx = (2 * lax.axis_index("x") + lax.axis_index("y")).astype(jnp.int32).reshape(1)

    s_me = 2 * lax.axis_index("x") + lax.axis_index("y")
    first_core = lax.axis_index("c") == 0

    qkv_parts = [("att_w_qkv", j) for j in range(QKV_PARTS)]
    gather_plan = {
        "ssm_in_z": [("ssm_w_out", None)],
        "ssm_in_xbc": [("ffn_w_gate", 0)],
        "conv_fwd": [("ffn_w_up", 0)],
        "ssd_fwd": [("ffn_w_down", 0), ("ple_w_proj", 0), ("ple_w_gate", 0), ("att_w_o", None)],
        "swiglu_fwd_0": qkv_parts[:2],
        "ffn_down_0": qkv_parts[2:],
        "att_qkv": [(n, 1) for n in LAYERED],
    }
    mamba = [("ssm_w_in", None)]
    own = {k: _weight_slab(w, k) for k in mamba + sum(gather_plan.values(), [])}
    prm = {n: w[n] for n in SMALL}

    def land(group, outputs):
        for k, g in zip(group, outputs):
            _install(prm, k, g, own[k], s_me)

    first = _gather_side([own[k] for k in mamba], whole=[ssm_conv_w[0]])
    _run_side(first, "gather_mamba")
    land(mamba, first.outputs)
    conv = lax.dynamic_update_slice(first.outputs[-1], ssm_conv_w, (s_me, 0, 0))
    prm["ssm_conv_w"] = conv.transpose(1, 0, 2).reshape(CONV_WIDTH, CONV_DIM)

    ffn1 = [(n, 1) for n in LAYERED]
    attention = [("att_w_qkv", None), ("att_w_o", None)]
    ffn0 = [(n, 0) for n in LAYERED] + [("ssm_w_out", None)]
    reduce_plan = {"att_out_dx": [("swap", ffn1)], "att_qkv_dx": [("exchange", ffn1)],
                   "swiglu_bwd_0": [("swap", attention)], "gate_norm_bwd": [("swap", ffn0)],
                   "ssd_bwd": [("exchange", attention), ("exchange", ffn0)],
                   "ssm_dh_z": [("swap", mamba)], "ssm_dh_xbc": [("exchange", mamba)]}
    state = {}

    def swap_side(group):
        state[_tag(group[0]), "g4"] = g4 = [_grad_slab(state["grads"], k) for k in group]
        return _swap_side(g4)

    def add_siblings(group, from_sibling):
        state[_tag(group[0]), "chipsums"] = [
            _add_sibling(g, r, c_idx, name="add_sibling_" + _tag(k))
            for g, r, k in zip(state[_tag(group[0]), "g4"], from_sibling, group)]

    def exchange_side(group):
        return _chip_exchange_side(state[_tag(group[0]), "chipsums"])

    def add_chips(group, from_chips):
        for k, cs, r in zip(group, state[_tag(group[0]), "chipsums"], from_chips):
            state["total", k] = _add_chips(cs, r, s_idx, name="add_chips_" + _tag(k))

    class Plan(_NoOverlap):
        def __init__(self):
            self.carried = {host: _gather_side([own[k] for k in group]) for host, group in gather_plan.items()}

        def begin_backward(self, grads):
            state["grads"] = grads

        def side(self, host):
            if host in reduce_plan:
                self.parts = [swap_side(group) if step == "swap" else exchange_side(group)
                              for step, group in reduce_plan[host]]
                self.carried[host] = _sides_together(self.parts)
            elif host == share_host:
                self.carried[host] = _share_side([state["total", k] for k in order])
            return self.carried.get(host)

        def after(self, host):
            if host in gather_plan:
                land(gather_plan[host], self.carried[host].outputs)
            elif host in reduce_plan:
                _share_out(self.carried[host], self.parts)
                for (step, group), part in zip(reduce_plan[host], self.parts):
                    (add_siblings if step == "swap" else add_chips)(group, part.outputs)
            elif host == share_host:
                state["shared"] = self.carried[host].outputs

    order = mamba + ffn0 + attention + ffn1
    share_host = "ssm_dh_dt"
    loss_row, dx, grads = _local_step(x[0], p[:, 0], loss_target[0], prm, Plan())

    reduced = {}
    for k, theirs in zip(order, state["shared"]):
        lo = jnp.where(first_core, state["total", k], theirs)
        hi = jnp.where(first_core, theirs, state["total", k])
        reduced.setdefault(k[0], {})[k[1]] = jnp.concatenate([lo, hi], axis=0)
    reduced = {n: [by_layer[i] for i in _layers(n)] for n, by_layer in reduced.items()}

    grad, delta, new_m, new_v = {}, {}, {}, {}
    for n in GATHER_ORDER:
        if n in UPDATED_TRANSPOSED:
            flip = lambda a: a.transpose(0, 2, 1)
            cols = w[n].shape[-1]
            g_t = jnp.stack([r[:cols] for r in reduced[n]])
            grad[n] = flip(g_t)
            delta[n], new_m[n], new_v[n] = [flip(o) for o in _adamw(flip(w[n]), g_t, flip(m[n]), flip(v[n]),
                                                                    name="adamw_" + n)]
            continue
        grad[n] = _natural_shard(n, reduced[n], w[n].shape)
        delta[n], new_m[n], new_v[n] = _adamw(w[n], grad[n], m[n], v[n], name="adamw_" + n)

    small_g = {n: (jnp.stack(grads[n]) if isinstance(grads[n], list) else grads[n]) for n in SMALL}
    small_g["loss"] = loss_row
    small_g["conv_w_full"] = grads["ssm_conv_w"]
    zero = {"loss": jnp.zeros((1, LANES), F32), "conv_w_full": jnp.zeros((CONV_WIDTH, CONV_DIM), F32)}
    packs = _small_packs([small_g, {**w, **zero}, {**m, **zero}, {**v, **zero}])
    outs = _small_allreduce_adamw(packs[0], packs[1], packs[2], packs[3])
    shapes = {n: w[n].shape for n in SMALL}
    shapes["loss"] = (1, LANES)
    shapes["conv_w_full"] = (CONV_WIDTH, CONV_DIM)
    sg, sd, sm, sv = [_small_unpack(o, shapes) for o in outs]
    for n in SMALL:
        grad[n], delta[n], new_m[n], new_v[n] = sg[n], sd[n], sm[n], sv[n]
    loss = sg["loss"][0, 0]
    conv_cols = CONV_DIM // N_CHIPS
    grad["ssm_conv_w"] = lax.dynamic_slice(sg["conv_w_full"], (0, s_me * conv_cols), (CONV_WIDTH, conv_cols))[None]
    delta["ssm_conv_w"], new_m["ssm_conv_w"], new_v["ssm_conv_w"] = _adamw(
        ssm_conv_w, grad["ssm_conv_w"], m_ssm_conv_w, v_ssm_conv_w, name="adamw_ssm_conv_w")

    return (loss, dx[None], *[grad[n] for n in WEIGHTS], *[delta[n] for n in WEIGHTS],
            *[new_m[n] for n in WEIGHTS], *[new_v[n] for n in WEIGHTS])
```

```python
import math

import jax
import jax.numpy as jnp
from jax import lax
from jax.experimental import pallas as pl
from jax.experimental.pallas import tpu as pltpu

F32 = jnp.float32
BF16 = jnp.bfloat16
HIGHEST = lax.Precision.HIGHEST

NORM_EPS = 1e-6
ADAM_LR, ADAM_B1, ADAM_B2, ADAM_EPS, ADAM_WD, ADAM_STEP = 0.001, 0.9, 0.999, 1e-08, 0.01, 10

D_MODEL = 1024
D_INNER = 2048
SSM_HEADS = 32
SSM_HEAD_DIM = 64
SSM_GROUPS = 4
SSM_STATE = 128
SSD_CHUNK = 128
CONV_DIM = 3072
CONV_WIDTH = 4
ATT_HEADS = 16
ATT_HEAD_DIM = 64
DIL_PATTERNS = ((128, 1), (512, 4), (2048, 16))
ATT_BLOCK = 128
FFN_HIDDEN = 2816
PLE_DIM = 256

LANES = 128
V7X_VMEM_LIMIT = 56 * 1024 * 1024
NEG_BIG = -1e30

N_CHIPS = 4


def _params(*sem):
    return pltpu.CompilerParams(dimension_semantics=sem, vmem_limit_bytes=V7X_VMEM_LIMIT)


def _tile(n, pref):
    if n <= pref:
        return n
    best = None
    for t in range(LANES, pref + 1, LANES):
        if n % t == 0:
            best = t
    assert best is not None, (n, pref)
    return best


def _sigmoid(v):
    return 1.0 / (1.0 + jnp.exp(-v))


def _dot(a, b):
    return jnp.dot(a, b, preferred_element_type=F32)


def _dot_nt(a, b):
    return lax.dot_general(a, b, (((1,), (1,)), ((), ())), preferred_element_type=F32)


def _dot_tn(a, b):
    return lax.dot_general(a, b, (((0,), (0,)), ((), ())), preferred_element_type=F32)


def _head_block_diag():
    i = lax.broadcasted_iota(jnp.int32, (LANES, LANES), 0) // ATT_HEAD_DIM
    j = lax.broadcasted_iota(jnp.int32, (LANES, LANES), 1) // ATT_HEAD_DIM
    return (i == j).astype(BF16)


def _split_dot(ones, z):
    hi = z.astype(BF16)
    lo = (z - hi.astype(F32)).astype(BF16)
    return _dot(ones, hi) + _dot(ones, lo)


def _head_sums(z, bd, terms=2):
    hi = z.astype(BF16)
    lo = (z - hi.astype(F32)).astype(BF16) if terms == 2 else None
    parts = []
    for t in range(z.shape[1] // LANES):
        sl = slice(t * LANES, (t + 1) * LANES)
        part = _dot(hi[:, sl], bd)
        parts.append(part + _dot(lo[:, sl], bd) if terms == 2 else part)
    return parts[0] if len(parts) == 1 else jnp.concatenate(parts, axis=1)


def _lane_lt64(rows):
    return lax.broadcasted_iota(jnp.int32, (rows, LANES), 1) < ATT_HEAD_DIM


MESH = pl.DeviceIdType.MESH
ANY = pl.BlockSpec(memory_space=pl.ANY)


class _Side:
    def __init__(self, inputs, out_shapes, n_sems, start, finish):
        self.inputs, self.out_shapes, self.n_sems = list(inputs), list(out_shapes), n_sems
        self.start, self.finish = start, finish
        self.outputs = None


class _SemaphoresFrom:
    def __init__(self, sems, first):
        self.sems, self.first = sems, first

    @property
    def at(self):
        return self

    def __getitem__(self, k):
        return self.sems.at[self.first + k]


def _sides_together(sides):
    def run(step):
        def both(ins, outs, send_sems, recv_sems):
            i = o = k = 0
            for s in sides:
                ni, no = len(s.inputs), len(s.out_shapes)
                getattr(s, step)(ins[i:i + ni], outs[o:o + no], _SemaphoresFrom(send_sems, k),
                                 _SemaphoresFrom(recv_sems, k))
                i, o, k = i + ni, o + no, k + s.n_sems
        return both

    return _Side(sum([s.inputs for s in sides], []), sum([s.out_shapes for s in sides], []),
                 sum(s.n_sems for s in sides), run("start"), run("finish"))


def _share_out(together, sides):
    o = 0
    for s in sides:
        s.outputs = together.outputs[o:o + len(s.out_shapes)]
        o += len(s.out_shapes)


def _call(body, side, *, name, grid, in_specs, out_specs, out_shape, scratch_shapes, semantics, args):
    in_specs, out_specs, out_shape = list(in_specs), list(out_specs), list(out_shape)
    scratch_shapes = list(scratch_shapes)
    if side is None:
        return pl.pallas_call(body, name=name, grid=grid, in_specs=in_specs, out_specs=out_specs,
                              out_shape=out_shape, scratch_shapes=scratch_shapes,
                              compiler_params=_params(*semantics))(*args)
    ni, no, ns = len(in_specs), len(out_specs), len(scratch_shapes)
    si, so = len(side.inputs), len(side.out_shapes)

    def hosted(*refs):
        ins, s_ins = refs[:ni], refs[ni:ni + si]
        outs, s_outs = refs[ni + si:ni + si + no], refs[ni + si + no:ni + si + no + so]
        scratch = refs[ni + si + no + so:ni + si + no + so + ns]
        send_sems, recv_sems = refs[-2], refs[-1]
        first = pl.program_id(0) == 0
        last = pl.program_id(0) == grid[0] - 1
        for axis in range(1, len(grid)):
            first = jnp.logical_and(first, pl.program_id(axis) == 0)
            last = jnp.logical_and(last, pl.program_id(axis) == grid[axis] - 1)

        @pl.when(first)
        def _():
            side.start(s_ins, s_outs, send_sems, recv_sems)

        body(*ins, *outs, *scratch)

        @pl.when(last)
        def _():
            side.finish(s_ins, s_outs, send_sems, recv_sems)

    res = pl.pallas_call(
        hosted, name=name, grid=grid, in_specs=in_specs + [ANY] * si, out_specs=out_specs + [ANY] * so,
        out_shape=out_shape + side.out_shapes,
        scratch_shapes=scratch_shapes + [pltpu.SemaphoreType.DMA((side.n_sems,)),
                                         pltpu.SemaphoreType.DMA((side.n_sems,))],
        compiler_params=_params(*["arbitrary"] * len(grid)),
    )(*args, *side.inputs)
    side.outputs = list(res[no:])
    return list(res[:no])


def _matmul(a, b, *, mode, name, out_dtype=F32, addend=None, tm=1024, tn=512, tk_max=3072, side=None, second=None):
    m, k = a.shape
    if mode == "nn":
        k2, n = b.shape
    else:
        n, k2 = b.shape
    assert k == k2, (a.shape, b.shape, mode)
    tm, tn, tk = _tile(m, tm), _tile(n, tn), _tile(k, tk_max)
    nk = k // tk
    has_add = addend is not None
    n_rows = len(second[1]) if second else 0
    n_out = 2 if second else 1

    def body(*refs):
        a_ref, b_ref = refs[0], refs[1]
        add_ref = refs[2] if has_add else None
        row_refs = refs[2 + has_add:2 + has_add + n_rows]
        o_ref, acc_ref = refs[-1 - n_out], refs[-1]
        kk = pl.program_id(2)
        col_tile = pl.program_id(1)
        av = a_ref[...].astype(BF16)
        bv = b_ref[...].astype(BF16)
        part = _dot(av, bv) if mode == "nn" else _dot_nt(av, bv)

        @pl.when(kk == 0)
        def _():
            acc_ref[...] = part

        @pl.when(kk > 0)
        def _():
            acc_ref[...] += part

        @pl.when(kk == nk - 1)
        def _():
            res = acc_ref[...]
            if has_add:
                res = res + add_ref[...]
            o_ref[...] = res.astype(out_dtype)
            if second:
                refs[-2][...] = second[0](res, col_tile, *row_refs).astype(second[2])

    a_spec = pl.BlockSpec((tm, tk), lambda i, j, kk: (i, kk))
    if mode == "nn":
        b_spec = pl.BlockSpec((tk, tn), lambda i, j, kk: (kk, j))
    else:
        b_spec = pl.BlockSpec((tn, tk), lambda i, j, kk: (j, kk))
    tile = pl.BlockSpec((tm, tn), lambda i, j, kk: (i, j))
    in_specs = [a_spec, b_spec]
    args = [a, b]
    if has_add:
        in_specs.append(tile)
        args.append(addend)
    if second:
        in_specs += [pl.BlockSpec((1, tn), lambda i, j, kk: (0, j))] * n_rows
        args += list(second[1])
    outs = _call(
        body, side, name=name, grid=(m // tm, n // tn, nk),
        in_specs=in_specs, out_specs=[tile] * n_out,
        out_shape=[jax.ShapeDtypeStruct((m, n), out_dtype)] + ([jax.ShapeDtypeStruct((m, n), second[2])] if second
                                                                 else []),
        scratch_shapes=[pltpu.VMEM((tm, tn), F32)],
        semantics=("parallel", "parallel", "arbitrary"), args=args,
    )
    return outs if second else outs[0]


def _matmul_tn(a, b, *, name, tm=1408, tn=512, tk=1024):
    t, m = a.shape
    t2, n = b.shape
    assert t == t2
    tm, tn, tk = _tile(m, tm), _tile(n, tn), _tile(t, tk)

    def body(a_ref, b_ref, o_ref):
        part = _dot_tn(a_ref[...].astype(BF16), b_ref[...].astype(BF16))

        @pl.when(pl.program_id(2) == 0)
        def _():
            o_ref[...] = part

        @pl.when(pl.program_id(2) > 0)
        def _():
            o_ref[...] += part

    return pl.pallas_call(
        body, name=name, grid=(m // tm, n // tn, t // tk),
        in_specs=[pl.BlockSpec((tk, tm), lambda i, j, kk: (kk, i)),
                  pl.BlockSpec((tk, tn), lambda i, j, kk: (kk, j))],
        out_specs=pl.BlockSpec((tm, tn), lambda i, j, kk: (i, j)),
        out_shape=jax.ShapeDtypeStruct((m, n), F32),
        compiler_params=_params("parallel", "parallel", "arbitrary"),
    )(a, b)


def _rmsnorm_rows(tile, j, gain_ref):
    r = lax.rsqrt(jnp.mean(tile * tile, axis=-1, keepdims=True) + NORM_EPS)
    return tile * r * gain_ref[...]


def _rmsnorm_fwd(x, gain, *, name):
    t, d = x.shape
    tm = _tile(t, 512)

    def body(x_ref, g_ref, o_ref):
        xv = x_ref[...]
        r = lax.rsqrt(jnp.mean(xv * xv, axis=-1, keepdims=True) + NORM_EPS)
        o_ref[...] = (xv * r * g_ref[...]).astype(BF16)

    return pl.pallas_call(
        body, name=name, grid=(t // tm,),
        in_specs=[pl.BlockSpec((tm, d), lambda i: (i, 0)), pl.BlockSpec((1, d), lambda i: (0, 0))],
        out_specs=pl.BlockSpec((tm, d), lambda i: (i, 0)),
        out_shape=jax.ShapeDtypeStruct((t, d), BF16),
        compiler_params=_params("parallel"),
    )(x, gain)


def _matmul_rmsnorm_bwd(a, b, addend, x, gain, dres, *, name, side=None, tm=512, tk_max=3072):
    m, k = a.shape
    d = b.shape[1]
    tm, tk = _tile(m, tm), _tile(k, tk_max)
    nk = k // tk

    def body(a_ref, b_ref, *rest):
        add_ref = rest[0] if addend is not None else None
        x_ref, g_ref, dres_ref, dx_ref, dg_ref, acc_ref = rest[-6:]
        i, kk = pl.program_id(0), pl.program_id(1)
        part = _dot(a_ref[...].astype(BF16), b_ref[...].astype(BF16))

        @pl.when(kk == 0)
        def _():
            acc_ref[...] = part

        @pl.when(kk > 0)
        def _():
            acc_ref[...] += part

        @pl.when(kk == nk - 1)
        def _():
            dyv = acc_ref[...] if addend is None else acc_ref[...] + add_ref[...]
            xv = x_ref[...]
            r = lax.rsqrt(jnp.mean(xv * xv, axis=-1, keepdims=True) + NORM_EPS)
            xh = xv * r
            dxh = dyv * g_ref[...]
            mean = jnp.mean(dxh * xh, axis=-1, keepdims=True)
            dx_ref[...] = dres_ref[...] + r * (dxh - xh * mean)
            gain_part = jnp.sum(dyv * xh, axis=0, keepdims=True)

            @pl.when(i == 0)
            def _():
                dg_ref[...] = gain_part

            @pl.when(i > 0)
            def _():
                dg_ref[...] += gain_part

    row = pl.BlockSpec((tm, d), lambda i, kk: (i, 0))
    vec = pl.BlockSpec((1, d), lambda i, kk: (0, 0))
    return _call(
        body, side, name=name, grid=(m // tm, nk),
        in_specs=[pl.BlockSpec((tm, tk), lambda i, kk: (i, kk)), pl.BlockSpec((tk, d), lambda i, kk: (kk, 0))]
        + ([row] if addend is not None else []) + [row, vec, row],
        out_specs=[row, vec],
        out_shape=[jax.ShapeDtypeStruct((m, d), F32), jax.ShapeDtypeStruct((1, d), F32)],
        scratch_shapes=[pltpu.VMEM((tm, d), F32)],
        semantics=("arbitrary", "arbitrary"),
        args=(a, b) + ((addend,) if addend is not None else ()) + (x, gain, dres),
    )


def _swiglu_fwd(h, w_gate_t, w_up_t, *, name, side=None):
    t, d = h.shape
    f = w_gate_t.shape[0]
    tm, tn = _tile(t, 1024), _tile(f, 256)

    def body(h_ref, wg_ref, wu_ref, g_ref, u_ref, a_ref):
        hv = h_ref[...]
        g = _dot_nt(hv, wg_ref[...])
        u = _dot_nt(hv, wu_ref[...])
        g_ref[...] = g.astype(BF16)
        u_ref[...] = u.astype(BF16)
        a_ref[...] = (g * _sigmoid(g) * u).astype(BF16)

    wspec = pl.BlockSpec((tn, d), lambda i, j: (j, 0))
    ospec = pl.BlockSpec((tm, tn), lambda i, j: (i, j))
    return _call(
        body, side, name=name, grid=(t // tm, f // tn),
        in_specs=[pl.BlockSpec((tm, d), lambda i, j: (i, 0)), wspec, wspec],
        out_specs=[ospec, ospec, ospec],
        out_shape=[jax.ShapeDtypeStruct((t, f), BF16), jax.ShapeDtypeStruct((t, f), BF16),
                   jax.ShapeDtypeStruct((t, f), BF16)],
        scratch_shapes=[], semantics=("parallel", "parallel"), args=(h, w_gate_t, w_up_t),
    )


def _swiglu_bwd(dx, w_down, g, u, *, name, side=None):
    t, d = dx.shape
    f = w_down.shape[0]
    tm, tn = _tile(t, 1024), _tile(f, 256)

    def body(dx_ref, wd_ref, g_ref, u_ref, dg_ref, du_ref):
        dact = _dot_nt(dx_ref[...].astype(BF16), wd_ref[...])
        gv, uv = g_ref[...].astype(F32), u_ref[...].astype(F32)
        sg = _sigmoid(gv)
        dg_ref[...] = (dact * uv * sg * (1.0 + gv * (1.0 - sg))).astype(BF16)
        du_ref[...] = (dact * gv * sg).astype(BF16)

    ospec = pl.BlockSpec((tm, tn), lambda i, j: (i, j))
    return _call(
        body, side, name=name, grid=(t // tm, f // tn),
        in_specs=[pl.BlockSpec((tm, d), lambda i, j: (i, 0)), pl.BlockSpec((tn, d), lambda i, j: (j, 0)),
                  ospec, ospec],
        out_specs=[ospec, ospec],
        out_shape=[jax.ShapeDtypeStruct((t, f), BF16), jax.ShapeDtypeStruct((t, f), BF16)],
        scratch_shapes=[], semantics=("parallel", "parallel"), args=(dx, w_down, g, u),
    )


def _ple_fwd(x, p, w_gate, w_proj_t, *, name, next_gain=None, target=None):
    t, d = x.shape
    e = p.shape[1]
    tm = _tile(t, 512)
    steps = t // tm

    def body(x_ref, p_ref, wg_ref, wp_ref, *rest):
        xv = x_ref[...]
        s = _dot(xv.astype(BF16), wg_ref[...])
        ple = _dot_nt(p_ref[...].astype(BF16), wp_ref[...])
        y = xv + _sigmoid(s) * ple
        if target is None:
            gain_ref, y_ref, h_ref = rest
            y_ref[...] = y
            r = lax.rsqrt(jnp.mean(y * y, axis=-1, keepdims=True) + NORM_EPS)
            h_ref[...] = (y * r * gain_ref[...]).astype(BF16)
        else:
            t_ref, dy_ref, l_ref, acc_ref = rest
            err = y - t_ref[...]
            dy_ref[...] = err * (1.0 / d)
            part = jnp.sum(err * err, axis=0, keepdims=True)

            @pl.when(pl.program_id(0) == 0)
            def _():
                acc_ref[...] = part

            @pl.when(pl.program_id(0) > 0)
            def _():
                acc_ref[...] += part

            @pl.when(pl.program_id(0) == steps - 1)
            def _():
                l_ref[...] = jnp.full((1, LANES), (0.5 / d), F32) * jnp.sum(acc_ref[...])

    row = pl.BlockSpec((tm, d), lambda i: (i, 0))
    fixed = lambda shape: pl.BlockSpec(shape, lambda i: (0, 0))
    in_specs = [row, pl.BlockSpec((tm, e), lambda i: (i, 0)), fixed((d, d)), fixed((d, e))]
    if target is None:
        return pl.pallas_call(
            body, name=name, grid=(steps,), in_specs=in_specs + [fixed((1, d))], out_specs=[row, row],
            out_shape=[jax.ShapeDtypeStruct((t, d), F32), jax.ShapeDtypeStruct((t, d), BF16)],
            compiler_params=_params("parallel"),
        )(x, p, w_gate, w_proj_t, next_gain)
    return pl.pallas_call(
        body, name=name, grid=(steps,), in_specs=in_specs + [row], out_specs=[row, fixed((1, LANES))],
        out_shape=[jax.ShapeDtypeStruct((t, d), F32), jax.ShapeDtypeStruct((1, LANES), F32)],
        scratch_shapes=[pltpu.VMEM((1, d), F32)],
        compiler_params=_params("arbitrary"),
    )(x, p, w_gate, w_proj_t, target)


def _ple_bwd(x, p, w_gate, w_proj_t, dout, *, name):
    t, d = x.shape
    e = p.shape[1]
    tm = _tile(t, 512)

    def body(x_ref, p_ref, wg_ref, wp_ref, do_ref, ds_ref, dple_ref, dx_ref):
        wg = wg_ref[...]
        s = _dot(x_ref[...].astype(BF16), wg)
        ple = _dot_nt(p_ref[...].astype(BF16), wp_ref[...])
        gate = _sigmoid(s)
        dov = do_ref[...]
        dple_ref[...] = (dov * gate).astype(BF16)
        ds = (dov * ple * gate * (1.0 - gate)).astype(BF16)
        ds_ref[...] = ds
        dx_ref[...] = dov + _dot_nt(ds, wg)

    row = pl.BlockSpec((tm, d), lambda i: (i, 0))
    fixed = lambda shape: pl.BlockSpec(shape, lambda i: (0, 0))
    return pl.pallas_call(
        body, name=name, grid=(t // tm,),
        in_specs=[row, pl.BlockSpec((tm, e), lambda i: (i, 0)), fixed((d, d)), fixed((d, e)), row],
        out_specs=[row, row, row],
        out_shape=[jax.ShapeDtypeStruct((t, d), BF16), jax.ShapeDtypeStruct((t, d), BF16),
                   jax.ShapeDtypeStruct((t, d), F32)],
        compiler_params=_params("parallel"),
    )(x, p, w_gate, w_proj_t, dout)


CONV_TIME_TILE = 256
CONV_HALO = 8


def _conv_taps(ext, w):
    acc = ext[CONV_HALO:, :] * w[CONV_WIDTH - 1:CONV_WIDTH, :]
    shifted = [ext[CONV_HALO:, :]]
    for j in range(1, CONV_WIDTH):
        sh = pltpu.roll(ext, j, 0)[CONV_HALO:, :]
        shifted.append(sh)
        acc = acc + sh * w[CONV_WIDTH - 1 - j:CONV_WIDTH - j, :]
    return acc, shifted


def _conv_fwd(u, w, b, side=None):
    t, c = u.shape
    tc = _tile(c, 256)
    tt = CONV_TIME_TILE

    def body(u_ref, w_ref, b_ref, o_ref):
        wv, bv = w_ref[...], b_ref[...]

        def tile(start, ext):
            pre = _conv_taps(ext, wv)[0] + bv
            o_ref[pl.ds(start, tt), :] = pre * _sigmoid(pre)

        tile(0, jnp.concatenate([jnp.zeros((CONV_HALO, tc), F32), u_ref[0:tt, :]], axis=0))

        def loop(i, carry):
            start = pl.multiple_of(i * tt, tt)
            tile(start, u_ref[pl.ds(start - CONV_HALO, tt + CONV_HALO), :])
            return carry

        lax.fori_loop(1, t // tt, loop, 0)

    col = pl.BlockSpec((t, tc), lambda j: (0, j))
    return _call(
        body, side, name="conv_fwd", grid=(c // tc,),
        in_specs=[col, pl.BlockSpec((CONV_WIDTH, tc), lambda j: (0, j)), pl.BlockSpec((1, tc), lambda j: (0, j))],
        out_specs=[col], out_shape=[jax.ShapeDtypeStruct((t, c), F32)],
        scratch_shapes=[], semantics=("parallel",), args=(u, w, b),
    )[0]


def _conv_bwd(u, w, b, dact, side=None):
    t, c = u.shape
    tc = _tile(c, 256)
    tt = CONV_TIME_TILE

    def body(u_ref, w_ref, b_ref, da_ref, du_ref, dw_ref, db_ref, dpre_ref):
        wv, bv = w_ref[...], b_ref[...]

        def tile(start, ext, sums):
            acc, shifted = _conv_taps(ext, wv)
            pre = acc + bv
            sg = _sigmoid(pre)
            dpre = da_ref[pl.ds(start, tt), :] * (sg * (1.0 + pre * (1.0 - sg)))
            dpre_ref[pl.ds(start, tt), :] = dpre
            new = [sums[0] + jnp.sum(dpre, axis=0, keepdims=True)]
            for j in range(CONV_WIDTH):
                new.append(sums[1 + j] + jnp.sum(dpre * shifted[j], axis=0, keepdims=True))
            return tuple(new)

        zero = jnp.zeros((1, tc), F32)
        sums = tile(0, jnp.concatenate([jnp.zeros((CONV_HALO, tc), F32), u_ref[0:tt, :]], axis=0),
                    (zero,) * (1 + CONV_WIDTH))

        def loop(i, sums):
            start = pl.multiple_of(i * tt, tt)
            return tile(start, u_ref[pl.ds(start - CONV_HALO, tt + CONV_HALO), :], sums)

        sums = lax.fori_loop(1, t // tt, loop, sums)
        db_ref[...] = sums[0]
        dw_ref[...] = jnp.concatenate([sums[1 + (CONV_WIDTH - 1 - k)] for k in range(CONV_WIDTH)], axis=0)
        dpre_ref[pl.ds(t, CONV_HALO), :] = jnp.zeros((CONV_HALO, tc), F32)

        def loop2(i, carry):
            start = pl.multiple_of(i * tt, tt)
            ext = dpre_ref[pl.ds(start, tt + CONV_HALO), :]
            acc = ext[0:tt, :] * wv[CONV_WIDTH - 1:CONV_WIDTH, :]
            for j in range(1, CONV_WIDTH):
                acc = acc + pltpu.roll(ext, tt + CONV_HALO - j, 0)[0:tt, :] * wv[CONV_WIDTH - 1 - j:CONV_WIDTH - j, :]
            du_ref[pl.ds(start, tt), :] = acc.astype(BF16)
            return carry

        lax.fori_loop(0, t // tt, loop2, 0)

    col = pl.BlockSpec((t, tc), lambda j: (0, j))
    return _call(
        body, side, name="conv_bwd", grid=(c // tc,),
        in_specs=[col, pl.BlockSpec((CONV_WIDTH, tc), lambda j: (0, j)), pl.BlockSpec((1, tc), lambda j: (0, j)), col],
        out_specs=[col, pl.BlockSpec((CONV_WIDTH, tc), lambda j: (0, j)), pl.BlockSpec((1, tc), lambda j: (0, j))],
        out_shape=[jax.ShapeDtypeStruct((t, c), BF16), jax.ShapeDtypeStruct((CONV_WIDTH, c), F32),
                   jax.ShapeDtypeStruct((1, c), F32)],
        scratch_shapes=[pltpu.VMEM((t + CONV_HALO, tc), F32)],
        semantics=("parallel",), args=(u, w, b, dact),
    )


def _softplus(v):
    e = jnp.exp(-jnp.abs(v))
    w = 1.0 + e
    log1p = jnp.where(w == 1.0, e, jnp.log(w) * (e / jnp.where(w == 1.0, 1.0, w - 1.0)))
    return jnp.maximum(v, 0.0) + log1p


def _split3(z):
    hi = z.astype(BF16)
    rest = z - hi.astype(F32)
    mid = rest.astype(BF16)
    return hi, mid, (rest - mid.astype(F32)).astype(BF16)


def _select_dot(z, ones):
    return sum(_dot(term, ones) for term in _split3(z))


def _ssd_prep_fwd(dt_raw, dt_bias, a_log):
    t = dt_raw.shape[0]
    cl = SSD_CHUNK

    def body(r_ref, b_ref, al_ref, acs_ref, dt_rep_ref, acs_rep_ref):
        dt = _softplus(r_ref[...] + b_ref[...])
        adt = dt * (-jnp.exp(al_ref[...]))
        li = lax.broadcasted_iota(jnp.int32, (cl, cl), 0)
        si = lax.broadcasted_iota(jnp.int32, (cl, cl), 1)
        tri = (si <= li).astype(F32)
        acs = jnp.dot(tri, adt, preferred_element_type=F32, precision=HIGHEST)
        acs_ref[...] = acs
        head = lax.broadcasted_iota(jnp.int32, (LANES, D_INNER), 0)
        chan = lax.broadcasted_iota(jnp.int32, (LANES, D_INNER), 1) // SSM_HEAD_DIM
        spread = (head == chan).astype(BF16)
        dt_rep_ref[...] = _select_dot(dt, spread)
        acs_rep_ref[...] = _select_dot(acs, spread)

    row = pl.BlockSpec((cl, LANES), lambda i: (i, 0))
    wide = pl.BlockSpec((cl, D_INNER), lambda i: (i, 0))
    vec = pl.BlockSpec((1, LANES), lambda i: (0, 0))
    return pl.pallas_call(
        body, name="ssd_prep_fwd", grid=(t // cl,),
        in_specs=[row, vec, vec], out_specs=[row, wide, wide],
        out_shape=[jax.ShapeDtypeStruct((t, LANES), F32), jax.ShapeDtypeStruct((t, D_INNER), F32),
                   jax.ShapeDtypeStruct((t, D_INNER), F32)],
        compiler_params=_params("parallel"),
    )(dt_raw, dt_bias, a_log)


def _ssd_prep_bwd(dt_raw, dt_bias, ddt):
    t = dt_raw.shape[0]
    tm = _tile(t, 512)

    def body(r_ref, b_ref, d_ref, o_ref, db_ref):
        g = d_ref[...] * _sigmoid(r_ref[...] + b_ref[...])
        o_ref[...] = g.astype(BF16)
        part = jnp.sum(g, axis=0, keepdims=True)

        @pl.when(pl.program_id(0) == 0)
        def _():
            db_ref[...] = part

        @pl.when(pl.program_id(0) > 0)
        def _():
            db_ref[...] += part

    row = pl.BlockSpec((tm, LANES), lambda i: (i, 0))
    vec = pl.BlockSpec((1, LANES), lambda i: (0, 0))
    return pl.pallas_call(
        body, name="ssd_prep_bwd", grid=(t // tm,),
        in_specs=[row, vec, row], out_specs=[row, vec],
        out_shape=[jax.ShapeDtypeStruct((t, LANES), BF16), jax.ShapeDtypeStruct((1, LANES), F32)],
        compiler_params=_params("arbitrary"),
    )(dt_raw, dt_bias, ddt)


GROUP_W = D_INNER // SSM_GROUPS
PAIRS_PER_GROUP = GROUP_W // LANES


def _head_cols(acs_pair, lt64):
    rolled = pltpu.roll(acs_pair, ATT_HEAD_DIM, 1)
    return jnp.where(lt64, acs_pair, rolled), jnp.where(lt64, rolled, acs_pair)


def _ssd_fwd(xbc, dt_rep, acs_rep, acs_t, dskip_rep, z, norm_w, side=None):
    t = xbc.shape[0]
    cl = SSD_CHUNK
    nc = t // cl

    def body(xbc_ref, dt_ref, acs_ref, acst_ref, dskip_ref, z_ref, nw_ref, y_ref, hin_ref, yn_ref, state_ref):
        @pl.when(pl.program_id(0) == 0)
        def _():
            state_ref[...] = jnp.zeros_like(state_ref)

        lt64 = _lane_lt64(cl)
        li = lax.broadcasted_iota(jnp.int32, (cl, cl), 0)
        si = lax.broadcasted_iota(jnp.int32, (cl, cl), 1)
        causal = li >= si
        hin_ref[...] = state_ref[...]
        for g in range(SSM_GROUPS):
            gsl = slice(g * GROUP_W, (g + 1) * GROUP_W)
            xg = xbc_ref[:, gsl]
            bg = xbc_ref[:, D_INNER + g * SSM_STATE:D_INNER + (g + 1) * SSM_STATE]
            cg = xbc_ref[:, D_INNER + SSM_GROUPS * SSM_STATE + g * SSM_STATE:
                         D_INNER + SSM_GROUPS * SSM_STATE + (g + 1) * SSM_STATE]
            acs = acs_ref[:, gsl]
            xdt = xg * dt_ref[:, gsl]
            atot = acs[cl - 1:cl, :]
            hin = state_ref[:, gsl]
            cgb = cg.astype(BF16)
            gmat = _dot_nt(cgb, bg.astype(BF16))
            yoff = _dot(cgb, hin.astype(BF16)) * jnp.exp(acs)
            snew = _dot(bg.T.astype(BF16), (xdt * jnp.exp(atot - acs)).astype(BF16))
            state_ref[:, gsl] = hin * jnp.exp(atot) + snew
            xdtb = xdt.astype(BF16)
            for pr in range(PAIRS_PER_GROUP):
                psl = slice(pr * LANES, (pr + 1) * LANES)
                cols = _head_cols(acs[:, psl], lt64)
                xp = xdtb[:, psl]
                ys = []
                for hh in range(2):
                    h = (g * PAIRS_PER_GROUP + pr) * 2 + hh
                    seg = cols[hh] - acst_ref[h:h + 1, :]
                    lm = jnp.exp(jnp.where(causal, seg, NEG_BIG))
                    ys.append(_dot((gmat * lm).astype(BF16), xp))
                ydiag = jnp.where(lt64, ys[0], ys[1])
                osl = slice(g * GROUP_W + pr * LANES, g * GROUP_W + (pr + 1) * LANES)
                y_ref[:, osl] = ydiag + yoff[:, psl] + xg[:, psl] * dskip_ref[:, osl]
            zv = z_ref[:, gsl]
            v = y_ref[:, gsl] * (zv * _sigmoid(zv))
            r = lax.rsqrt(jnp.mean(v * v, axis=-1, keepdims=True) + NORM_EPS)
            yn_ref[:, gsl] = (v * r * nw_ref[:, gsl]).astype(BF16)

    row = lambda w: pl.BlockSpec((cl, w), lambda c: (c, 0))
    vec = pl.BlockSpec((1, D_INNER), lambda c: (0, 0))
    return _call(
        body, side, name="ssd_fwd", grid=(nc,),
        in_specs=[row(CONV_DIM), row(D_INNER), row(D_INNER),
                  pl.BlockSpec((SSM_HEADS, cl), lambda c: (0, c)), vec, row(D_INNER), vec],
        out_specs=[row(D_INNER), pl.BlockSpec((None, SSM_STATE, D_INNER), lambda c: (c, 0, 0)), row(D_INNER)],
        out_shape=[jax.ShapeDtypeStruct((t, D_INNER), F32), jax.ShapeDtypeStruct((nc, SSM_STATE, D_INNER), F32),
                   jax.ShapeDtypeStruct((t, D_INNER), BF16)],
        scratch_shapes=[pltpu.VMEM((SSM_STATE, D_INNER), F32)],
        semantics=("arbitrary",), args=(xbc, dt_rep, acs_rep, acs_t, dskip_rep, z, norm_w),
    )


def _ssd_bwd(xbc, dt_rep, acs_rep, acs_t, dskip_rep, a_rep, hin_all, dy, side=None):
    t = xbc.shape[0]
    cl = SSD_CHUNK
    nc = t // cl

    def body(xbc_ref, dt_ref, acs_ref, acst_ref, dskip_ref, a_ref, hin_ref, dy_ref,
             dxbc_ref, ddt_ref, da_ref, dds_ref, dstate_ref, dacs_ref, dxs_ref):
        step = pl.program_id(0)

        @pl.when(step == 0)
        def _():
            dstate_ref[...] = jnp.zeros_like(dstate_ref)
            da_ref[...] = jnp.zeros_like(da_ref)
            dds_ref[...] = jnp.zeros_like(dds_ref)

        bd = _head_block_diag()
        lt64 = _lane_lt64(cl)
        li = lax.broadcasted_iota(jnp.int32, (cl, cl), 0)
        si = lax.broadcasted_iota(jnp.int32, (cl, cl), 1)
        lower = li >= si
        upper = si >= li
        last_row = lax.broadcasted_iota(jnp.int32, (cl, GROUP_W), 0) == cl - 1
        for g in range(SSM_GROUPS):
            gsl = slice(g * GROUP_W, (g + 1) * GROUP_W)
            bsl = slice(D_INNER + g * SSM_STATE, D_INNER + (g + 1) * SSM_STATE)
            csl = slice(D_INNER + SSM_GROUPS * SSM_STATE + g * SSM_STATE,
                        D_INNER + SSM_GROUPS * SSM_STATE + (g + 1) * SSM_STATE)
            xg = xbc_ref[:, gsl]
            bg = xbc_ref[:, bsl]
            cg = xbc_ref[:, csl]
            bgb, cgb = bg.astype(BF16), cg.astype(BF16)
            acs = acs_ref[:, gsl]
            xdt = xg * dt_ref[:, gsl]
            atot = acs[cl - 1:cl, :]
            eg = jnp.exp(acs)
            dk = jnp.exp(atot - acs)
            etot = jnp.exp(atot)
            hin = hin_ref[:, gsl]
            hinb = hin.astype(BF16)
            dh = dstate_ref[:, gsl]
            dhb = dh.astype(BF16)
            dyg = dy_ref[:, gsl]

            gmat = _dot_nt(cgb, bgb)
            gmat_t = _dot_nt(bgb, cgb)
            ch = _dot(cgb, hinb)
            dacs = _head_sums(dyg * ch * eg, bd)
            dye = (dyg * eg).astype(BF16)
            dc = _dot_nt(dye, hinb)
            dhin = _dot(cg.T.astype(BF16), dye)
            bdh = _dot(bgb, dhb)
            dxs = bdh * dk
            xdk = xdt * dk
            db = _dot_nt(xdk.astype(BF16), dhb)
            ddk = _head_sums(bdh * xdk, bd)
            dacs = dacs - ddk
            datot = jnp.sum(ddk, axis=0, keepdims=True) + etot * _head_sums(
                jnp.sum(dh * hin, axis=0, keepdims=True), bd)
            dacs = dacs + jnp.where(last_row, datot, 0.0)
            dstate_ref[:, gsl] = dh * etot + dhin

            xdtb = xdt.astype(BF16)
            dgsum = jnp.zeros((cl, cl), F32)
            dgsum_t = jnp.zeros((cl, cl), F32)
            for pr in range(PAIRS_PER_GROUP):
                psl = slice(pr * LANES, (pr + 1) * LANES)
                cols = _head_cols(acs[:, psl], lt64)
                xp = xdtb[:, psl]
                dyp = dyg[:, psl].astype(BF16)
                dx1, dac = [], []
                for hh in range(2):
                    h = (g * PAIRS_PER_GROUP + pr) * 2 + hh
                    mine = lt64 if hh == 0 else jnp.logical_not(lt64)
                    row = acst_ref[h:h + 1, :]
                    lm = jnp.exp(jnp.where(lower, cols[hh] - row, NEG_BIG))
                    lm_t = jnp.exp(jnp.where(upper, row - cols[hh], NEG_BIG))
                    dyh = jnp.where(mine, dyp, jnp.zeros_like(dyp))
                    xh = jnp.where(mine, xp, jnp.zeros_like(xp))
                    dm = _dot_nt(dyh, xp)
                    dm_t = _dot_nt(xh, dyp)
                    m_t = gmat_t * lm_t
                    dx1.append(_dot(m_t.astype(BF16), dyp))
                    w = dm * (gmat * lm)
                    w_t = dm_t * m_t
                    dac.append(jnp.sum(w, axis=1, keepdims=True) - jnp.sum(w_t, axis=1, keepdims=True))
                    dgsum = dgsum + dm * lm
                    dgsum_t = dgsum_t + dm_t * lm_t
                osl = slice(g * GROUP_W + pr * LANES, g * GROUP_W + (pr + 1) * LANES)
                dxs_ref[:, osl] = dxs[:, psl] + jnp.where(lt64, dx1[0], dx1[1])
                dacs_ref[:, osl] = dacs[:, psl] + jnp.where(lt64, jnp.broadcast_to(dac[0], (cl, LANES)),
                                                             jnp.broadcast_to(dac[1], (cl, LANES)))
            dxbc_ref[:, csl] = dc + _dot(dgsum.astype(BF16), bgb)
            dxbc_ref[:, bsl] = db + _dot(dgsum_t.astype(BF16), cgb)

        dadt = _split_dot(upper.astype(BF16), dacs_ref[...])
        xall = xbc_ref[:, 0:D_INNER]
        dtall = dt_ref[...]
        dxsall = dxs_ref[...]
        dyall = dy_ref[...]
        ddt_rep = dadt * a_ref[...] + _head_sums(dxsall * xall, bd)
        chan = lax.broadcasted_iota(jnp.int32, (D_INNER, LANES), 0)
        head = lax.broadcasted_iota(jnp.int32, (D_INNER, LANES), 1)
        ddt_ref[...] = _select_dot(ddt_rep, (chan == head * SSM_HEAD_DIM).astype(BF16))
        dxbc_ref[:, 0:D_INNER] = dxsall * dtall + dyall * dskip_ref[...]
        da_ref[...] += jnp.sum(dadt * dtall, axis=0, keepdims=True)
        dds_ref[...] += jnp.sum(dyall * xall, axis=0, keepdims=True)

        @pl.when(step == nc - 1)
        def _():
            dds_ref[...] = _head_sums(dds_ref[...], bd)

    row = lambda w: pl.BlockSpec((cl, w), lambda c: (nc - 1 - c, 0))
    vec = pl.BlockSpec((1, D_INNER), lambda c: (0, 0))
    return _call(
        body, side, name="ssd_bwd", grid=(nc,),
        in_specs=[row(CONV_DIM), row(D_INNER), row(D_INNER),
                  pl.BlockSpec((SSM_HEADS, cl), lambda c: (0, nc - 1 - c)), vec, vec,
                  pl.BlockSpec((None, SSM_STATE, D_INNER), lambda c: (nc - 1 - c, 0, 0)), row(D_INNER)],
        out_specs=[row(CONV_DIM), row(LANES), vec, vec],
        out_shape=[jax.ShapeDtypeStruct((t, CONV_DIM), F32), jax.ShapeDtypeStruct((t, LANES), F32),
                   jax.ShapeDtypeStruct((1, D_INNER), F32), jax.ShapeDtypeStruct((1, D_INNER), F32)],
        scratch_shapes=[pltpu.VMEM((SSM_STATE, D_INNER), F32), pltpu.VMEM((cl, D_INNER), F32),
                        pltpu.VMEM((cl, D_INNER), F32)],
        semantics=("arbitrary",), args=(xbc, dt_rep, acs_rep, acs_t, dskip_rep, a_rep, hin_all, dy),
    )


def _gate_norm_bwd(y, z, w, dx, w_out, side=None):
    t, c = y.shape
    d = dx.shape[1]
    tm = _tile(t, 256)

    def body(y_ref, z_ref, w_ref, dx_ref, wo_ref, dy_ref, dz_ref, dw_ref):
        @pl.when(pl.program_id(0) == 0)
        def _():
            dw_ref[...] = jnp.zeros_like(dw_ref)

        dxb = dx_ref[...].astype(BF16)
        for g in range(SSM_GROUPS):
            gsl = slice(g * GROUP_W, (g + 1) * GROUP_W)
            zv, yv, dov = z_ref[:, gsl], y_ref[:, gsl], _dot_nt(dxb, wo_ref[gsl, :])
            sg = _sigmoid(zv)
            sz = zv * sg
            v = yv * sz
            r = lax.rsqrt(jnp.mean(v * v, axis=-1, keepdims=True) + NORM_EPS)
            vh = v * r
            dvh = dov * w_ref[:, gsl]
            mean = jnp.mean(dvh * vh, axis=-1, keepdims=True)
            dv = r * (dvh - vh * mean)
            dy_ref[:, gsl] = dv * sz
            dz_ref[:, gsl] = (dv * yv * (sg * (1.0 + zv * (1.0 - sg)))).astype(BF16)
            dw_ref[:, gsl] += jnp.sum(dov * vh, axis=0, keepdims=True)

    row = pl.BlockSpec((tm, c), lambda i: (i, 0))
    vec = pl.BlockSpec((1, c), lambda i: (0, 0))
    return _call(
        body, side, name="gate_norm_bwd", grid=(t // tm,),
        in_specs=[row, row, vec, pl.BlockSpec((tm, d), lambda i: (i, 0)), pl.BlockSpec((c, d), lambda i: (0, 0))],
        out_specs=[row, row, vec],
        out_shape=[jax.ShapeDtypeStruct((t, c), F32), jax.ShapeDtypeStruct((t, c), BF16),
                   jax.ShapeDtypeStruct((1, c), F32)],
        scratch_shapes=[], semantics=("arbitrary",), args=(y, z, w, dx, w_out),
    )


ATT_W = ATT_HEADS * ATT_HEAD_DIM
N_QKV_BLOCKS = 9
ATT_SCALE = 1.0 / math.sqrt(ATT_HEAD_DIM)


def _head_rmsnorm(x, gain, bd):
    ms = _head_sums(x * x, bd, terms=1) * (1.0 / ATT_HEAD_DIM)
    return x * lax.rsqrt(ms + NORM_EPS) * gain


def _class_rows(ref, blk, r, dil):
    span = ATT_BLOCK * dil
    sub = ref.at[pl.ds(pl.multiple_of(blk * span, span), span), :]
    return sub[...] if dil == 1 else sub[pl.ds(r, ATT_BLOCK, stride=dil), :]


def _store_class_rows(ref, blk, r, dil, val):
    span = ATT_BLOCK * dil
    sub = ref.at[pl.ds(pl.multiple_of(blk * span, span), span), :]
    if dil == 1:
        sub[...] = val
    else:
        sub[pl.ds(r, ATT_BLOCK, stride=dil), :] = val


PAIRS = ATT_HEADS // 2


def _pair_col(g, j):
    return lambda pair: (0, (g * 3 + j) * PAIRS + pair)


def _pair_slopes(pair):
    steps = jnp.full((1, 2 * ATT_BLOCK), 2 * pair + 1, jnp.int32).astype(F32)
    first = jnp.exp(steps * (-0.5 * math.log(2.0)))
    return first, first * (2.0 ** -0.5)


NORM_ROWS = 512


ROW_SLICES = 4
SLICE_ROWS = 2 * ATT_BLOCK // ROW_SLICES


def _fill_band_bias(bias_ref, pair, dil, transposed):
    bq = ATT_BLOCK
    a = lax.broadcasted_iota(jnp.int32, (2 * bq, 2 * bq), 0) % bq
    b = lax.broadcasted_iota(jnp.int32, (2 * bq, 2 * bq), 1)
    dist = (b - a) if transposed else (a + bq - b)
    in_band = (dist >= 0) & (dist <= bq)
    s0, s1 = _pair_slopes(pair)
    first_head = lax.broadcasted_iota(jnp.int32, (2 * bq, 2 * bq), 0) < bq
    bias = jnp.where(first_head, s0, s1) * (dist.astype(F32) * float(dil))
    inside = (b < bq) if transposed else (b >= bq)
    bias_ref[1] = jnp.where(in_band, bias, -NEG_BIG)
    bias_ref[0] = jnp.where(in_band & inside, bias, -NEG_BIG)


def _row_slices():
    return [slice(i * SLICE_ROWS, (i + 1) * SLICE_ROWS) for i in range(ROW_SLICES)]


def _stack_heads(tile):
    rows = lax.broadcasted_iota(jnp.int32, (2 * ATT_BLOCK, LANES), 0) < ATT_BLOCK
    lanes = lax.broadcasted_iota(jnp.int32, (2 * ATT_BLOCK, LANES), 1) < ATT_HEAD_DIM
    both = jnp.concatenate([tile, tile], axis=0)
    return jnp.where(rows == lanes, both, jnp.zeros_like(both))


def _unstack_heads(stacked, lt64):
    return jnp.where(lt64, stacked[:ATT_BLOCK], stacked[ATT_BLOCK:])


ITEMS_PER_PASS = 4


def _item_loop(nb, dil, work):
    if dil == 1:
        def trip(i, carry):
            work([(i * ITEMS_PER_PASS + b, 0) for b in range(ITEMS_PER_PASS)])
            return carry

        lax.fori_loop(0, nb // ITEMS_PER_PASS, trip, 0)
    else:
        def trip(n, carry):
            for r0 in range(0, dil, ITEMS_PER_PASS):
                work([(n, r0 + j) for j in range(ITEMS_PER_PASS)])
            return carry

        lax.fori_loop(0, nb, trip, 0)


def _qk_normalised(tile, j, gq_ref, gk_ref):
    kind = (j // (ATT_W // tile.shape[1])) % 3
    gain = jnp.where(kind == 0, gq_ref[...] * ATT_SCALE, gk_ref[...])
    return jnp.where(kind == 2, tile, _head_rmsnorm(tile, gain, _head_block_diag()))


def _attn_fwd(qkn, g, dil):
    t = qkn.shape[0]
    nb = t // dil // ATT_BLOCK
    bq = ATT_BLOCK

    def body(qn_ref, kn_ref, v_ref, o_ref, l_ref, bias_ref):
        _fill_band_bias(bias_ref, pl.program_id(0), dil, False)
        lt64 = _lane_lt64(bq)

        def work(items):
            scores, values, probs = [], [], []
            for n, r in items:
                prev = jnp.maximum(n - 1, 0)
                q2 = _stack_heads(_class_rows(qn_ref, n, r, dil).astype(BF16))
                kcat = jnp.concatenate([_class_rows(kn_ref, prev, r, dil), _class_rows(kn_ref, n, r, dil)],
                                       axis=0).astype(BF16)
                values.append(jnp.concatenate([_class_rows(v_ref, prev, r, dil), _class_rows(v_ref, n, r, dil)],
                                              axis=0).astype(BF16))
                scores.append(_dot_nt(q2, kcat))
            for (n, r), sc in zip(items, scores):
                bias = bias_ref.at[jnp.minimum(n, 1)]
                ps, inv, lses = [], [], []
                for rows in _row_slices():
                    s = sc[rows] - bias[rows, :]
                    m = jnp.max(s, axis=1, keepdims=True)
                    p = jnp.exp(s - m)
                    l = jnp.sum(p, axis=1, keepdims=True)
                    ps.append(p.astype(BF16))
                    inv.append(jnp.broadcast_to(1.0 / l, (SLICE_ROWS, LANES)))
                    lses.append(jnp.broadcast_to(m + jnp.log(l), (SLICE_ROWS, LANES)))
                probs.append((jnp.concatenate(ps, axis=0), jnp.concatenate(inv, axis=0)))
                _store_class_rows(l_ref, n, r, dil, _unstack_heads(jnp.concatenate(lses, axis=0), lt64))
            for (n, r), (p, inv), vcat in zip(items, probs, values):
                _store_class_rows(o_ref, n, r, dil, _unstack_heads(_dot(p, vcat) * inv, lt64))

        _item_loop(nb, dil, work)

    col = lambda j: pl.BlockSpec((t, LANES), _pair_col(g, j))
    out = pl.BlockSpec((t, LANES), lambda pair: (0, pair))
    return pl.pallas_call(
        body, name=f"attn_fwd_g{g}", grid=(PAIRS,),
        in_specs=[col(0), col(1), col(2)], out_specs=[out, out],
        out_shape=[jax.ShapeDtypeStruct((t, ATT_W), F32), jax.ShapeDtypeStruct((t, ATT_W), F32)],
        scratch_shapes=[pltpu.VMEM((2, 2 * bq, 2 * bq), F32)],
        compiler_params=_params("parallel"),
    )(qkn, qkn, qkn)


def _one_per_head(rep):
    chan = lax.broadcasted_iota(jnp.int32, (ATT_W, LANES), 0)
    head = lax.broadcasted_iota(jnp.int32, (ATT_W, LANES), 1)
    return _select_dot(rep, (chan == head * ATT_HEAD_DIM).astype(BF16))


def _attn_combine_fwd(outs, lses):
    t = outs[0].shape[0]
    tm = _tile(t, 256)

    def body(o0, o1, o2, l0, l1, l2, ob_ref, of_ref, lt_ref, lc_ref):
        a, b, c = l0[...], l1[...], l2[...]
        m = jnp.maximum(jnp.maximum(a, b), c)
        ea, eb, ec = jnp.exp(a - m), jnp.exp(b - m), jnp.exp(c - m)
        ssum = ea + eb + ec
        o = (ea * o0[...] + eb * o1[...] + ec * o2[...]) / ssum
        ob_ref[...] = o.astype(BF16)
        of_ref[...] = o
        lse = m + jnp.log(ssum)
        lt_ref[...] = lse
        lc_ref[...] = _one_per_head(lse)

    row = pl.BlockSpec((tm, ATT_W), lambda i: (i, 0))
    return pl.pallas_call(
        body, name="attn_combine_fwd", grid=(t // tm,),
        in_specs=[row] * 6, out_specs=[row] * 3 + [pl.BlockSpec((tm, LANES), lambda i: (i, 0))],
        out_shape=[jax.ShapeDtypeStruct((t, ATT_W), BF16), jax.ShapeDtypeStruct((t, ATT_W), F32),
                   jax.ShapeDtypeStruct((t, ATT_W), F32), jax.ShapeDtypeStruct((t, LANES), F32)],
        compiler_params=_params("parallel"),
    )(*outs, *lses)


def _attn_out_bwd(dx, w_o, o, side=None):
    t, d = dx.shape
    tm = _tile(t, 256)

    def body(dx_ref, wo_ref, o_ref, do_ref, dl_ref, dc_ref):
        do = _dot_nt(dx_ref[...].astype(BF16), wo_ref[...])
        do_ref[...] = do
        dl = _head_sums(do * o_ref[...], _head_block_diag())
        dl_ref[...] = dl
        dc_ref[...] = _one_per_head(dl)

    row = pl.BlockSpec((tm, ATT_W), lambda i: (i, 0))
    return _call(
        body, side, name="att_out_dx", grid=(t // tm,),
        in_specs=[pl.BlockSpec((tm, d), lambda i: (i, 0)), pl.BlockSpec((ATT_W, d), lambda i: (0, 0)), row],
        out_specs=[row, row, pl.BlockSpec((tm, LANES), lambda i: (i, 0))],
        out_shape=[jax.ShapeDtypeStruct((t, ATT_W), F32), jax.ShapeDtypeStruct((t, ATT_W), F32),
                   jax.ShapeDtypeStruct((t, LANES), F32)],
        scratch_shapes=[], semantics=("parallel",), args=(dx, w_o, o),
    )


def _head_rmsnorm_bwd(x_ref, dy_ref, gain_ref, dx_ref, dgain_ref):
    bd = _head_block_diag()
    gain = gain_ref[...]

    def step(i, acc):
        rows = pl.ds(pl.multiple_of(i * NORM_ROWS, NORM_ROWS), NORM_ROWS)
        x, dy = x_ref[rows, :], dy_ref[rows, :]
        r = lax.rsqrt(_head_sums(x * x, bd, terms=1) * (1.0 / ATT_HEAD_DIM) + NORM_EPS)
        xh = x * r
        dxh = dy * gain
        mean = _head_sums(dxh * xh, bd, terms=1) * (1.0 / ATT_HEAD_DIM)
        dx_ref[rows, :] = (r * (dxh - xh * mean)).astype(BF16)
        return acc + jnp.sum(dy * xh, axis=0, keepdims=True)

    acc = lax.fori_loop(0, x_ref.shape[0] // NORM_ROWS, step, jnp.zeros((1, LANES), F32))
    dgain_ref[...] = jnp.broadcast_to(acc, dgain_ref.shape)


def _attn_bwd_dq(qkv, qkn, gq, do, l_rep, dl_rep, g, dil):
    t = qkv.shape[0]
    nb = t // dil // ATT_BLOCK
    bq = ATT_BLOCK

    def body(q_ref, qn_ref, kn_ref, v_ref, gq_ref, do_ref, l_ref, dl_ref, dx_ref, dgain_ref, bias_ref, dq_ref):
        _fill_band_bias(bias_ref, pl.program_id(0), dil, False)
        lt64 = _lane_lt64(bq)

        def per_row(tile):
            cols = _head_cols(tile, lt64)
            half = jnp.concatenate([cols[0], cols[1]], axis=0)
            return jnp.concatenate([half, half], axis=1)

        def work(items):
            products, keys, dscores = [], [], []
            for n, r in items:
                prev = jnp.maximum(n - 1, 0)
                q2 = _stack_heads(_class_rows(qn_ref, n, r, dil).astype(BF16))
                do2 = _stack_heads(_class_rows(do_ref, n, r, dil).astype(BF16))
                kcat = jnp.concatenate([_class_rows(kn_ref, prev, r, dil), _class_rows(kn_ref, n, r, dil)],
                                       axis=0).astype(BF16)
                vcat = jnp.concatenate([_class_rows(v_ref, prev, r, dil), _class_rows(v_ref, n, r, dil)],
                                       axis=0).astype(BF16)
                keys.append(kcat)
                products.append((_dot_nt(q2, kcat), _dot_nt(do2, vcat)))
            for (n, r), (scores, dps) in zip(items, products):
                bias = bias_ref.at[jnp.minimum(n, 1)]
                lse = per_row(_class_rows(l_ref, n, r, dil))
                dl = per_row(_class_rows(dl_ref, n, r, dil))
                dss = []
                for rows in _row_slices():
                    p = jnp.exp(scores[rows] - bias[rows, :] - lse[rows])
                    dss.append((p * (dps[rows] - dl[rows])).astype(BF16))
                dscores.append(jnp.concatenate(dss, axis=0))
            for (n, r), ds, kcat in zip(items, dscores, keys):
                _store_class_rows(dq_ref, n, r, dil, _unstack_heads(_dot(ds, kcat) * ATT_SCALE, lt64))

        _item_loop(nb, dil, work)
        _head_rmsnorm_bwd(q_ref, dq_ref, gq_ref, dx_ref, dgain_ref)

    col = lambda j: pl.BlockSpec((t, LANES), _pair_col(g, j))
    vec = pl.BlockSpec((1, LANES), lambda pair: (0, 0))
    tok = pl.BlockSpec((t, LANES), lambda pair: (0, pair))
    return pl.pallas_call(
        body, name=f"attn_bwd_dq_g{g}", grid=(PAIRS,),
        in_specs=[col(0), col(0), col(1), col(2), vec, tok, tok, tok],
        out_specs=[tok, pl.BlockSpec((None, 8, LANES), lambda pair: (pair, 0, 0))],
        out_shape=[jax.ShapeDtypeStruct((t, ATT_W), BF16), jax.ShapeDtypeStruct((PAIRS, 8, LANES), F32)],
        scratch_shapes=[pltpu.VMEM((2, 2 * bq, 2 * bq), F32), pltpu.VMEM((t, LANES), F32)],
        compiler_params=_params("parallel"),
    )(qkv, qkn, qkn, qkn, gq, do, l_rep, dl_rep)


def _attn_bwd_dkv(qkv, qkn, gk, do, l_row, dl_row, g, dil):
    t = qkv.shape[0]
    nb = t // dil // ATT_BLOCK
    bq = ATT_BLOCK

    def body(k_ref, qn_ref, kn_ref, v_ref, gk_ref, do_ref, l_ref, dl_ref, dkx_ref, dvx_ref, dgain_ref, bias_ref,
             dk_ref, dv_ref):
        _fill_band_bias(bias_ref, pl.program_id(0), dil, True)
        lt64 = _lane_lt64(bq)

        def per_query(ref, hh, lane_c, lane_n):
            return jnp.concatenate([ref[hh:hh + 1, pl.ds(lane_c, bq)], ref[hh:hh + 1, pl.ds(lane_n, bq)]], axis=1)

        def work(items):
            products, operands, weights = [], [], []
            for n, r in items:
                nxt = jnp.minimum(n + 1, nb - 1)
                k2 = _stack_heads(_class_rows(kn_ref, n, r, dil).astype(BF16))
                v2 = _stack_heads(_class_rows(v_ref, n, r, dil).astype(BF16))
                qcat = jnp.concatenate([_class_rows(qn_ref, n, r, dil), _class_rows(qn_ref, nxt, r, dil)],
                                       axis=0).astype(BF16)
                docat = jnp.concatenate([_class_rows(do_ref, n, r, dil), _class_rows(do_ref, nxt, r, dil)],
                                        axis=0).astype(BF16)
                operands.append((qcat, docat))
                products.append((_dot_nt(k2, qcat), _dot_nt(v2, docat)))
            for (n, r), (scores, dps) in zip(items, products):
                nxt = jnp.minimum(n + 1, nb - 1)
                bias = bias_ref.at[jnp.where(n == nb - 1, 0, 1)]
                lane_c = pl.multiple_of((r * nb + n) * bq, bq)
                lane_n = pl.multiple_of((r * nb + nxt) * bq, bq)
                lse = [per_query(l_ref, hh, lane_c, lane_n) for hh in range(2)]
                dl = [per_query(dl_ref, hh, lane_c, lane_n) for hh in range(2)]
                pts, dss = [], []
                for i, rows in enumerate(_row_slices()):
                    hh = i * SLICE_ROWS // bq
                    p_t = jnp.exp(scores[rows] - bias[rows, :] - lse[hh])
                    pts.append(p_t.astype(BF16))
                    dss.append((p_t * (dps[rows] - dl[hh])).astype(BF16))
                weights.append((jnp.concatenate(pts, axis=0), jnp.concatenate(dss, axis=0)))
            for (n, r), (p_t, ds_t), (qcat, docat) in zip(items, weights, operands):
                _store_class_rows(dv_ref, n, r, dil, _unstack_heads(_dot(p_t, docat), lt64))
                _store_class_rows(dk_ref, n, r, dil, _unstack_heads(_dot(ds_t, qcat), lt64))

        _item_loop(nb, dil, work)
        _head_rmsnorm_bwd(k_ref, dk_ref, gk_ref, dkx_ref, dgain_ref)

        def cast_rows(i, carry):
            rows = pl.ds(pl.multiple_of(i * NORM_ROWS, NORM_ROWS), NORM_ROWS)
            dvx_ref[rows, :] = dv_ref[rows, :].astype(BF16)
            return carry

        lax.fori_loop(0, t // NORM_ROWS, cast_rows, 0)

    col = lambda j: pl.BlockSpec((t, LANES), _pair_col(g, j))
    vec = pl.BlockSpec((1, LANES), lambda pair: (0, 0))
    tok = pl.BlockSpec((t, LANES), lambda pair: (0, pair))
    rows = pl.BlockSpec((None, 8, t), lambda pair: (pair, 0, 0))
    return pl.pallas_call(
        body, name=f"attn_bwd_dkv_g{g}", grid=(PAIRS,),
        in_specs=[col(1), col(0), col(1), col(2), vec, tok, rows, rows],
        out_specs=[tok, tok, pl.BlockSpec((None, 8, LANES), lambda pair: (pair, 0, 0))],
        out_shape=[jax.ShapeDtypeStruct((t, ATT_W), BF16), jax.ShapeDtypeStruct((t, ATT_W), BF16),
                   jax.ShapeDtypeStruct((PAIRS, 8, LANES), F32)],
        scratch_shapes=[pltpu.VMEM((2, 2 * bq, 2 * bq), F32), pltpu.VMEM((t, LANES), F32),
                        pltpu.VMEM((t, LANES), F32)],
        compiler_params=_params("parallel"),
    )(qkv, qkn, qkn, qkn, gk, do, l_row, dl_row)


def _rows_by_residue(one_per_head, dil):
    t = one_per_head.shape[0]
    per_head = one_per_head[:, :ATT_HEADS]
    rows = per_head.reshape(t // dil, dil, ATT_HEADS).transpose(2, 1, 0).reshape(PAIRS, 2, t)
    return jnp.pad(rows, ((0, 0), (0, 6), (0, 0)))


def _per_head(rep_row):
    return rep_row[0, ::SSM_HEAD_DIM]


def _rep_heads(v):
    return jnp.repeat(v, SSM_HEAD_DIM)[None, :]


def _pad_lanes(v):
    return jnp.pad(v, ((0, 0), (0, LANES - v.shape[1])))


class _NoOverlap:
    def side(self, host):
        return None

    def after(self, host):
        pass

    def begin_backward(self, grads):
        pass


def _hosted(plan, host, fn, *args, **kwargs):
    out = fn(*args, side=plan.side(host), **kwargs)
    plan.after(host)
    return out


def _ffn_ple_fwd(x1, h, p_i, prm, i, plan, next_gain=None, target=None):
    g, u, act = _hosted(plan, f"swiglu_fwd_{i}", _swiglu_fwd, h, prm["ffn_w_gate"][i], prm["ffn_w_up"][i],
                        name=f"swiglu_fwd_{i}")
    x2 = _hosted(plan, f"ffn_down_{i}", _matmul, act, prm["ffn_w_down"][i], mode="nn", addend=x1,
                 name=f"ffn_down_{i}")
    outs = _ple_fwd(x2, p_i, prm["ple_w_gate"][i], prm["ple_w_proj"][i], name=f"ple_fwd_{i}", next_gain=next_gain,
                    target=target)
    return outs, dict(x1=x1, h=h, g=g, u=u, act=act, x2=x2)


def _ffn_ple_bwd(dx3, p_i, prm, i, sv, grads, plan):
    ds, dple, dx2 = _ple_bwd(sv["x2"], p_i, prm["ple_w_gate"][i], prm["ple_w_proj"][i], dx3, name=f"ple_bwd_{i}")
    grads["ple_w_gate"][i] = _matmul_tn(sv["x2"], ds, name=f"d_ple_w_gate_{i}")
    grads["ple_w_proj"][i] = _matmul_tn(dple, p_i, name=f"d_ple_w_proj_{i}")
    grads["ffn_w_down"][i] = _matmul_tn(sv["act"], dx2, name=f"d_ffn_w_down_{i}")
    dg, du = _hosted(plan, f"swiglu_bwd_{i}", _swiglu_bwd, dx2, prm["ffn_w_down"][i], sv["g"], sv["u"],
                     name=f"swiglu_bwd_{i}")
    grads["ffn_w_gate"][i] = _matmul_tn(dg, sv["h"], name=f"d_ffn_w_gate_{i}")
    grads["ffn_w_up"][i] = _matmul_tn(du, sv["h"], name=f"d_ffn_w_up_{i}")
    dh = _matmul(dg, prm["ffn_w_gate"][i], mode="nn", name=f"ffn_dh_gate_{i}")
    dx1, dgain = _matmul_rmsnorm_bwd(du, prm["ffn_w_up"][i], dh, sv["x1"], prm["norm_ffn"][i:i + 1], dx2,
                                     name=f"ffn_dh_up_{i}")
    grads["norm_ffn"][i] = dgain[0]
    return dx1


def _mamba_fwd(x0, prm, plan):
    h = _rmsnorm_fwd(x0, prm["norm_mix"][0:1], name="mix_norm_fwd_0")
    z = _hosted(plan, "ssm_in_z", _matmul, h, prm["ssm_w_z"], mode="nt", name="ssm_in_z")
    xbc_pre = _hosted(plan, "ssm_in_xbc", _matmul, h, prm["ssm_w_xbc"], mode="nt", name="ssm_in_xbc")
    dt_raw = _matmul(h, prm["ssm_w_dt"], mode="nt", name="ssm_in_dt")
    xbc = _hosted(plan, "conv_fwd", _conv_fwd, xbc_pre, prm["ssm_conv_w"], prm["ssm_conv_b"])
    dt_bias = _pad_lanes(prm["ssm_dt_bias"])
    a_log = _pad_lanes(prm["ssm_a_log"])
    acs, dt_rep, acs_rep = _ssd_prep_fwd(dt_raw, dt_bias, a_log)
    acs_t = acs[:, :SSM_HEADS].T
    dskip_rep = _rep_heads(prm["ssm_d_skip"][0])
    y, hin_all, yn = _hosted(plan, "ssd_fwd", _ssd_fwd, xbc, dt_rep, acs_rep, acs_t, dskip_rep, z,
                             prm["ssm_norm_w"])
    x1, h_ffn = _matmul(yn, prm["ssm_w_out"], mode="nn", addend=x0, name="ssm_out", tm=512, tn=D_MODEL,
                        second=(_rmsnorm_rows, [prm["norm_ffn"][0:1]], BF16))
    sv = dict(x0=x0, h=h, z=z, xbc_pre=xbc_pre, dt_raw=dt_raw, xbc=xbc, dt_bias=dt_bias, dt_rep=dt_rep,
              acs_rep=acs_rep, acs_t=acs_t, dskip_rep=dskip_rep, y=y, hin_all=hin_all, yn=yn)
    return x1, h_ffn, sv


def _mamba_bwd(dx1, prm, sv, grads, plan):
    grads["ssm_w_out"] = _matmul_tn(sv["yn"], dx1, name="d_ssm_w_out")
    dy, dz, dnw = _hosted(plan, "gate_norm_bwd", _gate_norm_bwd, sv["y"], sv["z"], prm["ssm_norm_w"], dx1,
                          prm["ssm_w_out"])
    grads["ssm_norm_w"] = dnw
    a_rep = _rep_heads(-jnp.exp(prm["ssm_a_log"][0]))
    dxbc, ddt, da_rep, dds_rep = _hosted(plan, "ssd_bwd", _ssd_bwd, sv["xbc"], sv["dt_rep"], sv["acs_rep"],
                                             sv["acs_t"], sv["dskip_rep"], a_rep, sv["hin_all"], dy)
    grads["ssm_d_skip"] = _per_head(dds_rep)[None, :]
    grads["ssm_a_log"] = (_per_head(da_rep) * _per_head(a_rep))[None, :]
    ddt_raw, dbias = _ssd_prep_bwd(sv["dt_raw"], sv["dt_bias"], ddt)
    grads["ssm_dt_bias"] = dbias[:, :SSM_HEADS]
    du, dcw, dcb = _hosted(plan, "conv_bwd", _conv_bwd, sv["xbc_pre"], prm["ssm_conv_w"], prm["ssm_conv_b"], dxbc)
    grads["ssm_conv_w"] = dcw
    grads["ssm_conv_b"] = dcb
    h = sv["h"]
    grads["ssm_w_in"] = jnp.concatenate(
        [_matmul_tn(dz, h, name="d_ssm_w_z"), _matmul_tn(du, h, name="d_ssm_w_xbc"),
         _matmul_tn(ddt_raw, h, name="d_ssm_w_dt")[:SSM_HEADS]], axis=0)
    dh = _hosted(plan, "ssm_dh_z", _matmul, dz, prm["ssm_w_z"], mode="nn", name="ssm_dh_z")
    dh = _hosted(plan, "ssm_dh_xbc", _matmul, du, prm["ssm_w_xbc"], mode="nn", addend=dh, name="ssm_dh_xbc")
    dx0, dgain = _hosted(plan, "ssm_dh_dt", _matmul_rmsnorm_bwd, ddt_raw, prm["ssm_w_dt"], dh, sv["x0"],
                         prm["norm_mix"][0:1], dx1, name="ssm_dh_dt")
    grads["norm_mix"][0] = dgain[0]
    return dx0


def _attn_mixer_fwd(x0, h, prm, plan):
    n_heads = N_QKV_BLOCKS * ATT_HEADS
    gq = jnp.tile(prm["att_q_norm"], (1, n_heads))
    gk = jnp.tile(prm["att_k_norm"], (1, n_heads))
    qkv, qkn = _hosted(plan, "att_qkv", _matmul, h, prm["att_w_qkv"], mode="nt", name="att_qkv",
                       second=(_qk_normalised, [gq, gk], F32))
    outs, lses = [], []
    for g, (window, dil) in enumerate(DIL_PATTERNS):
        o_g, l_g = _attn_fwd(qkn, g, dil)
        outs.append(o_g)
        lses.append(l_g)
    o_b, o_f, l_rep, l_one = _attn_combine_fwd(outs, lses)
    x1, h_ffn = _matmul(o_b, prm["att_w_o"], mode="nn", addend=x0, name="att_out", tm=512, tn=D_MODEL,
                        second=(_rmsnorm_rows, [prm["norm_ffn"][1:2]], BF16))
    sv = dict(x0=x0, h=h, qkv=qkv, qkn=qkn, gq2=gq[:, :LANES], gk2=gk[:, :LANES], o_b=o_b, o_f=o_f, l_rep=l_rep,
              l_one=l_one)
    return x1, h_ffn, sv


def _attn_mixer_bwd(dx1, prm, sv, grads, plan):
    grads["att_w_o"] = _matmul_tn(sv["o_b"], dx1, name="d_att_w_o")
    do, dl_rep, dl_one = _hosted(plan, "att_out_dx", _attn_out_bwd, dx1, prm["att_w_o"], sv["o_f"])
    blocks, dgq, dgk = [], [], []
    for g, (window, dil) in enumerate(DIL_PATTERNS):
        dq, dgq_g = _attn_bwd_dq(sv["qkv"], sv["qkn"], sv["gq2"], do, sv["l_rep"], dl_rep, g, dil)
        dk, dv, dgk_g = _attn_bwd_dkv(sv["qkv"], sv["qkn"], sv["gk2"], do, _rows_by_residue(sv["l_one"], dil),
                                      _rows_by_residue(dl_one, dil), g, dil)
        blocks += [dq, dk, dv]
        dgq.append(dgq_g)
        dgk.append(dgk_g)
    dqkv = jnp.concatenate(blocks, axis=1)

    def fold(parts):
        return jnp.stack(parts)[:, :, 0].reshape(-1, ATT_HEAD_DIM).sum(axis=0)[None, :]

    grads["att_q_norm"] = fold(dgq)
    grads["att_k_norm"] = fold(dgk)
    grads["att_w_qkv"] = _matmul_tn(dqkv, sv["h"], name="d_att_w_qkv")
    dx0, dgain = _hosted(plan, "att_qkv_dx", _matmul_rmsnorm_bwd, dqkv, prm["att_w_qkv"], None, sv["x0"],
                         prm["norm_mix"][1:2], dx1, name="att_qkv_dx")
    grads["norm_mix"][1] = dgain[0]
    return dx0


def _local_step(x, p, target, prm, plan=None):
    plan = plan or _NoOverlap()
    grads = {k: [None, None] for k in ("norm_mix", "norm_ffn", "ffn_w_gate", "ffn_w_up", "ffn_w_down",
                                       "ple_w_proj", "ple_w_gate")}
    plan.begin_backward(grads)
    x1, h1, sv_m = _mamba_fwd(x, prm, plan)
    (x3, h3), sv_f0 = _ffn_ple_fwd(x1, h1, p[0], prm, 0, plan, next_gain=prm["norm_mix"][1:2])
    x4, h4, sv_a = _attn_mixer_fwd(x3, h3, prm, plan)
    (dy, loss_row), sv_f1 = _ffn_ple_fwd(x4, h4, p[1], prm, 1, plan, target=target)
    dx4 = _ffn_ple_bwd(dy, p[1], prm, 1, sv_f1, grads, plan)
    dx3 = _attn_mixer_bwd(dx4, prm, sv_a, grads, plan)
    dx1 = _ffn_ple_bwd(dx3, p[0], prm, 0, sv_f0, grads, plan)
    dx0 = _mamba_bwd(dx1, prm, sv_m, grads, plan)
    return loss_row, dx0, grads


W_IN_SLAB_ROWS = 1312


def _position():
    return lax.axis_index("x"), lax.axis_index("y"), lax.axis_index("c")


def _other_chips(x, y):
    return [(1 - x, y), (x, 1 - y), (1 - x, 1 - y)]


def _remote(send_sems, recv_sems, k, src, dst, to):
    return pltpu.make_async_remote_copy(src_ref=src, dst_ref=dst, send_sem=send_sems.at[k], recv_sem=recv_sems.at[k],
                                        device_id=to, device_id_type=MESH)


def _gather_side(entries, whole=()):
    n, nw = len(entries), len(whole)

    def first_hop(ins, outs, send_sems, recv_sems):
        x, y, c = _position()
        cps = []
        for j, chip in enumerate(_other_chips(x, y)):
            for e in range(n):
                cps.append(_remote(send_sems, recv_sems, 6 * e + j, ins[e].at[c], outs[e].at[2 * x + y, c], (*chip, c)))
            for e in range(nw):
                cps.append(_remote(send_sems, recv_sems, 6 * n + 3 * e + j, ins[n + e], outs[n + e].at[2 * x + y],
                                   (*chip, c)))
        return cps

    def start(ins, outs, send_sems, recv_sems):
        for cp in first_hop(ins, outs, send_sems, recv_sems):
            cp.start()

    def finish(ins, outs, send_sems, recv_sems):
        x, y, c = _position()
        me, sibling = (x, y, c), (x, y, 1 - c)
        chips = _other_chips(x, y)
        passed_on = []
        for j, (px, py) in enumerate(chips):
            for e in range(n):
                landed = outs[e].at[2 * px + py, c]
                _remote(send_sems, recv_sems, 6 * e + j, landed, landed, me).wait_recv()
                passed_on.append(_remote(send_sems, recv_sems, 6 * e + 3 + j, landed, landed, sibling))
                passed_on[-1].start()
            for e in range(nw):
                landed = outs[n + e].at[2 * px + py]
                _remote(send_sems, recv_sems, 6 * n + 3 * e + j, landed, landed, me).wait_recv()
        for j, (px, py) in enumerate(chips):
            for e in range(n):
                passed = outs[e].at[2 * px + py, 1 - c]
                _remote(send_sems, recv_sems, 6 * e + 3 + j, passed, passed, me).wait_recv()
        for cp in first_hop(ins, outs, send_sems, recv_sems) + passed_on:
            cp.wait_send()

    shapes = [jax.ShapeDtypeStruct((N_CHIPS,) + a.shape, a.dtype) for a in list(entries) + list(whole)]
    return _Side(list(entries) + list(whole), shapes, 6 * n + 3 * nw, start, finish)


def _run_side(side, name):
    si, so = len(side.inputs), len(side.out_shapes)

    def body(*refs):
        ins, outs, send_sems, recv_sems = refs[:si], refs[si:si + so], refs[-2], refs[-1]
        side.start(ins, outs, send_sems, recv_sems)
        side.finish(ins, outs, send_sems, recv_sems)

    side.outputs = list(pl.pallas_call(
        body, name=name, in_specs=[ANY] * si, out_specs=[ANY] * so, out_shape=side.out_shapes,
        scratch_shapes=[pltpu.SemaphoreType.DMA((side.n_sems,)), pltpu.SemaphoreType.DMA((side.n_sems,))],
    )(*side.inputs))
    return side.outputs


def _swap_side(grads):
    n = len(grads)

    def copies(ins, outs, send_sems, recv_sems):
        x, y, c = _position()
        return [_remote(send_sems, recv_sems, e, ins[e].at[:, 1 - c], outs[e], (x, y, 1 - c)) for e in range(n)]

    def start(ins, outs, send_sems, recv_sems):
        for cp in copies(ins, outs, send_sems, recv_sems):
            cp.start()

    def finish(ins, outs, send_sems, recv_sems):
        for cp in copies(ins, outs, send_sems, recv_sems):
            cp.wait()

    shapes = [jax.ShapeDtypeStruct((N_CHIPS,) + g.shape[2:], g.dtype) for g in grads]
    return _Side(grads, shapes, n, start, finish)


def _chip_exchange_side(chipsums):
    n = len(chipsums)

    def copies(ins, outs, send_sems, recv_sems):
        x, y, c = _position()
        return [_remote(send_sems, recv_sems, 3 * e + j, ins[e].at[2 * tx + ty], outs[e].at[j], (tx, ty, c))
                for j, (tx, ty) in enumerate(_other_chips(x, y)) for e in range(n)]

    def start(ins, outs, send_sems, recv_sems):
        for cp in copies(ins, outs, send_sems, recv_sems):
            cp.start()

    def finish(ins, outs, send_sems, recv_sems):
        for cp in copies(ins, outs, send_sems, recv_sems):
            cp.wait()

    shapes = [jax.ShapeDtypeStruct((3,) + cs.shape[1:], cs.dtype) for cs in chipsums]
    return _Side(chipsums, shapes, 3 * n, start, finish)


def _share_side(totals):
    n = len(totals)

    def copies(ins, outs, send_sems, recv_sems):
        x, y, c = _position()
        return [_remote(send_sems, recv_sems, e, ins[e], outs[e], (x, y, 1 - c)) for e in range(n)]

    def start(ins, outs, send_sems, recv_sems):
        for cp in copies(ins, outs, send_sems, recv_sems):
            cp.start()

    def finish(ins, outs, send_sems, recv_sems):
        for cp in copies(ins, outs, send_sems, recv_sems):
            cp.wait()

    return _Side(totals, [jax.ShapeDtypeStruct(t.shape, t.dtype) for t in totals], n, start, finish)


def _reduce_rows(h):
    return h if h <= 704 else h // 2


def _add_sibling(grad, recv, c_idx, *, name):
    _, _, h, cw = grad.shape
    th = _reduce_rows(h)

    def body(c_ref, g_ref, r_ref, o_ref):
        o_ref[...] = (g_ref[...] + r_ref[...]).astype(BF16)

    return pl.pallas_call(
        body, name=name,
        grid_spec=pltpu.PrefetchScalarGridSpec(
            num_scalar_prefetch=1, grid=(N_CHIPS, h // th),
            in_specs=[pl.BlockSpec((None, None, th, cw), lambda s, i, c_ref: (s, c_ref[0], i, 0)),
                      pl.BlockSpec((None, th, cw), lambda s, i, c_ref: (s, i, 0))],
            out_specs=pl.BlockSpec((None, th, cw), lambda s, i, c_ref: (s, i, 0))),
        out_shape=jax.ShapeDtypeStruct((N_CHIPS, h, cw), BF16),
        compiler_params=_params("parallel", "parallel"),
    )(c_idx, grad, recv)


def _add_chips(chipsum, recv, s_idx, *, name):
    _, h, cw = chipsum.shape
    th = _reduce_rows(h)

    def body(s_ref, own_ref, r_ref, o_ref):
        o_ref[...] = ((own_ref[...].astype(F32) + r_ref[0].astype(F32)) + r_ref[1].astype(F32)) + r_ref[2].astype(F32)

    return pl.pallas_call(
        body, name=name,
        grid_spec=pltpu.PrefetchScalarGridSpec(
            num_scalar_prefetch=1, grid=(h // th,),
            in_specs=[pl.BlockSpec((None, th, cw), lambda i, s_ref: (s_ref[0], i, 0)),
                      pl.BlockSpec((3, th, cw), lambda i, s_ref: (0, i, 0))],
            out_specs=pl.BlockSpec((th, cw), lambda i, s_ref: (i, 0))),
        out_shape=jax.ShapeDtypeStruct((h, cw), F32),
        compiler_params=_params("parallel"),
    )(s_idx, chipsum, recv)


def _adamw_math(w, g, m, v):
    m = ADAM_B1 * m + (1.0 - ADAM_B1) * g
    v = ADAM_B2 * v + (1.0 - ADAM_B2) * (g * g)
    m_hat = m / (1.0 - ADAM_B1 ** ADAM_STEP)
    v_hat = v / (1.0 - ADAM_B2 ** ADAM_STEP)
    delta = -ADAM_LR * (m_hat / (jnp.sqrt(v_hat) + ADAM_EPS) + ADAM_WD * w)
    return delta, m, v


ADAM_TILE_ELEMS = 256 * 1024


def _adamw(w, g, m, v, *, name):
    layers, rows, cols = w.shape
    tr = rows
    for cand in range(8, rows, 8):
        if rows % cand == 0 and cand * cols <= ADAM_TILE_ELEMS:
            tr = cand
    if rows * cols <= ADAM_TILE_ELEMS:
        tr = rows

    def body(w_ref, g_ref, m_ref, v_ref, d_ref, nm_ref, nv_ref):
        d, nm, nv = _adamw_math(w_ref[...], g_ref[...], m_ref[...], v_ref[...])
        d_ref[...] = d
        nm_ref[...] = nm
        nv_ref[...] = nv

    blk = pl.BlockSpec((None, tr, cols), lambda l, i: (l, i, 0))
    sds = jax.ShapeDtypeStruct(w.shape, F32)
    return pl.pallas_call(
        body, name=name, grid=(layers, rows // tr), in_specs=[blk] * 4, out_specs=[blk] * 3, out_shape=[sds] * 3,
        compiler_params=_params("parallel", "parallel"),
    )(w, g, m, v)


SMALL_LAYOUT = (("loss", 1), ("norm_mix", 16), ("norm_ffn", 16), ("ssm_conv_b", 24), ("ssm_dt_bias", 1),
                ("ssm_a_log", 1), ("ssm_d_skip", 1), ("ssm_norm_w", 16), ("att_q_norm", 1), ("att_k_norm", 1),
                ("conv_w_full", 96))
SMALL_ROWS = 176
N_DEVICES = 8


def _small_packs(dicts):
    parts = []
    for values in dicts:
        for name, rows in SMALL_LAYOUT:
            flat = values[name].reshape(-1).astype(F32)
            parts.append(jnp.pad(flat, (0, rows * LANES - flat.shape[0])).reshape(rows, LANES))
        used = sum(r for _, r in SMALL_LAYOUT)
        parts.append(jnp.zeros((SMALL_ROWS - used, LANES), F32))
    return jnp.concatenate(parts, axis=0).reshape(len(dicts), SMALL_ROWS, LANES)


def _small_unpack(pack, shapes):
    out, off = {}, 0
    for name, rows in SMALL_LAYOUT:
        shape = shapes[name]
        n = math.prod(shape)
        out[name] = pack[off:off + rows].reshape(-1)[:n].reshape(shape)
        off += rows
    return out


def _small_allreduce_adamw(g, w, m, v):
    def body(g_ref, w_ref, m_ref, v_ref, gs_ref, d_ref, nm_ref, nv_ref, buf, send_sems, recv_sems):
        x, y, c = _position()
        pos = (x, y, c)
        me = 4 * x + 2 * y + c
        buf[me] = g_ref[...]
        peers = []
        for k in range(1, N_DEVICES):
            bits = ((k >> 2) & 1, (k >> 1) & 1, k & 1)
            peers.append(tuple(1 - p if b else p for p, b in zip(pos, bits)))
        cps = [pltpu.make_async_remote_copy(src_ref=g_ref, dst_ref=buf.at[me], send_sem=send_sems.at[k],
                                            recv_sem=recv_sems.at[k], device_id=peer, device_id_type=MESH)
               for k, peer in enumerate(peers)]
        for cp in cps:
            cp.start()
        for k, (px, py, pc) in enumerate(peers):
            pltpu.make_async_remote_copy(src_ref=g_ref, dst_ref=buf.at[4 * px + 2 * py + pc],
                                         send_sem=send_sems.at[k], recv_sem=recv_sems.at[k],
                                         device_id=(px, py, pc), device_id_type=MESH).wait_recv()
        for cp in cps:
            cp.wait_send()
        total = buf[0]
        for dev in range(1, N_DEVICES):
            total = total + buf[dev]
        gs_ref[...] = total
        d, nm, nv = _adamw_math(w_ref[...], total, m_ref[...], v_ref[...])
        d_ref[...] = d
        nm_ref[...] = nm
        nv_ref[...] = nv

    vm = pl.BlockSpec(memory_space=pltpu.VMEM)
    sds = jax.ShapeDtypeStruct((SMALL_ROWS, LANES), F32)
    return pl.pallas_call(
        body, name="small_allreduce_adamw", in_specs=[vm] * 4, out_specs=[vm] * 4, out_shape=[sds] * 4,
        scratch_shapes=[pltpu.VMEM((N_DEVICES, SMALL_ROWS, LANES), F32),
                        pltpu.SemaphoreType.DMA((N_DEVICES - 1,)), pltpu.SemaphoreType.DMA((N_DEVICES - 1,))],
    )(g, w, m, v)


SMALL = tuple(n for n, _ in SMALL_LAYOUT if n not in ("loss", "conv_w_full"))
WEIGHTS = ("norm_mix", "norm_ffn", "ssm_w_in", "ssm_conv_w", "ssm_conv_b", "ssm_dt_bias", "ssm_a_log", "ssm_d_skip",
           "ssm_norm_w", "ssm_w_out", "att_w_qkv", "att_q_norm", "att_k_norm", "att_w_o", "ffn_w_gate", "ffn_w_up",
           "ffn_w_down", "ple_w_proj", "ple_w_gate")
COLUMN_SHARDED = ("ssm_w_in", "att_w_qkv", "ffn_w_gate", "ffn_w_up", "ple_w_proj")
LAYERED = ("ffn_w_gate", "ffn_w_up", "ffn_w_down", "ple_w_proj", "ple_w_gate")
UPDATED_TRANSPOSED = ("ssm_w_in", "ffn_w_gate", "ffn_w_up")
GATHER_ORDER = ("ssm_w_in", "ssm_w_out", "att_w_qkv", "att_w_o", "ffn_w_gate", "ffn_w_up", "ffn_w_down",
                "ple_w_proj", "ple_w_gate")


def _layers(n):
    return (0, 1) if n in LAYERED else (None,)


def _tag(key):
    return key[0] if key[1] is None else f"{key[0]}_{key[1]}"


QKV_PARTS = 3


def _weight_slab(w, key):
    n, i = key
    if n == "att_w_qkv":
        a = w[n][0].T
        rows = a.shape[0] // QKV_PARTS
        a = a[i * rows:(i + 1) * rows]
    else:
        a = w[n][0 if i is None else i]
        a = a.T if n in COLUMN_SHARDED else a
    if n == "ssm_w_in":
        a = jnp.pad(a, ((0, W_IN_SLAB_ROWS - a.shape[0]), (0, 0)))
    return a.reshape(2, a.shape[0] // 2, a.shape[1]).astype(BF16)


def _install(prm, key, gathered, own, s_me):
    n, i = key
    full = lax.dynamic_update_slice(gathered, own[None], (s_me, 0, 0, 0))
    full = full.reshape(N_CHIPS, 2 * full.shape[2], full.shape[3])
    if n == "att_w_qkv":
        parts = prm.setdefault("att_w_qkv_parts", {})
        parts[i] = full
        if len(parts) == QKV_PARTS:
            prm[n] = jnp.stack([parts[j] for j in range(QKV_PARTS)], axis=1).reshape(-1, D_MODEL)
        return
    if n == "ssm_w_in":
        rows = (D_INNER + CONV_DIM + SSM_HEADS) // N_CHIPS
        w_in_t = full[:, :rows].reshape(N_CHIPS * rows, D_MODEL)
        prm["ssm_w_z"] = w_in_t[:D_INNER]
        prm["ssm_w_xbc"] = w_in_t[D_INNER:D_INNER + CONV_DIM]
        prm["ssm_w_dt"] = jnp.pad(w_in_t[D_INNER + CONV_DIM:], ((0, LANES - SSM_HEADS), (0, 0)))
        return
    full = full.reshape(N_CHIPS * full.shape[1], full.shape[2])
    if i is None:
        prm[n] = full
    else:
        prm.setdefault(n, [None, None])[i] = full


def _grad_slab(grads, key):
    n, i = key
    g = grads[n] if i is None else grads[n][i]
    if n == "ssm_w_in":
        g = jnp.pad(g.reshape(N_CHIPS, g.shape[0] // N_CHIPS, D_MODEL),
                    ((0, 0), (0, W_IN_SLAB_ROWS - g.shape[0] // N_CHIPS), (0, 0)))
    rows = g.size // (N_CHIPS * g.shape[-1])
    return g.reshape(N_CHIPS, 2, rows // 2, g.shape[-1])


def _natural_shard(n, reduced, shape):
    def one(r):
        if n == "ssm_w_in":
            r = r[:shape[-1]]
        return r.T if n in COLUMN_SHARDED else r
    if n in LAYERED:
        return jnp.stack([one(r) for r in reduced]).reshape(shape)
    return one(reduced[0]).reshape(shape)


def kernel(x, p, norm_mix, norm_ffn, ssm_w_in, ssm_conv_w, ssm_conv_b, ssm_dt_bias, ssm_a_log, ssm_d_skip, ssm_norm_w, ssm_w_out, att_w_qkv, att_q_norm, att_k_norm, att_w_o, ffn_w_gate, ffn_w_up, ffn_w_down, ple_w_proj, ple_w_gate, loss_target, m_norm_mix, m_norm_ffn, m_ssm_w_in, m_ssm_conv_w, m_ssm_conv_b, m_ssm_dt_bias, m_ssm_a_log, m_ssm_d_skip, m_ssm_norm_w, m_ssm_w_out, m_att_w_qkv, m_att_q_norm, m_att_k_norm, m_att_w_o, m_ffn_w_gate, m_ffn_w_up, m_ffn_w_down, m_ple_w_proj, m_ple_w_gate, v_norm_mix, v_norm_ffn, v_ssm_w_in, v_ssm_conv_w, v_ssm_conv_b, v_ssm_dt_bias, v_ssm_a_log, v_ssm_d_skip, v_ssm_norm_w, v_ssm_w_out, v_att_w_qkv, v_att_q_norm, v_att_k_norm, v_att_w_o, v_ffn_w_gate, v_ffn_w_up, v_ffn_w_down, v_ple_w_proj, v_ple_w_gate):
    given = dict(locals())
    w = {n: given[n] for n in WEIGHTS}
    m = {n: given["m_" + n] for n in WEIGHTS}
    v = {n: given["v_" + n] for n in WEIGHTS}
    c_idx = lax.axis_index("c").astype(jnp.int32).reshape(1)
    s_idx = (2 * lax.axis_index("x") + lax.axis_index("y")).astype(jnp.int32).reshape(1)

    s_me = 2 * lax.axis_index("x") + lax.axis_index("y")
    first_core = lax.axis_index("c") == 0

    qkv_parts = [("att_w_qkv", j) for j in range(QKV_PARTS)]
    gather_plan = {
        "ssm_in_z": [("ssm_w_out", None)],
        "ssm_in_xbc": [("ffn_w_gate", 0)],
        "conv_fwd": [("ffn_w_up", 0)],
        "ssd_fwd": [("ffn_w_down", 0), ("ple_w_proj", 0), ("ple_w_gate", 0), ("att_w_o", None)],
        "swiglu_fwd_0": qkv_parts[:2],
        "ffn_down_0": qkv_parts[2:],
        "att_qkv": [(n, 1) for n in LAYERED],
    }
    mamba = [("ssm_w_in", None)]
    own = {k: _weight_slab(w, k) for k in mamba + sum(gather_plan.values(), [])}
    prm = {n: w[n] for n in SMALL}

    def land(group, outputs):
        for k, g in zip(group, outputs):
            _install(prm, k, g, own[k], s_me)

    first = _gather_side([own[k] for k in mamba], whole=[ssm_conv_w[0]])
    _run_side(first, "gather_mamba")
    land(mamba, first.outputs)
    conv = lax.dynamic_update_slice(first.outputs[-1], ssm_conv_w, (s_me, 0, 0))
    prm["ssm_conv_w"] = conv.transpose(1, 0, 2).reshape(CONV_WIDTH, CONV_DIM)

    ffn1 = [(n, 1) for n in LAYERED]
    attention = [("att_w_qkv", None), ("att_w_o", None)]
    ffn0 = [(n, 0) for n in LAYERED] + [("ssm_w_out", None)]
    reduce_plan = {"att_out_dx": [("swap", ffn1)], "att_qkv_dx": [("exchange", ffn1)],
                   "swiglu_bwd_0": [("swap", attention)], "gate_norm_bwd": [("swap", ffn0)],
                   "ssd_bwd": [("exchange", attention), ("exchange", ffn0)],
                   "ssm_dh_z": [("swap", mamba)], "ssm_dh_xbc": [("exchange", mamba)]}
    state = {}

    def swap_side(group):
        state[_tag(group[0]), "g4"] = g4 = [_grad_slab(state["grads"], k) for k in group]
        return _swap_side(g4)

    def add_siblings(group, from_sibling):
        state[_tag(group[0]), "chipsums"] = [
            _add_sibling(g, r, c_idx, name="add_sibling_" + _tag(k))
            for g, r, k in zip(state[_tag(group[0]), "g4"], from_sibling, group)]

    def exchange_side(group):
        return _chip_exchange_side(state[_tag(group[0]), "chipsums"])

    def add_chips(group, from_chips):
        for k, cs, r in zip(group, state[_tag(group[0]), "chipsums"], from_chips):
            state["total", k] = _add_chips(cs, r, s_idx, name="add_chips_" + _tag(k))

    class Plan(_NoOverlap):
        def __init__(self):
            self.carried = {host: _gather_side([own[k] for k in group]) for host, group in gather_plan.items()}

        def begin_backward(self, grads):
            state["grads"] = grads

        def side(self, host):
            if host in reduce_plan:
                self.parts = [swap_side(group) if step == "swap" else exchange_side(group)
                              for step, group in reduce_plan[host]]
                self.carried[host] = _sides_together(self.parts)
            elif host == share_host:
                self.carried[host] = _share_side([state["total", k] for k in order])
            return self.carried.get(host)

        def after(self, host):
            if host in gather_plan:
                land(gather_plan[host], self.carried[host].outputs)
            elif host in reduce_plan:
                _share_out(self.carried[host], self.parts)
                for (step, group), part in zip(reduce_plan[host], self.parts):
                    (add_siblings if step == "swap" else add_chips)(group, part.outputs)
            elif host == share_host:
                state["shared"] = self.carried[host].outputs

    order = mamba + ffn0 + attention + ffn1
    share_host = "ssm_dh_dt"
    loss_row, dx, grads = _local_step(x[0], p[:, 0], loss_target[0], prm, Plan())

    reduced = {}
    for k, theirs in zip(order, state["shared"]):
        lo = jnp.where(first_core, state["total", k], theirs)
        hi = jnp.where(first_core, theirs, state["total", k])
        reduced.setdefault(k[0], {})[k[1]] = jnp.concatenate([lo, hi], axis=0)
    reduced = {n: [by_layer[i] for i in _layers(n)] for n, by_layer in reduced.items()}

    grad, delta, new_m, new_v = {}, {}, {}, {}
    for n in GATHER_ORDER:
        if n in UPDATED_TRANSPOSED:
            flip = lambda a: a.transpose(0, 2, 1)
            cols = w[n].shape[-1]
            g_t = jnp.stack([r[:cols] for r in reduced[n]])
            grad[n] = flip(g_t)
            delta[n], new_m[n], new_v[n] = [flip(o) for o in _adamw(flip(w[n]), g_t, flip(m[n]), flip(v[n]),
                                                                    name="adamw_" + n)]
            continue
        grad[n] = _natural_shard(n, reduced[n], w[n].shape)
        delta[n], new_m[n], new_v[n] = _adamw(w[n], grad[n], m[n], v[n], name="adamw_" + n)

    small_g = {n: (jnp.stack(grads[n]) if isinstance(grads[n], list) else grads[n]) for n in SMALL}
    small_g["loss"] = loss_row
    small_g["conv_w_full"] = grads["ssm_conv_w"]
    zero = {"loss": jnp.zeros((1, LANES), F32), "conv_w_full": jnp.zeros((CONV_WIDTH, CONV_DIM), F32)}
    packs = _small_packs([small_g, {**w, **zero}, {**m, **zero}, {**v, **zero}])
    outs = _small_allreduce_adamw(packs[0], packs[1], packs[2], packs[3])
    shapes = {n: w[n].shape for n in SMALL}
    shapes["loss"] = (1, LANES)
    shapes["conv_w_full"] = (CONV_WIDTH, CONV_DIM)
    sg, sd, sm, sv = [_small_unpack(o, shapes) for o in outs]
    for n in SMALL:
        grad[n], delta[n], new_m[n], new_v[n] = sg[n], sd[n], sm[n], sv[n]
    loss = sg["loss"][0, 0]
    conv_cols = CONV_DIM // N_CHIPS
    grad["ssm_conv_w"] = lax.dynamic_slice(sg["conv_w_full"], (0, s_me * conv_cols), (CONV_WIDTH, conv_cols))[None]
    delta["ssm_conv_w"], new_m["ssm_conv_w"], new_v["ssm_conv_w"] = _adamw(
        ssm_conv_w, grad["ssm_conv_w"], m_ssm_conv_w, v_ssm_conv_w, name="adamw_ssm_conv_w")

    return (loss, dx[None], *[grad[n] for n in WEIGHTS], *[delta[n] for n in WEIGHTS],
            *[new_m[n] for n in WEIGHTS], *[new_v[n] for n in WEIGHTS])
```

```python
import math

import jax
import jax.numpy as jnp
from jax import lax
from jax.experimental import pallas as pl
from jax.experimental.pallas import tpu as pltpu

F32 = jnp.float32
BF16 = jnp.bfloat16
HIGHEST = lax.Precision.HIGHEST

NORM_EPS = 1e-6
ADAM_LR, ADAM_B1, ADAM_B2, ADAM_EPS, ADAM_WD, ADAM_STEP = 0.001, 0.9, 0.999, 1e-08, 0.01, 10

D_MODEL = 1024
D_INNER = 2048
SSM_HEADS = 32
SSM_HEAD_DIM = 64
SSM_GROUPS = 4
SSM_STATE = 128
SSD_CHUNK = 128
CONV_DIM = 3072
CONV_WIDTH = 4
ATT_HEADS = 16
ATT_HEAD_DIM = 64
DIL_PATTERNS = ((128, 1), (512, 4), (2048, 16))
ATT_BLOCK = 128
FFN_HIDDEN = 2816
PLE_DIM = 256

LANES = 128
V7X_VMEM_LIMIT = 56 * 1024 * 1024
NEG_BIG = -1e30

N_CHIPS = 4


def _params(*sem):
    return pltpu.CompilerParams(dimension_semantics=sem, vmem_limit_bytes=V7X_VMEM_LIMIT)


def _tile(n, pref):
    if n <= pref:
        return n
    best = None
    for t in range(LANES, pref + 1, LANES):
        if n % t == 0:
            best = t
    assert best is not None, (n, pref)
    return best


def _sigmoid(v):
    return 1.0 / (1.0 + jnp.exp(-v))


def _dot(a, b):
    return jnp.dot(a, b, preferred_element_type=F32)


def _dot_nt(a, b):
    return lax.dot_general(a, b, (((1,), (1,)), ((), ())), preferred_element_type=F32)


def _dot_tn(a, b):
    return lax.dot_general(a, b, (((0,), (0,)), ((), ())), preferred_element_type=F32)


def _head_block_diag():
    i = lax.broadcasted_iota(jnp.int32, (LANES, LANES), 0) // ATT_HEAD_DIM
    j = lax.broadcasted_iota(jnp.int32, (LANES, LANES), 1) // ATT_HEAD_DIM
    return (i == j).astype(BF16)


def _split_dot(ones, z):
    hi = z.astype(BF16)
    lo = (z - hi.astype(F32)).astype(BF16)
    return _dot(ones, hi) + _dot(ones, lo)


def _head_sums(z, bd, terms=2):
    hi = z.astype(BF16)
    lo = (z - hi.astype(F32)).astype(BF16) if terms == 2 else None
    parts = []
    for t in range(z.shape[1] // LANES):
        sl = slice(t * LANES, (t + 1) * LANES)
        part = _dot(hi[:, sl], bd)
        parts.append(part + _dot(lo[:, sl], bd) if terms == 2 else part)
    return parts[0] if len(parts) == 1 else jnp.concatenate(parts, axis=1)


def _lane_lt64(rows):
    return lax.broadcasted_iota(jnp.int32, (rows, LANES), 1) < ATT_HEAD_DIM


MESH = pl.DeviceIdType.MESH
ANY = pl.BlockSpec(memory_space=pl.ANY)


class _Side:
    def __init__(self, inputs, out_shapes, n_sems, start, finish):
        self.inputs, self.out_shapes, self.n_sems = list(inputs), list(out_shapes), n_sems
        self.start, self.finish = start, finish
        self.outputs = None


class _SemaphoresFrom:
    def __init__(self, sems, first):
        self.sems, self.first = sems, first

    @property
    def at(self):
        return self

    def __getitem__(self, k):
        return self.sems.at[self.first + k]


def _sides_together(sides):
    def run(step):
        def both(ins, outs, send_sems, recv_sems):
            i = o = k = 0
            for s in sides:
                ni, no = len(s.inputs), len(s.out_shapes)
                getattr(s, step)(ins[i:i + ni], outs[o:o + no], _SemaphoresFrom(send_sems, k),
                                 _SemaphoresFrom(recv_sems, k))
                i, o, k = i + ni, o + no, k + s.n_sems
        return both

    return _Side(sum([s.inputs for s in sides], []), sum([s.out_shapes for s in sides], []),
                 sum(s.n_sems for s in sides), run("start"), run("finish"))


def _share_out(together, sides):
    o = 0
    for s in sides:
        s.outputs = together.outputs[o:o + len(s.out_shapes)]
        o += len(s.out_shapes)


def _call(body, side, *, name, grid, in_specs, out_specs, out_shape, scratch_shapes, semantics, args):
    in_specs, out_specs, out_shape = list(in_specs), list(out_specs), list(out_shape)
    scratch_shapes = list(scratch_shapes)
    if side is None:
        return pl.pallas_call(body, name=name, grid=grid, in_specs=in_specs, out_specs=out_specs,
                              out_shape=out_shape, scratch_shapes=scratch_shapes,
                              compiler_params=_params(*semantics))(*args)
    ni, no, ns = len(in_specs), len(out_specs), len(scratch_shapes)
    si, so = len(side.inputs), len(side.out_shapes)

    def hosted(*refs):
        ins, s_ins = refs[:ni], refs[ni:ni + si]
        outs, s_outs = refs[ni + si:ni + si + no], refs[ni + si + no:ni + si + no + so]
        scratch = refs[ni + si + no + so:ni + si + no + so + ns]
        send_sems, recv_sems = refs[-2], refs[-1]
        first = pl.program_id(0) == 0
        last = pl.program_id(0) == grid[0] - 1
        for axis in range(1, len(grid)):
            first = jnp.logical_and(first, pl.program_id(axis) == 0)
            last = jnp.logical_and(last, pl.program_id(axis) == grid[axis] - 1)

        @pl.when(first)
        def _():
            side.start(s_ins, s_outs, send_sems, recv_sems)

        body(*ins, *outs, *scratch)

        @pl.when(last)
        def _():
            side.finish(s_ins, s_outs, send_sems, recv_sems)

    res = pl.pallas_call(
        hosted, name=name, grid=grid, in_specs=in_specs + [ANY] * si, out_specs=out_specs + [ANY] * so,
        out_shape=out_shape + side.out_shapes,
        scratch_shapes=scratch_shapes + [pltpu.SemaphoreType.DMA((side.n_sems,)),
                                         pltpu.SemaphoreType.DMA((side.n_sems,))],
        compiler_params=_params(*["arbitrary"] * len(grid)),
    )(*args, *side.inputs)
    side.outputs = list(res[no:])
    return list(res[:no])


def _matmul(a, b, *, mode, name, out_dtype=F32, addend=None, tm=1024, tn=512, tk_max=3072, side=None, second=None):
    m, k = a.shape
    if mode == "nn":
        k2, n = b.shape
    else:
        n, k2 = b.shape
    assert k == k2, (a.shape, b.shape, mode)
    tm, tn, tk = _tile(m, tm), _tile(n, tn), _tile(k, tk_max)
    nk = k // tk
    has_add = addend is not None
    n_rows = len(second[1]) if second else 0
    n_out = 2 if second else 1

    def body(*refs):
        a_ref, b_ref = refs[0], refs[1]
        add_ref = refs[2] if has_add else None
        row_refs = refs[2 + has_add:2 + has_add + n_rows]
        o_ref, acc_ref = refs[-1 - n_out], refs[-1]
        kk = pl.program_id(2)
        col_tile = pl.program_id(1)
        av = a_ref[...].astype(BF16)
        bv = b_ref[...].astype(BF16)
        part = _dot(av, bv) if mode == "nn" else _dot_nt(av, bv)

        @pl.when(kk == 0)
        def _():
            acc_ref[...] = part

        @pl.when(kk > 0)
        def _():
            acc_ref[...] += part

        @pl.when(kk == nk - 1)
        def _():
            res = acc_ref[...]
            if has_add:
                res = res + add_ref[...]
            o_ref[...] = res.astype(out_dtype)
            if second:
                refs[-2][...] = second[0](res, col_tile, *row_refs).astype(second[2])

    a_spec = pl.BlockSpec((tm, tk), lambda i, j, kk: (i, kk))
    if mode == "nn":
        b_spec = pl.BlockSpec((tk, tn), lambda i, j, kk: (kk, j))
    else:
        b_spec = pl.BlockSpec((tn, tk), lambda i, j, kk: (j, kk))
    tile = pl.BlockSpec((tm, tn), lambda i, j, kk: (i, j))
    in_specs = [a_spec, b_spec]
    args = [a, b]
    if has_add:
        in_specs.append(tile)
        args.append(addend)
    if second:
        in_specs += [pl.BlockSpec((1, tn), lambda i, j, kk: (0, j))] * n_rows
        args += list(second[1])
    outs = _call(
        body, side, name=name, grid=(m // tm, n // tn, nk),
        in_specs=in_specs, out_specs=[tile] * n_out,
        out_shape=[jax.ShapeDtypeStruct((m, n), out_dtype)] + ([jax.ShapeDtypeStruct((m, n), second[2])] if second
                                                                 else []),
        scratch_shapes=[pltpu.VMEM((tm, tn), F32)],
        semantics=("parallel", "parallel", "arbitrary"), args=args,
    )
    return outs if second else outs[0]


def _matmul_tn(a, b, *, name, tm=1408, tn=512, tk=1024):
    t, m = a.shape
    t2, n = b.shape
    assert t == t2
    tm, tn, tk = _tile(m, tm), _tile(n, tn), _tile(t, tk)

    def body(a_ref, b_ref, o_ref):
        part = _dot_tn(a_ref[...].astype(BF16), b_ref[...].astype(BF16))

        @pl.when(pl.program_id(2) == 0)
        def _():
            o_ref[...] = part

        @pl.when(pl.program_id(2) > 0)
        def _():
            o_ref[...] += part

    return pl.pallas_call(
        body, name=name, grid=(m // tm, n // tn, t // tk),
        in_specs=[pl.BlockSpec((tk, tm), lambda i, j, kk: (kk, i)),
                  pl.BlockSpec((tk, tn), lambda i, j, kk: (kk, j))],
        out_specs=pl.BlockSpec((tm, tn), lambda i, j, kk: (i, j)),
        out_shape=jax.ShapeDtypeStruct((m, n), F32),
        compiler_params=_params("parallel", "parallel", "arbitrary"),
    )(a, b)


def _rmsnorm_rows(tile, j, gain_ref):
    r = lax.rsqrt(jnp.mean(tile * tile, axis=-1, keepdims=True) + NORM_EPS)
    return tile * r * gain_ref[...]


def _rmsnorm_fwd(x, gain, *, name):
    t, d = x.shape
    tm = _tile(t, 512)

    def body(x_ref, g_ref, o_ref):
        xv = x_ref[...]
        r = lax.rsqrt(jnp.mean(xv * xv, axis=-1, keepdims=True) + NORM_EPS)
        o_ref[...] = (xv * r * g_ref[...]).astype(BF16)

    return pl.pallas_call(
        body, name=name, grid=(t // tm,),
        in_specs=[pl.BlockSpec((tm, d), lambda i: (i, 0)), pl.BlockSpec((1, d), lambda i: (0, 0))],
        out_specs=pl.BlockSpec((tm, d), lambda i: (i, 0)),
        out_shape=jax.ShapeDtypeStruct((t, d), BF16),
        compiler_params=_params("parallel"),
    )(x, gain)


def _matmul_rmsnorm_bwd(a, b, addend, x, gain, dres, *, name, side=None, tm=512, tk_max=3072):
    m, k = a.shape
    d = b.shape[1]
    tm, tk = _tile(m, tm), _tile(k, tk_max)
    nk = k // tk

    def body(a_ref, b_ref, *rest):
        add_ref = rest[0] if addend is not None else None
        x_ref, g_ref, dres_ref, dx_ref, dg_ref, acc_ref = rest[-6:]
        i, kk = pl.program_id(0), pl.program_id(1)
        part = _dot(a_ref[...].astype(BF16), b_ref[...].astype(BF16))

        @pl.when(kk == 0)
        def _():
            acc_ref[...] = part

        @pl.when(kk > 0)
        def _():
            acc_ref[...] += part

        @pl.when(kk == nk - 1)
        def _():
            dyv = acc_ref[...] if addend is None else acc_ref[...] + add_ref[...]
            xv = x_ref[...]
            r = lax.rsqrt(jnp.mean(xv * xv, axis=-1, keepdims=True) + NORM_EPS)
            xh = xv * r
            dxh = dyv * g_ref[...]
            mean = jnp.mean(dxh * xh, axis=-1, keepdims=True)
            dx_ref[...] = dres_ref[...] + r * (dxh - xh * mean)
            gain_part = jnp.sum(dyv * xh, axis=0, keepdims=True)

            @pl.when(i == 0)
            def _():
                dg_ref[...] = gain_part

            @pl.when(i > 0)
            def _():
                dg_ref[...] += gain_part

    row = pl.BlockSpec((tm, d), lambda i, kk: (i, 0))
    vec = pl.BlockSpec((1, d), lambda i, kk: (0, 0))
    return _call(
        body, side, name=name, grid=(m // tm, nk),
        in_specs=[pl.BlockSpec((tm, tk), lambda i, kk: (i, kk)), pl.BlockSpec((tk, d), lambda i, kk: (kk, 0))]
        + ([row] if addend is not None else []) + [row, vec, row],
        out_specs=[row, vec],
        out_shape=[jax.ShapeDtypeStruct((m, d), F32), jax.ShapeDtypeStruct((1, d), F32)],
        scratch_shapes=[pltpu.VMEM((tm, d), F32)],
        semantics=("arbitrary", "arbitrary"),
        args=(a, b) + ((addend,) if addend is not None else ()) + (x, gain, dres),
    )


def _swiglu_fwd(h, w_gate_t, w_up_t, *, name, side=None):
    t, d = h.shape
    f = w_gate_t.shape[0]
    tm, tn = _tile(t, 1024), _tile(f, 256)

    def body(h_ref, wg_ref, wu_ref, g_ref, u_ref, a_ref):
        hv = h_ref[...]
        g = _dot_nt(hv, wg_ref[...])
        u = _dot_nt(hv, wu_ref[...])
        g_ref[...] = g.astype(BF16)
        u_ref[...] = u.astype(BF16)
        a_ref[...] = (g * _sigmoid(g) * u).astype(BF16)

    wspec = pl.BlockSpec((tn, d), lambda i, j: (j, 0))
    ospec = pl.BlockSpec((tm, tn), lambda i, j: (i, j))
    return _call(
        body, side, name=name, grid=(t // tm, f // tn),
        in_specs=[pl.BlockSpec((tm, d), lambda i, j: (i, 0)), wspec, wspec],
        out_specs=[ospec, ospec, ospec],
        out_shape=[jax.ShapeDtypeStruct((t, f), BF16), jax.ShapeDtypeStruct((t, f), BF16),
                   jax.ShapeDtypeStruct((t, f), BF16)],
        scratch_shapes=[], semantics=("parallel", "parallel"), args=(h, w_gate_t, w_up_t),
    )


def _swiglu_bwd(dx, w_down, g, u, *, name, side=None):
    t, d = dx.shape
    f = w_down.shape[0]
    tm, tn = _tile(t, 1024), _tile(f, 256)

    def body(dx_ref, wd_ref, g_ref, u_ref, dg_ref, du_ref):
        dact = _dot_nt(dx_ref[...].astype(BF16), wd_ref[...])
        gv, uv = g_ref[...].astype(F32), u_ref[...].astype(F32)
        sg = _sigmoid(gv)
        dg_ref[...] = (dact * uv * sg * (1.0 + gv * (1.0 - sg))).astype(BF16)
        du_ref[...] = (dact * gv * sg).astype(BF16)

    ospec = pl.BlockSpec((tm, tn), lambda i, j: (i, j))
    return _call(
        body, side, name=name, grid=(t // tm, f // tn),
        in_specs=[pl.BlockSpec((tm, d), lambda i, j: (i, 0)), pl.BlockSpec((tn, d), lambda i, j: (j, 0)),
                  ospec, ospec],
        out_specs=[ospec, ospec],
        out_shape=[jax.ShapeDtypeStruct((t, f), BF16), jax.ShapeDtypeStruct((t, f), BF16)],
        scratch_shapes=[], semantics=("parallel", "parallel"), args=(dx, w_down, g, u),
    )


def _ple_fwd(x, p, w_gate, w_proj_t, *, name, next_gain=None, target=None):
    t, d = x.shape
    e = p.shape[1]
    tm = _tile(t, 512)
    steps = t // tm

    def body(x_ref, p_ref, wg_ref, wp_ref, *rest):
        xv = x_ref[...]
        s = _dot(xv.astype(BF16), wg_ref[...])
        ple = _dot_nt(p_ref[...].astype(BF16), wp_ref[...])
        y = xv + _sigmoid(s) * ple
        if target is None:
            gain_ref, y_ref, h_ref = rest
            y_ref[...] = y
            r = lax.rsqrt(jnp.mean(y * y, axis=-1, keepdims=True) + NORM_EPS)
            h_ref[...] = (y * r * gain_ref[...]).astype(BF16)
        else:
            t_ref, dy_ref, l_ref, acc_ref = rest
            err = y - t_ref[...]
            dy_ref[...] = err * (1.0 / d)
            part = jnp.sum(err * err, axis=0, keepdims=True)

            @pl.when(pl.program_id(0) == 0)
            def _():
                acc_ref[...] = part

            @pl.when(pl.program_id(0) > 0)
            def _():
                acc_ref[...] += part

            @pl.when(pl.program_id(0) == steps - 1)
            def _():
                l_ref[...] = jnp.full((1, LANES), (0.5 / d), F32) * jnp.sum(acc_ref[...])

    row = pl.BlockSpec((tm, d), lambda i: (i, 0))
    fixed = lambda shape: pl.BlockSpec(shape, lambda i: (0, 0))
    in_specs = [row, pl.BlockSpec((tm, e), lambda i: (i, 0)), fixed((d, d)), fixed((d, e))]
    if target is None:
        return pl.pallas_call(
            body, name=name, grid=(steps,), in_specs=in_specs + [fixed((1, d))], out_specs=[row, row],
            out_shape=[jax.ShapeDtypeStruct((t, d), F32), jax.ShapeDtypeStruct((t, d), BF16)],
            compiler_params=_params("parallel"),
        )(x, p, w_gate, w_proj_t, next_gain)
    return pl.pallas_call(
        body, name=name, grid=(steps,), in_specs=in_specs + [row], out_specs=[row, fixed((1, LANES))],
        out_shape=[jax.ShapeDtypeStruct((t, d), F32), jax.ShapeDtypeStruct((1, LANES), F32)],
        scratch_shapes=[pltpu.VMEM((1, d), F32)],
        compiler_params=_params("arbitrary"),
    )(x, p, w_gate, w_proj_t, target)


def _ple_bwd(x, p, w_gate, w_proj_t, dout, *, name):
    t, d = x.shape
    e = p.shape[1]
    tm = _tile(t, 512)

    def body(x_ref, p_ref, wg_ref, wp_ref, do_ref, ds_ref, dple_ref, dx_ref):
        wg = wg_ref[...]
        s = _dot(x_ref[...].astype(BF16), wg)
        ple = _dot_nt(p_ref[...].astype(BF16), wp_ref[...])
        gate = _sigmoid(s)
        dov = do_ref[...]
        dple_ref[...] = (dov * gate).astype(BF16)
        ds = (dov * ple * gate * (1.0 - gate)).astype(BF16)
        ds_ref[...] = ds
        dx_ref[...] = dov + _dot_nt(ds, wg)

    row = pl.BlockSpec((tm, d), lambda i: (i, 0))
    fixed = lambda shape: pl.BlockSpec(shape, lambda i: (0, 0))
    return pl.pallas_call(
        body, name=name, grid=(t // tm,),
        in_specs=[row, pl.BlockSpec((tm, e), lambda i: (i, 0)), fixed((d, d)), fixed((d, e)), row],
        out_specs=[row, row, row],
        out_shape=[jax.ShapeDtypeStruct((t, d), BF16), jax.ShapeDtypeStruct((t, d), BF16),
                   jax.ShapeDtypeStruct((t, d), F32)],
        compiler_params=_params("parallel"),
    )(x, p, w_gate, w_proj_t, dout)


CONV_TIME_TILE = 256
CONV_HALO = 8


def _conv_taps(ext, w):
    acc = ext[CONV_HALO:, :] * w[CONV_WIDTH - 1:CONV_WIDTH, :]
    shifted = [ext[CONV_HALO:, :]]
    for j in range(1, CONV_WIDTH):
        sh = pltpu.roll(ext, j, 0)[CONV_HALO:, :]
        shifted.append(sh)
        acc = acc + sh * w[CONV_WIDTH - 1 - j:CONV_WIDTH - j, :]
    return acc, shifted


def _conv_fwd(u, w, b, side=None):
    t, c = u.shape
    tc = _tile(c, 256)
    tt = CONV_TIME_TILE

    def body(u_ref, w_ref, b_ref, o_ref):
        wv, bv = w_ref[...], b_ref[...]

        def tile(start, ext):
            pre = _conv_taps(ext, wv)[0] + bv
            o_ref[pl.ds(start, tt), :] = pre * _sigmoid(pre)

        tile(0, jnp.concatenate([jnp.zeros((CONV_HALO, tc), F32), u_ref[0:tt, :]], axis=0))

        def loop(i, carry):
            start = pl.multiple_of(i * tt, tt)
            tile(start, u_ref[pl.ds(start - CONV_HALO, tt + CONV_HALO), :])
            return carry

        lax.fori_loop(1, t // tt, loop, 0)

    col = pl.BlockSpec((t, tc), lambda j: (0, j))
    return _call(
        body, side, name="conv_fwd", grid=(c // tc,),
        in_specs=[col, pl.BlockSpec((CONV_WIDTH, tc), lambda j: (0, j)), pl.BlockSpec((1, tc), lambda j: (0, j))],
        out_specs=[col], out_shape=[jax.ShapeDtypeStruct((t, c), F32)],
        scratch_shapes=[], semantics=("parallel",), args=(u, w, b),
    )[0]


def _conv_bwd(u, w, b, dact, side=None):
    t, c = u.shape
    tc = _tile(c, 256)
    tt = CONV_TIME_TILE

    def body(u_ref, w_ref, b_ref, da_ref, du_ref, dw_ref, db_ref, dpre_ref):
        wv, bv = w_ref[...], b_ref[...]

        def tile(start, ext, sums):
            acc, shifted = _conv_taps(ext, wv)
            pre = acc + bv
            sg = _sigmoid(pre)
            dpre = da_ref[pl.ds(start, tt), :] * (sg * (1.0 + pre * (1.0 - sg)))
            dpre_ref[pl.ds(start, tt), :] = dpre
            new = [sums[0] + jnp.sum(dpre, axis=0, keepdims=True)]
            for j in range(CONV_WIDTH):
                new.append(sums[1 + j] + jnp.sum(dpre * shifted[j], axis=0, keepdims=True))
            return tuple(new)

        zero = jnp.zeros((1, tc), F32)
        sums = tile(0, jnp.concatenate([jnp.zeros((CONV_HALO, tc), F32), u_ref[0:tt, :]], axis=0),
                    (zero,) * (1 + CONV_WIDTH))

        def loop(i, sums):
            start = pl.multiple_of(i * tt, tt)
            return tile(start, u_ref[pl.ds(start - CONV_HALO, tt + CONV_HALO), :], sums)

        sums = lax.fori_loop(1, t // tt, loop, sums)
        db_ref[...] = sums[0]
        dw_ref[...] = jnp.concatenate([sums[1 + (CONV_WIDTH - 1 - k)] for k in range(CONV_WIDTH)], axis=0)
        dpre_ref[pl.ds(t, CONV_HALO), :] = jnp.zeros((CONV_HALO, tc), F32)

        def loop2(i, carry):
            start = pl.multiple_of(i * tt, tt)
            ext = dpre_ref[pl.ds(start, tt + CONV_HALO), :]
            acc = ext[0:tt, :] * wv[CONV_WIDTH - 1:CONV_WIDTH, :]
            for j in range(1, CONV_WIDTH):
                acc = acc + pltpu.roll(ext, tt + CONV_HALO - j, 0)[0:tt, :] * wv[CONV_WIDTH - 1 - j:CONV_WIDTH - j, :]
            du_ref[pl.ds(start, tt), :] = acc.astype(BF16)
            return carry

        lax.fori_loop(0, t // tt, loop2, 0)

    col = pl.BlockSpec((t, tc), lambda j: (0, j))
    return _call(
        body, side, name="conv_bwd", grid=(c // tc,),
        in_specs=[col, pl.BlockSpec((CONV_WIDTH, tc), lambda j: (0, j)), pl.BlockSpec((1, tc), lambda j: (0, j)), col],
        out_specs=[col, pl.BlockSpec((CONV_WIDTH, tc), lambda j: (0, j)), pl.BlockSpec((1, tc), lambda j: (0, j))],
        out_shape=[jax.ShapeDtypeStruct((t, c), BF16), jax.ShapeDtypeStruct((CONV_WIDTH, c), F32),
                   jax.ShapeDtypeStruct((1, c), F32)],
        scratch_shapes=[pltpu.VMEM((t + CONV_HALO, tc), F32)],
        semantics=("parallel",), args=(u, w, b, dact),
    )


def _softplus(v):
    e = jnp.exp(-jnp.abs(v))
    w = 1.0 + e
    log1p = jnp.where(w == 1.0, e, jnp.log(w) * (e / jnp.where(w == 1.0, 1.0, w - 1.0)))
    return jnp.maximum(v, 0.0) + log1p


def _split3(z):
    hi = z.astype(BF16)
    rest = z - hi.astype(F32)
    mid = rest.astype(BF16)
    return hi, mid, (rest - mid.astype(F32)).astype(BF16)


def _select_dot(z, ones):
    return sum(_dot(term, ones) for term in _split3(z))


def _ssd_prep_fwd(dt_raw, dt_bias, a_log):
    t = dt_raw.shape[0]
    cl = SSD_CHUNK

    def body(r_ref, b_ref, al_ref, acs_ref, dt_rep_ref, acs_rep_ref):
        dt = _softplus(r_ref[...] + b_ref[...])
        adt = dt * (-jnp.exp(al_ref[...]))
        li = lax.broadcasted_iota(jnp.int32, (cl, cl), 0)
        si = lax.broadcasted_iota(jnp.int32, (cl, cl), 1)
        tri = (si <= li).astype(F32)
        acs = jnp.dot(tri, adt, preferred_element_type=F32, precision=HIGHEST)
        acs_ref[...] = acs
        head = lax.broadcasted_iota(jnp.int32, (LANES, D_INNER), 0)
        chan = lax.broadcasted_iota(jnp.int32, (LANES, D_INNER), 1) // SSM_HEAD_DIM
        spread = (head == chan).astype(BF16)
        dt_rep_ref[...] = _select_dot(dt, spread)
        acs_rep_ref[...] = _select_dot(acs, spread)

    row = pl.BlockSpec((cl, LANES), lambda i: (i, 0))
    wide = pl.BlockSpec((cl, D_INNER), lambda i: (i, 0))
    vec = pl.BlockSpec((1, LANES), lambda i: (0, 0))
    return pl.pallas_call(
        body, name="ssd_prep_fwd", grid=(t // cl,),
        in_specs=[row, vec, vec], out_specs=[row, wide, wide],
        out_shape=[jax.ShapeDtypeStruct((t, LANES), F32), jax.ShapeDtypeStruct((t, D_INNER), F32),
                   jax.ShapeDtypeStruct((t, D_INNER), F32)],
        compiler_params=_params("parallel"),
    )(dt_raw, dt_bias, a_log)


def _ssd_prep_bwd(dt_raw, dt_bias, ddt):
    t = dt_raw.shape[0]
    tm = _tile(t, 512)

    def body(r_ref, b_ref, d_ref, o_ref, db_ref):
        g = d_ref[...] * _sigmoid(r_ref[...] + b_ref[...])
        o_ref[...] = g.astype(BF16)
        part = jnp.sum(g, axis=0, keepdims=True)

        @pl.when(pl.program_id(0) == 0)
        def _():
            db_ref[...] = part

        @pl.when(pl.program_id(0) > 0)
        def _():
            db_ref[...] += part

    row = pl.BlockSpec((tm, LANES), lambda i: (i, 0))
    vec = pl.BlockSpec((1, LANES), lambda i: (0, 0))
    return pl.pallas_call(
        body, name="ssd_prep_bwd", grid=(t // tm,),
        in_specs=[row, vec, row], out_specs=[row, vec],
        out_shape=[jax.ShapeDtypeStruct((t, LANES), BF16), jax.ShapeDtypeStruct((1, LANES), F32)],
        compiler_params=_params("arbitrary"),
    )(dt_raw, dt_bias, ddt)


GROUP_W = D_INNER // SSM_GROUPS
PAIRS_PER_GROUP = GROUP_W // LANES


def _head_cols(acs_pair, lt64):
    rolled = pltpu.roll(acs_pair, ATT_HEAD_DIM, 1)
    return jnp.where(lt64, acs_pair, rolled), jnp.where(lt64, rolled, acs_pair)


def _ssd_fwd(xbc, dt_rep, acs_rep, acs_t, dskip_rep, z, norm_w, side=None):
    t = xbc.shape[0]
    cl = SSD_CHUNK
    nc = t // cl

    def body(xbc_ref, dt_ref, acs_ref, acst_ref, dskip_ref, z_ref, nw_ref, y_ref, hin_ref, yn_ref, state_ref):
        @pl.when(pl.program_id(0) == 0)
        def _():
            state_ref[...] = jnp.zeros_like(state_ref)

        lt64 = _lane_lt64(cl)
        li = lax.broadcasted_iota(jnp.int32, (cl, cl), 0)
        si = lax.broadcasted_iota(jnp.int32, (cl, cl), 1)
        causal = li >= si
        hin_ref[...] = state_ref[...]
        for g in range(SSM_GROUPS):
            gsl = slice(g * GROUP_W, (g + 1) * GROUP_W)
            xg = xbc_ref[:, gsl]
            bg = xbc_ref[:, D_INNER + g * SSM_STATE:D_INNER + (g + 1) * SSM_STATE]
            cg = xbc_ref[:, D_INNER + SSM_GROUPS * SSM_STATE + g * SSM_STATE:
                         D_INNER + SSM_GROUPS * SSM_STATE + (g + 1) * SSM_STATE]
            acs = acs_ref[:, gsl]
            xdt = xg * dt_ref[:, gsl]
            atot = acs[cl - 1:cl, :]
            hin = state_ref[:, gsl]
            cgb = cg.astype(BF16)
            gmat = _dot_nt(cgb, bg.astype(BF16))
            yoff = _dot(cgb, hin.astype(BF16)) * jnp.exp(acs)
            snew = _dot(bg.T.astype(BF16), (xdt * jnp.exp(atot - acs)).astype(BF16))
            state_ref[:, gsl] = hin * jnp.exp(atot) + snew
            xdtb = xdt.astype(BF16)
            for pr in range(PAIRS_PER_GROUP):
                psl = slice(pr * LANES, (pr + 1) * LANES)
                cols = _head_cols(acs[:, psl], lt64)
                xp = xdtb[:, psl]
                ys = []
                for hh in range(2):
                    h = (g * PAIRS_PER_GROUP + pr) * 2 + hh
                    seg = cols[hh] - acst_ref[h:h + 1, :]
                    lm = jnp.exp(jnp.where(causal, seg, NEG_BIG))
                    ys.append(_dot((gmat * lm).astype(BF16), xp))
                ydiag = jnp.where(lt64, ys[0], ys[1])
                osl = slice(g * GROUP_W + pr * LANES, g * GROUP_W + (pr + 1) * LANES)
                y_ref[:, osl] = ydiag + yoff[:, psl] + xg[:, psl] * dskip_ref[:, osl]
            zv = z_ref[:, gsl]
            v = y_ref[:, gsl] * (zv * _sigmoid(zv))
            r = lax.rsqrt(jnp.mean(v * v, axis=-1, keepdims=True) + NORM_EPS)
            yn_ref[:, gsl] = (v * r * nw_ref[:, gsl]).astype(BF16)

    row = lambda w: pl.BlockSpec((cl, w), lambda c: (c, 0))
    vec = pl.BlockSpec((1, D_INNER), lambda c: (0, 0))
    return _call(
        body, side, name="ssd_fwd", grid=(nc,),
        in_specs=[row(CONV_DIM), row(D_INNER), row(D_INNER),
                  pl.BlockSpec((SSM_HEADS, cl), lambda c: (0, c)), vec, row(D_INNER), vec],
        out_specs=[row(D_INNER), pl.BlockSpec((None, SSM_STATE, D_INNER), lambda c: (c, 0, 0)), row(D_INNER)],
        out_shape=[jax.ShapeDtypeStruct((t, D_INNER), F32), jax.ShapeDtypeStruct((nc, SSM_STATE, D_INNER), F32),
                   jax.ShapeDtypeStruct((t, D_INNER), BF16)],
        scratch_shapes=[pltpu.VMEM((SSM_STATE, D_INNER), F32)],
        semantics=("arbitrary",), args=(xbc, dt_rep, acs_rep, acs_t, dskip_rep, z, norm_w),
    )


def _ssd_bwd(xbc, dt_rep, acs_rep, acs_t, dskip_rep, a_rep, hin_all, dy, side=None):
    t = xbc.shape[0]
    cl = SSD_CHUNK
    nc = t // cl

    def body(xbc_ref, dt_ref, acs_ref, acst_ref, dskip_ref, a_ref, hin_ref, dy_ref,
             dxbc_ref, ddt_ref, da_ref, dds_ref, dstate_ref, dacs_ref, dxs_ref):
        step = pl.program_id(0)

        @pl.when(step == 0)
        def _():
            dstate_ref[...] = jnp.zeros_like(dstate_ref)
            da_ref[...] = jnp.zeros_like(da_ref)
            dds_ref[...] = jnp.zeros_like(dds_ref)

        bd = _head_block_diag()
        lt64 = _lane_lt64(cl)
        li = lax.broadcasted_iota(jnp.int32, (cl, cl), 0)
        si = lax.broadcasted_iota(jnp.int32, (cl, cl), 1)
        lower = li >= si
        upper = si >= li
        last_row = lax.broadcasted_iota(jnp.int32, (cl, GROUP_W), 0) == cl - 1
        for g in range(SSM_GROUPS):
            gsl = slice(g * GROUP_W, (g + 1) * GROUP_W)
            bsl = slice(D_INNER + g * SSM_STATE, D_INNER + (g + 1) * SSM_STATE)
            csl = slice(D_INNER + SSM_GROUPS * SSM_STATE + g * SSM_STATE,
                        D_INNER + SSM_GROUPS * SSM_STATE + (g + 1) * SSM_STATE)
            xg = xbc_ref[:, gsl]
            bg = xbc_ref[:, bsl]
            cg = xbc_ref[:, csl]
            bgb, cgb = bg.astype(BF16), cg.astype(BF16)
            acs = acs_ref[:, gsl]
            xdt = xg * dt_ref[:, gsl]
            atot = acs[cl - 1:cl, :]
            eg = jnp.exp(acs)
            dk = jnp.exp(atot - acs)
            etot = jnp.exp(atot)
            hin = hin_ref[:, gsl]
            hinb = hin.astype(BF16)
            dh = dstate_ref[:, gsl]
            dhb = dh.astype(BF16)
            dyg = dy_ref[:, gsl]

            gmat = _dot_nt(cgb, bgb)
            gmat_t = _dot_nt(bgb, cgb)
            ch = _dot(cgb, hinb)
            dacs = _head_sums(dyg * ch * eg, bd)
            dye = (dyg * eg).astype(BF16)
            dc = _dot_nt(dye, hinb)
            dhin = _dot(cg.T.astype(BF16), dye)
            bdh = _dot(bgb, dhb)
            dxs = bdh * dk
            xdk = xdt * dk
            db = _dot_nt(xdk.astype(BF16), dhb)
            ddk = _head_sums(bdh * xdk, bd)
            dacs = dacs - ddk
            datot = jnp.sum(ddk, axis=0, keepdims=True) + etot * _head_sums(
                jnp.sum(dh * hin, axis=0, keepdims=True), bd)
            dacs = dacs + jnp.where(last_row, datot, 0.0)
            dstate_ref[:, gsl] = dh * etot + dhin

            xdtb = xdt.astype(BF16)
            dgsum = jnp.zeros((cl, cl), F32)
            dgsum_t = jnp.zeros((cl, cl), F32)
            for pr in range(PAIRS_PER_GROUP):
                psl = slice(pr * LANES, (pr + 1) * LANES)
                cols = _head_cols(acs[:, psl], lt64)
                xp = xdtb[:, psl]
                dyp = dyg[:, psl].astype(BF16)
                dx1, dac = [], []
                for hh in range(2):
                    h = (g * PAIRS_PER_GROUP + pr) * 2 + hh
                    mine = lt64 if hh == 0 else jnp.logical_not(lt64)
                    row = acst_ref[h:h + 1, :]
                    lm = jnp.exp(jnp.where(lower, cols[hh] - row, NEG_BIG))
                    lm_t = jnp.exp(jnp.where(upper, row - cols[hh], NEG_BIG))
                    dyh = jnp.where(mine, dyp, jnp.zeros_like(dyp))
                    xh = jnp.where(mine, xp, jnp.zeros_like(xp))
                    dm = _dot_nt(dyh, xp)
                    dm_t = _dot_nt(xh, dyp)
                    m_t = gmat_t * lm_t
                    dx1.append(_dot(m_t.astype(BF16), dyp))
                    w = dm * (gmat * lm)
                    w_t = dm_t * m_t
                    dac.append(jnp.sum(w, axis=1, keepdims=True) - jnp.sum(w_t, axis=1, keepdims=True))
                    dgsum = dgsum + dm * lm
                    dgsum_t = dgsum_t + dm_t * lm_t
                osl = slice(g * GROUP_W + pr * LANES, g * GROUP_W + (pr + 1) * LANES)
                dxs_ref[:, osl] = dxs[:, psl] + jnp.where(lt64, dx1[0], dx1[1])
                dacs_ref[:, osl] = dacs[:, psl] + jnp.where(lt64, jnp.broadcast_to(dac[0], (cl, LANES)),
                                                             jnp.broadcast_to(dac[1], (cl, LANES)))
            dxbc_ref[:, csl] = dc + _dot(dgsum.astype(BF16), bgb)
            dxbc_ref[:, bsl] = db + _dot(dgsum_t.astype(BF16), cgb)

        dadt = _split_dot(upper.astype(BF16), dacs_ref[...])
        xall = xbc_ref[:, 0:D_INNER]
        dtall = dt_ref[...]
        dxsall = dxs_ref[...]
        dyall = dy_ref[...]
        ddt_rep = dadt * a_ref[...] + _head_sums(dxsall * xall, bd)
        chan = lax.broadcasted_iota(jnp.int32, (D_INNER, LANES), 0)
        head = lax.broadcasted_iota(jnp.int32, (D_INNER, LANES), 1)
        ddt_ref[...] = _select_dot(ddt_rep, (chan == head * SSM_HEAD_DIM).astype(BF16))
        dxbc_ref[:, 0:D_INNER] = dxsall * dtall + dyall * dskip_ref[...]
        da_ref[...] += jnp.sum(dadt * dtall, axis=0, keepdims=True)
        dds_ref[...] += jnp.sum(dyall * xall, axis=0, keepdims=True)

        @pl.when(step == nc - 1)
        def _():
            dds_ref[...] = _head_sums(dds_ref[...], bd)

    row = lambda w: pl.BlockSpec((cl, w), lambda c: (nc - 1 - c, 0))
    vec = pl.BlockSpec((1, D_INNER), lambda c: (0, 0))
    return _call(
        body, side, name="ssd_bwd", grid=(nc,),
        in_specs=[row(CONV_DIM), row(D_INNER), row(D_INNER),
                  pl.BlockSpec((SSM_HEADS, cl), lambda c: (0, nc - 1 - c)), vec, vec,
                  pl.BlockSpec((None, SSM_STATE, D_INNER), lambda c: (nc - 1 - c, 0, 0)), row(D_INNER)],
        out_specs=[row(CONV_DIM), row(LANES), vec, vec],
        out_shape=[jax.ShapeDtypeStruct((t, CONV_DIM), F32), jax.ShapeDtypeStruct((t, LANES), F32),
                   jax.ShapeDtypeStruct((1, D_INNER), F32), jax.ShapeDtypeStruct((1, D_INNER), F32)],
        scratch_shapes=[pltpu.VMEM((SSM_STATE, D_INNER), F32), pltpu.VMEM((cl, D_INNER), F32),
                        pltpu.VMEM((cl, D_INNER), F32)],
        semantics=("arbitrary",), args=(xbc, dt_rep, acs_rep, acs_t, dskip_rep, a_rep, hin_all, dy),
    )


def _gate_norm_bwd(y, z, w, dx, w_out, side=None):
    t, c = y.shape
    d = dx.shape[1]
    tm = _tile(t, 256)

    def body(y_ref, z_ref, w_ref, dx_ref, wo_ref, dy_ref, dz_ref, dw_ref):
        @pl.when(pl.program_id(0) == 0)
        def _():
            dw_ref[...] = jnp.zeros_like(dw_ref)

        dxb = dx_ref[...].astype(BF16)
        for g in range(SSM_GROUPS):
            gsl = slice(g * GROUP_W, (g + 1) * GROUP_W)
            zv, yv, dov = z_ref[:, gsl], y_ref[:, gsl], _dot_nt(dxb, wo_ref[gsl, :])
            sg = _sigmoid(zv)
            sz = zv * sg
            v = yv * sz
            r = lax.rsqrt(jnp.mean(v * v, axis=-1, keepdims=True) + NORM_EPS)
            vh = v * r
            dvh = dov * w_ref[:, gsl]
            mean = jnp.mean(dvh * vh, axis=-1, keepdims=True)
            dv = r * (dvh - vh * mean)
            dy_ref[:, gsl] = dv * sz
            dz_ref[:, gsl] = (dv * yv * (sg * (1.0 + zv * (1.0 - sg)))).astype(BF16)
            dw_ref[:, gsl] += jnp.sum(dov * vh, axis=0, keepdims=True)

    row = pl.BlockSpec((tm, c), lambda i: (i, 0))
    vec = pl.BlockSpec((1, c), lambda i: (0, 0))
    return _call(
        body, side, name="gate_norm_bwd", grid=(t // tm,),
        in_specs=[row, row, vec, pl.BlockSpec((tm, d), lambda i: (i, 0)), pl.BlockSpec((c, d), lambda i: (0, 0))],
        out_specs=[row, row, vec],
        out_shape=[jax.ShapeDtypeStruct((t, c), F32), jax.ShapeDtypeStruct((t, c), BF16),
                   jax.ShapeDtypeStruct((1, c), F32)],
        scratch_shapes=[], semantics=("arbitrary",), args=(y, z, w, dx, w_out),
    )


ATT_W = ATT_HEADS * ATT_HEAD_DIM
N_QKV_BLOCKS = 9
ATT_SCALE = 1.0 / math.sqrt(ATT_HEAD_DIM)


def _head_rmsnorm(x, gain, bd):
    ms = _head_sums(x * x, bd, terms=1) * (1.0 / ATT_HEAD_DIM)
    return x * lax.rsqrt(ms + NORM_EPS) * gain


def _class_rows(ref, blk, r, dil):
    span = ATT_BLOCK * dil
    sub = ref.at[pl.ds(pl.multiple_of(blk * span, span), span), :]
    return sub[...] if dil == 1 else sub[pl.ds(r, ATT_BLOCK, stride=dil), :]


def _store_class_rows(ref, blk, r, dil, val):
    span = ATT_BLOCK * dil
    sub = ref.at[pl.ds(pl.multiple_of(blk * span, span), span), :]
    if dil == 1:
        sub[...] = val
    else:
        sub[pl.ds(r, ATT_BLOCK, stride=dil), :] = val


PAIRS = ATT_HEADS // 2


def _pair_col(g, j):
    return lambda pair: (0, (g * 3 + j) * PAIRS + pair)


def _pair_slopes(pair):
    steps = jnp.full((1, 2 * ATT_BLOCK), 2 * pair + 1, jnp.int32).astype(F32)
    first = jnp.exp(steps * (-0.5 * math.log(2.0)))
    return first, first * (2.0 ** -0.5)


NORM_ROWS = 512


ROW_SLICES = 4
SLICE_ROWS = 2 * ATT_BLOCK // ROW_SLICES


def _fill_band_bias(bias_ref, pair, dil, transposed):
    bq = ATT_BLOCK
    a = lax.broadcasted_iota(jnp.int32, (2 * bq, 2 * bq), 0) % bq
    b = lax.broadcasted_iota(jnp.int32, (2 * bq, 2 * bq), 1)
    dist = (b - a) if transposed else (a + bq - b)
    in_band = (dist >= 0) & (dist <= bq)
    s0, s1 = _pair_slopes(pair)
    first_head = lax.broadcasted_iota(jnp.int32, (2 * bq, 2 * bq), 0) < bq
    bias = jnp.where(first_head, s0, s1) * (dist.astype(F32) * float(dil))
    inside = (b < bq) if transposed else (b >= bq)
    bias_ref[1] = jnp.where(in_band, bias, -NEG_BIG)
    bias_ref[0] = jnp.where(in_band & inside, bias, -NEG_BIG)


def _row_slices():
    return [slice(i * SLICE_ROWS, (i + 1) * SLICE_ROWS) for i in range(ROW_SLICES)]


def _stack_heads(tile):
    rows = lax.broadcasted_iota(jnp.int32, (2 * ATT_BLOCK, LANES), 0) < ATT_BLOCK
    lanes = lax.broadcasted_iota(jnp.int32, (2 * ATT_BLOCK, LANES), 1) < ATT_HEAD_DIM
    both = jnp.concatenate([tile, tile], axis=0)
    return jnp.where(rows == lanes, both, jnp.zeros_like(both))


def _unstack_heads(stacked, lt64):
    return jnp.where(lt64, stacked[:ATT_BLOCK], stacked[ATT_BLOCK:])


ITEMS_PER_PASS = 4


def _item_loop(nb, dil, work):
    if dil == 1:
        def trip(i, carry):
            work([(i * ITEMS_PER_PASS + b, 0) for b in range(ITEMS_PER_PASS)])
            return carry

        lax.fori_loop(0, nb // ITEMS_PER_PASS, trip, 0)
    else:
        def trip(n, carry):
            for r0 in range(0, dil, ITEMS_PER_PASS):
                work([(n, r0 + j) for j in range(ITEMS_PER_PASS)])
            return carry

        lax.fori_loop(0, nb, trip, 0)


def _qk_normalised(tile, j, gq_ref, gk_ref):
    kind = (j // (ATT_W // tile.shape[1])) % 3
    gain = jnp.where(kind == 0, gq_ref[...] * ATT_SCALE, gk_ref[...])
    return jnp.where(kind == 2, tile, _head_rmsnorm(tile, gain, _head_block_diag()))


def _attn_fwd(qkn, g, dil):
    t = qkn.shape[0]
    nb = t // dil // ATT_BLOCK
    bq = ATT_BLOCK

    def body(qn_ref, kn_ref, v_ref, o_ref, l_ref, bias_ref):
        _fill_band_bias(bias_ref, pl.program_id(0), dil, False)
        lt64 = _lane_lt64(bq)

        def work(items):
            scores, values, probs = [], [], []
            for n, r in items:
                prev = jnp.maximum(n - 1, 0)
                q2 = _stack_heads(_class_rows(qn_ref, n, r, dil).astype(BF16))
                kcat = jnp.concatenate([_class_rows(kn_ref, prev, r, dil), _class_rows(kn_ref, n, r, dil)],
                                       axis=0).astype(BF16)
                values.append(jnp.concatenate([_class_rows(v_ref, prev, r, dil), _class_rows(v_ref, n, r, dil)],
                                              axis=0).astype(BF16))
                scores.append(_dot_nt(q2, kcat))
            for (n, r), sc in zip(items, scores):
                bias = bias_ref.at[jnp.minimum(n, 1)]
                ps, inv, lses = [], [], []
                for rows in _row_slices():
                    s = sc[rows] - bias[rows, :]
                    m = jnp.max(s, axis=1, keepdims=True)
                    p = jnp.exp(s - m)
                    l = jnp.sum(p, axis=1, keepdims=True)
                    ps.append(p.astype(BF16))
                    inv.append(jnp.broadcast_to(1.0 / l, (SLICE_ROWS, LANES)))
                    lses.append(jnp.broadcast_to(m + jnp.log(l), (SLICE_ROWS, LANES)))
                probs.append((jnp.concatenate(ps, axis=0), jnp.concatenate(inv, axis=0)))
                _store_class_rows(l_ref, n, r, dil, _unstack_heads(jnp.concatenate(lses, axis=0), lt64))
            for (n, r), (p, inv), vcat in zip(items, probs, values):
                _store_class_rows(o_ref, n, r, dil, _unstack_heads(_dot(p, vcat) * inv, lt64))

        _item_loop(nb, dil, work)

    col = lambda j: pl.BlockSpec((t, LANES), _pair_col(g, j))
    out = pl.BlockSpec((t, LANES), lambda pair: (0, pair))
    return pl.pallas_call(
        body, name=f"attn_fwd_g{g}", grid=(PAIRS,),
        in_specs=[col(0), col(1), col(2)], out_specs=[out, out],
        out_shape=[jax.ShapeDtypeStruct((t, ATT_W), F32), jax.ShapeDtypeStruct((t, ATT_W), F32)],
        scratch_shapes=[pltpu.VMEM((2, 2 * bq, 2 * bq), F32)],
        compiler_params=_params("parallel"),
    )(qkn, qkn, qkn)


def _one_per_head(rep):
    chan = lax.broadcasted_iota(jnp.int32, (ATT_W, LANES), 0)
    head = lax.broadcasted_iota(jnp.int32, (ATT_W, LANES), 1)
    return _select_dot(rep, (chan == head * ATT_HEAD_DIM).astype(BF16))


def _attn_out_fwd(outs, lses, w_o, x0, next_gain):
    t, d = x0.shape
    tm = _tile(t, 256)

    def body(o0, o1, o2, l0, l1, l2, wo_ref, x_ref, g_ref, of_ref, lt_ref, lc_ref, x1_ref, h_ref):
        a, b, c = l0[...], l1[...], l2[...]
        m = jnp.maximum(jnp.maximum(a, b), c)
        ea, eb, ec = jnp.exp(a - m), jnp.exp(b - m), jnp.exp(c - m)
        ssum = ea + eb + ec
        o = (ea * o0[...] + eb * o1[...] + ec * o2[...]) / ssum
        of_ref[...] = o
        lse = m + jnp.log(ssum)
        lt_ref[...] = lse
        lc_ref[...] = _one_per_head(lse)
        x1 = x_ref[...] + _dot(o.astype(BF16), wo_ref[...])
        x1_ref[...] = x1
        r = lax.rsqrt(jnp.mean(x1 * x1, axis=-1, keepdims=True) + NORM_EPS)
        h_ref[...] = (x1 * r * g_ref[...]).astype(BF16)

    row = pl.BlockSpec((tm, ATT_W), lambda i: (i, 0))
    xrow = pl.BlockSpec((tm, d), lambda i: (i, 0))
    return pl.pallas_call(
        body, name="att_out", grid=(t // tm,),
        in_specs=[row] * 6 + [pl.BlockSpec((ATT_W, d), lambda i: (0, 0)), xrow, pl.BlockSpec((1, d), lambda i: (0, 0))],
        out_specs=[row, row, pl.BlockSpec((tm, LANES), lambda i: (i, 0)), xrow, xrow],
        out_shape=[jax.ShapeDtypeStruct((t, ATT_W), F32), jax.ShapeDtypeStruct((t, ATT_W), F32),
                   jax.ShapeDtypeStruct((t, LANES), F32), jax.ShapeDtypeStruct((t, d), F32),
                   jax.ShapeDtypeStruct((t, d), BF16)],
        compiler_params=_params("parallel"),
    )(*outs, *lses, w_o, x0, next_gain)


def _attn_out_bwd(dx, w_o, o, side=None):
    t, d = dx.shape
    tm = _tile(t, 256)

    def body(dx_ref, wo_ref, o_ref, do_ref, dl_ref, dc_ref):
        do = _dot_nt(dx_ref[...].astype(BF16), wo_ref[...])
        do_ref[...] = do
        dl = _head_sums(do * o_ref[...], _head_block_diag())
        dl_ref[...] = dl
        dc_ref[...] = _one_per_head(dl)

    row = pl.BlockSpec((tm, ATT_W), lambda i: (i, 0))
    return _call(
        body, side, name="att_out_dx", grid=(t // tm,),
        in_specs=[pl.BlockSpec((tm, d), lambda i: (i, 0)), pl.BlockSpec((ATT_W, d), lambda i: (0, 0)), row],
        out_specs=[row, row, pl.BlockSpec((tm, LANES), lambda i: (i, 0))],
        out_shape=[jax.ShapeDtypeStruct((t, ATT_W), F32), jax.ShapeDtypeStruct((t, ATT_W), F32),
                   jax.ShapeDtypeStruct((t, LANES), F32)],
        scratch_shapes=[], semantics=("parallel",), args=(dx, w_o, o),
    )


def _head_rmsnorm_bwd(x_ref, dy_ref, gain_ref, dx_ref, dgain_ref):
    bd = _head_block_diag()
    gain = gain_ref[...]

    def step(i, acc):
        rows = pl.ds(pl.multiple_of(i * NORM_ROWS, NORM_ROWS), NORM_ROWS)
        x, dy = x_ref[rows, :], dy_ref[rows, :]
        r = lax.rsqrt(_head_sums(x * x, bd, terms=1) * (1.0 / ATT_HEAD_DIM) + NORM_EPS)
        xh = x * r
        dxh = dy * gain
        mean = _head_sums(dxh * xh, bd, terms=1) * (1.0 / ATT_HEAD_DIM)
        dx_ref[rows, :] = (r * (dxh - xh * mean)).astype(BF16)
        return acc + jnp.sum(dy * xh, axis=0, keepdims=True)

    acc = lax.fori_loop(0, x_ref.shape[0] // NORM_ROWS, step, jnp.zeros((1, LANES), F32))
    dgain_ref[...] = jnp.broadcast_to(acc, dgain_ref.shape)


def _attn_bwd_dq(qkv, qkn, gq, do, l_rep, dl_rep, g, dil):
    t = qkv.shape[0]
    nb = t // dil // ATT_BLOCK
    bq = ATT_BLOCK

    def body(q_ref, qn_ref, kn_ref, v_ref, gq_ref, do_ref, l_ref, dl_ref, dx_ref, dgain_ref, bias_ref, dq_ref):
        _fill_band_bias(bias_ref, pl.program_id(0), dil, False)
        lt64 = _lane_lt64(bq)

        def per_row(tile):
            cols = _head_cols(tile, lt64)
            half = jnp.concatenate([cols[0], cols[1]], axis=0)
            return jnp.concatenate([half, half], axis=1)

        def work(items):
            products, keys, dscores = [], [], []
            for n, r in items:
                prev = jnp.maximum(n - 1, 0)
                q2 = _stack_heads(_class_rows(qn_ref, n, r, dil).astype(BF16))
                do2 = _stack_heads(_class_rows(do_ref, n, r, dil).astype(BF16))
                kcat = jnp.concatenate([_class_rows(kn_ref, prev, r, dil), _class_rows(kn_ref, n, r, dil)],
                                       axis=0).astype(BF16)
                vcat = jnp.concatenate([_class_rows(v_ref, prev, r, dil), _class_rows(v_ref, n, r, dil)],
                                       axis=0).astype(BF16)
                keys.append(kcat)
                products.append((_dot_nt(q2, kcat), _dot_nt(do2, vcat)))
            for (n, r), (scores, dps) in zip(items, products):
                bias = bias_ref.at[jnp.minimum(n, 1)]
                lse = per_row(_class_rows(l_ref, n, r, dil))
                dl = per_row(_class_rows(dl_ref, n, r, dil))
                dss = []
                for rows in _row_slices():
                    p = jnp.exp(scores[rows] - bias[rows, :] - lse[rows])
                    dss.append((p * (dps[rows] - dl[rows])).astype(BF16))
                dscores.append(jnp.concatenate(dss, axis=0))
            for (n, r), ds, kcat in zip(items, dscores, keys):
                _store_class_rows(dq_ref, n, r, dil, _unstack_heads(_dot(ds, kcat) * ATT_SCALE, lt64))

        _item_loop(nb, dil, work)
        _head_rmsnorm_bwd(q_ref, dq_ref, gq_ref, dx_ref, dgain_ref)

    col = lambda j: pl.BlockSpec((t, LANES), _pair_col(g, j))
    vec = pl.BlockSpec((1, LANES), lambda pair: (0, 0))
    tok = pl.BlockSpec((t, LANES), lambda pair: (0, pair))
    return pl.pallas_call(
        body, name=f"attn_bwd_dq_g{g}", grid=(PAIRS,),
        in_specs=[col(0), col(0), col(1), col(2), vec, tok, tok, tok],
        out_specs=[tok, pl.BlockSpec((None, 8, LANES), lambda pair: (pair, 0, 0))],
        out_shape=[jax.ShapeDtypeStruct((t, ATT_W), BF16), jax.ShapeDtypeStruct((PAIRS, 8, LANES), F32)],
        scratch_shapes=[pltpu.VMEM((2, 2 * bq, 2 * bq), F32), pltpu.VMEM((t, LANES), F32)],
        compiler_params=_params("parallel"),
    )(qkv, qkn, qkn, qkn, gq, do, l_rep, dl_rep)


def _attn_bwd_dkv(qkv, qkn, gk, do, l_row, dl_row, g, dil):
    t = qkv.shape[0]
    nb = t // dil // ATT_BLOCK
    bq = ATT_BLOCK

    def body(k_ref, qn_ref, kn_ref, v_ref, gk_ref, do_ref, l_ref, dl_ref, dkx_ref, dvx_ref, dgain_ref, bias_ref,
             dk_ref, dv_ref):
        _fill_band_bias(bias_ref, pl.program_id(0), dil, True)
        lt64 = _lane_lt64(bq)

        def per_query(ref, hh, lane_c, lane_n):
            return jnp.concatenate([ref[hh:hh + 1, pl.ds(lane_c, bq)], ref[hh:hh + 1, pl.ds(lane_n, bq)]], axis=1)

        def work(items):
            products, operands, weights = [], [], []
            for n, r in items:
                nxt = jnp.minimum(n + 1, nb - 1)
                k2 = _stack_heads(_class_rows(kn_ref, n, r, dil).astype(BF16))
                v2 = _stack_heads(_class_rows(v_ref, n, r, dil).astype(BF16))
                qcat = jnp.concatenate([_class_rows(qn_ref, n, r, dil), _class_rows(qn_ref, nxt, r, dil)],
                                       axis=0).astype(BF16)
                docat = jnp.concatenate([_class_rows(do_ref, n, r, dil), _class_rows(do_ref, nxt, r, dil)],
                                        axis=0).astype(BF16)
                operands.append((qcat, docat))
                products.append((_dot_nt(k2, qcat), _dot_nt(v2, docat)))
            for (n, r), (scores, dps) in zip(items, products):
                nxt = jnp.minimum(n + 1, nb - 1)
                bias = bias_ref.at[jnp.where(n == nb - 1, 0, 1)]
                lane_c = pl.multiple_of((r * nb + n) * bq, bq)
                lane_n = pl.multiple_of((r * nb + nxt) * bq, bq)
                lse = [per_query(l_ref, hh, lane_c, lane_n) for hh in range(2)]
                dl = [per_query(dl_ref, hh, lane_c, lane_n) for hh in range(2)]
                pts, dss = [], []
                for i, rows in enumerate(_row_slices()):
                    hh = i * SLICE_ROWS // bq
                    p_t = jnp.exp(scores[rows] - bias[rows, :] - lse[hh])
                    pts.append(p_t.astype(BF16))
                    dss.append((p_t * (dps[rows] - dl[hh])).astype(BF16))
                weights.append((jnp.concatenate(pts, axis=0), jnp.concatenate(dss, axis=0)))
            for (n, r), (p_t, ds_t), (qcat, docat) in zip(items, weights, operands):
                _store_class_rows(dv_ref, n, r, dil, _unstack_heads(_dot(p_t, docat), lt64))
                _store_class_rows(dk_ref, n, r, dil, _unstack_heads(_dot(ds_t, qcat), lt64))

        _item_loop(nb, dil, work)
        _head_rmsnorm_bwd(k_ref, dk_ref, gk_ref, dkx_ref, dgain_ref)

        def cast_rows(i, carry):
            rows = pl.ds(pl.multiple_of(i * NORM_ROWS, NORM_ROWS), NORM_ROWS)
            dvx_ref[rows, :] = dv_ref[rows, :].astype(BF16)
            return carry

        lax.fori_loop(0, t // NORM_ROWS, cast_rows, 0)

    col = lambda j: pl.BlockSpec((t, LANES), _pair_col(g, j))
    vec = pl.BlockSpec((1, LANES), lambda pair: (0, 0))
    tok = pl.BlockSpec((t, LANES), lambda pair: (0, pair))
    rows = pl.BlockSpec((None, 8, t), lambda pair: (pair, 0, 0))
    return pl.pallas_call(
        body, name=f"attn_bwd_dkv_g{g}", grid=(PAIRS,),
        in_specs=[col(1), col(0), col(1), col(2), vec, tok, rows, rows],
        out_specs=[tok, tok, pl.BlockSpec((None, 8, LANES), lambda pair: (pair, 0, 0))],
        out_shape=[jax.ShapeDtypeStruct((t, ATT_W), BF16), jax.ShapeDtypeStruct((t, ATT_W), BF16),
                   jax.ShapeDtypeStruct((PAIRS, 8, LANES), F32)],
        scratch_shapes=[pltpu.VMEM((2, 2 * bq, 2 * bq), F32), pltpu.VMEM((t, LANES), F32),
                        pltpu.VMEM((t, LANES), F32)],
        compiler_params=_params("parallel"),
    )(qkv, qkn, qkn, qkn, gk, do, l_row, dl_row)


def _rows_by_residue(one_per_head, dil):
    t = one_per_head.shape[0]
    per_head = one_per_head[:, :ATT_HEADS]
    rows = per_head.reshape(t // dil, dil, ATT_HEADS).transpose(2, 1, 0).reshape(PAIRS, 2, t)
    return jnp.pad(rows, ((0, 0), (0, 6), (0, 0)))


def _per_head(rep_row):
    return rep_row[0, ::SSM_HEAD_DIM]


def _rep_heads(v):
    return jnp.repeat(v, SSM_HEAD_DIM)[None, :]


def _pad_lanes(v):
    return jnp.pad(v, ((0, 0), (0, LANES - v.shape[1])))


class _NoOverlap:
    def side(self, host):
        return None

    def after(self, host):
        pass

    def begin_backward(self, grads):
        pass


def _hosted(plan, host, fn, *args, **kwargs):
    out = fn(*args, side=plan.side(host), **kwargs)
    plan.after(host)
    return out


def _ffn_ple_fwd(x1, h, p_i, prm, i, plan, next_gain=None, target=None):
    g, u, act = _hosted(plan, f"swiglu_fwd_{i}", _swiglu_fwd, h, prm["ffn_w_gate"][i], prm["ffn_w_up"][i],
                        name=f"swiglu_fwd_{i}")
    x2 = _hosted(plan, f"ffn_down_{i}", _matmul, act, prm["ffn_w_down"][i], mode="nn", addend=x1,
                 name=f"ffn_down_{i}")
    outs = _ple_fwd(x2, p_i, prm["ple_w_gate"][i], prm["ple_w_proj"][i], name=f"ple_fwd_{i}", next_gain=next_gain,
                    target=target)
    return outs, dict(x1=x1, h=h, g=g, u=u, act=act, x2=x2)


def _ffn_ple_bwd(dx3, p_i, prm, i, sv, grads, plan):
    ds, dple, dx2 = _ple_bwd(sv["x2"], p_i, prm["ple_w_gate"][i], prm["ple_w_proj"][i], dx3, name=f"ple_bwd_{i}")
    grads["ple_w_gate"][i] = _matmul_tn(sv["x2"], ds, name=f"d_ple_w_gate_{i}")
    grads["ple_w_proj"][i] = _matmul_tn(dple, p_i, name=f"d_ple_w_proj_{i}")
    grads["ffn_w_down"][i] = _matmul_tn(sv["act"], dx2, name=f"d_ffn_w_down_{i}")
    dg, du = _hosted(plan, f"swiglu_bwd_{i}", _swiglu_bwd, dx2, prm["ffn_w_down"][i], sv["g"], sv["u"],
                     name=f"swiglu_bwd_{i}")
    grads["ffn_w_gate"][i] = _matmul_tn(dg, sv["h"], name=f"d_ffn_w_gate_{i}")
    grads["ffn_w_up"][i] = _matmul_tn(du, sv["h"], name=f"d_ffn_w_up_{i}")
    dh = _matmul(dg, prm["ffn_w_gate"][i], mode="nn", name=f"ffn_dh_gate_{i}")
    dx1, dgain = _matmul_rmsnorm_bwd(du, prm["ffn_w_up"][i], dh, sv["x1"], prm["norm_ffn"][i:i + 1], dx2,
                                     name=f"ffn_dh_up_{i}")
    grads["norm_ffn"][i] = dgain[0]
    return dx1


def _mamba_fwd(x0, prm, plan):
    h = _rmsnorm_fwd(x0, prm["norm_mix"][0:1], name="mix_norm_fwd_0")
    z = _hosted(plan, "ssm_in_z", _matmul, h, prm["ssm_w_z"], mode="nt", name="ssm_in_z")
    xbc_pre = _hosted(plan, "ssm_in_xbc", _matmul, h, prm["ssm_w_xbc"], mode="nt", name="ssm_in_xbc")
    dt_raw = _matmul(h, prm["ssm_w_dt"], mode="nt", name="ssm_in_dt")
    xbc = _hosted(plan, "conv_fwd", _conv_fwd, xbc_pre, prm["ssm_conv_w"], prm["ssm_conv_b"])
    dt_bias = _pad_lanes(prm["ssm_dt_bias"])
    a_log = _pad_lanes(prm["ssm_a_log"])
    acs, dt_rep, acs_rep = _ssd_prep_fwd(dt_raw, dt_bias, a_log)
    acs_t = acs[:, :SSM_HEADS].T
    dskip_rep = _rep_heads(prm["ssm_d_skip"][0])
    y, hin_all, yn = _hosted(plan, "ssd_fwd", _ssd_fwd, xbc, dt_rep, acs_rep, acs_t, dskip_rep, z,
                             prm["ssm_norm_w"])
    x1, h_ffn = _matmul(yn, prm["ssm_w_out"], mode="nn", addend=x0, name="ssm_out", tm=512, tn=D_MODEL,
                        second=(_rmsnorm_rows, [prm["norm_ffn"][0:1]], BF16))
    sv = dict(x0=x0, h=h, z=z, xbc_pre=xbc_pre, dt_raw=dt_raw, xbc=xbc, dt_bias=dt_bias, dt_rep=dt_rep,
              acs_rep=acs_rep, acs_t=acs_t, dskip_rep=dskip_rep, y=y, hin_all=hin_all, yn=yn)
    return x1, h_ffn, sv


def _mamba_bwd(dx1, prm, sv, grads, plan):
    grads["ssm_w_out"] = _matmul_tn(sv["yn"], dx1, name="d_ssm_w_out")
    dy, dz, dnw = _hosted(plan, "gate_norm_bwd", _gate_norm_bwd, sv["y"], sv["z"], prm["ssm_norm_w"], dx1,
                          prm["ssm_w_out"])
    grads["ssm_norm_w"] = dnw
    a_rep = _rep_heads(-jnp.exp(prm["ssm_a_log"][0]))
    dxbc, ddt, da_rep, dds_rep = _hosted(plan, "ssd_bwd", _ssd_bwd, sv["xbc"], sv["dt_rep"], sv["acs_rep"],
                                             sv["acs_t"], sv["dskip_rep"], a_rep, sv["hin_all"], dy)
    grads["ssm_d_skip"] = _per_head(dds_rep)[None, :]
    grads["ssm_a_log"] = (_per_head(da_rep) * _per_head(a_rep))[None, :]
    ddt_raw, dbias = _ssd_prep_bwd(sv["dt_raw"], sv["dt_bias"], ddt)
    grads["ssm_dt_bias"] = dbias[:, :SSM_HEADS]
    du, dcw, dcb = _hosted(plan, "conv_bwd", _conv_bwd, sv["xbc_pre"], prm["ssm_conv_w"], prm["ssm_conv_b"], dxbc)
    grads["ssm_conv_w"] = dcw
    grads["ssm_conv_b"] = dcb
    h = sv["h"]
    grads["ssm_w_in"] = jnp.concatenate(
        [_matmul_tn(dz, h, name="d_ssm_w_z"), _matmul_tn(du, h, name="d_ssm_w_xbc"),
         _matmul_tn(ddt_raw, h, name="d_ssm_w_dt")[:SSM_HEADS]], axis=0)
    dh = _hosted(plan, "ssm_dh_z", _matmul, dz, prm["ssm_w_z"], mode="nn", name="ssm_dh_z")
    dh = _hosted(plan, "ssm_dh_xbc", _matmul, du, prm["ssm_w_xbc"], mode="nn", addend=dh, name="ssm_dh_xbc")
    dx0, dgain = _hosted(plan, "ssm_dh_dt", _matmul_rmsnorm_bwd, ddt_raw, prm["ssm_w_dt"], dh, sv["x0"],
                         prm["norm_mix"][0:1], dx1, name="ssm_dh_dt")
    grads["norm_mix"][0] = dgain[0]
    return dx0


def _attn_mixer_fwd(x0, h, prm, plan):
    n_heads = N_QKV_BLOCKS * ATT_HEADS
    gq = jnp.tile(prm["att_q_norm"], (1, n_heads))
    gk = jnp.tile(prm["att_k_norm"], (1, n_heads))
    qkv, qkn = _hosted(plan, "att_qkv", _matmul, h, prm["att_w_qkv"], mode="nt", name="att_qkv",
                       second=(_qk_normalised, [gq, gk], F32))
    outs, lses = [], []
    for g, (window, dil) in enumerate(DIL_PATTERNS):
        o_g, l_g = _attn_fwd(qkn, g, dil)
        outs.append(o_g)
        lses.append(l_g)
    o_f, l_rep, l_one, x1, h_ffn = _attn_out_fwd(outs, lses, prm["att_w_o"], x0, prm["norm_ffn"][1:2])
    sv = dict(x0=x0, h=h, qkv=qkv, qkn=qkn, gq2=gq[:, :LANES], gk2=gk[:, :LANES], o_f=o_f, l_rep=l_rep,
              l_one=l_one)
    return x1, h_ffn, sv


def _attn_mixer_bwd(dx1, prm, sv, grads, plan):
    grads["att_w_o"] = _matmul_tn(sv["o_f"], dx1, name="d_att_w_o")
    do, dl_rep, dl_one = _hosted(plan, "att_out_dx", _attn_out_bwd, dx1, prm["att_w_o"], sv["o_f"])
    blocks, dgq, dgk = [], [], []
    for g, (window, dil) in enumerate(DIL_PATTERNS):
        dq, dgq_g = _attn_bwd_dq(sv["qkv"], sv["qkn"], sv["gq2"], do, sv["l_rep"], dl_rep, g, dil)
        dk, dv, dgk_g = _attn_bwd_dkv(sv["qkv"], sv["qkn"], sv["gk2"], do, _rows_by_residue(sv["l_one"], dil),
                                      _rows_by_residue(dl_one, dil), g, dil)
        blocks += [dq, dk, dv]
        dgq.append(dgq_g)
        dgk.append(dgk_g)
    dqkv = jnp.concatenate(blocks, axis=1)

    def fold(parts):
        return jnp.stack(parts)[:, :, 0].reshape(-1, ATT_HEAD_DIM).sum(axis=0)[None, :]

    grads["att_q_norm"] = fold(dgq)
    grads["att_k_norm"] = fold(dgk)
    grads["att_w_qkv"] = _matmul_tn(dqkv, sv["h"], name="d_att_w_qkv")
    dx0, dgain = _hosted(plan, "att_qkv_dx", _matmul_rmsnorm_bwd, dqkv, prm["att_w_qkv"], None, sv["x0"],
                         prm["norm_mix"][1:2], dx1, name="att_qkv_dx")
    grads["norm_mix"][1] = dgain[0]
    return dx0


def _local_step(x, p, target, prm, plan=None):
    plan = plan or _NoOverlap()
    grads = {k: [None, None] for k in ("norm_mix", "norm_ffn", "ffn_w_gate", "ffn_w_up", "ffn_w_down",
                                       "ple_w_proj", "ple_w_gate")}
    plan.begin_backward(grads)
    x1, h1, sv_m = _mamba_fwd(x, prm, plan)
    (x3, h3), sv_f0 = _ffn_ple_fwd(x1, h1, p[0], prm, 0, plan, next_gain=prm["norm_mix"][1:2])
    x4, h4, sv_a = _attn_mixer_fwd(x3, h3, prm, plan)
    (dy, loss_row), sv_f1 = _ffn_ple_fwd(x4, h4, p[1], prm, 1, plan, target=target)
    dx4 = _ffn_ple_bwd(dy, p[1], prm, 1, sv_f1, grads, plan)
    dx3 = _attn_mixer_bwd(dx4, prm, sv_a, grads, plan)
    dx1 = _ffn_ple_bwd(dx3, p[0], prm, 0, sv_f0, grads, plan)
    dx0 = _mamba_bwd(dx1, prm, sv_m, grads, plan)
    return loss_row, dx0, grads


W_IN_SLAB_ROWS = 1312


def _position():
    return lax.axis_index("x"), lax.axis_index("y"), lax.axis_index("c")


def _other_chips(x, y):
    return [(1 - x, y), (x, 1 - y), (1 - x, 1 - y)]


def _remote(send_sems, recv_sems, k, src, dst, to):
    return pltpu.make_async_remote_copy(src_ref=src, dst_ref=dst, send_sem=send_sems.at[k], recv_sem=recv_sems.at[k],
                                        device_id=to, device_id_type=MESH)


def _gather_side(entries, whole=()):
    n, nw = len(entries), len(whole)

    def first_hop(ins, outs, send_sems, recv_sems):
        x, y, c = _position()
        cps = []
        for j, chip in enumerate(_other_chips(x, y)):
            for e in range(n):
                cps.append(_remote(send_sems, recv_sems, 6 * e + j, ins[e].at[c], outs[e].at[2 * x + y, c], (*chip, c)))
            for e in range(nw):
                cps.append(_remote(send_sems, recv_sems, 6 * n + 3 * e + j, ins[n + e], outs[n + e].at[2 * x + y],
                                   (*chip, c)))
        return cps

    def start(ins, outs, send_sems, recv_sems):
        for cp in first_hop(ins, outs, send_sems, recv_sems):
            cp.start()

    def finish(ins, outs, send_sems, recv_sems):
        x, y, c = _position()
        me, sibling = (x, y, c), (x, y, 1 - c)
        chips = _other_chips(x, y)
        passed_on = []
        for j, (px, py) in enumerate(chips):
            for e in range(n):
                landed = outs[e].at[2 * px + py, c]
                _remote(send_sems, recv_sems, 6 * e + j, landed, landed, me).wait_recv()
                passed_on.append(_remote(send_sems, recv_sems, 6 * e + 3 + j, landed, landed, sibling))
                passed_on[-1].start()
            for e in range(nw):
                landed = outs[n + e].at[2 * px + py]
                _remote(send_sems, recv_sems, 6 * n + 3 * e + j, landed, landed, me).wait_recv()
        for j, (px, py) in enumerate(chips):
            for e in range(n):
                passed = outs[e].at[2 * px + py, 1 - c]
                _remote(send_sems, recv_sems, 6 * e + 3 + j, passed, passed, me).wait_recv()
        for cp in first_hop(ins, outs, send_sems, recv_sems) + passed_on:
            cp.wait_send()

    shapes = [jax.ShapeDtypeStruct((N_CHIPS,) + a.shape, a.dtype) for a in list(entries) + list(whole)]
    return _Side(list(entries) + list(whole), shapes, 6 * n + 3 * nw, start, finish)


def _run_side(side, name):
    si, so = len(side.inputs), len(side.out_shapes)

    def body(*refs):
        ins, outs, send_sems, recv_sems = refs[:si], refs[si:si + so], refs[-2], refs[-1]
        side.start(ins, outs, send_sems, recv_sems)
        side.finish(ins, outs, send_sems, recv_sems)

    side.outputs = list(pl.pallas_call(
        body, name=name, in_specs=[ANY] * si, out_specs=[ANY] * so, out_shape=side.out_shapes,
        scratch_shapes=[pltpu.SemaphoreType.DMA((side.n_sems,)), pltpu.SemaphoreType.DMA((side.n_sems,))],
    )(*side.inputs))
    return side.outputs


def _swap_side(grads):
    n = len(grads)

    def copies(ins, outs, send_sems, recv_sems):
        x, y, c = _position()
        return [_remote(send_sems, recv_sems, e, ins[e].at[:, 1 - c], outs[e], (x, y, 1 - c)) for e in range(n)]

    def start(ins, outs, send_sems, recv_sems):
        for cp in copies(ins, outs, send_sems, recv_sems):
            cp.start()

    def finish(ins, outs, send_sems, recv_sems):
        for cp in copies(ins, outs, send_sems, recv_sems):
            cp.wait()

    shapes = [jax.ShapeDtypeStruct((N_CHIPS,) + g.shape[2:], g.dtype) for g in grads]
    return _Side(grads, shapes, n, start, finish)


def _chip_exchange_side(chipsums):
    n = len(chipsums)

    def copies(ins, outs, send_sems, recv_sems):
        x, y, c = _position()
        return [_remote(send_sems, recv_sems, 3 * e + j, ins[e].at[2 * tx + ty], outs[e].at[j], (tx, ty, c))
                for j, (tx, ty) in enumerate(_other_chips(x, y)) for e in range(n)]

    def start(ins, outs, send_sems, recv_sems):
        for cp in copies(ins, outs, send_sems, recv_sems):
            cp.start()

    def finish(ins, outs, send_sems, recv_sems):
        for cp in copies(ins, outs, send_sems, recv_sems):
            cp.wait()

    shapes = [jax.ShapeDtypeStruct((3,) + cs.shape[1:], cs.dtype) for cs in chipsums]
    return _Side(chipsums, shapes, 3 * n, start, finish)


def _share_side(totals):
    n = len(totals)

    def copies(ins, outs, send_sems, recv_sems):
        x, y, c = _position()
        return [_remote(send_sems, recv_sems, e, ins[e], outs[e], (x, y, 1 - c)) for e in range(n)]

    def start(ins, outs, send_sems, recv_sems):
        for cp in copies(ins, outs, send_sems, recv_sems):
            cp.start()

    def finish(ins, outs, send_sems, recv_sems):
        for cp in copies(ins, outs, send_sems, recv_sems):
            cp.wait()

    return _Side(totals, [jax.ShapeDtypeStruct(t.shape, t.dtype) for t in totals], n, start, finish)


def _reduce_rows(h):
    return h if h <= 704 else h // 2


def _add_sibling(grad, recv, c_idx, *, name):
    _, _, h, cw = grad.shape
    th = _reduce_rows(h)

    def body(c_ref, g_ref, r_ref, o_ref):
        o_ref[...] = (g_ref[...] + r_ref[...]).astype(BF16)

    return pl.pallas_call(
        body, name=name,
        grid_spec=pltpu.PrefetchScalarGridSpec(
            num_scalar_prefetch=1, grid=(N_CHIPS, h // th),
            in_specs=[pl.BlockSpec((None, None, th, cw), lambda s, i, c_ref: (s, c_ref[0], i, 0)),
                      pl.BlockSpec((None, th, cw), lambda s, i, c_ref: (s, i, 0))],
            out_specs=pl.BlockSpec((None, th, cw), lambda s, i, c_ref: (s, i, 0))),
        out_shape=jax.ShapeDtypeStruct((N_CHIPS, h, cw), BF16),
        compiler_params=_params("parallel", "parallel"),
    )(c_idx, grad, recv)


def _add_chips(chipsum, recv, s_idx, *, name):
    _, h, cw = chipsum.shape
    th = _reduce_rows(h)

    def body(s_ref, own_ref, r_ref, o_ref):
        o_ref[...] = ((own_ref[...].astype(F32) + r_ref[0].astype(F32)) + r_ref[1].astype(F32)) + r_ref[2].astype(F32)

    return pl.pallas_call(
        body, name=name,
        grid_spec=pltpu.PrefetchScalarGridSpec(
            num_scalar_prefetch=1, grid=(h // th,),
            in_specs=[pl.BlockSpec((None, th, cw), lambda i, s_ref: (s_ref[0], i, 0)),
                      pl.BlockSpec((3, th, cw), lambda i, s_ref: (0, i, 0))],
            out_specs=pl.BlockSpec((th, cw), lambda i, s_ref: (i, 0))),
        out_shape=jax.ShapeDtypeStruct((h, cw), F32),
        compiler_params=_params("parallel"),
    )(s_idx, chipsum, recv)


def _adamw_math(w, g, m, v):
    m = ADAM_B1 * m + (1.0 - ADAM_B1) * g
    v = ADAM_B2 * v + (1.0 - ADAM_B2) * (g * g)
    m_hat = m / (1.0 - ADAM_B1 ** ADAM_STEP)
    v_hat = v / (1.0 - ADAM_B2 ** ADAM_STEP)
    delta = -ADAM_LR * (m_hat / (jnp.sqrt(v_hat) + ADAM_EPS) + ADAM_WD * w)
    return delta, m, v


ADAM_TILE_ELEMS = 256 * 1024


def _adamw(w, g, m, v, *, name):
    layers, rows, cols = w.shape
    tr = rows
    for cand in range(8, rows, 8):
        if rows % cand == 0 and cand * cols <= ADAM_TILE_ELEMS:
            tr = cand
    if rows * cols <= ADAM_TILE_ELEMS:
        tr = rows

    def body(w_ref, g_ref, m_ref, v_ref, d_ref, nm_ref, nv_ref):
        d, nm, nv = _adamw_math(w_ref[...], g_ref[...], m_ref[...], v_ref[...])
        d_ref[...] = d
        nm_ref[...] = nm
        nv_ref[...] = nv

    blk = pl.BlockSpec((None, tr, cols), lambda l, i: (l, i, 0))
    sds = jax.ShapeDtypeStruct(w.shape, F32)
    return pl.pallas_call(
        body, name=name, grid=(layers, rows // tr), in_specs=[blk] * 4, out_specs=[blk] * 3, out_shape=[sds] * 3,
        compiler_params=_params("parallel", "parallel"),
    )(w, g, m, v)


SMALL_LAYOUT = (("loss", 1), ("norm_mix", 16), ("norm_ffn", 16), ("ssm_conv_b", 24), ("ssm_dt_bias", 1),
                ("ssm_a_log", 1), ("ssm_d_skip", 1), ("ssm_norm_w", 16), ("att_q_norm", 1), ("att_k_norm", 1),
                ("conv_w_full", 96))
SMALL_ROWS = 176
N_DEVICES = 8


def _small_packs(dicts):
    parts = []
    for values in dicts:
        for name, rows in SMALL_LAYOUT:
            flat = values[name].reshape(-1).astype(F32)
            parts.append(jnp.pad(flat, (0, rows * LANES - flat.shape[0])).reshape(rows, LANES))
        used = sum(r for _, r in SMALL_LAYOUT)
        parts.append(jnp.zeros((SMALL_ROWS - used, LANES), F32))
    return jnp.concatenate(parts, axis=0).reshape(len(dicts), SMALL_ROWS, LANES)


def _small_unpack(pack, shapes):
    out, off = {}, 0
    for name, rows in SMALL_LAYOUT:
        shape = shapes[name]
        n = math.prod(shape)
        out[name] = pack[off:off + rows].reshape(-1)[:n].reshape(shape)
        off += rows
    return out


def _small_allreduce_adamw(g, w, m, v):
    def body(g_ref, w_ref, m_ref, v_ref, gs_ref, d_ref, nm_ref, nv_ref, buf, send_sems, recv_sems):
        x, y, c = _position()
        pos = (x, y, c)
        me = 4 * x + 2 * y + c
        buf[me] = g_ref[...]
        peers = []
        for k in range(1, N_DEVICES):
            bits = ((k >> 2) & 1, (k >> 1) & 1, k & 1)
            peers.append(tuple(1 - p if b else p for p, b in zip(pos, bits)))
        cps = [pltpu.make_async_remote_copy(src_ref=g_ref, dst_ref=buf.at[me], send_sem=send_sems.at[k],
                                            recv_sem=recv_sems.at[k], device_id=peer, device_id_type=MESH)
               for k, peer in enumerate(peers)]
        for cp in cps:
            cp.start()
        for k, (px, py, pc) in enumerate(peers):
            pltpu.make_async_remote_copy(src_ref=g_ref, dst_ref=buf.at[4 * px + 2 * py + pc],
                                         send_sem=send_sems.at[k], recv_sem=recv_sems.at[k],
                                         device_id=(px, py, pc), device_id_type=MESH).wait_recv()
        for cp in cps:
            cp.wait_send()
        total = buf[0]
        for dev in range(1, N_DEVICES):
            total = total + buf[dev]
        gs_ref[...] = total
        d, nm, nv = _adamw_math(w_ref[...], total, m_ref[...], v_ref[...])
        d_ref[...] = d
        nm_ref[...] = nm
        nv_ref[...] = nv

    vm = pl.BlockSpec(memory_space=pltpu.VMEM)
    sds = jax.ShapeDtypeStruct((SMALL_ROWS, LANES), F32)
    return pl.pallas_call(
        body, name="small_allreduce_adamw", in_specs=[vm] * 4, out_specs=[vm] * 4, out_shape=[sds] * 4,
        scratch_shapes=[pltpu.VMEM((N_DEVICES, SMALL_ROWS, LANES), F32),
                        pltpu.SemaphoreType.DMA((N_DEVICES - 1,)), pltpu.SemaphoreType.DMA((N_DEVICES - 1,))],
    )(g, w, m, v)


SMALL = tuple(n for n, _ in SMALL_LAYOUT if n not in ("loss", "conv_w_full"))
WEIGHTS = ("norm_mix", "norm_ffn", "ssm_w_in", "ssm_conv_w", "ssm_conv_b", "ssm_dt_bias", "ssm_a_log", "ssm_d_skip",
           "ssm_norm_w", "ssm_w_out", "att_w_qkv", "att_q_norm", "att_k_norm", "att_w_o", "ffn_w_gate", "ffn_w_up",
           "ffn_w_down", "ple_w_proj", "ple_w_gate")
COLUMN_SHARDED = ("ssm_w_in", "att_w_qkv", "ffn_w_gate", "ffn_w_up", "ple_w_proj")
LAYERED = ("ffn_w_gate", "ffn_w_up", "ffn_w_down", "ple_w_proj", "ple_w_gate")
UPDATED_TRANSPOSED = ("ssm_w_in", "ffn_w_gate", "ffn_w_up")
GATHER_ORDER = ("ssm_w_in", "ssm_w_out", "att_w_qkv", "att_w_o", "ffn_w_gate", "ffn_w_up", "ffn_w_down",
                "ple_w_proj", "ple_w_gate")


def _layers(n):
    return (0, 1) if n in LAYERED else (None,)


def _tag(key):
    return key[0] if key[1] is None else f"{key[0]}_{key[1]}"


QKV_PARTS = 3


def _weight_slab(w, key):
    n, i = key
    if n == "att_w_qkv":
        a = w[n][0].T
        rows = a.shape[0] // QKV_PARTS
        a = a[i * rows:(i + 1) * rows]
    else:
        a = w[n][0 if i is None else i]
        a = a.T if n in COLUMN_SHARDED else a
    if n == "ssm_w_in":
        a = jnp.pad(a, ((0, W_IN_SLAB_ROWS - a.shape[0]), (0, 0)))
    return a.reshape(2, a.shape[0] // 2, a.shape[1]).astype(BF16)


def _install(prm, key, gathered, own, s_me):
    n, i = key
    full = lax.dynamic_update_slice(gathered, own[None], (s_me, 0, 0, 0))
    full = full.reshape(N_CHIPS, 2 * full.shape[2], full.shape[3])
    if n == "att_w_qkv":
        parts = prm.setdefault("att_w_qkv_parts", {})
        parts[i] = full
        if len(parts) == QKV_PARTS:
            prm[n] = jnp.stack([parts[j] for j in range(QKV_PARTS)], axis=1).reshape(-1, D_MODEL)
        return
    if n == "ssm_w_in":
        rows = (D_INNER + CONV_DIM + SSM_HEADS) // N_CHIPS
        w_in_t = full[:, :rows].reshape(N_CHIPS * rows, D_MODEL)
        prm["ssm_w_z"] = w_in_t[:D_INNER]
        prm["ssm_w_xbc"] = w_in_t[D_INNER:D_INNER + CONV_DIM]
        prm["ssm_w_dt"] = jnp.pad(w_in_t[D_INNER + CONV_DIM:], ((0, LANES - SSM_HEADS), (0, 0)))
        return
    full = full.reshape(N_CHIPS * full.shape[1], full.shape[2])
    if i is None:
        prm[n] = full
    else:
        prm.setdefault(n, [None, None])[i] = full


def _grad_slab(grads, key):
    n, i = key
    g = grads[n] if i is None else grads[n][i]
    if n == "ssm_w_in":
        g = jnp.pad(g.reshape(N_CHIPS, g.shape[0] // N_CHIPS, D_MODEL),
                    ((0, 0), (0, W_IN_SLAB_ROWS - g.shape[0] // N_CHIPS), (0, 0)))
    rows = g.size // (N_CHIPS * g.shape[-1])
    return g.reshape(N_CHIPS, 2, rows // 2, g.shape[-1])


def _natural_shard(n, reduced, shape):
    def one(r):
        if n == "ssm_w_in":
            r = r[:shape[-1]]
        return r.T if n in COLUMN_SHARDED else r
    if n in LAYERED:
        return jnp.stack([one(r) for r in reduced]).reshape(shape)
    return one(reduced[0]).reshape(shape)


def kernel(x, p, norm_mix, norm_ffn, ssm_w_in, ssm_conv_w, ssm_conv_b, ssm_dt_bias, ssm_a_log, ssm_d_skip, ssm_norm_w, ssm_w_out, att_w_qkv, att_q_norm, att_k_norm, att_w_o, ffn_w_gate, ffn_w_up, ffn_w_down, ple_w_proj, ple_w_gate, loss_target, m_norm_mix, m_norm_ffn, m_ssm_w_in, m_ssm_conv_w, m_ssm_conv_b, m_ssm_dt_bias, m_ssm_a_log, m_ssm_d_skip, m_ssm_norm_w, m_ssm_w_out, m_att_w_qkv, m_att_q_norm, m_att_k_norm, m_att_w_o, m_ffn_w_gate, m_ffn_w_up, m_ffn_w_down, m_ple_w_proj, m_ple_w_gate, v_norm_mix, v_norm_ffn, v_ssm_w_in, v_ssm_conv_w, v_ssm_conv_b, v_ssm_dt_bias, v_ssm_a_log, v_ssm_d_skip, v_ssm_norm_w, v_ssm_w_out, v_att_w_qkv, v_att_q_norm, v_att_k_norm, v_att_w_o, v_ffn_w_gate, v_ffn_w_up, v_ffn_w_down, v_ple_w_proj, v_ple_w_gate):
    given = dict(locals())
    w = {n: given[n] for n in WEIGHTS}
    m = {n: given["m_" + n] for n in WEIGHTS}
    v = {n: given["v_" + n] for n in WEIGHTS}
    c_idx = lax.axis_index("c").astype(jnp.int32).reshape(1)
    s_idx = (2 * lax.axis_index("x") + lax.axis_index("y")).astype(jnp.int32).reshape(1)

    s_me = 2 * lax.axis_index("x") + lax.axis_index("y")
    first_core = lax.axis_index("c") == 0

    qkv_parts = [("att_w_qkv", j) for j in range(QKV_PARTS)]
    gather_plan = {
        "ssm_in_z": [("ssm_w_out", None)],
        "ssm_in_xbc": [("ffn_w_gate", 0)],
        "conv_fwd": [("ffn_w_up", 0)],
        "ssd_fwd": [("ffn_w_down", 0), ("ple_w_proj", 0), ("ple_w_gate", 0), ("att_w_o", None)],
        "swiglu_fwd_0": qkv_parts[:2],
        "ffn_down_0": qkv_parts[2:],
        "att_qkv": [(n, 1) for n in LAYERED],
    }
    mamba = [("ssm_w_in", None)]
    own = {k: _weight_slab(w, k) for k in mamba + sum(gather_plan.values(), [])}
    prm = {n: w[n] for n in SMALL}

    def land(group, outputs):
        for k, g in zip(group, outputs):
            _install(prm, k, g, own[k], s_me)

    first = _gather_side([own[k] for k in mamba], whole=[ssm_conv_w[0]])
    _run_side(first, "gather_mamba")
    land(mamba, first.outputs)
    conv = lax.dynamic_update_slice(first.outputs[-1], ssm_conv_w, (s_me, 0, 0))
    prm["ssm_conv_w"] = conv.transpose(1, 0, 2).reshape(CONV_WIDTH, CONV_DIM)

    ffn1 = [(n, 1) for n in LAYERED]
    attention = [("att_w_qkv", None), ("att_w_o", None)]
    ffn0 = [(n, 0) for n in LAYERED] + [("ssm_w_out", None)]
    reduce_plan = {"att_out_dx": [("swap", ffn1)], "att_qkv_dx": [("exchange", ffn1)],
                   "swiglu_bwd_0": [("swap", attention)], "gate_norm_bwd": [("swap", ffn0)],
                   "ssd_bwd": [("exchange", attention), ("exchange", ffn0)],
                   "ssm_dh_z": [("swap", mamba)], "ssm_dh_xbc": [("exchange", mamba)]}
    state = {}

    def swap_side(group):
        state[_tag(group[0]), "g4"] = g4 = [_grad_slab(state["grads"], k) for k in group]
        return _swap_side(g4)

    def add_siblings(group, from_sibling):
        state[_tag(group[0]), "chipsums"] = [
            _add_sibling(g, r, c_idx, name="add_sibling_" + _tag(k))
            for g, r, k in zip(state[_tag(group[0]), "g4"], from_sibling, group)]

    def exchange_side(group):
        return _chip_exchange_side(state[_tag(group[0]), "chipsums"])

    def add_chips(group, from_chips):
        for k, cs, r in zip(group, state[_tag(group[0]), "chipsums"], from_chips):
            state["total", k] = _add_chips(cs, r, s_idx, name="add_chips_" + _tag(k))

    class Plan(_NoOverlap):
        def __init__(self):
            self.carried = {host: _gather_side([own[k] for k in group]) for host, group in gather_plan.items()}

        def begin_backward(self, grads):
            state["grads"] = grads

        def side(self, host):
            if host in reduce_plan:
                self.parts = [swap_side(group) if step == "swap" else exchange_side(group)
                              for step, group in reduce_plan[host]]
                self.carried[host] = _sides_together(self.parts)
            elif host == share_host:
                self.carried[host] = _share_side([state["total", k] for k in order])
            return self.carried.get(host)

        def after(self, host):
            if host in gather_plan:
                land(gather_plan[host], self.carried[host].outputs)
            elif host in reduce_plan:
                _share_out(self.carried[host], self.parts)
                for (step, group), part in zip(reduce_plan[host], self.parts):
                    (add_siblings if step == "swap" else add_chips)(group, part.outputs)
            elif host == share_host:
                state["shared"] = self.carried[host].outputs

    order = mamba + ffn0 + attention + ffn1
    share_host = "ssm_dh_dt"
    loss_row, dx, grads = _local_step(x[0], p[:, 0], loss_target[0], prm, Plan())

    reduced = {}
    for k, theirs in zip(order, state["shared"]):
        lo = jnp.where(first_core, state["total", k], theirs)
        hi = jnp.where(first_core, theirs, state["total", k])
        reduced.setdefault(k[0], {})[k[1]] = jnp.concatenate([lo, hi], axis=0)
    reduced = {n: [by_layer[i] for i in _layers(n)] for n, by_layer in reduced.items()}

    grad, delta, new_m, new_v = {}, {}, {}, {}
    for n in GATHER_ORDER:
        if n in UPDATED_TRANSPOSED:
            flip = lambda a: a.transpose(0, 2, 1)
            cols = w[n].shape[-1]
            g_t = jnp.stack([r[:cols] for r in reduced[n]])
            grad[n] = flip(g_t)
            delta[n], new_m[n], new_v[n] = [flip(o) for o in _adamw(flip(w[n]), g_t, flip(m[n]), flip(v[n]),
                                                                    name="adamw_" + n)]
            continue
        grad[n] = _natural_shard(n, reduced[n], w[n].shape)
        delta[n], new_m[n], new_v[n] = _adamw(w[n], grad[n], m[n], v[n], name="adamw_" + n)

    small_g = {n: (jnp.stack(grads[n]) if isinstance(grads[n], list) else grads[n]) for n in SMALL}
    small_g["loss"] = loss_row
    small_g["conv_w_full"] = grads["ssm_conv_w"]
    zero = {"loss": jnp.zeros((1, LANES), F32), "conv_w_full": jnp.zeros((CONV_WIDTH, CONV_DIM), F32)}
    packs = _small_packs([small_g, {**w, **zero}, {**m, **zero}, {**v, **zero}])
    outs = _small_allreduce_adamw(packs[0], packs[1], packs[2], packs[3])
    shapes = {n: w[n].shape for n in SMALL}
    shapes["loss"] = (1, LANES)
    shapes["conv_w_full"] = (CONV_WIDTH, CONV_DIM)
    sg, sd, sm, sv = [_small_unpack(o, shapes) for o in outs]
    for n in SMALL:
        grad[n], delta[n], new_m[n], new_v[n] = sg[n], sd[n], sm[n], sv[n]
    loss = sg["loss"][0, 0]
    conv_cols = CONV_DIM // N_CHIPS
    grad["ssm_conv_w"] = lax.dynamic_slice(sg["conv_w_full"], (0, s_me * conv_cols), (CONV_WIDTH, conv_cols))[None]
    delta["ssm_conv_w"], new_m["ssm_conv_w"], new_v["ssm_conv_w"] = _adamw(
        ssm_conv_w, grad["ssm_conv_w"], m_ssm_conv_w, v_ssm_conv_w, name="adamw_ssm_conv_w")

    return (loss, dx[None], *[grad[n] for n in WEIGHTS], *[delta[n] for n in WEIGHTS],
            *[new_m[n] for n in WEIGHTS], *[new_v[n] for n in WEIGHTS])
```

```python
import math

import jax
import jax.numpy as jnp
from jax import lax
from jax.experimental import pallas as pl
from jax.experimental.pallas import tpu as pltpu

F32 = jnp.float32
BF16 = jnp.bfloat16
HIGHEST = lax.Precision.HIGHEST

NORM_EPS = 1e-6
ADAM_LR, ADAM_B1, ADAM_B2, ADAM_EPS, ADAM_WD, ADAM_STEP = 0.001, 0.9, 0.999, 1e-08, 0.01, 10

D_MODEL = 1024
D_INNER = 2048
SSM_HEADS = 32
SSM_HEAD_DIM = 64
SSM_GROUPS = 4
SSM_STATE = 128
SSD_CHUNK = 128
CONV_DIM = 3072
CONV_WIDTH = 4
ATT_HEADS = 16
ATT_HEAD_DIM = 64
DIL_PATTERNS = ((128, 1), (512, 4), (2048, 16))
ATT_BLOCK = 128
FFN_HIDDEN = 2816
PLE_DIM = 256

LANES = 128
V7X_VMEM_LIMIT = 56 * 1024 * 1024
NEG_BIG = -1e30

N_CHIPS = 4


def _params(*sem):
    return pltpu.CompilerParams(dimension_semantics=sem, vmem_limit_bytes=V7X_VMEM_LIMIT)


def _tile(n, pref):
    if n <= pref:
        return n
    best = None
    for t in range(LANES, pref + 1, LANES):
        if n % t == 0:
            best = t
    assert best is not None, (n, pref)
    return best


def _sigmoid(v):
    return 1.0 / (1.0 + jnp.exp(-v))


def _dot(a, b):
    return jnp.dot(a, b, preferred_element_type=F32)


def _dot_nt(a, b):
    return lax.dot_general(a, b, (((1,), (1,)), ((), ())), preferred_element_type=F32)


def _dot_tn(a, b):
    return lax.dot_general(a, b, (((0,), (0,)), ((), ())), preferred_element_type=F32)


def _head_block_diag():
    i = lax.broadcasted_iota(jnp.int32, (LANES, LANES), 0) // ATT_HEAD_DIM
    j = lax.broadcasted_iota(jnp.int32, (LANES, LANES), 1) // ATT_HEAD_DIM
    return (i == j).astype(BF16)


def _split_dot(ones, z):
    hi = z.astype(BF16)
    lo = (z - hi.astype(F32)).astype(BF16)
    return _dot(ones, hi) + _dot(ones, lo)


def _head_sums(z, bd, terms=2):
    hi = z.astype(BF16)
    lo = (z - hi.astype(F32)).astype(BF16) if terms == 2 else None
    parts = []
    for t in range(z.shape[1] // LANES):
        sl = slice(t * LANES, (t + 1) * LANES)
        part = _dot(hi[:, sl], bd)
        parts.append(part + _dot(lo[:, sl], bd) if terms == 2 else part)
    return parts[0] if len(parts) == 1 else jnp.concatenate(parts, axis=1)


def _lane_lt64(rows):
    return lax.broadcasted_iota(jnp.int32, (rows, LANES), 1) < ATT_HEAD_DIM


MESH = pl.DeviceIdType.MESH
ANY = pl.BlockSpec(memory_space=pl.ANY)


class _Side:
    def __init__(self, inputs, out_shapes, n_sems, start, finish):
        self.inputs, self.out_shapes, self.n_sems = list(inputs), list(out_shapes), n_sems
        self.start, self.finish = start, finish
        self.outputs = None


class _SemaphoresFrom:
    def __init__(self, sems, first):
        self.sems, self.first = sems, first

    @property
    def at(self):
        return self

    def __getitem__(self, k):
        return self.sems.at[self.first + k]


def _sides_together(sides):
    def run(step):
        def both(ins, outs, send_sems, recv_sems):
            i = o = k = 0
            for s in sides:
                ni, no = len(s.inputs), len(s.out_shapes)
                getattr(s, step)(ins[i:i + ni], outs[o:o + no], _SemaphoresFrom(send_sems, k),
                                 _SemaphoresFrom(recv_sems, k))
                i, o, k = i + ni, o + no, k + s.n_sems
        return both

    return _Side(sum([s.inputs for s in sides], []), sum([s.out_shapes for s in sides], []),
                 sum(s.n_sems for s in sides), run("start"), run("finish"))


def _share_out(together, sides):
    o = 0
    for s in sides:
        s.outputs = together.outputs[o:o + len(s.out_shapes)]
        o += len(s.out_shapes)


def _call(body, side, *, name, grid, in_specs, out_specs, out_shape, scratch_shapes, semantics, args):
    in_specs, out_specs, out_shape = list(in_specs), list(out_specs), list(out_shape)
    scratch_shapes = list(scratch_shapes)
    if side is None:
        return pl.pallas_call(body, name=name, grid=grid, in_specs=in_specs, out_specs=out_specs,
                              out_shape=out_shape, scratch_shapes=scratch_shapes,
                              compiler_params=_params(*semantics))(*args)
    ni, no, ns = len(in_specs), len(out_specs), len(scratch_shapes)
    si, so = len(side.inputs), len(side.out_shapes)

    def hosted(*refs):
        ins, s_ins = refs[:ni], refs[ni:ni + si]
        outs, s_outs = refs[ni + si:ni + si + no], refs[ni + si + no:ni + si + no + so]
        scratch = refs[ni + si + no + so:ni + si + no + so + ns]
        send_sems, recv_sems = refs[-2], refs[-1]
        first = pl.program_id(0) == 0
        last = pl.program_id(0) == grid[0] - 1
        for axis in range(1, len(grid)):
            first = jnp.logical_and(first, pl.program_id(axis) == 0)
            last = jnp.logical_and(last, pl.program_id(axis) == grid[axis] - 1)

        @pl.when(first)
        def _():
            side.start(s_ins, s_outs, send_sems, recv_sems)

        body(*ins, *outs, *scratch)

        @pl.when(last)
        def _():
            side.finish(s_ins, s_outs, send_sems, recv_sems)

    res = pl.pallas_call(
        hosted, name=name, grid=grid, in_specs=in_specs + [ANY] * si, out_specs=out_specs + [ANY] * so,
        out_shape=out_shape + side.out_shapes,
        scratch_shapes=scratch_shapes + [pltpu.SemaphoreType.DMA((side.n_sems,)),
                                         pltpu.SemaphoreType.DMA((side.n_sems,))],
        compiler_params=_params(*["arbitrary"] * len(grid)),
    )(*args, *side.inputs)
    side.outputs = list(res[no:])
    return list(res[:no])


def _matmul(a, b, *, mode, name, out_dtype=F32, addend=None, tm=1024, tn=512, tk_max=3072, side=None, second=None):
    m, k = a.shape
    if mode == "nn":
        k2, n = b.shape
    else:
        n, k2 = b.shape
    assert k == k2, (a.shape, b.shape, mode)
    tm, tn, tk = _tile(m, tm), _tile(n, tn), _tile(k, tk_max)
    nk = k // tk
    has_add = addend is not None
    n_rows = len(second[1]) if second else 0
    n_out = 2 if second else 1

    def body(*refs):
        a_ref, b_ref = refs[0], refs[1]
        add_ref = refs[2] if has_add else None
        row_refs = refs[2 + has_add:2 + has_add + n_rows]
        o_ref, acc_ref = refs[-1 - n_out], refs[-1]
        kk = pl.program_id(2)
        col_tile = pl.program_id(1)
        av = a_ref[...].astype(BF16)
        bv = b_ref[...].astype(BF16)
        part = _dot(av, bv) if mode == "nn" else _dot_nt(av, bv)

        @pl.when(kk == 0)
        def _():
            acc_ref[...] = part

        @pl.when(kk > 0)
        def _():
            acc_ref[...] += part

        @pl.when(kk == nk - 1)
        def _():
            res = acc_ref[...]
            if has_add:
                res = res + add_ref[...]
            o_ref[...] = res.astype(out_dtype)
            if second:
                refs[-2][...] = second[0](res, col_tile, *row_refs).astype(second[2])

    a_spec = pl.BlockSpec((tm, tk), lambda i, j, kk: (i, kk))
    if mode == "nn":
        b_spec = pl.BlockSpec((tk, tn), lambda i, j, kk: (kk, j))
    else:
        b_spec = pl.BlockSpec((tn, tk), lambda i, j, kk: (j, kk))
    tile = pl.BlockSpec((tm, tn), lambda i, j, kk: (i, j))
    in_specs = [a_spec, b_spec]
    args = [a, b]
    if has_add:
        in_specs.append(tile)
        args.append(addend)
    if second:
        in_specs += [pl.BlockSpec((1, tn), lambda i, j, kk: (0, j))] * n_rows
        args += list(second[1])
    outs = _call(
        body, side, name=name, grid=(m // tm, n // tn, nk),
        in_specs=in_specs, out_specs=[tile] * n_out,
        out_shape=[jax.ShapeDtypeStruct((m, n), out_dtype)] + ([jax.ShapeDtypeStruct((m, n), second[2])] if second
                                                                 else []),
        scratch_shapes=[pltpu.VMEM((tm, tn), F32)],
        semantics=("parallel", "parallel", "arbitrary"), args=args,
    )
    return outs if second else outs[0]


def _matmul_tn(a, b, *, name, tm=1408, tn=512, tk=1024):
    t, m = a.shape
    t2, n = b.shape
    assert t == t2
    tm, tn, tk = _tile(m, tm), _tile(n, tn), _tile(t, tk)

    def body(a_ref, b_ref, o_ref):
        part = _dot_tn(a_ref[...].astype(BF16), b_ref[...].astype(BF16))

        @pl.when(pl.program_id(2) == 0)
        def _():
            o_ref[...] = part

        @pl.when(pl.program_id(2) > 0)
        def _():
            o_ref[...] += part

    return pl.pallas_call(
        body, name=name, grid=(m // tm, n // tn, t // tk),
        in_specs=[pl.BlockSpec((tk, tm), lambda i, j, kk: (kk, i)),
                  pl.BlockSpec((tk, tn), lambda i, j, kk: (kk, j))],
        out_specs=pl.BlockSpec((tm, tn), lambda i, j, kk: (i, j)),
        out_shape=jax.ShapeDtypeStruct((m, n), F32),
        compiler_params=_params("parallel", "parallel", "arbitrary"),
    )(a, b)


def _rmsnorm_rows(tile, j, gain_ref):
    r = lax.rsqrt(jnp.mean(tile * tile, axis=-1, keepdims=True) + NORM_EPS)
    return tile * r * gain_ref[...]


def _rmsnorm_fwd(x, gain, *, name):
    t, d = x.shape
    tm = _tile(t, 512)

    def body(x_ref, g_ref, o_ref):
        xv = x_ref[...]
        r = lax.rsqrt(jnp.mean(xv * xv, axis=-1, keepdims=True) + NORM_EPS)
        o_ref[...] = (xv * r * g_ref[...]).astype(BF16)

    return pl.pallas_call(
        body, name=name, grid=(t // tm,),
        in_specs=[pl.BlockSpec((tm, d), lambda i: (i, 0)), pl.BlockSpec((1, d), lambda i: (0, 0))],
        out_specs=pl.BlockSpec((tm, d), lambda i: (i, 0)),
        out_shape=jax.ShapeDtypeStruct((t, d), BF16),
        compiler_params=_params("parallel"),
    )(x, gain)


def _matmul_rmsnorm_bwd(a, b, addend, x, gain, dres, *, name, side=None, tm=512, tk_max=3072, more=None):
    m, k = a.shape
    d = b.shape[1]
    tm, tk = _tile(m, tm), _tile(k, tk_max)
    nk = k // tk

    def body(a_ref, b_ref, *rest):
        add_ref = rest[2 if more else 0] if addend is not None else None
        x_ref, g_ref, dres_ref, dx_ref, dg_ref, acc_ref = rest[-6:]
        i, kk = pl.program_id(0), pl.program_id(1)
        part = _dot(a_ref[...].astype(BF16), b_ref[...].astype(BF16))
        if more:
            part = part + _dot(rest[0][...].astype(BF16), rest[1][...].astype(BF16))

        @pl.when(kk == 0)
        def _():
            acc_ref[...] = part

        @pl.when(kk > 0)
        def _():
            acc_ref[...] += part

        @pl.when(kk == nk - 1)
        def _():
            dyv = acc_ref[...] if addend is None else acc_ref[...] + add_ref[...]
            xv = x_ref[...]
            r = lax.rsqrt(jnp.mean(xv * xv, axis=-1, keepdims=True) + NORM_EPS)
            xh = xv * r
            dxh = dyv * g_ref[...]
            mean = jnp.mean(dxh * xh, axis=-1, keepdims=True)
            dx_ref[...] = dres_ref[...] + r * (dxh - xh * mean)
            gain_part = jnp.sum(dyv * xh, axis=0, keepdims=True)

            @pl.when(i == 0)
            def _():
                dg_ref[...] = gain_part

            @pl.when(i > 0)
            def _():
                dg_ref[...] += gain_part

    row = pl.BlockSpec((tm, d), lambda i, kk: (i, 0))
    vec = pl.BlockSpec((1, d), lambda i, kk: (0, 0))
    return _call(
        body, side, name=name, grid=(m // tm, nk),
        in_specs=[pl.BlockSpec((tm, tk), lambda i, kk: (i, kk)), pl.BlockSpec((tk, d), lambda i, kk: (kk, 0))]
        * (2 if more else 1) + ([row] if addend is not None else []) + [row, vec, row],
        out_specs=[row, vec],
        out_shape=[jax.ShapeDtypeStruct((m, d), F32), jax.ShapeDtypeStruct((1, d), F32)],
        scratch_shapes=[pltpu.VMEM((tm, d), F32)],
        semantics=("arbitrary", "arbitrary"),
        args=(a, b) + (tuple(more) if more else ()) + ((addend,) if addend is not None else ()) + (x, gain, dres),
    )


def _swiglu_fwd(h, w_gate_t, w_up_t, *, name, side=None):
    t, d = h.shape
    f = w_gate_t.shape[0]
    tm, tn = _tile(t, 1024), _tile(f, 256)

    def body(h_ref, wg_ref, wu_ref, g_ref, u_ref, a_ref):
        hv = h_ref[...]
        g = _dot_nt(hv, wg_ref[...])
        u = _dot_nt(hv, wu_ref[...])
        g_ref[...] = g.astype(BF16)
        u_ref[...] = u.astype(BF16)
        a_ref[...] = (g * _sigmoid(g) * u).astype(BF16)

    wspec = pl.BlockSpec((tn, d), lambda i, j: (j, 0))
    ospec = pl.BlockSpec((tm, tn), lambda i, j: (i, j))
    return _call(
        body, side, name=name, grid=(t // tm, f // tn),
        in_specs=[pl.BlockSpec((tm, d), lambda i, j: (i, 0)), wspec, wspec],
        out_specs=[ospec, ospec, ospec],
        out_shape=[jax.ShapeDtypeStruct((t, f), BF16), jax.ShapeDtypeStruct((t, f), BF16),
                   jax.ShapeDtypeStruct((t, f), BF16)],
        scratch_shapes=[], semantics=("parallel", "parallel"), args=(h, w_gate_t, w_up_t),
    )


def _swiglu_bwd(dx, w_down, g, u, *, name, side=None):
    t, d = dx.shape
    f = w_down.shape[0]
    tm, tn = _tile(t, 1024), _tile(f, 256)

    def body(dx_ref, wd_ref, g_ref, u_ref, dg_ref, du_ref):
        dact = _dot_nt(dx_ref[...].astype(BF16), wd_ref[...])
        gv, uv = g_ref[...].astype(F32), u_ref[...].astype(F32)
        sg = _sigmoid(gv)
        dg_ref[...] = (dact * uv * sg * (1.0 + gv * (1.0 - sg))).astype(BF16)
        du_ref[...] = (dact * gv * sg).astype(BF16)

    ospec = pl.BlockSpec((tm, tn), lambda i, j: (i, j))
    return _call(
        body, side, name=name, grid=(t // tm, f // tn),
        in_specs=[pl.BlockSpec((tm, d), lambda i, j: (i, 0)), pl.BlockSpec((tn, d), lambda i, j: (j, 0)),
                  ospec, ospec],
        out_specs=[ospec, ospec],
        out_shape=[jax.ShapeDtypeStruct((t, f), BF16), jax.ShapeDtypeStruct((t, f), BF16)],
        scratch_shapes=[], semantics=("parallel", "parallel"), args=(dx, w_down, g, u),
    )


def _ple_fwd(x, p, w_gate, w_proj_t, *, name, next_gain=None, target=None):
    t, d = x.shape
    e = p.shape[1]
    tm = _tile(t, 512)
    steps = t // tm

    def body(x_ref, p_ref, wg_ref, wp_ref, *rest):
        xv = x_ref[...]
        s = _dot(xv.astype(BF16), wg_ref[...])
        ple = _dot_nt(p_ref[...].astype(BF16), wp_ref[...])
        y = xv + _sigmoid(s) * ple
        if target is None:
            gain_ref, y_ref, h_ref = rest
            y_ref[...] = y
            r = lax.rsqrt(jnp.mean(y * y, axis=-1, keepdims=True) + NORM_EPS)
            h_ref[...] = (y * r * gain_ref[...]).astype(BF16)
        else:
            t_ref, dy_ref, l_ref, acc_ref = rest
            err = y - t_ref[...]
            dy_ref[...] = err * (1.0 / d)
            part = jnp.sum(err * err, axis=0, keepdims=True)

            @pl.when(pl.program_id(0) == 0)
            def _():
                acc_ref[...] = part

            @pl.when(pl.program_id(0) > 0)
            def _():
                acc_ref[...] += part

            @pl.when(pl.program_id(0) == steps - 1)
            def _():
                l_ref[...] = jnp.full((1, LANES), (0.5 / d), F32) * jnp.sum(acc_ref[...])

    row = pl.BlockSpec((tm, d), lambda i: (i, 0))
    fixed = lambda shape: pl.BlockSpec(shape, lambda i: (0, 0))
    in_specs = [row, pl.BlockSpec((tm, e), lambda i: (i, 0)), fixed((d, d)), fixed((d, e))]
    if target is None:
        return pl.pallas_call(
            body, name=name, grid=(steps,), in_specs=in_specs + [fixed((1, d))], out_specs=[row, row],
            out_shape=[jax.ShapeDtypeStruct((t, d), F32), jax.ShapeDtypeStruct((t, d), BF16)],
            compiler_params=_params("parallel"),
        )(x, p, w_gate, w_proj_t, next_gain)
    return pl.pallas_call(
        body, name=name, grid=(steps,), in_specs=in_specs + [row], out_specs=[row, fixed((1, LANES))],
        out_shape=[jax.ShapeDtypeStruct((t, d), F32), jax.ShapeDtypeStruct((1, LANES), F32)],
        scratch_shapes=[pltpu.VMEM((1, d), F32)],
        compiler_params=_params("arbitrary"),
    )(x, p, w_gate, w_proj_t, target)


def _ple_bwd(x, p, w_gate, w_proj_t, dout, *, name):
    t, d = x.shape
    e = p.shape[1]
    tm = _tile(t, 512)

    def body(x_ref, p_ref, wg_ref, wp_ref, do_ref, ds_ref, dple_ref, dx_ref):
        wg = wg_ref[...]
        s = _dot(x_ref[...].astype(BF16), wg)
        ple = _dot_nt(p_ref[...].astype(BF16), wp_ref[...])
        gate = _sigmoid(s)
        dov = do_ref[...]
        dple_ref[...] = (dov * gate).astype(BF16)
        ds = (dov * ple * gate * (1.0 - gate)).astype(BF16)
        ds_ref[...] = ds
        dx_ref[...] = dov + _dot_nt(ds, wg)

    row = pl.BlockSpec((tm, d), lambda i: (i, 0))
    fixed = lambda shape: pl.BlockSpec(shape, lambda i: (0, 0))
    return pl.pallas_call(
        body, name=name, grid=(t // tm,),
        in_specs=[row, pl.BlockSpec((tm, e), lambda i: (i, 0)), fixed((d, d)), fixed((d, e)), row],
        out_specs=[row, row, row],
        out_shape=[jax.ShapeDtypeStruct((t, d), BF16), jax.ShapeDtypeStruct((t, d), BF16),
                   jax.ShapeDtypeStruct((t, d), F32)],
        compiler_params=_params("parallel"),
    )(x, p, w_gate, w_proj_t, dout)


CONV_TIME_TILE = 256
CONV_HALO = 8


def _conv_taps(ext, w):
    acc = ext[CONV_HALO:, :] * w[CONV_WIDTH - 1:CONV_WIDTH, :]
    shifted = [ext[CONV_HALO:, :]]
    for j in range(1, CONV_WIDTH):
        sh = pltpu.roll(ext, j, 0)[CONV_HALO:, :]
        shifted.append(sh)
        acc = acc + sh * w[CONV_WIDTH - 1 - j:CONV_WIDTH - j, :]
    return acc, shifted


def _conv_fwd(u, w, b, side=None):
    t, c = u.shape
    tc = _tile(c, 256)
    tt = CONV_TIME_TILE

    def body(u_ref, w_ref, b_ref, o_ref):
        wv, bv = w_ref[...], b_ref[...]

        def tile(start, ext):
            pre = _conv_taps(ext, wv)[0] + bv
            o_ref[pl.ds(start, tt), :] = pre * _sigmoid(pre)

        tile(0, jnp.concatenate([jnp.zeros((CONV_HALO, tc), F32), u_ref[0:tt, :]], axis=0))

        def loop(i, carry):
            start = pl.multiple_of(i * tt, tt)
            tile(start, u_ref[pl.ds(start - CONV_HALO, tt + CONV_HALO), :])
            return carry

        lax.fori_loop(1, t // tt, loop, 0)

    col = pl.BlockSpec((t, tc), lambda j: (0, j))
    return _call(
        body, side, name="conv_fwd", grid=(c // tc,),
        in_specs=[col, pl.BlockSpec((CONV_WIDTH, tc), lambda j: (0, j)), pl.BlockSpec((1, tc), lambda j: (0, j))],
        out_specs=[col], out_shape=[jax.ShapeDtypeStruct((t, c), F32)],
        scratch_shapes=[], semantics=("parallel",), args=(u, w, b),
    )[0]


def _conv_bwd(u, w, b, dact, side=None):
    t, c = u.shape
    tc = _tile(c, 256)
    tt = CONV_TIME_TILE

    def body(u_ref, w_ref, b_ref, da_ref, du_ref, dw_ref, db_ref, dpre_ref):
        wv, bv = w_ref[...], b_ref[...]

        def tile(start, ext, sums):
            acc, shifted = _conv_taps(ext, wv)
            pre = acc + bv
            sg = _sigmoid(pre)
            dpre = da_ref[pl.ds(start, tt), :] * (sg * (1.0 + pre * (1.0 - sg)))
            dpre_ref[pl.ds(start, tt), :] = dpre
            new = [sums[0] + jnp.sum(dpre, axis=0, keepdims=True)]
            for j in range(CONV_WIDTH):
                new.append(sums[1 + j] + jnp.sum(dpre * shifted[j], axis=0, keepdims=True))
            return tuple(new)

        zero = jnp.zeros((1, tc), F32)
        sums = tile(0, jnp.concatenate([jnp.zeros((CONV_HALO, tc), F32), u_ref[0:tt, :]], axis=0),
                    (zero,) * (1 + CONV_WIDTH))

        def loop(i, sums):
            start = pl.multiple_of(i * tt, tt)
            return tile(start, u_ref[pl.ds(start - CONV_HALO, tt + CONV_HALO), :], sums)

        sums = lax.fori_loop(1, t // tt, loop, sums)
        db_ref[...] = sums[0]
        dw_ref[...] = jnp.concatenate([sums[1 + (CONV_WIDTH - 1 - k)] for k in range(CONV_WIDTH)], axis=0)
        dpre_ref[pl.ds(t, CONV_HALO), :] = jnp.zeros((CONV_HALO, tc), F32)

        def loop2(i, carry):
            start = pl.multiple_of(i * tt, tt)
            ext = dpre_ref[pl.ds(start, tt + CONV_HALO), :]
            acc = ext[0:tt, :] * wv[CONV_WIDTH - 1:CONV_WIDTH, :]
            for j in range(1, CONV_WIDTH):
                acc = acc + pltpu.roll(ext, tt + CONV_HALO - j, 0)[0:tt, :] * wv[CONV_WIDTH - 1 - j:CONV_WIDTH - j, :]
            du_ref[pl.ds(start, tt), :] = acc.astype(BF16)
            return carry

        lax.fori_loop(0, t // tt, loop2, 0)

    col = pl.BlockSpec((t, tc), lambda j: (0, j))
    return _call(
        body, side, name="conv_bwd", grid=(c // tc,),
        in_specs=[col, pl.BlockSpec((CONV_WIDTH, tc), lambda j: (0, j)), pl.BlockSpec((1, tc), lambda j: (0, j)), col],
        out_specs=[col, pl.BlockSpec((CONV_WIDTH, tc), lambda j: (0, j)), pl.BlockSpec((1, tc), lambda j: (0, j))],
        out_shape=[jax.ShapeDtypeStruct((t, c), BF16), jax.ShapeDtypeStruct((CONV_WIDTH, c), F32),
                   jax.ShapeDtypeStruct((1, c), F32)],
        scratch_shapes=[pltpu.VMEM((t + CONV_HALO, tc), F32)],
        semantics=("parallel",), args=(u, w, b, dact),
    )


def _softplus(v):
    e = jnp.exp(-jnp.abs(v))
    w = 1.0 + e
    log1p = jnp.where(w == 1.0, e, jnp.log(w) * (e / jnp.where(w == 1.0, 1.0, w - 1.0)))
    return jnp.maximum(v, 0.0) + log1p


def _split3(z):
    hi = z.astype(BF16)
    rest = z - hi.astype(F32)
    mid = rest.astype(BF16)
    return hi, mid, (rest - mid.astype(F32)).astype(BF16)


def _select_dot(z, ones):
    return sum(_dot(term, ones) for term in _split3(z))


def _ssd_prep_fwd(dt_raw, dt_bias, a_log):
    t = dt_raw.shape[0]
    cl = SSD_CHUNK

    def body(r_ref, b_ref, al_ref, acs_ref, dt_rep_ref, acs_rep_ref):
        dt = _softplus(r_ref[...] + b_ref[...])
        adt = dt * (-jnp.exp(al_ref[...]))
        li = lax.broadcasted_iota(jnp.int32, (cl, cl), 0)
        si = lax.broadcasted_iota(jnp.int32, (cl, cl), 1)
        tri = (si <= li).astype(F32)
        acs = jnp.dot(tri, adt, preferred_element_type=F32, precision=HIGHEST)
        acs_ref[...] = acs
        head = lax.broadcasted_iota(jnp.int32, (LANES, D_INNER), 0)
        chan = lax.broadcasted_iota(jnp.int32, (LANES, D_INNER), 1) // SSM_HEAD_DIM
        spread = (head == chan).astype(BF16)
        dt_rep_ref[...] = _select_dot(dt, spread)
        acs_rep_ref[...] = _select_dot(acs, spread)

    row = pl.BlockSpec((cl, LANES), lambda i: (i, 0))
    wide = pl.BlockSpec((cl, D_INNER), lambda i: (i, 0))
    vec = pl.BlockSpec((1, LANES), lambda i: (0, 0))
    return pl.pallas_call(
        body, name="ssd_prep_fwd", grid=(t // cl,),
        in_specs=[row, vec, vec], out_specs=[row, wide, wide],
        out_shape=[jax.ShapeDtypeStruct((t, LANES), F32), jax.ShapeDtypeStruct((t, D_INNER), F32),
                   jax.ShapeDtypeStruct((t, D_INNER), F32)],
        compiler_params=_params("parallel"),
    )(dt_raw, dt_bias, a_log)


def _ssd_prep_bwd(dt_raw, dt_bias, ddt):
    t = dt_raw.shape[0]
    tm = _tile(t, 512)

    def body(r_ref, b_ref, d_ref, o_ref, db_ref):
        g = d_ref[...] * _sigmoid(r_ref[...] + b_ref[...])
        o_ref[...] = g.astype(BF16)
        part = jnp.sum(g, axis=0, keepdims=True)

        @pl.when(pl.program_id(0) == 0)
        def _():
            db_ref[...] = part

        @pl.when(pl.program_id(0) > 0)
        def _():
            db_ref[...] += part

    row = pl.BlockSpec((tm, LANES), lambda i: (i, 0))
    vec = pl.BlockSpec((1, LANES), lambda i: (0, 0))
    return pl.pallas_call(
        body, name="ssd_prep_bwd", grid=(t // tm,),
        in_specs=[row, vec, row], out_specs=[row, vec],
        out_shape=[jax.ShapeDtypeStruct((t, LANES), BF16), jax.ShapeDtypeStruct((1, LANES), F32)],
        compiler_params=_params("arbitrary"),
    )(dt_raw, dt_bias, ddt)


GROUP_W = D_INNER // SSM_GROUPS
PAIRS_PER_GROUP = GROUP_W // LANES


def _head_cols(acs_pair, lt64):
    rolled = pltpu.roll(acs_pair, ATT_HEAD_DIM, 1)
    return jnp.where(lt64, acs_pair, rolled), jnp.where(lt64, rolled, acs_pair)


def _ssd_fwd(xbc, dt_rep, acs_rep, acs_t, dskip_rep, z, norm_w, side=None):
    t = xbc.shape[0]
    cl = SSD_CHUNK
    nc = t // cl

    def body(xbc_ref, dt_ref, acs_ref, acst_ref, dskip_ref, z_ref, nw_ref, y_ref, hin_ref, yn_ref, state_ref):
        @pl.when(pl.program_id(0) == 0)
        def _():
            state_ref[...] = jnp.zeros_like(state_ref)

        lt64 = _lane_lt64(cl)
        li = lax.broadcasted_iota(jnp.int32, (cl, cl), 0)
        si = lax.broadcasted_iota(jnp.int32, (cl, cl), 1)
        causal = li >= si
        hin_ref[...] = state_ref[...]
        for g in range(SSM_GROUPS):
            gsl = slice(g * GROUP_W, (g + 1) * GROUP_W)
            xg = xbc_ref[:, gsl]
            bg = xbc_ref[:, D_INNER + g * SSM_STATE:D_INNER + (g + 1) * SSM_STATE]
            cg = xbc_ref[:, D_INNER + SSM_GROUPS * SSM_STATE + g * SSM_STATE:
                         D_INNER + SSM_GROUPS * SSM_STATE + (g + 1) * SSM_STATE]
            acs = acs_ref[:, gsl]
            xdt = xg * dt_ref[:, gsl]
            atot = acs[cl - 1:cl, :]
            hin = state_ref[:, gsl]
            cgb = cg.astype(BF16)
            gmat = _dot_nt(cgb, bg.astype(BF16))
            yoff = _dot(cgb, hin.astype(BF16)) * jnp.exp(acs)
            snew = _dot(bg.T.astype(BF16), (xdt * jnp.exp(atot - acs)).astype(BF16))
            state_ref[:, gsl] = hin * jnp.exp(atot) + snew
            xdtb = xdt.astype(BF16)
            for pr in range(PAIRS_PER_GROUP):
                psl = slice(pr * LANES, (pr + 1) * LANES)
                cols = _head_cols(acs[:, psl], lt64)
                xp = xdtb[:, psl]
                ys = []
                for hh in range(2):
                    h = (g * PAIRS_PER_GROUP + pr) * 2 + hh
                    seg = cols[hh] - acst_ref[h:h + 1, :]
                    lm = jnp.exp(jnp.where(causal, seg, NEG_BIG))
                    ys.append(_dot((gmat * lm).astype(BF16), xp))
                ydiag = jnp.where(lt64, ys[0], ys[1])
                osl = slice(g * GROUP_W + pr * LANES, g * GROUP_W + (pr + 1) * LANES)
                y_ref[:, osl] = ydiag + yoff[:, psl] + xg[:, psl] * dskip_ref[:, osl]
            zv = z_ref[:, gsl]
            v = y_ref[:, gsl] * (zv * _sigmoid(zv))
            r = lax.rsqrt(jnp.mean(v * v, axis=-1, keepdims=True) + NORM_EPS)
            yn_ref[:, gsl] = (v * r * nw_ref[:, gsl]).astype(BF16)

    row = lambda w: pl.BlockSpec((cl, w), lambda c: (c, 0))
    vec = pl.BlockSpec((1, D_INNER), lambda c: (0, 0))
    return _call(
        body, side, name="ssd_fwd", grid=(nc,),
        in_specs=[row(CONV_DIM), row(D_INNER), row(D_INNER),
                  pl.BlockSpec((SSM_HEADS, cl), lambda c: (0, c)), vec, row(D_INNER), vec],
        out_specs=[row(D_INNER), pl.BlockSpec((None, SSM_STATE, D_INNER), lambda c: (c, 0, 0)), row(D_INNER)],
        out_shape=[jax.ShapeDtypeStruct((t, D_INNER), F32), jax.ShapeDtypeStruct((nc, SSM_STATE, D_INNER), F32),
                   jax.ShapeDtypeStruct((t, D_INNER), BF16)],
        scratch_shapes=[pltpu.VMEM((SSM_STATE, D_INNER), F32)],
        semantics=("arbitrary",), args=(xbc, dt_rep, acs_rep, acs_t, dskip_rep, z, norm_w),
    )


def _ssd_bwd(xbc, dt_rep, acs_rep, acs_t, dskip_rep, a_rep, hin_all, dy, side=None):
    t = xbc.shape[0]
    cl = SSD_CHUNK
    nc = t // cl

    def body(xbc_ref, dt_ref, acs_ref, acst_ref, dskip_ref, a_ref, hin_ref, dy_ref,
             dxbc_ref, ddt_ref, da_ref, dds_ref, dstate_ref, dacs_ref, dxs_ref):
        step = pl.program_id(0)

        @pl.when(step == 0)
        def _():
            dstate_ref[...] = jnp.zeros_like(dstate_ref)
            da_ref[...] = jnp.zeros_like(da_ref)
            dds_ref[...] = jnp.zeros_like(dds_ref)

        bd = _head_block_diag()
        lt64 = _lane_lt64(cl)
        li = lax.broadcasted_iota(jnp.int32, (cl, cl), 0)
        si = lax.broadcasted_iota(jnp.int32, (cl, cl), 1)
        lower = li >= si
        upper = si >= li
        last_row = lax.broadcasted_iota(jnp.int32, (cl, GROUP_W), 0) == cl - 1
        for g in range(SSM_GROUPS):
            gsl = slice(g * GROUP_W, (g + 1) * GROUP_W)
            bsl = slice(D_INNER + g * SSM_STATE, D_INNER + (g + 1) * SSM_STATE)
            csl = slice(D_INNER + SSM_GROUPS * SSM_STATE + g * SSM_STATE,
                        D_INNER + SSM_GROUPS * SSM_STATE + (g + 1) * SSM_STATE)
            xg = xbc_ref[:, gsl]
            bg = xbc_ref[:, bsl]
            cg = xbc_ref[:, csl]
            bgb, cgb = bg.astype(BF16), cg.astype(BF16)
            acs = acs_ref[:, gsl]
            xdt = xg * dt_ref[:, gsl]
            atot = acs[cl - 1:cl, :]
            eg = jnp.exp(acs)
            dk = jnp.exp(atot - acs)
            etot = jnp.exp(atot)
            hin = hin_ref[:, gsl]
            hinb = hin.astype(BF16)
            dh = dstate_ref[:, gsl]
            dhb = dh.astype(BF16)
            dyg = dy_ref[:, gsl]

            gmat = _dot_nt(cgb, bgb)
            gmat_t = _dot_nt(bgb, cgb)
            ch = _dot(cgb, hinb)
            dacs = _head_sums(dyg * ch * eg, bd)
            dye = (dyg * eg).astype(BF16)
            dc = _dot_nt(dye, hinb)
            dhin = _dot(cg.T.astype(BF16), dye)
            bdh = _dot(bgb, dhb)
            dxs = bdh * dk
            xdk = xdt * dk
            db = _dot_nt(xdk.astype(BF16), dhb)
            ddk = _head_sums(bdh * xdk, bd)
            dacs = dacs - ddk
            datot = jnp.sum(ddk, axis=0, keepdims=True) + etot * _head_sums(
                jnp.sum(dh * hin, axis=0, keepdims=True), bd)
            dacs = dacs + jnp.where(last_row, datot, 0.0)
            dstate_ref[:, gsl] = dh * etot + dhin

            xdtb = xdt.astype(BF16)
            dgsum = jnp.zeros((cl, cl), F32)
            dgsum_t = jnp.zeros((cl, cl), F32)
            for pr in range(PAIRS_PER_GROUP):
                psl = slice(pr * LANES, (pr + 1) * LANES)
                cols = _head_cols(acs[:, psl], lt64)
                xp = xdtb[:, psl]
                dyp = dyg[:, psl].astype(BF16)
                dx1, dac = [], []
                for hh in range(2):
                    h = (g * PAIRS_PER_GROUP + pr) * 2 + hh
                    mine = lt64 if hh == 0 else jnp.logical_not(lt64)
                    row = acst_ref[h:h + 1, :]
                    lm = jnp.exp(jnp.where(lower, cols[hh] - row, NEG_BIG))
                    lm_t = jnp.exp(jnp.where(upper, row - cols[hh], NEG_BIG))
                    dyh = jnp.where(mine, dyp, jnp.zeros_like(dyp))
                    xh = jnp.where(mine, xp, jnp.zeros_like(xp))
                    dm = _dot_nt(dyh, xp)
                    dm_t = _dot_nt(xh, dyp)
                    m_t = gmat_t * lm_t
                    dx1.append(_dot(m_t.astype(BF16), dyp))
                    w = dm * (gmat * lm)
                    w_t = dm_t * m_t
                    dac.append(jnp.sum(w, axis=1, keepdims=True) - jnp.sum(w_t, axis=1, keepdims=True))
                    dgsum = dgsum + dm * lm
                    dgsum_t = dgsum_t + dm_t * lm_t
                osl = slice(g * GROUP_W + pr * LANES, g * GROUP_W + (pr + 1) * LANES)
                dxs_ref[:, osl] = dxs[:, psl] + jnp.where(lt64, dx1[0], dx1[1])
                dacs_ref[:, osl] = dacs[:, psl] + jnp.where(lt64, jnp.broadcast_to(dac[0], (cl, LANES)),
                                                             jnp.broadcast_to(dac[1], (cl, LANES)))
            dxbc_ref[:, csl] = dc + _dot(dgsum.astype(BF16), bgb)
            dxbc_ref[:, bsl] = db + _dot(dgsum_t.astype(BF16), cgb)

        dadt = _split_dot(upper.astype(BF16), dacs_ref[...])
        xall = xbc_ref[:, 0:D_INNER]
        dtall = dt_ref[...]
        dxsall = dxs_ref[...]
        dyall = dy_ref[...]
        ddt_rep = dadt * a_ref[...] + _head_sums(dxsall * xall, bd)
        chan = lax.broadcasted_iota(jnp.int32, (D_INNER, LANES), 0)
        head = lax.broadcasted_iota(jnp.int32, (D_INNER, LANES), 1)
        ddt_ref[...] = _select_dot(ddt_rep, (chan == head * SSM_HEAD_DIM).astype(BF16))
        dxbc_ref[:, 0:D_INNER] = dxsall * dtall + dyall * dskip_ref[...]
        da_ref[...] += jnp.sum(dadt * dtall, axis=0, keepdims=True)
        dds_ref[...] += jnp.sum(dyall * xall, axis=0, keepdims=True)

        @pl.when(step == nc - 1)
        def _():
            dds_ref[...] = _head_sums(dds_ref[...], bd)

    row = lambda w: pl.BlockSpec((cl, w), lambda c: (nc - 1 - c, 0))
    vec = pl.BlockSpec((1, D_INNER), lambda c: (0, 0))
    return _call(
        body, side, name="ssd_bwd", grid=(nc,),
        in_specs=[row(CONV_DIM), row(D_INNER), row(D_INNER),
                  pl.BlockSpec((SSM_HEADS, cl), lambda c: (0, nc - 1 - c)), vec, vec,
                  pl.BlockSpec((None, SSM_STATE, D_INNER), lambda c: (nc - 1 - c, 0, 0)), row(D_INNER)],
        out_specs=[row(CONV_DIM), row(LANES), vec, vec],
        out_shape=[jax.ShapeDtypeStruct((t, CONV_DIM), F32), jax.ShapeDtypeStruct((t, LANES), F32),
                   jax.ShapeDtypeStruct((1, D_INNER), F32), jax.ShapeDtypeStruct((1, D_INNER), F32)],
        scratch_shapes=[pltpu.VMEM((SSM_STATE, D_INNER), F32), pltpu.VMEM((cl, D_INNER), F32),
                        pltpu.VMEM((cl, D_INNER), F32)],
        semantics=("arbitrary",), args=(xbc, dt_rep, acs_rep, acs_t, dskip_rep, a_rep, hin_all, dy),
    )


def _gate_norm_bwd(y, z, w, dx, w_out, side=None):
    t, c = y.shape
    d = dx.shape[1]
    tm = _tile(t, 256)

    def body(y_ref, z_ref, w_ref, dx_ref, wo_ref, dy_ref, dz_ref, dw_ref):
        @pl.when(pl.program_id(0) == 0)
        def _():
            dw_ref[...] = jnp.zeros_like(dw_ref)

        dxb = dx_ref[...].astype(BF16)
        for g in range(SSM_GROUPS):
            gsl = slice(g * GROUP_W, (g + 1) * GROUP_W)
            zv, yv, dov = z_ref[:, gsl], y_ref[:, gsl], _dot_nt(dxb, wo_ref[gsl, :])
            sg = _sigmoid(zv)
            sz = zv * sg
            v = yv * sz
            r = lax.rsqrt(jnp.mean(v * v, axis=-1, keepdims=True) + NORM_EPS)
            vh = v * r
            dvh = dov * w_ref[:, gsl]
            mean = jnp.mean(dvh * vh, axis=-1, keepdims=True)
            dv = r * (dvh - vh * mean)
            dy_ref[:, gsl] = dv * sz
            dz_ref[:, gsl] = (dv * yv * (sg * (1.0 + zv * (1.0 - sg)))).astype(BF16)
            dw_ref[:, gsl] += jnp.sum(dov * vh, axis=0, keepdims=True)

    row = pl.BlockSpec((tm, c), lambda i: (i, 0))
    vec = pl.BlockSpec((1, c), lambda i: (0, 0))
    return _call(
        body, side, name="gate_norm_bwd", grid=(t // tm,),
        in_specs=[row, row, vec, pl.BlockSpec((tm, d), lambda i: (i, 0)), pl.BlockSpec((c, d), lambda i: (0, 0))],
        out_specs=[row, row, vec],
        out_shape=[jax.ShapeDtypeStruct((t, c), F32), jax.ShapeDtypeStruct((t, c), BF16),
                   jax.ShapeDtypeStruct((1, c), F32)],
        scratch_shapes=[], semantics=("arbitrary",), args=(y, z, w, dx, w_out),
    )


ATT_W = ATT_HEADS * ATT_HEAD_DIM
N_QKV_BLOCKS = 9
ATT_SCALE = 1.0 / math.sqrt(ATT_HEAD_DIM)


def _head_rmsnorm(x, gain, bd):
    ms = _head_sums(x * x, bd, terms=1) * (1.0 / ATT_HEAD_DIM)
    return x * lax.rsqrt(ms + NORM_EPS) * gain


def _class_rows(ref, blk, r, dil):
    span = ATT_BLOCK * dil
    sub = ref.at[pl.ds(pl.multiple_of(blk * span, span), span), :]
    return sub[...] if dil == 1 else sub[pl.ds(r, ATT_BLOCK, stride=dil), :]


def _store_class_rows(ref, blk, r, dil, val):
    span = ATT_BLOCK * dil
    sub = ref.at[pl.ds(pl.multiple_of(blk * span, span), span), :]
    if dil == 1:
        sub[...] = val
    else:
        sub[pl.ds(r, ATT_BLOCK, stride=dil), :] = val


PAIRS = ATT_HEADS // 2


def _pair_col(g, j):
    return lambda pair: (0, (g * 3 + j) * PAIRS + pair)


def _pair_slopes(pair):
    steps = jnp.full((1, 2 * ATT_BLOCK), 2 * pair + 1, jnp.int32).astype(F32)
    first = jnp.exp(steps * (-0.5 * math.log(2.0)))
    return first, first * (2.0 ** -0.5)


NORM_ROWS = 512


ROW_SLICES = 4
SLICE_ROWS = 2 * ATT_BLOCK // ROW_SLICES


def _fill_band_bias(bias_ref, pair, dil, transposed):
    bq = ATT_BLOCK
    a = lax.broadcasted_iota(jnp.int32, (2 * bq, 2 * bq), 0) % bq
    b = lax.broadcasted_iota(jnp.int32, (2 * bq, 2 * bq), 1)
    dist = (b - a) if transposed else (a + bq - b)
    in_band = (dist >= 0) & (dist <= bq)
    s0, s1 = _pair_slopes(pair)
    first_head = lax.broadcasted_iota(jnp.int32, (2 * bq, 2 * bq), 0) < bq
    bias = jnp.where(first_head, s0, s1) * (dist.astype(F32) * float(dil))
    inside = (b < bq) if transposed else (b >= bq)
    bias_ref[1] = jnp.where(in_band, bias, -NEG_BIG)
    bias_ref[0] = jnp.where(in_band & inside, bias, -NEG_BIG)


def _row_slices():
    return [slice(i * SLICE_ROWS, (i + 1) * SLICE_ROWS) for i in range(ROW_SLICES)]


def _stack_heads(tile):
    rows = lax.broadcasted_iota(jnp.int32, (2 * ATT_BLOCK, LANES), 0) < ATT_BLOCK
    lanes = lax.broadcasted_iota(jnp.int32, (2 * ATT_BLOCK, LANES), 1) < ATT_HEAD_DIM
    both = jnp.concatenate([tile, tile], axis=0)
    return jnp.where(rows == lanes, both, jnp.zeros_like(both))


def _unstack_heads(stacked, lt64):
    return jnp.where(lt64, stacked[:ATT_BLOCK], stacked[ATT_BLOCK:])


ITEMS_PER_PASS = 4


def _item_loop(nb, dil, work):
    if dil == 1:
        def trip(i, carry):
            work([(i * ITEMS_PER_PASS + b, 0) for b in range(ITEMS_PER_PASS)])
            return carry

        lax.fori_loop(0, nb // ITEMS_PER_PASS, trip, 0)
    else:
        def trip(n, carry):
            for r0 in range(0, dil, ITEMS_PER_PASS):
                work([(n, r0 + j) for j in range(ITEMS_PER_PASS)])
            return carry

        lax.fori_loop(0, nb, trip, 0)


def _qk_normalised(tile, j, gq_ref, gk_ref):
    kind = (j // (ATT_W // tile.shape[1])) % 3
    gain = jnp.where(kind == 0, gq_ref[...] * ATT_SCALE, gk_ref[...])
    return jnp.where(kind == 2, tile, _head_rmsnorm(tile, gain, _head_block_diag()))


def _attn_fwd(qkn, g, dil):
    t = qkn.shape[0]
    nb = t // dil // ATT_BLOCK
    bq = ATT_BLOCK

    def body(qn_ref, kn_ref, v_ref, o_ref, l_ref, bias_ref):
        _fill_band_bias(bias_ref, pl.program_id(0), dil, False)
        lt64 = _lane_lt64(bq)

        def work(items):
            scores, values, probs = [], [], []
            for n, r in items:
                prev = jnp.maximum(n - 1, 0)
                q2 = _stack_heads(_class_rows(qn_ref, n, r, dil).astype(BF16))
                kcat = jnp.concatenate([_class_rows(kn_ref, prev, r, dil), _class_rows(kn_ref, n, r, dil)],
                                       axis=0).astype(BF16)
                values.append(jnp.concatenate([_class_rows(v_ref, prev, r, dil), _class_rows(v_ref, n, r, dil)],
                                              axis=0).astype(BF16))
                scores.append(_dot_nt(q2, kcat))
            for (n, r), sc in zip(items, scores):
                bias = bias_ref.at[jnp.minimum(n, 1)]
                ps, inv, lses = [], [], []
                for rows in _row_slices():
                    s = sc[rows] - bias[rows, :]
                    m = jnp.max(s, axis=1, keepdims=True)
                    p = jnp.exp(s - m)
                    l = jnp.sum(p, axis=1, keepdims=True)
                    ps.append(p.astype(BF16))
                    inv.append(jnp.broadcast_to(1.0 / l, (SLICE_ROWS, LANES)))
                    lses.append(jnp.broadcast_to(m + jnp.log(l), (SLICE_ROWS, LANES)))
                probs.append((jnp.concatenate(ps, axis=0), jnp.concatenate(inv, axis=0)))
                _store_class_rows(l_ref, n, r, dil, _unstack_heads(jnp.concatenate(lses, axis=0), lt64))
            for (n, r), (p, inv), vcat in zip(items, probs, values):
                _store_class_rows(o_ref, n, r, dil, _unstack_heads(_dot(p, vcat) * inv, lt64))

        _item_loop(nb, dil, work)

    col = lambda j: pl.BlockSpec((t, LANES), _pair_col(g, j))
    out = pl.BlockSpec((t, LANES), lambda pair: (0, pair))
    return pl.pallas_call(
        body, name=f"attn_fwd_g{g}", grid=(PAIRS,),
        in_specs=[col(0), col(1), col(2)], out_specs=[out, out],
        out_shape=[jax.ShapeDtypeStruct((t, ATT_W), F32), jax.ShapeDtypeStruct((t, ATT_W), F32)],
        scratch_shapes=[pltpu.VMEM((2, 2 * bq, 2 * bq), F32)],
        compiler_params=_params("parallel"),
    )(qkn, qkn, qkn)


def _one_per_head(rep):
    chan = lax.broadcasted_iota(jnp.int32, (ATT_W, LANES), 0)
    head = lax.broadcasted_iota(jnp.int32, (ATT_W, LANES), 1)
    return _select_dot(rep, (chan == head * ATT_HEAD_DIM).astype(BF16))


def _attn_out_fwd(outs, lses, w_o, x0, next_gain):
    t, d = x0.shape
    tm = _tile(t, 256)

    def body(o0, o1, o2, l0, l1, l2, wo_ref, x_ref, g_ref, of_ref, lt_ref, lc_ref, x1_ref, h_ref):
        a, b, c = l0[...], l1[...], l2[...]
        m = jnp.maximum(jnp.maximum(a, b), c)
        ea, eb, ec = jnp.exp(a - m), jnp.exp(b - m), jnp.exp(c - m)
        ssum = ea + eb + ec
        o = (ea * o0[...] + eb * o1[...] + ec * o2[...]) / ssum
        of_ref[...] = o
        lse = m + jnp.log(ssum)
        lt_ref[...] = lse
        lc_ref[...] = _one_per_head(lse)
        x1 = x_ref[...] + _dot(o.astype(BF16), wo_ref[...])
        x1_ref[...] = x1
        r = lax.rsqrt(jnp.mean(x1 * x1, axis=-1, keepdims=True) + NORM_EPS)
        h_ref[...] = (x1 * r * g_ref[...]).astype(BF16)

    row = pl.BlockSpec((tm, ATT_W), lambda i: (i, 0))
    xrow = pl.BlockSpec((tm, d), lambda i: (i, 0))
    return pl.pallas_call(
        body, name="att_out", grid=(t // tm,),
        in_specs=[row] * 6 + [pl.BlockSpec((ATT_W, d), lambda i: (0, 0)), xrow, pl.BlockSpec((1, d), lambda i: (0, 0))],
        out_specs=[row, row, pl.BlockSpec((tm, LANES), lambda i: (i, 0)), xrow, xrow],
        out_shape=[jax.ShapeDtypeStruct((t, ATT_W), F32), jax.ShapeDtypeStruct((t, ATT_W), F32),
                   jax.ShapeDtypeStruct((t, LANES), F32), jax.ShapeDtypeStruct((t, d), F32),
                   jax.ShapeDtypeStruct((t, d), BF16)],
        compiler_params=_params("parallel"),
    )(*outs, *lses, w_o, x0, next_gain)


def _attn_out_bwd(dx, w_o, o, side=None):
    t, d = dx.shape
    tm = _tile(t, 256)

    def body(dx_ref, wo_ref, o_ref, do_ref, dl_ref, dc_ref):
        do = _dot_nt(dx_ref[...].astype(BF16), wo_ref[...])
        do_ref[...] = do
        dl = _head_sums(do * o_ref[...], _head_block_diag())
        dl_ref[...] = dl
        dc_ref[...] = _one_per_head(dl)

    row = pl.BlockSpec((tm, ATT_W), lambda i: (i, 0))
    return _call(
        body, side, name="att_out_dx", grid=(t // tm,),
        in_specs=[pl.BlockSpec((tm, d), lambda i: (i, 0)), pl.BlockSpec((ATT_W, d), lambda i: (0, 0)), row],
        out_specs=[row, row, pl.BlockSpec((tm, LANES), lambda i: (i, 0))],
        out_shape=[jax.ShapeDtypeStruct((t, ATT_W), F32), jax.ShapeDtypeStruct((t, ATT_W), F32),
                   jax.ShapeDtypeStruct((t, LANES), F32)],
        scratch_shapes=[], semantics=("parallel",), args=(dx, w_o, o),
    )


def _head_rmsnorm_bwd(x_ref, dy_ref, gain_ref, dx_ref, dgain_ref):
    bd = _head_block_diag()
    gain = gain_ref[...]

    def step(i, acc):
        rows = pl.ds(pl.multiple_of(i * NORM_ROWS, NORM_ROWS), NORM_ROWS)
        x, dy = x_ref[rows, :], dy_ref[rows, :]
        r = lax.rsqrt(_head_sums(x * x, bd, terms=1) * (1.0 / ATT_HEAD_DIM) + NORM_EPS)
        xh = x * r
        dxh = dy * gain
        mean = _head_sums(dxh * xh, bd, terms=1) * (1.0 / ATT_HEAD_DIM)
        dx_ref[rows, :] = (r * (dxh - xh * mean)).astype(BF16)
        return acc + jnp.sum(dy * xh, axis=0, keepdims=True)

    acc = lax.fori_loop(0, x_ref.shape[0] // NORM_ROWS, step, jnp.zeros((1, LANES), F32))
    dgain_ref[...] = jnp.broadcast_to(acc, dgain_ref.shape)


def _attn_bwd_dq(qkv, qkn, gq, do, l_rep, dl_rep, g, dil):
    t = qkv.shape[0]
    nb = t // dil // ATT_BLOCK
    bq = ATT_BLOCK

    def body(q_ref, qn_ref, kn_ref, v_ref, gq_ref, do_ref, l_ref, dl_ref, dx_ref, dgain_ref, bias_ref, dq_ref):
        _fill_band_bias(bias_ref, pl.program_id(0), dil, False)
        lt64 = _lane_lt64(bq)

        def per_row(tile):
            cols = _head_cols(tile, lt64)
            half = jnp.concatenate([cols[0], cols[1]], axis=0)
            return jnp.concatenate([half, half], axis=1)

        def work(items):
            products, keys, dscores = [], [], []
            for n, r in items:
                prev = jnp.maximum(n - 1, 0)
                q2 = _stack_heads(_class_rows(qn_ref, n, r, dil).astype(BF16))
                do2 = _stack_heads(_class_rows(do_ref, n, r, dil).astype(BF16))
                kcat = jnp.concatenate([_class_rows(kn_ref, prev, r, dil), _class_rows(kn_ref, n, r, dil)],
                                       axis=0).astype(BF16)
                vcat = jnp.concatenate([_class_rows(v_ref, prev, r, dil), _class_rows(v_ref, n, r, dil)],
                                       axis=0).astype(BF16)
                keys.append(kcat)
                products.append((_dot_nt(q2, kcat), _dot_nt(do2, vcat)))
            for (n, r), (scores, dps) in zip(items, products):
                bias = bias_ref.at[jnp.minimum(n, 1)]
                lse = per_row(_class_rows(l_ref, n, r, dil))
                dl = per_row(_class_rows(dl_ref, n, r, dil))
                dss = []
                for rows in _row_slices():
                    p = jnp.exp(scores[rows] - bias[rows, :] - lse[rows])
                    dss.append((p * (dps[rows] - dl[rows])).astype(BF16))
                dscores.append(jnp.concatenate(dss, axis=0))
            for (n, r), ds, kcat in zip(items, dscores, keys):
                _store_class_rows(dq_ref, n, r, dil, _unstack_heads(_dot(ds, kcat) * ATT_SCALE, lt64))

        _item_loop(nb, dil, work)
        _head_rmsnorm_bwd(q_ref, dq_ref, gq_ref, dx_ref, dgain_ref)

    col = lambda j: pl.BlockSpec((t, LANES), _pair_col(g, j))
    vec = pl.BlockSpec((1, LANES), lambda pair: (0, 0))
    tok = pl.BlockSpec((t, LANES), lambda pair: (0, pair))
    return pl.pallas_call(
        body, name=f"attn_bwd_dq_g{g}", grid=(PAIRS,),
        in_specs=[col(0), col(0), col(1), col(2), vec, tok, tok, tok],
        out_specs=[tok, pl.BlockSpec((None, 8, LANES), lambda pair: (pair, 0, 0))],
        out_shape=[jax.ShapeDtypeStruct((t, ATT_W), BF16), jax.ShapeDtypeStruct((PAIRS, 8, LANES), F32)],
        scratch_shapes=[pltpu.VMEM((2, 2 * bq, 2 * bq), F32), pltpu.VMEM((t, LANES), F32)],
        compiler_params=_params("parallel"),
    )(qkv, qkn, qkn, qkn, gq, do, l_rep, dl_rep)


def _attn_bwd_dkv(qkv, qkn, gk, do, l_row, dl_row, g, dil):
    t = qkv.shape[0]
    nb = t // dil // ATT_BLOCK
    bq = ATT_BLOCK

    def body(k_ref, qn_ref, kn_ref, v_ref, gk_ref, do_ref, l_ref, dl_ref, dkx_ref, dvx_ref, dgain_ref, bias_ref,
             dk_ref, dv_ref):
        _fill_band_bias(bias_ref, pl.program_id(0), dil, True)
        lt64 = _lane_lt64(bq)

        def per_query(ref, hh, lane_c, lane_n):
            return jnp.concatenate([ref[hh:hh + 1, pl.ds(lane_c, bq)], ref[hh:hh + 1, pl.ds(lane_n, bq)]], axis=1)

        def work(items):
            products, operands, weights = [], [], []
            for n, r in items:
                nxt = jnp.minimum(n + 1, nb - 1)
                k2 = _stack_heads(_class_rows(kn_ref, n, r, dil).astype(BF16))
                v2 = _stack_heads(_class_rows(v_ref, n, r, dil).astype(BF16))
                qcat = jnp.concatenate([_class_rows(qn_ref, n, r, dil), _class_rows(qn_ref, nxt, r, dil)],
                                       axis=0).astype(BF16)
                docat = jnp.concatenate([_class_rows(do_ref, n, r, dil), _class_rows(do_ref, nxt, r, dil)],
                                        axis=0).astype(BF16)
                operands.append((qcat, docat))
                products.append((_dot_nt(k2, qcat), _dot_nt(v2, docat)))
            for (n, r), (scores, dps) in zip(items, products):
                nxt = jnp.minimum(n + 1, nb - 1)
                bias = bias_ref.at[jnp.where(n == nb - 1, 0, 1)]
                lane_c = pl.multiple_of((r * nb + n) * bq, bq)
                lane_n = pl.multiple_of((r * nb + nxt) * bq, bq)
                lse = [per_query(l_ref, hh, lane_c, lane_n) for hh in range(2)]
                dl = [per_query(dl_ref, hh, lane_c, lane_n) for hh in range(2)]
                pts, dss = [], []
                for i, rows in enumerate(_row_slices()):
                    hh = i * SLICE_ROWS // bq
                    p_t = jnp.exp(scores[rows] - bias[rows, :] - lse[hh])
                    pts.append(p_t.astype(BF16))
                    dss.append((p_t * (dps[rows] - dl[hh])).astype(BF16))
                weights.append((jnp.concatenate(pts, axis=0), jnp.concatenate(dss, axis=0)))
            for (n, r), (p_t, ds_t), (qcat, docat) in zip(items, weights, operands):
                _store_class_rows(dv_ref, n, r, dil, _unstack_heads(_dot(p_t, docat), lt64))
                _store_class_rows(dk_ref, n, r, dil, _unstack_heads(_dot(ds_t, qcat), lt64))

        _item_loop(nb, dil, work)
        _head_rmsnorm_bwd(k_ref, dk_ref, gk_ref, dkx_ref, dgain_ref)

        def cast_rows(i, carry):
            rows = pl.ds(pl.multiple_of(i * NORM_ROWS, NORM_ROWS), NORM_ROWS)
            dvx_ref[rows, :] = dv_ref[rows, :].astype(BF16)
            return carry

        lax.fori_loop(0, t // NORM_ROWS, cast_rows, 0)

    col = lambda j: pl.BlockSpec((t, LANES), _pair_col(g, j))
    vec = pl.BlockSpec((1, LANES), lambda pair: (0, 0))
    tok = pl.BlockSpec((t, LANES), lambda pair: (0, pair))
    rows = pl.BlockSpec((None, 8, t), lambda pair: (pair, 0, 0))
    return pl.pallas_call(
        body, name=f"attn_bwd_dkv_g{g}", grid=(PAIRS,),
        in_specs=[col(1), col(0), col(1), col(2), vec, tok, rows, rows],
        out_specs=[tok, tok, pl.BlockSpec((None, 8, LANES), lambda pair: (pair, 0, 0))],
        out_shape=[jax.ShapeDtypeStruct((t, ATT_W), BF16), jax.ShapeDtypeStruct((t, ATT_W), BF16),
                   jax.ShapeDtypeStruct((PAIRS, 8, LANES), F32)],
        scratch_shapes=[pltpu.VMEM((2, 2 * bq, 2 * bq), F32), pltpu.VMEM((t, LANES), F32),
                        pltpu.VMEM((t, LANES), F32)],
        compiler_params=_params("parallel"),
    )(qkv, qkn, qkn, qkn, gk, do, l_row, dl_row)


def _rows_by_residue(one_per_head, dil):
    t = one_per_head.shape[0]
    per_head = one_per_head[:, :ATT_HEADS]
    rows = per_head.reshape(t // dil, dil, ATT_HEADS).transpose(2, 1, 0).reshape(PAIRS, 2, t)
    return jnp.pad(rows, ((0, 0), (0, 6), (0, 0)))


def _per_head(rep_row):
    return rep_row[0, ::SSM_HEAD_DIM]


def _rep_heads(v):
    return jnp.repeat(v, SSM_HEAD_DIM)[None, :]


def _pad_lanes(v):
    return jnp.pad(v, ((0, 0), (0, LANES - v.shape[1])))


class _NoOverlap:
    def side(self, host):
        return None

    def after(self, host):
        pass

    def begin_backward(self, grads):
        pass


def _hosted(plan, host, fn, *args, **kwargs):
    out = fn(*args, side=plan.side(host), **kwargs)
    plan.after(host)
    return out


def _ffn_ple_fwd(x1, h, p_i, prm, i, plan, next_gain=None, target=None):
    g, u, act = _hosted(plan, f"swiglu_fwd_{i}", _swiglu_fwd, h, prm["ffn_w_gate"][i], prm["ffn_w_up"][i],
                        name=f"swiglu_fwd_{i}")
    x2 = _hosted(plan, f"ffn_down_{i}", _matmul, act, prm["ffn_w_down"][i], mode="nn", addend=x1,
                 name=f"ffn_down_{i}")
    outs = _ple_fwd(x2, p_i, prm["ple_w_gate"][i], prm["ple_w_proj"][i], name=f"ple_fwd_{i}", next_gain=next_gain,
                    target=target)
    return outs, dict(x1=x1, h=h, g=g, u=u, act=act, x2=x2)


def _ffn_ple_bwd(dx3, p_i, prm, i, sv, grads, plan):
    ds, dple, dx2 = _ple_bwd(sv["x2"], p_i, prm["ple_w_gate"][i], prm["ple_w_proj"][i], dx3, name=f"ple_bwd_{i}")
    grads["ple_w_gate"][i] = _matmul_tn(sv["x2"], ds, name=f"d_ple_w_gate_{i}")
    grads["ple_w_proj"][i] = _matmul_tn(dple, p_i, name=f"d_ple_w_proj_{i}")
    grads["ffn_w_down"][i] = _matmul_tn(sv["act"], dx2, name=f"d_ffn_w_down_{i}")
    dg, du = _hosted(plan, f"swiglu_bwd_{i}", _swiglu_bwd, dx2, prm["ffn_w_down"][i], sv["g"], sv["u"],
                     name=f"swiglu_bwd_{i}")
    grads["ffn_w_gate"][i] = _matmul_tn(dg, sv["h"], name=f"d_ffn_w_gate_{i}")
    grads["ffn_w_up"][i] = _matmul_tn(du, sv["h"], name=f"d_ffn_w_up_{i}")
    dx1, dgain = _matmul_rmsnorm_bwd(dg, prm["ffn_w_gate"][i], None, sv["x1"], prm["norm_ffn"][i:i + 1], dx2,
                                     name=f"ffn_dh_{i}", tm=256, more=(du, prm["ffn_w_up"][i]))
    grads["norm_ffn"][i] = dgain[0]
    return dx1


def _mamba_fwd(x0, prm, plan):
    h = _rmsnorm_fwd(x0, prm["norm_mix"][0:1], name="mix_norm_fwd_0")
    z = _hosted(plan, "ssm_in_z", _matmul, h, prm["ssm_w_z"], mode="nt", name="ssm_in_z")
    xbc_pre = _hosted(plan, "ssm_in_xbc", _matmul, h, prm["ssm_w_xbc"], mode="nt", name="ssm_in_xbc")
    dt_raw = _matmul(h, prm["ssm_w_dt"], mode="nt", name="ssm_in_dt")
    xbc = _hosted(plan, "conv_fwd", _conv_fwd, xbc_pre, prm["ssm_conv_w"], prm["ssm_conv_b"])
    dt_bias = _pad_lanes(prm["ssm_dt_bias"])
    a_log = _pad_lanes(prm["ssm_a_log"])
    acs, dt_rep, acs_rep = _ssd_prep_fwd(dt_raw, dt_bias, a_log)
    acs_t = acs[:, :SSM_HEADS].T
    dskip_rep = _rep_heads(prm["ssm_d_skip"][0])
    y, hin_all, yn = _hosted(plan, "ssd_fwd", _ssd_fwd, xbc, dt_rep, acs_rep, acs_t, dskip_rep, z,
                             prm["ssm_norm_w"])
    x1, h_ffn = _matmul(yn, prm["ssm_w_out"], mode="nn", addend=x0, name="ssm_out", tm=512, tn=D_MODEL,
                        second=(_rmsnorm_rows, [prm["norm_ffn"][0:1]], BF16))
    sv = dict(x0=x0, h=h, z=z, xbc_pre=xbc_pre, dt_raw=dt_raw, xbc=xbc, dt_bias=dt_bias, dt_rep=dt_rep,
              acs_rep=acs_rep, acs_t=acs_t, dskip_rep=dskip_rep, y=y, hin_all=hin_all, yn=yn)
    return x1, h_ffn, sv


def _mamba_bwd(dx1, prm, sv, grads, plan):
    grads["ssm_w_out"] = _matmul_tn(sv["yn"], dx1, name="d_ssm_w_out")
    dy, dz, dnw = _hosted(plan, "gate_norm_bwd", _gate_norm_bwd, sv["y"], sv["z"], prm["ssm_norm_w"], dx1,
                          prm["ssm_w_out"])
    grads["ssm_norm_w"] = dnw
    a_rep = _rep_heads(-jnp.exp(prm["ssm_a_log"][0]))
    dxbc, ddt, da_rep, dds_rep = _hosted(plan, "ssd_bwd", _ssd_bwd, sv["xbc"], sv["dt_rep"], sv["acs_rep"],
                                             sv["acs_t"], sv["dskip_rep"], a_rep, sv["hin_all"], dy)
    grads["ssm_d_skip"] = _per_head(dds_rep)[None, :]
    grads["ssm_a_log"] = (_per_head(da_rep) * _per_head(a_rep))[None, :]
    ddt_raw, dbias = _ssd_prep_bwd(sv["dt_raw"], sv["dt_bias"], ddt)
    grads["ssm_dt_bias"] = dbias[:, :SSM_HEADS]
    du, dcw, dcb = _hosted(plan, "conv_bwd", _conv_bwd, sv["xbc_pre"], prm["ssm_conv_w"], prm["ssm_conv_b"], dxbc)
    grads["ssm_conv_w"] = dcw
    grads["ssm_conv_b"] = dcb
    h = sv["h"]
    grads["ssm_w_in"] = jnp.concatenate(
        [_matmul_tn(dz, h, name="d_ssm_w_z"), _matmul_tn(du, h, name="d_ssm_w_xbc"),
         _matmul_tn(ddt_raw, h, name="d_ssm_w_dt")[:SSM_HEADS]], axis=0)
    dh = _hosted(plan, "ssm_dh_z", _matmul, dz, prm["ssm_w_z"], mode="nn", name="ssm_dh_z")
    dh = _hosted(plan, "ssm_dh_xbc", _matmul, du, prm["ssm_w_xbc"], mode="nn", addend=dh, name="ssm_dh_xbc")
    dx0, dgain = _hosted(plan, "ssm_dh_dt", _matmul_rmsnorm_bwd, ddt_raw, prm["ssm_w_dt"], dh, sv["x0"],
                         prm["norm_mix"][0:1], dx1, name="ssm_dh_dt")
    grads["norm_mix"][0] = dgain[0]
    return dx0


def _attn_mixer_fwd(x0, h, prm, plan):
    n_heads = N_QKV_BLOCKS * ATT_HEADS
    gq = jnp.tile(prm["att_q_norm"], (1, n_heads))
    gk = jnp.tile(prm["att_k_norm"], (1, n_heads))
    qkv, qkn = _hosted(plan, "att_qkv", _matmul, h, prm["att_w_qkv"], mode="nt", name="att_qkv",
                       second=(_qk_normalised, [gq, gk], F32))
    outs, lses = [], []
    for g, (window, dil) in enumerate(DIL_PATTERNS):
        o_g, l_g = _attn_fwd(qkn, g, dil)
        outs.append(o_g)
        lses.append(l_g)
    o_f, l_rep, l_one, x1, h_ffn = _attn_out_fwd(outs, lses, prm["att_w_o"], x0, prm["norm_ffn"][1:2])
    sv = dict(x0=x0, h=h, qkv=qkv, qkn=qkn, gq2=gq[:, :LANES], gk2=gk[:, :LANES], o_f=o_f, l_rep=l_rep,
              l_one=l_one)
    return x1, h_ffn, sv


def _attn_mixer_bwd(dx1, prm, sv, grads, plan):
    grads["att_w_o"] = _matmul_tn(sv["o_f"], dx1, name="d_att_w_o")
    do, dl_rep, dl_one = _hosted(plan, "att_out_dx", _attn_out_bwd, dx1, prm["att_w_o"], sv["o_f"])
    blocks, dgq, dgk = [], [], []
    for g, (window, dil) in enumerate(DIL_PATTERNS):
        dq, dgq_g = _attn_bwd_dq(sv["qkv"], sv["qkn"], sv["gq2"], do, sv["l_rep"], dl_rep, g, dil)
        dk, dv, dgk_g = _attn_bwd_dkv(sv["qkv"], sv["qkn"], sv["gk2"], do, _rows_by_residue(sv["l_one"], dil),
                                      _rows_by_residue(dl_one, dil), g, dil)
        blocks += [dq, dk, dv]
        dgq.append(dgq_g)
        dgk.append(dgk_g)
    dqkv = jnp.concatenate(blocks, axis=1)

    def fold(parts):
        return jnp.stack(parts)[:, :, 0].reshape(-1, ATT_HEAD_DIM).sum(axis=0)[None, :]

    grads["att_q_norm"] = fold(dgq)
    grads["att_k_norm"] = fold(dgk)
    grads["att_w_qkv"] = _matmul_tn(dqkv, sv["h"], name="d_att_w_qkv")
    dx0, dgain = _hosted(plan, "att_qkv_dx", _matmul_rmsnorm_bwd, dqkv, prm["att_w_qkv"], None, sv["x0"],
                         prm["norm_mix"][1:2], dx1, name="att_qkv_dx")
    grads["norm_mix"][1] = dgain[0]
    return dx0


def _local_step(x, p, target, prm, plan=None):
    plan = plan or _NoOverlap()
    grads = {k: [None, None] for k in ("norm_mix", "norm_ffn", "ffn_w_gate", "ffn_w_up", "ffn_w_down",
                                       "ple_w_proj", "ple_w_gate")}
    plan.begin_backward(grads)
    x1, h1, sv_m = _mamba_fwd(x, prm, plan)
    (x3, h3), sv_f0 = _ffn_ple_fwd(x1, h1, p[0], prm, 0, plan, next_gain=prm["norm_mix"][1:2])
    x4, h4, sv_a = _attn_mixer_fwd(x3, h3, prm, plan)
    (dy, loss_row), sv_f1 = _ffn_ple_fwd(x4, h4, p[1], prm, 1, plan, target=target)
    dx4 = _ffn_ple_bwd(dy, p[1], prm, 1, sv_f1, grads, plan)
    dx3 = _attn_mixer_bwd(dx4, prm, sv_a, grads, plan)
    dx1 = _ffn_ple_bwd(dx3, p[0], prm, 0, sv_f0, grads, plan)
    dx0 = _mamba_bwd(dx1, prm, sv_m, grads, plan)
    return loss_row, dx0, grads


W_IN_SLAB_ROWS = 1312


def _position():
    return lax.axis_index("x"), lax.axis_index("y"), lax.axis_index("c")


def _other_chips(x, y):
    return [(1 - x, y), (x, 1 - y), (1 - x, 1 - y)]


def _remote(send_sems, recv_sems, k, src, dst, to):
    return pltpu.make_async_remote_copy(src_ref=src, dst_ref=dst, send_sem=send_sems.at[k], recv_sem=recv_sems.at[k],
                                        device_id=to, device_id_type=MESH)


def _gather_side(entries, whole=()):
    n, nw = len(entries), len(whole)

    def first_hop(ins, outs, send_sems, recv_sems):
        x, y, c = _position()
        cps = []
        for j, chip in enumerate(_other_chips(x, y)):
            for e in range(n):
                cps.append(_remote(send_sems, recv_sems, 6 * e + j, ins[e].at[c], outs[e].at[2 * x + y, c], (*chip, c)))
            for e in range(nw):
                cps.append(_remote(send_sems, recv_sems, 6 * n + 3 * e + j, ins[n + e], outs[n + e].at[2 * x + y],
                                   (*chip, c)))
        return cps

    def start(ins, outs, send_sems, recv_sems):
        for cp in first_hop(ins, outs, send_sems, recv_sems):
            cp.start()

    def finish(ins, outs, send_sems, recv_sems):
        x, y, c = _position()
        me, sibling = (x, y, c), (x, y, 1 - c)
        chips = _other_chips(x, y)
        passed_on = []
        for j, (px, py) in enumerate(chips):
            for e in range(n):
                landed = outs[e].at[2 * px + py, c]
                _remote(send_sems, recv_sems, 6 * e + j, landed, landed, me).wait_recv()
                passed_on.append(_remote(send_sems, recv_sems, 6 * e + 3 + j, landed, landed, sibling))
                passed_on[-1].start()
            for e in range(nw):
                landed = outs[n + e].at[2 * px + py]
                _remote(send_sems, recv_sems, 6 * n + 3 * e + j, landed, landed, me).wait_recv()
        for j, (px, py) in enumerate(chips):
            for e in range(n):
                passed = outs[e].at[2 * px + py, 1 - c]
                _remote(send_sems, recv_sems, 6 * e + 3 + j, passed, passed, me).wait_recv()
        for cp in first_hop(ins, outs, send_sems, recv_sems) + passed_on:
            cp.wait_send()

    shapes = [jax.ShapeDtypeStruct((N_CHIPS,) + a.shape, a.dtype) for a in list(entries) + list(whole)]
    return _Side(list(entries) + list(whole), shapes, 6 * n + 3 * nw, start, finish)


def _run_side(side, name):
    si, so = len(side.inputs), len(side.out_shapes)

    def body(*refs):
        ins, outs, send_sems, recv_sems = refs[:si], refs[si:si + so], refs[-2], refs[-1]
        side.start(ins, outs, send_sems, recv_sems)
        side.finish(ins, outs, send_sems, recv_sems)

    side.outputs = list(pl.pallas_call(
        body, name=name, in_specs=[ANY] * si, out_specs=[ANY] * so, out_shape=side.out_shapes,
        scratch_shapes=[pltpu.SemaphoreType.DMA((side.n_sems,)), pltpu.SemaphoreType.DMA((side.n_sems,))],
    )(*side.inputs))
    return side.outputs


def _swap_side(grads):
    n = len(grads)

    def copies(ins, outs, send_sems, recv_sems):
        x, y, c = _position()
        return [_remote(send_sems, recv_sems, e, ins[e].at[:, 1 - c], outs[e], (x, y, 1 - c)) for e in range(n)]

    def start(ins, outs, send_sems, recv_sems):
        for cp in copies(ins, outs, send_sems, recv_sems):
            cp.start()

    def finish(ins, outs, send_sems, recv_sems):
        for cp in copies(ins, outs, send_sems, recv_sems):
            cp.wait()

    shapes = [jax.ShapeDtypeStruct((N_CHIPS,) + g.shape[2:], g.dtype) for g in grads]
    return _Side(grads, shapes, n, start, finish)


def _chip_exchange_side(chipsums):
    n = len(chipsums)

    def copies(ins, outs, send_sems, recv_sems):
        x, y, c = _position()
        return [_remote(send_sems, recv_sems, 3 * e + j, ins[e].at[2 * tx + ty], outs[e].at[j], (tx, ty, c))
                for j, (tx, ty) in enumerate(_other_chips(x, y)) for e in range(n)]

    def start(ins, outs, send_sems, recv_sems):
        for cp in copies(ins, outs, send_sems, recv_sems):
            cp.start()

    def finish(ins, outs, send_sems, recv_sems):
        for cp in copies(ins, outs, send_sems, recv_sems):
            cp.wait()

    shapes = [jax.ShapeDtypeStruct((3,) + cs.shape[1:], cs.dtype) for cs in chipsums]
    return _Side(chipsums, shapes, 3 * n, start, finish)


def _share_side(totals):
    n = len(totals)

    def copies(ins, outs, send_sems, recv_sems):
        x, y, c = _position()
        return [_remote(send_sems, recv_sems, e, ins[e], outs[e], (x, y, 1 - c)) for e in range(n)]

    def start(ins, outs, send_sems, recv_sems):
        for cp in copies(ins, outs, send_sems, recv_sems):
            cp.start()

    def finish(ins, outs, send_sems, recv_sems):
        for cp in copies(ins, outs, send_sems, recv_sems):
            cp.wait()

    return _Side(totals, [jax.ShapeDtypeStruct(t.shape, t.dtype) for t in totals], n, start, finish)


def _reduce_rows(h):
    return h if h <= 704 else h // 2


def _add_sibling(grad, recv, c_idx, *, name):
    _, _, h, cw = grad.shape
    th = _reduce_rows(h)

    def body(c_ref, g_ref, r_ref, o_ref):
        o_ref[...] = (g_ref[...] + r_ref[...]).astype(BF16)

    return pl.pallas_call(
        body, name=name,
        grid_spec=pltpu.PrefetchScalarGridSpec(
            num_scalar_prefetch=1, grid=(N_CHIPS, h // th),
            in_specs=[pl.BlockSpec((None, None, th, cw), lambda s, i, c_ref: (s, c_ref[0], i, 0)),
                      pl.BlockSpec((None, th, cw), lambda s, i, c_ref: (s, i, 0))],
            out_specs=pl.BlockSpec((None, th, cw), lambda s, i, c_ref: (s, i, 0))),
        out_shape=jax.ShapeDtypeStruct((N_CHIPS, h, cw), BF16),
        compiler_params=_params("parallel", "parallel"),
    )(c_idx, grad, recv)


def _add_chips(chipsum, recv, s_idx, *, name):
    _, h, cw = chipsum.shape
    th = _reduce_rows(h)

    def body(s_ref, own_ref, r_ref, o_ref):
        o_ref[...] = ((own_ref[...].astype(F32) + r_ref[0].astype(F32)) + r_ref[1].astype(F32)) + r_ref[2].astype(F32)

    return pl.pallas_call(
        body, name=name,
        grid_spec=pltpu.PrefetchScalarGridSpec(
            num_scalar_prefetch=1, grid=(h // th,),
            in_specs=[pl.BlockSpec((None, th, cw), lambda i, s_ref: (s_ref[0], i, 0)),
                      pl.BlockSpec((3, th, cw), lambda i, s_ref: (0, i, 0))],
            out_specs=pl.BlockSpec((th, cw), lambda i, s_ref: (i, 0))),
        out_shape=jax.ShapeDtypeStruct((h, cw), F32),
        compiler_params=_params("parallel"),
    )(s_idx, chipsum, recv)


def _adamw_math(w, g, m, v):
    m = ADAM_B1 * m + (1.0 - ADAM_B1) * g
    v = ADAM_B2 * v + (1.0 - ADAM_B2) * (g * g)
    m_hat = m / (1.0 - ADAM_B1 ** ADAM_STEP)
    v_hat = v / (1.0 - ADAM_B2 ** ADAM_STEP)
    delta = -ADAM_LR * (m_hat / (jnp.sqrt(v_hat) + ADAM_EPS) + ADAM_WD * w)
    return delta, m, v


ADAM_TILE_ELEMS = 256 * 1024


def _adamw(w, g, m, v, *, name):
    layers, rows, cols = w.shape
    tr = rows
    for cand in range(8, rows, 8):
        if rows % cand == 0 and cand * cols <= ADAM_TILE_ELEMS:
            tr = cand
    if rows * cols <= ADAM_TILE_ELEMS:
        tr = rows

    def body(w_ref, g_ref, m_ref, v_ref, d_ref, nm_ref, nv_ref):
        d, nm, nv = _adamw_math(w_ref[...], g_ref[...], m_ref[...], v_ref[...])
        d_ref[...] = d
        nm_ref[...] = nm
        nv_ref[...] = nv

    blk = pl.BlockSpec((None, tr, cols), lambda l, i: (l, i, 0))
    sds = jax.ShapeDtypeStruct(w.shape, F32)
    return pl.pallas_call(
        body, name=name, grid=(layers, rows // tr), in_specs=[blk] * 4, out_specs=[blk] * 3, out_shape=[sds] * 3,
        compiler_params=_params("parallel", "parallel"),
    )(w, g, m, v)


SMALL_LAYOUT = (("loss", 1), ("norm_mix", 16), ("norm_ffn", 16), ("ssm_conv_b", 24), ("ssm_dt_bias", 1),
                ("ssm_a_log", 1), ("ssm_d_skip", 1), ("ssm_norm_w", 16), ("att_q_norm", 1), ("att_k_norm", 1),
                ("conv_w_full", 96))
SMALL_ROWS = 176
N_DEVICES = 8


def _small_packs(dicts):
    parts = []
    for values in dicts:
        for name, rows in SMALL_LAYOUT:
            flat = values[name].reshape(-1).astype(F32)
            parts.append(jnp.pad(flat, (0, rows * LANES - flat.shape[0])).reshape(rows, LANES))
        used = sum(r for _, r in SMALL_LAYOUT)
        parts.append(jnp.zeros((SMALL_ROWS - used, LANES), F32))
    return jnp.concatenate(parts, axis=0).reshape(len(dicts), SMALL_ROWS, LANES)


def _small_unpack(pack, shapes):
    out, off = {}, 0
    for name, rows in SMALL_LAYOUT:
        shape = shapes[name]
        n = math.prod(shape)
        out[name] = pack[off:off + rows].reshape(-1)[:n].reshape(shape)
        off += rows
    return out


def _small_allreduce_adamw(g, w, m, v):
    def body(g_ref, w_ref, m_ref, v_ref, gs_ref, d_ref, nm_ref, nv_ref, buf, send_sems, recv_sems):
        x, y, c = _position()
        pos = (x, y, c)
        me = 4 * x + 2 * y + c
        buf[me] = g_ref[...]
        peers = []
        for k in range(1, N_DEVICES):
            bits = ((k >> 2) & 1, (k >> 1) & 1, k & 1)
            peers.append(tuple(1 - p if b else p for p, b in zip(pos, bits)))
        cps = [pltpu.make_async_remote_copy(src_ref=g_ref, dst_ref=buf.at[me], send_sem=send_sems.at[k],
                                            recv_sem=recv_sems.at[k], device_id=peer, device_id_type=MESH)
               for k, peer in enumerate(peers)]
        for cp in cps:
            cp.start()
        for k, (px, py, pc) in enumerate(peers):
            pltpu.make_async_remote_copy(src_ref=g_ref, dst_ref=buf.at[4 * px + 2 * py + pc],
                                         send_sem=send_sems.at[k], recv_sem=recv_sems.at[k],
                                         device_id=(px, py, pc), device_id_type=MESH).wait_recv()
        for cp in cps:
            cp.wait_send()
        total = buf[0]
        for dev in range(1, N_DEVICES):
            total = total + buf[dev]
        gs_ref[...] = total
        d, nm, nv = _adamw_math(w_ref[...], total, m_ref[...], v_ref[...])
        d_ref[...] = d
        nm_ref[...] = nm
        nv_ref[...] = nv

    vm = pl.BlockSpec(memory_space=pltpu.VMEM)
    sds = jax.ShapeDtypeStruct((SMALL_ROWS, LANES), F32)
    return pl.pallas_call(
        body, name="small_allreduce_adamw", in_specs=[vm] * 4, out_specs=[vm] * 4, out_shape=[sds] * 4,
        scratch_shapes=[pltpu.VMEM((N_DEVICES, SMALL_ROWS, LANES), F32),
                        pltpu.SemaphoreType.DMA((N_DEVICES - 1,)), pltpu.SemaphoreType.DMA((N_DEVICES - 1,))],
    )(g, w, m, v)


SMALL = tuple(n for n, _ in SMALL_LAYOUT if n not in ("loss", "conv_w_full"))
WEIGHTS = ("norm_mix", "norm_ffn", "ssm_w_in", "ssm_conv_w", "ssm_conv_b", "ssm_dt_bias", "ssm_a_log", "ssm_d_skip",
           "ssm_norm_w", "ssm_w_out", "att_w_qkv", "att_q_norm", "att_k_norm", "att_w_o", "ffn_w_gate", "ffn_w_up",
           "ffn_w_down", "ple_w_proj", "ple_w_gate")
COLUMN_SHARDED = ("ssm_w_in", "att_w_qkv", "ffn_w_gate", "ffn_w_up", "ple_w_proj")
LAYERED = ("ffn_w_gate", "ffn_w_up", "ffn_w_down", "ple_w_proj", "ple_w_gate")
UPDATED_TRANSPOSED = ("ssm_w_in", "ffn_w_gate", "ffn_w_up")
GATHER_ORDER = ("ssm_w_in", "ssm_w_out", "att_w_qkv", "att_w_o", "ffn_w_gate", "ffn_w_up", "ffn_w_down",
                "ple_w_proj", "ple_w_gate")


def _layers(n):
    return (0, 1) if n in LAYERED else (None,)


def _tag(key):
    return key[0] if key[1] is None else f"{key[0]}_{key[1]}"


QKV_PARTS = 3


def _weight_slab(w, key):
    n, i = key
    if n == "att_w_qkv":
        a = w[n][0].T
        rows = a.shape[0] // QKV_PARTS
        a = a[i * rows:(i + 1) * rows]
    else:
        a = w[n][0 if i is None else i]
        a = a.T if n in COLUMN_SHARDED else a
    if n == "ssm_w_in":
        a = jnp.pad(a, ((0, W_IN_SLAB_ROWS - a.shape[0]), (0, 0)))
    return a.reshape(2, a.shape[0] // 2, a.shape[1]).astype(BF16)


def _install(prm, key, gathered, own, s_me):
    n, i = key
    full = lax.dynamic_update_slice(gathered, own[None], (s_me, 0, 0, 0))
    full = full.reshape(N_CHIPS, 2 * full.shape[2], full.shape[3])
    if n == "att_w_qkv":
        parts = prm.setdefault("att_w_qkv_parts", {})
        parts[i] = full
        if len(parts) == QKV_PARTS:
            prm[n] = jnp.stack([parts[j] for j in range(QKV_PARTS)], axis=1).reshape(-1, D_MODEL)
        return
    if n == "ssm_w_in":
        rows = (D_INNER + CONV_DIM + SSM_HEADS) // N_CHIPS
        w_in_t = full[:, :rows].reshape(N_CHIPS * rows, D_MODEL)
        prm["ssm_w_z"] = w_in_t[:D_INNER]
        prm["ssm_w_xbc"] = w_in_t[D_INNER:D_INNER + CONV_DIM]
        prm["ssm_w_dt"] = jnp.pad(w_in_t[D_INNER + CONV_DIM:], ((0, LANES - SSM_HEADS), (0, 0)))
        return
    full = full.reshape(N_CHIPS * full.shape[1], full.shape[2])
    if i is None:
        prm[n] = full
    else:
        prm.setdefault(n, [None, None])[i] = full


def _grad_slab(grads, key):
    n, i = key
    g = grads[n] if i is None else grads[n][i]
    if n == "ssm_w_in":
        g = jnp.pad(g.reshape(N_CHIPS, g.shape[0] // N_CHIPS, D_MODEL),
                    ((0, 0), (0, W_IN_SLAB_ROWS - g.shape[0] // N_CHIPS), (0, 0)))
    rows = g.size // (N_CHIPS * g.shape[-1])
    return g.reshape(N_CHIPS, 2, rows // 2, g.shape[-1])


def _natural_shard(n, reduced, shape):
    def one(r):
        if n == "ssm_w_in":
            r = r[:shape[-1]]
        return r.T if n in COLUMN_SHARDED else r
    if n in LAYERED:
        return jnp.stack([one(r) for r in reduced]).reshape(shape)
    return one(reduced[0]).reshape(shape)


def kernel(x, p, norm_mix, norm_ffn, ssm_w_in, ssm_conv_w, ssm_conv_b, ssm_dt_bias, ssm_a_log, ssm_d_skip, ssm_norm_w, ssm_w_out, att_w_qkv, att_q_norm, att_k_norm, att_w_o, ffn_w_gate, ffn_w_up, ffn_w_down, ple_w_proj, ple_w_gate, loss_target, m_norm_mix, m_norm_ffn, m_ssm_w_in, m_ssm_conv_w, m_ssm_conv_b, m_ssm_dt_bias, m_ssm_a_log, m_ssm_d_skip, m_ssm_norm_w, m_ssm_w_out, m_att_w_qkv, m_att_q_norm, m_att_k_norm, m_att_w_o, m_ffn_w_gate, m_ffn_w_up, m_ffn_w_down, m_ple_w_proj, m_ple_w_gate, v_norm_mix, v_norm_ffn, v_ssm_w_in, v_ssm_conv_w, v_ssm_conv_b, v_ssm_dt_bias, v_ssm_a_log, v_ssm_d_skip, v_ssm_norm_w, v_ssm_w_out, v_att_w_qkv, v_att_q_norm, v_att_k_norm, v_att_w_o, v_ffn_w_gate, v_ffn_w_up, v_ffn_w_down, v_ple_w_proj, v_ple_w_gate):
    given = dict(locals())
    w = {n: given[n] for n in WEIGHTS}
    m = {n: given["m_" + n] for n in WEIGHTS}
    v = {n: given["v_" + n] for n in WEIGHTS}
    c_idx = lax.axis_index("c").astype(jnp.int32).reshape(1)
    s_idx = (2 * lax.axis_index("x") + lax.axis_index("y")).astype(jnp.int32).reshape(1)

    s_me = 2 * lax.axis_index("x") + lax.axis_index("y")
    first_core = lax.axis_index("c") == 0

    qkv_parts = [("att_w_qkv", j) for j in range(QKV_PARTS)]
    gather_plan = {
        "ssm_in_z": [("ssm_w_out", None)],
        "ssm_in_xbc": [("ffn_w_gate", 0)],
        "conv_fwd": [("ffn_w_up", 0)],
        "ssd_fwd": [("ffn_w_down", 0), ("ple_w_proj", 0), ("ple_w_gate", 0), ("att_w_o", None)],
        "swiglu_fwd_0": qkv_parts[:2],
        "ffn_down_0": qkv_parts[2:],
        "att_qkv": [(n, 1) for n in LAYERED],
    }
    mamba = [("ssm_w_in", None)]
    own = {k: _weight_slab(w, k) for k in mamba + sum(gather_plan.values(), [])}
    prm = {n: w[n] for n in SMALL}

    def land(group, outputs):
        for k, g in zip(group, outputs):
            _install(prm, k, g, own[k], s_me)

    first = _gather_side([own[k] for k in mamba], whole=[ssm_conv_w[0]])
    _run_side(first, "gather_mamba")
    land(mamba, first.outputs)
    conv = lax.dynamic_update_slice(first.outputs[-1], ssm_conv_w, (s_me, 0, 0))
    prm["ssm_conv_w"] = conv.transpose(1, 0, 2).reshape(CONV_WIDTH, CONV_DIM)

    ffn1 = [(n, 1) for n in LAYERED]
    attention = [("att_w_qkv", None), ("att_w_o", None)]
    ffn0 = [(n, 0) for n in LAYERED] + [("ssm_w_out", None)]
    reduce_plan = {"att_out_dx": [("swap", ffn1)], "att_qkv_dx": [("exchange", ffn1)],
                   "swiglu_bwd_0": [("swap", attention)], "gate_norm_bwd": [("swap", ffn0)],
                   "ssd_bwd": [("exchange", attention), ("exchange", ffn0)],
                   "ssm_dh_z": [("swap", mamba)], "ssm_dh_xbc": [("exchange", mamba)]}
    state = {}

    def swap_side(group):
        state[_tag(group[0]), "g4"] = g4 = [_grad_slab(state["grads"], k) for k in group]
        return _swap_side(g4)

    def add_siblings(group, from_sibling):
        state[_tag(group[0]), "chipsums"] = [
            _add_sibling(g, r, c_idx, name="add_sibling_" + _tag(k))
            for g, r, k in zip(state[_tag(group[0]), "g4"], from_sibling, group)]

    def exchange_side(group):
        return _chip_exchange_side(state[_tag(group[0]), "chipsums"])

    def add_chips(group, from_chips):
        for k, cs, r in zip(group, state[_tag(group[0]), "chipsums"], from_chips):
            state["total", k] = _add_chips(cs, r, s_idx, name="add_chips_" + _tag(k))

    class Plan(_NoOverlap):
        def __init__(self):
            self.carried = {host: _gather_side([own[k] for k in group]) for host, group in gather_plan.items()}

        def begin_backward(self, grads):
            state["grads"] = grads

        def side(self, host):
            if host in reduce_plan:
                self.parts = [swap_side(group) if step == "swap" else exchange_side(group)
                              for step, group in reduce_plan[host]]
                self.carried[host] = _sides_together(self.parts)
            elif host == share_host:
                self.carried[host] = _share_side([state["total", k] for k in order])
            return self.carried.get(host)

        def after(self, host):
            if host in gather_plan:
                land(gather_plan[host], self.carried[host].outputs)
            elif host in reduce_plan:
                _share_out(self.carried[host], self.parts)
                for (step, group), part in zip(reduce_plan[host], self.parts):
                    (add_siblings if step == "swap" else add_chips)(group, part.outputs)
            elif host == share_host:
                state["shared"] = self.carried[host].outputs

    order = mamba + ffn0 + attention + ffn1
    share_host = "ssm_dh_dt"
    loss_row, dx, grads = _local_step(x[0], p[:, 0], loss_target[0], prm, Plan())

    reduced = {}
    for k, theirs in zip(order, state["shared"]):
        lo = jnp.where(first_core, state["total", k], theirs)
        hi = jnp.where(first_core, theirs, state["total", k])
        reduced.setdefault(k[0], {})[k[1]] = jnp.concatenate([lo, hi], axis=0)
    reduced = {n: [by_layer[i] for i in _layers(n)] for n, by_layer in reduced.items()}

    grad, delta, new_m, new_v = {}, {}, {}, {}
    for n in GATHER_ORDER:
        if n in UPDATED_TRANSPOSED:
            flip = lambda a: a.transpose(0, 2, 1)
            cols = w[n].shape[-1]
            g_t = jnp.stack([r[:cols] for r in reduced[n]])
            grad[n] = flip(g_t)
            delta[n], new_m[n], new_v[n] = [flip(o) for o in _adamw(flip(w[n]), g_t, flip(m[n]), flip(v[n]),
                                                                    name="adamw_" + n)]
            continue
        grad[n] = _natural_shard(n, reduced[n], w[n].shape)
        delta[n], new_m[n], new_v[n] = _adamw(w[n], grad[n], m[n], v[n], name="adamw_" + n)

    small_g = {n: (jnp.stack(grads[n]) if isinstance(grads[n], list) else grads[n]) for n in SMALL}
    small_g["loss"] = loss_row
    small_g["conv_w_full"] = grads["ssm_conv_w"]
    zero = {"loss": jnp.zeros((1, LANES), F32), "conv_w_full": jnp.zeros((CONV_WIDTH, CONV_DIM), F32)}
    packs = _small_packs([small_g, {**w, **zero}, {**m, **zero}, {**v, **zero}])
    outs = _small_allreduce_adamw(packs[0], packs[1], packs[2], packs[3])
    shapes = {n: w[n].shape for n in SMALL}
    shapes["loss"] = (1, LANES)
    shapes["conv_w_full"] = (CONV_WIDTH, CONV_DIM)
    sg, sd, sm, sv = [_small_unpack(o, shapes) for o in outs]
    for n in SMALL:
        grad[n], delta[n], new_m[n], new_v[n] = sg[n], sd[n], sm[n], sv[n]
    loss = sg["loss"][0, 0]
    conv_cols = CONV_DIM // N_CHIPS
    grad["ssm_conv_w"] = lax.dynamic_slice(sg["conv_w_full"], (0, s_me * conv_cols), (CONV_WIDTH, conv_cols))[None]
    delta["ssm_conv_w"], new_m["ssm_conv_w"], new_v["ssm_conv_w"] = _adamw(
        ssm_conv_w, grad["ssm_conv_w"], m_ssm_conv_w, v_ssm_conv_w, name="adamw_ssm_conv_w")

    return (loss, dx[None], *[grad[n] for n in WEIGHTS], *[delta[n] for n in WEIGHTS],
            *[new_m[n] for n in WEIGHTS], *[new_v[n] for n in WEIGHTS])
```

```python
import math

import jax
import jax.numpy as jnp
from jax import lax
from jax.experimental import pallas as pl
from jax.experimental.pallas import tpu as pltpu

F32 = jnp.float32
BF16 = jnp.bfloat16
HIGHEST = lax.Precision.HIGHEST

NORM_EPS = 1e-6
ADAM_LR, ADAM_B1, ADAM_B2, ADAM_EPS, ADAM_WD, ADAM_STEP = 0.001, 0.9, 0.999, 1e-08, 0.01, 10

D_MODEL = 1024
D_INNER = 2048
SSM_HEADS = 32
SSM_HEAD_DIM = 64
SSM_GROUPS = 4
SSM_STATE = 128
SSD_CHUNK = 128
CONV_DIM = 3072
CONV_WIDTH = 4
ATT_HEADS = 16
ATT_HEAD_DIM = 64
DIL_PATTERNS = ((128, 1), (512, 4), (2048, 16))
ATT_BLOCK = 128
FFN_HIDDEN = 2816
PLE_DIM = 256

LANES = 128
V7X_VMEM_LIMIT = 56 * 1024 * 1024
NEG_BIG = -1e30

N_CHIPS = 4


def _params(*sem):
    return pltpu.CompilerParams(dimension_semantics=sem, vmem_limit_bytes=V7X_VMEM_LIMIT)


def _tile(n, pref):
    if n <= pref:
        return n
    best = None
    for t in range(LANES, pref + 1, LANES):
        if n % t == 0:
            best = t
    assert best is not None, (n, pref)
    return best


def _sigmoid(v):
    return 1.0 / (1.0 + jnp.exp(-v))


def _dot(a, b):
    return jnp.dot(a, b, preferred_element_type=F32)


def _dot_nt(a, b):
    return lax.dot_general(a, b, (((1,), (1,)), ((), ())), preferred_element_type=F32)


def _dot_tn(a, b):
    return lax.dot_general(a, b, (((0,), (0,)), ((), ())), preferred_element_type=F32)


def _head_block_diag():
    i = lax.broadcasted_iota(jnp.int32, (LANES, LANES), 0) // ATT_HEAD_DIM
    j = lax.broadcasted_iota(jnp.int32, (LANES, LANES), 1) // ATT_HEAD_DIM
    return (i == j).astype(BF16)


def _split_dot(ones, z):
    hi = z.astype(BF16)
    lo = (z - hi.astype(F32)).astype(BF16)
    return _dot(ones, hi) + _dot(ones, lo)


def _head_sums(z, bd, terms=2):
    hi = z.astype(BF16)
    lo = (z - hi.astype(F32)).astype(BF16) if terms == 2 else None
    parts = []
    for t in range(z.shape[1] // LANES):
        sl = slice(t * LANES, (t + 1) * LANES)
        part = _dot(hi[:, sl], bd)
        parts.append(part + _dot(lo[:, sl], bd) if terms == 2 else part)
    return parts[0] if len(parts) == 1 else jnp.concatenate(parts, axis=1)


def _lane_lt64(rows):
    return lax.broadcasted_iota(jnp.int32, (rows, LANES), 1) < ATT_HEAD_DIM


MESH = pl.DeviceIdType.MESH
ANY = pl.BlockSpec(memory_space=pl.ANY)


class _Side:
    def __init__(self, inputs, out_shapes, n_sems, start, finish):
        self.inputs, self.out_shapes, self.n_sems = list(inputs), list(out_shapes), n_sems
        self.start, self.finish = start, finish
        self.outputs = None


class _SemaphoresFrom:
    def __init__(self, sems, first):
        self.sems, self.first = sems, first

    @property
    def at(self):
        return self

    def __getitem__(self, k):
        return self.sems.at[self.first + k]


def _sides_together(sides):
    def run(step):
        def both(ins, outs, send_sems, recv_sems):
            i = o = k = 0
            for s in sides:
                ni, no = len(s.inputs), len(s.out_shapes)
                getattr(s, step)(ins[i:i + ni], outs[o:o + no], _SemaphoresFrom(send_sems, k),
                                 _SemaphoresFrom(recv_sems, k))
                i, o, k = i + ni, o + no, k + s.n_sems
        return both

    return _Side(sum([s.inputs for s in sides], []), sum([s.out_shapes for s in sides], []),
                 sum(s.n_sems for s in sides), run("start"), run("finish"))


def _share_out(together, sides):
    o = 0
    for s in sides:
        s.outputs = together.outputs[o:o + len(s.out_shapes)]
        o += len(s.out_shapes)


def _call(body, side, *, name, grid, in_specs, out_specs, out_shape, scratch_shapes, semantics, args):
    in_specs, out_specs, out_shape = list(in_specs), list(out_specs), list(out_shape)
    scratch_shapes = list(scratch_shapes)
    if side is None:
        return pl.pallas_call(body, name=name, grid=grid, in_specs=in_specs, out_specs=out_specs,
                              out_shape=out_shape, scratch_shapes=scratch_shapes,
                              compiler_params=_params(*semantics))(*args)
    ni, no, ns = len(in_specs), len(out_specs), len(scratch_shapes)
    si, so = len(side.inputs), len(side.out_shapes)

    def hosted(*refs):
        ins, s_ins = refs[:ni], refs[ni:ni + si]
        outs, s_outs = refs[ni + si:ni + si + no], refs[ni + si + no:ni + si + no + so]
        scratch = refs[ni + si + no + so:ni + si + no + so + ns]
        send_sems, recv_sems = refs[-2], refs[-1]
        first = pl.program_id(0) == 0
        last = pl.program_id(0) == grid[0] - 1
        for axis in range(1, len(grid)):
            first = jnp.logical_and(first, pl.program_id(axis) == 0)
            last = jnp.logical_and(last, pl.program_id(axis) == grid[axis] - 1)

        @pl.when(first)
        def _():
            side.start(s_ins, s_outs, send_sems, recv_sems)

        body(*ins, *outs, *scratch)

        @pl.when(last)
        def _():
            side.finish(s_ins, s_outs, send_sems, recv_sems)

    res = pl.pallas_call(
        hosted, name=name, grid=grid, in_specs=in_specs + [ANY] * si, out_specs=out_specs + [ANY] * so,
        out_shape=out_shape + side.out_shapes,
        scratch_shapes=scratch_shapes + [pltpu.SemaphoreType.DMA((side.n_sems,)),
                                         pltpu.SemaphoreType.DMA((side.n_sems,))],
        compiler_params=_params(*["arbitrary"] * len(grid)),
    )(*args, *side.inputs)
    side.outputs = list(res[no:])
    return list(res[:no])


def _matmul(a, b, *, mode, name, out_dtype=F32, addend=None, tm=1024, tn=512, tk_max=3072, side=None, second=None):
    m, k = a.shape
    if mode == "nn":
        k2, n = b.shape
    else:
        n, k2 = b.shape
    assert k == k2, (a.shape, b.shape, mode)
    tm, tn, tk = _tile(m, tm), _tile(n, tn), _tile(k, tk_max)
    nk = k // tk
    has_add = addend is not None
    n_rows = len(second[1]) if second else 0
    n_out = 2 if second else 1

    def body(*refs):
        a_ref, b_ref = refs[0], refs[1]
        add_ref = refs[2] if has_add else None
        row_refs = refs[2 + has_add:2 + has_add + n_rows]
        o_ref, acc_ref = refs[-1 - n_out], refs[-1]
        kk = pl.program_id(2)
        col_tile = pl.program_id(1)
        av = a_ref[...].astype(BF16)
        bv = b_ref[...].astype(BF16)
        part = _dot(av, bv) if mode == "nn" else _dot_nt(av, bv)

        @pl.when(kk == 0)
        def _():
            acc_ref[...] = part

        @pl.when(kk > 0)
        def _():
            acc_ref[...] += part

        @pl.when(kk == nk - 1)
        def _():
            res = acc_ref[...]
            if has_add:
                res = res + add_ref[...]
            o_ref[...] = res.astype(out_dtype)
            if second:
                refs[-2][...] = second[0](res, col_tile, *row_refs).astype(second[2])

    a_spec = pl.BlockSpec((tm, tk), lambda i, j, kk: (i, kk))
    if mode == "nn":
        b_spec = pl.BlockSpec((tk, tn), lambda i, j, kk: (kk, j))
    else:
        b_spec = pl.BlockSpec((tn, tk), lambda i, j, kk: (j, kk))
    tile = pl.BlockSpec((tm, tn), lambda i, j, kk: (i, j))
    in_specs = [a_spec, b_spec]
    args = [a, b]
    if has_add:
        in_specs.append(tile)
        args.append(addend)
    if second:
        in_specs += [pl.BlockSpec((1, tn), lambda i, j, kk: (0, j))] * n_rows
        args += list(second[1])
    outs = _call(
        body, side, name=name, grid=(m // tm, n // tn, nk),
        in_specs=in_specs, out_specs=[tile] * n_out,
        out_shape=[jax.ShapeDtypeStruct((m, n), out_dtype)] + ([jax.ShapeDtypeStruct((m, n), second[2])] if second
                                                                 else []),
        scratch_shapes=[pltpu.VMEM((tm, tn), F32)],
        semantics=("parallel", "parallel", "arbitrary"), args=args,
    )
    return outs if second else outs[0]


def _matmul_tn(a, b, *, name, tm=1408, tn=512, tk=2048):
    t, m = a.shape
    t2, n = b.shape
    assert t == t2
    tm, tn, tk = _tile(m, tm), _tile(n, tn), _tile(t, tk)

    def body(a_ref, b_ref, o_ref):
        part = _dot_tn(a_ref[...].astype(BF16), b_ref[...].astype(BF16))

        @pl.when(pl.program_id(2) == 0)
        def _():
            o_ref[...] = part

        @pl.when(pl.program_id(2) > 0)
        def _():
            o_ref[...] += part

    return pl.pallas_call(
        body, name=name, grid=(m // tm, n // tn, t // tk),
        in_specs=[pl.BlockSpec((tk, tm), lambda i, j, kk: (kk, i)),
                  pl.BlockSpec((tk, tn), lambda i, j, kk: (kk, j))],
        out_specs=pl.BlockSpec((tm, tn), lambda i, j, kk: (i, j)),
        out_shape=jax.ShapeDtypeStruct((m, n), F32),
        compiler_params=_params("parallel", "parallel", "arbitrary"),
    )(a, b)


def _rmsnorm_rows(tile, j, gain_ref):
    r = lax.rsqrt(jnp.mean(tile * tile, axis=-1, keepdims=True) + NORM_EPS)
    return tile * r * gain_ref[...]


def _rmsnorm_fwd(x, gain, *, name):
    t, d = x.shape
    tm = _tile(t, 512)

    def body(x_ref, g_ref, o_ref):
        xv = x_ref[...]
        r = lax.rsqrt(jnp.mean(xv * xv, axis=-1, keepdims=True) + NORM_EPS)
        o_ref[...] = (xv * r * g_ref[...]).astype(BF16)

    return pl.pallas_call(
        body, name=name, grid=(t // tm,),
        in_specs=[pl.BlockSpec((tm, d), lambda i: (i, 0)), pl.BlockSpec((1, d), lambda i: (0, 0))],
        out_specs=pl.BlockSpec((tm, d), lambda i: (i, 0)),
        out_shape=jax.ShapeDtypeStruct((t, d), BF16),
        compiler_params=_params("parallel"),
    )(x, gain)


def _matmul_rmsnorm_bwd(a, b, addend, x, gain, dres, *, name, side=None, tm=512, tk_max=3072, more=None):
    m, k = a.shape
    d = b.shape[1]
    tm, tk = _tile(m, tm), _tile(k, tk_max)
    nk = k // tk

    def body(a_ref, b_ref, *rest):
        add_ref = rest[2 if more else 0] if addend is not None else None
        x_ref, g_ref, dres_ref, dx_ref, dg_ref, acc_ref = rest[-6:]
        i, kk = pl.program_id(0), pl.program_id(1)
        part = _dot(a_ref[...].astype(BF16), b_ref[...].astype(BF16))
        if more:
            part = part + _dot(rest[0][...].astype(BF16), rest[1][...].astype(BF16))

        @pl.when(kk == 0)
        def _():
            acc_ref[...] = part

        @pl.when(kk > 0)
        def _():
            acc_ref[...] += part

        @pl.when(kk == nk - 1)
        def _():
            dyv = acc_ref[...] if addend is None else acc_ref[...] + add_ref[...]
            xv = x_ref[...]
            r = lax.rsqrt(jnp.mean(xv * xv, axis=-1, keepdims=True) + NORM_EPS)
            xh = xv * r
            dxh = dyv * g_ref[...]
            mean = jnp.mean(dxh * xh, axis=-1, keepdims=True)
            dx_ref[...] = dres_ref[...] + r * (dxh - xh * mean)
            gain_part = jnp.sum(dyv * xh, axis=0, keepdims=True)

            @pl.when(i == 0)
            def _():
                dg_ref[...] = gain_part

            @pl.when(i > 0)
            def _():
                dg_ref[...] += gain_part

    row = pl.BlockSpec((tm, d), lambda i, kk: (i, 0))
    vec = pl.BlockSpec((1, d), lambda i, kk: (0, 0))
    return _call(
        body, side, name=name, grid=(m // tm, nk),
        in_specs=[pl.BlockSpec((tm, tk), lambda i, kk: (i, kk)), pl.BlockSpec((tk, d), lambda i, kk: (kk, 0))]
        * (2 if more else 1) + ([row] if addend is not None else []) + [row, vec, row],
        out_specs=[row, vec],
        out_shape=[jax.ShapeDtypeStruct((m, d), F32), jax.ShapeDtypeStruct((1, d), F32)],
        scratch_shapes=[pltpu.VMEM((tm, d), F32)],
        semantics=("arbitrary", "arbitrary"),
        args=(a, b) + (tuple(more) if more else ()) + ((addend,) if addend is not None else ()) + (x, gain, dres),
    )


def _swiglu_fwd(h, w_gate_t, w_up_t, *, name, side=None):
    t, d = h.shape
    f = w_gate_t.shape[0]
    tm, tn = _tile(t, 1024), _tile(f, 256)

    def body(h_ref, wg_ref, wu_ref, g_ref, u_ref, a_ref):
        hv = h_ref[...]
        g = _dot_nt(hv, wg_ref[...])
        u = _dot_nt(hv, wu_ref[...])
        g_ref[...] = g.astype(BF16)
        u_ref[...] = u.astype(BF16)
        a_ref[...] = (g * _sigmoid(g) * u).astype(BF16)

    wspec = pl.BlockSpec((tn, d), lambda i, j: (j, 0))
    ospec = pl.BlockSpec((tm, tn), lambda i, j: (i, j))
    return _call(
        body, side, name=name, grid=(t // tm, f // tn),
        in_specs=[pl.BlockSpec((tm, d), lambda i, j: (i, 0)), wspec, wspec],
        out_specs=[ospec, ospec, ospec],
        out_shape=[jax.ShapeDtypeStruct((t, f), BF16), jax.ShapeDtypeStruct((t, f), BF16),
                   jax.ShapeDtypeStruct((t, f), BF16)],
        scratch_shapes=[], semantics=("parallel", "parallel"), args=(h, w_gate_t, w_up_t),
    )


def _swiglu_bwd(dx, w_down, g, u, *, name, side=None):
    t, d = dx.shape
    f = w_down.shape[0]
    tm, tn = _tile(t, 1024), _tile(f, 256)

    def body(dx_ref, wd_ref, g_ref, u_ref, dg_ref, du_ref):
        dact = _dot_nt(dx_ref[...].astype(BF16), wd_ref[...])
        gv, uv = g_ref[...].astype(F32), u_ref[...].astype(F32)
        sg = _sigmoid(gv)
        dg_ref[...] = (dact * uv * sg * (1.0 + gv * (1.0 - sg))).astype(BF16)
        du_ref[...] = (dact * gv * sg).astype(BF16)

    ospec = pl.BlockSpec((tm, tn), lambda i, j: (i, j))
    return _call(
        body, side, name=name, grid=(t // tm, f // tn),
        in_specs=[pl.BlockSpec((tm, d), lambda i, j: (i, 0)), pl.BlockSpec((tn, d), lambda i, j: (j, 0)),
                  ospec, ospec],
        out_specs=[ospec, ospec],
        out_shape=[jax.ShapeDtypeStruct((t, f), BF16), jax.ShapeDtypeStruct((t, f), BF16)],
        scratch_shapes=[], semantics=("parallel", "parallel"), args=(dx, w_down, g, u),
    )


def _ple_fwd(x, p, w_gate, w_proj_t, *, name, next_gain=None, target=None):
    t, d = x.shape
    e = p.shape[1]
    tm = _tile(t, 512)
    steps = t // tm

    def body(x_ref, p_ref, wg_ref, wp_ref, *rest):
        xv = x_ref[...]
        s = _dot(xv.astype(BF16), wg_ref[...])
        ple = _dot_nt(p_ref[...].astype(BF16), wp_ref[...])
        y = xv + _sigmoid(s) * ple
        if target is None:
            gain_ref, y_ref, h_ref = rest
            y_ref[...] = y
            r = lax.rsqrt(jnp.mean(y * y, axis=-1, keepdims=True) + NORM_EPS)
            h_ref[...] = (y * r * gain_ref[...]).astype(BF16)
        else:
            t_ref, dy_ref, l_ref, acc_ref = rest
            err = y - t_ref[...]
            dy_ref[...] = err * (1.0 / d)
            part = jnp.sum(err * err, axis=0, keepdims=True)

            @pl.when(pl.program_id(0) == 0)
            def _():
                acc_ref[...] = part

            @pl.when(pl.program_id(0) > 0)
            def _():
                acc_ref[...] += part

            @pl.when(pl.program_id(0) == steps - 1)
            def _():
                l_ref[...] = jnp.full((1, LANES), (0.5 / d), F32) * jnp.sum(acc_ref[...])

    row = pl.BlockSpec((tm, d), lambda i: (i, 0))
    fixed = lambda shape: pl.BlockSpec(shape, lambda i: (0, 0))
    in_specs = [row, pl.BlockSpec((tm, e), lambda i: (i, 0)), fixed((d, d)), fixed((d, e))]
    if target is None:
        return pl.pallas_call(
            body, name=name, grid=(steps,), in_specs=in_specs + [fixed((1, d))], out_specs=[row, row],
            out_shape=[jax.ShapeDtypeStruct((t, d), F32), jax.ShapeDtypeStruct((t, d), BF16)],
            compiler_params=_params("parallel"),
        )(x, p, w_gate, w_proj_t, next_gain)
    return pl.pallas_call(
        body, name=name, grid=(steps,), in_specs=in_specs + [row], out_specs=[row, fixed((1, LANES))],
        out_shape=[jax.ShapeDtypeStruct((t, d), F32), jax.ShapeDtypeStruct((1, LANES), F32)],
        scratch_shapes=[pltpu.VMEM((1, d), F32)],
        compiler_params=_params("arbitrary"),
    )(x, p, w_gate, w_proj_t, target)


def _ple_bwd(x, p, w_gate, w_proj_t, dout, *, name):
    t, d = x.shape
    e = p.shape[1]
    tm = _tile(t, 512)

    def body(x_ref, p_ref, wg_ref, wp_ref, do_ref, ds_ref, dple_ref, dx_ref):
        wg = wg_ref[...]
        s = _dot(x_ref[...].astype(BF16), wg)
        ple = _dot_nt(p_ref[...].astype(BF16), wp_ref[...])
        gate = _sigmoid(s)
        dov = do_ref[...]
        dple_ref[...] = (dov * gate).astype(BF16)
        ds = (dov * ple * gate * (1.0 - gate)).astype(BF16)
        ds_ref[...] = ds
        dx_ref[...] = dov + _dot_nt(ds, wg)

    row = pl.BlockSpec((tm, d), lambda i: (i, 0))
    fixed = lambda shape: pl.BlockSpec(shape, lambda i: (0, 0))
    return pl.pallas_call(
        body, name=name, grid=(t // tm,),
        in_specs=[row, pl.BlockSpec((tm, e), lambda i: (i, 0)), fixed((d, d)), fixed((d, e)), row],
        out_specs=[row, row, row],
        out_shape=[jax.ShapeDtypeStruct((t, d), BF16), jax.ShapeDtypeStruct((t, d), BF16),
                   jax.ShapeDtypeStruct((t, d), F32)],
        compiler_params=_params("parallel"),
    )(x, p, w_gate, w_proj_t, dout)


CONV_TIME_TILE = 256
CONV_HALO = 8


def _conv_taps(ext, w):
    acc = ext[CONV_HALO:, :] * w[CONV_WIDTH - 1:CONV_WIDTH, :]
    shifted = [ext[CONV_HALO:, :]]
    for j in range(1, CONV_WIDTH):
        sh = pltpu.roll(ext, j, 0)[CONV_HALO:, :]
        shifted.append(sh)
        acc = acc + sh * w[CONV_WIDTH - 1 - j:CONV_WIDTH - j, :]
    return acc, shifted


def _conv_fwd(u, w, b, side=None):
    t, c = u.shape
    tc = _tile(c, 256)
    tt = CONV_TIME_TILE

    def body(u_ref, w_ref, b_ref, o_ref):
        wv, bv = w_ref[...], b_ref[...]

        def tile(start, ext):
            pre = _conv_taps(ext, wv)[0] + bv
            o_ref[pl.ds(start, tt), :] = pre * _sigmoid(pre)

        tile(0, jnp.concatenate([jnp.zeros((CONV_HALO, tc), F32), u_ref[0:tt, :]], axis=0))

        def loop(i, carry):
            start = pl.multiple_of(i * tt, tt)
            tile(start, u_ref[pl.ds(start - CONV_HALO, tt + CONV_HALO), :])
            return carry

        lax.fori_loop(1, t // tt, loop, 0)

    col = pl.BlockSpec((t, tc), lambda j: (0, j))
    return _call(
        body, side, name="conv_fwd", grid=(c // tc,),
        in_specs=[col, pl.BlockSpec((CONV_WIDTH, tc), lambda j: (0, j)), pl.BlockSpec((1, tc), lambda j: (0, j))],
        out_specs=[col], out_shape=[jax.ShapeDtypeStruct((t, c), F32)],
        scratch_shapes=[], semantics=("parallel",), args=(u, w, b),
    )[0]


def _conv_bwd(u, w, b, dact, side=None):
    t, c = u.shape
    tc = _tile(c, 256)
    tt = CONV_TIME_TILE

    def body(u_ref, w_ref, b_ref, da_ref, du_ref, dw_ref, db_ref, dpre_ref):
        wv, bv = w_ref[...], b_ref[...]

        def tile(start, ext, sums):
            acc, shifted = _conv_taps(ext, wv)
            pre = acc + bv
            sg = _sigmoid(pre)
            dpre = da_ref[pl.ds(start, tt), :] * (sg * (1.0 + pre * (1.0 - sg)))
            dpre_ref[pl.ds(start, tt), :] = dpre
            new = [sums[0] + jnp.sum(dpre, axis=0, keepdims=True)]
            for j in range(CONV_WIDTH):
                new.append(sums[1 + j] + jnp.sum(dpre * shifted[j], axis=0, keepdims=True))
            return tuple(new)

        zero = jnp.zeros((1, tc), F32)
        sums = tile(0, jnp.concatenate([jnp.zeros((CONV_HALO, tc), F32), u_ref[0:tt, :]], axis=0),
                    (zero,) * (1 + CONV_WIDTH))

        def loop(i, sums):
            start = pl.multiple_of(i * tt, tt)
            return tile(start, u_ref[pl.ds(start - CONV_HALO, tt + CONV_HALO), :], sums)

        sums = lax.fori_loop(1, t // tt, loop, sums)
        db_ref[...] = sums[0]
        dw_ref[...] = jnp.concatenate([sums[1 + (CONV_WIDTH - 1 - k)] for k in range(CONV_WIDTH)], axis=0)
        dpre_ref[pl.ds(t, CONV_HALO), :] = jnp.zeros((CONV_HALO, tc), F32)

        def loop2(i, carry):
            start = pl.multiple_of(i * tt, tt)
            ext = dpre_ref[pl.ds(start, tt + CONV_HALO), :]
            acc = ext[0:tt, :] * wv[CONV_WIDTH - 1:CONV_WIDTH, :]
            for j in range(1, CONV_WIDTH):
                acc = acc + pltpu.roll(ext, tt + CONV_HALO - j, 0)[0:tt, :] * wv[CONV_WIDTH - 1 - j:CONV_WIDTH - j, :]
            du_ref[pl.ds(start, tt), :] = acc.astype(BF16)
            return carry

        lax.fori_loop(0, t // tt, loop2, 0)

    col = pl.BlockSpec((t, tc), lambda j: (0, j))
    return _call(
        body, side, name="conv_bwd", grid=(c // tc,),
        in_specs=[col, pl.BlockSpec((CONV_WIDTH, tc), lambda j: (0, j)), pl.BlockSpec((1, tc), lambda j: (0, j)), col],
        out_specs=[col, pl.BlockSpec((CONV_WIDTH, tc), lambda j: (0, j)), pl.BlockSpec((1, tc), lambda j: (0, j))],
        out_shape=[jax.ShapeDtypeStruct((t, c), BF16), jax.ShapeDtypeStruct((CONV_WIDTH, c), F32),
                   jax.ShapeDtypeStruct((1, c), F32)],
        scratch_shapes=[pltpu.VMEM((t + CONV_HALO, tc), F32)],
        semantics=("parallel",), args=(u, w, b, dact),
    )


def _softplus(v):
    e = jnp.exp(-jnp.abs(v))
    w = 1.0 + e
    log1p = jnp.where(w == 1.0, e, jnp.log(w) * (e / jnp.where(w == 1.0, 1.0, w - 1.0)))
    return jnp.maximum(v, 0.0) + log1p


def _split3(z):
    hi = z.astype(BF16)
    rest = z - hi.astype(F32)
    mid = rest.astype(BF16)
    return hi, mid, (rest - mid.astype(F32)).astype(BF16)


def _select_dot(z, ones):
    return sum(_dot(term, ones) for term in _split3(z))


def _ssd_prep_fwd(dt_raw, dt_bias, a_log):
    t = dt_raw.shape[0]
    cl = SSD_CHUNK

    def body(r_ref, b_ref, al_ref, acs_ref, dt_rep_ref, acs_rep_ref):
        dt = _softplus(r_ref[...] + b_ref[...])
        adt = dt * (-jnp.exp(al_ref[...]))
        li = lax.broadcasted_iota(jnp.int32, (cl, cl), 0)
        si = lax.broadcasted_iota(jnp.int32, (cl, cl), 1)
        tri = (si <= li).astype(F32)
        acs = jnp.dot(tri, adt, preferred_element_type=F32, precision=HIGHEST)
        acs_ref[...] = acs
        head = lax.broadcasted_iota(jnp.int32, (LANES, D_INNER), 0)
        chan = lax.broadcasted_iota(jnp.int32, (LANES, D_INNER), 1) // SSM_HEAD_DIM
        spread = (head == chan).astype(BF16)
        dt_rep_ref[...] = _select_dot(dt, spread)
        acs_rep_ref[...] = _select_dot(acs, spread)

    row = pl.BlockSpec((cl, LANES), lambda i: (i, 0))
    wide = pl.BlockSpec((cl, D_INNER), lambda i: (i, 0))
    vec = pl.BlockSpec((1, LANES), lambda i: (0, 0))
    return pl.pallas_call(
        body, name="ssd_prep_fwd", grid=(t // cl,),
        in_specs=[row, vec, vec], out_specs=[row, wide, wide],
        out_shape=[jax.ShapeDtypeStruct((t, LANES), F32), jax.ShapeDtypeStruct((t, D_INNER), F32),
                   jax.ShapeDtypeStruct((t, D_INNER), F32)],
        compiler_params=_params("parallel"),
    )(dt_raw, dt_bias, a_log)


def _ssd_prep_bwd(dt_raw, dt_bias, ddt):
    t = dt_raw.shape[0]
    tm = _tile(t, 512)

    def body(r_ref, b_ref, d_ref, o_ref, db_ref):
        g = d_ref[...] * _sigmoid(r_ref[...] + b_ref[...])
        o_ref[...] = g.astype(BF16)
        part = jnp.sum(g, axis=0, keepdims=True)

        @pl.when(pl.program_id(0) == 0)
        def _():
            db_ref[...] = part

        @pl.when(pl.program_id(0) > 0)
        def _():
            db_ref[...] += part

    row = pl.BlockSpec((tm, LANES), lambda i: (i, 0))
    vec = pl.BlockSpec((1, LANES), lambda i: (0, 0))
    return pl.pallas_call(
        body, name="ssd_prep_bwd", grid=(t // tm,),
        in_specs=[row, vec, row], out_specs=[row, vec],
        out_shape=[jax.ShapeDtypeStruct((t, LANES), BF16), jax.ShapeDtypeStruct((1, LANES), F32)],
        compiler_params=_params("arbitrary"),
    )(dt_raw, dt_bias, ddt)


GROUP_W = D_INNER // SSM_GROUPS
PAIRS_PER_GROUP = GROUP_W // LANES


def _head_cols(acs_pair, lt64):
    rolled = pltpu.roll(acs_pair, ATT_HEAD_DIM, 1)
    return jnp.where(lt64, acs_pair, rolled), jnp.where(lt64, rolled, acs_pair)


def _ssd_fwd(xbc, dt_rep, acs_rep, acs_t, dskip_rep, z, norm_w, side=None):
    t = xbc.shape[0]
    cl = SSD_CHUNK
    nc = t // cl

    def body(xbc_ref, dt_ref, acs_ref, acst_ref, dskip_ref, z_ref, nw_ref, y_ref, hin_ref, yn_ref, state_ref):
        @pl.when(pl.program_id(0) == 0)
        def _():
            state_ref[...] = jnp.zeros_like(state_ref)

        lt64 = _lane_lt64(cl)
        li = lax.broadcasted_iota(jnp.int32, (cl, cl), 0)
        si = lax.broadcasted_iota(jnp.int32, (cl, cl), 1)
        causal = li >= si
        hin_ref[...] = state_ref[...]
        for g in range(SSM_GROUPS):
            gsl = slice(g * GROUP_W, (g + 1) * GROUP_W)
            xg = xbc_ref[:, gsl]
            bg = xbc_ref[:, D_INNER + g * SSM_STATE:D_INNER + (g + 1) * SSM_STATE]
            cg = xbc_ref[:, D_INNER + SSM_GROUPS * SSM_STATE + g * SSM_STATE:
                         D_INNER + SSM_GROUPS * SSM_STATE + (g + 1) * SSM_STATE]
            acs = acs_ref[:, gsl]
            xdt = xg * dt_ref[:, gsl]
            atot = acs[cl - 1:cl, :]
            hin = state_ref[:, gsl]
            cgb = cg.astype(BF16)
            gmat = _dot_nt(cgb, bg.astype(BF16))
            yoff = _dot(cgb, hin.astype(BF16)) * jnp.exp(acs)
            snew = _dot(bg.T.astype(BF16), (xdt * jnp.exp(atot - acs)).astype(BF16))
            state_ref[:, gsl] = hin * jnp.exp(atot) + snew
            xdtb = xdt.astype(BF16)
            for pr in range(PAIRS_PER_GROUP):
                psl = slice(pr * LANES, (pr + 1) * LANES)
                cols = _head_cols(acs[:, psl], lt64)
                xp = xdtb[:, psl]
                ys = []
                for hh in range(2):
                    h = (g * PAIRS_PER_GROUP + pr) * 2 + hh
                    seg = cols[hh] - acst_ref[h:h + 1, :]
                    lm = jnp.exp(jnp.where(causal, seg, NEG_BIG))
                    ys.append(_dot((gmat * lm).astype(BF16), xp))
                ydiag = jnp.where(lt64, ys[0], ys[1])
                osl = slice(g * GROUP_W + pr * LANES, g * GROUP_W + (pr + 1) * LANES)
                y_ref[:, osl] = ydiag + yoff[:, psl] + xg[:, psl] * dskip_ref[:, osl]
            zv = z_ref[:, gsl]
            v = y_ref[:, gsl] * (zv * _sigmoid(zv))
            r = lax.rsqrt(jnp.mean(v * v, axis=-1, keepdims=True) + NORM_EPS)
            yn_ref[:, gsl] = (v * r * nw_ref[:, gsl]).astype(BF16)

    row = lambda w: pl.BlockSpec((cl, w), lambda c: (c, 0))
    vec = pl.BlockSpec((1, D_INNER), lambda c: (0, 0))
    return _call(
        body, side, name="ssd_fwd", grid=(nc,),
        in_specs=[row(CONV_DIM), row(D_INNER), row(D_INNER),
                  pl.BlockSpec((SSM_HEADS, cl), lambda c: (0, c)), vec, row(D_INNER), vec],
        out_specs=[row(D_INNER), pl.BlockSpec((None, SSM_STATE, D_INNER), lambda c: (c, 0, 0)), row(D_INNER)],
        out_shape=[jax.ShapeDtypeStruct((t, D_INNER), F32), jax.ShapeDtypeStruct((nc, SSM_STATE, D_INNER), F32),
                   jax.ShapeDtypeStruct((t, D_INNER), BF16)],
        scratch_shapes=[pltpu.VMEM((SSM_STATE, D_INNER), F32)],
        semantics=("arbitrary",), args=(xbc, dt_rep, acs_rep, acs_t, dskip_rep, z, norm_w),
    )


def _ssd_bwd(xbc, dt_rep, acs_rep, acs_t, dskip_rep, a_rep, hin_all, dy, side=None):
    t = xbc.shape[0]
    cl = SSD_CHUNK
    nc = t // cl

    def body(xbc_ref, dt_ref, acs_ref, acst_ref, dskip_ref, a_ref, hin_ref, dy_ref,
             dxbc_ref, ddt_ref, da_ref, dds_ref, dstate_ref, dacs_ref, dxs_ref):
        step = pl.program_id(0)

        @pl.when(step == 0)
        def _():
            dstate_ref[...] = jnp.zeros_like(dstate_ref)
            da_ref[...] = jnp.zeros_like(da_ref)
            dds_ref[...] = jnp.zeros_like(dds_ref)

        bd = _head_block_diag()
        lt64 = _lane_lt64(cl)
        li = lax.broadcasted_iota(jnp.int32, (cl, cl), 0)
        si = lax.broadcasted_iota(jnp.int32, (cl, cl), 1)
        lower = li >= si
        upper = si >= li
        last_row = lax.broadcasted_iota(jnp.int32, (cl, GROUP_W), 0) == cl - 1
        for g in range(SSM_GROUPS):
            gsl = slice(g * GROUP_W, (g + 1) * GROUP_W)
            bsl = slice(D_INNER + g * SSM_STATE, D_INNER + (g + 1) * SSM_STATE)
            csl = slice(D_INNER + SSM_GROUPS * SSM_STATE + g * SSM_STATE,
                        D_INNER + SSM_GROUPS * SSM_STATE + (g + 1) * SSM_STATE)
            xg = xbc_ref[:, gsl]
            bg = xbc_ref[:, bsl]
            cg = xbc_ref[:, csl]
            bgb, cgb = bg.astype(BF16), cg.astype(BF16)
            acs = acs_ref[:, gsl]
            xdt = xg * dt_ref[:, gsl]
            atot = acs[cl - 1:cl, :]
            eg = jnp.exp(acs)
            dk = jnp.exp(atot - acs)
            etot = jnp.exp(atot)
            hin = hin_ref[:, gsl]
            hinb = hin.astype(BF16)
            dh = dstate_ref[:, gsl]
            dhb = dh.astype(BF16)
            dyg = dy_ref[:, gsl]

            gmat = _dot_nt(cgb, bgb)
            gmat_t = _dot_nt(bgb, cgb)
            ch = _dot(cgb, hinb)
            dacs = _head_sums(dyg * ch * eg, bd)
            dye = (dyg * eg).astype(BF16)
            dc = _dot_nt(dye, hinb)
            dhin = _dot(cg.T.astype(BF16), dye)
            bdh = _dot(bgb, dhb)
            dxs = bdh * dk
            xdk = xdt * dk
            db = _dot_nt(xdk.astype(BF16), dhb)
            ddk = _head_sums(bdh * xdk, bd)
            dacs = dacs - ddk
            datot = jnp.sum(ddk, axis=0, keepdims=True) + etot * _head_sums(
                jnp.sum(dh * hin, axis=0, keepdims=True), bd)
            dacs = dacs + jnp.where(last_row, datot, 0.0)
            dstate_ref[:, gsl] = dh * etot + dhin

            xdtb = xdt.astype(BF16)
            dgsum = jnp.zeros((cl, cl), F32)
            dgsum_t = jnp.zeros((cl, cl), F32)
            for pr in range(PAIRS_PER_GROUP):
                psl = slice(pr * LANES, (pr + 1) * LANES)
                cols = _head_cols(acs[:, psl], lt64)
                xp = xdtb[:, psl]
                dyp = dyg[:, psl].astype(BF16)
                dx1, dac = [], []
                for hh in range(2):
                    h = (g * PAIRS_PER_GROUP + pr) * 2 + hh
                    mine = lt64 if hh == 0 else jnp.logical_not(lt64)
                    row = acst_ref[h:h + 1, :]
                    lm = jnp.exp(jnp.where(lower, cols[hh] - row, NEG_BIG))
                    lm_t = jnp.exp(jnp.where(upper, row - cols[hh], NEG_BIG))
                    dyh = jnp.where(mine, dyp, jnp.zeros_like(dyp))
                    xh = jnp.where(mine, xp, jnp.zeros_like(xp))
                    dm = _dot_nt(dyh, xp)
                    dm_t = _dot_nt(xh, dyp)
                    m_t = gmat_t * lm_t
                    dx1.append(_dot(m_t.astype(BF16), dyp))
                    w = dm * (gmat * lm)
                    w_t = dm_t * m_t
                    dac.append(jnp.sum(w, axis=1, keepdims=True) - jnp.sum(w_t, axis=1, keepdims=True))
                    dgsum = dgsum + dm * lm
                    dgsum_t = dgsum_t + dm_t * lm_t
                osl = slice(g * GROUP_W + pr * LANES, g * GROUP_W + (pr + 1) * LANES)
                dxs_ref[:, osl] = dxs[:, psl] + jnp.where(lt64, dx1[0], dx1[1])
                dacs_ref[:, osl] = dacs[:, psl] + jnp.where(lt64, jnp.broadcast_to(dac[0], (cl, LANES)),
                                                             jnp.broadcast_to(dac[1], (cl, LANES)))
            dxbc_ref[:, csl] = dc + _dot(dgsum.astype(BF16), bgb)
            dxbc_ref[:, bsl] = db + _dot(dgsum_t.astype(BF16), cgb)

        dadt = _split_dot(upper.astype(BF16), dacs_ref[...])
        xall = xbc_ref[:, 0:D_INNER]
        dtall = dt_ref[...]
        dxsall = dxs_ref[...]
        dyall = dy_ref[...]
        ddt_rep = dadt * a_ref[...] + _head_sums(dxsall * xall, bd)
        chan = lax.broadcasted_iota(jnp.int32, (D_INNER, LANES), 0)
        head = lax.broadcasted_iota(jnp.int32, (D_INNER, LANES), 1)
        ddt_ref[...] = _select_dot(ddt_rep, (chan == head * SSM_HEAD_DIM).astype(BF16))
        dxbc_ref[:, 0:D_INNER] = dxsall * dtall + dyall * dskip_ref[...]
        da_ref[...] += jnp.sum(dadt * dtall, axis=0, keepdims=True)
        dds_ref[...] += jnp.sum(dyall * xall, axis=0, keepdims=True)

        @pl.when(step == nc - 1)
        def _():
            dds_ref[...] = _head_sums(dds_ref[...], bd)

    row = lambda w: pl.BlockSpec((cl, w), lambda c: (nc - 1 - c, 0))
    vec = pl.BlockSpec((1, D_INNER), lambda c: (0, 0))
    return _call(
        body, side, name="ssd_bwd", grid=(nc,),
        in_specs=[row(CONV_DIM), row(D_INNER), row(D_INNER),
                  pl.BlockSpec((SSM_HEADS, cl), lambda c: (0, nc - 1 - c)), vec, vec,
                  pl.BlockSpec((None, SSM_STATE, D_INNER), lambda c: (nc - 1 - c, 0, 0)), row(D_INNER)],
        out_specs=[row(CONV_DIM), row(LANES), vec, vec],
        out_shape=[jax.ShapeDtypeStruct((t, CONV_DIM), F32), jax.ShapeDtypeStruct((t, LANES), F32),
                   jax.ShapeDtypeStruct((1, D_INNER), F32), jax.ShapeDtypeStruct((1, D_INNER), F32)],
        scratch_shapes=[pltpu.VMEM((SSM_STATE, D_INNER), F32), pltpu.VMEM((cl, D_INNER), F32),
                        pltpu.VMEM((cl, D_INNER), F32)],
        semantics=("arbitrary",), args=(xbc, dt_rep, acs_rep, acs_t, dskip_rep, a_rep, hin_all, dy),
    )


def _gate_norm_bwd(y, z, w, dx, w_out, side=None):
    t, c = y.shape
    d = dx.shape[1]
    tm = _tile(t, 256)

    def body(y_ref, z_ref, w_ref, dx_ref, wo_ref, dy_ref, dz_ref, dw_ref):
        @pl.when(pl.program_id(0) == 0)
        def _():
            dw_ref[...] = jnp.zeros_like(dw_ref)

        dxb = dx_ref[...].astype(BF16)
        for g in range(SSM_GROUPS):
            gsl = slice(g * GROUP_W, (g + 1) * GROUP_W)
            zv, yv, dov = z_ref[:, gsl], y_ref[:, gsl], _dot_nt(dxb, wo_ref[gsl, :])
            sg = _sigmoid(zv)
            sz = zv * sg
            v = yv * sz
            r = lax.rsqrt(jnp.mean(v * v, axis=-1, keepdims=True) + NORM_EPS)
            vh = v * r
            dvh = dov * w_ref[:, gsl]
            mean = jnp.mean(dvh * vh, axis=-1, keepdims=True)
            dv = r * (dvh - vh * mean)
            dy_ref[:, gsl] = dv * sz
            dz_ref[:, gsl] = (dv * yv * (sg * (1.0 + zv * (1.0 - sg)))).astype(BF16)
            dw_ref[:, gsl] += jnp.sum(dov * vh, axis=0, keepdims=True)

    row = pl.BlockSpec((tm, c), lambda i: (i, 0))
    vec = pl.BlockSpec((1, c), lambda i: (0, 0))
    return _call(
        body, side, name="gate_norm_bwd", grid=(t // tm,),
        in_specs=[row, row, vec, pl.BlockSpec((tm, d), lambda i: (i, 0)), pl.BlockSpec((c, d), lambda i: (0, 0))],
        out_specs=[row, row, vec],
        out_shape=[jax.ShapeDtypeStruct((t, c), F32), jax.ShapeDtypeStruct((t, c), BF16),
                   jax.ShapeDtypeStruct((1, c), F32)],
        scratch_shapes=[], semantics=("arbitrary",), args=(y, z, w, dx, w_out),
    )


ATT_W = ATT_HEADS * ATT_HEAD_DIM
N_QKV_BLOCKS = 9
ATT_SCALE = 1.0 / math.sqrt(ATT_HEAD_DIM)


def _head_rmsnorm(x, gain, bd):
    ms = _head_sums(x * x, bd, terms=1) * (1.0 / ATT_HEAD_DIM)
    return x * lax.rsqrt(ms + NORM_EPS) * gain


def _class_rows(ref, blk, r, dil):
    span = ATT_BLOCK * dil
    sub = ref.at[pl.ds(pl.multiple_of(blk * span, span), span), :]
    return sub[...] if dil == 1 else sub[pl.ds(r, ATT_BLOCK, stride=dil), :]


def _store_class_rows(ref, blk, r, dil, val):
    span = ATT_BLOCK * dil
    sub = ref.at[pl.ds(pl.multiple_of(blk * span, span), span), :]
    if dil == 1:
        sub[...] = val
    else:
        sub[pl.ds(r, ATT_BLOCK, stride=dil), :] = val


PAIRS = ATT_HEADS // 2


def _pair_col(g, j):
    return lambda pair: (0, (g * 3 + j) * PAIRS + pair)


def _pair_slopes(pair):
    steps = jnp.full((1, 2 * ATT_BLOCK), 2 * pair + 1, jnp.int32).astype(F32)
    first = jnp.exp(steps * (-0.5 * math.log(2.0)))
    return first, first * (2.0 ** -0.5)


NORM_ROWS = 512


ROW_SLICES = 4
SLICE_ROWS = 2 * ATT_BLOCK // ROW_SLICES


def _fill_band_bias(bias_ref, pair, dil, transposed):
    bq = ATT_BLOCK
    a = lax.broadcasted_iota(jnp.int32, (2 * bq, 2 * bq), 0) % bq
    b = lax.broadcasted_iota(jnp.int32, (2 * bq, 2 * bq), 1)
    dist = (b - a) if transposed else (a + bq - b)
    in_band = (dist >= 0) & (dist <= bq)
    s0, s1 = _pair_slopes(pair)
    first_head = lax.broadcasted_iota(jnp.int32, (2 * bq, 2 * bq), 0) < bq
    bias = jnp.where(first_head, s0, s1) * (dist.astype(F32) * float(dil))
    inside = (b < bq) if transposed else (b >= bq)
    bias_ref[1] = jnp.where(in_band, bias, -NEG_BIG)
    bias_ref[0] = jnp.where(in_band & inside, bias, -NEG_BIG)


def _row_slices():
    return [slice(i * SLICE_ROWS, (i + 1) * SLICE_ROWS) for i in range(ROW_SLICES)]


def _stack_heads(tile):
    rows = lax.broadcasted_iota(jnp.int32, (2 * ATT_BLOCK, LANES), 0) < ATT_BLOCK
    lanes = lax.broadcasted_iota(jnp.int32, (2 * ATT_BLOCK, LANES), 1) < ATT_HEAD_DIM
    both = jnp.concatenate([tile, tile], axis=0)
    return jnp.where(rows == lanes, both, jnp.zeros_like(both))


def _unstack_heads(stacked, lt64):
    return jnp.where(lt64, stacked[:ATT_BLOCK], stacked[ATT_BLOCK:])


ITEMS_PER_PASS = 4


def _item_loop(nb, dil, work):
    if dil == 1:
        def trip(i, carry):
            work([(i * ITEMS_PER_PASS + b, 0) for b in range(ITEMS_PER_PASS)])
            return carry

        lax.fori_loop(0, nb // ITEMS_PER_PASS, trip, 0)
    else:
        def trip(n, carry):
            for r0 in range(0, dil, ITEMS_PER_PASS):
                work([(n, r0 + j) for j in range(ITEMS_PER_PASS)])
            return carry

        lax.fori_loop(0, nb, trip, 0)


def _qk_normalised(tile, j, gq_ref, gk_ref):
    kind = (j // (ATT_W // tile.shape[1])) % 3
    gain = jnp.where(kind == 0, gq_ref[...] * ATT_SCALE, gk_ref[...])
    return jnp.where(kind == 2, tile, _head_rmsnorm(tile, gain, _head_block_diag()))


def _attn_fwd(qkn, g, dil):
    t = qkn.shape[0]
    nb = t // dil // ATT_BLOCK
    bq = ATT_BLOCK

    def body(qn_ref, kn_ref, v_ref, o_ref, l_ref, bias_ref):
        _fill_band_bias(bias_ref, pl.program_id(0), dil, False)
        lt64 = _lane_lt64(bq)

        def work(items):
            scores, values, probs = [], [], []
            for n, r in items:
                prev = jnp.maximum(n - 1, 0)
                q2 = _stack_heads(_class_rows(qn_ref, n, r, dil).astype(BF16))
                kcat = jnp.concatenate([_class_rows(kn_ref, prev, r, dil), _class_rows(kn_ref, n, r, dil)],
                                       axis=0).astype(BF16)
                values.append(jnp.concatenate([_class_rows(v_ref, prev, r, dil), _class_rows(v_ref, n, r, dil)],
                                              axis=0).astype(BF16))
                scores.append(_dot_nt(q2, kcat))
            for (n, r), sc in zip(items, scores):
                bias = bias_ref.at[jnp.minimum(n, 1)]
                ps, inv, lses = [], [], []
                for rows in _row_slices():
                    s = sc[rows] - bias[rows, :]
                    m = jnp.max(s, axis=1, keepdims=True)
                    p = jnp.exp(s - m)
                    l = jnp.sum(p, axis=1, keepdims=True)
                    ps.append(p.astype(BF16))
                    inv.append(jnp.broadcast_to(1.0 / l, (SLICE_ROWS, LANES)))
                    lses.append(jnp.broadcast_to(m + jnp.log(l), (SLICE_ROWS, LANES)))
                probs.append((jnp.concatenate(ps, axis=0), jnp.concatenate(inv, axis=0)))
                _store_class_rows(l_ref, n, r, dil, _unstack_heads(jnp.concatenate(lses, axis=0), lt64))
            for (n, r), (p, inv), vcat in zip(items, probs, values):
                _store_class_rows(o_ref, n, r, dil, _unstack_heads(_dot(p, vcat) * inv, lt64))

        _item_loop(nb, dil, work)

    col = lambda j: pl.BlockSpec((t, LANES), _pair_col(g, j))
    out = pl.BlockSpec((t, LANES), lambda pair: (0, pair))
    return pl.pallas_call(
        body, name=f"attn_fwd_g{g}", grid=(PAIRS,),
        in_specs=[col(0), col(1), col(2)], out_specs=[out, out],
        out_shape=[jax.ShapeDtypeStruct((t, ATT_W), F32), jax.ShapeDtypeStruct((t, ATT_W), F32)],
        scratch_shapes=[pltpu.VMEM((2, 2 * bq, 2 * bq), F32)],
        compiler_params=_params("parallel"),
    )(qkn, qkn, qkn)


def _one_per_head(rep):
    chan = lax.broadcasted_iota(jnp.int32, (ATT_W, LANES), 0)
    head = lax.broadcasted_iota(jnp.int32, (ATT_W, LANES), 1)
    return _select_dot(rep, (chan == head * ATT_HEAD_DIM).astype(BF16))


def _attn_out_fwd(outs, lses, w_o, x0, next_gain):
    t, d = x0.shape
    tm = _tile(t, 256)

    def body(o0, o1, o2, l0, l1, l2, wo_ref, x_ref, g_ref, of_ref, lt_ref, lc_ref, x1_ref, h_ref):
        a, b, c = l0[...], l1[...], l2[...]
        m = jnp.maximum(jnp.maximum(a, b), c)
        ea, eb, ec = jnp.exp(a - m), jnp.exp(b - m), jnp.exp(c - m)
        ssum = ea + eb + ec
        o = (ea * o0[...] + eb * o1[...] + ec * o2[...]) / ssum
        of_ref[...] = o
        lse = m + jnp.log(ssum)
        lt_ref[...] = lse
        lc_ref[...] = _one_per_head(lse)
        x1 = x_ref[...] + _dot(o.astype(BF16), wo_ref[...])
        x1_ref[...] = x1
        r = lax.rsqrt(jnp.mean(x1 * x1, axis=-1, keepdims=True) + NORM_EPS)
        h_ref[...] = (x1 * r * g_ref[...]).astype(BF16)

    row = pl.BlockSpec((tm, ATT_W), lambda i: (i, 0))
    xrow = pl.BlockSpec((tm, d), lambda i: (i, 0))
    return pl.pallas_call(
        body, name="att_out", grid=(t // tm,),
        in_specs=[row] * 6 + [pl.BlockSpec((ATT_W, d), lambda i: (0, 0)), xrow, pl.BlockSpec((1, d), lambda i: (0, 0))],
        out_specs=[row, row, pl.BlockSpec((tm, LANES), lambda i: (i, 0)), xrow, xrow],
        out_shape=[jax.ShapeDtypeStruct((t, ATT_W), F32), jax.ShapeDtypeStruct((t, ATT_W), F32),
                   jax.ShapeDtypeStruct((t, LANES), F32), jax.ShapeDtypeStruct((t, d), F32),
                   jax.ShapeDtypeStruct((t, d), BF16)],
        compiler_params=_params("parallel"),
    )(*outs, *lses, w_o, x0, next_gain)


def _attn_out_bwd(dx, w_o, o, side=None):
    t, d = dx.shape
    tm = _tile(t, 256)

    def body(dx_ref, wo_ref, o_ref, do_ref, dl_ref, dc_ref):
        do = _dot_nt(dx_ref[...].astype(BF16), wo_ref[...])
        do_ref[...] = do
        dl = _head_sums(do * o_ref[...], _head_block_diag())
        dl_ref[...] = dl
        dc_ref[...] = _one_per_head(dl)

    row = pl.BlockSpec((tm, ATT_W), lambda i: (i, 0))
    return _call(
        body, side, name="att_out_dx", grid=(t // tm,),
        in_specs=[pl.BlockSpec((tm, d), lambda i: (i, 0)), pl.BlockSpec((ATT_W, d), lambda i: (0, 0)), row],
        out_specs=[row, row, pl.BlockSpec((tm, LANES), lambda i: (i, 0))],
        out_shape=[jax.ShapeDtypeStruct((t, ATT_W), F32), jax.ShapeDtypeStruct((t, ATT_W), F32),
                   jax.ShapeDtypeStruct((t, LANES), F32)],
        scratch_shapes=[], semantics=("parallel",), args=(dx, w_o, o),
    )


def _head_rmsnorm_bwd(x_ref, dy_ref, gain_ref, dx_ref, dgain_ref):
    bd = _head_block_diag()
    gain = gain_ref[...]

    def step(i, acc):
        rows = pl.ds(pl.multiple_of(i * NORM_ROWS, NORM_ROWS), NORM_ROWS)
        x, dy = x_ref[rows, :], dy_ref[rows, :]
        r = lax.rsqrt(_head_sums(x * x, bd, terms=1) * (1.0 / ATT_HEAD_DIM) + NORM_EPS)
        xh = x * r
        dxh = dy * gain
        mean = _head_sums(dxh * xh, bd, terms=1) * (1.0 / ATT_HEAD_DIM)
        dx_ref[rows, :] = (r * (dxh - xh * mean)).astype(BF16)
        return acc + jnp.sum(dy * xh, axis=0, keepdims=True)

    acc = lax.fori_loop(0, x_ref.shape[0] // NORM_ROWS, step, jnp.zeros((1, LANES), F32))
    dgain_ref[...] = jnp.broadcast_to(acc, dgain_ref.shape)


def _attn_bwd_dq(qkv, qkn, gq, do, l_rep, dl_rep, g, dil):
    t = qkv.shape[0]
    nb = t // dil // ATT_BLOCK
    bq = ATT_BLOCK

    def body(q_ref, qn_ref, kn_ref, v_ref, gq_ref, do_ref, l_ref, dl_ref, dx_ref, dgain_ref, bias_ref, dq_ref):
        _fill_band_bias(bias_ref, pl.program_id(0), dil, False)
        lt64 = _lane_lt64(bq)

        def per_row(tile):
            cols = _head_cols(tile, lt64)
            half = jnp.concatenate([cols[0], cols[1]], axis=0)
            return jnp.concatenate([half, half], axis=1)

        def work(items):
            products, keys, dscores = [], [], []
            for n, r in items:
                prev = jnp.maximum(n - 1, 0)
                q2 = _stack_heads(_class_rows(qn_ref, n, r, dil).astype(BF16))
                do2 = _stack_heads(_class_rows(do_ref, n, r, dil).astype(BF16))
                kcat = jnp.concatenate([_class_rows(kn_ref, prev, r, dil), _class_rows(kn_ref, n, r, dil)],
                                       axis=0).astype(BF16)
                vcat = jnp.concatenate([_class_rows(v_ref, prev, r, dil), _class_rows(v_ref, n, r, dil)],
                                       axis=0).astype(BF16)
                keys.append(kcat)
                products.append((_dot_nt(q2, kcat), _dot_nt(do2, vcat)))
            for (n, r), (scores, dps) in zip(items, products):
                bias = bias_ref.at[jnp.minimum(n, 1)]
                lse = per_row(_class_rows(l_ref, n, r, dil))
                dl = per_row(_class_rows(dl_ref, n, r, dil))
                dss = []
                for rows in _row_slices():
                    p = jnp.exp(scores[rows] - bias[rows, :] - lse[rows])
                    dss.append((p * (dps[rows] - dl[rows])).astype(BF16))
                dscores.append(jnp.concatenate(dss, axis=0))
            for (n, r), ds, kcat in zip(items, dscores, keys):
                _store_class_rows(dq_ref, n, r, dil, _unstack_heads(_dot(ds, kcat) * ATT_SCALE, lt64))

        _item_loop(nb, dil, work)
        _head_rmsnorm_bwd(q_ref, dq_ref, gq_ref, dx_ref, dgain_ref)

    col = lambda j: pl.BlockSpec((t, LANES), _pair_col(g, j))
    vec = pl.BlockSpec((1, LANES), lambda pair: (0, 0))
    tok = pl.BlockSpec((t, LANES), lambda pair: (0, pair))
    return pl.pallas_call(
        body, name=f"attn_bwd_dq_g{g}", grid=(PAIRS,),
        in_specs=[col(0), col(0), col(1), col(2), vec, tok, tok, tok],
        out_specs=[tok, pl.BlockSpec((None, 8, LANES), lambda pair: (pair, 0, 0))],
        out_shape=[jax.ShapeDtypeStruct((t, ATT_W), BF16), jax.ShapeDtypeStruct((PAIRS, 8, LANES), F32)],
        scratch_shapes=[pltpu.VMEM((2, 2 * bq, 2 * bq), F32), pltpu.VMEM((t, LANES), F32)],
        compiler_params=_params("parallel"),
    )(qkv, qkn, qkn, qkn, gq, do, l_rep, dl_rep)


def _attn_bwd_dkv(qkv, qkn, gk, do, l_row, dl_row, g, dil):
    t = qkv.shape[0]
    nb = t // dil // ATT_BLOCK
    bq = ATT_BLOCK

    def body(k_ref, qn_ref, kn_ref, v_ref, gk_ref, do_ref, l_ref, dl_ref, dkx_ref, dvx_ref, dgain_ref, bias_ref,
             dk_ref, dv_ref):
        _fill_band_bias(bias_ref, pl.program_id(0), dil, True)
        lt64 = _lane_lt64(bq)

        def per_query(ref, hh, lane_c, lane_n):
            return jnp.concatenate([ref[hh:hh + 1, pl.ds(lane_c, bq)], ref[hh:hh + 1, pl.ds(lane_n, bq)]], axis=1)

        def work(items):
            products, operands, weights = [], [], []
            for n, r in items:
                nxt = jnp.minimum(n + 1, nb - 1)
                k2 = _stack_heads(_class_rows(kn_ref, n, r, dil).astype(BF16))
                v2 = _stack_heads(_class_rows(v_ref, n, r, dil).astype(BF16))
                qcat = jnp.concatenate([_class_rows(qn_ref, n, r, dil), _class_rows(qn_ref, nxt, r, dil)],
                                       axis=0).astype(BF16)
                docat = jnp.concatenate([_class_rows(do_ref, n, r, dil), _class_rows(do_ref, nxt, r, dil)],
                                        axis=0).astype(BF16)
                operands.append((qcat, docat))
                products.append((_dot_nt(k2, qcat), _dot_nt(v2, docat)))
            for (n, r), (scores, dps) in zip(items, products):
                nxt = jnp.minimum(n + 1, nb - 1)
                bias = bias_ref.at[jnp.where(n == nb - 1, 0, 1)]
                lane_c = pl.multiple_of((r * nb + n) * bq, bq)
                lane_n = pl.multiple_of((r * nb + nxt) * bq, bq)
                lse = [per_query(l_ref, hh, lane_c, lane_n) for hh in range(2)]
                dl = [per_query(dl_ref, hh, lane_c, lane_n) for hh in range(2)]
                pts, dss = [], []
                for i, rows in enumerate(_row_slices()):
                    hh = i * SLICE_ROWS // bq
                    p_t = jnp.exp(scores[rows] - bias[rows, :] - lse[hh])
                    pts.append(p_t.astype(BF16))
                    dss.append((p_t * (dps[rows] - dl[hh])).astype(BF16))
                weights.append((jnp.concatenate(pts, axis=0), jnp.concatenate(dss, axis=0)))
            for (n, r), (p_t, ds_t), (qcat, docat) in zip(items, weights, operands):
                _store_class_rows(dv_ref, n, r, dil, _unstack_heads(_dot(p_t, docat), lt64))
                _store_class_rows(dk_ref, n, r, dil, _unstack_heads(_dot(ds_t, qcat), lt64))

        _item_loop(nb, dil, work)
        _head_rmsnorm_bwd(k_ref, dk_ref, gk_ref, dkx_ref, dgain_ref)

        def cast_rows(i, carry):
            rows = pl.ds(pl.multiple_of(i * NORM_ROWS, NORM_ROWS), NORM_ROWS)
            dvx_ref[rows, :] = dv_ref[rows, :].astype(BF16)
            return carry

        lax.fori_loop(0, t // NORM_ROWS, cast_rows, 0)

    col = lambda j: pl.BlockSpec((t, LANES), _pair_col(g, j))
    vec = pl.BlockSpec((1, LANES), lambda pair: (0, 0))
    tok = pl.BlockSpec((t, LANES), lambda pair: (0, pair))
    rows = pl.BlockSpec((None, 8, t), lambda pair: (pair, 0, 0))
    return pl.pallas_call(
        body, name=f"attn_bwd_dkv_g{g}", grid=(PAIRS,),
        in_specs=[col(1), col(0), col(1), col(2), vec, tok, rows, rows],
        out_specs=[tok, tok, pl.BlockSpec((None, 8, LANES), lambda pair: (pair, 0, 0))],
        out_shape=[jax.ShapeDtypeStruct((t, ATT_W), BF16), jax.ShapeDtypeStruct((t, ATT_W), BF16),
                   jax.ShapeDtypeStruct((PAIRS, 8, LANES), F32)],
        scratch_shapes=[pltpu.VMEM((2, 2 * bq, 2 * bq), F32), pltpu.VMEM((t, LANES), F32),
                        pltpu.VMEM((t, LANES), F32)],
        compiler_params=_params("parallel"),
    )(qkv, qkn, qkn, qkn, gk, do, l_row, dl_row)


def _rows_by_residue(one_per_head, dil):
    t = one_per_head.shape[0]
    per_head = one_per_head[:, :ATT_HEADS]
    rows = per_head.reshape(t // dil, dil, ATT_HEADS).transpose(2, 1, 0).reshape(PAIRS, 2, t)
    return jnp.pad(rows, ((0, 0), (0, 6), (0, 0)))


def _per_head(rep_row):
    return rep_row[0, ::SSM_HEAD_DIM]


def _rep_heads(v):
    return jnp.repeat(v, SSM_HEAD_DIM)[None, :]


def _pad_lanes(v):
    return jnp.pad(v, ((0, 0), (0, LANES - v.shape[1])))


class _NoOverlap:
    def side(self, host):
        return None

    def after(self, host):
        pass

    def begin_backward(self, grads):
        pass


def _hosted(plan, host, fn, *args, **kwargs):
    out = fn(*args, side=plan.side(host), **kwargs)
    plan.after(host)
    return out


def _ffn_ple_fwd(x1, h, p_i, prm, i, plan, next_gain=None, target=None):
    g, u, act = _hosted(plan, f"swiglu_fwd_{i}", _swiglu_fwd, h, prm["ffn_w_gate"][i], prm["ffn_w_up"][i],
                        name=f"swiglu_fwd_{i}")
    x2 = _hosted(plan, f"ffn_down_{i}", _matmul, act, prm["ffn_w_down"][i], mode="nn", addend=x1,
                 name=f"ffn_down_{i}")
    outs = _ple_fwd(x2, p_i, prm["ple_w_gate"][i], prm["ple_w_proj"][i], name=f"ple_fwd_{i}", next_gain=next_gain,
                    target=target)
    return outs, dict(x1=x1, h=h, g=g, u=u, act=act, x2=x2)


def _ffn_ple_bwd(dx3, p_i, prm, i, sv, grads, plan):
    ds, dple, dx2 = _ple_bwd(sv["x2"], p_i, prm["ple_w_gate"][i], prm["ple_w_proj"][i], dx3, name=f"ple_bwd_{i}")
    grads["ple_w_gate"][i] = _matmul_tn(sv["x2"], ds, name=f"d_ple_w_gate_{i}")
    grads["ple_w_proj"][i] = _matmul_tn(dple, p_i, name=f"d_ple_w_proj_{i}")
    grads["ffn_w_down"][i] = _matmul_tn(sv["act"], dx2, name=f"d_ffn_w_down_{i}")
    dg, du = _hosted(plan, f"swiglu_bwd_{i}", _swiglu_bwd, dx2, prm["ffn_w_down"][i], sv["g"], sv["u"],
                     name=f"swiglu_bwd_{i}")
    grads["ffn_w_gate"][i] = _matmul_tn(dg, sv["h"], name=f"d_ffn_w_gate_{i}")
    grads["ffn_w_up"][i] = _matmul_tn(du, sv["h"], name=f"d_ffn_w_up_{i}")
    dx1, dgain = _matmul_rmsnorm_bwd(dg, prm["ffn_w_gate"][i], None, sv["x1"], prm["norm_ffn"][i:i + 1], dx2,
                                     name=f"ffn_dh_{i}", tm=256, more=(du, prm["ffn_w_up"][i]))
    grads["norm_ffn"][i] = dgain[0]
    return dx1


def _mamba_fwd(x0, prm, plan):
    h = _rmsnorm_fwd(x0, prm["norm_mix"][0:1], name="mix_norm_fwd_0")
    z = _hosted(plan, "ssm_in_z", _matmul, h, prm["ssm_w_z"], mode="nt", name="ssm_in_z")
    xbc_pre = _hosted(plan, "ssm_in_xbc", _matmul, h, prm["ssm_w_xbc"], mode="nt", name="ssm_in_xbc")
    dt_raw = _matmul(h, prm["ssm_w_dt"], mode="nt", name="ssm_in_dt")
    xbc = _hosted(plan, "conv_fwd", _conv_fwd, xbc_pre, prm["ssm_conv_w"], prm["ssm_conv_b"])
    dt_bias = _pad_lanes(prm["ssm_dt_bias"])
    a_log = _pad_lanes(prm["ssm_a_log"])
    acs, dt_rep, acs_rep = _ssd_prep_fwd(dt_raw, dt_bias, a_log)
    acs_t = acs[:, :SSM_HEADS].T
    dskip_rep = _rep_heads(prm["ssm_d_skip"][0])
    y, hin_all, yn = _hosted(plan, "ssd_fwd", _ssd_fwd, xbc, dt_rep, acs_rep, acs_t, dskip_rep, z,
                             prm["ssm_norm_w"])
    x1, h_ffn = _matmul(yn, prm["ssm_w_out"], mode="nn", addend=x0, name="ssm_out", tm=512, tn=D_MODEL,
                        second=(_rmsnorm_rows, [prm["norm_ffn"][0:1]], BF16))
    sv = dict(x0=x0, h=h, z=z, xbc_pre=xbc_pre, dt_raw=dt_raw, xbc=xbc, dt_bias=dt_bias, dt_rep=dt_rep,
              acs_rep=acs_rep, acs_t=acs_t, dskip_rep=dskip_rep, y=y, hin_all=hin_all, yn=yn)
    return x1, h_ffn, sv


def _mamba_bwd(dx1, prm, sv, grads, plan):
    grads["ssm_w_out"] = _matmul_tn(sv["yn"], dx1, name="d_ssm_w_out")
    dy, dz, dnw = _hosted(plan, "gate_norm_bwd", _gate_norm_bwd, sv["y"], sv["z"], prm["ssm_norm_w"], dx1,
                          prm["ssm_w_out"])
    grads["ssm_norm_w"] = dnw
    a_rep = _rep_heads(-jnp.exp(prm["ssm_a_log"][0]))
    dxbc, ddt, da_rep, dds_rep = _hosted(plan, "ssd_bwd", _ssd_bwd, sv["xbc"], sv["dt_rep"], sv["acs_rep"],
                                             sv["acs_t"], sv["dskip_rep"], a_rep, sv["hin_all"], dy)
    grads["ssm_d_skip"] = _per_head(dds_rep)[None, :]
    grads["ssm_a_log"] = (_per_head(da_rep) * _per_head(a_rep))[None, :]
    ddt_raw, dbias = _ssd_prep_bwd(sv["dt_raw"], sv["dt_bias"], ddt)
    grads["ssm_dt_bias"] = dbias[:, :SSM_HEADS]
    du, dcw, dcb = _hosted(plan, "conv_bwd", _conv_bwd, sv["xbc_pre"], prm["ssm_conv_w"], prm["ssm_conv_b"], dxbc)
    grads["ssm_conv_w"] = dcw
    grads["ssm_conv_b"] = dcb
    h = sv["h"]
    grads["ssm_w_in"] = jnp.concatenate(
        [_matmul_tn(dz, h, name="d_ssm_w_z"), _matmul_tn(du, h, name="d_ssm_w_xbc"),
         _matmul_tn(ddt_raw, h, name="d_ssm_w_dt")[:SSM_HEADS]], axis=0)
    dh = _hosted(plan, "ssm_dh_z", _matmul, dz, prm["ssm_w_z"], mode="nn", name="ssm_dh_z")
    dh = _hosted(plan, "ssm_dh_xbc", _matmul, du, prm["ssm_w_xbc"], mode="nn", addend=dh, name="ssm_dh_xbc")
    dx0, dgain = _hosted(plan, "ssm_dh_dt", _matmul_rmsnorm_bwd, ddt_raw, prm["ssm_w_dt"], dh, sv["x0"],
                         prm["norm_mix"][0:1], dx1, name="ssm_dh_dt")
    grads["norm_mix"][0] = dgain[0]
    return dx0


def _attn_mixer_fwd(x0, h, prm, plan):
    n_heads = N_QKV_BLOCKS * ATT_HEADS
    gq = jnp.tile(prm["att_q_norm"], (1, n_heads))
    gk = jnp.tile(prm["att_k_norm"], (1, n_heads))
    qkv, qkn = _hosted(plan, "att_qkv", _matmul, h, prm["att_w_qkv"], mode="nt", name="att_qkv",
                       second=(_qk_normalised, [gq, gk], F32))
    outs, lses = [], []
    for g, (window, dil) in enumerate(DIL_PATTERNS):
        o_g, l_g = _attn_fwd(qkn, g, dil)
        outs.append(o_g)
        lses.append(l_g)
    o_f, l_rep, l_one, x1, h_ffn = _attn_out_fwd(outs, lses, prm["att_w_o"], x0, prm["norm_ffn"][1:2])
    sv = dict(x0=x0, h=h, qkv=qkv, qkn=qkn, gq2=gq[:, :LANES], gk2=gk[:, :LANES], o_f=o_f, l_rep=l_rep,
              l_one=l_one)
    return x1, h_ffn, sv


def _attn_mixer_bwd(dx1, prm, sv, grads, plan):
    grads["att_w_o"] = _matmul_tn(sv["o_f"], dx1, name="d_att_w_o")
    do, dl_rep, dl_one = _hosted(plan, "att_out_dx", _attn_out_bwd, dx1, prm["att_w_o"], sv["o_f"])
    blocks, dgq, dgk = [], [], []
    for g, (window, dil) in enumerate(DIL_PATTERNS):
        dq, dgq_g = _attn_bwd_dq(sv["qkv"], sv["qkn"], sv["gq2"], do, sv["l_rep"], dl_rep, g, dil)
        dk, dv, dgk_g = _attn_bwd_dkv(sv["qkv"], sv["qkn"], sv["gk2"], do, _rows_by_residue(sv["l_one"], dil),
                                      _rows_by_residue(dl_one, dil), g, dil)
        blocks += [dq, dk, dv]
        dgq.append(dgq_g)
        dgk.append(dgk_g)
    dqkv = jnp.concatenate(blocks, axis=1)

    def fold(parts):
        return jnp.stack(parts)[:, :, 0].reshape(-1, ATT_HEAD_DIM).sum(axis=0)[None, :]

    grads["att_q_norm"] = fold(dgq)
    grads["att_k_norm"] = fold(dgk)
    grads["att_w_qkv"] = _matmul_tn(dqkv, sv["h"], name="d_att_w_qkv")
    dx0, dgain = _hosted(plan, "att_qkv_dx", _matmul_rmsnorm_bwd, dqkv, prm["att_w_qkv"], None, sv["x0"],
                         prm["norm_mix"][1:2], dx1, name="att_qkv_dx")
    grads["norm_mix"][1] = dgain[0]
    return dx0


def _local_step(x, p, target, prm, plan=None):
    plan = plan or _NoOverlap()
    grads = {k: [None, None] for k in ("norm_mix", "norm_ffn", "ffn_w_gate", "ffn_w_up", "ffn_w_down",
                                       "ple_w_proj", "ple_w_gate")}
    plan.begin_backward(grads)
    x1, h1, sv_m = _mamba_fwd(x, prm, plan)
    (x3, h3), sv_f0 = _ffn_ple_fwd(x1, h1, p[0], prm, 0, plan, next_gain=prm["norm_mix"][1:2])
    x4, h4, sv_a = _attn_mixer_fwd(x3, h3, prm, plan)
    (dy, loss_row), sv_f1 = _ffn_ple_fwd(x4, h4, p[1], prm, 1, plan, target=target)
    dx4 = _ffn_ple_bwd(dy, p[1], prm, 1, sv_f1, grads, plan)
    dx3 = _attn_mixer_bwd(dx4, prm, sv_a, grads, plan)
    dx1 = _ffn_ple_bwd(dx3, p[0], prm, 0, sv_f0, grads, plan)
    dx0 = _mamba_bwd(dx1, prm, sv_m, grads, plan)
    return loss_row, dx0, grads


W_IN_SLAB_ROWS = 1312


def _position():
    return lax.axis_index("x"), lax.axis_index("y"), lax.axis_index("c")


def _other_chips(x, y):
    return [(1 - x, y), (x, 1 - y), (1 - x, 1 - y)]


def _remote(send_sems, recv_sems, k, src, dst, to):
    return pltpu.make_async_remote_copy(src_ref=src, dst_ref=dst, send_sem=send_sems.at[k], recv_sem=recv_sems.at[k],
                                        device_id=to, device_id_type=MESH)


def _gather_side(entries, whole=()):
    n, nw = len(entries), len(whole)

    def first_hop(ins, outs, send_sems, recv_sems):
        x, y, c = _position()
        cps = []
        for j, chip in enumerate(_other_chips(x, y)):
            for e in range(n):
                cps.append(_remote(send_sems, recv_sems, 6 * e + j, ins[e].at[c], outs[e].at[2 * x + y, c], (*chip, c)))
            for e in range(nw):
                cps.append(_remote(send_sems, recv_sems, 6 * n + 3 * e + j, ins[n + e], outs[n + e].at[2 * x + y],
                                   (*chip, c)))
        return cps

    def start(ins, outs, send_sems, recv_sems):
        for cp in first_hop(ins, outs, send_sems, recv_sems):
            cp.start()

    def finish(ins, outs, send_sems, recv_sems):
        x, y, c = _position()
        me, sibling = (x, y, c), (x, y, 1 - c)
        chips = _other_chips(x, y)
        passed_on = []
        for j, (px, py) in enumerate(chips):
            for e in range(n):
                landed = outs[e].at[2 * px + py, c]
                _remote(send_sems, recv_sems, 6 * e + j, landed, landed, me).wait_recv()
                passed_on.append(_remote(send_sems, recv_sems, 6 * e + 3 + j, landed, landed, sibling))
                passed_on[-1].start()
            for e in range(nw):
                landed = outs[n + e].at[2 * px + py]
                _remote(send_sems, recv_sems, 6 * n + 3 * e + j, landed, landed, me).wait_recv()
        for j, (px, py) in enumerate(chips):
            for e in range(n):
                passed = outs[e].at[2 * px + py, 1 - c]
                _remote(send_sems, recv_sems, 6 * e + 3 + j, passed, passed, me).wait_recv()
        for cp in first_hop(ins, outs, send_sems, recv_sems) + passed_on:
            cp.wait_send()

    shapes = [jax.ShapeDtypeStruct((N_CHIPS,) + a.shape, a.dtype) for a in list(entries) + list(whole)]
    return _Side(list(entries) + list(whole), shapes, 6 * n + 3 * nw, start, finish)


def _run_side(side, name):
    si, so = len(side.inputs), len(side.out_shapes)

    def body(*refs):
        ins, outs, send_sems, recv_sems = refs[:si], refs[si:si + so], refs[-2], refs[-1]
        side.start(ins, outs, send_sems, recv_sems)
        side.finish(ins, outs, send_sems, recv_sems)

    side.outputs = list(pl.pallas_call(
        body, name=name, in_specs=[ANY] * si, out_specs=[ANY] * so, out_shape=side.out_shapes,
        scratch_shapes=[pltpu.SemaphoreType.DMA((side.n_sems,)), pltpu.SemaphoreType.DMA((side.n_sems,))],
    )(*side.inputs))
    return side.outputs


def _swap_side(grads):
    n = len(grads)

    def copies(ins, outs, send_sems, recv_sems):
        x, y, c = _position()
        return [_remote(send_sems, recv_sems, e, ins[e].at[:, 1 - c], outs[e], (x, y, 1 - c)) for e in range(n)]

    def start(ins, outs, send_sems, recv_sems):
        for cp in copies(ins, outs, send_sems, recv_sems):
            cp.start()

    def finish(ins, outs, send_sems, recv_sems):
        for cp in copies(ins, outs, send_sems, recv_sems):
            cp.wait()

    shapes = [jax.ShapeDtypeStruct((N_CHIPS,) + g.shape[2:], g.dtype) for g in grads]
    return _Side(grads, shapes, n, start, finish)


def _chip_exchange_side(chipsums):
    n = len(chipsums)

    def copies(ins, outs, send_sems, recv_sems):
        x, y, c = _position()
        return [_remote(send_sems, recv_sems, 3 * e + j, ins[e].at[2 * tx + ty], outs[e].at[j], (tx, ty, c))
                for j, (tx, ty) in enumerate(_other_chips(x, y)) for e in range(n)]

    def start(ins, outs, send_sems, recv_sems):
        for cp in copies(ins, outs, send_sems, recv_sems):
            cp.start()

    def finish(ins, outs, send_sems, recv_sems):
        for cp in copies(ins, outs, send_sems, recv_sems):
            cp.wait()

    shapes = [jax.ShapeDtypeStruct((3,) + cs.shape[1:], cs.dtype) for cs in chipsums]
    return _Side(chipsums, shapes, 3 * n, start, finish)


def _share_side(totals):
    n = len(totals)

    def copies(ins, outs, send_sems, recv_sems):
        x, y, c = _position()
        return [_remote(send_sems, recv_sems, e, ins[e], outs[e], (x, y, 1 - c)) for e in range(n)]

    def start(ins, outs, send_sems, recv_sems):
        for cp in copies(ins, outs, send_sems, recv_sems):
            cp.start()

    def finish(ins, outs, send_sems, recv_sems):
        for cp in copies(ins, outs, send_sems, recv_sems):
            cp.wait()

    return _Side(totals, [jax.ShapeDtypeStruct(t.shape, t.dtype) for t in totals], n, start, finish)


def _reduce_rows(h):
    return h if h <= 704 else h // 2


def _add_sibling(grad, recv, c_idx, *, name):
    _, _, h, cw = grad.shape
    th = _reduce_rows(h)

    def body(c_ref, g_ref, r_ref, o_ref):
        o_ref[...] = (g_ref[...] + r_ref[...]).astype(BF16)

    return pl.pallas_call(
        body, name=name,
        grid_spec=pltpu.PrefetchScalarGridSpec(
            num_scalar_prefetch=1, grid=(N_CHIPS, h // th),
            in_specs=[pl.BlockSpec((None, None, th, cw), lambda s, i, c_ref: (s, c_ref[0], i, 0)),
                      pl.BlockSpec((None, th, cw), lambda s, i, c_ref: (s, i, 0))],
            out_specs=pl.BlockSpec((None, th, cw), lambda s, i, c_ref: (s, i, 0))),
        out_shape=jax.ShapeDtypeStruct((N_CHIPS, h, cw), BF16),
        compiler_params=_params("parallel", "parallel"),
    )(c_idx, grad, recv)


def _add_chips(chipsum, recv, s_idx, *, name):
    _, h, cw = chipsum.shape
    th = _reduce_rows(h)

    def body(s_ref, own_ref, r_ref, o_ref):
        o_ref[...] = ((own_ref[...].astype(F32) + r_ref[0].astype(F32)) + r_ref[1].astype(F32)) + r_ref[2].astype(F32)

    return pl.pallas_call(
        body, name=name,
        grid_spec=pltpu.PrefetchScalarGridSpec(
            num_scalar_prefetch=1, grid=(h // th,),
            in_specs=[pl.BlockSpec((None, th, cw), lambda i, s_ref: (s_ref[0], i, 0)),
                      pl.BlockSpec((3, th, cw), lambda i, s_ref: (0, i, 0))],
            out_specs=pl.BlockSpec((th, cw), lambda i, s_ref: (i, 0))),
        out_shape=jax.ShapeDtypeStruct((h, cw), F32),
        compiler_params=_params("parallel"),
    )(s_idx, chipsum, recv)


def _adamw_math(w, g, m, v):
    m = ADAM_B1 * m + (1.0 - ADAM_B1) * g
    v = ADAM_B2 * v + (1.0 - ADAM_B2) * (g * g)
    m_hat = m / (1.0 - ADAM_B1 ** ADAM_STEP)
    v_hat = v / (1.0 - ADAM_B2 ** ADAM_STEP)
    delta = -ADAM_LR * (m_hat / (jnp.sqrt(v_hat) + ADAM_EPS) + ADAM_WD * w)
    return delta, m, v


ADAM_TILE_ELEMS = 256 * 1024


def _adamw(w, g, m, v, *, name):
    layers, rows, cols = w.shape
    tr = rows
    for cand in range(8, rows, 8):
        if rows % cand == 0 and cand * cols <= ADAM_TILE_ELEMS:
            tr = cand
    if rows * cols <= ADAM_TILE_ELEMS:
        tr = rows

    def body(w_ref, g_ref, m_ref, v_ref, d_ref, nm_ref, nv_ref):
        d, nm, nv = _adamw_math(w_ref[...], g_ref[...], m_ref[...], v_ref[...])
        d_ref[...] = d
        nm_ref[...] = nm
        nv_ref[...] = nv

    blk = pl.BlockSpec((None, tr, cols), lambda l, i: (l, i, 0))
    sds = jax.ShapeDtypeStruct(w.shape, F32)
    return pl.pallas_call(
        body, name=name, grid=(layers, rows // tr), in_specs=[blk] * 4, out_specs=[blk] * 3, out_shape=[sds] * 3,
        compiler_params=_params("parallel", "parallel"),
    )(w, g, m, v)


SMALL_LAYOUT = (("loss", 1), ("norm_mix", 16), ("norm_ffn", 16), ("ssm_conv_b", 24), ("ssm_dt_bias", 1),
                ("ssm_a_log", 1), ("ssm_d_skip", 1), ("ssm_norm_w", 16), ("att_q_norm", 1), ("att_k_norm", 1),
                ("conv_w_full", 96))
SMALL_ROWS = 176
N_DEVICES = 8


def _small_packs(dicts):
    parts = []
    for values in dicts:
        for name, rows in SMALL_LAYOUT:
            flat = values[name].reshape(-1).astype(F32)
            parts.append(jnp.pad(flat, (0, rows * LANES - flat.shape[0])).reshape(rows, LANES))
        used = sum(r for _, r in SMALL_LAYOUT)
        parts.append(jnp.zeros((SMALL_ROWS - used, LANES), F32))
    return jnp.concatenate(parts, axis=0).reshape(len(dicts), SMALL_ROWS, LANES)


def _small_unpack(pack, shapes):
    out, off = {}, 0
    for name, rows in SMALL_LAYOUT:
        shape = shapes[name]
        n = math.prod(shape)
        out[name] = pack[off:off + rows].reshape(-1)[:n].reshape(shape)
        off += rows
    return out


def _small_allreduce_adamw(g, w, m, v):
    def body(g_ref, w_ref, m_ref, v_ref, gs_ref, d_ref, nm_ref, nv_ref, buf, send_sems, recv_sems):
        x, y, c = _position()
        pos = (x, y, c)
        me = 4 * x + 2 * y + c
        buf[me] = g_ref[...]
        peers = []
        for k in range(1, N_DEVICES):
            bits = ((k >> 2) & 1, (k >> 1) & 1, k & 1)
            peers.append(tuple(1 - p if b else p for p, b in zip(pos, bits)))
        cps = [pltpu.make_async_remote_copy(src_ref=g_ref, dst_ref=buf.at[me], send_sem=send_sems.at[k],
                                            recv_sem=recv_sems.at[k], device_id=peer, device_id_type=MESH)
               for k, peer in enumerate(peers)]
        for cp in cps:
            cp.start()
        for k, (px, py, pc) in enumerate(peers):
            pltpu.make_async_remote_copy(src_ref=g_ref, dst_ref=buf.at[4 * px + 2 * py + pc],
                                         send_sem=send_sems.at[k], recv_sem=recv_sems.at[k],
                                         device_id=(px, py, pc), device_id_type=MESH).wait_recv()
        for cp in cps:
            cp.wait_send()
        total = buf[0]
        for dev in range(1, N_DEVICES):
            total = total + buf[dev]
        gs_ref[...] = total
        d, nm, nv = _adamw_math(w_ref[...], total, m_ref[...], v_ref[...])
        d_ref[...] = d
        nm_ref[...] = nm
        nv_ref[...] = nv

    vm = pl.BlockSpec(memory_space=pltpu.VMEM)
    sds = jax.ShapeDtypeStruct((SMALL_ROWS, LANES), F32)
    return pl.pallas_call(
        body, name="small_allreduce_adamw", in_specs=[vm] * 4, out_specs=[vm] * 4, out_shape=[sds] * 4,
        scratch_shapes=[pltpu.VMEM((N_DEVICES, SMALL_ROWS, LANES), F32),
                        pltpu.SemaphoreType.DMA((N_DEVICES - 1,)), pltpu.SemaphoreType.DMA((N_DEVICES - 1,))],
    )(g, w, m, v)


SMALL = tuple(n for n, _ in SMALL_LAYOUT if n not in ("loss", "conv_w_full"))
WEIGHTS = ("norm_mix", "norm_ffn", "ssm_w_in", "ssm_conv_w", "ssm_conv_b", "ssm_dt_bias", "ssm_a_log", "ssm_d_skip",
           "ssm_norm_w", "ssm_w_out", "att_w_qkv", "att_q_norm", "att_k_norm", "att_w_o", "ffn_w_gate", "ffn_w_up",
           "ffn_w_down", "ple_w_proj", "ple_w_gate")
COLUMN_SHARDED = ("ssm_w_in", "att_w_qkv", "ffn_w_gate", "ffn_w_up", "ple_w_proj")
LAYERED = ("ffn_w_gate", "ffn_w_up", "ffn_w_down", "ple_w_proj", "ple_w_gate")
UPDATED_TRANSPOSED = ("ssm_w_in", "ffn_w_gate", "ffn_w_up")
GATHER_ORDER = ("ssm_w_in", "ssm_w_out", "att_w_qkv", "att_w_o", "ffn_w_gate", "ffn_w_up", "ffn_w_down",
                "ple_w_proj", "ple_w_gate")


def _layers(n):
    return (0, 1) if n in LAYERED else (None,)


def _tag(key):
    return key[0] if key[1] is None else f"{key[0]}_{key[1]}"


QKV_PARTS = 3


def _weight_slab(w, key):
    n, i = key
    if n == "att_w_qkv":
        a = w[n][0].T
        rows = a.shape[0] // QKV_PARTS
        a = a[i * rows:(i + 1) * rows]
    else:
        a = w[n][0 if i is None else i]
        a = a.T if n in COLUMN_SHARDED else a
    if n == "ssm_w_in":
        a = jnp.pad(a, ((0, W_IN_SLAB_ROWS - a.shape[0]), (0, 0)))
    return a.reshape(2, a.shape[0] // 2, a.shape[1]).astype(BF16)


def _install(prm, key, gathered, own, s_me):
    n, i = key
    full = lax.dynamic_update_slice(gathered, own[None], (s_me, 0, 0, 0))
    full = full.reshape(N_CHIPS, 2 * full.shape[2], full.shape[3])
    if n == "att_w_qkv":
        parts = prm.setdefault("att_w_qkv_parts", {})
        parts[i] = full
        if len(parts) == QKV_PARTS:
            prm[n] = jnp.stack([parts[j] for j in range(QKV_PARTS)], axis=1).reshape(-1, D_MODEL)
        return
    if n == "ssm_w_in":
        rows = (D_INNER + CONV_DIM + SSM_HEADS) // N_CHIPS
        w_in_t = full[:, :rows].reshape(N_CHIPS * rows, D_MODEL)
        prm["ssm_w_z"] = w_in_t[:D_INNER]
        prm["ssm_w_xbc"] = w_in_t[D_INNER:D_INNER + CONV_DIM]
        prm["ssm_w_dt"] = jnp.pad(w_in_t[D_INNER + CONV_DIM:], ((0, LANES - SSM_HEADS), (0, 0)))
        return
    full = full.reshape(N_CHIPS * full.shape[1], full.shape[2])
    if i is None:
        prm[n] = full
    else:
        prm.setdefault(n, [None, None])[i] = full


def _grad_slab(grads, key):
    n, i = key
    g = grads[n] if i is None else grads[n][i]
    if n == "ssm_w_in":
        g = jnp.pad(g.reshape(N_CHIPS, g.shape[0] // N_CHIPS, D_MODEL),
                    ((0, 0), (0, W_IN_SLAB_ROWS - g.shape[0] // N_CHIPS), (0, 0)))
    rows = g.size // (N_CHIPS * g.shape[-1])
    return g.reshape(N_CHIPS, 2, rows // 2, g.shape[-1])


def _natural_shard(n, reduced, shape):
    def one(r):
        if n == "ssm_w_in":
            r = r[:shape[-1]]
        return r.T if n in COLUMN_SHARDED else r
    if n in LAYERED:
        return jnp.stack([one(r) for r in reduced]).reshape(shape)
    return one(reduced[0]).reshape(shape)


def kernel(x, p, norm_mix, norm_ffn, ssm_w_in, ssm_conv_w, ssm_conv_b, ssm_dt_bias, ssm_a_log, ssm_d_skip, ssm_norm_w, ssm_w_out, att_w_qkv, att_q_norm, att_k_norm, att_w_o, ffn_w_gate, ffn_w_up, ffn_w_down, ple_w_proj, ple_w_gate, loss_target, m_norm_mix, m_norm_ffn, m_ssm_w_in, m_ssm_conv_w, m_ssm_conv_b, m_ssm_dt_bias, m_ssm_a_log, m_ssm_d_skip, m_ssm_norm_w, m_ssm_w_out, m_att_w_qkv, m_att_q_norm, m_att_k_norm, m_att_w_o, m_ffn_w_gate, m_ffn_w_up, m_ffn_w_down, m_ple_w_proj, m_ple_w_gate, v_norm_mix, v_norm_ffn, v_ssm_w_in, v_ssm_conv_w, v_ssm_conv_b, v_ssm_dt_bias, v_ssm_a_log, v_ssm_d_skip, v_ssm_norm_w, v_ssm_w_out, v_att_w_qkv, v_att_q_norm, v_att_k_norm, v_att_w_o, v_ffn_w_gate, v_ffn_w_up, v_ffn_w_down, v_ple_w_proj, v_ple_w_gate):
    given = dict(locals())
    w = {n: given[n] for n in WEIGHTS}
    m = {n: given["m_" + n] for n in WEIGHTS}
    v = {n: given["v_" + n] for n in WEIGHTS}
    c_idx = lax.axis_index("c").astype(jnp.int32).reshape(1)
    s_idx = (2 * lax.axis_index("x") + lax.axis_index("y")).astype(jnp.int32).reshape(1)

    s_me = 2 * lax.axis_index("x") + lax.axis_index("y")
    first_core = lax.axis_index("c") == 0

    qkv_parts = [("att_w_qkv", j) for j in range(QKV_PARTS)]
    gather_plan = {
        "ssm_in_z": [("ssm_w_out", None)],
        "ssm_in_xbc": [("ffn_w_gate", 0)],
        "conv_fwd": [("ffn_w_up", 0)],
        "ssd_fwd": [("ffn_w_down", 0), ("ple_w_proj", 0), ("ple_w_gate", 0), ("att_w_o", None)],
        "swiglu_fwd_0": qkv_parts[:2],
        "ffn_down_0": qkv_parts[2:],
        "att_qkv": [(n, 1) for n in LAYERED],
    }
    mamba = [("ssm_w_in", None)]
    own = {k: _weight_slab(w, k) for k in mamba + sum(gather_plan.values(), [])}
    prm = {n: w[n] for n in SMALL}

    def land(group, outputs):
        for k, g in zip(group, outputs):
            _install(prm, k, g, own[k], s_me)

    first = _gather_side([own[k] for k in mamba], whole=[ssm_conv_w[0]])
    _run_side(first, "gather_mamba")
    land(mamba, first.outputs)
    conv = lax.dynamic_update_slice(first.outputs[-1], ssm_conv_w, (s_me, 0, 0))
    prm["ssm_conv_w"] = conv.transpose(1, 0, 2).reshape(CONV_WIDTH, CONV_DIM)

    ffn1 = [(n, 1) for n in LAYERED]
    attention = [("att_w_qkv", None), ("att_w_o", None)]
    ffn0 = [(n, 0) for n in LAYERED] + [("ssm_w_out", None)]
    reduce_plan = {"att_out_dx": [("swap", ffn1)], "att_qkv_dx": [("exchange", ffn1)],
                   "swiglu_bwd_0": [("swap", attention)], "gate_norm_bwd": [("swap", ffn0)],
                   "ssd_bwd": [("exchange", attention), ("exchange", ffn0)],
                   "ssm_dh_z": [("swap", mamba)], "ssm_dh_xbc": [("exchange", mamba)]}
    state = {}

    def swap_side(group):
        state[_tag(group[0]), "g4"] = g4 = [_grad_slab(state["grads"], k) for k in group]
        return _swap_side(g4)

    def add_siblings(group, from_sibling):
        state[_tag(group[0]), "chipsums"] = [
            _add_sibling(g, r, c_idx, name="add_sibling_" + _tag(k))
            for g, r, k in zip(state[_tag(group[0]), "g4"], from_sibling, group)]

    def exchange_side(group):
        return _chip_exchange_side(state[_tag(group[0]), "chipsums"])

    def add_chips(group, from_chips):
        for k, cs, r in zip(group, state[_tag(group[0]), "chipsums"], from_chips):
            state["total", k] = _add_chips(cs, r, s_idx, name="add_chips_" + _tag(k))

    class Plan(_NoOverlap):
        def __init__(self):
            self.carried = {host: _gather_side([own[k] for k in group]) for host, group in gather_plan.items()}

        def begin_backward(self, grads):
            state["grads"] = grads

        def side(self, host):
            if host in reduce_plan:
                self.parts = [swap_side(group) if step == "swap" else exchange_side(group)
                              for step, group in reduce_plan[host]]
                self.carried[host] = _sides_together(self.parts)
            elif host == share_host:
                self.carried[host] = _share_side([state["total", k] for k in order])
            return self.carried.get(host)

        def after(self, host):
            if host in gather_plan:
                land(gather_plan[host], self.carried[host].outputs)
            elif host in reduce_plan:
                _share_out(self.carried[host], self.parts)
                for (step, group), part in zip(reduce_plan[host], self.parts):
                    (add_siblings if step == "swap" else add_chips)(group, part.outputs)
            elif host == share_host:
                state["shared"] = self.carried[host].outputs

    order = mamba + ffn0 + attention + ffn1
    share_host = "ssm_dh_dt"
    loss_row, dx, grads = _local_step(x[0], p[:, 0], loss_target[0], prm, Plan())

    reduced = {}
    for k, theirs in zip(order, state["shared"]):
        lo = jnp.where(first_core, state["total", k], theirs)
        hi = jnp.where(first_core, theirs, state["total", k])
        reduced.setdefault(k[0], {})[k[1]] = jnp.concatenate([lo, hi], axis=0)
    reduced = {n: [by_layer[i] for i in _layers(n)] for n, by_layer in reduced.items()}

    grad, delta, new_m, new_v = {}, {}, {}, {}
    for n in GATHER_ORDER:
        if n in UPDATED_TRANSPOSED:
            flip = lambda a: a.transpose(0, 2, 1)
            cols = w[n].shape[-1]
            g_t = jnp.stack([r[:cols] for r in reduced[n]])
            grad[n] = flip(g_t)
            delta[n], new_m[n], new_v[n] = [flip(o) for o in _adamw(flip(w[n]), g_t, flip(m[n]), flip(v[n]),
                                                                    name="adamw_" + n)]
            continue
        grad[n] = _natural_shard(n, reduced[n], w[n].shape)
        delta[n], new_m[n], new_v[n] = _adamw(w[n], grad[n], m[n], v[n], name="adamw_" + n)

    small_g = {n: (jnp.stack(grads[n]) if isinstance(grads[n], list) else grads[n]) for n in SMALL}
    small_g["loss"] = loss_row
    small_g["conv_w_full"] = grads["ssm_conv_w"]
    zero = {"loss": jnp.zeros((1, LANES), F32), "conv_w_full": jnp.zeros((CONV_WIDTH, CONV_DIM), F32)}
    packs = _small_packs([small_g, {**w, **zero}, {**m, **zero}, {**v, **zero}])
    outs = _small_allreduce_adamw(packs[0], packs[1], packs[2], packs[3])
    shapes = {n: w[n].shape for n in SMALL}
    shapes["loss"] = (1, LANES)
    shapes["conv_w_full"] = (CONV_WIDTH, CONV_DIM)
    sg, sd, sm, sv = [_small_unpack(o, shapes) for o in outs]
    for n in SMALL:
        grad[n], delta[n], new_m[n], new_v[n] = sg[n], sd[n], sm[n], sv[n]
    loss = sg["loss"][0, 0]
    conv_cols = CONV_DIM // N_CHIPS
    grad["ssm_conv_w"] = lax.dynamic_slice(sg["conv_w_full"], (0, s_me * conv_cols), (CONV_WIDTH, conv_cols))[None]
    delta["ssm_conv_w"], new_m["ssm_conv_w"], new_v["ssm_conv_w"] = _adamw(
        ssm_conv_w, grad["ssm_conv_w"], m_ssm_conv_w, v_ssm_conv_w, name="adamw_ssm_conv_w")

    return (loss, dx[None], *[grad[n] for n in WEIGHTS], *[delta[n] for n in WEIGHTS],
            *[new_m[n] for n in WEIGHTS], *[new_v[n] for n in WEIGHTS])
```

```python
import math

import jax
import jax.numpy as jnp
from jax import lax
from jax.experimental import pallas as pl
from jax.experimental.pallas import tpu as pltpu

F32 = jnp.float32
BF16 = jnp.bfloat16
HIGHEST = lax.Precision.HIGHEST

NORM_EPS = 1e-6
ADAM_LR, ADAM_B1, ADAM_B2, ADAM_EPS, ADAM_WD, ADAM_STEP = 0.001, 0.9, 0.999, 1e-08, 0.01, 10

D_MODEL = 1024
D_INNER = 2048
SSM_HEADS = 32
SSM_HEAD_DIM = 64
SSM_GROUPS = 4
SSM_STATE = 128
SSD_CHUNK = 128
CONV_DIM = 3072
CONV_WIDTH = 4
ATT_HEADS = 16
ATT_HEAD_DIM = 64
DIL_PATTERNS = ((128, 1), (512, 4), (2048, 16))
ATT_BLOCK = 128
FFN_HIDDEN = 2816
PLE_DIM = 256

LANES = 128
V7X_VMEM_LIMIT = 56 * 1024 * 1024
NEG_BIG = -1e30

N_CHIPS = 4


def _params(*sem):
    return pltpu.CompilerParams(dimension_semantics=sem, vmem_limit_bytes=V7X_VMEM_LIMIT)


def _tile(n, pref):
    if n <= pref:
        return n
    best = None
    for t in range(LANES, pref + 1, LANES):
        if n % t == 0:
            best = t
    assert best is not None, (n, pref)
    return best


def _sigmoid(v):
    return 1.0 / (1.0 + jnp.exp(-v))


def _dot(a, b):
    return jnp.dot(a, b, preferred_element_type=F32)


def _dot_nt(a, b):
    return lax.dot_general(a, b, (((1,), (1,)), ((), ())), preferred_element_type=F32)


def _dot_tn(a, b):
    return lax.dot_general(a, b, (((0,), (0,)), ((), ())), preferred_element_type=F32)


def _head_block_diag():
    i = lax.broadcasted_iota(jnp.int32, (LANES, LANES), 0) // ATT_HEAD_DIM
    j = lax.broadcasted_iota(jnp.int32, (LANES, LANES), 1) // ATT_HEAD_DIM
    return (i == j).astype(BF16)


def _split_dot(ones, z):
    hi = z.astype(BF16)
    lo = (z - hi.astype(F32)).astype(BF16)
    return _dot(ones, hi) + _dot(ones, lo)


def _head_sums(z, bd, terms=2):
    hi = z.astype(BF16)
    lo = (z - hi.astype(F32)).astype(BF16) if terms == 2 else None
    parts = []
    for t in range(z.shape[1] // LANES):
        sl = slice(t * LANES, (t + 1) * LANES)
        part = _dot(hi[:, sl], bd)
        parts.append(part + _dot(lo[:, sl], bd) if terms == 2 else part)
    return parts[0] if len(parts) == 1 else jnp.concatenate(parts, axis=1)


def _lane_lt64(rows):
    return lax.broadcasted_iota(jnp.int32, (rows, LANES), 1) < ATT_HEAD_DIM


MESH = pl.DeviceIdType.MESH
ANY = pl.BlockSpec(memory_space=pl.ANY)


class _Side:
    def __init__(self, inputs, out_shapes, n_sems, start, finish):
        self.inputs, self.out_shapes, self.n_sems = list(inputs), list(out_shapes), n_sems
        self.start, self.finish = start, finish
        self.outputs = None


class _SemaphoresFrom:
    def __init__(self, sems, first):
        self.sems, self.first = sems, first

    @property
    def at(self):
        return self

    def __getitem__(self, k):
        return self.sems.at[self.first + k]


def _sides_together(sides):
    def run(step):
        def both(ins, outs, send_sems, recv_sems):
            i = o = k = 0
            for s in sides:
                ni, no = len(s.inputs), len(s.out_shapes)
                getattr(s, step)(ins[i:i + ni], outs[o:o + no], _SemaphoresFrom(send_sems, k),
                                 _SemaphoresFrom(recv_sems, k))
                i, o, k = i + ni, o + no, k + s.n_sems
        return both

    return _Side(sum([s.inputs for s in sides], []), sum([s.out_shapes for s in sides], []),
                 sum(s.n_sems for s in sides), run("start"), run("finish"))


def _share_out(together, sides):
    o = 0
    for s in sides:
        s.outputs = together.outputs[o:o + len(s.out_shapes)]
        o += len(s.out_shapes)


def _call(body, side, *, name, grid, in_specs, out_specs, out_shape, scratch_shapes, semantics, args):
    in_specs, out_specs, out_shape = list(in_specs), list(out_specs), list(out_shape)
    scratch_shapes = list(scratch_shapes)
    if side is None:
        return pl.pallas_call(body, name=name, grid=grid, in_specs=in_specs, out_specs=out_specs,
                              out_shape=out_shape, scratch_shapes=scratch_shapes,
                              compiler_params=_params(*semantics))(*args)
    ni, no, ns = len(in_specs), len(out_specs), len(scratch_shapes)
    si, so = len(side.inputs), len(side.out_shapes)

    def hosted(*refs):
        ins, s_ins = refs[:ni], refs[ni:ni + si]
        outs, s_outs = refs[ni + si:ni + si + no], refs[ni + si + no:ni + si + no + so]
        scratch = refs[ni + si + no + so:ni + si + no + so + ns]
        send_sems, recv_sems = refs[-2], refs[-1]
        first = pl.program_id(0) == 0
        last = pl.program_id(0) == grid[0] - 1
        for axis in range(1, len(grid)):
            first = jnp.logical_and(first, pl.program_id(axis) == 0)
            last = jnp.logical_and(last, pl.program_id(axis) == grid[axis] - 1)

        @pl.when(first)
        def _():
            side.start(s_ins, s_outs, send_sems, recv_sems)

        body(*ins, *outs, *scratch)

        @pl.when(last)
        def _():
            side.finish(s_ins, s_outs, send_sems, recv_sems)

    res = pl.pallas_call(
        hosted, name=name, grid=grid, in_specs=in_specs + [ANY] * si, out_specs=out_specs + [ANY] * so,
        out_shape=out_shape + side.out_shapes,
        scratch_shapes=scratch_shapes + [pltpu.SemaphoreType.DMA((side.n_sems,)),
                                         pltpu.SemaphoreType.DMA((side.n_sems,))],
        compiler_params=_params(*["arbitrary"] * len(grid)),
    )(*args, *side.inputs)
    side.outputs = list(res[no:])
    return list(res[:no])


def _matmul(a, b, *, mode, name, out_dtype=F32, addend=None, tm=1024, tn=512, tk_max=3072, side=None, second=None):
    m, k = a.shape
    if mode == "nn":
        k2, n = b.shape
    else:
        n, k2 = b.shape
    assert k == k2, (a.shape, b.shape, mode)
    tm, tn, tk = _tile(m, tm), _tile(n, tn), _tile(k, tk_max)
    nk = k // tk
    has_add = addend is not None
    n_rows = len(second[1]) if second else 0
    n_out = 2 if second else 1

    def body(*refs):
        a_ref, b_ref = refs[0], refs[1]
        add_ref = refs[2] if has_add else None
        row_refs = refs[2 + has_add:2 + has_add + n_rows]
        o_ref, acc_ref = refs[-1 - n_out], refs[-1]
        kk = pl.program_id(2)
        col_tile = pl.program_id(1)
        av = a_ref[...].astype(BF16)
        bv = b_ref[...].astype(BF16)
        part = _dot(av, bv) if mode == "nn" else _dot_nt(av, bv)

        @pl.when(kk == 0)
        def _():
            acc_ref[...] = part

        @pl.when(kk > 0)
        def _():
            acc_ref[...] += part

        @pl.when(kk == nk - 1)
        def _():
            res = acc_ref[...]
            if has_add:
                res = res + add_ref[...]
            o_ref[...] = res.astype(out_dtype)
            if second:
                refs[-2][...] = second[0](res, col_tile, *row_refs).astype(second[2])

    a_spec = pl.BlockSpec((tm, tk), lambda i, j, kk: (i, kk))
    if mode == "nn":
        b_spec = pl.BlockSpec((tk, tn), lambda i, j, kk: (kk, j))
    else:
        b_spec = pl.BlockSpec((tn, tk), lambda i, j, kk: (j, kk))
    tile = pl.BlockSpec((tm, tn), lambda i, j, kk: (i, j))
    in_specs = [a_spec, b_spec]
    args = [a, b]
    if has_add:
        in_specs.append(tile)
        args.append(addend)
    if second:
        in_specs += [pl.BlockSpec((1, tn), lambda i, j, kk: (0, j))] * n_rows
        args += list(second[1])
    outs = _call(
        body, side, name=name, grid=(m // tm, n // tn, nk),
        in_specs=in_specs, out_specs=[tile] * n_out,
        out_shape=[jax.ShapeDtypeStruct((m, n), out_dtype)] + ([jax.ShapeDtypeStruct((m, n), second[2])] if second
                                                                 else []),
        scratch_shapes=[pltpu.VMEM((tm, tn), F32)],
        semantics=("parallel", "parallel", "arbitrary"), args=args,
    )
    return outs if second else outs[0]


def _matmul_tn(a, b, *, name, tm=1408, tn=512, tk=4096):
    t, m = a.shape
    t2, n = b.shape
    assert t == t2
    tm, tn, tk = _tile(m, tm), _tile(n, tn), _tile(t, tk)

    def body(a_ref, b_ref, o_ref):
        part = _dot_tn(a_ref[...].astype(BF16), b_ref[...].astype(BF16))

        @pl.when(pl.program_id(2) == 0)
        def _():
            o_ref[...] = part

        @pl.when(pl.program_id(2) > 0)
        def _():
            o_ref[...] += part

    return pl.pallas_call(
        body, name=name, grid=(m // tm, n // tn, t // tk),
        in_specs=[pl.BlockSpec((tk, tm), lambda i, j, kk: (kk, i)),
                  pl.BlockSpec((tk, tn), lambda i, j, kk: (kk, j))],
        out_specs=pl.BlockSpec((tm, tn), lambda i, j, kk: (i, j)),
        out_shape=jax.ShapeDtypeStruct((m, n), F32),
        compiler_params=_params("parallel", "parallel", "arbitrary"),
    )(a, b)


def _rmsnorm_rows(tile, j, gain_ref):
    r = lax.rsqrt(jnp.mean(tile * tile, axis=-1, keepdims=True) + NORM_EPS)
    return tile * r * gain_ref[...]


def _rmsnorm_fwd(x, gain, *, name):
    t, d = x.shape
    tm = _tile(t, 512)

    def body(x_ref, g_ref, o_ref):
        xv = x_ref[...]
        r = lax.rsqrt(jnp.mean(xv * xv, axis=-1, keepdims=True) + NORM_EPS)
        o_ref[...] = (xv * r * g_ref[...]).astype(BF16)

    return pl.pallas_call(
        body, name=name, grid=(t // tm,),
        in_specs=[pl.BlockSpec((tm, d), lambda i: (i, 0)), pl.BlockSpec((1, d), lambda i: (0, 0))],
        out_specs=pl.BlockSpec((tm, d), lambda i: (i, 0)),
        out_shape=jax.ShapeDtypeStruct((t, d), BF16),
        compiler_params=_params("parallel"),
    )(x, gain)


def _matmul_rmsnorm_bwd(a, b, addend, x, gain, dres, *, name, side=None, tm=512, tk_max=3072, more=None):
    m, k = a.shape
    d = b.shape[1]
    tm, tk = _tile(m, tm), _tile(k, tk_max)
    nk = k // tk

    def body(a_ref, b_ref, *rest):
        add_ref = rest[2 if more else 0] if addend is not None else None
        x_ref, g_ref, dres_ref, dx_ref, dg_ref, acc_ref = rest[-6:]
        i, kk = pl.program_id(0), pl.program_id(1)
        part = _dot(a_ref[...].astype(BF16), b_ref[...].astype(BF16))
        if more:
            part = part + _dot(rest[0][...].astype(BF16), rest[1][...].astype(BF16))

        @pl.when(kk == 0)
        def _():
            acc_ref[...] = part

        @pl.when(kk > 0)
        def _():
            acc_ref[...] += part

        @pl.when(kk == nk - 1)
        def _():
            dyv = acc_ref[...] if addend is None else acc_ref[...] + add_ref[...]
            xv = x_ref[...]
            r = lax.rsqrt(jnp.mean(xv * xv, axis=-1, keepdims=True) + NORM_EPS)
            xh = xv * r
            dxh = dyv * g_ref[...]
            mean = jnp.mean(dxh * xh, axis=-1, keepdims=True)
            dx_ref[...] = dres_ref[...] + r * (dxh - xh * mean)
            gain_part = jnp.sum(dyv * xh, axis=0, keepdims=True)

            @pl.when(i == 0)
            def _():
                dg_ref[...] = gain_part

            @pl.when(i > 0)
            def _():
                dg_ref[...] += gain_part

    row = pl.BlockSpec((tm, d), lambda i, kk: (i, 0))
    vec = pl.BlockSpec((1, d), lambda i, kk: (0, 0))
    return _call(
        body, side, name=name, grid=(m // tm, nk),
        in_specs=[pl.BlockSpec((tm, tk), lambda i, kk: (i, kk)), pl.BlockSpec((tk, d), lambda i, kk: (kk, 0))]
        * (2 if more else 1) + ([row] if addend is not None else []) + [row, vec, row],
        out_specs=[row, vec],
        out_shape=[jax.ShapeDtypeStruct((m, d), F32), jax.ShapeDtypeStruct((1, d), F32)],
        scratch_shapes=[pltpu.VMEM((tm, d), F32)],
        semantics=("arbitrary", "arbitrary"),
        args=(a, b) + (tuple(more) if more else ()) + ((addend,) if addend is not None else ()) + (x, gain, dres),
    )


def _swiglu_fwd(h, w_gate_t, w_up_t, *, name, side=None):
    t, d = h.shape
    f = w_gate_t.shape[0]
    tm, tn = _tile(t, 1024), _tile(f, 256)

    def body(h_ref, wg_ref, wu_ref, g_ref, u_ref, a_ref):
        hv = h_ref[...]
        g = _dot_nt(hv, wg_ref[...])
        u = _dot_nt(hv, wu_ref[...])
        g_ref[...] = g.astype(BF16)
        u_ref[...] = u.astype(BF16)
        a_ref[...] = (g * _sigmoid(g) * u).astype(BF16)

    wspec = pl.BlockSpec((tn, d), lambda i, j: (j, 0))
    ospec = pl.BlockSpec((tm, tn), lambda i, j: (i, j))
    return _call(
        body, side, name=name, grid=(t // tm, f // tn),
        in_specs=[pl.BlockSpec((tm, d), lambda i, j: (i, 0)), wspec, wspec],
        out_specs=[ospec, ospec, ospec],
        out_shape=[jax.ShapeDtypeStruct((t, f), BF16), jax.ShapeDtypeStruct((t, f), BF16),
                   jax.ShapeDtypeStruct((t, f), BF16)],
        scratch_shapes=[], semantics=("parallel", "parallel"), args=(h, w_gate_t, w_up_t),
    )


def _swiglu_bwd(dx, w_down, g, u, *, name, side=None):
    t, d = dx.shape
    f = w_down.shape[0]
    tm, tn = _tile(t, 1024), _tile(f, 256)

    def body(dx_ref, wd_ref, g_ref, u_ref, dg_ref, du_ref):
        dact = _dot_nt(dx_ref[...].astype(BF16), wd_ref[...])
        gv, uv = g_ref[...].astype(F32), u_ref[...].astype(F32)
        sg = _sigmoid(gv)
        dg_ref[...] = (dact * uv * sg * (1.0 + gv * (1.0 - sg))).astype(BF16)
        du_ref[...] = (dact * gv * sg).astype(BF16)

    ospec = pl.BlockSpec((tm, tn), lambda i, j: (i, j))
    return _call(
        body, side, name=name, grid=(t // tm, f // tn),
        in_specs=[pl.BlockSpec((tm, d), lambda i, j: (i, 0)), pl.BlockSpec((tn, d), lambda i, j: (j, 0)),
                  ospec, ospec],
        out_specs=[ospec, ospec],
        out_shape=[jax.ShapeDtypeStruct((t, f), BF16), jax.ShapeDtypeStruct((t, f), BF16)],
        scratch_shapes=[], semantics=("parallel", "parallel"), args=(dx, w_down, g, u),
    )


def _ple_fwd(x, p, w_gate, w_proj_t, *, name, next_gain=None, target=None):
    t, d = x.shape
    e = p.shape[1]
    tm = _tile(t, 512)
    steps = t // tm

    def body(x_ref, p_ref, wg_ref, wp_ref, *rest):
        xv = x_ref[...]
        s = _dot(xv.astype(BF16), wg_ref[...])
        ple = _dot_nt(p_ref[...].astype(BF16), wp_ref[...])
        y = xv + _sigmoid(s) * ple
        if target is None:
            gain_ref, y_ref, h_ref = rest
            y_ref[...] = y
            r = lax.rsqrt(jnp.mean(y * y, axis=-1, keepdims=True) + NORM_EPS)
            h_ref[...] = (y * r * gain_ref[...]).astype(BF16)
        else:
            t_ref, dy_ref, l_ref, acc_ref = rest
            err = y - t_ref[...]
            dy_ref[...] = err * (1.0 / d)
            part = jnp.sum(err * err, axis=0, keepdims=True)

            @pl.when(pl.program_id(0) == 0)
            def _():
                acc_ref[...] = part

            @pl.when(pl.program_id(0) > 0)
            def _():
                acc_ref[...] += part

            @pl.when(pl.program_id(0) == steps - 1)
            def _():
                l_ref[...] = jnp.full((1, LANES), (0.5 / d), F32) * jnp.sum(acc_ref[...])

    row = pl.BlockSpec((tm, d), lambda i: (i, 0))
    fixed = lambda shape: pl.BlockSpec(shape, lambda i: (0, 0))
    in_specs = [row, pl.BlockSpec((tm, e), lambda i: (i, 0)), fixed((d, d)), fixed((d, e))]
    if target is None:
        return pl.pallas_call(
            body, name=name, grid=(steps,), in_specs=in_specs + [fixed((1, d))], out_specs=[row, row],
            out_shape=[jax.ShapeDtypeStruct((t, d), F32), jax.ShapeDtypeStruct((t, d), BF16)],
            compiler_params=_params("parallel"),
        )(x, p, w_gate, w_proj_t, next_gain)
    return pl.pallas_call(
        body, name=name, grid=(steps,), in_specs=in_specs + [row], out_specs=[row, fixed((1, LANES))],
        out_shape=[jax.ShapeDtypeStruct((t, d), F32), jax.ShapeDtypeStruct((1, LANES), F32)],
        scratch_shapes=[pltpu.VMEM((1, d), F32)],
        compiler_params=_params("arbitrary"),
    )(x, p, w_gate, w_proj_t, target)


def _ple_bwd(x, p, w_gate, w_proj_t, dout, *, name):
    t, d = x.shape
    e = p.shape[1]
    tm = _tile(t, 512)

    def body(x_ref, p_ref, wg_ref, wp_ref, do_ref, ds_ref, dple_ref, dx_ref):
        wg = wg_ref[...]
        s = _dot(x_ref[...].astype(BF16), wg)
        ple = _dot_nt(p_ref[...].astype(BF16), wp_ref[...])
        gate = _sigmoid(s)
        dov = do_ref[...]
        dple_ref[...] = (dov * gate).astype(BF16)
        ds = (dov * ple * gate * (1.0 - gate)).astype(BF16)
        ds_ref[...] = ds
        dx_ref[...] = dov + _dot_nt(ds, wg)

    row = pl.BlockSpec((tm, d), lambda i: (i, 0))
    fixed = lambda shape: pl.BlockSpec(shape, lambda i: (0, 0))
    return pl.pallas_call(
        body, name=name, grid=(t // tm,),
        in_specs=[row, pl.BlockSpec((tm, e), lambda i: (i, 0)), fixed((d, d)), fixed((d, e)), row],
        out_specs=[row, row, row],
        out_shape=[jax.ShapeDtypeStruct((t, d), BF16), jax.ShapeDtypeStruct((t, d), BF16),
                   jax.ShapeDtypeStruct((t, d), F32)],
        compiler_params=_params("parallel"),
    )(x, p, w_gate, w_proj_t, dout)


CONV_TIME_TILE = 256
CONV_HALO = 8


def _conv_taps(ext, w):
    acc = ext[CONV_HALO:, :] * w[CONV_WIDTH - 1:CONV_WIDTH, :]
    shifted = [ext[CONV_HALO:, :]]
    for j in range(1, CONV_WIDTH):
        sh = pltpu.roll(ext, j, 0)[CONV_HALO:, :]
        shifted.append(sh)
        acc = acc + sh * w[CONV_WIDTH - 1 - j:CONV_WIDTH - j, :]
    return acc, shifted


def _conv_fwd(u, w, b, side=None):
    t, c = u.shape
    tc = _tile(c, 256)
    tt = CONV_TIME_TILE

    def body(u_ref, w_ref, b_ref, o_ref):
        wv, bv = w_ref[...], b_ref[...]

        def tile(start, ext):
            pre = _conv_taps(ext, wv)[0] + bv
            o_ref[pl.ds(start, tt), :] = pre * _sigmoid(pre)

        tile(0, jnp.concatenate([jnp.zeros((CONV_HALO, tc), F32), u_ref[0:tt, :]], axis=0))

        def loop(i, carry):
            start = pl.multiple_of(i * tt, tt)
            tile(start, u_ref[pl.ds(start - CONV_HALO, tt + CONV_HALO), :])
            return carry

        lax.fori_loop(1, t // tt, loop, 0)

    col = pl.BlockSpec((t, tc), lambda j: (0, j))
    return _call(
        body, side, name="conv_fwd", grid=(c // tc,),
        in_specs=[col, pl.BlockSpec((CONV_WIDTH, tc), lambda j: (0, j)), pl.BlockSpec((1, tc), lambda j: (0, j))],
        out_specs=[col], out_shape=[jax.ShapeDtypeStruct((t, c), F32)],
        scratch_shapes=[], semantics=("parallel",), args=(u, w, b),
    )[0]


def _conv_bwd(u, w, b, dact, side=None):
    t, c = u.shape
    tc = _tile(c, 256)
    tt = CONV_TIME_TILE

    def body(u_ref, w_ref, b_ref, da_ref, du_ref, dw_ref, db_ref, dpre_ref):
        wv, bv = w_ref[...], b_ref[...]

        def tile(start, ext, sums):
            acc, shifted = _conv_taps(ext, wv)
            pre = acc + bv
            sg = _sigmoid(pre)
            dpre = da_ref[pl.ds(start, tt), :] * (sg * (1.0 + pre * (1.0 - sg)))
            dpre_ref[pl.ds(start, tt), :] = dpre
            new = [sums[0] + jnp.sum(dpre, axis=0, keepdims=True)]
            for j in range(CONV_WIDTH):
                new.append(sums[1 + j] + jnp.sum(dpre * shifted[j], axis=0, keepdims=True))
            return tuple(new)

        zero = jnp.zeros((1, tc), F32)
        sums = tile(0, jnp.concatenate([jnp.zeros((CONV_HALO, tc), F32), u_ref[0:tt, :]], axis=0),
                    (zero,) * (1 + CONV_WIDTH))

        def loop(i, sums):
            start = pl.multiple_of(i * tt, tt)
            return tile(start, u_ref[pl.ds(start - CONV_HALO, tt + CONV_HALO), :], sums)

        sums = lax.fori_loop(1, t // tt, loop, sums)
        db_ref[...] = sums[0]
        dw_ref[...] = jnp.concatenate([sums[1 + (CONV_WIDTH - 1 - k)] for k in range(CONV_WIDTH)], axis=0)
        dpre_ref[pl.ds(t, CONV_HALO), :] = jnp.zeros((CONV_HALO, tc), F32)

        def loop2(i, carry):
            start = pl.multiple_of(i * tt, tt)
            ext = dpre_ref[pl.ds(start, tt + CONV_HALO), :]
            acc = ext[0:tt, :] * wv[CONV_WIDTH - 1:CONV_WIDTH, :]
            for j in range(1, CONV_WIDTH):
                acc = acc + pltpu.roll(ext, tt + CONV_HALO - j, 0)[0:tt, :] * wv[CONV_WIDTH - 1 - j:CONV_WIDTH - j, :]
            du_ref[pl.ds(start, tt), :] = acc.astype(BF16)
            return carry

        lax.fori_loop(0, t // tt, loop2, 0)

    col = pl.BlockSpec((t, tc), lambda j: (0, j))
    return _call(
        body, side, name="conv_bwd", grid=(c // tc,),
        in_specs=[col, pl.BlockSpec((CONV_WIDTH, tc), lambda j: (0, j)), pl.BlockSpec((1, tc), lambda j: (0, j)), col],
        out_specs=[col, pl.BlockSpec((CONV_WIDTH, tc), lambda j: (0, j)), pl.BlockSpec((1, tc), lambda j: (0, j))],
        out_shape=[jax.ShapeDtypeStruct((t, c), BF16), jax.ShapeDtypeStruct((CONV_WIDTH, c), F32),
                   jax.ShapeDtypeStruct((1, c), F32)],
        scratch_shapes=[pltpu.VMEM((t + CONV_HALO, tc), F32)],
        semantics=("parallel",), args=(u, w, b, dact),
    )


def _softplus(v):
    e = jnp.exp(-jnp.abs(v))
    w = 1.0 + e
    log1p = jnp.where(w == 1.0, e, jnp.log(w) * (e / jnp.where(w == 1.0, 1.0, w - 1.0)))
    return jnp.maximum(v, 0.0) + log1p


def _split3(z):
    hi = z.astype(BF16)
    rest = z - hi.astype(F32)
    mid = rest.astype(BF16)
    return hi, mid, (rest - mid.astype(F32)).astype(BF16)


def _select_dot(z, ones):
    return sum(_dot(term, ones) for term in _split3(z))


def _ssd_prep_fwd(dt_raw, dt_bias, a_log):
    t = dt_raw.shape[0]
    cl = SSD_CHUNK

    def body(r_ref, b_ref, al_ref, acs_ref, dt_rep_ref, acs_rep_ref):
        dt = _softplus(r_ref[...] + b_ref[...])
        adt = dt * (-jnp.exp(al_ref[...]))
        li = lax.broadcasted_iota(jnp.int32, (cl, cl), 0)
        si = lax.broadcasted_iota(jnp.int32, (cl, cl), 1)
        tri = (si <= li).astype(F32)
        acs = jnp.dot(tri, adt, preferred_element_type=F32, precision=HIGHEST)
        acs_ref[...] = acs
        head = lax.broadcasted_iota(jnp.int32, (LANES, D_INNER), 0)
        chan = lax.broadcasted_iota(jnp.int32, (LANES, D_INNER), 1) // SSM_HEAD_DIM
        spread = (head == chan).astype(BF16)
        dt_rep_ref[...] = _select_dot(dt, spread)
        acs_rep_ref[...] = _select_dot(acs, spread)

    row = pl.BlockSpec((cl, LANES), lambda i: (i, 0))
    wide = pl.BlockSpec((cl, D_INNER), lambda i: (i, 0))
    vec = pl.BlockSpec((1, LANES), lambda i: (0, 0))
    return pl.pallas_call(
        body, name="ssd_prep_fwd", grid=(t // cl,),
        in_specs=[row, vec, vec], out_specs=[row, wide, wide],
        out_shape=[jax.ShapeDtypeStruct((t, LANES), F32), jax.ShapeDtypeStruct((t, D_INNER), F32),
                   jax.ShapeDtypeStruct((t, D_INNER), F32)],
        compiler_params=_params("parallel"),
    )(dt_raw, dt_bias, a_log)


def _ssd_prep_bwd(dt_raw, dt_bias, ddt):
    t = dt_raw.shape[0]
    tm = _tile(t, 512)

    def body(r_ref, b_ref, d_ref, o_ref, db_ref):
        g = d_ref[...] * _sigmoid(r_ref[...] + b_ref[...])
        o_ref[...] = g.astype(BF16)
        part = jnp.sum(g, axis=0, keepdims=True)

        @pl.when(pl.program_id(0) == 0)
        def _():
            db_ref[...] = part

        @pl.when(pl.program_id(0) > 0)
        def _():
            db_ref[...] += part

    row = pl.BlockSpec((tm, LANES), lambda i: (i, 0))
    vec = pl.BlockSpec((1, LANES), lambda i: (0, 0))
    return pl.pallas_call(
        body, name="ssd_prep_bwd", grid=(t // tm,),
        in_specs=[row, vec, row], out_specs=[row, vec],
        out_shape=[jax.ShapeDtypeStruct((t, LANES), BF16), jax.ShapeDtypeStruct((1, LANES), F32)],
        compiler_params=_params("arbitrary"),
    )(dt_raw, dt_bias, ddt)


GROUP_W = D_INNER // SSM_GROUPS
PAIRS_PER_GROUP = GROUP_W // LANES


def _head_cols(acs_pair, lt64):
    rolled = pltpu.roll(acs_pair, ATT_HEAD_DIM, 1)
    return jnp.where(lt64, acs_pair, rolled), jnp.where(lt64, rolled, acs_pair)


def _ssd_fwd(xbc, dt_rep, acs_rep, acs_t, dskip_rep, z, norm_w, side=None):
    t = xbc.shape[0]
    cl = SSD_CHUNK
    nc = t // cl

    def body(xbc_ref, dt_ref, acs_ref, acst_ref, dskip_ref, z_ref, nw_ref, y_ref, hin_ref, yn_ref, state_ref):
        @pl.when(pl.program_id(0) == 0)
        def _():
            state_ref[...] = jnp.zeros_like(state_ref)

        lt64 = _lane_lt64(cl)
        li = lax.broadcasted_iota(jnp.int32, (cl, cl), 0)
        si = lax.broadcasted_iota(jnp.int32, (cl, cl), 1)
        causal = li >= si
        hin_ref[...] = state_ref[...]
        for g in range(SSM_GROUPS):
            gsl = slice(g * GROUP_W, (g + 1) * GROUP_W)
            xg = xbc_ref[:, gsl]
            bg = xbc_ref[:, D_INNER + g * SSM_STATE:D_INNER + (g + 1) * SSM_STATE]
            cg = xbc_ref[:, D_INNER + SSM_GROUPS * SSM_STATE + g * SSM_STATE:
                         D_INNER + SSM_GROUPS * SSM_STATE + (g + 1) * SSM_STATE]
            acs = acs_ref[:, gsl]
            xdt = xg * dt_ref[:, gsl]
            atot = acs[cl - 1:cl, :]
            hin = state_ref[:, gsl]
            cgb = cg.astype(BF16)
            gmat = _dot_nt(cgb, bg.astype(BF16))
            yoff = _dot(cgb, hin.astype(BF16)) * jnp.exp(acs)
            snew = _dot(bg.T.astype(BF16), (xdt * jnp.exp(atot - acs)).astype(BF16))
            state_ref[:, gsl] = hin * jnp.exp(atot) + snew
            xdtb = xdt.astype(BF16)
            for pr in range(PAIRS_PER_GROUP):
                psl = slice(pr * LANES, (pr + 1) * LANES)
                cols = _head_cols(acs[:, psl], lt64)
                xp = xdtb[:, psl]
                ys = []
                for hh in range(2):
                    h = (g * PAIRS_PER_GROUP + pr) * 2 + hh
                    seg = cols[hh] - acst_ref[h:h + 1, :]
                    lm = jnp.exp(jnp.where(causal, seg, NEG_BIG))
                    ys.append(_dot((gmat * lm).astype(BF16), xp))
                ydiag = jnp.where(lt64, ys[0], ys[1])
                osl = slice(g * GROUP_W + pr * LANES, g * GROUP_W + (pr + 1) * LANES)
                y_ref[:, osl] = ydiag + yoff[:, psl] + xg[:, psl] * dskip_ref[:, osl]
            zv = z_ref[:, gsl]
            v = y_ref[:, gsl] * (zv * _sigmoid(zv))
            r = lax.rsqrt(jnp.mean(v * v, axis=-1, keepdims=True) + NORM_EPS)
            yn_ref[:, gsl] = (v * r * nw_ref[:, gsl]).astype(BF16)

    row = lambda w: pl.BlockSpec((cl, w), lambda c: (c, 0))
    vec = pl.BlockSpec((1, D_INNER), lambda c: (0, 0))
    return _call(
        body, side, name="ssd_fwd", grid=(nc,),
        in_specs=[row(CONV_DIM), row(D_INNER), row(D_INNER),
                  pl.BlockSpec((SSM_HEADS, cl), lambda c: (0, c)), vec, row(D_INNER), vec],
        out_specs=[row(D_INNER), pl.BlockSpec((None, SSM_STATE, D_INNER), lambda c: (c, 0, 0)), row(D_INNER)],
        out_shape=[jax.ShapeDtypeStruct((t, D_INNER), F32), jax.ShapeDtypeStruct((nc, SSM_STATE, D_INNER), F32),
                   jax.ShapeDtypeStruct((t, D_INNER), BF16)],
        scratch_shapes=[pltpu.VMEM((SSM_STATE, D_INNER), F32)],
        semantics=("arbitrary",), args=(xbc, dt_rep, acs_rep, acs_t, dskip_rep, z, norm_w),
    )


def _ssd_bwd(xbc, dt_rep, acs_rep, acs_t, dskip_rep, a_rep, hin_all, dy, side=None):
    t = xbc.shape[0]
    cl = SSD_CHUNK
    nc = t // cl

    def body(xbc_ref, dt_ref, acs_ref, acst_ref, dskip_ref, a_ref, hin_ref, dy_ref,
             dxbc_ref, ddt_ref, da_ref, dds_ref, dstate_ref, dacs_ref, dxs_ref):
        step = pl.program_id(0)

        @pl.when(step == 0)
        def _():
            dstate_ref[...] = jnp.zeros_like(dstate_ref)
            da_ref[...] = jnp.zeros_like(da_ref)
            dds_ref[...] = jnp.zeros_like(dds_ref)

        bd = _head_block_diag()
        lt64 = _lane_lt64(cl)
        li = lax.broadcasted_iota(jnp.int32, (cl, cl), 0)
        si = lax.broadcasted_iota(jnp.int32, (cl, cl), 1)
        lower = li >= si
        upper = si >= li
        last_row = lax.broadcasted_iota(jnp.int32, (cl, GROUP_W), 0) == cl - 1
        for g in range(SSM_GROUPS):
            gsl = slice(g * GROUP_W, (g + 1) * GROUP_W)
            bsl = slice(D_INNER + g * SSM_STATE, D_INNER + (g + 1) * SSM_STATE)
            csl = slice(D_INNER + SSM_GROUPS * SSM_STATE + g * SSM_STATE,
                        D_INNER + SSM_GROUPS * SSM_STATE + (g + 1) * SSM_STATE)
            xg = xbc_ref[:, gsl]
            bg = xbc_ref[:, bsl]
            cg = xbc_ref[:, csl]
            bgb, cgb = bg.astype(BF16), cg.astype(BF16)
            acs = acs_ref[:, gsl]
            xdt = xg * dt_ref[:, gsl]
            atot = acs[cl - 1:cl, :]
            eg = jnp.exp(acs)
            dk = jnp.exp(atot - acs)
            etot = jnp.exp(atot)
            hin = hin_ref[:, gsl]
            hinb = hin.astype(BF16)
            dh = dstate_ref[:, gsl]
            dhb = dh.astype(BF16)
            dyg = dy_ref[:, gsl]

            gmat = _dot_nt(cgb, bgb)
            gmat_t = _dot_nt(bgb, cgb)
            ch = _dot(cgb, hinb)
            dacs = _head_sums(dyg * ch * eg, bd)
            dye = (dyg * eg).astype(BF16)
            dc = _dot_nt(dye, hinb)
            dhin = _dot(cg.T.astype(BF16), dye)
            bdh = _dot(bgb, dhb)
            dxs = bdh * dk
            xdk = xdt * dk
            db = _dot_nt(xdk.astype(BF16), dhb)
            ddk = _head_sums(bdh * xdk, bd)
            dacs = dacs - ddk
            datot = jnp.sum(ddk, axis=0, keepdims=True) + etot * _head_sums(
                jnp.sum(dh * hin, axis=0, keepdims=True), bd)
            dacs = dacs + jnp.where(last_row, datot, 0.0)
            dstate_ref[:, gsl] = dh * etot + dhin

            xdtb = xdt.astype(BF16)
            dgsum = jnp.zeros((cl, cl), F32)
            dgsum_t = jnp.zeros((cl, cl), F32)
            for pr in range(PAIRS_PER_GROUP):
                psl = slice(pr * LANES, (pr + 1) * LANES)
                cols = _head_cols(acs[:, psl], lt64)
                xp = xdtb[:, psl]
                dyp = dyg[:, psl].astype(BF16)
                dx1, dac = [], []
                for hh in range(2):
                    h = (g * PAIRS_PER_GROUP + pr) * 2 + hh
                    mine = lt64 if hh == 0 else jnp.logical_not(lt64)
                    row = acst_ref[h:h + 1, :]
                    lm = jnp.exp(jnp.where(lower, cols[hh] - row, NEG_BIG))
                    lm_t = jnp.exp(jnp.where(upper, row - cols[hh], NEG_BIG))
                    dyh = jnp.where(mine, dyp, jnp.zeros_like(dyp))
                    xh = jnp.where(mine, xp, jnp.zeros_like(xp))
                    dm = _dot_nt(dyh, xp)
                    dm_t = _dot_nt(xh, dyp)
                    m_t = gmat_t * lm_t
                    dx1.append(_dot(m_t.astype(BF16), dyp))
                    w = dm * (gmat * lm)
                    w_t = dm_t * m_t
                    dac.append(jnp.sum(w, axis=1, keepdims=True) - jnp.sum(w_t, axis=1, keepdims=True))
                    dgsum = dgsum + dm * lm
                    dgsum_t = dgsum_t + dm_t * lm_t
                osl = slice(g * GROUP_W + pr * LANES, g * GROUP_W + (pr + 1) * LANES)
                dxs_ref[:, osl] = dxs[:, psl] + jnp.where(lt64, dx1[0], dx1[1])
                dacs_ref[:, osl] = dacs[:, psl] + jnp.where(lt64, jnp.broadcast_to(dac[0], (cl, LANES)),
                                                             jnp.broadcast_to(dac[1], (cl, LANES)))
            dxbc_ref[:, csl] = dc + _dot(dgsum.astype(BF16), bgb)
            dxbc_ref[:, bsl] = db + _dot(dgsum_t.astype(BF16), cgb)

        dadt = _split_dot(upper.astype(BF16), dacs_ref[...])
        xall = xbc_ref[:, 0:D_INNER]
        dtall = dt_ref[...]
        dxsall = dxs_ref[...]
        dyall = dy_ref[...]
        ddt_rep = dadt * a_ref[...] + _head_sums(dxsall * xall, bd)
        chan = lax.broadcasted_iota(jnp.int32, (D_INNER, LANES), 0)
        head = lax.broadcasted_iota(jnp.int32, (D_INNER, LANES), 1)
        ddt_ref[...] = _select_dot(ddt_rep, (chan == head * SSM_HEAD_DIM).astype(BF16))
        dxbc_ref[:, 0:D_INNER] = dxsall * dtall + dyall * dskip_ref[...]
        da_ref[...] += jnp.sum(dadt * dtall, axis=0, keepdims=True)
        dds_ref[...] += jnp.sum(dyall * xall, axis=0, keepdims=True)

        @pl.when(step == nc - 1)
        def _():
            dds_ref[...] = _head_sums(dds_ref[...], bd)

    row = lambda w: pl.BlockSpec((cl, w), lambda c: (nc - 1 - c, 0))
    vec = pl.BlockSpec((1, D_INNER), lambda c: (0, 0))
    return _call(
        body, side, name="ssd_bwd", grid=(nc,),
        in_specs=[row(CONV_DIM), row(D_INNER), row(D_INNER),
                  pl.BlockSpec((SSM_HEADS, cl), lambda c: (0, nc - 1 - c)), vec, vec,
                  pl.BlockSpec((None, SSM_STATE, D_INNER), lambda c: (nc - 1 - c, 0, 0)), row(D_INNER)],
        out_specs=[row(CONV_DIM), row(LANES), vec, vec],
        out_shape=[jax.ShapeDtypeStruct((t, CONV_DIM), F32), jax.ShapeDtypeStruct((t, LANES), F32),
                   jax.ShapeDtypeStruct((1, D_INNER), F32), jax.ShapeDtypeStruct((1, D_INNER), F32)],
        scratch_shapes=[pltpu.VMEM((SSM_STATE, D_INNER), F32), pltpu.VMEM((cl, D_INNER), F32),
                        pltpu.VMEM((cl, D_INNER), F32)],
        semantics=("arbitrary",), args=(xbc, dt_rep, acs_rep, acs_t, dskip_rep, a_rep, hin_all, dy),
    )


def _gate_norm_bwd(y, z, w, dx, w_out, side=None):
    t, c = y.shape
    d = dx.shape[1]
    tm = _tile(t, 256)

    def body(y_ref, z_ref, w_ref, dx_ref, wo_ref, dy_ref, dz_ref, dw_ref):
        @pl.when(pl.program_id(0) == 0)
        def _():
            dw_ref[...] = jnp.zeros_like(dw_ref)

        dxb = dx_ref[...].astype(BF16)
        for g in range(SSM_GROUPS):
            gsl = slice(g * GROUP_W, (g + 1) * GROUP_W)
            zv, yv, dov = z_ref[:, gsl], y_ref[:, gsl], _dot_nt(dxb, wo_ref[gsl, :])
            sg = _sigmoid(zv)
            sz = zv * sg
            v = yv * sz
            r = lax.rsqrt(jnp.mean(v * v, axis=-1, keepdims=True) + NORM_EPS)
            vh = v * r
            dvh = dov * w_ref[:, gsl]
            mean = jnp.mean(dvh * vh, axis=-1, keepdims=True)
            dv = r * (dvh - vh * mean)
            dy_ref[:, gsl] = dv * sz
            dz_ref[:, gsl] = (dv * yv * (sg * (1.0 + zv * (1.0 - sg)))).astype(BF16)
            dw_ref[:, gsl] += jnp.sum(dov * vh, axis=0, keepdims=True)

    row = pl.BlockSpec((tm, c), lambda i: (i, 0))
    vec = pl.BlockSpec((1, c), lambda i: (0, 0))
    return _call(
        body, side, name="gate_norm_bwd", grid=(t // tm,),
        in_specs=[row, row, vec, pl.BlockSpec((tm, d), lambda i: (i, 0)), pl.BlockSpec((c, d), lambda i: (0, 0))],
        out_specs=[row, row, vec],
        out_shape=[jax.ShapeDtypeStruct((t, c), F32), jax.ShapeDtypeStruct((t, c), BF16),
                   jax.ShapeDtypeStruct((1, c), F32)],
        scratch_shapes=[], semantics=("arbitrary",), args=(y, z, w, dx, w_out),
    )


ATT_W = ATT_HEADS * ATT_HEAD_DIM
N_QKV_BLOCKS = 9
ATT_SCALE = 1.0 / math.sqrt(ATT_HEAD_DIM)


def _head_rmsnorm(x, gain, bd):
    ms = _head_sums(x * x, bd, terms=1) * (1.0 / ATT_HEAD_DIM)
    return x * lax.rsqrt(ms + NORM_EPS) * gain


def _class_rows(ref, blk, r, dil):
    span = ATT_BLOCK * dil
    sub = ref.at[pl.ds(pl.multiple_of(blk * span, span), span), :]
    return sub[...] if dil == 1 else sub[pl.ds(r, ATT_BLOCK, stride=dil), :]


def _store_class_rows(ref, blk, r, dil, val):
    span = ATT_BLOCK * dil
    sub = ref.at[pl.ds(pl.multiple_of(blk * span, span), span), :]
    if dil == 1:
        sub[...] = val
    else:
        sub[pl.ds(r, ATT_BLOCK, stride=dil), :] = val


PAIRS = ATT_HEADS // 2


def _pair_col(g, j):
    return lambda pair: (0, (g * 3 + j) * PAIRS + pair)


def _pair_slopes(pair):
    steps = jnp.full((1, 2 * ATT_BLOCK), 2 * pair + 1, jnp.int32).astype(F32)
    first = jnp.exp(steps * (-0.5 * math.log(2.0)))
    return first, first * (2.0 ** -0.5)


NORM_ROWS = 512


ROW_SLICES = 4
SLICE_ROWS = 2 * ATT_BLOCK // ROW_SLICES


def _fill_band_bias(bias_ref, pair, dil, transposed):
    bq = ATT_BLOCK
    a = lax.broadcasted_iota(jnp.int32, (2 * bq, 2 * bq), 0) % bq
    b = lax.broadcasted_iota(jnp.int32, (2 * bq, 2 * bq), 1)
    dist = (b - a) if transposed else (a + bq - b)
    in_band = (dist >= 0) & (dist <= bq)
    s0, s1 = _pair_slopes(pair)
    first_head = lax.broadcasted_iota(jnp.int32, (2 * bq, 2 * bq), 0) < bq
    bias = jnp.where(first_head, s0, s1) * (dist.astype(F32) * float(dil))
    inside = (b < bq) if transposed else (b >= bq)
    bias_ref[1] = jnp.where(in_band, bias, -NEG_BIG)
    bias_ref[0] = jnp.where(in_band & inside, bias, -NEG_BIG)


def _row_slices():
    return [slice(i * SLICE_ROWS, (i + 1) * SLICE_ROWS) for i in range(ROW_SLICES)]


def _stack_heads(tile):
    rows = lax.broadcasted_iota(jnp.int32, (2 * ATT_BLOCK, LANES), 0) < ATT_BLOCK
    lanes = lax.broadcasted_iota(jnp.int32, (2 * ATT_BLOCK, LANES), 1) < ATT_HEAD_DIM
    both = jnp.concatenate([tile, tile], axis=0)
    return jnp.where(rows == lanes, both, jnp.zeros_like(both))


def _unstack_heads(stacked, lt64):
    return jnp.where(lt64, stacked[:ATT_BLOCK], stacked[ATT_BLOCK:])


ITEMS_PER_PASS = 4


def _item_loop(nb, dil, work):
    if dil == 1:
        def trip(i, carry):
            work([(i * ITEMS_PER_PASS + b, 0) for b in range(ITEMS_PER_PASS)])
            return carry

        lax.fori_loop(0, nb // ITEMS_PER_PASS, trip, 0)
    else:
        def trip(n, carry):
            for r0 in range(0, dil, ITEMS_PER_PASS):
                work([(n, r0 + j) for j in range(ITEMS_PER_PASS)])
            return carry

        lax.fori_loop(0, nb, trip, 0)


def _qk_normalised(tile, j, gq_ref, gk_ref):
    kind = (j // (ATT_W // tile.shape[1])) % 3
    gain = jnp.where(kind == 0, gq_ref[...] * ATT_SCALE, gk_ref[...])
    return jnp.where(kind == 2, tile, _head_rmsnorm(tile, gain, _head_block_diag()))


def _attn_fwd(qkn, g, dil):
    t = qkn.shape[0]
    nb = t // dil // ATT_BLOCK
    bq = ATT_BLOCK

    def body(qn_ref, kn_ref, v_ref, o_ref, l_ref, bias_ref):
        _fill_band_bias(bias_ref, pl.program_id(0), dil, False)
        lt64 = _lane_lt64(bq)

        def work(items):
            scores, values, probs = [], [], []
            for n, r in items:
                prev = jnp.maximum(n - 1, 0)
                q2 = _stack_heads(_class_rows(qn_ref, n, r, dil).astype(BF16))
                kcat = jnp.concatenate([_class_rows(kn_ref, prev, r, dil), _class_rows(kn_ref, n, r, dil)],
                                       axis=0).astype(BF16)
                values.append(jnp.concatenate([_class_rows(v_ref, prev, r, dil), _class_rows(v_ref, n, r, dil)],
                                              axis=0).astype(BF16))
                scores.append(_dot_nt(q2, kcat))
            for (n, r), sc in zip(items, scores):
                bias = bias_ref.at[jnp.minimum(n, 1)]
                ps, inv, lses = [], [], []
                for rows in _row_slices():
                    s = sc[rows] - bias[rows, :]
                    m = jnp.max(s, axis=1, keepdims=True)
                    p = jnp.exp(s - m)
                    l = jnp.sum(p, axis=1, keepdims=True)
                    ps.append(p.astype(BF16))
                    inv.append(jnp.broadcast_to(1.0 / l, (SLICE_ROWS, LANES)))
                    lses.append(jnp.broadcast_to(m + jnp.log(l), (SLICE_ROWS, LANES)))
                probs.append((jnp.concatenate(ps, axis=0), jnp.concatenate(inv, axis=0)))
                _store_class_rows(l_ref, n, r, dil, _unstack_heads(jnp.concatenate(lses, axis=0), lt64))
            for (n, r), (p, inv), vcat in zip(items, probs, values):
                _store_class_rows(o_ref, n, r, dil, _unstack_heads(_dot(p, vcat) * inv, lt64))

        _item_loop(nb, dil, work)

    col = lambda j: pl.BlockSpec((t, LANES), _pair_col(g, j))
    out = pl.BlockSpec((t, LANES), lambda pair: (0, pair))
    return pl.pallas_call(
        body, name=f"attn_fwd_g{g}", grid=(PAIRS,),
        in_specs=[col(0), col(1), col(2)], out_specs=[out, out],
        out_shape=[jax.ShapeDtypeStruct((t, ATT_W), F32), jax.ShapeDtypeStruct((t, ATT_W), F32)],
        scratch_shapes=[pltpu.VMEM((2, 2 * bq, 2 * bq), F32)],
        compiler_params=_params("parallel"),
    )(qkn, qkn, qkn)


def _one_per_head(rep):
    chan = lax.broadcasted_iota(jnp.int32, (ATT_W, LANES), 0)
    head = lax.broadcasted_iota(jnp.int32, (ATT_W, LANES), 1)
    return _select_dot(rep, (chan == head * ATT_HEAD_DIM).astype(BF16))


def _attn_out_fwd(outs, lses, w_o, x0, next_gain):
    t, d = x0.shape
    tm = _tile(t, 256)

    def body(o0, o1, o2, l0, l1, l2, wo_ref, x_ref, g_ref, of_ref, lt_ref, lc_ref, x1_ref, h_ref):
        a, b, c = l0[...], l1[...], l2[...]
        m = jnp.maximum(jnp.maximum(a, b), c)
        ea, eb, ec = jnp.exp(a - m), jnp.exp(b - m), jnp.exp(c - m)
        ssum = ea + eb + ec
        o = (ea * o0[...] + eb * o1[...] + ec * o2[...]) / ssum
        of_ref[...] = o
        lse = m + jnp.log(ssum)
        lt_ref[...] = lse
        lc_ref[...] = _one_per_head(lse)
        x1 = x_ref[...] + _dot(o.astype(BF16), wo_ref[...])
        x1_ref[...] = x1
        r = lax.rsqrt(jnp.mean(x1 * x1, axis=-1, keepdims=True) + NORM_EPS)
        h_ref[...] = (x1 * r * g_ref[...]).astype(BF16)

    row = pl.BlockSpec((tm, ATT_W), lambda i: (i, 0))
    xrow = pl.BlockSpec((tm, d), lambda i: (i, 0))
    return pl.pallas_call(
        body, name="att_out", grid=(t // tm,),
        in_specs=[row] * 6 + [pl.BlockSpec((ATT_W, d), lambda i: (0, 0)), xrow, pl.BlockSpec((1, d), lambda i: (0, 0))],
        out_specs=[row, row, pl.BlockSpec((tm, LANES), lambda i: (i, 0)), xrow, xrow],
        out_shape=[jax.ShapeDtypeStruct((t, ATT_W), F32), jax.ShapeDtypeStruct((t, ATT_W), F32),
                   jax.ShapeDtypeStruct((t, LANES), F32), jax.ShapeDtypeStruct((t, d), F32),
                   jax.ShapeDtypeStruct((t, d), BF16)],
        compiler_params=_params("parallel"),
    )(*outs, *lses, w_o, x0, next_gain)


def _attn_out_bwd(dx, w_o, o, side=None):
    t, d = dx.shape
    tm = _tile(t, 256)

    def body(dx_ref, wo_ref, o_ref, do_ref, dl_ref, dc_ref):
        do = _dot_nt(dx_ref[...].astype(BF16), wo_ref[...])
        do_ref[...] = do
        dl = _head_sums(do * o_ref[...], _head_block_diag())
        dl_ref[...] = dl
        dc_ref[...] = _one_per_head(dl)

    row = pl.BlockSpec((tm, ATT_W), lambda i: (i, 0))
    return _call(
        body, side, name="att_out_dx", grid=(t // tm,),
        in_specs=[pl.BlockSpec((tm, d), lambda i: (i, 0)), pl.BlockSpec((ATT_W, d), lambda i: (0, 0)), row],
        out_specs=[row, row, pl.BlockSpec((tm, LANES), lambda i: (i, 0))],
        out_shape=[jax.ShapeDtypeStruct((t, ATT_W), F32), jax.ShapeDtypeStruct((t, ATT_W), F32),
                   jax.ShapeDtypeStruct((t, LANES), F32)],
        scratch_shapes=[], semantics=("parallel",), args=(dx, w_o, o),
    )


def _head_rmsnorm_bwd(x_ref, dy_ref, gain_ref, dx_ref, dgain_ref):
    bd = _head_block_diag()
    gain = gain_ref[...]

    def step(i, acc):
        rows = pl.ds(pl.multiple_of(i * NORM_ROWS, NORM_ROWS), NORM_ROWS)
        x, dy = x_ref[rows, :], dy_ref[rows, :]
        r = lax.rsqrt(_head_sums(x * x, bd, terms=1) * (1.0 / ATT_HEAD_DIM) + NORM_EPS)
        xh = x * r
        dxh = dy * gain
        mean = _head_sums(dxh * xh, bd, terms=1) * (1.0 / ATT_HEAD_DIM)
        dx_ref[rows, :] = (r * (dxh - xh * mean)).astype(BF16)
        return acc + jnp.sum(dy * xh, axis=0, keepdims=True)

    acc = lax.fori_loop(0, x_ref.shape[0] // NORM_ROWS, step, jnp.zeros((1, LANES), F32))
    dgain_ref[...] = jnp.broadcast_to(acc, dgain_ref.shape)


def _attn_bwd_dq(qkv, qkn, gq, do, l_rep, dl_rep, g, dil):
    t = qkv.shape[0]
    nb = t // dil // ATT_BLOCK
    bq = ATT_BLOCK

    def body(q_ref, qn_ref, kn_ref, v_ref, gq_ref, do_ref, l_ref, dl_ref, dx_ref, dgain_ref, bias_ref, dq_ref):
        _fill_band_bias(bias_ref, pl.program_id(0), dil, False)
        lt64 = _lane_lt64(bq)

        def per_row(tile):
            cols = _head_cols(tile, lt64)
            half = jnp.concatenate([cols[0], cols[1]], axis=0)
            return jnp.concatenate([half, half], axis=1)

        def work(items):
            products, keys, dscores = [], [], []
            for n, r in items:
                prev = jnp.maximum(n - 1, 0)
                q2 = _stack_heads(_class_rows(qn_ref, n, r, dil).astype(BF16))
                do2 = _stack_heads(_class_rows(do_ref, n, r, dil).astype(BF16))
                kcat = jnp.concatenate([_class_rows(kn_ref, prev, r, dil), _class_rows(kn_ref, n, r, dil)],
                                       axis=0).astype(BF16)
                vcat = jnp.concatenate([_class_rows(v_ref, prev, r, dil), _class_rows(v_ref, n, r, dil)],
                                       axis=0).astype(BF16)
                keys.append(kcat)
                products.append((_dot_nt(q2, kcat), _dot_nt(do2, vcat)))
            for (n, r), (scores, dps) in zip(items, products):
                bias = bias_ref.at[jnp.minimum(n, 1)]
                lse = per_row(_class_rows(l_ref, n, r, dil))
                dl = per_row(_class_rows(dl_ref, n, r, dil))
                dss = []
                for rows in _row_slices():
                    p = jnp.exp(scores[rows] - bias[rows, :] - lse[rows])
                    dss.append((p * (dps[rows] - dl[rows])).astype(BF16))
                dscores.append(jnp.concatenate(dss, axis=0))
            for (n, r), ds, kcat in zip(items, dscores, keys):
                _store_class_rows(dq_ref, n, r, dil, _unstack_heads(_dot(ds, kcat) * ATT_SCALE, lt64))

        _item_loop(nb, dil, work)
        _head_rmsnorm_bwd(q_ref, dq_ref, gq_ref, dx_ref, dgain_ref)

    col = lambda j: pl.BlockSpec((t, LANES), _pair_col(g, j))
    vec = pl.BlockSpec((1, LANES), lambda pair: (0, 0))
    tok = pl.BlockSpec((t, LANES), lambda pair: (0, pair))
    return pl.pallas_call(
        body, name=f"attn_bwd_dq_g{g}", grid=(PAIRS,),
        in_specs=[col(0), col(0), col(1), col(2), vec, tok, tok, tok],
        out_specs=[tok, pl.BlockSpec((None, 8, LANES), lambda pair: (pair, 0, 0))],
        out_shape=[jax.ShapeDtypeStruct((t, ATT_W), BF16), jax.ShapeDtypeStruct((PAIRS, 8, LANES), F32)],
        scratch_shapes=[pltpu.VMEM((2, 2 * bq, 2 * bq), F32), pltpu.VMEM((t, LANES), F32)],
        compiler_params=_params("parallel"),
    )(qkv, qkn, qkn, qkn, gq, do, l_rep, dl_rep)


def _attn_bwd_dkv(qkv, qkn, gk, do, l_row, dl_row, g, dil):
    t = qkv.shape[0]
    nb = t // dil // ATT_BLOCK
    bq = ATT_BLOCK

    def body(k_ref, qn_ref, kn_ref, v_ref, gk_ref, do_ref, l_ref, dl_ref, dkx_ref, dvx_ref, dgain_ref, bias_ref,
             dk_ref, dv_ref):
        _fill_band_bias(bias_ref, pl.program_id(0), dil, True)
        lt64 = _lane_lt64(bq)

        def per_query(ref, hh, lane_c, lane_n):
            return jnp.concatenate([ref[hh:hh + 1, pl.ds(lane_c, bq)], ref[hh:hh + 1, pl.ds(lane_n, bq)]], axis=1)

        def work(items):
            products, operands, weights = [], [], []
            for n, r in items:
                nxt = jnp.minimum(n + 1, nb - 1)
                k2 = _stack_heads(_class_rows(kn_ref, n, r, dil).astype(BF16))
                v2 = _stack_heads(_class_rows(v_ref, n, r, dil).astype(BF16))
                qcat = jnp.concatenate([_class_rows(qn_ref, n, r, dil), _class_rows(qn_ref, nxt, r, dil)],
                                       axis=0).astype(BF16)
                docat = jnp.concatenate([_class_rows(do_ref, n, r, dil), _class_rows(do_ref, nxt, r, dil)],
                                        axis=0).astype(BF16)
                operands.append((qcat, docat))
                products.append((_dot_nt(k2, qcat), _dot_nt(v2, docat)))
            for (n, r), (scores, dps) in zip(items, products):
                nxt = jnp.minimum(n + 1, nb - 1)
                bias = bias_ref.at[jnp.where(n == nb - 1, 0, 1)]
                lane_c = pl.multiple_of((r * nb + n) * bq, bq)
                lane_n = pl.multiple_of((r * nb + nxt) * bq, bq)
                lse = [per_query(l_ref, hh, lane_c, lane_n) for hh in range(2)]
                dl = [per_query(dl_ref, hh, lane_c, lane_n) for hh in range(2)]
                pts, dss = [], []
                for i, rows in enumerate(_row_slices()):
                    hh = i * SLICE_ROWS // bq
                    p_t = jnp.exp(scores[rows] - bias[rows, :] - lse[hh])
                    pts.append(p_t.astype(BF16))
                    dss.append((p_t * (dps[rows] - dl[hh])).astype(BF16))
                weights.append((jnp.concatenate(pts, axis=0), jnp.concatenate(dss, axis=0)))
            for (n, r), (p_t, ds_t), (qcat, docat) in zip(items, weights, operands):
                _store_class_rows(dv_ref, n, r, dil, _unstack_heads(_dot(p_t, docat), lt64))
                _store_class_rows(dk_ref, n, r, dil, _unstack_heads(_dot(ds_t, qcat), lt64))

        _item_loop(nb, dil, work)
        _head_rmsnorm_bwd(k_ref, dk_ref, gk_ref, dkx_ref, dgain_ref)

        def cast_rows(i, carry):
            rows = pl.ds(pl.multiple_of(i * NORM_ROWS, NORM_ROWS), NORM_ROWS)
            dvx_ref[rows, :] = dv_ref[rows, :].astype(BF16)
            return carry

        lax.fori_loop(0, t // NORM_ROWS, cast_rows, 0)

    col = lambda j: pl.BlockSpec((t, LANES), _pair_col(g, j))
    vec = pl.BlockSpec((1, LANES), lambda pair: (0, 0))
    tok = pl.BlockSpec((t, LANES), lambda pair: (0, pair))
    rows = pl.BlockSpec((None, 8, t), lambda pair: (pair, 0, 0))
    return pl.pallas_call(
        body, name=f"attn_bwd_dkv_g{g}", grid=(PAIRS,),
        in_specs=[col(1), col(0), col(1), col(2), vec, tok, rows, rows],
        out_specs=[tok, tok, pl.BlockSpec((None, 8, LANES), lambda pair: (pair, 0, 0))],
        out_shape=[jax.ShapeDtypeStruct((t, ATT_W), BF16), jax.ShapeDtypeStruct((t, ATT_W), BF16),
                   jax.ShapeDtypeStruct((PAIRS, 8, LANES), F32)],
        scratch_shapes=[pltpu.VMEM((2, 2 * bq, 2 * bq), F32), pltpu.VMEM((t, LANES), F32),
                        pltpu.VMEM((t, LANES), F32)],
        compiler_params=_params("parallel"),
    )(qkv, qkn, qkn, qkn, gk, do, l_row, dl_row)


def _rows_by_residue(one_per_head, dil):
    t = one_per_head.shape[0]
    per_head = one_per_head[:, :ATT_HEADS]
    rows = per_head.reshape(t // dil, dil, ATT_HEADS).transpose(2, 1, 0).reshape(PAIRS, 2, t)
    return jnp.pad(rows, ((0, 0), (0, 6), (0, 0)))


def _per_head(rep_row):
    return rep_row[0, ::SSM_HEAD_DIM]


def _rep_heads(v):
    return jnp.repeat(v, SSM_HEAD_DIM)[None, :]


def _pad_lanes(v):
    return jnp.pad(v, ((0, 0), (0, LANES - v.shape[1])))


class _NoOverlap:
    def side(self, host):
        return None

    def after(self, host):
        pass

    def begin_backward(self, grads):
        pass


def _hosted(plan, host, fn, *args, **kwargs):
    out = fn(*args, side=plan.side(host), **kwargs)
    plan.after(host)
    return out


def _ffn_ple_fwd(x1, h, p_i, prm, i, plan, next_gain=None, target=None):
    g, u, act = _hosted(plan, f"swiglu_fwd_{i}", _swiglu_fwd, h, prm["ffn_w_gate"][i], prm["ffn_w_up"][i],
                        name=f"swiglu_fwd_{i}")
    x2 = _hosted(plan, f"ffn_down_{i}", _matmul, act, prm["ffn_w_down"][i], mode="nn", addend=x1,
                 name=f"ffn_down_{i}")
    outs = _ple_fwd(x2, p_i, prm["ple_w_gate"][i], prm["ple_w_proj"][i], name=f"ple_fwd_{i}", next_gain=next_gain,
                    target=target)
    return outs, dict(x1=x1, h=h, g=g, u=u, act=act, x2=x2)


def _ffn_ple_bwd(dx3, p_i, prm, i, sv, grads, plan):
    ds, dple, dx2 = _ple_bwd(sv["x2"], p_i, prm["ple_w_gate"][i], prm["ple_w_proj"][i], dx3, name=f"ple_bwd_{i}")
    grads["ple_w_gate"][i] = _matmul_tn(sv["x2"], ds, name=f"d_ple_w_gate_{i}")
    grads["ple_w_proj"][i] = _matmul_tn(dple, p_i, name=f"d_ple_w_proj_{i}")
    grads["ffn_w_down"][i] = _matmul_tn(sv["act"], dx2, name=f"d_ffn_w_down_{i}")
    dg, du = _hosted(plan, f"swiglu_bwd_{i}", _swiglu_bwd, dx2, prm["ffn_w_down"][i], sv["g"], sv["u"],
                     name=f"swiglu_bwd_{i}")
    grads["ffn_w_gate"][i] = _matmul_tn(dg, sv["h"], name=f"d_ffn_w_gate_{i}")
    grads["ffn_w_up"][i] = _matmul_tn(du, sv["h"], name=f"d_ffn_w_up_{i}")
    dx1, dgain = _matmul_rmsnorm_bwd(dg, prm["ffn_w_gate"][i], None, sv["x1"], prm["norm_ffn"][i:i + 1], dx2,
                                     name=f"ffn_dh_{i}", tm=256, more=(du, prm["ffn_w_up"][i]))
    grads["norm_ffn"][i] = dgain[0]
    return dx1


def _mamba_fwd(x0, prm, plan):
    h = _rmsnorm_fwd(x0, prm["norm_mix"][0:1], name="mix_norm_fwd_0")
    z = _hosted(plan, "ssm_in_z", _matmul, h, prm["ssm_w_z"], mode="nt", name="ssm_in_z")
    xbc_pre = _hosted(plan, "ssm_in_xbc", _matmul, h, prm["ssm_w_xbc"], mode="nt", name="ssm_in_xbc")
    dt_raw = _matmul(h, prm["ssm_w_dt"], mode="nt", name="ssm_in_dt")
    xbc = _hosted(plan, "conv_fwd", _conv_fwd, xbc_pre, prm["ssm_conv_w"], prm["ssm_conv_b"])
    dt_bias = _pad_lanes(prm["ssm_dt_bias"])
    a_log = _pad_lanes(prm["ssm_a_log"])
    acs, dt_rep, acs_rep = _ssd_prep_fwd(dt_raw, dt_bias, a_log)
    acs_t = acs[:, :SSM_HEADS].T
    dskip_rep = _rep_heads(prm["ssm_d_skip"][0])
    y, hin_all, yn = _hosted(plan, "ssd_fwd", _ssd_fwd, xbc, dt_rep, acs_rep, acs_t, dskip_rep, z,
                             prm["ssm_norm_w"])
    x1, h_ffn = _matmul(yn, prm["ssm_w_out"], mode="nn", addend=x0, name="ssm_out", tm=512, tn=D_MODEL,
                        second=(_rmsnorm_rows, [prm["norm_ffn"][0:1]], BF16))
    sv = dict(x0=x0, h=h, z=z, xbc_pre=xbc_pre, dt_raw=dt_raw, xbc=xbc, dt_bias=dt_bias, dt_rep=dt_rep,
              acs_rep=acs_rep, acs_t=acs_t, dskip_rep=dskip_rep, y=y, hin_all=hin_all, yn=yn)
    return x1, h_ffn, sv


def _mamba_bwd(dx1, prm, sv, grads, plan):
    grads["ssm_w_out"] = _matmul_tn(sv["yn"], dx1, name="d_ssm_w_out")
    dy, dz, dnw = _hosted(plan, "gate_norm_bwd", _gate_norm_bwd, sv["y"], sv["z"], prm["ssm_norm_w"], dx1,
                          prm["ssm_w_out"])
    grads["ssm_norm_w"] = dnw
    a_rep = _rep_heads(-jnp.exp(prm["ssm_a_log"][0]))
    dxbc, ddt, da_rep, dds_rep = _hosted(plan, "ssd_bwd", _ssd_bwd, sv["xbc"], sv["dt_rep"], sv["acs_rep"],
                                             sv["acs_t"], sv["dskip_rep"], a_rep, sv["hin_all"], dy)
    grads["ssm_d_skip"] = _per_head(dds_rep)[None, :]
    grads["ssm_a_log"] = (_per_head(da_rep) * _per_head(a_rep))[None, :]
    ddt_raw, dbias = _ssd_prep_bwd(sv["dt_raw"], sv["dt_bias"], ddt)
    grads["ssm_dt_bias"] = dbias[:, :SSM_HEADS]
    du, dcw, dcb = _hosted(plan, "conv_bwd", _conv_bwd, sv["xbc_pre"], prm["ssm_conv_w"], prm["ssm_conv_b"], dxbc)
    grads["ssm_conv_w"] = dcw
    grads["ssm_conv_b"] = dcb
    h = sv["h"]
    grads["ssm_w_in"] = jnp.concatenate(
        [_matmul_tn(dz, h, name="d_ssm_w_z"), _matmul_tn(du, h, name="d_ssm_w_xbc"),
         _matmul_tn(ddt_raw, h, name="d_ssm_w_dt")[:SSM_HEADS]], axis=0)
    dh = _hosted(plan, "ssm_dh_z", _matmul, dz, prm["ssm_w_z"], mode="nn", name="ssm_dh_z")
    dh = _hosted(plan, "ssm_dh_xbc", _matmul, du, prm["ssm_w_xbc"], mode="nn", addend=dh, name="ssm_dh_xbc")
    dx0, dgain = _hosted(plan, "ssm_dh_dt", _matmul_rmsnorm_bwd, ddt_raw, prm["ssm_w_dt"], dh, sv["x0"],
                         prm["norm_mix"][0:1], dx1, name="ssm_dh_dt")
    grads["norm_mix"][0] = dgain[0]
    return dx0


def _attn_mixer_fwd(x0, h, prm, plan):
    n_heads = N_QKV_BLOCKS * ATT_HEADS
    gq = jnp.tile(prm["att_q_norm"], (1, n_heads))
    gk = jnp.tile(prm["att_k_norm"], (1, n_heads))
    qkv, qkn = _hosted(plan, "att_qkv", _matmul, h, prm["att_w_qkv"], mode="nt", name="att_qkv",
                       second=(_qk_normalised, [gq, gk], F32))
    outs, lses = [], []
    for g, (window, dil) in enumerate(DIL_PATTERNS):
        o_g, l_g = _attn_fwd(qkn, g, dil)
        outs.append(o_g)
        lses.append(l_g)
    o_f, l_rep, l_one, x1, h_ffn = _attn_out_fwd(outs, lses, prm["att_w_o"], x0, prm["norm_ffn"][1:2])
    sv = dict(x0=x0, h=h, qkv=qkv, qkn=qkn, gq2=gq[:, :LANES], gk2=gk[:, :LANES], o_f=o_f, l_rep=l_rep,
              l_one=l_one)
    return x1, h_ffn, sv


def _attn_mixer_bwd(dx1, prm, sv, grads, plan):
    grads["att_w_o"] = _matmul_tn(sv["o_f"], dx1, name="d_att_w_o")
    do, dl_rep, dl_one = _hosted(plan, "att_out_dx", _attn_out_bwd, dx1, prm["att_w_o"], sv["o_f"])
    blocks, dgq, dgk = [], [], []
    for g, (window, dil) in enumerate(DIL_PATTERNS):
        dq, dgq_g = _attn_bwd_dq(sv["qkv"], sv["qkn"], sv["gq2"], do, sv["l_rep"], dl_rep, g, dil)
        dk, dv, dgk_g = _attn_bwd_dkv(sv["qkv"], sv["qkn"], sv["gk2"], do, _rows_by_residue(sv["l_one"], dil),
                                      _rows_by_residue(dl_one, dil), g, dil)
        blocks += [dq, dk, dv]
        dgq.append(dgq_g)
        dgk.append(dgk_g)
    dqkv = jnp.concatenate(blocks, axis=1)

    def fold(parts):
        return jnp.stack(parts)[:, :, 0].reshape(-1, ATT_HEAD_DIM).sum(axis=0)[None, :]

    grads["att_q_norm"] = fold(dgq)
    grads["att_k_norm"] = fold(dgk)
    grads["att_w_qkv"] = _matmul_tn(dqkv, sv["h"], name="d_att_w_qkv")
    dx0, dgain = _hosted(plan, "att_qkv_dx", _matmul_rmsnorm_bwd, dqkv, prm["att_w_qkv"], None, sv["x0"],
                         prm["norm_mix"][1:2], dx1, name="att_qkv_dx")
    grads["norm_mix"][1] = dgain[0]
    return dx0


def _local_step(x, p, target, prm, plan=None):
    plan = plan or _NoOverlap()
    grads = {k: [None, None] for k in ("norm_mix", "norm_ffn", "ffn_w_gate", "ffn_w_up", "ffn_w_down",
                                       "ple_w_proj", "ple_w_gate")}
    plan.begin_backward(grads)
    x1, h1, sv_m = _mamba_fwd(x, prm, plan)
    (x3, h3), sv_f0 = _ffn_ple_fwd(x1, h1, p[0], prm, 0, plan, next_gain=prm["norm_mix"][1:2])
    x4, h4, sv_a = _attn_mixer_fwd(x3, h3, prm, plan)
    (dy, loss_row), sv_f1 = _ffn_ple_fwd(x4, h4, p[1], prm, 1, plan, target=target)
    dx4 = _ffn_ple_bwd(dy, p[1], prm, 1, sv_f1, grads, plan)
    dx3 = _attn_mixer_bwd(dx4, prm, sv_a, grads, plan)
    dx1 = _ffn_ple_bwd(dx3, p[0], prm, 0, sv_f0, grads, plan)
    dx0 = _mamba_bwd(dx1, prm, sv_m, grads, plan)
    return loss_row, dx0, grads


W_IN_SLAB_ROWS = 1312


def _position():
    return lax.axis_index("x"), lax.axis_index("y"), lax.axis_index("c")


def _other_chips(x, y):
    return [(1 - x, y), (x, 1 - y), (1 - x, 1 - y)]


def _remote(send_sems, recv_sems, k, src, dst, to):
    return pltpu.make_async_remote_copy(src_ref=src, dst_ref=dst, send_sem=send_sems.at[k], recv_sem=recv_sems.at[k],
                                        device_id=to, device_id_type=MESH)


def _gather_side(entries, whole=()):
    n, nw = len(entries), len(whole)

    def first_hop(ins, outs, send_sems, recv_sems):
        x, y, c = _position()
        cps = []
        for j, chip in enumerate(_other_chips(x, y)):
            for e in range(n):
                cps.append(_remote(send_sems, recv_sems, 6 * e + j, ins[e].at[c], outs[e].at[2 * x + y, c], (*chip, c)))
            for e in range(nw):
                cps.append(_remote(send_sems, recv_sems, 6 * n + 3 * e + j, ins[n + e], outs[n + e].at[2 * x + y],
                                   (*chip, c)))
        return cps

    def start(ins, outs, send_sems, recv_sems):
        for cp in first_hop(ins, outs, send_sems, recv_sems):
            cp.start()

    def finish(ins, outs, send_sems, recv_sems):
        x, y, c = _position()
        me, sibling = (x, y, c), (x, y, 1 - c)
        chips = _other_chips(x, y)
        passed_on = []
        for j, (px, py) in enumerate(chips):
            for e in range(n):
                landed = outs[e].at[2 * px + py, c]
                _remote(send_sems, recv_sems, 6 * e + j, landed, landed, me).wait_recv()
                passed_on.append(_remote(send_sems, recv_sems, 6 * e + 3 + j, landed, landed, sibling))
                passed_on[-1].start()
            for e in range(nw):
                landed = outs[n + e].at[2 * px + py]
                _remote(send_sems, recv_sems, 6 * n + 3 * e + j, landed, landed, me).wait_recv()
        for j, (px, py) in enumerate(chips):
            for e in range(n):
                passed = outs[e].at[2 * px + py, 1 - c]
                _remote(send_sems, recv_sems, 6 * e + 3 + j, passed, passed, me).wait_recv()
        for cp in first_hop(ins, outs, send_sems, recv_sems) + passed_on:
            cp.wait_send()

    shapes = [jax.ShapeDtypeStruct((N_CHIPS,) + a.shape, a.dtype) for a in list(entries) + list(whole)]
    return _Side(list(entries) + list(whole), shapes, 6 * n + 3 * nw, start, finish)


def _run_side(side, name):
    si, so = len(side.inputs), len(side.out_shapes)

    def body(*refs):
        ins, outs, send_sems, recv_sems = refs[:si], refs[si:si + so], refs[-2], refs[-1]
        side.start(ins, outs, send_sems, recv_sems)
        side.finish(ins, outs, send_sems, recv_sems)

    side.outputs = list(pl.pallas_call(
        body, name=name, in_specs=[ANY] * si, out_specs=[ANY] * so, out_shape=side.out_shapes,
        scratch_shapes=[pltpu.SemaphoreType.DMA((side.n_sems,)), pltpu.SemaphoreType.DMA((side.n_sems,))],
    )(*side.inputs))
    return side.outputs


def _swap_side(grads):
    n = len(grads)

    def copies(ins, outs, send_sems, recv_sems):
        x, y, c = _position()
        return [_remote(send_sems, recv_sems, e, ins[e].at[:, 1 - c], outs[e], (x, y, 1 - c)) for e in range(n)]

    def start(ins, outs, send_sems, recv_sems):
        for cp in copies(ins, outs, send_sems, recv_sems):
            cp.start()

    def finish(ins, outs, send_sems, recv_sems):
        for cp in copies(ins, outs, send_sems, recv_sems):
            cp.wait()

    shapes = [jax.ShapeDtypeStruct((N_CHIPS,) + g.shape[2:], g.dtype) for g in grads]
    return _Side(grads, shapes, n, start, finish)


def _chip_exchange_side(chipsums):
    n = len(chipsums)

    def copies(ins, outs, send_sems, recv_sems):
        x, y, c = _position()
        return [_remote(send_sems, recv_sems, 3 * e + j, ins[e].at[2 * tx + ty], outs[e].at[j], (tx, ty, c))
                for j, (tx, ty) in enumerate(_other_chips(x, y)) for e in range(n)]

    def start(ins, outs, send_sems, recv_sems):
        for cp in copies(ins, outs, send_sems, recv_sems):
            cp.start()

    def finish(ins, outs, send_sems, recv_sems):
        for cp in copies(ins, outs, send_sems, recv_sems):
            cp.wait()

    shapes = [jax.ShapeDtypeStruct((3,) + cs.shape[1:], cs.dtype) for cs in chipsums]
    return _Side(chipsums, shapes, 3 * n, start, finish)


def _share_side(totals):
    n = len(totals)

    def copies(ins, outs, send_sems, recv_sems):
        x, y, c = _position()
        return [_remote(send_sems, recv_sems, e, ins[e], outs[e], (x, y, 1 - c)) for e in range(n)]

    def start(ins, outs, send_sems, recv_sems):
        for cp in copies(ins, outs, send_sems, recv_sems):
            cp.start()

    def finish(ins, outs, send_sems, recv_sems):
        for cp in copies(ins, outs, send_sems, recv_sems):
            cp.wait()

    return _Side(totals, [jax.ShapeDtypeStruct(t.shape, t.dtype) for t in totals], n, start, finish)


def _reduce_rows(h):
    return h if h <= 704 else h // 2


def _add_sibling(grad, recv, c_idx, *, name):
    _, _, h, cw = grad.shape
    th = _reduce_rows(h)

    def body(c_ref, g_ref, r_ref, o_ref):
        o_ref[...] = (g_ref[...] + r_ref[...]).astype(BF16)

    return pl.pallas_call(
        body, name=name,
        grid_spec=pltpu.PrefetchScalarGridSpec(
            num_scalar_prefetch=1, grid=(N_CHIPS, h // th),
            in_specs=[pl.BlockSpec((None, None, th, cw), lambda s, i, c_ref: (s, c_ref[0], i, 0)),
                      pl.BlockSpec((None, th, cw), lambda s, i, c_ref: (s, i, 0))],
            out_specs=pl.BlockSpec((None, th, cw), lambda s, i, c_ref: (s, i, 0))),
        out_shape=jax.ShapeDtypeStruct((N_CHIPS, h, cw), BF16),
        compiler_params=_params("parallel", "parallel"),
    )(c_idx, grad, recv)


def _add_chips(chipsum, recv, s_idx, *, name):
    _, h, cw = chipsum.shape
    th = _reduce_rows(h)

    def body(s_ref, own_ref, r_ref, o_ref):
        o_ref[...] = ((own_ref[...].astype(F32) + r_ref[0].astype(F32)) + r_ref[1].astype(F32)) + r_ref[2].astype(F32)

    return pl.pallas_call(
        body, name=name,
        grid_spec=pltpu.PrefetchScalarGridSpec(
            num_scalar_prefetch=1, grid=(h // th,),
            in_specs=[pl.BlockSpec((None, th, cw), lambda i, s_ref: (s_ref[0], i, 0)),
                      pl.BlockSpec((3, th, cw), lambda i, s_ref: (0, i, 0))],
            out_specs=pl.BlockSpec((th, cw), lambda i, s_ref: (i, 0))),
        out_shape=jax.ShapeDtypeStruct((h, cw), F32),
        compiler_params=_params("parallel"),
    )(s_idx, chipsum, recv)


def _adamw_math(w, g, m, v):
    m = ADAM_B1 * m + (1.0 - ADAM_B1) * g
    v = ADAM_B2 * v + (1.0 - ADAM_B2) * (g * g)
    m_hat = m / (1.0 - ADAM_B1 ** ADAM_STEP)
    v_hat = v / (1.0 - ADAM_B2 ** ADAM_STEP)
    delta = -ADAM_LR * (m_hat / (jnp.sqrt(v_hat) + ADAM_EPS) + ADAM_WD * w)
    return delta, m, v


ADAM_TILE_ELEMS = 256 * 1024


def _adamw(w, g, m, v, *, name):
    layers, rows, cols = w.shape
    tr = rows
    for cand in range(8, rows, 8):
        if rows % cand == 0 and cand * cols <= ADAM_TILE_ELEMS:
            tr = cand
    if rows * cols <= ADAM_TILE_ELEMS:
        tr = rows

    def body(w_ref, g_ref, m_ref, v_ref, d_ref, nm_ref, nv_ref):
        d, nm, nv = _adamw_math(w_ref[...], g_ref[...], m_ref[...], v_ref[...])
        d_ref[...] = d
        nm_ref[...] = nm
        nv_ref[...] = nv

    blk = pl.BlockSpec((None, tr, cols), lambda l, i: (l, i, 0))
    sds = jax.ShapeDtypeStruct(w.shape, F32)
    return pl.pallas_call(
        body, name=name, grid=(layers, rows // tr), in_specs=[blk] * 4, out_specs=[blk] * 3, out_shape=[sds] * 3,
        compiler_params=_params("parallel", "parallel"),
    )(w, g, m, v)


SMALL_LAYOUT = (("loss", 1), ("norm_mix", 16), ("norm_ffn", 16), ("ssm_conv_b", 24), ("ssm_dt_bias", 1),
                ("ssm_a_log", 1), ("ssm_d_skip", 1), ("ssm_norm_w", 16), ("att_q_norm", 1), ("att_k_norm", 1),
                ("conv_w_full", 96))
SMALL_ROWS = 176
N_DEVICES = 8


def _small_packs(dicts):
    parts = []
    for values in dicts:
        for name, rows in SMALL_LAYOUT:
            flat = values[name].reshape(-1).astype(F32)
            parts.append(jnp.pad(flat, (0, rows * LANES - flat.shape[0])).reshape(rows, LANES))
        used = sum(r for _, r in SMALL_LAYOUT)
        parts.append(jnp.zeros((SMALL_ROWS - used, LANES), F32))
    return jnp.concatenate(parts, axis=0).reshape(len(dicts), SMALL_ROWS, LANES)


def _small_unpack(pack, shapes):
    out, off = {}, 0
    for name, rows in SMALL_LAYOUT:
        shape = shapes[name]
        n = math.prod(shape)
        out[name] = pack[off:off + rows].reshape(-1)[:n].reshape(shape)
        off += rows
    return out


def _small_allreduce_adamw(g, w, m, v):
    def body(g_ref, w_ref, m_ref, v_ref, gs_ref, d_ref, nm_ref, nv_ref, buf, send_sems, recv_sems):
        x, y, c = _position()
        pos = (x, y, c)
        me = 4 * x + 2 * y + c
        buf[me] = g_ref[...]
        peers = []
        for k in range(1, N_DEVICES):
            bits = ((k >> 2) & 1, (k >> 1) & 1, k & 1)
            peers.append(tuple(1 - p if b else p for p, b in zip(pos, bits)))
        cps = [pltpu.make_async_remote_copy(src_ref=g_ref, dst_ref=buf.at[me], send_sem=send_sems.at[k],
                                            recv_sem=recv_sems.at[k], device_id=peer, device_id_type=MESH)
               for k, peer in enumerate(peers)]
        for cp in cps:
            cp.start()
        for k, (px, py, pc) in enumerate(peers):
            pltpu.make_async_remote_copy(src_ref=g_ref, dst_ref=buf.at[4 * px + 2 * py + pc],
                                         send_sem=send_sems.at[k], recv_sem=recv_sems.at[k],
                                         device_id=(px, py, pc), device_id_type=MESH).wait_recv()
        for cp in cps:
            cp.wait_send()
        total = buf[0]
        for dev in range(1, N_DEVICES):
            total = total + buf[dev]
        gs_ref[...] = total
        d, nm, nv = _adamw_math(w_ref[...], total, m_ref[...], v_ref[...])
        d_ref[...] = d
        nm_ref[...] = nm
        nv_ref[...] = nv

    vm = pl.BlockSpec(memory_space=pltpu.VMEM)
    sds = jax.ShapeDtypeStruct((SMALL_ROWS, LANES), F32)
    return pl.pallas_call(
        body, name="small_allreduce_adamw", in_specs=[vm] * 4, out_specs=[vm] * 4, out_shape=[sds] * 4,
        scratch_shapes=[pltpu.VMEM((N_DEVICES, SMALL_ROWS, LANES), F32),
                        pltpu.SemaphoreType.DMA((N_DEVICES - 1,)), pltpu.SemaphoreType.DMA((N_DEVICES - 1,))],
    )(g, w, m, v)


SMALL = tuple(n for n, _ in SMALL_LAYOUT if n not in ("loss", "conv_w_full"))
WEIGHTS = ("norm_mix", "norm_ffn", "ssm_w_in", "ssm_conv_w", "ssm_conv_b", "ssm_dt_bias", "ssm_a_log", "ssm_d_skip",
           "ssm_norm_w", "ssm_w_out", "att_w_qkv", "att_q_norm", "att_k_norm", "att_w_o", "ffn_w_gate", "ffn_w_up",
           "ffn_w_down", "ple_w_proj", "ple_w_gate")
COLUMN_SHARDED = ("ssm_w_in", "att_w_qkv", "ffn_w_gate", "ffn_w_up", "ple_w_proj")
LAYERED = ("ffn_w_gate", "ffn_w_up", "ffn_w_down", "ple_w_proj", "ple_w_gate")
UPDATED_TRANSPOSED = ("ssm_w_in", "ffn_w_gate", "ffn_w_up")
GATHER_ORDER = ("ssm_w_in", "ssm_w_out", "att_w_qkv", "att_w_o", "ffn_w_gate", "ffn_w_up", "ffn_w_down",
                "ple_w_proj", "ple_w_gate")


def _layers(n):
    return (0, 1) if n in LAYERED else (None,)


def _tag(key):
    return key[0] if key[1] is None else f"{key[0]}_{key[1]}"


QKV_PARTS = 3


def _weight_slab(w, key):
    n, i = key
    if n == "att_w_qkv":
        a = w[n][0].T
        rows = a.shape[0] // QKV_PARTS
        a = a[i * rows:(i + 1) * rows]
    else:
        a = w[n][0 if i is None else i]
        a = a.T if n in COLUMN_SHARDED else a
    if n == "ssm_w_in":
        a = jnp.pad(a, ((0, W_IN_SLAB_ROWS - a.shape[0]), (0, 0)))
    return a.reshape(2, a.shape[0] // 2, a.shape[1]).astype(BF16)


def _install(prm, key, gathered, own, s_me):
    n, i = key
    full = lax.dynamic_update_slice(gathered, own[None], (s_me, 0, 0, 0))
    full = full.reshape(N_CHIPS, 2 * full.shape[2], full.shape[3])
    if n == "att_w_qkv":
        parts = prm.setdefault("att_w_qkv_parts", {})
        parts[i] = full
        if len(parts) == QKV_PARTS:
            prm[n] = jnp.stack([parts[j] for j in range(QKV_PARTS)], axis=1).reshape(-1, D_MODEL)
        return
    if n == "ssm_w_in":
        rows = (D_INNER + CONV_DIM + SSM_HEADS) // N_CHIPS
        w_in_t = full[:, :rows].reshape(N_CHIPS * rows, D_MODEL)
        prm["ssm_w_z"] = w_in_t[:D_INNER]
        prm["ssm_w_xbc"] = w_in_t[D_INNER:D_INNER + CONV_DIM]
        prm["ssm_w_dt"] = jnp.pad(w_in_t[D_INNER + CONV_DIM:], ((0, LANES - SSM_HEADS), (0, 0)))
        return
    full = full.reshape(N_CHIPS * full.shape[1], full.shape[2])
    if i is None:
        prm[n] = full
    else:
        prm.setdefault(n, [None, None])[i] = full


def _grad_slab(grads, key):
    n, i = key
    g = grads[n] if i is None else grads[n][i]
    if n == "ssm_w_in":
        g = jnp.pad(g.reshape(N_CHIPS, g.shape[0] // N_CHIPS, D_MODEL),
                    ((0, 0), (0, W_IN_SLAB_ROWS - g.shape[0] // N_CHIPS), (0, 0)))
    rows = g.size // (N_CHIPS * g.shape[-1])
    return g.reshape(N_CHIPS, 2, rows // 2, g.shape[-1])


def _natural_shard(n, reduced, shape):
    def one(r):
        if n == "ssm_w_in":
            r = r[:shape[-1]]
        return r.T if n in COLUMN_SHARDED else r
    if n in LAYERED:
        return jnp.stack([one(r) for r in reduced]).reshape(shape)
    return one(reduced[0]).reshape(shape)


def kernel(x, p, norm_mix, norm_ffn, ssm_w_in, ssm_conv_w, ssm_conv_b, ssm_dt_bias, ssm_a_log, ssm_d_skip, ssm_norm_w, ssm_w_out, att_w_qkv, att_q_norm, att_k_norm, att_w_o, ffn_w_gate, ffn_w_up, ffn_w_down, ple_w_proj, ple_w_gate, loss_target, m_norm_mix, m_norm_ffn, m_ssm_w_in, m_ssm_conv_w, m_ssm_conv_b, m_ssm_dt_bias, m_ssm_a_log, m_ssm_d_skip, m_ssm_norm_w, m_ssm_w_out, m_att_w_qkv, m_att_q_norm, m_att_k_norm, m_att_w_o, m_ffn_w_gate, m_ffn_w_up, m_ffn_w_down, m_ple_w_proj, m_ple_w_gate, v_norm_mix, v_norm_ffn, v_ssm_w_in, v_ssm_conv_w, v_ssm_conv_b, v_ssm_dt_bias, v_ssm_a_log, v_ssm_d_skip, v_ssm_norm_w, v_ssm_w_out, v_att_w_qkv, v_att_q_norm, v_att_k_norm, v_att_w_o, v_ffn_w_gate, v_ffn_w_up, v_ffn_w_down, v_ple_w_proj, v_ple_w_gate):
    given = dict(locals())
    w = {n: given[n] for n in WEIGHTS}
    m = {n: given["m_" + n] for n in WEIGHTS}
    v = {n: given["v_" + n] for n in WEIGHTS}
    c_idx = lax.axis_index("c").astype(jnp.int32).reshape(1)
    s_idx = (2 * lax.axis_index("x") + lax.axis_index("y")).astype(jnp.int32).reshape(1)

    s_me = 2 * lax.axis_index("x") + lax.axis_index("y")
    first_core = lax.axis_index("c") == 0

    qkv_parts = [("att_w_qkv", j) for j in range(QKV_PARTS)]
    gather_plan = {
        "ssm_in_z": [("ssm_w_out", None)],
        "ssm_in_xbc": [("ffn_w_gate", 0)],
        "conv_fwd": [("ffn_w_up", 0)],
        "ssd_fwd": [("ffn_w_down", 0), ("ple_w_proj", 0), ("ple_w_gate", 0), ("att_w_o", None)],
        "swiglu_fwd_0": qkv_parts[:2],
        "ffn_down_0": qkv_parts[2:],
        "att_qkv": [(n, 1) for n in LAYERED],
    }
    mamba = [("ssm_w_in", None)]
    own = {k: _weight_slab(w, k) for k in mamba + sum(gather_plan.values(), [])}
    prm = {n: w[n] for n in SMALL}

    def land(group, outputs):
        for k, g in zip(group, outputs):
            _install(prm, k, g, own[k], s_me)

    first = _gather_side([own[k] for k in mamba], whole=[ssm_conv_w[0]])
    _run_side(first, "gather_mamba")
    land(mamba, first.outputs)
    conv = lax.dynamic_update_slice(first.outputs[-1], ssm_conv_w, (s_me, 0, 0))
    prm["ssm_conv_w"] = conv.transpose(1, 0, 2).reshape(CONV_WIDTH, CONV_DIM)

    ffn1 = [(n, 1) for n in LAYERED]
    attention = [("att_w_qkv", None), ("att_w_o", None)]
    ffn0 = [(n, 0) for n in LAYERED] + [("ssm_w_out", None)]
    reduce_plan = {"att_out_dx": [("swap", ffn1)], "att_qkv_dx": [("exchange", ffn1)],
                   "swiglu_bwd_0": [("swap", attention)], "gate_norm_bwd": [("swap", ffn0)],
                   "ssd_bwd": [("exchange", attention), ("exchange", ffn0)],
                   "ssm_dh_z": [("swap", mamba)], "ssm_dh_xbc": [("exchange", mamba)]}
    state = {}

    def swap_side(group):
        state[_tag(group[0]), "g4"] = g4 = [_grad_slab(state["grads"], k) for k in group]
        return _swap_side(g4)

    def add_siblings(group, from_sibling):
        state[_tag(group[0]), "chipsums"] = [
            _add_sibling(g, r, c_idx, name="add_sibling_" + _tag(k))
            for g, r, k in zip(state[_tag(group[0]), "g4"], from_sibling, group)]

    def exchange_side(group):
        return _chip_exchange_side(state[_tag(group[0]), "chipsums"])

    def add_chips(group, from_chips):
        for k, cs, r in zip(group, state[_tag(group[0]), "chipsums"], from_chips):
            state["total", k] = _add_chips(cs, r, s_idx, name="add_chips_" + _tag(k))

    class Plan(_NoOverlap):
        def __init__(self):
            self.carried = {host: _gather_side([own[k] for k in group]) for host, group in gather_plan.items()}

        def begin_backward(self, grads):
            state["grads"] = grads

        def side(self, host):
            if host in reduce_plan:
                self.parts = [swap_side(group) if step == "swap" else exchange_side(group)
                              for step, group in reduce_plan[host]]
                self.carried[host] = _sides_together(self.parts)
            elif host == share_host:
                self.carried[host] = _share_side([state["total", k] for k in order])
            return self.carried.get(host)

        def after(self, host):
            if host in gather_plan:
                land(gather_plan[host], self.carried[host].outputs)
            elif host in reduce_plan:
                _share_out(self.carried[host], self.parts)
                for (step, group), part in zip(reduce_plan[host], self.parts):
                    (add_siblings if step == "swap" else add_chips)(group, part.outputs)
            elif host == share_host:
                state["shared"] = self.carried[host].outputs

    order = mamba + ffn0 + attention + ffn1
    share_host = "ssm_dh_dt"
    loss_row, dx, grads = _local_step(x[0], p[:, 0], loss_target[0], prm, Plan())

    reduced = {}
    for k, theirs in zip(order, state["shared"]):
        lo = jnp.where(first_core, state["total", k], theirs)
        hi = jnp.where(first_core, theirs, state["total", k])
        reduced.setdefault(k[0], {})[k[1]] = jnp.concatenate([lo, hi], axis=0)
    reduced = {n: [by_layer[i] for i in _layers(n)] for n, by_layer in reduced.items()}

    grad, delta, new_m, new_v = {}, {}, {}, {}
    for n in GATHER_ORDER:
        if n in UPDATED_TRANSPOSED:
            flip = lambda a: a.transpose(0, 2, 1)
            cols = w[n].shape[-1]
            g_t = jnp.stack([r[:cols] for r in reduced[n]])
            grad[n] = flip(g_t)
            delta[n], new_m[n], new_v[n] = [flip(o) for o in _adamw(flip(w[n]), g_t, flip(m[n]), flip(v[n]),
                                                                    name="adamw_" + n)]
            continue
        grad[n] = _natural_shard(n, reduced[n], w[n].shape)
        delta[n], new_m[n], new_v[n] = _adamw(w[n], grad[n], m[n], v[n], name="adamw_" + n)

    small_g = {n: (jnp.stack(grads[n]) if isinstance(grads[n], list) else grads[n]) for n in SMALL}
    small_g["loss"] = loss_row
    small_g["conv_w_full"] = grads["ssm_conv_w"]
    zero = {"loss": jnp.zeros((1, LANES), F32), "conv_w_full": jnp.zeros((CONV_WIDTH, CONV_DIM), F32)}
    packs = _small_packs([small_g, {**w, **zero}, {**m, **zero}, {**v, **zero}])
    outs = _small_allreduce_adamw(packs[0], packs[1], packs[2], packs[3])
    shapes = {n: w[n].shape for n in SMALL}
    shapes["loss"] = (1, LANES)
    shapes["conv_w_full"] = (CONV_WIDTH, CONV_DIM)
    sg, sd, sm, sv = [_small_unpack(o, shapes) for o in outs]
    for n in SMALL:
        grad[n], delta[n], new_m[n], new_v[n] = sg[n], sd[n], sm[n], sv[n]
    loss = sg["loss"][0, 0]
    conv_cols = CONV_DIM // N_CHIPS
    grad["ssm_conv_w"] = lax.dynamic_slice(sg["conv_w_full"], (0, s_me * conv_cols), (CONV_WIDTH, conv_cols))[None]
    delta["ssm_conv_w"], new_m["ssm_conv_w"], new_v["ssm_conv_w"] = _adamw(
        ssm_conv_w, grad["ssm_conv_w"], m_ssm_conv_w, v_ssm_conv_w, name="adamw_ssm_conv_w")

    return (loss, dx[None], *[grad[n] for n in WEIGHTS], *[delta[n] for n in WEIGHTS],
            *[new_m[n] for n in WEIGHTS], *[new_v[n] for n in WEIGHTS])
```
